```python
import math
import jax, jax.numpy as jnp
from jax import lax
import numpy as np

D_MODEL = 1024
BATCH = 8
SEQ = 8192
DEPTH = 2

PLE_DIM = 256
HEAD_DIM = 64
N_ATTN_HEADS = 8
ATTN_WIDTH = N_ATTN_HEADS * HEAD_DIM
DILATED_BRANCHES = ((128, 1), (512, 4), (2048, 16))
ATTN_BLOCK = 128
N_REL_BUCKETS = 32
REL_MAX_DISTANCE = 2048
SGU_GROUPS = 4
SGU_GROUP_WIDTH = 64
SGU_WIDTH = SGU_GROUPS * SGU_GROUP_WIDTH
SGU_CHUNK = 128
SSM_GROUP_CH = 16
SSM_WIDTH = 256
SSM_GROUPS = SSM_WIDTH // SSM_GROUP_CH
SSM_STATE = 64
MIX_WIDTH = ATTN_WIDTH + SGU_WIDTH + SSM_WIDTH
IN_WIDTH = 3 * ATTN_WIDTH + 2 * SGU_WIDTH + SSM_WIDTH
D_FF = 2816
FFN_CONV = 3
EPS = 1e-6
NEG_INF = -1e30

kernel_name = "hybrid_dilated_sgu_s5_block"


def rms_scale(x):
    xf = x.astype(jnp.float32)
    return (xf * lax.rsqrt(jnp.mean(xf * xf, axis=-1, keepdims=True) + EPS)).astype(x.dtype)


def rms_norm(x, g):
    return rms_scale(x) * g.astype(x.dtype)


def t5_bucket(dist):
    max_exact = N_REL_BUCKETS // 2
    d = np.maximum(dist, 0)
    large = max_exact + (np.log(np.maximum(d, 1) / max_exact)
                         / np.log(REL_MAX_DISTANCE / max_exact)
                         * (N_REL_BUCKETS - max_exact)).astype(np.int32)
    large = np.minimum(large, N_REL_BUCKETS - 1)
    return np.where(d < max_exact, d, large).astype(np.int32)


def dilated_branch(q, k, v, rel_bias, window, dil):
    B, S, H, Dh = q.shape
    blk = ATTN_BLOCK
    steps = window // dil
    n = S // dil
    nblk = -(-n // blk)
    pad = nblk * blk - n

    def to_blocks(t):
        t = t.reshape(B, n, dil, H, Dh).transpose(0, 2, 1, 3, 4)
        t = jnp.pad(t, ((0, 0), (0, 0), (0, pad), (0, 0), (0, 0)))
        return t.reshape(B, dil, nblk, blk, H, Dh)

    def with_prev(t):
        prev = jnp.concatenate([jnp.zeros_like(t[:, :, :1]), t[:, :, :-1]], axis=2)
        return jnp.concatenate([prev, t], axis=3)

    qb = to_blocks(q)
    kw = with_prev(to_blocks(k))
    vw = with_prev(to_blocks(v))

    q_loc = np.arange(blk)[:, None]
    k_loc = np.arange(2 * blk)[None, :]
    rel = q_loc + blk - k_loc
    band = (rel >= 0) & (rel <= steps)
    first = (np.arange(nblk)[:, None, None] > 0) | (k_loc[None] >= blk)
    valid = (band[None] & first)[:, None]
    bias = rel_bias.astype(jnp.float32)[t5_bucket(rel * dil)]

    s = jnp.einsum('brcqhd,brckhd->brchqk', qb, kw).astype(jnp.float32) * (Dh ** -0.5)
    s = s + jnp.transpose(bias, (2, 0, 1))
    s = jnp.where(valid, s, NEG_INF)
    m = jnp.max(s, axis=-1, keepdims=True)
    e = jnp.exp(s - m)
    den = jnp.sum(e, axis=-1)
    o = jnp.einsum('brchqk,brckhd->brcqhd', e, vw.astype(jnp.float32))
    o = o / jnp.swapaxes(den, -1, -2)[..., None]
    lse = jnp.swapaxes(m[..., 0] + jnp.log(den), -1, -2)

    def from_blocks(t):
        t = t.reshape((B, dil, nblk * blk) + t.shape[4:])[:, :, :n]
        t = jnp.swapaxes(t, 1, 2)
        return t.reshape((B, S) + t.shape[3:])

    return from_blocks(o), from_blocks(lse)


def dilated_attention(q, k, v, rel_bias):
    B, S, H, Dh = q.shape
    outs, lses = [], []
    for window, dil in DILATED_BRANCHES:
        o, l = dilated_branch(q, k, v, rel_bias, window, dil)
        outs.append(o)
        lses.append(l)
    wts = jax.nn.softmax(jnp.stack(lses), axis=0)
    o = jnp.sum(wts[..., None] * jnp.stack(outs), axis=0)
    return o.reshape(B, S, H * Dh)


def spatial_gating(z, ln_g, ln_b, w_s, b_s):
    B, S, _ = z.shape
    z = jax.nn.gelu(z)
    u, v = jnp.split(z, 2, axis=-1)
    vf = v.astype(jnp.float32).reshape(B, S // SGU_CHUNK, SGU_CHUNK, SGU_GROUPS, SGU_GROUP_WIDTH)
    mu = jnp.mean(vf, axis=-1, keepdims=True)
    var = jnp.mean(jnp.square(vf - mu), axis=-1, keepdims=True)
    vn = ((vf - mu) * lax.rsqrt(var + EPS)
          * ln_g.astype(jnp.float32).reshape(SGU_GROUPS, SGU_GROUP_WIDTH)
          + ln_b.astype(jnp.float32).reshape(SGU_GROUPS, SGU_GROUP_WIDTH))
    causal = np.tril(np.ones((SGU_CHUNK, SGU_CHUNK), dtype=bool))
    w = jnp.where(causal, w_s.astype(jnp.float32), 0.0)
    mixed = jnp.einsum('gts,bnsgc->bntgc', w, vn) + b_s.astype(jnp.float32).T[:, :, None]
    return (u.astype(jnp.float32) * mixed.reshape(B, S, SGU_WIDTH)).astype(z.dtype)


def s5_ssm(u, a_re, a_im, log_dt, b_re, b_im, c_re, c_im, d_skip, glu_w, glu_b):
    B, S, _ = u.shape
    f32 = lambda t: t.astype(jnp.float32)
    lam = lax.complex(f32(a_re), f32(a_im))
    dt = jnp.exp(f32(log_dt))[:, None]
    a_bar = jnp.exp(lam * dt)
    b_bar = ((a_bar - 1.0) / lam)[:, :, None] * lax.complex(f32(b_re), f32(b_im))
    c_mat = lax.complex(f32(c_re), f32(c_im))
    ug = f32(u).reshape(B, S, SSM_GROUPS, SSM_GROUP_CH)
    bu = jnp.einsum('gnc,bsgc->bsgn', b_bar, ug.astype(jnp.complex64))
    a_seq = jnp.broadcast_to(a_bar, bu.shape)

    def combine(left, right):
        a_l, b_l = left
        a_r, b_r = right
        return a_r * a_l, a_r * b_l + b_r

    _, xs = lax.associative_scan(combine, (a_seq, bu), axis=1)
    y = jnp.real(jnp.einsum('gcn,bsgn->bsgc', c_mat, xs)) \
        + f32(d_skip).reshape(SSM_GROUPS, SSM_GROUP_CH) * ug
    y = jax.nn.gelu(y.reshape(B, S, SSM_WIDTH))
    y = y * jax.nn.sigmoid(y @ f32(glu_w) + f32(glu_b))
    return y.astype(u.dtype)


def conv_ffn(x, w_up, conv_w, conv_b, w_down):
    h = x @ w_up
    ch = h.shape[-1]
    h = lax.conv_general_dilated(h, conv_w[:, None, :].astype(h.dtype), window_strides=(1,),
                                 padding=((FFN_CONV - 1, 0),),
                                 dimension_numbers=('NWC', 'WIO', 'NWC'),
                                 feature_group_count=ch) + conv_b
    val, gate = jnp.split(h, 2, axis=-1)
    return (jax.nn.gelu(gate) * val) @ w_down


def _fwd_setup_inputs(seed: int = 0) -> dict:
    key = jax.random.key(seed)
    ks = jax.random.split(key, 30)
    nrm = lambda k, shape, scale: jax.random.normal(k, shape, jnp.float32) * scale
    gain = lambda k, shape: 1.0 + 0.05 * jax.random.normal(k, shape, jnp.float32)
    L = DEPTH
    return {
        "x": nrm(ks[0], (BATCH, SEQ, D_MODEL), 1.0),
        "p": nrm(ks[1], (DEPTH, BATCH, SEQ, PLE_DIM), 1.0),
        "rel_bias": nrm(ks[2], (N_REL_BUCKETS, N_ATTN_HEADS), 0.5),
        "norm_attn_g": gain(ks[3], (L, D_MODEL)),
        "w_in": nrm(ks[4], (L, D_MODEL, IN_WIDTH), D_MODEL ** -0.5),
        "sgu_ln_g": gain(ks[5], (L, SGU_WIDTH)),
        "sgu_ln_b": nrm(ks[6], (L, SGU_WIDTH), 0.02),
        "sgu_w": nrm(ks[7], (L, SGU_GROUPS, SGU_CHUNK, SGU_CHUNK), 0.5 * SGU_CHUNK ** -0.5),
        "sgu_b": 1.0 + nrm(ks[8], (L, SGU_GROUPS, SGU_CHUNK), 0.01),
        "ssm_a_re": -0.5 + nrm(ks[9], (L, SSM_GROUPS, SSM_STATE), 0.01),
        "ssm_a_im": math.pi * jnp.arange(SSM_STATE, dtype=jnp.float32)
                    + nrm(ks[10], (L, SSM_GROUPS, SSM_STATE), 0.01),
        "ssm_log_dt": jax.random.uniform(ks[11], (L, SSM_GROUPS), jnp.float32,
                                         math.log(1e-3), math.log(1e-1)),
        "ssm_b_re": nrm(ks[12], (L, SSM_GROUPS, SSM_STATE, SSM_GROUP_CH), (2 * SSM_GROUP_CH) ** -0.5),
        "ssm_b_im": nrm(ks[13], (L, SSM_GROUPS, SSM_STATE, SSM_GROUP_CH), (2 * SSM_GROUP_CH) ** -0.5),
        "ssm_c_re": nrm(ks[14], (L, SSM_GROUPS, SSM_GROUP_CH, SSM_STATE), SSM_STATE ** -0.5),
        "ssm_c_im": nrm(ks[15], (L, SSM_GROUPS, SSM_GROUP_CH, SSM_STATE), SSM_STATE ** -0.5),
        "ssm_d": nrm(ks[16], (L, SSM_WIDTH), 1.0),
        "ssm_glu_w": nrm(ks[17], (L, SSM_WIDTH, SSM_WIDTH), SSM_WIDTH ** -0.5),
        "ssm_glu_b": nrm(ks[18], (L, SSM_WIDTH), 0.01),
        "branch_norm_g": gain(ks[19], (L, MIX_WIDTH)),
        "w_out": nrm(ks[20], (L, MIX_WIDTH, D_MODEL), MIX_WIDTH ** -0.5),
        "norm_ffn_g": gain(ks[21], (L, D_MODEL)),
        "ffn_w_up": nrm(ks[22], (L, D_MODEL, 2 * D_FF), D_MODEL ** -0.5),
        "ffn_conv_w": nrm(ks[23], (L, FFN_CONV, 2 * D_FF), FFN_CONV ** -0.5),
        "ffn_conv_b": nrm(ks[24], (L, 2 * D_FF), 0.01),
        "ffn_w_down": nrm(ks[25], (L, D_FF, D_MODEL), D_FF ** -0.5),
        "norm_ple_g": gain(ks[26], (L, D_MODEL)),
        "ple_w_gate": nrm(ks[27], (L, D_MODEL, D_MODEL), D_MODEL ** -0.5),
        "ple_w_proj": nrm(ks[28], (L, PLE_DIM, D_MODEL), PLE_DIM ** -0.5),
        "final_norm_g": gain(ks[29], (D_MODEL,)),
    }


def _fwd_reference(x, p, rel_bias, norm_attn_g, w_in, sgu_ln_g, sgu_ln_b, sgu_w, sgu_b,
              ssm_a_re, ssm_a_im, ssm_log_dt, ssm_b_re, ssm_b_im, ssm_c_re, ssm_c_im,
              ssm_d, ssm_glu_w, ssm_glu_b, branch_norm_g, w_out, norm_ffn_g,
              ffn_w_up, ffn_conv_w, ffn_conv_b, ffn_w_down, norm_ple_g, ple_w_gate,
              ple_w_proj, final_norm_g):
    B, S, _ = x.shape
    o_k = ATTN_WIDTH
    o_v = 2 * ATTN_WIDTH
    o_g = 3 * ATTN_WIDTH
    o_s = o_g + 2 * SGU_WIDTH
    h = x
    for i in range(DEPTH):
        z = rms_norm(h, norm_attn_g[i]) @ w_in[i]
        q = z[..., :o_k].reshape(B, S, N_ATTN_HEADS, HEAD_DIM)
        k = z[..., o_k:o_v].reshape(B, S, N_ATTN_HEADS, HEAD_DIM)
        v = z[..., o_v:o_g].reshape(B, S, N_ATTN_HEADS, HEAD_DIM)
        y_attn = dilated_attention(q, k, v, rel_bias).astype(h.dtype)
        y_sgu = spatial_gating(z[..., o_g:o_s], sgu_ln_g[i], sgu_ln_b[i], sgu_w[i], sgu_b[i])
        y_ssm = s5_ssm(z[..., o_s:], ssm_a_re[i], ssm_a_im[i], ssm_log_dt[i],
                       ssm_b_re[i], ssm_b_im[i], ssm_c_re[i], ssm_c_im[i],
                       ssm_d[i], ssm_glu_w[i], ssm_glu_b[i])
        mix = jnp.concatenate([rms_scale(y_attn), rms_scale(y_sgu), rms_scale(y_ssm)],
                              axis=-1) * branch_norm_g[i]
        h = h + mix @ w_out[i]
        h = h + conv_ffn(rms_norm(h, norm_ffn_g[i]), ffn_w_up[i], ffn_conv_w[i],
                         ffn_conv_b[i], ffn_w_down[i])
        gate = jax.nn.sigmoid(rms_norm(h, norm_ple_g[i]) @ ple_w_gate[i])
        h = h + gate * (p[i] @ ple_w_proj[i])
    return rms_norm(h, final_norm_g)


import jax as _jax
import jax.numpy as _jnp

TWIN_FORMAT = 'train_step'
FWD_PARAMS = ['x', 'p', 'rel_bias', 'norm_attn_g', 'w_in', 'sgu_ln_g', 'sgu_ln_b', 'sgu_w', 'sgu_b', 'ssm_a_re', 'ssm_a_im', 'ssm_log_dt', 'ssm_b_re', 'ssm_b_im', 'ssm_c_re', 'ssm_c_im', 'ssm_d', 'ssm_glu_w', 'ssm_glu_b', 'branch_norm_g', 'w_out', 'norm_ffn_g', 'ffn_w_up', 'ffn_conv_w', 'ffn_conv_b', 'ffn_w_down', 'norm_ple_g', 'ple_w_gate', 'ple_w_proj', 'final_norm_g']
TWIN_WEIGHTS = ['rel_bias', 'norm_attn_g', 'w_in', 'sgu_ln_g', 'sgu_ln_b', 'sgu_w', 'sgu_b', 'ssm_a_re', 'ssm_a_im', 'ssm_log_dt', 'ssm_b_re', 'ssm_b_im', 'ssm_c_re', 'ssm_c_im', 'ssm_d', 'ssm_glu_w', 'ssm_glu_b', 'branch_norm_g', 'w_out', 'norm_ffn_g', 'ffn_w_up', 'ffn_conv_w', 'ffn_conv_b', 'ffn_w_down', 'norm_ple_g', 'ple_w_gate', 'ple_w_proj', 'final_norm_g']
TWIN_DIFF_INPUT = 'x'
TWIN_INPUTS = ['x', 'p', 'rel_bias', 'norm_attn_g', 'w_in', 'sgu_ln_g', 'sgu_ln_b', 'sgu_w', 'sgu_b', 'ssm_a_re', 'ssm_a_im', 'ssm_log_dt', 'ssm_b_re', 'ssm_b_im', 'ssm_c_re', 'ssm_c_im', 'ssm_d', 'ssm_glu_w', 'ssm_glu_b', 'branch_norm_g', 'w_out', 'norm_ffn_g', 'ffn_w_up', 'ffn_conv_w', 'ffn_conv_b', 'ffn_w_down', 'norm_ple_g', 'ple_w_gate', 'ple_w_proj', 'final_norm_g', 'loss_target', 'm_rel_bias', 'm_norm_attn_g', 'm_w_in', 'm_sgu_ln_g', 'm_sgu_ln_b', 'm_sgu_w', 'm_sgu_b', 'm_ssm_a_re', 'm_ssm_a_im', 'm_ssm_log_dt', 'm_ssm_b_re', 'm_ssm_b_im', 'm_ssm_c_re', 'm_ssm_c_im', 'm_ssm_d', 'm_ssm_glu_w', 'm_ssm_glu_b', 'm_branch_norm_g', 'm_w_out', 'm_norm_ffn_g', 'm_ffn_w_up', 'm_ffn_conv_w', 'm_ffn_conv_b', 'm_ffn_w_down', 'm_norm_ple_g', 'm_ple_w_gate', 'm_ple_w_proj', 'm_final_norm_g', 'v_rel_bias', 'v_norm_attn_g', 'v_w_in', 'v_sgu_ln_g', 'v_sgu_ln_b', 'v_sgu_w', 'v_sgu_b', 'v_ssm_a_re', 'v_ssm_a_im', 'v_ssm_log_dt', 'v_ssm_b_re', 'v_ssm_b_im', 'v_ssm_c_re', 'v_ssm_c_im', 'v_ssm_d', 'v_ssm_glu_w', 'v_ssm_glu_b', 'v_branch_norm_g', 'v_w_out', 'v_norm_ffn_g', 'v_ffn_w_up', 'v_ffn_conv_w', 'v_ffn_conv_b', 'v_ffn_w_down', 'v_norm_ple_g', 'v_ple_w_gate', 'v_ple_w_proj', 'v_final_norm_g']
TWIN_OUTPUTS = ['loss', 'grad_x', 'grad_rel_bias', 'grad_norm_attn_g', 'grad_w_in', 'grad_sgu_ln_g', 'grad_sgu_ln_b', 'grad_sgu_w', 'grad_sgu_b', 'grad_ssm_a_re', 'grad_ssm_a_im', 'grad_ssm_log_dt', 'grad_ssm_b_re', 'grad_ssm_b_im', 'grad_ssm_c_re', 'grad_ssm_c_im', 'grad_ssm_d', 'grad_ssm_glu_w', 'grad_ssm_glu_b', 'grad_branch_norm_g', 'grad_w_out', 'grad_norm_ffn_g', 'grad_ffn_w_up', 'grad_ffn_conv_w', 'grad_ffn_conv_b', 'grad_ffn_w_down', 'grad_norm_ple_g', 'grad_ple_w_gate', 'grad_ple_w_proj', 'grad_final_norm_g', 'delta_rel_bias', 'delta_norm_attn_g', 'delta_w_in', 'delta_sgu_ln_g', 'delta_sgu_ln_b', 'delta_sgu_w', 'delta_sgu_b', 'delta_ssm_a_re', 'delta_ssm_a_im', 'delta_ssm_log_dt', 'delta_ssm_b_re', 'delta_ssm_b_im', 'delta_ssm_c_re', 'delta_ssm_c_im', 'delta_ssm_d', 'delta_ssm_glu_w', 'delta_ssm_glu_b', 'delta_branch_norm_g', 'delta_w_out', 'delta_norm_ffn_g', 'delta_ffn_w_up', 'delta_ffn_conv_w', 'delta_ffn_conv_b', 'delta_ffn_w_down', 'delta_norm_ple_g', 'delta_ple_w_gate', 'delta_ple_w_proj', 'delta_final_norm_g', 'new_m_rel_bias', 'new_m_norm_attn_g', 'new_m_w_in', 'new_m_sgu_ln_g', 'new_m_sgu_ln_b', 'new_m_sgu_w', 'new_m_sgu_b', 'new_m_ssm_a_re', 'new_m_ssm_a_im', 'new_m_ssm_log_dt', 'new_m_ssm_b_re', 'new_m_ssm_b_im', 'new_m_ssm_c_re', 'new_m_ssm_c_im', 'new_m_ssm_d', 'new_m_ssm_glu_w', 'new_m_ssm_glu_b', 'new_m_branch_norm_g', 'new_m_w_out', 'new_m_norm_ffn_g', 'new_m_ffn_w_up', 'new_m_ffn_conv_w', 'new_m_ffn_conv_b', 'new_m_ffn_w_down', 'new_m_norm_ple_g', 'new_m_ple_w_gate', 'new_m_ple_w_proj', 'new_m_final_norm_g', 'new_v_rel_bias', 'new_v_norm_attn_g', 'new_v_w_in', 'new_v_sgu_ln_g', 'new_v_sgu_ln_b', 'new_v_sgu_w', 'new_v_sgu_b', 'new_v_ssm_a_re', 'new_v_ssm_a_im', 'new_v_ssm_log_dt', 'new_v_ssm_b_re', 'new_v_ssm_b_im', 'new_v_ssm_c_re', 'new_v_ssm_c_im', 'new_v_ssm_d', 'new_v_ssm_glu_w', 'new_v_ssm_glu_b', 'new_v_branch_norm_g', 'new_v_w_out', 'new_v_norm_ffn_g', 'new_v_ffn_w_up', 'new_v_ffn_conv_w', 'new_v_ffn_conv_b', 'new_v_ffn_w_down', 'new_v_norm_ple_g', 'new_v_ple_w_gate', 'new_v_ple_w_proj', 'new_v_final_norm_g']
TWIN_LEAF_KINDS = {'loss': 'loss', 'grad_x': 'grad_x', 'grad_rel_bias': 'grad_w', 'grad_norm_attn_g': 'grad_w', 'grad_w_in': 'grad_w', 'grad_sgu_ln_g': 'grad_w', 'grad_sgu_ln_b': 'grad_w', 'grad_sgu_w': 'grad_w', 'grad_sgu_b': 'grad_w', 'grad_ssm_a_re': 'grad_w', 'grad_ssm_a_im': 'grad_w', 'grad_ssm_log_dt': 'grad_w', 'grad_ssm_b_re': 'grad_w', 'grad_ssm_b_im': 'grad_w', 'grad_ssm_c_re': 'grad_w', 'grad_ssm_c_im': 'grad_w', 'grad_ssm_d': 'grad_w', 'grad_ssm_glu_w': 'grad_w', 'grad_ssm_glu_b': 'grad_w', 'grad_branch_norm_g': 'grad_w', 'grad_w_out': 'grad_w', 'grad_norm_ffn_g': 'grad_w', 'grad_ffn_w_up': 'grad_w', 'grad_ffn_conv_w': 'grad_w', 'grad_ffn_conv_b': 'grad_w', 'grad_ffn_w_down': 'grad_w', 'grad_norm_ple_g': 'grad_w', 'grad_ple_w_gate': 'grad_w', 'grad_ple_w_proj': 'grad_w', 'grad_final_norm_g': 'grad_w', 'delta_rel_bias': 'delta_w', 'delta_norm_attn_g': 'delta_w', 'delta_w_in': 'delta_w', 'delta_sgu_ln_g': 'delta_w', 'delta_sgu_ln_b': 'delta_w', 'delta_sgu_w': 'delta_w', 'delta_sgu_b': 'delta_w', 'delta_ssm_a_re': 'delta_w', 'delta_ssm_a_im': 'delta_w', 'delta_ssm_log_dt': 'delta_w', 'delta_ssm_b_re': 'delta_w', 'delta_ssm_b_im': 'delta_w', 'delta_ssm_c_re': 'delta_w', 'delta_ssm_c_im': 'delta_w', 'delta_ssm_d': 'delta_w', 'delta_ssm_glu_w': 'delta_w', 'delta_ssm_glu_b': 'delta_w', 'delta_branch_norm_g': 'delta_w', 'delta_w_out': 'delta_w', 'delta_norm_ffn_g': 'delta_w', 'delta_ffn_w_up': 'delta_w', 'delta_ffn_conv_w': 'delta_w', 'delta_ffn_conv_b': 'delta_w', 'delta_ffn_w_down': 'delta_w', 'delta_norm_ple_g': 'delta_w', 'delta_ple_w_gate': 'delta_w', 'delta_ple_w_proj': 'delta_w', 'delta_final_norm_g': 'delta_w', 'new_m_rel_bias': 'new_m', 'new_m_norm_attn_g': 'new_m', 'new_m_w_in': 'new_m', 'new_m_sgu_ln_g': 'new_m', 'new_m_sgu_ln_b': 'new_m', 'new_m_sgu_w': 'new_m', 'new_m_sgu_b': 'new_m', 'new_m_ssm_a_re': 'new_m', 'new_m_ssm_a_im': 'new_m', 'new_m_ssm_log_dt': 'new_m', 'new_m_ssm_b_re': 'new_m', 'new_m_ssm_b_im': 'new_m', 'new_m_ssm_c_re': 'new_m', 'new_m_ssm_c_im': 'new_m', 'new_m_ssm_d': 'new_m', 'new_m_ssm_glu_w': 'new_m', 'new_m_ssm_glu_b': 'new_m', 'new_m_branch_norm_g': 'new_m', 'new_m_w_out': 'new_m', 'new_m_norm_ffn_g': 'new_m', 'new_m_ffn_w_up': 'new_m', 'new_m_ffn_conv_w': 'new_m', 'new_m_ffn_conv_b': 'new_m', 'new_m_ffn_w_down': 'new_m', 'new_m_norm_ple_g': 'new_m', 'new_m_ple_w_gate': 'new_m', 'new_m_ple_w_proj': 'new_m', 'new_m_final_norm_g': 'new_m', 'new_v_rel_bias': 'new_v', 'new_v_norm_attn_g': 'new_v', 'new_v_w_in': 'new_v', 'new_v_sgu_ln_g': 'new_v', 'new_v_sgu_ln_b': 'new_v', 'new_v_sgu_w': 'new_v', 'new_v_sgu_b': 'new_v', 'new_v_ssm_a_re': 'new_v', 'new_v_ssm_a_im': 'new_v', 'new_v_ssm_log_dt': 'new_v', 'new_v_ssm_b_re': 'new_v', 'new_v_ssm_b_im': 'new_v', 'new_v_ssm_c_re': 'new_v', 'new_v_ssm_c_im': 'new_v', 'new_v_ssm_d': 'new_v', 'new_v_ssm_glu_w': 'new_v', 'new_v_ssm_glu_b': 'new_v', 'new_v_branch_norm_g': 'new_v', 'new_v_w_out': 'new_v', 'new_v_norm_ffn_g': 'new_v', 'new_v_ffn_w_up': 'new_v', 'new_v_ffn_conv_w': 'new_v', 'new_v_ffn_conv_b': 'new_v', 'new_v_ffn_w_down': 'new_v', 'new_v_norm_ple_g': 'new_v', 'new_v_ple_w_gate': 'new_v', 'new_v_ple_w_proj': 'new_v', 'new_v_final_norm_g': 'new_v'}


def _forward(args):
    return _fwd_reference(*[args[k] for k in FWD_PARAMS])


def _output_shape():
    def fwd():
        inp = _fwd_setup_inputs(0)
        return _fwd_reference(*[inp[k] for k in FWD_PARAMS])
    out = _jax.eval_shape(fwd)
    return out.shape, out.dtype

N_MICROBATCH = 1
ADAM_LR = 0.001
ADAM_B1 = 0.9
ADAM_B2 = 0.999
ADAM_EPS = 1e-08
ADAM_WD = 0.01
ADAM_STEP = 10
PER_EXAMPLE_BATCH_AXIS = {'x': 0, 'p': 1, 'loss_target': 0}
SHARED_INPUTS = []
_WEIGHT_DTYPES = {'rel_bias': _jnp.float32, 'norm_attn_g': _jnp.float32, 'w_in': _jnp.float32, 'sgu_ln_g': _jnp.float32, 'sgu_ln_b': _jnp.float32, 'sgu_w': _jnp.float32, 'sgu_b': _jnp.float32, 'ssm_a_re': _jnp.float32, 'ssm_a_im': _jnp.float32, 'ssm_log_dt': _jnp.float32, 'ssm_b_re': _jnp.float32, 'ssm_b_im': _jnp.float32, 'ssm_c_re': _jnp.float32, 'ssm_c_im': _jnp.float32, 'ssm_d': _jnp.float32, 'ssm_glu_w': _jnp.float32, 'ssm_glu_b': _jnp.float32, 'branch_norm_g': _jnp.float32, 'w_out': _jnp.float32, 'norm_ffn_g': _jnp.float32, 'ffn_w_up': _jnp.float32, 'ffn_conv_w': _jnp.float32, 'ffn_conv_b': _jnp.float32, 'ffn_w_down': _jnp.float32, 'norm_ple_g': _jnp.float32, 'ple_w_gate': _jnp.float32, 'ple_w_proj': _jnp.float32, 'final_norm_g': _jnp.float32}
MOMENT_SCALE = {'rel_bias': 2.719916e-01, 'norm_attn_g': 3.105974e-01, 'w_in': 1.955230e-01, 'sgu_ln_g': 5.445404e-02, 'sgu_ln_b': 5.803787e-02, 'sgu_w': 7.590255e-02, 'sgu_b': 9.533707e-02, 'ssm_a_re': 2.681161e-02, 'ssm_a_im': 2.106981e-02, 'ssm_log_dt': 1.296047e+01, 'ssm_b_re': 1.090659e-02, 'ssm_b_im': 1.313401e-02, 'ssm_c_re': 1.639541e-02, 'ssm_c_im': 1.776020e-02, 'ssm_d': 5.893276e-01, 'ssm_glu_w': 9.121505e-02, 'ssm_glu_b': 2.912772e-01, 'branch_norm_g': 4.622900e-01, 'w_out': 4.207770e-01, 'norm_ffn_g': 1.668776e-01, 'ffn_w_up': 6.479542e-02, 'ffn_conv_w': 6.787619e-02, 'ffn_conv_b': 1.650147e-01, 'ffn_w_down': 1.172318e-01, 'norm_ple_g': 3.362103e-02, 'ple_w_gate': 3.611319e-02, 'ple_w_proj': 7.539498e-02, 'final_norm_g': 6.465338e+01}


def _to_microbatches(a, axis):
    t = _jnp.moveaxis(a, axis, 0)
    t = t.reshape((N_MICROBATCH, t.shape[0] // N_MICROBATCH) + t.shape[1:])
    return _jnp.moveaxis(t, 1, axis + 1)


def setup_inputs(seed: int = 0) -> dict:
    inp = _fwd_setup_inputs(seed)
    key = _jax.random.fold_in(_jax.random.key(seed), 7919)
    shape, _ = _output_shape()
    out = dict(inp)
    out["loss_target"] = _jax.random.normal(_jax.random.fold_in(key, 0), shape, _jnp.float32)
    for i, name in enumerate(TWIN_WEIGHTS):
        w = inp[name].astype(_jnp.float32)
        if MOMENT_SCALE is None:
            s = _jnp.sqrt(_jnp.mean(_jnp.square(w)) + 1e-30)
        else:
            s = MOMENT_SCALE[name]
        km, kv = _jax.random.split(_jax.random.fold_in(key, i + 1))
        out[name] = w
        out["m_" + name] = s * _jax.random.normal(km, w.shape, _jnp.float32)
        out["v_" + name] = (s * s) * _jax.random.uniform(kv, w.shape, _jnp.float32, 0.5, 1.5)
    if N_MICROBATCH > 1:
        for name, axis in PER_EXAMPLE_BATCH_AXIS.items():
            out[name] = _to_microbatches(out[name], axis)
    return {'x': out['x'], 'p': out['p'], 'rel_bias': out['rel_bias'], 'norm_attn_g': out['norm_attn_g'], 'w_in': out['w_in'], 'sgu_ln_g': out['sgu_ln_g'], 'sgu_ln_b': out['sgu_ln_b'], 'sgu_w': out['sgu_w'], 'sgu_b': out['sgu_b'], 'ssm_a_re': out['ssm_a_re'], 'ssm_a_im': out['ssm_a_im'], 'ssm_log_dt': out['ssm_log_dt'], 'ssm_b_re': out['ssm_b_re'], 'ssm_b_im': out['ssm_b_im'], 'ssm_c_re': out['ssm_c_re'], 'ssm_c_im': out['ssm_c_im'], 'ssm_d': out['ssm_d'], 'ssm_glu_w': out['ssm_glu_w'], 'ssm_glu_b': out['ssm_glu_b'], 'branch_norm_g': out['branch_norm_g'], 'w_out': out['w_out'], 'norm_ffn_g': out['norm_ffn_g'], 'ffn_w_up': out['ffn_w_up'], 'ffn_conv_w': out['ffn_conv_w'], 'ffn_conv_b': out['ffn_conv_b'], 'ffn_w_down': out['ffn_w_down'], 'norm_ple_g': out['norm_ple_g'], 'ple_w_gate': out['ple_w_gate'], 'ple_w_proj': out['ple_w_proj'], 'final_norm_g': out['final_norm_g'], 'loss_target': out['loss_target'], 'm_rel_bias': out['m_rel_bias'], 'm_norm_attn_g': out['m_norm_attn_g'], 'm_w_in': out['m_w_in'], 'm_sgu_ln_g': out['m_sgu_ln_g'], 'm_sgu_ln_b': out['m_sgu_ln_b'], 'm_sgu_w': out['m_sgu_w'], 'm_sgu_b': out['m_sgu_b'], 'm_ssm_a_re': out['m_ssm_a_re'], 'm_ssm_a_im': out['m_ssm_a_im'], 'm_ssm_log_dt': out['m_ssm_log_dt'], 'm_ssm_b_re': out['m_ssm_b_re'], 'm_ssm_b_im': out['m_ssm_b_im'], 'm_ssm_c_re': out['m_ssm_c_re'], 'm_ssm_c_im': out['m_ssm_c_im'], 'm_ssm_d': out['m_ssm_d'], 'm_ssm_glu_w': out['m_ssm_glu_w'], 'm_ssm_glu_b': out['m_ssm_glu_b'], 'm_branch_norm_g': out['m_branch_norm_g'], 'm_w_out': out['m_w_out'], 'm_norm_ffn_g': out['m_norm_ffn_g'], 'm_ffn_w_up': out['m_ffn_w_up'], 'm_ffn_conv_w': out['m_ffn_conv_w'], 'm_ffn_conv_b': out['m_ffn_conv_b'], 'm_ffn_w_down': out['m_ffn_w_down'], 'm_norm_ple_g': out['m_norm_ple_g'], 'm_ple_w_gate': out['m_ple_w_gate'], 'm_ple_w_proj': out['m_ple_w_proj'], 'm_final_norm_g': out['m_final_norm_g'], 'v_rel_bias': out['v_rel_bias'], 'v_norm_attn_g': out['v_norm_attn_g'], 'v_w_in': out['v_w_in'], 'v_sgu_ln_g': out['v_sgu_ln_g'], 'v_sgu_ln_b': out['v_sgu_ln_b'], 'v_sgu_w': out['v_sgu_w'], 'v_sgu_b': out['v_sgu_b'], 'v_ssm_a_re': out['v_ssm_a_re'], 'v_ssm_a_im': out['v_ssm_a_im'], 'v_ssm_log_dt': out['v_ssm_log_dt'], 'v_ssm_b_re': out['v_ssm_b_re'], 'v_ssm_b_im': out['v_ssm_b_im'], 'v_ssm_c_re': out['v_ssm_c_re'], 'v_ssm_c_im': out['v_ssm_c_im'], 'v_ssm_d': out['v_ssm_d'], 'v_ssm_glu_w': out['v_ssm_glu_w'], 'v_ssm_glu_b': out['v_ssm_glu_b'], 'v_branch_norm_g': out['v_branch_norm_g'], 'v_w_out': out['v_w_out'], 'v_norm_ffn_g': out['v_norm_ffn_g'], 'v_ffn_w_up': out['v_ffn_w_up'], 'v_ffn_conv_w': out['v_ffn_conv_w'], 'v_ffn_conv_b': out['v_ffn_conv_b'], 'v_ffn_w_down': out['v_ffn_w_down'], 'v_norm_ple_g': out['v_norm_ple_g'], 'v_ple_w_gate': out['v_ple_w_gate'], 'v_ple_w_proj': out['v_ple_w_proj'], 'v_final_norm_g': out['v_final_norm_g']}


def _loss(weights, diff, rest, loss_target):
    with _jax.named_scope("forward"):
        args = {**rest, TWIN_DIFF_INPUT: diff, **{k: w.astype(_WEIGHT_DTYPES[k]) for k, w in weights.items()}}
        y = _forward(args)
    with _jax.named_scope("loss_head"):
        err = _jnp.square(y.astype(_jnp.float32) - loss_target)
        return 0.5 * _jnp.sum(_jnp.mean(err, axis=-1)) if err.ndim else 0.5 * err


def _adamw(w, g, m, v):
    m = ADAM_B1 * m + (1.0 - ADAM_B1) * g
    v = ADAM_B2 * v + (1.0 - ADAM_B2) * _jnp.square(g)
    m_hat = m / (1.0 - ADAM_B1 ** ADAM_STEP)
    v_hat = v / (1.0 - ADAM_B2 ** ADAM_STEP)
    delta = -ADAM_LR * (m_hat / (_jnp.sqrt(v_hat) + ADAM_EPS) + ADAM_WD * w)
    return delta, m, v


def reference(x, p, rel_bias, norm_attn_g, w_in, sgu_ln_g, sgu_ln_b, sgu_w, sgu_b, ssm_a_re, ssm_a_im, ssm_log_dt, ssm_b_re, ssm_b_im, ssm_c_re, ssm_c_im, ssm_d, ssm_glu_w, ssm_glu_b, branch_norm_g, w_out, norm_ffn_g, ffn_w_up, ffn_conv_w, ffn_conv_b, ffn_w_down, norm_ple_g, ple_w_gate, ple_w_proj, final_norm_g, loss_target, m_rel_bias, m_norm_attn_g, m_w_in, m_sgu_ln_g, m_sgu_ln_b, m_sgu_w, m_sgu_b, m_ssm_a_re, m_ssm_a_im, m_ssm_log_dt, m_ssm_b_re, m_ssm_b_im, m_ssm_c_re, m_ssm_c_im, m_ssm_d, m_ssm_glu_w, m_ssm_glu_b, m_branch_norm_g, m_w_out, m_norm_ffn_g, m_ffn_w_up, m_ffn_conv_w, m_ffn_conv_b, m_ffn_w_down, m_norm_ple_g, m_ple_w_gate, m_ple_w_proj, m_final_norm_g, v_rel_bias, v_norm_attn_g, v_w_in, v_sgu_ln_g, v_sgu_ln_b, v_sgu_w, v_sgu_b, v_ssm_a_re, v_ssm_a_im, v_ssm_log_dt, v_ssm_b_re, v_ssm_b_im, v_ssm_c_re, v_ssm_c_im, v_ssm_d, v_ssm_glu_w, v_ssm_glu_b, v_branch_norm_g, v_w_out, v_norm_ffn_g, v_ffn_w_up, v_ffn_conv_w, v_ffn_conv_b, v_ffn_w_down, v_norm_ple_g, v_ple_w_gate, v_ple_w_proj, v_final_norm_g):
    given = dict(x=x, p=p, rel_bias=rel_bias, norm_attn_g=norm_attn_g, w_in=w_in, sgu_ln_g=sgu_ln_g, sgu_ln_b=sgu_ln_b, sgu_w=sgu_w, sgu_b=sgu_b, ssm_a_re=ssm_a_re, ssm_a_im=ssm_a_im, ssm_log_dt=ssm_log_dt, ssm_b_re=ssm_b_re, ssm_b_im=ssm_b_im, ssm_c_re=ssm_c_re, ssm_c_im=ssm_c_im, ssm_d=ssm_d, ssm_glu_w=ssm_glu_w, ssm_glu_b=ssm_glu_b, branch_norm_g=branch_norm_g, w_out=w_out, norm_ffn_g=norm_ffn_g, ffn_w_up=ffn_w_up, ffn_conv_w=ffn_conv_w, ffn_conv_b=ffn_conv_b, ffn_w_down=ffn_w_down, norm_ple_g=norm_ple_g, ple_w_gate=ple_w_gate, ple_w_proj=ple_w_proj, final_norm_g=final_norm_g, loss_target=loss_target, m_rel_bias=m_rel_bias, m_norm_attn_g=m_norm_attn_g, m_w_in=m_w_in, m_sgu_ln_g=m_sgu_ln_g, m_sgu_ln_b=m_sgu_ln_b, m_sgu_w=m_sgu_w, m_sgu_b=m_sgu_b, m_ssm_a_re=m_ssm_a_re, m_ssm_a_im=m_ssm_a_im, m_ssm_log_dt=m_ssm_log_dt, m_ssm_b_re=m_ssm_b_re, m_ssm_b_im=m_ssm_b_im, m_ssm_c_re=m_ssm_c_re, m_ssm_c_im=m_ssm_c_im, m_ssm_d=m_ssm_d, m_ssm_glu_w=m_ssm_glu_w, m_ssm_glu_b=m_ssm_glu_b, m_branch_norm_g=m_branch_norm_g, m_w_out=m_w_out, m_norm_ffn_g=m_norm_ffn_g, m_ffn_w_up=m_ffn_w_up, m_ffn_conv_w=m_ffn_conv_w, m_ffn_conv_b=m_ffn_conv_b, m_ffn_w_down=m_ffn_w_down, m_norm_ple_g=m_norm_ple_g, m_ple_w_gate=m_ple_w_gate, m_ple_w_proj=m_ple_w_proj, m_final_norm_g=m_final_norm_g, v_rel_bias=v_rel_bias, v_norm_attn_g=v_norm_attn_g, v_w_in=v_w_in, v_sgu_ln_g=v_sgu_ln_g, v_sgu_ln_b=v_sgu_ln_b, v_sgu_w=v_sgu_w, v_sgu_b=v_sgu_b, v_ssm_a_re=v_ssm_a_re, v_ssm_a_im=v_ssm_a_im, v_ssm_log_dt=v_ssm_log_dt, v_ssm_b_re=v_ssm_b_re, v_ssm_b_im=v_ssm_b_im, v_ssm_c_re=v_ssm_c_re, v_ssm_c_im=v_ssm_c_im, v_ssm_d=v_ssm_d, v_ssm_glu_w=v_ssm_glu_w, v_ssm_glu_b=v_ssm_glu_b, v_branch_norm_g=v_branch_norm_g, v_w_out=v_w_out, v_norm_ffn_g=v_norm_ffn_g, v_ffn_w_up=v_ffn_w_up, v_ffn_conv_w=v_ffn_conv_w, v_ffn_conv_b=v_ffn_conv_b, v_ffn_w_down=v_ffn_w_down, v_norm_ple_g=v_norm_ple_g, v_ple_w_gate=v_ple_w_gate, v_ple_w_proj=v_ple_w_proj, v_final_norm_g=v_final_norm_g)
    weights = {n: given[n] for n in TWIN_WEIGHTS}
    shared = {n: given[n] for n in SHARED_INPUTS}
    per_example = {n: given[n] for n in ['x', 'p']}
    grad_fn = _jax.value_and_grad(_loss, argnums=(0, 1))

    def one_microbatch(ex, loss_target):
        ex = dict(ex)
        diff = ex.pop(TWIN_DIFF_INPUT)
        return grad_fn(weights, diff, {**shared, **ex}, loss_target)

    if N_MICROBATCH == 1:
        loss, (grad_w, grad_x) = one_microbatch(per_example, given["loss_target"])
    else:
        def body(carry, xs):
            loss_sum, grad_sum = carry
            l_k, (gw_k, gx_k) = one_microbatch(xs[0], xs[1])
            with _jax.named_scope("update"):
                return (loss_sum + l_k, _jax.tree.map(_jnp.add, grad_sum, gw_k)), gx_k

        init = (_jnp.zeros((), _jnp.float32), _jax.tree.map(_jnp.zeros_like, weights))
        (loss, grad_w), grad_x = _jax.lax.scan(body, init, (per_example, given["loss_target"]))
    with _jax.named_scope("update"):
        delta_w, new_m, new_v = {}, {}, {}
        for n in TWIN_WEIGHTS:
            delta_w[n], new_m[n], new_v[n] = _adamw(weights[n], grad_w[n], given["m_" + n], given["v_" + n])
    return (loss, grad_x, *[grad_w[n] for n in TWIN_WEIGHTS], *[delta_w[n] for n in TWIN_WEIGHTS],
            *[new_m[n] for n in TWIN_WEIGHTS], *[new_v[n] for n in TWIN_WEIGHTS])
```

```python
import functools
import math

import numpy as np
import jax
import jax.numpy as jnp
from jax import lax
from jax.experimental import pallas as pl
from jax.experimental.pallas import tpu as pltpu

F32 = jnp.float32
BF16 = jnp.bfloat16

D_MODEL = 1024
HEAD_DIM = 64
N_HEADS = 8
ATTN_W = 512
SGU_W = 256
SGU_GROUPS = 4
SGU_CHUNK = 128
SSM_W = 256
SSM_GROUPS = 16
SSM_CH = 16
SSM_STATE = 64
SSM_NS = SSM_GROUPS * SSM_STATE
IN_W = 2304
D_FF = 2816
PLE_DIM = 256
BRANCHES = ((128, 1), (512, 4), (2048, 16))
BLK = 128
N_BUCKETS = 32
REL_MAX = 2048
EPS = 1e-6
NEG_INF = -1e30
N_DEV = 8

ADAM_LR = 0.001
ADAM_B1 = 0.9
ADAM_B2 = 0.999
ADAM_EPS = 1e-08
ADAM_WD = 0.01
ADAM_STEP = 10

VMEM_LIMIT_BYTES = 56 * 1024 * 1024
GELU_C = math.sqrt(2.0 / math.pi)

BIG_NAMES = ("w_in", "ssm_glu_w", "w_out", "ffn_w_up", "ffn_conv_w", "ffn_w_down", "ple_w_gate", "ple_w_proj")
BIG_AXIS = {"w_in": 2, "ssm_glu_w": 1, "w_out": 1, "ffn_w_up": 2, "ffn_conv_w": 2, "ffn_w_down": 1,
            "ple_w_gate": 1, "ple_w_proj": 2}
SMALL_NAMES = ("rel_bias", "norm_attn_g", "sgu_ln_g", "sgu_ln_b", "sgu_w", "sgu_b", "ssm_a_re", "ssm_a_im",
               "ssm_log_dt", "ssm_b_re", "ssm_b_im", "ssm_c_re", "ssm_c_im", "ssm_d", "ssm_glu_b",
               "branch_norm_g", "norm_ffn_g", "ffn_conv_b", "norm_ple_g", "final_norm_g")
WEIGHT_NAMES = ("rel_bias", "norm_attn_g", "w_in", "sgu_ln_g", "sgu_ln_b", "sgu_w", "sgu_b", "ssm_a_re",
                "ssm_a_im", "ssm_log_dt", "ssm_b_re", "ssm_b_im", "ssm_c_re", "ssm_c_im", "ssm_d", "ssm_glu_w",
                "ssm_glu_b", "branch_norm_g", "w_out", "norm_ffn_g", "ffn_w_up", "ffn_conv_w", "ffn_conv_b",
                "ffn_w_down", "norm_ple_g", "ple_w_gate", "ple_w_proj", "final_norm_g")
PACK_COLS = 512
PACK_ROW_ALIGN = 16


def _params(sem):
    return pltpu.CompilerParams(dimension_semantics=sem, vmem_limit_bytes=VMEM_LIMIT_BYTES)


def _pick(dim, target):
    if dim <= target:
        return dim
    best = None
    for t in range(128, target + 1, 128):
        if dim % t == 0:
            best = t
    return dim if best is None else best


def _gelu(x):
    return 0.5 * x * (1.0 + jnp.tanh(GELU_C * (x + 0.044715 * (x * x * x))))


def _gelu_grad(x):
    t = jnp.tanh(GELU_C * (x + 0.044715 * (x * x * x)))
    return 0.5 * (1.0 + t) + 0.5 * x * (1.0 - t * t) * (GELU_C * (1.0 + 3.0 * 0.044715 * (x * x)))


def _sigmoid(x):
    return 1.0 / (1.0 + jnp.exp(-x))


_DIMS = {"nn": (((1,), (0,)), ((), ())), "tn": (((0,), (0,)), ((), ())), "nt": (((1,), (1,)), ((), ()))}


def _mm(a, b, mode, name, add=None, tm=512, tn=1024, tk=512):
    if mode == "nn":
        m, k = a.shape
        k2, n = b.shape
    elif mode == "tn":
        k, m = a.shape
        k2, n = b.shape
    else:
        m, k = a.shape
        n, k2 = b.shape
    assert k == k2, (name, a.shape, b.shape, mode)
    tm, tn, tk = _pick(m, tm), _pick(n, tn), _pick(k, tk)
    nk = k // tk
    dims = _DIMS[mode]
    has_add = add is not None

    def body(*refs):
        if has_add:
            a_ref, b_ref, add_ref, o_ref, acc_ref = refs
        else:
            a_ref, b_ref, o_ref, acc_ref = refs
        kk = pl.program_id(2)
        part = lax.dot_general(a_ref[...].astype(BF16), b_ref[...].astype(BF16), dims,
                               preferred_element_type=F32)

        @pl.when(kk == 0)
        def _():
            acc_ref[...] = part

        @pl.when(kk > 0)
        def _():
            acc_ref[...] += part

        @pl.when(kk == nk - 1)
        def _():
            r = acc_ref[...]
            if has_add:
                r = r + add_ref[...]
            o_ref[...] = r

    if mode == "tn":
        a_spec = pl.BlockSpec((tk, tm), lambda i, j, kk: (kk, i))
    else:
        a_spec = pl.BlockSpec((tm, tk), lambda i, j, kk: (i, kk))
    if mode == "nt":
        b_spec = pl.BlockSpec((tn, tk), lambda i, j, kk: (j, kk))
    else:
        b_spec = pl.BlockSpec((tk, tn), lambda i, j, kk: (kk, j))
    o_spec = pl.BlockSpec((tm, tn), lambda i, j, kk: (i, j))
    in_specs = [a_spec, b_spec] + ([o_spec] if has_add else [])
    args = (a, b) + ((add,) if has_add else ())
    return pl.pallas_call(
        body, name=name, grid=(m // tm, n // tn, nk),
        in_specs=in_specs, out_specs=o_spec,
        out_shape=jax.ShapeDtypeStruct((m, n), F32),
        scratch_shapes=[pltpu.VMEM((tm, tn), F32)],
        compiler_params=_params(("parallel", "parallel", "arbitrary")),
    )(*args)


def _rb(tm, w, cb=0):
    return pl.BlockSpec((tm, w), lambda i: (i, cb))


def _fb(shape):
    nd = len(shape)
    return pl.BlockSpec(shape, lambda i: (0,) * nd)


def _rowcall(body, name, n_rows, tm, in_specs, args, out_specs, out_shapes):
    return pl.pallas_call(
        body, name=name, grid=(n_rows // tm,), in_specs=in_specs, out_specs=out_specs, out_shape=out_shapes,
        compiler_params=_params(("arbitrary",)),
    )(*args)


def _sds(shape):
    return jax.ShapeDtypeStruct(shape, F32)


def _rms_fwd(h, g, name, tm=256):
    s, d = h.shape

    def body(h_ref, g_ref, o_ref):
        x = h_ref[...]
        r = lax.rsqrt(jnp.mean(x * x, axis=-1, keepdims=True) + EPS)
        o_ref[...] = x * r * g_ref[...]

    return _rowcall(body, name, s, tm, [_rb(tm, d), _fb((1, d))], (h, g.reshape(1, d)), _rb(tm, d), _sds((s, d)))


def _rms_bwd(da, h, g, dres, name, tm=256):
    s, d = h.shape

    def body(da_ref, h_ref, g_ref, dres_ref, dh_ref, dg_ref):
        @pl.when(pl.program_id(0) == 0)
        def _():
            dg_ref[...] = jnp.zeros_like(dg_ref)

        x = h_ref[...]
        r = lax.rsqrt(jnp.mean(x * x, axis=-1, keepdims=True) + EPS)
        xh = x * r
        dy = da_ref[...]
        dg_ref[...] += jnp.sum(dy * xh, axis=0, keepdims=True)
        dxh = dy * g_ref[...]
        dh_ref[...] = dres_ref[...] + r * (dxh - xh * jnp.mean(dxh * xh, axis=-1, keepdims=True))

    dh, dg = _rowcall(body, name, s, tm, [_rb(tm, d), _rb(tm, d), _fb((1, d)), _rb(tm, d)],
                      (da, h, g.reshape(1, d), dres), [_rb(tm, d), _fb((1, d))], [_sds((s, d)), _sds((1, d))])
    return dh, dg.reshape(d)


def _loss_head(h, target, g, name, tm=256):
    s, d = h.shape

    def body(h_ref, t_ref, g_ref, dh_ref, loss_ref, dg_ref):
        @pl.when(pl.program_id(0) == 0)
        def _():
            dg_ref[...] = jnp.zeros_like(dg_ref)
            loss_ref[...] = jnp.zeros_like(loss_ref)

        x = h_ref[...]
        r = lax.rsqrt(jnp.mean(x * x, axis=-1, keepdims=True) + EPS)
        xh = x * r
        gg = g_ref[...]
        err = xh * gg - t_ref[...]
        loss_ref[...] += jnp.sum(err * err) * (0.5 / d)
        dy = err * (1.0 / d)
        dg_ref[...] += jnp.sum(dy * xh, axis=0, keepdims=True)
        dxh = dy * gg
        dh_ref[...] = r * (dxh - xh * jnp.mean(dxh * xh, axis=-1, keepdims=True))

    dh, loss, dg = _rowcall(body, name, s, tm, [_rb(tm, d), _rb(tm, d), _fb((1, d))], (h, target, g.reshape(1, d)),
                            [_rb(tm, d), _fb((1, 128)), _fb((1, d))], [_sds((s, d)), _sds((1, 128)), _sds((1, d))])
    return dh, loss[0, 0], dg.reshape(d)


_MIX_PARTS = ((0, 512), (512, 768), (768, 1024))


def _mix_fwd(ya, ysg, yss, g, name, tm=256):
    s = ya.shape[0]

    def body(a_ref, b_ref, c_ref, g_ref, o_ref):
        for ref, (lo, hi) in zip((a_ref, b_ref, c_ref), _MIX_PARTS):
            y = ref[...]
            r = lax.rsqrt(jnp.mean(y * y, axis=-1, keepdims=True) + EPS)
            o_ref[:, lo:hi] = y * r * g_ref[:, lo:hi]

    return _rowcall(body, name, s, tm, [_rb(tm, 512), _rb(tm, 256), _rb(tm, 256), _fb((1, 1024))],
                    (ya, ysg, yss, g.reshape(1, 1024)), _rb(tm, 1024), _sds((s, 1024)))


def _mix_bwd(dmix, ya, ysg, yss, g, name, tm=256):
    s = ya.shape[0]

    def body(dm_ref, a_ref, b_ref, c_ref, g_ref, da_ref, db_ref, dc_ref, dg_ref):
        @pl.when(pl.program_id(0) == 0)
        def _():
            dg_ref[...] = jnp.zeros_like(dg_ref)

        for ref, dref, (lo, hi) in zip((a_ref, b_ref, c_ref), (da_ref, db_ref, dc_ref), _MIX_PARTS):
            y = ref[...]
            r = lax.rsqrt(jnp.mean(y * y, axis=-1, keepdims=True) + EPS)
            xh = y * r
            dm = dm_ref[:, lo:hi]
            dg_ref[:, lo:hi] += jnp.sum(dm * xh, axis=0, keepdims=True)
            dxh = dm * g_ref[:, lo:hi]
            dref[...] = r * (dxh - xh * jnp.mean(dxh * xh, axis=-1, keepdims=True))

    da, db, dc, dg = _rowcall(
        body, name, s, tm, [_rb(tm, 1024), _rb(tm, 512), _rb(tm, 256), _rb(tm, 256), _fb((1, 1024))],
        (dmix, ya, ysg, yss, g.reshape(1, 1024)),
        [_rb(tm, 512), _rb(tm, 256), _rb(tm, 256), _fb((1, 1024))],
        [_sds((s, 512)), _sds((s, 256)), _sds((s, 256)), _sds((1, 1024))])
    return da, db, dc, dg.reshape(1024)


def _ssm_post_fwd(yc, z, d, gw, gb, name, tm=256):
    s = yc.shape[0]

    def body(yc_ref, u_ref, d_ref, gw_ref, gb_ref, o_ref):
        y1 = yc_ref[...] + d_ref[...] * u_ref[...]
        y2 = _gelu(y1)
        gl = jnp.dot(y2.astype(BF16), gw_ref[...], preferred_element_type=F32) + gb_ref[...]
        o_ref[...] = y2 * _sigmoid(gl)

    return _rowcall(body, name, s, tm, [_rb(tm, 256), _rb(tm, 256, 8), _fb((1, 256)), _fb((256, 256)), _fb((1, 256))],
                    (yc, z, d.reshape(1, 256), gw, gb.reshape(1, 256)), _rb(tm, 256), _sds((s, 256)))


def _ssm_post_bwd(dy, yc, z, d, gw, gb, name, tm=256):
    s = yc.shape[0]

    def body(dy_ref, yc_ref, u_ref, d_ref, gw_ref, gb_ref, dy1_ref, dgl_ref, y2_ref, dud_ref, dd_ref, dgb_ref):
        @pl.when(pl.program_id(0) == 0)
        def _():
            dd_ref[...] = jnp.zeros_like(dd_ref)
            dgb_ref[...] = jnp.zeros_like(dgb_ref)

        u = u_ref[...]
        dd = d_ref[...]
        y1 = yc_ref[...] + dd * u
        y2 = _gelu(y1)
        gw_v = gw_ref[...]
        gl = jnp.dot(y2.astype(BF16), gw_v, preferred_element_type=F32) + gb_ref[...]
        sg = _sigmoid(gl)
        dyv = dy_ref[...]
        dgl = dyv * y2 * sg * (1.0 - sg)
        dy2 = dyv * sg + lax.dot_general(dgl.astype(BF16), gw_v, _DIMS["nt"], preferred_element_type=F32)
        dy1 = dy2 * _gelu_grad(y1)
        dy1_ref[...] = dy1
        dgl_ref[...] = dgl
        y2_ref[...] = y2
        dud_ref[...] = dy1 * dd
        dd_ref[...] += jnp.sum(dy1 * u, axis=0, keepdims=True)
        dgb_ref[...] += jnp.sum(dgl, axis=0, keepdims=True)

    outs = _rowcall(
        body, name, s, tm,
        [_rb(tm, 256), _rb(tm, 256), _rb(tm, 256, 8), _fb((1, 256)), _fb((256, 256)), _fb((1, 256))],
        (dy, yc, z, d.reshape(1, 256), gw, gb.reshape(1, 256)),
        [_rb(tm, 256)] * 4 + [_fb((1, 256))] * 2, [_sds((s, 256))] * 4 + [_sds((1, 256))] * 2)
    dy1, dgl, y2, dud, dd, dgb = outs
    return dy1, dgl, y2, dud, dd.reshape(256), dgb.reshape(256)


def _lane_col(x, lane, j):
    return jnp.sum(jnp.where(lane == j, x, 0.0), axis=1, keepdims=True)


def _scan_chunk(xr, xi, pr, pi, lane):
    k = 1
    while k < 128:
        ar = _lane_col(pr, lane, k - 1)
        ai = _lane_col(pi, lane, k - 1)
        keep = lane >= k
        sr = jnp.where(keep, pltpu.roll(xr, k, axis=1), 0.0)
        si = jnp.where(keep, pltpu.roll(xi, k, axis=1), 0.0)
        xr, xi = xr + ar * sr - ai * si, xi + ar * si + ai * sr
        k *= 2
    return xr, xi


def _scan_chunk_rev(xr, xi, pr, pi, lane):
    k = 1
    while k < 128:
        br = _lane_col(pr, lane, 128 - k)
        bi = _lane_col(pi, lane, 128 - k)
        keep = lane < 128 - k
        sr = jnp.where(keep, pltpu.roll(xr, 128 - k, axis=1), 0.0)
        si = jnp.where(keep, pltpu.roll(xi, 128 - k, axis=1), 0.0)
        xr, xi = xr + br * sr - bi * si, xi + br * si + bi * sr
        k *= 2
    return xr, xi


def _scan_fwd(bu_t, pw, name, cb=512):
    two_ns, s = bu_t.shape
    ns = two_ns // 2
    nrb = ns // cb
    nch = s // 128

    def body(br_ref, bi_ref, pr_ref, pi_ref, xr_ref, xi_ref, cr_ref, ci_ref):
        @pl.when(pl.program_id(1) == 0)
        def _():
            cr_ref[...] = jnp.zeros_like(cr_ref)
            ci_ref[...] = jnp.zeros_like(ci_ref)

        lane = lax.broadcasted_iota(jnp.int32, (cb, 128), 1)
        pr = pr_ref[...]
        pi = pi_ref[...]
        xr, xi = _scan_chunk(br_ref[...], bi_ref[...], pr, pi, lane)
        cr = cr_ref[...]
        ci = ci_ref[...]
        xr = xr + pr * cr - pi * ci
        xi = xi + pr * ci + pi * cr
        xr_ref[...] = xr
        xi_ref[...] = xi
        cr_ref[...] = jnp.broadcast_to(_lane_col(xr, lane, 127), (cb, 128))
        ci_ref[...] = jnp.broadcast_to(_lane_col(xi, lane, 127), (cb, 128))

    re_spec = pl.BlockSpec((cb, 128), lambda i, c: (i, c))
    im_spec = pl.BlockSpec((cb, 128), lambda i, c: (i + nrb, c))
    pre_spec = pl.BlockSpec((cb, 128), lambda i, c: (i, 0))
    pim_spec = pl.BlockSpec((cb, 128), lambda i, c: (i + nrb, 0))
    xr, xi = pl.pallas_call(
        body, name=name, grid=(nrb, nch), in_specs=[re_spec, im_spec, pre_spec, pim_spec],
        out_specs=[re_spec, re_spec], out_shape=[_sds((ns, s)), _sds((ns, s))],
        scratch_shapes=[pltpu.VMEM((cb, 128), F32), pltpu.VMEM((cb, 128), F32)],
        compiler_params=_params(("parallel", "arbitrary")),
    )(bu_t, bu_t, pw, pw)
    return xr, xi


def _scan_bwd(g_t, xr, xi, pw_rev, name, cb=512):
    two_ns, s = g_t.shape
    ns = two_ns // 2
    nrb = ns // cb
    nch = s // 128

    def body(gr_ref, gi_ref, pr_ref, pi_ref, xr_ref, xi_ref, xpr_ref, xpi_ref,
             lr_ref, li_ref, dar_ref, dai_ref, cr_ref, ci_ref, ar_acc, ai_acc):
        c = pl.program_id(1)

        @pl.when(c == 0)
        def _():
            cr_ref[...] = jnp.zeros_like(cr_ref)
            ci_ref[...] = jnp.zeros_like(ci_ref)
            ar_acc[...] = jnp.zeros_like(ar_acc)
            ai_acc[...] = jnp.zeros_like(ai_acc)

        lane = lax.broadcasted_iota(jnp.int32, (cb, 128), 1)
        pr = pr_ref[...]
        pi = pi_ref[...]
        lr, li = _scan_chunk_rev(gr_ref[...], gi_ref[...], pr, pi, lane)
        cr = cr_ref[...]
        ci = ci_ref[...]
        lr = lr + pr * cr - pi * ci
        li = li + pr * ci + pi * cr
        lr_ref[...] = lr
        li_ref[...] = li
        cr_ref[...] = jnp.broadcast_to(_lane_col(lr, lane, 0), (cb, 128))
        ci_ref[...] = jnp.broadcast_to(_lane_col(li, lane, 0), (cb, 128))
        has_prev = (c < nch - 1).astype(F32)
        pvr = _lane_col(xpr_ref[...], lane, 127) * has_prev
        pvi = _lane_col(xpi_ref[...], lane, 127) * has_prev
        sxr = jnp.where(lane == 0, pvr, pltpu.roll(xr_ref[...], 1, axis=1))
        sxi = jnp.where(lane == 0, pvi, pltpu.roll(xi_ref[...], 1, axis=1))
        ar_acc[...] += lr * sxr + li * sxi
        ai_acc[...] += li * sxr - lr * sxi

        @pl.when(c == nch - 1)
        def _():
            dar_ref[...] = jnp.broadcast_to(jnp.sum(ar_acc[...], axis=1, keepdims=True), (cb, 128))
            dai_ref[...] = jnp.broadcast_to(jnp.sum(ai_acc[...], axis=1, keepdims=True), (cb, 128))

    rev = lambda c: nch - 1 - c
    re_spec = pl.BlockSpec((cb, 128), lambda i, c: (i, rev(c)))
    im_spec = pl.BlockSpec((cb, 128), lambda i, c: (i + nrb, rev(c)))
    prev_spec = pl.BlockSpec((cb, 128), lambda i, c: (i, jnp.maximum(rev(c) - 1, 0)))
    pre_spec = pl.BlockSpec((cb, 128), lambda i, c: (i, 0))
    pim_spec = pl.BlockSpec((cb, 128), lambda i, c: (i + nrb, 0))
    acc_spec = pl.BlockSpec((cb, 128), lambda i, c: (i, 0))
    lr, li, dar, dai = pl.pallas_call(
        body, name=name, grid=(nrb, nch),
        in_specs=[re_spec, im_spec, pre_spec, pim_spec, re_spec, re_spec, prev_spec, prev_spec],
        out_specs=[re_spec, re_spec, acc_spec, acc_spec],
        out_shape=[_sds((ns, s)), _sds((ns, s)), _sds((ns, 128)), _sds((ns, 128))],
        scratch_shapes=[pltpu.VMEM((cb, 128), F32)] * 4,
        compiler_params=_params(("parallel", "arbitrary")),
    )(g_t, g_t, pw_rev, pw_rev, xr, xi, xr, xi)
    return lr, li, dar[:, 0], dai[:, 0]


def _group_ids():
    return lax.broadcasted_iota(jnp.int32, (1, SGU_W), 1) // 64


def _group_mean(val, gid):
    out = jnp.zeros_like(val)
    for g in range(SGU_GROUPS):
        mg = gid == g
        out = jnp.where(mg, jnp.sum(jnp.where(mg, val, 0.0), axis=1, keepdims=True) * (1.0 / 64), out)
    return out


def _causal_w(w_ref, g):
    t = lax.broadcasted_iota(jnp.int32, (SGU_CHUNK, SGU_CHUNK), 0)
    s = lax.broadcasted_iota(jnp.int32, (SGU_CHUNK, SGU_CHUNK), 1)
    return jnp.where(t >= s, w_ref[g], 0.0).astype(BF16)


def _sgu_core(x, lng, lnb, w_ref, bexp, gid):
    zz = _gelu(x)
    u = zz[:, :SGU_W]
    v = zz[:, SGU_W:]
    vc = v - _group_mean(v, gid)
    rstd = lax.rsqrt(_group_mean(vc * vc, gid) + EPS)
    vhat = vc * rstd
    vn = vhat * lng + lnb
    vnb = vn.astype(BF16)
    mixed = bexp
    for g in range(SGU_GROUPS):
        mm = jnp.dot(_causal_w(w_ref, g), vnb, preferred_element_type=F32)
        mixed = jnp.where(gid == g, mm + bexp, mixed)
    return u, rstd, vhat, vnb, mixed


def _sgu_fwd(z, lng, lnb, w, bexp, name, tm=512):
    s = z.shape[0]

    def body(z_ref, lng_ref, lnb_ref, w_ref, b_ref, o_ref):
        gid = _group_ids()
        for j in range(tm // SGU_CHUNK):
            rows = pl.ds(j * SGU_CHUNK, SGU_CHUNK)
            u, _, _, _, mixed = _sgu_core(z_ref[rows, :], lng_ref[...], lnb_ref[...], w_ref, b_ref[...], gid)
            o_ref[rows, :] = u * mixed

    return _rowcall(body, name, s, tm,
                    [_rb(tm, 512, 3), _fb((1, 256)), _fb((1, 256)), _fb((4, 128, 128)), _fb((128, 256))],
                    (z, lng.reshape(1, 256), lnb.reshape(1, 256), w, bexp), _rb(tm, 256), _sds((s, 256)))


def _sgu_bwd(z, dy, lng, lnb, w, bexp, name, tm=512):
    s = z.shape[0]

    def body(z_ref, dy_ref, lng_ref, lnb_ref, w_ref, b_ref, dz_ref, dw_ref, db_ref, dlng_ref, dlnb_ref):
        @pl.when(pl.program_id(0) == 0)
        def _():
            dw_ref[...] = jnp.zeros_like(dw_ref)
            db_ref[...] = jnp.zeros_like(db_ref)
            dlng_ref[...] = jnp.zeros_like(dlng_ref)
            dlnb_ref[...] = jnp.zeros_like(dlnb_ref)

        gid = _group_ids()
        t = lax.broadcasted_iota(jnp.int32, (SGU_CHUNK, SGU_CHUNK), 0)
        sidx = lax.broadcasted_iota(jnp.int32, (SGU_CHUNK, SGU_CHUNK), 1)
        lng_v = lng_ref[...]
        for j in range(tm // SGU_CHUNK):
            rows = pl.ds(j * SGU_CHUNK, SGU_CHUNK)
            x = z_ref[rows, :]
            u, rstd, vhat, vnb, mixed = _sgu_core(x, lng_v, lnb_ref[...], w_ref, b_ref[...], gid)
            dyv = dy_ref[rows, :]
            dmixed = dyv * u
            du = dyv * mixed
            db_ref[...] += dmixed
            dvn = jnp.zeros_like(dmixed)
            for g in range(SGU_GROUPS):
                dmg = jnp.where(gid == g, dmixed, 0.0).astype(BF16)
                dvn = dvn + lax.dot_general(_causal_w(w_ref, g), dmg, _DIMS["tn"], preferred_element_type=F32)
                dwg = lax.dot_general(dmg, vnb, _DIMS["nt"], preferred_element_type=F32)
                dw_ref[g] += jnp.where(t >= sidx, dwg, 0.0)
            dlnb_ref[...] += jnp.sum(dvn, axis=0, keepdims=True)
            dlng_ref[...] += jnp.sum(dvn * vhat, axis=0, keepdims=True)
            dvh = dvn * lng_v
            dv = rstd * (dvh - _group_mean(dvh, gid) - vhat * _group_mean(dvh * vhat, gid))
            gg = _gelu_grad(x)
            dz_ref[rows, 0:SGU_W] = du * gg[:, :SGU_W]
            dz_ref[rows, SGU_W:2 * SGU_W] = dv * gg[:, SGU_W:]

    dz, dw, db, dlng, dlnb = _rowcall(
        body, name, s, tm,
        [_rb(tm, 512, 3), _rb(tm, 256), _fb((1, 256)), _fb((1, 256)), _fb((4, 128, 128)), _fb((128, 256))],
        (z, dy, lng.reshape(1, 256), lnb.reshape(1, 256), w, bexp),
        [_rb(tm, 512), _fb((4, 128, 128)), _fb((128, 256)), _fb((1, 256)), _fb((1, 256))],
        [_sds((s, 512)), _sds((4, 128, 128)), _sds((128, 256)), _sds((1, 256)), _sds((1, 256))])
    return dz, dw, db, dlng.reshape(256), dlnb.reshape(256)


CONV_TC = 1408
N_CT = D_FF // CONV_TC


def _row_of(block8, j):
    r = lax.broadcasted_iota(jnp.int32, block8.shape, 0)
    return jnp.sum(jnp.where(r == j, block8, 0.0), axis=0, keepdims=True)


def _shift_down(x, tail, has_prev, row):
    r7 = _row_of(tail, 7) * has_prev
    r6 = _row_of(tail, 6) * has_prev
    x1 = jnp.where(row == 0, r7, pltpu.roll(x, 1, axis=0))
    x2 = jnp.where(row == 0, r6, jnp.where(row == 1, r7, pltpu.roll(x, 2, axis=0)))
    return x1, x2


def _conv_fwd(hu, cw, cb, name, tm=256):
    s = hu.shape[0]
    n8 = tm // 8

    def body(xv_ref, xg_ref, tv_ref, tg_ref, wv_ref, wg_ref, bv_ref, bg_ref, hv_ref, hg_ref, act_ref):
        i = pl.program_id(1)
        has_prev = (i > 0).astype(F32)
        row = lax.broadcasted_iota(jnp.int32, (tm, CONV_TC), 0)

        def conv(x_ref, t_ref, w_ref, b_ref):
            x = x_ref[...]
            x1, x2 = _shift_down(x, t_ref[...], has_prev, row)
            return w_ref[0:1, :] * x2 + w_ref[1:2, :] * x1 + w_ref[2:3, :] * x + b_ref[...]

        hv = conv(xv_ref, tv_ref, wv_ref, bv_ref)
        hg = conv(xg_ref, tg_ref, wg_ref, bg_ref)
        hv_ref[...] = hv
        hg_ref[...] = hg
        act_ref[...] = _gelu(hg) * hv

    def xs(off):
        return pl.BlockSpec((tm, CONV_TC), lambda j, i: (i, j + off))

    def ts(off):
        return pl.BlockSpec((8, CONV_TC), lambda j, i: (jnp.maximum(i * n8 - 1, 0), j + off))

    def ws(rows, off):
        return pl.BlockSpec((rows, CONV_TC), lambda j, i: (0, j + off))

    o_spec = pl.BlockSpec((tm, CONV_TC), lambda j, i: (i, j))
    return pl.pallas_call(
        body, name=name, grid=(N_CT, s // tm),
        in_specs=[xs(0), xs(N_CT), ts(0), ts(N_CT), ws(3, 0), ws(3, N_CT), ws(1, 0), ws(1, N_CT)],
        out_specs=[o_spec] * 3, out_shape=[_sds((s, D_FF))] * 3,
        compiler_params=_params(("parallel", "arbitrary")),
    )(hu, hu, hu, hu, cw, cw, cb.reshape(1, 2 * D_FF), cb.reshape(1, 2 * D_FF))


def _conv_bwd_act(dact, hv, hg, name, tm=256):
    s = dact.shape[0]

    def body(d_ref, hv_ref, hg_ref, dv_ref, dg_ref):
        d = d_ref[...]
        g = hg_ref[...]
        dv_ref[...] = d * _gelu(g)
        dg_ref[...] = d * hv_ref[...] * _gelu_grad(g)

    spec = pl.BlockSpec((tm, CONV_TC), lambda i, j: (i, j))
    return pl.pallas_call(
        body, name=name, grid=(s // tm, N_CT), in_specs=[spec] * 3, out_specs=[spec] * 2,
        out_shape=[_sds((s, D_FF))] * 2, compiler_params=_params(("parallel", "parallel")),
    )(dact, hv, hg)


def _conv_bwd(dhc, hu, cw, half, name, tm=256):
    s = dhc.shape[0]
    n8 = tm // 8
    off = half * N_CT
    last8 = s // 8 - 1

    def body(d_ref, dn_ref, x_ref, t_ref, w_ref, dx_ref, dw_ref, db_ref):
        i = pl.program_id(1)

        @pl.when(i == 0)
        def _():
            dw_ref[...] = jnp.zeros_like(dw_ref)
            db_ref[...] = jnp.zeros_like(db_ref)

        has_prev = (i > 0).astype(F32)
        has_next = (i < s // tm - 1).astype(F32)
        row = lax.broadcasted_iota(jnp.int32, (tm, CONV_TC), 0)
        d = d_ref[...]
        n0 = _row_of(dn_ref[...], 0) * has_next
        n1 = _row_of(dn_ref[...], 1) * has_next
        d1 = jnp.where(row == tm - 1, n0, pltpu.roll(d, tm - 1, axis=0))
        d2 = jnp.where(row == tm - 2, n0, jnp.where(row == tm - 1, n1, pltpu.roll(d, tm - 2, axis=0)))
        dx_ref[...] = w_ref[2:3, :] * d + w_ref[1:2, :] * d1 + w_ref[0:1, :] * d2
        x = x_ref[...]
        x1, x2 = _shift_down(x, t_ref[...], has_prev, row)
        dw_ref[0:1, :] += jnp.sum(d * x2, axis=0, keepdims=True)
        dw_ref[1:2, :] += jnp.sum(d * x1, axis=0, keepdims=True)
        dw_ref[2:3, :] += jnp.sum(d * x, axis=0, keepdims=True)
        db_ref[...] += jnp.sum(d, axis=0, keepdims=True)

    d_spec = pl.BlockSpec((tm, CONV_TC), lambda j, i: (i, j))
    dn_spec = pl.BlockSpec((8, CONV_TC), lambda j, i: (jnp.minimum((i + 1) * n8, last8), j))
    x_spec = pl.BlockSpec((tm, CONV_TC), lambda j, i: (i, j + off))
    t_spec = pl.BlockSpec((8, CONV_TC), lambda j, i: (jnp.maximum(i * n8 - 1, 0), j + off))
    w_spec = pl.BlockSpec((3, CONV_TC), lambda j, i: (0, j + off))
    dw_spec = pl.BlockSpec((3, CONV_TC), lambda j, i: (0, j))
    db_spec = pl.BlockSpec((1, CONV_TC), lambda j, i: (0, j))
    return pl.pallas_call(
        body, name=name, grid=(N_CT, s // tm), in_specs=[d_spec, dn_spec, x_spec, t_spec, w_spec],
        out_specs=[d_spec, dw_spec, db_spec], out_shape=[_sds((s, D_FF)), _sds((3, D_FF)), _sds((1, D_FF))],
        compiler_params=_params(("parallel", "arbitrary")),
    )(dhc, dhc, hu, hu, cw)


def _ple_fwd(h, gp, pp, name, tm=256):
    s, d = h.shape

    def body(h_ref, g_ref, p_ref, o_ref):
        o_ref[...] = h_ref[...] + _sigmoid(g_ref[...]) * p_ref[...]

    return _rowcall(body, name, s, tm, [_rb(tm, d)] * 3, (h, gp, pp), _rb(tm, d), _sds((s, d)))


def _ple_bwd(dh, gp, pp, name, tm=256):
    s, d = dh.shape

    def body(d_ref, g_ref, p_ref, dp_ref, dg_ref):
        sg = _sigmoid(g_ref[...])
        dv = d_ref[...]
        dp_ref[...] = dv * sg
        dg_ref[...] = dv * p_ref[...] * sg * (1.0 - sg)

    return _rowcall(body, name, s, tm, [_rb(tm, d)] * 3, (dh, gp, pp), [_rb(tm, d)] * 2, [_sds((s, d))] * 2)


N_CB = IN_W // BLK
SCALE = HEAD_DIM ** -0.5


def _head_masks():
    lane = lax.broadcasted_iota(jnp.int32, (1, BLK), 1)
    return [lane < HEAD_DIM, lane >= HEAD_DIM]


def _band_valid(c):
    qi = lax.broadcasted_iota(jnp.int32, (BLK, 2 * BLK), 0)
    ki = lax.broadcasted_iota(jnp.int32, (BLK, 2 * BLK), 1)
    rel = qi + BLK - ki
    return (rel >= 0) & (rel <= BLK) & ((c > 0) | (ki >= BLK))


def _zspec(off, prev=False):
    if prev:
        return pl.BlockSpec((BLK, BLK), lambda hp, r, c: (jnp.maximum(c - 1, 0), r * N_CB + off + hp))
    return pl.BlockSpec((BLK, BLK), lambda hp, r, c: (c, r * N_CB + off + hp))


def _sspec():
    return pl.BlockSpec((BLK, BLK), lambda hp, r, c: (c, r * 4 + hp))


def _bspec():
    return pl.BlockSpec((2, BLK, 2 * BLK), lambda hp, r, c: (hp, 0, 0))


def _attn_fwd(z, bias, state, dil, first, last, name):
    s = z.shape[0]
    n = s // dil
    nblk = n // BLK
    zv = z.reshape(n, dil * IN_W)

    def body(*refs):
        q_ref, kp_ref, kc_ref, vp_ref, vc_ref, b_ref = refs[:6]
        rest = refs[6:]
        if not first:
            m_ref, l_ref, a_ref = rest[:3]
            rest = rest[3:]
        c = pl.program_id(2)
        q = q_ref[...]
        k = jnp.concatenate([kp_ref[...], kc_ref[...]], axis=0).astype(BF16)
        v = jnp.concatenate([vp_ref[...], vc_ref[...]], axis=0).astype(BF16)
        valid = _band_valid(c)
        mb = lb = ob = None
        for hh, mh in enumerate(_head_masks()):
            qh = jnp.where(mh, q, 0.0).astype(BF16)
            sc = lax.dot_general(qh, k, _DIMS["nt"], preferred_element_type=F32) * SCALE + b_ref[hh]
            sc = jnp.where(valid, sc, NEG_INF)
            mx = jnp.max(sc, axis=1, keepdims=True)
            e = jnp.exp(sc - mx)
            den = jnp.sum(e, axis=1, keepdims=True)
            o = jnp.dot(e.astype(BF16), v, preferred_element_type=F32)
            if hh == 0:
                mb = jnp.broadcast_to(mx, (BLK, BLK))
                lb = jnp.broadcast_to(den, (BLK, BLK))
                ob = o
            else:
                mb = jnp.where(mh, mx, mb)
                lb = jnp.where(mh, den, lb)
                ob = jnp.where(mh, o, ob)
        if first:
            m_new, l_new, a_new = mb, lb, ob
        else:
            m_old = m_ref[...]
            m_new = jnp.maximum(m_old, mb)
            al = jnp.exp(m_old - m_new)
            be = jnp.exp(mb - m_new)
            l_new = al * l_ref[...] + be * lb
            a_new = al * a_ref[...] + be * ob
        if last:
            y_ref, lse_ref = rest
            y_ref[...] = a_new / l_new
            lse_ref[...] = m_new + jnp.log(l_new)
        else:
            mo_ref, lo_ref, ao_ref = rest
            mo_ref[...] = m_new
            lo_ref[...] = l_new
            ao_ref[...] = a_new

    in_specs = [_zspec(0), _zspec(4, True), _zspec(4), _zspec(8, True), _zspec(8), _bspec()]
    args = [zv, zv, zv, zv, zv, bias]
    if not first:
        in_specs += [_sspec()] * 3
        args += [t.reshape(n, dil * ATTN_W) for t in state]
    n_out = 2 if last else 3
    outs = pl.pallas_call(
        body, name=name, grid=(4, dil, nblk), in_specs=in_specs, out_specs=[_sspec()] * n_out,
        out_shape=[_sds((n, dil * ATTN_W))] * n_out,
        compiler_params=_params(("parallel", "parallel", "arbitrary")),
    )(*args)
    return tuple(t.reshape(s, ATTN_W) for t in outs)


def _row_stats(mh, dy, y, lse):
    delta = jnp.sum(jnp.where(mh, dy * y, 0.0), axis=1, keepdims=True)
    lse_h = jnp.max(jnp.where(mh, lse, NEG_INF), axis=1, keepdims=True)
    return delta, lse_h


def _attn_bwd_dq(z, bias, dy, y, lse, dq_prev, dil, name):
    s = z.shape[0]
    n = s // dil
    nblk = n // BLK
    zv = z.reshape(n, dil * IN_W)
    first = dq_prev is None

    def body(*refs):
        q_ref, kp_ref, kc_ref, vp_ref, vc_ref, b_ref, dy_ref, y_ref, lse_ref = refs[:9]
        rest = refs[9:]
        if not first:
            dqp_ref = rest[0]
            rest = rest[1:]
        dq_ref, db_ref = rest
        c = pl.program_id(2)

        @pl.when((pl.program_id(1) == 0) & (c == 0))
        def _():
            db_ref[...] = jnp.zeros_like(db_ref)

        q = q_ref[...]
        k = jnp.concatenate([kp_ref[...], kc_ref[...]], axis=0).astype(BF16)
        v = jnp.concatenate([vp_ref[...], vc_ref[...]], axis=0).astype(BF16)
        dy_v = dy_ref[...]
        valid = _band_valid(c)
        dq = None
        for hh, mh in enumerate(_head_masks()):
            delta, lse_h = _row_stats(mh, dy_v, y_ref[...], lse_ref[...])
            qh = jnp.where(mh, q, 0.0).astype(BF16)
            sc = lax.dot_general(qh, k, _DIMS["nt"], preferred_element_type=F32) * SCALE + b_ref[hh]
            p = jnp.where(valid, jnp.exp(jnp.where(valid, sc, NEG_INF) - lse_h), 0.0)
            dyh = jnp.where(mh, dy_v, 0.0).astype(BF16)
            dp = lax.dot_general(dyh, v, _DIMS["nt"], preferred_element_type=F32)
            ds = p * (dp - delta)
            db_ref[hh] += ds
            dqh = jnp.dot(ds.astype(BF16), k, preferred_element_type=F32) * SCALE
            dq = dqh if hh == 0 else jnp.where(mh, dqh, dq)
        if not first:
            dq = dq + dqp_ref[...]
        dq_ref[...] = dq

    in_specs = [_zspec(0), _zspec(4, True), _zspec(4), _zspec(8, True), _zspec(8), _bspec()] + [_sspec()] * 3
    args = [zv, zv, zv, zv, zv, bias] + [t.reshape(n, dil * ATTN_W) for t in (dy, y, lse)]
    if not first:
        in_specs.append(_sspec())
        args.append(dq_prev.reshape(n, dil * ATTN_W))
    dq, db = pl.pallas_call(
        body, name=name, grid=(4, dil, nblk), in_specs=in_specs, out_specs=[_sspec(), _bspec()],
        out_shape=[_sds((n, dil * ATTN_W)), _sds((N_HEADS, BLK, 2 * BLK))],
        compiler_params=_params(("parallel", "arbitrary", "arbitrary")),
    )(*args)
    return dq.reshape(s, ATTN_W), db


def _attn_bwd_dkv(z, bias, dy, y, lse, dk_prev, dv_prev, dil, name):
    s = z.shape[0]
    n = s // dil
    nblk = n // BLK
    zv = z.reshape(n, dil * IN_W)
    first = dk_prev is None

    def nxt(off):
        return pl.BlockSpec((BLK, BLK), lambda hp, r, c: (jnp.minimum(c + 1, nblk - 1), r * N_CB + off + hp))

    def snxt():
        return pl.BlockSpec((BLK, BLK), lambda hp, r, c: (jnp.minimum(c + 1, nblk - 1), r * 4 + hp))

    def body(*refs):
        k_ref, v_ref, b_ref = refs[:3]
        cur = refs[3:7]
        nx = refs[7:11]
        rest = refs[11:]
        if not first:
            dkp_ref, dvp_ref = rest[:2]
            rest = rest[2:]
        dk_ref, dv_ref = rest
        c = pl.program_id(2)
        k = k_ref[...].astype(BF16)
        v = v_ref[...].astype(BF16)
        qi = lax.broadcasted_iota(jnp.int32, (BLK, BLK), 0)
        ki = lax.broadcasted_iota(jnp.int32, (BLK, BLK), 1)

        def contrib(q_ref, dy_ref, y_ref, lse_ref, same_block):
            q = q_ref[...]
            dy_v = dy_ref[...]
            valid = (qi >= ki) if same_block else (ki >= qi)
            dk = jnp.zeros((BLK, BLK), F32)
            dv = jnp.zeros((BLK, BLK), F32)
            for hh, mh in enumerate(_head_masks()):
                delta, lse_h = _row_stats(mh, dy_v, y_ref[...], lse_ref[...])
                qh = jnp.where(mh, q, 0.0).astype(BF16)
                bsl = b_ref[hh, :, BLK:2 * BLK] if same_block else b_ref[hh, :, 0:BLK]
                sc = lax.dot_general(qh, k, _DIMS["nt"], preferred_element_type=F32) * SCALE + bsl
                p = jnp.where(valid, jnp.exp(jnp.where(valid, sc, NEG_INF) - lse_h), 0.0)
                dyh = jnp.where(mh, dy_v, 0.0).astype(BF16)
                dp = lax.dot_general(dyh, v, _DIMS["nt"], preferred_element_type=F32)
                ds = p * (dp - delta)
                dk = dk + lax.dot_general(ds.astype(BF16), qh, _DIMS["tn"], preferred_element_type=F32) * SCALE
                dv = dv + lax.dot_general(p.astype(BF16), dyh, _DIMS["tn"], preferred_element_type=F32)
            return dk, dv

        dk0, dv0 = contrib(*cur, True)
        if not first:
            dk0 = dk0 + dkp_ref[...]
            dv0 = dv0 + dvp_ref[...]
        dk_ref[...] = dk0
        dv_ref[...] = dv0

        @pl.when(c < nblk - 1)
        def _():
            dk1, dv1 = contrib(*nx, False)
            dk_ref[...] += dk1
            dv_ref[...] += dv1

    in_specs = ([_zspec(4), _zspec(8), _bspec(), _zspec(0)] + [_sspec()] * 3 + [nxt(0)] + [snxt()] * 3)
    views = [t.reshape(n, dil * ATTN_W) for t in (dy, y, lse)]
    args = [zv, zv, bias, zv] + views + [zv] + views
    if not first:
        in_specs += [_sspec()] * 2
        args += [dk_prev.reshape(n, dil * ATTN_W), dv_prev.reshape(n, dil * ATTN_W)]
    dk, dv = pl.pallas_call(
        body, name=name, grid=(4, dil, nblk), in_specs=in_specs, out_specs=[_sspec()] * 2,
        out_shape=[_sds((n, dil * ATTN_W))] * 2,
        compiler_params=_params(("parallel", "parallel", "arbitrary")),
    )(*args)
    return dk.reshape(s, ATTN_W), dv.reshape(s, ATTN_W)


def _t5_bucket(dist):
    max_exact = N_BUCKETS // 2
    d = np.maximum(dist, 0)
    large = max_exact + (np.log(np.maximum(d, 1) / max_exact) / np.log(REL_MAX / max_exact)
                         * (N_BUCKETS - max_exact)).astype(np.int32)
    large = np.minimum(large, N_BUCKETS - 1)
    return np.where(d < max_exact, d, large).astype(np.int32)


def _bias_tables(rel_bias):
    period = 3 * BLK
    tabs = []
    for _, dil in BRANCHES:
        onehot = np.zeros((period, N_BUCKETS), np.float32)
        d = np.arange(BLK + 1)
        onehot[d, _t5_bucket((BLK - d) * dil)] = 1.0
        f = jnp.dot(jnp.asarray(onehot), rel_bias, precision=lax.Precision.HIGHEST)
        flat = jnp.tile(f.T, (1, BLK))[:, :BLK * (period - 1)]
        tabs.append(flat.reshape(N_HEADS, BLK, period - 1)[:, :, :2 * BLK])
    return tabs


def _bucket_onehot():
    maps = []
    q = np.arange(BLK)[:, None]
    k = np.arange(2 * BLK)[None, :]
    rel = q + BLK - k
    for _, dil in BRANCHES:
        maps.append(np.where((rel >= 0) & (rel <= BLK), _t5_bucket(rel * dil), -1).reshape(-1))
    bmap = jnp.asarray(np.concatenate(maps).astype(np.int32))
    return (bmap[:, None] == jnp.arange(128, dtype=jnp.int32)[None, :]).astype(BF16)


def _block_diag(t):
    g, n, c = t.shape
    eye = jnp.eye(g, dtype=t.dtype)
    return (t[:, :, None, :] * eye[:, None, :, None]).reshape(g * n, g * c)


def _ssm_prep(a_re, a_im, log_dt, b_re, b_im, c_re, c_im):
    lam = lax.complex(a_re, a_im)
    dt = jnp.exp(log_dt)[:, None]
    a_bar = jnp.exp(lam * dt)
    b_bar = ((a_bar - 1.0) / lam)[:, :, None] * lax.complex(b_re, b_im)
    bdt = jnp.concatenate([_block_diag(jnp.real(b_bar)), _block_diag(jnp.imag(b_bar))], axis=0)
    cd = jnp.concatenate([_block_diag(jnp.transpose(c_re, (0, 2, 1))),
                          _block_diag(-jnp.transpose(c_im, (0, 2, 1)))], axis=0)
    return jnp.real(a_bar).reshape(-1), jnp.imag(a_bar).reshape(-1), bdt, cd


def _powers(ar, ai):
    pr, pi = ar[:, None], ai[:, None]
    k = 1
    while k < 128:
        lr, li = pr[:, -1:], pi[:, -1:]
        pr, pi = (jnp.concatenate([pr, pr * lr - pi * li], axis=1),
                  jnp.concatenate([pi, pr * li + pi * lr], axis=1))
        k *= 2
    return pr, pi


def _sgu_bias_expand(b):
    return jnp.repeat(b.T, 64, axis=1)


def _layer_fwd(i, h, p_i, big, small, bias_tabs):
    nm = "l%d_" % i
    sv = {"h": h}
    a1 = _rms_fwd(h, small["norm_attn_g"][i], nm + "rms_attn")
    z = _mm(a1, big["w_in"][i], "nn", nm + "in_proj")
    st = None
    for b, (_, dil) in enumerate(BRANCHES):
        st = _attn_fwd(z, bias_tabs[b], st, dil, b == 0, b == len(BRANCHES) - 1, nm + "attn_fwd%d" % b)
    y_attn, lse = st
    bexp = _sgu_bias_expand(small["sgu_b"][i])
    y_sgu = _sgu_fwd(z, small["sgu_ln_g"][i], small["sgu_ln_b"][i], small["sgu_w"][i], bexp, nm + "sgu_fwd")
    ar, ai, bdt, cd = _ssm_prep(*[small[k][i] for k in ("ssm_a_re", "ssm_a_im", "ssm_log_dt", "ssm_b_re",
                                                         "ssm_b_im", "ssm_c_re", "ssm_c_im")])
    pr, pi = _powers(ar, ai)
    u = z[:, IN_W - SSM_W:]
    bu_t = _mm(bdt, u, "nt", nm + "ssm_bu")
    xr, xi = _scan_fwd(bu_t, jnp.concatenate([pr, pi], axis=0), nm + "ssm_scan")
    yc = _mm(xr, cd[:SSM_NS], "tn", nm + "ssm_cx_re")
    yc = _mm(xi, cd[SSM_NS:], "tn", nm + "ssm_cx_im", add=yc)
    y_ssm = _ssm_post_fwd(yc, z, small["ssm_d"][i], big["ssm_glu_w"][i], small["ssm_glu_b"][i], nm + "ssm_post")
    mix = _mix_fwd(y_attn, y_sgu, y_ssm, small["branch_norm_g"][i], nm + "mix")
    h2 = _mm(mix, big["w_out"][i], "nn", nm + "out_proj", add=h)
    a2 = _rms_fwd(h2, small["norm_ffn_g"][i], nm + "rms_ffn")
    hu = _mm(a2, big["ffn_w_up"][i], "nn", nm + "ffn_up")
    hv, hg, act = _conv_fwd(hu, big["ffn_conv_w"][i], small["ffn_conv_b"][i], nm + "ffn_conv")
    h3 = _mm(act, big["ffn_w_down"][i], "nn", nm + "ffn_down", add=h2)
    a3 = _rms_fwd(h3, small["norm_ple_g"][i], nm + "rms_ple")
    gp = _mm(a3, big["ple_w_gate"][i], "nn", nm + "ple_gate")
    pp = _mm(p_i, big["ple_w_proj"][i], "nn", nm + "ple_proj")
    h4 = _ple_fwd(h3, gp, pp, nm + "ple_add")
    sv.update(a1=a1, z=z, y_attn=y_attn, lse=lse, y_sgu=y_sgu, y_ssm=y_ssm, yc=yc, xr=xr, xi=xi, mix=mix, h2=h2,
              a2=a2, hu=hu, hv=hv, hg=hg, act=act, h3=h3, a3=a3, gp=gp, pp=pp, u=u)
    return h4, sv


def _layer_bwd(i, dh4, sv, p_i, big, small, bias_tabs):
    nm = "l%d_" % i
    g = {}
    dpp, dgp = _ple_bwd(dh4, sv["gp"], sv["pp"], nm + "ple_bwd")
    g["ple_w_proj"] = _mm(p_i, dpp, "tn", nm + "d_ple_proj")
    g["ple_w_gate"] = _mm(sv["a3"], dgp, "tn", nm + "d_ple_gate")
    da3 = _mm(dgp, big["ple_w_gate"][i], "nt", nm + "ple_gate_t")
    dh3, g["norm_ple_g"] = _rms_bwd(da3, sv["h3"], small["norm_ple_g"][i], dh4, nm + "rms_ple_bwd")
    g["ffn_w_down"] = _mm(sv["act"], dh3, "tn", nm + "d_ffn_down")
    dact = _mm(dh3, big["ffn_w_down"][i], "nt", nm + "ffn_down_t")
    dhv, dhg = _conv_bwd_act(dact, sv["hv"], sv["hg"], nm + "ffn_act_bwd")
    cw = big["ffn_conv_w"][i]
    dxv, dwv, dbv = _conv_bwd(dhv, sv["hu"], cw, 0, nm + "ffn_conv_bwd_v")
    dxg, dwg, dbg = _conv_bwd(dhg, sv["hu"], cw, 1, nm + "ffn_conv_bwd_g")
    g["ffn_conv_w"] = jnp.concatenate([dwv, dwg], axis=1)
    g["ffn_conv_b"] = jnp.concatenate([dbv, dbg], axis=1).reshape(2 * D_FF)
    g["ffn_w_up"] = jnp.concatenate([_mm(sv["a2"], dxv, "tn", nm + "d_ffn_up_v"),
                                     _mm(sv["a2"], dxg, "tn", nm + "d_ffn_up_g")], axis=1)
    wu = big["ffn_w_up"][i]
    da2 = _mm(dxv, wu[:, :D_FF], "nt", nm + "ffn_up_t_v")
    da2 = _mm(dxg, wu[:, D_FF:], "nt", nm + "ffn_up_t_g", add=da2)
    dh2, g["norm_ffn_g"] = _rms_bwd(da2, sv["h2"], small["norm_ffn_g"][i], dh3, nm + "rms_ffn_bwd")
    g["w_out"] = _mm(sv["mix"], dh2, "tn", nm + "d_out_proj")
    dmix = _mm(dh2, big["w_out"][i], "nt", nm + "out_proj_t")
    dya, dysg, dyss, g["branch_norm_g"] = _mix_bwd(dmix, sv["y_attn"], sv["y_sgu"], sv["y_ssm"],
                                                   small["branch_norm_g"][i], nm + "mix_bwd")
    ssm_keys = ("ssm_a_re", "ssm_a_im", "ssm_log_dt", "ssm_b_re", "ssm_b_im", "ssm_c_re", "ssm_c_im")
    (ar, ai, bdt, cd), prep_vjp = jax.vjp(_ssm_prep, *[small[k][i] for k in ssm_keys])
    pr, pi = _powers(ar, ai)
    pw_rev = jnp.concatenate([pr[:, ::-1], -pi[:, ::-1]], axis=0)
    dy1, dgl, y2, dud, g["ssm_d"], g["ssm_glu_b"] = _ssm_post_bwd(
        dyss, sv["yc"], sv["z"], small["ssm_d"][i], big["ssm_glu_w"][i], small["ssm_glu_b"][i], nm + "ssm_post_bwd")
    g["ssm_glu_w"] = _mm(y2, dgl, "tn", nm + "d_ssm_glu")
    g_t = _mm(cd, dy1, "nt", nm + "ssm_cx_t")
    dcd = jnp.concatenate([_mm(sv["xr"], dy1, "nn", nm + "d_ssm_c_re"),
                           _mm(sv["xi"], dy1, "nn", nm + "d_ssm_c_im")], axis=0)
    lr, li, dar, dai = _scan_bwd(g_t, sv["xr"], sv["xi"], pw_rev, nm + "ssm_scan_bwd")
    u = sv["u"]
    dbdt = jnp.concatenate([_mm(lr, u, "nn", nm + "d_ssm_b_re"), _mm(li, u, "nn", nm + "d_ssm_b_im")], axis=0)
    du = _mm(lr, bdt[:SSM_NS], "tn", nm + "ssm_bu_t_re", add=dud)
    du = _mm(li, bdt[SSM_NS:], "tn", nm + "ssm_bu_t_im", add=du)
    for k, val in zip(ssm_keys, prep_vjp((dar, dai, dbdt, dcd))):
        g[k] = val
    bexp, bexp_vjp = jax.vjp(_sgu_bias_expand, small["sgu_b"][i])
    dzs, g["sgu_w"], dbexp, g["sgu_ln_g"], g["sgu_ln_b"] = _sgu_bwd(
        sv["z"], dysg, small["sgu_ln_g"][i], small["sgu_ln_b"][i], small["sgu_w"][i], bexp, nm + "sgu_bwd")
    g["sgu_b"] = bexp_vjp(dbexp)[0]
    dq = dk = dv = None
    dbs = []
    for b, (_, dil) in enumerate(BRANCHES):
        dq, db = _attn_bwd_dq(sv["z"], bias_tabs[b], dya, sv["y_attn"], sv["lse"], dq, dil, nm + "attn_dq%d" % b)
        dk, dv = _attn_bwd_dkv(sv["z"], bias_tabs[b], dya, sv["y_attn"], sv["lse"], dk, dv, dil,
                               nm + "attn_dkv%d" % b)
        dbs.append(db.reshape(N_HEADS, BLK * 2 * BLK))
    dz = jnp.concatenate([dq, dk, dv, dzs, du], axis=1)
    g["w_in"] = _mm(sv["a1"], dz, "tn", nm + "d_in_proj")
    da1 = _mm(dz, big["w_in"][i], "nt", nm + "in_proj_t")
    dh, g["norm_attn_g"] = _rms_bwd(da1, sv["h"], small["norm_attn_g"][i], dh2, nm + "rms_attn_bwd")
    return dh, g, jnp.concatenate(dbs, axis=1)


def _local_step(x, p, target, big, small):
    depth = p.shape[0]
    bias_tabs = _bias_tables(small["rel_bias"])
    h = x
    saved = []
    for i in range(depth):
        h, sv = _layer_fwd(i, h, p[i], big, small, bias_tabs)
        saved.append(sv)
    dh, loss, g_final = _loss_head(h, target, small["final_norm_g"], "loss_head")
    layer_grads = [None] * depth
    dbias = [None] * depth
    for i in reversed(range(depth)):
        dh, layer_grads[i], dbias[i] = _layer_bwd(i, dh, saved[i], p[i], big, small, bias_tabs)
    grads = {k: jnp.stack([layer_grads[i][k] for i in range(depth)]) for k in layer_grads[0]}
    grads["final_norm_g"] = g_final
    onehot = _bucket_onehot()
    g_rb = _mm(jnp.concatenate(dbias, axis=1), jnp.concatenate([onehot] * depth, axis=0), "nn", "d_rel_bias",
               tk=2048)
    grads["rel_bias"] = g_rb[:, :N_BUCKETS].T
    return loss, dh, grads


_ANY = pl.BlockSpec(memory_space=pl.ANY)
MESH_IDS = pl.DeviceIdType.MESH


def _all_gather(block, name):
    r, c_ = block.shape

    def body(x_ref, out_ref, send_sems, recv_sems, local_sem):
        x, y, c = lax.axis_index("x"), lax.axis_index("y"), lax.axis_index("c")
        me, sibling = (x, y, c), (x, y, 1 - c)
        chips = [(1 - x, y), (x, 1 - y), (1 - x, 1 - y)]

        def slot(px, py, pc):
            return out_ref.at[4 * px + 2 * py + pc]

        def copy(k, blk, to, src=None):
            return pltpu.make_async_remote_copy(
                src_ref=slot(*blk) if src is None else src, dst_ref=slot(*blk),
                send_sem=send_sems.at[k], recv_sem=recv_sems.at[k], device_id=to, device_id_type=MESH_IDS)

        mine = pltpu.make_async_copy(x_ref, slot(*me), local_sem)
        mine.start()
        first = [copy(0, me, sibling, src=x_ref)]
        first += [copy(1 + j, me, (*chip, c), src=x_ref) for j, chip in enumerate(chips)]
        for cp in first:
            cp.start()
        passed = [copy(4 + j, (*chip, c), sibling) for j, chip in enumerate(chips)]
        for j, chip in enumerate(chips):
            copy(1 + j, (*chip, c), me).wait_recv()
            passed[j].start()
        copy(0, sibling, me).wait_recv()
        for j, chip in enumerate(chips):
            copy(4 + j, (*chip, 1 - c), me).wait_recv()
        for cp in first + passed:
            cp.wait_send()
        mine.wait()

    return pl.pallas_call(
        body, name=name, out_shape=jax.ShapeDtypeStruct((N_DEV, r, c_), block.dtype),
        in_specs=[_ANY], out_specs=_ANY,
        scratch_shapes=[pltpu.SemaphoreType.DMA((7,)), pltpu.SemaphoreType.DMA((7,)), pltpu.SemaphoreType.DMA],
    )(block)


def _all_to_all(blocks, name):
    _, r, c_ = blocks.shape

    def body(x_ref, out_ref, send_sems, recv_sems, local_sem):
        x, y, c = lax.axis_index("x"), lax.axis_index("y"), lax.axis_index("c")
        me = 4 * x + 2 * y + c
        mine = pltpu.make_async_copy(x_ref.at[me], out_ref.at[me], local_sem)
        mine.start()
        copies = []
        for k in range(1, N_DEV):
            px = 1 - x if k & 4 else x
            py = 1 - y if k & 2 else y
            pc = 1 - c if k & 1 else c
            cp = pltpu.make_async_remote_copy(
                src_ref=x_ref.at[4 * px + 2 * py + pc], dst_ref=out_ref.at[me],
                send_sem=send_sems.at[k - 1], recv_sem=recv_sems.at[k - 1],
                device_id=(px, py, pc), device_id_type=MESH_IDS)
            cp.start()
            copies.append(cp)
        for cp in copies:
            cp.wait()
        mine.wait()

    return pl.pallas_call(
        body, name=name, out_shape=jax.ShapeDtypeStruct((N_DEV, r, c_), blocks.dtype),
        in_specs=[_ANY], out_specs=_ANY,
        scratch_shapes=[pltpu.SemaphoreType.DMA((7,)), pltpu.SemaphoreType.DMA((7,)), pltpu.SemaphoreType.DMA],
    )(blocks)


def _adamw(parts, w, m, v, name, tr):
    r, c_ = w.shape

    def body(p_ref, w_ref, m_ref, v_ref, g_ref, d_ref, mo_ref, vo_ref):
        g = p_ref[0].astype(F32)
        for j in range(1, N_DEV):
            g = g + p_ref[j].astype(F32)
        m2 = ADAM_B1 * m_ref[...] + (1.0 - ADAM_B1) * g
        v2 = ADAM_B2 * v_ref[...] + (1.0 - ADAM_B2) * (g * g)
        m_hat = m2 / (1.0 - ADAM_B1 ** ADAM_STEP)
        v_hat = v2 / (1.0 - ADAM_B2 ** ADAM_STEP)
        g_ref[...] = g
        d_ref[...] = -ADAM_LR * (m_hat / (jnp.sqrt(v_hat) + ADAM_EPS) + ADAM_WD * w_ref[...])
        mo_ref[...] = m2
        vo_ref[...] = v2

    spec = pl.BlockSpec((tr, c_), lambda i: (i, 0))
    return pl.pallas_call(
        body, name=name, grid=(r // tr,),
        in_specs=[pl.BlockSpec((N_DEV, tr, c_), lambda i: (0, i, 0)), spec, spec, spec],
        out_specs=[spec] * 4, out_shape=[_sds((r, c_))] * 4, compiler_params=_params(("parallel",)),
    )(parts, w, m, v)


def _pack_rows(n_elems, align):
    rows = -(-n_elems // PACK_COLS)
    return -(-rows // align) * align


def _pack(arrs, rows, dtype=F32):
    flat = jnp.concatenate([a.reshape(-1) for a in arrs]).astype(dtype)
    return jnp.pad(flat, (0, rows * PACK_COLS - flat.shape[0])).reshape(rows, PACK_COLS)


def _unpack(pack, shapes):
    flat = pack.reshape(-1)
    out, off = [], 0
    for shp in shapes:
        size = int(np.prod(shp))
        out.append(flat[off:off + size].reshape(shp))
        off += size
    return out


def _shard_major(full, axis):
    l, a, b = full.shape
    if axis == 2:
        t = full.reshape(l, a, N_DEV, b // N_DEV).transpose(2, 0, 1, 3)
    else:
        t = full.reshape(l, N_DEV, a // N_DEV, b).transpose(1, 0, 2, 3)
    return t.reshape(N_DEV, -1)


def _merge_shards(stacked, axis):
    _, l, a, b = stacked.shape
    if axis == 2:
        return stacked.transpose(1, 2, 0, 3).reshape(l, a, N_DEV * b)
    return stacked.transpose(1, 0, 2, 3).reshape(l, N_DEV * a, b)


BIG_TILE_ROWS = 256
SMALL_TILE_ROWS = 64


def kernel(x, p, rel_bias, norm_attn_g, w_in, sgu_ln_g, sgu_ln_b, sgu_w, sgu_b, ssm_a_re, ssm_a_im, ssm_log_dt, ssm_b_re, ssm_b_im, ssm_c_re, ssm_c_im, ssm_d, ssm_glu_w, ssm_glu_b, branch_norm_g, w_out, norm_ffn_g, ffn_w_up, ffn_conv_w, ffn_conv_b, ffn_w_down, norm_ple_g, ple_w_gate, ple_w_proj, final_norm_g, loss_target, m_rel_bias, m_norm_attn_g, m_w_in, m_sgu_ln_g, m_sgu_ln_b, m_sgu_w, m_sgu_b, m_ssm_a_re, m_ssm_a_im, m_ssm_log_dt, m_ssm_b_re, m_ssm_b_im, m_ssm_c_re, m_ssm_c_im, m_ssm_d, m_ssm_glu_w, m_ssm_glu_b, m_branch_norm_g, m_w_out, m_norm_ffn_g, m_ffn_w_up, m_ffn_conv_w, m_ffn_conv_b, m_ffn_w_down, m_norm_ple_g, m_ple_w_gate, m_ple_w_proj, m_final_norm_g, v_rel_bias, v_norm_attn_g, v_w_in, v_sgu_ln_g, v_sgu_ln_b, v_sgu_w, v_sgu_b, v_ssm_a_re, v_ssm_a_im, v_ssm_log_dt, v_ssm_b_re, v_ssm_b_im, v_ssm_c_re, v_ssm_c_im, v_ssm_d, v_ssm_glu_w, v_ssm_glu_b, v_branch_norm_g, v_w_out, v_norm_ffn_g, v_ffn_w_up, v_ffn_conv_w, v_ffn_conv_b, v_ffn_w_down, v_norm_ple_g, v_ple_w_gate, v_ple_w_proj, v_final_norm_g):
    given = dict(locals())
    w = {n: given[n] for n in WEIGHT_NAMES}
    m = {n: given["m_" + n] for n in WEIGHT_NAMES}
    v = {n: given["v_" + n] for n in WEIGHT_NAMES}
    big_shapes = [w[n].shape for n in BIG_NAMES]
    small_shapes = [w[n].shape for n in SMALL_NAMES]
    big_rows = _pack_rows(sum(int(np.prod(s)) for s in big_shapes), BIG_TILE_ROWS)
    small_rows = _pack_rows(sum(int(np.prod(s)) for s in small_shapes), SMALL_TILE_ROWS)

    gathered = _all_gather(_pack([w[n] for n in BIG_NAMES], big_rows, BF16), "gather_weights")
    shards = zip(*[_unpack(gathered[j], big_shapes) for j in range(N_DEV)])
    big = {n: _merge_shards(jnp.stack(parts), BIG_AXIS[n]) for n, parts in zip(BIG_NAMES, shards)}
    conv_rows = _pack_rows(int(np.prod(ffn_conv_w.shape)), 8)
    conv_g = _all_gather(_pack([ffn_conv_w], conv_rows), "gather_conv_taps")
    big["ffn_conv_w"] = _merge_shards(
        jnp.stack([_unpack(conv_g[j], [ffn_conv_w.shape])[0] for j in range(N_DEV)]), BIG_AXIS["ffn_conv_w"])
    small = {n: w[n] for n in SMALL_NAMES}

    loss, dx, grads = _local_step(x[0], p[:, 0], loss_target[0], big, small)
    loss = lax.psum(loss, ("x", "y", "c"))

    send = jnp.concatenate([_shard_major(grads[n], BIG_AXIS[n]) for n in BIG_NAMES], axis=1)
    send = jnp.pad(send, ((0, 0), (0, big_rows * PACK_COLS - send.shape[1])))
    big_parts = _all_to_all(send.astype(BF16).reshape(N_DEV, big_rows, PACK_COLS), "scatter_weight_grads")
    small_parts = _all_gather(_pack([grads[n] for n in SMALL_NAMES], small_rows), "gather_small_grads")

    big_out = _adamw(big_parts, _pack([w[n] for n in BIG_NAMES], big_rows), _pack([m[n] for n in BIG_NAMES], big_rows),
                     _pack([v[n] for n in BIG_NAMES], big_rows), "adamw_sharded", BIG_TILE_ROWS)
    small_out = _adamw(small_parts, _pack([w[n] for n in SMALL_NAMES], small_rows),
                       _pack([m[n] for n in SMALL_NAMES], small_rows),
                       _pack([v[n] for n in SMALL_NAMES], small_rows), "adamw_replicated", SMALL_TILE_ROWS)
    results = []
    for kind in range(4):
        per_name = dict(zip(BIG_NAMES, _unpack(big_out[kind], big_shapes)))
        per_name.update(zip(SMALL_NAMES, _unpack(small_out[kind], small_shapes)))
        results.append([per_name[n] for n in WEIGHT_NAMES])
    return (loss, dx[None], *results[0], *results[1], *results[2], *results[3])
```

```python
import math

import numpy as np
import jax
import jax.numpy as jnp
from jax import lax
from jax.experimental import pallas as pl
from jax.experimental.pallas import tpu as pltpu

F32 = jnp.float32
BF16 = jnp.bfloat16

D_MODEL = 1024
HEAD_DIM = 64
N_HEADS = 8
ATTN_W = 512
SGU_W = 256
SGU_GROUPS = 4
SGU_CHUNK = 128
SSM_W = 256
SSM_GROUPS = 16
SSM_CH = 16
SSM_STATE = 64
SSM_NS = SSM_GROUPS * SSM_STATE
IN_W = 2304
D_FF = 2816
PLE_DIM = 256
BRANCHES = ((128, 1), (512, 4), (2048, 16))
BLK = 128
N_BUCKETS = 32
REL_MAX = 2048
EPS = 1e-6
NEG_INF = -1e30
N_DEV = 8

ADAM_LR = 0.001
ADAM_B1 = 0.9
ADAM_B2 = 0.999
ADAM_EPS = 1e-08
ADAM_WD = 0.01
ADAM_STEP = 10

VMEM_LIMIT_BYTES = 56 * 1024 * 1024
GELU_C = math.sqrt(2.0 / math.pi)

SMALL_NAMES = ("rel_bias", "norm_attn_g", "sgu_ln_g", "sgu_ln_b", "sgu_w", "sgu_b", "ssm_a_re", "ssm_a_im",
               "ssm_log_dt", "ssm_b_re", "ssm_b_im", "ssm_c_re", "ssm_c_im", "ssm_d", "ssm_glu_b",
               "branch_norm_g", "norm_ffn_g", "ffn_conv_b", "norm_ple_g", "final_norm_g")
WEIGHT_NAMES = ("rel_bias", "norm_attn_g", "w_in", "sgu_ln_g", "sgu_ln_b", "sgu_w", "sgu_b", "ssm_a_re",
                "ssm_a_im", "ssm_log_dt", "ssm_b_re", "ssm_b_im", "ssm_c_re", "ssm_c_im", "ssm_d", "ssm_glu_w",
                "ssm_glu_b", "branch_norm_g", "w_out", "norm_ffn_g", "ffn_w_up", "ffn_conv_w", "ffn_conv_b",
                "ffn_w_down", "norm_ple_g", "ple_w_gate", "ple_w_proj", "final_norm_g")
PACK_COLS = 512


def _params(sem):
    return pltpu.CompilerParams(dimension_semantics=sem, vmem_limit_bytes=VMEM_LIMIT_BYTES)


def _pick(dim, target):
    if dim <= target:
        return dim
    best = None
    for t in range(128, target + 1, 128):
        if dim % t == 0:
            best = t
    return dim if best is None else best


def _gelu(x):
    return 0.5 * x * (1.0 + jnp.tanh(GELU_C * (x + 0.044715 * (x * x * x))))


def _gelu_grad(x):
    t = jnp.tanh(GELU_C * (x + 0.044715 * (x * x * x)))
    return 0.5 * (1.0 + t) + 0.5 * x * (1.0 - t * t) * (GELU_C * (1.0 + 3.0 * 0.044715 * (x * x)))


def _sigmoid(x):
    return 1.0 / (1.0 + jnp.exp(-x))


_DIMS = {"nn": (((1,), (0,)), ((), ())), "tn": (((0,), (0,)), ((), ())), "nt": (((1,), (1,)), ((), ()))}


def _mm(a, b, mode, name, add=None, out_dtype=F32, b_layer=None, tm=1024, tn=1408, tk=1408):
    b_shape = b.shape if b_layer is None else b.shape[1:]
    if mode == "nn":
        m, k = a.shape
        k2, n = b_shape
    elif mode == "tn":
        k, m = a.shape
        k2, n = b_shape
    else:
        m, k = a.shape
        n, k2 = b_shape
    assert k == k2, (name, a.shape, b.shape, mode)
    tm, tn, tk = _pick(m, tm), _pick(n, tn), _pick(k, tk)
    nk = k // tk
    dims = _DIMS[mode]
    has_add = add is not None

    def body(*refs):
        if has_add:
            a_ref, b_ref, add_ref, o_ref = refs[:4]
        else:
            a_ref, b_ref, o_ref = refs[:3]
        part = lax.dot_general(a_ref[...].astype(BF16), b_ref[...].astype(BF16), dims,
                               preferred_element_type=F32)

        def finish(r):
            if has_add:
                r = r + add_ref[...]
            o_ref[...] = r.astype(out_dtype)

        if nk == 1:
            finish(part)
            return
        acc_ref = refs[-1]
        kk = pl.program_id(2)

        @pl.when(kk == 0)
        def _():
            acc_ref[...] = part

        @pl.when((kk > 0) & (kk < nk - 1))
        def _():
            acc_ref[...] += part

        @pl.when(kk == nk - 1)
        def _():
            finish(acc_ref[...] + part)

    if mode == "tn":
        a_spec = pl.BlockSpec((tk, tm), lambda i, j, kk: (kk, i))
    else:
        a_spec = pl.BlockSpec((tm, tk), lambda i, j, kk: (i, kk))
    b_blk = (tn, tk) if mode == "nt" else (tk, tn)
    if mode == "nt":
        b_idx = lambda i, j, kk: (j, kk)
    else:
        b_idx = lambda i, j, kk: (kk, j)
    if b_layer is None:
        b_spec = pl.BlockSpec(b_blk, b_idx)
    else:
        b_spec = pl.BlockSpec((None,) + b_blk, lambda i, j, kk: (b_layer,) + b_idx(i, j, kk))
    o_spec = pl.BlockSpec((tm, tn), lambda i, j, kk: (i, j))
    in_specs = [a_spec, b_spec] + ([o_spec] if has_add else [])
    args = (a, b) + ((add,) if has_add else ())
    return pl.pallas_call(
        body, name=name, grid=(m // tm, n // tn, nk),
        in_specs=in_specs, out_specs=o_spec,
        out_shape=jax.ShapeDtypeStruct((m, n), out_dtype),
        scratch_shapes=[pltpu.VMEM((tm, tn), F32)] if nk > 1 else [],
        compiler_params=_params(("parallel", "parallel", "arbitrary")),
    )(*args)


def _rb(tm, w, cb=0):
    return pl.BlockSpec((tm, w), lambda i: (i, cb))


def _fb(shape):
    nd = len(shape)
    return pl.BlockSpec(shape, lambda i: (0,) * nd)


def _rowcall(body, name, n_rows, tm, in_specs, args, out_specs, out_shapes):
    return pl.pallas_call(
        body, name=name, grid=(n_rows // tm,), in_specs=in_specs, out_specs=out_specs, out_shape=out_shapes,
        compiler_params=_params(("arbitrary",)),
    )(*args)


def _sds(shape, dtype=F32):
    return jax.ShapeDtypeStruct(shape, dtype)


def _rms_fwd(h, g, name, tm=256):
    s, d = h.shape

    def body(h_ref, g_ref, o_ref):
        x = h_ref[...]
        r = lax.rsqrt(jnp.mean(x * x, axis=-1, keepdims=True) + EPS)
        o_ref[...] = (x * r * g_ref[...]).astype(BF16)

    return _rowcall(body, name, s, tm, [_rb(tm, d), _fb((1, d))], (h, g.reshape(1, d)), _rb(tm, d),
                    _sds((s, d), BF16))


def _rms_bwd(da, h, g, dres, name, tm=256):
    s, d = h.shape

    def body(da_ref, h_ref, g_ref, dres_ref, dh_ref, dg_ref):
        @pl.when(pl.program_id(0) == 0)
        def _():
            dg_ref[...] = jnp.zeros_like(dg_ref)

        x = h_ref[...]
        r = lax.rsqrt(jnp.mean(x * x, axis=-1, keepdims=True) + EPS)
        xh = x * r
        dy = da_ref[...]
        dg_ref[...] += jnp.sum(dy * xh, axis=0, keepdims=True)
        dxh = dy * g_ref[...]
        dh_ref[...] = dres_ref[...] + r * (dxh - xh * jnp.mean(dxh * xh, axis=-1, keepdims=True))

    dh, dg = _rowcall(body, name, s, tm, [_rb(tm, d), _rb(tm, d), _fb((1, d)), _rb(tm, d)],
                      (da, h, g.reshape(1, d), dres), [_rb(tm, d), _fb((1, d))], [_sds((s, d)), _sds((1, d))])
    return dh, dg.reshape(d)


def _loss_head(h, target, g, name, tm=256):
    s, d = h.shape

    def body(h_ref, t_ref, g_ref, dh_ref, loss_ref, dg_ref):
        @pl.when(pl.program_id(0) == 0)
        def _():
            dg_ref[...] = jnp.zeros_like(dg_ref)
            loss_ref[...] = jnp.zeros_like(loss_ref)

        x = h_ref[...]
        r = lax.rsqrt(jnp.mean(x * x, axis=-1, keepdims=True) + EPS)
        xh = x * r
        gg = g_ref[...]
        err = xh * gg - t_ref[...]
        loss_ref[...] += jnp.sum(err * err) * (0.5 / d)
        dy = err * (1.0 / d)
        dg_ref[...] += jnp.sum(dy * xh, axis=0, keepdims=True)
        dxh = dy * gg
        dh_ref[...] = r * (dxh - xh * jnp.mean(dxh * xh, axis=-1, keepdims=True))

    dh, loss, dg = _rowcall(body, name, s, tm, [_rb(tm, d), _rb(tm, d), _fb((1, d))], (h, target, g.reshape(1, d)),
                            [_rb(tm, d), _fb((1, 128)), _fb((1, d))], [_sds((s, d)), _sds((1, 128)), _sds((1, d))])
    return dh, loss[0, 0], dg.reshape(d)


_MIX_PARTS = ((0, 512), (512, 768), (768, 1024))


def _mix_fwd(ya, ysg, yss, g, name, tm=256):
    s = ya.shape[0]

    def body(a_ref, b_ref, c_ref, g_ref, o_ref):
        for ref, (lo, hi) in zip((a_ref, b_ref, c_ref), _MIX_PARTS):
            y = ref[...]
            r = lax.rsqrt(jnp.mean(y * y, axis=-1, keepdims=True) + EPS)
            o_ref[:, lo:hi] = (y * r * g_ref[:, lo:hi]).astype(BF16)

    return _rowcall(body, name, s, tm, [_rb(tm, 512), _rb(tm, 256), _rb(tm, 256), _fb((1, 1024))],
                    (ya, ysg, yss, g.reshape(1, 1024)), _rb(tm, 1024), _sds((s, 1024), BF16))


def _mix_bwd(dmix, ya, ysg, yss, g, name, tm=256):
    s = ya.shape[0]

    def body(dm_ref, a_ref, b_ref, c_ref, g_ref, da_ref, db_ref, dc_ref, dg_ref):
        @pl.when(pl.program_id(0) == 0)
        def _():
            dg_ref[...] = jnp.zeros_like(dg_ref)

        for ref, dref, (lo, hi) in zip((a_ref, b_ref, c_ref), (da_ref, db_ref, dc_ref), _MIX_PARTS):
            y = ref[...]
            r = lax.rsqrt(jnp.mean(y * y, axis=-1, keepdims=True) + EPS)
            xh = y * r
            dm = dm_ref[:, lo:hi]
            dg_ref[:, lo:hi] += jnp.sum(dm * xh, axis=0, keepdims=True)
            dxh = dm * g_ref[:, lo:hi]
            dref[...] = r * (dxh - xh * jnp.mean(dxh * xh, axis=-1, keepdims=True))

    da, db, dc, dg = _rowcall(
        body, name, s, tm, [_rb(tm, 1024), _rb(tm, 512), _rb(tm, 256), _rb(tm, 256), _fb((1, 1024))],
        (dmix, ya, ysg, yss, g.reshape(1, 1024)),
        [_rb(tm, 512), _rb(tm, 256), _rb(tm, 256), _fb((1, 1024))],
        [_sds((s, 512)), _sds((s, 256)), _sds((s, 256)), _sds((1, 1024))])
    return da, db, dc, dg.reshape(1024)


def _ssm_post_fwd(yc, z, d, gw, gb, name, tm=256):
    s = yc.shape[0]

    def body(yc_ref, u_ref, d_ref, gw_ref, gb_ref, o_ref):
        y1 = yc_ref[...] + d_ref[...] * u_ref[...]
        y2 = _gelu(y1)
        gl = jnp.dot(y2.astype(BF16), gw_ref[...], preferred_element_type=F32) + gb_ref[...]
        o_ref[...] = y2 * _sigmoid(gl)

    return _rowcall(body, name, s, tm, [_rb(tm, 256), _rb(tm, 256, 8), _fb((1, 256)), _fb((256, 256)), _fb((1, 256))],
                    (yc, z, d.reshape(1, 256), gw, gb.reshape(1, 256)), _rb(tm, 256), _sds((s, 256)))


def _ssm_post_bwd(dy, yc, z, d, gw, gb, name, tm=256):
    s = yc.shape[0]

    def body(dy_ref, yc_ref, u_ref, d_ref, gw_ref, gb_ref, dy1_ref, dgl_ref, y2_ref, dud_ref, dd_ref, dgb_ref):
        @pl.when(pl.program_id(0) == 0)
        def _():
            dd_ref[...] = jnp.zeros_like(dd_ref)
            dgb_ref[...] = jnp.zeros_like(dgb_ref)

        u = u_ref[...]
        dd = d_ref[...]
        y1 = yc_ref[...] + dd * u
        y2 = _gelu(y1)
        gw_v = gw_ref[...]
        gl = jnp.dot(y2.astype(BF16), gw_v, preferred_element_type=F32) + gb_ref[...]
        sg = _sigmoid(gl)
        dyv = dy_ref[...]
        dgl = dyv * y2 * sg * (1.0 - sg)
        dy2 = dyv * sg + lax.dot_general(dgl.astype(BF16), gw_v, _DIMS["nt"], preferred_element_type=F32)
        dy1 = dy2 * _gelu_grad(y1)
        dy1_ref[...] = dy1.astype(BF16)
        dgl_ref[...] = dgl.astype(BF16)
        y2_ref[...] = y2.astype(BF16)
        dud_ref[...] = dy1 * dd
        dd_ref[...] += jnp.sum(dy1 * u, axis=0, keepdims=True)
        dgb_ref[...] += jnp.sum(dgl, axis=0, keepdims=True)

    outs = _rowcall(
        body, name, s, tm,
        [_rb(tm, 256), _rb(tm, 256), _rb(tm, 256, 8), _fb((1, 256)), _fb((256, 256)), _fb((1, 256))],
        (dy, yc, z, d.reshape(1, 256), gw, gb.reshape(1, 256)),
        [_rb(tm, 256)] * 4 + [_fb((1, 256))] * 2,
        [_sds((s, 256), BF16)] * 3 + [_sds((s, 256))] + [_sds((1, 256))] * 2)
    dy1, dgl, y2, dud, dd, dgb = outs
    return dy1, dgl, y2, dud, dd.reshape(256), dgb.reshape(256)


def _lane_col(x, lane, j):
    return jnp.sum(jnp.where(lane == j, x, 0.0), axis=1, keepdims=True)


def _scan_chunk(xr, xi, pr, pi, lane):
    k = 1
    while k < 128:
        ar = _lane_col(pr, lane, k - 1)
        ai = _lane_col(pi, lane, k - 1)
        keep = lane >= k
        sr = jnp.where(keep, pltpu.roll(xr, k, axis=1), 0.0)
        si = jnp.where(keep, pltpu.roll(xi, k, axis=1), 0.0)
        xr, xi = xr + ar * sr - ai * si, xi + ar * si + ai * sr
        k *= 2
    return xr, xi


def _scan_chunk_rev(xr, xi, pr, pi, lane):
    k = 1
    while k < 128:
        br = _lane_col(pr, lane, 128 - k)
        bi = _lane_col(pi, lane, 128 - k)
        keep = lane < 128 - k
        sr = jnp.where(keep, pltpu.roll(xr, 128 - k, axis=1), 0.0)
        si = jnp.where(keep, pltpu.roll(xi, 128 - k, axis=1), 0.0)
        xr, xi = xr + br * sr - bi * si, xi + br * si + bi * sr
        k *= 2
    return xr, xi


def _scan_fwd(bu_t, pw, name, cb=512):
    two_ns, s = bu_t.shape
    ns = two_ns // 2
    nrb = ns // cb
    nch = s // 128

    def body(br_ref, bi_ref, pr_ref, pi_ref, xr_ref, xi_ref, cr_ref, ci_ref):
        @pl.when(pl.program_id(1) == 0)
        def _():
            cr_ref[...] = jnp.zeros_like(cr_ref)
            ci_ref[...] = jnp.zeros_like(ci_ref)

        lane = lax.broadcasted_iota(jnp.int32, (cb, 128), 1)
        pr = pr_ref[...]
        pi = pi_ref[...]
        xr, xi = _scan_chunk(br_ref[...], bi_ref[...], pr, pi, lane)
        cr = cr_ref[...]
        ci = ci_ref[...]
        xr = xr + pr * cr - pi * ci
        xi = xi + pr * ci + pi * cr
        xr_ref[...] = xr
        xi_ref[...] = xi
        cr_ref[...] = jnp.broadcast_to(_lane_col(xr, lane, 127), (cb, 128))
        ci_ref[...] = jnp.broadcast_to(_lane_col(xi, lane, 127), (cb, 128))

    re_spec = pl.BlockSpec((cb, 128), lambda i, c: (i, c))
    im_spec = pl.BlockSpec((cb, 128), lambda i, c: (i + nrb, c))
    pre_spec = pl.BlockSpec((cb, 128), lambda i, c: (i, 0))
    pim_spec = pl.BlockSpec((cb, 128), lambda i, c: (i + nrb, 0))
    xr, xi = pl.pallas_call(
        body, name=name, grid=(nrb, nch), in_specs=[re_spec, im_spec, pre_spec, pim_spec],
        out_specs=[re_spec, re_spec], out_shape=[_sds((ns, s)), _sds((ns, s))],
        scratch_shapes=[pltpu.VMEM((cb, 128), F32), pltpu.VMEM((cb, 128), F32)],
        compiler_params=_params(("parallel", "arbitrary")),
    )(bu_t, bu_t, pw, pw)
    return xr, xi


def _scan_bwd(g_t, xr, xi, pw_rev, name, cb=512):
    two_ns, s = g_t.shape
    ns = two_ns // 2
    nrb = ns // cb
    nch = s // 128

    def body(gr_ref, gi_ref, pr_ref, pi_ref, xr_ref, xi_ref, xpr_ref, xpi_ref,
             lr_ref, li_ref, dar_ref, dai_ref, cr_ref, ci_ref, ar_acc, ai_acc):
        c = pl.program_id(1)

        @pl.when(c == 0)
        def _():
            cr_ref[...] = jnp.zeros_like(cr_ref)
            ci_ref[...] = jnp.zeros_like(ci_ref)
            ar_acc[...] = jnp.zeros_like(ar_acc)
            ai_acc[...] = jnp.zeros_like(ai_acc)

        lane = lax.broadcasted_iota(jnp.int32, (cb, 128), 1)
        pr = pr_ref[...]
        pi = pi_ref[...]
        lr, li = _scan_chunk_rev(gr_ref[...], gi_ref[...], pr, pi, lane)
        cr = cr_ref[...]
        ci = ci_ref[...]
        lr = lr + pr * cr - pi * ci
        li = li + pr * ci + pi * cr
        lr_ref[...] = lr
        li_ref[...] = li
        cr_ref[...] = jnp.broadcast_to(_lane_col(lr, lane, 0), (cb, 128))
        ci_ref[...] = jnp.broadcast_to(_lane_col(li, lane, 0), (cb, 128))
        has_prev = (c < nch - 1).astype(F32)
        pvr = _lane_col(xpr_ref[...], lane, 127) * has_prev
        pvi = _lane_col(xpi_ref[...], lane, 127) * has_prev
        sxr = jnp.where(lane == 0, pvr, pltpu.roll(xr_ref[...], 1, axis=1))
        sxi = jnp.where(lane == 0, pvi, pltpu.roll(xi_ref[...], 1, axis=1))
        ar_acc[...] += lr * sxr + li * sxi
        ai_acc[...] += li * sxr - lr * sxi

        @pl.when(c == nch - 1)
        def _():
            dar_ref[...] = jnp.broadcast_to(jnp.sum(ar_acc[...], axis=1, keepdims=True), (cb, 128))
            dai_ref[...] = jnp.broadcast_to(jnp.sum(ai_acc[...], axis=1, keepdims=True), (cb, 128))

    rev = lambda c: nch - 1 - c
    re_spec = pl.BlockSpec((cb, 128), lambda i, c: (i, rev(c)))
    im_spec = pl.BlockSpec((cb, 128), lambda i, c: (i + nrb, rev(c)))
    prev_spec = pl.BlockSpec((cb, 128), lambda i, c: (i, jnp.maximum(rev(c) - 1, 0)))
    pre_spec = pl.BlockSpec((cb, 128), lambda i, c: (i, 0))
    pim_spec = pl.BlockSpec((cb, 128), lambda i, c: (i + nrb, 0))
    acc_spec = pl.BlockSpec((cb, 128), lambda i, c: (i, 0))
    lr, li, dar, dai = pl.pallas_call(
        body, name=name, grid=(nrb, nch),
        in_specs=[re_spec, im_spec, pre_spec, pim_spec, re_spec, re_spec, prev_spec, prev_spec],
        out_specs=[re_spec, re_spec, acc_spec, acc_spec],
        out_shape=[_sds((ns, s)), _sds((ns, s)), _sds((ns, 128)), _sds((ns, 128))],
        scratch_shapes=[pltpu.VMEM((cb, 128), F32)] * 4,
        compiler_params=_params(("parallel", "arbitrary")),
    )(g_t, g_t, pw_rev, pw_rev, xr, xi, xr, xi)
    return lr, li, dar[:, 0], dai[:, 0]


def _group_ids():
    return lax.broadcasted_iota(jnp.int32, (1, SGU_W), 1) // 64


def _group_mean(val, gid):
    out = jnp.zeros_like(val)
    for g in range(SGU_GROUPS):
        mg = gid == g
        out = jnp.where(mg, jnp.sum(jnp.where(mg, val, 0.0), axis=1, keepdims=True) * (1.0 / 64), out)
    return out


def _causal_w(w_ref, g):
    t = lax.broadcasted_iota(jnp.int32, (SGU_CHUNK, SGU_CHUNK), 0)
    s = lax.broadcasted_iota(jnp.int32, (SGU_CHUNK, SGU_CHUNK), 1)
    return jnp.where(t >= s, w_ref[g], 0.0).astype(BF16)


def _sgu_core(x, lng, lnb, w_ref, bexp, gid):
    zz = _gelu(x)
    u = zz[:, :SGU_W]
    v = zz[:, SGU_W:]
    vc = v - _group_mean(v, gid)
    rstd = lax.rsqrt(_group_mean(vc * vc, gid) + EPS)
    vhat = vc * rstd
    vn = vhat * lng + lnb
    vnb = vn.astype(BF16)
    mixed = bexp
    for g in range(SGU_GROUPS):
        mm = jnp.dot(_causal_w(w_ref, g), vnb, preferred_element_type=F32)
        mixed = jnp.where(gid == g, mm + bexp, mixed)
    return u, rstd, vhat, vnb, mixed


def _sgu_fwd(z, lng, lnb, w, bexp, name, tm=512):
    s = z.shape[0]

    def body(z_ref, lng_ref, lnb_ref, w_ref, b_ref, o_ref):
        gid = _group_ids()
        for j in range(tm // SGU_CHUNK):
            rows = pl.ds(j * SGU_CHUNK, SGU_CHUNK)
            u, _, _, _, mixed = _sgu_core(z_ref[rows, :], lng_ref[...], lnb_ref[...], w_ref, b_ref[...], gid)
            o_ref[rows, :] = u * mixed

    return _rowcall(body, name, s, tm,
                    [_rb(tm, 512, 3), _fb((1, 256)), _fb((1, 256)), _fb((4, 128, 128)), _fb((128, 256))],
                    (z, lng.reshape(1, 256), lnb.reshape(1, 256), w, bexp), _rb(tm, 256), _sds((s, 256)))


def _sgu_bwd(z, dy, lng, lnb, w, bexp, name, tm=512):
    s = z.shape[0]

    def body(z_ref, dy_ref, lng_ref, lnb_ref, w_ref, b_ref, dz_ref, dw_ref, db_ref, dlng_ref, dlnb_ref):
        @pl.when(pl.program_id(0) == 0)
        def _():
            dw_ref[...] = jnp.zeros_like(dw_ref)
            db_ref[...] = jnp.zeros_like(db_ref)
            dlng_ref[...] = jnp.zeros_like(dlng_ref)
            dlnb_ref[...] = jnp.zeros_like(dlnb_ref)

        gid = _group_ids()
        t = lax.broadcasted_iota(jnp.int32, (SGU_CHUNK, SGU_CHUNK), 0)
        sidx = lax.broadcasted_iota(jnp.int32, (SGU_CHUNK, SGU_CHUNK), 1)
        lng_v = lng_ref[...]
        for j in range(tm // SGU_CHUNK):
            rows = pl.ds(j * SGU_CHUNK, SGU_CHUNK)
            x = z_ref[rows, :]
            u, rstd, vhat, vnb, mixed = _sgu_core(x, lng_v, lnb_ref[...], w_ref, b_ref[...], gid)
            dyv = dy_ref[rows, :]
            dmixed = dyv * u
            du = dyv * mixed
            db_ref[...] += dmixed
            dvn = jnp.zeros_like(dmixed)
            for g in range(SGU_GROUPS):
                dmg = jnp.where(gid == g, dmixed, 0.0).astype(BF16)
                dvn = dvn + lax.dot_general(_causal_w(w_ref, g), dmg, _DIMS["tn"], preferred_element_type=F32)
                dwg = lax.dot_general(dmg, vnb, _DIMS["nt"], preferred_element_type=F32)
                dw_ref[g] += jnp.where(t >= sidx, dwg, 0.0)
            dlnb_ref[...] += jnp.sum(dvn, axis=0, keepdims=True)
            dlng_ref[...] += jnp.sum(dvn * vhat, axis=0, keepdims=True)
            dvh = dvn * lng_v
            dv = rstd * (dvh - _group_mean(dvh, gid) - vhat * _group_mean(dvh * vhat, gid))
            gg = _gelu_grad(x)
            dz_ref[rows, 0:SGU_W] = du * gg[:, :SGU_W]
            dz_ref[rows, SGU_W:2 * SGU_W] = dv * gg[:, SGU_W:]

    dz, dw, db, dlng, dlnb = _rowcall(
        body, name, s, tm,
        [_rb(tm, 512, 3), _rb(tm, 256), _fb((1, 256)), _fb((1, 256)), _fb((4, 128, 128)), _fb((128, 256))],
        (z, dy, lng.reshape(1, 256), lnb.reshape(1, 256), w, bexp),
        [_rb(tm, 512), _fb((4, 128, 128)), _fb((128, 256)), _fb((1, 256)), _fb((1, 256))],
        [_sds((s, 512)), _sds((4, 128, 128)), _sds((128, 256)), _sds((1, 256)), _sds((1, 256))])
    return dz, dw, db, dlng.reshape(256), dlnb.reshape(256)


CONV_TC = 1408
N_CT = D_FF // CONV_TC


def _row_of(block8, j):
    r = lax.broadcasted_iota(jnp.int32, block8.shape, 0)
    return jnp.sum(jnp.where(r == j, block8, 0.0), axis=0, keepdims=True)


def _shift_down(x, tail, has_prev, row):
    r7 = _row_of(tail, 7) * has_prev
    r6 = _row_of(tail, 6) * has_prev
    x1 = jnp.where(row == 0, r7, pltpu.roll(x, 1, axis=0))
    x2 = jnp.where(row == 0, r6, jnp.where(row == 1, r7, pltpu.roll(x, 2, axis=0)))
    return x1, x2


def _conv_fwd(hu, cw, cb, name, tm=256):
    s = hu.shape[0]
    n8 = tm // 8

    def body(xv_ref, xg_ref, tv_ref, tg_ref, wv_ref, wg_ref, bv_ref, bg_ref, hv_ref, hg_ref, act_ref):
        i = pl.program_id(1)
        has_prev = (i > 0).astype(F32)
        row = lax.broadcasted_iota(jnp.int32, (tm, CONV_TC), 0)

        def conv(x_ref, t_ref, w_ref, b_ref):
            x = x_ref[...]
            x1, x2 = _shift_down(x, t_ref[...], has_prev, row)
            return w_ref[0:1, :] * x2 + w_ref[1:2, :] * x1 + w_ref[2:3, :] * x + b_ref[...]

        hv = conv(xv_ref, tv_ref, wv_ref, bv_ref)
        hg = conv(xg_ref, tg_ref, wg_ref, bg_ref)
        hv_ref[...] = hv
        hg_ref[...] = hg
        act_ref[...] = (_gelu(hg) * hv).astype(BF16)

    def xs(off):
        return pl.BlockSpec((tm, CONV_TC), lambda j, i: (i, j + off))

    def ts(off):
        return pl.BlockSpec((8, CONV_TC), lambda j, i: (jnp.maximum(i * n8 - 1, 0), j + off))

    def ws(rows, off):
        return pl.BlockSpec((rows, CONV_TC), lambda j, i: (0, j + off))

    o_spec = pl.BlockSpec((tm, CONV_TC), lambda j, i: (i, j))
    return pl.pallas_call(
        body, name=name, grid=(N_CT, s // tm),
        in_specs=[xs(0), xs(N_CT), ts(0), ts(N_CT), ws(3, 0), ws(3, N_CT), ws(1, 0), ws(1, N_CT)],
        out_specs=[o_spec] * 3, out_shape=[_sds((s, D_FF)), _sds((s, D_FF)), _sds((s, D_FF), BF16)],
        compiler_params=_params(("parallel", "arbitrary")),
    )(hu, hu, hu, hu, cw, cw, cb.reshape(1, 2 * D_FF), cb.reshape(1, 2 * D_FF))


def _conv_bwd_act(dact, hv, hg, name, tm=256):
    s = dact.shape[0]

    def body(d_ref, hv_ref, hg_ref, o_ref):
        d = d_ref[...]
        g = hg_ref[...]

        @pl.when(pl.program_id(1) < N_CT)
        def _():
            o_ref[...] = d * _gelu(g)

        @pl.when(pl.program_id(1) >= N_CT)
        def _():
            o_ref[...] = d * hv_ref[...] * _gelu_grad(g)

    spec = pl.BlockSpec((tm, CONV_TC), lambda i, j: (i, j % N_CT))
    o_spec = pl.BlockSpec((tm, CONV_TC), lambda i, j: (i, j))
    return pl.pallas_call(
        body, name=name, grid=(s // tm, 2 * N_CT), in_specs=[spec] * 3, out_specs=o_spec,
        out_shape=_sds((s, 2 * D_FF)), compiler_params=_params(("parallel", "parallel")),
    )(dact, hv, hg)


def _conv_bwd(dhc, hu, cw, name, tm=256):
    s = dhc.shape[0]
    n8 = tm // 8
    off = 0
    last8 = s // 8 - 1

    def body(d_ref, dn_ref, x_ref, t_ref, w_ref, dx_ref, dw_ref, db_ref):
        i = pl.program_id(1)

        @pl.when(i == 0)
        def _():
            dw_ref[...] = jnp.zeros_like(dw_ref)
            db_ref[...] = jnp.zeros_like(db_ref)

        has_prev = (i > 0).astype(F32)
        has_next = (i < s // tm - 1).astype(F32)
        row = lax.broadcasted_iota(jnp.int32, (tm, CONV_TC), 0)
        d = d_ref[...]
        n0 = _row_of(dn_ref[...], 0) * has_next
        n1 = _row_of(dn_ref[...], 1) * has_next
        d1 = jnp.where(row == tm - 1, n0, pltpu.roll(d, tm - 1, axis=0))
        d2 = jnp.where(row == tm - 2, n0, jnp.where(row == tm - 1, n1, pltpu.roll(d, tm - 2, axis=0)))
        dx_ref[...] = (w_ref[2:3, :] * d + w_ref[1:2, :] * d1 + w_ref[0:1, :] * d2).astype(BF16)
        x = x_ref[...]
        x1, x2 = _shift_down(x, t_ref[...], has_prev, row)
        dw_ref[0:1, :] += jnp.sum(d * x2, axis=0, keepdims=True)
        dw_ref[1:2, :] += jnp.sum(d * x1, axis=0, keepdims=True)
        dw_ref[2:3, :] += jnp.sum(d * x, axis=0, keepdims=True)
        db_ref[...] += jnp.sum(d, axis=0, keepdims=True)

    d_spec = pl.BlockSpec((tm, CONV_TC), lambda j, i: (i, j))
    dn_spec = pl.BlockSpec((8, CONV_TC), lambda j, i: (jnp.minimum((i + 1) * n8, last8), j))
    x_spec = pl.BlockSpec((tm, CONV_TC), lambda j, i: (i, j + off))
    t_spec = pl.BlockSpec((8, CONV_TC), lambda j, i: (jnp.maximum(i * n8 - 1, 0), j + off))
    w_spec = pl.BlockSpec((3, CONV_TC), lambda j, i: (0, j + off))
    dw_spec = pl.BlockSpec((3, CONV_TC), lambda j, i: (0, j))
    db_spec = pl.BlockSpec((1, CONV_TC), lambda j, i: (0, j))
    return pl.pallas_call(
        body, name=name, grid=(2 * N_CT, s // tm), in_specs=[d_spec, dn_spec, x_spec, t_spec, w_spec],
        out_specs=[d_spec, dw_spec, db_spec],
        out_shape=[_sds((s, 2 * D_FF), BF16), _sds((3, 2 * D_FF)), _sds((1, 2 * D_FF))],
        compiler_params=_params(("parallel", "arbitrary")),
    )(dhc, dhc, hu, hu, cw)


def _ple_fwd(h, gp, pp, name, tm=256):
    s, d = h.shape

    def body(h_ref, g_ref, p_ref, o_ref):
        o_ref[...] = h_ref[...] + _sigmoid(g_ref[...]) * p_ref[...]

    return _rowcall(body, name, s, tm, [_rb(tm, d)] * 3, (h, gp, pp), _rb(tm, d), _sds((s, d)))


def _ple_bwd(dh, gp, pp, name, tm=256):
    s, d = dh.shape

    def body(d_ref, g_ref, p_ref, dp_ref, dg_ref):
        sg = _sigmoid(g_ref[...])
        dv = d_ref[...]
        dp_ref[...] = (dv * sg).astype(BF16)
        dg_ref[...] = (dv * p_ref[...] * sg * (1.0 - sg)).astype(BF16)

    return _rowcall(body, name, s, tm, [_rb(tm, d)] * 3, (dh, gp, pp), [_rb(tm, d)] * 2,
                    [_sds((s, d), BF16)] * 2)


SCALE = HEAD_DIM ** -0.5
ATT_ROWS = 2048


def _att_geom(s, dil):
    w = min(ATT_ROWS, s)
    p = BLK * dil
    assert w % p == 0 and s % w == 0
    return w, p, w // p


def _rows(start, dil):
    return pl.ds(start, BLK, stride=dil) if dil > 1 else pl.ds(start, BLK)


def _head_masks():
    lane = lax.broadcasted_iota(jnp.int32, (1, BLK), 1)
    return [lane < HEAD_DIM, lane >= HEAD_DIM]


def _band_valid(has_prev):
    qi = lax.broadcasted_iota(jnp.int32, (BLK, 2 * BLK), 0)
    ki = lax.broadcasted_iota(jnp.int32, (BLK, 2 * BLK), 1)
    rel = qi + BLK - ki
    band = (rel >= 0) & (rel <= BLK)
    if has_prev is True:
        return band
    return band & (has_prev | (ki >= BLK))


def _zcur(w):
    return lambda off: pl.BlockSpec((w, BLK), lambda hp, i: (i, off + hp))


def _zprev(p, nb):
    return lambda off: pl.BlockSpec((p, BLK), lambda hp, i: (jnp.maximum(i * nb - 1, 0), off + hp))


def _znext(p, nb, s):
    return lambda off: pl.BlockSpec((p, BLK), lambda hp, i: (jnp.minimum((i + 1) * nb, s // p - 1), off + hp))


def _scur(w):
    return pl.BlockSpec((w, BLK), lambda hp, i: (i, hp))


def _snext(p, nb, s):
    return pl.BlockSpec((p, BLK), lambda hp, i: (jnp.minimum((i + 1) * nb, s // p - 1), hp))


def _bspec():
    return pl.BlockSpec((2, BLK, 2 * BLK), lambda hp, i: (hp, 0, 0))


def _attn_fwd(z, bias, state, dil, first, last, name):
    s = z.shape[0]
    w, p, nb = _att_geom(s, dil)

    def body(*refs):
        q_ref, kp_ref, kc_ref, vp_ref, vc_ref, b_ref = refs[:6]
        rest = refs[6:]
        if not first:
            m_ref, l_ref, a_ref = rest[:3]
            rest = rest[3:]
        i = pl.program_id(1)
        for r in range(dil):
            for b in range(nb):
                rows = _rows(r + p * b, dil)
                prev_rows = _rows(r + p * (b - 1), dil) if b > 0 else _rows(r, dil)
                kprev, vprev = (kc_ref, vc_ref) if b > 0 else (kp_ref, vp_ref)
                q = q_ref[rows, :]
                k = jnp.concatenate([kprev[prev_rows, :], kc_ref[rows, :]], axis=0).astype(BF16)
                v = jnp.concatenate([vprev[prev_rows, :], vc_ref[rows, :]], axis=0).astype(BF16)
                valid = _band_valid(True if b > 0 else i > 0)
                mb = lb = ob = None
                for hh, mh in enumerate(_head_masks()):
                    qh = jnp.where(mh, q, 0.0).astype(BF16)
                    sc = lax.dot_general(qh, k, _DIMS["nt"], preferred_element_type=F32) * SCALE + b_ref[hh]
                    sc = jnp.where(valid, sc, NEG_INF)
                    mx = jnp.max(sc, axis=1, keepdims=True)
                    e = jnp.exp(sc - mx)
                    den = jnp.sum(e, axis=1, keepdims=True)
                    o = jnp.dot(e.astype(BF16), v, preferred_element_type=F32)
                    if hh == 0:
                        mb = jnp.broadcast_to(mx, (BLK, BLK))
                        lb = jnp.broadcast_to(den, (BLK, BLK))
                        ob = o
                    else:
                        mb = jnp.where(mh, mx, mb)
                        lb = jnp.where(mh, den, lb)
                        ob = jnp.where(mh, o, ob)
                if first:
                    m_new, l_new, a_new = mb, lb, ob
                else:
                    m_old = m_ref[rows, :]
                    m_new = jnp.maximum(m_old, mb)
                    al = jnp.exp(m_old - m_new)
                    be = jnp.exp(mb - m_new)
                    l_new = al * l_ref[rows, :] + be * lb
                    a_new = al * a_ref[rows, :] + be * ob
                if last:
                    y_ref, lse_ref = rest
                    y_ref[rows, :] = a_new / l_new
                    lse_ref[rows, :] = m_new + jnp.log(l_new)
                else:
                    mo_ref, lo_ref, ao_ref = rest
                    mo_ref[rows, :] = m_new
                    lo_ref[rows, :] = l_new
                    ao_ref[rows, :] = a_new

    cur, prv = _zcur(w), _zprev(p, nb)
    in_specs = [cur(0), prv(4), cur(4), prv(8), cur(8), _bspec()]
    args = [z, z, z, z, z, bias]
    if not first:
        in_specs += [_scur(w)] * 3
        args += list(state)
    n_out = 2 if last else 3
    return pl.pallas_call(
        body, name=name, grid=(4, s // w), in_specs=in_specs, out_specs=[_scur(w)] * n_out,
        out_shape=[_sds((s, ATTN_W))] * n_out,
        compiler_params=_params(("parallel", "parallel")),
    )(*args)


def _row_stats(mh, dy, y, lse):
    delta = jnp.sum(jnp.where(mh, dy * y, 0.0), axis=1, keepdims=True)
    lse_h = jnp.max(jnp.where(mh, lse, NEG_INF), axis=1, keepdims=True)
    return delta, lse_h


def _attn_bwd_dq(z, bias, dy, y, lse, dq_prev, dil, name):
    s = z.shape[0]
    w, p, nb = _att_geom(s, dil)
    first = dq_prev is None

    def body(*refs):
        q_ref, kp_ref, kc_ref, vp_ref, vc_ref, b_ref, dy_ref, y_ref, lse_ref = refs[:9]
        rest = refs[9:]
        if not first:
            dqp_ref = rest[0]
            rest = rest[1:]
        dq_ref, db_ref = rest
        i = pl.program_id(1)

        @pl.when(i == 0)
        def _():
            db_ref[...] = jnp.zeros_like(db_ref)

        db = [jnp.zeros((BLK, 2 * BLK), F32), jnp.zeros((BLK, 2 * BLK), F32)]
        for r in range(dil):
            for b in range(nb):
                rows = _rows(r + p * b, dil)
                prev_rows = _rows(r + p * (b - 1), dil) if b > 0 else _rows(r, dil)
                kprev, vprev = (kc_ref, vc_ref) if b > 0 else (kp_ref, vp_ref)
                q = q_ref[rows, :]
                k = jnp.concatenate([kprev[prev_rows, :], kc_ref[rows, :]], axis=0).astype(BF16)
                v = jnp.concatenate([vprev[prev_rows, :], vc_ref[rows, :]], axis=0).astype(BF16)
                dy_v = dy_ref[rows, :]
                y_v = y_ref[rows, :]
                lse_v = lse_ref[rows, :]
                valid = _band_valid(True if b > 0 else i > 0)
                dq = None
                for hh, mh in enumerate(_head_masks()):
                    delta, lse_h = _row_stats(mh, dy_v, y_v, lse_v)
                    qh = jnp.where(mh, q, 0.0).astype(BF16)
                    sc = lax.dot_general(qh, k, _DIMS["nt"], preferred_element_type=F32) * SCALE + b_ref[hh]
                    pr = jnp.where(valid, jnp.exp(jnp.where(valid, sc, NEG_INF) - lse_h), 0.0)
                    dyh = jnp.where(mh, dy_v, 0.0).astype(BF16)
                    dp = lax.dot_general(dyh, v, _DIMS["nt"], preferred_element_type=F32)
                    ds = pr * (dp - delta)
                    db[hh] = db[hh] + ds
                    dqh = jnp.dot(ds.astype(BF16), k, preferred_element_type=F32) * SCALE
                    dq = dqh if hh == 0 else jnp.where(mh, dqh, dq)
                if not first:
                    dq = dq + dqp_ref[rows, :]
                dq_ref[rows, :] = dq
        db_ref[0] += db[0]
        db_ref[1] += db[1]

    cur, prv = _zcur(w), _zprev(p, nb)
    in_specs = [cur(0), prv(4), cur(4), prv(8), cur(8), _bspec()] + [_scur(w)] * 3
    args = [z, z, z, z, z, bias, dy, y, lse]
    if not first:
        in_specs.append(_scur(w))
        args.append(dq_prev)
    return pl.pallas_call(
        body, name=name, grid=(4, s // w), in_specs=in_specs, out_specs=[_scur(w), _bspec()],
        out_shape=[_sds((s, ATTN_W)), _sds((N_HEADS, BLK, 2 * BLK))],
        compiler_params=_params(("parallel", "arbitrary")),
    )(*args)


def _attn_bwd_dkv(z, bias, dy, y, lse, dk_prev, dv_prev, dil, name):
    s = z.shape[0]
    w, p, nb = _att_geom(s, dil)
    n_steps = s // w
    first = dk_prev is None

    def body(*refs):
        k_ref, v_ref, b_ref = refs[:3]
        cur = refs[3:7]
        nx = refs[7:11]
        rest = refs[11:]
        if not first:
            dkp_ref, dvp_ref = rest[:2]
            rest = rest[2:]
        dk_ref, dv_ref = rest
        i = pl.program_id(1)
        qi = lax.broadcasted_iota(jnp.int32, (BLK, BLK), 0)
        ki = lax.broadcasted_iota(jnp.int32, (BLK, BLK), 1)

        def contrib(k, v, srcs, rows, same_block):
            q_ref, dy_ref, y_ref, lse_ref = srcs
            q = q_ref[rows, :]
            dy_v = dy_ref[rows, :]
            y_v = y_ref[rows, :]
            lse_v = lse_ref[rows, :]
            valid = (qi >= ki) if same_block else (ki >= qi)
            dk = jnp.zeros((BLK, BLK), F32)
            dv = jnp.zeros((BLK, BLK), F32)
            for hh, mh in enumerate(_head_masks()):
                delta, lse_h = _row_stats(mh, dy_v, y_v, lse_v)
                qh = jnp.where(mh, q, 0.0).astype(BF16)
                bsl = b_ref[hh, :, BLK:2 * BLK] if same_block else b_ref[hh, :, 0:BLK]
                sc = lax.dot_general(qh, k, _DIMS["nt"], preferred_element_type=F32) * SCALE + bsl
                pr = jnp.where(valid, jnp.exp(jnp.where(valid, sc, NEG_INF) - lse_h), 0.0)
                dyh = jnp.where(mh, dy_v, 0.0).astype(BF16)
                dp = lax.dot_general(dyh, v, _DIMS["nt"], preferred_element_type=F32)
                ds = pr * (dp - delta)
                dk = dk + lax.dot_general(ds.astype(BF16), qh, _DIMS["tn"], preferred_element_type=F32) * SCALE
                dv = dv + lax.dot_general(pr.astype(BF16), dyh, _DIMS["tn"], preferred_element_type=F32)
            return dk, dv

        for r in range(dil):
            for b in range(nb):
                rows = _rows(r + p * b, dil)
                k = k_ref[rows, :].astype(BF16)
                v = v_ref[rows, :].astype(BF16)
                dk0, dv0 = contrib(k, v, cur, rows, True)
                if not first:
                    dk0 = dk0 + dkp_ref[rows, :]
                    dv0 = dv0 + dvp_ref[rows, :]
                if b < nb - 1:
                    dk1, dv1 = contrib(k, v, cur, _rows(r + p * (b + 1), dil), False)
                    dk_ref[rows, :] = dk0 + dk1
                    dv_ref[rows, :] = dv0 + dv1
                else:
                    dk_ref[rows, :] = dk0
                    dv_ref[rows, :] = dv0

                    @pl.when(i < n_steps - 1)
                    def _():
                        dk1, dv1 = contrib(k, v, nx, _rows(r, dil), False)
                        dk_ref[rows, :] += dk1
                        dv_ref[rows, :] += dv1

    cur, nxt = _zcur(w), _znext(p, nb, s)
    in_specs = ([cur(4), cur(8), _bspec(), cur(0)] + [_scur(w)] * 3 + [nxt(0)] + [_snext(p, nb, s)] * 3)
    args = [z, z, bias, z, dy, y, lse, z, dy, y, lse]
    if not first:
        in_specs += [_scur(w)] * 2
        args += [dk_prev, dv_prev]
    return pl.pallas_call(
        body, name=name, grid=(4, n_steps), in_specs=in_specs, out_specs=[_scur(w)] * 2,
        out_shape=[_sds((s, ATTN_W))] * 2,
        compiler_params=_params(("parallel", "parallel")),
    )(*args)


def _t5_bucket(dist):
    max_exact = N_BUCKETS // 2
    d = np.maximum(dist, 0)
    large = max_exact + (np.log(np.maximum(d, 1) / max_exact) / np.log(REL_MAX / max_exact)
                         * (N_BUCKETS - max_exact)).astype(np.int32)
    large = np.minimum(large, N_BUCKETS - 1)
    return np.where(d < max_exact, d, large).astype(np.int32)


def _bias_tables(rel_bias):
    period = 3 * BLK
    tabs = []
    for _, dil in BRANCHES:
        onehot = np.zeros((period, N_BUCKETS), np.float32)
        d = np.arange(BLK + 1)
        onehot[d, _t5_bucket((BLK - d) * dil)] = 1.0
        f = jnp.dot(jnp.asarray(onehot), rel_bias, precision=lax.Precision.HIGHEST)
        flat = jnp.tile(f.T, (1, BLK))[:, :BLK * (period - 1)]
        tabs.append(flat.reshape(N_HEADS, BLK, period - 1)[:, :, :2 * BLK])
    return tabs


def _bucket_onehot():
    maps = []
    q = np.arange(BLK)[:, None]
    k = np.arange(2 * BLK)[None, :]
    rel = q + BLK - k
    for _, dil in BRANCHES:
        maps.append(np.where((rel >= 0) & (rel <= BLK), _t5_bucket(rel * dil), -1).reshape(-1))
    bmap = jnp.asarray(np.concatenate(maps).astype(np.int32))
    return (bmap[:, None] == jnp.arange(128, dtype=jnp.int32)[None, :]).astype(BF16)


def _block_diag(t):
    g, n, c = t.shape
    eye = jnp.eye(g, dtype=t.dtype)
    return (t[:, :, None, :] * eye[:, None, :, None]).reshape(g * n, g * c)


def _ssm_prep(a_re, a_im, log_dt, b_re, b_im, c_re, c_im):
    lam = lax.complex(a_re, a_im)
    dt = jnp.exp(log_dt)[:, None]
    a_bar = jnp.exp(lam * dt)
    b_bar = ((a_bar - 1.0) / lam)[:, :, None] * lax.complex(b_re, b_im)
    bdt = jnp.concatenate([_block_diag(jnp.real(b_bar)), _block_diag(jnp.imag(b_bar))], axis=0)
    cd = jnp.concatenate([_block_diag(jnp.transpose(c_re, (0, 2, 1))),
                          _block_diag(-jnp.transpose(c_im, (0, 2, 1)))], axis=0)
    return jnp.real(a_bar).reshape(-1), jnp.imag(a_bar).reshape(-1), bdt, cd


def _powers(ar, ai):
    pr, pi = ar[:, None], ai[:, None]
    k = 1
    while k < 128:
        lr, li = pr[:, -1:], pi[:, -1:]
        pr, pi = (jnp.concatenate([pr, pr * lr - pi * li], axis=1),
                  jnp.concatenate([pi, pr * li + pi * lr], axis=1))
        k *= 2
    return pr, pi


def _sgu_bias_expand(b):
    return jnp.repeat(b.T, 64, axis=1)


def _layer_fwd(i, h, p_i, big, small, bias_tabs):
    nm = "l%d_" % i
    sv = {"h": h}
    a1 = _rms_fwd(h, small["norm_attn_g"][i], nm + "rms_attn")
    z = _mm(a1, big["w_in"], "nt", nm + "in_proj", b_layer=i)
    st = None
    for b, (_, dil) in enumerate(BRANCHES):
        st = _attn_fwd(z, bias_tabs[b], st, dil, b == 0, b == len(BRANCHES) - 1, nm + "attn_fwd%d" % b)
    y_attn, lse = st
    bexp = _sgu_bias_expand(small["sgu_b"][i])
    y_sgu = _sgu_fwd(z, small["sgu_ln_g"][i], small["sgu_ln_b"][i], small["sgu_w"][i], bexp, nm + "sgu_fwd")
    ar, ai, bdt, cd = _ssm_prep(*[small[k][i] for k in ("ssm_a_re", "ssm_a_im", "ssm_log_dt", "ssm_b_re",
                                                         "ssm_b_im", "ssm_c_re", "ssm_c_im")])
    pr, pi = _powers(ar, ai)
    u = z[:, IN_W - SSM_W:].astype(BF16)
    bu_t = _mm(bdt, u, "nt", nm + "ssm_bu")
    xr, xi = _scan_fwd(bu_t, jnp.concatenate([pr, pi], axis=0), nm + "ssm_scan")
    yc = _mm(xr, cd[:SSM_NS], "tn", nm + "ssm_cx_re")
    yc = _mm(xi, cd[SSM_NS:], "tn", nm + "ssm_cx_im", add=yc)
    y_ssm = _ssm_post_fwd(yc, z, small["ssm_d"][i], big["ssm_glu_w"][i], small["ssm_glu_b"][i], nm + "ssm_post")
    mix = _mix_fwd(y_attn, y_sgu, y_ssm, small["branch_norm_g"][i], nm + "mix")
    h2 = _mm(mix, big["w_out"], "nn", nm + "out_proj", add=h, b_layer=i)
    a2 = _rms_fwd(h2, small["norm_ffn_g"][i], nm + "rms_ffn")
    hu = _mm(a2, big["ffn_w_up"], "nt", nm + "ffn_up", b_layer=i)
    hv, hg, act = _conv_fwd(hu, big["ffn_conv_w"][i], small["ffn_conv_b"][i], nm + "ffn_conv")
    h3 = _mm(act, big["ffn_w_down"], "nn", nm + "ffn_down", add=h2, b_layer=i)
    a3 = _rms_fwd(h3, small["norm_ple_g"][i], nm + "rms_ple")
    gp = _mm(a3, big["ple_w_gate"], "nn", nm + "ple_gate", b_layer=i)
    pp = _mm(p_i, big["ple_w_proj"], "nt", nm + "ple_proj", b_layer=i)
    h4 = _ple_fwd(h3, gp, pp, nm + "ple_add")
    sv.update(a1=a1, z=z, y_attn=y_attn, lse=lse, y_sgu=y_sgu, y_ssm=y_ssm, yc=yc, xr=xr, xi=xi, mix=mix, h2=h2,
              a2=a2, hu=hu, hv=hv, hg=hg, act=act, h3=h3, a3=a3, gp=gp, pp=pp, u=u)
    return h4, sv


def _layer_bwd(i, dh4, sv, p_i, big, small, bias_tabs):
    nm = "l%d_" % i
    g = {}
    dpp, dgp = _ple_bwd(dh4, sv["gp"], sv["pp"], nm + "ple_bwd")
    g["ple_w_proj"] = _mm(dpp, p_i, "tn", nm + "d_ple_proj", out_dtype=BF16)
    g["ple_w_gate"] = _mm(sv["a3"], dgp, "tn", nm + "d_ple_gate", out_dtype=BF16)
    da3 = _mm(dgp, big["ple_w_gate"], "nt", nm + "ple_gate_t", b_layer=i)
    dh3, g["norm_ple_g"] = _rms_bwd(da3, sv["h3"], small["norm_ple_g"][i], dh4, nm + "rms_ple_bwd")
    g["ffn_w_down"] = _mm(sv["act"], dh3, "tn", nm + "d_ffn_down", out_dtype=BF16)
    dact = _mm(dh3, big["ffn_w_down"], "nt", nm + "ffn_down_t", b_layer=i)
    dhc = _conv_bwd_act(dact, sv["hv"], sv["hg"], nm + "ffn_act_bwd")
    dhu, g["ffn_conv_w"], dcb = _conv_bwd(dhc, sv["hu"], big["ffn_conv_w"][i], nm + "ffn_conv_bwd")
    g["ffn_conv_b"] = dcb.reshape(2 * D_FF)
    g["ffn_w_up"] = _mm(dhu, sv["a2"], "tn", nm + "d_ffn_up", out_dtype=BF16)
    da2 = _mm(dhu, big["ffn_w_up"], "nn", nm + "ffn_up_t", b_layer=i)
    dh2, g["norm_ffn_g"] = _rms_bwd(da2, sv["h2"], small["norm_ffn_g"][i], dh3, nm + "rms_ffn_bwd")
    g["w_out"] = _mm(sv["mix"], dh2, "tn", nm + "d_out_proj", out_dtype=BF16)
    dmix = _mm(dh2, big["w_out"], "nt", nm + "out_proj_t", b_layer=i)
    dya, dysg, dyss, g["branch_norm_g"] = _mix_bwd(dmix, sv["y_attn"], sv["y_sgu"], sv["y_ssm"],
                                                   small["branch_norm_g"][i], nm + "mix_bwd")
    ssm_keys = ("ssm_a_re", "ssm_a_im", "ssm_log_dt", "ssm_b_re", "ssm_b_im", "ssm_c_re", "ssm_c_im")
    (ar, ai, bdt, cd), prep_vjp = jax.vjp(_ssm_prep, *[small[k][i] for k in ssm_keys])
    pr, pi = _powers(ar, ai)
    pw_rev = jnp.concatenate([pr[:, ::-1], -pi[:, ::-1]], axis=0)
    dy1, dgl, y2, dud, g["ssm_d"], g["ssm_glu_b"] = _ssm_post_bwd(
        dyss, sv["yc"], sv["z"], small["ssm_d"][i], big["ssm_glu_w"][i], small["ssm_glu_b"][i], nm + "ssm_post_bwd")
    g["ssm_glu_w"] = _mm(y2, dgl, "tn", nm + "d_ssm_glu", out_dtype=BF16)
    g_t = _mm(cd, dy1, "nt", nm + "ssm_cx_t")
    dcd = jnp.concatenate([_mm(sv["xr"], dy1, "nn", nm + "d_ssm_c_re"),
                           _mm(sv["xi"], dy1, "nn", nm + "d_ssm_c_im")], axis=0)
    lr, li, dar, dai = _scan_bwd(g_t, sv["xr"], sv["xi"], pw_rev, nm + "ssm_scan_bwd")
    u = sv["u"]
    dbdt = jnp.concatenate([_mm(lr, u, "nn", nm + "d_ssm_b_re"), _mm(li, u, "nn", nm + "d_ssm_b_im")], axis=0)
    du = _mm(lr, bdt[:SSM_NS], "tn", nm + "ssm_bu_t_re", add=dud)
    du = _mm(li, bdt[SSM_NS:], "tn", nm + "ssm_bu_t_im", add=du)
    for k, val in zip(ssm_keys, prep_vjp((dar, dai, dbdt, dcd))):
        g[k] = val
    bexp, bexp_vjp = jax.vjp(_sgu_bias_expand, small["sgu_b"][i])
    dzs, g["sgu_w"], dbexp, g["sgu_ln_g"], g["sgu_ln_b"] = _sgu_bwd(
        sv["z"], dysg, small["sgu_ln_g"][i], small["sgu_ln_b"][i], small["sgu_w"][i], bexp, nm + "sgu_bwd")
    g["sgu_b"] = bexp_vjp(dbexp)[0]
    dq = dk = dv = None
    dbs = []
    for b, (_, dil) in enumerate(BRANCHES):
        dq, db = _attn_bwd_dq(sv["z"], bias_tabs[b], dya, sv["y_attn"], sv["lse"], dq, dil, nm + "attn_dq%d" % b)
        dk, dv = _attn_bwd_dkv(sv["z"], bias_tabs[b], dya, sv["y_attn"], sv["lse"], dk, dv, dil,
                               nm + "attn_dkv%d" % b)
        dbs.append(db.reshape(N_HEADS, BLK * 2 * BLK))
    dz = jnp.concatenate([dq, dk, dv, dzs, du], axis=1).astype(BF16)
    g["w_in"] = _mm(dz, sv["a1"], "tn", nm + "d_in_proj", out_dtype=BF16)
    da1 = _mm(dz, big["w_in"], "nn", nm + "in_proj_t", b_layer=i)
    dh, g["norm_attn_g"] = _rms_bwd(da1, sv["h"], small["norm_attn_g"][i], dh2, nm + "rms_attn_bwd")
    return dh, g, jnp.concatenate(dbs, axis=1)


def _local_step(x, p, target, big, small):
    depth = p.shape[0]
    bias_tabs = _bias_tables(small["rel_bias"])
    h = x
    saved = []
    for i in range(depth):
        h, sv = _layer_fwd(i, h, p[i], big, small, bias_tabs)
        saved.append(sv)
    dh, loss, g_final = _loss_head(h, target, small["final_norm_g"], "loss_head")
    layer_grads = [None] * depth
    dbias = [None] * depth
    for i in reversed(range(depth)):
        dh, layer_grads[i], dbias[i] = _layer_bwd(i, dh, saved[i], p[i], big, small, bias_tabs)
    grads = {k: jnp.stack([layer_grads[i][k] for i in range(depth)]) for k in layer_grads[0]}
    grads["final_norm_g"] = g_final
    onehot = _bucket_onehot()
    g_rb = _mm(jnp.concatenate(dbias, axis=1), jnp.concatenate([onehot] * depth, axis=0), "nn", "d_rel_bias",
               tk=2048)
    grads["rel_bias"] = g_rb[:, :N_BUCKETS].T
    return loss, dh, grads


_ANY = pl.BlockSpec(memory_space=pl.ANY)
MESH_IDS = pl.DeviceIdType.MESH


def _slot(ref, axis, j):
    return ref.at[(slice(None),) * axis + (j,)]


def _all_gather(blocks, axis, name):
    nt = len(blocks)

    def body(*refs):
        x_refs, o_refs = refs[:nt], refs[nt:2 * nt]
        send_sems, recv_sems, local_sems = refs[2 * nt:]
        x, y, c = lax.axis_index("x"), lax.axis_index("y"), lax.axis_index("c")
        me, sibling = (x, y, c), (x, y, 1 - c)
        chips = [(1 - x, y), (x, 1 - y), (1 - x, 1 - y)]

        def slot(t, px, py, pc):
            return _slot(o_refs[t], axis, 4 * px + 2 * py + pc)

        def copy(t, k, blk, to, src=None):
            return pltpu.make_async_remote_copy(
                src_ref=slot(t, *blk) if src is None else src, dst_ref=slot(t, *blk),
                send_sem=send_sems.at[7 * t + k], recv_sem=recv_sems.at[7 * t + k],
                device_id=to, device_id_type=MESH_IDS)

        mine = [pltpu.make_async_copy(x_refs[t], slot(t, *me), local_sems.at[t]) for t in range(nt)]
        for cp in mine:
            cp.start()
        first = []
        for t in range(nt):
            first.append(copy(t, 0, me, sibling, src=x_refs[t]))
            first += [copy(t, 1 + j, me, (*chip, c), src=x_refs[t]) for j, chip in enumerate(chips)]
        for cp in first:
            cp.start()
        passed = []
        for t in range(nt):
            for j, chip in enumerate(chips):
                copy(t, 1 + j, (*chip, c), me).wait_recv()
                passed.append(copy(t, 4 + j, (*chip, c), sibling))
                passed[-1].start()
        for t in range(nt):
            copy(t, 0, sibling, me).wait_recv()
            for j, chip in enumerate(chips):
                copy(t, 4 + j, (*chip, 1 - c), me).wait_recv()
        for cp in first + passed:
            cp.wait_send()
        for cp in mine:
            cp.wait()

    out_shape = [jax.ShapeDtypeStruct(b.shape[:axis] + (N_DEV,) + b.shape[axis:], b.dtype) for b in blocks]
    return pl.pallas_call(
        body, name=name, out_shape=out_shape, in_specs=[_ANY] * nt, out_specs=[_ANY] * nt,
        scratch_shapes=[pltpu.SemaphoreType.DMA((7 * nt,)), pltpu.SemaphoreType.DMA((7 * nt,)),
                        pltpu.SemaphoreType.DMA((nt,))],
    )(*blocks)


def _all_to_all(blocks, name):
    nt = len(blocks)

    def body(*refs):
        x_refs, o_refs = refs[:nt], refs[nt:2 * nt]
        send_sems, recv_sems, local_sems = refs[2 * nt:]
        x, y, c = lax.axis_index("x"), lax.axis_index("y"), lax.axis_index("c")
        me = 4 * x + 2 * y + c
        mine = [pltpu.make_async_copy(_slot(x_refs[t], 1, me), o_refs[t].at[me], local_sems.at[t])
                for t in range(nt)]
        for cp in mine:
            cp.start()
        copies = []
        for k in range(1, N_DEV):
            px = 1 - x if k & 4 else x
            py = 1 - y if k & 2 else y
            pc = 1 - c if k & 1 else c
            for t in range(nt):
                cp = pltpu.make_async_remote_copy(
                    src_ref=_slot(x_refs[t], 1, 4 * px + 2 * py + pc), dst_ref=o_refs[t].at[me],
                    send_sem=send_sems.at[7 * t + k - 1], recv_sem=recv_sems.at[7 * t + k - 1],
                    device_id=(px, py, pc), device_id_type=MESH_IDS)
                cp.start()
                copies.append(cp)
        for cp in copies:
            cp.wait()
        for cp in mine:
            cp.wait()

    out_shape = [jax.ShapeDtypeStruct((N_DEV, b.shape[0]) + b.shape[2:], b.dtype) for b in blocks]
    return pl.pallas_call(
        body, name=name, out_shape=out_shape, in_specs=[_ANY] * nt, out_specs=[_ANY] * nt,
        scratch_shapes=[pltpu.SemaphoreType.DMA((7 * nt,)), pltpu.SemaphoreType.DMA((7 * nt,)),
                        pltpu.SemaphoreType.DMA((nt,))],
    )(*blocks)


def _adamw(parts, w, m, v, name, tr):
    r, c_ = w.shape

    def body(p_ref, w_ref, m_ref, v_ref, g_ref, d_ref, mo_ref, vo_ref):
        g = p_ref[0].astype(F32)
        for j in range(1, N_DEV):
            g = g + p_ref[j].astype(F32)
        m2 = ADAM_B1 * m_ref[...] + (1.0 - ADAM_B1) * g
        v2 = ADAM_B2 * v_ref[...] + (1.0 - ADAM_B2) * (g * g)
        m_hat = m2 / (1.0 - ADAM_B1 ** ADAM_STEP)
        v_hat = v2 / (1.0 - ADAM_B2 ** ADAM_STEP)
        g_ref[...] = g
        d_ref[...] = -ADAM_LR * (m_hat / (jnp.sqrt(v_hat) + ADAM_EPS) + ADAM_WD * w_ref[...])
        mo_ref[...] = m2
        vo_ref[...] = v2

    spec = pl.BlockSpec((tr, c_), lambda i: (i, 0))
    return pl.pallas_call(
        body, name=name, grid=(r // tr,),
        in_specs=[pl.BlockSpec((N_DEV, tr, c_), lambda i: (0, i, 0)), spec, spec, spec],
        out_specs=[spec] * 4, out_shape=[_sds((r, c_))] * 4, compiler_params=_params(("parallel",)),
    )(parts, w, m, v)


def _pack_rows(n_elems, align):
    rows = -(-n_elems // PACK_COLS)
    return -(-rows // align) * align


def _pack(arrs, rows, dtype=F32):
    flat = jnp.concatenate([a.reshape(-1) for a in arrs]).astype(dtype)
    return jnp.pad(flat, (0, rows * PACK_COLS - flat.shape[0])).reshape(rows, PACK_COLS)


def _unpack(pack, shapes):
    flat = pack.reshape(-1)
    out, off = [], 0
    for shp in shapes:
        size = int(np.prod(shp))
        out.append(flat[off:off + size].reshape(shp))
        off += size
    return out


def _tile_rows(rows, target, align=16):
    best = align
    for t in range(align, target + 1, align):
        if rows % t == 0:
            best = t
    return best


COMM_NAMES = ("w_in", "ssm_glu_w", "w_out", "ffn_w_up", "ffn_w_down", "ple_w_gate", "ple_w_proj")
COMM_TRANSPOSED = ("w_in", "ffn_w_up", "ple_w_proj")
SMALL_TILE_ROWS = 64
CONV_NAME = "ffn_conv_w"


def _to_comm(name, a):
    return jnp.swapaxes(a, 1, 2) if name in COMM_TRANSPOSED else a


def kernel(x, p, rel_bias, norm_attn_g, w_in, sgu_ln_g, sgu_ln_b, sgu_w, sgu_b, ssm_a_re, ssm_a_im, ssm_log_dt, ssm_b_re, ssm_b_im, ssm_c_re, ssm_c_im, ssm_d, ssm_glu_w, ssm_glu_b, branch_norm_g, w_out, norm_ffn_g, ffn_w_up, ffn_conv_w, ffn_conv_b, ffn_w_down, norm_ple_g, ple_w_gate, ple_w_proj, final_norm_g, loss_target, m_rel_bias, m_norm_attn_g, m_w_in, m_sgu_ln_g, m_sgu_ln_b, m_sgu_w, m_sgu_b, m_ssm_a_re, m_ssm_a_im, m_ssm_log_dt, m_ssm_b_re, m_ssm_b_im, m_ssm_c_re, m_ssm_c_im, m_ssm_d, m_ssm_glu_w, m_ssm_glu_b, m_branch_norm_g, m_w_out, m_norm_ffn_g, m_ffn_w_up, m_ffn_conv_w, m_ffn_conv_b, m_ffn_w_down, m_norm_ple_g, m_ple_w_gate, m_ple_w_proj, m_final_norm_g, v_rel_bias, v_norm_attn_g, v_w_in, v_sgu_ln_g, v_sgu_ln_b, v_sgu_w, v_sgu_b, v_ssm_a_re, v_ssm_a_im, v_ssm_log_dt, v_ssm_b_re, v_ssm_b_im, v_ssm_c_re, v_ssm_c_im, v_ssm_d, v_ssm_glu_w, v_ssm_glu_b, v_branch_norm_g, v_w_out, v_norm_ffn_g, v_ffn_w_up, v_ffn_conv_w, v_ffn_conv_b, v_ffn_w_down, v_norm_ple_g, v_ple_w_gate, v_ple_w_proj, v_final_norm_g):
    given = dict(locals())
    w = {n: given[n] for n in WEIGHT_NAMES}
    m = {n: given["m_" + n] for n in WEIGHT_NAMES}
    v = {n: given["v_" + n] for n in WEIGHT_NAMES}
    depth = p.shape[0]
    dev = 4 * lax.axis_index("x") + 2 * lax.axis_index("y") + lax.axis_index("c")

    wc = {n: _to_comm(n, w[n]) for n in COMM_NAMES}
    gathered = _all_gather([wc[n].astype(BF16) for n in COMM_NAMES], 1, "gather_weights")
    big = {n: t.reshape(depth, -1, t.shape[-1]) for n, t in zip(COMM_NAMES, gathered)}
    conv_local = [w[CONV_NAME], m[CONV_NAME], v[CONV_NAME]]
    conv_rows = _pack_rows(sum(int(np.prod(t.shape)) for t in conv_local), 8)
    conv_g, = _all_gather([_pack(conv_local, conv_rows)], 0, "gather_conv_taps")
    conv_parts = zip(*[_unpack(conv_g[j], [t.shape for t in conv_local]) for j in range(N_DEV)])
    conv_w, conv_m, conv_v = [jnp.concatenate(parts, axis=2) for parts in conv_parts]
    big[CONV_NAME] = conv_w
    small = {n: w[n] for n in SMALL_NAMES}

    loss, dx, grads = _local_step(x[0], p[:, 0], loss_target[0], big, small)
    loss = lax.psum(loss, ("x", "y", "c"))

    send = [grads[n].reshape((depth, N_DEV, wc[n].shape[1], wc[n].shape[2])) for n in COMM_NAMES]
    recv = _all_to_all(send, "scatter_weight_grads")
    rep_names = SMALL_NAMES + (CONV_NAME,)
    rep_w = dict(small, **{CONV_NAME: conv_w})
    rep_m = dict({n: m[n] for n in SMALL_NAMES}, **{CONV_NAME: conv_m})
    rep_v = dict({n: v[n] for n in SMALL_NAMES}, **{CONV_NAME: conv_v})
    rep_shapes = [rep_w[n].shape for n in rep_names]
    rep_rows = _pack_rows(sum(int(np.prod(s)) for s in rep_shapes), SMALL_TILE_ROWS)
    rep_parts, = _all_gather([_pack([grads[n] for n in rep_names], rep_rows)], 0, "gather_small_grads")

    out = {}
    for n, parts in zip(COMM_NAMES, recv):
        shp = wc[n].shape
        rows = shp[0] * shp[1]
        res = _adamw(parts.reshape(N_DEV, rows, shp[2]), wc[n].reshape(rows, shp[2]),
                     _to_comm(n, m[n]).reshape(rows, shp[2]), _to_comm(n, v[n]).reshape(rows, shp[2]),
                     "adamw_" + n, _tile_rows(rows, 256))
        out[n] = [_to_comm(n, t.reshape(shp)) for t in res]
    rep_out = _adamw(rep_parts, _pack([rep_w[n] for n in rep_names], rep_rows),
                     _pack([rep_m[n] for n in rep_names], rep_rows), _pack([rep_v[n] for n in rep_names], rep_rows),
                     "adamw_replicated", SMALL_TILE_ROWS)
    for n, vals in zip(rep_names, zip(*[_unpack(t, rep_shapes) for t in rep_out])):
        out[n] = list(vals)
    shard = ffn_conv_w.shape[2]
    out[CONV_NAME] = [lax.dynamic_slice_in_dim(t, dev * shard, shard, axis=2) for t in out[CONV_NAME]]
    results = [[out[n][kind] for n in WEIGHT_NAMES] for kind in range(4)]
    return (loss, dx[None], *results[0], *results[1], *results[2], *results[3])
```

```python
import math

import numpy as np
import jax
import jax.numpy as jnp
from jax import lax
from jax.experimental import pallas as pl
from jax.experimental.pallas import tpu as pltpu

F32 = jnp.float32
BF16 = jnp.bfloat16

D_MODEL = 1024
HEAD_DIM = 64
N_HEADS = 8
ATTN_W = 512
SGU_W = 256
SGU_GROUPS = 4
SGU_CHUNK = 128
SSM_W = 256
SSM_GROUPS = 16
SSM_CH = 16
SSM_STATE = 64
SSM_NS = SSM_GROUPS * SSM_STATE
IN_W = 2304
D_FF = 2816
PLE_DIM = 256
BRANCHES = ((128, 1), (512, 4), (2048, 16))
BLK = 128
N_BUCKETS = 32
REL_MAX = 2048
EPS = 1e-6
NEG_INF = -1e30
N_DEV = 8

ADAM_LR = 0.001
ADAM_B1 = 0.9
ADAM_B2 = 0.999
ADAM_EPS = 1e-08
ADAM_WD = 0.01
ADAM_STEP = 10

VMEM_LIMIT_BYTES = 56 * 1024 * 1024
GELU_C = math.sqrt(2.0 / math.pi)

SMALL_NAMES = ("rel_bias", "norm_attn_g", "sgu_ln_g", "sgu_ln_b", "sgu_w", "sgu_b", "ssm_a_re", "ssm_a_im",
               "ssm_log_dt", "ssm_b_re", "ssm_b_im", "ssm_c_re", "ssm_c_im", "ssm_d", "ssm_glu_b",
               "branch_norm_g", "norm_ffn_g", "ffn_conv_b", "norm_ple_g", "final_norm_g")
WEIGHT_NAMES = ("rel_bias", "norm_attn_g", "w_in", "sgu_ln_g", "sgu_ln_b", "sgu_w", "sgu_b", "ssm_a_re",
                "ssm_a_im", "ssm_log_dt", "ssm_b_re", "ssm_b_im", "ssm_c_re", "ssm_c_im", "ssm_d", "ssm_glu_w",
                "ssm_glu_b", "branch_norm_g", "w_out", "norm_ffn_g", "ffn_w_up", "ffn_conv_w", "ffn_conv_b",
                "ffn_w_down", "norm_ple_g", "ple_w_gate", "ple_w_proj", "final_norm_g")
PACK_COLS = 512


def _params(sem):
    return pltpu.CompilerParams(dimension_semantics=sem, vmem_limit_bytes=VMEM_LIMIT_BYTES)


def _pick(dim, target):
    if dim <= target:
        return dim
    best = None
    for t in range(128, target + 1, 128):
        if dim % t == 0:
            best = t
    return dim if best is None else best


def _gelu(x):
    return 0.5 * x * (1.0 + jnp.tanh(GELU_C * (x + 0.044715 * (x * x * x))))


def _gelu_grad(x):
    t = jnp.tanh(GELU_C * (x + 0.044715 * (x * x * x)))
    return 0.5 * (1.0 + t) + 0.5 * x * (1.0 - t * t) * (GELU_C * (1.0 + 3.0 * 0.044715 * (x * x)))


def _sigmoid(x):
    return 1.0 / (1.0 + jnp.exp(-x))


_DIMS = {"nn": (((1,), (0,)), ((), ())), "tn": (((0,), (0,)), ((), ())), "nt": (((1,), (1,)), ((), ()))}


def _mm(a, b, mode, name, add=None, out_dtype=F32, b_layer=None, tm=1024, tn=1408, tk=1408):
    b_shape = b.shape if b_layer is None else b.shape[1:]
    if mode == "nn":
        m, k = a.shape
        k2, n = b_shape
    elif mode == "tn":
        k, m = a.shape
        k2, n = b_shape
    else:
        m, k = a.shape
        n, k2 = b_shape
    assert k == k2, (name, a.shape, b.shape, mode)
    tm, tn, tk = _pick(m, tm), _pick(n, tn), _pick(k, tk)
    nk = k // tk
    dims = _DIMS[mode]
    has_add = add is not None

    def body(*refs):
        if has_add:
            a_ref, b_ref, add_ref, o_ref = refs[:4]
        else:
            a_ref, b_ref, o_ref = refs[:3]
        part = lax.dot_general(a_ref[...].astype(BF16), b_ref[...].astype(BF16), dims,
                               preferred_element_type=F32)

        def finish(r):
            if has_add:
                r = r + add_ref[...]
            o_ref[...] = r.astype(out_dtype)

        if nk == 1:
            finish(part)
            return
        acc_ref = refs[-1]
        kk = pl.program_id(2)

        @pl.when(kk == 0)
        def _():
            acc_ref[...] = part

        @pl.when((kk > 0) & (kk < nk - 1))
        def _():
            acc_ref[...] += part

        @pl.when(kk == nk - 1)
        def _():
            finish(acc_ref[...] + part)

    if mode == "tn":
        a_spec = pl.BlockSpec((tk, tm), lambda i, j, kk: (kk, i))
    else:
        a_spec = pl.BlockSpec((tm, tk), lambda i, j, kk: (i, kk))
    b_blk = (tn, tk) if mode == "nt" else (tk, tn)
    if mode == "nt":
        b_idx = lambda i, j, kk: (j, kk)
    else:
        b_idx = lambda i, j, kk: (kk, j)
    if b_layer is None:
        b_spec = pl.BlockSpec(b_blk, b_idx)
    else:
        b_spec = pl.BlockSpec((None,) + b_blk, lambda i, j, kk: (b_layer,) + b_idx(i, j, kk))
    o_spec = pl.BlockSpec((tm, tn), lambda i, j, kk: (i, j))
    in_specs = [a_spec, b_spec] + ([o_spec] if has_add else [])
    args = (a, b) + ((add,) if has_add else ())
    return pl.pallas_call(
        body, name=name, grid=(m // tm, n // tn, nk),
        in_specs=in_specs, out_specs=o_spec,
        out_shape=jax.ShapeDtypeStruct((m, n), out_dtype),
        scratch_shapes=[pltpu.VMEM((tm, tn), F32)] if nk > 1 else [],
        compiler_params=_params(("parallel", "parallel", "arbitrary")),
    )(*args)


def _rb(tm, w, cb=0):
    return pl.BlockSpec((tm, w), lambda i: (i, cb))


def _fb(shape):
    nd = len(shape)
    return pl.BlockSpec(shape, lambda i: (0,) * nd)


def _rowcall(body, name, n_rows, tm, in_specs, args, out_specs, out_shapes):
    return pl.pallas_call(
        body, name=name, grid=(n_rows // tm,), in_specs=in_specs, out_specs=out_specs, out_shape=out_shapes,
        compiler_params=_params(("arbitrary",)),
    )(*args)


def _sds(shape, dtype=F32):
    return jax.ShapeDtypeStruct(shape, dtype)


def _rms_fwd(h, g, name, tm=256):
    s, d = h.shape

    def body(h_ref, g_ref, o_ref):
        x = h_ref[...]
        r = lax.rsqrt(jnp.mean(x * x, axis=-1, keepdims=True) + EPS)
        o_ref[...] = (x * r * g_ref[...]).astype(BF16)

    return _rowcall(body, name, s, tm, [_rb(tm, d), _fb((1, d))], (h, g.reshape(1, d)), _rb(tm, d),
                    _sds((s, d), BF16))


def _rms_bwd(da, h, g, dres, name, tm=256):
    s, d = h.shape

    def body(da_ref, h_ref, g_ref, dres_ref, dh_ref, dg_ref):
        @pl.when(pl.program_id(0) == 0)
        def _():
            dg_ref[...] = jnp.zeros_like(dg_ref)

        x = h_ref[...]
        r = lax.rsqrt(jnp.mean(x * x, axis=-1, keepdims=True) + EPS)
        xh = x * r
        dy = da_ref[...]
        dg_ref[...] += jnp.sum(dy * xh, axis=0, keepdims=True)
        dxh = dy * g_ref[...]
        dh_ref[...] = dres_ref[...] + r * (dxh - xh * jnp.mean(dxh * xh, axis=-1, keepdims=True))

    dh, dg = _rowcall(body, name, s, tm, [_rb(tm, d), _rb(tm, d), _fb((1, d)), _rb(tm, d)],
                      (da, h, g.reshape(1, d), dres), [_rb(tm, d), _fb((1, d))], [_sds((s, d)), _sds((1, d))])
    return dh, dg.reshape(d)


def _loss_head(h, target, g, name, tm=256):
    s, d = h.shape

    def body(h_ref, t_ref, g_ref, dh_ref, loss_ref, dg_ref):
        @pl.when(pl.program_id(0) == 0)
        def _():
            dg_ref[...] = jnp.zeros_like(dg_ref)
            loss_ref[...] = jnp.zeros_like(loss_ref)

        x = h_ref[...]
        r = lax.rsqrt(jnp.mean(x * x, axis=-1, keepdims=True) + EPS)
        xh = x * r
        gg = g_ref[...]
        err = xh * gg - t_ref[...]
        loss_ref[...] += jnp.sum(err * err) * (0.5 / d)
        dy = err * (1.0 / d)
        dg_ref[...] += jnp.sum(dy * xh, axis=0, keepdims=True)
        dxh = dy * gg
        dh_ref[...] = r * (dxh - xh * jnp.mean(dxh * xh, axis=-1, keepdims=True))

    dh, loss, dg = _rowcall(body, name, s, tm, [_rb(tm, d), _rb(tm, d), _fb((1, d))], (h, target, g.reshape(1, d)),
                            [_rb(tm, d), _fb((1, 128)), _fb((1, d))], [_sds((s, d)), _sds((1, 128)), _sds((1, d))])
    return dh, loss[0, 0], dg.reshape(d)


_MIX_PARTS = ((0, 512), (512, 768), (768, 1024))


def _mix_fwd(ya, ysg, yss, g, name, tm=256):
    s = ya.shape[0]

    def body(a_ref, b_ref, c_ref, g_ref, o_ref):
        for ref, (lo, hi) in zip((a_ref, b_ref, c_ref), _MIX_PARTS):
            y = ref[...]
            r = lax.rsqrt(jnp.mean(y * y, axis=-1, keepdims=True) + EPS)
            o_ref[:, lo:hi] = (y * r * g_ref[:, lo:hi]).astype(BF16)

    return _rowcall(body, name, s, tm, [_rb(tm, 512), _rb(tm, 256), _rb(tm, 256), _fb((1, 1024))],
                    (ya, ysg, yss, g.reshape(1, 1024)), _rb(tm, 1024), _sds((s, 1024), BF16))


def _mix_bwd(dmix, ya, ysg, yss, g, name, tm=256):
    s = ya.shape[0]

    def body(dm_ref, a_ref, b_ref, c_ref, g_ref, da_ref, db_ref, dc_ref, dg_ref):
        @pl.when(pl.program_id(0) == 0)
        def _():
            dg_ref[...] = jnp.zeros_like(dg_ref)

        for ref, dref, (lo, hi) in zip((a_ref, b_ref, c_ref), (da_ref, db_ref, dc_ref), _MIX_PARTS):
            y = ref[...]
            r = lax.rsqrt(jnp.mean(y * y, axis=-1, keepdims=True) + EPS)
            xh = y * r
            dm = dm_ref[:, lo:hi]
            dg_ref[:, lo:hi] += jnp.sum(dm * xh, axis=0, keepdims=True)
            dxh = dm * g_ref[:, lo:hi]
            dref[...] = r * (dxh - xh * jnp.mean(dxh * xh, axis=-1, keepdims=True))

    da, db, dc, dg = _rowcall(
        body, name, s, tm, [_rb(tm, 1024), _rb(tm, 512), _rb(tm, 256), _rb(tm, 256), _fb((1, 1024))],
        (dmix, ya, ysg, yss, g.reshape(1, 1024)),
        [_rb(tm, 512), _rb(tm, 256), _rb(tm, 256), _fb((1, 1024))],
        [_sds((s, 512)), _sds((s, 256)), _sds((s, 256)), _sds((1, 1024))])
    return da, db, dc, dg.reshape(1024)


def _ssm_post_fwd(yc, z, d, gw, gb, name, tm=256):
    s = yc.shape[0]

    def body(yc_ref, u_ref, d_ref, gw_ref, gb_ref, o_ref):
        y1 = yc_ref[...] + d_ref[...] * u_ref[...]
        y2 = _gelu(y1)
        gl = jnp.dot(y2.astype(BF16), gw_ref[...], preferred_element_type=F32) + gb_ref[...]
        o_ref[...] = y2 * _sigmoid(gl)

    return _rowcall(body, name, s, tm, [_rb(tm, 256), _rb(tm, 256, 8), _fb((1, 256)), _fb((256, 256)), _fb((1, 256))],
                    (yc, z, d.reshape(1, 256), gw, gb.reshape(1, 256)), _rb(tm, 256), _sds((s, 256)))


def _ssm_post_bwd(dy, yc, z, d, gw, gb, name, tm=256):
    s = yc.shape[0]

    def body(dy_ref, yc_ref, u_ref, d_ref, gw_ref, gb_ref, dy1_ref, dgl_ref, y2_ref, dud_ref, dd_ref, dgb_ref):
        @pl.when(pl.program_id(0) == 0)
        def _():
            dd_ref[...] = jnp.zeros_like(dd_ref)
            dgb_ref[...] = jnp.zeros_like(dgb_ref)

        u = u_ref[...]
        dd = d_ref[...]
        y1 = yc_ref[...] + dd * u
        y2 = _gelu(y1)
        gw_v = gw_ref[...]
        gl = jnp.dot(y2.astype(BF16), gw_v, preferred_element_type=F32) + gb_ref[...]
        sg = _sigmoid(gl)
        dyv = dy_ref[...]
        dgl = dyv * y2 * sg * (1.0 - sg)
        dy2 = dyv * sg + lax.dot_general(dgl.astype(BF16), gw_v, _DIMS["nt"], preferred_element_type=F32)
        dy1 = dy2 * _gelu_grad(y1)
        dy1_ref[...] = dy1.astype(BF16)
        dgl_ref[...] = dgl.astype(BF16)
        y2_ref[...] = y2.astype(BF16)
        dud_ref[...] = dy1 * dd
        dd_ref[...] += jnp.sum(dy1 * u, axis=0, keepdims=True)
        dgb_ref[...] += jnp.sum(dgl, axis=0, keepdims=True)

    outs = _rowcall(
        body, name, s, tm,
        [_rb(tm, 256), _rb(tm, 256), _rb(tm, 256, 8), _fb((1, 256)), _fb((256, 256)), _fb((1, 256))],
        (dy, yc, z, d.reshape(1, 256), gw, gb.reshape(1, 256)),
        [_rb(tm, 256)] * 4 + [_fb((1, 256))] * 2,
        [_sds((s, 256), BF16)] * 3 + [_sds((s, 256))] + [_sds((1, 256))] * 2)
    dy1, dgl, y2, dud, dd, dgb = outs
    return dy1, dgl, y2, dud, dd.reshape(256), dgb.reshape(256)


def _lane_col(x, lane, j):
    return jnp.sum(jnp.where(lane == j, x, 0.0), axis=1, keepdims=True)


N_DOUBLINGS = 7


def _scan_chunk(xr, xi, kr_ref, ki_ref, reverse):
    for n in range(N_DOUBLINGS):
        shift = 128 - 2 ** n if reverse else 2 ** n
        ar, ai = kr_ref[n], ki_ref[n]
        sr = pltpu.roll(xr, shift, axis=1)
        si = pltpu.roll(xi, shift, axis=1)
        xr, xi = xr + ar * sr - ai * si, xi + ar * si + ai * sr
    return xr, xi


def _doubling_tables(pr, pi, reverse):
    lane = jnp.arange(128)[None, :]
    tabs = []
    for n in range(N_DOUBLINGS):
        k = 2 ** n
        keep = (lane < 128 - k) if reverse else (lane >= k)
        re = jnp.where(keep, pr[:, k - 1:k], 0.0)
        im = jnp.where(keep, -pi[:, k - 1:k] if reverse else pi[:, k - 1:k], 0.0)
        tabs.append(jnp.concatenate([re, im], axis=0))
    return jnp.stack(tabs)


def _scan_fwd(bu_t, pw, pk, name, cb=512):
    two_ns, s = bu_t.shape
    ns = two_ns // 2
    nrb = ns // cb
    nch = s // 128

    def body(br_ref, bi_ref, pr_ref, pi_ref, kr_ref, ki_ref, xr_ref, xi_ref, cr_ref, ci_ref):
        @pl.when(pl.program_id(1) == 0)
        def _():
            cr_ref[...] = jnp.zeros_like(cr_ref)
            ci_ref[...] = jnp.zeros_like(ci_ref)

        lane = lax.broadcasted_iota(jnp.int32, (cb, 128), 1)
        pr = pr_ref[...]
        pi = pi_ref[...]
        xr, xi = _scan_chunk(br_ref[...], bi_ref[...], kr_ref, ki_ref, False)
        cr = cr_ref[...]
        ci = ci_ref[...]
        xr = xr + pr * cr - pi * ci
        xi = xi + pr * ci + pi * cr
        xr_ref[...] = xr
        xi_ref[...] = xi
        cr_ref[...] = jnp.broadcast_to(_lane_col(xr, lane, 127), (cb, 128))
        ci_ref[...] = jnp.broadcast_to(_lane_col(xi, lane, 127), (cb, 128))

    re_spec = pl.BlockSpec((cb, 128), lambda i, c: (i, c))
    im_spec = pl.BlockSpec((cb, 128), lambda i, c: (i + nrb, c))
    pre_spec = pl.BlockSpec((cb, 128), lambda i, c: (i, 0))
    pim_spec = pl.BlockSpec((cb, 128), lambda i, c: (i + nrb, 0))
    kre_spec = pl.BlockSpec((N_DOUBLINGS, cb, 128), lambda i, c: (0, i, 0))
    kim_spec = pl.BlockSpec((N_DOUBLINGS, cb, 128), lambda i, c: (0, i + nrb, 0))
    xr, xi = pl.pallas_call(
        body, name=name, grid=(nrb, nch), in_specs=[re_spec, im_spec, pre_spec, pim_spec, kre_spec, kim_spec],
        out_specs=[re_spec, re_spec], out_shape=[_sds((ns, s)), _sds((ns, s))],
        scratch_shapes=[pltpu.VMEM((cb, 128), F32), pltpu.VMEM((cb, 128), F32)],
        compiler_params=_params(("parallel", "arbitrary")),
    )(bu_t, bu_t, pw, pw, pk, pk)
    return xr, xi


def _scan_bwd(g_t, xr, xi, pw_rev, pk_rev, name, cb=512):
    two_ns, s = g_t.shape
    ns = two_ns // 2
    nrb = ns // cb
    nch = s // 128

    def body(gr_ref, gi_ref, pr_ref, pi_ref, kr_ref, ki_ref, xr_ref, xi_ref, xpr_ref, xpi_ref,
             lr_ref, li_ref, dar_ref, dai_ref, cr_ref, ci_ref, ar_acc, ai_acc):
        c = pl.program_id(1)

        @pl.when(c == 0)
        def _():
            cr_ref[...] = jnp.zeros_like(cr_ref)
            ci_ref[...] = jnp.zeros_like(ci_ref)
            ar_acc[...] = jnp.zeros_like(ar_acc)
            ai_acc[...] = jnp.zeros_like(ai_acc)

        lane = lax.broadcasted_iota(jnp.int32, (cb, 128), 1)
        pr = pr_ref[...]
        pi = pi_ref[...]
        lr, li = _scan_chunk(gr_ref[...], gi_ref[...], kr_ref, ki_ref, True)
        cr = cr_ref[...]
        ci = ci_ref[...]
        lr = lr + pr * cr - pi * ci
        li = li + pr * ci + pi * cr
        lr_ref[...] = lr
        li_ref[...] = li
        cr_ref[...] = jnp.broadcast_to(_lane_col(lr, lane, 0), (cb, 128))
        ci_ref[...] = jnp.broadcast_to(_lane_col(li, lane, 0), (cb, 128))
        has_prev = (c < nch - 1).astype(F32)
        pvr = _lane_col(xpr_ref[...], lane, 127) * has_prev
        pvi = _lane_col(xpi_ref[...], lane, 127) * has_prev
        sxr = jnp.where(lane == 0, pvr, pltpu.roll(xr_ref[...], 1, axis=1))
        sxi = jnp.where(lane == 0, pvi, pltpu.roll(xi_ref[...], 1, axis=1))
        ar_acc[...] += lr * sxr + li * sxi
        ai_acc[...] += li * sxr - lr * sxi

        @pl.when(c == nch - 1)
        def _():
            dar_ref[...] = jnp.broadcast_to(jnp.sum(ar_acc[...], axis=1, keepdims=True), (cb, 128))
            dai_ref[...] = jnp.broadcast_to(jnp.sum(ai_acc[...], axis=1, keepdims=True), (cb, 128))

    rev = lambda c: nch - 1 - c
    re_spec = pl.BlockSpec((cb, 128), lambda i, c: (i, rev(c)))
    im_spec = pl.BlockSpec((cb, 128), lambda i, c: (i + nrb, rev(c)))
    prev_spec = pl.BlockSpec((cb, 128), lambda i, c: (i, jnp.maximum(rev(c) - 1, 0)))
    pre_spec = pl.BlockSpec((cb, 128), lambda i, c: (i, 0))
    pim_spec = pl.BlockSpec((cb, 128), lambda i, c: (i + nrb, 0))
    acc_spec = pl.BlockSpec((cb, 128), lambda i, c: (i, 0))
    kre_spec = pl.BlockSpec((N_DOUBLINGS, cb, 128), lambda i, c: (0, i, 0))
    kim_spec = pl.BlockSpec((N_DOUBLINGS, cb, 128), lambda i, c: (0, i + nrb, 0))
    lr, li, dar, dai = pl.pallas_call(
        body, name=name, grid=(nrb, nch),
        in_specs=[re_spec, im_spec, pre_spec, pim_spec, kre_spec, kim_spec, re_spec, re_spec, prev_spec, prev_spec],
        out_specs=[re_spec, re_spec, acc_spec, acc_spec],
        out_shape=[_sds((ns, s)), _sds((ns, s)), _sds((ns, 128)), _sds((ns, 128))],
        scratch_shapes=[pltpu.VMEM((cb, 128), F32)] * 4,
        compiler_params=_params(("parallel", "arbitrary")),
    )(g_t, g_t, pw_rev, pw_rev, pk_rev, pk_rev, xr, xi, xr, xi)
    return lr, li, dar[:, 0], dai[:, 0]


def _group_ids():
    return lax.broadcasted_iota(jnp.int32, (1, SGU_W), 1) // 64


def _group_mean(val, gid):
    out = jnp.zeros_like(val)
    for g in range(SGU_GROUPS):
        mg = gid == g
        out = jnp.where(mg, jnp.sum(jnp.where(mg, val, 0.0), axis=1, keepdims=True) * (1.0 / 64), out)
    return out


def _causal_w(w_ref, g):
    t = lax.broadcasted_iota(jnp.int32, (SGU_CHUNK, SGU_CHUNK), 0)
    s = lax.broadcasted_iota(jnp.int32, (SGU_CHUNK, SGU_CHUNK), 1)
    return jnp.where(t >= s, w_ref[g], 0.0).astype(BF16)


def _sgu_core(x, lng, lnb, w_ref, bexp, gid):
    zz = _gelu(x)
    u = zz[:, :SGU_W]
    v = zz[:, SGU_W:]
    vc = v - _group_mean(v, gid)
    rstd = lax.rsqrt(_group_mean(vc * vc, gid) + EPS)
    vhat = vc * rstd
    vn = vhat * lng + lnb
    vnb = vn.astype(BF16)
    mixed = bexp
    for g in range(SGU_GROUPS):
        mm = jnp.dot(_causal_w(w_ref, g), vnb, preferred_element_type=F32)
        mixed = jnp.where(gid == g, mm + bexp, mixed)
    return u, rstd, vhat, vnb, mixed


def _sgu_fwd(z, lng, lnb, w, bexp, name, tm=512):
    s = z.shape[0]

    def body(z_ref, lng_ref, lnb_ref, w_ref, b_ref, o_ref):
        gid = _group_ids()
        for j in range(tm // SGU_CHUNK):
            rows = pl.ds(j * SGU_CHUNK, SGU_CHUNK)
            u, _, _, _, mixed = _sgu_core(z_ref[rows, :], lng_ref[...], lnb_ref[...], w_ref, b_ref[...], gid)
            o_ref[rows, :] = u * mixed

    return _rowcall(body, name, s, tm,
                    [_rb(tm, 512, 3), _fb((1, 256)), _fb((1, 256)), _fb((4, 128, 128)), _fb((128, 256))],
                    (z, lng.reshape(1, 256), lnb.reshape(1, 256), w, bexp), _rb(tm, 256), _sds((s, 256)))


def _sgu_bwd(z, dy, lng, lnb, w, bexp, name, tm=512):
    s = z.shape[0]

    def body(z_ref, dy_ref, lng_ref, lnb_ref, w_ref, b_ref, dz_ref, dw_ref, db_ref, dlng_ref, dlnb_ref):
        @pl.when(pl.program_id(0) == 0)
        def _():
            dw_ref[...] = jnp.zeros_like(dw_ref)
            db_ref[...] = jnp.zeros_like(db_ref)
            dlng_ref[...] = jnp.zeros_like(dlng_ref)
            dlnb_ref[...] = jnp.zeros_like(dlnb_ref)

        gid = _group_ids()
        t = lax.broadcasted_iota(jnp.int32, (SGU_CHUNK, SGU_CHUNK), 0)
        sidx = lax.broadcasted_iota(jnp.int32, (SGU_CHUNK, SGU_CHUNK), 1)
        lng_v = lng_ref[...]
        for j in range(tm // SGU_CHUNK):
            rows = pl.ds(j * SGU_CHUNK, SGU_CHUNK)
            x = z_ref[rows, :]
            u, rstd, vhat, vnb, mixed = _sgu_core(x, lng_v, lnb_ref[...], w_ref, b_ref[...], gid)
            dyv = dy_ref[rows, :]
            dmixed = dyv * u
            du = dyv * mixed
            db_ref[...] += dmixed
            dvn = jnp.zeros_like(dmixed)
            for g in range(SGU_GROUPS):
                dmg = jnp.where(gid == g, dmixed, 0.0).astype(BF16)
                dvn = dvn + lax.dot_general(_causal_w(w_ref, g), dmg, _DIMS["tn"], preferred_element_type=F32)
                dwg = lax.dot_general(dmg, vnb, _DIMS["nt"], preferred_element_type=F32)
                dw_ref[g] += jnp.where(t >= sidx, dwg, 0.0)
            dlnb_ref[...] += jnp.sum(dvn, axis=0, keepdims=True)
            dlng_ref[...] += jnp.sum(dvn * vhat, axis=0, keepdims=True)
            dvh = dvn * lng_v
            dv = rstd * (dvh - _group_mean(dvh, gid) - vhat * _group_mean(dvh * vhat, gid))
            gg = _gelu_grad(x)
            dz_ref[rows, 0:SGU_W] = du * gg[:, :SGU_W]
            dz_ref[rows, SGU_W:2 * SGU_W] = dv * gg[:, SGU_W:]

    dz, dw, db, dlng, dlnb = _rowcall(
        body, name, s, tm,
        [_rb(tm, 512, 3), _rb(tm, 256), _fb((1, 256)), _fb((1, 256)), _fb((4, 128, 128)), _fb((128, 256))],
        (z, dy, lng.reshape(1, 256), lnb.reshape(1, 256), w, bexp),
        [_rb(tm, 512), _fb((4, 128, 128)), _fb((128, 256)), _fb((1, 256)), _fb((1, 256))],
        [_sds((s, 512)), _sds((4, 128, 128)), _sds((128, 256)), _sds((1, 256)), _sds((1, 256))])
    return dz, dw, db, dlng.reshape(256), dlnb.reshape(256)


CONV_TC = 1408
N_CT = D_FF // CONV_TC


def _row_of(block8, j):
    r = lax.broadcasted_iota(jnp.int32, block8.shape, 0)
    return jnp.sum(jnp.where(r == j, block8, 0.0), axis=0, keepdims=True)


def _shift_down(x, tail, has_prev, row):
    r7 = _row_of(tail, 7) * has_prev
    r6 = _row_of(tail, 6) * has_prev
    x1 = jnp.where(row == 0, r7, pltpu.roll(x, 1, axis=0))
    x2 = jnp.where(row == 0, r6, jnp.where(row == 1, r7, pltpu.roll(x, 2, axis=0)))
    return x1, x2


def _conv_fwd(hu, cw, cb, name, tm=256):
    s = hu.shape[0]
    n8 = tm // 8

    def body(xv_ref, xg_ref, tv_ref, tg_ref, wv_ref, wg_ref, bv_ref, bg_ref, hv_ref, hg_ref, act_ref):
        i = pl.program_id(1)
        has_prev = (i > 0).astype(F32)
        row = lax.broadcasted_iota(jnp.int32, (tm, CONV_TC), 0)

        def conv(x_ref, t_ref, w_ref, b_ref):
            x = x_ref[...]
            x1, x2 = _shift_down(x, t_ref[...], has_prev, row)
            return w_ref[0:1, :] * x2 + w_ref[1:2, :] * x1 + w_ref[2:3, :] * x + b_ref[...]

        hv = conv(xv_ref, tv_ref, wv_ref, bv_ref)
        hg = conv(xg_ref, tg_ref, wg_ref, bg_ref)
        hv_ref[...] = hv
        hg_ref[...] = hg
        act_ref[...] = (_gelu(hg) * hv).astype(BF16)

    def xs(off):
        return pl.BlockSpec((tm, CONV_TC), lambda j, i: (i, j + off))

    def ts(off):
        return pl.BlockSpec((8, CONV_TC), lambda j, i: (jnp.maximum(i * n8 - 1, 0), j + off))

    def ws(rows, off):
        return pl.BlockSpec((rows, CONV_TC), lambda j, i: (0, j + off))

    o_spec = pl.BlockSpec((tm, CONV_TC), lambda j, i: (i, j))
    return pl.pallas_call(
        body, name=name, grid=(N_CT, s // tm),
        in_specs=[xs(0), xs(N_CT), ts(0), ts(N_CT), ws(3, 0), ws(3, N_CT), ws(1, 0), ws(1, N_CT)],
        out_specs=[o_spec] * 3, out_shape=[_sds((s, D_FF)), _sds((s, D_FF)), _sds((s, D_FF), BF16)],
        compiler_params=_params(("parallel", "arbitrary")),
    )(hu, hu, hu, hu, cw, cw, cb.reshape(1, 2 * D_FF), cb.reshape(1, 2 * D_FF))


def _conv_bwd_act(dact, hv, hg, name, tm=256):
    s = dact.shape[0]

    def body(d_ref, hv_ref, hg_ref, o_ref):
        d = d_ref[...]
        g = hg_ref[...]

        @pl.when(pl.program_id(1) < N_CT)
        def _():
            o_ref[...] = d * _gelu(g)

        @pl.when(pl.program_id(1) >= N_CT)
        def _():
            o_ref[...] = d * hv_ref[...] * _gelu_grad(g)

    spec = pl.BlockSpec((tm, CONV_TC), lambda i, j: (i, j % N_CT))
    o_spec = pl.BlockSpec((tm, CONV_TC), lambda i, j: (i, j))
    return pl.pallas_call(
        body, name=name, grid=(s // tm, 2 * N_CT), in_specs=[spec] * 3, out_specs=o_spec,
        out_shape=_sds((s, 2 * D_FF)), compiler_params=_params(("parallel", "parallel")),
    )(dact, hv, hg)


def _conv_bwd(dhc, hu, cw, name, tm=256):
    s = dhc.shape[0]
    n8 = tm // 8
    off = 0
    last8 = s // 8 - 1

    def body(d_ref, dn_ref, x_ref, t_ref, w_ref, dx_ref, dw_ref, db_ref):
        i = pl.program_id(1)

        @pl.when(i == 0)
        def _():
            dw_ref[...] = jnp.zeros_like(dw_ref)
            db_ref[...] = jnp.zeros_like(db_ref)

        has_prev = (i > 0).astype(F32)
        has_next = (i < s // tm - 1).astype(F32)
        row = lax.broadcasted_iota(jnp.int32, (tm, CONV_TC), 0)
        d = d_ref[...]
        n0 = _row_of(dn_ref[...], 0) * has_next
        n1 = _row_of(dn_ref[...], 1) * has_next
        d1 = jnp.where(row == tm - 1, n0, pltpu.roll(d, tm - 1, axis=0))
        d2 = jnp.where(row == tm - 2, n0, jnp.where(row == tm - 1, n1, pltpu.roll(d, tm - 2, axis=0)))
        dx_ref[...] = (w_ref[2:3, :] * d + w_ref[1:2, :] * d1 + w_ref[0:1, :] * d2).astype(BF16)
        x = x_ref[...]
        x1, x2 = _shift_down(x, t_ref[...], has_prev, row)
        dw_ref[0:1, :] += jnp.sum(d * x2, axis=0, keepdims=True)
        dw_ref[1:2, :] += jnp.sum(d * x1, axis=0, keepdims=True)
        dw_ref[2:3, :] += jnp.sum(d * x, axis=0, keepdims=True)
        db_ref[...] += jnp.sum(d, axis=0, keepdims=True)

    d_spec = pl.BlockSpec((tm, CONV_TC), lambda j, i: (i, j))
    dn_spec = pl.BlockSpec((8, CONV_TC), lambda j, i: (jnp.minimum((i + 1) * n8, last8), j))
    x_spec = pl.BlockSpec((tm, CONV_TC), lambda j, i: (i, j + off))
    t_spec = pl.BlockSpec((8, CONV_TC), lambda j, i: (jnp.maximum(i * n8 - 1, 0), j + off))
    w_spec = pl.BlockSpec((3, CONV_TC), lambda j, i: (0, j + off))
    dw_spec = pl.BlockSpec((3, CONV_TC), lambda j, i: (0, j))
    db_spec = pl.BlockSpec((1, CONV_TC), lambda j, i: (0, j))
    return pl.pallas_call(
        body, name=name, grid=(2 * N_CT, s // tm), in_specs=[d_spec, dn_spec, x_spec, t_spec, w_spec],
        out_specs=[d_spec, dw_spec, db_spec],
        out_shape=[_sds((s, 2 * D_FF), BF16), _sds((3, 2 * D_FF)), _sds((1, 2 * D_FF))],
        compiler_params=_params(("parallel", "arbitrary")),
    )(dhc, dhc, hu, hu, cw)


def _ple_fwd(h, gp, pp, name, tm=256):
    s, d = h.shape

    def body(h_ref, g_ref, p_ref, o_ref):
        o_ref[...] = h_ref[...] + _sigmoid(g_ref[...]) * p_ref[...]

    return _rowcall(body, name, s, tm, [_rb(tm, d)] * 3, (h, gp, pp), _rb(tm, d), _sds((s, d)))


def _ple_bwd(dh, gp, pp, name, tm=256):
    s, d = dh.shape

    def body(d_ref, g_ref, p_ref, dp_ref, dg_ref):
        sg = _sigmoid(g_ref[...])
        dv = d_ref[...]
        dp_ref[...] = (dv * sg).astype(BF16)
        dg_ref[...] = (dv * p_ref[...] * sg * (1.0 - sg)).astype(BF16)

    return _rowcall(body, name, s, tm, [_rb(tm, d)] * 3, (dh, gp, pp), [_rb(tm, d)] * 2,
                    [_sds((s, d), BF16)] * 2)


SCALE = HEAD_DIM ** -0.5
ATT_ROWS = 2048


def _att_geom(s, dil):
    w = min(ATT_ROWS, s)
    p = BLK * dil
    assert w % p == 0 and s % w == 0
    return w, p, w // p


def _rows(start, dil):
    return pl.ds(start, BLK, stride=dil) if dil > 1 else pl.ds(start, BLK)


def _head_masks():
    lane = lax.broadcasted_iota(jnp.int32, (1, BLK), 1)
    return [lane < HEAD_DIM, lane >= HEAD_DIM]


def _band_valid(has_prev):
    qi = lax.broadcasted_iota(jnp.int32, (BLK, 2 * BLK), 0)
    ki = lax.broadcasted_iota(jnp.int32, (BLK, 2 * BLK), 1)
    rel = qi + BLK - ki
    band = (rel >= 0) & (rel <= BLK)
    if has_prev is True:
        return band
    return band & (has_prev | (ki >= BLK))


def _zcur(w):
    return lambda off: pl.BlockSpec((w, BLK), lambda hp, i: (i, off + hp))


def _zprev(p, nb):
    return lambda off: pl.BlockSpec((p, BLK), lambda hp, i: (jnp.maximum(i * nb - 1, 0), off + hp))


def _scur(w):
    return pl.BlockSpec((w, BLK), lambda hp, i: (i, hp))


def _bspec():
    return pl.BlockSpec((2, BLK, 2 * BLK), lambda hp, i: (hp, 0, 0))


def _attn_fwd(z, bias, state, dil, first, last, name):
    s = z.shape[0]
    w, p, nb = _att_geom(s, dil)

    def body(*refs):
        q_ref, kp_ref, kc_ref, vp_ref, vc_ref, b_ref = refs[:6]
        rest = refs[6:]
        if not first:
            m_ref, l_ref, a_ref = rest[:3]
            rest = rest[3:]
        i = pl.program_id(1)
        for r in range(dil):
            for b in range(nb):
                rows = _rows(r + p * b, dil)
                prev_rows = _rows(r + p * (b - 1), dil) if b > 0 else _rows(r, dil)
                kprev, vprev = (kc_ref, vc_ref) if b > 0 else (kp_ref, vp_ref)
                q = q_ref[rows, :]
                k = jnp.concatenate([kprev[prev_rows, :], kc_ref[rows, :]], axis=0).astype(BF16)
                v = jnp.concatenate([vprev[prev_rows, :], vc_ref[rows, :]], axis=0).astype(BF16)
                valid = _band_valid(True if b > 0 else i > 0)
                mb = lb = ob = None
                for hh, mh in enumerate(_head_masks()):
                    qh = jnp.where(mh, q, 0.0).astype(BF16)
                    sc = lax.dot_general(qh, k, _DIMS["nt"], preferred_element_type=F32) * SCALE + b_ref[hh]
                    sc = jnp.where(valid, sc, NEG_INF)
                    mx = jnp.max(sc, axis=1, keepdims=True)
                    e = jnp.exp(sc - mx)
                    den = jnp.sum(e, axis=1, keepdims=True)
                    o = jnp.dot(e.astype(BF16), v, preferred_element_type=F32)
                    if hh == 0:
                        mb = jnp.broadcast_to(mx, (BLK, BLK))
                        lb = jnp.broadcast_to(den, (BLK, BLK))
                        ob = o
                    else:
                        mb = jnp.where(mh, mx, mb)
                        lb = jnp.where(mh, den, lb)
                        ob = jnp.where(mh, o, ob)
                if first:
                    m_new, l_new, a_new = mb, lb, ob
                else:
                    m_old = m_ref[rows, :]
                    m_new = jnp.maximum(m_old, mb)
                    al = jnp.exp(m_old - m_new)
                    be = jnp.exp(mb - m_new)
                    l_new = al * l_ref[rows, :] + be * lb
                    a_new = al * a_ref[rows, :] + be * ob
                if last:
                    y_ref, lse_ref = rest
                    y_ref[rows, :] = a_new / l_new
                    lse_ref[rows, :] = m_new + jnp.log(l_new)
                else:
                    mo_ref, lo_ref, ao_ref = rest
                    mo_ref[rows, :] = m_new
                    lo_ref[rows, :] = l_new
                    ao_ref[rows, :] = a_new

    cur, prv = _zcur(w), _zprev(p, nb)
    in_specs = [cur(0), prv(4), cur(4), prv(8), cur(8), _bspec()]
    args = [z, z, z, z, z, bias]
    if not first:
        in_specs += [_scur(w)] * 3
        args += list(state)
    n_out = 2 if last else 3
    return pl.pallas_call(
        body, name=name, grid=(4, s // w), in_specs=in_specs, out_specs=[_scur(w)] * n_out,
        out_shape=[_sds((s, ATTN_W))] * n_out,
        compiler_params=_params(("parallel", "parallel")),
    )(*args)


def _row_stats(mh, dy, y, lse):
    delta = jnp.sum(jnp.where(mh, dy * y, 0.0), axis=1, keepdims=True)
    lse_h = jnp.max(jnp.where(mh, lse, NEG_INF), axis=1, keepdims=True)
    return delta, lse_h


def _attn_bwd(z, bias, dy, y, lse, prev, dil, name):
    s = z.shape[0]
    w, p, nb = _att_geom(s, dil)
    n_steps = s // w
    first = prev is None

    def body(*refs):
        q_ref, kp_ref, kc_ref, vp_ref, vc_ref, b_ref, dy_ref, y_ref, lse_ref = refs[:9]
        rest = refs[9:]
        if not first:
            dqp_ref, dkp_ref, dvp_ref = rest[:3]
            rest = rest[3:]
        dq_ref, dk_ref, dv_ref, dkx_ref, dvx_ref, db_ref = rest
        i = pl.program_id(1)

        @pl.when(i == 0)
        def _():
            db_ref[...] = jnp.zeros_like(db_ref)

        qi = lax.broadcasted_iota(jnp.int32, (BLK, BLK), 0)
        ki = lax.broadcasted_iota(jnp.int32, (BLK, BLK), 1)
        masks = _head_masks()

        def flush(rows, dk, dv):
            if not first:
                dk = dk + dkp_ref[rows, :]
                dv = dv + dvp_ref[rows, :]
            dk_ref[rows, :] = dk
            dv_ref[rows, :] = dv

        for r in range(dil):
            carry = None
            for b in range(nb):
                rows = _rows(r + p * b, dil)
                prev_rows = _rows(r + p * (b - 1), dil) if b > 0 else _rows(r, dil)
                kprev, vprev = (kc_ref, vc_ref) if b > 0 else (kp_ref, vp_ref)
                keys = [(kprev[prev_rows, :].astype(BF16), vprev[prev_rows, :].astype(BF16)),
                        (kc_ref[rows, :].astype(BF16), vc_ref[rows, :].astype(BF16))]
                valid = [(ki >= qi) if b > 0 else ((ki >= qi) & (i > 0)), qi >= ki]
                q = q_ref[rows, :]
                dy_v = dy_ref[rows, :]
                y_v = y_ref[rows, :]
                lse_v = lse_ref[rows, :]
                dq = None
                dk = [jnp.zeros((BLK, BLK), F32), jnp.zeros((BLK, BLK), F32)]
                dv = [jnp.zeros((BLK, BLK), F32), jnp.zeros((BLK, BLK), F32)]
                for hh, mh in enumerate(masks):
                    delta, lse_h = _row_stats(mh, dy_v, y_v, lse_v)
                    qh = jnp.where(mh, q, 0.0).astype(BF16)
                    dyh = jnp.where(mh, dy_v, 0.0).astype(BF16)
                    dqh = jnp.zeros((BLK, BLK), F32)
                    for half in range(2):
                        kh, vh = keys[half]
                        sc = lax.dot_general(qh, kh, _DIMS["nt"], preferred_element_type=F32) * SCALE
                        sc = sc + b_ref[hh, :, half * BLK:(half + 1) * BLK]
                        pr = jnp.where(valid[half], jnp.exp(jnp.where(valid[half], sc, NEG_INF) - lse_h), 0.0)
                        dp = lax.dot_general(dyh, vh, _DIMS["nt"], preferred_element_type=F32)
                        ds = pr * (dp - delta)
                        db_ref[hh, :, half * BLK:(half + 1) * BLK] += ds
                        dsb = ds.astype(BF16)
                        dqh = dqh + jnp.dot(dsb, kh, preferred_element_type=F32)
                        dk[half] = dk[half] + lax.dot_general(dsb, qh, _DIMS["tn"], preferred_element_type=F32)
                        dv[half] = dv[half] + lax.dot_general(pr.astype(BF16), dyh, _DIMS["tn"],
                                                              preferred_element_type=F32)
                    dq = dqh if hh == 0 else jnp.where(mh, dqh, dq)
                dq = dq * SCALE
                if not first:
                    dq = dq + dqp_ref[rows, :]
                dq_ref[rows, :] = dq
                if b > 0:
                    flush(prev_rows, carry[0] + dk[0] * SCALE, carry[1] + dv[0])
                else:
                    dkx_ref[prev_rows, :] = dk[0] * SCALE
                    dvx_ref[prev_rows, :] = dv[0]
                carry = (dk[1] * SCALE, dv[1])
            flush(_rows(r + p * (nb - 1), dil), *carry)

    cur, prv = _zcur(w), _zprev(p, nb)
    in_specs = [cur(0), prv(4), cur(4), prv(8), cur(8), _bspec()] + [_scur(w)] * 3
    args = [z, z, z, z, z, bias, dy, y, lse]
    if not first:
        in_specs += [_scur(w)] * 3
        args += list(prev)
    x_spec = pl.BlockSpec((p, BLK), lambda hp, i: (i, hp))
    return pl.pallas_call(
        body, name=name, grid=(4, n_steps), in_specs=in_specs,
        out_specs=[_scur(w)] * 3 + [x_spec] * 2 + [_bspec()],
        out_shape=[_sds((s, ATTN_W))] * 3 + [_sds((n_steps * p, ATTN_W))] * 2 + [_sds((N_HEADS, BLK, 2 * BLK))],
        compiler_params=_params(("parallel", "arbitrary")),
    )(*args)


ASM_ROWS = 512


def _assemble_dz(dq, dk, dv, extras, dzs, du, name):
    s = dq.shape[0]
    w = min(ATT_ROWS, s)
    n_steps = s // w
    per_step = w // ASM_ROWS
    assert w % ASM_ROWS == 0

    def body(*refs):
        dq_ref, dk_ref, dv_ref, dzs_ref, du_ref = refs[:5]
        x_refs = refs[5:5 + 2 * len(extras)]
        o_ref, acc_ref = refs[-2:]
        j = pl.program_id(0)
        step = j // per_step
        has_next = (step < n_steps - 1).astype(F32)
        last_of_step = ((j + 1) % per_step == 0).astype(F32)
        o_ref[:, 0:ATTN_W] = dq_ref[...].astype(BF16)
        o_ref[:, 3 * ATTN_W:3 * ATTN_W + 2 * SGU_W] = dzs_ref[...].astype(BF16)
        o_ref[:, 3 * ATTN_W + 2 * SGU_W:IN_W] = du_ref[...].astype(BF16)
        for part, (base_ref, col) in enumerate(((dk_ref, ATTN_W), (dv_ref, 2 * ATTN_W))):
            acc_ref[...] = base_ref[...]
            for n, (_, dil) in enumerate(BRANCHES):
                rows = min(BLK * dil, ASM_ROWS)
                scale = has_next if BLK * dil >= w else has_next * last_of_step
                acc_ref[ASM_ROWS - rows:, :] += x_refs[2 * n + part][...] * scale
            o_ref[:, col:col + ATTN_W] = acc_ref[...].astype(BF16)

    def x_spec(dil):
        p = BLK * dil
        rows = min(p, ASM_ROWS)
        blocks_per_step = p // rows
        total = n_steps * blocks_per_step

        def idx(j):
            step = j // per_step
            within = (j % per_step) - (per_step - blocks_per_step)
            return (jnp.clip((step + 1) * blocks_per_step + jnp.maximum(within, 0), 0, total - 1), 0)

        return pl.BlockSpec((rows, ATTN_W), idx)

    in_specs = [_rb(ASM_ROWS, ATTN_W)] * 3 + [_rb(ASM_ROWS, 2 * SGU_W), _rb(ASM_ROWS, SSM_W)]
    args = [dq, dk, dv, dzs, du]
    for (dkx, dvx), (_, dil) in zip(extras, BRANCHES):
        in_specs += [x_spec(dil)] * 2
        args += [dkx, dvx]
    return pl.pallas_call(
        body, name=name, grid=(s // ASM_ROWS,), in_specs=in_specs, out_specs=_rb(ASM_ROWS, IN_W),
        out_shape=_sds((s, IN_W), BF16), scratch_shapes=[pltpu.VMEM((ASM_ROWS, ATTN_W), F32)],
        compiler_params=_params(("parallel",)),
    )(*args)


def _t5_bucket(dist):
    max_exact = N_BUCKETS // 2
    d = np.maximum(dist, 0)
    large = max_exact + (np.log(np.maximum(d, 1) / max_exact) / np.log(REL_MAX / max_exact)
                         * (N_BUCKETS - max_exact)).astype(np.int32)
    large = np.minimum(large, N_BUCKETS - 1)
    return np.where(d < max_exact, d, large).astype(np.int32)


def _bias_tables(rel_bias):
    period = 3 * BLK
    tabs = []
    for _, dil in BRANCHES:
        onehot = np.zeros((period, N_BUCKETS), np.float32)
        d = np.arange(BLK + 1)
        onehot[d, _t5_bucket((BLK - d) * dil)] = 1.0
        f = jnp.dot(jnp.asarray(onehot), rel_bias, precision=lax.Precision.HIGHEST)
        flat = jnp.tile(f.T, (1, BLK))[:, :BLK * (period - 1)]
        tabs.append(flat.reshape(N_HEADS, BLK, period - 1)[:, :, :2 * BLK])
    return tabs


def _bucket_onehot():
    maps = []
    q = np.arange(BLK)[:, None]
    k = np.arange(2 * BLK)[None, :]
    rel = q + BLK - k
    for _, dil in BRANCHES:
        maps.append(np.where((rel >= 0) & (rel <= BLK), _t5_bucket(rel * dil), -1).reshape(-1))
    bmap = jnp.asarray(np.concatenate(maps).astype(np.int32))
    return (bmap[:, None] == jnp.arange(128, dtype=jnp.int32)[None, :]).astype(BF16)


def _block_diag(t):
    g, n, c = t.shape
    eye = jnp.eye(g, dtype=t.dtype)
    return (t[:, :, None, :] * eye[:, None, :, None]).reshape(g * n, g * c)


def _ssm_prep(a_re, a_im, log_dt, b_re, b_im, c_re, c_im):
    lam = lax.complex(a_re, a_im)
    dt = jnp.exp(log_dt)[:, None]
    a_bar = jnp.exp(lam * dt)
    b_bar = ((a_bar - 1.0) / lam)[:, :, None] * lax.complex(b_re, b_im)
    bdt = jnp.concatenate([_block_diag(jnp.real(b_bar)), _block_diag(jnp.imag(b_bar))], axis=0)
    cd = jnp.concatenate([_block_diag(jnp.transpose(c_re, (0, 2, 1))),
                          _block_diag(-jnp.transpose(c_im, (0, 2, 1)))], axis=0)
    return jnp.real(a_bar).reshape(-1), jnp.imag(a_bar).reshape(-1), bdt, cd


def _powers(ar, ai):
    pr, pi = ar[:, None], ai[:, None]
    k = 1
    while k < 128:
        lr, li = pr[:, -1:], pi[:, -1:]
        pr, pi = (jnp.concatenate([pr, pr * lr - pi * li], axis=1),
                  jnp.concatenate([pi, pr * li + pi * lr], axis=1))
        k *= 2
    return pr, pi


def _sgu_bias_expand(b):
    return jnp.repeat(b.T, 64, axis=1)


def _layer_fwd(i, h, p_i, big, small, bias_tabs):
    nm = "l%d_" % i
    sv = {"h": h}
    a1 = _rms_fwd(h, small["norm_attn_g"][i], nm + "rms_attn")
    z = _mm(a1, big["w_in"], "nt", nm + "in_proj", b_layer=i)
    st = None
    for b, (_, dil) in enumerate(BRANCHES):
        st = _attn_fwd(z, bias_tabs[b], st, dil, b == 0, b == len(BRANCHES) - 1, nm + "attn_fwd%d" % b)
    y_attn, lse = st
    bexp = _sgu_bias_expand(small["sgu_b"][i])
    y_sgu = _sgu_fwd(z, small["sgu_ln_g"][i], small["sgu_ln_b"][i], small["sgu_w"][i], bexp, nm + "sgu_fwd")
    ar, ai, bdt, cd = _ssm_prep(*[small[k][i] for k in ("ssm_a_re", "ssm_a_im", "ssm_log_dt", "ssm_b_re",
                                                         "ssm_b_im", "ssm_c_re", "ssm_c_im")])
    pr, pi = _powers(ar, ai)
    u = z[:, IN_W - SSM_W:].astype(BF16)
    bu_t = _mm(bdt, u, "nt", nm + "ssm_bu")
    xr, xi = _scan_fwd(bu_t, jnp.concatenate([pr, pi], axis=0), _doubling_tables(pr, pi, False), nm + "ssm_scan")
    yc = _mm(xr, cd[:SSM_NS], "tn", nm + "ssm_cx_re")
    yc = _mm(xi, cd[SSM_NS:], "tn", nm + "ssm_cx_im", add=yc)
    y_ssm = _ssm_post_fwd(yc, z, small["ssm_d"][i], big["ssm_glu_w"][i], small["ssm_glu_b"][i], nm + "ssm_post")
    mix = _mix_fwd(y_attn, y_sgu, y_ssm, small["branch_norm_g"][i], nm + "mix")
    h2 = _mm(mix, big["w_out"], "nn", nm + "out_proj", add=h, b_layer=i)
    a2 = _rms_fwd(h2, small["norm_ffn_g"][i], nm + "rms_ffn")
    hu = _mm(a2, big["ffn_w_up"], "nt", nm + "ffn_up", b_layer=i)
    hv, hg, act = _conv_fwd(hu, big["ffn_conv_w"][i], small["ffn_conv_b"][i], nm + "ffn_conv")
    h3 = _mm(act, big["ffn_w_down"], "nn", nm + "ffn_down", add=h2, b_layer=i)
    a3 = _rms_fwd(h3, small["norm_ple_g"][i], nm + "rms_ple")
    gp = _mm(a3, big["ple_w_gate"], "nn", nm + "ple_gate", b_layer=i)
    pp = _mm(p_i, big["ple_w_proj"], "nt", nm + "ple_proj", b_layer=i)
    h4 = _ple_fwd(h3, gp, pp, nm + "ple_add")
    sv.update(a1=a1, z=z, y_attn=y_attn, lse=lse, y_sgu=y_sgu, y_ssm=y_ssm, yc=yc, xr=xr, xi=xi, mix=mix, h2=h2,
              a2=a2, hu=hu, hv=hv, hg=hg, act=act, h3=h3, a3=a3, gp=gp, pp=pp, u=u)
    return h4, sv


def _layer_bwd(i, dh4, sv, p_i, big, small, bias_tabs):
    nm = "l%d_" % i
    g = {}
    dpp, dgp = _ple_bwd(dh4, sv["gp"], sv["pp"], nm + "ple_bwd")
    g["ple_w_proj"] = _mm(dpp, p_i, "tn", nm + "d_ple_proj", out_dtype=BF16)
    g["ple_w_gate"] = _mm(sv["a3"], dgp, "tn", nm + "d_ple_gate", out_dtype=BF16)
    da3 = _mm(dgp, big["ple_w_gate"], "nt", nm + "ple_gate_t", b_layer=i)
    dh3, g["norm_ple_g"] = _rms_bwd(da3, sv["h3"], small["norm_ple_g"][i], dh4, nm + "rms_ple_bwd")
    g["ffn_w_down"] = _mm(sv["act"], dh3, "tn", nm + "d_ffn_down", out_dtype=BF16)
    dact = _mm(dh3, big["ffn_w_down"], "nt", nm + "ffn_down_t", b_layer=i)
    dhc = _conv_bwd_act(dact, sv["hv"], sv["hg"], nm + "ffn_act_bwd")
    dhu, g["ffn_conv_w"], dcb = _conv_bwd(dhc, sv["hu"], big["ffn_conv_w"][i], nm + "ffn_conv_bwd")
    g["ffn_conv_b"] = dcb.reshape(2 * D_FF)
    g["ffn_w_up"] = _mm(dhu, sv["a2"], "tn", nm + "d_ffn_up", out_dtype=BF16)
    da2 = _mm(dhu, big["ffn_w_up"], "nn", nm + "ffn_up_t", b_layer=i)
    dh2, g["norm_ffn_g"] = _rms_bwd(da2, sv["h2"], small["norm_ffn_g"][i], dh3, nm + "rms_ffn_bwd")
    g["w_out"] = _mm(sv["mix"], dh2, "tn", nm + "d_out_proj", out_dtype=BF16)
    dmix = _mm(dh2, big["w_out"], "nt", nm + "out_proj_t", b_layer=i)
    dya, dysg, dyss, g["branch_norm_g"] = _mix_bwd(dmix, sv["y_attn"], sv["y_sgu"], sv["y_ssm"],
                                                   small["branch_norm_g"][i], nm + "mix_bwd")
    ssm_keys = ("ssm_a_re", "ssm_a_im", "ssm_log_dt", "ssm_b_re", "ssm_b_im", "ssm_c_re", "ssm_c_im")
    (ar, ai, bdt, cd), prep_vjp = jax.vjp(_ssm_prep, *[small[k][i] for k in ssm_keys])
    pr, pi = _powers(ar, ai)
    pw_rev = jnp.concatenate([pr[:, ::-1], -pi[:, ::-1]], axis=0)
    dy1, dgl, y2, dud, g["ssm_d"], g["ssm_glu_b"] = _ssm_post_bwd(
        dyss, sv["yc"], sv["z"], small["ssm_d"][i], big["ssm_glu_w"][i], small["ssm_glu_b"][i], nm + "ssm_post_bwd")
    g["ssm_glu_w"] = _mm(y2, dgl, "tn", nm + "d_ssm_glu", out_dtype=BF16)
    g_t = _mm(cd, dy1, "nt", nm + "ssm_cx_t")
    dcd = jnp.concatenate([_mm(sv["xr"], dy1, "nn", nm + "d_ssm_c_re"),
                           _mm(sv["xi"], dy1, "nn", nm + "d_ssm_c_im")], axis=0)
    lr, li, dar, dai = _scan_bwd(g_t, sv["xr"], sv["xi"], pw_rev, _doubling_tables(pr, pi, True),
                                 nm + "ssm_scan_bwd")
    u = sv["u"]
    dbdt = jnp.concatenate([_mm(lr, u, "nn", nm + "d_ssm_b_re"), _mm(li, u, "nn", nm + "d_ssm_b_im")], axis=0)
    du = _mm(lr, bdt[:SSM_NS], "tn", nm + "ssm_bu_t_re", add=dud)
    du = _mm(li, bdt[SSM_NS:], "tn", nm + "ssm_bu_t_im", add=du)
    for k, val in zip(ssm_keys, prep_vjp((dar, dai, dbdt, dcd))):
        g[k] = val
    bexp, bexp_vjp = jax.vjp(_sgu_bias_expand, small["sgu_b"][i])
    dzs, g["sgu_w"], dbexp, g["sgu_ln_g"], g["sgu_ln_b"] = _sgu_bwd(
        sv["z"], dysg, small["sgu_ln_g"][i], small["sgu_ln_b"][i], small["sgu_w"][i], bexp, nm + "sgu_bwd")
    g["sgu_b"] = bexp_vjp(dbexp)[0]
    prev = None
    dbs, extras = [], []
    for b, (_, dil) in enumerate(BRANCHES):
        dq, dk, dv, dkx, dvx, db = _attn_bwd(sv["z"], bias_tabs[b], dya, sv["y_attn"], sv["lse"], prev, dil,
                                             nm + "attn_bwd%d" % b)
        prev = (dq, dk, dv)
        extras.append((dkx, dvx))
        dbs.append(db.reshape(N_HEADS, BLK * 2 * BLK))
    dz = _assemble_dz(dq, dk, dv, extras, dzs, du, nm + "assemble_dz")
    g["w_in"] = _mm(dz, sv["a1"], "tn", nm + "d_in_proj", out_dtype=BF16)
    da1 = _mm(dz, big["w_in"], "nn", nm + "in_proj_t", b_layer=i)
    dh, g["norm_attn_g"] = _rms_bwd(da1, sv["h"], small["norm_attn_g"][i], dh2, nm + "rms_attn_bwd")
    return dh, g, jnp.concatenate(dbs, axis=1)


def _local_step(x, p, target, big, small):
    depth = p.shape[0]
    bias_tabs = _bias_tables(small["rel_bias"])
    h = x
    saved = []
    for i in range(depth):
        h, sv = _layer_fwd(i, h, p[i], big, small, bias_tabs)
        saved.append(sv)
    dh, loss, g_final = _loss_head(h, target, small["final_norm_g"], "loss_head")
    layer_grads = [None] * depth
    dbias = [None] * depth
    for i in reversed(range(depth)):
        dh, layer_grads[i], dbias[i] = _layer_bwd(i, dh, saved[i], p[i], big, small, bias_tabs)
    grads = {k: jnp.stack([layer_grads[i][k] for i in range(depth)]) for k in layer_grads[0]}
    grads["final_norm_g"] = g_final
    g_rb = _mm(sum(dbias[1:], dbias[0]), _bucket_onehot(), "nn", "d_rel_bias", tk=2048)
    grads["rel_bias"] = g_rb[:, :N_BUCKETS].T
    return loss, dh, grads


_ANY = pl.BlockSpec(memory_space=pl.ANY)
MESH_IDS = pl.DeviceIdType.MESH


def _slot(ref, axis, j):
    return ref.at[(slice(None),) * axis + (j,)]


def _all_gather(blocks, axis, name):
    nt = len(blocks)

    def body(*refs):
        x_refs, o_refs = refs[:nt], refs[nt:2 * nt]
        send_sems, recv_sems, local_sems = refs[2 * nt:]
        x, y, c = lax.axis_index("x"), lax.axis_index("y"), lax.axis_index("c")
        me, sibling = (x, y, c), (x, y, 1 - c)
        chips = [(1 - x, y), (x, 1 - y), (1 - x, 1 - y)]

        def slot(t, px, py, pc):
            return _slot(o_refs[t], axis, 4 * px + 2 * py + pc)

        def copy(t, k, blk, to, src=None):
            return pltpu.make_async_remote_copy(
                src_ref=slot(t, *blk) if src is None else src, dst_ref=slot(t, *blk),
                send_sem=send_sems.at[7 * t + k], recv_sem=recv_sems.at[7 * t + k],
                device_id=to, device_id_type=MESH_IDS)

        mine = [pltpu.make_async_copy(x_refs[t], slot(t, *me), local_sems.at[t]) for t in range(nt)]
        for cp in mine:
            cp.start()
        first = []
        for t in range(nt):
            first.append(copy(t, 0, me, sibling, src=x_refs[t]))
            first += [copy(t, 1 + j, me, (*chip, c), src=x_refs[t]) for j, chip in enumerate(chips)]
        for cp in first:
            cp.start()
        passed = []
        for t in range(nt):
            for j, chip in enumerate(chips):
                copy(t, 1 + j, (*chip, c), me).wait_recv()
                passed.append(copy(t, 4 + j, (*chip, c), sibling))
                passed[-1].start()
        for t in range(nt):
            copy(t, 0, sibling, me).wait_recv()
            for j, chip in enumerate(chips):
                copy(t, 4 + j, (*chip, 1 - c), me).wait_recv()
        for cp in first + passed:
            cp.wait_send()
        for cp in mine:
            cp.wait()

    out_shape = [jax.ShapeDtypeStruct(b.shape[:axis] + (N_DEV,) + b.shape[axis:], b.dtype) for b in blocks]
    return pl.pallas_call(
        body, name=name, out_shape=out_shape, in_specs=[_ANY] * nt, out_specs=[_ANY] * nt,
        scratch_shapes=[pltpu.SemaphoreType.DMA((7 * nt,)), pltpu.SemaphoreType.DMA((7 * nt,)),
                        pltpu.SemaphoreType.DMA((nt,))],
    )(*blocks)


def _all_to_all(blocks, name):
    nt = len(blocks)

    def body(*refs):
        x_refs, o_refs = refs[:nt], refs[nt:2 * nt]
        send_sems, recv_sems, local_sems = refs[2 * nt:]
        x, y, c = lax.axis_index("x"), lax.axis_index("y"), lax.axis_index("c")
        me = 4 * x + 2 * y + c
        mine = [pltpu.make_async_copy(_slot(x_refs[t], 1, me), o_refs[t].at[me], local_sems.at[t])
                for t in range(nt)]
        for cp in mine:
            cp.start()
        copies = []
        for k in range(1, N_DEV):
            px = 1 - x if k & 4 else x
            py = 1 - y if k & 2 else y
            pc = 1 - c if k & 1 else c
            for t in range(nt):
                cp = pltpu.make_async_remote_copy(
                    src_ref=_slot(x_refs[t], 1, 4 * px + 2 * py + pc), dst_ref=o_refs[t].at[me],
                    send_sem=send_sems.at[7 * t + k - 1], recv_sem=recv_sems.at[7 * t + k - 1],
                    device_id=(px, py, pc), device_id_type=MESH_IDS)
                cp.start()
                copies.append(cp)
        for cp in copies:
            cp.wait()
        for cp in mine:
            cp.wait()

    out_shape = [jax.ShapeDtypeStruct((N_DEV, b.shape[0]) + b.shape[2:], b.dtype) for b in blocks]
    return pl.pallas_call(
        body, name=name, out_shape=out_shape, in_specs=[_ANY] * nt, out_specs=[_ANY] * nt,
        scratch_shapes=[pltpu.SemaphoreType.DMA((7 * nt,)), pltpu.SemaphoreType.DMA((7 * nt,)),
                        pltpu.SemaphoreType.DMA((nt,))],
    )(*blocks)


def _adamw(parts, w, m, v, name, tr):
    r, c_ = w.shape

    def body(p_ref, w_ref, m_ref, v_ref, g_ref, d_ref, mo_ref, vo_ref):
        g = p_ref[0].astype(F32)
        for j in range(1, N_DEV):
            g = g + p_ref[j].astype(F32)
        m2 = ADAM_B1 * m_ref[...] + (1.0 - ADAM_B1) * g
        v2 = ADAM_B2 * v_ref[...] + (1.0 - ADAM_B2) * (g * g)
        m_hat = m2 / (1.0 - ADAM_B1 ** ADAM_STEP)
        v_hat = v2 / (1.0 - ADAM_B2 ** ADAM_STEP)
        g_ref[...] = g
        d_ref[...] = -ADAM_LR * (m_hat / (jnp.sqrt(v_hat) + ADAM_EPS) + ADAM_WD * w_ref[...])
        mo_ref[...] = m2
        vo_ref[...] = v2

    spec = pl.BlockSpec((tr, c_), lambda i: (i, 0))
    return pl.pallas_call(
        body, name=name, grid=(r // tr,),
        in_specs=[pl.BlockSpec((N_DEV, tr, c_), lambda i: (0, i, 0)), spec, spec, spec],
        out_specs=[spec] * 4, out_shape=[_sds((r, c_))] * 4, compiler_params=_params(("parallel",)),
    )(parts, w, m, v)


def _pack_rows(n_elems, align):
    rows = -(-n_elems // PACK_COLS)
    return -(-rows // align) * align


def _pack(arrs, rows, dtype=F32):
    flat = jnp.concatenate([a.reshape(-1) for a in arrs]).astype(dtype)
    return jnp.pad(flat, (0, rows * PACK_COLS - flat.shape[0])).reshape(rows, PACK_COLS)


def _unpack(pack, shapes):
    flat = pack.reshape(-1)
    out, off = [], 0
    for shp in shapes:
        size = int(np.prod(shp))
        out.append(flat[off:off + size].reshape(shp))
        off += size
    return out


def _tile_rows(rows, target, align=16):
    best = align
    for t in range(align, target + 1, align):
        if rows % t == 0:
            best = t
    return best


COMM_NAMES = ("w_in", "ssm_glu_w", "w_out", "ffn_w_up", "ffn_w_down", "ple_w_gate", "ple_w_proj")
COMM_TRANSPOSED = ("w_in", "ffn_w_up", "ple_w_proj")
SMALL_TILE_ROWS = 64
CONV_NAME = "ffn_conv_w"


def _to_comm(name, a):
    return jnp.swapaxes(a, 1, 2) if name in COMM_TRANSPOSED else a


def kernel(x, p, rel_bias, norm_attn_g, w_in, sgu_ln_g, sgu_ln_b, sgu_w, sgu_b, ssm_a_re, ssm_a_im, ssm_log_dt, ssm_b_re, ssm_b_im, ssm_c_re, ssm_c_im, ssm_d, ssm_glu_w, ssm_glu_b, branch_norm_g, w_out, norm_ffn_g, ffn_w_up, ffn_conv_w, ffn_conv_b, ffn_w_down, norm_ple_g, ple_w_gate, ple_w_proj, final_norm_g, loss_target, m_rel_bias, m_norm_attn_g, m_w_in, m_sgu_ln_g, m_sgu_ln_b, m_sgu_w, m_sgu_b, m_ssm_a_re, m_ssm_a_im, m_ssm_log_dt, m_ssm_b_re, m_ssm_b_im, m_ssm_c_re, m_ssm_c_im, m_ssm_d, m_ssm_glu_w, m_ssm_glu_b, m_branch_norm_g, m_w_out, m_norm_ffn_g, m_ffn_w_up, m_ffn_conv_w, m_ffn_conv_b, m_ffn_w_down, m_norm_ple_g, m_ple_w_gate, m_ple_w_proj, m_final_norm_g, v_rel_bias, v_norm_attn_g, v_w_in, v_sgu_ln_g, v_sgu_ln_b, v_sgu_w, v_sgu_b, v_ssm_a_re, v_ssm_a_im, v_ssm_log_dt, v_ssm_b_re, v_ssm_b_im, v_ssm_c_re, v_ssm_c_im, v_ssm_d, v_ssm_glu_w, v_ssm_glu_b, v_branch_norm_g, v_w_out, v_norm_ffn_g, v_ffn_w_up, v_ffn_conv_w, v_ffn_conv_b, v_ffn_w_down, v_norm_ple_g, v_ple_w_gate, v_ple_w_proj, v_final_norm_g):
    given = dict(locals())
    w = {n: given[n] for n in WEIGHT_NAMES}
    m = {n: given["m_" + n] for n in WEIGHT_NAMES}
    v = {n: given["v_" + n] for n in WEIGHT_NAMES}
    depth = p.shape[0]
    dev = 4 * lax.axis_index("x") + 2 * lax.axis_index("y") + lax.axis_index("c")

    wc = {n: _to_comm(n, w[n]) for n in COMM_NAMES}
    gathered = _all_gather([wc[n].astype(BF16) for n in COMM_NAMES], 1, "gather_weights")
    big = {n: t.reshape(depth, -1, t.shape[-1]) for n, t in zip(COMM_NAMES, gathered)}
    conv_local = [w[CONV_NAME], m[CONV_NAME], v[CONV_NAME]]
    conv_rows = _pack_rows(sum(int(np.prod(t.shape)) for t in conv_local), 8)
    conv_g, = _all_gather([_pack(conv_local, conv_rows)], 0, "gather_conv_taps")
    conv_parts = zip(*[_unpack(conv_g[j], [t.shape for t in conv_local]) for j in range(N_DEV)])
    conv_w, conv_m, conv_v = [jnp.concatenate(parts, axis=2) for parts in conv_parts]
    big[CONV_NAME] = conv_w
    small = {n: w[n] for n in SMALL_NAMES}

    loss, dx, grads = _local_step(x[0], p[:, 0], loss_target[0], big, small)
    loss = lax.psum(loss, ("x", "y", "c"))

    send = [grads[n].reshape((depth, N_DEV, wc[n].shape[1], wc[n].shape[2])) for n in COMM_NAMES]
    recv = _all_to_all(send, "scatter_weight_grads")
    rep_names = SMALL_NAMES + (CONV_NAME,)
    rep_w = dict(small, **{CONV_NAME: conv_w})
    rep_m = dict({n: m[n] for n in SMALL_NAMES}, **{CONV_NAME: conv_m})
    rep_v = dict({n: v[n] for n in SMALL_NAMES}, **{CONV_NAME: conv_v})
    rep_shapes = [rep_w[n].shape for n in rep_names]
    rep_rows = _pack_rows(sum(int(np.prod(s)) for s in rep_shapes), SMALL_TILE_ROWS)
    rep_parts, = _all_gather([_pack([grads[n] for n in rep_names], rep_rows)], 0, "gather_small_grads")

    out = {}
    for n, parts in zip(COMM_NAMES, recv):
        shp = wc[n].shape
        rows = shp[0] * shp[1]
        res = _adamw(parts.reshape(N_DEV, rows, shp[2]), wc[n].reshape(rows, shp[2]),
                     _to_comm(n, m[n]).reshape(rows, shp[2]), _to_comm(n, v[n]).reshape(rows, shp[2]),
                     "adamw_" + n, _tile_rows(rows, 256))
        out[n] = [_to_comm(n, t.reshape(shp)) for t in res]
    rep_out = _adamw(rep_parts, _pack([rep_w[n] for n in rep_names], rep_rows),
                     _pack([rep_m[n] for n in rep_names], rep_rows), _pack([rep_v[n] for n in rep_names], rep_rows),
                     "adamw_replicated", SMALL_TILE_ROWS)
    for n, vals in zip(rep_names, zip(*[_unpack(t, rep_shapes) for t in rep_out])):
        out[n] = list(vals)
    shard = ffn_conv_w.shape[2]
    out[CONV_NAME] = [lax.dynamic_slice_in_dim(t, dev * shard, shard, axis=2) for t in out[CONV_NAME]]
    results = [[out[n][kind] for n in WEIGHT_NAMES] for kind in range(4)]
    return (loss, dx[None], *results[0], *results[1], *results[2], *results[3])
```

```python
import math

import numpy as np
import jax
import jax.numpy as jnp
from jax import lax
from jax.experimental import pallas as pl
from jax.experimental.pallas import tpu as pltpu

F32 = jnp.float32
BF16 = jnp.bfloat16

D_MODEL = 1024
HEAD_DIM = 64
N_HEADS = 8
ATTN_W = 512
SGU_W = 256
SGU_GROUPS = 4
SGU_CHUNK = 128
SSM_W = 256
SSM_GROUPS = 16
SSM_CH = 16
SSM_STATE = 64
SSM_NS = SSM_GROUPS * SSM_STATE
IN_W = 2304
D_FF = 2816
PLE_DIM = 256
BRANCHES = ((128, 1), (512, 4), (2048, 16))
BLK = 128
N_BUCKETS = 32
REL_MAX = 2048
EPS = 1e-6
NEG_INF = -1e30
N_DEV = 8

ADAM_LR = 0.001
ADAM_B1 = 0.9
ADAM_B2 = 0.999
ADAM_EPS = 1e-08
ADAM_WD = 0.01
ADAM_STEP = 10

VMEM_LIMIT_BYTES = 56 * 1024 * 1024
GELU_C = math.sqrt(2.0 / math.pi)

SMALL_NAMES = ("rel_bias", "norm_attn_g", "sgu_ln_g", "sgu_ln_b", "sgu_w", "sgu_b", "ssm_a_re", "ssm_a_im",
               "ssm_log_dt", "ssm_b_re", "ssm_b_im", "ssm_c_re", "ssm_c_im", "ssm_d", "ssm_glu_b",
               "branch_norm_g", "norm_ffn_g", "ffn_conv_b", "norm_ple_g", "final_norm_g")
WEIGHT_NAMES = ("rel_bias", "norm_attn_g", "w_in", "sgu_ln_g", "sgu_ln_b", "sgu_w", "sgu_b", "ssm_a_re",
                "ssm_a_im", "ssm_log_dt", "ssm_b_re", "ssm_b_im", "ssm_c_re", "ssm_c_im", "ssm_d", "ssm_glu_w",
                "ssm_glu_b", "branch_norm_g", "w_out", "norm_ffn_g", "ffn_w_up", "ffn_conv_w", "ffn_conv_b",
                "ffn_w_down", "norm_ple_g", "ple_w_gate", "ple_w_proj", "final_norm_g")
PACK_COLS = 512


def _params(sem):
    return pltpu.CompilerParams(dimension_semantics=sem, vmem_limit_bytes=VMEM_LIMIT_BYTES)


def _pick(dim, target):
    if dim <= target:
        return dim
    best = None
    for t in range(128, target + 1, 128):
        if dim % t == 0:
            best = t
    return dim if best is None else best


def _gelu(x):
    return 0.5 * x * (1.0 + jnp.tanh(GELU_C * (x + 0.044715 * (x * x * x))))


def _gelu_grad(x):
    t = jnp.tanh(GELU_C * (x + 0.044715 * (x * x * x)))
    return 0.5 * (1.0 + t) + 0.5 * x * (1.0 - t * t) * (GELU_C * (1.0 + 3.0 * 0.044715 * (x * x)))


def _sigmoid(x):
    return 1.0 / (1.0 + jnp.exp(-x))


_DIMS = {"nn": (((1,), (0,)), ((), ())), "tn": (((0,), (0,)), ((), ())), "nt": (((1,), (1,)), ((), ()))}


def _mm(a, b, mode, name, add=None, out_dtype=F32, tm=1024, tn=1408, tk=1408):
    if mode == "nn":
        m, k = a.shape
        k2, n = b.shape
    elif mode == "tn":
        k, m = a.shape
        k2, n = b.shape
    else:
        m, k = a.shape
        n, k2 = b.shape
    assert k == k2, (name, a.shape, b.shape, mode)
    tm, tn, tk = _pick(m, tm), _pick(n, tn), _pick(k, tk)
    nk = k // tk
    dims = _DIMS[mode]
    has_add = add is not None

    def body(*refs):
        if has_add:
            a_ref, b_ref, add_ref, o_ref = refs[:4]
        else:
            a_ref, b_ref, o_ref = refs[:3]
        part = lax.dot_general(a_ref[...].astype(BF16), b_ref[...].astype(BF16), dims,
                               preferred_element_type=F32)

        def finish(r):
            if has_add:
                r = r + add_ref[...]
            o_ref[...] = r.astype(out_dtype)

        if nk == 1:
            finish(part)
            return
        acc_ref = refs[-1]
        kk = pl.program_id(2)

        @pl.when(kk == 0)
        def _():
            acc_ref[...] = part

        @pl.when((kk > 0) & (kk < nk - 1))
        def _():
            acc_ref[...] += part

        @pl.when(kk == nk - 1)
        def _():
            finish(acc_ref[...] + part)

    if mode == "tn":
        a_spec = pl.BlockSpec((tk, tm), lambda i, j, kk: (kk, i))
    else:
        a_spec = pl.BlockSpec((tm, tk), lambda i, j, kk: (i, kk))
    if mode == "nt":
        b_spec = pl.BlockSpec((tn, tk), lambda i, j, kk: (j, kk))
    else:
        b_spec = pl.BlockSpec((tk, tn), lambda i, j, kk: (kk, j))
    o_spec = pl.BlockSpec((tm, tn), lambda i, j, kk: (i, j))
    in_specs = [a_spec, b_spec] + ([o_spec] if has_add else [])
    args = (a, b) + ((add,) if has_add else ())
    return pl.pallas_call(
        body, name=name, grid=(m // tm, n // tn, nk),
        in_specs=in_specs, out_specs=o_spec,
        out_shape=jax.ShapeDtypeStruct((m, n), out_dtype),
        scratch_shapes=[pltpu.VMEM((tm, tn), F32)] if nk > 1 else [],
        compiler_params=_params(("parallel", "parallel", "arbitrary")),
    )(*args)


def _rb(tm, w, cb=0):
    return pl.BlockSpec((tm, w), lambda i: (i, cb))


def _fb(shape):
    nd = len(shape)
    return pl.BlockSpec(shape, lambda i: (0,) * nd)


def _rowcall(body, name, n_rows, tm, in_specs, args, out_specs, out_shapes):
    return pl.pallas_call(
        body, name=name, grid=(n_rows // tm,), in_specs=in_specs, out_specs=out_specs, out_shape=out_shapes,
        compiler_params=_params(("arbitrary",)),
    )(*args)


def _sds(shape, dtype=F32):
    return jax.ShapeDtypeStruct(shape, dtype)


def _rms_fwd(h, g, name, tm=256):
    s, d = h.shape

    def body(h_ref, g_ref, o_ref):
        x = h_ref[...]
        r = lax.rsqrt(jnp.mean(x * x, axis=-1, keepdims=True) + EPS)
        o_ref[...] = (x * r * g_ref[...]).astype(BF16)

    return _rowcall(body, name, s, tm, [_rb(tm, d), _fb((1, d))], (h, g.reshape(1, d)), _rb(tm, d),
                    _sds((s, d), BF16))


def _rms_bwd(da, h, g, dres, name, tm=256):
    s, d = h.shape

    def body(da_ref, h_ref, g_ref, dres_ref, dh_ref, dg_ref):
        @pl.when(pl.program_id(0) == 0)
        def _():
            dg_ref[...] = jnp.zeros_like(dg_ref)

        x = h_ref[...]
        r = lax.rsqrt(jnp.mean(x * x, axis=-1, keepdims=True) + EPS)
        xh = x * r
        dy = da_ref[...]
        dg_ref[...] += jnp.sum(dy * xh, axis=0, keepdims=True)
        dxh = dy * g_ref[...]
        dh_ref[...] = dres_ref[...] + r * (dxh - xh * jnp.mean(dxh * xh, axis=-1, keepdims=True))

    dh, dg = _rowcall(body, name, s, tm, [_rb(tm, d), _rb(tm, d), _fb((1, d)), _rb(tm, d)],
                      (da, h, g.reshape(1, d), dres), [_rb(tm, d), _fb((1, d))], [_sds((s, d)), _sds((1, d))])
    return dh, dg.reshape(d)


def _loss_head(h, target, g, name, tm=256):
    s, d = h.shape

    def body(h_ref, t_ref, g_ref, dh_ref, loss_ref, dg_ref):
        @pl.when(pl.program_id(0) == 0)
        def _():
            dg_ref[...] = jnp.zeros_like(dg_ref)
            loss_ref[...] = jnp.zeros_like(loss_ref)

        x = h_ref[...]
        r = lax.rsqrt(jnp.mean(x * x, axis=-1, keepdims=True) + EPS)
        xh = x * r
        gg = g_ref[...]
        err = xh * gg - t_ref[...]
        loss_ref[...] += jnp.sum(err * err) * (0.5 / d)
        dy = err * (1.0 / d)
        dg_ref[...] += jnp.sum(dy * xh, axis=0, keepdims=True)
        dxh = dy * gg
        dh_ref[...] = r * (dxh - xh * jnp.mean(dxh * xh, axis=-1, keepdims=True))

    dh, loss, dg = _rowcall(body, name, s, tm, [_rb(tm, d), _rb(tm, d), _fb((1, d))], (h, target, g.reshape(1, d)),
                            [_rb(tm, d), _fb((1, 128)), _fb((1, d))], [_sds((s, d)), _sds((1, 128)), _sds((1, d))])
    return dh, loss[0, 0], dg.reshape(d)


_MIX_PARTS = ((0, 512), (512, 768), (768, 1024))


def _mix_fwd(ya, ysg, yss, g, name, tm=256):
    s = ya.shape[0]

    def body(a_ref, b_ref, c_ref, g_ref, o_ref):
        for ref, (lo, hi) in zip((a_ref, b_ref, c_ref), _MIX_PARTS):
            y = ref[...]
            r = lax.rsqrt(jnp.mean(y * y, axis=-1, keepdims=True) + EPS)
            o_ref[:, lo:hi] = (y * r * g_ref[:, lo:hi]).astype(BF16)

    return _rowcall(body, name, s, tm, [_rb(tm, 512), _rb(tm, 256), _rb(tm, 256), _fb((1, 1024))],
                    (ya, ysg, yss, g.reshape(1, 1024)), _rb(tm, 1024), _sds((s, 1024), BF16))


def _mix_bwd(dmix, ya, ysg, yss, g, name, tm=256):
    s = ya.shape[0]

    def body(dm_ref, a_ref, b_ref, c_ref, g_ref, da_ref, db_ref, dc_ref, dg_ref):
        @pl.when(pl.program_id(0) == 0)
        def _():
            dg_ref[...] = jnp.zeros_like(dg_ref)

        for ref, dref, (lo, hi) in zip((a_ref, b_ref, c_ref), (da_ref, db_ref, dc_ref), _MIX_PARTS):
            y = ref[...]
            r = lax.rsqrt(jnp.mean(y * y, axis=-1, keepdims=True) + EPS)
            xh = y * r
            dm = dm_ref[:, lo:hi]
            dg_ref[:, lo:hi] += jnp.sum(dm * xh, axis=0, keepdims=True)
            dxh = dm * g_ref[:, lo:hi]
            dref[...] = r * (dxh - xh * jnp.mean(dxh * xh, axis=-1, keepdims=True))

    da, db, dc, dg = _rowcall(
        body, name, s, tm, [_rb(tm, 1024), _rb(tm, 512), _rb(tm, 256), _rb(tm, 256), _fb((1, 1024))],
        (dmix, ya, ysg, yss, g.reshape(1, 1024)),
        [_rb(tm, 512), _rb(tm, 256), _rb(tm, 256), _fb((1, 1024))],
        [_sds((s, 512)), _sds((s, 256)), _sds((s, 256)), _sds((1, 1024))])
    return da, db, dc, dg.reshape(1024)


def _ssm_post_fwd(yc, z, d, gw, gb, name, tm=256):
    s = yc.shape[0]

    def body(yc_ref, u_ref, d_ref, gw_ref, gb_ref, o_ref):
        y1 = yc_ref[...] + d_ref[...] * u_ref[...]
        y2 = _gelu(y1)
        gl = jnp.dot(y2.astype(BF16), gw_ref[...], preferred_element_type=F32) + gb_ref[...]
        o_ref[...] = y2 * _sigmoid(gl)

    return _rowcall(body, name, s, tm, [_rb(tm, 256), _rb(tm, 256, 8), _fb((1, 256)), _fb((256, 256)), _fb((1, 256))],
                    (yc, z, d.reshape(1, 256), gw, gb.reshape(1, 256)), _rb(tm, 256), _sds((s, 256)))


def _ssm_post_bwd(dy, yc, z, d, gw, gb, name, tm=256):
    s = yc.shape[0]

    def body(dy_ref, yc_ref, u_ref, d_ref, gw_ref, gb_ref, dy1_ref, dgl_ref, y2_ref, dud_ref, dd_ref, dgb_ref):
        @pl.when(pl.program_id(0) == 0)
        def _():
            dd_ref[...] = jnp.zeros_like(dd_ref)
            dgb_ref[...] = jnp.zeros_like(dgb_ref)

        u = u_ref[...]
        dd = d_ref[...]
        y1 = yc_ref[...] + dd * u
        y2 = _gelu(y1)
        gw_v = gw_ref[...]
        gl = jnp.dot(y2.astype(BF16), gw_v, preferred_element_type=F32) + gb_ref[...]
        sg = _sigmoid(gl)
        dyv = dy_ref[...]
        dgl = dyv * y2 * sg * (1.0 - sg)
        dy2 = dyv * sg + lax.dot_general(dgl.astype(BF16), gw_v, _DIMS["nt"], preferred_element_type=F32)
        dy1 = dy2 * _gelu_grad(y1)
        dy1_ref[...] = dy1.astype(BF16)
        dgl_ref[...] = dgl.astype(BF16)
        y2_ref[...] = y2.astype(BF16)
        dud_ref[...] = dy1 * dd
        dd_ref[...] += jnp.sum(dy1 * u, axis=0, keepdims=True)
        dgb_ref[...] += jnp.sum(dgl, axis=0, keepdims=True)

    outs = _rowcall(
        body, name, s, tm,
        [_rb(tm, 256), _rb(tm, 256), _rb(tm, 256, 8), _fb((1, 256)), _fb((256, 256)), _fb((1, 256))],
        (dy, yc, z, d.reshape(1, 256), gw, gb.reshape(1, 256)),
        [_rb(tm, 256)] * 4 + [_fb((1, 256))] * 2,
        [_sds((s, 256), BF16)] * 3 + [_sds((s, 256))] + [_sds((1, 256))] * 2)
    dy1, dgl, y2, dud, dd, dgb = outs
    return dy1, dgl, y2, dud, dd.reshape(256), dgb.reshape(256)


def _lane_col(x, lane, j):
    return jnp.sum(jnp.where(lane == j, x, 0.0), axis=1, keepdims=True)


N_DOUBLINGS = 7


def _scan_chunk(xr, xi, kr_ref, ki_ref, reverse):
    for n in range(N_DOUBLINGS):
        shift = 128 - 2 ** n if reverse else 2 ** n
        ar, ai = kr_ref[n], ki_ref[n]
        sr = pltpu.roll(xr, shift, axis=1)
        si = pltpu.roll(xi, shift, axis=1)
        xr, xi = xr + ar * sr - ai * si, xi + ar * si + ai * sr
    return xr, xi


def _doubling_tables(pr, pi, reverse):
    lane = jnp.arange(128)[None, :]
    tabs = []
    for n in range(N_DOUBLINGS):
        k = 2 ** n
        keep = (lane < 128 - k) if reverse else (lane >= k)
        re = jnp.where(keep, pr[:, k - 1:k], 0.0)
        im = jnp.where(keep, -pi[:, k - 1:k] if reverse else pi[:, k - 1:k], 0.0)
        tabs.append(jnp.concatenate([re, im], axis=0))
    return jnp.stack(tabs)


def _scan_fwd(bu_t, pw, pk, name, cb=512):
    two_ns, s = bu_t.shape
    ns = two_ns // 2
    nrb = ns // cb
    nch = s // 128

    def body(br_ref, bi_ref, pr_ref, pi_ref, kr_ref, ki_ref, xr_ref, xi_ref, cr_ref, ci_ref):
        @pl.when(pl.program_id(1) == 0)
        def _():
            cr_ref[...] = jnp.zeros_like(cr_ref)
            ci_ref[...] = jnp.zeros_like(ci_ref)

        lane = lax.broadcasted_iota(jnp.int32, (cb, 128), 1)
        pr = pr_ref[...]
        pi = pi_ref[...]
        xr, xi = _scan_chunk(br_ref[...], bi_ref[...], kr_ref, ki_ref, False)
        cr = cr_ref[...]
        ci = ci_ref[...]
        xr = xr + pr * cr - pi * ci
        xi = xi + pr * ci + pi * cr
        xr_ref[...] = xr
        xi_ref[...] = xi
        cr_ref[...] = jnp.broadcast_to(_lane_col(xr, lane, 127), (cb, 128))
        ci_ref[...] = jnp.broadcast_to(_lane_col(xi, lane, 127), (cb, 128))

    re_spec = pl.BlockSpec((cb, 128), lambda i, c: (i, c))
    im_spec = pl.BlockSpec((cb, 128), lambda i, c: (i + nrb, c))
    pre_spec = pl.BlockSpec((cb, 128), lambda i, c: (i, 0))
    pim_spec = pl.BlockSpec((cb, 128), lambda i, c: (i + nrb, 0))
    kre_spec = pl.BlockSpec((N_DOUBLINGS, cb, 128), lambda i, c: (0, i, 0))
    kim_spec = pl.BlockSpec((N_DOUBLINGS, cb, 128), lambda i, c: (0, i + nrb, 0))
    xr, xi = pl.pallas_call(
        body, name=name, grid=(nrb, nch), in_specs=[re_spec, im_spec, pre_spec, pim_spec, kre_spec, kim_spec],
        out_specs=[re_spec, re_spec], out_shape=[_sds((ns, s)), _sds((ns, s))],
        scratch_shapes=[pltpu.VMEM((cb, 128), F32), pltpu.VMEM((cb, 128), F32)],
        compiler_params=_params(("parallel", "arbitrary")),
    )(bu_t, bu_t, pw, pw, pk, pk)
    return xr, xi


def _scan_bwd(g_t, xr, xi, pw_rev, pk_rev, name, cb=512):
    two_ns, s = g_t.shape
    ns = two_ns // 2
    nrb = ns // cb
    nch = s // 128

    def body(gr_ref, gi_ref, pr_ref, pi_ref, kr_ref, ki_ref, xr_ref, xi_ref, xpr_ref, xpi_ref,
             lr_ref, li_ref, dar_ref, dai_ref, cr_ref, ci_ref, ar_acc, ai_acc):
        c = pl.program_id(1)

        @pl.when(c == 0)
        def _():
            cr_ref[...] = jnp.zeros_like(cr_ref)
            ci_ref[...] = jnp.zeros_like(ci_ref)
            ar_acc[...] = jnp.zeros_like(ar_acc)
            ai_acc[...] = jnp.zeros_like(ai_acc)

        lane = lax.broadcasted_iota(jnp.int32, (cb, 128), 1)
        pr = pr_ref[...]
        pi = pi_ref[...]
        lr, li = _scan_chunk(gr_ref[...], gi_ref[...], kr_ref, ki_ref, True)
        cr = cr_ref[...]
        ci = ci_ref[...]
        lr = lr + pr * cr - pi * ci
        li = li + pr * ci + pi * cr
        lr_ref[...] = lr
        li_ref[...] = li
        cr_ref[...] = jnp.broadcast_to(_lane_col(lr, lane, 0), (cb, 128))
        ci_ref[...] = jnp.broadcast_to(_lane_col(li, lane, 0), (cb, 128))
        has_prev = (c < nch - 1).astype(F32)
        pvr = _lane_col(xpr_ref[...], lane, 127) * has_prev
        pvi = _lane_col(xpi_ref[...], lane, 127) * has_prev
        sxr = jnp.where(lane == 0, pvr, pltpu.roll(xr_ref[...], 1, axis=1))
        sxi = jnp.where(lane == 0, pvi, pltpu.roll(xi_ref[...], 1, axis=1))
        ar_acc[...] += lr * sxr + li * sxi
        ai_acc[...] += li * sxr - lr * sxi

        @pl.when(c == nch - 1)
        def _():
            dar_ref[...] = jnp.broadcast_to(jnp.sum(ar_acc[...], axis=1, keepdims=True), (cb, 128))
            dai_ref[...] = jnp.broadcast_to(jnp.sum(ai_acc[...], axis=1, keepdims=True), (cb, 128))

    rev = lambda c: nch - 1 - c
    re_spec = pl.BlockSpec((cb, 128), lambda i, c: (i, rev(c)))
    im_spec = pl.BlockSpec((cb, 128), lambda i, c: (i + nrb, rev(c)))
    prev_spec = pl.BlockSpec((cb, 128), lambda i, c: (i, jnp.maximum(rev(c) - 1, 0)))
    pre_spec = pl.BlockSpec((cb, 128), lambda i, c: (i, 0))
    pim_spec = pl.BlockSpec((cb, 128), lambda i, c: (i + nrb, 0))
    acc_spec = pl.BlockSpec((cb, 128), lambda i, c: (i, 0))
    kre_spec = pl.BlockSpec((N_DOUBLINGS, cb, 128), lambda i, c: (0, i, 0))
    kim_spec = pl.BlockSpec((N_DOUBLINGS, cb, 128), lambda i, c: (0, i + nrb, 0))
    lr, li, dar, dai = pl.pallas_call(
        body, name=name, grid=(nrb, nch),
        in_specs=[re_spec, im_spec, pre_spec, pim_spec, kre_spec, kim_spec, re_spec, re_spec, prev_spec, prev_spec],
        out_specs=[re_spec, re_spec, acc_spec, acc_spec],
        out_shape=[_sds((ns, s)), _sds((ns, s)), _sds((ns, 128)), _sds((ns, 128))],
        scratch_shapes=[pltpu.VMEM((cb, 128), F32)] * 4,
        compiler_params=_params(("parallel", "arbitrary")),
    )(g_t, g_t, pw_rev, pw_rev, pk_rev, pk_rev, xr, xi, xr, xi)
    return lr, li, dar[:, 0], dai[:, 0]


def _group_ids():
    return lax.broadcasted_iota(jnp.int32, (1, SGU_W), 1) // 64


def _group_mean(val, gid):
    out = jnp.zeros_like(val)
    for g in range(SGU_GROUPS):
        mg = gid == g
        out = jnp.where(mg, jnp.sum(jnp.where(mg, val, 0.0), axis=1, keepdims=True) * (1.0 / 64), out)
    return out


def _causal_w(w_ref, g):
    t = lax.broadcasted_iota(jnp.int32, (SGU_CHUNK, SGU_CHUNK), 0)
    s = lax.broadcasted_iota(jnp.int32, (SGU_CHUNK, SGU_CHUNK), 1)
    return jnp.where(t >= s, w_ref[g], 0.0).astype(BF16)


def _sgu_core(x, lng, lnb, w_ref, bexp, gid):
    zz = _gelu(x)
    u = zz[:, :SGU_W]
    v = zz[:, SGU_W:]
    vc = v - _group_mean(v, gid)
    rstd = lax.rsqrt(_group_mean(vc * vc, gid) + EPS)
    vhat = vc * rstd
    vn = vhat * lng + lnb
    vnb = vn.astype(BF16)
    mixed = bexp
    for g in range(SGU_GROUPS):
        mm = jnp.dot(_causal_w(w_ref, g), vnb, preferred_element_type=F32)
        mixed = jnp.where(gid == g, mm + bexp, mixed)
    return u, rstd, vhat, vnb, mixed


def _sgu_fwd(z, lng, lnb, w, bexp, name, tm=512):
    s = z.shape[0]

    def body(z_ref, lng_ref, lnb_ref, w_ref, b_ref, o_ref):
        gid = _group_ids()
        for j in range(tm // SGU_CHUNK):
            rows = pl.ds(j * SGU_CHUNK, SGU_CHUNK)
            u, _, _, _, mixed = _sgu_core(z_ref[rows, :], lng_ref[...], lnb_ref[...], w_ref, b_ref[...], gid)
            o_ref[rows, :] = u * mixed

    return _rowcall(body, name, s, tm,
                    [_rb(tm, 512, 3), _fb((1, 256)), _fb((1, 256)), _fb((4, 128, 128)), _fb((128, 256))],
                    (z, lng.reshape(1, 256), lnb.reshape(1, 256), w, bexp), _rb(tm, 256), _sds((s, 256)))


def _sgu_bwd(z, dy, lng, lnb, w, bexp, name, tm=512):
    s = z.shape[0]

    def body(z_ref, dy_ref, lng_ref, lnb_ref, w_ref, b_ref, dz_ref, dw_ref, db_ref, dlng_ref, dlnb_ref):
        @pl.when(pl.program_id(0) == 0)
        def _():
            dw_ref[...] = jnp.zeros_like(dw_ref)
            db_ref[...] = jnp.zeros_like(db_ref)
            dlng_ref[...] = jnp.zeros_like(dlng_ref)
            dlnb_ref[...] = jnp.zeros_like(dlnb_ref)

        gid = _group_ids()
        t = lax.broadcasted_iota(jnp.int32, (SGU_CHUNK, SGU_CHUNK), 0)
        sidx = lax.broadcasted_iota(jnp.int32, (SGU_CHUNK, SGU_CHUNK), 1)
        lng_v = lng_ref[...]
        for j in range(tm // SGU_CHUNK):
            rows = pl.ds(j * SGU_CHUNK, SGU_CHUNK)
            x = z_ref[rows, :]
            u, rstd, vhat, vnb, mixed = _sgu_core(x, lng_v, lnb_ref[...], w_ref, b_ref[...], gid)
            dyv = dy_ref[rows, :]
            dmixed = dyv * u
            du = dyv * mixed
            db_ref[...] += dmixed
            dvn = jnp.zeros_like(dmixed)
            for g in range(SGU_GROUPS):
                dmg = jnp.where(gid == g, dmixed, 0.0).astype(BF16)
                dvn = dvn + lax.dot_general(_causal_w(w_ref, g), dmg, _DIMS["tn"], preferred_element_type=F32)
                dwg = lax.dot_general(dmg, vnb, _DIMS["nt"], preferred_element_type=F32)
                dw_ref[g] += jnp.where(t >= sidx, dwg, 0.0)
            dlnb_ref[...] += jnp.sum(dvn, axis=0, keepdims=True)
            dlng_ref[...] += jnp.sum(dvn * vhat, axis=0, keepdims=True)
            dvh = dvn * lng_v
            dv = rstd * (dvh - _group_mean(dvh, gid) - vhat * _group_mean(dvh * vhat, gid))
            gg = _gelu_grad(x)
            dz_ref[rows, 0:SGU_W] = du * gg[:, :SGU_W]
            dz_ref[rows, SGU_W:2 * SGU_W] = dv * gg[:, SGU_W:]

    dz, dw, db, dlng, dlnb = _rowcall(
        body, name, s, tm,
        [_rb(tm, 512, 3), _rb(tm, 256), _fb((1, 256)), _fb((1, 256)), _fb((4, 128, 128)), _fb((128, 256))],
        (z, dy, lng.reshape(1, 256), lnb.reshape(1, 256), w, bexp),
        [_rb(tm, 512), _fb((4, 128, 128)), _fb((128, 256)), _fb((1, 256)), _fb((1, 256))],
        [_sds((s, 512)), _sds((4, 128, 128)), _sds((128, 256)), _sds((1, 256)), _sds((1, 256))])
    return dz, dw, db, dlng.reshape(256), dlnb.reshape(256)


CONV_TC = 1408
N_CT = D_FF // CONV_TC


def _row_of(block8, j):
    r = lax.broadcasted_iota(jnp.int32, block8.shape, 0)
    return jnp.sum(jnp.where(r == j, block8, 0.0), axis=0, keepdims=True)


def _shift_down(x, tail, has_prev, row):
    r7 = _row_of(tail, 7) * has_prev
    r6 = _row_of(tail, 6) * has_prev
    x1 = jnp.where(row == 0, r7, pltpu.roll(x, 1, axis=0))
    x2 = jnp.where(row == 0, r6, jnp.where(row == 1, r7, pltpu.roll(x, 2, axis=0)))
    return x1, x2


def _conv_fwd(hu, cw, cb, name, tm=256):
    s = hu.shape[0]
    n8 = tm // 8

    def body(xv_ref, xg_ref, tv_ref, tg_ref, wv_ref, wg_ref, bv_ref, bg_ref, hv_ref, hg_ref, act_ref):
        i = pl.program_id(1)
        has_prev = (i > 0).astype(F32)
        row = lax.broadcasted_iota(jnp.int32, (tm, CONV_TC), 0)

        def conv(x_ref, t_ref, w_ref, b_ref):
            x = x_ref[...]
            x1, x2 = _shift_down(x, t_ref[...], has_prev, row)
            return w_ref[0:1, :] * x2 + w_ref[1:2, :] * x1 + w_ref[2:3, :] * x + b_ref[...]

        hv = conv(xv_ref, tv_ref, wv_ref, bv_ref)
        hg = conv(xg_ref, tg_ref, wg_ref, bg_ref)
        hv_ref[...] = hv.astype(BF16)
        hg_ref[...] = hg.astype(BF16)
        act_ref[...] = (_gelu(hg) * hv).astype(BF16)

    def xs(off):
        return pl.BlockSpec((tm, CONV_TC), lambda j, i: (i, j + off))

    def ts(off):
        return pl.BlockSpec((8, CONV_TC), lambda j, i: (jnp.maximum(i * n8 - 1, 0), j + off))

    def ws(rows, off):
        return pl.BlockSpec((rows, CONV_TC), lambda j, i: (0, j + off))

    o_spec = pl.BlockSpec((tm, CONV_TC), lambda j, i: (i, j))
    return pl.pallas_call(
        body, name=name, grid=(N_CT, s // tm),
        in_specs=[xs(0), xs(N_CT), ts(0), ts(N_CT), ws(3, 0), ws(3, N_CT), ws(1, 0), ws(1, N_CT)],
        out_specs=[o_spec] * 3, out_shape=[_sds((s, D_FF), BF16)] * 3,
        compiler_params=_params(("parallel", "arbitrary")),
    )(hu, hu, hu, hu, cw, cw, cb.reshape(1, 2 * D_FF), cb.reshape(1, 2 * D_FF))


HALO = 16


def _conv_bwd(dact, hv, hg, hu, cw, name, tm=256):
    s = dact.shape[0]
    n8 = tm // 8

    def body(da_ref, dan_ref, hv_ref, hvn_ref, hg_ref, hgn_ref, x_ref, t_ref, w_ref, dx_ref, dw_ref, db_ref, d_scr):
        i = pl.program_id(1)
        is_value = pl.program_id(0) < N_CT

        @pl.when(i == 0)
        def _():
            dw_ref[...] = jnp.zeros_like(dw_ref)
            db_ref[...] = jnp.zeros_like(db_ref)

        for rows, (a_ref, v_ref, g_ref) in ((pl.ds(0, tm), (da_ref, hv_ref, hg_ref)),
                                            (pl.ds(tm, HALO), (dan_ref, hvn_ref, hgn_ref))):
            @pl.when(is_value)
            def _():
                d_scr[rows, :] = a_ref[...].astype(F32) * _gelu(g_ref[...].astype(F32))

            @pl.when(jnp.logical_not(is_value))
            def _():
                d_scr[rows, :] = (a_ref[...].astype(F32) * v_ref[...].astype(F32)
                                  * _gelu_grad(g_ref[...].astype(F32)))

        has_prev = (i > 0).astype(F32)
        has_next = (i < s // tm - 1).astype(F32)
        row = lax.broadcasted_iota(jnp.int32, (tm, CONV_TC), 0)
        d = d_scr[0:tm, :]
        nxt = d_scr[tm:tm + HALO, :]
        n0 = _row_of(nxt, 0) * has_next
        n1 = _row_of(nxt, 1) * has_next
        d1 = jnp.where(row == tm - 1, n0, pltpu.roll(d, tm - 1, axis=0))
        d2 = jnp.where(row == tm - 2, n0, jnp.where(row == tm - 1, n1, pltpu.roll(d, tm - 2, axis=0)))
        dx_ref[...] = (w_ref[2:3, :] * d + w_ref[1:2, :] * d1 + w_ref[0:1, :] * d2).astype(BF16)
        x = x_ref[...]
        x1, x2 = _shift_down(x, t_ref[...], has_prev, row)
        dw_ref[0:1, :] += jnp.sum(d * x2, axis=0, keepdims=True)
        dw_ref[1:2, :] += jnp.sum(d * x1, axis=0, keepdims=True)
        dw_ref[2:3, :] += jnp.sum(d * x, axis=0, keepdims=True)
        db_ref[...] += jnp.sum(d, axis=0, keepdims=True)

    a_spec = pl.BlockSpec((tm, CONV_TC), lambda j, i: (i, j % N_CT))
    an_spec = pl.BlockSpec((HALO, CONV_TC),
                           lambda j, i: (jnp.minimum((i + 1) * (tm // HALO), s // HALO - 1), j % N_CT))
    x_spec = pl.BlockSpec((tm, CONV_TC), lambda j, i: (i, j))
    t_spec = pl.BlockSpec((8, CONV_TC), lambda j, i: (jnp.maximum(i * n8 - 1, 0), j))
    w_spec = pl.BlockSpec((3, CONV_TC), lambda j, i: (0, j))
    db_spec = pl.BlockSpec((1, CONV_TC), lambda j, i: (0, j))
    return pl.pallas_call(
        body, name=name, grid=(2 * N_CT, s // tm),
        in_specs=[a_spec, an_spec, a_spec, an_spec, a_spec, an_spec, x_spec, t_spec, w_spec],
        out_specs=[x_spec, w_spec, db_spec],
        out_shape=[_sds((s, 2 * D_FF), BF16), _sds((3, 2 * D_FF)), _sds((1, 2 * D_FF))],
        scratch_shapes=[pltpu.VMEM((tm + HALO, CONV_TC), F32)],
        compiler_params=_params(("parallel", "arbitrary")),
    )(dact, dact, hv, hv, hg, hg, hu, hu, cw)


def _ple_fwd(h, gp, pp, name, tm=256):
    s, d = h.shape

    def body(h_ref, g_ref, p_ref, o_ref):
        o_ref[...] = h_ref[...] + _sigmoid(g_ref[...]) * p_ref[...]

    return _rowcall(body, name, s, tm, [_rb(tm, d)] * 3, (h, gp, pp), _rb(tm, d), _sds((s, d)))


def _ple_bwd(dh, gp, pp, name, tm=256):
    s, d = dh.shape

    def body(d_ref, g_ref, p_ref, dp_ref, dg_ref):
        sg = _sigmoid(g_ref[...])
        dv = d_ref[...]
        dp_ref[...] = (dv * sg).astype(BF16)
        dg_ref[...] = (dv * p_ref[...] * sg * (1.0 - sg)).astype(BF16)

    return _rowcall(body, name, s, tm, [_rb(tm, d)] * 3, (dh, gp, pp), [_rb(tm, d)] * 2,
                    [_sds((s, d), BF16)] * 2)


SCALE = HEAD_DIM ** -0.5
ATT_ROWS = 2048


def _att_geom(s, dil):
    w = min(ATT_ROWS, s)
    p = BLK * dil
    assert w % p == 0 and s % w == 0
    return w, p, w // p


def _rows(start, dil):
    return pl.ds(start, BLK, stride=dil) if dil > 1 else pl.ds(start, BLK)


def _head_masks():
    lane = lax.broadcasted_iota(jnp.int32, (1, BLK), 1)
    return [lane < HEAD_DIM, lane >= HEAD_DIM]


def _band_valid(has_prev):
    qi = lax.broadcasted_iota(jnp.int32, (BLK, 2 * BLK), 0)
    ki = lax.broadcasted_iota(jnp.int32, (BLK, 2 * BLK), 1)
    rel = qi + BLK - ki
    band = (rel >= 0) & (rel <= BLK)
    if has_prev is True:
        return band
    return band & (has_prev | (ki >= BLK))


def _zcur(w):
    return lambda off: pl.BlockSpec((w, BLK), lambda hp, i: (i, off + hp))


def _zprev(p, nb):
    return lambda off: pl.BlockSpec((p, BLK), lambda hp, i: (jnp.maximum(i * nb - 1, 0), off + hp))


def _scur(w):
    return pl.BlockSpec((w, BLK), lambda hp, i: (i, hp))


def _bspec():
    return pl.BlockSpec((2, BLK, 2 * BLK), lambda hp, i: (hp, 0, 0))


def _attn_fwd(z, bias, state, dil, first, last, name):
    s = z.shape[0]
    w, p, nb = _att_geom(s, dil)

    def body(*refs):
        q_ref, kp_ref, kc_ref, vp_ref, vc_ref, b_ref = refs[:6]
        rest = refs[6:]
        if not first:
            m_ref, l_ref, a_ref = rest[:3]
            rest = rest[3:]
        i = pl.program_id(1)
        for r in range(dil):
            for b in range(nb):
                rows = _rows(r + p * b, dil)
                prev_rows = _rows(r + p * (b - 1), dil) if b > 0 else _rows(r, dil)
                kprev, vprev = (kc_ref, vc_ref) if b > 0 else (kp_ref, vp_ref)
                q = q_ref[rows, :]
                k = jnp.concatenate([kprev[prev_rows, :], kc_ref[rows, :]], axis=0).astype(BF16)
                v = jnp.concatenate([vprev[prev_rows, :], vc_ref[rows, :]], axis=0).astype(BF16)
                valid = _band_valid(True if b > 0 else i > 0)
                mb = lb = ob = None
                for hh, mh in enumerate(_head_masks()):
                    qh = jnp.where(mh, q, 0.0).astype(BF16)
                    sc = lax.dot_general(qh, k, _DIMS["nt"], preferred_element_type=F32) * SCALE + b_ref[hh]
                    sc = jnp.where(valid, sc, NEG_INF)
                    mx = jnp.max(sc, axis=1, keepdims=True)
                    e = jnp.exp(sc - mx)
                    den = jnp.sum(e, axis=1, keepdims=True)
                    o = jnp.dot(e.astype(BF16), v, preferred_element_type=F32)
                    if hh == 0:
                        mb = jnp.broadcast_to(mx, (BLK, BLK))
                        lb = jnp.broadcast_to(den, (BLK, BLK))
                        ob = o
                    else:
                        mb = jnp.where(mh, mx, mb)
                        lb = jnp.where(mh, den, lb)
                        ob = jnp.where(mh, o, ob)
                if first:
                    m_new, l_new, a_new = mb, lb, ob
                else:
                    m_old = m_ref[rows, :]
                    m_new = jnp.maximum(m_old, mb)
                    al = jnp.exp(m_old - m_new)
                    be = jnp.exp(mb - m_new)
                    l_new = al * l_ref[rows, :] + be * lb
                    a_new = al * a_ref[rows, :] + be * ob
                if last:
                    y_ref, lse_ref = rest
                    y_ref[rows, :] = a_new / l_new
                    lse_ref[rows, :] = m_new + jnp.log(l_new)
                else:
                    mo_ref, lo_ref, ao_ref = rest
                    mo_ref[rows, :] = m_new
                    lo_ref[rows, :] = l_new
                    ao_ref[rows, :] = a_new

    cur, prv = _zcur(w), _zprev(p, nb)
    in_specs = [cur(0), prv(4), cur(4), prv(8), cur(8), _bspec()]
    args = [z, z, z, z, z, bias]
    if not first:
        in_specs += [_scur(w)] * 3
        args += list(state)
    n_out = 2 if last else 3
    return pl.pallas_call(
        body, name=name, grid=(4, s // w), in_specs=in_specs, out_specs=[_scur(w)] * n_out,
        out_shape=[_sds((s, ATTN_W))] * n_out,
        compiler_params=_params(("parallel", "parallel")),
    )(*args)


def _row_stats(mh, dy, y, lse):
    delta = jnp.sum(jnp.where(mh, dy * y, 0.0), axis=1, keepdims=True)
    lse_h = jnp.max(jnp.where(mh, lse, NEG_INF), axis=1, keepdims=True)
    return delta, lse_h


def _attn_bwd(z, bias, dy, y, lse, prev, dil, name):
    s = z.shape[0]
    w, p, nb = _att_geom(s, dil)
    n_steps = s // w
    first = prev is None

    def body(*refs):
        q_ref, kp_ref, kc_ref, vp_ref, vc_ref, b_ref, dy_ref, y_ref, lse_ref = refs[:9]
        rest = refs[9:]
        if not first:
            dqp_ref, dkp_ref, dvp_ref = rest[:3]
            rest = rest[3:]
        dq_ref, dk_ref, dv_ref, dkx_ref, dvx_ref, db_ref = rest
        i = pl.program_id(1)

        @pl.when(i == 0)
        def _():
            db_ref[...] = jnp.zeros_like(db_ref)

        qi = lax.broadcasted_iota(jnp.int32, (BLK, BLK), 0)
        ki = lax.broadcasted_iota(jnp.int32, (BLK, BLK), 1)
        masks = _head_masks()

        def flush(rows, dk, dv):
            if not first:
                dk = dk + dkp_ref[rows, :]
                dv = dv + dvp_ref[rows, :]
            dk_ref[rows, :] = dk
            dv_ref[rows, :] = dv

        for r in range(dil):
            carry = None
            for b in range(nb):
                rows = _rows(r + p * b, dil)
                prev_rows = _rows(r + p * (b - 1), dil) if b > 0 else _rows(r, dil)
                kprev, vprev = (kc_ref, vc_ref) if b > 0 else (kp_ref, vp_ref)
                keys = [(kprev[prev_rows, :].astype(BF16), vprev[prev_rows, :].astype(BF16)),
                        (kc_ref[rows, :].astype(BF16), vc_ref[rows, :].astype(BF16))]
                valid = [(ki >= qi) if b > 0 else ((ki >= qi) & (i > 0)), qi >= ki]
                q = q_ref[rows, :]
                dy_v = dy_ref[rows, :]
                y_v = y_ref[rows, :]
                lse_v = lse_ref[rows, :]
                dq = None
                dk = [jnp.zeros((BLK, BLK), F32), jnp.zeros((BLK, BLK), F32)]
                dv = [jnp.zeros((BLK, BLK), F32), jnp.zeros((BLK, BLK), F32)]
                for hh, mh in enumerate(masks):
                    delta, lse_h = _row_stats(mh, dy_v, y_v, lse_v)
                    qh = jnp.where(mh, q, 0.0).astype(BF16)
                    dyh = jnp.where(mh, dy_v, 0.0).astype(BF16)
                    dqh = jnp.zeros((BLK, BLK), F32)
                    for half in range(2):
                        kh, vh = keys[half]
                        sc = lax.dot_general(qh, kh, _DIMS["nt"], preferred_element_type=F32) * SCALE
                        sc = sc + b_ref[hh, :, half * BLK:(half + 1) * BLK]
                        pr = jnp.where(valid[half], jnp.exp(jnp.where(valid[half], sc, NEG_INF) - lse_h), 0.0)
                        dp = lax.dot_general(dyh, vh, _DIMS["nt"], preferred_element_type=F32)
                        ds = pr * (dp - delta)
                        db_ref[hh, :, half * BLK:(half + 1) * BLK] += ds
                        dsb = ds.astype(BF16)
                        dqh = dqh + jnp.dot(dsb, kh, preferred_element_type=F32)
                        dk[half] = dk[half] + lax.dot_general(dsb, qh, _DIMS["tn"], preferred_element_type=F32)
                        dv[half] = dv[half] + lax.dot_general(pr.astype(BF16), dyh, _DIMS["tn"],
                                                              preferred_element_type=F32)
                    dq = dqh if hh == 0 else jnp.where(mh, dqh, dq)
                dq = dq * SCALE
                if not first:
                    dq = dq + dqp_ref[rows, :]
                dq_ref[rows, :] = dq
                if b > 0:
                    flush(prev_rows, carry[0] + dk[0] * SCALE, carry[1] + dv[0])
                else:
                    dkx_ref[prev_rows, :] = dk[0] * SCALE
                    dvx_ref[prev_rows, :] = dv[0]
                carry = (dk[1] * SCALE, dv[1])
            flush(_rows(r + p * (nb - 1), dil), *carry)

    cur, prv = _zcur(w), _zprev(p, nb)
    in_specs = [cur(0), prv(4), cur(4), prv(8), cur(8), _bspec()] + [_scur(w)] * 3
    args = [z, z, z, z, z, bias, dy, y, lse]
    if not first:
        in_specs += [_scur(w)] * 3
        args += list(prev)
    x_spec = pl.BlockSpec((p, BLK), lambda hp, i: (i, hp))
    return pl.pallas_call(
        body, name=name, grid=(4, n_steps), in_specs=in_specs,
        out_specs=[_scur(w)] * 3 + [x_spec] * 2 + [_bspec()],
        out_shape=[_sds((s, ATTN_W))] * 3 + [_sds((n_steps * p, ATTN_W))] * 2 + [_sds((N_HEADS, BLK, 2 * BLK))],
        compiler_params=_params(("parallel", "arbitrary")),
    )(*args)


ASM_ROWS = 512


def _assemble_dz(dq, dk, dv, extras, dzs, du, name):
    s = dq.shape[0]
    w = min(ATT_ROWS, s)
    n_steps = s // w
    per_step = w // ASM_ROWS
    assert w % ASM_ROWS == 0

    def body(*refs):
        dq_ref, dk_ref, dv_ref, dzs_ref, du_ref = refs[:5]
        x_refs = refs[5:5 + 2 * len(extras)]
        o_ref, acc_ref = refs[-2:]
        j = pl.program_id(0)
        step = j // per_step
        has_next = (step < n_steps - 1).astype(F32)
        last_of_step = ((j + 1) % per_step == 0).astype(F32)
        o_ref[:, 0:ATTN_W] = dq_ref[...].astype(BF16)
        o_ref[:, 3 * ATTN_W:3 * ATTN_W + 2 * SGU_W] = dzs_ref[...].astype(BF16)
        o_ref[:, 3 * ATTN_W + 2 * SGU_W:IN_W] = du_ref[...].astype(BF16)
        for part, (base_ref, col) in enumerate(((dk_ref, ATTN_W), (dv_ref, 2 * ATTN_W))):
            acc_ref[...] = base_ref[...]
            for n, (_, dil) in enumerate(BRANCHES):
                rows = min(BLK * dil, ASM_ROWS)
                scale = has_next if BLK * dil >= w else has_next * last_of_step
                acc_ref[ASM_ROWS - rows:, :] += x_refs[2 * n + part][...] * scale
            o_ref[:, col:col + ATTN_W] = acc_ref[...].astype(BF16)

    def x_spec(dil):
        p = BLK * dil
        rows = min(p, ASM_ROWS)
        blocks_per_step = p // rows
        total = n_steps * blocks_per_step

        def idx(j):
            step = j // per_step
            within = (j % per_step) - (per_step - blocks_per_step)
            return (jnp.clip((step + 1) * blocks_per_step + jnp.maximum(within, 0), 0, total - 1), 0)

        return pl.BlockSpec((rows, ATTN_W), idx)

    in_specs = [_rb(ASM_ROWS, ATTN_W)] * 3 + [_rb(ASM_ROWS, 2 * SGU_W), _rb(ASM_ROWS, SSM_W)]
    args = [dq, dk, dv, dzs, du]
    for (dkx, dvx), (_, dil) in zip(extras, BRANCHES):
        in_specs += [x_spec(dil)] * 2
        args += [dkx, dvx]
    return pl.pallas_call(
        body, name=name, grid=(s // ASM_ROWS,), in_specs=in_specs, out_specs=_rb(ASM_ROWS, IN_W),
        out_shape=_sds((s, IN_W), BF16), scratch_shapes=[pltpu.VMEM((ASM_ROWS, ATTN_W), F32)],
        compiler_params=_params(("parallel",)),
    )(*args)


def _t5_bucket(dist):
    max_exact = N_BUCKETS // 2
    d = np.maximum(dist, 0)
    large = max_exact + (np.log(np.maximum(d, 1) / max_exact) / np.log(REL_MAX / max_exact)
                         * (N_BUCKETS - max_exact)).astype(np.int32)
    large = np.minimum(large, N_BUCKETS - 1)
    return np.where(d < max_exact, d, large).astype(np.int32)


def _bias_tables(rel_bias):
    period = 3 * BLK
    tabs = []
    for _, dil in BRANCHES:
        onehot = np.zeros((period, N_BUCKETS), np.float32)
        d = np.arange(BLK + 1)
        onehot[d, _t5_bucket((BLK - d) * dil)] = 1.0
        f = jnp.dot(jnp.asarray(onehot), rel_bias, precision=lax.Precision.HIGHEST)
        flat = jnp.tile(f.T, (1, BLK))[:, :BLK * (period - 1)]
        tabs.append(flat.reshape(N_HEADS, BLK, period - 1)[:, :, :2 * BLK])
    return tabs


def _bucket_onehot():
    maps = []
    q = np.arange(BLK)[:, None]
    k = np.arange(2 * BLK)[None, :]
    rel = q + BLK - k
    for _, dil in BRANCHES:
        maps.append(np.where((rel >= 0) & (rel <= BLK), _t5_bucket(rel * dil), -1).reshape(-1))
    bmap = jnp.asarray(np.concatenate(maps).astype(np.int32))
    return (bmap[:, None] == jnp.arange(128, dtype=jnp.int32)[None, :]).astype(BF16)


def _block_diag(t):
    g, n, c = t.shape
    eye = jnp.eye(g, dtype=t.dtype)
    return (t[:, :, None, :] * eye[:, None, :, None]).reshape(g * n, g * c)


def _ssm_prep(a_re, a_im, log_dt, b_re, b_im, c_re, c_im):
    lam = lax.complex(a_re, a_im)
    dt = jnp.exp(log_dt)[:, None]
    a_bar = jnp.exp(lam * dt)
    b_bar = ((a_bar - 1.0) / lam)[:, :, None] * lax.complex(b_re, b_im)
    bdt = jnp.concatenate([_block_diag(jnp.real(b_bar)), _block_diag(jnp.imag(b_bar))], axis=0)
    cd = jnp.concatenate([_block_diag(jnp.transpose(c_re, (0, 2, 1))),
                          _block_diag(-jnp.transpose(c_im, (0, 2, 1)))], axis=0)
    return jnp.real(a_bar).reshape(-1), jnp.imag(a_bar).reshape(-1), bdt, cd


def _powers(ar, ai):
    pr, pi = ar[:, None], ai[:, None]
    k = 1
    while k < 128:
        lr, li = pr[:, -1:], pi[:, -1:]
        pr, pi = (jnp.concatenate([pr, pr * lr - pi * li], axis=1),
                  jnp.concatenate([pi, pr * li + pi * lr], axis=1))
        k *= 2
    return pr, pi


def _sgu_bias_expand(b):
    return jnp.repeat(b.T, 64, axis=1)


def _layer_fwd(i, h, p_i, big, small, bias_tabs):
    nm = "l%d_" % i
    sv = {"h": h}
    a1 = _rms_fwd(h, small["norm_attn_g"][i], nm + "rms_attn")
    z = _mm(a1, big["w_in"], "nt", nm + "in_proj")
    st = None
    for b, (_, dil) in enumerate(BRANCHES):
        st = _attn_fwd(z, bias_tabs[b], st, dil, b == 0, b == len(BRANCHES) - 1, nm + "attn_fwd%d" % b)
    y_attn, lse = st
    bexp = _sgu_bias_expand(small["sgu_b"][i])
    y_sgu = _sgu_fwd(z, small["sgu_ln_g"][i], small["sgu_ln_b"][i], small["sgu_w"][i], bexp, nm + "sgu_fwd")
    ar, ai, bdt, cd = _ssm_prep(*[small[k][i] for k in ("ssm_a_re", "ssm_a_im", "ssm_log_dt", "ssm_b_re",
                                                         "ssm_b_im", "ssm_c_re", "ssm_c_im")])
    pr, pi = _powers(ar, ai)
    u = z[:, IN_W - SSM_W:].astype(BF16)
    bu_t = _mm(bdt, u, "nt", nm + "ssm_bu")
    xr, xi = _scan_fwd(bu_t, jnp.concatenate([pr, pi], axis=0), _doubling_tables(pr, pi, False), nm + "ssm_scan")
    yc = _mm(xr, cd[:SSM_NS], "tn", nm + "ssm_cx_re")
    yc = _mm(xi, cd[SSM_NS:], "tn", nm + "ssm_cx_im", add=yc)
    y_ssm = _ssm_post_fwd(yc, z, small["ssm_d"][i], big["ssm_glu_w"], small["ssm_glu_b"][i], nm + "ssm_post")
    mix = _mix_fwd(y_attn, y_sgu, y_ssm, small["branch_norm_g"][i], nm + "mix")
    h2 = _mm(mix, big["w_out"], "nn", nm + "out_proj", add=h)
    a2 = _rms_fwd(h2, small["norm_ffn_g"][i], nm + "rms_ffn")
    hu = _mm(a2, big["ffn_w_up"], "nt", nm + "ffn_up")
    hv, hg, act = _conv_fwd(hu, big["ffn_conv_w"], small["ffn_conv_b"][i], nm + "ffn_conv")
    h3 = _mm(act, big["ffn_w_down"], "nn", nm + "ffn_down", add=h2)
    a3 = _rms_fwd(h3, small["norm_ple_g"][i], nm + "rms_ple")
    gp = _mm(a3, big["ple_w_gate"], "nn", nm + "ple_gate")
    pp = _mm(p_i, big["ple_w_proj"], "nt", nm + "ple_proj")
    h4 = _ple_fwd(h3, gp, pp, nm + "ple_add")
    sv.update(a1=a1, z=z, y_attn=y_attn, lse=lse, y_sgu=y_sgu, y_ssm=y_ssm, yc=yc, xr=xr, xi=xi, mix=mix, h2=h2,
              a2=a2, hu=hu, hv=hv, hg=hg, act=act, h3=h3, a3=a3, gp=gp, pp=pp, u=u)
    return h4, sv


def _layer_bwd(i, dh4, sv, p_i, big, small, bias_tabs):
    nm = "l%d_" % i
    g = {}
    dpp, dgp = _ple_bwd(dh4, sv["gp"], sv["pp"], nm + "ple_bwd")
    g["ple_w_proj"] = _mm(dpp, p_i, "tn", nm + "d_ple_proj", out_dtype=BF16)
    g["ple_w_gate"] = _mm(sv["a3"], dgp, "tn", nm + "d_ple_gate", out_dtype=BF16)
    da3 = _mm(dgp, big["ple_w_gate"], "nt", nm + "ple_gate_t")
    dh3, g["norm_ple_g"] = _rms_bwd(da3, sv["h3"], small["norm_ple_g"][i], dh4, nm + "rms_ple_bwd")
    g["ffn_w_down"] = _mm(sv["act"], dh3, "tn", nm + "d_ffn_down", out_dtype=BF16)
    dact = _mm(dh3, big["ffn_w_down"], "nt", nm + "ffn_down_t", out_dtype=BF16)
    dhu, g["ffn_conv_w"], dcb = _conv_bwd(dact, sv["hv"], sv["hg"], sv["hu"], big["ffn_conv_w"],
                                          nm + "ffn_conv_bwd")
    g["ffn_conv_b"] = dcb.reshape(2 * D_FF)
    g["ffn_w_up"] = _mm(dhu, sv["a2"], "tn", nm + "d_ffn_up", out_dtype=BF16)
    da2 = _mm(dhu, big["ffn_w_up"], "nn", nm + "ffn_up_t")
    dh2, g["norm_ffn_g"] = _rms_bwd(da2, sv["h2"], small["norm_ffn_g"][i], dh3, nm + "rms_ffn_bwd")
    g["w_out"] = _mm(sv["mix"], dh2, "tn", nm + "d_out_proj", out_dtype=BF16)
    dmix = _mm(dh2, big["w_out"], "nt", nm + "out_proj_t")
    dya, dysg, dyss, g["branch_norm_g"] = _mix_bwd(dmix, sv["y_attn"], sv["y_sgu"], sv["y_ssm"],
                                                   small["branch_norm_g"][i], nm + "mix_bwd")
    ssm_keys = ("ssm_a_re", "ssm_a_im", "ssm_log_dt", "ssm_b_re", "ssm_b_im", "ssm_c_re", "ssm_c_im")
    (ar, ai, bdt, cd), prep_vjp = jax.vjp(_ssm_prep, *[small[k][i] for k in ssm_keys])
    pr, pi = _powers(ar, ai)
    pw_rev = jnp.concatenate([pr[:, ::-1], -pi[:, ::-1]], axis=0)
    dy1, dgl, y2, dud, g["ssm_d"], g["ssm_glu_b"] = _ssm_post_bwd(
        dyss, sv["yc"], sv["z"], small["ssm_d"][i], big["ssm_glu_w"], small["ssm_glu_b"][i], nm + "ssm_post_bwd")
    g["ssm_glu_w"] = _mm(y2, dgl, "tn", nm + "d_ssm_glu", out_dtype=BF16)
    g_t = _mm(cd, dy1, "nt", nm + "ssm_cx_t")
    dcd = jnp.concatenate([_mm(sv["xr"], dy1, "nn", nm + "d_ssm_c_re"),
                           _mm(sv["xi"], dy1, "nn", nm + "d_ssm_c_im")], axis=0)
    lr, li, dar, dai = _scan_bwd(g_t, sv["xr"], sv["xi"], pw_rev, _doubling_tables(pr, pi, True),
                                 nm + "ssm_scan_bwd")
    u = sv["u"]
    dbdt = jnp.concatenate([_mm(lr, u, "nn", nm + "d_ssm_b_re"), _mm(li, u, "nn", nm + "d_ssm_b_im")], axis=0)
    du = _mm(lr, bdt[:SSM_NS], "tn", nm + "ssm_bu_t_re", add=dud)
    du = _mm(li, bdt[SSM_NS:], "tn", nm + "ssm_bu_t_im", add=du)
    for k, val in zip(ssm_keys, prep_vjp((dar, dai, dbdt, dcd))):
        g[k] = val
    bexp, bexp_vjp = jax.vjp(_sgu_bias_expand, small["sgu_b"][i])
    dzs, g["sgu_w"], dbexp, g["sgu_ln_g"], g["sgu_ln_b"] = _sgu_bwd(
        sv["z"], dysg, small["sgu_ln_g"][i], small["sgu_ln_b"][i], small["sgu_w"][i], bexp, nm + "sgu_bwd")
    g["sgu_b"] = bexp_vjp(dbexp)[0]
    prev = None
    dbs, extras = [], []
    for b, (_, dil) in enumerate(BRANCHES):
        dq, dk, dv, dkx, dvx, db = _attn_bwd(sv["z"], bias_tabs[b], dya, sv["y_attn"], sv["lse"], prev, dil,
                                             nm + "attn_bwd%d" % b)
        prev = (dq, dk, dv)
        extras.append((dkx, dvx))
        dbs.append(db.reshape(N_HEADS, BLK * 2 * BLK))
    dz = _assemble_dz(dq, dk, dv, extras, dzs, du, nm + "assemble_dz")
    g["w_in"] = _mm(dz, sv["a1"], "tn", nm + "d_in_proj", out_dtype=BF16)
    da1 = _mm(dz, big["w_in"], "nn", nm + "in_proj_t")
    dh, g["norm_attn_g"] = _rms_bwd(da1, sv["h"], small["norm_attn_g"][i], dh2, nm + "rms_attn_bwd")
    return dh, g, jnp.concatenate(dbs, axis=1)


def _local_step(x, p, target, layer_weights, small, layer_done=None):
    depth = p.shape[0]
    bias_tabs = _bias_tables(small["rel_bias"])
    h = x
    saved, bigs = [], []
    for i in range(depth):
        bigs.append(layer_weights(i, h))
        h, sv = _layer_fwd(i, h, p[i], bigs[i], small, bias_tabs)
        saved.append(sv)
    dh, loss, g_final = _loss_head(h, target, small["final_norm_g"], "loss_head")
    layer_grads = [None] * depth
    dbias = [None] * depth
    for i in reversed(range(depth)):
        dh, layer_grads[i], dbias[i] = _layer_bwd(i, dh, saved[i], p[i], bigs[i], small, bias_tabs)
        if layer_done is not None:
            small = layer_done(i, layer_grads[i], small)
    big_grads = [{k: lg.pop(k) for k in COMM_NAMES} for lg in layer_grads]
    grads = {k: jnp.stack([layer_grads[i][k] for i in range(depth)]) for k in layer_grads[0]}
    grads["final_norm_g"] = g_final
    g_rb = _mm(sum(dbias[1:], dbias[0]), _bucket_onehot(), "nn", "d_rel_bias", tk=2048)
    grads["rel_bias"] = g_rb[:, :N_BUCKETS].T
    return loss, dh, big_grads, grads


_ANY = pl.BlockSpec(memory_space=pl.ANY)
MESH_IDS = pl.DeviceIdType.MESH


def _slot(ref, axis, j):
    return ref.at[(slice(None),) * axis + (j,)]


def _all_gather(blocks, axis, name):
    nt = len(blocks)

    def body(*refs):
        x_refs, o_refs = refs[:nt], refs[nt:2 * nt]
        send_sems, recv_sems, local_sems = refs[2 * nt:]
        x, y, c = lax.axis_index("x"), lax.axis_index("y"), lax.axis_index("c")
        me, sibling = (x, y, c), (x, y, 1 - c)
        chips = [(1 - x, y), (x, 1 - y), (1 - x, 1 - y)]

        def slot(t, px, py, pc):
            return _slot(o_refs[t], axis, 4 * px + 2 * py + pc)

        def copy(t, k, blk, to, src=None):
            return pltpu.make_async_remote_copy(
                src_ref=slot(t, *blk) if src is None else src, dst_ref=slot(t, *blk),
                send_sem=send_sems.at[7 * t + k], recv_sem=recv_sems.at[7 * t + k],
                device_id=to, device_id_type=MESH_IDS)

        mine = [pltpu.make_async_copy(x_refs[t], slot(t, *me), local_sems.at[t]) for t in range(nt)]
        for cp in mine:
            cp.start()
        first = []
        for t in range(nt):
            first.append(copy(t, 0, me, sibling, src=x_refs[t]))
            first += [copy(t, 1 + j, me, (*chip, c), src=x_refs[t]) for j, chip in enumerate(chips)]
        for cp in first:
            cp.start()
        passed = []
        for t in range(nt):
            for j, chip in enumerate(chips):
                copy(t, 1 + j, (*chip, c), me).wait_recv()
                passed.append(copy(t, 4 + j, (*chip, c), sibling))
                passed[-1].start()
        for t in range(nt):
            copy(t, 0, sibling, me).wait_recv()
            for j, chip in enumerate(chips):
                copy(t, 4 + j, (*chip, 1 - c), me).wait_recv()
        for cp in first + passed:
            cp.wait_send()
        for cp in mine:
            cp.wait()

    out_shape = [jax.ShapeDtypeStruct(b.shape[:axis] + (N_DEV,) + b.shape[axis:], b.dtype) for b in blocks]
    return pl.pallas_call(
        body, name=name, out_shape=out_shape, in_specs=[_ANY] * nt, out_specs=[_ANY] * nt,
        scratch_shapes=[pltpu.SemaphoreType.DMA((7 * nt,)), pltpu.SemaphoreType.DMA((7 * nt,)),
                        pltpu.SemaphoreType.DMA((nt,))],
    )(*blocks)


def _peer(k):
    x, y, c = lax.axis_index("x"), lax.axis_index("y"), lax.axis_index("c")
    px = 1 - x if k & 4 else x
    py = 1 - y if k & 2 else y
    pc = 1 - c if k & 1 else c
    return (px, py, pc), 4 * px + 2 * py + pc


def _all_to_all(blocks, name):
    nt = len(blocks)

    def body(*refs):
        x_refs, o_refs = refs[:nt], refs[nt:2 * nt]
        send_sems, recv_sems, local_sems = refs[2 * nt:]
        _, me = _peer(0)
        mine = [pltpu.make_async_copy(x_refs[t].at[me], o_refs[t].at[me], local_sems.at[t]) for t in range(nt)]
        for cp in mine:
            cp.start()
        copies = []
        for k in range(1, N_DEV):
            peer, idx = _peer(k)
            for t in range(nt):
                cp = pltpu.make_async_remote_copy(
                    src_ref=x_refs[t].at[idx], dst_ref=o_refs[t].at[me],
                    send_sem=send_sems.at[7 * t + k - 1], recv_sem=recv_sems.at[7 * t + k - 1],
                    device_id=peer, device_id_type=MESH_IDS)
                cp.start()
                copies.append(cp)
        for cp in copies:
            cp.wait()
        for cp in mine:
            cp.wait()

    return pl.pallas_call(
        body, name=name, out_shape=[jax.ShapeDtypeStruct(b.shape, b.dtype) for b in blocks],
        in_specs=[_ANY] * nt, out_specs=[_ANY] * nt,
        scratch_shapes=[pltpu.SemaphoreType.DMA((7 * nt,)), pltpu.SemaphoreType.DMA((7 * nt,)),
                        pltpu.SemaphoreType.DMA((nt,))],
    )(*blocks)


_HBM = pl.BlockSpec(memory_space=pltpu.HBM)
_SEM = pl.BlockSpec(memory_space=pltpu.SEMAPHORE)
_EFFECT = pltpu.SideEffectType.DATAFLOW_SIDE_EFFECTING


def _split_copy(src_ref, land_ref, send_sems, recv_sems, t, k, gather):
    peer, idx = _peer(k)
    _, me = _peer(0)
    return pltpu.make_async_remote_copy(
        src_ref=src_ref if gather else src_ref.at[idx], dst_ref=land_ref.at[me],
        send_sem=send_sems.at[7 * t + k - 1], recv_sem=recv_sems.at[7 * t + k - 1],
        device_id=peer, device_id_type=MESH_IDS)


def _exchange_start(srcs, lands, gather, name):
    nt = len(srcs)

    def body(*refs):
        src_refs, land_refs = refs[:nt], refs[nt:2 * nt]
        send_sems, recv_sems = refs[2 * nt:2 * nt + 2]
        token = refs[-1]
        for k in range(1, N_DEV):
            for t in range(nt):
                _split_copy(src_refs[t], land_refs[t], send_sems, recv_sems, t, k, gather).start()
        token[...] = jnp.zeros_like(token)

    hbm = lambda a: pltpu.HBM(a.shape, a.dtype)
    outs = pl.pallas_call(
        body, name=name,
        out_shape=(pltpu.SemaphoreType.DMA((7 * nt,)), pltpu.SemaphoreType.DMA((7 * nt,)),
                   *[hbm(a) for a in srcs], *[hbm(a) for a in lands], jax.ShapeDtypeStruct((8, 128), F32)),
        in_specs=[_HBM] * (2 * nt),
        out_specs=(_SEM, _SEM, *[_HBM] * (2 * nt), pl.BlockSpec(memory_space=pltpu.VMEM)),
        input_output_aliases={j: 2 + j for j in range(2 * nt)},
        compiler_params=pltpu.CompilerParams(has_side_effects=_EFFECT),
    )(*[pltpu.with_memory_space_constraint(a, pltpu.HBM) for a in list(srcs) + list(lands)])
    return outs[0], outs[1], outs[2:2 + nt], outs[2 + nt:2 + 2 * nt], outs[-1]


def _exchange_wait(send_sems, recv_sems, srcs, lands, after, gather, name):
    nt = len(srcs)

    def body(*refs):
        src_refs, land_refs = refs[:nt], refs[nt:2 * nt]
        send_sems, recv_sems = refs[2 * nt:2 * nt + 2]
        for k in range(1, N_DEV):
            _, idx = _peer(k)
            for t in range(nt):
                _split_copy(src_refs[t], land_refs[t], send_sems, recv_sems, t, k, gather).wait_send()
                arrival = pltpu.make_async_remote_copy(
                    src_ref=land_refs[t].at[idx], dst_ref=land_refs[t].at[idx],
                    send_sem=send_sems.at[7 * t + k - 1], recv_sem=recv_sems.at[7 * t + k - 1],
                    device_id=_peer(k)[0], device_id_type=MESH_IDS)
                arrival.wait_recv()

    hbm = lambda a: pltpu.HBM(a.shape, a.dtype)
    outs = pl.pallas_call(
        body, name=name, out_shape=tuple(hbm(a) for a in list(srcs) + list(lands)),
        in_specs=[_HBM] * (2 * nt) + [_SEM, _SEM, _ANY], out_specs=tuple([_HBM] * (2 * nt)),
        input_output_aliases={j: j for j in range(2 * nt)},
        compiler_params=pltpu.CompilerParams(has_side_effects=_EFFECT),
    )(*srcs, *lands, send_sems, recv_sems, after)
    return outs[nt:]


def _adamw(parts, w, m, v, name, tr):
    n_layers, r, c_ = w.shape
    assert len(parts) == n_layers

    def body(*refs):
        p_refs = refs[:n_layers]
        w_ref, m_ref, v_ref, g_ref, d_ref, mo_ref, vo_ref = refs[n_layers:]

        def update(p_ref):
            g = p_ref[0].astype(F32)
            for j in range(1, N_DEV):
                g = g + p_ref[j].astype(F32)
            m2 = ADAM_B1 * m_ref[...] + (1.0 - ADAM_B1) * g
            v2 = ADAM_B2 * v_ref[...] + (1.0 - ADAM_B2) * (g * g)
            m_hat = m2 / (1.0 - ADAM_B1 ** ADAM_STEP)
            v_hat = v2 / (1.0 - ADAM_B2 ** ADAM_STEP)
            g_ref[...] = g
            d_ref[...] = -ADAM_LR * (m_hat / (jnp.sqrt(v_hat) + ADAM_EPS) + ADAM_WD * w_ref[...])
            mo_ref[...] = m2
            vo_ref[...] = v2

        for layer in range(n_layers):
            pl.when(pl.program_id(0) == layer)(lambda layer=layer: update(p_refs[layer]))

    spec = pl.BlockSpec((None, tr, c_), lambda l, i: (l, i, 0))
    p_spec = pl.BlockSpec((N_DEV, tr, c_), lambda l, i: (0, i, 0))
    return pl.pallas_call(
        body, name=name, grid=(n_layers, r // tr), in_specs=[p_spec] * n_layers + [spec] * 3,
        out_specs=[spec] * 4, out_shape=[_sds((n_layers, r, c_))] * 4,
        compiler_params=_params(("parallel", "parallel")),
    )(*parts, w, m, v)


def _pack_rows(n_elems, align):
    rows = -(-n_elems // PACK_COLS)
    return -(-rows // align) * align


def _pack(arrs, rows, dtype=F32):
    flat = jnp.concatenate([a.reshape(-1) for a in arrs]).astype(dtype)
    return jnp.pad(flat, (0, rows * PACK_COLS - flat.shape[0])).reshape(rows, PACK_COLS)


def _unpack(pack, shapes):
    flat = pack.reshape(-1)
    out, off = [], 0
    for shp in shapes:
        size = int(np.prod(shp))
        out.append(flat[off:off + size].reshape(shp))
        off += size
    return out


def _tile_rows(rows, target, align=16):
    best = align
    for t in range(align, target + 1, align):
        if rows % t == 0:
            best = t
    return best


COMM_NAMES = ("w_in", "ssm_glu_w", "w_out", "ffn_w_up", "ffn_w_down", "ple_w_gate", "ple_w_proj")
COMM_TRANSPOSED = ("w_in", "ffn_w_up", "ple_w_proj")
SMALL_TILE_ROWS = 64
CONV_NAME = "ffn_conv_w"


def _to_comm(name, a):
    return jnp.swapaxes(a, 1, 2) if name in COMM_TRANSPOSED else a


def kernel(x, p, rel_bias, norm_attn_g, w_in, sgu_ln_g, sgu_ln_b, sgu_w, sgu_b, ssm_a_re, ssm_a_im, ssm_log_dt, ssm_b_re, ssm_b_im, ssm_c_re, ssm_c_im, ssm_d, ssm_glu_w, ssm_glu_b, branch_norm_g, w_out, norm_ffn_g, ffn_w_up, ffn_conv_w, ffn_conv_b, ffn_w_down, norm_ple_g, ple_w_gate, ple_w_proj, final_norm_g, loss_target, m_rel_bias, m_norm_attn_g, m_w_in, m_sgu_ln_g, m_sgu_ln_b, m_sgu_w, m_sgu_b, m_ssm_a_re, m_ssm_a_im, m_ssm_log_dt, m_ssm_b_re, m_ssm_b_im, m_ssm_c_re, m_ssm_c_im, m_ssm_d, m_ssm_glu_w, m_ssm_glu_b, m_branch_norm_g, m_w_out, m_norm_ffn_g, m_ffn_w_up, m_ffn_conv_w, m_ffn_conv_b, m_ffn_w_down, m_norm_ple_g, m_ple_w_gate, m_ple_w_proj, m_final_norm_g, v_rel_bias, v_norm_attn_g, v_w_in, v_sgu_ln_g, v_sgu_ln_b, v_sgu_w, v_sgu_b, v_ssm_a_re, v_ssm_a_im, v_ssm_log_dt, v_ssm_b_re, v_ssm_b_im, v_ssm_c_re, v_ssm_c_im, v_ssm_d, v_ssm_glu_w, v_ssm_glu_b, v_branch_norm_g, v_w_out, v_norm_ffn_g, v_ffn_w_up, v_ffn_conv_w, v_ffn_conv_b, v_ffn_w_down, v_norm_ple_g, v_ple_w_gate, v_ple_w_proj, v_final_norm_g):
    given = dict(locals())
    w = {n: given[n] for n in WEIGHT_NAMES}
    m = {n: given["m_" + n] for n in WEIGHT_NAMES}
    v = {n: given["v_" + n] for n in WEIGHT_NAMES}
    depth = p.shape[0]
    dev = 4 * lax.axis_index("x") + 2 * lax.axis_index("y") + lax.axis_index("c")

    wc = {n: _to_comm(n, w[n]) for n in COMM_NAMES}
    wb = {n: wc[n].astype(BF16) for n in COMM_NAMES}
    conv_local = [w[CONV_NAME], m[CONV_NAME], v[CONV_NAME]]
    conv_rows = _pack_rows(sum(int(np.prod(t.shape)) for t in conv_local), 8)
    conv_g, = _all_gather([_pack(conv_local, conv_rows)], 0, "gather_conv_taps")
    conv_parts = zip(*[_unpack(conv_g[j], [t.shape for t in conv_local]) for j in range(N_DEV)])
    conv_w, conv_m, conv_v = [jnp.concatenate(parts, axis=2) for parts in conv_parts]
    small = {n: w[n] for n in SMALL_NAMES}

    def whole(blocks):
        return {n: t.reshape(-1, t.shape[-1]) for n, t in zip(COMM_NAMES, blocks)}

    def own_slot(block):
        return lax.dynamic_update_slice_in_dim(jnp.zeros((N_DEV,) + block.shape, block.dtype), block[None], dev, 0)

    first = whole(_all_gather([wb[n][0] for n in COMM_NAMES], 0, "gather_weights_0"))
    in_flight = {}
    for i in range(1, depth):
        srcs = [wb[n][i] for n in COMM_NAMES]
        in_flight[i] = _exchange_start(srcs, [own_slot(s) for s in srcs], True, "gather_weights_%d_start" % i)
        small["norm_attn_g"] = small["norm_attn_g"] + in_flight[i][4][0, 0]

    def layer_weights(i, h):
        if i == 0:
            got = first
        else:
            send_sems, recv_sems, srcs, lands, _ = in_flight.pop(i)
            got = whole(_exchange_wait(send_sems, recv_sems, srcs, lands, h, True, "gather_weights_%d_wait" % i))
        return dict(got, **{CONV_NAME: conv_w[i]})

    def as_slots(g, n):
        return g.reshape((N_DEV,) + wc[n].shape[1:])

    def layer_done(i, g, small_now):
        if i == 0:
            return small_now
        srcs = [as_slots(g[n], n) for n in COMM_NAMES]
        lands = [own_slot(lax.dynamic_index_in_dim(s, dev, 0, keepdims=False)) for s in srcs]
        in_flight[i] = _exchange_start(srcs, lands, False, "scatter_weight_grads_%d_start" % i)
        return dict(small_now, norm_ple_g=small_now["norm_ple_g"] + in_flight[i][4][0, 0])

    loss, dx, big_grads, grads = _local_step(x[0], p[:, 0], loss_target[0], layer_weights, small, layer_done)
    loss = lax.psum(loss, ("x", "y", "c"))

    recv = [_all_to_all([as_slots(big_grads[0][n], n) for n in COMM_NAMES], "scatter_weight_grads_0")]
    for i in range(1, depth):
        send_sems, recv_sems, srcs, lands, _ = in_flight.pop(i)
        recv.append(_exchange_wait(send_sems, recv_sems, srcs, lands, dx, False, "scatter_weight_grads_%d_wait" % i))
    rep_names = SMALL_NAMES + (CONV_NAME,)
    rep_w = dict({n: w[n] for n in SMALL_NAMES}, **{CONV_NAME: conv_w})
    rep_m = dict({n: m[n] for n in SMALL_NAMES}, **{CONV_NAME: conv_m})
    rep_v = dict({n: v[n] for n in SMALL_NAMES}, **{CONV_NAME: conv_v})
    rep_shapes = [rep_w[n].shape for n in rep_names]
    rep_rows = _pack_rows(sum(int(np.prod(s)) for s in rep_shapes), SMALL_TILE_ROWS)
    rep_parts, = _all_gather([_pack([grads[n] for n in rep_names], rep_rows)], 0, "gather_small_grads")

    out = {}
    for t, n in enumerate(COMM_NAMES):
        res = _adamw([recv[i][t] for i in range(depth)], wc[n], _to_comm(n, m[n]), _to_comm(n, v[n]),
                     "adamw_" + n, _tile_rows(wc[n].shape[1], 256))
        out[n] = [_to_comm(n, r) for r in res]
    rep_out = _adamw([rep_parts], *[_pack([src[n] for n in rep_names], rep_rows)[None] for src in (rep_w, rep_m, rep_v)],
                     "adamw_replicated", SMALL_TILE_ROWS)
    for n, vals in zip(rep_names, zip(*[_unpack(r[0], rep_shapes) for r in rep_out])):
        out[n] = list(vals)
    shard = ffn_conv_w.shape[2]
    out[CONV_NAME] = [lax.dynamic_slice_in_dim(t, dev * shard, shard, axis=2) for t in out[CONV_NAME]]
    results = [[out[n][kind] for n in WEIGHT_NAMES] for kind in range(4)]
    return (loss, dx[None], *results[0], *results[1], *results[2], *results[3])
```

```python
import math

import numpy as np
import jax
import jax.numpy as jnp
from jax import lax
from jax.experimental import pallas as pl
from jax.experimental.pallas import tpu as pltpu

F32 = jnp.float32
BF16 = jnp.bfloat16

D_MODEL = 1024
HEAD_DIM = 64
N_HEADS = 8
ATTN_W = 512
SGU_W = 256
SGU_GROUPS = 4
SGU_CHUNK = 128
SSM_W = 256
SSM_GROUPS = 16
SSM_CH = 16
SSM_STATE = 64
SSM_NS = SSM_GROUPS * SSM_STATE
IN_W = 2304
D_FF = 2816
PLE_DIM = 256
BRANCHES = ((128, 1), (512, 4), (2048, 16))
BLK = 128
N_BUCKETS = 32
REL_MAX = 2048
EPS = 1e-6
NEG_INF = -1e30
N_DEV = 8

ADAM_LR = 0.001
ADAM_B1 = 0.9
ADAM_B2 = 0.999
ADAM_EPS = 1e-08
ADAM_WD = 0.01
ADAM_STEP = 10

VMEM_LIMIT_BYTES = 56 * 1024 * 1024
GELU_C = math.sqrt(2.0 / math.pi)

SMALL_NAMES = ("rel_bias", "norm_attn_g", "sgu_ln_g", "sgu_ln_b", "sgu_w", "sgu_b", "ssm_a_re", "ssm_a_im",
               "ssm_log_dt", "ssm_b_re", "ssm_b_im", "ssm_c_re", "ssm_c_im", "ssm_d", "ssm_glu_b",
               "branch_norm_g", "norm_ffn_g", "ffn_conv_b", "norm_ple_g", "final_norm_g")
WEIGHT_NAMES = ("rel_bias", "norm_attn_g", "w_in", "sgu_ln_g", "sgu_ln_b", "sgu_w", "sgu_b", "ssm_a_re",
                "ssm_a_im", "ssm_log_dt", "ssm_b_re", "ssm_b_im", "ssm_c_re", "ssm_c_im", "ssm_d", "ssm_glu_w",
                "ssm_glu_b", "branch_norm_g", "w_out", "norm_ffn_g", "ffn_w_up", "ffn_conv_w", "ffn_conv_b",
                "ffn_w_down", "norm_ple_g", "ple_w_gate", "ple_w_proj", "final_norm_g")
PACK_COLS = 512


def _params(sem):
    return pltpu.CompilerParams(dimension_semantics=sem, vmem_limit_bytes=VMEM_LIMIT_BYTES)


def _pick(dim, target):
    if dim <= target:
        return dim
    best = None
    for t in range(128, target + 1, 128):
        if dim % t == 0:
            best = t
    return dim if best is None else best


def _gelu(x):
    return 0.5 * x * (1.0 + jnp.tanh(GELU_C * (x + 0.044715 * (x * x * x))))


def _gelu_grad(x):
    t = jnp.tanh(GELU_C * (x + 0.044715 * (x * x * x)))
    return 0.5 * (1.0 + t) + 0.5 * x * (1.0 - t * t) * (GELU_C * (1.0 + 3.0 * 0.044715 * (x * x)))


def _sigmoid(x):
    return 1.0 / (1.0 + jnp.exp(-x))


_DIMS = {"nn": (((1,), (0,)), ((), ())), "tn": (((0,), (0,)), ((), ())), "nt": (((1,), (1,)), ((), ()))}


def _mm(a, b, mode, name, add=None, out_dtype=F32, tm=1408, tn=1408, tk=1408):
    if mode == "nn":
        m, k = a.shape
        k2, n = b.shape
    elif mode == "tn":
        k, m = a.shape
        k2, n = b.shape
    else:
        m, k = a.shape
        n, k2 = b.shape
    assert k == k2, (name, a.shape, b.shape, mode)
    tm, tn, tk = _pick(m, tm), _pick(n, tn), _pick(k, tk)
    nk = k // tk
    dims = _DIMS[mode]
    has_add = add is not None

    def body(*refs):
        if has_add:
            a_ref, b_ref, add_ref, o_ref = refs[:4]
        else:
            a_ref, b_ref, o_ref = refs[:3]
        part = lax.dot_general(a_ref[...].astype(BF16), b_ref[...].astype(BF16), dims,
                               preferred_element_type=F32)

        def finish(r):
            if has_add:
                r = r + add_ref[...]
            o_ref[...] = r.astype(out_dtype)

        if nk == 1:
            finish(part)
            return
        acc_ref = refs[-1]
        kk = pl.program_id(2)

        @pl.when(kk == 0)
        def _():
            acc_ref[...] = part

        @pl.when((kk > 0) & (kk < nk - 1))
        def _():
            acc_ref[...] += part

        @pl.when(kk == nk - 1)
        def _():
            finish(acc_ref[...] + part)

    if mode == "tn":
        a_spec = pl.BlockSpec((tk, tm), lambda i, j, kk: (kk, i))
    else:
        a_spec = pl.BlockSpec((tm, tk), lambda i, j, kk: (i, kk))
    if mode == "nt":
        b_spec = pl.BlockSpec((tn, tk), lambda i, j, kk: (j, kk))
    else:
        b_spec = pl.BlockSpec((tk, tn), lambda i, j, kk: (kk, j))
    o_spec = pl.BlockSpec((tm, tn), lambda i, j, kk: (i, j))
    in_specs = [a_spec, b_spec] + ([o_spec] if has_add else [])
    args = (a, b) + ((add,) if has_add else ())
    return pl.pallas_call(
        body, name=name, grid=(m // tm, n // tn, nk),
        in_specs=in_specs, out_specs=o_spec,
        out_shape=jax.ShapeDtypeStruct((m, n), out_dtype),
        scratch_shapes=[pltpu.VMEM((tm, tn), F32)] if nk > 1 else [],
        compiler_params=_params(("parallel", "parallel", "arbitrary")),
    )(*args)


def _rb(tm, w, cb=0):
    return pl.BlockSpec((tm, w), lambda i: (i, cb))


def _fb(shape):
    nd = len(shape)
    return pl.BlockSpec(shape, lambda i: (0,) * nd)


def _rowcall(body, name, n_rows, tm, in_specs, args, out_specs, out_shapes):
    return pl.pallas_call(
        body, name=name, grid=(n_rows // tm,), in_specs=in_specs, out_specs=out_specs, out_shape=out_shapes,
        compiler_params=_params(("arbitrary",)),
    )(*args)


def _sds(shape, dtype=F32):
    return jax.ShapeDtypeStruct(shape, dtype)


def _rms_fwd(h, g, name, tm=512):
    s, d = h.shape

    def body(h_ref, g_ref, o_ref):
        x = h_ref[...]
        r = lax.rsqrt(jnp.mean(x * x, axis=-1, keepdims=True) + EPS)
        o_ref[...] = (x * r * g_ref[...]).astype(BF16)

    return _rowcall(body, name, s, tm, [_rb(tm, d), _fb((1, d))], (h, g.reshape(1, d)), _rb(tm, d),
                    _sds((s, d), BF16))


def _rms_bwd(da, h, g, dres, name, tm=512):
    s, d = h.shape

    def body(da_ref, h_ref, g_ref, dres_ref, dh_ref, dg_ref):
        @pl.when(pl.program_id(0) == 0)
        def _():
            dg_ref[...] = jnp.zeros_like(dg_ref)

        x = h_ref[...]
        r = lax.rsqrt(jnp.mean(x * x, axis=-1, keepdims=True) + EPS)
        xh = x * r
        dy = da_ref[...]
        dg_ref[...] += jnp.sum(dy * xh, axis=0, keepdims=True)
        dxh = dy * g_ref[...]
        dh_ref[...] = dres_ref[...] + r * (dxh - xh * jnp.mean(dxh * xh, axis=-1, keepdims=True))

    dh, dg = _rowcall(body, name, s, tm, [_rb(tm, d), _rb(tm, d), _fb((1, d)), _rb(tm, d)],
                      (da, h, g.reshape(1, d), dres), [_rb(tm, d), _fb((1, d))], [_sds((s, d)), _sds((1, d))])
    return dh, dg.reshape(d)


def _loss_head(h, target, g, name, tm=512):
    s, d = h.shape

    def body(h_ref, t_ref, g_ref, dh_ref, loss_ref, dg_ref):
        @pl.when(pl.program_id(0) == 0)
        def _():
            dg_ref[...] = jnp.zeros_like(dg_ref)
            loss_ref[...] = jnp.zeros_like(loss_ref)

        x = h_ref[...]
        r = lax.rsqrt(jnp.mean(x * x, axis=-1, keepdims=True) + EPS)
        xh = x * r
        gg = g_ref[...]
        err = xh * gg - t_ref[...]
        loss_ref[...] += jnp.sum(err * err) * (0.5 / d)
        dy = err * (1.0 / d)
        dg_ref[...] += jnp.sum(dy * xh, axis=0, keepdims=True)
        dxh = dy * gg
        dh_ref[...] = r * (dxh - xh * jnp.mean(dxh * xh, axis=-1, keepdims=True))

    dh, loss, dg = _rowcall(body, name, s, tm, [_rb(tm, d), _rb(tm, d), _fb((1, d))], (h, target, g.reshape(1, d)),
                            [_rb(tm, d), _fb((1, 128)), _fb((1, d))], [_sds((s, d)), _sds((1, 128)), _sds((1, d))])
    return dh, loss[0, 0], dg.reshape(d)


_MIX_PARTS = ((0, 512), (512, 768), (768, 1024))


def _mix_fwd(ya, ysg, yss, g, name, tm=512):
    s = ya.shape[0]

    def body(a_ref, b_ref, c_ref, g_ref, o_ref):
        for ref, (lo, hi) in zip((a_ref, b_ref, c_ref), _MIX_PARTS):
            y = ref[...]
            r = lax.rsqrt(jnp.mean(y * y, axis=-1, keepdims=True) + EPS)
            o_ref[:, lo:hi] = (y * r * g_ref[:, lo:hi]).astype(BF16)

    return _rowcall(body, name, s, tm, [_rb(tm, 512), _rb(tm, 256), _rb(tm, 256), _fb((1, 1024))],
                    (ya, ysg, yss, g.reshape(1, 1024)), _rb(tm, 1024), _sds((s, 1024), BF16))


def _mix_bwd(dmix, ya, ysg, yss, g, name, tm=512):
    s = ya.shape[0]

    def body(dm_ref, a_ref, b_ref, c_ref, g_ref, da_ref, db_ref, dc_ref, dg_ref):
        @pl.when(pl.program_id(0) == 0)
        def _():
            dg_ref[...] = jnp.zeros_like(dg_ref)

        for ref, dref, (lo, hi) in zip((a_ref, b_ref, c_ref), (da_ref, db_ref, dc_ref), _MIX_PARTS):
            y = ref[...]
            r = lax.rsqrt(jnp.mean(y * y, axis=-1, keepdims=True) + EPS)
            xh = y * r
            dm = dm_ref[:, lo:hi]
            dg_ref[:, lo:hi] += jnp.sum(dm * xh, axis=0, keepdims=True)
            dxh = dm * g_ref[:, lo:hi]
            dref[...] = r * (dxh - xh * jnp.mean(dxh * xh, axis=-1, keepdims=True))

    da, db, dc, dg = _rowcall(
        body, name, s, tm, [_rb(tm, 1024), _rb(tm, 512), _rb(tm, 256), _rb(tm, 256), _fb((1, 1024))],
        (dmix, ya, ysg, yss, g.reshape(1, 1024)),
        [_rb(tm, 512), _rb(tm, 256), _rb(tm, 256), _fb((1, 1024))],
        [_sds((s, 512)), _sds((s, 256)), _sds((s, 256)), _sds((1, 1024))])
    return da, db, dc, dg.reshape(1024)


def _ssm_post_fwd(yc, z, d, gw, gb, name, tm=1024):
    s = yc.shape[0]

    def body(yc_ref, u_ref, d_ref, gw_ref, gb_ref, o_ref):
        y1 = yc_ref[...] + d_ref[...] * u_ref[...]
        y2 = _gelu(y1)
        gl = jnp.dot(y2.astype(BF16), gw_ref[...], preferred_element_type=F32) + gb_ref[...]
        o_ref[...] = y2 * _sigmoid(gl)

    return _rowcall(body, name, s, tm, [_rb(tm, 256), _rb(tm, 256, 8), _fb((1, 256)), _fb((256, 256)), _fb((1, 256))],
                    (yc, z, d.reshape(1, 256), gw, gb.reshape(1, 256)), _rb(tm, 256), _sds((s, 256)))


def _ssm_post_bwd(dy, yc, z, d, gw, gb, name, tm=1024):
    s = yc.shape[0]

    def body(dy_ref, yc_ref, u_ref, d_ref, gw_ref, gb_ref, dy1_ref, dgl_ref, y2_ref, dud_ref, dd_ref, dgb_ref):
        @pl.when(pl.program_id(0) == 0)
        def _():
            dd_ref[...] = jnp.zeros_like(dd_ref)
            dgb_ref[...] = jnp.zeros_like(dgb_ref)

        u = u_ref[...]
        dd = d_ref[...]
        y1 = yc_ref[...] + dd * u
        y2 = _gelu(y1)
        gw_v = gw_ref[...]
        gl = jnp.dot(y2.astype(BF16), gw_v, preferred_element_type=F32) + gb_ref[...]
        sg = _sigmoid(gl)
        dyv = dy_ref[...]
        dgl = dyv * y2 * sg * (1.0 - sg)
        dy2 = dyv * sg + lax.dot_general(dgl.astype(BF16), gw_v, _DIMS["nt"], preferred_element_type=F32)
        dy1 = dy2 * _gelu_grad(y1)
        dy1_ref[...] = dy1.astype(BF16)
        dgl_ref[...] = dgl.astype(BF16)
        y2_ref[...] = y2.astype(BF16)
        dud_ref[...] = dy1 * dd
        dd_ref[...] += jnp.sum(dy1 * u, axis=0, keepdims=True)
        dgb_ref[...] += jnp.sum(dgl, axis=0, keepdims=True)

    outs = _rowcall(
        body, name, s, tm,
        [_rb(tm, 256), _rb(tm, 256), _rb(tm, 256, 8), _fb((1, 256)), _fb((256, 256)), _fb((1, 256))],
        (dy, yc, z, d.reshape(1, 256), gw, gb.reshape(1, 256)),
        [_rb(tm, 256)] * 4 + [_fb((1, 256))] * 2,
        [_sds((s, 256), BF16)] * 3 + [_sds((s, 256))] + [_sds((1, 256))] * 2)
    dy1, dgl, y2, dud, dd, dgb = outs
    return dy1, dgl, y2, dud, dd.reshape(256), dgb.reshape(256)


def _lane_col(x, lane, j):
    return jnp.sum(jnp.where(lane == j, x, 0.0), axis=1, keepdims=True)


N_DOUBLINGS = 7


def _scan_chunk(xr, xi, kr_ref, ki_ref, reverse):
    for n in range(N_DOUBLINGS):
        shift = 128 - 2 ** n if reverse else 2 ** n
        ar, ai = kr_ref[n], ki_ref[n]
        sr = pltpu.roll(xr, shift, axis=1)
        si = pltpu.roll(xi, shift, axis=1)
        xr, xi = xr + ar * sr - ai * si, xi + ar * si + ai * sr
    return xr, xi


def _doubling_tables(pr, pi, reverse):
    lane = jnp.arange(128)[None, :]
    tabs = []
    for n in range(N_DOUBLINGS):
        k = 2 ** n
        keep = (lane < 128 - k) if reverse else (lane >= k)
        re = jnp.where(keep, pr[:, k - 1:k], 0.0)
        im = jnp.where(keep, -pi[:, k - 1:k] if reverse else pi[:, k - 1:k], 0.0)
        tabs.append(jnp.concatenate([re, im], axis=0))
    return jnp.stack(tabs)


def _scan_fwd(bu_t, pw, pk, name, cb=512):
    two_ns, s = bu_t.shape
    ns = two_ns // 2
    nrb = ns // cb
    nch = s // 128

    def body(br_ref, bi_ref, pr_ref, pi_ref, kr_ref, ki_ref, xr_ref, xi_ref, cr_ref, ci_ref):
        @pl.when(pl.program_id(1) == 0)
        def _():
            cr_ref[...] = jnp.zeros_like(cr_ref)
            ci_ref[...] = jnp.zeros_like(ci_ref)

        lane = lax.broadcasted_iota(jnp.int32, (cb, 128), 1)
        pr = pr_ref[...]
        pi = pi_ref[...]
        xr, xi = _scan_chunk(br_ref[...], bi_ref[...], kr_ref, ki_ref, False)
        cr = cr_ref[...]
        ci = ci_ref[...]
        xr = xr + pr * cr - pi * ci
        xi = xi + pr * ci + pi * cr
        xr_ref[...] = xr
        xi_ref[...] = xi
        cr_ref[...] = jnp.broadcast_to(_lane_col(xr, lane, 127), (cb, 128))
        ci_ref[...] = jnp.broadcast_to(_lane_col(xi, lane, 127), (cb, 128))

    re_spec = pl.BlockSpec((cb, 128), lambda i, c: (i, c))
    im_spec = pl.BlockSpec((cb, 128), lambda i, c: (i + nrb, c))
    pre_spec = pl.BlockSpec((cb, 128), lambda i, c: (i, 0))
    pim_spec = pl.BlockSpec((cb, 128), lambda i, c: (i + nrb, 0))
    kre_spec = pl.BlockSpec((N_DOUBLINGS, cb, 128), lambda i, c: (0, i, 0))
    kim_spec = pl.BlockSpec((N_DOUBLINGS, cb, 128), lambda i, c: (0, i + nrb, 0))
    xr, xi = pl.pallas_call(
        body, name=name, grid=(nrb, nch), in_specs=[re_spec, im_spec, pre_spec, pim_spec, kre_spec, kim_spec],
        out_specs=[re_spec, re_spec], out_shape=[_sds((ns, s)), _sds((ns, s))],
        scratch_shapes=[pltpu.VMEM((cb, 128), F32), pltpu.VMEM((cb, 128), F32)],
        compiler_params=_params(("parallel", "arbitrary")),
    )(bu_t, bu_t, pw, pw, pk, pk)
    return xr, xi


def _scan_bwd(g_t, xr, xi, pw_rev, pk_rev, name, cb=512):
    two_ns, s = g_t.shape
    ns = two_ns // 2
    nrb = ns // cb
    nch = s // 128

    def body(gr_ref, gi_ref, pr_ref, pi_ref, kr_ref, ki_ref, xr_ref, xi_ref, xpr_ref, xpi_ref,
             lr_ref, li_ref, dar_ref, dai_ref, cr_ref, ci_ref, ar_acc, ai_acc):
        c = pl.program_id(1)

        @pl.when(c == 0)
        def _():
            cr_ref[...] = jnp.zeros_like(cr_ref)
            ci_ref[...] = jnp.zeros_like(ci_ref)
            ar_acc[...] = jnp.zeros_like(ar_acc)
            ai_acc[...] = jnp.zeros_like(ai_acc)

        lane = lax.broadcasted_iota(jnp.int32, (cb, 128), 1)
        pr = pr_ref[...]
        pi = pi_ref[...]
        lr, li = _scan_chunk(gr_ref[...], gi_ref[...], kr_ref, ki_ref, True)
        cr = cr_ref[...]
        ci = ci_ref[...]
        lr = lr + pr * cr - pi * ci
        li = li + pr * ci + pi * cr
        lr_ref[...] = lr
        li_ref[...] = li
        cr_ref[...] = jnp.broadcast_to(_lane_col(lr, lane, 0), (cb, 128))
        ci_ref[...] = jnp.broadcast_to(_lane_col(li, lane, 0), (cb, 128))
        has_prev = (c < nch - 1).astype(F32)
        pvr = _lane_col(xpr_ref[...], lane, 127) * has_prev
        pvi = _lane_col(xpi_ref[...], lane, 127) * has_prev
        sxr = jnp.where(lane == 0, pvr, pltpu.roll(xr_ref[...], 1, axis=1))
        sxi = jnp.where(lane == 0, pvi, pltpu.roll(xi_ref[...], 1, axis=1))
        ar_acc[...] += lr * sxr + li * sxi
        ai_acc[...] += li * sxr - lr * sxi

        @pl.when(c == nch - 1)
        def _():
            dar_ref[...] = jnp.broadcast_to(jnp.sum(ar_acc[...], axis=1, keepdims=True), (cb, 128))
            dai_ref[...] = jnp.broadcast_to(jnp.sum(ai_acc[...], axis=1, keepdims=True), (cb, 128))

    rev = lambda c: nch - 1 - c
    re_spec = pl.BlockSpec((cb, 128), lambda i, c: (i, rev(c)))
    im_spec = pl.BlockSpec((cb, 128), lambda i, c: (i + nrb, rev(c)))
    prev_spec = pl.BlockSpec((cb, 128), lambda i, c: (i, jnp.maximum(rev(c) - 1, 0)))
    pre_spec = pl.BlockSpec((cb, 128), lambda i, c: (i, 0))
    pim_spec = pl.BlockSpec((cb, 128), lambda i, c: (i + nrb, 0))
    acc_spec = pl.BlockSpec((cb, 128), lambda i, c: (i, 0))
    kre_spec = pl.BlockSpec((N_DOUBLINGS, cb, 128), lambda i, c: (0, i, 0))
    kim_spec = pl.BlockSpec((N_DOUBLINGS, cb, 128), lambda i, c: (0, i + nrb, 0))
    lr, li, dar, dai = pl.pallas_call(
        body, name=name, grid=(nrb, nch),
        in_specs=[re_spec, im_spec, pre_spec, pim_spec, kre_spec, kim_spec, re_spec, re_spec, prev_spec, prev_spec],
        out_specs=[re_spec, re_spec, acc_spec, acc_spec],
        out_shape=[_sds((ns, s)), _sds((ns, s)), _sds((ns, 128)), _sds((ns, 128))],
        scratch_shapes=[pltpu.VMEM((cb, 128), F32)] * 4,
        compiler_params=_params(("parallel", "arbitrary")),
    )(g_t, g_t, pw_rev, pw_rev, pk_rev, pk_rev, xr, xi, xr, xi)
    return lr, li, dar[:, 0], dai[:, 0]


def _group_ids():
    return lax.broadcasted_iota(jnp.int32, (1, SGU_W), 1) // 64


def _group_mean(val, gid):
    out = jnp.zeros_like(val)
    for g in range(SGU_GROUPS):
        mg = gid == g
        out = jnp.where(mg, jnp.sum(jnp.where(mg, val, 0.0), axis=1, keepdims=True) * (1.0 / 64), out)
    return out


def _causal_w(w_ref, g):
    t = lax.broadcasted_iota(jnp.int32, (SGU_CHUNK, SGU_CHUNK), 0)
    s = lax.broadcasted_iota(jnp.int32, (SGU_CHUNK, SGU_CHUNK), 1)
    return jnp.where(t >= s, w_ref[g], 0.0).astype(BF16)


def _sgu_core(x, lng, lnb, w_ref, bexp, gid):
    zz = _gelu(x)
    u = zz[:, :SGU_W]
    v = zz[:, SGU_W:]
    vc = v - _group_mean(v, gid)
    rstd = lax.rsqrt(_group_mean(vc * vc, gid) + EPS)
    vhat = vc * rstd
    vn = vhat * lng + lnb
    vnb = vn.astype(BF16)
    mixed = bexp
    for g in range(SGU_GROUPS):
        mm = jnp.dot(_causal_w(w_ref, g), vnb, preferred_element_type=F32)
        mixed = jnp.where(gid == g, mm + bexp, mixed)
    return u, rstd, vhat, vnb, mixed


def _sgu_fwd(z, lng, lnb, w, bexp, name, tm=512):
    s = z.shape[0]

    def body(z_ref, lng_ref, lnb_ref, w_ref, b_ref, o_ref):
        gid = _group_ids()
        for j in range(tm // SGU_CHUNK):
            rows = pl.ds(j * SGU_CHUNK, SGU_CHUNK)
            u, _, _, _, mixed = _sgu_core(z_ref[rows, :], lng_ref[...], lnb_ref[...], w_ref, b_ref[...], gid)
            o_ref[rows, :] = u * mixed

    return _rowcall(body, name, s, tm,
                    [_rb(tm, 512, 3), _fb((1, 256)), _fb((1, 256)), _fb((4, 128, 128)), _fb((128, 256))],
                    (z, lng.reshape(1, 256), lnb.reshape(1, 256), w, bexp), _rb(tm, 256), _sds((s, 256)))


def _sgu_bwd(z, dy, lng, lnb, w, bexp, name, tm=512):
    s = z.shape[0]

    def body(z_ref, dy_ref, lng_ref, lnb_ref, w_ref, b_ref, dz_ref, dw_ref, db_ref, dlng_ref, dlnb_ref):
        @pl.when(pl.program_id(0) == 0)
        def _():
            dw_ref[...] = jnp.zeros_like(dw_ref)
            db_ref[...] = jnp.zeros_like(db_ref)
            dlng_ref[...] = jnp.zeros_like(dlng_ref)
            dlnb_ref[...] = jnp.zeros_like(dlnb_ref)

        gid = _group_ids()
        t = lax.broadcasted_iota(jnp.int32, (SGU_CHUNK, SGU_CHUNK), 0)
        sidx = lax.broadcasted_iota(jnp.int32, (SGU_CHUNK, SGU_CHUNK), 1)
        lng_v = lng_ref[...]
        for j in range(tm // SGU_CHUNK):
            rows = pl.ds(j * SGU_CHUNK, SGU_CHUNK)
            x = z_ref[rows, :]
            u, rstd, vhat, vnb, mixed = _sgu_core(x, lng_v, lnb_ref[...], w_ref, b_ref[...], gid)
            dyv = dy_ref[rows, :]
            dmixed = dyv * u
            du = dyv * mixed
            db_ref[...] += dmixed
            dvn = jnp.zeros_like(dmixed)
            for g in range(SGU_GROUPS):
                dmg = jnp.where(gid == g, dmixed, 0.0).astype(BF16)
                dvn = dvn + lax.dot_general(_causal_w(w_ref, g), dmg, _DIMS["tn"], preferred_element_type=F32)
                dwg = lax.dot_general(dmg, vnb, _DIMS["nt"], preferred_element_type=F32)
                dw_ref[g] += jnp.where(t >= sidx, dwg, 0.0)
            dlnb_ref[...] += jnp.sum(dvn, axis=0, keepdims=True)
            dlng_ref[...] += jnp.sum(dvn * vhat, axis=0, keepdims=True)
            dvh = dvn * lng_v
            dv = rstd * (dvh - _group_mean(dvh, gid) - vhat * _group_mean(dvh * vhat, gid))
            gg = _gelu_grad(x)
            dz_ref[rows, 0:SGU_W] = du * gg[:, :SGU_W]
            dz_ref[rows, SGU_W:2 * SGU_W] = dv * gg[:, SGU_W:]

    dz, dw, db, dlng, dlnb = _rowcall(
        body, name, s, tm,
        [_rb(tm, 512, 3), _rb(tm, 256), _fb((1, 256)), _fb((1, 256)), _fb((4, 128, 128)), _fb((128, 256))],
        (z, dy, lng.reshape(1, 256), lnb.reshape(1, 256), w, bexp),
        [_rb(tm, 512), _fb((4, 128, 128)), _fb((128, 256)), _fb((1, 256)), _fb((1, 256))],
        [_sds((s, 512)), _sds((4, 128, 128)), _sds((128, 256)), _sds((1, 256)), _sds((1, 256))])
    return dz, dw, db, dlng.reshape(256), dlnb.reshape(256)


CONV_TC = 1408
N_CT = D_FF // CONV_TC


def _row_of(block8, j):
    r = lax.broadcasted_iota(jnp.int32, block8.shape, 0)
    return jnp.sum(jnp.where(r == j, block8, 0.0), axis=0, keepdims=True)


def _shift_down(x, tail, has_prev, row):
    r7 = _row_of(tail, 7) * has_prev
    r6 = _row_of(tail, 6) * has_prev
    x1 = jnp.where(row == 0, r7, pltpu.roll(x, 1, axis=0))
    x2 = jnp.where(row == 0, r6, jnp.where(row == 1, r7, pltpu.roll(x, 2, axis=0)))
    return x1, x2


def _conv_fwd(hu, cw, cb, name, tm=256):
    s = hu.shape[0]
    n8 = tm // 8

    def body(xv_ref, xg_ref, tv_ref, tg_ref, wv_ref, wg_ref, bv_ref, bg_ref, hv_ref, hg_ref, act_ref):
        i = pl.program_id(1)
        has_prev = (i > 0).astype(F32)
        row = lax.broadcasted_iota(jnp.int32, (tm, CONV_TC), 0)

        def conv(x_ref, t_ref, w_ref, b_ref):
            x = x_ref[...]
            x1, x2 = _shift_down(x, t_ref[...], has_prev, row)
            return w_ref[0:1, :] * x2 + w_ref[1:2, :] * x1 + w_ref[2:3, :] * x + b_ref[...]

        hv = conv(xv_ref, tv_ref, wv_ref, bv_ref)
        hg = conv(xg_ref, tg_ref, wg_ref, bg_ref)
        hv_ref[...] = hv.astype(BF16)
        hg_ref[...] = hg.astype(BF16)
        act_ref[...] = (_gelu(hg) * hv).astype(BF16)

    def xs(off):
        return pl.BlockSpec((tm, CONV_TC), lambda j, i: (i, j + off))

    def ts(off):
        return pl.BlockSpec((8, CONV_TC), lambda j, i: (jnp.maximum(i * n8 - 1, 0), j + off))

    def ws(rows, off):
        return pl.BlockSpec((rows, CONV_TC), lambda j, i: (0, j + off))

    o_spec = pl.BlockSpec((tm, CONV_TC), lambda j, i: (i, j))
    return pl.pallas_call(
        body, name=name, grid=(N_CT, s // tm),
        in_specs=[xs(0), xs(N_CT), ts(0), ts(N_CT), ws(3, 0), ws(3, N_CT), ws(1, 0), ws(1, N_CT)],
        out_specs=[o_spec] * 3, out_shape=[_sds((s, D_FF), BF16)] * 3,
        compiler_params=_params(("parallel", "arbitrary")),
    )(hu, hu, hu, hu, cw, cw, cb.reshape(1, 2 * D_FF), cb.reshape(1, 2 * D_FF))


HALO = 16


def _conv_bwd(dact, hv, hg, hu, cw, name, tm=256):
    s = dact.shape[0]
    n8 = tm // 8

    def body(da_ref, dan_ref, hv_ref, hvn_ref, hg_ref, hgn_ref, x_ref, t_ref, w_ref, dx_ref, dw_ref, db_ref, d_scr):
        i = pl.program_id(1)
        is_value = pl.program_id(0) < N_CT

        @pl.when(i == 0)
        def _():
            dw_ref[...] = jnp.zeros_like(dw_ref)
            db_ref[...] = jnp.zeros_like(db_ref)

        for rows, (a_ref, v_ref, g_ref) in ((pl.ds(0, tm), (da_ref, hv_ref, hg_ref)),
                                            (pl.ds(tm, HALO), (dan_ref, hvn_ref, hgn_ref))):
            @pl.when(is_value)
            def _():
                d_scr[rows, :] = a_ref[...].astype(F32) * _gelu(g_ref[...].astype(F32))

            @pl.when(jnp.logical_not(is_value))
            def _():
                d_scr[rows, :] = (a_ref[...].astype(F32) * v_ref[...].astype(F32)
                                  * _gelu_grad(g_ref[...].astype(F32)))

        has_prev = (i > 0).astype(F32)
        has_next = (i < s // tm - 1).astype(F32)
        row = lax.broadcasted_iota(jnp.int32, (tm, CONV_TC), 0)
        d = d_scr[0:tm, :]
        nxt = d_scr[tm:tm + HALO, :]
        n0 = _row_of(nxt, 0) * has_next
        n1 = _row_of(nxt, 1) * has_next
        d1 = jnp.where(row == tm - 1, n0, pltpu.roll(d, tm - 1, axis=0))
        d2 = jnp.where(row == tm - 2, n0, jnp.where(row == tm - 1, n1, pltpu.roll(d, tm - 2, axis=0)))
        dx_ref[...] = (w_ref[2:3, :] * d + w_ref[1:2, :] * d1 + w_ref[0:1, :] * d2).astype(BF16)
        x = x_ref[...]
        x1, x2 = _shift_down(x, t_ref[...], has_prev, row)
        dw_ref[0:1, :] += jnp.sum(d * x2, axis=0, keepdims=True)
        dw_ref[1:2, :] += jnp.sum(d * x1, axis=0, keepdims=True)
        dw_ref[2:3, :] += jnp.sum(d * x, axis=0, keepdims=True)
        db_ref[...] += jnp.sum(d, axis=0, keepdims=True)

    a_spec = pl.BlockSpec((tm, CONV_TC), lambda j, i: (i, j % N_CT))
    an_spec = pl.BlockSpec((HALO, CONV_TC),
                           lambda j, i: (jnp.minimum((i + 1) * (tm // HALO), s // HALO - 1), j % N_CT))
    x_spec = pl.BlockSpec((tm, CONV_TC), lambda j, i: (i, j))
    t_spec = pl.BlockSpec((8, CONV_TC), lambda j, i: (jnp.maximum(i * n8 - 1, 0), j))
    w_spec = pl.BlockSpec((3, CONV_TC), lambda j, i: (0, j))
    db_spec = pl.BlockSpec((1, CONV_TC), lambda j, i: (0, j))
    return pl.pallas_call(
        body, name=name, grid=(2 * N_CT, s // tm),
        in_specs=[a_spec, an_spec, a_spec, an_spec, a_spec, an_spec, x_spec, t_spec, w_spec],
        out_specs=[x_spec, w_spec, db_spec],
        out_shape=[_sds((s, 2 * D_FF), BF16), _sds((3, 2 * D_FF)), _sds((1, 2 * D_FF))],
        scratch_shapes=[pltpu.VMEM((tm + HALO, CONV_TC), F32)],
        compiler_params=_params(("parallel", "arbitrary")),
    )(dact, dact, hv, hv, hg, hg, hu, hu, cw)


def _ple_fwd(h, gp, pp, name, tm=512):
    s, d = h.shape

    def body(h_ref, g_ref, p_ref, o_ref):
        o_ref[...] = h_ref[...] + _sigmoid(g_ref[...]) * p_ref[...]

    return _rowcall(body, name, s, tm, [_rb(tm, d)] * 3, (h, gp, pp), _rb(tm, d), _sds((s, d)))


def _ple_bwd(dh, gp, pp, name, tm=512):
    s, d = dh.shape

    def body(d_ref, g_ref, p_ref, dp_ref, dg_ref):
        sg = _sigmoid(g_ref[...])
        dv = d_ref[...]
        dp_ref[...] = (dv * sg).astype(BF16)
        dg_ref[...] = (dv * p_ref[...] * sg * (1.0 - sg)).astype(BF16)

    return _rowcall(body, name, s, tm, [_rb(tm, d)] * 3, (dh, gp, pp), [_rb(tm, d)] * 2,
                    [_sds((s, d), BF16)] * 2)


SCALE = HEAD_DIM ** -0.5
ATT_ROWS = 2048


def _att_geom(s, dil):
    w = min(ATT_ROWS, s)
    p = BLK * dil
    assert w % p == 0 and s % w == 0
    return w, p, w // p


def _rows(start, dil):
    return pl.ds(start, BLK, stride=dil) if dil > 1 else pl.ds(start, BLK)


def _head_masks():
    lane = lax.broadcasted_iota(jnp.int32, (1, BLK), 1)
    return [lane < HEAD_DIM, lane >= HEAD_DIM]


def _band_valid(has_prev):
    qi = lax.broadcasted_iota(jnp.int32, (BLK, 2 * BLK), 0)
    ki = lax.broadcasted_iota(jnp.int32, (BLK, 2 * BLK), 1)
    rel = qi + BLK - ki
    band = (rel >= 0) & (rel <= BLK)
    if has_prev is True:
        return band
    return band & (has_prev | (ki >= BLK))


def _zcur(w):
    return lambda off: pl.BlockSpec((w, BLK), lambda hp, i: (i, off + hp))


def _zprev(p, nb):
    return lambda off: pl.BlockSpec((p, BLK), lambda hp, i: (jnp.maximum(i * nb - 1, 0), off + hp))


def _scur(w):
    return pl.BlockSpec((w, BLK), lambda hp, i: (i, hp))


def _pair_rows(t, masks):
    return jnp.concatenate([jnp.where(masks[0], t, 0.0), jnp.where(masks[1], t, 0.0)], axis=0).astype(BF16)


def _pair_bias_fwd(bias):
    return bias.reshape(4, 2, BLK, 2 * BLK).transpose(0, 2, 1, 3).reshape(4, BLK, 4 * BLK)


def _pair_bias_bwd(bias):
    return bias.reshape(4, 2, BLK, 2, BLK).transpose(0, 3, 2, 1, 4).reshape(4, 2, BLK, 2 * BLK)


def _unpair_bias_bwd(db):
    return db.reshape(4, 2, BLK, 2, BLK).transpose(0, 3, 2, 1, 4).reshape(N_HEADS, BLK, 2 * BLK)


def _attn_fwd(z, bias, state, dil, first, last, name):
    s = z.shape[0]
    w, p, nb = _att_geom(s, dil)

    def body(*refs):
        q_ref, kp_ref, kc_ref, vp_ref, vc_ref, b_ref = refs[:6]
        rest = refs[6:]
        if not first:
            m_ref, l_ref, a_ref = rest[:3]
            rest = rest[3:]
        i = pl.program_id(1)
        masks = _head_masks()
        first_head = lax.broadcasted_iota(jnp.int32, (1, 4 * BLK), 1) < 2 * BLK
        for r in range(dil):
            for b in range(nb):
                rows = _rows(r + p * b, dil)
                prev_rows = _rows(r + p * (b - 1), dil) if b > 0 else _rows(r, dil)
                kprev, vprev = (kc_ref, vc_ref) if b > 0 else (kp_ref, vp_ref)
                q = q_ref[rows, :].astype(BF16)
                k = _pair_rows(jnp.concatenate([kprev[prev_rows, :], kc_ref[rows, :]], axis=0), masks)
                v = _pair_rows(jnp.concatenate([vprev[prev_rows, :], vc_ref[rows, :]], axis=0), masks)
                valid = _band_valid(True if b > 0 else i > 0)
                valid = jnp.concatenate([valid, valid], axis=1)
                sc = lax.dot_general(q, k, _DIMS["nt"], preferred_element_type=F32) * SCALE + b_ref[...]
                sc = jnp.where(valid, sc, NEG_INF)
                mx = [jnp.max(sc[:, :2 * BLK], axis=1, keepdims=True), jnp.max(sc[:, 2 * BLK:], axis=1, keepdims=True)]
                e = jnp.exp(sc - jnp.where(first_head, mx[0], mx[1]))
                den = [jnp.sum(e[:, :2 * BLK], axis=1, keepdims=True), jnp.sum(e[:, 2 * BLK:], axis=1, keepdims=True)]
                ob = jnp.dot(e.astype(BF16), v, preferred_element_type=F32)
                mb = jnp.where(masks[0], mx[0], mx[1])
                lb = jnp.where(masks[0], den[0], den[1])
                if first:
                    m_new, l_new, a_new = mb, lb, ob
                else:
                    m_old = m_ref[rows, :]
                    m_new = jnp.maximum(m_old, mb)
                    al = jnp.exp(m_old - m_new)
                    be = jnp.exp(mb - m_new)
                    l_new = al * l_ref[rows, :] + be * lb
                    a_new = al * a_ref[rows, :] + be * ob
                if last:
                    y_ref, lse_ref = rest
                    y_ref[rows, :] = a_new / l_new
                    lse_ref[rows, :] = m_new + jnp.log(l_new)
                else:
                    mo_ref, lo_ref, ao_ref = rest
                    mo_ref[rows, :] = m_new
                    lo_ref[rows, :] = l_new
                    ao_ref[rows, :] = a_new

    cur, prv = _zcur(w), _zprev(p, nb)
    b_spec = pl.BlockSpec((None, BLK, 4 * BLK), lambda hp, i: (hp, 0, 0))
    in_specs = [cur(0), prv(4), cur(4), prv(8), cur(8), b_spec]
    args = [z, z, z, z, z, bias]
    if not first:
        in_specs += [_scur(w)] * 3
        args += list(state)
    n_out = 2 if last else 3
    return pl.pallas_call(
        body, name=name, grid=(4, s // w), in_specs=in_specs, out_specs=[_scur(w)] * n_out,
        out_shape=[_sds((s, ATTN_W))] * n_out,
        compiler_params=_params(("parallel", "parallel")),
    )(*args)


def _row_stats(mh, dy, y, lse):
    delta = jnp.sum(jnp.where(mh, dy * y, 0.0), axis=1, keepdims=True)
    lse_h = jnp.max(jnp.where(mh, lse, NEG_INF), axis=1, keepdims=True)
    return delta, lse_h


def _attn_bwd(z, bias, dy, y, lse, prev, dil, name):
    s = z.shape[0]
    w, p, nb = _att_geom(s, dil)
    n_steps = s // w
    first = prev is None

    def body(*refs):
        q_ref, kp_ref, kc_ref, vp_ref, vc_ref, b_ref, dy_ref, y_ref, lse_ref = refs[:9]
        rest = refs[9:]
        if not first:
            dqp_ref, dkp_ref, dvp_ref = rest[:3]
            rest = rest[3:]
        dq_ref, dk_ref, dv_ref, dkx_ref, dvx_ref, db_ref = rest
        i = pl.program_id(1)

        @pl.when(i == 0)
        def _():
            db_ref[...] = jnp.zeros_like(db_ref)

        qi = lax.broadcasted_iota(jnp.int32, (BLK, BLK), 0)
        ki = lax.broadcasted_iota(jnp.int32, (BLK, BLK), 1)
        masks = _head_masks()
        first_head = lax.broadcasted_iota(jnp.int32, (1, 2 * BLK), 1) < BLK

        def flush(rows, dk, dv):
            if not first:
                dk = dk + dkp_ref[rows, :]
                dv = dv + dvp_ref[rows, :]
            dk_ref[rows, :] = dk
            dv_ref[rows, :] = dv

        for r in range(dil):
            carry = None
            for b in range(nb):
                rows = _rows(r + p * b, dil)
                prev_rows = _rows(r + p * (b - 1), dil) if b > 0 else _rows(r, dil)
                kprev, vprev = (kc_ref, vc_ref) if b > 0 else (kp_ref, vp_ref)
                keys = [(_pair_rows(kprev[prev_rows, :], masks), _pair_rows(vprev[prev_rows, :], masks)),
                        (_pair_rows(kc_ref[rows, :], masks), _pair_rows(vc_ref[rows, :], masks))]
                valid = [(ki >= qi) if b > 0 else ((ki >= qi) & (i > 0)), qi >= ki]
                q = q_ref[rows, :].astype(BF16)
                dy_v = dy_ref[rows, :]
                dyb = dy_v.astype(BF16)
                stats = [_row_stats(mh, dy_v, y_ref[rows, :], lse_ref[rows, :]) for mh in masks]
                delta = jnp.where(first_head, stats[0][0], stats[1][0])
                lse_h = jnp.where(first_head, stats[0][1], stats[1][1])
                dq = jnp.zeros((BLK, BLK), F32)
                dk, dv = [], []
                for half in range(2):
                    kh, vh = keys[half]
                    ok = jnp.concatenate([valid[half], valid[half]], axis=1)
                    sc = lax.dot_general(q, kh, _DIMS["nt"], preferred_element_type=F32) * SCALE + b_ref[half]
                    pr = jnp.where(ok, jnp.exp(jnp.where(ok, sc, NEG_INF) - lse_h), 0.0)
                    dp = lax.dot_general(dyb, vh, _DIMS["nt"], preferred_element_type=F32)
                    ds = pr * (dp - delta)
                    db_ref[half] += ds
                    dsb = ds.astype(BF16)
                    dq = dq + jnp.dot(dsb, kh, preferred_element_type=F32)
                    dk2 = lax.dot_general(dsb, q, _DIMS["tn"], preferred_element_type=F32)
                    dv2 = lax.dot_general(pr.astype(BF16), dyb, _DIMS["tn"], preferred_element_type=F32)
                    dk.append(jnp.where(masks[0], dk2[:BLK], dk2[BLK:]))
                    dv.append(jnp.where(masks[0], dv2[:BLK], dv2[BLK:]))
                dq = dq * SCALE
                if not first:
                    dq = dq + dqp_ref[rows, :]
                dq_ref[rows, :] = dq
                if b > 0:
                    flush(prev_rows, carry[0] + dk[0] * SCALE, carry[1] + dv[0])
                else:
                    dkx_ref[prev_rows, :] = dk[0] * SCALE
                    dvx_ref[prev_rows, :] = dv[0]
                carry = (dk[1] * SCALE, dv[1])
            flush(_rows(r + p * (nb - 1), dil), *carry)

    cur, prv = _zcur(w), _zprev(p, nb)
    b_spec = pl.BlockSpec((None, 2, BLK, 2 * BLK), lambda hp, i: (hp, 0, 0, 0))
    in_specs = [cur(0), prv(4), cur(4), prv(8), cur(8), b_spec] + [_scur(w)] * 3
    args = [z, z, z, z, z, bias, dy, y, lse]
    if not first:
        in_specs += [_scur(w)] * 3
        args += list(prev)
    x_spec = pl.BlockSpec((p, BLK), lambda hp, i: (i, hp))
    *outs, db = pl.pallas_call(
        body, name=name, grid=(4, n_steps), in_specs=in_specs,
        out_specs=[_scur(w)] * 3 + [x_spec] * 2 + [b_spec],
        out_shape=[_sds((s, ATTN_W))] * 3 + [_sds((n_steps * p, ATTN_W))] * 2 + [_sds((4, 2, BLK, 2 * BLK))],
        compiler_params=_params(("parallel", "arbitrary")),
    )(*args)
    return (*outs, _unpair_bias_bwd(db))


ASM_ROWS = 512


def _assemble_dz(dq, dk, dv, extras, dzs, du, name):
    s = dq.shape[0]
    w = min(ATT_ROWS, s)
    n_steps = s // w
    per_step = w // ASM_ROWS
    assert w % ASM_ROWS == 0

    def body(*refs):
        dq_ref, dk_ref, dv_ref, dzs_ref, du_ref = refs[:5]
        x_refs = refs[5:5 + 2 * len(extras)]
        o_ref, acc_ref = refs[-2:]
        j = pl.program_id(0)
        step = j // per_step
        has_next = (step < n_steps - 1).astype(F32)
        last_of_step = ((j + 1) % per_step == 0).astype(F32)
        o_ref[:, 0:ATTN_W] = dq_ref[...].astype(BF16)
        o_ref[:, 3 * ATTN_W:3 * ATTN_W + 2 * SGU_W] = dzs_ref[...].astype(BF16)
        o_ref[:, 3 * ATTN_W + 2 * SGU_W:IN_W] = du_ref[...].astype(BF16)
        for part, (base_ref, col) in enumerate(((dk_ref, ATTN_W), (dv_ref, 2 * ATTN_W))):
            acc_ref[...] = base_ref[...]
            for n, (_, dil) in enumerate(BRANCHES):
                rows = min(BLK * dil, ASM_ROWS)
                scale = has_next if BLK * dil >= w else has_next * last_of_step
                acc_ref[ASM_ROWS - rows:, :] += x_refs[2 * n + part][...] * scale
            o_ref[:, col:col + ATTN_W] = acc_ref[...].astype(BF16)

    def x_spec(dil):
        p = BLK * dil
        rows = min(p, ASM_ROWS)
        blocks_per_step = p // rows
        total = n_steps * blocks_per_step

        def idx(j):
            step = j // per_step
            within = (j % per_step) - (per_step - blocks_per_step)
            return (jnp.clip((step + 1) * blocks_per_step + jnp.maximum(within, 0), 0, total - 1), 0)

        return pl.BlockSpec((rows, ATTN_W), idx)

    in_specs = [_rb(ASM_ROWS, ATTN_W)] * 3 + [_rb(ASM_ROWS, 2 * SGU_W), _rb(ASM_ROWS, SSM_W)]
    args = [dq, dk, dv, dzs, du]
    for (dkx, dvx), (_, dil) in zip(extras, BRANCHES):
        in_specs += [x_spec(dil)] * 2
        args += [dkx, dvx]
    return pl.pallas_call(
        body, name=name, grid=(s // ASM_ROWS,), in_specs=in_specs, out_specs=_rb(ASM_ROWS, IN_W),
        out_shape=_sds((s, IN_W), BF16), scratch_shapes=[pltpu.VMEM((ASM_ROWS, ATTN_W), F32)],
        compiler_params=_params(("parallel",)),
    )(*args)


def _t5_bucket(dist):
    max_exact = N_BUCKETS // 2
    d = np.maximum(dist, 0)
    large = max_exact + (np.log(np.maximum(d, 1) / max_exact) / np.log(REL_MAX / max_exact)
                         * (N_BUCKETS - max_exact)).astype(np.int32)
    large = np.minimum(large, N_BUCKETS - 1)
    return np.where(d < max_exact, d, large).astype(np.int32)


def _bias_tables(rel_bias):
    period = 3 * BLK
    tabs = []
    for _, dil in BRANCHES:
        onehot = np.zeros((period, N_BUCKETS), np.float32)
        d = np.arange(BLK + 1)
        onehot[d, _t5_bucket((BLK - d) * dil)] = 1.0
        f = jnp.dot(jnp.asarray(onehot), rel_bias, precision=lax.Precision.HIGHEST)
        flat = jnp.tile(f.T, (1, BLK))[:, :BLK * (period - 1)]
        tabs.append(flat.reshape(N_HEADS, BLK, period - 1)[:, :, :2 * BLK])
    return tabs


def _bucket_onehot():
    maps = []
    q = np.arange(BLK)[:, None]
    k = np.arange(2 * BLK)[None, :]
    rel = q + BLK - k
    for _, dil in BRANCHES:
        maps.append(np.where((rel >= 0) & (rel <= BLK), _t5_bucket(rel * dil), -1).reshape(-1))
    bmap = jnp.asarray(np.concatenate(maps).astype(np.int32))
    return (bmap[:, None] == jnp.arange(128, dtype=jnp.int32)[None, :]).astype(BF16)


def _block_diag(t):
    g, n, c = t.shape
    eye = jnp.eye(g, dtype=t.dtype)
    return (t[:, :, None, :] * eye[:, None, :, None]).reshape(g * n, g * c)


def _ssm_prep(a_re, a_im, log_dt, b_re, b_im, c_re, c_im):
    lam = lax.complex(a_re, a_im)
    dt = jnp.exp(log_dt)[:, None]
    a_bar = jnp.exp(lam * dt)
    b_bar = ((a_bar - 1.0) / lam)[:, :, None] * lax.complex(b_re, b_im)
    bdt = jnp.concatenate([_block_diag(jnp.real(b_bar)), _block_diag(jnp.imag(b_bar))], axis=0)
    cd = jnp.concatenate([_block_diag(jnp.transpose(c_re, (0, 2, 1))),
                          _block_diag(-jnp.transpose(c_im, (0, 2, 1)))], axis=0)
    return jnp.real(a_bar).reshape(-1), jnp.imag(a_bar).reshape(-1), bdt, cd


def _powers(ar, ai):
    pr, pi = ar[:, None], ai[:, None]
    k = 1
    while k < 128:
        lr, li = pr[:, -1:], pi[:, -1:]
        pr, pi = (jnp.concatenate([pr, pr * lr - pi * li], axis=1),
                  jnp.concatenate([pi, pr * li + pi * lr], axis=1))
        k *= 2
    return pr, pi


def _sgu_bias_expand(b):
    return jnp.repeat(b.T, 64, axis=1)


def _layer_fwd(i, h, p_i, big, small, bias_tabs):
    nm = "l%d_" % i
    sv = {"h": h}
    a1 = _rms_fwd(h, small["norm_attn_g"][i], nm + "rms_attn")
    z = _mm(a1, big["w_in"], "nt", nm + "in_proj")
    st = None
    for b, (_, dil) in enumerate(BRANCHES):
        st = _attn_fwd(z, bias_tabs[b][0], st, dil, b == 0, b == len(BRANCHES) - 1, nm + "attn_fwd%d" % b)
    y_attn, lse = st
    bexp = _sgu_bias_expand(small["sgu_b"][i])
    y_sgu = _sgu_fwd(z, small["sgu_ln_g"][i], small["sgu_ln_b"][i], small["sgu_w"][i], bexp, nm + "sgu_fwd")
    ar, ai, bdt, cd = _ssm_prep(*[small[k][i] for k in ("ssm_a_re", "ssm_a_im", "ssm_log_dt", "ssm_b_re",
                                                         "ssm_b_im", "ssm_c_re", "ssm_c_im")])
    pr, pi = _powers(ar, ai)
    u = z[:, IN_W - SSM_W:].astype(BF16)
    bu_t = _mm(bdt, u, "nt", nm + "ssm_bu")
    xr, xi = _scan_fwd(bu_t, jnp.concatenate([pr, pi], axis=0), _doubling_tables(pr, pi, False), nm + "ssm_scan")
    yc = _mm(xr, cd[:SSM_NS], "tn", nm + "ssm_cx_re")
    yc = _mm(xi, cd[SSM_NS:], "tn", nm + "ssm_cx_im", add=yc)
    y_ssm = _ssm_post_fwd(yc, z, small["ssm_d"][i], big["ssm_glu_w"], small["ssm_glu_b"][i], nm + "ssm_post")
    mix = _mix_fwd(y_attn, y_sgu, y_ssm, small["branch_norm_g"][i], nm + "mix")
    h2 = _mm(mix, big["w_out"], "nn", nm + "out_proj", add=h)
    a2 = _rms_fwd(h2, small["norm_ffn_g"][i], nm + "rms_ffn")
    hu = _mm(a2, big["ffn_w_up"], "nt", nm + "ffn_up")
    hv, hg, act = _conv_fwd(hu, big["ffn_conv_w"], small["ffn_conv_b"][i], nm + "ffn_conv")
    h3 = _mm(act, big["ffn_w_down"], "nn", nm + "ffn_down", add=h2)
    a3 = _rms_fwd(h3, small["norm_ple_g"][i], nm + "rms_ple")
    gp = _mm(a3, big["ple_w_gate"], "nn", nm + "ple_gate")
    pp = _mm(p_i, big["ple_w_proj"], "nt", nm + "ple_proj")
    h4 = _ple_fwd(h3, gp, pp, nm + "ple_add")
    sv.update(a1=a1, z=z, y_attn=y_attn, lse=lse, y_sgu=y_sgu, y_ssm=y_ssm, yc=yc, xr=xr, xi=xi, mix=mix, h2=h2,
              a2=a2, hu=hu, hv=hv, hg=hg, act=act, h3=h3, a3=a3, gp=gp, pp=pp, u=u)
    return h4, sv


def _layer_bwd(i, dh4, sv, p_i, big, small, bias_tabs):
    nm = "l%d_" % i
    g = {}
    dpp, dgp = _ple_bwd(dh4, sv["gp"], sv["pp"], nm + "ple_bwd")
    g["ple_w_proj"] = _mm(dpp, p_i, "tn", nm + "d_ple_proj", out_dtype=BF16)
    g["ple_w_gate"] = _mm(sv["a3"], dgp, "tn", nm + "d_ple_gate", out_dtype=BF16)
    da3 = _mm(dgp, big["ple_w_gate"], "nt", nm + "ple_gate_t")
    dh3, g["norm_ple_g"] = _rms_bwd(da3, sv["h3"], small["norm_ple_g"][i], dh4, nm + "rms_ple_bwd")
    g["ffn_w_down"] = _mm(sv["act"], dh3, "tn", nm + "d_ffn_down", out_dtype=BF16)
    dact = _mm(dh3, big["ffn_w_down"], "nt", nm + "ffn_down_t", out_dtype=BF16)
    dhu, g["ffn_conv_w"], dcb = _conv_bwd(dact, sv["hv"], sv["hg"], sv["hu"], big["ffn_conv_w"],
                                          nm + "ffn_conv_bwd")
    g["ffn_conv_b"] = dcb.reshape(2 * D_FF)
    g["ffn_w_up"] = _mm(dhu, sv["a2"], "tn", nm + "d_ffn_up", out_dtype=BF16)
    da2 = _mm(dhu, big["ffn_w_up"], "nn", nm + "ffn_up_t")
    dh2, g["norm_ffn_g"] = _rms_bwd(da2, sv["h2"], small["norm_ffn_g"][i], dh3, nm + "rms_ffn_bwd")
    g["w_out"] = _mm(sv["mix"], dh2, "tn", nm + "d_out_proj", out_dtype=BF16)
    dmix = _mm(dh2, big["w_out"], "nt", nm + "out_proj_t")
    dya, dysg, dyss, g["branch_norm_g"] = _mix_bwd(dmix, sv["y_attn"], sv["y_sgu"], sv["y_ssm"],
                                                   small["branch_norm_g"][i], nm + "mix_bwd")
    ssm_keys = ("ssm_a_re", "ssm_a_im", "ssm_log_dt", "ssm_b_re", "ssm_b_im", "ssm_c_re", "ssm_c_im")
    (ar, ai, bdt, cd), prep_vjp = jax.vjp(_ssm_prep, *[small[k][i] for k in ssm_keys])
    pr, pi = _powers(ar, ai)
    pw_rev = jnp.concatenate([pr[:, ::-1], -pi[:, ::-1]], axis=0)
    dy1, dgl, y2, dud, g["ssm_d"], g["ssm_glu_b"] = _ssm_post_bwd(
        dyss, sv["yc"], sv["z"], small["ssm_d"][i], big["ssm_glu_w"], small["ssm_glu_b"][i], nm + "ssm_post_bwd")
    g["ssm_glu_w"] = _mm(y2, dgl, "tn", nm + "d_ssm_glu", out_dtype=BF16)
    g_t = _mm(cd, dy1, "nt", nm + "ssm_cx_t")
    dcd = jnp.concatenate([_mm(sv["xr"], dy1, "nn", nm + "d_ssm_c_re"),
                           _mm(sv["xi"], dy1, "nn", nm + "d_ssm_c_im")], axis=0)
    lr, li, dar, dai = _scan_bwd(g_t, sv["xr"], sv["xi"], pw_rev, _doubling_tables(pr, pi, True),
                                 nm + "ssm_scan_bwd")
    u = sv["u"]
    dbdt = jnp.concatenate([_mm(lr, u, "nn", nm + "d_ssm_b_re"), _mm(li, u, "nn", nm + "d_ssm_b_im")], axis=0)
    du = _mm(lr, bdt[:SSM_NS], "tn", nm + "ssm_bu_t_re", add=dud)
    du = _mm(li, bdt[SSM_NS:], "tn", nm + "ssm_bu_t_im", add=du)
    for k, val in zip(ssm_keys, prep_vjp((dar, dai, dbdt, dcd))):
        g[k] = val
    bexp, bexp_vjp = jax.vjp(_sgu_bias_expand, small["sgu_b"][i])
    dzs, g["sgu_w"], dbexp, g["sgu_ln_g"], g["sgu_ln_b"] = _sgu_bwd(
        sv["z"], dysg, small["sgu_ln_g"][i], small["sgu_ln_b"][i], small["sgu_w"][i], bexp, nm + "sgu_bwd")
    g["sgu_b"] = bexp_vjp(dbexp)[0]
    prev = None
    dbs, extras = [], []
    for b, (_, dil) in enumerate(BRANCHES):
        dq, dk, dv, dkx, dvx, db = _attn_bwd(sv["z"], bias_tabs[b][1], dya, sv["y_attn"], sv["lse"], prev, dil,
                                             nm + "attn_bwd%d" % b)
        prev = (dq, dk, dv)
        extras.append((dkx, dvx))
        dbs.append(db.reshape(N_HEADS, BLK * 2 * BLK))
    dz = _assemble_dz(dq, dk, dv, extras, dzs, du, nm + "assemble_dz")
    g["w_in"] = _mm(dz, sv["a1"], "tn", nm + "d_in_proj", out_dtype=BF16)
    da1 = _mm(dz, big["w_in"], "nn", nm + "in_proj_t")
    dh, g["norm_attn_g"] = _rms_bwd(da1, sv["h"], small["norm_attn_g"][i], dh2, nm + "rms_attn_bwd")
    return dh, g, jnp.concatenate(dbs, axis=1)


def _local_step(x, p, target, layer_weights, small, layer_done=None):
    depth = p.shape[0]
    bias_tabs = [(_pair_bias_fwd(t), _pair_bias_bwd(t)) for t in _bias_tables(small["rel_bias"])]
    h = x
    saved, bigs = [], []
    for i in range(depth):
        bigs.append(layer_weights(i, h))
        h, sv = _layer_fwd(i, h, p[i], bigs[i], small, bias_tabs)
        saved.append(sv)
    dh, loss, g_final = _loss_head(h, target, small["final_norm_g"], "loss_head")
    layer_grads = [None] * depth
    dbias = [None] * depth
    for i in reversed(range(depth)):
        dh, layer_grads[i], dbias[i] = _layer_bwd(i, dh, saved[i], p[i], bigs[i], small, bias_tabs)
        if layer_done is not None:
            small = layer_done(i, layer_grads[i], small)
    big_grads = [{k: lg.pop(k) for k in COMM_NAMES} for lg in layer_grads]
    grads = {k: jnp.stack([layer_grads[i][k] for i in range(depth)]) for k in layer_grads[0]}
    grads["final_norm_g"] = g_final
    g_rb = _mm(sum(dbias[1:], dbias[0]), _bucket_onehot(), "nn", "d_rel_bias", tk=2048)
    grads["rel_bias"] = g_rb[:, :N_BUCKETS].T
    return loss, dh, big_grads, grads


_ANY = pl.BlockSpec(memory_space=pl.ANY)
MESH_IDS = pl.DeviceIdType.MESH


def _slot(ref, axis, j):
    return ref.at[(slice(None),) * axis + (j,)]


def _all_gather(blocks, axis, name):
    nt = len(blocks)

    def body(*refs):
        x_refs, o_refs = refs[:nt], refs[nt:2 * nt]
        send_sems, recv_sems, local_sems = refs[2 * nt:]
        x, y, c = lax.axis_index("x"), lax.axis_index("y"), lax.axis_index("c")
        me, sibling = (x, y, c), (x, y, 1 - c)
        chips = [(1 - x, y), (x, 1 - y), (1 - x, 1 - y)]

        def slot(t, px, py, pc):
            return _slot(o_refs[t], axis, 4 * px + 2 * py + pc)

        def copy(t, k, blk, to, src=None):
            return pltpu.make_async_remote_copy(
                src_ref=slot(t, *blk) if src is None else src, dst_ref=slot(t, *blk),
                send_sem=send_sems.at[7 * t + k], recv_sem=recv_sems.at[7 * t + k],
                device_id=to, device_id_type=MESH_IDS)

        mine = [pltpu.make_async_copy(x_refs[t], slot(t, *me), local_sems.at[t]) for t in range(nt)]
        for cp in mine:
            cp.start()
        first = []
        for t in range(nt):
            first.append(copy(t, 0, me, sibling, src=x_refs[t]))
            first += [copy(t, 1 + j, me, (*chip, c), src=x_refs[t]) for j, chip in enumerate(chips)]
        for cp in first:
            cp.start()
        passed = []
        for t in range(nt):
            for j, chip in enumerate(chips):
                copy(t, 1 + j, (*chip, c), me).wait_recv()
                passed.append(copy(t, 4 + j, (*chip, c), sibling))
                passed[-1].start()
        for t in range(nt):
            copy(t, 0, sibling, me).wait_recv()
            for j, chip in enumerate(chips):
                copy(t, 4 + j, (*chip, 1 - c), me).wait_recv()
        for cp in first + passed:
            cp.wait_send()
        for cp in mine:
            cp.wait()

    out_shape = [jax.ShapeDtypeStruct(b.shape[:axis] + (N_DEV,) + b.shape[axis:], b.dtype) for b in blocks]
    return pl.pallas_call(
        body, name=name, out_shape=out_shape, in_specs=[_ANY] * nt, out_specs=[_ANY] * nt,
        scratch_shapes=[pltpu.SemaphoreType.DMA((7 * nt,)), pltpu.SemaphoreType.DMA((7 * nt,)),
                        pltpu.SemaphoreType.DMA((nt,))],
    )(*blocks)


def _peer(k):
    x, y, c = lax.axis_index("x"), lax.axis_index("y"), lax.axis_index("c")
    px = 1 - x if k & 4 else x
    py = 1 - y if k & 2 else y
    pc = 1 - c if k & 1 else c
    return (px, py, pc), 4 * px + 2 * py + pc


def _all_to_all(blocks, name):
    nt = len(blocks)

    def body(*refs):
        x_refs, o_refs = refs[:nt], refs[nt:2 * nt]
        send_sems, recv_sems, local_sems = refs[2 * nt:]
        _, me = _peer(0)
        mine = [pltpu.make_async_copy(x_refs[t].at[me], o_refs[t].at[me], local_sems.at[t]) for t in range(nt)]
        for cp in mine:
            cp.start()
        copies = []
        for k in range(1, N_DEV):
            peer, idx = _peer(k)
            for t in range(nt):
                cp = pltpu.make_async_remote_copy(
                    src_ref=x_refs[t].at[idx], dst_ref=o_refs[t].at[me],
                    send_sem=send_sems.at[7 * t + k - 1], recv_sem=recv_sems.at[7 * t + k - 1],
                    device_id=peer, device_id_type=MESH_IDS)
                cp.start()
                copies.append(cp)
        for cp in copies:
            cp.wait()
        for cp in mine:
            cp.wait()

    return pl.pallas_call(
        body, name=name, out_shape=[jax.ShapeDtypeStruct(b.shape, b.dtype) for b in blocks],
        in_specs=[_ANY] * nt, out_specs=[_ANY] * nt,
        scratch_shapes=[pltpu.SemaphoreType.DMA((7 * nt,)), pltpu.SemaphoreType.DMA((7 * nt,)),
                        pltpu.SemaphoreType.DMA((nt,))],
    )(*blocks)


_HBM = pl.BlockSpec(memory_space=pltpu.HBM)
_SEM = pl.BlockSpec(memory_space=pltpu.SEMAPHORE)
_EFFECT = pltpu.SideEffectType.DATAFLOW_SIDE_EFFECTING


def _split_copy(src_ref, land_ref, send_sems, recv_sems, t, k, gather):
    peer, idx = _peer(k)
    _, me = _peer(0)
    return pltpu.make_async_remote_copy(
        src_ref=src_ref if gather else src_ref.at[idx], dst_ref=land_ref.at[me],
        send_sem=send_sems.at[7 * t + k - 1], recv_sem=recv_sems.at[7 * t + k - 1],
        device_id=peer, device_id_type=MESH_IDS)


def _exchange_start(srcs, lands, gather, name):
    nt = len(srcs)

    def body(*refs):
        src_refs, land_refs = refs[:nt], refs[nt:2 * nt]
        send_sems, recv_sems = refs[2 * nt:2 * nt + 2]
        token = refs[-1]
        for k in range(1, N_DEV):
            for t in range(nt):
                _split_copy(src_refs[t], land_refs[t], send_sems, recv_sems, t, k, gather).start()
        token[...] = jnp.zeros_like(token)

    hbm = lambda a: pltpu.HBM(a.shape, a.dtype)
    outs = pl.pallas_call(
        body, name=name,
        out_shape=(pltpu.SemaphoreType.DMA((7 * nt,)), pltpu.SemaphoreType.DMA((7 * nt,)),
                   *[hbm(a) for a in srcs], *[hbm(a) for a in lands], jax.ShapeDtypeStruct((8, 128), F32)),
        in_specs=[_HBM] * (2 * nt),
        out_specs=(_SEM, _SEM, *[_HBM] * (2 * nt), pl.BlockSpec(memory_space=pltpu.VMEM)),
        input_output_aliases={j: 2 + j for j in range(2 * nt)},
        compiler_params=pltpu.CompilerParams(has_side_effects=_EFFECT),
    )(*[pltpu.with_memory_space_constraint(a, pltpu.HBM) for a in list(srcs) + list(lands)])
    return outs[0], outs[1], outs[2:2 + nt], outs[2 + nt:2 + 2 * nt], outs[-1]


def _exchange_wait(send_sems, recv_sems, srcs, lands, after, gather, name):
    nt = len(srcs)

    def body(*refs):
        src_refs, land_refs = refs[:nt], refs[nt:2 * nt]
        send_sems, recv_sems = refs[2 * nt:2 * nt + 2]
        for k in range(1, N_DEV):
            _, idx = _peer(k)
            for t in range(nt):
                _split_copy(src_refs[t], land_refs[t], send_sems, recv_sems, t, k, gather).wait_send()
                arrival = pltpu.make_async_remote_copy(
                    src_ref=land_refs[t].at[idx], dst_ref=land_refs[t].at[idx],
                    send_sem=send_sems.at[7 * t + k - 1], recv_sem=recv_sems.at[7 * t + k - 1],
                    device_id=_peer(k)[0], device_id_type=MESH_IDS)
                arrival.wait_recv()

    hbm = lambda a: pltpu.HBM(a.shape, a.dtype)
    outs = pl.pallas_call(
        body, name=name, out_shape=tuple(hbm(a) for a in list(srcs) + list(lands)),
        in_specs=[_HBM] * (2 * nt) + [_SEM, _SEM, _ANY], out_specs=tuple([_HBM] * (2 * nt)),
        input_output_aliases={j: j for j in range(2 * nt)},
        compiler_params=pltpu.CompilerParams(has_side_effects=_EFFECT),
    )(*srcs, *lands, send_sems, recv_sems, after)
    return outs[nt:]


def _adamw(parts, w, m, v, name, tr):
    n_layers, r, c_ = w.shape
    assert len(parts) == n_layers

    def body(*refs):
        p_refs = refs[:n_layers]
        w_ref, m_ref, v_ref, g_ref, d_ref, mo_ref, vo_ref = refs[n_layers:]

        def update(p_ref):
            g = p_ref[0].astype(F32)
            for j in range(1, N_DEV):
                g = g + p_ref[j].astype(F32)
            m2 = ADAM_B1 * m_ref[...] + (1.0 - ADAM_B1) * g
            v2 = ADAM_B2 * v_ref[...] + (1.0 - ADAM_B2) * (g * g)
            m_hat = m2 / (1.0 - ADAM_B1 ** ADAM_STEP)
            v_hat = v2 / (1.0 - ADAM_B2 ** ADAM_STEP)
            g_ref[...] = g
            d_ref[...] = -ADAM_LR * (m_hat / (jnp.sqrt(v_hat) + ADAM_EPS) + ADAM_WD * w_ref[...])
            mo_ref[...] = m2
            vo_ref[...] = v2

        for layer in range(n_layers):
            pl.when(pl.program_id(0) == layer)(lambda layer=layer: update(p_refs[layer]))

    spec = pl.BlockSpec((None, tr, c_), lambda l, i: (l, i, 0))
    p_spec = pl.BlockSpec((N_DEV, tr, c_), lambda l, i: (0, i, 0))
    return pl.pallas_call(
        body, name=name, grid=(n_layers, r // tr), in_specs=[p_spec] * n_layers + [spec] * 3,
        out_specs=[spec] * 4, out_shape=[_sds((n_layers, r, c_))] * 4,
        compiler_params=_params(("parallel", "parallel")),
    )(*parts, w, m, v)


def _pack_rows(n_elems, align):
    rows = -(-n_elems // PACK_COLS)
    return -(-rows // align) * align


def _pack(arrs, rows, dtype=F32):
    flat = jnp.concatenate([a.reshape(-1) for a in arrs]).astype(dtype)
    return jnp.pad(flat, (0, rows * PACK_COLS - flat.shape[0])).reshape(rows, PACK_COLS)


def _unpack(pack, shapes):
    flat = pack.reshape(-1)
    out, off = [], 0
    for shp in shapes:
        size = int(np.prod(shp))
        out.append(flat[off:off + size].reshape(shp))
        off += size
    return out


def _tile_rows(rows, target, align=16):
    best = align
    for t in range(align, target + 1, align):
        if rows % t == 0:
            best = t
    return best


COMM_NAMES = ("w_in", "ssm_glu_w", "w_out", "ffn_w_up", "ffn_w_down", "ple_w_gate", "ple_w_proj")
COMM_TRANSPOSED = ("w_in", "ffn_w_up", "ple_w_proj")
SMALL_TILE_ROWS = 64
CONV_NAME = "ffn_conv_w"


def _to_comm(name, a):
    return jnp.swapaxes(a, 1, 2) if name in COMM_TRANSPOSED else a


def kernel(x, p, rel_bias, norm_attn_g, w_in, sgu_ln_g, sgu_ln_b, sgu_w, sgu_b, ssm_a_re, ssm_a_im, ssm_log_dt, ssm_b_re, ssm_b_im, ssm_c_re, ssm_c_im, ssm_d, ssm_glu_w, ssm_glu_b, branch_norm_g, w_out, norm_ffn_g, ffn_w_up, ffn_conv_w, ffn_conv_b, ffn_w_down, norm_ple_g, ple_w_gate, ple_w_proj, final_norm_g, loss_target, m_rel_bias, m_norm_attn_g, m_w_in, m_sgu_ln_g, m_sgu_ln_b, m_sgu_w, m_sgu_b, m_ssm_a_re, m_ssm_a_im, m_ssm_log_dt, m_ssm_b_re, m_ssm_b_im, m_ssm_c_re, m_ssm_c_im, m_ssm_d, m_ssm_glu_w, m_ssm_glu_b, m_branch_norm_g, m_w_out, m_norm_ffn_g, m_ffn_w_up, m_ffn_conv_w, m_ffn_conv_b, m_ffn_w_down, m_norm_ple_g, m_ple_w_gate, m_ple_w_proj, m_final_norm_g, v_rel_bias, v_norm_attn_g, v_w_in, v_sgu_ln_g, v_sgu_ln_b, v_sgu_w, v_sgu_b, v_ssm_a_re, v_ssm_a_im, v_ssm_log_dt, v_ssm_b_re, v_ssm_b_im, v_ssm_c_re, v_ssm_c_im, v_ssm_d, v_ssm_glu_w, v_ssm_glu_b, v_branch_norm_g, v_w_out, v_norm_ffn_g, v_ffn_w_up, v_ffn_conv_w, v_ffn_conv_b, v_ffn_w_down, v_norm_ple_g, v_ple_w_gate, v_ple_w_proj, v_final_norm_g):
    given = dict(locals())
    w = {n: given[n] for n in WEIGHT_NAMES}
    m = {n: given["m_" + n] for n in WEIGHT_NAMES}
    v = {n: given["v_" + n] for n in WEIGHT_NAMES}
    depth = p.shape[0]
    dev = 4 * lax.axis_index("x") + 2 * lax.axis_index("y") + lax.axis_index("c")

    wc = {n: _to_comm(n, w[n]) for n in COMM_NAMES}
    wb = {n: wc[n].astype(BF16) for n in COMM_NAMES}
    conv_local = [w[CONV_NAME], m[CONV_NAME], v[CONV_NAME]]
    conv_rows = _pack_rows(sum(int(np.prod(t.shape)) for t in conv_local), 8)
    conv_g, = _all_gather([_pack(conv_local, conv_rows)], 0, "gather_conv_taps")
    conv_parts = zip(*[_unpack(conv_g[j], [t.shape for t in conv_local]) for j in range(N_DEV)])
    conv_w, conv_m, conv_v = [jnp.concatenate(parts, axis=2) for parts in conv_parts]
    small = {n: w[n] for n in SMALL_NAMES}

    def whole(blocks):
        return {n: t.reshape(-1, t.shape[-1]) for n, t in zip(COMM_NAMES, blocks)}

    def own_slot(block):
        return lax.dynamic_update_slice_in_dim(jnp.zeros((N_DEV,) + block.shape, block.dtype), block[None], dev, 0)

    first = _all_gather([wb[n][0] for n in COMM_NAMES], 0, "gather_weights_0")
    in_flight = {}
    for i in range(1, depth):
        srcs, first = lax.optimization_barrier(([wb[n][i] for n in COMM_NAMES], first))
        in_flight[i] = _exchange_start(srcs, [own_slot(s) for s in srcs], True, "gather_weights_%d_start" % i)
        small["norm_attn_g"] = small["norm_attn_g"] + in_flight[i][4][0, 0]

    def layer_weights(i, h):
        if i == 0:
            got = whole(first)
        else:
            send_sems, recv_sems, srcs, lands, _ = in_flight.pop(i)
            got = whole(_exchange_wait(send_sems, recv_sems, srcs, lands, h, True, "gather_weights_%d_wait" % i))
        return dict(got, **{CONV_NAME: conv_w[i]})

    def as_slots(g, n):
        return g.reshape((N_DEV,) + wc[n].shape[1:])

    def layer_done(i, g, small_now):
        if i == 0:
            return small_now
        srcs = [as_slots(g[n], n) for n in COMM_NAMES]
        lands = [own_slot(lax.dynamic_index_in_dim(s, dev, 0, keepdims=False)) for s in srcs]
        in_flight[i] = _exchange_start(srcs, lands, False, "scatter_weight_grads_%d_start" % i)
        return dict(small_now, norm_ple_g=small_now["norm_ple_g"] + in_flight[i][4][0, 0])

    loss, dx, big_grads, grads = _local_step(x[0], p[:, 0], loss_target[0], layer_weights, small, layer_done)
    loss = lax.psum(loss, ("x", "y", "c"))

    recv = [_all_to_all([as_slots(big_grads[0][n], n) for n in COMM_NAMES], "scatter_weight_grads_0")]
    for i in range(1, depth):
        send_sems, recv_sems, srcs, lands, _ = in_flight.pop(i)
        recv.append(_exchange_wait(send_sems, recv_sems, srcs, lands, dx, False, "scatter_weight_grads_%d_wait" % i))
    rep_names = SMALL_NAMES + (CONV_NAME,)
    rep_w = dict({n: w[n] for n in SMALL_NAMES}, **{CONV_NAME: conv_w})
    rep_m = dict({n: m[n] for n in SMALL_NAMES}, **{CONV_NAME: conv_m})
    rep_v = dict({n: v[n] for n in SMALL_NAMES}, **{CONV_NAME: conv_v})
    rep_shapes = [rep_w[n].shape for n in rep_names]
    rep_rows = _pack_rows(sum(int(np.prod(s)) for s in rep_shapes), SMALL_TILE_ROWS)
    rep_parts, = _all_gather([_pack([grads[n] for n in rep_names], rep_rows)], 0, "gather_small_grads")

    out = {}
    for t, n in enumerate(COMM_NAMES):
        res = _adamw([recv[i][t] for i in range(depth)], wc[n], _to_comm(n, m[n]), _to_comm(n, v[n]),
                     "adamw_" + n, _tile_rows(wc[n].shape[1], 256))
        out[n] = [_to_comm(n, r) for r in res]
    rep_out = _adamw([rep_parts], *[_pack([src[n] for n in rep_names], rep_rows)[None] for src in (rep_w, rep_m, rep_v)],
                     "adamw_replicated", SMALL_TILE_ROWS)
    for n, vals in zip(rep_names, zip(*[_unpack(r[0], rep_shapes) for r in rep_out])):
        out[n] = list(vals)
    shard = ffn_conv_w.shape[2]
    out[CONV_NAME] = [lax.dynamic_slice_in_dim(t, dev * shard, shard, axis=2) for t in out[CONV_NAME]]
    results = [[out[n][kind] for n in WEIGHT_NAMES] for kind in range(4)]
    return (loss, dx[None], *results[0], *results[1], *results[2], *results[3])
```

```python
import math

import numpy as np
import jax
import jax.numpy as jnp
from jax import lax
from jax.experimental import pallas as pl
from jax.experimental.pallas import tpu as pltpu

F32 = jnp.float32
BF16 = jnp.bfloat16

D_MODEL = 1024
HEAD_DIM = 64
N_HEADS = 8
ATTN_W = 512
SGU_W = 256
SGU_GROUPS = 4
SGU_CHUNK = 128
SSM_W = 256
SSM_GROUPS = 16
SSM_CH = 16
SSM_STATE = 64
SSM_NS = SSM_GROUPS * SSM_STATE
IN_W = 2304
D_FF = 2816
PLE_DIM = 256
BRANCHES = ((128, 1), (512, 4), (2048, 16))
BLK = 128
N_BUCKETS = 32
REL_MAX = 2048
EPS = 1e-6
NEG_INF = -1e30
N_DEV = 8

ADAM_LR = 0.001
ADAM_B1 = 0.9
ADAM_B2 = 0.999
ADAM_EPS = 1e-08
ADAM_WD = 0.01
ADAM_STEP = 10

VMEM_LIMIT_BYTES = 56 * 1024 * 1024
GELU_C = math.sqrt(2.0 / math.pi)

SMALL_NAMES = ("rel_bias", "norm_attn_g", "sgu_ln_g", "sgu_ln_b", "sgu_w", "sgu_b", "ssm_a_re", "ssm_a_im",
               "ssm_log_dt", "ssm_b_re", "ssm_b_im", "ssm_c_re", "ssm_c_im", "ssm_d", "ssm_glu_b",
               "branch_norm_g", "norm_ffn_g", "ffn_conv_b", "norm_ple_g", "final_norm_g")
WEIGHT_NAMES = ("rel_bias", "norm_attn_g", "w_in", "sgu_ln_g", "sgu_ln_b", "sgu_w", "sgu_b", "ssm_a_re",
                "ssm_a_im", "ssm_log_dt", "ssm_b_re", "ssm_b_im", "ssm_c_re", "ssm_c_im", "ssm_d", "ssm_glu_w",
                "ssm_glu_b", "branch_norm_g", "w_out", "norm_ffn_g", "ffn_w_up", "ffn_conv_w", "ffn_conv_b",
                "ffn_w_down", "norm_ple_g", "ple_w_gate", "ple_w_proj", "final_norm_g")
PACK_COLS = 512


def _params(sem):
    return pltpu.CompilerParams(dimension_semantics=sem, vmem_limit_bytes=VMEM_LIMIT_BYTES)


def _pick(dim, target):
    if dim <= target:
        return dim
    best = None
    for t in range(128, target + 1, 128):
        if dim % t == 0:
            best = t
    return dim if best is None else best


def _gelu(x):
    return 0.5 * x * (1.0 + jnp.tanh(GELU_C * (x + 0.044715 * (x * x * x))))


def _gelu_grad(x):
    t = jnp.tanh(GELU_C * (x + 0.044715 * (x * x * x)))
    return 0.5 * (1.0 + t) + 0.5 * x * (1.0 - t * t) * (GELU_C * (1.0 + 3.0 * 0.044715 * (x * x)))


def _sigmoid(x):
    return 1.0 / (1.0 + jnp.exp(-x))


_DIMS = {"nn": (((1,), (0,)), ((), ())), "tn": (((0,), (0,)), ((), ())), "nt": (((1,), (1,)), ((), ()))}


def _mm(a, b, mode, name, add=None, out_dtype=F32, a_cols=None, b_cols=None, tm=1408, tn=1408, tk=1408):
    if mode == "nn":
        m, k = a.shape
        k2, n = b.shape
    elif mode == "tn":
        k, m = a.shape
        k2, n = b.shape
    else:
        m, k = a.shape
        n, k2 = b.shape
    if a_cols is not None:
        assert mode != "tn"
        k = a_cols[1]
    if b_cols is not None:
        assert mode != "nt"
        n = b_cols[1]
    assert k == k2, (name, a.shape, b.shape, mode)
    tm, tn, tk = _pick(m, tm), _pick(n, tn), _pick(k, tk)
    nk = k // tk
    a_off = 0 if a_cols is None else a_cols[0] // tk
    b_off = 0 if b_cols is None else b_cols[0] // tn
    assert a_cols is None or a_cols[0] % tk == 0
    assert b_cols is None or b_cols[0] % tn == 0
    dims = _DIMS[mode]
    has_add = add is not None

    def body(*refs):
        if has_add:
            a_ref, b_ref, add_ref, o_ref = refs[:4]
        else:
            a_ref, b_ref, o_ref = refs[:3]
        part = lax.dot_general(a_ref[...].astype(BF16), b_ref[...].astype(BF16), dims,
                               preferred_element_type=F32)

        def finish(r):
            if has_add:
                r = r + add_ref[...]
            o_ref[...] = r.astype(out_dtype)

        if nk == 1:
            finish(part)
            return
        acc_ref = refs[-1]
        kk = pl.program_id(2)

        @pl.when(kk == 0)
        def _():
            acc_ref[...] = part

        @pl.when((kk > 0) & (kk < nk - 1))
        def _():
            acc_ref[...] += part

        @pl.when(kk == nk - 1)
        def _():
            finish(acc_ref[...] + part)

    if mode == "tn":
        a_spec = pl.BlockSpec((tk, tm), lambda i, j, kk: (kk, i))
    else:
        a_spec = pl.BlockSpec((tm, tk), lambda i, j, kk: (i, kk + a_off))
    if mode == "nt":
        b_spec = pl.BlockSpec((tn, tk), lambda i, j, kk: (j, kk))
    else:
        b_spec = pl.BlockSpec((tk, tn), lambda i, j, kk: (kk, j + b_off))
    o_spec = pl.BlockSpec((tm, tn), lambda i, j, kk: (i, j))
    in_specs = [a_spec, b_spec] + ([o_spec] if has_add else [])
    args = (a, b) + ((add,) if has_add else ())
    return pl.pallas_call(
        body, name=name, grid=(m // tm, n // tn, nk),
        in_specs=in_specs, out_specs=o_spec,
        out_shape=jax.ShapeDtypeStruct((m, n), out_dtype),
        scratch_shapes=[pltpu.VMEM((tm, tn), F32)] if nk > 1 else [],
        compiler_params=_params(("parallel", "parallel", "arbitrary")),
    )(*args)


def _rb(tm, w, cb=0):
    return pl.BlockSpec((tm, w), lambda i: (i, cb))


def _fb(shape):
    nd = len(shape)
    return pl.BlockSpec(shape, lambda i: (0,) * nd)


def _rowcall(body, name, n_rows, tm, in_specs, args, out_specs, out_shapes):
    return pl.pallas_call(
        body, name=name, grid=(n_rows // tm,), in_specs=in_specs, out_specs=out_specs, out_shape=out_shapes,
        compiler_params=_params(("arbitrary",)),
    )(*args)


def _sds(shape, dtype=F32):
    return jax.ShapeDtypeStruct(shape, dtype)


def _rms_fwd(h, g, name, tm=512):
    s, d = h.shape

    def body(h_ref, g_ref, o_ref):
        x = h_ref[...]
        r = lax.rsqrt(jnp.mean(x * x, axis=-1, keepdims=True) + EPS)
        o_ref[...] = (x * r * g_ref[...]).astype(BF16)

    return _rowcall(body, name, s, tm, [_rb(tm, d), _fb((1, d))], (h, g.reshape(1, d)), _rb(tm, d),
                    _sds((s, d), BF16))


def _rms_bwd(da, h, g, dres, name, tm=512):
    s, d = h.shape

    def body(da_ref, h_ref, g_ref, dres_ref, dh_ref, dg_ref):
        @pl.when(pl.program_id(0) == 0)
        def _():
            dg_ref[...] = jnp.zeros_like(dg_ref)

        x = h_ref[...]
        r = lax.rsqrt(jnp.mean(x * x, axis=-1, keepdims=True) + EPS)
        xh = x * r
        dy = da_ref[...]
        dg_ref[...] += jnp.sum(dy * xh, axis=0, keepdims=True)
        dxh = dy * g_ref[...]
        dh_ref[...] = dres_ref[...] + r * (dxh - xh * jnp.mean(dxh * xh, axis=-1, keepdims=True))

    dh, dg = _rowcall(body, name, s, tm, [_rb(tm, d), _rb(tm, d), _fb((1, d)), _rb(tm, d)],
                      (da, h, g.reshape(1, d), dres), [_rb(tm, d), _fb((1, d))], [_sds((s, d)), _sds((1, d))])
    return dh, dg.reshape(d)


def _loss_head(h, target, g, name, tm=512):
    s, d = h.shape

    def body(h_ref, t_ref, g_ref, dh_ref, loss_ref, dg_ref):
        @pl.when(pl.program_id(0) == 0)
        def _():
            dg_ref[...] = jnp.zeros_like(dg_ref)
            loss_ref[...] = jnp.zeros_like(loss_ref)

        x = h_ref[...]
        r = lax.rsqrt(jnp.mean(x * x, axis=-1, keepdims=True) + EPS)
        xh = x * r
        gg = g_ref[...]
        err = xh * gg - t_ref[...]
        loss_ref[...] += jnp.sum(err * err) * (0.5 / d)
        dy = err * (1.0 / d)
        dg_ref[...] += jnp.sum(dy * xh, axis=0, keepdims=True)
        dxh = dy * gg
        dh_ref[...] = r * (dxh - xh * jnp.mean(dxh * xh, axis=-1, keepdims=True))

    dh, loss, dg = _rowcall(body, name, s, tm, [_rb(tm, d), _rb(tm, d), _fb((1, d))], (h, target, g.reshape(1, d)),
                            [_rb(tm, d), _fb((1, 128)), _fb((1, d))], [_sds((s, d)), _sds((1, 128)), _sds((1, d))])
    return dh, loss[0, 0], dg.reshape(d)


_MIX_PARTS = ((0, 512), (512, 768), (768, 1024))


def _mix_fwd(ya, ysg, yss, g, name, tm=512):
    s = ya.shape[0]

    def body(a_ref, b_ref, c_ref, g_ref, o_ref):
        for ref, (lo, hi) in zip((a_ref, b_ref, c_ref), _MIX_PARTS):
            y = ref[...]
            r = lax.rsqrt(jnp.mean(y * y, axis=-1, keepdims=True) + EPS)
            o_ref[:, lo:hi] = (y * r * g_ref[:, lo:hi]).astype(BF16)

    return _rowcall(body, name, s, tm, [_rb(tm, 512), _rb(tm, 256), _rb(tm, 256), _fb((1, 1024))],
                    (ya, ysg, yss, g.reshape(1, 1024)), _rb(tm, 1024), _sds((s, 1024), BF16))


def _mix_bwd(dmix, ya, ysg, yss, g, name, tm=512):
    s = ya.shape[0]

    def body(dm_ref, a_ref, b_ref, c_ref, g_ref, da_ref, db_ref, dc_ref, dg_ref):
        @pl.when(pl.program_id(0) == 0)
        def _():
            dg_ref[...] = jnp.zeros_like(dg_ref)

        for ref, dref, (lo, hi) in zip((a_ref, b_ref, c_ref), (da_ref, db_ref, dc_ref), _MIX_PARTS):
            y = ref[...]
            r = lax.rsqrt(jnp.mean(y * y, axis=-1, keepdims=True) + EPS)
            xh = y * r
            dm = dm_ref[:, lo:hi]
            dg_ref[:, lo:hi] += jnp.sum(dm * xh, axis=0, keepdims=True)
            dxh = dm * g_ref[:, lo:hi]
            dref[...] = r * (dxh - xh * jnp.mean(dxh * xh, axis=-1, keepdims=True))

    da, db, dc, dg = _rowcall(
        body, name, s, tm, [_rb(tm, 1024), _rb(tm, 512), _rb(tm, 256), _rb(tm, 256), _fb((1, 1024))],
        (dmix, ya, ysg, yss, g.reshape(1, 1024)),
        [_rb(tm, 512), _rb(tm, 256), _rb(tm, 256), _fb((1, 1024))],
        [_sds((s, 512)), _sds((s, 256)), _sds((s, 256)), _sds((1, 1024))])
    return da, db, dc, dg.reshape(1024)


def _ssm_post_fwd(yc, z, d, gw, gb, name, tm=1024):
    s = yc.shape[0]

    def body(yc_ref, u_ref, d_ref, gw_ref, gb_ref, o_ref):
        y1 = yc_ref[...] + d_ref[...] * u_ref[...]
        y2 = _gelu(y1)
        gl = jnp.dot(y2.astype(BF16), gw_ref[...], preferred_element_type=F32) + gb_ref[...]
        o_ref[...] = y2 * _sigmoid(gl)

    return _rowcall(body, name, s, tm, [_rb(tm, 256), _rb(tm, 256, 8), _fb((1, 256)), _fb((256, 256)), _fb((1, 256))],
                    (yc, z, d.reshape(1, 256), gw, gb.reshape(1, 256)), _rb(tm, 256), _sds((s, 256)))


def _ssm_post_bwd(dy, yc, z, d, gw, gb, name, tm=1024):
    s = yc.shape[0]

    def body(dy_ref, yc_ref, u_ref, d_ref, gw_ref, gb_ref, dy1_ref, dgl_ref, y2_ref, dud_ref, dd_ref, dgb_ref):
        @pl.when(pl.program_id(0) == 0)
        def _():
            dd_ref[...] = jnp.zeros_like(dd_ref)
            dgb_ref[...] = jnp.zeros_like(dgb_ref)

        u = u_ref[...]
        dd = d_ref[...]
        y1 = yc_ref[...] + dd * u
        y2 = _gelu(y1)
        gw_v = gw_ref[...]
        gl = jnp.dot(y2.astype(BF16), gw_v, preferred_element_type=F32) + gb_ref[...]
        sg = _sigmoid(gl)
        dyv = dy_ref[...]
        dgl = dyv * y2 * sg * (1.0 - sg)
        dy2 = dyv * sg + lax.dot_general(dgl.astype(BF16), gw_v, _DIMS["nt"], preferred_element_type=F32)
        dy1 = dy2 * _gelu_grad(y1)
        dy1_ref[...] = dy1.astype(BF16)
        dgl_ref[...] = dgl.astype(BF16)
        y2_ref[...] = y2.astype(BF16)
        dud_ref[...] = dy1 * dd
        dd_ref[...] += jnp.sum(dy1 * u, axis=0, keepdims=True)
        dgb_ref[...] += jnp.sum(dgl, axis=0, keepdims=True)

    outs = _rowcall(
        body, name, s, tm,
        [_rb(tm, 256), _rb(tm, 256), _rb(tm, 256, 8), _fb((1, 256)), _fb((256, 256)), _fb((1, 256))],
        (dy, yc, z, d.reshape(1, 256), gw, gb.reshape(1, 256)),
        [_rb(tm, 256)] * 4 + [_fb((1, 256))] * 2,
        [_sds((s, 256), BF16)] * 3 + [_sds((s, 256))] + [_sds((1, 256))] * 2)
    dy1, dgl, y2, dud, dd, dgb = outs
    return dy1, dgl, y2, dud, dd.reshape(256), dgb.reshape(256)


SCAN_T = 512
SCAN_C = 512
N_SCAN_TABLES = 6


def _scan_tables(pr, pi, reverse):
    ns = pr.shape[0]
    sign = -1.0 if reverse else 1.0
    power = [(jnp.ones((ns,), F32), jnp.zeros((ns,), F32))] + [(pr[:, k], sign * pi[:, k]) for k in range(8)]
    zero = (jnp.zeros((ns,), F32), jnp.zeros((ns,), F32))

    def table(exponents):
        rows = [zero if e is None else power[e] for e in exponents]
        return jnp.stack([jnp.concatenate(row) for row in rows])

    tabs = []
    for k in (1, 2, 4):
        has_partner = [(s < 8 - k) if reverse else (s >= k) for s in range(8)]
        tabs.append(table([k if ok else None for ok in has_partner]))
    tabs.append(table([s if reverse else 7 - s for s in range(8)]))
    tabs.append(table([8 - s if reverse else s + 1 for s in range(8)]))
    tabs.append(table([8] * 8))
    return jnp.stack(tabs)


def _cmul(ar, ai, br, bi):
    return ar * br - ai * bi, ar * bi + ai * br


def _scan_group(ur, ui, cr, ci, tr_ref, ti_ref, reverse):
    xr, xi = ur, ui
    for n, k in enumerate((1, 2, 4)):
        shift = 8 - k if reverse else k
        pr, pi = _cmul(tr_ref[n], ti_ref[n], pltpu.roll(xr, shift, axis=0), pltpu.roll(xi, shift, axis=0))
        xr, xi = xr + pr, xi + pi
    sr, si = _cmul(tr_ref[3], ti_ref[3], ur, ui)
    for k in (1, 2, 4):
        sr, si = sr + pltpu.roll(sr, k, axis=0), si + pltpu.roll(si, k, axis=0)
    pr, pi = _cmul(tr_ref[4], ti_ref[4], cr, ci)
    nr, ni = _cmul(tr_ref[5], ti_ref[5], cr, ci)
    return xr + pr, xi + pi, nr + sr, ni + si


def _scan_specs(ns, n_t, reverse):
    ncb = ns // SCAN_C
    tmap = (lambda t: n_t - 1 - t) if reverse else (lambda t: t)
    re = pl.BlockSpec((SCAN_T, SCAN_C), lambda j, t: (tmap(t), j))
    im = pl.BlockSpec((SCAN_T, SCAN_C), lambda j, t: (tmap(t), j + ncb))
    tab_re = pl.BlockSpec((N_SCAN_TABLES, 8, SCAN_C), lambda j, t: (0, 0, j))
    tab_im = pl.BlockSpec((N_SCAN_TABLES, 8, SCAN_C), lambda j, t: (0, 0, j + ncb))
    return ncb, tmap, re, im, tab_re, tab_im


def _scan_fwd(bu, tabs, name):
    s, two_ns = bu.shape
    ns = two_ns // 2
    n_t = s // SCAN_T
    ncb, _, re, im, tab_re, tab_im = _scan_specs(ns, n_t, False)

    def body(ur_ref, ui_ref, tr_ref, ti_ref, xr_ref, xi_ref, cr_ref, ci_ref):
        @pl.when(pl.program_id(1) == 0)
        def _():
            cr_ref[...] = jnp.zeros_like(cr_ref)
            ci_ref[...] = jnp.zeros_like(ci_ref)

        def group(g, carry):
            rows = pl.ds(pl.multiple_of(g * 8, 8), 8)
            xr, xi, cr, ci = _scan_group(ur_ref[rows, :], ui_ref[rows, :], *carry, tr_ref, ti_ref, False)
            xr_ref[rows, :] = xr
            xi_ref[rows, :] = xi
            return cr, ci

        cr, ci = lax.fori_loop(0, SCAN_T // 8, group, (cr_ref[...], ci_ref[...]), unroll=2)
        cr_ref[...] = cr
        ci_ref[...] = ci

    out = pl.BlockSpec((SCAN_T, SCAN_C), lambda j, t: (t, j))
    return pl.pallas_call(
        body, name=name, grid=(ncb, n_t), in_specs=[re, im, tab_re, tab_im], out_specs=[out, out],
        out_shape=[_sds((s, ns)), _sds((s, ns))], scratch_shapes=[pltpu.VMEM((8, SCAN_C), F32)] * 2,
        compiler_params=_params(("parallel", "arbitrary")),
    )(bu, bu, tabs, tabs)


def _scan_bwd(g, xr, xi, tabs, name):
    s, two_ns = g.shape
    ns = two_ns // 2
    n_t = s // SCAN_T
    ncb, tmap, re, im, tab_re, tab_im = _scan_specs(ns, n_t, True)
    n_groups = SCAN_T // 8

    def body(gr_ref, gi_ref, tr_ref, ti_ref, xr_ref, xi_ref, pxr_ref, pxi_ref,
             lr_ref, li_ref, dar_ref, dai_ref, cr_ref, ci_ref, ar_ref, ai_ref, sxr_ref, sxi_ref):
        t = pl.program_id(1)

        @pl.when(t == 0)
        def _():
            for ref in (cr_ref, ci_ref, ar_ref, ai_ref):
                ref[...] = jnp.zeros_like(ref)

        has_before = (t < n_t - 1).astype(F32)
        sxr_ref[0:8, :] = pxr_ref[...] * has_before
        sxi_ref[0:8, :] = pxi_ref[...] * has_before
        sxr_ref[8:, :] = xr_ref[...]
        sxi_ref[8:, :] = xi_ref[...]
        first_row = lax.broadcasted_iota(jnp.int32, (8, SCAN_C), 0) == 0

        def group(k, carry):
            cr, ci, ar, ai = carry
            g8 = pl.multiple_of((n_groups - 1 - k) * 8, 8)
            rows = pl.ds(g8, 8)
            lr, li, cr, ci = _scan_group(gr_ref[rows, :], gi_ref[rows, :], cr, ci, tr_ref, ti_ref, True)
            lr_ref[rows, :] = lr
            li_ref[rows, :] = li
            here, before = pl.ds(g8 + 8, 8), rows
            pr = jnp.where(first_row, pltpu.roll(sxr_ref[before, :], 1, axis=0), pltpu.roll(sxr_ref[here, :], 1, axis=0))
            pi = jnp.where(first_row, pltpu.roll(sxi_ref[before, :], 1, axis=0), pltpu.roll(sxi_ref[here, :], 1, axis=0))
            return cr, ci, ar + lr * pr + li * pi, ai + li * pr - lr * pi

        cr, ci, ar, ai = lax.fori_loop(0, n_groups, group,
                                       (cr_ref[...], ci_ref[...], ar_ref[...], ai_ref[...]), unroll=2)
        cr_ref[...] = cr
        ci_ref[...] = ci
        ar_ref[...] = ar
        ai_ref[...] = ai

        @pl.when(t == n_t - 1)
        def _():
            for k in (1, 2, 4):
                ar_ref[...] += pltpu.roll(ar_ref[...], k, axis=0)
                ai_ref[...] += pltpu.roll(ai_ref[...], k, axis=0)
            dar_ref[...] = ar_ref[...]
            dai_ref[...] = ai_ref[...]

    x_spec = pl.BlockSpec((SCAN_T, SCAN_C), lambda j, t: (tmap(t), j))
    before_spec = pl.BlockSpec((8, SCAN_C), lambda j, t: (jnp.maximum(tmap(t) * (SCAN_T // 8) - 1, 0), j))
    acc_spec = pl.BlockSpec((8, SCAN_C), lambda j, t: (0, j))
    lr, li, dar, dai = pl.pallas_call(
        body, name=name, grid=(ncb, n_t),
        in_specs=[re, im, tab_re, tab_im, x_spec, x_spec, before_spec, before_spec],
        out_specs=[x_spec, x_spec, acc_spec, acc_spec],
        out_shape=[_sds((s, ns)), _sds((s, ns)), _sds((8, ns)), _sds((8, ns))],
        scratch_shapes=[pltpu.VMEM((8, SCAN_C), F32)] * 4 + [pltpu.VMEM((SCAN_T + 8, SCAN_C), F32)] * 2,
        compiler_params=_params(("parallel", "arbitrary")),
    )(g, g, tabs, tabs, xr, xi, xr, xi)
    return lr, li, dar[0], dai[0]


def _group_ids():
    return lax.broadcasted_iota(jnp.int32, (1, SGU_W), 1) // 64


def _group_mean(val, gid):
    out = jnp.zeros_like(val)
    for g in range(SGU_GROUPS):
        mg = gid == g
        out = jnp.where(mg, jnp.sum(jnp.where(mg, val, 0.0), axis=1, keepdims=True) * (1.0 / 64), out)
    return out


def _causal_w(w_ref, g):
    t = lax.broadcasted_iota(jnp.int32, (SGU_CHUNK, SGU_CHUNK), 0)
    s = lax.broadcasted_iota(jnp.int32, (SGU_CHUNK, SGU_CHUNK), 1)
    return jnp.where(t >= s, w_ref[g], 0.0).astype(BF16)


def _sgu_core(x, lng, lnb, w_ref, bexp, gid):
    zz = _gelu(x)
    u = zz[:, :SGU_W]
    v = zz[:, SGU_W:]
    vc = v - _group_mean(v, gid)
    rstd = lax.rsqrt(_group_mean(vc * vc, gid) + EPS)
    vhat = vc * rstd
    vn = vhat * lng + lnb
    vnb = vn.astype(BF16)
    mixed = bexp
    for g in range(SGU_GROUPS):
        mm = jnp.dot(_causal_w(w_ref, g), vnb, preferred_element_type=F32)
        mixed = jnp.where(gid == g, mm + bexp, mixed)
    return u, rstd, vhat, vnb, mixed


def _sgu_fwd(z, lng, lnb, w, bexp, name, tm=512):
    s = z.shape[0]

    def body(z_ref, lng_ref, lnb_ref, w_ref, b_ref, o_ref):
        gid = _group_ids()
        for j in range(tm // SGU_CHUNK):
            rows = pl.ds(j * SGU_CHUNK, SGU_CHUNK)
            u, _, _, _, mixed = _sgu_core(z_ref[rows, :], lng_ref[...], lnb_ref[...], w_ref, b_ref[...], gid)
            o_ref[rows, :] = u * mixed

    return _rowcall(body, name, s, tm,
                    [_rb(tm, 512, 3), _fb((1, 256)), _fb((1, 256)), _fb((4, 128, 128)), _fb((128, 256))],
                    (z, lng.reshape(1, 256), lnb.reshape(1, 256), w, bexp), _rb(tm, 256), _sds((s, 256)))


def _sgu_bwd(z, dy, lng, lnb, w, bexp, name, tm=512):
    s = z.shape[0]

    def body(z_ref, dy_ref, lng_ref, lnb_ref, w_ref, b_ref, dz_ref, dw_ref, db_ref, dlng_ref, dlnb_ref):
        @pl.when(pl.program_id(0) == 0)
        def _():
            dw_ref[...] = jnp.zeros_like(dw_ref)
            db_ref[...] = jnp.zeros_like(db_ref)
            dlng_ref[...] = jnp.zeros_like(dlng_ref)
            dlnb_ref[...] = jnp.zeros_like(dlnb_ref)

        gid = _group_ids()
        t = lax.broadcasted_iota(jnp.int32, (SGU_CHUNK, SGU_CHUNK), 0)
        sidx = lax.broadcasted_iota(jnp.int32, (SGU_CHUNK, SGU_CHUNK), 1)
        lng_v = lng_ref[...]
        for j in range(tm // SGU_CHUNK):
            rows = pl.ds(j * SGU_CHUNK, SGU_CHUNK)
            x = z_ref[rows, :]
            u, rstd, vhat, vnb, mixed = _sgu_core(x, lng_v, lnb_ref[...], w_ref, b_ref[...], gid)
            dyv = dy_ref[rows, :]
            dmixed = dyv * u
            du = dyv * mixed
            db_ref[...] += dmixed
            dvn = jnp.zeros_like(dmixed)
            for g in range(SGU_GROUPS):
                dmg = jnp.where(gid == g, dmixed, 0.0).astype(BF16)
                dvn = dvn + lax.dot_general(_causal_w(w_ref, g), dmg, _DIMS["tn"], preferred_element_type=F32)
                dwg = lax.dot_general(dmg, vnb, _DIMS["nt"], preferred_element_type=F32)
                dw_ref[g] += jnp.where(t >= sidx, dwg, 0.0)
            dlnb_ref[...] += jnp.sum(dvn, axis=0, keepdims=True)
            dlng_ref[...] += jnp.sum(dvn * vhat, axis=0, keepdims=True)
            dvh = dvn * lng_v
            dv = rstd * (dvh - _group_mean(dvh, gid) - vhat * _group_mean(dvh * vhat, gid))
            gg = _gelu_grad(x)
            dz_ref[rows, 0:SGU_W] = du * gg[:, :SGU_W]
            dz_ref[rows, SGU_W:2 * SGU_W] = dv * gg[:, SGU_W:]

    dz, dw, db, dlng, dlnb = _rowcall(
        body, name, s, tm,
        [_rb(tm, 512, 3), _rb(tm, 256), _fb((1, 256)), _fb((1, 256)), _fb((4, 128, 128)), _fb((128, 256))],
        (z, dy, lng.reshape(1, 256), lnb.reshape(1, 256), w, bexp),
        [_rb(tm, 512), _fb((4, 128, 128)), _fb((128, 256)), _fb((1, 256)), _fb((1, 256))],
        [_sds((s, 512)), _sds((4, 128, 128)), _sds((128, 256)), _sds((1, 256)), _sds((1, 256))])
    return dz, dw, db, dlng.reshape(256), dlnb.reshape(256)


CONV_TC = 1408
N_CT = D_FF // CONV_TC


def _row_of(block8, j):
    r = lax.broadcasted_iota(jnp.int32, block8.shape, 0)
    return jnp.sum(jnp.where(r == j, block8, 0.0), axis=0, keepdims=True)


def _shift_down(x, tail, has_prev, row):
    r7 = _row_of(tail, 7) * has_prev
    r6 = _row_of(tail, 6) * has_prev
    x1 = jnp.where(row == 0, r7, pltpu.roll(x, 1, axis=0))
    x2 = jnp.where(row == 0, r6, jnp.where(row == 1, r7, pltpu.roll(x, 2, axis=0)))
    return x1, x2


def _conv_fwd(hu, cw, cb, name, tm=256):
    s = hu.shape[0]
    n8 = tm // 8

    def body(xv_ref, xg_ref, tv_ref, tg_ref, wv_ref, wg_ref, bv_ref, bg_ref, hv_ref, hg_ref, act_ref):
        i = pl.program_id(1)
        has_prev = (i > 0).astype(F32)
        row = lax.broadcasted_iota(jnp.int32, (tm, CONV_TC), 0)

        def conv(x_ref, t_ref, w_ref, b_ref):
            x = x_ref[...]
            x1, x2 = _shift_down(x, t_ref[...], has_prev, row)
            return w_ref[0:1, :] * x2 + w_ref[1:2, :] * x1 + w_ref[2:3, :] * x + b_ref[...]

        hv = conv(xv_ref, tv_ref, wv_ref, bv_ref)
        hg = conv(xg_ref, tg_ref, wg_ref, bg_ref)
        hv_ref[...] = hv.astype(BF16)
        hg_ref[...] = hg.astype(BF16)
        act_ref[...] = (_gelu(hg) * hv).astype(BF16)

    def xs(off):
        return pl.BlockSpec((tm, CONV_TC), lambda j, i: (i, j + off))

    def ts(off):
        return pl.BlockSpec((8, CONV_TC), lambda j, i: (jnp.maximum(i * n8 - 1, 0), j + off))

    def ws(rows, off):
        return pl.BlockSpec((rows, CONV_TC), lambda j, i: (0, j + off))

    o_spec = pl.BlockSpec((tm, CONV_TC), lambda j, i: (i, j))
    return pl.pallas_call(
        body, name=name, grid=(N_CT, s // tm),
        in_specs=[xs(0), xs(N_CT), ts(0), ts(N_CT), ws(3, 0), ws(3, N_CT), ws(1, 0), ws(1, N_CT)],
        out_specs=[o_spec] * 3, out_shape=[_sds((s, D_FF), BF16)] * 3,
        compiler_params=_params(("parallel", "arbitrary")),
    )(hu, hu, hu, hu, cw, cw, cb.reshape(1, 2 * D_FF), cb.reshape(1, 2 * D_FF))


HALO = 16


def _conv_bwd(dact, hv, hg, hu, cw, name, tm=256):
    s = dact.shape[0]
    n8 = tm // 8

    def body(da_ref, dan_ref, hv_ref, hvn_ref, hg_ref, hgn_ref, x_ref, t_ref, w_ref, dx_ref, dw_ref, db_ref, d_scr):
        i = pl.program_id(1)
        is_value = pl.program_id(0) < N_CT

        @pl.when(i == 0)
        def _():
            dw_ref[...] = jnp.zeros_like(dw_ref)
            db_ref[...] = jnp.zeros_like(db_ref)

        for rows, (a_ref, v_ref, g_ref) in ((pl.ds(0, tm), (da_ref, hv_ref, hg_ref)),
                                            (pl.ds(tm, HALO), (dan_ref, hvn_ref, hgn_ref))):
            @pl.when(is_value)
            def _():
                d_scr[rows, :] = a_ref[...].astype(F32) * _gelu(g_ref[...].astype(F32))

            @pl.when(jnp.logical_not(is_value))
            def _():
                d_scr[rows, :] = (a_ref[...].astype(F32) * v_ref[...].astype(F32)
                                  * _gelu_grad(g_ref[...].astype(F32)))

        has_prev = (i > 0).astype(F32)
        has_next = (i < s // tm - 1).astype(F32)
        row = lax.broadcasted_iota(jnp.int32, (tm, CONV_TC), 0)
        d = d_scr[0:tm, :]
        nxt = d_scr[tm:tm + HALO, :]
        n0 = _row_of(nxt, 0) * has_next
        n1 = _row_of(nxt, 1) * has_next
        d1 = jnp.where(row == tm - 1, n0, pltpu.roll(d, tm - 1, axis=0))
        d2 = jnp.where(row == tm - 2, n0, jnp.where(row == tm - 1, n1, pltpu.roll(d, tm - 2, axis=0)))
        dx_ref[...] = (w_ref[2:3, :] * d + w_ref[1:2, :] * d1 + w_ref[0:1, :] * d2).astype(BF16)
        x = x_ref[...]
        x1, x2 = _shift_down(x, t_ref[...], has_prev, row)
        dw_ref[0:1, :] += jnp.sum(d * x2, axis=0, keepdims=True)
        dw_ref[1:2, :] += jnp.sum(d * x1, axis=0, keepdims=True)
        dw_ref[2:3, :] += jnp.sum(d * x, axis=0, keepdims=True)
        db_ref[...] += jnp.sum(d, axis=0, keepdims=True)

    a_spec = pl.BlockSpec((tm, CONV_TC), lambda j, i: (i, j % N_CT))
    an_spec = pl.BlockSpec((HALO, CONV_TC),
                           lambda j, i: (jnp.minimum((i + 1) * (tm // HALO), s // HALO - 1), j % N_CT))
    x_spec = pl.BlockSpec((tm, CONV_TC), lambda j, i: (i, j))
    t_spec = pl.BlockSpec((8, CONV_TC), lambda j, i: (jnp.maximum(i * n8 - 1, 0), j))
    w_spec = pl.BlockSpec((3, CONV_TC), lambda j, i: (0, j))
    db_spec = pl.BlockSpec((1, CONV_TC), lambda j, i: (0, j))
    return pl.pallas_call(
        body, name=name, grid=(2 * N_CT, s // tm),
        in_specs=[a_spec, an_spec, a_spec, an_spec, a_spec, an_spec, x_spec, t_spec, w_spec],
        out_specs=[x_spec, w_spec, db_spec],
        out_shape=[_sds((s, 2 * D_FF), BF16), _sds((3, 2 * D_FF)), _sds((1, 2 * D_FF))],
        scratch_shapes=[pltpu.VMEM((tm + HALO, CONV_TC), F32)],
        compiler_params=_params(("parallel", "arbitrary")),
    )(dact, dact, hv, hv, hg, hg, hu, hu, cw)


def _ple_fwd(h, gp, pp, name, tm=512):
    s, d = h.shape

    def body(h_ref, g_ref, p_ref, o_ref):
        o_ref[...] = h_ref[...] + _sigmoid(g_ref[...]) * p_ref[...]

    return _rowcall(body, name, s, tm, [_rb(tm, d)] * 3, (h, gp, pp), _rb(tm, d), _sds((s, d)))


def _ple_bwd(dh, gp, pp, name, tm=512):
    s, d = dh.shape

    def body(d_ref, g_ref, p_ref, dp_ref, dg_ref):
        sg = _sigmoid(g_ref[...])
        dv = d_ref[...]
        dp_ref[...] = (dv * sg).astype(BF16)
        dg_ref[...] = (dv * p_ref[...] * sg * (1.0 - sg)).astype(BF16)

    return _rowcall(body, name, s, tm, [_rb(tm, d)] * 3, (dh, gp, pp), [_rb(tm, d)] * 2,
                    [_sds((s, d), BF16)] * 2)


SCALE = HEAD_DIM ** -0.5
ATT_ROWS = 2048


def _att_geom(s, dil):
    w = min(ATT_ROWS, s)
    p = BLK * dil
    assert w % p == 0 and s % w == 0
    return w, p, w // p


def _rows(start, dil):
    return pl.ds(start, BLK, stride=dil) if dil > 1 else pl.ds(start, BLK)


def _head_masks():
    lane = lax.broadcasted_iota(jnp.int32, (1, BLK), 1)
    return [lane < HEAD_DIM, lane >= HEAD_DIM]


def _band_valid(has_prev):
    qi = lax.broadcasted_iota(jnp.int32, (BLK, 2 * BLK), 0)
    ki = lax.broadcasted_iota(jnp.int32, (BLK, 2 * BLK), 1)
    rel = qi + BLK - ki
    band = (rel >= 0) & (rel <= BLK)
    if has_prev is True:
        return band
    return band & (has_prev | (ki >= BLK))


def _zcur(w):
    return lambda off: pl.BlockSpec((w, BLK), lambda hp, i: (i, off + hp))


def _zprev(p, nb):
    return lambda off: pl.BlockSpec((p, BLK), lambda hp, i: (jnp.maximum(i * nb - 1, 0), off + hp))


def _scur(w):
    return pl.BlockSpec((w, BLK), lambda hp, i: (i, hp))


def _pair_rows(t, masks):
    return jnp.concatenate([jnp.where(masks[0], t, 0.0), jnp.where(masks[1], t, 0.0)], axis=0).astype(BF16)


def _pair_bias_bwd(bias):
    return bias.reshape(4, 2, BLK, 2, BLK).transpose(0, 3, 2, 1, 4).reshape(4, 2, BLK, 2 * BLK)


def _unpair_bias_bwd(db):
    return db.reshape(4, 2, BLK, 2, BLK).transpose(0, 3, 2, 1, 4).reshape(N_HEADS, BLK, 2 * BLK)


def _attn_fwd(z, bias, state, dil, first, last, name):
    s = z.shape[0]
    w, p, nb = _att_geom(s, dil)

    def body(*refs):
        q_ref, kp_ref, kc_ref, vp_ref, vc_ref, b_ref = refs[:6]
        rest = refs[6:]
        if not first:
            m_ref, l_ref, a_ref = rest[:3]
            rest = rest[3:]
        i = pl.program_id(1)
        masks = _head_masks()
        for r in range(dil):
            for b in range(nb):
                rows = _rows(r + p * b, dil)
                prev_rows = _rows(r + p * (b - 1), dil) if b > 0 else _rows(r, dil)
                kprev, vprev = (kc_ref, vc_ref) if b > 0 else (kp_ref, vp_ref)
                q = q_ref[rows, :]
                k = jnp.concatenate([kprev[prev_rows, :], kc_ref[rows, :]], axis=0).astype(BF16)
                v = jnp.concatenate([vprev[prev_rows, :], vc_ref[rows, :]], axis=0).astype(BF16)
                valid = _band_valid(True if b > 0 else i > 0)
                mb = lb = ob = None
                for hh, mh in enumerate(masks):
                    qh = jnp.where(mh, q, 0.0).astype(BF16)
                    sc = lax.dot_general(qh, k, _DIMS["nt"], preferred_element_type=F32) * SCALE + b_ref[hh]
                    sc = jnp.where(valid, sc, NEG_INF)
                    mx = jnp.max(sc, axis=1, keepdims=True)
                    e = jnp.exp(sc - mx)
                    den = jnp.sum(e, axis=1, keepdims=True)
                    o = jnp.dot(e.astype(BF16), v, preferred_element_type=F32)
                    if hh == 0:
                        mb = jnp.broadcast_to(mx, (BLK, BLK))
                        lb = jnp.broadcast_to(den, (BLK, BLK))
                        ob = o
                    else:
                        mb = jnp.where(mh, mx, mb)
                        lb = jnp.where(mh, den, lb)
                        ob = jnp.where(mh, o, ob)
                if first:
                    m_new, l_new, a_new = mb, lb, ob
                else:
                    m_old = m_ref[rows, :]
                    m_new = jnp.maximum(m_old, mb)
                    al = jnp.exp(m_old - m_new)
                    be = jnp.exp(mb - m_new)
                    l_new = al * l_ref[rows, :] + be * lb
                    a_new = al * a_ref[rows, :] + be * ob
                if last:
                    y_ref, lse_ref = rest
                    y_ref[rows, :] = a_new / l_new
                    lse_ref[rows, :] = m_new + jnp.log(l_new)
                else:
                    mo_ref, lo_ref, ao_ref = rest
                    mo_ref[rows, :] = m_new
                    lo_ref[rows, :] = l_new
                    ao_ref[rows, :] = a_new

    cur, prv = _zcur(w), _zprev(p, nb)
    b_spec = pl.BlockSpec((2, BLK, 2 * BLK), lambda hp, i: (hp, 0, 0))
    in_specs = [cur(0), prv(4), cur(4), prv(8), cur(8), b_spec]
    args = [z, z, z, z, z, bias]
    if not first:
        in_specs += [_scur(w)] * 3
        args += list(state)
    n_out = 2 if last else 3
    return pl.pallas_call(
        body, name=name, grid=(4, s // w), in_specs=in_specs, out_specs=[_scur(w)] * n_out,
        out_shape=[_sds((s, ATTN_W))] * n_out,
        compiler_params=_params(("parallel", "parallel")),
    )(*args)


def _row_stats(mh, dy, y, lse):
    delta = jnp.sum(jnp.where(mh, dy * y, 0.0), axis=1, keepdims=True)
    lse_h = jnp.max(jnp.where(mh, lse, NEG_INF), axis=1, keepdims=True)
    return delta, lse_h


def _attn_bwd(z, bias, dy, y, lse, prev, dil, name):
    s = z.shape[0]
    w, p, nb = _att_geom(s, dil)
    n_steps = s // w
    first = prev is None

    def body(*refs):
        q_ref, kp_ref, kc_ref, vp_ref, vc_ref, b_ref, dy_ref, y_ref, lse_ref = refs[:9]
        rest = refs[9:]
        if not first:
            dqp_ref, dkp_ref, dvp_ref = rest[:3]
            rest = rest[3:]
        dq_ref, dk_ref, dv_ref, dkx_ref, dvx_ref, db_ref = rest
        i = pl.program_id(1)

        @pl.when(i == 0)
        def _():
            db_ref[...] = jnp.zeros_like(db_ref)

        qi = lax.broadcasted_iota(jnp.int32, (BLK, BLK), 0)
        ki = lax.broadcasted_iota(jnp.int32, (BLK, BLK), 1)
        masks = _head_masks()
        first_head = lax.broadcasted_iota(jnp.int32, (1, 2 * BLK), 1) < BLK

        def flush(rows, dk, dv):
            if not first:
                dk = dk + dkp_ref[rows, :]
                dv = dv + dvp_ref[rows, :]
            dk_ref[rows, :] = dk
            dv_ref[rows, :] = dv

        for r in range(dil):
            carry = None
            for b in range(nb):
                rows = _rows(r + p * b, dil)
                prev_rows = _rows(r + p * (b - 1), dil) if b > 0 else _rows(r, dil)
                kprev, vprev = (kc_ref, vc_ref) if b > 0 else (kp_ref, vp_ref)
                keys = [(_pair_rows(kprev[prev_rows, :], masks), _pair_rows(vprev[prev_rows, :], masks)),
                        (_pair_rows(kc_ref[rows, :], masks), _pair_rows(vc_ref[rows, :], masks))]
                valid = [(ki >= qi) if b > 0 else ((ki >= qi) & (i > 0)), qi >= ki]
                q = q_ref[rows, :].astype(BF16)
                dy_v = dy_ref[rows, :]
                dyb = dy_v.astype(BF16)
                stats = [_row_stats(mh, dy_v, y_ref[rows, :], lse_ref[rows, :]) for mh in masks]
                delta = jnp.where(first_head, stats[0][0], stats[1][0])
                lse_h = jnp.where(first_head, stats[0][1], stats[1][1])
                dq = jnp.zeros((BLK, BLK), F32)
                dk, dv = [], []
                for half in range(2):
                    kh, vh = keys[half]
                    ok = jnp.concatenate([valid[half], valid[half]], axis=1)
                    sc = lax.dot_general(q, kh, _DIMS["nt"], preferred_element_type=F32) * SCALE + b_ref[half]
                    pr = jnp.where(ok, jnp.exp(jnp.where(ok, sc, NEG_INF) - lse_h), 0.0)
                    dp = lax.dot_general(dyb, vh, _DIMS["nt"], preferred_element_type=F32)
                    ds = pr * (dp - delta)
                    db_ref[half] += ds
                    dsb = ds.astype(BF16)
                    dq = dq + jnp.dot(dsb, kh, preferred_element_type=F32)
                    dk2 = lax.dot_general(dsb, q, _DIMS["tn"], preferred_element_type=F32)
                    dv2 = lax.dot_general(pr.astype(BF16), dyb, _DIMS["tn"], preferred_element_type=F32)
                    dk.append(jnp.where(masks[0], dk2[:BLK], dk2[BLK:]))
                    dv.append(jnp.where(masks[0], dv2[:BLK], dv2[BLK:]))
                dq = dq * SCALE
                if not first:
                    dq = dq + dqp_ref[rows, :]
                dq_ref[rows, :] = dq
                if b > 0:
                    flush(prev_rows, carry[0] + dk[0] * SCALE, carry[1] + dv[0])
                else:
                    dkx_ref[prev_rows, :] = dk[0] * SCALE
                    dvx_ref[prev_rows, :] = dv[0]
                carry = (dk[1] * SCALE, dv[1])
            flush(_rows(r + p * (nb - 1), dil), *carry)

    cur, prv = _zcur(w), _zprev(p, nb)
    b_spec = pl.BlockSpec((None, 2, BLK, 2 * BLK), lambda hp, i: (hp, 0, 0, 0))
    in_specs = [cur(0), prv(4), cur(4), prv(8), cur(8), b_spec] + [_scur(w)] * 3
    args = [z, z, z, z, z, bias, dy, y, lse]
    if not first:
        in_specs += [_scur(w)] * 3
        args += list(prev)
    x_spec = pl.BlockSpec((p, BLK), lambda hp, i: (i, hp))
    *outs, db = pl.pallas_call(
        body, name=name, grid=(4, n_steps), in_specs=in_specs,
        out_specs=[_scur(w)] * 3 + [x_spec] * 2 + [b_spec],
        out_shape=[_sds((s, ATTN_W))] * 3 + [_sds((n_steps * p, ATTN_W))] * 2 + [_sds((4, 2, BLK, 2 * BLK))],
        compiler_params=_params(("parallel", "arbitrary")),
    )(*args)
    return (*outs, _unpair_bias_bwd(db))


ASM_ROWS = 512


def _assemble_dz(dq, dk, dv, extras, dzs, du, name):
    s = dq.shape[0]
    w = min(ATT_ROWS, s)
    n_steps = s // w
    per_step = w // ASM_ROWS
    assert w % ASM_ROWS == 0

    def body(*refs):
        dq_ref, dk_ref, dv_ref, dzs_ref, du_ref = refs[:5]
        x_refs = refs[5:5 + 2 * len(extras)]
        o_ref, acc_ref = refs[-2:]
        j = pl.program_id(0)
        step = j // per_step
        has_next = (step < n_steps - 1).astype(F32)
        last_of_step = ((j + 1) % per_step == 0).astype(F32)
        o_ref[:, 0:ATTN_W] = dq_ref[...].astype(BF16)
        o_ref[:, 3 * ATTN_W:3 * ATTN_W + 2 * SGU_W] = dzs_ref[...].astype(BF16)
        o_ref[:, 3 * ATTN_W + 2 * SGU_W:IN_W] = du_ref[...].astype(BF16)
        for part, (base_ref, col) in enumerate(((dk_ref, ATTN_W), (dv_ref, 2 * ATTN_W))):
            acc_ref[...] = base_ref[...]
            for n, (_, dil) in enumerate(BRANCHES):
                rows = min(BLK * dil, ASM_ROWS)
                scale = has_next if BLK * dil >= w else has_next * last_of_step
                acc_ref[ASM_ROWS - rows:, :] += x_refs[2 * n + part][...] * scale
            o_ref[:, col:col + ATTN_W] = acc_ref[...].astype(BF16)

    def x_spec(dil):
        p = BLK * dil
        rows = min(p, ASM_ROWS)
        blocks_per_step = p // rows
        total = n_steps * blocks_per_step

        def idx(j):
            step = j // per_step
            within = (j % per_step) - (per_step - blocks_per_step)
            return (jnp.clip((step + 1) * blocks_per_step + jnp.maximum(within, 0), 0, total - 1), 0)

        return pl.BlockSpec((rows, ATTN_W), idx)

    in_specs = [_rb(ASM_ROWS, ATTN_W)] * 3 + [_rb(ASM_ROWS, 2 * SGU_W), _rb(ASM_ROWS, SSM_W)]
    args = [dq, dk, dv, dzs, du]
    for (dkx, dvx), (_, dil) in zip(extras, BRANCHES):
        in_specs += [x_spec(dil)] * 2
        args += [dkx, dvx]
    return pl.pallas_call(
        body, name=name, grid=(s // ASM_ROWS,), in_specs=in_specs, out_specs=_rb(ASM_ROWS, IN_W),
        out_shape=_sds((s, IN_W), BF16), scratch_shapes=[pltpu.VMEM((ASM_ROWS, ATTN_W), F32)],
        compiler_params=_params(("parallel",)),
    )(*args)


def _t5_bucket(dist):
    max_exact = N_BUCKETS // 2
    d = np.maximum(dist, 0)
    large = max_exact + (np.log(np.maximum(d, 1) / max_exact) / np.log(REL_MAX / max_exact)
                         * (N_BUCKETS - max_exact)).astype(np.int32)
    large = np.minimum(large, N_BUCKETS - 1)
    return np.where(d < max_exact, d, large).astype(np.int32)


def _bias_tables(rel_bias):
    period = 3 * BLK
    tabs = []
    for _, dil in BRANCHES:
        onehot = np.zeros((period, N_BUCKETS), np.float32)
        d = np.arange(BLK + 1)
        onehot[d, _t5_bucket((BLK - d) * dil)] = 1.0
        f = jnp.dot(jnp.asarray(onehot), rel_bias, precision=lax.Precision.HIGHEST)
        flat = jnp.tile(f.T, (1, BLK))[:, :BLK * (period - 1)]
        tabs.append(flat.reshape(N_HEADS, BLK, period - 1)[:, :, :2 * BLK])
    return tabs


def _bucket_onehot():
    maps = []
    q = np.arange(BLK)[:, None]
    k = np.arange(2 * BLK)[None, :]
    rel = q + BLK - k
    for _, dil in BRANCHES:
        maps.append(np.where((rel >= 0) & (rel <= BLK), _t5_bucket(rel * dil), -1).reshape(-1))
    bmap = jnp.asarray(np.concatenate(maps).astype(np.int32))
    return (bmap[:, None] == jnp.arange(128, dtype=jnp.int32)[None, :]).astype(BF16)


def _block_diag(t):
    g, n, c = t.shape
    eye = jnp.eye(g, dtype=t.dtype)
    return (t[:, :, None, :] * eye[:, None, :, None]).reshape(g * n, g * c)


def _ssm_prep(a_re, a_im, log_dt, b_re, b_im, c_re, c_im):
    lam = lax.complex(a_re, a_im)
    dt = jnp.exp(log_dt)[:, None]
    a_bar = jnp.exp(lam * dt)
    b_bar = ((a_bar - 1.0) / lam)[:, :, None] * lax.complex(b_re, b_im)
    bdt = jnp.concatenate([_block_diag(jnp.real(b_bar)), _block_diag(jnp.imag(b_bar))], axis=0)
    cd = jnp.concatenate([_block_diag(jnp.transpose(c_re, (0, 2, 1))),
                          _block_diag(-jnp.transpose(c_im, (0, 2, 1)))], axis=0)
    return jnp.real(a_bar).reshape(-1), jnp.imag(a_bar).reshape(-1), bdt, cd


def _powers(ar, ai):
    pr, pi = ar[:, None], ai[:, None]
    k = 1
    while k < 8:
        lr, li = pr[:, -1:], pi[:, -1:]
        pr, pi = (jnp.concatenate([pr, pr * lr - pi * li], axis=1),
                  jnp.concatenate([pi, pr * li + pi * lr], axis=1))
        k *= 2
    return pr, pi


def _sgu_bias_expand(b):
    return jnp.repeat(b.T, 64, axis=1)


def _layer_fwd(i, h, p_i, big, small, bias_tabs):
    nm = "l%d_" % i
    sv = {"h": h}
    a1 = _rms_fwd(h, small["norm_attn_g"][i], nm + "rms_attn")
    z = _mm(a1, big["w_in"], "nt", nm + "in_proj")
    st = None
    for b, (_, dil) in enumerate(BRANCHES):
        st = _attn_fwd(z, bias_tabs[b][0], st, dil, b == 0, b == len(BRANCHES) - 1, nm + "attn_fwd%d" % b)
    y_attn, lse = st
    bexp = _sgu_bias_expand(small["sgu_b"][i])
    y_sgu = _sgu_fwd(z, small["sgu_ln_g"][i], small["sgu_ln_b"][i], small["sgu_w"][i], bexp, nm + "sgu_fwd")
    ar, ai, bdt, cd = _ssm_prep(*[small[k][i] for k in ("ssm_a_re", "ssm_a_im", "ssm_log_dt", "ssm_b_re",
                                                         "ssm_b_im", "ssm_c_re", "ssm_c_im")])
    u_cols = (IN_W - SSM_W, SSM_W)
    bu = _mm(z, bdt, "nt", nm + "ssm_bu", a_cols=u_cols)
    xr, xi = _scan_fwd(bu, _scan_tables(*_powers(ar, ai), False), nm + "ssm_scan")
    yc = _mm(xr, cd[:SSM_NS], "nn", nm + "ssm_cx_re")
    yc = _mm(xi, cd[SSM_NS:], "nn", nm + "ssm_cx_im", add=yc)
    y_ssm = _ssm_post_fwd(yc, z, small["ssm_d"][i], big["ssm_glu_w"], small["ssm_glu_b"][i], nm + "ssm_post")
    mix = _mix_fwd(y_attn, y_sgu, y_ssm, small["branch_norm_g"][i], nm + "mix")
    h2 = _mm(mix, big["w_out"], "nn", nm + "out_proj", add=h)
    a2 = _rms_fwd(h2, small["norm_ffn_g"][i], nm + "rms_ffn")
    hu = _mm(a2, big["ffn_w_up"], "nt", nm + "ffn_up")
    hv, hg, act = _conv_fwd(hu, big["ffn_conv_w"], small["ffn_conv_b"][i], nm + "ffn_conv")
    h3 = _mm(act, big["ffn_w_down"], "nn", nm + "ffn_down", add=h2)
    a3 = _rms_fwd(h3, small["norm_ple_g"][i], nm + "rms_ple")
    gp = _mm(a3, big["ple_w_gate"], "nn", nm + "ple_gate")
    pp = _mm(p_i, big["ple_w_proj"], "nt", nm + "ple_proj")
    h4 = _ple_fwd(h3, gp, pp, nm + "ple_add")
    sv.update(a1=a1, z=z, y_attn=y_attn, lse=lse, y_sgu=y_sgu, y_ssm=y_ssm, yc=yc, xr=xr, xi=xi, mix=mix, h2=h2,
              a2=a2, hu=hu, hv=hv, hg=hg, act=act, h3=h3, a3=a3, gp=gp, pp=pp)
    return h4, sv


def _layer_bwd(i, dh4, sv, p_i, big, small, bias_tabs):
    nm = "l%d_" % i
    g = {}
    dpp, dgp = _ple_bwd(dh4, sv["gp"], sv["pp"], nm + "ple_bwd")
    g["ple_w_proj"] = _mm(dpp, p_i, "tn", nm + "d_ple_proj", out_dtype=BF16)
    g["ple_w_gate"] = _mm(sv["a3"], dgp, "tn", nm + "d_ple_gate", out_dtype=BF16)
    da3 = _mm(dgp, big["ple_w_gate"], "nt", nm + "ple_gate_t")
    dh3, g["norm_ple_g"] = _rms_bwd(da3, sv["h3"], small["norm_ple_g"][i], dh4, nm + "rms_ple_bwd")
    g["ffn_w_down"] = _mm(sv["act"], dh3, "tn", nm + "d_ffn_down", out_dtype=BF16)
    dact = _mm(dh3, big["ffn_w_down"], "nt", nm + "ffn_down_t", out_dtype=BF16)
    dhu, g["ffn_conv_w"], dcb = _conv_bwd(dact, sv["hv"], sv["hg"], sv["hu"], big["ffn_conv_w"],
                                          nm + "ffn_conv_bwd")
    g["ffn_conv_b"] = dcb.reshape(2 * D_FF)
    g["ffn_w_up"] = _mm(dhu, sv["a2"], "tn", nm + "d_ffn_up", out_dtype=BF16)
    da2 = _mm(dhu, big["ffn_w_up"], "nn", nm + "ffn_up_t")
    dh2, g["norm_ffn_g"] = _rms_bwd(da2, sv["h2"], small["norm_ffn_g"][i], dh3, nm + "rms_ffn_bwd")
    g["w_out"] = _mm(sv["mix"], dh2, "tn", nm + "d_out_proj", out_dtype=BF16)
    dmix = _mm(dh2, big["w_out"], "nt", nm + "out_proj_t")
    dya, dysg, dyss, g["branch_norm_g"] = _mix_bwd(dmix, sv["y_attn"], sv["y_sgu"], sv["y_ssm"],
                                                   small["branch_norm_g"][i], nm + "mix_bwd")
    ssm_keys = ("ssm_a_re", "ssm_a_im", "ssm_log_dt", "ssm_b_re", "ssm_b_im", "ssm_c_re", "ssm_c_im")
    (ar, ai, bdt, cd), prep_vjp = jax.vjp(_ssm_prep, *[small[k][i] for k in ssm_keys])
    dy1, dgl, y2, dud, g["ssm_d"], g["ssm_glu_b"] = _ssm_post_bwd(
        dyss, sv["yc"], sv["z"], small["ssm_d"][i], big["ssm_glu_w"], small["ssm_glu_b"][i], nm + "ssm_post_bwd")
    g["ssm_glu_w"] = _mm(y2, dgl, "tn", nm + "d_ssm_glu", out_dtype=BF16)
    g_x = _mm(dy1, cd, "nt", nm + "ssm_cx_t")
    dcd = jnp.concatenate([_mm(sv["xr"], dy1, "tn", nm + "d_ssm_c_re"),
                           _mm(sv["xi"], dy1, "tn", nm + "d_ssm_c_im")], axis=0)
    lr, li, dar, dai = _scan_bwd(g_x, sv["xr"], sv["xi"], _scan_tables(*_powers(ar, ai), True),
                                 nm + "ssm_scan_bwd")
    u_cols = (IN_W - SSM_W, SSM_W)
    dbdt = jnp.concatenate([_mm(lr, sv["z"], "tn", nm + "d_ssm_b_re", b_cols=u_cols),
                            _mm(li, sv["z"], "tn", nm + "d_ssm_b_im", b_cols=u_cols)], axis=0)
    du = _mm(lr, bdt[:SSM_NS], "nn", nm + "ssm_bu_t_re", add=dud)
    du = _mm(li, bdt[SSM_NS:], "nn", nm + "ssm_bu_t_im", add=du)
    for k, val in zip(ssm_keys, prep_vjp((dar, dai, dbdt, dcd))):
        g[k] = val
    bexp, bexp_vjp = jax.vjp(_sgu_bias_expand, small["sgu_b"][i])
    dzs, g["sgu_w"], dbexp, g["sgu_ln_g"], g["sgu_ln_b"] = _sgu_bwd(
        sv["z"], dysg, small["sgu_ln_g"][i], small["sgu_ln_b"][i], small["sgu_w"][i], bexp, nm + "sgu_bwd")
    g["sgu_b"] = bexp_vjp(dbexp)[0]
    prev = None
    dbs, extras = [], []
    for b, (_, dil) in enumerate(BRANCHES):
        dq, dk, dv, dkx, dvx, db = _attn_bwd(sv["z"], bias_tabs[b][1], dya, sv["y_attn"], sv["lse"], prev, dil,
                                             nm + "attn_bwd%d" % b)
        prev = (dq, dk, dv)
        extras.append((dkx, dvx))
        dbs.append(db.reshape(N_HEADS, BLK * 2 * BLK))
    dz = _assemble_dz(dq, dk, dv, extras, dzs, du, nm + "assemble_dz")
    g["w_in"] = _mm(dz, sv["a1"], "tn", nm + "d_in_proj", out_dtype=BF16)
    da1 = _mm(dz, big["w_in"], "nn", nm + "in_proj_t")
    dh, g["norm_attn_g"] = _rms_bwd(da1, sv["h"], small["norm_attn_g"][i], dh2, nm + "rms_attn_bwd")
    return dh, g, jnp.concatenate(dbs, axis=1)


def _local_step(x, p, target, layer_weights, small, layer_done=None):
    depth = p.shape[0]
    bias_tabs = [(t, _pair_bias_bwd(t)) for t in _bias_tables(small["rel_bias"])]
    h = x
    saved, bigs = [], []
    for i in range(depth):
        bigs.append(layer_weights(i, h))
        h, sv = _layer_fwd(i, h, p[i], bigs[i], small, bias_tabs)
        saved.append(sv)
    dh, loss, g_final = _loss_head(h, target, small["final_norm_g"], "loss_head")
    layer_grads = [None] * depth
    dbias = [None] * depth
    for i in reversed(range(depth)):
        dh, layer_grads[i], dbias[i] = _layer_bwd(i, dh, saved[i], p[i], bigs[i], small, bias_tabs)
        if layer_done is not None:
            small = layer_done(i, layer_grads[i], small)
    big_grads = [{k: lg.pop(k) for k in COMM_NAMES} for lg in layer_grads]
    grads = {k: jnp.stack([layer_grads[i][k] for i in range(depth)]) for k in layer_grads[0]}
    grads["final_norm_g"] = g_final
    g_rb = _mm(sum(dbias[1:], dbias[0]), _bucket_onehot(), "nn", "d_rel_bias", tk=2048)
    grads["rel_bias"] = g_rb[:, :N_BUCKETS].T
    return loss, dh, big_grads, grads


_ANY = pl.BlockSpec(memory_space=pl.ANY)
MESH_IDS = pl.DeviceIdType.MESH


def _slot(ref, axis, j):
    return ref.at[(slice(None),) * axis + (j,)]


def _all_gather(blocks, axis, name):
    nt = len(blocks)

    def body(*refs):
        x_refs, o_refs = refs[:nt], refs[nt:2 * nt]
        send_sems, recv_sems, local_sems = refs[2 * nt:]
        x, y, c = lax.axis_index("x"), lax.axis_index("y"), lax.axis_index("c")
        me, sibling = (x, y, c), (x, y, 1 - c)
        chips = [(1 - x, y), (x, 1 - y), (1 - x, 1 - y)]

        def slot(t, px, py, pc):
            return _slot(o_refs[t], axis, 4 * px + 2 * py + pc)

        def copy(t, k, blk, to, src=None):
            return pltpu.make_async_remote_copy(
                src_ref=slot(t, *blk) if src is None else src, dst_ref=slot(t, *blk),
                send_sem=send_sems.at[7 * t + k], recv_sem=recv_sems.at[7 * t + k],
                device_id=to, device_id_type=MESH_IDS)

        mine = [pltpu.make_async_copy(x_refs[t], slot(t, *me), local_sems.at[t]) for t in range(nt)]
        for cp in mine:
            cp.start()
        first = []
        for t in range(nt):
            first.append(copy(t, 0, me, sibling, src=x_refs[t]))
            first += [copy(t, 1 + j, me, (*chip, c), src=x_refs[t]) for j, chip in enumerate(chips)]
        for cp in first:
            cp.start()
        passed = []
        for t in range(nt):
            for j, chip in enumerate(chips):
                copy(t, 1 + j, (*chip, c), me).wait_recv()
                passed.append(copy(t, 4 + j, (*chip, c), sibling))
                passed[-1].start()
        for t in range(nt):
            copy(t, 0, sibling, me).wait_recv()
            for j, chip in enumerate(chips):
                copy(t, 4 + j, (*chip, 1 - c), me).wait_recv()
        for cp in first + passed:
            cp.wait_send()
        for cp in mine:
            cp.wait()

    out_shape = [jax.ShapeDtypeStruct(b.shape[:axis] + (N_DEV,) + b.shape[axis:], b.dtype) for b in blocks]
    return pl.pallas_call(
        body, name=name, out_shape=out_shape, in_specs=[_ANY] * nt, out_specs=[_ANY] * nt,
        scratch_shapes=[pltpu.SemaphoreType.DMA((7 * nt,)), pltpu.SemaphoreType.DMA((7 * nt,)),
                        pltpu.SemaphoreType.DMA((nt,))],
    )(*blocks)


def _peer(k):
    x, y, c = lax.axis_index("x"), lax.axis_index("y"), lax.axis_index("c")
    px = 1 - x if k & 4 else x
    py = 1 - y if k & 2 else y
    pc = 1 - c if k & 1 else c
    return (px, py, pc), 4 * px + 2 * py + pc


def _all_to_all(blocks, name):
    nt = len(blocks)

    def body(*refs):
        x_refs, o_refs = refs[:nt], refs[nt:2 * nt]
        send_sems, recv_sems, local_sems = refs[2 * nt:]
        _, me = _peer(0)
        mine = [pltpu.make_async_copy(x_refs[t].at[me], o_refs[t].at[me], local_sems.at[t]) for t in range(nt)]
        for cp in mine:
            cp.start()
        copies = []
        for k in range(1, N_DEV):
            peer, idx = _peer(k)
            for t in range(nt):
                cp = pltpu.make_async_remote_copy(
                    src_ref=x_refs[t].at[idx], dst_ref=o_refs[t].at[me],
                    send_sem=send_sems.at[7 * t + k - 1], recv_sem=recv_sems.at[7 * t + k - 1],
                    device_id=peer, device_id_type=MESH_IDS)
                cp.start()
                copies.append(cp)
        for cp in copies:
            cp.wait()
        for cp in mine:
            cp.wait()

    return pl.pallas_call(
        body, name=name, out_shape=[jax.ShapeDtypeStruct(b.shape, b.dtype) for b in blocks],
        in_specs=[_ANY] * nt, out_specs=[_ANY] * nt,
        scratch_shapes=[pltpu.SemaphoreType.DMA((7 * nt,)), pltpu.SemaphoreType.DMA((7 * nt,)),
                        pltpu.SemaphoreType.DMA((nt,))],
    )(*blocks)


_HBM = pl.BlockSpec(memory_space=pltpu.HBM)
_SEM = pl.BlockSpec(memory_space=pltpu.SEMAPHORE)
_EFFECT = pltpu.SideEffectType.DATAFLOW_SIDE_EFFECTING


def _split_copy(src_ref, land_ref, send_sems, recv_sems, t, k, gather):
    peer, idx = _peer(k)
    _, me = _peer(0)
    return pltpu.make_async_remote_copy(
        src_ref=src_ref if gather else src_ref.at[idx], dst_ref=land_ref.at[me],
        send_sem=send_sems.at[7 * t + k - 1], recv_sem=recv_sems.at[7 * t + k - 1],
        device_id=peer, device_id_type=MESH_IDS)


def _exchange_start(srcs, lands, gather, name):
    nt = len(srcs)

    def body(*refs):
        src_refs, land_refs = refs[:nt], refs[nt:2 * nt]
        send_sems, recv_sems = refs[2 * nt:2 * nt + 2]
        token = refs[-1]
        for k in range(1, N_DEV):
            for t in range(nt):
                _split_copy(src_refs[t], land_refs[t], send_sems, recv_sems, t, k, gather).start()
        token[...] = jnp.zeros_like(token)

    hbm = lambda a: pltpu.HBM(a.shape, a.dtype)
    outs = pl.pallas_call(
        body, name=name,
        out_shape=(pltpu.SemaphoreType.DMA((7 * nt,)), pltpu.SemaphoreType.DMA((7 * nt,)),
                   *[hbm(a) for a in srcs], *[hbm(a) for a in lands], jax.ShapeDtypeStruct((8, 128), F32)),
        in_specs=[_HBM] * (2 * nt),
        out_specs=(_SEM, _SEM, *[_HBM] * (2 * nt), pl.BlockSpec(memory_space=pltpu.VMEM)),
        input_output_aliases={j: 2 + j for j in range(2 * nt)},
        compiler_params=pltpu.CompilerParams(has_side_effects=_EFFECT),
    )(*[pltpu.with_memory_space_constraint(a, pltpu.HBM) for a in list(srcs) + list(lands)])
    return outs[0], outs[1], outs[2:2 + nt], outs[2 + nt:2 + 2 * nt], outs[-1]


def _exchange_wait(send_sems, recv_sems, srcs, lands, after, gather, name):
    nt = len(srcs)

    def body(*refs):
        src_refs, land_refs = refs[:nt], refs[nt:2 * nt]
        send_sems, recv_sems = refs[2 * nt:2 * nt + 2]
        for k in range(1, N_DEV):
            _, idx = _peer(k)
            for t in range(nt):
                _split_copy(src_refs[t], land_refs[t], send_sems, recv_sems, t, k, gather).wait_send()
                arrival = pltpu.make_async_remote_copy(
                    src_ref=land_refs[t].at[idx], dst_ref=land_refs[t].at[idx],
                    send_sem=send_sems.at[7 * t + k - 1], recv_sem=recv_sems.at[7 * t + k - 1],
                    device_id=_peer(k)[0], device_id_type=MESH_IDS)
                arrival.wait_recv()

    hbm = lambda a: pltpu.HBM(a.shape, a.dtype)
    outs = pl.pallas_call(
        body, name=name, out_shape=tuple(hbm(a) for a in list(srcs) + list(lands)),
        in_specs=[_HBM] * (2 * nt) + [_SEM, _SEM, _ANY], out_specs=tuple([_HBM] * (2 * nt)),
        input_output_aliases={j: j for j in range(2 * nt)},
        compiler_params=pltpu.CompilerParams(has_side_effects=_EFFECT),
    )(*srcs, *lands, send_sems, recv_sems, after)
    return outs[nt:]


def _adamw(parts, w, m, v, name, tr):
    n_layers, r, c_ = w.shape
    assert len(parts) == n_layers

    def body(*refs):
        p_refs = refs[:n_layers]
        w_ref, m_ref, v_ref, g_ref, d_ref, mo_ref, vo_ref = refs[n_layers:]

        def update(p_ref):
            g = p_ref[0].astype(F32)
            for j in range(1, N_DEV):
                g = g + p_ref[j].astype(F32)
            m2 = ADAM_B1 * m_ref[...] + (1.0 - ADAM_B1) * g
            v2 = ADAM_B2 * v_ref[...] + (1.0 - ADAM_B2) * (g * g)
            m_hat = m2 / (1.0 - ADAM_B1 ** ADAM_STEP)
            v_hat = v2 / (1.0 - ADAM_B2 ** ADAM_STEP)
            g_ref[...] = g
            d_ref[...] = -ADAM_LR * (m_hat / (jnp.sqrt(v_hat) + ADAM_EPS) + ADAM_WD * w_ref[...])
            mo_ref[...] = m2
            vo_ref[...] = v2

        for layer in range(n_layers):
            pl.when(pl.program_id(0) == layer)(lambda layer=layer: update(p_refs[layer]))

    spec = pl.BlockSpec((None, tr, c_), lambda l, i: (l, i, 0))
    p_spec = pl.BlockSpec((N_DEV, tr, c_), lambda l, i: (0, i, 0))
    return pl.pallas_call(
        body, name=name, grid=(n_layers, r // tr), in_specs=[p_spec] * n_layers + [spec] * 3,
        out_specs=[spec] * 4, out_shape=[_sds((n_layers, r, c_))] * 4,
        compiler_params=_params(("parallel", "parallel")),
    )(*parts, w, m, v)


def _pack_rows(n_elems, align):
    rows = -(-n_elems // PACK_COLS)
    return -(-rows // align) * align


def _pack(arrs, rows, dtype=F32):
    flat = jnp.concatenate([a.reshape(-1) for a in arrs]).astype(dtype)
    return jnp.pad(flat, (0, rows * PACK_COLS - flat.shape[0])).reshape(rows, PACK_COLS)


def _unpack(pack, shapes):
    flat = pack.reshape(-1)
    out, off = [], 0
    for shp in shapes:
        size = int(np.prod(shp))
        out.append(flat[off:off + size].reshape(shp))
        off += size
    return out


def _tile_rows(rows, target, align=16):
    best = align
    for t in range(align, target + 1, align):
        if rows % t == 0:
            best = t
    return best


COMM_NAMES = ("w_in", "ssm_glu_w", "w_out", "ffn_w_up", "ffn_w_down", "ple_w_gate", "ple_w_proj")
COMM_TRANSPOSED = ("w_in", "ffn_w_up", "ple_w_proj")
SMALL_TILE_ROWS = 64
CONV_NAME = "ffn_conv_w"


def _to_comm(name, a):
    return jnp.swapaxes(a, 1, 2) if name in COMM_TRANSPOSED else a


def kernel(x, p, rel_bias, norm_attn_g, w_in, sgu_ln_g, sgu_ln_b, sgu_w, sgu_b, ssm_a_re, ssm_a_im, ssm_log_dt, ssm_b_re, ssm_b_im, ssm_c_re, ssm_c_im, ssm_d, ssm_glu_w, ssm_glu_b, branch_norm_g, w_out, norm_ffn_g, ffn_w_up, ffn_conv_w, ffn_conv_b, ffn_w_down, norm_ple_g, ple_w_gate, ple_w_proj, final_norm_g, loss_target, m_rel_bias, m_norm_attn_g, m_w_in, m_sgu_ln_g, m_sgu_ln_b, m_sgu_w, m_sgu_b, m_ssm_a_re, m_ssm_a_im, m_ssm_log_dt, m_ssm_b_re, m_ssm_b_im, m_ssm_c_re, m_ssm_c_im, m_ssm_d, m_ssm_glu_w, m_ssm_glu_b, m_branch_norm_g, m_w_out, m_norm_ffn_g, m_ffn_w_up, m_ffn_conv_w, m_ffn_conv_b, m_ffn_w_down, m_norm_ple_g, m_ple_w_gate, m_ple_w_proj, m_final_norm_g, v_rel_bias, v_norm_attn_g, v_w_in, v_sgu_ln_g, v_sgu_ln_b, v_sgu_w, v_sgu_b, v_ssm_a_re, v_ssm_a_im, v_ssm_log_dt, v_ssm_b_re, v_ssm_b_im, v_ssm_c_re, v_ssm_c_im, v_ssm_d, v_ssm_glu_w, v_ssm_glu_b, v_branch_norm_g, v_w_out, v_norm_ffn_g, v_ffn_w_up, v_ffn_conv_w, v_ffn_conv_b, v_ffn_w_down, v_norm_ple_g, v_ple_w_gate, v_ple_w_proj, v_final_norm_g):
    given = dict(locals())
    w = {n: given[n] for n in WEIGHT_NAMES}
    m = {n: given["m_" + n] for n in WEIGHT_NAMES}
    v = {n: given["v_" + n] for n in WEIGHT_NAMES}
    depth = p.shape[0]
    dev = 4 * lax.axis_index("x") + 2 * lax.axis_index("y") + lax.axis_index("c")

    wc = {n: _to_comm(n, w[n]) for n in COMM_NAMES}
    wb = {n: wc[n].astype(BF16) for n in COMM_NAMES}
    conv_local = [w[CONV_NAME], m[CONV_NAME], v[CONV_NAME]]
    conv_rows = _pack_rows(sum(int(np.prod(t.shape)) for t in conv_local), 8)
    conv_g, = _all_gather([_pack(conv_local, conv_rows)], 0, "gather_conv_taps")
    conv_parts = zip(*[_unpack(conv_g[j], [t.shape for t in conv_local]) for j in range(N_DEV)])
    conv_w, conv_m, conv_v = [jnp.concatenate(parts, axis=2) for parts in conv_parts]
    small = {n: w[n] for n in SMALL_NAMES}

    def whole(blocks):
        return {n: t.reshape(-1, t.shape[-1]) for n, t in zip(COMM_NAMES, blocks)}

    def own_slot(block):
        return lax.dynamic_update_slice_in_dim(jnp.zeros((N_DEV,) + block.shape, block.dtype), block[None], dev, 0)

    first = _all_gather([wb[n][0] for n in COMM_NAMES], 0, "gather_weights_0")
    in_flight = {}
    for i in range(1, depth):
        srcs, first = lax.optimization_barrier(([wb[n][i] for n in COMM_NAMES], first))
        in_flight[i] = _exchange_start(srcs, [own_slot(s) for s in srcs], True, "gather_weights_%d_start" % i)
        small["norm_attn_g"] = small["norm_attn_g"] + in_flight[i][4][0, 0]

    def layer_weights(i, h):
        if i == 0:
            got = whole(first)
        else:
            send_sems, recv_sems, srcs, lands, _ = in_flight.pop(i)
            got = whole(_exchange_wait(send_sems, recv_sems, srcs, lands, h, True, "gather_weights_%d_wait" % i))
        return dict(got, **{CONV_NAME: conv_w[i]})

    def as_slots(g, n):
        return g.reshape((N_DEV,) + wc[n].shape[1:])

    def layer_done(i, g, small_now):
        if i == 0:
            return small_now
        srcs = [as_slots(g[n], n) for n in COMM_NAMES]
        lands = [own_slot(lax.dynamic_index_in_dim(s, dev, 0, keepdims=False)) for s in srcs]
        in_flight[i] = _exchange_start(srcs, lands, False, "scatter_weight_grads_%d_start" % i)
        return dict(small_now, norm_ple_g=small_now["norm_ple_g"] + in_flight[i][4][0, 0])

    loss, dx, big_grads, grads = _local_step(x[0], p[:, 0], loss_target[0], layer_weights, small, layer_done)
    loss = lax.psum(loss, ("x", "y", "c"))

    recv = [_all_to_all([as_slots(big_grads[0][n], n) for n in COMM_NAMES], "scatter_weight_grads_0")]
    for i in range(1, depth):
        send_sems, recv_sems, srcs, lands, _ = in_flight.pop(i)
        recv.append(_exchange_wait(send_sems, recv_sems, srcs, lands, dx, False, "scatter_weight_grads_%d_wait" % i))
    rep_names = SMALL_NAMES + (CONV_NAME,)
    rep_w = dict({n: w[n] for n in SMALL_NAMES}, **{CONV_NAME: conv_w})
    rep_m = dict({n: m[n] for n in SMALL_NAMES}, **{CONV_NAME: conv_m})
    rep_v = dict({n: v[n] for n in SMALL_NAMES}, **{CONV_NAME: conv_v})
    rep_shapes = [rep_w[n].shape for n in rep_names]
    rep_rows = _pack_rows(sum(int(np.prod(s)) for s in rep_shapes), SMALL_TILE_ROWS)
    rep_parts, = _all_gather([_pack([grads[n] for n in rep_names], rep_rows)], 0, "gather_small_grads")

    out = {}
    for t, n in enumerate(COMM_NAMES):
        res = _adamw([recv[i][t] for i in range(depth)], wc[n], _to_comm(n, m[n]), _to_comm(n, v[n]),
                     "adamw_" + n, _tile_rows(wc[n].shape[1], 256))
        out[n] = [_to_comm(n, r) for r in res]
    rep_out = _adamw([rep_parts], *[_pack([src[n] for n in rep_names], rep_rows)[None] for src in (rep_w, rep_m, rep_v)],
                     "adamw_replicated", SMALL_TILE_ROWS)
    for n, vals in zip(rep_names, zip(*[_unpack(r[0], rep_shapes) for r in rep_out])):
        out[n] = list(vals)
    shard = ffn_conv_w.shape[2]
    out[CONV_NAME] = [lax.dynamic_slice_in_dim(t, dev * shard, shard, axis=2) for t in out[CONV_NAME]]
    results = [[out[n][kind] for n in WEIGHT_NAMES] for kind in range(4)]
    return (loss, dx[None], *results[0], *results[1], *results[2], *results[3])
```

```python
import math

import numpy as np
import jax
import jax.numpy as jnp
from jax import lax
from jax.experimental import pallas as pl
from jax.experimental.pallas import tpu as pltpu

F32 = jnp.float32
BF16 = jnp.bfloat16

D_MODEL = 1024
HEAD_DIM = 64
N_HEADS = 8
ATTN_W = 512
SGU_W = 256
SGU_GROUPS = 4
SGU_CHUNK = 128
SSM_W = 256
SSM_GROUPS = 16
SSM_CH = 16
SSM_STATE = 64
SSM_NS = SSM_GROUPS * SSM_STATE
IN_W = 2304
D_FF = 2816
PLE_DIM = 256
BRANCHES = ((128, 1), (512, 4), (2048, 16))
BLK = 128
N_BUCKETS = 32
REL_MAX = 2048
EPS = 1e-6
NEG_INF = -1e30
N_DEV = 8

ADAM_LR = 0.001
ADAM_B1 = 0.9
ADAM_B2 = 0.999
ADAM_EPS = 1e-08
ADAM_WD = 0.01
ADAM_STEP = 10

VMEM_LIMIT_BYTES = 56 * 1024 * 1024
GELU_C = math.sqrt(2.0 / math.pi)

SMALL_NAMES = ("rel_bias", "norm_attn_g", "sgu_ln_g", "sgu_ln_b", "sgu_w", "sgu_b", "ssm_a_re", "ssm_a_im",
               "ssm_log_dt", "ssm_b_re", "ssm_b_im", "ssm_c_re", "ssm_c_im", "ssm_d", "ssm_glu_b",
               "branch_norm_g", "norm_ffn_g", "ffn_conv_b", "norm_ple_g", "final_norm_g")
WEIGHT_NAMES = ("rel_bias", "norm_attn_g", "w_in", "sgu_ln_g", "sgu_ln_b", "sgu_w", "sgu_b", "ssm_a_re",
                "ssm_a_im", "ssm_log_dt", "ssm_b_re", "ssm_b_im", "ssm_c_re", "ssm_c_im", "ssm_d", "ssm_glu_w",
                "ssm_glu_b", "branch_norm_g", "w_out", "norm_ffn_g", "ffn_w_up", "ffn_conv_w", "ffn_conv_b",
                "ffn_w_down", "norm_ple_g", "ple_w_gate", "ple_w_proj", "final_norm_g")
PACK_COLS = 512


def _params(sem):
    return pltpu.CompilerParams(dimension_semantics=sem, vmem_limit_bytes=VMEM_LIMIT_BYTES)


def _pick(dim, target):
    if dim <= target:
        return dim
    best = None
    for t in range(128, target + 1, 128):
        if dim % t == 0:
            best = t
    return dim if best is None else best


def _gelu(x):
    return 0.5 * x * (1.0 + jnp.tanh(GELU_C * (x + 0.044715 * (x * x * x))))


def _gelu_grad(x):
    t = jnp.tanh(GELU_C * (x + 0.044715 * (x * x * x)))
    return 0.5 * (1.0 + t) + 0.5 * x * (1.0 - t * t) * (GELU_C * (1.0 + 3.0 * 0.044715 * (x * x)))


def _sigmoid(x):
    return 1.0 / (1.0 + jnp.exp(-x))


_DIMS = {"nn": (((1,), (0,)), ((), ())), "tn": (((0,), (0,)), ((), ())), "nt": (((1,), (1,)), ((), ()))}


def _mm(a, b, mode, name, add=None, out_dtype=F32, a_cols=None, b_cols=None, tm=1408, tn=1408, tk=1408):
    if mode == "nn":
        m, k = a.shape
        k2, n = b.shape
    elif mode == "tn":
        k, m = a.shape
        k2, n = b.shape
    else:
        m, k = a.shape
        n, k2 = b.shape
    if a_cols is not None:
        assert mode != "tn"
        k = a_cols[1]
    if b_cols is not None:
        assert mode != "nt"
        n = b_cols[1]
    assert k == k2, (name, a.shape, b.shape, mode)
    tm, tn, tk = _pick(m, tm), _pick(n, tn), _pick(k, tk)
    nk = k // tk
    a_off = 0 if a_cols is None else a_cols[0] // tk
    b_off = 0 if b_cols is None else b_cols[0] // tn
    assert a_cols is None or a_cols[0] % tk == 0
    assert b_cols is None or b_cols[0] % tn == 0
    dims = _DIMS[mode]
    has_add = add is not None

    def body(*refs):
        if has_add:
            a_ref, b_ref, add_ref, o_ref = refs[:4]
        else:
            a_ref, b_ref, o_ref = refs[:3]
        part = lax.dot_general(a_ref[...].astype(BF16), b_ref[...].astype(BF16), dims,
                               preferred_element_type=F32)

        def finish(r):
            if has_add:
                r = r + add_ref[...]
            o_ref[...] = r.astype(out_dtype)

        if nk == 1:
            finish(part)
            return
        acc_ref = refs[-1]
        kk = pl.program_id(2)

        @pl.when(kk == 0)
        def _():
            acc_ref[...] = part

        @pl.when((kk > 0) & (kk < nk - 1))
        def _():
            acc_ref[...] += part

        @pl.when(kk == nk - 1)
        def _():
            finish(acc_ref[...] + part)

    if mode == "tn":
        a_spec = pl.BlockSpec((tk, tm), lambda i, j, kk: (kk, i))
    else:
        a_spec = pl.BlockSpec((tm, tk), lambda i, j, kk: (i, kk + a_off))
    if mode == "nt":
        b_spec = pl.BlockSpec((tn, tk), lambda i, j, kk: (j, kk))
    else:
        b_spec = pl.BlockSpec((tk, tn), lambda i, j, kk: (kk, j + b_off))
    o_spec = pl.BlockSpec((tm, tn), lambda i, j, kk: (i, j))
    in_specs = [a_spec, b_spec] + ([o_spec] if has_add else [])
    args = (a, b) + ((add,) if has_add else ())
    return pl.pallas_call(
        body, name=name, grid=(m // tm, n // tn, nk),
        in_specs=in_specs, out_specs=o_spec,
        out_shape=jax.ShapeDtypeStruct((m, n), out_dtype),
        scratch_shapes=[pltpu.VMEM((tm, tn), F32)] if nk > 1 else [],
        compiler_params=_params(("parallel", "parallel", "arbitrary")),
    )(*args)


def _rb(tm, w, cb=0):
    return pl.BlockSpec((tm, w), lambda i: (i, cb))


def _fb(shape):
    nd = len(shape)
    return pl.BlockSpec(shape, lambda i: (0,) * nd)


def _rowcall(body, name, n_rows, tm, in_specs, args, out_specs, out_shapes):
    return pl.pallas_call(
        body, name=name, grid=(n_rows // tm,), in_specs=in_specs, out_specs=out_specs, out_shape=out_shapes,
        compiler_params=_params(("arbitrary",)),
    )(*args)


def _sds(shape, dtype=F32):
    return jax.ShapeDtypeStruct(shape, dtype)


def _rms_fwd(h, g, name, tm=512):
    s, d = h.shape

    def body(h_ref, g_ref, o_ref):
        x = h_ref[...]
        r = lax.rsqrt(jnp.mean(x * x, axis=-1, keepdims=True) + EPS)
        o_ref[...] = (x * r * g_ref[...]).astype(BF16)

    return _rowcall(body, name, s, tm, [_rb(tm, d), _fb((1, d))], (h, g.reshape(1, d)), _rb(tm, d),
                    _sds((s, d), BF16))


def _rms_bwd(da, h, g, dres, name, tm=512):
    s, d = h.shape

    def body(da_ref, h_ref, g_ref, dres_ref, dh_ref, dg_ref):
        @pl.when(pl.program_id(0) == 0)
        def _():
            dg_ref[...] = jnp.zeros_like(dg_ref)

        x = h_ref[...]
        r = lax.rsqrt(jnp.mean(x * x, axis=-1, keepdims=True) + EPS)
        xh = x * r
        dy = da_ref[...]
        dg_ref[...] += jnp.sum(dy * xh, axis=0, keepdims=True)
        dxh = dy * g_ref[...]
        dh_ref[...] = dres_ref[...] + r * (dxh - xh * jnp.mean(dxh * xh, axis=-1, keepdims=True))

    dh, dg = _rowcall(body, name, s, tm, [_rb(tm, d), _rb(tm, d), _fb((1, d)), _rb(tm, d)],
                      (da, h, g.reshape(1, d), dres), [_rb(tm, d), _fb((1, d))], [_sds((s, d)), _sds((1, d))])
    return dh, dg.reshape(d)


def _loss_head(h, target, g, name, tm=512):
    s, d = h.shape

    def body(h_ref, t_ref, g_ref, dh_ref, loss_ref, dg_ref):
        @pl.when(pl.program_id(0) == 0)
        def _():
            dg_ref[...] = jnp.zeros_like(dg_ref)
            loss_ref[...] = jnp.zeros_like(loss_ref)

        x = h_ref[...]
        r = lax.rsqrt(jnp.mean(x * x, axis=-1, keepdims=True) + EPS)
        xh = x * r
        gg = g_ref[...]
        err = xh * gg - t_ref[...]
        loss_ref[...] += jnp.sum(err * err) * (0.5 / d)
        dy = err * (1.0 / d)
        dg_ref[...] += jnp.sum(dy * xh, axis=0, keepdims=True)
        dxh = dy * gg
        dh_ref[...] = r * (dxh - xh * jnp.mean(dxh * xh, axis=-1, keepdims=True))

    dh, loss, dg = _rowcall(body, name, s, tm, [_rb(tm, d), _rb(tm, d), _fb((1, d))], (h, target, g.reshape(1, d)),
                            [_rb(tm, d), _fb((1, 128)), _fb((1, d))], [_sds((s, d)), _sds((1, 128)), _sds((1, d))])
    return dh, loss[0, 0], dg.reshape(d)


_MIX_PARTS = ((0, 512), (512, 768), (768, 1024))


def _mix_fwd(ya, ysg, yss, g, name, tm=512):
    s = ya.shape[0]

    def body(a_ref, b_ref, c_ref, g_ref, o_ref):
        for ref, (lo, hi) in zip((a_ref, b_ref, c_ref), _MIX_PARTS):
            y = ref[...]
            r = lax.rsqrt(jnp.mean(y * y, axis=-1, keepdims=True) + EPS)
            o_ref[:, lo:hi] = (y * r * g_ref[:, lo:hi]).astype(BF16)

    return _rowcall(body, name, s, tm, [_rb(tm, 512), _rb(tm, 256), _rb(tm, 256), _fb((1, 1024))],
                    (ya, ysg, yss, g.reshape(1, 1024)), _rb(tm, 1024), _sds((s, 1024), BF16))


def _mix_bwd(dmix, ya, ysg, yss, g, name, tm=512):
    s = ya.shape[0]

    def body(dm_ref, a_ref, b_ref, c_ref, g_ref, da_ref, db_ref, dc_ref, dg_ref):
        @pl.when(pl.program_id(0) == 0)
        def _():
            dg_ref[...] = jnp.zeros_like(dg_ref)

        for ref, dref, (lo, hi) in zip((a_ref, b_ref, c_ref), (da_ref, db_ref, dc_ref), _MIX_PARTS):
            y = ref[...]
            r = lax.rsqrt(jnp.mean(y * y, axis=-1, keepdims=True) + EPS)
            xh = y * r
            dm = dm_ref[:, lo:hi]
            dg_ref[:, lo:hi] += jnp.sum(dm * xh, axis=0, keepdims=True)
            dxh = dm * g_ref[:, lo:hi]
            dref[...] = r * (dxh - xh * jnp.mean(dxh * xh, axis=-1, keepdims=True))

    da, db, dc, dg = _rowcall(
        body, name, s, tm, [_rb(tm, 1024), _rb(tm, 512), _rb(tm, 256), _rb(tm, 256), _fb((1, 1024))],
        (dmix, ya, ysg, yss, g.reshape(1, 1024)),
        [_rb(tm, 512), _rb(tm, 256), _rb(tm, 256), _fb((1, 1024))],
        [_sds((s, 512)), _sds((s, 256)), _sds((s, 256)), _sds((1, 1024))])
    return da, db, dc, dg.reshape(1024)


def _ssm_post_fwd(yc, z, d, gw, gb, name, tm=1024):
    s = yc.shape[0]

    def body(yc_ref, u_ref, d_ref, gw_ref, gb_ref, o_ref):
        y1 = yc_ref[...] + d_ref[...] * u_ref[...]
        y2 = _gelu(y1)
        gl = jnp.dot(y2.astype(BF16), gw_ref[...], preferred_element_type=F32) + gb_ref[...]
        o_ref[...] = y2 * _sigmoid(gl)

    return _rowcall(body, name, s, tm, [_rb(tm, 256), _rb(tm, 256, 8), _fb((1, 256)), _fb((256, 256)), _fb((1, 256))],
                    (yc, z, d.reshape(1, 256), gw, gb.reshape(1, 256)), _rb(tm, 256), _sds((s, 256)))


def _ssm_post_bwd(dy, yc, z, d, gw, gb, name, tm=1024):
    s = yc.shape[0]

    def body(dy_ref, yc_ref, u_ref, d_ref, gw_ref, gb_ref, dy1_ref, dgl_ref, y2_ref, dud_ref, dd_ref, dgb_ref):
        @pl.when(pl.program_id(0) == 0)
        def _():
            dd_ref[...] = jnp.zeros_like(dd_ref)
            dgb_ref[...] = jnp.zeros_like(dgb_ref)

        u = u_ref[...]
        dd = d_ref[...]
        y1 = yc_ref[...] + dd * u
        y2 = _gelu(y1)
        gw_v = gw_ref[...]
        gl = jnp.dot(y2.astype(BF16), gw_v, preferred_element_type=F32) + gb_ref[...]
        sg = _sigmoid(gl)
        dyv = dy_ref[...]
        dgl = dyv * y2 * sg * (1.0 - sg)
        dy2 = dyv * sg + lax.dot_general(dgl.astype(BF16), gw_v, _DIMS["nt"], preferred_element_type=F32)
        dy1 = dy2 * _gelu_grad(y1)
        dy1_ref[...] = dy1.astype(BF16)
        dgl_ref[...] = dgl.astype(BF16)
        y2_ref[...] = y2.astype(BF16)
        dud_ref[...] = dy1 * dd
        dd_ref[...] += jnp.sum(dy1 * u, axis=0, keepdims=True)
        dgb_ref[...] += jnp.sum(dgl, axis=0, keepdims=True)

    outs = _rowcall(
        body, name, s, tm,
        [_rb(tm, 256), _rb(tm, 256), _rb(tm, 256, 8), _fb((1, 256)), _fb((256, 256)), _fb((1, 256))],
        (dy, yc, z, d.reshape(1, 256), gw, gb.reshape(1, 256)),
        [_rb(tm, 256)] * 4 + [_fb((1, 256))] * 2,
        [_sds((s, 256), BF16)] * 3 + [_sds((s, 256))] + [_sds((1, 256))] * 2)
    dy1, dgl, y2, dud, dd, dgb = outs
    return dy1, dgl, y2, dud, dd.reshape(256), dgb.reshape(256)


SCAN_T = 512
SCAN_C = 512
N_SCAN_TABLES = 6


def _scan_tables(pr, pi, reverse):
    ns = pr.shape[0]
    sign = -1.0 if reverse else 1.0
    power = [(jnp.ones((ns,), F32), jnp.zeros((ns,), F32))] + [(pr[:, k], sign * pi[:, k]) for k in range(8)]
    zero = (jnp.zeros((ns,), F32), jnp.zeros((ns,), F32))

    def table(exponents):
        rows = [zero if e is None else power[e] for e in exponents]
        return jnp.stack([jnp.concatenate(row) for row in rows])

    tabs = []
    for k in (1, 2, 4):
        has_partner = [(s < 8 - k) if reverse else (s >= k) for s in range(8)]
        tabs.append(table([k if ok else None for ok in has_partner]))
    tabs.append(table([s if reverse else 7 - s for s in range(8)]))
    tabs.append(table([8 - s if reverse else s + 1 for s in range(8)]))
    tabs.append(table([8] * 8))
    return jnp.stack(tabs)


def _cmul(ar, ai, br, bi):
    return ar * br - ai * bi, ar * bi + ai * br


def _scan_group(ur, ui, cr, ci, tr_ref, ti_ref, reverse):
    xr, xi = ur, ui
    for n, k in enumerate((1, 2, 4)):
        shift = 8 - k if reverse else k
        pr, pi = _cmul(tr_ref[n], ti_ref[n], pltpu.roll(xr, shift, axis=0), pltpu.roll(xi, shift, axis=0))
        xr, xi = xr + pr, xi + pi
    sr, si = _cmul(tr_ref[3], ti_ref[3], ur, ui)
    for k in (1, 2, 4):
        sr, si = sr + pltpu.roll(sr, k, axis=0), si + pltpu.roll(si, k, axis=0)
    pr, pi = _cmul(tr_ref[4], ti_ref[4], cr, ci)
    nr, ni = _cmul(tr_ref[5], ti_ref[5], cr, ci)
    return xr + pr, xi + pi, nr + sr, ni + si


def _scan_specs(ns, n_t, reverse):
    ncb = ns // SCAN_C
    tmap = (lambda t: n_t - 1 - t) if reverse else (lambda t: t)
    re = pl.BlockSpec((SCAN_T, SCAN_C), lambda j, t: (tmap(t), j))
    im = pl.BlockSpec((SCAN_T, SCAN_C), lambda j, t: (tmap(t), j + ncb))
    tab_re = pl.BlockSpec((N_SCAN_TABLES, 8, SCAN_C), lambda j, t: (0, 0, j))
    tab_im = pl.BlockSpec((N_SCAN_TABLES, 8, SCAN_C), lambda j, t: (0, 0, j + ncb))
    return ncb, tmap, re, im, tab_re, tab_im


def _scan_fwd(bu, tabs, name):
    s, two_ns = bu.shape
    ns = two_ns // 2
    n_t = s // SCAN_T
    ncb, _, re, im, tab_re, tab_im = _scan_specs(ns, n_t, False)

    def body(ur_ref, ui_ref, tr_ref, ti_ref, xr_ref, xi_ref, cr_ref, ci_ref):
        @pl.when(pl.program_id(1) == 0)
        def _():
            cr_ref[...] = jnp.zeros_like(cr_ref)
            ci_ref[...] = jnp.zeros_like(ci_ref)

        def group(g, carry):
            rows = pl.ds(pl.multiple_of(g * 8, 8), 8)
            xr, xi, cr, ci = _scan_group(ur_ref[rows, :], ui_ref[rows, :], *carry, tr_ref, ti_ref, False)
            xr_ref[rows, :] = xr
            xi_ref[rows, :] = xi
            return cr, ci

        cr, ci = lax.fori_loop(0, SCAN_T // 8, group, (cr_ref[...], ci_ref[...]), unroll=2)
        cr_ref[...] = cr
        ci_ref[...] = ci

    out = pl.BlockSpec((SCAN_T, SCAN_C), lambda j, t: (t, j))
    return pl.pallas_call(
        body, name=name, grid=(ncb, n_t), in_specs=[re, im, tab_re, tab_im], out_specs=[out, out],
        out_shape=[_sds((s, ns)), _sds((s, ns))], scratch_shapes=[pltpu.VMEM((8, SCAN_C), F32)] * 2,
        compiler_params=_params(("parallel", "arbitrary")),
    )(bu, bu, tabs, tabs)


def _scan_bwd(g, xr, xi, tabs, name):
    s, two_ns = g.shape
    ns = two_ns // 2
    n_t = s // SCAN_T
    ncb, tmap, re, im, tab_re, tab_im = _scan_specs(ns, n_t, True)
    n_groups = SCAN_T // 8

    def body(gr_ref, gi_ref, tr_ref, ti_ref, xr_ref, xi_ref, pxr_ref, pxi_ref,
             lr_ref, li_ref, dar_ref, dai_ref, cr_ref, ci_ref, ar_ref, ai_ref, sxr_ref, sxi_ref):
        t = pl.program_id(1)

        @pl.when(t == 0)
        def _():
            for ref in (cr_ref, ci_ref, ar_ref, ai_ref):
                ref[...] = jnp.zeros_like(ref)

        has_before = (t < n_t - 1).astype(F32)
        sxr_ref[0:8, :] = pxr_ref[...] * has_before
        sxi_ref[0:8, :] = pxi_ref[...] * has_before
        sxr_ref[8:, :] = xr_ref[...]
        sxi_ref[8:, :] = xi_ref[...]
        first_row = lax.broadcasted_iota(jnp.int32, (8, SCAN_C), 0) == 0

        def group(k, carry):
            cr, ci, ar, ai = carry
            g8 = pl.multiple_of((n_groups - 1 - k) * 8, 8)
            rows = pl.ds(g8, 8)
            lr, li, cr, ci = _scan_group(gr_ref[rows, :], gi_ref[rows, :], cr, ci, tr_ref, ti_ref, True)
            lr_ref[rows, :] = lr
            li_ref[rows, :] = li
            here, before = pl.ds(g8 + 8, 8), rows
            pr = jnp.where(first_row, pltpu.roll(sxr_ref[before, :], 1, axis=0), pltpu.roll(sxr_ref[here, :], 1, axis=0))
            pi = jnp.where(first_row, pltpu.roll(sxi_ref[before, :], 1, axis=0), pltpu.roll(sxi_ref[here, :], 1, axis=0))
            return cr, ci, ar + lr * pr + li * pi, ai + li * pr - lr * pi

        cr, ci, ar, ai = lax.fori_loop(0, n_groups, group,
                                       (cr_ref[...], ci_ref[...], ar_ref[...], ai_ref[...]), unroll=2)
        cr_ref[...] = cr
        ci_ref[...] = ci
        ar_ref[...] = ar
        ai_ref[...] = ai

        @pl.when(t == n_t - 1)
        def _():
            for k in (1, 2, 4):
                ar_ref[...] += pltpu.roll(ar_ref[...], k, axis=0)
                ai_ref[...] += pltpu.roll(ai_ref[...], k, axis=0)
            dar_ref[...] = ar_ref[...]
            dai_ref[...] = ai_ref[...]

    x_spec = pl.BlockSpec((SCAN_T, SCAN_C), lambda j, t: (tmap(t), j))
    before_spec = pl.BlockSpec((8, SCAN_C), lambda j, t: (jnp.maximum(tmap(t) * (SCAN_T // 8) - 1, 0), j))
    acc_spec = pl.BlockSpec((8, SCAN_C), lambda j, t: (0, j))
    lr, li, dar, dai = pl.pallas_call(
        body, name=name, grid=(ncb, n_t),
        in_specs=[re, im, tab_re, tab_im, x_spec, x_spec, before_spec, before_spec],
        out_specs=[x_spec, x_spec, acc_spec, acc_spec],
        out_shape=[_sds((s, ns)), _sds((s, ns)), _sds((8, ns)), _sds((8, ns))],
        scratch_shapes=[pltpu.VMEM((8, SCAN_C), F32)] * 4 + [pltpu.VMEM((SCAN_T + 8, SCAN_C), F32)] * 2,
        compiler_params=_params(("parallel", "arbitrary")),
    )(g, g, tabs, tabs, xr, xi, xr, xi)
    return lr, li, dar[0], dai[0]


def _group_ids():
    return lax.broadcasted_iota(jnp.int32, (1, SGU_W), 1) // 64


def _group_mean(val, gid):
    out = jnp.zeros_like(val)
    for g in range(SGU_GROUPS):
        mg = gid == g
        out = jnp.where(mg, jnp.sum(jnp.where(mg, val, 0.0), axis=1, keepdims=True) * (1.0 / 64), out)
    return out


def _causal_w(w_ref, g):
    t = lax.broadcasted_iota(jnp.int32, (SGU_CHUNK, SGU_CHUNK), 0)
    s = lax.broadcasted_iota(jnp.int32, (SGU_CHUNK, SGU_CHUNK), 1)
    return jnp.where(t >= s, w_ref[g], 0.0).astype(BF16)


def _sgu_core(x, lng, lnb, w_ref, bexp, gid):
    zz = _gelu(x)
    u = zz[:, :SGU_W]
    v = zz[:, SGU_W:]
    vc = v - _group_mean(v, gid)
    rstd = lax.rsqrt(_group_mean(vc * vc, gid) + EPS)
    vhat = vc * rstd
    vn = vhat * lng + lnb
    vnb = vn.astype(BF16)
    mixed = bexp
    for g in range(SGU_GROUPS):
        mm = jnp.dot(_causal_w(w_ref, g), vnb, preferred_element_type=F32)
        mixed = jnp.where(gid == g, mm + bexp, mixed)
    return u, rstd, vhat, vnb, mixed


def _sgu_fwd(z, lng, lnb, w, bexp, name, tm=512):
    s = z.shape[0]

    def body(z_ref, lng_ref, lnb_ref, w_ref, b_ref, o_ref):
        gid = _group_ids()
        for j in range(tm // SGU_CHUNK):
            rows = pl.ds(j * SGU_CHUNK, SGU_CHUNK)
            u, _, _, _, mixed = _sgu_core(z_ref[rows, :], lng_ref[...], lnb_ref[...], w_ref, b_ref[...], gid)
            o_ref[rows, :] = u * mixed

    return _rowcall(body, name, s, tm,
                    [_rb(tm, 512, 3), _fb((1, 256)), _fb((1, 256)), _fb((4, 128, 128)), _fb((128, 256))],
                    (z, lng.reshape(1, 256), lnb.reshape(1, 256), w, bexp), _rb(tm, 256), _sds((s, 256)))


def _sgu_bwd(z, dy, lng, lnb, w, bexp, name, tm=512):
    s = z.shape[0]

    def body(z_ref, dy_ref, lng_ref, lnb_ref, w_ref, b_ref, dz_ref, dw_ref, db_ref, dlng_ref, dlnb_ref):
        @pl.when(pl.program_id(0) == 0)
        def _():
            dw_ref[...] = jnp.zeros_like(dw_ref)
            db_ref[...] = jnp.zeros_like(db_ref)
            dlng_ref[...] = jnp.zeros_like(dlng_ref)
            dlnb_ref[...] = jnp.zeros_like(dlnb_ref)

        gid = _group_ids()
        t = lax.broadcasted_iota(jnp.int32, (SGU_CHUNK, SGU_CHUNK), 0)
        sidx = lax.broadcasted_iota(jnp.int32, (SGU_CHUNK, SGU_CHUNK), 1)
        lng_v = lng_ref[...]
        for j in range(tm // SGU_CHUNK):
            rows = pl.ds(j * SGU_CHUNK, SGU_CHUNK)
            x = z_ref[rows, :]
            u, rstd, vhat, vnb, mixed = _sgu_core(x, lng_v, lnb_ref[...], w_ref, b_ref[...], gid)
            dyv = dy_ref[rows, :]
            dmixed = dyv * u
            du = dyv * mixed
            db_ref[...] += dmixed
            dvn = jnp.zeros_like(dmixed)
            for g in range(SGU_GROUPS):
                dmg = jnp.where(gid == g, dmixed, 0.0).astype(BF16)
                dvn = dvn + lax.dot_general(_causal_w(w_ref, g), dmg, _DIMS["tn"], preferred_element_type=F32)
                dwg = lax.dot_general(dmg, vnb, _DIMS["nt"], preferred_element_type=F32)
                dw_ref[g] += jnp.where(t >= sidx, dwg, 0.0)
            dlnb_ref[...] += jnp.sum(dvn, axis=0, keepdims=True)
            dlng_ref[...] += jnp.sum(dvn * vhat, axis=0, keepdims=True)
            dvh = dvn * lng_v
            dv = rstd * (dvh - _group_mean(dvh, gid) - vhat * _group_mean(dvh * vhat, gid))
            gg = _gelu_grad(x)
            dz_ref[rows, 0:SGU_W] = du * gg[:, :SGU_W]
            dz_ref[rows, SGU_W:2 * SGU_W] = dv * gg[:, SGU_W:]

    dz, dw, db, dlng, dlnb = _rowcall(
        body, name, s, tm,
        [_rb(tm, 512, 3), _rb(tm, 256), _fb((1, 256)), _fb((1, 256)), _fb((4, 128, 128)), _fb((128, 256))],
        (z, dy, lng.reshape(1, 256), lnb.reshape(1, 256), w, bexp),
        [_rb(tm, 512), _fb((4, 128, 128)), _fb((128, 256)), _fb((1, 256)), _fb((1, 256))],
        [_sds((s, 512)), _sds((4, 128, 128)), _sds((128, 256)), _sds((1, 256)), _sds((1, 256))])
    return dz, dw, db, dlng.reshape(256), dlnb.reshape(256)


CONV_TC = 1408
N_CT = D_FF // CONV_TC


def _row_of(block8, j):
    r = lax.broadcasted_iota(jnp.int32, block8.shape, 0)
    return jnp.sum(jnp.where(r == j, block8, 0.0), axis=0, keepdims=True)


EDGE = 16


def _conv_fwd(hu, cw, cb, name, tm=256):
    s = hu.shape[0]
    n8 = tm // 8

    def body(xv_ref, xg_ref, tv_ref, tg_ref, wv_ref, wg_ref, bv_ref, bg_ref, hv_ref, hg_ref, act_ref):
        has_prev = (pl.program_id(1) > 0).astype(F32)
        row = lax.broadcasted_iota(jnp.int32, (EDGE, CONV_TC), 0)

        def conv(x_ref, t_ref, w_ref, b_ref):
            x = x_ref[...]
            w0, w1, w2, bb = w_ref[0:1, :], w_ref[1:2, :], w_ref[2:3, :], b_ref[...]
            whole = w0 * pltpu.roll(x, 2, axis=0) + w1 * pltpu.roll(x, 1, axis=0) + w2 * x + bb
            r7 = _row_of(t_ref[...], 7) * has_prev
            r6 = _row_of(t_ref[...], 6) * has_prev
            xe = x_ref[0:EDGE, :]
            x1 = jnp.where(row == 0, r7, pltpu.roll(xe, 1, axis=0))
            x2 = jnp.where(row == 0, r6, jnp.where(row == 1, r7, pltpu.roll(xe, 2, axis=0)))
            return whole, w0 * x2 + w1 * x1 + w2 * xe + bb

        hv, hv_edge = conv(xv_ref, tv_ref, wv_ref, bv_ref)
        hg, hg_edge = conv(xg_ref, tg_ref, wg_ref, bg_ref)
        hv_ref[...] = hv.astype(BF16)
        hg_ref[...] = hg.astype(BF16)
        act_ref[...] = (_gelu(hg) * hv).astype(BF16)
        hv_ref[0:EDGE, :] = hv_edge.astype(BF16)
        hg_ref[0:EDGE, :] = hg_edge.astype(BF16)
        act_ref[0:EDGE, :] = (_gelu(hg_edge) * hv_edge).astype(BF16)

    def xs(off):
        return pl.BlockSpec((tm, CONV_TC), lambda j, i: (i, j + off))

    def ts(off):
        return pl.BlockSpec((8, CONV_TC), lambda j, i: (jnp.maximum(i * n8 - 1, 0), j + off))

    def ws(rows, off):
        return pl.BlockSpec((rows, CONV_TC), lambda j, i: (0, j + off))

    o_spec = pl.BlockSpec((tm, CONV_TC), lambda j, i: (i, j))
    return pl.pallas_call(
        body, name=name, grid=(N_CT, s // tm),
        in_specs=[xs(0), xs(N_CT), ts(0), ts(N_CT), ws(3, 0), ws(3, N_CT), ws(1, 0), ws(1, N_CT)],
        out_specs=[o_spec] * 3, out_shape=[_sds((s, D_FF), BF16)] * 3,
        compiler_params=_params(("parallel", "arbitrary")),
    )(hu, hu, hu, hu, cw, cw, cb.reshape(1, 2 * D_FF), cb.reshape(1, 2 * D_FF))


HALO = EDGE


def _conv_bwd(dact, hv, hg, hu, cw, name, tm=256):
    s = dact.shape[0]
    n8 = tm // 8

    def body(da_ref, dan_ref, hv_ref, hvn_ref, hg_ref, hgn_ref, x_ref, t_ref, w_ref, dx_ref, dw_ref, db_ref, d_scr):
        i = pl.program_id(1)
        is_value = pl.program_id(0) < N_CT

        @pl.when(i == 0)
        def _():
            dw_ref[...] = jnp.zeros_like(dw_ref)
            db_ref[...] = jnp.zeros_like(db_ref)

        for rows, (a_ref, v_ref, g_ref) in ((pl.ds(0, tm), (da_ref, hv_ref, hg_ref)),
                                            (pl.ds(tm, HALO), (dan_ref, hvn_ref, hgn_ref))):
            @pl.when(is_value)
            def _():
                d_scr[rows, :] = a_ref[...].astype(F32) * _gelu(g_ref[...].astype(F32))

            @pl.when(jnp.logical_not(is_value))
            def _():
                d_scr[rows, :] = (a_ref[...].astype(F32) * v_ref[...].astype(F32)
                                  * _gelu_grad(g_ref[...].astype(F32)))

        has_prev = (i > 0).astype(F32)
        has_next = (i < s // tm - 1).astype(F32)
        w0, w1, w2 = w_ref[0:1, :], w_ref[1:2, :], w_ref[2:3, :]
        d = d_scr[0:tm, :]
        dx_ref[...] = (w2 * d + w1 * pltpu.roll(d, tm - 1, axis=0) + w0 * pltpu.roll(d, tm - 2, axis=0)).astype(BF16)
        row = lax.broadcasted_iota(jnp.int32, (EDGE, CONV_TC), 0)
        nxt = d_scr[tm:tm + HALO, :]
        n0 = _row_of(nxt, 0) * has_next
        n1 = _row_of(nxt, 1) * has_next
        de = d_scr[tm - EDGE:tm, :]
        d1 = jnp.where(row == EDGE - 1, n0, pltpu.roll(de, EDGE - 1, axis=0))
        d2 = jnp.where(row == EDGE - 2, n0, jnp.where(row == EDGE - 1, n1, pltpu.roll(de, EDGE - 2, axis=0)))
        dx_ref[tm - EDGE:tm, :] = (w2 * de + w1 * d1 + w0 * d2).astype(BF16)
        x = x_ref[...]
        r7 = _row_of(t_ref[...], 7) * has_prev
        r6 = _row_of(t_ref[...], 6) * has_prev
        last = x_ref[tm - 8:tm, :]
        l7, l6 = _row_of(last, 7), _row_of(last, 6)
        head = d_scr[0:8, :]
        d0, d1h = _row_of(head, 0), _row_of(head, 1)
        dw_ref[0:1, :] += (jnp.sum(d * pltpu.roll(x, 2, axis=0), axis=0, keepdims=True)
                           + d0 * (r6 - l6) + d1h * (r7 - l7))
        dw_ref[1:2, :] += jnp.sum(d * pltpu.roll(x, 1, axis=0), axis=0, keepdims=True) + d0 * (r7 - l7)
        dw_ref[2:3, :] += jnp.sum(d * x, axis=0, keepdims=True)
        db_ref[...] += jnp.sum(d, axis=0, keepdims=True)

    a_spec = pl.BlockSpec((tm, CONV_TC), lambda j, i: (i, j % N_CT))
    an_spec = pl.BlockSpec((HALO, CONV_TC),
                           lambda j, i: (jnp.minimum((i + 1) * (tm // HALO), s // HALO - 1), j % N_CT))
    x_spec = pl.BlockSpec((tm, CONV_TC), lambda j, i: (i, j))
    t_spec = pl.BlockSpec((8, CONV_TC), lambda j, i: (jnp.maximum(i * n8 - 1, 0), j))
    w_spec = pl.BlockSpec((3, CONV_TC), lambda j, i: (0, j))
    db_spec = pl.BlockSpec((1, CONV_TC), lambda j, i: (0, j))
    return pl.pallas_call(
        body, name=name, grid=(2 * N_CT, s // tm),
        in_specs=[a_spec, an_spec, a_spec, an_spec, a_spec, an_spec, x_spec, t_spec, w_spec],
        out_specs=[x_spec, w_spec, db_spec],
        out_shape=[_sds((s, 2 * D_FF), BF16), _sds((3, 2 * D_FF)), _sds((1, 2 * D_FF))],
        scratch_shapes=[pltpu.VMEM((tm + HALO, CONV_TC), F32)],
        compiler_params=_params(("parallel", "arbitrary")),
    )(dact, dact, hv, hv, hg, hg, hu, hu, cw)


def _ple_fwd(h, gp, pp, name, tm=512):
    s, d = h.shape

    def body(h_ref, g_ref, p_ref, o_ref):
        o_ref[...] = h_ref[...] + _sigmoid(g_ref[...]) * p_ref[...]

    return _rowcall(body, name, s, tm, [_rb(tm, d)] * 3, (h, gp, pp), _rb(tm, d), _sds((s, d)))


def _ple_bwd(dh, gp, pp, name, tm=512):
    s, d = dh.shape

    def body(d_ref, g_ref, p_ref, dp_ref, dg_ref):
        sg = _sigmoid(g_ref[...])
        dv = d_ref[...]
        dp_ref[...] = (dv * sg).astype(BF16)
        dg_ref[...] = (dv * p_ref[...] * sg * (1.0 - sg)).astype(BF16)

    return _rowcall(body, name, s, tm, [_rb(tm, d)] * 3, (dh, gp, pp), [_rb(tm, d)] * 2,
                    [_sds((s, d), BF16)] * 2)


SCALE = HEAD_DIM ** -0.5
ATT_ROWS = 2048


def _att_geom(s, dil):
    w = min(ATT_ROWS, s)
    p = BLK * dil
    assert w % p == 0 and s % w == 0
    return w, p, w // p


def _rows(start, dil):
    return pl.ds(start, BLK, stride=dil) if dil > 1 else pl.ds(start, BLK)


def _head_masks():
    lane = lax.broadcasted_iota(jnp.int32, (1, BLK), 1)
    return [lane < HEAD_DIM, lane >= HEAD_DIM]


def _band():
    rel = np.arange(BLK)[:, None] + BLK - np.arange(2 * BLK)[None, :]
    return (rel >= 0) & (rel <= BLK)


def _zcur(w):
    return lambda off: pl.BlockSpec((w, BLK), lambda hp, i: (i, off + hp))


def _zprev(p, nb):
    return lambda off: pl.BlockSpec((p, BLK), lambda hp, i: (jnp.maximum(i * nb - 1, 0), off + hp))


def _scur(w):
    return pl.BlockSpec((w, BLK), lambda hp, i: (i, hp))


def _pair_rows(t, masks):
    return jnp.concatenate([jnp.where(masks[0], t, 0.0), jnp.where(masks[1], t, 0.0)], axis=0).astype(BF16)


def _pair_bias_bwd(bias):
    return bias.reshape(4, 2, BLK, 2, BLK).transpose(0, 3, 2, 1, 4).reshape(4, 2, BLK, 2 * BLK)


def _unpair_bias_bwd(db):
    return db.reshape(4, 2, BLK, 2, BLK).transpose(0, 3, 2, 1, 4).reshape(N_HEADS, BLK, 2 * BLK)


def _attn_fwd(z, bias, state, dil, first, last, name):
    s = z.shape[0]
    w, p, nb = _att_geom(s, dil)

    def body(*refs):
        q_ref, kp_ref, kc_ref, vp_ref, vc_ref, b_ref = refs[:6]
        rest = refs[6:]
        if not first:
            m_ref, l_ref, a_ref = rest[:3]
            rest = rest[3:]
        i = pl.program_id(1)
        masks = _head_masks()
        own_block = lax.broadcasted_iota(jnp.int32, (1, 2 * BLK), 1) >= BLK
        for r in range(dil):
            for b in range(nb):
                rows = _rows(r + p * b, dil)
                prev_rows = _rows(r + p * (b - 1), dil) if b > 0 else _rows(r, dil)
                kprev, vprev = (kc_ref, vc_ref) if b > 0 else (kp_ref, vp_ref)
                q = q_ref[rows, :] * SCALE
                k = jnp.concatenate([kprev[prev_rows, :], kc_ref[rows, :]], axis=0).astype(BF16)
                v = jnp.concatenate([vprev[prev_rows, :], vc_ref[rows, :]], axis=0).astype(BF16)
                mb = lb = ob = None
                for hh, mh in enumerate(masks):
                    qh = jnp.where(mh, q, 0.0).astype(BF16)
                    sc = lax.dot_general(qh, k, _DIMS["nt"], preferred_element_type=F32) + b_ref[hh]
                    if b == 0:
                        sc = jnp.where(own_block | (i > 0), sc, NEG_INF)
                    mx = jnp.max(sc, axis=1, keepdims=True)
                    e = jnp.exp(sc - mx)
                    den = jnp.sum(e, axis=1, keepdims=True)
                    o = jnp.dot(e.astype(BF16), v, preferred_element_type=F32)
                    if hh == 0:
                        mb = jnp.broadcast_to(mx, (BLK, BLK))
                        lb = jnp.broadcast_to(den, (BLK, BLK))
                        ob = o
                    else:
                        mb = jnp.where(mh, mx, mb)
                        lb = jnp.where(mh, den, lb)
                        ob = jnp.where(mh, o, ob)
                if first:
                    m_new, l_new, a_new = mb, lb, ob
                else:
                    m_old = m_ref[rows, :]
                    m_new = jnp.maximum(m_old, mb)
                    al = jnp.exp(m_old - m_new)
                    be = jnp.exp(mb - m_new)
                    l_new = al * l_ref[rows, :] + be * lb
                    a_new = al * a_ref[rows, :] + be * ob
                if last:
                    y_ref, lse_ref = rest
                    y_ref[rows, :] = a_new / l_new
                    lse_ref[rows, :] = m_new + jnp.log(l_new)
                else:
                    mo_ref, lo_ref, ao_ref = rest
                    mo_ref[rows, :] = m_new
                    lo_ref[rows, :] = l_new
                    ao_ref[rows, :] = a_new

    cur, prv = _zcur(w), _zprev(p, nb)
    b_spec = pl.BlockSpec((2, BLK, 2 * BLK), lambda hp, i: (hp, 0, 0))
    in_specs = [cur(0), prv(4), cur(4), prv(8), cur(8), b_spec]
    args = [z, z, z, z, z, bias]
    if not first:
        in_specs += [_scur(w)] * 3
        args += list(state)
    n_out = 2 if last else 3
    return pl.pallas_call(
        body, name=name, grid=(4, s // w), in_specs=in_specs, out_specs=[_scur(w)] * n_out,
        out_shape=[_sds((s, ATTN_W))] * n_out,
        compiler_params=_params(("parallel", "parallel")),
    )(*args)


def _row_stats(mh, dy, y, lse):
    delta = jnp.sum(jnp.where(mh, dy * y, 0.0), axis=1, keepdims=True)
    lse_h = jnp.max(jnp.where(mh, lse, NEG_INF), axis=1, keepdims=True)
    return delta, lse_h


def _attn_bwd(z, bias, dy, y, lse, prev, dil, name):
    s = z.shape[0]
    w, p, nb = _att_geom(s, dil)
    n_steps = s // w
    first = prev is None

    def body(*refs):
        q_ref, kp_ref, kc_ref, vp_ref, vc_ref, b_ref, dy_ref, y_ref, lse_ref = refs[:9]
        rest = refs[9:]
        if not first:
            dqp_ref, dkp_ref, dvp_ref = rest[:3]
            rest = rest[3:]
        dq_ref, dk_ref, dv_ref, dkx_ref, dvx_ref, db_ref = rest
        i = pl.program_id(1)

        @pl.when(i == 0)
        def _():
            db_ref[...] = jnp.zeros_like(db_ref)

        masks = _head_masks()
        first_head = lax.broadcasted_iota(jnp.int32, (1, 2 * BLK), 1) < BLK

        def flush(rows, dk, dv):
            if not first:
                dk = dk + dkp_ref[rows, :]
                dv = dv + dvp_ref[rows, :]
            dk_ref[rows, :] = dk
            dv_ref[rows, :] = dv

        for r in range(dil):
            carry = None
            for b in range(nb):
                rows = _rows(r + p * b, dil)
                prev_rows = _rows(r + p * (b - 1), dil) if b > 0 else _rows(r, dil)
                kprev, vprev = (kc_ref, vc_ref) if b > 0 else (kp_ref, vp_ref)
                keys = [(_pair_rows(kprev[prev_rows, :], masks), _pair_rows(vprev[prev_rows, :], masks)),
                        (_pair_rows(kc_ref[rows, :], masks), _pair_rows(vc_ref[rows, :], masks))]
                q = (q_ref[rows, :] * SCALE).astype(BF16)
                dy_v = dy_ref[rows, :]
                dyb = dy_v.astype(BF16)
                stats = [_row_stats(mh, dy_v, y_ref[rows, :], lse_ref[rows, :]) for mh in masks]
                delta = jnp.where(first_head, stats[0][0], stats[1][0])
                lse_h = jnp.where(first_head, stats[0][1], stats[1][1])
                dq = jnp.zeros((BLK, BLK), F32)
                dk, dv = [], []
                for half in range(2):
                    kh, vh = keys[half]
                    sc = lax.dot_general(q, kh, _DIMS["nt"], preferred_element_type=F32) + b_ref[half]
                    pr = jnp.exp(sc - lse_h)
                    if b == 0 and half == 0:
                        pr = pr * (i > 0).astype(F32)
                    dp = lax.dot_general(dyb, vh, _DIMS["nt"], preferred_element_type=F32)
                    ds = pr * (dp - delta)
                    db_ref[half] += ds
                    dsb = ds.astype(BF16)
                    dq = dq + jnp.dot(dsb, kh, preferred_element_type=F32)
                    dk2 = lax.dot_general(dsb, q, _DIMS["tn"], preferred_element_type=F32)
                    dv2 = lax.dot_general(pr.astype(BF16), dyb, _DIMS["tn"], preferred_element_type=F32)
                    dk.append(jnp.where(masks[0], dk2[:BLK], dk2[BLK:]))
                    dv.append(jnp.where(masks[0], dv2[:BLK], dv2[BLK:]))
                dq = dq * SCALE
                if not first:
                    dq = dq + dqp_ref[rows, :]
                dq_ref[rows, :] = dq
                if b > 0:
                    flush(prev_rows, carry[0] + dk[0], carry[1] + dv[0])
                else:
                    dkx_ref[prev_rows, :] = dk[0]
                    dvx_ref[prev_rows, :] = dv[0]
                carry = (dk[1], dv[1])
            flush(_rows(r + p * (nb - 1), dil), *carry)

    cur, prv = _zcur(w), _zprev(p, nb)
    b_spec = pl.BlockSpec((None, 2, BLK, 2 * BLK), lambda hp, i: (hp, 0, 0, 0))
    in_specs = [cur(0), prv(4), cur(4), prv(8), cur(8), b_spec] + [_scur(w)] * 3
    args = [z, z, z, z, z, bias, dy, y, lse]
    if not first:
        in_specs += [_scur(w)] * 3
        args += list(prev)
    x_spec = pl.BlockSpec((p, BLK), lambda hp, i: (i, hp))
    *outs, db = pl.pallas_call(
        body, name=name, grid=(4, n_steps), in_specs=in_specs,
        out_specs=[_scur(w)] * 3 + [x_spec] * 2 + [b_spec],
        out_shape=[_sds((s, ATTN_W))] * 3 + [_sds((n_steps * p, ATTN_W))] * 2 + [_sds((4, 2, BLK, 2 * BLK))],
        compiler_params=_params(("parallel", "arbitrary")),
    )(*args)
    return (*outs, _unpair_bias_bwd(db))


ASM_ROWS = 512


def _assemble_dz(dq, dk, dv, extras, dzs, du, name):
    s = dq.shape[0]
    w = min(ATT_ROWS, s)
    n_steps = s // w
    per_step = w // ASM_ROWS
    assert w % ASM_ROWS == 0

    def body(*refs):
        dq_ref, dk_ref, dv_ref, dzs_ref, du_ref = refs[:5]
        x_refs = refs[5:5 + 2 * len(extras)]
        o_ref, acc_ref = refs[-2:]
        j = pl.program_id(0)
        step = j // per_step
        has_next = (step < n_steps - 1).astype(F32)
        last_of_step = ((j + 1) % per_step == 0).astype(F32)
        o_ref[:, 0:ATTN_W] = dq_ref[...].astype(BF16)
        o_ref[:, 3 * ATTN_W:3 * ATTN_W + 2 * SGU_W] = dzs_ref[...].astype(BF16)
        o_ref[:, 3 * ATTN_W + 2 * SGU_W:IN_W] = du_ref[...].astype(BF16)
        for part, (base_ref, col) in enumerate(((dk_ref, ATTN_W), (dv_ref, 2 * ATTN_W))):
            acc_ref[...] = base_ref[...]
            for n, (_, dil) in enumerate(BRANCHES):
                rows = min(BLK * dil, ASM_ROWS)
                scale = has_next if BLK * dil >= w else has_next * last_of_step
                acc_ref[ASM_ROWS - rows:, :] += x_refs[2 * n + part][...] * scale
            o_ref[:, col:col + ATTN_W] = acc_ref[...].astype(BF16)

    def x_spec(dil):
        p = BLK * dil
        rows = min(p, ASM_ROWS)
        blocks_per_step = p // rows
        total = n_steps * blocks_per_step

        def idx(j):
            step = j // per_step
            within = (j % per_step) - (per_step - blocks_per_step)
            return (jnp.clip((step + 1) * blocks_per_step + jnp.maximum(within, 0), 0, total - 1), 0)

        return pl.BlockSpec((rows, ATTN_W), idx)

    in_specs = [_rb(ASM_ROWS, ATTN_W)] * 3 + [_rb(ASM_ROWS, 2 * SGU_W), _rb(ASM_ROWS, SSM_W)]
    args = [dq, dk, dv, dzs, du]
    for (dkx, dvx), (_, dil) in zip(extras, BRANCHES):
        in_specs += [x_spec(dil)] * 2
        args += [dkx, dvx]
    return pl.pallas_call(
        body, name=name, grid=(s // ASM_ROWS,), in_specs=in_specs, out_specs=_rb(ASM_ROWS, IN_W),
        out_shape=_sds((s, IN_W), BF16), scratch_shapes=[pltpu.VMEM((ASM_ROWS, ATTN_W), F32)],
        compiler_params=_params(("parallel",)),
    )(*args)


def _t5_bucket(dist):
    max_exact = N_BUCKETS // 2
    d = np.maximum(dist, 0)
    large = max_exact + (np.log(np.maximum(d, 1) / max_exact) / np.log(REL_MAX / max_exact)
                         * (N_BUCKETS - max_exact)).astype(np.int32)
    large = np.minimum(large, N_BUCKETS - 1)
    return np.where(d < max_exact, d, large).astype(np.int32)


def _bias_tables(rel_bias):
    period = 3 * BLK
    tabs = []
    for _, dil in BRANCHES:
        onehot = np.zeros((period, N_BUCKETS), np.float32)
        d = np.arange(BLK + 1)
        onehot[d, _t5_bucket((BLK - d) * dil)] = 1.0
        f = jnp.dot(jnp.asarray(onehot), rel_bias, precision=lax.Precision.HIGHEST)
        flat = jnp.tile(f.T, (1, BLK))[:, :BLK * (period - 1)]
        tab = flat.reshape(N_HEADS, BLK, period - 1)[:, :, :2 * BLK]
        tabs.append(jnp.where(_band()[None], tab, NEG_INF))
    return tabs


def _bucket_onehot():
    maps = []
    q = np.arange(BLK)[:, None]
    k = np.arange(2 * BLK)[None, :]
    rel = q + BLK - k
    for _, dil in BRANCHES:
        maps.append(np.where((rel >= 0) & (rel <= BLK), _t5_bucket(rel * dil), -1).reshape(-1))
    bmap = jnp.asarray(np.concatenate(maps).astype(np.int32))
    return (bmap[:, None] == jnp.arange(128, dtype=jnp.int32)[None, :]).astype(BF16)


def _block_diag(t):
    g, n, c = t.shape
    eye = jnp.eye(g, dtype=t.dtype)
    return (t[:, :, None, :] * eye[:, None, :, None]).reshape(g * n, g * c)


def _ssm_prep(a_re, a_im, log_dt, b_re, b_im, c_re, c_im):
    lam = lax.complex(a_re, a_im)
    dt = jnp.exp(log_dt)[:, None]
    a_bar = jnp.exp(lam * dt)
    b_bar = ((a_bar - 1.0) / lam)[:, :, None] * lax.complex(b_re, b_im)
    bdt = jnp.concatenate([_block_diag(jnp.real(b_bar)), _block_diag(jnp.imag(b_bar))], axis=0)
    cd = jnp.concatenate([_block_diag(jnp.transpose(c_re, (0, 2, 1))),
                          _block_diag(-jnp.transpose(c_im, (0, 2, 1)))], axis=0)
    return jnp.real(a_bar).reshape(-1), jnp.imag(a_bar).reshape(-1), bdt, cd


def _powers(ar, ai):
    pr, pi = ar[:, None], ai[:, None]
    k = 1
    while k < 8:
        lr, li = pr[:, -1:], pi[:, -1:]
        pr, pi = (jnp.concatenate([pr, pr * lr - pi * li], axis=1),
                  jnp.concatenate([pi, pr * li + pi * lr], axis=1))
        k *= 2
    return pr, pi


def _sgu_bias_expand(b):
    return jnp.repeat(b.T, 64, axis=1)


def _layer_fwd(i, h, p_i, big, small, bias_tabs):
    nm = "l%d_" % i
    sv = {"h": h}
    a1 = _rms_fwd(h, small["norm_attn_g"][i], nm + "rms_attn")
    z = _mm(a1, big["w_in"], "nt", nm + "in_proj")
    st = None
    for b, (_, dil) in enumerate(BRANCHES):
        st = _attn_fwd(z, bias_tabs[b][0], st, dil, b == 0, b == len(BRANCHES) - 1, nm + "attn_fwd%d" % b)
    y_attn, lse = st
    bexp = _sgu_bias_expand(small["sgu_b"][i])
    y_sgu = _sgu_fwd(z, small["sgu_ln_g"][i], small["sgu_ln_b"][i], small["sgu_w"][i], bexp, nm + "sgu_fwd")
    ar, ai, bdt, cd = _ssm_prep(*[small[k][i] for k in ("ssm_a_re", "ssm_a_im", "ssm_log_dt", "ssm_b_re",
                                                         "ssm_b_im", "ssm_c_re", "ssm_c_im")])
    u_cols = (IN_W - SSM_W, SSM_W)
    bu = _mm(z, bdt, "nt", nm + "ssm_bu", a_cols=u_cols)
    xr, xi = _scan_fwd(bu, _scan_tables(*_powers(ar, ai), False), nm + "ssm_scan")
    yc = _mm(xr, cd[:SSM_NS], "nn", nm + "ssm_cx_re")
    yc = _mm(xi, cd[SSM_NS:], "nn", nm + "ssm_cx_im", add=yc)
    y_ssm = _ssm_post_fwd(yc, z, small["ssm_d"][i], big["ssm_glu_w"], small["ssm_glu_b"][i], nm + "ssm_post")
    mix = _mix_fwd(y_attn, y_sgu, y_ssm, small["branch_norm_g"][i], nm + "mix")
    h2 = _mm(mix, big["w_out"], "nn", nm + "out_proj", add=h)
    a2 = _rms_fwd(h2, small["norm_ffn_g"][i], nm + "rms_ffn")
    hu = _mm(a2, big["ffn_w_up"], "nt", nm + "ffn_up")
    hv, hg, act = _conv_fwd(hu, big["ffn_conv_w"], small["ffn_conv_b"][i], nm + "ffn_conv")
    h3 = _mm(act, big["ffn_w_down"], "nn", nm + "ffn_down", add=h2)
    a3 = _rms_fwd(h3, small["norm_ple_g"][i], nm + "rms_ple")
    gp = _mm(a3, big["ple_w_gate"], "nn", nm + "ple_gate")
    pp = _mm(p_i, big["ple_w_proj"], "nt", nm + "ple_proj")
    h4 = _ple_fwd(h3, gp, pp, nm + "ple_add")
    sv.update(a1=a1, z=z, y_attn=y_attn, lse=lse, y_sgu=y_sgu, y_ssm=y_ssm, yc=yc, xr=xr, xi=xi, mix=mix, h2=h2,
              a2=a2, hu=hu, hv=hv, hg=hg, act=act, h3=h3, a3=a3, gp=gp, pp=pp)
    return h4, sv


def _layer_bwd(i, dh4, sv, p_i, big, small, bias_tabs, ffn_done=None):
    nm = "l%d_" % i
    g = {}
    dpp, dgp = _ple_bwd(dh4, sv["gp"], sv["pp"], nm + "ple_bwd")
    g["ple_w_proj"] = _mm(dpp, p_i, "tn", nm + "d_ple_proj", out_dtype=BF16)
    g["ple_w_gate"] = _mm(sv["a3"], dgp, "tn", nm + "d_ple_gate", out_dtype=BF16)
    da3 = _mm(dgp, big["ple_w_gate"], "nt", nm + "ple_gate_t")
    dh3, g["norm_ple_g"] = _rms_bwd(da3, sv["h3"], small["norm_ple_g"][i], dh4, nm + "rms_ple_bwd")
    g["ffn_w_down"] = _mm(sv["act"], dh3, "tn", nm + "d_ffn_down", out_dtype=BF16)
    dact = _mm(dh3, big["ffn_w_down"], "nt", nm + "ffn_down_t", out_dtype=BF16)
    dhu, g["ffn_conv_w"], dcb = _conv_bwd(dact, sv["hv"], sv["hg"], sv["hu"], big["ffn_conv_w"],
                                          nm + "ffn_conv_bwd")
    g["ffn_conv_b"] = dcb.reshape(2 * D_FF)
    g["ffn_w_up"] = _mm(dhu, sv["a2"], "tn", nm + "d_ffn_up", out_dtype=BF16)
    da2 = _mm(dhu, big["ffn_w_up"], "nn", nm + "ffn_up_t")
    dh2, g["norm_ffn_g"] = _rms_bwd(da2, sv["h2"], small["norm_ffn_g"][i], dh3, nm + "rms_ffn_bwd")
    if ffn_done is not None:
        small = ffn_done(g, small)
    g["w_out"] = _mm(sv["mix"], dh2, "tn", nm + "d_out_proj", out_dtype=BF16)
    dmix = _mm(dh2, big["w_out"], "nt", nm + "out_proj_t")
    dya, dysg, dyss, g["branch_norm_g"] = _mix_bwd(dmix, sv["y_attn"], sv["y_sgu"], sv["y_ssm"],
                                                   small["branch_norm_g"][i], nm + "mix_bwd")
    ssm_keys = ("ssm_a_re", "ssm_a_im", "ssm_log_dt", "ssm_b_re", "ssm_b_im", "ssm_c_re", "ssm_c_im")
    (ar, ai, bdt, cd), prep_vjp = jax.vjp(_ssm_prep, *[small[k][i] for k in ssm_keys])
    dy1, dgl, y2, dud, g["ssm_d"], g["ssm_glu_b"] = _ssm_post_bwd(
        dyss, sv["yc"], sv["z"], small["ssm_d"][i], big["ssm_glu_w"], small["ssm_glu_b"][i], nm + "ssm_post_bwd")
    g["ssm_glu_w"] = _mm(y2, dgl, "tn", nm + "d_ssm_glu", out_dtype=BF16)
    g_x = _mm(dy1, cd, "nt", nm + "ssm_cx_t")
    dcd = jnp.concatenate([_mm(sv["xr"], dy1, "tn", nm + "d_ssm_c_re"),
                           _mm(sv["xi"], dy1, "tn", nm + "d_ssm_c_im")], axis=0)
    lr, li, dar, dai = _scan_bwd(g_x, sv["xr"], sv["xi"], _scan_tables(*_powers(ar, ai), True),
                                 nm + "ssm_scan_bwd")
    u_cols = (IN_W - SSM_W, SSM_W)
    dbdt = jnp.concatenate([_mm(lr, sv["z"], "tn", nm + "d_ssm_b_re", b_cols=u_cols),
                            _mm(li, sv["z"], "tn", nm + "d_ssm_b_im", b_cols=u_cols)], axis=0)
    du = _mm(lr, bdt[:SSM_NS], "nn", nm + "ssm_bu_t_re", add=dud)
    du = _mm(li, bdt[SSM_NS:], "nn", nm + "ssm_bu_t_im", add=du)
    for k, val in zip(ssm_keys, prep_vjp((dar, dai, dbdt, dcd))):
        g[k] = val
    bexp, bexp_vjp = jax.vjp(_sgu_bias_expand, small["sgu_b"][i])
    dzs, g["sgu_w"], dbexp, g["sgu_ln_g"], g["sgu_ln_b"] = _sgu_bwd(
        sv["z"], dysg, small["sgu_ln_g"][i], small["sgu_ln_b"][i], small["sgu_w"][i], bexp, nm + "sgu_bwd")
    g["sgu_b"] = bexp_vjp(dbexp)[0]
    prev = None
    dbs, extras = [], []
    for b, (_, dil) in enumerate(BRANCHES):
        dq, dk, dv, dkx, dvx, db = _attn_bwd(sv["z"], bias_tabs[b][1], dya, sv["y_attn"], sv["lse"], prev, dil,
                                             nm + "attn_bwd%d" % b)
        prev = (dq, dk, dv)
        extras.append((dkx, dvx))
        dbs.append(db.reshape(N_HEADS, BLK * 2 * BLK))
    dz = _assemble_dz(dq, dk, dv, extras, dzs, du, nm + "assemble_dz")
    g["w_in"] = _mm(dz, sv["a1"], "tn", nm + "d_in_proj", out_dtype=BF16)
    da1 = _mm(dz, big["w_in"], "nn", nm + "in_proj_t")
    dh, g["norm_attn_g"] = _rms_bwd(da1, sv["h"], small["norm_attn_g"][i], dh2, nm + "rms_attn_bwd")
    return dh, g, jnp.concatenate(dbs, axis=1)


def _local_step(x, p, target, layer_weights, small, layer_done=None):
    depth = p.shape[0]
    bias_tabs = [(t, _pair_bias_bwd(t)) for t in _bias_tables(small["rel_bias"])]
    h = x
    saved, bigs = [], []
    for i in range(depth):
        bigs.append(layer_weights(i, h))
        h, sv = _layer_fwd(i, h, p[i], bigs[i], small, bias_tabs)
        saved.append(sv)
    dh, loss, g_final = _loss_head(h, target, small["final_norm_g"], "loss_head")
    layer_grads = [None] * depth
    dbias = [None] * depth
    for i in reversed(range(depth)):
        ffn_done = None if layer_done is None else (lambda g, sm, i=i: layer_done(i, "ffn", g, sm))
        dh, layer_grads[i], dbias[i] = _layer_bwd(i, dh, saved[i], p[i], bigs[i], small, bias_tabs, ffn_done)
        if layer_done is not None:
            small = layer_done(i, "all", layer_grads[i], small)
    big_grads = [{k: lg.pop(k) for k in COMM_NAMES} for lg in layer_grads]
    grads = {k: jnp.stack([layer_grads[i][k] for i in range(depth)]) for k in layer_grads[0]}
    grads["final_norm_g"] = g_final
    g_rb = _mm(sum(dbias[1:], dbias[0]), _bucket_onehot(), "nn", "d_rel_bias", tk=2048)
    grads["rel_bias"] = g_rb[:, :N_BUCKETS].T
    return loss, dh, big_grads, grads


_ANY = pl.BlockSpec(memory_space=pl.ANY)
MESH_IDS = pl.DeviceIdType.MESH


def _slot(ref, axis, j):
    return ref.at[(slice(None),) * axis + (j,)]


def _all_gather(blocks, axis, name):
    nt = len(blocks)

    def body(*refs):
        x_refs, o_refs = refs[:nt], refs[nt:2 * nt]
        send_sems, recv_sems, local_sems = refs[2 * nt:]
        x, y, c = lax.axis_index("x"), lax.axis_index("y"), lax.axis_index("c")
        me, sibling = (x, y, c), (x, y, 1 - c)
        chips = [(1 - x, y), (x, 1 - y), (1 - x, 1 - y)]

        def slot(t, px, py, pc):
            return _slot(o_refs[t], axis, 4 * px + 2 * py + pc)

        def copy(t, k, blk, to, src=None):
            return pltpu.make_async_remote_copy(
                src_ref=slot(t, *blk) if src is None else src, dst_ref=slot(t, *blk),
                send_sem=send_sems.at[7 * t + k], recv_sem=recv_sems.at[7 * t + k],
                device_id=to, device_id_type=MESH_IDS)

        mine = [pltpu.make_async_copy(x_refs[t], slot(t, *me), local_sems.at[t]) for t in range(nt)]
        for cp in mine:
            cp.start()
        first = []
        for t in range(nt):
            first.append(copy(t, 0, me, sibling, src=x_refs[t]))
            first += [copy(t, 1 + j, me, (*chip, c), src=x_refs[t]) for j, chip in enumerate(chips)]
        for cp in first:
            cp.start()
        passed = []
        for t in range(nt):
            for j, chip in enumerate(chips):
                copy(t, 1 + j, (*chip, c), me).wait_recv()
                passed.append(copy(t, 4 + j, (*chip, c), sibling))
                passed[-1].start()
        for t in range(nt):
            copy(t, 0, sibling, me).wait_recv()
            for j, chip in enumerate(chips):
                copy(t, 4 + j, (*chip, 1 - c), me).wait_recv()
        for cp in first + passed:
            cp.wait_send()
        for cp in mine:
            cp.wait()

    out_shape = [jax.ShapeDtypeStruct(b.shape[:axis] + (N_DEV,) + b.shape[axis:], b.dtype) for b in blocks]
    return pl.pallas_call(
        body, name=name, out_shape=out_shape, in_specs=[_ANY] * nt, out_specs=[_ANY] * nt,
        scratch_shapes=[pltpu.SemaphoreType.DMA((7 * nt,)), pltpu.SemaphoreType.DMA((7 * nt,)),
                        pltpu.SemaphoreType.DMA((nt,))],
    )(*blocks)


def _peer(k):
    x, y, c = lax.axis_index("x"), lax.axis_index("y"), lax.axis_index("c")
    px = 1 - x if k & 4 else x
    py = 1 - y if k & 2 else y
    pc = 1 - c if k & 1 else c
    return (px, py, pc), 4 * px + 2 * py + pc


def _all_to_all(blocks, name):
    nt = len(blocks)

    def body(*refs):
        x_refs, o_refs = refs[:nt], refs[nt:2 * nt]
        send_sems, recv_sems, local_sems = refs[2 * nt:]
        _, me = _peer(0)
        mine = [pltpu.make_async_copy(x_refs[t].at[me], o_refs[t].at[me], local_sems.at[t]) for t in range(nt)]
        for cp in mine:
            cp.start()
        copies = []
        for k in range(1, N_DEV):
            peer, idx = _peer(k)
            for t in range(nt):
                cp = pltpu.make_async_remote_copy(
                    src_ref=x_refs[t].at[idx], dst_ref=o_refs[t].at[me],
                    send_sem=send_sems.at[7 * t + k - 1], recv_sem=recv_sems.at[7 * t + k - 1],
                    device_id=peer, device_id_type=MESH_IDS)
                cp.start()
                copies.append(cp)
        for cp in copies:
            cp.wait()
        for cp in mine:
            cp.wait()

    return pl.pallas_call(
        body, name=name, out_shape=[jax.ShapeDtypeStruct(b.shape, b.dtype) for b in blocks],
        in_specs=[_ANY] * nt, out_specs=[_ANY] * nt,
        scratch_shapes=[pltpu.SemaphoreType.DMA((7 * nt,)), pltpu.SemaphoreType.DMA((7 * nt,)),
                        pltpu.SemaphoreType.DMA((nt,))],
    )(*blocks)


_HBM = pl.BlockSpec(memory_space=pltpu.HBM)
_SEM = pl.BlockSpec(memory_space=pltpu.SEMAPHORE)
_EFFECT = pltpu.SideEffectType.DATAFLOW_SIDE_EFFECTING


def _split_copy(src_ref, land_ref, send_sems, recv_sems, t, k, gather):
    peer, idx = _peer(k)
    _, me = _peer(0)
    return pltpu.make_async_remote_copy(
        src_ref=src_ref if gather else src_ref.at[idx], dst_ref=land_ref.at[me],
        send_sem=send_sems.at[7 * t + k - 1], recv_sem=recv_sems.at[7 * t + k - 1],
        device_id=peer, device_id_type=MESH_IDS)


def _exchange_start(srcs, lands, gather, name):
    nt = len(srcs)

    def body(*refs):
        src_refs, land_refs = refs[:nt], refs[nt:2 * nt]
        send_sems, recv_sems = refs[2 * nt:2 * nt + 2]
        token = refs[-1]
        for k in range(1, N_DEV):
            for t in range(nt):
                _split_copy(src_refs[t], land_refs[t], send_sems, recv_sems, t, k, gather).start()
        token[...] = jnp.zeros_like(token)

    hbm = lambda a: pltpu.HBM(a.shape, a.dtype)
    outs = pl.pallas_call(
        body, name=name,
        out_shape=(pltpu.SemaphoreType.DMA((7 * nt,)), pltpu.SemaphoreType.DMA((7 * nt,)),
                   *[hbm(a) for a in srcs], *[hbm(a) for a in lands], jax.ShapeDtypeStruct((8, 128), F32)),
        in_specs=[_HBM] * (2 * nt),
        out_specs=(_SEM, _SEM, *[_HBM] * (2 * nt), pl.BlockSpec(memory_space=pltpu.VMEM)),
        input_output_aliases={j: 2 + j for j in range(2 * nt)},
        compiler_params=pltpu.CompilerParams(has_side_effects=_EFFECT),
    )(*[pltpu.with_memory_space_constraint(a, pltpu.HBM) for a in list(srcs) + list(lands)])
    return outs[0], outs[1], outs[2:2 + nt], outs[2 + nt:2 + 2 * nt], outs[-1]


def _exchange_wait(send_sems, recv_sems, srcs, lands, after, gather, name):
    nt = len(srcs)

    def body(*refs):
        src_refs, land_refs = refs[:nt], refs[nt:2 * nt]
        send_sems, recv_sems = refs[2 * nt:2 * nt + 2]
        for k in range(1, N_DEV):
            _, idx = _peer(k)
            for t in range(nt):
                _split_copy(src_refs[t], land_refs[t], send_sems, recv_sems, t, k, gather).wait_send()
                arrival = pltpu.make_async_remote_copy(
                    src_ref=land_refs[t].at[idx], dst_ref=land_refs[t].at[idx],
                    send_sem=send_sems.at[7 * t + k - 1], recv_sem=recv_sems.at[7 * t + k - 1],
                    device_id=_peer(k)[0], device_id_type=MESH_IDS)
                arrival.wait_recv()

    hbm = lambda a: pltpu.HBM(a.shape, a.dtype)
    outs = pl.pallas_call(
        body, name=name, out_shape=tuple(hbm(a) for a in list(srcs) + list(lands)),
        in_specs=[_HBM] * (2 * nt) + [_SEM, _SEM, _ANY], out_specs=tuple([_HBM] * (2 * nt)),
        input_output_aliases={j: j for j in range(2 * nt)},
        compiler_params=pltpu.CompilerParams(has_side_effects=_EFFECT),
    )(*srcs, *lands, send_sems, recv_sems, after)
    return outs[nt:]


def _adamw(parts, w, m, v, name, tr):
    n_layers, r, c_ = w.shape
    assert len(parts) == n_layers

    def body(*refs):
        p_refs = refs[:n_layers]
        w_ref, m_ref, v_ref, g_ref, d_ref, mo_ref, vo_ref = refs[n_layers:]

        def update(p_ref):
            g = p_ref[0].astype(F32)
            for j in range(1, N_DEV):
                g = g + p_ref[j].astype(F32)
            m2 = ADAM_B1 * m_ref[...] + (1.0 - ADAM_B1) * g
            v2 = ADAM_B2 * v_ref[...] + (1.0 - ADAM_B2) * (g * g)
            m_hat = m2 / (1.0 - ADAM_B1 ** ADAM_STEP)
            v_hat = v2 / (1.0 - ADAM_B2 ** ADAM_STEP)
            g_ref[...] = g
            d_ref[...] = -ADAM_LR * (m_hat / (jnp.sqrt(v_hat) + ADAM_EPS) + ADAM_WD * w_ref[...])
            mo_ref[...] = m2
            vo_ref[...] = v2

        for layer in range(n_layers):
            pl.when(pl.program_id(0) == layer)(lambda layer=layer: update(p_refs[layer]))

    spec = pl.BlockSpec((None, tr, c_), lambda l, i: (l, i, 0))
    p_spec = pl.BlockSpec((N_DEV, tr, c_), lambda l, i: (0, i, 0))
    return pl.pallas_call(
        body, name=name, grid=(n_layers, r // tr), in_specs=[p_spec] * n_layers + [spec] * 3,
        out_specs=[spec] * 4, out_shape=[_sds((n_layers, r, c_))] * 4,
        compiler_params=_params(("parallel", "parallel")),
    )(*parts, w, m, v)


def _pack_rows(n_elems, align):
    rows = -(-n_elems // PACK_COLS)
    return -(-rows // align) * align


def _pack(arrs, rows, dtype=F32):
    flat = jnp.concatenate([a.reshape(-1) for a in arrs]).astype(dtype)
    return jnp.pad(flat, (0, rows * PACK_COLS - flat.shape[0])).reshape(rows, PACK_COLS)


def _unpack(pack, shapes):
    flat = pack.reshape(-1)
    out, off = [], 0
    for shp in shapes:
        size = int(np.prod(shp))
        out.append(flat[off:off + size].reshape(shp))
        off += size
    return out


def _tile_rows(rows, target, align=16):
    best = align
    for t in range(align, target + 1, align):
        if rows % t == 0:
            best = t
    return best


COMM_NAMES = ("w_in", "ssm_glu_w", "w_out", "ffn_w_up", "ffn_w_down", "ple_w_gate", "ple_w_proj")
COMM_TRANSPOSED = ("w_in", "ffn_w_up", "ple_w_proj")
COMM_EARLY = ("ple_w_proj", "ple_w_gate", "ffn_w_down", "ffn_w_up")
COMM_LATE = ("w_in", "ssm_glu_w", "w_out")
SMALL_TILE_ROWS = 64
CONV_NAME = "ffn_conv_w"


def _to_comm(name, a):
    return jnp.swapaxes(a, 1, 2) if name in COMM_TRANSPOSED else a


def kernel(x, p, rel_bias, norm_attn_g, w_in, sgu_ln_g, sgu_ln_b, sgu_w, sgu_b, ssm_a_re, ssm_a_im, ssm_log_dt, ssm_b_re, ssm_b_im, ssm_c_re, ssm_c_im, ssm_d, ssm_glu_w, ssm_glu_b, branch_norm_g, w_out, norm_ffn_g, ffn_w_up, ffn_conv_w, ffn_conv_b, ffn_w_down, norm_ple_g, ple_w_gate, ple_w_proj, final_norm_g, loss_target, m_rel_bias, m_norm_attn_g, m_w_in, m_sgu_ln_g, m_sgu_ln_b, m_sgu_w, m_sgu_b, m_ssm_a_re, m_ssm_a_im, m_ssm_log_dt, m_ssm_b_re, m_ssm_b_im, m_ssm_c_re, m_ssm_c_im, m_ssm_d, m_ssm_glu_w, m_ssm_glu_b, m_branch_norm_g, m_w_out, m_norm_ffn_g, m_ffn_w_up, m_ffn_conv_w, m_ffn_conv_b, m_ffn_w_down, m_norm_ple_g, m_ple_w_gate, m_ple_w_proj, m_final_norm_g, v_rel_bias, v_norm_attn_g, v_w_in, v_sgu_ln_g, v_sgu_ln_b, v_sgu_w, v_sgu_b, v_ssm_a_re, v_ssm_a_im, v_ssm_log_dt, v_ssm_b_re, v_ssm_b_im, v_ssm_c_re, v_ssm_c_im, v_ssm_d, v_ssm_glu_w, v_ssm_glu_b, v_branch_norm_g, v_w_out, v_norm_ffn_g, v_ffn_w_up, v_ffn_conv_w, v_ffn_conv_b, v_ffn_w_down, v_norm_ple_g, v_ple_w_gate, v_ple_w_proj, v_final_norm_g):
    given = dict(locals())
    w = {n: given[n] for n in WEIGHT_NAMES}
    m = {n: given["m_" + n] for n in WEIGHT_NAMES}
    v = {n: given["v_" + n] for n in WEIGHT_NAMES}
    depth = p.shape[0]
    dev = 4 * lax.axis_index("x") + 2 * lax.axis_index("y") + lax.axis_index("c")

    wc = {n: _to_comm(n, w[n]) for n in COMM_NAMES}
    wb = {n: wc[n].astype(BF16) for n in COMM_NAMES}
    conv_local = [w[CONV_NAME], m[CONV_NAME], v[CONV_NAME]]
    conv_rows = _pack_rows(sum(int(np.prod(t.shape)) for t in conv_local), 8)
    conv_g, = _all_gather([_pack(conv_local, conv_rows)], 0, "gather_conv_taps")
    conv_parts = zip(*[_unpack(conv_g[j], [t.shape for t in conv_local]) for j in range(N_DEV)])
    conv_w, conv_m, conv_v = [jnp.concatenate(parts, axis=2) for parts in conv_parts]
    small = {n: w[n] for n in SMALL_NAMES}

    def whole(blocks):
        return {n: t.reshape(-1, t.shape[-1]) for n, t in zip(COMM_NAMES, blocks)}

    def own_slot(block):
        return lax.dynamic_update_slice_in_dim(jnp.zeros((N_DEV,) + block.shape, block.dtype), block[None], dev, 0)

    first = _all_gather([wb[n][0] for n in COMM_NAMES], 0, "gather_weights_0")
    in_flight = {}
    for i in range(1, depth):
        srcs, first = lax.optimization_barrier(([wb[n][i] for n in COMM_NAMES], first))
        in_flight[i] = _exchange_start(srcs, [own_slot(s) for s in srcs], True, "gather_weights_%d_start" % i)
        small["norm_attn_g"] = small["norm_attn_g"] + in_flight[i][4][0, 0]

    def layer_weights(i, h):
        if i == 0:
            got = whole(first)
        else:
            send_sems, recv_sems, srcs, lands, _ = in_flight.pop(i)
            got = whole(_exchange_wait(send_sems, recv_sems, srcs, lands, h, True, "gather_weights_%d_wait" % i))
        return dict(got, **{CONV_NAME: conv_w[i]})

    def as_slots(g, n):
        return g.reshape((N_DEV,) + wc[n].shape[1:])

    scattering = {}

    def layer_done(i, stage, g, small_now):
        if stage == "all" and i == 0:
            return small_now
        names = COMM_EARLY if stage == "ffn" else COMM_LATE
        srcs = [as_slots(g[n], n) for n in names]
        lands = [own_slot(lax.dynamic_index_in_dim(s, dev, 0, keepdims=False)) for s in srcs]
        started = _exchange_start(srcs, lands, False, "scatter_weight_grads_%d_%s_start" % (i, stage))
        scattering[i, stage] = (names, started)
        pin = "branch_norm_g" if stage == "ffn" else "norm_ple_g"
        return dict(small_now, **{pin: small_now[pin] + started[4][0, 0]})

    loss, dx, big_grads, grads = _local_step(x[0], p[:, 0], loss_target[0], layer_weights, small, layer_done)
    loss = lax.psum(loss, ("x", "y", "c"))

    recv = [{} for _ in range(depth)]
    last = _all_to_all([as_slots(big_grads[0][n], n) for n in COMM_LATE], "scatter_weight_grads_0_all")
    recv[0].update(zip(COMM_LATE, last))
    for (i, stage), (names, (send_sems, recv_sems, srcs, lands, _)) in scattering.items():
        got = _exchange_wait(send_sems, recv_sems, srcs, lands, dx, False,
                             "scatter_weight_grads_%d_%s_wait" % (i, stage))
        recv[i].update(zip(names, got))
    rep_names = SMALL_NAMES + (CONV_NAME,)
    rep_w = dict({n: w[n] for n in SMALL_NAMES}, **{CONV_NAME: conv_w})
    rep_m = dict({n: m[n] for n in SMALL_NAMES}, **{CONV_NAME: conv_m})
    rep_v = dict({n: v[n] for n in SMALL_NAMES}, **{CONV_NAME: conv_v})
    rep_shapes = [rep_w[n].shape for n in rep_names]
    rep_rows = _pack_rows(sum(int(np.prod(s)) for s in rep_shapes), SMALL_TILE_ROWS)
    rep_parts, = _all_gather([_pack([grads[n] for n in rep_names], rep_rows)], 0, "gather_small_grads")

    out = {}
    for n in COMM_NAMES:
        res = _adamw([recv[i][n] for i in range(depth)], wc[n], _to_comm(n, m[n]), _to_comm(n, v[n]),
                     "adamw_" + n, _tile_rows(wc[n].shape[1], 256))
        out[n] = [_to_comm(n, r) for r in res]
    rep_out = _adamw([rep_parts], *[_pack([src[n] for n in rep_names], rep_rows)[None] for src in (rep_w, rep_m, rep_v)],
                     "adamw_replicated", SMALL_TILE_ROWS)
    for n, vals in zip(rep_names, zip(*[_unpack(r[0], rep_shapes) for r in rep_out])):
        out[n] = list(vals)
    shard = ffn_conv_w.shape[2]
    out[CONV_NAME] = [lax.dynamic_slice_in_dim(t, dev * shard, shard, axis=2) for t in out[CONV_NAME]]
    results = [[out[n][kind] for n in WEIGHT_NAMES] for kind in range(4)]
    return (loss, dx[None], *results[0], *results[1], *results[2], *results[3])
```

```python
import math

import numpy as np
import jax
import jax.numpy as jnp
from jax import lax
from jax.experimental import pallas as pl
from jax.experimental.pallas import tpu as pltpu

F32 = jnp.float32
BF16 = jnp.bfloat16

D_MODEL = 1024
HEAD_DIM = 64
N_HEADS = 8
ATTN_W = 512
SGU_W = 256
SGU_GROUPS = 4
SGU_CHUNK = 128
SSM_W = 256
SSM_GROUPS = 16
SSM_CH = 16
SSM_STATE = 64
SSM_NS = SSM_GROUPS * SSM_STATE
IN_W = 2304
D_FF = 2816
PLE_DIM = 256
BRANCHES = ((128, 1), (512, 4), (2048, 16))
BLK = 128
N_BUCKETS = 32
REL_MAX = 2048
EPS = 1e-6
NEG_INF = -1e30
N_DEV = 8

ADAM_LR = 0.001
ADAM_B1 = 0.9
ADAM_B2 = 0.999
ADAM_EPS = 1e-08
ADAM_WD = 0.01
ADAM_STEP = 10

VMEM_LIMIT_BYTES = 56 * 1024 * 1024
GELU_C = math.sqrt(2.0 / math.pi)

SMALL_NAMES = ("rel_bias", "norm_attn_g", "sgu_ln_g", "sgu_ln_b", "sgu_w", "sgu_b", "ssm_a_re", "ssm_a_im",
               "ssm_log_dt", "ssm_b_re", "ssm_b_im", "ssm_c_re", "ssm_c_im", "ssm_d", "ssm_glu_b",
               "branch_norm_g", "norm_ffn_g", "ffn_conv_b", "norm_ple_g", "final_norm_g")
WEIGHT_NAMES = ("rel_bias", "norm_attn_g", "w_in", "sgu_ln_g", "sgu_ln_b", "sgu_w", "sgu_b", "ssm_a_re",
                "ssm_a_im", "ssm_log_dt", "ssm_b_re", "ssm_b_im", "ssm_c_re", "ssm_c_im", "ssm_d", "ssm_glu_w",
                "ssm_glu_b", "branch_norm_g", "w_out", "norm_ffn_g", "ffn_w_up", "ffn_conv_w", "ffn_conv_b",
                "ffn_w_down", "norm_ple_g", "ple_w_gate", "ple_w_proj", "final_norm_g")
PACK_COLS = 512


def _params(sem):
    return pltpu.CompilerParams(dimension_semantics=sem, vmem_limit_bytes=VMEM_LIMIT_BYTES)


def _pick(dim, target):
    if dim <= target:
        return dim
    best = None
    for t in range(128, target + 1, 128):
        if dim % t == 0:
            best = t
    return dim if best is None else best


def _gelu(x):
    return 0.5 * x * (1.0 + jnp.tanh(GELU_C * (x + 0.044715 * (x * x * x))))


def _gelu_grad(x):
    t = jnp.tanh(GELU_C * (x + 0.044715 * (x * x * x)))
    return 0.5 * (1.0 + t) + 0.5 * x * (1.0 - t * t) * (GELU_C * (1.0 + 3.0 * 0.044715 * (x * x)))


def _sigmoid(x):
    return 1.0 / (1.0 + jnp.exp(-x))


_DIMS = {"nn": (((1,), (0,)), ((), ())), "tn": (((0,), (0,)), ((), ())), "nt": (((1,), (1,)), ((), ()))}


def _mm(a, b, mode, name, add=None, out_dtype=F32, tm=1408, tn=1408, tk=1408):
    if mode == "nn":
        m, k = a.shape
        k2, n = b.shape
    elif mode == "tn":
        k, m = a.shape
        k2, n = b.shape
    else:
        m, k = a.shape
        n, k2 = b.shape
    assert k == k2, (name, a.shape, b.shape, mode)
    tm, tn, tk = _pick(m, tm), _pick(n, tn), _pick(k, tk)
    nk = k // tk
    dims = _DIMS[mode]
    has_add = add is not None

    def body(*refs):
        if has_add:
            a_ref, b_ref, add_ref, o_ref = refs[:4]
        else:
            a_ref, b_ref, o_ref = refs[:3]
        part = lax.dot_general(a_ref[...].astype(BF16), b_ref[...].astype(BF16), dims,
                               preferred_element_type=F32)

        def finish(r):
            if has_add:
                r = r + add_ref[...]
            o_ref[...] = r.astype(out_dtype)

        if nk == 1:
            finish(part)
            return
        acc_ref = refs[-1]
        kk = pl.program_id(2)

        @pl.when(kk == 0)
        def _():
            acc_ref[...] = part

        @pl.when((kk > 0) & (kk < nk - 1))
        def _():
            acc_ref[...] += part

        @pl.when(kk == nk - 1)
        def _():
            finish(acc_ref[...] + part)

    if mode == "tn":
        a_spec = pl.BlockSpec((tk, tm), lambda i, j, kk: (kk, i))
    else:
        a_spec = pl.BlockSpec((tm, tk), lambda i, j, kk: (i, kk))
    if mode == "nt":
        b_spec = pl.BlockSpec((tn, tk), lambda i, j, kk: (j, kk))
    else:
        b_spec = pl.BlockSpec((tk, tn), lambda i, j, kk: (kk, j))
    o_spec = pl.BlockSpec((tm, tn), lambda i, j, kk: (i, j))
    in_specs = [a_spec, b_spec] + ([o_spec] if has_add else [])
    args = (a, b) + ((add,) if has_add else ())
    return pl.pallas_call(
        body, name=name, grid=(m // tm, n // tn, nk),
        in_specs=in_specs, out_specs=o_spec,
        out_shape=jax.ShapeDtypeStruct((m, n), out_dtype),
        scratch_shapes=[pltpu.VMEM((tm, tn), F32)] if nk > 1 else [],
        compiler_params=_params(("parallel", "parallel", "arbitrary")),
    )(*args)


def _rb(tm, w, cb=0):
    return pl.BlockSpec((tm, w), lambda i: (i, cb))


def _fb(shape):
    nd = len(shape)
    return pl.BlockSpec(shape, lambda i: (0,) * nd)


def _rowcall(body, name, n_rows, tm, in_specs, args, out_specs, out_shapes):
    return pl.pallas_call(
        body, name=name, grid=(n_rows // tm,), in_specs=in_specs, out_specs=out_specs, out_shape=out_shapes,
        compiler_params=_params(("arbitrary",)),
    )(*args)


def _sds(shape, dtype=F32):
    return jax.ShapeDtypeStruct(shape, dtype)


def _rms_fwd(h, g, name, tm=512):
    s, d = h.shape

    def body(h_ref, g_ref, o_ref):
        x = h_ref[...]
        r = lax.rsqrt(jnp.mean(x * x, axis=-1, keepdims=True) + EPS)
        o_ref[...] = (x * r * g_ref[...]).astype(BF16)

    return _rowcall(body, name, s, tm, [_rb(tm, d), _fb((1, d))], (h, g.reshape(1, d)), _rb(tm, d),
                    _sds((s, d), BF16))


def _rms_bwd(da, h, g, dres, name, tm=512):
    s, d = h.shape

    def body(da_ref, h_ref, g_ref, dres_ref, dh_ref, dg_ref):
        @pl.when(pl.program_id(0) == 0)
        def _():
            dg_ref[...] = jnp.zeros_like(dg_ref)

        x = h_ref[...]
        r = lax.rsqrt(jnp.mean(x * x, axis=-1, keepdims=True) + EPS)
        xh = x * r
        dy = da_ref[...]
        dg_ref[...] += jnp.sum(dy * xh, axis=0, keepdims=True)
        dxh = dy * g_ref[...]
        dh_ref[...] = dres_ref[...] + r * (dxh - xh * jnp.mean(dxh * xh, axis=-1, keepdims=True))

    dh, dg = _rowcall(body, name, s, tm, [_rb(tm, d), _rb(tm, d), _fb((1, d)), _rb(tm, d)],
                      (da, h, g.reshape(1, d), dres), [_rb(tm, d), _fb((1, d))], [_sds((s, d)), _sds((1, d))])
    return dh, dg.reshape(d)


def _loss_head(h, target, g, name, tm=512):
    s, d = h.shape

    def body(h_ref, t_ref, g_ref, dh_ref, loss_ref, dg_ref):
        @pl.when(pl.program_id(0) == 0)
        def _():
            dg_ref[...] = jnp.zeros_like(dg_ref)
            loss_ref[...] = jnp.zeros_like(loss_ref)

        x = h_ref[...]
        r = lax.rsqrt(jnp.mean(x * x, axis=-1, keepdims=True) + EPS)
        xh = x * r
        gg = g_ref[...]
        err = xh * gg - t_ref[...]
        loss_ref[...] += jnp.sum(err * err) * (0.5 / d)
        dy = err * (1.0 / d)
        dg_ref[...] += jnp.sum(dy * xh, axis=0, keepdims=True)
        dxh = dy * gg
        dh_ref[...] = r * (dxh - xh * jnp.mean(dxh * xh, axis=-1, keepdims=True))

    dh, loss, dg = _rowcall(body, name, s, tm, [_rb(tm, d), _rb(tm, d), _fb((1, d))], (h, target, g.reshape(1, d)),
                            [_rb(tm, d), _fb((1, 128)), _fb((1, d))], [_sds((s, d)), _sds((1, 128)), _sds((1, d))])
    return dh, loss[0, 0], dg.reshape(d)


_MIX_PARTS = ((0, 512), (512, 768), (768, 1024))


def _mix_fwd(ya, ysg, yss, g, name, tm=512):
    s = ya.shape[0]

    def body(a_ref, b_ref, c_ref, g_ref, o_ref):
        for ref, (lo, hi) in zip((a_ref, b_ref, c_ref), _MIX_PARTS):
            y = ref[...]
            r = lax.rsqrt(jnp.mean(y * y, axis=-1, keepdims=True) + EPS)
            o_ref[:, lo:hi] = (y * r * g_ref[:, lo:hi]).astype(BF16)

    return _rowcall(body, name, s, tm, [_rb(tm, 512), _rb(tm, 256), _rb(tm, 256), _fb((1, 1024))],
                    (ya, ysg, yss, g.reshape(1, 1024)), _rb(tm, 1024), _sds((s, 1024), BF16))


def _mix_bwd(dmix, ya, ysg, yss, g, name, tm=512):
    s = ya.shape[0]

    def body(dm_ref, a_ref, b_ref, c_ref, g_ref, da_ref, db_ref, dc_ref, dg_ref):
        @pl.when(pl.program_id(0) == 0)
        def _():
            dg_ref[...] = jnp.zeros_like(dg_ref)

        for ref, dref, (lo, hi) in zip((a_ref, b_ref, c_ref), (da_ref, db_ref, dc_ref), _MIX_PARTS):
            y = ref[...]
            r = lax.rsqrt(jnp.mean(y * y, axis=-1, keepdims=True) + EPS)
            xh = y * r
            dm = dm_ref[:, lo:hi]
            dg_ref[:, lo:hi] += jnp.sum(dm * xh, axis=0, keepdims=True)
            dxh = dm * g_ref[:, lo:hi]
            dref[...] = r * (dxh - xh * jnp.mean(dxh * xh, axis=-1, keepdims=True))

    da, db, dc, dg = _rowcall(
        body, name, s, tm, [_rb(tm, 1024), _rb(tm, 512), _rb(tm, 256), _rb(tm, 256), _fb((1, 1024))],
        (dmix, ya, ysg, yss, g.reshape(1, 1024)),
        [_rb(tm, 512), _rb(tm, 256), _rb(tm, 256), _fb((1, 1024))],
        [_sds((s, 512)), _sds((s, 256)), _sds((s, 256)), _sds((1, 1024))])
    return da, db, dc, dg.reshape(1024)


def _ssm_post_fwd(yc, z, d, gw, gb, name, tm=1024):
    s = yc.shape[0]

    def body(yc_ref, u_ref, d_ref, gw_ref, gb_ref, o_ref):
        y1 = yc_ref[...] + d_ref[...] * u_ref[...]
        y2 = _gelu(y1)
        gl = jnp.dot(y2.astype(BF16), gw_ref[...], preferred_element_type=F32) + gb_ref[...]
        o_ref[...] = y2 * _sigmoid(gl)

    return _rowcall(body, name, s, tm, [_rb(tm, 256), _rb(tm, 256, 8), _fb((1, 256)), _fb((256, 256)), _fb((1, 256))],
                    (yc, z, d.reshape(1, 256), gw, gb.reshape(1, 256)), _rb(tm, 256), _sds((s, 256)))


def _ssm_post_bwd(dy, yc, z, d, gw, gb, name, tm=1024):
    s = yc.shape[0]

    def body(dy_ref, yc_ref, u_ref, d_ref, gw_ref, gb_ref, dy1_ref, dgl_ref, y2_ref, dud_ref, dd_ref, dgb_ref):
        @pl.when(pl.program_id(0) == 0)
        def _():
            dd_ref[...] = jnp.zeros_like(dd_ref)
            dgb_ref[...] = jnp.zeros_like(dgb_ref)

        u = u_ref[...]
        dd = d_ref[...]
        y1 = yc_ref[...] + dd * u
        y2 = _gelu(y1)
        gw_v = gw_ref[...]
        gl = jnp.dot(y2.astype(BF16), gw_v, preferred_element_type=F32) + gb_ref[...]
        sg = _sigmoid(gl)
        dyv = dy_ref[...]
        dgl = dyv * y2 * sg * (1.0 - sg)
        dy2 = dyv * sg + lax.dot_general(dgl.astype(BF16), gw_v, _DIMS["nt"], preferred_element_type=F32)
        dy1 = dy2 * _gelu_grad(y1)
        dy1_ref[...] = dy1.astype(BF16)
        dgl_ref[...] = dgl.astype(BF16)
        y2_ref[...] = y2.astype(BF16)
        dud_ref[...] = dy1 * dd
        dd_ref[...] += jnp.sum(dy1 * u, axis=0, keepdims=True)
        dgb_ref[...] += jnp.sum(dgl, axis=0, keepdims=True)

    outs = _rowcall(
        body, name, s, tm,
        [_rb(tm, 256), _rb(tm, 256), _rb(tm, 256, 8), _fb((1, 256)), _fb((256, 256)), _fb((1, 256))],
        (dy, yc, z, d.reshape(1, 256), gw, gb.reshape(1, 256)),
        [_rb(tm, 256)] * 4 + [_fb((1, 256))] * 2,
        [_sds((s, 256), BF16)] * 3 + [_sds((s, 256))] + [_sds((1, 256))] * 2)
    dy1, dgl, y2, dud, dd, dgb = outs
    return dy1, dgl, y2, dud, dd.reshape(256), dgb.reshape(256)


SCAN_T = 512
N_SCAN_TABLES = 6


def _scan_tables(pr, pi, reverse):
    ns = pr.shape[0]
    sign = -1.0 if reverse else 1.0
    power = [(jnp.ones((ns,), F32), jnp.zeros((ns,), F32))] + [(pr[:, k], sign * pi[:, k]) for k in range(8)]
    zero = (jnp.zeros((ns,), F32), jnp.zeros((ns,), F32))

    def table(exponents):
        rows = [zero if e is None else power[e] for e in exponents]
        return jnp.stack([jnp.concatenate(row) for row in rows])

    tabs = []
    for k in (1, 2, 4):
        has_partner = [(s < 8 - k) if reverse else (s >= k) for s in range(8)]
        tabs.append(table([k if ok else None for ok in has_partner]))
    tabs.append(table([s if reverse else 7 - s for s in range(8)]))
    tabs.append(table([8 - s if reverse else s + 1 for s in range(8)]))
    tabs.append(table([8] * 8))
    return jnp.stack(tabs)


def _cmul(ar, ai, br, bi):
    return ar * br - ai * bi, ar * bi + ai * br


def _scan_group(ur, ui, cr, ci, tr_ref, ti_ref, reverse):
    xr, xi = ur, ui
    for n, k in enumerate((1, 2, 4)):
        shift = 8 - k if reverse else k
        pr, pi = _cmul(tr_ref[n], ti_ref[n], pltpu.roll(xr, shift, axis=0), pltpu.roll(xi, shift, axis=0))
        xr, xi = xr + pr, xi + pi
    sr, si = _cmul(tr_ref[3], ti_ref[3], ur, ui)
    for k in (1, 2, 4):
        sr, si = sr + pltpu.roll(sr, k, axis=0), si + pltpu.roll(si, k, axis=0)
    pr, pi = _cmul(tr_ref[4], ti_ref[4], cr, ci)
    nr, ni = _cmul(tr_ref[5], ti_ref[5], cr, ci)
    return xr + pr, xi + pi, nr + sr, ni + si


def _table_halves(t_ref):
    return t_ref.at[:, :, pl.ds(0, SSM_NS)], t_ref.at[:, :, pl.ds(SSM_NS, SSM_NS)]


_U_BLOCK = (IN_W - SSM_W) // SSM_W


def _ssm_fwd(z, bdt, cd, tabs, name):
    s = z.shape[0]
    ns = SSM_NS
    n_t = s // SCAN_T

    def body(u_ref, b_ref, c_ref, t_ref, xr_ref, xi_ref, y_ref, cr_ref, ci_ref, ur_ref, ui_ref):
        @pl.when(pl.program_id(0) == 0)
        def _():
            cr_ref[...] = jnp.zeros_like(cr_ref)
            ci_ref[...] = jnp.zeros_like(ci_ref)

        bu = lax.dot_general(u_ref[...].astype(BF16), b_ref[...], _DIMS["nt"], preferred_element_type=F32)
        ur_ref[...] = bu[:, :ns]
        ui_ref[...] = bu[:, ns:]
        tr_ref, ti_ref = _table_halves(t_ref)

        def group(g, carry):
            rows = pl.ds(pl.multiple_of(g * 8, 8), 8)
            xr, xi, cr, ci = _scan_group(ur_ref[rows, :], ui_ref[rows, :], *carry, tr_ref, ti_ref, False)
            xr_ref[rows, :] = xr
            xi_ref[rows, :] = xi
            return cr, ci

        cr, ci = lax.fori_loop(0, SCAN_T // 8, group, (cr_ref[...], ci_ref[...]), unroll=2)
        cr_ref[...] = cr
        ci_ref[...] = ci
        y_ref[...] = (jnp.dot(xr_ref[...].astype(BF16), c_ref[0:ns, :], preferred_element_type=F32)
                      + jnp.dot(xi_ref[...].astype(BF16), c_ref[ns:, :], preferred_element_type=F32))

    x_spec = pl.BlockSpec((SCAN_T, ns), lambda t: (t, 0))
    return pl.pallas_call(
        body, name=name, grid=(n_t,),
        in_specs=[pl.BlockSpec((SCAN_T, SSM_W), lambda t: (t, _U_BLOCK)), _fb((2 * ns, SSM_W)),
                  _fb((2 * ns, SSM_W)), _fb((N_SCAN_TABLES, 8, 2 * ns))],
        out_specs=[x_spec, x_spec, _rb(SCAN_T, SSM_W)],
        out_shape=[_sds((s, ns)), _sds((s, ns)), _sds((s, SSM_W))],
        scratch_shapes=[pltpu.VMEM((8, ns), F32)] * 2 + [pltpu.VMEM((SCAN_T, ns), F32)] * 2,
        compiler_params=_params(("arbitrary",)),
    )(z, bdt, cd, tabs)


def _ssm_bwd(dy1, dud, z, xr, xi, bdt, cd, tabs, name):
    s = z.shape[0]
    ns = SSM_NS
    n_t = s // SCAN_T
    n_groups = SCAN_T // 8

    def body(dy_ref, dud_ref, u_ref, xr_ref, xi_ref, pxr_ref, pxi_ref, b_ref, c_ref, t_ref,
             du_ref, dbd_ref, dcd_ref, dar_ref, dai_ref,
             cr_ref, ci_ref, ar_ref, ai_ref, sxr_ref, sxi_ref, gr_ref, gi_ref, lr_ref, li_ref, bacc_ref, cacc_ref):
        t = pl.program_id(0)

        @pl.when(t == 0)
        def _():
            for ref in (cr_ref, ci_ref, ar_ref, ai_ref, bacc_ref, cacc_ref):
                ref[...] = jnp.zeros_like(ref)

        dyb = dy_ref[...]
        g = lax.dot_general(dyb, c_ref[...], _DIMS["nt"], preferred_element_type=F32)
        gr_ref[...] = g[:, :ns]
        gi_ref[...] = g[:, ns:]
        has_before = (t < n_t - 1).astype(F32)
        sxr_ref[0:8, :] = pxr_ref[...] * has_before
        sxi_ref[0:8, :] = pxi_ref[...] * has_before
        sxr_ref[8:, :] = xr_ref[...]
        sxi_ref[8:, :] = xi_ref[...]
        first_row = lax.broadcasted_iota(jnp.int32, (8, ns), 0) == 0
        tr_ref, ti_ref = _table_halves(t_ref)

        def group(k, carry):
            cr, ci, ar, ai = carry
            g8 = pl.multiple_of((n_groups - 1 - k) * 8, 8)
            rows = pl.ds(g8, 8)
            lr, li, cr, ci = _scan_group(gr_ref[rows, :], gi_ref[rows, :], cr, ci, tr_ref, ti_ref, True)
            lr_ref[rows, :] = lr
            li_ref[rows, :] = li
            here, before = pl.ds(g8 + 8, 8), rows
            pr = jnp.where(first_row, pltpu.roll(sxr_ref[before, :], 1, axis=0), pltpu.roll(sxr_ref[here, :], 1, axis=0))
            pi = jnp.where(first_row, pltpu.roll(sxi_ref[before, :], 1, axis=0), pltpu.roll(sxi_ref[here, :], 1, axis=0))
            return cr, ci, ar + lr * pr + li * pi, ai + li * pr - lr * pi

        cr, ci, ar, ai = lax.fori_loop(0, n_groups, group,
                                       (cr_ref[...], ci_ref[...], ar_ref[...], ai_ref[...]), unroll=2)
        cr_ref[...] = cr
        ci_ref[...] = ci
        ar_ref[...] = ar
        ai_ref[...] = ai
        lrb = lr_ref[...].astype(BF16)
        lib = li_ref[...].astype(BF16)
        ub = u_ref[...].astype(BF16)
        du_ref[...] = (dud_ref[...] + jnp.dot(lrb, b_ref[0:ns, :], preferred_element_type=F32)
                       + jnp.dot(lib, b_ref[ns:, :], preferred_element_type=F32))
        bacc_ref[0:ns, :] += lax.dot_general(lrb, ub, _DIMS["tn"], preferred_element_type=F32)
        bacc_ref[ns:, :] += lax.dot_general(lib, ub, _DIMS["tn"], preferred_element_type=F32)
        cacc_ref[0:ns, :] += lax.dot_general(xr_ref[...].astype(BF16), dyb, _DIMS["tn"], preferred_element_type=F32)
        cacc_ref[ns:, :] += lax.dot_general(xi_ref[...].astype(BF16), dyb, _DIMS["tn"], preferred_element_type=F32)

        @pl.when(t == n_t - 1)
        def _():
            for k in (1, 2, 4):
                ar_ref[...] += pltpu.roll(ar_ref[...], k, axis=0)
                ai_ref[...] += pltpu.roll(ai_ref[...], k, axis=0)
            dar_ref[...] = ar_ref[...]
            dai_ref[...] = ai_ref[...]
            dbd_ref[...] = bacc_ref[...]
            dcd_ref[...] = cacc_ref[...]

    rev = lambda t: n_t - 1 - t
    row_spec = pl.BlockSpec((SCAN_T, SSM_W), lambda t: (rev(t), 0))
    x_spec = pl.BlockSpec((SCAN_T, ns), lambda t: (rev(t), 0))
    before_spec = pl.BlockSpec((8, ns), lambda t: (jnp.maximum(rev(t) * (SCAN_T // 8) - 1, 0), 0))
    du, dbd, dcd, dar, dai = pl.pallas_call(
        body, name=name, grid=(n_t,),
        in_specs=[row_spec, row_spec, pl.BlockSpec((SCAN_T, SSM_W), lambda t: (rev(t), _U_BLOCK)),
                  x_spec, x_spec, before_spec, before_spec,
                  _fb((2 * ns, SSM_W)), _fb((2 * ns, SSM_W)), _fb((N_SCAN_TABLES, 8, 2 * ns))],
        out_specs=[row_spec, _fb((2 * ns, SSM_W)), _fb((2 * ns, SSM_W)), _fb((8, ns)), _fb((8, ns))],
        out_shape=[_sds((s, SSM_W)), _sds((2 * ns, SSM_W)), _sds((2 * ns, SSM_W)), _sds((8, ns)), _sds((8, ns))],
        scratch_shapes=([pltpu.VMEM((8, ns), F32)] * 4 + [pltpu.VMEM((SCAN_T + 8, ns), F32)] * 2
                        + [pltpu.VMEM((SCAN_T, ns), F32)] * 4 + [pltpu.VMEM((2 * ns, SSM_W), F32)] * 2),
        compiler_params=_params(("arbitrary",)),
    )(dy1, dud, z, xr, xi, xr, xi, bdt, cd, tabs)
    return du, dbd, dcd, dar[0], dai[0]


def _group_ids():
    return lax.broadcasted_iota(jnp.int32, (1, SGU_W), 1) // 64


def _group_mean(val, gid):
    out = jnp.zeros_like(val)
    for g in range(SGU_GROUPS):
        mg = gid == g
        out = jnp.where(mg, jnp.sum(jnp.where(mg, val, 0.0), axis=1, keepdims=True) * (1.0 / 64), out)
    return out


def _causal_w(w_ref, g):
    t = lax.broadcasted_iota(jnp.int32, (SGU_CHUNK, SGU_CHUNK), 0)
    s = lax.broadcasted_iota(jnp.int32, (SGU_CHUNK, SGU_CHUNK), 1)
    return jnp.where(t >= s, w_ref[g], 0.0).astype(BF16)


def _sgu_core(x, lng, lnb, w_ref, bexp, gid):
    zz = _gelu(x)
    u = zz[:, :SGU_W]
    v = zz[:, SGU_W:]
    vc = v - _group_mean(v, gid)
    rstd = lax.rsqrt(_group_mean(vc * vc, gid) + EPS)
    vhat = vc * rstd
    vn = vhat * lng + lnb
    vnb = vn.astype(BF16)
    mixed = bexp
    for g in range(SGU_GROUPS):
        mm = jnp.dot(_causal_w(w_ref, g), vnb, preferred_element_type=F32)
        mixed = jnp.where(gid == g, mm + bexp, mixed)
    return u, rstd, vhat, vnb, mixed


def _sgu_fwd(z, lng, lnb, w, bexp, name, tm=512):
    s = z.shape[0]

    def body(z_ref, lng_ref, lnb_ref, w_ref, b_ref, o_ref):
        gid = _group_ids()
        for j in range(tm // SGU_CHUNK):
            rows = pl.ds(j * SGU_CHUNK, SGU_CHUNK)
            u, _, _, _, mixed = _sgu_core(z_ref[rows, :], lng_ref[...], lnb_ref[...], w_ref, b_ref[...], gid)
            o_ref[rows, :] = u * mixed

    return _rowcall(body, name, s, tm,
                    [_rb(tm, 512, 3), _fb((1, 256)), _fb((1, 256)), _fb((4, 128, 128)), _fb((128, 256))],
                    (z, lng.reshape(1, 256), lnb.reshape(1, 256), w, bexp), _rb(tm, 256), _sds((s, 256)))


def _sgu_bwd(z, dy, lng, lnb, w, bexp, name, tm=512):
    s = z.shape[0]

    def body(z_ref, dy_ref, lng_ref, lnb_ref, w_ref, b_ref, dz_ref, dw_ref, db_ref, dlng_ref, dlnb_ref):
        @pl.when(pl.program_id(0) == 0)
        def _():
            dw_ref[...] = jnp.zeros_like(dw_ref)
            db_ref[...] = jnp.zeros_like(db_ref)
            dlng_ref[...] = jnp.zeros_like(dlng_ref)
            dlnb_ref[...] = jnp.zeros_like(dlnb_ref)

        gid = _group_ids()
        t = lax.broadcasted_iota(jnp.int32, (SGU_CHUNK, SGU_CHUNK), 0)
        sidx = lax.broadcasted_iota(jnp.int32, (SGU_CHUNK, SGU_CHUNK), 1)
        lng_v = lng_ref[...]
        for j in range(tm // SGU_CHUNK):
            rows = pl.ds(j * SGU_CHUNK, SGU_CHUNK)
            x = z_ref[rows, :]
            u, rstd, vhat, vnb, mixed = _sgu_core(x, lng_v, lnb_ref[...], w_ref, b_ref[...], gid)
            dyv = dy_ref[rows, :]
            dmixed = dyv * u
            du = dyv * mixed
            db_ref[...] += dmixed
            dvn = jnp.zeros_like(dmixed)
            for g in range(SGU_GROUPS):
                dmg = jnp.where(gid == g, dmixed, 0.0).astype(BF16)
                dvn = dvn + lax.dot_general(_causal_w(w_ref, g), dmg, _DIMS["tn"], preferred_element_type=F32)
                dwg = lax.dot_general(dmg, vnb, _DIMS["nt"], preferred_element_type=F32)
                dw_ref[g] += jnp.where(t >= sidx, dwg, 0.0)
            dlnb_ref[...] += jnp.sum(dvn, axis=0, keepdims=True)
            dlng_ref[...] += jnp.sum(dvn * vhat, axis=0, keepdims=True)
            dvh = dvn * lng_v
            dv = rstd * (dvh - _group_mean(dvh, gid) - vhat * _group_mean(dvh * vhat, gid))
            gg = _gelu_grad(x)
            dz_ref[rows, 0:SGU_W] = du * gg[:, :SGU_W]
            dz_ref[rows, SGU_W:2 * SGU_W] = dv * gg[:, SGU_W:]

    dz, dw, db, dlng, dlnb = _rowcall(
        body, name, s, tm,
        [_rb(tm, 512, 3), _rb(tm, 256), _fb((1, 256)), _fb((1, 256)), _fb((4, 128, 128)), _fb((128, 256))],
        (z, dy, lng.reshape(1, 256), lnb.reshape(1, 256), w, bexp),
        [_rb(tm, 512), _fb((4, 128, 128)), _fb((128, 256)), _fb((1, 256)), _fb((1, 256))],
        [_sds((s, 512)), _sds((4, 128, 128)), _sds((128, 256)), _sds((1, 256)), _sds((1, 256))])
    return dz, dw, db, dlng.reshape(256), dlnb.reshape(256)


CONV_TC = 1408
N_CT = D_FF // CONV_TC


def _row_of(block8, j):
    r = lax.broadcasted_iota(jnp.int32, block8.shape, 0)
    return jnp.sum(jnp.where(r == j, block8, 0.0), axis=0, keepdims=True)


EDGE = 16


def _conv_fwd(hu, cw, cb, name, tm=256):
    s = hu.shape[0]
    n8 = tm // 8

    def body(xv_ref, xg_ref, tv_ref, tg_ref, wv_ref, wg_ref, bv_ref, bg_ref, hv_ref, hg_ref, act_ref):
        has_prev = (pl.program_id(1) > 0).astype(F32)
        row = lax.broadcasted_iota(jnp.int32, (EDGE, CONV_TC), 0)

        def conv(x_ref, t_ref, w_ref, b_ref):
            x = x_ref[...]
            w0, w1, w2, bb = w_ref[0:1, :], w_ref[1:2, :], w_ref[2:3, :], b_ref[...]
            whole = w0 * pltpu.roll(x, 2, axis=0) + w1 * pltpu.roll(x, 1, axis=0) + w2 * x + bb
            r7 = _row_of(t_ref[...], 7) * has_prev
            r6 = _row_of(t_ref[...], 6) * has_prev
            xe = x_ref[0:EDGE, :]
            x1 = jnp.where(row == 0, r7, pltpu.roll(xe, 1, axis=0))
            x2 = jnp.where(row == 0, r6, jnp.where(row == 1, r7, pltpu.roll(xe, 2, axis=0)))
            return whole, w0 * x2 + w1 * x1 + w2 * xe + bb

        hv, hv_edge = conv(xv_ref, tv_ref, wv_ref, bv_ref)
        hg, hg_edge = conv(xg_ref, tg_ref, wg_ref, bg_ref)
        hv_ref[...] = hv.astype(BF16)
        hg_ref[...] = hg.astype(BF16)
        act_ref[...] = (_gelu(hg) * hv).astype(BF16)
        hv_ref[0:EDGE, :] = hv_edge.astype(BF16)
        hg_ref[0:EDGE, :] = hg_edge.astype(BF16)
        act_ref[0:EDGE, :] = (_gelu(hg_edge) * hv_edge).astype(BF16)

    def xs(off):
        return pl.BlockSpec((tm, CONV_TC), lambda j, i: (i, j + off))

    def ts(off):
        return pl.BlockSpec((8, CONV_TC), lambda j, i: (jnp.maximum(i * n8 - 1, 0), j + off))

    def ws(rows, off):
        return pl.BlockSpec((rows, CONV_TC), lambda j, i: (0, j + off))

    o_spec = pl.BlockSpec((tm, CONV_TC), lambda j, i: (i, j))
    return pl.pallas_call(
        body, name=name, grid=(N_CT, s // tm),
        in_specs=[xs(0), xs(N_CT), ts(0), ts(N_CT), ws(3, 0), ws(3, N_CT), ws(1, 0), ws(1, N_CT)],
        out_specs=[o_spec] * 3, out_shape=[_sds((s, D_FF), BF16)] * 3,
        compiler_params=_params(("parallel", "arbitrary")),
    )(hu, hu, hu, hu, cw, cw, cb.reshape(1, 2 * D_FF), cb.reshape(1, 2 * D_FF))


HALO = EDGE


def _conv_bwd(dact, hv, hg, hu, cw, name, tm=256):
    s = dact.shape[0]
    n8 = tm // 8

    def body(da_ref, dan_ref, hv_ref, hvn_ref, hg_ref, hgn_ref, x_ref, t_ref, w_ref, dx_ref, dw_ref, db_ref, d_scr):
        i = pl.program_id(1)
        is_value = pl.program_id(0) < N_CT

        @pl.when(i == 0)
        def _():
            dw_ref[...] = jnp.zeros_like(dw_ref)
            db_ref[...] = jnp.zeros_like(db_ref)

        for rows, (a_ref, v_ref, g_ref) in ((pl.ds(0, tm), (da_ref, hv_ref, hg_ref)),
                                            (pl.ds(tm, HALO), (dan_ref, hvn_ref, hgn_ref))):
            @pl.when(is_value)
            def _():
                d_scr[rows, :] = a_ref[...].astype(F32) * _gelu(g_ref[...].astype(F32))

            @pl.when(jnp.logical_not(is_value))
            def _():
                d_scr[rows, :] = (a_ref[...].astype(F32) * v_ref[...].astype(F32)
                                  * _gelu_grad(g_ref[...].astype(F32)))

        has_prev = (i > 0).astype(F32)
        has_next = (i < s // tm - 1).astype(F32)
        w0, w1, w2 = w_ref[0:1, :], w_ref[1:2, :], w_ref[2:3, :]
        d = d_scr[0:tm, :]
        dx_ref[...] = (w2 * d + w1 * pltpu.roll(d, tm - 1, axis=0) + w0 * pltpu.roll(d, tm - 2, axis=0)).astype(BF16)
        row = lax.broadcasted_iota(jnp.int32, (EDGE, CONV_TC), 0)
        nxt = d_scr[tm:tm + HALO, :]
        n0 = _row_of(nxt, 0) * has_next
        n1 = _row_of(nxt, 1) * has_next
        de = d_scr[tm - EDGE:tm, :]
        d1 = jnp.where(row == EDGE - 1, n0, pltpu.roll(de, EDGE - 1, axis=0))
        d2 = jnp.where(row == EDGE - 2, n0, jnp.where(row == EDGE - 1, n1, pltpu.roll(de, EDGE - 2, axis=0)))
        dx_ref[tm - EDGE:tm, :] = (w2 * de + w1 * d1 + w0 * d2).astype(BF16)
        x = x_ref[...]
        r7 = _row_of(t_ref[...], 7) * has_prev
        r6 = _row_of(t_ref[...], 6) * has_prev
        last = x_ref[tm - 8:tm, :]
        l7, l6 = _row_of(last, 7), _row_of(last, 6)
        head = d_scr[0:8, :]
        d0, d1h = _row_of(head, 0), _row_of(head, 1)
        dw_ref[0:1, :] += (jnp.sum(d * pltpu.roll(x, 2, axis=0), axis=0, keepdims=True)
                           + d0 * (r6 - l6) + d1h * (r7 - l7))
        dw_ref[1:2, :] += jnp.sum(d * pltpu.roll(x, 1, axis=0), axis=0, keepdims=True) + d0 * (r7 - l7)
        dw_ref[2:3, :] += jnp.sum(d * x, axis=0, keepdims=True)
        db_ref[...] += jnp.sum(d, axis=0, keepdims=True)

    a_spec = pl.BlockSpec((tm, CONV_TC), lambda j, i: (i, j % N_CT))
    an_spec = pl.BlockSpec((HALO, CONV_TC),
                           lambda j, i: (jnp.minimum((i + 1) * (tm // HALO), s // HALO - 1), j % N_CT))
    x_spec = pl.BlockSpec((tm, CONV_TC), lambda j, i: (i, j))
    t_spec = pl.BlockSpec((8, CONV_TC), lambda j, i: (jnp.maximum(i * n8 - 1, 0), j))
    w_spec = pl.BlockSpec((3, CONV_TC), lambda j, i: (0, j))
    db_spec = pl.BlockSpec((1, CONV_TC), lambda j, i: (0, j))
    return pl.pallas_call(
        body, name=name, grid=(2 * N_CT, s // tm),
        in_specs=[a_spec, an_spec, a_spec, an_spec, a_spec, an_spec, x_spec, t_spec, w_spec],
        out_specs=[x_spec, w_spec, db_spec],
        out_shape=[_sds((s, 2 * D_FF), BF16), _sds((3, 2 * D_FF)), _sds((1, 2 * D_FF))],
        scratch_shapes=[pltpu.VMEM((tm + HALO, CONV_TC), F32)],
        compiler_params=_params(("parallel", "arbitrary")),
    )(dact, dact, hv, hv, hg, hg, hu, hu, cw)


def _ple_fwd(h, gp, pp, name, tm=512):
    s, d = h.shape

    def body(h_ref, g_ref, p_ref, o_ref):
        o_ref[...] = h_ref[...] + _sigmoid(g_ref[...]) * p_ref[...]

    return _rowcall(body, name, s, tm, [_rb(tm, d)] * 3, (h, gp, pp), _rb(tm, d), _sds((s, d)))


def _ple_bwd(dh, gp, pp, name, tm=512):
    s, d = dh.shape

    def body(d_ref, g_ref, p_ref, dp_ref, dg_ref):
        sg = _sigmoid(g_ref[...])
        dv = d_ref[...]
        dp_ref[...] = (dv * sg).astype(BF16)
        dg_ref[...] = (dv * p_ref[...] * sg * (1.0 - sg)).astype(BF16)

    return _rowcall(body, name, s, tm, [_rb(tm, d)] * 3, (dh, gp, pp), [_rb(tm, d)] * 2,
                    [_sds((s, d), BF16)] * 2)


SCALE = HEAD_DIM ** -0.5
ATT_ROWS = 2048


def _att_geom(s, dil):
    w = min(ATT_ROWS, s)
    p = BLK * dil
    assert w % p == 0 and s % w == 0
    return w, p, w // p


def _rows(start, dil):
    return pl.ds(start, BLK, stride=dil) if dil > 1 else pl.ds(start, BLK)


def _head_masks():
    lane = lax.broadcasted_iota(jnp.int32, (1, BLK), 1)
    return [lane < HEAD_DIM, lane >= HEAD_DIM]


def _band():
    rel = np.arange(BLK)[:, None] + BLK - np.arange(2 * BLK)[None, :]
    return (rel >= 0) & (rel <= BLK)


def _zcur(w):
    return lambda off: pl.BlockSpec((w, BLK), lambda hp, i: (i, off + hp))


def _zprev(p, nb):
    return lambda off: pl.BlockSpec((p, BLK), lambda hp, i: (jnp.maximum(i * nb - 1, 0), off + hp))


def _scur(w):
    return pl.BlockSpec((w, BLK), lambda hp, i: (i, hp))


def _pair_rows(t, masks):
    return jnp.concatenate([jnp.where(masks[0], t, 0.0), jnp.where(masks[1], t, 0.0)], axis=0).astype(BF16)


def _pair_bias_bwd(bias):
    return bias.reshape(4, 2, BLK, 2, BLK).transpose(0, 3, 2, 1, 4).reshape(4, 2, BLK, 2 * BLK)


def _unpair_bias_bwd(db):
    return db.reshape(4, 2, BLK, 2, BLK).transpose(0, 3, 2, 1, 4).reshape(N_HEADS, BLK, 2 * BLK)


def _attn_fwd(z, bias, state, dil, first, last, name):
    s = z.shape[0]
    w, p, nb = _att_geom(s, dil)

    def body(*refs):
        q_ref, kp_ref, kc_ref, vp_ref, vc_ref, b_ref = refs[:6]
        rest = refs[6:]
        if not first:
            m_ref, l_ref, a_ref = rest[:3]
            rest = rest[3:]
        i = pl.program_id(1)
        masks = _head_masks()
        own_block = lax.broadcasted_iota(jnp.int32, (1, 2 * BLK), 1) >= BLK
        for r in range(dil):
            for b in range(nb):
                rows = _rows(r + p * b, dil)
                prev_rows = _rows(r + p * (b - 1), dil) if b > 0 else _rows(r, dil)
                kprev, vprev = (kc_ref, vc_ref) if b > 0 else (kp_ref, vp_ref)
                q = q_ref[rows, :] * SCALE
                k = jnp.concatenate([kprev[prev_rows, :], kc_ref[rows, :]], axis=0).astype(BF16)
                v = jnp.concatenate([vprev[prev_rows, :], vc_ref[rows, :]], axis=0).astype(BF16)
                mb = lb = ob = None
                for hh, mh in enumerate(masks):
                    qh = jnp.where(mh, q, 0.0).astype(BF16)
                    sc = lax.dot_general(qh, k, _DIMS["nt"], preferred_element_type=F32) + b_ref[hh]
                    if b == 0:
                        sc = jnp.where(own_block | (i > 0), sc, NEG_INF)
                    mx = jnp.max(sc, axis=1, keepdims=True)
                    e = jnp.exp(sc - mx)
                    den = jnp.sum(e, axis=1, keepdims=True)
                    o = jnp.dot(e.astype(BF16), v, preferred_element_type=F32)
                    if hh == 0:
                        mb = jnp.broadcast_to(mx, (BLK, BLK))
                        lb = jnp.broadcast_to(den, (BLK, BLK))
                        ob = o
                    else:
                        mb = jnp.where(mh, mx, mb)
                        lb = jnp.where(mh, den, lb)
                        ob = jnp.where(mh, o, ob)
                if first:
                    m_new, l_new, a_new = mb, lb, ob
                else:
                    m_old = m_ref[rows, :]
                    m_new = jnp.maximum(m_old, mb)
                    al = jnp.exp(m_old - m_new)
                    be = jnp.exp(mb - m_new)
                    l_new = al * l_ref[rows, :] + be * lb
                    a_new = al * a_ref[rows, :] + be * ob
                if last:
                    y_ref, lse_ref = rest
                    y_ref[rows, :] = a_new / l_new
                    lse_ref[rows, :] = m_new + jnp.log(l_new)
                else:
                    mo_ref, lo_ref, ao_ref = rest
                    mo_ref[rows, :] = m_new
                    lo_ref[rows, :] = l_new
                    ao_ref[rows, :] = a_new

    cur, prv = _zcur(w), _zprev(p, nb)
    b_spec = pl.BlockSpec((2, BLK, 2 * BLK), lambda hp, i: (hp, 0, 0))
    in_specs = [cur(0), prv(4), cur(4), prv(8), cur(8), b_spec]
    args = [z, z, z, z, z, bias]
    if not first:
        in_specs += [_scur(w)] * 3
        args += list(state)
    n_out = 2 if last else 3
    return pl.pallas_call(
        body, name=name, grid=(4, s // w), in_specs=in_specs, out_specs=[_scur(w)] * n_out,
        out_shape=[_sds((s, ATTN_W))] * n_out,
        compiler_params=_params(("parallel", "parallel")),
    )(*args)


def _row_stats(mh, dy, y, lse):
    delta = jnp.sum(jnp.where(mh, dy * y, 0.0), axis=1, keepdims=True)
    lse_h = jnp.max(jnp.where(mh, lse, NEG_INF), axis=1, keepdims=True)
    return delta, lse_h


def _attn_bwd(z, bias, dy, y, lse, prev, dil, name):
    s = z.shape[0]
    w, p, nb = _att_geom(s, dil)
    n_steps = s // w
    first = prev is None

    def body(*refs):
        q_ref, kp_ref, kc_ref, vp_ref, vc_ref, b_ref, dy_ref, y_ref, lse_ref = refs[:9]
        rest = refs[9:]
        if not first:
            dqp_ref, dkp_ref, dvp_ref = rest[:3]
            rest = rest[3:]
        dq_ref, dk_ref, dv_ref, dkx_ref, dvx_ref, db_ref = rest
        i = pl.program_id(1)

        @pl.when(i == 0)
        def _():
            db_ref[...] = jnp.zeros_like(db_ref)

        masks = _head_masks()
        first_head = lax.broadcasted_iota(jnp.int32, (1, 2 * BLK), 1) < BLK

        def flush(rows, dk, dv):
            if not first:
                dk = dk + dkp_ref[rows, :]
                dv = dv + dvp_ref[rows, :]
            dk_ref[rows, :] = dk
            dv_ref[rows, :] = dv

        for r in range(dil):
            carry = None
            for b in range(nb):
                rows = _rows(r + p * b, dil)
                prev_rows = _rows(r + p * (b - 1), dil) if b > 0 else _rows(r, dil)
                kprev, vprev = (kc_ref, vc_ref) if b > 0 else (kp_ref, vp_ref)
                keys = [(_pair_rows(kprev[prev_rows, :], masks), _pair_rows(vprev[prev_rows, :], masks)),
                        (_pair_rows(kc_ref[rows, :], masks), _pair_rows(vc_ref[rows, :], masks))]
                q = (q_ref[rows, :] * SCALE).astype(BF16)
                dy_v = dy_ref[rows, :]
                dyb = dy_v.astype(BF16)
                stats = [_row_stats(mh, dy_v, y_ref[rows, :], lse_ref[rows, :]) for mh in masks]
                delta = jnp.where(first_head, stats[0][0], stats[1][0])
                lse_h = jnp.where(first_head, stats[0][1], stats[1][1])
                dq = jnp.zeros((BLK, BLK), F32)
                dk, dv = [], []
                for half in range(2):
                    kh, vh = keys[half]
                    sc = lax.dot_general(q, kh, _DIMS["nt"], preferred_element_type=F32) + b_ref[half]
                    pr = jnp.exp(sc - lse_h)
                    if b == 0 and half == 0:
                        pr = pr * (i > 0).astype(F32)
                    dp = lax.dot_general(dyb, vh, _DIMS["nt"], preferred_element_type=F32)
                    ds = pr * (dp - delta)
                    db_ref[half] += ds
                    dsb = ds.astype(BF16)
                    dq = dq + jnp.dot(dsb, kh, preferred_element_type=F32)
                    dk2 = lax.dot_general(dsb, q, _DIMS["tn"], preferred_element_type=F32)
                    dv2 = lax.dot_general(pr.astype(BF16), dyb, _DIMS["tn"], preferred_element_type=F32)
                    dk.append(jnp.where(masks[0], dk2[:BLK], dk2[BLK:]))
                    dv.append(jnp.where(masks[0], dv2[:BLK], dv2[BLK:]))
                dq = dq * SCALE
                if not first:
                    dq = dq + dqp_ref[rows, :]
                dq_ref[rows, :] = dq
                if b > 0:
                    flush(prev_rows, carry[0] + dk[0], carry[1] + dv[0])
                else:
                    dkx_ref[prev_rows, :] = dk[0]
                    dvx_ref[prev_rows, :] = dv[0]
                carry = (dk[1], dv[1])
            flush(_rows(r + p * (nb - 1), dil), *carry)

    cur, prv = _zcur(w), _zprev(p, nb)
    b_spec = pl.BlockSpec((None, 2, BLK, 2 * BLK), lambda hp, i: (hp, 0, 0, 0))
    in_specs = [cur(0), prv(4), cur(4), prv(8), cur(8), b_spec] + [_scur(w)] * 3
    args = [z, z, z, z, z, bias, dy, y, lse]
    if not first:
        in_specs += [_scur(w)] * 3
        args += list(prev)
    x_spec = pl.BlockSpec((p, BLK), lambda hp, i: (i, hp))
    *outs, db = pl.pallas_call(
        body, name=name, grid=(4, n_steps), in_specs=in_specs,
        out_specs=[_scur(w)] * 3 + [x_spec] * 2 + [b_spec],
        out_shape=[_sds((s, ATTN_W))] * 3 + [_sds((n_steps * p, ATTN_W))] * 2 + [_sds((4, 2, BLK, 2 * BLK))],
        compiler_params=_params(("parallel", "arbitrary")),
    )(*args)
    return (*outs, _unpair_bias_bwd(db))


ASM_ROWS = 512


def _assemble_dz(dq, dk, dv, extras, dzs, du, name):
    s = dq.shape[0]
    w = min(ATT_ROWS, s)
    n_steps = s // w
    per_step = w // ASM_ROWS
    assert w % ASM_ROWS == 0

    def body(*refs):
        dq_ref, dk_ref, dv_ref, dzs_ref, du_ref = refs[:5]
        x_refs = refs[5:5 + 2 * len(extras)]
        o_ref, acc_ref = refs[-2:]
        j = pl.program_id(0)
        step = j // per_step
        has_next = (step < n_steps - 1).astype(F32)
        last_of_step = ((j + 1) % per_step == 0).astype(F32)
        o_ref[:, 0:ATTN_W] = dq_ref[...].astype(BF16)
        o_ref[:, 3 * ATTN_W:3 * ATTN_W + 2 * SGU_W] = dzs_ref[...].astype(BF16)
        o_ref[:, 3 * ATTN_W + 2 * SGU_W:IN_W] = du_ref[...].astype(BF16)
        for part, (base_ref, col) in enumerate(((dk_ref, ATTN_W), (dv_ref, 2 * ATTN_W))):
            acc_ref[...] = base_ref[...]
            for n, (_, dil) in enumerate(BRANCHES):
                rows = min(BLK * dil, ASM_ROWS)
                scale = has_next if BLK * dil >= w else has_next * last_of_step
                acc_ref[ASM_ROWS - rows:, :] += x_refs[2 * n + part][...] * scale
            o_ref[:, col:col + ATTN_W] = acc_ref[...].astype(BF16)

    def x_spec(dil):
        p = BLK * dil
        rows = min(p, ASM_ROWS)
        blocks_per_step = p // rows
        total = n_steps * blocks_per_step

        def idx(j):
            step = j // per_step
            within = (j % per_step) - (per_step - blocks_per_step)
            return (jnp.clip((step + 1) * blocks_per_step + jnp.maximum(within, 0), 0, total - 1), 0)

        return pl.BlockSpec((rows, ATTN_W), idx)

    in_specs = [_rb(ASM_ROWS, ATTN_W)] * 3 + [_rb(ASM_ROWS, 2 * SGU_W), _rb(ASM_ROWS, SSM_W)]
    args = [dq, dk, dv, dzs, du]
    for (dkx, dvx), (_, dil) in zip(extras, BRANCHES):
        in_specs += [x_spec(dil)] * 2
        args += [dkx, dvx]
    return pl.pallas_call(
        body, name=name, grid=(s // ASM_ROWS,), in_specs=in_specs, out_specs=_rb(ASM_ROWS, IN_W),
        out_shape=_sds((s, IN_W), BF16), scratch_shapes=[pltpu.VMEM((ASM_ROWS, ATTN_W), F32)],
        compiler_params=_params(("parallel",)),
    )(*args)


def _t5_bucket(dist):
    max_exact = N_BUCKETS // 2
    d = np.maximum(dist, 0)
    large = max_exact + (np.log(np.maximum(d, 1) / max_exact) / np.log(REL_MAX / max_exact)
                         * (N_BUCKETS - max_exact)).astype(np.int32)
    large = np.minimum(large, N_BUCKETS - 1)
    return np.where(d < max_exact, d, large).astype(np.int32)


def _bias_tables(rel_bias):
    period = 3 * BLK
    tabs = []
    for _, dil in BRANCHES:
        onehot = np.zeros((period, N_BUCKETS), np.float32)
        d = np.arange(BLK + 1)
        onehot[d, _t5_bucket((BLK - d) * dil)] = 1.0
        f = jnp.dot(jnp.asarray(onehot), rel_bias, precision=lax.Precision.HIGHEST)
        flat = jnp.tile(f.T, (1, BLK))[:, :BLK * (period - 1)]
        tab = flat.reshape(N_HEADS, BLK, period - 1)[:, :, :2 * BLK]
        tabs.append(jnp.where(_band()[None], tab, NEG_INF))
    return tabs


def _bucket_onehot():
    maps = []
    q = np.arange(BLK)[:, None]
    k = np.arange(2 * BLK)[None, :]
    rel = q + BLK - k
    for _, dil in BRANCHES:
        maps.append(np.where((rel >= 0) & (rel <= BLK), _t5_bucket(rel * dil), -1).reshape(-1))
    bmap = jnp.asarray(np.concatenate(maps).astype(np.int32))
    return (bmap[:, None] == jnp.arange(128, dtype=jnp.int32)[None, :]).astype(BF16)


def _block_diag(t):
    g, n, c = t.shape
    eye = jnp.eye(g, dtype=t.dtype)
    return (t[:, :, None, :] * eye[:, None, :, None]).reshape(g * n, g * c)


def _ssm_prep(a_re, a_im, log_dt, b_re, b_im, c_re, c_im):
    lam = lax.complex(a_re, a_im)
    dt = jnp.exp(log_dt)[:, None]
    a_bar = jnp.exp(lam * dt)
    b_bar = ((a_bar - 1.0) / lam)[:, :, None] * lax.complex(b_re, b_im)
    bdt = jnp.concatenate([_block_diag(jnp.real(b_bar)), _block_diag(jnp.imag(b_bar))], axis=0)
    cd = jnp.concatenate([_block_diag(jnp.transpose(c_re, (0, 2, 1))),
                          _block_diag(-jnp.transpose(c_im, (0, 2, 1)))], axis=0)
    return jnp.real(a_bar).reshape(-1), jnp.imag(a_bar).reshape(-1), bdt, cd


def _powers(ar, ai):
    pr, pi = ar[:, None], ai[:, None]
    k = 1
    while k < 8:
        lr, li = pr[:, -1:], pi[:, -1:]
        pr, pi = (jnp.concatenate([pr, pr * lr - pi * li], axis=1),
                  jnp.concatenate([pi, pr * li + pi * lr], axis=1))
        k *= 2
    return pr, pi


def _sgu_bias_expand(b):
    return jnp.repeat(b.T, 64, axis=1)


def _layer_fwd(i, h, p_i, big, small, bias_tabs):
    nm = "l%d_" % i
    sv = {"h": h}
    a1 = _rms_fwd(h, small["norm_attn_g"][i], nm + "rms_attn")
    z = _mm(a1, big["w_in"], "nt", nm + "in_proj")
    st = None
    for b, (_, dil) in enumerate(BRANCHES):
        st = _attn_fwd(z, bias_tabs[b][0], st, dil, b == 0, b == len(BRANCHES) - 1, nm + "attn_fwd%d" % b)
    y_attn, lse = st
    bexp = _sgu_bias_expand(small["sgu_b"][i])
    y_sgu = _sgu_fwd(z, small["sgu_ln_g"][i], small["sgu_ln_b"][i], small["sgu_w"][i], bexp, nm + "sgu_fwd")
    ar, ai, bdt, cd = _ssm_prep(*[small[k][i] for k in ("ssm_a_re", "ssm_a_im", "ssm_log_dt", "ssm_b_re",
                                                         "ssm_b_im", "ssm_c_re", "ssm_c_im")])
    xr, xi, yc = _ssm_fwd(z, bdt.astype(BF16), cd.astype(BF16), _scan_tables(*_powers(ar, ai), False),
                          nm + "ssm_core")
    y_ssm = _ssm_post_fwd(yc, z, small["ssm_d"][i], big["ssm_glu_w"], small["ssm_glu_b"][i], nm + "ssm_post")
    mix = _mix_fwd(y_attn, y_sgu, y_ssm, small["branch_norm_g"][i], nm + "mix")
    h2 = _mm(mix, big["w_out"], "nn", nm + "out_proj", add=h)
    a2 = _rms_fwd(h2, small["norm_ffn_g"][i], nm + "rms_ffn")
    hu = _mm(a2, big["ffn_w_up"], "nt", nm + "ffn_up")
    hv, hg, act = _conv_fwd(hu, big["ffn_conv_w"], small["ffn_conv_b"][i], nm + "ffn_conv")
    h3 = _mm(act, big["ffn_w_down"], "nn", nm + "ffn_down", add=h2)
    a3 = _rms_fwd(h3, small["norm_ple_g"][i], nm + "rms_ple")
    gp = _mm(a3, big["ple_w_gate"], "nn", nm + "ple_gate")
    pp = _mm(p_i, big["ple_w_proj"], "nt", nm + "ple_proj")
    h4 = _ple_fwd(h3, gp, pp, nm + "ple_add")
    sv.update(a1=a1, z=z, y_attn=y_attn, lse=lse, y_sgu=y_sgu, y_ssm=y_ssm, yc=yc, xr=xr, xi=xi, mix=mix, h2=h2,
              a2=a2, hu=hu, hv=hv, hg=hg, act=act, h3=h3, a3=a3, gp=gp, pp=pp)
    return h4, sv


def _layer_bwd(i, dh4, sv, p_i, big, small, bias_tabs, ffn_done=None):
    nm = "l%d_" % i
    g = {}
    dpp, dgp = _ple_bwd(dh4, sv["gp"], sv["pp"], nm + "ple_bwd")
    g["ple_w_proj"] = _mm(dpp, p_i, "tn", nm + "d_ple_proj", out_dtype=BF16)
    g["ple_w_gate"] = _mm(sv["a3"], dgp, "tn", nm + "d_ple_gate", out_dtype=BF16)
    da3 = _mm(dgp, big["ple_w_gate"], "nt", nm + "ple_gate_t")
    dh3, g["norm_ple_g"] = _rms_bwd(da3, sv["h3"], small["norm_ple_g"][i], dh4, nm + "rms_ple_bwd")
    g["ffn_w_down"] = _mm(sv["act"], dh3, "tn", nm + "d_ffn_down", out_dtype=BF16)
    dact = _mm(dh3, big["ffn_w_down"], "nt", nm + "ffn_down_t", out_dtype=BF16)
    dhu, g["ffn_conv_w"], dcb = _conv_bwd(dact, sv["hv"], sv["hg"], sv["hu"], big["ffn_conv_w"],
                                          nm + "ffn_conv_bwd")
    g["ffn_conv_b"] = dcb.reshape(2 * D_FF)
    g["ffn_w_up"] = _mm(dhu, sv["a2"], "tn", nm + "d_ffn_up", out_dtype=BF16)
    da2 = _mm(dhu, big["ffn_w_up"], "nn", nm + "ffn_up_t")
    dh2, g["norm_ffn_g"] = _rms_bwd(da2, sv["h2"], small["norm_ffn_g"][i], dh3, nm + "rms_ffn_bwd")
    if ffn_done is not None:
        small = ffn_done(g, small)
    g["w_out"] = _mm(sv["mix"], dh2, "tn", nm + "d_out_proj", out_dtype=BF16)
    dmix = _mm(dh2, big["w_out"], "nt", nm + "out_proj_t")
    dya, dysg, dyss, g["branch_norm_g"] = _mix_bwd(dmix, sv["y_attn"], sv["y_sgu"], sv["y_ssm"],
                                                   small["branch_norm_g"][i], nm + "mix_bwd")
    ssm_keys = ("ssm_a_re", "ssm_a_im", "ssm_log_dt", "ssm_b_re", "ssm_b_im", "ssm_c_re", "ssm_c_im")
    (ar, ai, bdt, cd), prep_vjp = jax.vjp(_ssm_prep, *[small[k][i] for k in ssm_keys])
    dy1, dgl, y2, dud, g["ssm_d"], g["ssm_glu_b"] = _ssm_post_bwd(
        dyss, sv["yc"], sv["z"], small["ssm_d"][i], big["ssm_glu_w"], small["ssm_glu_b"][i], nm + "ssm_post_bwd")
    g["ssm_glu_w"] = _mm(y2, dgl, "tn", nm + "d_ssm_glu", out_dtype=BF16)
    du, dbdt, dcd, dar, dai = _ssm_bwd(dy1, dud, sv["z"], sv["xr"], sv["xi"], bdt.astype(BF16), cd.astype(BF16),
                                       _scan_tables(*_powers(ar, ai), True), nm + "ssm_core_bwd")
    for k, val in zip(ssm_keys, prep_vjp((dar, dai, dbdt, dcd))):
        g[k] = val
    bexp, bexp_vjp = jax.vjp(_sgu_bias_expand, small["sgu_b"][i])
    dzs, g["sgu_w"], dbexp, g["sgu_ln_g"], g["sgu_ln_b"] = _sgu_bwd(
        sv["z"], dysg, small["sgu_ln_g"][i], small["sgu_ln_b"][i], small["sgu_w"][i], bexp, nm + "sgu_bwd")
    g["sgu_b"] = bexp_vjp(dbexp)[0]
    prev = None
    dbs, extras = [], []
    for b, (_, dil) in enumerate(BRANCHES):
        dq, dk, dv, dkx, dvx, db = _attn_bwd(sv["z"], bias_tabs[b][1], dya, sv["y_attn"], sv["lse"], prev, dil,
                                             nm + "attn_bwd%d" % b)
        prev = (dq, dk, dv)
        extras.append((dkx, dvx))
        dbs.append(db.reshape(N_HEADS, BLK * 2 * BLK))
    dz = _assemble_dz(dq, dk, dv, extras, dzs, du, nm + "assemble_dz")
    g["w_in"] = _mm(dz, sv["a1"], "tn", nm + "d_in_proj", out_dtype=BF16)
    da1 = _mm(dz, big["w_in"], "nn", nm + "in_proj_t")
    dh, g["norm_attn_g"] = _rms_bwd(da1, sv["h"], small["norm_attn_g"][i], dh2, nm + "rms_attn_bwd")
    return dh, g, jnp.concatenate(dbs, axis=1)


def _local_step(x, p, target, layer_weights, small, layer_done=None):
    depth = p.shape[0]
    bias_tabs = [(t, _pair_bias_bwd(t)) for t in _bias_tables(small["rel_bias"])]
    h = x
    saved, bigs = [], []
    for i in range(depth):
        bigs.append(layer_weights(i, h))
        h, sv = _layer_fwd(i, h, p[i], bigs[i], small, bias_tabs)
        saved.append(sv)
    dh, loss, g_final = _loss_head(h, target, small["final_norm_g"], "loss_head")
    layer_grads = [None] * depth
    dbias = [None] * depth
    for i in reversed(range(depth)):
        ffn_done = None if layer_done is None else (lambda g, sm, i=i: layer_done(i, "ffn", g, sm))
        dh, layer_grads[i], dbias[i] = _layer_bwd(i, dh, saved[i], p[i], bigs[i], small, bias_tabs, ffn_done)
        if layer_done is not None:
            small = layer_done(i, "all", layer_grads[i], small)
    big_grads = [{k: lg.pop(k) for k in COMM_NAMES} for lg in layer_grads]
    grads = {k: jnp.stack([layer_grads[i][k] for i in range(depth)]) for k in layer_grads[0]}
    grads["final_norm_g"] = g_final
    g_rb = _mm(sum(dbias[1:], dbias[0]), _bucket_onehot(), "nn", "d_rel_bias", tk=2048)
    grads["rel_bias"] = g_rb[:, :N_BUCKETS].T
    return loss, dh, big_grads, grads


_ANY = pl.BlockSpec(memory_space=pl.ANY)
MESH_IDS = pl.DeviceIdType.MESH


def _slot(ref, axis, j):
    return ref.at[(slice(None),) * axis + (j,)]


def _all_gather(blocks, axis, name):
    nt = len(blocks)

    def body(*refs):
        x_refs, o_refs = refs[:nt], refs[nt:2 * nt]
        send_sems, recv_sems, local_sems = refs[2 * nt:]
        x, y, c = lax.axis_index("x"), lax.axis_index("y"), lax.axis_index("c")
        me, sibling = (x, y, c), (x, y, 1 - c)
        chips = [(1 - x, y), (x, 1 - y), (1 - x, 1 - y)]

        def slot(t, px, py, pc):
            return _slot(o_refs[t], axis, 4 * px + 2 * py + pc)

        def copy(t, k, blk, to, src=None):
            return pltpu.make_async_remote_copy(
                src_ref=slot(t, *blk) if src is None else src, dst_ref=slot(t, *blk),
                send_sem=send_sems.at[7 * t + k], recv_sem=recv_sems.at[7 * t + k],
                device_id=to, device_id_type=MESH_IDS)

        mine = [pltpu.make_async_copy(x_refs[t], slot(t, *me), local_sems.at[t]) for t in range(nt)]
        for cp in mine:
            cp.start()
        first = []
        for t in range(nt):
            first.append(copy(t, 0, me, sibling, src=x_refs[t]))
            first += [copy(t, 1 + j, me, (*chip, c), src=x_refs[t]) for j, chip in enumerate(chips)]
        for cp in first:
            cp.start()
        passed = []
        for t in range(nt):
            for j, chip in enumerate(chips):
                copy(t, 1 + j, (*chip, c), me).wait_recv()
                passed.append(copy(t, 4 + j, (*chip, c), sibling))
                passed[-1].start()
        for t in range(nt):
            copy(t, 0, sibling, me).wait_recv()
            for j, chip in enumerate(chips):
                copy(t, 4 + j, (*chip, 1 - c), me).wait_recv()
        for cp in first + passed:
            cp.wait_send()
        for cp in mine:
            cp.wait()

    out_shape = [jax.ShapeDtypeStruct(b.shape[:axis] + (N_DEV,) + b.shape[axis:], b.dtype) for b in blocks]
    return pl.pallas_call(
        body, name=name, out_shape=out_shape, in_specs=[_ANY] * nt, out_specs=[_ANY] * nt,
        scratch_shapes=[pltpu.SemaphoreType.DMA((7 * nt,)), pltpu.SemaphoreType.DMA((7 * nt,)),
                        pltpu.SemaphoreType.DMA((nt,))],
    )(*blocks)


def _peer(k):
    x, y, c = lax.axis_index("x"), lax.axis_index("y"), lax.axis_index("c")
    px = 1 - x if k & 4 else x
    py = 1 - y if k & 2 else y
    pc = 1 - c if k & 1 else c
    return (px, py, pc), 4 * px + 2 * py + pc


def _all_to_all(blocks, name):
    nt = len(blocks)

    def body(*refs):
        x_refs, o_refs = refs[:nt], refs[nt:2 * nt]
        send_sems, recv_sems, local_sems = refs[2 * nt:]
        _, me = _peer(0)
        mine = [pltpu.make_async_copy(x_refs[t].at[me], o_refs[t].at[me], local_sems.at[t]) for t in range(nt)]
        for cp in mine:
            cp.start()
        copies = []
        for k in range(1, N_DEV):
            peer, idx = _peer(k)
            for t in range(nt):
                cp = pltpu.make_async_remote_copy(
                    src_ref=x_refs[t].at[idx], dst_ref=o_refs[t].at[me],
                    send_sem=send_sems.at[7 * t + k - 1], recv_sem=recv_sems.at[7 * t + k - 1],
                    device_id=peer, device_id_type=MESH_IDS)
                cp.start()
                copies.append(cp)
        for cp in copies:
            cp.wait()
        for cp in mine:
            cp.wait()

    return pl.pallas_call(
        body, name=name, out_shape=[jax.ShapeDtypeStruct(b.shape, b.dtype) for b in blocks],
        in_specs=[_ANY] * nt, out_specs=[_ANY] * nt,
        scratch_shapes=[pltpu.SemaphoreType.DMA((7 * nt,)), pltpu.SemaphoreType.DMA((7 * nt,)),
                        pltpu.SemaphoreType.DMA((nt,))],
    )(*blocks)


_HBM = pl.BlockSpec(memory_space=pltpu.HBM)
_SEM = pl.BlockSpec(memory_space=pltpu.SEMAPHORE)
_EFFECT = pltpu.SideEffectType.DATAFLOW_SIDE_EFFECTING


def _split_copy(src_ref, land_ref, send_sems, recv_sems, t, k, gather):
    peer, idx = _peer(k)
    _, me = _peer(0)
    return pltpu.make_async_remote_copy(
        src_ref=src_ref if gather else src_ref.at[idx], dst_ref=land_ref.at[me],
        send_sem=send_sems.at[7 * t + k - 1], recv_sem=recv_sems.at[7 * t + k - 1],
        device_id=peer, device_id_type=MESH_IDS)


def _exchange_start(srcs, lands, gather, name):
    nt = len(srcs)

    def body(*refs):
        src_refs, land_refs = refs[:nt], refs[nt:2 * nt]
        send_sems, recv_sems = refs[2 * nt:2 * nt + 2]
        token = refs[-1]
        for k in range(1, N_DEV):
            for t in range(nt):
                _split_copy(src_refs[t], land_refs[t], send_sems, recv_sems, t, k, gather).start()
        token[...] = jnp.zeros_like(token)

    hbm = lambda a: pltpu.HBM(a.shape, a.dtype)
    outs = pl.pallas_call(
        body, name=name,
        out_shape=(pltpu.SemaphoreType.DMA((7 * nt,)), pltpu.SemaphoreType.DMA((7 * nt,)),
                   *[hbm(a) for a in srcs], *[hbm(a) for a in lands], jax.ShapeDtypeStruct((8, 128), F32)),
        in_specs=[_HBM] * (2 * nt),
        out_specs=(_SEM, _SEM, *[_HBM] * (2 * nt), pl.BlockSpec(memory_space=pltpu.VMEM)),
        input_output_aliases={j: 2 + j for j in range(2 * nt)},
        compiler_params=pltpu.CompilerParams(has_side_effects=_EFFECT),
    )(*[pltpu.with_memory_space_constraint(a, pltpu.HBM) for a in list(srcs) + list(lands)])
    return outs[0], outs[1], outs[2:2 + nt], outs[2 + nt:2 + 2 * nt], outs[-1]


def _exchange_wait(send_sems, recv_sems, srcs, lands, after, gather, name):
    nt = len(srcs)

    def body(*refs):
        src_refs, land_refs = refs[:nt], refs[nt:2 * nt]
        send_sems, recv_sems = refs[2 * nt:2 * nt + 2]
        for k in range(1, N_DEV):
            _, idx = _peer(k)
            for t in range(nt):
                _split_copy(src_refs[t], land_refs[t], send_sems, recv_sems, t, k, gather).wait_send()
                arrival = pltpu.make_async_remote_copy(
                    src_ref=land_refs[t].at[idx], dst_ref=land_refs[t].at[idx],
                    send_sem=send_sems.at[7 * t + k - 1], recv_sem=recv_sems.at[7 * t + k - 1],
                    device_id=_peer(k)[0], device_id_type=MESH_IDS)
                arrival.wait_recv()

    hbm = lambda a: pltpu.HBM(a.shape, a.dtype)
    outs = pl.pallas_call(
        body, name=name, out_shape=tuple(hbm(a) for a in list(srcs) + list(lands)),
        in_specs=[_HBM] * (2 * nt) + [_SEM, _SEM, _ANY], out_specs=tuple([_HBM] * (2 * nt)),
        input_output_aliases={j: j for j in range(2 * nt)},
        compiler_params=pltpu.CompilerParams(has_side_effects=_EFFECT),
    )(*srcs, *lands, send_sems, recv_sems, after)
    return outs[nt:]


def _adamw(parts, w, m, v, name, tr):
    n_layers, r, c_ = w.shape
    assert len(parts) == n_layers

    def body(*refs):
        p_refs = refs[:n_layers]
        w_ref, m_ref, v_ref, g_ref, d_ref, mo_ref, vo_ref = refs[n_layers:]

        def update(p_ref):
            g = p_ref[0].astype(F32)
            for j in range(1, N_DEV):
                g = g + p_ref[j].astype(F32)
            m2 = ADAM_B1 * m_ref[...] + (1.0 - ADAM_B1) * g
            v2 = ADAM_B2 * v_ref[...] + (1.0 - ADAM_B2) * (g * g)
            m_hat = m2 / (1.0 - ADAM_B1 ** ADAM_STEP)
            v_hat = v2 / (1.0 - ADAM_B2 ** ADAM_STEP)
            g_ref[...] = g
            d_ref[...] = -ADAM_LR * (m_hat / (jnp.sqrt(v_hat) + ADAM_EPS) + ADAM_WD * w_ref[...])
            mo_ref[...] = m2
            vo_ref[...] = v2

        for layer in range(n_layers):
            pl.when(pl.program_id(0) == layer)(lambda layer=layer: update(p_refs[layer]))

    spec = pl.BlockSpec((None, tr, c_), lambda l, i: (l, i, 0))
    p_spec = pl.BlockSpec((N_DEV, tr, c_), lambda l, i: (0, i, 0))
    return pl.pallas_call(
        body, name=name, grid=(n_layers, r // tr), in_specs=[p_spec] * n_layers + [spec] * 3,
        out_specs=[spec] * 4, out_shape=[_sds((n_layers, r, c_))] * 4,
        compiler_params=_params(("parallel", "parallel")),
    )(*parts, w, m, v)


def _pack_rows(n_elems, align):
    rows = -(-n_elems // PACK_COLS)
    return -(-rows // align) * align


def _pack(arrs, rows, dtype=F32):
    flat = jnp.concatenate([a.reshape(-1) for a in arrs]).astype(dtype)
    return jnp.pad(flat, (0, rows * PACK_COLS - flat.shape[0])).reshape(rows, PACK_COLS)


def _unpack(pack, shapes):
    flat = pack.reshape(-1)
    out, off = [], 0
    for shp in shapes:
        size = int(np.prod(shp))
        out.append(flat[off:off + size].reshape(shp))
        off += size
    return out


def _tile_rows(rows, target, align=16):
    best = align
    for t in range(align, target + 1, align):
        if rows % t == 0:
            best = t
    return best


COMM_NAMES = ("w_in", "ssm_glu_w", "w_out", "ffn_w_up", "ffn_w_down", "ple_w_gate", "ple_w_proj")
COMM_TRANSPOSED = ("w_in", "ffn_w_up", "ple_w_proj")
COMM_EARLY = ("ple_w_proj", "ple_w_gate", "ffn_w_down", "ffn_w_up")
COMM_LATE = ("w_in", "ssm_glu_w", "w_out")
SMALL_TILE_ROWS = 64
CONV_NAME = "ffn_conv_w"


def _to_comm(name, a):
    return jnp.swapaxes(a, 1, 2) if name in COMM_TRANSPOSED else a


def kernel(x, p, rel_bias, norm_attn_g, w_in, sgu_ln_g, sgu_ln_b, sgu_w, sgu_b, ssm_a_re, ssm_a_im, ssm_log_dt, ssm_b_re, ssm_b_im, ssm_c_re, ssm_c_im, ssm_d, ssm_glu_w, ssm_glu_b, branch_norm_g, w_out, norm_ffn_g, ffn_w_up, ffn_conv_w, ffn_conv_b, ffn_w_down, norm_ple_g, ple_w_gate, ple_w_proj, final_norm_g, loss_target, m_rel_bias, m_norm_attn_g, m_w_in, m_sgu_ln_g, m_sgu_ln_b, m_sgu_w, m_sgu_b, m_ssm_a_re, m_ssm_a_im, m_ssm_log_dt, m_ssm_b_re, m_ssm_b_im, m_ssm_c_re, m_ssm_c_im, m_ssm_d, m_ssm_glu_w, m_ssm_glu_b, m_branch_norm_g, m_w_out, m_norm_ffn_g, m_ffn_w_up, m_ffn_conv_w, m_ffn_conv_b, m_ffn_w_down, m_norm_ple_g, m_ple_w_gate, m_ple_w_proj, m_final_norm_g, v_rel_bias, v_norm_attn_g, v_w_in, v_sgu_ln_g, v_sgu_ln_b, v_sgu_w, v_sgu_b, v_ssm_a_re, v_ssm_a_im, v_ssm_log_dt, v_ssm_b_re, v_ssm_b_im, v_ssm_c_re, v_ssm_c_im, v_ssm_d, v_ssm_glu_w, v_ssm_glu_b, v_branch_norm_g, v_w_out, v_norm_ffn_g, v_ffn_w_up, v_ffn_conv_w, v_ffn_conv_b, v_ffn_w_down, v_norm_ple_g, v_ple_w_gate, v_ple_w_proj, v_final_norm_g):
    given = dict(locals())
    w = {n: given[n] for n in WEIGHT_NAMES}
    m = {n: given["m_" + n] for n in WEIGHT_NAMES}
    v = {n: given["v_" + n] for n in WEIGHT_NAMES}
    depth = p.shape[0]
    dev = 4 * lax.axis_index("x") + 2 * lax.axis_index("y") + lax.axis_index("c")

    wc = {n: _to_comm(n, w[n]) for n in COMM_NAMES}
    wb = {n: wc[n].astype(BF16) for n in COMM_NAMES}
    conv_local = [w[CONV_NAME], m[CONV_NAME], v[CONV_NAME]]
    conv_rows = _pack_rows(sum(int(np.prod(t.shape)) for t in conv_local), 8)
    conv_g, = _all_gather([_pack(conv_local, conv_rows)], 0, "gather_conv_taps")
    conv_parts = zip(*[_unpack(conv_g[j], [t.shape for t in conv_local]) for j in range(N_DEV)])
    conv_w, conv_m, conv_v = [jnp.concatenate(parts, axis=2) for parts in conv_parts]
    small = {n: w[n] for n in SMALL_NAMES}

    def whole(blocks):
        return {n: t.reshape(-1, t.shape[-1]) for n, t in zip(COMM_NAMES, blocks)}

    def own_slot(block):
        return lax.dynamic_update_slice_in_dim(jnp.zeros((N_DEV,) + block.shape, block.dtype), block[None], dev, 0)

    first = _all_gather([wb[n][0] for n in COMM_NAMES], 0, "gather_weights_0")
    in_flight = {}
    for i in range(1, depth):
        srcs, first = lax.optimization_barrier(([wb[n][i] for n in COMM_NAMES], first))
        in_flight[i] = _exchange_start(srcs, [own_slot(s) for s in srcs], True, "gather_weights_%d_start" % i)
        small["norm_attn_g"] = small["norm_attn_g"] + in_flight[i][4][0, 0]

    def layer_weights(i, h):
        if i == 0:
            got = whole(first)
        else:
            send_sems, recv_sems, srcs, lands, _ = in_flight.pop(i)
            got = whole(_exchange_wait(send_sems, recv_sems, srcs, lands, h, True, "gather_weights_%d_wait" % i))
        return dict(got, **{CONV_NAME: conv_w[i]})

    def as_slots(g, n):
        return g.reshape((N_DEV,) + wc[n].shape[1:])

    scattering = {}

    def layer_done(i, stage, g, small_now):
        if stage == "all" and i == 0:
            return small_now
        names = COMM_EARLY if stage == "ffn" else COMM_LATE
        srcs = [as_slots(g[n], n) for n in names]
        lands = [own_slot(lax.dynamic_index_in_dim(s, dev, 0, keepdims=False)) for s in srcs]
        started = _exchange_start(srcs, lands, False, "scatter_weight_grads_%d_%s_start" % (i, stage))
        scattering[i, stage] = (names, started)
        pin = "branch_norm_g" if stage == "ffn" else "norm_ple_g"
        return dict(small_now, **{pin: small_now[pin] + started[4][0, 0]})

    loss, dx, big_grads, grads = _local_step(x[0], p[:, 0], loss_target[0], layer_weights, small, layer_done)
    loss = lax.psum(loss, ("x", "y", "c"))

    recv = [{} for _ in range(depth)]
    last = _all_to_all([as_slots(big_grads[0][n], n) for n in COMM_LATE], "scatter_weight_grads_0_all")
    recv[0].update(zip(COMM_LATE, last))
    for (i, stage), (names, (send_sems, recv_sems, srcs, lands, _)) in scattering.items():
        got = _exchange_wait(send_sems, recv_sems, srcs, lands, dx, False,
                             "scatter_weight_grads_%d_%s_wait" % (i, stage))
        recv[i].update(zip(names, got))
    rep_names = SMALL_NAMES + (CONV_NAME,)
    rep_w = dict({n: w[n] for n in SMALL_NAMES}, **{CONV_NAME: conv_w})
    rep_m = dict({n: m[n] for n in SMALL_NAMES}, **{CONV_NAME: conv_m})
    rep_v = dict({n: v[n] for n in SMALL_NAMES}, **{CONV_NAME: conv_v})
    rep_shapes = [rep_w[n].shape for n in rep_names]
    rep_rows = _pack_rows(sum(int(np.prod(s)) for s in rep_shapes), SMALL_TILE_ROWS)
    rep_parts, = _all_gather([_pack([grads[n] for n in rep_names], rep_rows)], 0, "gather_small_grads")

    out = {}
    for n in COMM_NAMES:
        res = _adamw([recv[i][n] for i in range(depth)], wc[n], _to_comm(n, m[n]), _to_comm(n, v[n]),
                     "adamw_" + n, _tile_rows(wc[n].shape[1], 256))
        out[n] = [_to_comm(n, r) for r in res]
    rep_out = _adamw([rep_parts], *[_pack([src[n] for n in rep_names], rep_rows)[None] for src in (rep_w, rep_m, rep_v)],
                     "adamw_replicated", SMALL_TILE_ROWS)
    for n, vals in zip(rep_names, zip(*[_unpack(r[0], rep_shapes) for r in rep_out])):
        out[n] = list(vals)
    shard = ffn_conv_w.shape[2]
    out[CONV_NAME] = [lax.dynamic_slice_in_dim(t, dev * shard, shard, axis=2) for t in out[CONV_NAME]]
    results = [[out[n][kind] for n in WEIGHT_NAMES] for kind in range(4)]
    return (loss, dx[None], *results[0], *results[1], *results[2], *results[3])
```

```python
import math

import numpy as np
import jax
import jax.numpy as jnp
from jax import lax
from jax.experimental import pallas as pl
from jax.experimental.pallas import tpu as pltpu

F32 = jnp.float32
BF16 = jnp.bfloat16

D_MODEL = 1024
HEAD_DIM = 64
N_HEADS = 8
ATTN_W = 512
SGU_W = 256
SGU_GROUPS = 4
SGU_CHUNK = 128
SSM_W = 256
SSM_GROUPS = 16
SSM_CH = 16
SSM_STATE = 64
SSM_NS = SSM_GROUPS * SSM_STATE
IN_W = 2304
D_FF = 2816
PLE_DIM = 256
BRANCHES = ((128, 1), (512, 4), (2048, 16))
BLK = 128
N_BUCKETS = 32
REL_MAX = 2048
EPS = 1e-6
NEG_INF = -1e30
N_DEV = 8

ADAM_LR = 0.001
ADAM_B1 = 0.9
ADAM_B2 = 0.999
ADAM_EPS = 1e-08
ADAM_WD = 0.01
ADAM_STEP = 10

VMEM_LIMIT_BYTES = 56 * 1024 * 1024
GELU_C = math.sqrt(2.0 / math.pi)

SMALL_NAMES = ("rel_bias", "norm_attn_g", "sgu_ln_g", "sgu_ln_b", "sgu_w", "sgu_b", "ssm_a_re", "ssm_a_im",
               "ssm_log_dt", "ssm_b_re", "ssm_b_im", "ssm_c_re", "ssm_c_im", "ssm_d", "ssm_glu_b",
               "branch_norm_g", "norm_ffn_g", "ffn_conv_b", "norm_ple_g", "final_norm_g")
WEIGHT_NAMES = ("rel_bias", "norm_attn_g", "w_in", "sgu_ln_g", "sgu_ln_b", "sgu_w", "sgu_b", "ssm_a_re",
                "ssm_a_im", "ssm_log_dt", "ssm_b_re", "ssm_b_im", "ssm_c_re", "ssm_c_im", "ssm_d", "ssm_glu_w",
                "ssm_glu_b", "branch_norm_g", "w_out", "norm_ffn_g", "ffn_w_up", "ffn_conv_w", "ffn_conv_b",
                "ffn_w_down", "norm_ple_g", "ple_w_gate", "ple_w_proj", "final_norm_g")
PACK_COLS = 512


def _params(sem):
    return pltpu.CompilerParams(dimension_semantics=sem, vmem_limit_bytes=VMEM_LIMIT_BYTES)


def _pick(dim, target):
    if dim <= target:
        return dim
    best = None
    for t in range(128, target + 1, 128):
        if dim % t == 0:
            best = t
    return dim if best is None else best


def _gelu(x):
    return 0.5 * x * (1.0 + jnp.tanh(GELU_C * (x + 0.044715 * (x * x * x))))


def _gelu_grad(x):
    t = jnp.tanh(GELU_C * (x + 0.044715 * (x * x * x)))
    return 0.5 * (1.0 + t) + 0.5 * x * (1.0 - t * t) * (GELU_C * (1.0 + 3.0 * 0.044715 * (x * x)))


def _sigmoid(x):
    return 1.0 / (1.0 + jnp.exp(-x))


_DIMS = {"nn": (((1,), (0,)), ((), ())), "tn": (((0,), (0,)), ((), ())), "nt": (((1,), (1,)), ((), ()))}


def _mm(a, b, mode, name, add=None, out_dtype=F32, norm_gain=None, tm=1408, tn=1408, tk=1408):
    if mode == "nn":
        m, k = a.shape
        k2, n = b.shape
    elif mode == "tn":
        k, m = a.shape
        k2, n = b.shape
    else:
        m, k = a.shape
        n, k2 = b.shape
    assert k == k2, (name, a.shape, b.shape, mode)
    tm, tn, tk = _pick(m, tm), _pick(n, tn), _pick(k, tk)
    nk = k // tk
    dims = _DIMS[mode]
    has_add = add is not None
    has_norm = norm_gain is not None
    assert not has_norm or tn == n

    def body(*refs):
        a_ref, b_ref = refs[:2]
        rest = list(refs[2:])
        add_ref = rest.pop(0) if has_add else None
        g_ref = rest.pop(0) if has_norm else None
        o_ref = rest.pop(0)
        n_ref = rest.pop(0) if has_norm else None
        part = lax.dot_general(a_ref[...].astype(BF16), b_ref[...].astype(BF16), dims,
                               preferred_element_type=F32)

        def finish(r):
            if has_add:
                r = r + add_ref[...]
            o_ref[...] = r.astype(out_dtype)
            if has_norm:
                scale = lax.rsqrt(jnp.mean(r * r, axis=-1, keepdims=True) + EPS)
                n_ref[...] = (r * scale * g_ref[...]).astype(BF16)

        if nk == 1:
            finish(part)
            return
        acc_ref = refs[-1]
        kk = pl.program_id(2)

        @pl.when(kk == 0)
        def _():
            acc_ref[...] = part

        @pl.when((kk > 0) & (kk < nk - 1))
        def _():
            acc_ref[...] += part

        @pl.when(kk == nk - 1)
        def _():
            finish(acc_ref[...] + part)

    if mode == "tn":
        a_spec = pl.BlockSpec((tk, tm), lambda i, j, kk: (kk, i))
    else:
        a_spec = pl.BlockSpec((tm, tk), lambda i, j, kk: (i, kk))
    if mode == "nt":
        b_spec = pl.BlockSpec((tn, tk), lambda i, j, kk: (j, kk))
    else:
        b_spec = pl.BlockSpec((tk, tn), lambda i, j, kk: (kk, j))
    o_spec = pl.BlockSpec((tm, tn), lambda i, j, kk: (i, j))
    in_specs = [a_spec, b_spec] + ([o_spec] if has_add else [])
    args = (a, b) + ((add,) if has_add else ())
    out_specs, out_shape = o_spec, jax.ShapeDtypeStruct((m, n), out_dtype)
    if has_norm:
        in_specs.append(pl.BlockSpec((1, n), lambda i, j, kk: (0, 0)))
        args += (norm_gain.reshape(1, n),)
        out_specs, out_shape = [o_spec, o_spec], [out_shape, jax.ShapeDtypeStruct((m, n), BF16)]
    return pl.pallas_call(
        body, name=name, grid=(m // tm, n // tn, nk),
        in_specs=in_specs, out_specs=out_specs, out_shape=out_shape,
        scratch_shapes=[pltpu.VMEM((tm, tn), F32)] if nk > 1 else [],
        compiler_params=_params(("parallel", "parallel", "arbitrary")),
    )(*args)


def _rb(tm, w, cb=0):
    return pl.BlockSpec((tm, w), lambda i: (i, cb))


def _fb(shape):
    nd = len(shape)
    return pl.BlockSpec(shape, lambda i: (0,) * nd)


def _rowcall(body, name, n_rows, tm, in_specs, args, out_specs, out_shapes):
    return pl.pallas_call(
        body, name=name, grid=(n_rows // tm,), in_specs=in_specs, out_specs=out_specs, out_shape=out_shapes,
        compiler_params=_params(("arbitrary",)),
    )(*args)


def _sds(shape, dtype=F32):
    return jax.ShapeDtypeStruct(shape, dtype)


def _rms_fwd(h, g, name, tm=512):
    s, d = h.shape

    def body(h_ref, g_ref, o_ref):
        x = h_ref[...]
        r = lax.rsqrt(jnp.mean(x * x, axis=-1, keepdims=True) + EPS)
        o_ref[...] = (x * r * g_ref[...]).astype(BF16)

    return _rowcall(body, name, s, tm, [_rb(tm, d), _fb((1, d))], (h, g.reshape(1, d)), _rb(tm, d),
                    _sds((s, d), BF16))


def _rms_bwd(da, h, g, dres, name, tm=512):
    s, d = h.shape

    def body(da_ref, h_ref, g_ref, dres_ref, dh_ref, dg_ref):
        @pl.when(pl.program_id(0) == 0)
        def _():
            dg_ref[...] = jnp.zeros_like(dg_ref)

        x = h_ref[...]
        r = lax.rsqrt(jnp.mean(x * x, axis=-1, keepdims=True) + EPS)
        xh = x * r
        dy = da_ref[...]
        dg_ref[...] += jnp.sum(dy * xh, axis=0, keepdims=True)
        dxh = dy * g_ref[...]
        dh_ref[...] = dres_ref[...] + r * (dxh - xh * jnp.mean(dxh * xh, axis=-1, keepdims=True))

    dh, dg = _rowcall(body, name, s, tm, [_rb(tm, d), _rb(tm, d), _fb((1, d)), _rb(tm, d)],
                      (da, h, g.reshape(1, d), dres), [_rb(tm, d), _fb((1, d))], [_sds((s, d)), _sds((1, d))])
    return dh, dg.reshape(d)


def _loss_head(h, target, g, name, tm=512):
    s, d = h.shape

    def body(h_ref, t_ref, g_ref, dh_ref, loss_ref, dg_ref):
        @pl.when(pl.program_id(0) == 0)
        def _():
            dg_ref[...] = jnp.zeros_like(dg_ref)
            loss_ref[...] = jnp.zeros_like(loss_ref)

        x = h_ref[...]
        r = lax.rsqrt(jnp.mean(x * x, axis=-1, keepdims=True) + EPS)
        xh = x * r
        gg = g_ref[...]
        err = xh * gg - t_ref[...]
        loss_ref[...] += jnp.sum(err * err) * (0.5 / d)
        dy = err * (1.0 / d)
        dg_ref[...] += jnp.sum(dy * xh, axis=0, keepdims=True)
        dxh = dy * gg
        dh_ref[...] = r * (dxh - xh * jnp.mean(dxh * xh, axis=-1, keepdims=True))

    dh, loss, dg = _rowcall(body, name, s, tm, [_rb(tm, d), _rb(tm, d), _fb((1, d))], (h, target, g.reshape(1, d)),
                            [_rb(tm, d), _fb((1, 128)), _fb((1, d))], [_sds((s, d)), _sds((1, 128)), _sds((1, d))])
    return dh, loss[0, 0], dg.reshape(d)


_MIX_PARTS = ((0, 512), (512, 768), (768, 1024))


def _mix_fwd(ya, ysg, yss, g, name, tm=512):
    s = ya.shape[0]

    def body(a_ref, b_ref, c_ref, g_ref, o_ref):
        for ref, (lo, hi) in zip((a_ref, b_ref, c_ref), _MIX_PARTS):
            y = ref[...]
            r = lax.rsqrt(jnp.mean(y * y, axis=-1, keepdims=True) + EPS)
            o_ref[:, lo:hi] = (y * r * g_ref[:, lo:hi]).astype(BF16)

    return _rowcall(body, name, s, tm, [_rb(tm, 512), _rb(tm, 256), _rb(tm, 256), _fb((1, 1024))],
                    (ya, ysg, yss, g.reshape(1, 1024)), _rb(tm, 1024), _sds((s, 1024), BF16))


def _mix_bwd(dmix, ya, ysg, yss, g, name, tm=512):
    s = ya.shape[0]

    def body(dm_ref, a_ref, b_ref, c_ref, g_ref, da_ref, db_ref, dc_ref, dg_ref):
        @pl.when(pl.program_id(0) == 0)
        def _():
            dg_ref[...] = jnp.zeros_like(dg_ref)

        for ref, dref, (lo, hi) in zip((a_ref, b_ref, c_ref), (da_ref, db_ref, dc_ref), _MIX_PARTS):
            y = ref[...]
            r = lax.rsqrt(jnp.mean(y * y, axis=-1, keepdims=True) + EPS)
            xh = y * r
            dm = dm_ref[:, lo:hi]
            dg_ref[:, lo:hi] += jnp.sum(dm * xh, axis=0, keepdims=True)
            dxh = dm * g_ref[:, lo:hi]
            dref[...] = r * (dxh - xh * jnp.mean(dxh * xh, axis=-1, keepdims=True))

    da, db, dc, dg = _rowcall(
        body, name, s, tm, [_rb(tm, 1024), _rb(tm, 512), _rb(tm, 256), _rb(tm, 256), _fb((1, 1024))],
        (dmix, ya, ysg, yss, g.reshape(1, 1024)),
        [_rb(tm, 512), _rb(tm, 256), _rb(tm, 256), _fb((1, 1024))],
        [_sds((s, 512)), _sds((s, 256)), _sds((s, 256)), _sds((1, 1024))])
    return da, db, dc, dg.reshape(1024)


def _ssm_post_fwd(yc, z, d, gw, gb, name, tm=1024):
    s = yc.shape[0]

    def body(yc_ref, u_ref, d_ref, gw_ref, gb_ref, o_ref):
        y1 = yc_ref[...] + d_ref[...] * u_ref[...]
        y2 = _gelu(y1)
        gl = jnp.dot(y2.astype(BF16), gw_ref[...], preferred_element_type=F32) + gb_ref[...]
        o_ref[...] = y2 * _sigmoid(gl)

    return _rowcall(body, name, s, tm, [_rb(tm, 256), _rb(tm, 256, 8), _fb((1, 256)), _fb((256, 256)), _fb((1, 256))],
                    (yc, z, d.reshape(1, 256), gw, gb.reshape(1, 256)), _rb(tm, 256), _sds((s, 256)))


def _ssm_post_bwd(dy, yc, z, d, gw, gb, name, tm=1024):
    s = yc.shape[0]

    def body(dy_ref, yc_ref, u_ref, d_ref, gw_ref, gb_ref, dy1_ref, dgl_ref, y2_ref, dud_ref, dd_ref, dgb_ref):
        @pl.when(pl.program_id(0) == 0)
        def _():
            dd_ref[...] = jnp.zeros_like(dd_ref)
            dgb_ref[...] = jnp.zeros_like(dgb_ref)

        u = u_ref[...]
        dd = d_ref[...]
        y1 = yc_ref[...] + dd * u
        y2 = _gelu(y1)
        gw_v = gw_ref[...]
        gl = jnp.dot(y2.astype(BF16), gw_v, preferred_element_type=F32) + gb_ref[...]
        sg = _sigmoid(gl)
        dyv = dy_ref[...]
        dgl = dyv * y2 * sg * (1.0 - sg)
        dy2 = dyv * sg + lax.dot_general(dgl.astype(BF16), gw_v, _DIMS["nt"], preferred_element_type=F32)
        dy1 = dy2 * _gelu_grad(y1)
        dy1_ref[...] = dy1.astype(BF16)
        dgl_ref[...] = dgl.astype(BF16)
        y2_ref[...] = y2.astype(BF16)
        dud_ref[...] = dy1 * dd
        dd_ref[...] += jnp.sum(dy1 * u, axis=0, keepdims=True)
        dgb_ref[...] += jnp.sum(dgl, axis=0, keepdims=True)

    outs = _rowcall(
        body, name, s, tm,
        [_rb(tm, 256), _rb(tm, 256), _rb(tm, 256, 8), _fb((1, 256)), _fb((256, 256)), _fb((1, 256))],
        (dy, yc, z, d.reshape(1, 256), gw, gb.reshape(1, 256)),
        [_rb(tm, 256)] * 4 + [_fb((1, 256))] * 2,
        [_sds((s, 256), BF16)] * 3 + [_sds((s, 256))] + [_sds((1, 256))] * 2)
    dy1, dgl, y2, dud, dd, dgb = outs
    return dy1, dgl, y2, dud, dd.reshape(256), dgb.reshape(256)


SCAN_T = 512
N_SCAN_TABLES = 6


def _scan_tables(pr, pi, reverse):
    ns = pr.shape[0]
    sign = -1.0 if reverse else 1.0
    power = [(jnp.ones((ns,), F32), jnp.zeros((ns,), F32))] + [(pr[:, k], sign * pi[:, k]) for k in range(8)]
    zero = (jnp.zeros((ns,), F32), jnp.zeros((ns,), F32))

    def table(exponents):
        rows = [zero if e is None else power[e] for e in exponents]
        return jnp.stack([jnp.concatenate(row) for row in rows])

    tabs = []
    for k in (1, 2, 4):
        has_partner = [(s < 8 - k) if reverse else (s >= k) for s in range(8)]
        tabs.append(table([k if ok else None for ok in has_partner]))
    tabs.append(table([s if reverse else 7 - s for s in range(8)]))
    tabs.append(table([8 - s if reverse else s + 1 for s in range(8)]))
    tabs.append(table([8] * 8))
    return jnp.stack(tabs)


def _cmul(ar, ai, br, bi):
    return ar * br - ai * bi, ar * bi + ai * br


def _scan_group(ur, ui, cr, ci, tr_ref, ti_ref, reverse):
    xr, xi = ur, ui
    for n, k in enumerate((1, 2, 4)):
        shift = 8 - k if reverse else k
        pr, pi = _cmul(tr_ref[n], ti_ref[n], pltpu.roll(xr, shift, axis=0), pltpu.roll(xi, shift, axis=0))
        xr, xi = xr + pr, xi + pi
    sr, si = _cmul(tr_ref[3], ti_ref[3], ur, ui)
    for k in (1, 2, 4):
        sr, si = sr + pltpu.roll(sr, k, axis=0), si + pltpu.roll(si, k, axis=0)
    pr, pi = _cmul(tr_ref[4], ti_ref[4], cr, ci)
    nr, ni = _cmul(tr_ref[5], ti_ref[5], cr, ci)
    return xr + pr, xi + pi, nr + sr, ni + si


def _table_halves(t_ref):
    return t_ref.at[:, :, pl.ds(0, SSM_NS)], t_ref.at[:, :, pl.ds(SSM_NS, SSM_NS)]


_U_BLOCK = (IN_W - SSM_W) // SSM_W


def _ssm_fwd(z, bdt, cd, tabs, name):
    s = z.shape[0]
    ns = SSM_NS
    n_t = s // SCAN_T

    def body(u_ref, b_ref, c_ref, t_ref, xr_ref, xi_ref, y_ref, cr_ref, ci_ref, ur_ref, ui_ref):
        @pl.when(pl.program_id(0) == 0)
        def _():
            cr_ref[...] = jnp.zeros_like(cr_ref)
            ci_ref[...] = jnp.zeros_like(ci_ref)

        bu = lax.dot_general(u_ref[...].astype(BF16), b_ref[...], _DIMS["nt"], preferred_element_type=F32)
        ur_ref[...] = bu[:, :ns]
        ui_ref[...] = bu[:, ns:]
        tr_ref, ti_ref = _table_halves(t_ref)

        def group(g, carry):
            rows = pl.ds(pl.multiple_of(g * 8, 8), 8)
            xr, xi, cr, ci = _scan_group(ur_ref[rows, :], ui_ref[rows, :], *carry, tr_ref, ti_ref, False)
            xr_ref[rows, :] = xr
            xi_ref[rows, :] = xi
            return cr, ci

        cr, ci = lax.fori_loop(0, SCAN_T // 8, group, (cr_ref[...], ci_ref[...]), unroll=2)
        cr_ref[...] = cr
        ci_ref[...] = ci
        y_ref[...] = (jnp.dot(xr_ref[...].astype(BF16), c_ref[0:ns, :], preferred_element_type=F32)
                      + jnp.dot(xi_ref[...].astype(BF16), c_ref[ns:, :], preferred_element_type=F32))

    x_spec = pl.BlockSpec((SCAN_T, ns), lambda t: (t, 0))
    return pl.pallas_call(
        body, name=name, grid=(n_t,),
        in_specs=[pl.BlockSpec((SCAN_T, SSM_W), lambda t: (t, _U_BLOCK)), _fb((2 * ns, SSM_W)),
                  _fb((2 * ns, SSM_W)), _fb((N_SCAN_TABLES, 8, 2 * ns))],
        out_specs=[x_spec, x_spec, _rb(SCAN_T, SSM_W)],
        out_shape=[_sds((s, ns)), _sds((s, ns)), _sds((s, SSM_W))],
        scratch_shapes=[pltpu.VMEM((8, ns), F32)] * 2 + [pltpu.VMEM((SCAN_T, ns), F32)] * 2,
        compiler_params=_params(("arbitrary",)),
    )(z, bdt, cd, tabs)


def _ssm_bwd(dy1, dud, z, xr, xi, bdt, cd, tabs, name):
    s = z.shape[0]
    ns = SSM_NS
    n_t = s // SCAN_T
    n_groups = SCAN_T // 8

    def body(dy_ref, dud_ref, u_ref, xr_ref, xi_ref, pxr_ref, pxi_ref, b_ref, c_ref, t_ref,
             du_ref, dbd_ref, dcd_ref, dar_ref, dai_ref,
             cr_ref, ci_ref, ar_ref, ai_ref, sxr_ref, sxi_ref, gr_ref, gi_ref, lr_ref, li_ref, bacc_ref, cacc_ref):
        t = pl.program_id(0)

        @pl.when(t == 0)
        def _():
            for ref in (cr_ref, ci_ref, ar_ref, ai_ref, bacc_ref, cacc_ref):
                ref[...] = jnp.zeros_like(ref)

        dyb = dy_ref[...]
        g = lax.dot_general(dyb, c_ref[...], _DIMS["nt"], preferred_element_type=F32)
        gr_ref[...] = g[:, :ns]
        gi_ref[...] = g[:, ns:]
        has_before = (t < n_t - 1).astype(F32)
        sxr_ref[0:8, :] = pxr_ref[...] * has_before
        sxi_ref[0:8, :] = pxi_ref[...] * has_before
        sxr_ref[8:, :] = xr_ref[...]
        sxi_ref[8:, :] = xi_ref[...]
        first_row = lax.broadcasted_iota(jnp.int32, (8, ns), 0) == 0
        tr_ref, ti_ref = _table_halves(t_ref)

        def group(k, carry):
            cr, ci, ar, ai = carry
            g8 = pl.multiple_of((n_groups - 1 - k) * 8, 8)
            rows = pl.ds(g8, 8)
            lr, li, cr, ci = _scan_group(gr_ref[rows, :], gi_ref[rows, :], cr, ci, tr_ref, ti_ref, True)
            lr_ref[rows, :] = lr
            li_ref[rows, :] = li
            here, before = pl.ds(g8 + 8, 8), rows
            pr = jnp.where(first_row, pltpu.roll(sxr_ref[before, :], 1, axis=0), pltpu.roll(sxr_ref[here, :], 1, axis=0))
            pi = jnp.where(first_row, pltpu.roll(sxi_ref[before, :], 1, axis=0), pltpu.roll(sxi_ref[here, :], 1, axis=0))
            return cr, ci, ar + lr * pr + li * pi, ai + li * pr - lr * pi

        cr, ci, ar, ai = lax.fori_loop(0, n_groups, group,
                                       (cr_ref[...], ci_ref[...], ar_ref[...], ai_ref[...]), unroll=2)
        cr_ref[...] = cr
        ci_ref[...] = ci
        ar_ref[...] = ar
        ai_ref[...] = ai
        lrb = lr_ref[...].astype(BF16)
        lib = li_ref[...].astype(BF16)
        ub = u_ref[...].astype(BF16)
        du_ref[...] = (dud_ref[...] + jnp.dot(lrb, b_ref[0:ns, :], preferred_element_type=F32)
                       + jnp.dot(lib, b_ref[ns:, :], preferred_element_type=F32))
        bacc_ref[0:ns, :] += lax.dot_general(lrb, ub, _DIMS["tn"], preferred_element_type=F32)
        bacc_ref[ns:, :] += lax.dot_general(lib, ub, _DIMS["tn"], preferred_element_type=F32)
        cacc_ref[0:ns, :] += lax.dot_general(xr_ref[...].astype(BF16), dyb, _DIMS["tn"], preferred_element_type=F32)
        cacc_ref[ns:, :] += lax.dot_general(xi_ref[...].astype(BF16), dyb, _DIMS["tn"], preferred_element_type=F32)

        @pl.when(t == n_t - 1)
        def _():
            for k in (1, 2, 4):
                ar_ref[...] += pltpu.roll(ar_ref[...], k, axis=0)
                ai_ref[...] += pltpu.roll(ai_ref[...], k, axis=0)
            dar_ref[...] = ar_ref[...]
            dai_ref[...] = ai_ref[...]
            dbd_ref[...] = bacc_ref[...]
            dcd_ref[...] = cacc_ref[...]

    rev = lambda t: n_t - 1 - t
    row_spec = pl.BlockSpec((SCAN_T, SSM_W), lambda t: (rev(t), 0))
    x_spec = pl.BlockSpec((SCAN_T, ns), lambda t: (rev(t), 0))
    before_spec = pl.BlockSpec((8, ns), lambda t: (jnp.maximum(rev(t) * (SCAN_T // 8) - 1, 0), 0))
    du, dbd, dcd, dar, dai = pl.pallas_call(
        body, name=name, grid=(n_t,),
        in_specs=[row_spec, row_spec, pl.BlockSpec((SCAN_T, SSM_W), lambda t: (rev(t), _U_BLOCK)),
                  x_spec, x_spec, before_spec, before_spec,
                  _fb((2 * ns, SSM_W)), _fb((2 * ns, SSM_W)), _fb((N_SCAN_TABLES, 8, 2 * ns))],
        out_specs=[row_spec, _fb((2 * ns, SSM_W)), _fb((2 * ns, SSM_W)), _fb((8, ns)), _fb((8, ns))],
        out_shape=[_sds((s, SSM_W)), _sds((2 * ns, SSM_W)), _sds((2 * ns, SSM_W)), _sds((8, ns)), _sds((8, ns))],
        scratch_shapes=([pltpu.VMEM((8, ns), F32)] * 4 + [pltpu.VMEM((SCAN_T + 8, ns), F32)] * 2
                        + [pltpu.VMEM((SCAN_T, ns), F32)] * 4 + [pltpu.VMEM((2 * ns, SSM_W), F32)] * 2),
        compiler_params=_params(("arbitrary",)),
    )(dy1, dud, z, xr, xi, xr, xi, bdt, cd, tabs)
    return du, dbd, dcd, dar[0], dai[0]


def _group_ids():
    return lax.broadcasted_iota(jnp.int32, (1, SGU_W), 1) // 64


def _group_mean(val, gid):
    out = jnp.zeros_like(val)
    for g in range(SGU_GROUPS):
        mg = gid == g
        out = jnp.where(mg, jnp.sum(jnp.where(mg, val, 0.0), axis=1, keepdims=True) * (1.0 / 64), out)
    return out


def _causal_w(w_ref, g):
    t = lax.broadcasted_iota(jnp.int32, (SGU_CHUNK, SGU_CHUNK), 0)
    s = lax.broadcasted_iota(jnp.int32, (SGU_CHUNK, SGU_CHUNK), 1)
    return jnp.where(t >= s, w_ref[g], 0.0).astype(BF16)


def _sgu_core(x, lng, lnb, w_ref, bexp, gid):
    zz = _gelu(x)
    u = zz[:, :SGU_W]
    v = zz[:, SGU_W:]
    vc = v - _group_mean(v, gid)
    rstd = lax.rsqrt(_group_mean(vc * vc, gid) + EPS)
    vhat = vc * rstd
    vn = vhat * lng + lnb
    vnb = vn.astype(BF16)
    mixed = bexp
    for g in range(SGU_GROUPS):
        mm = jnp.dot(_causal_w(w_ref, g), vnb, preferred_element_type=F32)
        mixed = jnp.where(gid == g, mm + bexp, mixed)
    return u, rstd, vhat, vnb, mixed


def _sgu_fwd(z, lng, lnb, w, bexp, name, tm=512):
    s = z.shape[0]

    def body(z_ref, lng_ref, lnb_ref, w_ref, b_ref, o_ref):
        gid = _group_ids()
        for j in range(tm // SGU_CHUNK):
            rows = pl.ds(j * SGU_CHUNK, SGU_CHUNK)
            u, _, _, _, mixed = _sgu_core(z_ref[rows, :], lng_ref[...], lnb_ref[...], w_ref, b_ref[...], gid)
            o_ref[rows, :] = u * mixed

    return _rowcall(body, name, s, tm,
                    [_rb(tm, 512, 3), _fb((1, 256)), _fb((1, 256)), _fb((4, 128, 128)), _fb((128, 256))],
                    (z, lng.reshape(1, 256), lnb.reshape(1, 256), w, bexp), _rb(tm, 256), _sds((s, 256)))


def _sgu_bwd(z, dy, lng, lnb, w, bexp, name, tm=512):
    s = z.shape[0]

    def body(z_ref, dy_ref, lng_ref, lnb_ref, w_ref, b_ref, dz_ref, dw_ref, db_ref, dlng_ref, dlnb_ref):
        @pl.when(pl.program_id(0) == 0)
        def _():
            dw_ref[...] = jnp.zeros_like(dw_ref)
            db_ref[...] = jnp.zeros_like(db_ref)
            dlng_ref[...] = jnp.zeros_like(dlng_ref)
            dlnb_ref[...] = jnp.zeros_like(dlnb_ref)

        gid = _group_ids()
        t = lax.broadcasted_iota(jnp.int32, (SGU_CHUNK, SGU_CHUNK), 0)
        sidx = lax.broadcasted_iota(jnp.int32, (SGU_CHUNK, SGU_CHUNK), 1)
        lng_v = lng_ref[...]
        for j in range(tm // SGU_CHUNK):
            rows = pl.ds(j * SGU_CHUNK, SGU_CHUNK)
            x = z_ref[rows, :]
            u, rstd, vhat, vnb, mixed = _sgu_core(x, lng_v, lnb_ref[...], w_ref, b_ref[...], gid)
            dyv = dy_ref[rows, :]
            dmixed = dyv * u
            du = dyv * mixed
            db_ref[...] += dmixed
            dvn = jnp.zeros_like(dmixed)
            for g in range(SGU_GROUPS):
                dmg = jnp.where(gid == g, dmixed, 0.0).astype(BF16)
                dvn = dvn + lax.dot_general(_causal_w(w_ref, g), dmg, _DIMS["tn"], preferred_element_type=F32)
                dwg = lax.dot_general(dmg, vnb, _DIMS["nt"], preferred_element_type=F32)
                dw_ref[g] += jnp.where(t >= sidx, dwg, 0.0)
            dlnb_ref[...] += jnp.sum(dvn, axis=0, keepdims=True)
            dlng_ref[...] += jnp.sum(dvn * vhat, axis=0, keepdims=True)
            dvh = dvn * lng_v
            dv = rstd * (dvh - _group_mean(dvh, gid) - vhat * _group_mean(dvh * vhat, gid))
            gg = _gelu_grad(x)
            dz_ref[rows, 0:SGU_W] = du * gg[:, :SGU_W]
            dz_ref[rows, SGU_W:2 * SGU_W] = dv * gg[:, SGU_W:]

    dz, dw, db, dlng, dlnb = _rowcall(
        body, name, s, tm,
        [_rb(tm, 512, 3), _rb(tm, 256), _fb((1, 256)), _fb((1, 256)), _fb((4, 128, 128)), _fb((128, 256))],
        (z, dy, lng.reshape(1, 256), lnb.reshape(1, 256), w, bexp),
        [_rb(tm, 512), _fb((4, 128, 128)), _fb((128, 256)), _fb((1, 256)), _fb((1, 256))],
        [_sds((s, 512)), _sds((4, 128, 128)), _sds((128, 256)), _sds((1, 256)), _sds((1, 256))])
    return dz, dw, db, dlng.reshape(256), dlnb.reshape(256)


CONV_TC = 1408
N_CT = D_FF // CONV_TC


def _row_of(block8, j):
    r = lax.broadcasted_iota(jnp.int32, block8.shape, 0)
    return jnp.sum(jnp.where(r == j, block8, 0.0), axis=0, keepdims=True)


EDGE = 16


def _conv_fwd(hu, cw, cb, name, tm=256):
    s = hu.shape[0]

    def body(xv_ref, xg_ref, tv_ref, tg_ref, wv_ref, wg_ref, bv_ref, bg_ref, hv_ref, hg_ref, act_ref):
        has_prev = (pl.program_id(1) > 0).astype(F32)
        row = lax.broadcasted_iota(jnp.int32, (EDGE, CONV_TC), 0)

        def conv(x_ref, t_ref, w_ref, b_ref):
            x = x_ref[...].astype(F32)
            w0, w1, w2, bb = w_ref[0:1, :], w_ref[1:2, :], w_ref[2:3, :], b_ref[...]
            whole = w0 * pltpu.roll(x, 2, axis=0) + w1 * pltpu.roll(x, 1, axis=0) + w2 * x + bb
            tail = t_ref[...].astype(F32)
            r7 = _row_of(tail, EDGE - 1) * has_prev
            r6 = _row_of(tail, EDGE - 2) * has_prev
            xe = x_ref[0:EDGE, :].astype(F32)
            x1 = jnp.where(row == 0, r7, pltpu.roll(xe, 1, axis=0))
            x2 = jnp.where(row == 0, r6, jnp.where(row == 1, r7, pltpu.roll(xe, 2, axis=0)))
            return whole, w0 * x2 + w1 * x1 + w2 * xe + bb

        hv, hv_edge = conv(xv_ref, tv_ref, wv_ref, bv_ref)
        hg, hg_edge = conv(xg_ref, tg_ref, wg_ref, bg_ref)
        hv_ref[...] = hv.astype(BF16)
        hg_ref[...] = hg.astype(BF16)
        act_ref[...] = (_gelu(hg) * hv).astype(BF16)
        hv_ref[0:EDGE, :] = hv_edge.astype(BF16)
        hg_ref[0:EDGE, :] = hg_edge.astype(BF16)
        act_ref[0:EDGE, :] = (_gelu(hg_edge) * hv_edge).astype(BF16)

    def xs(off):
        return pl.BlockSpec((tm, CONV_TC), lambda j, i: (i, j + off))

    def ts(off):
        return pl.BlockSpec((EDGE, CONV_TC), lambda j, i: (jnp.maximum(i * (tm // EDGE) - 1, 0), j + off))

    def ws(rows, off):
        return pl.BlockSpec((rows, CONV_TC), lambda j, i: (0, j + off))

    o_spec = pl.BlockSpec((tm, CONV_TC), lambda j, i: (i, j))
    return pl.pallas_call(
        body, name=name, grid=(N_CT, s // tm),
        in_specs=[xs(0), xs(N_CT), ts(0), ts(N_CT), ws(3, 0), ws(3, N_CT), ws(1, 0), ws(1, N_CT)],
        out_specs=[o_spec] * 3, out_shape=[_sds((s, D_FF), BF16)] * 3,
        compiler_params=_params(("parallel", "arbitrary")),
    )(hu, hu, hu, hu, cw, cw, cb.reshape(1, 2 * D_FF), cb.reshape(1, 2 * D_FF))


HALO = EDGE


def _conv_bwd(dact, hv, hg, hu, cw, name, tm=256):
    s = dact.shape[0]

    def body(da_ref, dan_ref, hv_ref, hvn_ref, hg_ref, hgn_ref, x_ref, t_ref, w_ref, dx_ref, dw_ref, db_ref, d_scr):
        i = pl.program_id(1)
        is_value = pl.program_id(0) < N_CT

        @pl.when(i == 0)
        def _():
            dw_ref[...] = jnp.zeros_like(dw_ref)
            db_ref[...] = jnp.zeros_like(db_ref)

        for rows, (a_ref, v_ref, g_ref) in ((pl.ds(0, tm), (da_ref, hv_ref, hg_ref)),
                                            (pl.ds(tm, HALO), (dan_ref, hvn_ref, hgn_ref))):
            @pl.when(is_value)
            def _():
                d_scr[rows, :] = a_ref[...].astype(F32) * _gelu(g_ref[...].astype(F32))

            @pl.when(jnp.logical_not(is_value))
            def _():
                d_scr[rows, :] = (a_ref[...].astype(F32) * v_ref[...].astype(F32)
                                  * _gelu_grad(g_ref[...].astype(F32)))

        has_prev = (i > 0).astype(F32)
        has_next = (i < s // tm - 1).astype(F32)
        w0, w1, w2 = w_ref[0:1, :], w_ref[1:2, :], w_ref[2:3, :]
        d = d_scr[0:tm, :]
        dx_ref[...] = (w2 * d + w1 * pltpu.roll(d, tm - 1, axis=0) + w0 * pltpu.roll(d, tm - 2, axis=0)).astype(BF16)
        row = lax.broadcasted_iota(jnp.int32, (EDGE, CONV_TC), 0)
        nxt = d_scr[tm:tm + HALO, :]
        n0 = _row_of(nxt, 0) * has_next
        n1 = _row_of(nxt, 1) * has_next
        de = d_scr[tm - EDGE:tm, :]
        d1 = jnp.where(row == EDGE - 1, n0, pltpu.roll(de, EDGE - 1, axis=0))
        d2 = jnp.where(row == EDGE - 2, n0, jnp.where(row == EDGE - 1, n1, pltpu.roll(de, EDGE - 2, axis=0)))
        dx_ref[tm - EDGE:tm, :] = (w2 * de + w1 * d1 + w0 * d2).astype(BF16)
        x = x_ref[...].astype(F32)
        tail = t_ref[...].astype(F32)
        r7 = _row_of(tail, EDGE - 1) * has_prev
        r6 = _row_of(tail, EDGE - 2) * has_prev
        last = x_ref[tm - EDGE:tm, :].astype(F32)
        l7, l6 = _row_of(last, EDGE - 1), _row_of(last, EDGE - 2)
        head = d_scr[0:8, :]
        d0, d1h = _row_of(head, 0), _row_of(head, 1)
        dw_ref[0:1, :] += (jnp.sum(d * pltpu.roll(x, 2, axis=0), axis=0, keepdims=True)
                           + d0 * (r6 - l6) + d1h * (r7 - l7))
        dw_ref[1:2, :] += jnp.sum(d * pltpu.roll(x, 1, axis=0), axis=0, keepdims=True) + d0 * (r7 - l7)
        dw_ref[2:3, :] += jnp.sum(d * x, axis=0, keepdims=True)
        db_ref[...] += jnp.sum(d, axis=0, keepdims=True)

    a_spec = pl.BlockSpec((tm, CONV_TC), lambda j, i: (i, j % N_CT))
    an_spec = pl.BlockSpec((HALO, CONV_TC),
                           lambda j, i: (jnp.minimum((i + 1) * (tm // HALO), s // HALO - 1), j % N_CT))
    x_spec = pl.BlockSpec((tm, CONV_TC), lambda j, i: (i, j))
    t_spec = pl.BlockSpec((EDGE, CONV_TC), lambda j, i: (jnp.maximum(i * (tm // EDGE) - 1, 0), j))
    w_spec = pl.BlockSpec((3, CONV_TC), lambda j, i: (0, j))
    db_spec = pl.BlockSpec((1, CONV_TC), lambda j, i: (0, j))
    return pl.pallas_call(
        body, name=name, grid=(2 * N_CT, s // tm),
        in_specs=[a_spec, an_spec, a_spec, an_spec, a_spec, an_spec, x_spec, t_spec, w_spec],
        out_specs=[x_spec, w_spec, db_spec],
        out_shape=[_sds((s, 2 * D_FF), BF16), _sds((3, 2 * D_FF)), _sds((1, 2 * D_FF))],
        scratch_shapes=[pltpu.VMEM((tm + HALO, CONV_TC), F32)],
        compiler_params=_params(("parallel", "arbitrary")),
    )(dact, dact, hv, hv, hg, hg, hu, hu, cw)


def _ple_fwd(h, gp, pp, next_gain, name, tm=512):
    s, d = h.shape
    with_norm = next_gain is not None

    def body(*refs):
        h_ref, g_ref, p_ref = refs[:3]
        out = h_ref[...] + _sigmoid(g_ref[...].astype(F32)) * p_ref[...].astype(F32)
        if with_norm:
            n_ref, o_ref, a_ref = refs[3:]
            scale = lax.rsqrt(jnp.mean(out * out, axis=-1, keepdims=True) + EPS)
            a_ref[...] = (out * scale * n_ref[...]).astype(BF16)
        else:
            o_ref, = refs[3:]
        o_ref[...] = out

    if not with_norm:
        return _rowcall(body, name, s, tm, [_rb(tm, d)] * 3, (h, gp, pp), _rb(tm, d), _sds((s, d))), None
    return _rowcall(body, name, s, tm, [_rb(tm, d)] * 3 + [_fb((1, d))], (h, gp, pp, next_gain.reshape(1, d)),
                    [_rb(tm, d)] * 2, [_sds((s, d)), _sds((s, d), BF16)])


def _ple_bwd(dh, gp, pp, name, tm=512):
    s, d = dh.shape

    def body(d_ref, g_ref, p_ref, dp_ref, dg_ref):
        sg = _sigmoid(g_ref[...].astype(F32))
        dv = d_ref[...]
        dp_ref[...] = (dv * sg).astype(BF16)
        dg_ref[...] = (dv * p_ref[...].astype(F32) * sg * (1.0 - sg)).astype(BF16)

    return _rowcall(body, name, s, tm, [_rb(tm, d)] * 3, (dh, gp, pp), [_rb(tm, d)] * 2,
                    [_sds((s, d), BF16)] * 2)


SCALE = HEAD_DIM ** -0.5
ATT_ROWS = 2048


def _att_geom(s, dil):
    w = min(ATT_ROWS, s)
    p = BLK * dil
    assert w % p == 0 and s % w == 0
    return w, p, w // p


def _rows(start, dil):
    return pl.ds(start, BLK, stride=dil) if dil > 1 else pl.ds(start, BLK)


def _head_masks():
    lane = lax.broadcasted_iota(jnp.int32, (1, BLK), 1)
    return [lane < HEAD_DIM, lane >= HEAD_DIM]


def _band():
    rel = np.arange(BLK)[:, None] + BLK - np.arange(2 * BLK)[None, :]
    return (rel >= 0) & (rel <= BLK)


def _zcur(w):
    return lambda off: pl.BlockSpec((w, BLK), lambda hp, i: (i, off + hp))


def _zprev(p, nb):
    return lambda off: pl.BlockSpec((p, BLK), lambda hp, i: (jnp.maximum(i * nb - 1, 0), off + hp))


def _scur(w):
    return pl.BlockSpec((w, BLK), lambda hp, i: (i, hp))


def _pair_rows(t, masks):
    return jnp.concatenate([jnp.where(masks[0], t, 0.0), jnp.where(masks[1], t, 0.0)], axis=0).astype(BF16)


def _pair_bias_bwd(bias):
    return bias.reshape(4, 2, BLK, 2, BLK).transpose(0, 3, 2, 1, 4).reshape(4, 2, BLK, 2 * BLK)


def _unpair_bias_bwd(db):
    return db.reshape(4, 2, BLK, 2, BLK).transpose(0, 3, 2, 1, 4).reshape(N_HEADS, BLK, 2 * BLK)


def _attn_fwd(z, bias, state, dil, first, last, name):
    s = z.shape[0]
    w, p, nb = _att_geom(s, dil)

    def body(*refs):
        q_ref, kp_ref, kc_ref, vp_ref, vc_ref, b_ref = refs[:6]
        rest = refs[6:]
        if not first:
            m_ref, l_ref, a_ref = rest[:3]
            rest = rest[3:]
        i = pl.program_id(1)
        masks = _head_masks()
        own_block = lax.broadcasted_iota(jnp.int32, (1, 2 * BLK), 1) >= BLK
        for r in range(dil):
            for b in range(nb):
                rows = _rows(r + p * b, dil)
                prev_rows = _rows(r + p * (b - 1), dil) if b > 0 else _rows(r, dil)
                kprev, vprev = (kc_ref, vc_ref) if b > 0 else (kp_ref, vp_ref)
                q = q_ref[rows, :] * SCALE
                k = jnp.concatenate([kprev[prev_rows, :], kc_ref[rows, :]], axis=0).astype(BF16)
                v = jnp.concatenate([vprev[prev_rows, :], vc_ref[rows, :]], axis=0).astype(BF16)
                mb = lb = ob = None
                for hh, mh in enumerate(masks):
                    qh = jnp.where(mh, q, 0.0).astype(BF16)
                    sc = lax.dot_general(qh, k, _DIMS["nt"], preferred_element_type=F32) + b_ref[hh]
                    if b == 0:
                        sc = jnp.where(own_block | (i > 0), sc, NEG_INF)
                    mx = jnp.max(sc, axis=1, keepdims=True)
                    e = jnp.exp(sc - mx)
                    den = jnp.sum(e, axis=1, keepdims=True)
                    o = jnp.dot(e.astype(BF16), v, preferred_element_type=F32)
                    if hh == 0:
                        mb = jnp.broadcast_to(mx, (BLK, BLK))
                        lb = jnp.broadcast_to(den, (BLK, BLK))
                        ob = o
                    else:
                        mb = jnp.where(mh, mx, mb)
                        lb = jnp.where(mh, den, lb)
                        ob = jnp.where(mh, o, ob)
                if first:
                    m_new, l_new, a_new = mb, lb, ob
                else:
                    m_old = m_ref[rows, :]
                    m_new = jnp.maximum(m_old, mb)
                    al = jnp.exp(m_old - m_new)
                    be = jnp.exp(mb - m_new)
                    l_new = al * l_ref[rows, :] + be * lb
                    a_new = al * a_ref[rows, :] + be * ob
                if last:
                    y_ref, lse_ref = rest
                    y_ref[rows, :] = a_new / l_new
                    lse_ref[rows, :] = m_new + jnp.log(l_new)
                else:
                    mo_ref, lo_ref, ao_ref = rest
                    mo_ref[rows, :] = m_new
                    lo_ref[rows, :] = l_new
                    ao_ref[rows, :] = a_new

    cur, prv = _zcur(w), _zprev(p, nb)
    b_spec = pl.BlockSpec((2, BLK, 2 * BLK), lambda hp, i: (hp, 0, 0))
    in_specs = [cur(0), prv(4), cur(4), prv(8), cur(8), b_spec]
    args = [z, z, z, z, z, bias]
    if not first:
        in_specs += [_scur(w)] * 3
        args += list(state)
    n_out = 2 if last else 3
    return pl.pallas_call(
        body, name=name, grid=(4, s // w), in_specs=in_specs, out_specs=[_scur(w)] * n_out,
        out_shape=[_sds((s, ATTN_W))] * n_out,
        compiler_params=_params(("parallel", "parallel")),
    )(*args)


def _row_stats(mh, dy, y, lse):
    delta = jnp.sum(jnp.where(mh, dy * y, 0.0), axis=1, keepdims=True)
    lse_h = jnp.max(jnp.where(mh, lse, NEG_INF), axis=1, keepdims=True)
    return delta, lse_h


def _attn_bwd(z, bias, dy, y, lse, prev, dil, name):
    s = z.shape[0]
    w, p, nb = _att_geom(s, dil)
    n_steps = s // w
    first = prev is None

    def body(*refs):
        q_ref, kp_ref, kc_ref, vp_ref, vc_ref, b_ref, dy_ref, y_ref, lse_ref = refs[:9]
        rest = refs[9:]
        if not first:
            dqp_ref, dkp_ref, dvp_ref = rest[:3]
            rest = rest[3:]
        dq_ref, dk_ref, dv_ref, dkx_ref, dvx_ref, db_ref = rest
        i = pl.program_id(1)

        @pl.when(i == 0)
        def _():
            db_ref[...] = jnp.zeros_like(db_ref)

        masks = _head_masks()
        first_head = lax.broadcasted_iota(jnp.int32, (1, 2 * BLK), 1) < BLK

        def flush(rows, dk, dv):
            if not first:
                dk = dk + dkp_ref[rows, :]
                dv = dv + dvp_ref[rows, :]
            dk_ref[rows, :] = dk
            dv_ref[rows, :] = dv

        for r in range(dil):
            carry = None
            for b in range(nb):
                rows = _rows(r + p * b, dil)
                prev_rows = _rows(r + p * (b - 1), dil) if b > 0 else _rows(r, dil)
                kprev, vprev = (kc_ref, vc_ref) if b > 0 else (kp_ref, vp_ref)
                keys = [(_pair_rows(kprev[prev_rows, :], masks), _pair_rows(vprev[prev_rows, :], masks)),
                        (_pair_rows(kc_ref[rows, :], masks), _pair_rows(vc_ref[rows, :], masks))]
                q = (q_ref[rows, :] * SCALE).astype(BF16)
                dy_v = dy_ref[rows, :]
                dyb = dy_v.astype(BF16)
                stats = [_row_stats(mh, dy_v, y_ref[rows, :], lse_ref[rows, :]) for mh in masks]
                delta = jnp.where(first_head, stats[0][0], stats[1][0])
                lse_h = jnp.where(first_head, stats[0][1], stats[1][1])
                dq = jnp.zeros((BLK, BLK), F32)
                dk, dv = [], []
                for half in range(2):
                    kh, vh = keys[half]
                    sc = lax.dot_general(q, kh, _DIMS["nt"], preferred_element_type=F32) + b_ref[half]
                    pr = jnp.exp(sc - lse_h)
                    if b == 0 and half == 0:
                        pr = pr * (i > 0).astype(F32)
                    dp = lax.dot_general(dyb, vh, _DIMS["nt"], preferred_element_type=F32)
                    ds = pr * (dp - delta)
                    db_ref[half] += ds
                    dsb = ds.astype(BF16)
                    dq = dq + jnp.dot(dsb, kh, preferred_element_type=F32)
                    dk2 = lax.dot_general(dsb, q, _DIMS["tn"], preferred_element_type=F32)
                    dv2 = lax.dot_general(pr.astype(BF16), dyb, _DIMS["tn"], preferred_element_type=F32)
                    dk.append(jnp.where(masks[0], dk2[:BLK], dk2[BLK:]))
                    dv.append(jnp.where(masks[0], dv2[:BLK], dv2[BLK:]))
                dq = dq * SCALE
                if not first:
                    dq = dq + dqp_ref[rows, :]
                dq_ref[rows, :] = dq
                if b > 0:
                    flush(prev_rows, carry[0] + dk[0], carry[1] + dv[0])
                else:
                    dkx_ref[prev_rows, :] = dk[0]
                    dvx_ref[prev_rows, :] = dv[0]
                carry = (dk[1], dv[1])
            flush(_rows(r + p * (nb - 1), dil), *carry)

    cur, prv = _zcur(w), _zprev(p, nb)
    b_spec = pl.BlockSpec((None, 2, BLK, 2 * BLK), lambda hp, i: (hp, 0, 0, 0))
    in_specs = [cur(0), prv(4), cur(4), prv(8), cur(8), b_spec] + [_scur(w)] * 3
    args = [z, z, z, z, z, bias, dy, y, lse]
    if not first:
        in_specs += [_scur(w)] * 3
        args += list(prev)
    x_spec = pl.BlockSpec((p, BLK), lambda hp, i: (i, hp))
    *outs, db = pl.pallas_call(
        body, name=name, grid=(4, n_steps), in_specs=in_specs,
        out_specs=[_scur(w)] * 3 + [x_spec] * 2 + [b_spec],
        out_shape=[_sds((s, ATTN_W))] * 3 + [_sds((n_steps * p, ATTN_W))] * 2 + [_sds((4, 2, BLK, 2 * BLK))],
        compiler_params=_params(("parallel", "arbitrary")),
    )(*args)
    return (*outs, _unpair_bias_bwd(db))


ASM_ROWS = 512


def _assemble_dz(dq, dk, dv, extras, dzs, du, name):
    s = dq.shape[0]
    w = min(ATT_ROWS, s)
    n_steps = s // w
    per_step = w // ASM_ROWS
    assert w % ASM_ROWS == 0

    def body(*refs):
        dq_ref, dk_ref, dv_ref, dzs_ref, du_ref = refs[:5]
        x_refs = refs[5:5 + 2 * len(extras)]
        o_ref, acc_ref = refs[-2:]
        j = pl.program_id(0)
        step = j // per_step
        has_next = (step < n_steps - 1).astype(F32)
        last_of_step = ((j + 1) % per_step == 0).astype(F32)
        o_ref[:, 0:ATTN_W] = dq_ref[...].astype(BF16)
        o_ref[:, 3 * ATTN_W:3 * ATTN_W + 2 * SGU_W] = dzs_ref[...].astype(BF16)
        o_ref[:, 3 * ATTN_W + 2 * SGU_W:IN_W] = du_ref[...].astype(BF16)
        for part, (base_ref, col) in enumerate(((dk_ref, ATTN_W), (dv_ref, 2 * ATTN_W))):
            acc_ref[...] = base_ref[...]
            for n, (_, dil) in enumerate(BRANCHES):
                rows = min(BLK * dil, ASM_ROWS)
                scale = has_next if BLK * dil >= w else has_next * last_of_step
                acc_ref[ASM_ROWS - rows:, :] += x_refs[2 * n + part][...] * scale
            o_ref[:, col:col + ATTN_W] = acc_ref[...].astype(BF16)

    def x_spec(dil):
        p = BLK * dil
        rows = min(p, ASM_ROWS)
        blocks_per_step = p // rows
        total = n_steps * blocks_per_step

        def idx(j):
            step = j // per_step
            within = (j % per_step) - (per_step - blocks_per_step)
            return (jnp.clip((step + 1) * blocks_per_step + jnp.maximum(within, 0), 0, total - 1), 0)

        return pl.BlockSpec((rows, ATTN_W), idx)

    in_specs = [_rb(ASM_ROWS, ATTN_W)] * 3 + [_rb(ASM_ROWS, 2 * SGU_W), _rb(ASM_ROWS, SSM_W)]
    args = [dq, dk, dv, dzs, du]
    for (dkx, dvx), (_, dil) in zip(extras, BRANCHES):
        in_specs += [x_spec(dil)] * 2
        args += [dkx, dvx]
    return pl.pallas_call(
        body, name=name, grid=(s // ASM_ROWS,), in_specs=in_specs, out_specs=_rb(ASM_ROWS, IN_W),
        out_shape=_sds((s, IN_W), BF16), scratch_shapes=[pltpu.VMEM((ASM_ROWS, ATTN_W), F32)],
        compiler_params=_params(("parallel",)),
    )(*args)


def _t5_bucket(dist):
    max_exact = N_BUCKETS // 2
    d = np.maximum(dist, 0)
    large = max_exact + (np.log(np.maximum(d, 1) / max_exact) / np.log(REL_MAX / max_exact)
                         * (N_BUCKETS - max_exact)).astype(np.int32)
    large = np.minimum(large, N_BUCKETS - 1)
    return np.where(d < max_exact, d, large).astype(np.int32)


def _bias_tables(rel_bias):
    period = 3 * BLK
    tabs = []
    for _, dil in BRANCHES:
        onehot = np.zeros((period, N_BUCKETS), np.float32)
        d = np.arange(BLK + 1)
        onehot[d, _t5_bucket((BLK - d) * dil)] = 1.0
        f = jnp.dot(jnp.asarray(onehot), rel_bias, precision=lax.Precision.HIGHEST)
        flat = jnp.tile(f.T, (1, BLK))[:, :BLK * (period - 1)]
        tab = flat.reshape(N_HEADS, BLK, period - 1)[:, :, :2 * BLK]
        tabs.append(jnp.where(_band()[None], tab, NEG_INF))
    return tabs


def _bucket_onehot():
    maps = []
    q = np.arange(BLK)[:, None]
    k = np.arange(2 * BLK)[None, :]
    rel = q + BLK - k
    for _, dil in BRANCHES:
        maps.append(np.where((rel >= 0) & (rel <= BLK), _t5_bucket(rel * dil), -1).reshape(-1))
    bmap = jnp.asarray(np.concatenate(maps).astype(np.int32))
    return (bmap[:, None] == jnp.arange(128, dtype=jnp.int32)[None, :]).astype(BF16)


def _block_diag(t):
    g, n, c = t.shape
    eye = jnp.eye(g, dtype=t.dtype)
    return (t[:, :, None, :] * eye[:, None, :, None]).reshape(g * n, g * c)


def _ssm_prep(a_re, a_im, log_dt, b_re, b_im, c_re, c_im):
    lam = lax.complex(a_re, a_im)
    dt = jnp.exp(log_dt)[:, None]
    a_bar = jnp.exp(lam * dt)
    b_bar = ((a_bar - 1.0) / lam)[:, :, None] * lax.complex(b_re, b_im)
    bdt = jnp.concatenate([_block_diag(jnp.real(b_bar)), _block_diag(jnp.imag(b_bar))], axis=0)
    cd = jnp.concatenate([_block_diag(jnp.transpose(c_re, (0, 2, 1))),
                          _block_diag(-jnp.transpose(c_im, (0, 2, 1)))], axis=0)
    return jnp.real(a_bar).reshape(-1), jnp.imag(a_bar).reshape(-1), bdt, cd


def _powers(ar, ai):
    pr, pi = ar[:, None], ai[:, None]
    k = 1
    while k < 8:
        lr, li = pr[:, -1:], pi[:, -1:]
        pr, pi = (jnp.concatenate([pr, pr * lr - pi * li], axis=1),
                  jnp.concatenate([pi, pr * li + pi * lr], axis=1))
        k *= 2
    return pr, pi


def _sgu_bias_expand(b):
    return jnp.repeat(b.T, 64, axis=1)


def _layer_fwd(i, h, a1, p_i, big, small, bias_tabs, next_gain):
    nm = "l%d_" % i
    sv = {"h": h}
    if a1 is None:
        a1 = _rms_fwd(h, small["norm_attn_g"][i], nm + "rms_attn")
    z = _mm(a1, big["w_in"], "nt", nm + "in_proj")
    st = None
    for b, (_, dil) in enumerate(BRANCHES):
        st = _attn_fwd(z, bias_tabs[b][0], st, dil, b == 0, b == len(BRANCHES) - 1, nm + "attn_fwd%d" % b)
    y_attn, lse = st
    bexp = _sgu_bias_expand(small["sgu_b"][i])
    y_sgu = _sgu_fwd(z, small["sgu_ln_g"][i], small["sgu_ln_b"][i], small["sgu_w"][i], bexp, nm + "sgu_fwd")
    ar, ai, bdt, cd = _ssm_prep(*[small[k][i] for k in ("ssm_a_re", "ssm_a_im", "ssm_log_dt", "ssm_b_re",
                                                         "ssm_b_im", "ssm_c_re", "ssm_c_im")])
    xr, xi, yc = _ssm_fwd(z, bdt.astype(BF16), cd.astype(BF16), _scan_tables(*_powers(ar, ai), False),
                          nm + "ssm_core")
    y_ssm = _ssm_post_fwd(yc, z, small["ssm_d"][i], big["ssm_glu_w"], small["ssm_glu_b"][i], nm + "ssm_post")
    mix = _mix_fwd(y_attn, y_sgu, y_ssm, small["branch_norm_g"][i], nm + "mix")
    h2, a2 = _mm(mix, big["w_out"], "nn", nm + "out_proj", add=h, norm_gain=small["norm_ffn_g"][i])
    hu = _mm(a2, big["ffn_w_up"], "nt", nm + "ffn_up", out_dtype=BF16)
    hv, hg, act = _conv_fwd(hu, big["ffn_conv_w"], small["ffn_conv_b"][i], nm + "ffn_conv")
    h3, a3 = _mm(act, big["ffn_w_down"], "nn", nm + "ffn_down", add=h2, norm_gain=small["norm_ple_g"][i])
    gp = _mm(a3, big["ple_w_gate"], "nn", nm + "ple_gate", out_dtype=BF16)
    pp = _mm(p_i, big["ple_w_proj"], "nt", nm + "ple_proj", out_dtype=BF16)
    h4, a_next = _ple_fwd(h3, gp, pp, next_gain, nm + "ple_add")
    sv.update(a1=a1, z=z, y_attn=y_attn, lse=lse, y_sgu=y_sgu, y_ssm=y_ssm, yc=yc, xr=xr, xi=xi, mix=mix, h2=h2,
              a2=a2, hu=hu, hv=hv, hg=hg, act=act, h3=h3, a3=a3, gp=gp, pp=pp)
    return h4, a_next, sv


def _layer_bwd(i, dh4, sv, p_i, big, small, bias_tabs, ffn_done=None):
    nm = "l%d_" % i
    g = {}
    dpp, dgp = _ple_bwd(dh4, sv["gp"], sv["pp"], nm + "ple_bwd")
    g["ple_w_proj"] = _mm(dpp, p_i, "tn", nm + "d_ple_proj", out_dtype=BF16)
    g["ple_w_gate"] = _mm(sv["a3"], dgp, "tn", nm + "d_ple_gate", out_dtype=BF16)
    da3 = _mm(dgp, big["ple_w_gate"], "nt", nm + "ple_gate_t")
    dh3, g["norm_ple_g"] = _rms_bwd(da3, sv["h3"], small["norm_ple_g"][i], dh4, nm + "rms_ple_bwd")
    g["ffn_w_down"] = _mm(sv["act"], dh3, "tn", nm + "d_ffn_down", out_dtype=BF16)
    dact = _mm(dh3, big["ffn_w_down"], "nt", nm + "ffn_down_t", out_dtype=BF16)
    dhu, g["ffn_conv_w"], dcb = _conv_bwd(dact, sv["hv"], sv["hg"], sv["hu"], big["ffn_conv_w"],
                                          nm + "ffn_conv_bwd")
    g["ffn_conv_b"] = dcb.reshape(2 * D_FF)
    g["ffn_w_up"] = _mm(dhu, sv["a2"], "tn", nm + "d_ffn_up", out_dtype=BF16)
    da2 = _mm(dhu, big["ffn_w_up"], "nn", nm + "ffn_up_t")
    dh2, g["norm_ffn_g"] = _rms_bwd(da2, sv["h2"], small["norm_ffn_g"][i], dh3, nm + "rms_ffn_bwd")
    if ffn_done is not None:
        small = ffn_done(g, small)
    g["w_out"] = _mm(sv["mix"], dh2, "tn", nm + "d_out_proj", out_dtype=BF16)
    dmix = _mm(dh2, big["w_out"], "nt", nm + "out_proj_t")
    dya, dysg, dyss, g["branch_norm_g"] = _mix_bwd(dmix, sv["y_attn"], sv["y_sgu"], sv["y_ssm"],
                                                   small["branch_norm_g"][i], nm + "mix_bwd")
    ssm_keys = ("ssm_a_re", "ssm_a_im", "ssm_log_dt", "ssm_b_re", "ssm_b_im", "ssm_c_re", "ssm_c_im")
    (ar, ai, bdt, cd), prep_vjp = jax.vjp(_ssm_prep, *[small[k][i] for k in ssm_keys])
    dy1, dgl, y2, dud, g["ssm_d"], g["ssm_glu_b"] = _ssm_post_bwd(
        dyss, sv["yc"], sv["z"], small["ssm_d"][i], big["ssm_glu_w"], small["ssm_glu_b"][i], nm + "ssm_post_bwd")
    g["ssm_glu_w"] = _mm(y2, dgl, "tn", nm + "d_ssm_glu", out_dtype=BF16)
    du, dbdt, dcd, dar, dai = _ssm_bwd(dy1, dud, sv["z"], sv["xr"], sv["xi"], bdt.astype(BF16), cd.astype(BF16),
                                       _scan_tables(*_powers(ar, ai), True), nm + "ssm_core_bwd")
    for k, val in zip(ssm_keys, prep_vjp((dar, dai, dbdt, dcd))):
        g[k] = val
    bexp, bexp_vjp = jax.vjp(_sgu_bias_expand, small["sgu_b"][i])
    dzs, g["sgu_w"], dbexp, g["sgu_ln_g"], g["sgu_ln_b"] = _sgu_bwd(
        sv["z"], dysg, small["sgu_ln_g"][i], small["sgu_ln_b"][i], small["sgu_w"][i], bexp, nm + "sgu_bwd")
    g["sgu_b"] = bexp_vjp(dbexp)[0]
    prev = None
    dbs, extras = [], []
    for b, (_, dil) in enumerate(BRANCHES):
        dq, dk, dv, dkx, dvx, db = _attn_bwd(sv["z"], bias_tabs[b][1], dya, sv["y_attn"], sv["lse"], prev, dil,
                                             nm + "attn_bwd%d" % b)
        prev = (dq, dk, dv)
        extras.append((dkx, dvx))
        dbs.append(db.reshape(N_HEADS, BLK * 2 * BLK))
    dz = _assemble_dz(dq, dk, dv, extras, dzs, du, nm + "assemble_dz")
    g["w_in"] = _mm(dz, sv["a1"], "tn", nm + "d_in_proj", out_dtype=BF16)
    da1 = _mm(dz, big["w_in"], "nn", nm + "in_proj_t")
    dh, g["norm_attn_g"] = _rms_bwd(da1, sv["h"], small["norm_attn_g"][i], dh2, nm + "rms_attn_bwd")
    return dh, g, jnp.concatenate(dbs, axis=1)


def _local_step(x, p, target, layer_weights, small, layer_done=None):
    depth = p.shape[0]
    bias_tabs = [(t, _pair_bias_bwd(t)) for t in _bias_tables(small["rel_bias"])]
    h, a1 = x, None
    saved, bigs = [], []
    for i in range(depth):
        bigs.append(layer_weights(i, h))
        next_gain = small["norm_attn_g"][i + 1] if i + 1 < depth else None
        h, a1, sv = _layer_fwd(i, h, a1, p[i], bigs[i], small, bias_tabs, next_gain)
        saved.append(sv)
    dh, loss, g_final = _loss_head(h, target, small["final_norm_g"], "loss_head")
    layer_grads = [None] * depth
    dbias = [None] * depth
    for i in reversed(range(depth)):
        ffn_done = None if layer_done is None else (lambda g, sm, i=i: layer_done(i, "ffn", g, sm))
        dh, layer_grads[i], dbias[i] = _layer_bwd(i, dh, saved[i], p[i], bigs[i], small, bias_tabs, ffn_done)
        if layer_done is not None:
            small = layer_done(i, "all", layer_grads[i], small)
    big_grads = [{k: lg.pop(k) for k in COMM_NAMES} for lg in layer_grads]
    grads = {k: jnp.stack([layer_grads[i][k] for i in range(depth)]) for k in layer_grads[0]}
    grads["final_norm_g"] = g_final
    g_rb = _mm(sum(dbias[1:], dbias[0]), _bucket_onehot(), "nn", "d_rel_bias", tk=2048)
    grads["rel_bias"] = g_rb[:, :N_BUCKETS].T
    return loss, dh, big_grads, grads


_ANY = pl.BlockSpec(memory_space=pl.ANY)
MESH_IDS = pl.DeviceIdType.MESH


def _slot(ref, axis, j):
    return ref.at[(slice(None),) * axis + (j,)]


def _all_gather(blocks, axis, name):
    nt = len(blocks)

    def body(*refs):
        x_refs, o_refs = refs[:nt], refs[nt:2 * nt]
        send_sems, recv_sems, local_sems = refs[2 * nt:]
        x, y, c = lax.axis_index("x"), lax.axis_index("y"), lax.axis_index("c")
        me, sibling = (x, y, c), (x, y, 1 - c)
        chips = [(1 - x, y), (x, 1 - y), (1 - x, 1 - y)]

        def slot(t, px, py, pc):
            return _slot(o_refs[t], axis, 4 * px + 2 * py + pc)

        def copy(t, k, blk, to, src=None):
            return pltpu.make_async_remote_copy(
                src_ref=slot(t, *blk) if src is None else src, dst_ref=slot(t, *blk),
                send_sem=send_sems.at[7 * t + k], recv_sem=recv_sems.at[7 * t + k],
                device_id=to, device_id_type=MESH_IDS)

        mine = [pltpu.make_async_copy(x_refs[t], slot(t, *me), local_sems.at[t]) for t in range(nt)]
        for cp in mine:
            cp.start()
        first = []
        for t in range(nt):
            first.append(copy(t, 0, me, sibling, src=x_refs[t]))
            first += [copy(t, 1 + j, me, (*chip, c), src=x_refs[t]) for j, chip in enumerate(chips)]
        for cp in first:
            cp.start()
        passed = []
        for t in range(nt):
            for j, chip in enumerate(chips):
                copy(t, 1 + j, (*chip, c), me).wait_recv()
                passed.append(copy(t, 4 + j, (*chip, c), sibling))
                passed[-1].start()
        for t in range(nt):
            copy(t, 0, sibling, me).wait_recv()
            for j, chip in enumerate(chips):
                copy(t, 4 + j, (*chip, 1 - c), me).wait_recv()
        for cp in first + passed:
            cp.wait_send()
        for cp in mine:
            cp.wait()

    out_shape = [jax.ShapeDtypeStruct(b.shape[:axis] + (N_DEV,) + b.shape[axis:], b.dtype) for b in blocks]
    return pl.pallas_call(
        body, name=name, out_shape=out_shape, in_specs=[_ANY] * nt, out_specs=[_ANY] * nt,
        scratch_shapes=[pltpu.SemaphoreType.DMA((7 * nt,)), pltpu.SemaphoreType.DMA((7 * nt,)),
                        pltpu.SemaphoreType.DMA((nt,))],
    )(*blocks)


def _peer(k):
    x, y, c = lax.axis_index("x"), lax.axis_index("y"), lax.axis_index("c")
    px = 1 - x if k & 4 else x
    py = 1 - y if k & 2 else y
    pc = 1 - c if k & 1 else c
    return (px, py, pc), 4 * px + 2 * py + pc


def _all_to_all(blocks, name):
    nt = len(blocks)

    def body(*refs):
        x_refs, o_refs = refs[:nt], refs[nt:2 * nt]
        send_sems, recv_sems, local_sems = refs[2 * nt:]
        _, me = _peer(0)
        mine = [pltpu.make_async_copy(x_refs[t].at[me], o_refs[t].at[me], local_sems.at[t]) for t in range(nt)]
        for cp in mine:
            cp.start()
        copies = []
        for k in range(1, N_DEV):
            peer, idx = _peer(k)
            for t in range(nt):
                cp = pltpu.make_async_remote_copy(
                    src_ref=x_refs[t].at[idx], dst_ref=o_refs[t].at[me],
                    send_sem=send_sems.at[7 * t + k - 1], recv_sem=recv_sems.at[7 * t + k - 1],
                    device_id=peer, device_id_type=MESH_IDS)
                cp.start()
                copies.append(cp)
        for cp in copies:
            cp.wait()
        for cp in mine:
            cp.wait()

    return pl.pallas_call(
        body, name=name, out_shape=[jax.ShapeDtypeStruct(b.shape, b.dtype) for b in blocks],
        in_specs=[_ANY] * nt, out_specs=[_ANY] * nt,
        scratch_shapes=[pltpu.SemaphoreType.DMA((7 * nt,)), pltpu.SemaphoreType.DMA((7 * nt,)),
                        pltpu.SemaphoreType.DMA((nt,))],
    )(*blocks)


_HBM = pl.BlockSpec(memory_space=pltpu.HBM)
_SEM = pl.BlockSpec(memory_space=pltpu.SEMAPHORE)
_EFFECT = pltpu.SideEffectType.DATAFLOW_SIDE_EFFECTING


def _split_copy(src_ref, land_ref, send_sems, recv_sems, t, k, gather):
    peer, idx = _peer(k)
    _, me = _peer(0)
    return pltpu.make_async_remote_copy(
        src_ref=src_ref if gather else src_ref.at[idx], dst_ref=land_ref.at[me],
        send_sem=send_sems.at[7 * t + k - 1], recv_sem=recv_sems.at[7 * t + k - 1],
        device_id=peer, device_id_type=MESH_IDS)


def _exchange_start(srcs, lands, gather, name):
    nt = len(srcs)

    def body(*refs):
        src_refs, land_refs = refs[:nt], refs[nt:2 * nt]
        send_sems, recv_sems = refs[2 * nt:2 * nt + 2]
        token = refs[-1]
        for k in range(1, N_DEV):
            for t in range(nt):
                _split_copy(src_refs[t], land_refs[t], send_sems, recv_sems, t, k, gather).start()
        token[...] = jnp.zeros_like(token)

    hbm = lambda a: pltpu.HBM(a.shape, a.dtype)
    outs = pl.pallas_call(
        body, name=name,
        out_shape=(pltpu.SemaphoreType.DMA((7 * nt,)), pltpu.SemaphoreType.DMA((7 * nt,)),
                   *[hbm(a) for a in srcs], *[hbm(a) for a in lands], jax.ShapeDtypeStruct((8, 128), F32)),
        in_specs=[_HBM] * (2 * nt),
        out_specs=(_SEM, _SEM, *[_HBM] * (2 * nt), pl.BlockSpec(memory_space=pltpu.VMEM)),
        input_output_aliases={j: 2 + j for j in range(2 * nt)},
        compiler_params=pltpu.CompilerParams(has_side_effects=_EFFECT),
    )(*[pltpu.with_memory_space_constraint(a, pltpu.HBM) for a in list(srcs) + list(lands)])
    return outs[0], outs[1], outs[2:2 + nt], outs[2 + nt:2 + 2 * nt], outs[-1]


def _exchange_wait(send_sems, recv_sems, srcs, lands, after, gather, name):
    nt = len(srcs)

    def body(*refs):
        src_refs, land_refs = refs[:nt], refs[nt:2 * nt]
        send_sems, recv_sems = refs[2 * nt:2 * nt + 2]
        for k in range(1, N_DEV):
            _, idx = _peer(k)
            for t in range(nt):
                _split_copy(src_refs[t], land_refs[t], send_sems, recv_sems, t, k, gather).wait_send()
                arrival = pltpu.make_async_remote_copy(
                    src_ref=land_refs[t].at[idx], dst_ref=land_refs[t].at[idx],
                    send_sem=send_sems.at[7 * t + k - 1], recv_sem=recv_sems.at[7 * t + k - 1],
                    device_id=_peer(k)[0], device_id_type=MESH_IDS)
                arrival.wait_recv()

    hbm = lambda a: pltpu.HBM(a.shape, a.dtype)
    outs = pl.pallas_call(
        body, name=name, out_shape=tuple(hbm(a) for a in list(srcs) + list(lands)),
        in_specs=[_HBM] * (2 * nt) + [_SEM, _SEM, _ANY], out_specs=tuple([_HBM] * (2 * nt)),
        input_output_aliases={j: j for j in range(2 * nt)},
        compiler_params=pltpu.CompilerParams(has_side_effects=_EFFECT),
    )(*srcs, *lands, send_sems, recv_sems, after)
    return outs[nt:]


def _adamw(parts, w, m, v, name, tr):
    n_layers, r, c_ = w.shape
    assert len(parts) == n_layers

    def body(*refs):
        p_refs = refs[:n_layers]
        w_ref, m_ref, v_ref, g_ref, d_ref, mo_ref, vo_ref = refs[n_layers:]

        def update(p_ref):
            g = p_ref[0].astype(F32)
            for j in range(1, N_DEV):
                g = g + p_ref[j].astype(F32)
            m2 = ADAM_B1 * m_ref[...] + (1.0 - ADAM_B1) * g
            v2 = ADAM_B2 * v_ref[...] + (1.0 - ADAM_B2) * (g * g)
            m_hat = m2 / (1.0 - ADAM_B1 ** ADAM_STEP)
            v_hat = v2 / (1.0 - ADAM_B2 ** ADAM_STEP)
            g_ref[...] = g
            d_ref[...] = -ADAM_LR * (m_hat / (jnp.sqrt(v_hat) + ADAM_EPS) + ADAM_WD * w_ref[...])
            mo_ref[...] = m2
            vo_ref[...] = v2

        for layer in range(n_layers):
            pl.when(pl.program_id(0) == layer)(lambda layer=layer: update(p_refs[layer]))

    spec = pl.BlockSpec((None, tr, c_), lambda l, i: (l, i, 0))
    p_spec = pl.BlockSpec((N_DEV, tr, c_), lambda l, i: (0, i, 0))
    return pl.pallas_call(
        body, name=name, grid=(n_layers, r // tr), in_specs=[p_spec] * n_layers + [spec] * 3,
        out_specs=[spec] * 4, out_shape=[_sds((n_layers, r, c_))] * 4,
        compiler_params=_params(("parallel", "parallel")),
    )(*parts, w, m, v)


def _pack_rows(n_elems, align):
    rows = -(-n_elems // PACK_COLS)
    return -(-rows // align) * align


def _pack(arrs, rows, dtype=F32):
    flat = jnp.concatenate([a.reshape(-1) for a in arrs]).astype(dtype)
    return jnp.pad(flat, (0, rows * PACK_COLS - flat.shape[0])).reshape(rows, PACK_COLS)


def _unpack(pack, shapes):
    flat = pack.reshape(-1)
    out, off = [], 0
    for shp in shapes:
        size = int(np.prod(shp))
        out.append(flat[off:off + size].reshape(shp))
        off += size
    return out


def _tile_rows(rows, target, align=16):
    best = align
    for t in range(align, target + 1, align):
        if rows % t == 0:
            best = t
    return best


COMM_NAMES = ("w_in", "ssm_glu_w", "w_out", "ffn_w_up", "ffn_w_down", "ple_w_gate", "ple_w_proj")
COMM_TRANSPOSED = ("w_in", "ffn_w_up", "ple_w_proj")
COMM_EARLY = ("ple_w_proj", "ple_w_gate", "ffn_w_down", "ffn_w_up")
COMM_LATE = ("w_in", "ssm_glu_w", "w_out")
SMALL_TILE_ROWS = 64
CONV_NAME = "ffn_conv_w"


def _to_comm(name, a):
    return jnp.swapaxes(a, 1, 2) if name in COMM_TRANSPOSED else a


def kernel(x, p, rel_bias, norm_attn_g, w_in, sgu_ln_g, sgu_ln_b, sgu_w, sgu_b, ssm_a_re, ssm_a_im, ssm_log_dt, ssm_b_re, ssm_b_im, ssm_c_re, ssm_c_im, ssm_d, ssm_glu_w, ssm_glu_b, branch_norm_g, w_out, norm_ffn_g, ffn_w_up, ffn_conv_w, ffn_conv_b, ffn_w_down, norm_ple_g, ple_w_gate, ple_w_proj, final_norm_g, loss_target, m_rel_bias, m_norm_attn_g, m_w_in, m_sgu_ln_g, m_sgu_ln_b, m_sgu_w, m_sgu_b, m_ssm_a_re, m_ssm_a_im, m_ssm_log_dt, m_ssm_b_re, m_ssm_b_im, m_ssm_c_re, m_ssm_c_im, m_ssm_d, m_ssm_glu_w, m_ssm_glu_b, m_branch_norm_g, m_w_out, m_norm_ffn_g, m_ffn_w_up, m_ffn_conv_w, m_ffn_conv_b, m_ffn_w_down, m_norm_ple_g, m_ple_w_gate, m_ple_w_proj, m_final_norm_g, v_rel_bias, v_norm_attn_g, v_w_in, v_sgu_ln_g, v_sgu_ln_b, v_sgu_w, v_sgu_b, v_ssm_a_re, v_ssm_a_im, v_ssm_log_dt, v_ssm_b_re, v_ssm_b_im, v_ssm_c_re, v_ssm_c_im, v_ssm_d, v_ssm_glu_w, v_ssm_glu_b, v_branch_norm_g, v_w_out, v_norm_ffn_g, v_ffn_w_up, v_ffn_conv_w, v_ffn_conv_b, v_ffn_w_down, v_norm_ple_g, v_ple_w_gate, v_ple_w_proj, v_final_norm_g):
    given = dict(locals())
    w = {n: given[n] for n in WEIGHT_NAMES}
    m = {n: given["m_" + n] for n in WEIGHT_NAMES}
    v = {n: given["v_" + n] for n in WEIGHT_NAMES}
    depth = p.shape[0]
    dev = 4 * lax.axis_index("x") + 2 * lax.axis_index("y") + lax.axis_index("c")

    wc = {n: _to_comm(n, w[n]) for n in COMM_NAMES}
    wb = {n: wc[n].astype(BF16) for n in COMM_NAMES}
    conv_local = [w[CONV_NAME], m[CONV_NAME], v[CONV_NAME]]
    conv_rows = _pack_rows(sum(int(np.prod(t.shape)) for t in conv_local), 8)
    conv_g, = _all_gather([_pack(conv_local, conv_rows)], 0, "gather_conv_taps")
    conv_parts = zip(*[_unpack(conv_g[j], [t.shape for t in conv_local]) for j in range(N_DEV)])
    conv_w, conv_m, conv_v = [jnp.concatenate(parts, axis=2) for parts in conv_parts]
    small = {n: w[n] for n in SMALL_NAMES}

    def whole(blocks):
        return {n: t.reshape(-1, t.shape[-1]) for n, t in zip(COMM_NAMES, blocks)}

    def own_slot(block):
        return lax.dynamic_update_slice_in_dim(jnp.zeros((N_DEV,) + block.shape, block.dtype), block[None], dev, 0)

    first = _all_gather([wb[n][0] for n in COMM_NAMES], 0, "gather_weights_0")
    in_flight = {}
    for i in range(1, depth):
        srcs, first = lax.optimization_barrier(([wb[n][i] for n in COMM_NAMES], first))
        in_flight[i] = _exchange_start(srcs, [own_slot(s) for s in srcs], True, "gather_weights_%d_start" % i)
        small["norm_attn_g"] = small["norm_attn_g"] + in_flight[i][4][0, 0]

    def layer_weights(i, h):
        if i == 0:
            got = whole(first)
        else:
            send_sems, recv_sems, srcs, lands, _ = in_flight.pop(i)
            got = whole(_exchange_wait(send_sems, recv_sems, srcs, lands, h, True, "gather_weights_%d_wait" % i))
        return dict(got, **{CONV_NAME: conv_w[i]})

    def as_slots(g, n):
        return g.reshape((N_DEV,) + wc[n].shape[1:])

    scattering = {}

    def layer_done(i, stage, g, small_now):
        if stage == "all" and i == 0:
            return small_now
        names = COMM_EARLY if stage == "ffn" else COMM_LATE
        srcs = [as_slots(g[n], n) for n in names]
        lands = [own_slot(lax.dynamic_index_in_dim(s, dev, 0, keepdims=False)) for s in srcs]
        started = _exchange_start(srcs, lands, False, "scatter_weight_grads_%d_%s_start" % (i, stage))
        scattering[i, stage] = (names, started)
        pin = "branch_norm_g" if stage == "ffn" else "norm_ple_g"
        return dict(small_now, **{pin: small_now[pin] + started[4][0, 0]})

    loss, dx, big_grads, grads = _local_step(x[0], p[:, 0], loss_target[0], layer_weights, small, layer_done)
    loss = lax.psum(loss, ("x", "y", "c"))

    recv = [{} for _ in range(depth)]
    last = _all_to_all([as_slots(big_grads[0][n], n) for n in COMM_LATE], "scatter_weight_grads_0_all")
    recv[0].update(zip(COMM_LATE, last))
    for (i, stage), (names, (send_sems, recv_sems, srcs, lands, _)) in scattering.items():
        got = _exchange_wait(send_sems, recv_sems, srcs, lands, dx, False,
                             "scatter_weight_grads_%d_%s_wait" % (i, stage))
        recv[i].update(zip(names, got))
    rep_names = SMALL_NAMES + (CONV_NAME,)
    rep_w = dict({n: w[n] for n in SMALL_NAMES}, **{CONV_NAME: conv_w})
    rep_m = dict({n: m[n] for n in SMALL_NAMES}, **{CONV_NAME: conv_m})
    rep_v = dict({n: v[n] for n in SMALL_NAMES}, **{CONV_NAME: conv_v})
    rep_shapes = [rep_w[n].shape for n in rep_names]
    rep_rows = _pack_rows(sum(int(np.prod(s)) for s in rep_shapes), SMALL_TILE_ROWS)
    rep_parts, = _all_gather([_pack([grads[n] for n in rep_names], rep_rows)], 0, "gather_small_grads")

    out = {}
    for n in COMM_NAMES:
        res = _adamw([recv[i][n] for i in range(depth)], wc[n], _to_comm(n, m[n]), _to_comm(n, v[n]),
                     "adamw_" + n, _tile_rows(wc[n].shape[1], 256))
        out[n] = [_to_comm(n, r) for r in res]
    rep_out = _adamw([rep_parts], *[_pack([src[n] for n in rep_names], rep_rows)[None] for src in (rep_w, rep_m, rep_v)],
                     "adamw_replicated", SMALL_TILE_ROWS)
    for n, vals in zip(rep_names, zip(*[_unpack(r[0], rep_shapes) for r in rep_out])):
        out[n] = list(vals)
    shard = ffn_conv_w.shape[2]
    out[CONV_NAME] = [lax.dynamic_slice_in_dim(t, dev * shard, shard, axis=2) for t in out[CONV_NAME]]
    results = [[out[n][kind] for n in WEIGHT_NAMES] for kind in range(4)]
    return (loss, dx[None], *results[0], *results[1], *results[2], *results[3])
```

```python
import math

import numpy as np
import jax
import jax.numpy as jnp
from jax import lax
from jax.experimental import pallas as pl
from jax.experimental.pallas import tpu as pltpu

F32 = jnp.float32
BF16 = jnp.bfloat16

D_MODEL = 1024
HEAD_DIM = 64
N_HEADS = 8
ATTN_W = 512
SGU_W = 256
SGU_GROUPS = 4
SGU_CHUNK = 128
SSM_W = 256
SSM_GROUPS = 16
SSM_CH = 16
SSM_STATE = 64
SSM_NS = SSM_GROUPS * SSM_STATE
IN_W = 2304
D_FF = 2816
PLE_DIM = 256
BRANCHES = ((128, 1), (512, 4), (2048, 16))
BLK = 128
N_BUCKETS = 32
REL_MAX = 2048
EPS = 1e-6
NEG_INF = -1e30
N_DEV = 8

ADAM_LR = 0.001
ADAM_B1 = 0.9
ADAM_B2 = 0.999
ADAM_EPS = 1e-08
ADAM_WD = 0.01
ADAM_STEP = 10

VMEM_LIMIT_BYTES = 56 * 1024 * 1024
GELU_C = math.sqrt(2.0 / math.pi)

SMALL_NAMES = ("rel_bias", "norm_attn_g", "sgu_ln_g", "sgu_ln_b", "sgu_w", "sgu_b", "ssm_a_re", "ssm_a_im",
               "ssm_log_dt", "ssm_b_re", "ssm_b_im", "ssm_c_re", "ssm_c_im", "ssm_d", "ssm_glu_b",
               "branch_norm_g", "norm_ffn_g", "ffn_conv_b", "norm_ple_g", "final_norm_g")
WEIGHT_NAMES = ("rel_bias", "norm_attn_g", "w_in", "sgu_ln_g", "sgu_ln_b", "sgu_w", "sgu_b", "ssm_a_re",
                "ssm_a_im", "ssm_log_dt", "ssm_b_re", "ssm_b_im", "ssm_c_re", "ssm_c_im", "ssm_d", "ssm_glu_w",
                "ssm_glu_b", "branch_norm_g", "w_out", "norm_ffn_g", "ffn_w_up", "ffn_conv_w", "ffn_conv_b",
                "ffn_w_down", "norm_ple_g", "ple_w_gate", "ple_w_proj", "final_norm_g")
PACK_COLS = 512


def _params(sem):
    return pltpu.CompilerParams(dimension_semantics=sem, vmem_limit_bytes=VMEM_LIMIT_BYTES)


def _pick(dim, target):
    if dim <= target:
        return dim
    best = None
    for t in range(128, target + 1, 128):
        if dim % t == 0:
            best = t
    return dim if best is None else best


def _gelu(x):
    return 0.5 * x * (1.0 + jnp.tanh(GELU_C * (x + 0.044715 * (x * x * x))))


def _gelu_grad(x):
    t = jnp.tanh(GELU_C * (x + 0.044715 * (x * x * x)))
    return 0.5 * (1.0 + t) + 0.5 * x * (1.0 - t * t) * (GELU_C * (1.0 + 3.0 * 0.044715 * (x * x)))


def _sigmoid(x):
    return 1.0 / (1.0 + jnp.exp(-x))


_DIMS = {"nn": (((1,), (0,)), ((), ())), "tn": (((0,), (0,)), ((), ())), "nt": (((1,), (1,)), ((), ()))}


def _mm(a, b, mode, name, add=None, out_dtype=F32, norm_gain=None, norm_bwd=None, tm=1408, tn=1408, tk=1408):
    if mode == "nn":
        m, k = a.shape
        k2, n = b.shape
    elif mode == "tn":
        k, m = a.shape
        k2, n = b.shape
    else:
        m, k = a.shape
        n, k2 = b.shape
    assert k == k2, (name, a.shape, b.shape, mode)
    tm, tn, tk = _pick(m, tm), _pick(n, tn), _pick(k, tk)
    nk = k // tk
    dims = _DIMS[mode]
    has_add = add is not None
    has_norm = norm_gain is not None
    has_nbwd = norm_bwd is not None
    assert not (has_norm or has_nbwd) or tn == n

    def body(*refs):
        a_ref, b_ref = refs[:2]
        rest = list(refs[2:])
        add_ref = rest.pop(0) if has_add else None
        g_ref = rest.pop(0) if has_norm else None
        h_ref, hg_ref = (rest.pop(0), rest.pop(0)) if has_nbwd else (None, None)
        o_ref = rest.pop(0)
        n_ref = rest.pop(0) if has_norm else None
        dg_ref = rest.pop(0) if has_nbwd else None
        part = lax.dot_general(a_ref[...].astype(BF16), b_ref[...].astype(BF16), dims,
                               preferred_element_type=F32)
        if has_nbwd:
            @pl.when((pl.program_id(0) == 0) & (pl.program_id(2) == 0))
            def _():
                dg_ref[...] = jnp.zeros_like(dg_ref)

        def finish(r):
            if has_nbwd:
                x = h_ref[...]
                scale = lax.rsqrt(jnp.mean(x * x, axis=-1, keepdims=True) + EPS)
                xh = x * scale
                dg_ref[...] += jnp.sum(r * xh, axis=0, keepdims=True)
                dxh = r * hg_ref[...]
                r = scale * (dxh - xh * jnp.mean(dxh * xh, axis=-1, keepdims=True))
            if has_add:
                r = r + add_ref[...]
            o_ref[...] = r.astype(out_dtype)
            if has_norm:
                scale = lax.rsqrt(jnp.mean(r * r, axis=-1, keepdims=True) + EPS)
                n_ref[...] = (r * scale * g_ref[...]).astype(BF16)

        if nk == 1:
            finish(part)
            return
        acc_ref = refs[-1]
        kk = pl.program_id(2)

        @pl.when(kk == 0)
        def _():
            acc_ref[...] = part

        @pl.when((kk > 0) & (kk < nk - 1))
        def _():
            acc_ref[...] += part

        @pl.when(kk == nk - 1)
        def _():
            finish(acc_ref[...] + part)

    if mode == "tn":
        a_spec = pl.BlockSpec((tk, tm), lambda i, j, kk: (kk, i))
    else:
        a_spec = pl.BlockSpec((tm, tk), lambda i, j, kk: (i, kk))
    if mode == "nt":
        b_spec = pl.BlockSpec((tn, tk), lambda i, j, kk: (j, kk))
    else:
        b_spec = pl.BlockSpec((tk, tn), lambda i, j, kk: (kk, j))
    o_spec = pl.BlockSpec((tm, tn), lambda i, j, kk: (i, j))
    in_specs = [a_spec, b_spec] + ([o_spec] if has_add else [])
    args = (a, b) + ((add,) if has_add else ())
    out_specs, out_shape = o_spec, jax.ShapeDtypeStruct((m, n), out_dtype)
    if has_norm:
        in_specs.append(pl.BlockSpec((1, n), lambda i, j, kk: (0, 0)))
        args += (norm_gain.reshape(1, n),)
        out_specs, out_shape = [o_spec, o_spec], [out_shape, jax.ShapeDtypeStruct((m, n), BF16)]
    if has_nbwd:
        row_spec = pl.BlockSpec((1, n), lambda i, j, kk: (0, 0))
        in_specs += [o_spec, row_spec]
        args += (norm_bwd[0], norm_bwd[1].reshape(1, n))
        out_specs, out_shape = [o_spec, row_spec], [out_shape, jax.ShapeDtypeStruct((1, n), F32)]
    sem = ("arbitrary",) * 3 if has_nbwd else ("parallel", "parallel", "arbitrary")
    return pl.pallas_call(
        body, name=name, grid=(m // tm, n // tn, nk),
        in_specs=in_specs, out_specs=out_specs, out_shape=out_shape,
        scratch_shapes=[pltpu.VMEM((tm, tn), F32)] if nk > 1 else [], compiler_params=_params(sem),
    )(*args)


def _rb(tm, w, cb=0):
    return pl.BlockSpec((tm, w), lambda i: (i, cb))


def _fb(shape):
    nd = len(shape)
    return pl.BlockSpec(shape, lambda i: (0,) * nd)


def _rowcall(body, name, n_rows, tm, in_specs, args, out_specs, out_shapes):
    return pl.pallas_call(
        body, name=name, grid=(n_rows // tm,), in_specs=in_specs, out_specs=out_specs, out_shape=out_shapes,
        compiler_params=_params(("arbitrary",)),
    )(*args)


def _sds(shape, dtype=F32):
    return jax.ShapeDtypeStruct(shape, dtype)


def _rms_fwd(h, g, name, tm=512):
    s, d = h.shape

    def body(h_ref, g_ref, o_ref):
        x = h_ref[...]
        r = lax.rsqrt(jnp.mean(x * x, axis=-1, keepdims=True) + EPS)
        o_ref[...] = (x * r * g_ref[...]).astype(BF16)

    return _rowcall(body, name, s, tm, [_rb(tm, d), _fb((1, d))], (h, g.reshape(1, d)), _rb(tm, d),
                    _sds((s, d), BF16))


def _loss_head(h, target, g, name, tm=512):
    s, d = h.shape

    def body(h_ref, t_ref, g_ref, dh_ref, loss_ref, dg_ref):
        @pl.when(pl.program_id(0) == 0)
        def _():
            dg_ref[...] = jnp.zeros_like(dg_ref)
            loss_ref[...] = jnp.zeros_like(loss_ref)

        x = h_ref[...]
        r = lax.rsqrt(jnp.mean(x * x, axis=-1, keepdims=True) + EPS)
        xh = x * r
        gg = g_ref[...]
        err = xh * gg - t_ref[...]
        loss_ref[...] += jnp.sum(err * err) * (0.5 / d)
        dy = err * (1.0 / d)
        dg_ref[...] += jnp.sum(dy * xh, axis=0, keepdims=True)
        dxh = dy * gg
        dh_ref[...] = r * (dxh - xh * jnp.mean(dxh * xh, axis=-1, keepdims=True))

    dh, loss, dg = _rowcall(body, name, s, tm, [_rb(tm, d), _rb(tm, d), _fb((1, d))], (h, target, g.reshape(1, d)),
                            [_rb(tm, d), _fb((1, 128)), _fb((1, d))], [_sds((s, d)), _sds((1, 128)), _sds((1, d))])
    return dh, loss[0, 0], dg.reshape(d)


_MIX_PARTS = ((0, 512), (512, 768), (768, 1024))


def _mix_fwd(ya, ysg, yss, g, name, tm=512):
    s = ya.shape[0]

    def body(a_ref, b_ref, c_ref, g_ref, o_ref):
        for ref, (lo, hi) in zip((a_ref, b_ref, c_ref), _MIX_PARTS):
            y = ref[...]
            r = lax.rsqrt(jnp.mean(y * y, axis=-1, keepdims=True) + EPS)
            o_ref[:, lo:hi] = (y * r * g_ref[:, lo:hi]).astype(BF16)

    return _rowcall(body, name, s, tm, [_rb(tm, 512), _rb(tm, 256), _rb(tm, 256), _fb((1, 1024))],
                    (ya, ysg, yss, g.reshape(1, 1024)), _rb(tm, 1024), _sds((s, 1024), BF16))


def _mix_bwd(dmix, ya, ysg, yss, g, name, tm=512):
    s = ya.shape[0]

    def body(dm_ref, a_ref, b_ref, c_ref, g_ref, da_ref, db_ref, dc_ref, dg_ref):
        @pl.when(pl.program_id(0) == 0)
        def _():
            dg_ref[...] = jnp.zeros_like(dg_ref)

        for ref, dref, (lo, hi) in zip((a_ref, b_ref, c_ref), (da_ref, db_ref, dc_ref), _MIX_PARTS):
            y = ref[...]
            r = lax.rsqrt(jnp.mean(y * y, axis=-1, keepdims=True) + EPS)
            xh = y * r
            dm = dm_ref[:, lo:hi]
            dg_ref[:, lo:hi] += jnp.sum(dm * xh, axis=0, keepdims=True)
            dxh = dm * g_ref[:, lo:hi]
            dref[...] = r * (dxh - xh * jnp.mean(dxh * xh, axis=-1, keepdims=True))

    da, db, dc, dg = _rowcall(
        body, name, s, tm, [_rb(tm, 1024), _rb(tm, 512), _rb(tm, 256), _rb(tm, 256), _fb((1, 1024))],
        (dmix, ya, ysg, yss, g.reshape(1, 1024)),
        [_rb(tm, 512), _rb(tm, 256), _rb(tm, 256), _fb((1, 1024))],
        [_sds((s, 512)), _sds((s, 256)), _sds((s, 256)), _sds((1, 1024))])
    return da, db, dc, dg.reshape(1024)


def _ssm_post_fwd(yc, z, d, gw, gb, name, tm=1024):
    s = yc.shape[0]

    def body(yc_ref, u_ref, d_ref, gw_ref, gb_ref, o_ref):
        y1 = yc_ref[...] + d_ref[...] * u_ref[...]
        y2 = _gelu(y1)
        gl = jnp.dot(y2.astype(BF16), gw_ref[...], preferred_element_type=F32) + gb_ref[...]
        o_ref[...] = y2 * _sigmoid(gl)

    return _rowcall(body, name, s, tm, [_rb(tm, 256), _rb(tm, 256, 8), _fb((1, 256)), _fb((256, 256)), _fb((1, 256))],
                    (yc, z, d.reshape(1, 256), gw, gb.reshape(1, 256)), _rb(tm, 256), _sds((s, 256)))


def _ssm_post_bwd(dy, yc, z, d, gw, gb, name, tm=1024):
    s = yc.shape[0]

    def body(dy_ref, yc_ref, u_ref, d_ref, gw_ref, gb_ref, dy1_ref, dgl_ref, y2_ref, dud_ref, dd_ref, dgb_ref):
        @pl.when(pl.program_id(0) == 0)
        def _():
            dd_ref[...] = jnp.zeros_like(dd_ref)
            dgb_ref[...] = jnp.zeros_like(dgb_ref)

        u = u_ref[...]
        dd = d_ref[...]
        y1 = yc_ref[...] + dd * u
        y2 = _gelu(y1)
        gw_v = gw_ref[...]
        gl = jnp.dot(y2.astype(BF16), gw_v, preferred_element_type=F32) + gb_ref[...]
        sg = _sigmoid(gl)
        dyv = dy_ref[...]
        dgl = dyv * y2 * sg * (1.0 - sg)
        dy2 = dyv * sg + lax.dot_general(dgl.astype(BF16), gw_v, _DIMS["nt"], preferred_element_type=F32)
        dy1 = dy2 * _gelu_grad(y1)
        dy1_ref[...] = dy1.astype(BF16)
        dgl_ref[...] = dgl.astype(BF16)
        y2_ref[...] = y2.astype(BF16)
        dud_ref[...] = dy1 * dd
        dd_ref[...] += jnp.sum(dy1 * u, axis=0, keepdims=True)
        dgb_ref[...] += jnp.sum(dgl, axis=0, keepdims=True)

    outs = _rowcall(
        body, name, s, tm,
        [_rb(tm, 256), _rb(tm, 256), _rb(tm, 256, 8), _fb((1, 256)), _fb((256, 256)), _fb((1, 256))],
        (dy, yc, z, d.reshape(1, 256), gw, gb.reshape(1, 256)),
        [_rb(tm, 256)] * 4 + [_fb((1, 256))] * 2,
        [_sds((s, 256), BF16)] * 3 + [_sds((s, 256))] + [_sds((1, 256))] * 2)
    dy1, dgl, y2, dud, dd, dgb = outs
    return dy1, dgl, y2, dud, dd.reshape(256), dgb.reshape(256)


SCAN_T = 512
N_SCAN_TABLES = 6


def _scan_tables(pr, pi, reverse):
    ns = pr.shape[0]
    sign = -1.0 if reverse else 1.0
    power = [(jnp.ones((ns,), F32), jnp.zeros((ns,), F32))] + [(pr[:, k], sign * pi[:, k]) for k in range(8)]
    zero = (jnp.zeros((ns,), F32), jnp.zeros((ns,), F32))

    def table(exponents):
        rows = [zero if e is None else power[e] for e in exponents]
        return jnp.stack([jnp.concatenate(row) for row in rows])

    tabs = []
    for k in (1, 2, 4):
        has_partner = [(s < 8 - k) if reverse else (s >= k) for s in range(8)]
        tabs.append(table([k if ok else None for ok in has_partner]))
    tabs.append(table([s if reverse else 7 - s for s in range(8)]))
    tabs.append(table([8 - s if reverse else s + 1 for s in range(8)]))
    tabs.append(table([8] * 8))
    return jnp.stack(tabs)


def _cmul(ar, ai, br, bi):
    return ar * br - ai * bi, ar * bi + ai * br


def _scan_group(ur, ui, cr, ci, tr_ref, ti_ref, reverse):
    xr, xi = ur, ui
    for n, k in enumerate((1, 2, 4)):
        shift = 8 - k if reverse else k
        pr, pi = _cmul(tr_ref[n], ti_ref[n], pltpu.roll(xr, shift, axis=0), pltpu.roll(xi, shift, axis=0))
        xr, xi = xr + pr, xi + pi
    sr, si = _cmul(tr_ref[3], ti_ref[3], ur, ui)
    for k in (1, 2, 4):
        sr, si = sr + pltpu.roll(sr, k, axis=0), si + pltpu.roll(si, k, axis=0)
    pr, pi = _cmul(tr_ref[4], ti_ref[4], cr, ci)
    nr, ni = _cmul(tr_ref[5], ti_ref[5], cr, ci)
    return xr + pr, xi + pi, nr + sr, ni + si


def _table_halves(t_ref):
    return t_ref.at[:, :, pl.ds(0, SSM_NS)], t_ref.at[:, :, pl.ds(SSM_NS, SSM_NS)]


_U_BLOCK = (IN_W - SSM_W) // SSM_W


def _ssm_fwd(z, bdt, cd, tabs, name):
    s = z.shape[0]
    ns = SSM_NS
    n_t = s // SCAN_T

    def body(u_ref, b_ref, c_ref, t_ref, xr_ref, xi_ref, y_ref, cr_ref, ci_ref, ur_ref, ui_ref):
        @pl.when(pl.program_id(0) == 0)
        def _():
            cr_ref[...] = jnp.zeros_like(cr_ref)
            ci_ref[...] = jnp.zeros_like(ci_ref)

        bu = lax.dot_general(u_ref[...].astype(BF16), b_ref[...], _DIMS["nt"], preferred_element_type=F32)
        ur_ref[...] = bu[:, :ns]
        ui_ref[...] = bu[:, ns:]
        tr_ref, ti_ref = _table_halves(t_ref)

        def group(g, carry):
            rows = pl.ds(pl.multiple_of(g * 8, 8), 8)
            xr, xi, cr, ci = _scan_group(ur_ref[rows, :], ui_ref[rows, :], *carry, tr_ref, ti_ref, False)
            xr_ref[rows, :] = xr
            xi_ref[rows, :] = xi
            return cr, ci

        cr, ci = lax.fori_loop(0, SCAN_T // 8, group, (cr_ref[...], ci_ref[...]), unroll=2)
        cr_ref[...] = cr
        ci_ref[...] = ci
        y_ref[...] = (jnp.dot(xr_ref[...].astype(BF16), c_ref[0:ns, :], preferred_element_type=F32)
                      + jnp.dot(xi_ref[...].astype(BF16), c_ref[ns:, :], preferred_element_type=F32))

    x_spec = pl.BlockSpec((SCAN_T, ns), lambda t: (t, 0))
    return pl.pallas_call(
        body, name=name, grid=(n_t,),
        in_specs=[pl.BlockSpec((SCAN_T, SSM_W), lambda t: (t, _U_BLOCK)), _fb((2 * ns, SSM_W)),
                  _fb((2 * ns, SSM_W)), _fb((N_SCAN_TABLES, 8, 2 * ns))],
        out_specs=[x_spec, x_spec, _rb(SCAN_T, SSM_W)],
        out_shape=[_sds((s, ns)), _sds((s, ns)), _sds((s, SSM_W))],
        scratch_shapes=[pltpu.VMEM((8, ns), F32)] * 2 + [pltpu.VMEM((SCAN_T, ns), F32)] * 2,
        compiler_params=_params(("arbitrary",)),
    )(z, bdt, cd, tabs)


def _ssm_bwd(dy1, dud, z, xr, xi, bdt, cd, tabs, name):
    s = z.shape[0]
    ns = SSM_NS
    n_t = s // SCAN_T
    n_groups = SCAN_T // 8

    def body(dy_ref, dud_ref, u_ref, xr_ref, xi_ref, pxr_ref, pxi_ref, b_ref, c_ref, t_ref,
             du_ref, dbd_ref, dcd_ref, dar_ref, dai_ref,
             cr_ref, ci_ref, ar_ref, ai_ref, sxr_ref, sxi_ref, gr_ref, gi_ref, lr_ref, li_ref, bacc_ref, cacc_ref):
        t = pl.program_id(0)

        @pl.when(t == 0)
        def _():
            for ref in (cr_ref, ci_ref, ar_ref, ai_ref, bacc_ref, cacc_ref):
                ref[...] = jnp.zeros_like(ref)

        dyb = dy_ref[...]
        g = lax.dot_general(dyb, c_ref[...], _DIMS["nt"], preferred_element_type=F32)
        gr_ref[...] = g[:, :ns]
        gi_ref[...] = g[:, ns:]
        has_before = (t < n_t - 1).astype(F32)
        sxr_ref[0:8, :] = pxr_ref[...] * has_before
        sxi_ref[0:8, :] = pxi_ref[...] * has_before
        sxr_ref[8:, :] = xr_ref[...]
        sxi_ref[8:, :] = xi_ref[...]
        first_row = lax.broadcasted_iota(jnp.int32, (8, ns), 0) == 0
        tr_ref, ti_ref = _table_halves(t_ref)

        def group(k, carry):
            cr, ci, ar, ai = carry
            g8 = pl.multiple_of((n_groups - 1 - k) * 8, 8)
            rows = pl.ds(g8, 8)
            lr, li, cr, ci = _scan_group(gr_ref[rows, :], gi_ref[rows, :], cr, ci, tr_ref, ti_ref, True)
            lr_ref[rows, :] = lr
            li_ref[rows, :] = li
            here, before = pl.ds(g8 + 8, 8), rows
            pr = jnp.where(first_row, pltpu.roll(sxr_ref[before, :], 1, axis=0), pltpu.roll(sxr_ref[here, :], 1, axis=0))
            pi = jnp.where(first_row, pltpu.roll(sxi_ref[before, :], 1, axis=0), pltpu.roll(sxi_ref[here, :], 1, axis=0))
            return cr, ci, ar + lr * pr + li * pi, ai + li * pr - lr * pi

        cr, ci, ar, ai = lax.fori_loop(0, n_groups, group,
                                       (cr_ref[...], ci_ref[...], ar_ref[...], ai_ref[...]), unroll=2)
        cr_ref[...] = cr
        ci_ref[...] = ci
        ar_ref[...] = ar
        ai_ref[...] = ai
        lrb = lr_ref[...].astype(BF16)
        lib = li_ref[...].astype(BF16)
        ub = u_ref[...].astype(BF16)
        du_ref[...] = (dud_ref[...] + jnp.dot(lrb, b_ref[0:ns, :], preferred_element_type=F32)
                       + jnp.dot(lib, b_ref[ns:, :], preferred_element_type=F32))
        bacc_ref[0:ns, :] += lax.dot_general(lrb, ub, _DIMS["tn"], preferred_element_type=F32)
        bacc_ref[ns:, :] += lax.dot_general(lib, ub, _DIMS["tn"], preferred_element_type=F32)
        cacc_ref[0:ns, :] += lax.dot_general(xr_ref[...].astype(BF16), dyb, _DIMS["tn"], preferred_element_type=F32)
        cacc_ref[ns:, :] += lax.dot_general(xi_ref[...].astype(BF16), dyb, _DIMS["tn"], preferred_element_type=F32)

        @pl.when(t == n_t - 1)
        def _():
            for k in (1, 2, 4):
                ar_ref[...] += pltpu.roll(ar_ref[...], k, axis=0)
                ai_ref[...] += pltpu.roll(ai_ref[...], k, axis=0)
            dar_ref[...] = ar_ref[...]
            dai_ref[...] = ai_ref[...]
            dbd_ref[...] = bacc_ref[...]
            dcd_ref[...] = cacc_ref[...]

    rev = lambda t: n_t - 1 - t
    row_spec = pl.BlockSpec((SCAN_T, SSM_W), lambda t: (rev(t), 0))
    x_spec = pl.BlockSpec((SCAN_T, ns), lambda t: (rev(t), 0))
    before_spec = pl.BlockSpec((8, ns), lambda t: (jnp.maximum(rev(t) * (SCAN_T // 8) - 1, 0), 0))
    du, dbd, dcd, dar, dai = pl.pallas_call(
        body, name=name, grid=(n_t,),
        in_specs=[row_spec, row_spec, pl.BlockSpec((SCAN_T, SSM_W), lambda t: (rev(t), _U_BLOCK)),
                  x_spec, x_spec, before_spec, before_spec,
                  _fb((2 * ns, SSM_W)), _fb((2 * ns, SSM_W)), _fb((N_SCAN_TABLES, 8, 2 * ns))],
        out_specs=[row_spec, _fb((2 * ns, SSM_W)), _fb((2 * ns, SSM_W)), _fb((8, ns)), _fb((8, ns))],
        out_shape=[_sds((s, SSM_W)), _sds((2 * ns, SSM_W)), _sds((2 * ns, SSM_W)), _sds((8, ns)), _sds((8, ns))],
        scratch_shapes=([pltpu.VMEM((8, ns), F32)] * 4 + [pltpu.VMEM((SCAN_T + 8, ns), F32)] * 2
                        + [pltpu.VMEM((SCAN_T, ns), F32)] * 4 + [pltpu.VMEM((2 * ns, SSM_W), F32)] * 2),
        compiler_params=_params(("arbitrary",)),
    )(dy1, dud, z, xr, xi, xr, xi, bdt, cd, tabs)
    return du, dbd, dcd, dar[0], dai[0]


def _group_ids():
    return lax.broadcasted_iota(jnp.int32, (1, SGU_W), 1) // 64


def _group_mean(val, gid):
    out = jnp.zeros_like(val)
    for g in range(SGU_GROUPS):
        mg = gid == g
        out = jnp.where(mg, jnp.sum(jnp.where(mg, val, 0.0), axis=1, keepdims=True) * (1.0 / 64), out)
    return out


def _causal_w(w_ref, g):
    t = lax.broadcasted_iota(jnp.int32, (SGU_CHUNK, SGU_CHUNK), 0)
    s = lax.broadcasted_iota(jnp.int32, (SGU_CHUNK, SGU_CHUNK), 1)
    return jnp.where(t >= s, w_ref[g], 0.0).astype(BF16)


def _sgu_core(x, lng, lnb, w_ref, bexp, gid):
    zz = _gelu(x)
    u = zz[:, :SGU_W]
    v = zz[:, SGU_W:]
    vc = v - _group_mean(v, gid)
    rstd = lax.rsqrt(_group_mean(vc * vc, gid) + EPS)
    vhat = vc * rstd
    vn = vhat * lng + lnb
    vnb = vn.astype(BF16)
    mixed = bexp
    for g in range(SGU_GROUPS):
        mm = jnp.dot(_causal_w(w_ref, g), vnb, preferred_element_type=F32)
        mixed = jnp.where(gid == g, mm + bexp, mixed)
    return u, rstd, vhat, vnb, mixed


def _sgu_fwd(z, lng, lnb, w, bexp, name, tm=512):
    s = z.shape[0]

    def body(z_ref, lng_ref, lnb_ref, w_ref, b_ref, o_ref):
        gid = _group_ids()
        for j in range(tm // SGU_CHUNK):
            rows = pl.ds(j * SGU_CHUNK, SGU_CHUNK)
            u, _, _, _, mixed = _sgu_core(z_ref[rows, :], lng_ref[...], lnb_ref[...], w_ref, b_ref[...], gid)
            o_ref[rows, :] = u * mixed

    return _rowcall(body, name, s, tm,
                    [_rb(tm, 512, 3), _fb((1, 256)), _fb((1, 256)), _fb((4, 128, 128)), _fb((128, 256))],
                    (z, lng.reshape(1, 256), lnb.reshape(1, 256), w, bexp), _rb(tm, 256), _sds((s, 256)))


def _sgu_bwd(z, dy, lng, lnb, w, bexp, name, tm=512):
    s = z.shape[0]

    def body(z_ref, dy_ref, lng_ref, lnb_ref, w_ref, b_ref, dz_ref, dw_ref, db_ref, dlng_ref, dlnb_ref):
        @pl.when(pl.program_id(0) == 0)
        def _():
            dw_ref[...] = jnp.zeros_like(dw_ref)
            db_ref[...] = jnp.zeros_like(db_ref)
            dlng_ref[...] = jnp.zeros_like(dlng_ref)
            dlnb_ref[...] = jnp.zeros_like(dlnb_ref)

        gid = _group_ids()
        t = lax.broadcasted_iota(jnp.int32, (SGU_CHUNK, SGU_CHUNK), 0)
        sidx = lax.broadcasted_iota(jnp.int32, (SGU_CHUNK, SGU_CHUNK), 1)
        lng_v = lng_ref[...]
        for j in range(tm // SGU_CHUNK):
            rows = pl.ds(j * SGU_CHUNK, SGU_CHUNK)
            x = z_ref[rows, :]
            u, rstd, vhat, vnb, mixed = _sgu_core(x, lng_v, lnb_ref[...], w_ref, b_ref[...], gid)
            dyv = dy_ref[rows, :]
            dmixed = dyv * u
            du = dyv * mixed
            db_ref[...] += dmixed
            dvn = jnp.zeros_like(dmixed)
            for g in range(SGU_GROUPS):
                dmg = jnp.where(gid == g, dmixed, 0.0).astype(BF16)
                dvn = dvn + lax.dot_general(_causal_w(w_ref, g), dmg, _DIMS["tn"], preferred_element_type=F32)
                dwg = lax.dot_general(dmg, vnb, _DIMS["nt"], preferred_element_type=F32)
                dw_ref[g] += jnp.where(t >= sidx, dwg, 0.0)
            dlnb_ref[...] += jnp.sum(dvn, axis=0, keepdims=True)
            dlng_ref[...] += jnp.sum(dvn * vhat, axis=0, keepdims=True)
            dvh = dvn * lng_v
            dv = rstd * (dvh - _group_mean(dvh, gid) - vhat * _group_mean(dvh * vhat, gid))
            gg = _gelu_grad(x)
            dz_ref[rows, 0:SGU_W] = du * gg[:, :SGU_W]
            dz_ref[rows, SGU_W:2 * SGU_W] = dv * gg[:, SGU_W:]

    dz, dw, db, dlng, dlnb = _rowcall(
        body, name, s, tm,
        [_rb(tm, 512, 3), _rb(tm, 256), _fb((1, 256)), _fb((1, 256)), _fb((4, 128, 128)), _fb((128, 256))],
        (z, dy, lng.reshape(1, 256), lnb.reshape(1, 256), w, bexp),
        [_rb(tm, 512), _fb((4, 128, 128)), _fb((128, 256)), _fb((1, 256)), _fb((1, 256))],
        [_sds((s, 512)), _sds((4, 128, 128)), _sds((128, 256)), _sds((1, 256)), _sds((1, 256))])
    return dz, dw, db, dlng.reshape(256), dlnb.reshape(256)


CONV_TC = 1408
N_CT = D_FF // CONV_TC


def _row_of(block8, j):
    r = lax.broadcasted_iota(jnp.int32, block8.shape, 0)
    return jnp.sum(jnp.where(r == j, block8, 0.0), axis=0, keepdims=True)


EDGE = 16


def _conv_fwd(hu, cw, cb, name, tm=256):
    s = hu.shape[0]

    def body(xv_ref, xg_ref, tv_ref, tg_ref, wv_ref, wg_ref, bv_ref, bg_ref, hv_ref, hg_ref, act_ref):
        has_prev = (pl.program_id(1) > 0).astype(F32)
        row = lax.broadcasted_iota(jnp.int32, (EDGE, CONV_TC), 0)

        def conv(x_ref, t_ref, w_ref, b_ref):
            x = x_ref[...].astype(F32)
            w0, w1, w2, bb = w_ref[0:1, :], w_ref[1:2, :], w_ref[2:3, :], b_ref[...]
            whole = w0 * pltpu.roll(x, 2, axis=0) + w1 * pltpu.roll(x, 1, axis=0) + w2 * x + bb
            tail = t_ref[...].astype(F32)
            r7 = _row_of(tail, EDGE - 1) * has_prev
            r6 = _row_of(tail, EDGE - 2) * has_prev
            xe = x_ref[0:EDGE, :].astype(F32)
            x1 = jnp.where(row == 0, r7, pltpu.roll(xe, 1, axis=0))
            x2 = jnp.where(row == 0, r6, jnp.where(row == 1, r7, pltpu.roll(xe, 2, axis=0)))
            return whole, w0 * x2 + w1 * x1 + w2 * xe + bb

        hv, hv_edge = conv(xv_ref, tv_ref, wv_ref, bv_ref)
        hg, hg_edge = conv(xg_ref, tg_ref, wg_ref, bg_ref)
        hv_ref[...] = hv.astype(BF16)
        hg_ref[...] = hg.astype(BF16)
        act_ref[...] = (_gelu(hg) * hv).astype(BF16)
        hv_ref[0:EDGE, :] = hv_edge.astype(BF16)
        hg_ref[0:EDGE, :] = hg_edge.astype(BF16)
        act_ref[0:EDGE, :] = (_gelu(hg_edge) * hv_edge).astype(BF16)

    def xs(off):
        return pl.BlockSpec((tm, CONV_TC), lambda j, i: (i, j + off))

    def ts(off):
        return pl.BlockSpec((EDGE, CONV_TC), lambda j, i: (jnp.maximum(i * (tm // EDGE) - 1, 0), j + off))

    def ws(rows, off):
        return pl.BlockSpec((rows, CONV_TC), lambda j, i: (0, j + off))

    o_spec = pl.BlockSpec((tm, CONV_TC), lambda j, i: (i, j))
    return pl.pallas_call(
        body, name=name, grid=(N_CT, s // tm),
        in_specs=[xs(0), xs(N_CT), ts(0), ts(N_CT), ws(3, 0), ws(3, N_CT), ws(1, 0), ws(1, N_CT)],
        out_specs=[o_spec] * 3, out_shape=[_sds((s, D_FF), BF16)] * 3,
        compiler_params=_params(("parallel", "arbitrary")),
    )(hu, hu, hu, hu, cw, cw, cb.reshape(1, 2 * D_FF), cb.reshape(1, 2 * D_FF))


HALO = EDGE


def _conv_bwd(dact, hv, hg, hu, cw, name, tm=256):
    s = dact.shape[0]

    def body(da_ref, dan_ref, hv_ref, hvn_ref, hg_ref, hgn_ref, x_ref, t_ref, w_ref, dx_ref, dw_ref, db_ref, d_scr):
        i = pl.program_id(1)
        is_value = pl.program_id(0) < N_CT

        @pl.when(i == 0)
        def _():
            dw_ref[...] = jnp.zeros_like(dw_ref)
            db_ref[...] = jnp.zeros_like(db_ref)

        for rows, (a_ref, v_ref, g_ref) in ((pl.ds(0, tm), (da_ref, hv_ref, hg_ref)),
                                            (pl.ds(tm, HALO), (dan_ref, hvn_ref, hgn_ref))):
            @pl.when(is_value)
            def _():
                d_scr[rows, :] = a_ref[...].astype(F32) * _gelu(g_ref[...].astype(F32))

            @pl.when(jnp.logical_not(is_value))
            def _():
                d_scr[rows, :] = (a_ref[...].astype(F32) * v_ref[...].astype(F32)
                                  * _gelu_grad(g_ref[...].astype(F32)))

        has_prev = (i > 0).astype(F32)
        has_next = (i < s // tm - 1).astype(F32)
        w0, w1, w2 = w_ref[0:1, :], w_ref[1:2, :], w_ref[2:3, :]
        d = d_scr[0:tm, :]
        dx_ref[...] = (w2 * d + w1 * pltpu.roll(d, tm - 1, axis=0) + w0 * pltpu.roll(d, tm - 2, axis=0)).astype(BF16)
        row = lax.broadcasted_iota(jnp.int32, (EDGE, CONV_TC), 0)
        nxt = d_scr[tm:tm + HALO, :]
        n0 = _row_of(nxt, 0) * has_next
        n1 = _row_of(nxt, 1) * has_next
        de = d_scr[tm - EDGE:tm, :]
        d1 = jnp.where(row == EDGE - 1, n0, pltpu.roll(de, EDGE - 1, axis=0))
        d2 = jnp.where(row == EDGE - 2, n0, jnp.where(row == EDGE - 1, n1, pltpu.roll(de, EDGE - 2, axis=0)))
        dx_ref[tm - EDGE:tm, :] = (w2 * de + w1 * d1 + w0 * d2).astype(BF16)
        x = x_ref[...].astype(F32)
        tail = t_ref[...].astype(F32)
        r7 = _row_of(tail, EDGE - 1) * has_prev
        r6 = _row_of(tail, EDGE - 2) * has_prev
        last = x_ref[tm - EDGE:tm, :].astype(F32)
        l7, l6 = _row_of(last, EDGE - 1), _row_of(last, EDGE - 2)
        head = d_scr[0:8, :]
        d0, d1h = _row_of(head, 0), _row_of(head, 1)
        dw_ref[0:1, :] += (jnp.sum(d * pltpu.roll(x, 2, axis=0), axis=0, keepdims=True)
                           + d0 * (r6 - l6) + d1h * (r7 - l7))
        dw_ref[1:2, :] += jnp.sum(d * pltpu.roll(x, 1, axis=0), axis=0, keepdims=True) + d0 * (r7 - l7)
        dw_ref[2:3, :] += jnp.sum(d * x, axis=0, keepdims=True)
        db_ref[...] += jnp.sum(d, axis=0, keepdims=True)

    a_spec = pl.BlockSpec((tm, CONV_TC), lambda j, i: (i, j % N_CT))
    an_spec = pl.BlockSpec((HALO, CONV_TC),
                           lambda j, i: (jnp.minimum((i + 1) * (tm // HALO), s // HALO - 1), j % N_CT))
    x_spec = pl.BlockSpec((tm, CONV_TC), lambda j, i: (i, j))
    t_spec = pl.BlockSpec((EDGE, CONV_TC), lambda j, i: (jnp.maximum(i * (tm // EDGE) - 1, 0), j))
    w_spec = pl.BlockSpec((3, CONV_TC), lambda j, i: (0, j))
    db_spec = pl.BlockSpec((1, CONV_TC), lambda j, i: (0, j))
    return pl.pallas_call(
        body, name=name, grid=(2 * N_CT, s // tm),
        in_specs=[a_spec, an_spec, a_spec, an_spec, a_spec, an_spec, x_spec, t_spec, w_spec],
        out_specs=[x_spec, w_spec, db_spec],
        out_shape=[_sds((s, 2 * D_FF), BF16), _sds((3, 2 * D_FF)), _sds((1, 2 * D_FF))],
        scratch_shapes=[pltpu.VMEM((tm + HALO, CONV_TC), F32)],
        compiler_params=_params(("parallel", "arbitrary")),
    )(dact, dact, hv, hv, hg, hg, hu, hu, cw)


def _ple_fwd(h, gp, pp, next_gain, name, tm=512):
    s, d = h.shape
    with_norm = next_gain is not None

    def body(*refs):
        h_ref, g_ref, p_ref = refs[:3]
        out = h_ref[...] + _sigmoid(g_ref[...].astype(F32)) * p_ref[...].astype(F32)
        if with_norm:
            n_ref, o_ref, a_ref = refs[3:]
            scale = lax.rsqrt(jnp.mean(out * out, axis=-1, keepdims=True) + EPS)
            a_ref[...] = (out * scale * n_ref[...]).astype(BF16)
        else:
            o_ref, = refs[3:]
        o_ref[...] = out

    if not with_norm:
        return _rowcall(body, name, s, tm, [_rb(tm, d)] * 3, (h, gp, pp), _rb(tm, d), _sds((s, d))), None
    return _rowcall(body, name, s, tm, [_rb(tm, d)] * 3 + [_fb((1, d))], (h, gp, pp, next_gain.reshape(1, d)),
                    [_rb(tm, d)] * 2, [_sds((s, d)), _sds((s, d), BF16)])


def _ple_bwd(dh, gp, pp, name, tm=512):
    s, d = dh.shape

    def body(d_ref, g_ref, p_ref, dp_ref, dg_ref):
        sg = _sigmoid(g_ref[...].astype(F32))
        dv = d_ref[...]
        dp_ref[...] = (dv * sg).astype(BF16)
        dg_ref[...] = (dv * p_ref[...].astype(F32) * sg * (1.0 - sg)).astype(BF16)

    return _rowcall(body, name, s, tm, [_rb(tm, d)] * 3, (dh, gp, pp), [_rb(tm, d)] * 2,
                    [_sds((s, d), BF16)] * 2)


SCALE = HEAD_DIM ** -0.5
ATT_ROWS = 2048


def _att_geom(s, dil):
    w = min(ATT_ROWS, s)
    p = BLK * dil
    assert w % p == 0 and s % w == 0
    return w, p, w // p


def _rows(start, dil):
    return pl.ds(start, BLK, stride=dil) if dil > 1 else pl.ds(start, BLK)


def _head_masks():
    lane = lax.broadcasted_iota(jnp.int32, (1, BLK), 1)
    return [lane < HEAD_DIM, lane >= HEAD_DIM]


def _band():
    rel = np.arange(BLK)[:, None] + BLK - np.arange(2 * BLK)[None, :]
    return (rel >= 0) & (rel <= BLK)


def _zcur(w):
    return lambda off: pl.BlockSpec((w, BLK), lambda hp, i: (i, off + hp))


def _zprev(p, nb):
    return lambda off: pl.BlockSpec((p, BLK), lambda hp, i: (jnp.maximum(i * nb - 1, 0), off + hp))


def _scur(w):
    return pl.BlockSpec((w, BLK), lambda hp, i: (i, hp))


def _pair_rows(t, masks):
    return jnp.concatenate([jnp.where(masks[0], t, 0.0), jnp.where(masks[1], t, 0.0)], axis=0).astype(BF16)


def _pair_bias_bwd(bias):
    return bias.reshape(4, 2, BLK, 2, BLK).transpose(0, 3, 2, 1, 4).reshape(4, 2, BLK, 2 * BLK)


def _unpair_bias_bwd(db):
    return db.reshape(4, 2, BLK, 2, BLK).transpose(0, 3, 2, 1, 4).reshape(N_HEADS, BLK, 2 * BLK)


def _attn_fwd(z, bias, state, dil, first, last, name):
    s = z.shape[0]
    w, p, nb = _att_geom(s, dil)

    def body(*refs):
        q_ref, kp_ref, kc_ref, vp_ref, vc_ref, b_ref = refs[:6]
        rest = refs[6:]
        if not first:
            m_ref, l_ref, a_ref = rest[:3]
            rest = rest[3:]
        i = pl.program_id(1)
        masks = _head_masks()
        own_block = lax.broadcasted_iota(jnp.int32, (1, 2 * BLK), 1) >= BLK
        for r in range(dil):
            for b in range(nb):
                rows = _rows(r + p * b, dil)
                prev_rows = _rows(r + p * (b - 1), dil) if b > 0 else _rows(r, dil)
                kprev, vprev = (kc_ref, vc_ref) if b > 0 else (kp_ref, vp_ref)
                q = q_ref[rows, :] * SCALE
                k = jnp.concatenate([kprev[prev_rows, :], kc_ref[rows, :]], axis=0).astype(BF16)
                v = jnp.concatenate([vprev[prev_rows, :], vc_ref[rows, :]], axis=0).astype(BF16)
                mb = lb = ob = None
                for hh, mh in enumerate(masks):
                    qh = jnp.where(mh, q, 0.0).astype(BF16)
                    sc = lax.dot_general(qh, k, _DIMS["nt"], preferred_element_type=F32) + b_ref[hh]
                    if b == 0:
                        sc = jnp.where(own_block | (i > 0), sc, NEG_INF)
                    mx = jnp.max(sc, axis=1, keepdims=True)
                    e = jnp.exp(sc - mx)
                    den = jnp.sum(e, axis=1, keepdims=True)
                    o = jnp.dot(e.astype(BF16), v, preferred_element_type=F32)
                    if hh == 0:
                        mb = jnp.broadcast_to(mx, (BLK, BLK))
                        lb = jnp.broadcast_to(den, (BLK, BLK))
                        ob = o
                    else:
                        mb = jnp.where(mh, mx, mb)
                        lb = jnp.where(mh, den, lb)
                        ob = jnp.where(mh, o, ob)
                if first:
                    m_new, l_new, a_new = mb, lb, ob
                else:
                    m_old = m_ref[rows, :]
                    m_new = jnp.maximum(m_old, mb)
                    al = jnp.exp(m_old - m_new)
                    be = jnp.exp(mb - m_new)
                    l_new = al * l_ref[rows, :] + be * lb
                    a_new = al * a_ref[rows, :] + be * ob
                if last:
                    y_ref, lse_ref = rest
                    y_ref[rows, :] = a_new / l_new
                    lse_ref[rows, :] = m_new + jnp.log(l_new)
                else:
                    mo_ref, lo_ref, ao_ref = rest
                    mo_ref[rows, :] = m_new
                    lo_ref[rows, :] = l_new
                    ao_ref[rows, :] = a_new

    cur, prv = _zcur(w), _zprev(p, nb)
    b_spec = pl.BlockSpec((2, BLK, 2 * BLK), lambda hp, i: (hp, 0, 0))
    in_specs = [cur(0), prv(4), cur(4), prv(8), cur(8), b_spec]
    args = [z, z, z, z, z, bias]
    if not first:
        in_specs += [_scur(w)] * 3
        args += list(state)
    n_out = 2 if last else 3
    return pl.pallas_call(
        body, name=name, grid=(4, s // w), in_specs=in_specs, out_specs=[_scur(w)] * n_out,
        out_shape=[_sds((s, ATTN_W))] * n_out,
        compiler_params=_params(("parallel", "parallel")),
    )(*args)


def _row_stats(mh, dy, y, lse):
    delta = jnp.sum(jnp.where(mh, dy * y, 0.0), axis=1, keepdims=True)
    lse_h = jnp.max(jnp.where(mh, lse, NEG_INF), axis=1, keepdims=True)
    return delta, lse_h


def _attn_bwd(z, bias, dy, y, lse, prev, dil, name):
    s = z.shape[0]
    w, p, nb = _att_geom(s, dil)
    n_steps = s // w
    first = prev is None

    def body(*refs):
        q_ref, kp_ref, kc_ref, vp_ref, vc_ref, b_ref, dy_ref, y_ref, lse_ref = refs[:9]
        rest = refs[9:]
        if not first:
            dqp_ref, dkp_ref, dvp_ref = rest[:3]
            rest = rest[3:]
        dq_ref, dk_ref, dv_ref, dkx_ref, dvx_ref, db_ref = rest
        i = pl.program_id(1)

        @pl.when(i == 0)
        def _():
            db_ref[...] = jnp.zeros_like(db_ref)

        masks = _head_masks()
        first_head = lax.broadcasted_iota(jnp.int32, (1, 2 * BLK), 1) < BLK

        def flush(rows, dk, dv):
            if not first:
                dk = dk + dkp_ref[rows, :]
                dv = dv + dvp_ref[rows, :]
            dk_ref[rows, :] = dk
            dv_ref[rows, :] = dv

        for r in range(dil):
            carry = None
            for b in range(nb):
                rows = _rows(r + p * b, dil)
                prev_rows = _rows(r + p * (b - 1), dil) if b > 0 else _rows(r, dil)
                kprev, vprev = (kc_ref, vc_ref) if b > 0 else (kp_ref, vp_ref)
                keys = [(_pair_rows(kprev[prev_rows, :], masks), _pair_rows(vprev[prev_rows, :], masks)),
                        (_pair_rows(kc_ref[rows, :], masks), _pair_rows(vc_ref[rows, :], masks))]
                q = (q_ref[rows, :] * SCALE).astype(BF16)
                dy_v = dy_ref[rows, :]
                dyb = dy_v.astype(BF16)
                stats = [_row_stats(mh, dy_v, y_ref[rows, :], lse_ref[rows, :]) for mh in masks]
                delta = jnp.where(first_head, stats[0][0], stats[1][0])
                lse_h = jnp.where(first_head, stats[0][1], stats[1][1])
                dq = jnp.zeros((BLK, BLK), F32)
                dk, dv = [], []
                for half in range(2):
                    kh, vh = keys[half]
                    sc = lax.dot_general(q, kh, _DIMS["nt"], preferred_element_type=F32) + b_ref[half]
                    pr = jnp.exp(sc - lse_h)
                    if b == 0 and half == 0:
                        pr = pr * (i > 0).astype(F32)
                    dp = lax.dot_general(dyb, vh, _DIMS["nt"], preferred_element_type=F32)
                    ds = pr * (dp - delta)
                    db_ref[half] += ds
                    dsb = ds.astype(BF16)
                    dq = dq + jnp.dot(dsb, kh, preferred_element_type=F32)
                    dk2 = lax.dot_general(dsb, q, _DIMS["tn"], preferred_element_type=F32)
                    dv2 = lax.dot_general(pr.astype(BF16), dyb, _DIMS["tn"], preferred_element_type=F32)
                    dk.append(jnp.where(masks[0], dk2[:BLK], dk2[BLK:]))
                    dv.append(jnp.where(masks[0], dv2[:BLK], dv2[BLK:]))
                dq = dq * SCALE
                if not first:
                    dq = dq + dqp_ref[rows, :]
                dq_ref[rows, :] = dq
                if b > 0:
                    flush(prev_rows, carry[0] + dk[0], carry[1] + dv[0])
                else:
                    dkx_ref[prev_rows, :] = dk[0]
                    dvx_ref[prev_rows, :] = dv[0]
                carry = (dk[1], dv[1])
            flush(_rows(r + p * (nb - 1), dil), *carry)

    cur, prv = _zcur(w), _zprev(p, nb)
    b_spec = pl.BlockSpec((None, 2, BLK, 2 * BLK), lambda hp, i: (hp, 0, 0, 0))
    in_specs = [cur(0), prv(4), cur(4), prv(8), cur(8), b_spec] + [_scur(w)] * 3
    args = [z, z, z, z, z, bias, dy, y, lse]
    if not first:
        in_specs += [_scur(w)] * 3
        args += list(prev)
    x_spec = pl.BlockSpec((p, BLK), lambda hp, i: (i, hp))
    *outs, db = pl.pallas_call(
        body, name=name, grid=(4, n_steps), in_specs=in_specs,
        out_specs=[_scur(w)] * 3 + [x_spec] * 2 + [b_spec],
        out_shape=[_sds((s, ATTN_W))] * 3 + [_sds((n_steps * p, ATTN_W))] * 2 + [_sds((4, 2, BLK, 2 * BLK))],
        compiler_params=_params(("parallel", "arbitrary")),
    )(*args)
    return (*outs, _unpair_bias_bwd(db))


ASM_ROWS = 512


def _assemble_dz(dq, dk, dv, extras, dzs, du, name):
    s = dq.shape[0]
    w = min(ATT_ROWS, s)
    n_steps = s // w
    per_step = w // ASM_ROWS
    assert w % ASM_ROWS == 0

    def body(*refs):
        dq_ref, dk_ref, dv_ref, dzs_ref, du_ref = refs[:5]
        x_refs = refs[5:5 + 2 * len(extras)]
        o_ref, acc_ref = refs[-2:]
        j = pl.program_id(0)
        step = j // per_step
        has_next = (step < n_steps - 1).astype(F32)
        last_of_step = ((j + 1) % per_step == 0).astype(F32)
        o_ref[:, 0:ATTN_W] = dq_ref[...].astype(BF16)
        o_ref[:, 3 * ATTN_W:3 * ATTN_W + 2 * SGU_W] = dzs_ref[...].astype(BF16)
        o_ref[:, 3 * ATTN_W + 2 * SGU_W:IN_W] = du_ref[...].astype(BF16)
        for part, (base_ref, col) in enumerate(((dk_ref, ATTN_W), (dv_ref, 2 * ATTN_W))):
            acc_ref[...] = base_ref[...]
            for n, (_, dil) in enumerate(BRANCHES):
                rows = min(BLK * dil, ASM_ROWS)
                scale = has_next if BLK * dil >= w else has_next * last_of_step
                acc_ref[ASM_ROWS - rows:, :] += x_refs[2 * n + part][...] * scale
            o_ref[:, col:col + ATTN_W] = acc_ref[...].astype(BF16)

    def x_spec(dil):
        p = BLK * dil
        rows = min(p, ASM_ROWS)
        blocks_per_step = p // rows
        total = n_steps * blocks_per_step

        def idx(j):
            step = j // per_step
            within = (j % per_step) - (per_step - blocks_per_step)
            return (jnp.clip((step + 1) * blocks_per_step + jnp.maximum(within, 0), 0, total - 1), 0)

        return pl.BlockSpec((rows, ATTN_W), idx)

    in_specs = [_rb(ASM_ROWS, ATTN_W)] * 3 + [_rb(ASM_ROWS, 2 * SGU_W), _rb(ASM_ROWS, SSM_W)]
    args = [dq, dk, dv, dzs, du]
    for (dkx, dvx), (_, dil) in zip(extras, BRANCHES):
        in_specs += [x_spec(dil)] * 2
        args += [dkx, dvx]
    return pl.pallas_call(
        body, name=name, grid=(s // ASM_ROWS,), in_specs=in_specs, out_specs=_rb(ASM_ROWS, IN_W),
        out_shape=_sds((s, IN_W), BF16), scratch_shapes=[pltpu.VMEM((ASM_ROWS, ATTN_W), F32)],
        compiler_params=_params(("parallel",)),
    )(*args)


def _t5_bucket(dist):
    max_exact = N_BUCKETS // 2
    d = np.maximum(dist, 0)
    large = max_exact + (np.log(np.maximum(d, 1) / max_exact) / np.log(REL_MAX / max_exact)
                         * (N_BUCKETS - max_exact)).astype(np.int32)
    large = np.minimum(large, N_BUCKETS - 1)
    return np.where(d < max_exact, d, large).astype(np.int32)


def _bias_tables(rel_bias):
    period = 3 * BLK
    tabs = []
    for _, dil in BRANCHES:
        onehot = np.zeros((period, N_BUCKETS), np.float32)
        d = np.arange(BLK + 1)
        onehot[d, _t5_bucket((BLK - d) * dil)] = 1.0
        f = jnp.dot(jnp.asarray(onehot), rel_bias, precision=lax.Precision.HIGHEST)
        flat = jnp.tile(f.T, (1, BLK))[:, :BLK * (period - 1)]
        tab = flat.reshape(N_HEADS, BLK, period - 1)[:, :, :2 * BLK]
        tabs.append(jnp.where(_band()[None], tab, NEG_INF))
    return tabs


def _bucket_onehot():
    maps = []
    q = np.arange(BLK)[:, None]
    k = np.arange(2 * BLK)[None, :]
    rel = q + BLK - k
    for _, dil in BRANCHES:
        maps.append(np.where((rel >= 0) & (rel <= BLK), _t5_bucket(rel * dil), -1).reshape(-1))
    bmap = jnp.asarray(np.concatenate(maps).astype(np.int32))
    return (bmap[:, None] == jnp.arange(128, dtype=jnp.int32)[None, :]).astype(BF16)


def _block_diag(t):
    g, n, c = t.shape
    eye = jnp.eye(g, dtype=t.dtype)
    return (t[:, :, None, :] * eye[:, None, :, None]).reshape(g * n, g * c)


def _ssm_prep(a_re, a_im, log_dt, b_re, b_im, c_re, c_im):
    lam = lax.complex(a_re, a_im)
    dt = jnp.exp(log_dt)[:, None]
    a_bar = jnp.exp(lam * dt)
    b_bar = ((a_bar - 1.0) / lam)[:, :, None] * lax.complex(b_re, b_im)
    bdt = jnp.concatenate([_block_diag(jnp.real(b_bar)), _block_diag(jnp.imag(b_bar))], axis=0)
    cd = jnp.concatenate([_block_diag(jnp.transpose(c_re, (0, 2, 1))),
                          _block_diag(-jnp.transpose(c_im, (0, 2, 1)))], axis=0)
    return jnp.real(a_bar).reshape(-1), jnp.imag(a_bar).reshape(-1), bdt, cd


def _powers(ar, ai):
    pr, pi = ar[:, None], ai[:, None]
    k = 1
    while k < 8:
        lr, li = pr[:, -1:], pi[:, -1:]
        pr, pi = (jnp.concatenate([pr, pr * lr - pi * li], axis=1),
                  jnp.concatenate([pi, pr * li + pi * lr], axis=1))
        k *= 2
    return pr, pi


def _sgu_bias_expand(b):
    return jnp.repeat(b.T, 64, axis=1)


def _layer_fwd(i, h, a1, p_i, big, small, bias_tabs, next_gain):
    nm = "l%d_" % i
    sv = {"h": h}
    if a1 is None:
        a1 = _rms_fwd(h, small["norm_attn_g"][i], nm + "rms_attn")
    z = _mm(a1, big["w_in"], "nt", nm + "in_proj")
    st = None
    for b, (_, dil) in enumerate(BRANCHES):
        st = _attn_fwd(z, bias_tabs[b][0], st, dil, b == 0, b == len(BRANCHES) - 1, nm + "attn_fwd%d" % b)
    y_attn, lse = st
    bexp = _sgu_bias_expand(small["sgu_b"][i])
    y_sgu = _sgu_fwd(z, small["sgu_ln_g"][i], small["sgu_ln_b"][i], small["sgu_w"][i], bexp, nm + "sgu_fwd")
    ar, ai, bdt, cd = _ssm_prep(*[small[k][i] for k in ("ssm_a_re", "ssm_a_im", "ssm_log_dt", "ssm_b_re",
                                                         "ssm_b_im", "ssm_c_re", "ssm_c_im")])
    xr, xi, yc = _ssm_fwd(z, bdt.astype(BF16), cd.astype(BF16), _scan_tables(*_powers(ar, ai), False),
                          nm + "ssm_core")
    y_ssm = _ssm_post_fwd(yc, z, small["ssm_d"][i], big["ssm_glu_w"], small["ssm_glu_b"][i], nm + "ssm_post")
    mix = _mix_fwd(y_attn, y_sgu, y_ssm, small["branch_norm_g"][i], nm + "mix")
    if "rest" in big:
        big = dict({k: t for k, t in big.items() if k != "rest"}, **big["rest"](mix))
    h2, a2 = _mm(mix, big["w_out"], "nn", nm + "out_proj", add=h, norm_gain=small["norm_ffn_g"][i])
    hu = _mm(a2, big["ffn_w_up"], "nt", nm + "ffn_up", out_dtype=BF16)
    hv, hg, act = _conv_fwd(hu, big["ffn_conv_w"], small["ffn_conv_b"][i], nm + "ffn_conv")
    h3, a3 = _mm(act, big["ffn_w_down"], "nn", nm + "ffn_down", add=h2, norm_gain=small["norm_ple_g"][i])
    gp = _mm(a3, big["ple_w_gate"], "nn", nm + "ple_gate", out_dtype=BF16)
    pp = _mm(p_i, big["ple_w_proj"], "nt", nm + "ple_proj", out_dtype=BF16)
    h4, a_next = _ple_fwd(h3, gp, pp, next_gain, nm + "ple_add")
    sv.update(big=big, a1=a1, z=z, y_attn=y_attn, lse=lse, y_sgu=y_sgu, y_ssm=y_ssm, yc=yc, xr=xr, xi=xi, mix=mix, h2=h2,
              a2=a2, hu=hu, hv=hv, hg=hg, act=act, h3=h3, a3=a3, gp=gp, pp=pp)
    return h4, a_next, sv


def _layer_bwd(i, dh4, sv, p_i, big, small, bias_tabs, ffn_done=None):
    nm = "l%d_" % i
    g = {}
    dpp, dgp = _ple_bwd(dh4, sv["gp"], sv["pp"], nm + "ple_bwd")
    g["ple_w_proj"] = _mm(dpp, p_i, "tn", nm + "d_ple_proj", out_dtype=BF16)
    g["ple_w_gate"] = _mm(sv["a3"], dgp, "tn", nm + "d_ple_gate", out_dtype=BF16)
    dh3, dgain = _mm(dgp, big["ple_w_gate"], "nt", nm + "ple_gate_t", add=dh4,
                     norm_bwd=(sv["h3"], small["norm_ple_g"][i]))
    g["norm_ple_g"] = dgain.reshape(D_MODEL)
    g["ffn_w_down"] = _mm(sv["act"], dh3, "tn", nm + "d_ffn_down", out_dtype=BF16)
    dact = _mm(dh3, big["ffn_w_down"], "nt", nm + "ffn_down_t", out_dtype=BF16)
    dhu, g["ffn_conv_w"], dcb = _conv_bwd(dact, sv["hv"], sv["hg"], sv["hu"], big["ffn_conv_w"],
                                          nm + "ffn_conv_bwd")
    g["ffn_conv_b"] = dcb.reshape(2 * D_FF)
    g["ffn_w_up"] = _mm(dhu, sv["a2"], "tn", nm + "d_ffn_up", out_dtype=BF16)
    dh2, dgain = _mm(dhu, big["ffn_w_up"], "nn", nm + "ffn_up_t", add=dh3,
                     norm_bwd=(sv["h2"], small["norm_ffn_g"][i]))
    g["norm_ffn_g"] = dgain.reshape(D_MODEL)
    if ffn_done is not None:
        small = ffn_done(g, small)
    g["w_out"] = _mm(sv["mix"], dh2, "tn", nm + "d_out_proj", out_dtype=BF16)
    dmix = _mm(dh2, big["w_out"], "nt", nm + "out_proj_t")
    dya, dysg, dyss, g["branch_norm_g"] = _mix_bwd(dmix, sv["y_attn"], sv["y_sgu"], sv["y_ssm"],
                                                   small["branch_norm_g"][i], nm + "mix_bwd")
    ssm_keys = ("ssm_a_re", "ssm_a_im", "ssm_log_dt", "ssm_b_re", "ssm_b_im", "ssm_c_re", "ssm_c_im")
    (ar, ai, bdt, cd), prep_vjp = jax.vjp(_ssm_prep, *[small[k][i] for k in ssm_keys])
    dy1, dgl, y2, dud, g["ssm_d"], g["ssm_glu_b"] = _ssm_post_bwd(
        dyss, sv["yc"], sv["z"], small["ssm_d"][i], big["ssm_glu_w"], small["ssm_glu_b"][i], nm + "ssm_post_bwd")
    g["ssm_glu_w"] = _mm(y2, dgl, "tn", nm + "d_ssm_glu", out_dtype=BF16)
    du, dbdt, dcd, dar, dai = _ssm_bwd(dy1, dud, sv["z"], sv["xr"], sv["xi"], bdt.astype(BF16), cd.astype(BF16),
                                       _scan_tables(*_powers(ar, ai), True), nm + "ssm_core_bwd")
    for k, val in zip(ssm_keys, prep_vjp((dar, dai, dbdt, dcd))):
        g[k] = val
    bexp, bexp_vjp = jax.vjp(_sgu_bias_expand, small["sgu_b"][i])
    dzs, g["sgu_w"], dbexp, g["sgu_ln_g"], g["sgu_ln_b"] = _sgu_bwd(
        sv["z"], dysg, small["sgu_ln_g"][i], small["sgu_ln_b"][i], small["sgu_w"][i], bexp, nm + "sgu_bwd")
    g["sgu_b"] = bexp_vjp(dbexp)[0]
    prev = None
    dbs, extras = [], []
    for b, (_, dil) in enumerate(BRANCHES):
        dq, dk, dv, dkx, dvx, db = _attn_bwd(sv["z"], bias_tabs[b][1], dya, sv["y_attn"], sv["lse"], prev, dil,
                                             nm + "attn_bwd%d" % b)
        prev = (dq, dk, dv)
        extras.append((dkx, dvx))
        dbs.append(db.reshape(N_HEADS, BLK * 2 * BLK))
    dz = _assemble_dz(dq, dk, dv, extras, dzs, du, nm + "assemble_dz")
    g["w_in"] = _mm(dz, sv["a1"], "tn", nm + "d_in_proj", out_dtype=BF16)
    dh, dgain = _mm(dz, big["w_in"], "nn", nm + "in_proj_t", add=dh2, norm_bwd=(sv["h"], small["norm_attn_g"][i]))
    g["norm_attn_g"] = dgain.reshape(D_MODEL)
    return dh, g, jnp.concatenate(dbs, axis=1)


def _local_step(x, p, target, layer_weights, small, layer_done=None):
    depth = p.shape[0]
    bias_tabs = [(t, _pair_bias_bwd(t)) for t in _bias_tables(small["rel_bias"])]
    h, a1 = x, None
    saved = []
    for i in range(depth):
        next_gain = small["norm_attn_g"][i + 1] if i + 1 < depth else None
        h, a1, sv = _layer_fwd(i, h, a1, p[i], layer_weights(i, h), small, bias_tabs, next_gain)
        saved.append(sv)
    dh, loss, g_final = _loss_head(h, target, small["final_norm_g"], "loss_head")
    layer_grads = [None] * depth
    dbias = [None] * depth
    for i in reversed(range(depth)):
        ffn_done = None if layer_done is None else (lambda g, sm, i=i: layer_done(i, "ffn", g, sm))
        dh, layer_grads[i], dbias[i] = _layer_bwd(i, dh, saved[i], p[i], saved[i]["big"], small, bias_tabs,
                                                  ffn_done)
        if layer_done is not None:
            small = layer_done(i, "all", layer_grads[i], small)
    big_grads = [{k: lg.pop(k) for k in COMM_NAMES} for lg in layer_grads]
    grads = {k: jnp.stack([layer_grads[i][k] for i in range(depth)]) for k in layer_grads[0]}
    grads["final_norm_g"] = g_final
    g_rb = _mm(sum(dbias[1:], dbias[0]), _bucket_onehot(), "nn", "d_rel_bias", tk=2048)
    grads["rel_bias"] = g_rb[:, :N_BUCKETS].T
    return loss, dh, big_grads, grads


_ANY = pl.BlockSpec(memory_space=pl.ANY)
MESH_IDS = pl.DeviceIdType.MESH


def _slot(ref, axis, j):
    return ref.at[(slice(None),) * axis + (j,)]


def _all_gather(blocks, axis, name):
    nt = len(blocks)

    def body(*refs):
        x_refs, o_refs = refs[:nt], refs[nt:2 * nt]
        send_sems, recv_sems, local_sems = refs[2 * nt:]
        x, y, c = lax.axis_index("x"), lax.axis_index("y"), lax.axis_index("c")
        me, sibling = (x, y, c), (x, y, 1 - c)
        chips = [(1 - x, y), (x, 1 - y), (1 - x, 1 - y)]

        def slot(t, px, py, pc):
            return _slot(o_refs[t], axis, 4 * px + 2 * py + pc)

        def copy(t, k, blk, to, src=None):
            return pltpu.make_async_remote_copy(
                src_ref=slot(t, *blk) if src is None else src, dst_ref=slot(t, *blk),
                send_sem=send_sems.at[7 * t + k], recv_sem=recv_sems.at[7 * t + k],
                device_id=to, device_id_type=MESH_IDS)

        mine = [pltpu.make_async_copy(x_refs[t], slot(t, *me), local_sems.at[t]) for t in range(nt)]
        for cp in mine:
            cp.start()
        first = []
        for t in range(nt):
            first.append(copy(t, 0, me, sibling, src=x_refs[t]))
            first += [copy(t, 1 + j, me, (*chip, c), src=x_refs[t]) for j, chip in enumerate(chips)]
        for cp in first:
            cp.start()
        passed = []
        for t in range(nt):
            for j, chip in enumerate(chips):
                copy(t, 1 + j, (*chip, c), me).wait_recv()
                passed.append(copy(t, 4 + j, (*chip, c), sibling))
                passed[-1].start()
        for t in range(nt):
            copy(t, 0, sibling, me).wait_recv()
            for j, chip in enumerate(chips):
                copy(t, 4 + j, (*chip, 1 - c), me).wait_recv()
        for cp in first + passed:
            cp.wait_send()
        for cp in mine:
            cp.wait()

    out_shape = [jax.ShapeDtypeStruct(b.shape[:axis] + (N_DEV,) + b.shape[axis:], b.dtype) for b in blocks]
    return pl.pallas_call(
        body, name=name, out_shape=out_shape, in_specs=[_ANY] * nt, out_specs=[_ANY] * nt,
        scratch_shapes=[pltpu.SemaphoreType.DMA((7 * nt,)), pltpu.SemaphoreType.DMA((7 * nt,)),
                        pltpu.SemaphoreType.DMA((nt,))],
    )(*blocks)


def _peer(k):
    x, y, c = lax.axis_index("x"), lax.axis_index("y"), lax.axis_index("c")
    px = 1 - x if k & 4 else x
    py = 1 - y if k & 2 else y
    pc = 1 - c if k & 1 else c
    return (px, py, pc), 4 * px + 2 * py + pc


def _all_to_all(blocks, name):
    nt = len(blocks)

    def body(*refs):
        x_refs, o_refs = refs[:nt], refs[nt:2 * nt]
        send_sems, recv_sems, local_sems = refs[2 * nt:]
        _, me = _peer(0)
        mine = [pltpu.make_async_copy(x_refs[t].at[me], o_refs[t].at[me], local_sems.at[t]) for t in range(nt)]
        for cp in mine:
            cp.start()
        copies = []
        for k in range(1, N_DEV):
            peer, idx = _peer(k)
            for t in range(nt):
                cp = pltpu.make_async_remote_copy(
                    src_ref=x_refs[t].at[idx], dst_ref=o_refs[t].at[me],
                    send_sem=send_sems.at[7 * t + k - 1], recv_sem=recv_sems.at[7 * t + k - 1],
                    device_id=peer, device_id_type=MESH_IDS)
                cp.start()
                copies.append(cp)
        for cp in copies:
            cp.wait()
        for cp in mine:
            cp.wait()

    return pl.pallas_call(
        body, name=name, out_shape=[jax.ShapeDtypeStruct(b.shape, b.dtype) for b in blocks],
        in_specs=[_ANY] * nt, out_specs=[_ANY] * nt,
        scratch_shapes=[pltpu.SemaphoreType.DMA((7 * nt,)), pltpu.SemaphoreType.DMA((7 * nt,)),
                        pltpu.SemaphoreType.DMA((nt,))],
    )(*blocks)


_HBM = pl.BlockSpec(memory_space=pltpu.HBM)
_SEM = pl.BlockSpec(memory_space=pltpu.SEMAPHORE)
_EFFECT = pltpu.SideEffectType.DATAFLOW_SIDE_EFFECTING


def _split_copy(src_ref, land_ref, send_sems, recv_sems, t, k, gather):
    peer, idx = _peer(k)
    _, me = _peer(0)
    return pltpu.make_async_remote_copy(
        src_ref=src_ref if gather else src_ref.at[idx], dst_ref=land_ref.at[me],
        send_sem=send_sems.at[7 * t + k - 1], recv_sem=recv_sems.at[7 * t + k - 1],
        device_id=peer, device_id_type=MESH_IDS)


def _exchange_start(srcs, lands, gather, name):
    nt = len(srcs)

    def body(*refs):
        src_refs, land_refs = refs[:nt], refs[nt:2 * nt]
        send_sems, recv_sems = refs[2 * nt:2 * nt + 2]
        token = refs[-1]
        for k in range(1, N_DEV):
            for t in range(nt):
                _split_copy(src_refs[t], land_refs[t], send_sems, recv_sems, t, k, gather).start()
        token[...] = jnp.zeros_like(token)

    hbm = lambda a: pltpu.HBM(a.shape, a.dtype)
    outs = pl.pallas_call(
        body, name=name,
        out_shape=(pltpu.SemaphoreType.DMA((7 * nt,)), pltpu.SemaphoreType.DMA((7 * nt,)),
                   *[hbm(a) for a in srcs], *[hbm(a) for a in lands], jax.ShapeDtypeStruct((8, 128), F32)),
        in_specs=[_HBM] * (2 * nt),
        out_specs=(_SEM, _SEM, *[_HBM] * (2 * nt), pl.BlockSpec(memory_space=pltpu.VMEM)),
        input_output_aliases={j: 2 + j for j in range(2 * nt)},
        compiler_params=pltpu.CompilerParams(has_side_effects=_EFFECT),
    )(*[pltpu.with_memory_space_constraint(a, pltpu.HBM) for a in list(srcs) + list(lands)])
    return outs[0], outs[1], outs[2:2 + nt], outs[2 + nt:2 + 2 * nt], outs[-1]


def _exchange_wait(send_sems, recv_sems, srcs, lands, after, gather, name):
    nt = len(srcs)

    def body(*refs):
        src_refs, land_refs = refs[:nt], refs[nt:2 * nt]
        send_sems, recv_sems = refs[2 * nt:2 * nt + 2]
        for k in range(1, N_DEV):
            _, idx = _peer(k)
            for t in range(nt):
                _split_copy(src_refs[t], land_refs[t], send_sems, recv_sems, t, k, gather).wait_send()
                arrival = pltpu.make_async_remote_copy(
                    src_ref=land_refs[t].at[idx], dst_ref=land_refs[t].at[idx],
                    send_sem=send_sems.at[7 * t + k - 1], recv_sem=recv_sems.at[7 * t + k - 1],
                    device_id=_peer(k)[0], device_id_type=MESH_IDS)
                arrival.wait_recv()

    hbm = lambda a: pltpu.HBM(a.shape, a.dtype)
    outs = pl.pallas_call(
        body, name=name, out_shape=tuple(hbm(a) for a in list(srcs) + list(lands)),
        in_specs=[_HBM] * (2 * nt) + [_SEM, _SEM, _ANY], out_specs=tuple([_HBM] * (2 * nt)),
        input_output_aliases={j: j for j in range(2 * nt)},
        compiler_params=pltpu.CompilerParams(has_side_effects=_EFFECT),
    )(*srcs, *lands, send_sems, recv_sems, after)
    return outs[nt:]


def _adamw(parts, w, m, v, name, tr):
    n_layers, r, c_ = w.shape
    assert len(parts) == n_layers

    def body(*refs):
        p_refs = refs[:n_layers]
        w_ref, m_ref, v_ref, g_ref, d_ref, mo_ref, vo_ref = refs[n_layers:]

        def update(p_ref):
            g = p_ref[0].astype(F32)
            for j in range(1, N_DEV):
                g = g + p_ref[j].astype(F32)
            m2 = ADAM_B1 * m_ref[...] + (1.0 - ADAM_B1) * g
            v2 = ADAM_B2 * v_ref[...] + (1.0 - ADAM_B2) * (g * g)
            m_hat = m2 / (1.0 - ADAM_B1 ** ADAM_STEP)
            v_hat = v2 / (1.0 - ADAM_B2 ** ADAM_STEP)
            g_ref[...] = g
            d_ref[...] = -ADAM_LR * (m_hat / (jnp.sqrt(v_hat) + ADAM_EPS) + ADAM_WD * w_ref[...])
            mo_ref[...] = m2
            vo_ref[...] = v2

        for layer in range(n_layers):
            pl.when(pl.program_id(0) == layer)(lambda layer=layer: update(p_refs[layer]))

    spec = pl.BlockSpec((None, tr, c_), lambda l, i: (l, i, 0))
    p_spec = pl.BlockSpec((N_DEV, tr, c_), lambda l, i: (0, i, 0))
    return pl.pallas_call(
        body, name=name, grid=(n_layers, r // tr), in_specs=[p_spec] * n_layers + [spec] * 3,
        out_specs=[spec] * 4, out_shape=[_sds((n_layers, r, c_))] * 4,
        compiler_params=_params(("parallel", "parallel")),
    )(*parts, w, m, v)


def _pack_rows(n_elems, align):
    rows = -(-n_elems // PACK_COLS)
    return -(-rows // align) * align


def _pack(arrs, rows, dtype=F32):
    flat = jnp.concatenate([a.reshape(-1) for a in arrs]).astype(dtype)
    return jnp.pad(flat, (0, rows * PACK_COLS - flat.shape[0])).reshape(rows, PACK_COLS)


def _unpack(pack, shapes):
    flat = pack.reshape(-1)
    out, off = [], 0
    for shp in shapes:
        size = int(np.prod(shp))
        out.append(flat[off:off + size].reshape(shp))
        off += size
    return out


def _tile_rows(rows, target, align=16):
    best = align
    for t in range(align, target + 1, align):
        if rows % t == 0:
            best = t
    return best


COMM_NAMES = ("w_in", "ssm_glu_w", "w_out", "ffn_w_up", "ffn_w_down", "ple_w_gate", "ple_w_proj")
COMM_TRANSPOSED = ("w_in", "ffn_w_up", "ple_w_proj")
COMM_EARLY = ("ple_w_proj", "ple_w_gate", "ffn_w_down", "ffn_w_up")
COMM_LATE = ("w_in", "ssm_glu_w", "w_out")
SMALL_TILE_ROWS = 64
CONV_NAME = "ffn_conv_w"


def _to_comm(name, a):
    return jnp.swapaxes(a, 1, 2) if name in COMM_TRANSPOSED else a


def kernel(x, p, rel_bias, norm_attn_g, w_in, sgu_ln_g, sgu_ln_b, sgu_w, sgu_b, ssm_a_re, ssm_a_im, ssm_log_dt, ssm_b_re, ssm_b_im, ssm_c_re, ssm_c_im, ssm_d, ssm_glu_w, ssm_glu_b, branch_norm_g, w_out, norm_ffn_g, ffn_w_up, ffn_conv_w, ffn_conv_b, ffn_w_down, norm_ple_g, ple_w_gate, ple_w_proj, final_norm_g, loss_target, m_rel_bias, m_norm_attn_g, m_w_in, m_sgu_ln_g, m_sgu_ln_b, m_sgu_w, m_sgu_b, m_ssm_a_re, m_ssm_a_im, m_ssm_log_dt, m_ssm_b_re, m_ssm_b_im, m_ssm_c_re, m_ssm_c_im, m_ssm_d, m_ssm_glu_w, m_ssm_glu_b, m_branch_norm_g, m_w_out, m_norm_ffn_g, m_ffn_w_up, m_ffn_conv_w, m_ffn_conv_b, m_ffn_w_down, m_norm_ple_g, m_ple_w_gate, m_ple_w_proj, m_final_norm_g, v_rel_bias, v_norm_attn_g, v_w_in, v_sgu_ln_g, v_sgu_ln_b, v_sgu_w, v_sgu_b, v_ssm_a_re, v_ssm_a_im, v_ssm_log_dt, v_ssm_b_re, v_ssm_b_im, v_ssm_c_re, v_ssm_c_im, v_ssm_d, v_ssm_glu_w, v_ssm_glu_b, v_branch_norm_g, v_w_out, v_norm_ffn_g, v_ffn_w_up, v_ffn_conv_w, v_ffn_conv_b, v_ffn_w_down, v_norm_ple_g, v_ple_w_gate, v_ple_w_proj, v_final_norm_g):
    given = dict(locals())
    w = {n: given[n] for n in WEIGHT_NAMES}
    m = {n: given["m_" + n] for n in WEIGHT_NAMES}
    v = {n: given["v_" + n] for n in WEIGHT_NAMES}
    depth = p.shape[0]
    dev = 4 * lax.axis_index("x") + 2 * lax.axis_index("y") + lax.axis_index("c")

    wc = {n: _to_comm(n, w[n]) for n in COMM_NAMES}
    wb = {n: wc[n].astype(BF16) for n in COMM_NAMES}
    conv_local = [w[CONV_NAME], m[CONV_NAME], v[CONV_NAME]]
    conv_rows = _pack_rows(sum(int(np.prod(t.shape)) for t in conv_local), 8)
    conv_g, = _all_gather([_pack(conv_local, conv_rows)], 0, "gather_conv_taps")
    conv_parts = zip(*[_unpack(conv_g[j], [t.shape for t in conv_local]) for j in range(N_DEV)])
    conv_w, conv_m, conv_v = [jnp.concatenate(parts, axis=2) for parts in conv_parts]
    small = {n: w[n] for n in SMALL_NAMES}

    def whole(names, blocks):
        return {n: t.reshape(-1, t.shape[-1]) for n, t in zip(names, blocks)}

    def own_slot(block):
        return lax.dynamic_update_slice_in_dim(jnp.zeros((N_DEV,) + block.shape, block.dtype), block[None], dev, 0)

    def start_gather(names, i, after):
        srcs, after = lax.optimization_barrier(([wb[n][i] for n in names], after))
        return _exchange_start(srcs, [own_slot(s) for s in srcs], True, "gather_weights_%d_start" % i), after

    def wait_gather(names, i, started, after):
        send_sems, recv_sems, srcs, lands, _ = started
        return whole(names, _exchange_wait(send_sems, recv_sems, srcs, lands, after, True,
                                           "gather_weights_%d_wait" % i))

    at_once = ("w_in", "ssm_glu_w")
    later = tuple(n for n in COMM_NAMES if n not in at_once)
    w_in_0 = _all_gather([wb[n][0] for n in at_once], 0, "gather_w_in_0")
    gathering = {}
    gathering[0], w_in_0 = start_gather(later, 0, w_in_0)
    small["norm_attn_g"] = small["norm_attn_g"] + gathering[0][4][0, 0]

    def layer_weights(i, h):
        if i > 0:
            got = wait_gather(COMM_NAMES, i, gathering.pop(i), h)
            if i + 1 < depth:
                gathering[i + 1], ordered = start_gather(COMM_NAMES, i + 1, got["w_in"])
                got["w_in"] = ordered + gathering[i + 1][4][0, 0].astype(BF16)
            return dict(got, **{CONV_NAME: conv_w[i]})

        def rest(after):
            got = wait_gather(later, 0, gathering.pop(0), after)
            if depth > 1:
                gathering[1], ordered = start_gather(COMM_NAMES, 1, got["w_out"])
                got["w_out"] = ordered + gathering[1][4][0, 0].astype(BF16)
            return got

        return dict(whole(at_once, w_in_0), **{CONV_NAME: conv_w[0], "rest": rest})

    def as_slots(g, n):
        return g.reshape((N_DEV,) + wc[n].shape[1:])

    scattering = {}

    def layer_done(i, stage, g, small_now):
        if stage == "all" and i == 0:
            return small_now
        names = COMM_EARLY if stage == "ffn" else COMM_LATE
        srcs = [as_slots(g[n], n) for n in names]
        lands = [own_slot(lax.dynamic_index_in_dim(s, dev, 0, keepdims=False)) for s in srcs]
        started = _exchange_start(srcs, lands, False, "scatter_weight_grads_%d_%s_start" % (i, stage))
        scattering[i, stage] = (names, started)
        pin = "branch_norm_g" if stage == "ffn" else "norm_ple_g"
        return dict(small_now, **{pin: small_now[pin] + started[4][0, 0]})

    loss, dx, big_grads, grads = _local_step(x[0], p[:, 0], loss_target[0], layer_weights, small, layer_done)
    loss = lax.psum(loss, ("x", "y", "c"))

    recv = [{} for _ in range(depth)]
    last = _all_to_all([as_slots(big_grads[0][n], n) for n in COMM_LATE], "scatter_weight_grads_0_all")
    recv[0].update(zip(COMM_LATE, last))
    for (i, stage), (names, (send_sems, recv_sems, srcs, lands, _)) in scattering.items():
        got = _exchange_wait(send_sems, recv_sems, srcs, lands, dx, False,
                             "scatter_weight_grads_%d_%s_wait" % (i, stage))
        recv[i].update(zip(names, got))
    rep_names = SMALL_NAMES + (CONV_NAME,)
    rep_w = dict({n: w[n] for n in SMALL_NAMES}, **{CONV_NAME: conv_w})
    rep_m = dict({n: m[n] for n in SMALL_NAMES}, **{CONV_NAME: conv_m})
    rep_v = dict({n: v[n] for n in SMALL_NAMES}, **{CONV_NAME: conv_v})
    rep_shapes = [rep_w[n].shape for n in rep_names]
    rep_rows = _pack_rows(sum(int(np.prod(s)) for s in rep_shapes), SMALL_TILE_ROWS)
    rep_parts, = _all_gather([_pack([grads[n] for n in rep_names], rep_rows)], 0, "gather_small_grads")

    out = {}
    for n in COMM_NAMES:
        res = _adamw([recv[i][n] for i in range(depth)], wc[n], _to_comm(n, m[n]), _to_comm(n, v[n]),
                     "adamw_" + n, _tile_rows(wc[n].shape[1], 256))
        out[n] = [_to_comm(n, r) for r in res]
    rep_out = _adamw([rep_parts], *[_pack([src[n] for n in rep_names], rep_rows)[None] for src in (rep_w, rep_m, rep_v)],
                     "adamw_replicated", SMALL_TILE_ROWS)
    for n, vals in zip(rep_names, zip(*[_unpack(r[0], rep_shapes) for r in rep_out])):
        out[n] = list(vals)
    shard = ffn_conv_w.shape[2]
    out[CONV_NAME] = [lax.dynamic_slice_in_dim(t, dev * shard, shard, axis=2) for t in out[CONV_NAME]]
    results = [[out[n][kind] for n in WEIGHT_NAMES] for kind in range(4)]
    return (loss, dx[None], *results[0], *results[1], *results[2], *results[3])
```

```python
import math

import numpy as np
import jax
import jax.numpy as jnp
from jax import lax
from jax.experimental import pallas as pl
from jax.experimental.pallas import tpu as pltpu

F32 = jnp.float32
BF16 = jnp.bfloat16

D_MODEL = 1024
HEAD_DIM = 64
N_HEADS = 8
ATTN_W = 512
SGU_W = 256
SGU_GROUPS = 4
SGU_CHUNK = 128
SSM_W = 256
SSM_GROUPS = 16
SSM_CH = 16
SSM_STATE = 64
SSM_NS = SSM_GROUPS * SSM_STATE
IN_W = 2304
D_FF = 2816
PLE_DIM = 256
BRANCHES = ((128, 1), (512, 4), (2048, 16))
BLK = 128
N_BUCKETS = 32
REL_MAX = 2048
EPS = 1e-6
NEG_INF = -1e30
N_DEV = 8

ADAM_LR = 0.001
ADAM_B1 = 0.9
ADAM_B2 = 0.999
ADAM_EPS = 1e-08
ADAM_WD = 0.01
ADAM_STEP = 10

VMEM_LIMIT_BYTES = 56 * 1024 * 1024
GELU_C = math.sqrt(2.0 / math.pi)

SMALL_NAMES = ("rel_bias", "norm_attn_g", "sgu_ln_g", "sgu_ln_b", "sgu_w", "sgu_b", "ssm_a_re", "ssm_a_im",
               "ssm_log_dt", "ssm_b_re", "ssm_b_im", "ssm_c_re", "ssm_c_im", "ssm_d", "ssm_glu_b",
               "branch_norm_g", "norm_ffn_g", "ffn_conv_b", "norm_ple_g", "final_norm_g")
WEIGHT_NAMES = ("rel_bias", "norm_attn_g", "w_in", "sgu_ln_g", "sgu_ln_b", "sgu_w", "sgu_b", "ssm_a_re",
                "ssm_a_im", "ssm_log_dt", "ssm_b_re", "ssm_b_im", "ssm_c_re", "ssm_c_im", "ssm_d", "ssm_glu_w",
                "ssm_glu_b", "branch_norm_g", "w_out", "norm_ffn_g", "ffn_w_up", "ffn_conv_w", "ffn_conv_b",
                "ffn_w_down", "norm_ple_g", "ple_w_gate", "ple_w_proj", "final_norm_g")
PACK_COLS = 512


def _params(sem):
    return pltpu.CompilerParams(dimension_semantics=sem, vmem_limit_bytes=VMEM_LIMIT_BYTES)


def _pick(dim, target):
    if dim <= target:
        return dim
    best = None
    for t in range(128, target + 1, 128):
        if dim % t == 0:
            best = t
    return dim if best is None else best


def _gelu(x):
    return 0.5 * x * (1.0 + jnp.tanh(GELU_C * (x + 0.044715 * (x * x * x))))


def _gelu_grad(x):
    t = jnp.tanh(GELU_C * (x + 0.044715 * (x * x * x)))
    return 0.5 * (1.0 + t) + 0.5 * x * (1.0 - t * t) * (GELU_C * (1.0 + 3.0 * 0.044715 * (x * x)))


def _sigmoid(x):
    return 1.0 / (1.0 + jnp.exp(-x))


_DIMS = {"nn": (((1,), (0,)), ((), ())), "tn": (((0,), (0,)), ((), ())), "nt": (((1,), (1,)), ((), ()))}


def _mm(a, b, mode, name, add=None, out_dtype=F32, norm_gain=None, norm_bwd=None, tm=1408, tn=1408, tk=1408):
    if mode == "nn":
        m, k = a.shape
        k2, n = b.shape
    elif mode == "tn":
        k, m = a.shape
        k2, n = b.shape
    else:
        m, k = a.shape
        n, k2 = b.shape
    assert k == k2, (name, a.shape, b.shape, mode)
    tm, tn, tk = _pick(m, tm), _pick(n, tn), _pick(k, tk)
    nk = k // tk
    dims = _DIMS[mode]
    has_add = add is not None
    has_norm = norm_gain is not None
    has_nbwd = norm_bwd is not None
    assert not (has_norm or has_nbwd) or tn == n

    def body(*refs):
        a_ref, b_ref = refs[:2]
        rest = list(refs[2:])
        add_ref = rest.pop(0) if has_add else None
        g_ref = rest.pop(0) if has_norm else None
        h_ref, hg_ref = (rest.pop(0), rest.pop(0)) if has_nbwd else (None, None)
        o_ref = rest.pop(0)
        n_ref = rest.pop(0) if has_norm else None
        dg_ref = rest.pop(0) if has_nbwd else None
        part = lax.dot_general(a_ref[...].astype(BF16), b_ref[...].astype(BF16), dims,
                               preferred_element_type=F32)
        if has_nbwd:
            @pl.when((pl.program_id(0) == 0) & (pl.program_id(2) == 0))
            def _():
                dg_ref[...] = jnp.zeros_like(dg_ref)

        def finish(r):
            if has_nbwd:
                x = h_ref[...]
                scale = lax.rsqrt(jnp.mean(x * x, axis=-1, keepdims=True) + EPS)
                xh = x * scale
                dg_ref[...] += jnp.sum(r * xh, axis=0, keepdims=True)
                dxh = r * hg_ref[...]
                r = scale * (dxh - xh * jnp.mean(dxh * xh, axis=-1, keepdims=True))
            if has_add:
                r = r + add_ref[...]
            o_ref[...] = r.astype(out_dtype)
            if has_norm:
                scale = lax.rsqrt(jnp.mean(r * r, axis=-1, keepdims=True) + EPS)
                n_ref[...] = (r * scale * g_ref[...]).astype(BF16)

        if nk == 1:
            finish(part)
            return
        acc_ref = refs[-1]
        kk = pl.program_id(2)

        @pl.when(kk == 0)
        def _():
            acc_ref[...] = part

        @pl.when((kk > 0) & (kk < nk - 1))
        def _():
            acc_ref[...] += part

        @pl.when(kk == nk - 1)
        def _():
            finish(acc_ref[...] + part)

    if mode == "tn":
        a_spec = pl.BlockSpec((tk, tm), lambda i, j, kk: (kk, i))
    else:
        a_spec = pl.BlockSpec((tm, tk), lambda i, j, kk: (i, kk))
    if mode == "nt":
        b_spec = pl.BlockSpec((tn, tk), lambda i, j, kk: (j, kk))
    else:
        b_spec = pl.BlockSpec((tk, tn), lambda i, j, kk: (kk, j))
    o_spec = pl.BlockSpec((tm, tn), lambda i, j, kk: (i, j))
    in_specs = [a_spec, b_spec] + ([o_spec] if has_add else [])
    args = (a, b) + ((add,) if has_add else ())
    out_specs, out_shape = o_spec, jax.ShapeDtypeStruct((m, n), out_dtype)
    if has_norm:
        in_specs.append(pl.BlockSpec((1, n), lambda i, j, kk: (0, 0)))
        args += (norm_gain.reshape(1, n),)
        out_specs, out_shape = [o_spec, o_spec], [out_shape, jax.ShapeDtypeStruct((m, n), BF16)]
    if has_nbwd:
        row_spec = pl.BlockSpec((1, n), lambda i, j, kk: (0, 0))
        in_specs += [o_spec, row_spec]
        args += (norm_bwd[0], norm_bwd[1].reshape(1, n))
        out_specs, out_shape = [o_spec, row_spec], [out_shape, jax.ShapeDtypeStruct((1, n), F32)]
    sem = ("arbitrary",) * 3 if has_nbwd else ("parallel", "parallel", "arbitrary")
    return pl.pallas_call(
        body, name=name, grid=(m // tm, n // tn, nk),
        in_specs=in_specs, out_specs=out_specs, out_shape=out_shape,
        scratch_shapes=[pltpu.VMEM((tm, tn), F32)] if nk > 1 else [], compiler_params=_params(sem),
    )(*args)


def _rb(tm, w, cb=0):
    return pl.BlockSpec((tm, w), lambda i: (i, cb))


def _fb(shape):
    nd = len(shape)
    return pl.BlockSpec(shape, lambda i: (0,) * nd)


def _rowcall(body, name, n_rows, tm, in_specs, args, out_specs, out_shapes):
    return pl.pallas_call(
        body, name=name, grid=(n_rows // tm,), in_specs=in_specs, out_specs=out_specs, out_shape=out_shapes,
        compiler_params=_params(("arbitrary",)),
    )(*args)


def _sds(shape, dtype=F32):
    return jax.ShapeDtypeStruct(shape, dtype)


def _rms_fwd(h, g, name, tm=512):
    s, d = h.shape

    def body(h_ref, g_ref, o_ref):
        x = h_ref[...]
        r = lax.rsqrt(jnp.mean(x * x, axis=-1, keepdims=True) + EPS)
        o_ref[...] = (x * r * g_ref[...]).astype(BF16)

    return _rowcall(body, name, s, tm, [_rb(tm, d), _fb((1, d))], (h, g.reshape(1, d)), _rb(tm, d),
                    _sds((s, d), BF16))


def _loss_head(h, target, g, name, tm=512):
    s, d = h.shape

    def body(h_ref, t_ref, g_ref, dh_ref, loss_ref, dg_ref):
        @pl.when(pl.program_id(0) == 0)
        def _():
            dg_ref[...] = jnp.zeros_like(dg_ref)
            loss_ref[...] = jnp.zeros_like(loss_ref)

        x = h_ref[...]
        r = lax.rsqrt(jnp.mean(x * x, axis=-1, keepdims=True) + EPS)
        xh = x * r
        gg = g_ref[...]
        err = xh * gg - t_ref[...]
        loss_ref[...] += jnp.sum(err * err) * (0.5 / d)
        dy = err * (1.0 / d)
        dg_ref[...] += jnp.sum(dy * xh, axis=0, keepdims=True)
        dxh = dy * gg
        dh_ref[...] = r * (dxh - xh * jnp.mean(dxh * xh, axis=-1, keepdims=True))

    dh, loss, dg = _rowcall(body, name, s, tm, [_rb(tm, d), _rb(tm, d), _fb((1, d))], (h, target, g.reshape(1, d)),
                            [_rb(tm, d), _fb((1, 128)), _fb((1, d))], [_sds((s, d)), _sds((1, 128)), _sds((1, d))])
    return dh, loss[0, 0], dg.reshape(d)


_MIX_PARTS = ((0, 512), (512, 768), (768, 1024))


def _mix_fwd(ya, ysg, yss, g, name, tm=512):
    s = ya.shape[0]

    def body(a_ref, b_ref, c_ref, g_ref, o_ref):
        for ref, (lo, hi) in zip((a_ref, b_ref, c_ref), _MIX_PARTS):
            y = ref[...]
            r = lax.rsqrt(jnp.mean(y * y, axis=-1, keepdims=True) + EPS)
            o_ref[:, lo:hi] = (y * r * g_ref[:, lo:hi]).astype(BF16)

    return _rowcall(body, name, s, tm, [_rb(tm, 512), _rb(tm, 256), _rb(tm, 256), _fb((1, 1024))],
                    (ya, ysg, yss, g.reshape(1, 1024)), _rb(tm, 1024), _sds((s, 1024), BF16))


def _mix_bwd(dmix, ya, ysg, yss, g, name, tm=512):
    s = ya.shape[0]

    def body(dm_ref, a_ref, b_ref, c_ref, g_ref, da_ref, db_ref, dc_ref, dg_ref):
        @pl.when(pl.program_id(0) == 0)
        def _():
            dg_ref[...] = jnp.zeros_like(dg_ref)

        for ref, dref, (lo, hi) in zip((a_ref, b_ref, c_ref), (da_ref, db_ref, dc_ref), _MIX_PARTS):
            y = ref[...]
            r = lax.rsqrt(jnp.mean(y * y, axis=-1, keepdims=True) + EPS)
            xh = y * r
            dm = dm_ref[:, lo:hi]
            dg_ref[:, lo:hi] += jnp.sum(dm * xh, axis=0, keepdims=True)
            dxh = dm * g_ref[:, lo:hi]
            dref[...] = r * (dxh - xh * jnp.mean(dxh * xh, axis=-1, keepdims=True))

    da, db, dc, dg = _rowcall(
        body, name, s, tm, [_rb(tm, 1024), _rb(tm, 512), _rb(tm, 256), _rb(tm, 256), _fb((1, 1024))],
        (dmix, ya, ysg, yss, g.reshape(1, 1024)),
        [_rb(tm, 512), _rb(tm, 256), _rb(tm, 256), _fb((1, 1024))],
        [_sds((s, 512)), _sds((s, 256)), _sds((s, 256)), _sds((1, 1024))])
    return da, db, dc, dg.reshape(1024)


def _ssm_post_fwd(yc, z, d, gw, gb, name, tm=1024):
    s = yc.shape[0]

    def body(yc_ref, u_ref, d_ref, gw_ref, gb_ref, o_ref):
        y1 = yc_ref[...] + d_ref[...] * u_ref[...]
        y2 = _gelu(y1)
        gl = jnp.dot(y2.astype(BF16), gw_ref[...], preferred_element_type=F32) + gb_ref[...]
        o_ref[...] = y2 * _sigmoid(gl)

    return _rowcall(body, name, s, tm, [_rb(tm, 256), _rb(tm, 256, 8), _fb((1, 256)), _fb((256, 256)), _fb((1, 256))],
                    (yc, z, d.reshape(1, 256), gw, gb.reshape(1, 256)), _rb(tm, 256), _sds((s, 256)))


def _ssm_post_bwd(dy, yc, z, d, gw, gb, name, tm=1024):
    s = yc.shape[0]

    def body(dy_ref, yc_ref, u_ref, d_ref, gw_ref, gb_ref, dy1_ref, dgl_ref, y2_ref, dud_ref, dd_ref, dgb_ref):
        @pl.when(pl.program_id(0) == 0)
        def _():
            dd_ref[...] = jnp.zeros_like(dd_ref)
            dgb_ref[...] = jnp.zeros_like(dgb_ref)

        u = u_ref[...]
        dd = d_ref[...]
        y1 = yc_ref[...] + dd * u
        y2 = _gelu(y1)
        gw_v = gw_ref[...]
        gl = jnp.dot(y2.astype(BF16), gw_v, preferred_element_type=F32) + gb_ref[...]
        sg = _sigmoid(gl)
        dyv = dy_ref[...]
        dgl = dyv * y2 * sg * (1.0 - sg)
        dy2 = dyv * sg + lax.dot_general(dgl.astype(BF16), gw_v, _DIMS["nt"], preferred_element_type=F32)
        dy1 = dy2 * _gelu_grad(y1)
        dy1_ref[...] = dy1.astype(BF16)
        dgl_ref[...] = dgl.astype(BF16)
        y2_ref[...] = y2.astype(BF16)
        dud_ref[...] = dy1 * dd
        dd_ref[...] += jnp.sum(dy1 * u, axis=0, keepdims=True)
        dgb_ref[...] += jnp.sum(dgl, axis=0, keepdims=True)

    outs = _rowcall(
        body, name, s, tm,
        [_rb(tm, 256), _rb(tm, 256), _rb(tm, 256, 8), _fb((1, 256)), _fb((256, 256)), _fb((1, 256))],
        (dy, yc, z, d.reshape(1, 256), gw, gb.reshape(1, 256)),
        [_rb(tm, 256)] * 4 + [_fb((1, 256))] * 2,
        [_sds((s, 256), BF16)] * 3 + [_sds((s, 256))] + [_sds((1, 256))] * 2)
    dy1, dgl, y2, dud, dd, dgb = outs
    return dy1, dgl, y2, dud, dd.reshape(256), dgb.reshape(256)


SCAN_T = 512
N_SCAN_TABLES = 6


def _scan_tables(pr, pi, reverse):
    ns = pr.shape[0]
    sign = -1.0 if reverse else 1.0
    power = [(jnp.ones((ns,), F32), jnp.zeros((ns,), F32))] + [(pr[:, k], sign * pi[:, k]) for k in range(8)]
    zero = (jnp.zeros((ns,), F32), jnp.zeros((ns,), F32))

    def table(exponents):
        rows = [zero if e is None else power[e] for e in exponents]
        return jnp.stack([jnp.concatenate(row) for row in rows])

    tabs = []
    for k in (1, 2, 4):
        has_partner = [(s < 8 - k) if reverse else (s >= k) for s in range(8)]
        tabs.append(table([k if ok else None for ok in has_partner]))
    tabs.append(table([s if reverse else 7 - s for s in range(8)]))
    tabs.append(table([8 - s if reverse else s + 1 for s in range(8)]))
    tabs.append(table([8] * 8))
    return jnp.stack(tabs)


def _cmul(ar, ai, br, bi):
    return ar * br - ai * bi, ar * bi + ai * br


def _scan_group(ur, ui, cr, ci, tr_ref, ti_ref, reverse):
    xr, xi = ur, ui
    for n, k in enumerate((1, 2, 4)):
        shift = 8 - k if reverse else k
        pr, pi = _cmul(tr_ref[n], ti_ref[n], pltpu.roll(xr, shift, axis=0), pltpu.roll(xi, shift, axis=0))
        xr, xi = xr + pr, xi + pi
    sr, si = _cmul(tr_ref[3], ti_ref[3], ur, ui)
    for k in (1, 2, 4):
        sr, si = sr + pltpu.roll(sr, k, axis=0), si + pltpu.roll(si, k, axis=0)
    pr, pi = _cmul(tr_ref[4], ti_ref[4], cr, ci)
    nr, ni = _cmul(tr_ref[5], ti_ref[5], cr, ci)
    return xr + pr, xi + pi, nr + sr, ni + si


def _table_halves(t_ref):
    return t_ref.at[:, :, pl.ds(0, SSM_NS)], t_ref.at[:, :, pl.ds(SSM_NS, SSM_NS)]


_U_BLOCK = (IN_W - SSM_W) // SSM_W


def _ssm_fwd(z, bdt, cd, tabs, name):
    s = z.shape[0]
    ns = SSM_NS
    n_t = s // SCAN_T

    def body(u_ref, b_ref, c_ref, t_ref, xr_ref, xi_ref, y_ref, cr_ref, ci_ref, ur_ref, ui_ref):
        @pl.when(pl.program_id(0) == 0)
        def _():
            cr_ref[...] = jnp.zeros_like(cr_ref)
            ci_ref[...] = jnp.zeros_like(ci_ref)

        bu = lax.dot_general(u_ref[...].astype(BF16), b_ref[...], _DIMS["nt"], preferred_element_type=F32)
        ur_ref[...] = bu[:, :ns]
        ui_ref[...] = bu[:, ns:]
        tr_ref, ti_ref = _table_halves(t_ref)

        def group(g, carry):
            rows = pl.ds(pl.multiple_of(g * 8, 8), 8)
            xr, xi, cr, ci = _scan_group(ur_ref[rows, :], ui_ref[rows, :], *carry, tr_ref, ti_ref, False)
            xr_ref[rows, :] = xr
            xi_ref[rows, :] = xi
            return cr, ci

        cr, ci = lax.fori_loop(0, SCAN_T // 8, group, (cr_ref[...], ci_ref[...]), unroll=2)
        cr_ref[...] = cr
        ci_ref[...] = ci
        y_ref[...] = (jnp.dot(xr_ref[...].astype(BF16), c_ref[0:ns, :], preferred_element_type=F32)
                      + jnp.dot(xi_ref[...].astype(BF16), c_ref[ns:, :], preferred_element_type=F32))

    x_spec = pl.BlockSpec((SCAN_T, ns), lambda t: (t, 0))
    return pl.pallas_call(
        body, name=name, grid=(n_t,),
        in_specs=[pl.BlockSpec((SCAN_T, SSM_W), lambda t: (t, _U_BLOCK)), _fb((2 * ns, SSM_W)),
                  _fb((2 * ns, SSM_W)), _fb((N_SCAN_TABLES, 8, 2 * ns))],
        out_specs=[x_spec, x_spec, _rb(SCAN_T, SSM_W)],
        out_shape=[_sds((s, ns)), _sds((s, ns)), _sds((s, SSM_W))],
        scratch_shapes=[pltpu.VMEM((8, ns), F32)] * 2 + [pltpu.VMEM((SCAN_T, ns), F32)] * 2,
        compiler_params=_params(("arbitrary",)),
    )(z, bdt, cd, tabs)


def _ssm_bwd(dy1, dud, z, xr, xi, bdt, cd, tabs, name):
    s = z.shape[0]
    ns = SSM_NS
    n_t = s // SCAN_T
    n_groups = SCAN_T // 8

    def body(dy_ref, dud_ref, u_ref, xr_ref, xi_ref, pxr_ref, pxi_ref, b_ref, c_ref, t_ref,
             du_ref, dbd_ref, dcd_ref, dar_ref, dai_ref,
             cr_ref, ci_ref, ar_ref, ai_ref, sxr_ref, sxi_ref, gr_ref, gi_ref, lr_ref, li_ref, bacc_ref, cacc_ref):
        t = pl.program_id(0)

        @pl.when(t == 0)
        def _():
            for ref in (cr_ref, ci_ref, ar_ref, ai_ref, bacc_ref, cacc_ref):
                ref[...] = jnp.zeros_like(ref)

        dyb = dy_ref[...]
        g = lax.dot_general(dyb, c_ref[...], _DIMS["nt"], preferred_element_type=F32)
        gr_ref[...] = g[:, :ns]
        gi_ref[...] = g[:, ns:]
        has_before = (t < n_t - 1).astype(F32)
        sxr_ref[0:8, :] = pxr_ref[...] * has_before
        sxi_ref[0:8, :] = pxi_ref[...] * has_before
        sxr_ref[8:, :] = xr_ref[...]
        sxi_ref[8:, :] = xi_ref[...]
        first_row = lax.broadcasted_iota(jnp.int32, (8, ns), 0) == 0
        tr_ref, ti_ref = _table_halves(t_ref)

        def group(k, carry):
            cr, ci, ar, ai = carry
            g8 = pl.multiple_of((n_groups - 1 - k) * 8, 8)
            rows = pl.ds(g8, 8)
            lr, li, cr, ci = _scan_group(gr_ref[rows, :], gi_ref[rows, :], cr, ci, tr_ref, ti_ref, True)
            lr_ref[rows, :] = lr
            li_ref[rows, :] = li
            here, before = pl.ds(g8 + 8, 8), rows
            pr = jnp.where(first_row, pltpu.roll(sxr_ref[before, :], 1, axis=0), pltpu.roll(sxr_ref[here, :], 1, axis=0))
            pi = jnp.where(first_row, pltpu.roll(sxi_ref[before, :], 1, axis=0), pltpu.roll(sxi_ref[here, :], 1, axis=0))
            return cr, ci, ar + lr * pr + li * pi, ai + li * pr - lr * pi

        cr, ci, ar, ai = lax.fori_loop(0, n_groups, group,
                                       (cr_ref[...], ci_ref[...], ar_ref[...], ai_ref[...]), unroll=2)
        cr_ref[...] = cr
        ci_ref[...] = ci
        ar_ref[...] = ar
        ai_ref[...] = ai
        lrb = lr_ref[...].astype(BF16)
        lib = li_ref[...].astype(BF16)
        ub = u_ref[...].astype(BF16)
        du_ref[...] = (dud_ref[...] + jnp.dot(lrb, b_ref[0:ns, :], preferred_element_type=F32)
                       + jnp.dot(lib, b_ref[ns:, :], preferred_element_type=F32))
        bacc_ref[0:ns, :] += lax.dot_general(lrb, ub, _DIMS["tn"], preferred_element_type=F32)
        bacc_ref[ns:, :] += lax.dot_general(lib, ub, _DIMS["tn"], preferred_element_type=F32)
        cacc_ref[0:ns, :] += lax.dot_general(xr_ref[...].astype(BF16), dyb, _DIMS["tn"], preferred_element_type=F32)
        cacc_ref[ns:, :] += lax.dot_general(xi_ref[...].astype(BF16), dyb, _DIMS["tn"], preferred_element_type=F32)

        @pl.when(t == n_t - 1)
        def _():
            for k in (1, 2, 4):
                ar_ref[...] += pltpu.roll(ar_ref[...], k, axis=0)
                ai_ref[...] += pltpu.roll(ai_ref[...], k, axis=0)
            dar_ref[...] = ar_ref[...]
            dai_ref[...] = ai_ref[...]
            dbd_ref[...] = bacc_ref[...]
            dcd_ref[...] = cacc_ref[...]

    rev = lambda t: n_t - 1 - t
    row_spec = pl.BlockSpec((SCAN_T, SSM_W), lambda t: (rev(t), 0))
    x_spec = pl.BlockSpec((SCAN_T, ns), lambda t: (rev(t), 0))
    before_spec = pl.BlockSpec((8, ns), lambda t: (jnp.maximum(rev(t) * (SCAN_T // 8) - 1, 0), 0))
    du, dbd, dcd, dar, dai = pl.pallas_call(
        body, name=name, grid=(n_t,),
        in_specs=[row_spec, row_spec, pl.BlockSpec((SCAN_T, SSM_W), lambda t: (rev(t), _U_BLOCK)),
                  x_spec, x_spec, before_spec, before_spec,
                  _fb((2 * ns, SSM_W)), _fb((2 * ns, SSM_W)), _fb((N_SCAN_TABLES, 8, 2 * ns))],
        out_specs=[row_spec, _fb((2 * ns, SSM_W)), _fb((2 * ns, SSM_W)), _fb((8, ns)), _fb((8, ns))],
        out_shape=[_sds((s, SSM_W)), _sds((2 * ns, SSM_W)), _sds((2 * ns, SSM_W)), _sds((8, ns)), _sds((8, ns))],
        scratch_shapes=([pltpu.VMEM((8, ns), F32)] * 4 + [pltpu.VMEM((SCAN_T + 8, ns), F32)] * 2
                        + [pltpu.VMEM((SCAN_T, ns), F32)] * 4 + [pltpu.VMEM((2 * ns, SSM_W), F32)] * 2),
        compiler_params=_params(("arbitrary",)),
    )(dy1, dud, z, xr, xi, xr, xi, bdt, cd, tabs)
    return du, dbd, dcd, dar[0], dai[0]


def _group_ids():
    return lax.broadcasted_iota(jnp.int32, (1, SGU_W), 1) // 64


def _group_mean(val, gid):
    out = jnp.zeros_like(val)
    for g in range(SGU_GROUPS):
        mg = gid == g
        out = jnp.where(mg, jnp.sum(jnp.where(mg, val, 0.0), axis=1, keepdims=True) * (1.0 / 64), out)
    return out


def _causal_w(w_ref, g):
    t = lax.broadcasted_iota(jnp.int32, (SGU_CHUNK, SGU_CHUNK), 0)
    s = lax.broadcasted_iota(jnp.int32, (SGU_CHUNK, SGU_CHUNK), 1)
    return jnp.where(t >= s, w_ref[g], 0.0).astype(BF16)


def _sgu_core(x, lng, lnb, w_ref, bexp, gid):
    zz = _gelu(x)
    u = zz[:, :SGU_W]
    v = zz[:, SGU_W:]
    vc = v - _group_mean(v, gid)
    rstd = lax.rsqrt(_group_mean(vc * vc, gid) + EPS)
    vhat = vc * rstd
    vn = vhat * lng + lnb
    vnb = vn.astype(BF16)
    mixed = bexp
    for g in range(SGU_GROUPS):
        mm = jnp.dot(_causal_w(w_ref, g), vnb, preferred_element_type=F32)
        mixed = jnp.where(gid == g, mm + bexp, mixed)
    return u, rstd, vhat, vnb, mixed


def _sgu_fwd(z, lng, lnb, w, bexp, name, tm=512):
    s = z.shape[0]

    def body(z_ref, lng_ref, lnb_ref, w_ref, b_ref, o_ref):
        gid = _group_ids()
        for j in range(tm // SGU_CHUNK):
            rows = pl.ds(j * SGU_CHUNK, SGU_CHUNK)
            u, _, _, _, mixed = _sgu_core(z_ref[rows, :], lng_ref[...], lnb_ref[...], w_ref, b_ref[...], gid)
            o_ref[rows, :] = u * mixed

    return _rowcall(body, name, s, tm,
                    [_rb(tm, 512, 3), _fb((1, 256)), _fb((1, 256)), _fb((4, 128, 128)), _fb((128, 256))],
                    (z, lng.reshape(1, 256), lnb.reshape(1, 256), w, bexp), _rb(tm, 256), _sds((s, 256)))


def _sgu_bwd(z, dy, lng, lnb, w, bexp, name, tm=512):
    s = z.shape[0]

    def body(z_ref, dy_ref, lng_ref, lnb_ref, w_ref, b_ref, dz_ref, dw_ref, db_ref, dlng_ref, dlnb_ref):
        @pl.when(pl.program_id(0) == 0)
        def _():
            dw_ref[...] = jnp.zeros_like(dw_ref)
            db_ref[...] = jnp.zeros_like(db_ref)
            dlng_ref[...] = jnp.zeros_like(dlng_ref)
            dlnb_ref[...] = jnp.zeros_like(dlnb_ref)

        gid = _group_ids()
        t = lax.broadcasted_iota(jnp.int32, (SGU_CHUNK, SGU_CHUNK), 0)
        sidx = lax.broadcasted_iota(jnp.int32, (SGU_CHUNK, SGU_CHUNK), 1)
        lng_v = lng_ref[...]
        for j in range(tm // SGU_CHUNK):
            rows = pl.ds(j * SGU_CHUNK, SGU_CHUNK)
            x = z_ref[rows, :]
            u, rstd, vhat, vnb, mixed = _sgu_core(x, lng_v, lnb_ref[...], w_ref, b_ref[...], gid)
            dyv = dy_ref[rows, :]
            dmixed = dyv * u
            du = dyv * mixed
            db_ref[...] += dmixed
            dvn = jnp.zeros_like(dmixed)
            for g in range(SGU_GROUPS):
                dmg = jnp.where(gid == g, dmixed, 0.0).astype(BF16)
                dvn = dvn + lax.dot_general(_causal_w(w_ref, g), dmg, _DIMS["tn"], preferred_element_type=F32)
                dwg = lax.dot_general(dmg, vnb, _DIMS["nt"], preferred_element_type=F32)
                dw_ref[g] += jnp.where(t >= sidx, dwg, 0.0)
            dlnb_ref[...] += jnp.sum(dvn, axis=0, keepdims=True)
            dlng_ref[...] += jnp.sum(dvn * vhat, axis=0, keepdims=True)
            dvh = dvn * lng_v
            dv = rstd * (dvh - _group_mean(dvh, gid) - vhat * _group_mean(dvh * vhat, gid))
            gg = _gelu_grad(x)
            dz_ref[rows, 0:SGU_W] = du * gg[:, :SGU_W]
            dz_ref[rows, SGU_W:2 * SGU_W] = dv * gg[:, SGU_W:]

    dz, dw, db, dlng, dlnb = _rowcall(
        body, name, s, tm,
        [_rb(tm, 512, 3), _rb(tm, 256), _fb((1, 256)), _fb((1, 256)), _fb((4, 128, 128)), _fb((128, 256))],
        (z, dy, lng.reshape(1, 256), lnb.reshape(1, 256), w, bexp),
        [_rb(tm, 512), _fb((4, 128, 128)), _fb((128, 256)), _fb((1, 256)), _fb((1, 256))],
        [_sds((s, 512)), _sds((4, 128, 128)), _sds((128, 256)), _sds((1, 256)), _sds((1, 256))])
    return dz, dw, db, dlng.reshape(256), dlnb.reshape(256)


CONV_TC = 1408
N_CT = D_FF // CONV_TC


def _row_of(block8, j):
    r = lax.broadcasted_iota(jnp.int32, block8.shape, 0)
    return jnp.sum(jnp.where(r == j, block8, 0.0), axis=0, keepdims=True)


EDGE = 16


def _conv_fwd(hu, cw, cb, name, tm=256):
    s = hu.shape[0]

    def body(xv_ref, xg_ref, tv_ref, tg_ref, wv_ref, wg_ref, bv_ref, bg_ref, hv_ref, hg_ref, act_ref):
        has_prev = (pl.program_id(1) > 0).astype(F32)
        row = lax.broadcasted_iota(jnp.int32, (EDGE, CONV_TC), 0)

        def conv(x_ref, t_ref, w_ref, b_ref):
            x = x_ref[...].astype(F32)
            w0, w1, w2, bb = w_ref[0:1, :], w_ref[1:2, :], w_ref[2:3, :], b_ref[...]
            whole = w0 * pltpu.roll(x, 2, axis=0) + w1 * pltpu.roll(x, 1, axis=0) + w2 * x + bb
            tail = t_ref[...].astype(F32)
            r7 = _row_of(tail, EDGE - 1) * has_prev
            r6 = _row_of(tail, EDGE - 2) * has_prev
            xe = x_ref[0:EDGE, :].astype(F32)
            x1 = jnp.where(row == 0, r7, pltpu.roll(xe, 1, axis=0))
            x2 = jnp.where(row == 0, r6, jnp.where(row == 1, r7, pltpu.roll(xe, 2, axis=0)))
            return whole, w0 * x2 + w1 * x1 + w2 * xe + bb

        hv, hv_edge = conv(xv_ref, tv_ref, wv_ref, bv_ref)
        hg, hg_edge = conv(xg_ref, tg_ref, wg_ref, bg_ref)
        hv_ref[...] = hv.astype(BF16)
        hg_ref[...] = hg.astype(BF16)
        act_ref[...] = (_gelu(hg) * hv).astype(BF16)
        hv_ref[0:EDGE, :] = hv_edge.astype(BF16)
        hg_ref[0:EDGE, :] = hg_edge.astype(BF16)
        act_ref[0:EDGE, :] = (_gelu(hg_edge) * hv_edge).astype(BF16)

    def xs(off):
        return pl.BlockSpec((tm, CONV_TC), lambda j, i: (i, j + off))

    def ts(off):
        return pl.BlockSpec((EDGE, CONV_TC), lambda j, i: (jnp.maximum(i * (tm // EDGE) - 1, 0), j + off))

    def ws(rows, off):
        return pl.BlockSpec((rows, CONV_TC), lambda j, i: (0, j + off))

    o_spec = pl.BlockSpec((tm, CONV_TC), lambda j, i: (i, j))
    return pl.pallas_call(
        body, name=name, grid=(N_CT, s // tm),
        in_specs=[xs(0), xs(N_CT), ts(0), ts(N_CT), ws(3, 0), ws(3, N_CT), ws(1, 0), ws(1, N_CT)],
        out_specs=[o_spec] * 3, out_shape=[_sds((s, D_FF), BF16)] * 3,
        compiler_params=_params(("parallel", "arbitrary")),
    )(hu, hu, hu, hu, cw, cw, cb.reshape(1, 2 * D_FF), cb.reshape(1, 2 * D_FF))


HALO = EDGE


def _conv_bwd(dact, hv, hg, hu, cw, name, tm=256):
    s = dact.shape[0]

    def body(da_ref, dan_ref, hv_ref, hvn_ref, hg_ref, hgn_ref, x_ref, t_ref, w_ref, dx_ref, dw_ref, db_ref, d_scr):
        i = pl.program_id(1)
        is_value = pl.program_id(0) < N_CT

        @pl.when(i == 0)
        def _():
            dw_ref[...] = jnp.zeros_like(dw_ref)
            db_ref[...] = jnp.zeros_like(db_ref)

        for rows, (a_ref, v_ref, g_ref) in ((pl.ds(0, tm), (da_ref, hv_ref, hg_ref)),
                                            (pl.ds(tm, HALO), (dan_ref, hvn_ref, hgn_ref))):
            @pl.when(is_value)
            def _():
                d_scr[rows, :] = a_ref[...].astype(F32) * _gelu(g_ref[...].astype(F32))

            @pl.when(jnp.logical_not(is_value))
            def _():
                d_scr[rows, :] = (a_ref[...].astype(F32) * v_ref[...].astype(F32)
                                  * _gelu_grad(g_ref[...].astype(F32)))

        has_prev = (i > 0).astype(F32)
        has_next = (i < s // tm - 1).astype(F32)
        w0, w1, w2 = w_ref[0:1, :], w_ref[1:2, :], w_ref[2:3, :]
        d = d_scr[0:tm, :]
        dx_ref[...] = (w2 * d + w1 * pltpu.roll(d, tm - 1, axis=0) + w0 * pltpu.roll(d, tm - 2, axis=0)).astype(BF16)
        row = lax.broadcasted_iota(jnp.int32, (EDGE, CONV_TC), 0)
        nxt = d_scr[tm:tm + HALO, :]
        n0 = _row_of(nxt, 0) * has_next
        n1 = _row_of(nxt, 1) * has_next
        de = d_scr[tm - EDGE:tm, :]
        d1 = jnp.where(row == EDGE - 1, n0, pltpu.roll(de, EDGE - 1, axis=0))
        d2 = jnp.where(row == EDGE - 2, n0, jnp.where(row == EDGE - 1, n1, pltpu.roll(de, EDGE - 2, axis=0)))
        dx_ref[tm - EDGE:tm, :] = (w2 * de + w1 * d1 + w0 * d2).astype(BF16)
        x = x_ref[...].astype(F32)
        tail = t_ref[...].astype(F32)
        r7 = _row_of(tail, EDGE - 1) * has_prev
        r6 = _row_of(tail, EDGE - 2) * has_prev
        last = x_ref[tm - EDGE:tm, :].astype(F32)
        l7, l6 = _row_of(last, EDGE - 1), _row_of(last, EDGE - 2)
        head = d_scr[0:8, :]
        d0, d1h = _row_of(head, 0), _row_of(head, 1)
        dw_ref[0:1, :] += (jnp.sum(d * pltpu.roll(x, 2, axis=0), axis=0, keepdims=True)
                           + d0 * (r6 - l6) + d1h * (r7 - l7))
        dw_ref[1:2, :] += jnp.sum(d * pltpu.roll(x, 1, axis=0), axis=0, keepdims=True) + d0 * (r7 - l7)
        dw_ref[2:3, :] += jnp.sum(d * x, axis=0, keepdims=True)
        db_ref[...] += jnp.sum(d, axis=0, keepdims=True)

    a_spec = pl.BlockSpec((tm, CONV_TC), lambda j, i: (i, j % N_CT))
    an_spec = pl.BlockSpec((HALO, CONV_TC),
                           lambda j, i: (jnp.minimum((i + 1) * (tm // HALO), s // HALO - 1), j % N_CT))
    x_spec = pl.BlockSpec((tm, CONV_TC), lambda j, i: (i, j))
    t_spec = pl.BlockSpec((EDGE, CONV_TC), lambda j, i: (jnp.maximum(i * (tm // EDGE) - 1, 0), j))
    w_spec = pl.BlockSpec((3, CONV_TC), lambda j, i: (0, j))
    db_spec = pl.BlockSpec((1, CONV_TC), lambda j, i: (0, j))
    return pl.pallas_call(
        body, name=name, grid=(2 * N_CT, s // tm),
        in_specs=[a_spec, an_spec, a_spec, an_spec, a_spec, an_spec, x_spec, t_spec, w_spec],
        out_specs=[x_spec, w_spec, db_spec],
        out_shape=[_sds((s, 2 * D_FF), BF16), _sds((3, 2 * D_FF)), _sds((1, 2 * D_FF))],
        scratch_shapes=[pltpu.VMEM((tm + HALO, CONV_TC), F32)],
        compiler_params=_params(("parallel", "arbitrary")),
    )(dact, dact, hv, hv, hg, hg, hu, hu, cw)


def _ple_fwd(h, gp, pp, next_gain, name, tm=512):
    s, d = h.shape
    with_norm = next_gain is not None

    def body(*refs):
        h_ref, g_ref, p_ref = refs[:3]
        out = h_ref[...] + _sigmoid(g_ref[...].astype(F32)) * p_ref[...].astype(F32)
        if with_norm:
            n_ref, o_ref, a_ref = refs[3:]
            scale = lax.rsqrt(jnp.mean(out * out, axis=-1, keepdims=True) + EPS)
            a_ref[...] = (out * scale * n_ref[...]).astype(BF16)
        else:
            o_ref, = refs[3:]
        o_ref[...] = out

    if not with_norm:
        return _rowcall(body, name, s, tm, [_rb(tm, d)] * 3, (h, gp, pp), _rb(tm, d), _sds((s, d))), None
    return _rowcall(body, name, s, tm, [_rb(tm, d)] * 3 + [_fb((1, d))], (h, gp, pp, next_gain.reshape(1, d)),
                    [_rb(tm, d)] * 2, [_sds((s, d)), _sds((s, d), BF16)])


def _ple_bwd(dh, gp, pp, name, tm=512):
    s, d = dh.shape

    def body(d_ref, g_ref, p_ref, dp_ref, dg_ref):
        sg = _sigmoid(g_ref[...].astype(F32))
        dv = d_ref[...]
        dp_ref[...] = (dv * sg).astype(BF16)
        dg_ref[...] = (dv * p_ref[...].astype(F32) * sg * (1.0 - sg)).astype(BF16)

    return _rowcall(body, name, s, tm, [_rb(tm, d)] * 3, (dh, gp, pp), [_rb(tm, d)] * 2,
                    [_sds((s, d), BF16)] * 2)


SCALE = HEAD_DIM ** -0.5
ATT_ROWS = 2048


def _att_geom(s, dil):
    w = min(ATT_ROWS, s)
    p = BLK * dil
    assert w % p == 0 and s % w == 0
    return w, p, w // p


def _rows(start, dil):
    return pl.ds(start, BLK, stride=dil) if dil > 1 else pl.ds(start, BLK)


def _head_masks():
    lane = lax.broadcasted_iota(jnp.int32, (1, BLK), 1)
    return [lane < HEAD_DIM, lane >= HEAD_DIM]


def _band():
    rel = np.arange(BLK)[:, None] + BLK - np.arange(2 * BLK)[None, :]
    return (rel >= 0) & (rel <= BLK)


def _zcur(w):
    return lambda off: pl.BlockSpec((w, BLK), lambda hp, i: (i, off + hp))


def _zprev(p, nb):
    return lambda off: pl.BlockSpec((p, BLK), lambda hp, i: (jnp.maximum(i * nb - 1, 0), off + hp))


def _scur(w):
    return pl.BlockSpec((w, BLK), lambda hp, i: (i, hp))


def _pair_rows(t, masks):
    return jnp.concatenate([jnp.where(masks[0], t, 0.0), jnp.where(masks[1], t, 0.0)], axis=0).astype(BF16)


def _pair_bias_bwd(bias):
    return bias.reshape(4, 2, BLK, 2, BLK).transpose(0, 3, 2, 1, 4).reshape(4, 2, BLK, 2 * BLK)


def _unpair_bias_bwd(db):
    return db.reshape(4, 2, BLK, 2, BLK).transpose(0, 3, 2, 1, 4).reshape(N_HEADS, BLK, 2 * BLK)


def _attn_fwd(z, bias, state, dil, first, last, name):
    s = z.shape[0]
    w, p, nb = _att_geom(s, dil)

    def body(*refs):
        q_ref, kp_ref, kc_ref, vp_ref, vc_ref, b_ref = refs[:6]
        rest = refs[6:]
        if not first:
            m_ref, l_ref, a_ref = rest[:3]
            rest = rest[3:]
        i = pl.program_id(1)
        masks = _head_masks()
        own_block = lax.broadcasted_iota(jnp.int32, (1, 2 * BLK), 1) >= BLK
        for r in range(dil):
            for b in range(nb):
                rows = _rows(r + p * b, dil)
                prev_rows = _rows(r + p * (b - 1), dil) if b > 0 else _rows(r, dil)
                kprev, vprev = (kc_ref, vc_ref) if b > 0 else (kp_ref, vp_ref)
                q = q_ref[rows, :] * SCALE
                k = jnp.concatenate([kprev[prev_rows, :], kc_ref[rows, :]], axis=0).astype(BF16)
                v = jnp.concatenate([vprev[prev_rows, :], vc_ref[rows, :]], axis=0).astype(BF16)
                mb = lb = ob = None
                for hh, mh in enumerate(masks):
                    qh = jnp.where(mh, q, 0.0).astype(BF16)
                    sc = lax.dot_general(qh, k, _DIMS["nt"], preferred_element_type=F32) + b_ref[hh]
                    if b == 0:
                        sc = jnp.where(own_block | (i > 0), sc, NEG_INF)
                    mx = jnp.max(sc, axis=1, keepdims=True)
                    e = jnp.exp(sc - mx)
                    den = jnp.sum(e, axis=1, keepdims=True)
                    o = jnp.dot(e.astype(BF16), v, preferred_element_type=F32)
                    if hh == 0:
                        mb = jnp.broadcast_to(mx, (BLK, BLK))
                        lb = jnp.broadcast_to(den, (BLK, BLK))
                        ob = o
                    else:
                        mb = jnp.where(mh, mx, mb)
                        lb = jnp.where(mh, den, lb)
                        ob = jnp.where(mh, o, ob)
                if first:
                    m_new, l_new, a_new = mb, lb, ob
                else:
                    m_old = m_ref[rows, :]
                    m_new = jnp.maximum(m_old, mb)
                    al = jnp.exp(m_old - m_new)
                    be = jnp.exp(mb - m_new)
                    l_new = al * l_ref[rows, :] + be * lb
                    a_new = al * a_ref[rows, :] + be * ob
                if last:
                    y_ref, lse_ref = rest
                    y_ref[rows, :] = a_new / l_new
                    lse_ref[rows, :] = m_new + jnp.log(l_new)
                else:
                    mo_ref, lo_ref, ao_ref = rest
                    mo_ref[rows, :] = m_new
                    lo_ref[rows, :] = l_new
                    ao_ref[rows, :] = a_new

    cur, prv = _zcur(w), _zprev(p, nb)
    b_spec = pl.BlockSpec((2, BLK, 2 * BLK), lambda hp, i: (hp, 0, 0))
    in_specs = [cur(0), prv(4), cur(4), prv(8), cur(8), b_spec]
    args = [z, z, z, z, z, bias]
    if not first:
        in_specs += [_scur(w)] * 3
        args += list(state)
    n_out = 2 if last else 3
    return pl.pallas_call(
        body, name=name, grid=(4, s // w), in_specs=in_specs, out_specs=[_scur(w)] * n_out,
        out_shape=[_sds((s, ATTN_W))] * n_out,
        compiler_params=_params(("parallel", "parallel")),
    )(*args)


def _row_stats(mh, dy, y, lse):
    delta = jnp.sum(jnp.where(mh, dy * y, 0.0), axis=1, keepdims=True)
    lse_h = jnp.max(jnp.where(mh, lse, NEG_INF), axis=1, keepdims=True)
    return delta, lse_h


def _attn_bwd(z, bias, dy, y, lse, prev, dil, name):
    s = z.shape[0]
    w, p, nb = _att_geom(s, dil)
    n_steps = s // w
    first = prev is None

    def body(*refs):
        q_ref, kp_ref, kc_ref, vp_ref, vc_ref, b_ref, dy_ref, y_ref, lse_ref = refs[:9]
        rest = refs[9:]
        if not first:
            dqp_ref, dkp_ref, dvp_ref = rest[:3]
            rest = rest[3:]
        dq_ref, dk_ref, dv_ref, dkx_ref, dvx_ref, db_ref = rest
        i = pl.program_id(1)

        @pl.when(i == 0)
        def _():
            db_ref[...] = jnp.zeros_like(db_ref)

        masks = _head_masks()
        first_head = lax.broadcasted_iota(jnp.int32, (1, 2 * BLK), 1) < BLK

        def flush(rows, dk, dv):
            if not first:
                dk = dk + dkp_ref[rows, :]
                dv = dv + dvp_ref[rows, :]
            dk_ref[rows, :] = dk
            dv_ref[rows, :] = dv

        for r in range(dil):
            carry = None
            for b in range(nb):
                rows = _rows(r + p * b, dil)
                prev_rows = _rows(r + p * (b - 1), dil) if b > 0 else _rows(r, dil)
                kprev, vprev = (kc_ref, vc_ref) if b > 0 else (kp_ref, vp_ref)
                keys = [(_pair_rows(kprev[prev_rows, :], masks), _pair_rows(vprev[prev_rows, :], masks)),
                        (_pair_rows(kc_ref[rows, :], masks), _pair_rows(vc_ref[rows, :], masks))]
                q = (q_ref[rows, :] * SCALE).astype(BF16)
                dy_v = dy_ref[rows, :]
                dyb = dy_v.astype(BF16)
                stats = [_row_stats(mh, dy_v, y_ref[rows, :], lse_ref[rows, :]) for mh in masks]
                delta = jnp.where(first_head, stats[0][0], stats[1][0])
                lse_h = jnp.where(first_head, stats[0][1], stats[1][1])
                dq = jnp.zeros((BLK, BLK), F32)
                dk, dv = [], []
                for half in range(2):
                    kh, vh = keys[half]
                    sc = lax.dot_general(q, kh, _DIMS["nt"], preferred_element_type=F32) + b_ref[half]
                    pr = jnp.exp(sc - lse_h)
                    if b == 0 and half == 0:
                        pr = pr * (i > 0).astype(F32)
                    dp = lax.dot_general(dyb, vh, _DIMS["nt"], preferred_element_type=F32)
                    ds = pr * (dp - delta)
                    db_ref[half] += ds
                    dsb = ds.astype(BF16)
                    dq = dq + jnp.dot(dsb, kh, preferred_element_type=F32)
                    dk2 = lax.dot_general(dsb, q, _DIMS["tn"], preferred_element_type=F32)
                    dv2 = lax.dot_general(pr.astype(BF16), dyb, _DIMS["tn"], preferred_element_type=F32)
                    dk.append(jnp.where(masks[0], dk2[:BLK], dk2[BLK:]))
                    dv.append(jnp.where(masks[0], dv2[:BLK], dv2[BLK:]))
                dq = dq * SCALE
                if not first:
                    dq = dq + dqp_ref[rows, :]
                dq_ref[rows, :] = dq
                if b > 0:
                    flush(prev_rows, carry[0] + dk[0], carry[1] + dv[0])
                else:
                    dkx_ref[prev_rows, :] = dk[0]
                    dvx_ref[prev_rows, :] = dv[0]
                carry = (dk[1], dv[1])
            flush(_rows(r + p * (nb - 1), dil), *carry)

    cur, prv = _zcur(w), _zprev(p, nb)
    b_spec = pl.BlockSpec((None, 2, BLK, 2 * BLK), lambda hp, i: (hp, 0, 0, 0))
    in_specs = [cur(0), prv(4), cur(4), prv(8), cur(8), b_spec] + [_scur(w)] * 3
    args = [z, z, z, z, z, bias, dy, y, lse]
    if not first:
        in_specs += [_scur(w)] * 3
        args += list(prev)
    x_spec = pl.BlockSpec((p, BLK), lambda hp, i: (i, hp))
    *outs, db = pl.pallas_call(
        body, name=name, grid=(4, n_steps), in_specs=in_specs,
        out_specs=[_scur(w)] * 3 + [x_spec] * 2 + [b_spec],
        out_shape=[_sds((s, ATTN_W))] * 3 + [_sds((n_steps * p, ATTN_W))] * 2 + [_sds((4, 2, BLK, 2 * BLK))],
        compiler_params=_params(("parallel", "arbitrary")),
    )(*args)
    return (*outs, _unpair_bias_bwd(db))


ASM_ROWS = 512


def _assemble_dz(dq, dk, dv, extras, dzs, du, name):
    s = dq.shape[0]
    w = min(ATT_ROWS, s)
    n_steps = s // w
    per_step = w // ASM_ROWS
    assert w % ASM_ROWS == 0

    def body(*refs):
        dq_ref, dk_ref, dv_ref, dzs_ref, du_ref = refs[:5]
        x_refs = refs[5:5 + 2 * len(extras)]
        o_ref, acc_ref = refs[-2:]
        j = pl.program_id(0)
        step = j // per_step
        has_next = (step < n_steps - 1).astype(F32)
        last_of_step = ((j + 1) % per_step == 0).astype(F32)
        o_ref[:, 0:ATTN_W] = dq_ref[...].astype(BF16)
        o_ref[:, 3 * ATTN_W:3 * ATTN_W + 2 * SGU_W] = dzs_ref[...].astype(BF16)
        o_ref[:, 3 * ATTN_W + 2 * SGU_W:IN_W] = du_ref[...].astype(BF16)
        for part, (base_ref, col) in enumerate(((dk_ref, ATTN_W), (dv_ref, 2 * ATTN_W))):
            acc_ref[...] = base_ref[...]
            for n, (_, dil) in enumerate(BRANCHES):
                rows = min(BLK * dil, ASM_ROWS)
                scale = has_next if BLK * dil >= w else has_next * last_of_step
                acc_ref[ASM_ROWS - rows:, :] += x_refs[2 * n + part][...] * scale
            o_ref[:, col:col + ATTN_W] = acc_ref[...].astype(BF16)

    def x_spec(dil):
        p = BLK * dil
        rows = min(p, ASM_ROWS)
        blocks_per_step = p // rows
        total = n_steps * blocks_per_step

        def idx(j):
            step = j // per_step
            within = (j % per_step) - (per_step - blocks_per_step)
            return (jnp.clip((step + 1) * blocks_per_step + jnp.maximum(within, 0), 0, total - 1), 0)

        return pl.BlockSpec((rows, ATTN_W), idx)

    in_specs = [_rb(ASM_ROWS, ATTN_W)] * 3 + [_rb(ASM_ROWS, 2 * SGU_W), _rb(ASM_ROWS, SSM_W)]
    args = [dq, dk, dv, dzs, du]
    for (dkx, dvx), (_, dil) in zip(extras, BRANCHES):
        in_specs += [x_spec(dil)] * 2
        args += [dkx, dvx]
    return pl.pallas_call(
        body, name=name, grid=(s // ASM_ROWS,), in_specs=in_specs, out_specs=_rb(ASM_ROWS, IN_W),
        out_shape=_sds((s, IN_W), BF16), scratch_shapes=[pltpu.VMEM((ASM_ROWS, ATTN_W), F32)],
        compiler_params=_params(("parallel",)),
    )(*args)


def _t5_bucket(dist):
    max_exact = N_BUCKETS // 2
    d = np.maximum(dist, 0)
    large = max_exact + (np.log(np.maximum(d, 1) / max_exact) / np.log(REL_MAX / max_exact)
                         * (N_BUCKETS - max_exact)).astype(np.int32)
    large = np.minimum(large, N_BUCKETS - 1)
    return np.where(d < max_exact, d, large).astype(np.int32)


def _bias_tables(rel_bias):
    period = 3 * BLK
    tabs = []
    for _, dil in BRANCHES:
        onehot = np.zeros((period, N_BUCKETS), np.float32)
        d = np.arange(BLK + 1)
        onehot[d, _t5_bucket((BLK - d) * dil)] = 1.0
        f = jnp.dot(jnp.asarray(onehot), rel_bias, precision=lax.Precision.HIGHEST)
        flat = jnp.tile(f.T, (1, BLK))[:, :BLK * (period - 1)]
        tab = flat.reshape(N_HEADS, BLK, period - 1)[:, :, :2 * BLK]
        tabs.append(jnp.where(_band()[None], tab, NEG_INF))
    return tabs


def _bucket_onehot():
    maps = []
    q = np.arange(BLK)[:, None]
    k = np.arange(2 * BLK)[None, :]
    rel = q + BLK - k
    for _, dil in BRANCHES:
        maps.append(np.where((rel >= 0) & (rel <= BLK), _t5_bucket(rel * dil), -1).reshape(-1))
    bmap = jnp.asarray(np.concatenate(maps).astype(np.int32))
    return (bmap[:, None] == jnp.arange(128, dtype=jnp.int32)[None, :]).astype(BF16)


def _block_diag(t):
    g, n, c = t.shape
    eye = jnp.eye(g, dtype=t.dtype)
    return (t[:, :, None, :] * eye[:, None, :, None]).reshape(g * n, g * c)


def _ssm_prep(a_re, a_im, log_dt, b_re, b_im, c_re, c_im):
    lam = lax.complex(a_re, a_im)
    dt = jnp.exp(log_dt)[:, None]
    a_bar = jnp.exp(lam * dt)
    b_bar = ((a_bar - 1.0) / lam)[:, :, None] * lax.complex(b_re, b_im)
    bdt = jnp.concatenate([_block_diag(jnp.real(b_bar)), _block_diag(jnp.imag(b_bar))], axis=0)
    cd = jnp.concatenate([_block_diag(jnp.transpose(c_re, (0, 2, 1))),
                          _block_diag(-jnp.transpose(c_im, (0, 2, 1)))], axis=0)
    return jnp.real(a_bar).reshape(-1), jnp.imag(a_bar).reshape(-1), bdt, cd


def _powers(ar, ai):
    pr, pi = ar[:, None], ai[:, None]
    k = 1
    while k < 8:
        lr, li = pr[:, -1:], pi[:, -1:]
        pr, pi = (jnp.concatenate([pr, pr * lr - pi * li], axis=1),
                  jnp.concatenate([pi, pr * li + pi * lr], axis=1))
        k *= 2
    return pr, pi


def _sgu_bias_expand(b):
    return jnp.repeat(b.T, 64, axis=1)


def _layer_fwd(i, h, a1, p_i, big, small, bias_tabs, next_gain):
    nm = "l%d_" % i
    sv = {"h": h}
    if a1 is None:
        a1 = _rms_fwd(h, small["norm_attn_g"][i], nm + "rms_attn")
    z = _mm(a1, big["w_in"], "nt", nm + "in_proj")
    st = None
    for b, (_, dil) in enumerate(BRANCHES):
        st = _attn_fwd(z, bias_tabs[b][0], st, dil, b == 0, b == len(BRANCHES) - 1, nm + "attn_fwd%d" % b)
    y_attn, lse = st
    bexp = _sgu_bias_expand(small["sgu_b"][i])
    y_sgu = _sgu_fwd(z, small["sgu_ln_g"][i], small["sgu_ln_b"][i], small["sgu_w"][i], bexp, nm + "sgu_fwd")
    ar, ai, bdt, cd = _ssm_prep(*[small[k][i] for k in ("ssm_a_re", "ssm_a_im", "ssm_log_dt", "ssm_b_re",
                                                         "ssm_b_im", "ssm_c_re", "ssm_c_im")])
    xr, xi, yc = _ssm_fwd(z, bdt.astype(BF16), cd.astype(BF16), _scan_tables(*_powers(ar, ai), False),
                          nm + "ssm_core")
    y_ssm = _ssm_post_fwd(yc, z, small["ssm_d"][i], big["ssm_glu_w"], small["ssm_glu_b"][i], nm + "ssm_post")
    mix = _mix_fwd(y_attn, y_sgu, y_ssm, small["branch_norm_g"][i], nm + "mix")
    if "rest" in big:
        big = dict({k: t for k, t in big.items() if k != "rest"}, **big["rest"](mix))
    h2, a2 = _mm(mix, big["w_out"], "nn", nm + "out_proj", add=h, norm_gain=small["norm_ffn_g"][i])
    hu = _mm(a2, big["ffn_w_up"], "nt", nm + "ffn_up", out_dtype=BF16)
    hv, hg, act = _conv_fwd(hu, big["ffn_conv_w"], small["ffn_conv_b"][i], nm + "ffn_conv")
    h3, a3 = _mm(act, big["ffn_w_down"], "nn", nm + "ffn_down", add=h2, norm_gain=small["norm_ple_g"][i])
    gp = _mm(a3, big["ple_w_gate"], "nn", nm + "ple_gate", out_dtype=BF16)
    pp = _mm(p_i, big["ple_w_proj"], "nt", nm + "ple_proj", out_dtype=BF16)
    h4, a_next = _ple_fwd(h3, gp, pp, next_gain, nm + "ple_add")
    sv.update(big=big, a1=a1, z=z, y_attn=y_attn, lse=lse, y_sgu=y_sgu, y_ssm=y_ssm, yc=yc, xr=xr, xi=xi, mix=mix, h2=h2,
              a2=a2, hu=hu, hv=hv, hg=hg, act=act, h3=h3, a3=a3, gp=gp, pp=pp)
    return h4, a_next, sv


def _layer_bwd(i, dh4, sv, p_i, big, small, bias_tabs, ffn_done=None):
    nm = "l%d_" % i
    g = {}
    dpp, dgp = _ple_bwd(dh4, sv["gp"], sv["pp"], nm + "ple_bwd")
    g["ple_w_proj"] = _mm(dpp, p_i, "tn", nm + "d_ple_proj", out_dtype=BF16)
    g["ple_w_gate"] = _mm(sv["a3"], dgp, "tn", nm + "d_ple_gate", out_dtype=BF16)
    dh3, dgain = _mm(dgp, big["ple_w_gate"], "nt", nm + "ple_gate_t", add=dh4,
                     norm_bwd=(sv["h3"], small["norm_ple_g"][i]))
    g["norm_ple_g"] = dgain.reshape(D_MODEL)
    g["ffn_w_down"] = _mm(sv["act"], dh3, "tn", nm + "d_ffn_down", out_dtype=BF16)
    dact = _mm(dh3, big["ffn_w_down"], "nt", nm + "ffn_down_t", out_dtype=BF16)
    dhu, g["ffn_conv_w"], dcb = _conv_bwd(dact, sv["hv"], sv["hg"], sv["hu"], big["ffn_conv_w"],
                                          nm + "ffn_conv_bwd")
    g["ffn_conv_b"] = dcb.reshape(2 * D_FF)
    g["ffn_w_up"] = _mm(dhu, sv["a2"], "tn", nm + "d_ffn_up", out_dtype=BF16)
    dh2, dgain = _mm(dhu, big["ffn_w_up"], "nn", nm + "ffn_up_t", add=dh3,
                     norm_bwd=(sv["h2"], small["norm_ffn_g"][i]))
    g["norm_ffn_g"] = dgain.reshape(D_MODEL)
    if ffn_done is not None:
        small = ffn_done(g, small)
    g["w_out"] = _mm(sv["mix"], dh2, "tn", nm + "d_out_proj", out_dtype=BF16)
    dmix = _mm(dh2, big["w_out"], "nt", nm + "out_proj_t")
    dya, dysg, dyss, g["branch_norm_g"] = _mix_bwd(dmix, sv["y_attn"], sv["y_sgu"], sv["y_ssm"],
                                                   small["branch_norm_g"][i], nm + "mix_bwd")
    ssm_keys = ("ssm_a_re", "ssm_a_im", "ssm_log_dt", "ssm_b_re", "ssm_b_im", "ssm_c_re", "ssm_c_im")
    (ar, ai, bdt, cd), prep_vjp = jax.vjp(_ssm_prep, *[small[k][i] for k in ssm_keys])
    dy1, dgl, y2, dud, g["ssm_d"], g["ssm_glu_b"] = _ssm_post_bwd(
        dyss, sv["yc"], sv["z"], small["ssm_d"][i], big["ssm_glu_w"], small["ssm_glu_b"][i], nm + "ssm_post_bwd")
    g["ssm_glu_w"] = _mm(y2, dgl, "tn", nm + "d_ssm_glu", out_dtype=BF16)
    du, dbdt, dcd, dar, dai = _ssm_bwd(dy1, dud, sv["z"], sv["xr"], sv["xi"], bdt.astype(BF16), cd.astype(BF16),
                                       _scan_tables(*_powers(ar, ai), True), nm + "ssm_core_bwd")
    for k, val in zip(ssm_keys, prep_vjp((dar, dai, dbdt, dcd))):
        g[k] = val
    bexp, bexp_vjp = jax.vjp(_sgu_bias_expand, small["sgu_b"][i])
    dzs, g["sgu_w"], dbexp, g["sgu_ln_g"], g["sgu_ln_b"] = _sgu_bwd(
        sv["z"], dysg, small["sgu_ln_g"][i], small["sgu_ln_b"][i], small["sgu_w"][i], bexp, nm + "sgu_bwd")
    g["sgu_b"] = bexp_vjp(dbexp)[0]
    prev = None
    dbs, extras = [], []
    for b, (_, dil) in enumerate(BRANCHES):
        dq, dk, dv, dkx, dvx, db = _attn_bwd(sv["z"], bias_tabs[b][1], dya, sv["y_attn"], sv["lse"], prev, dil,
                                             nm + "attn_bwd%d" % b)
        prev = (dq, dk, dv)
        extras.append((dkx, dvx))
        dbs.append(db.reshape(N_HEADS, BLK * 2 * BLK))
    dz = _assemble_dz(dq, dk, dv, extras, dzs, du, nm + "assemble_dz")
    g["w_in"] = _mm(dz, sv["a1"], "tn", nm + "d_in_proj", out_dtype=BF16)
    dh, dgain = _mm(dz, big["w_in"], "nn", nm + "in_proj_t", add=dh2, norm_bwd=(sv["h"], small["norm_attn_g"][i]))
    g["norm_attn_g"] = dgain.reshape(D_MODEL)
    return dh, g, jnp.concatenate(dbs, axis=1)


def _local_step(x, p, target, layer_weights, small, layer_done=None):
    depth = p.shape[0]
    bias_tabs = [(t, _pair_bias_bwd(t)) for t in _bias_tables(small["rel_bias"])]
    h, a1 = x, None
    saved = []
    for i in range(depth):
        next_gain = small["norm_attn_g"][i + 1] if i + 1 < depth else None
        h, a1, sv = _layer_fwd(i, h, a1, p[i], layer_weights(i, h), small, bias_tabs, next_gain)
        saved.append(sv)
    dh, loss, g_final = _loss_head(h, target, small["final_norm_g"], "loss_head")
    layer_grads = [None] * depth
    dbias = [None] * depth
    for i in reversed(range(depth)):
        ffn_done = None if layer_done is None else (lambda g, sm, i=i: layer_done(i, "ffn", g, sm))
        dh, layer_grads[i], dbias[i] = _layer_bwd(i, dh, saved[i], p[i], saved[i]["big"], small, bias_tabs,
                                                  ffn_done)
        if layer_done is not None:
            small = layer_done(i, "all", layer_grads[i], small)
    big_grads = [{k: lg.pop(k) for k in COMM_NAMES} for lg in layer_grads]
    grads = {k: jnp.stack([layer_grads[i][k] for i in range(depth)]) for k in layer_grads[0]}
    grads["final_norm_g"] = g_final
    g_rb = _mm(sum(dbias[1:], dbias[0]), _bucket_onehot(), "nn", "d_rel_bias", tk=2048)
    grads["rel_bias"] = g_rb[:, :N_BUCKETS].T
    return loss, dh, big_grads, grads


_ANY = pl.BlockSpec(memory_space=pl.ANY)
MESH_IDS = pl.DeviceIdType.MESH


def _slot(ref, axis, j):
    return ref.at[(slice(None),) * axis + (j,)]


def _all_gather(blocks, axis, name):
    nt = len(blocks)

    def body(*refs):
        x_refs, o_refs = refs[:nt], refs[nt:2 * nt]
        send_sems, recv_sems, local_sems = refs[2 * nt:]
        x, y, c = lax.axis_index("x"), lax.axis_index("y"), lax.axis_index("c")
        me, sibling = (x, y, c), (x, y, 1 - c)
        chips = [(1 - x, y), (x, 1 - y), (1 - x, 1 - y)]

        def slot(t, px, py, pc):
            return _slot(o_refs[t], axis, 4 * px + 2 * py + pc)

        def copy(t, k, blk, to, src=None):
            return pltpu.make_async_remote_copy(
                src_ref=slot(t, *blk) if src is None else src, dst_ref=slot(t, *blk),
                send_sem=send_sems.at[7 * t + k], recv_sem=recv_sems.at[7 * t + k],
                device_id=to, device_id_type=MESH_IDS)

        mine = [pltpu.make_async_copy(x_refs[t], slot(t, *me), local_sems.at[t]) for t in range(nt)]
        for cp in mine:
            cp.start()
        first = []
        for t in range(nt):
            first.append(copy(t, 0, me, sibling, src=x_refs[t]))
            first += [copy(t, 1 + j, me, (*chip, c), src=x_refs[t]) for j, chip in enumerate(chips)]
        for cp in first:
            cp.start()
        passed = []
        for t in range(nt):
            for j, chip in enumerate(chips):
                copy(t, 1 + j, (*chip, c), me).wait_recv()
                passed.append(copy(t, 4 + j, (*chip, c), sibling))
                passed[-1].start()
        for t in range(nt):
            copy(t, 0, sibling, me).wait_recv()
            for j, chip in enumerate(chips):
                copy(t, 4 + j, (*chip, 1 - c), me).wait_recv()
        for cp in first + passed:
            cp.wait_send()
        for cp in mine:
            cp.wait()

    out_shape = [jax.ShapeDtypeStruct(b.shape[:axis] + (N_DEV,) + b.shape[axis:], b.dtype) for b in blocks]
    return pl.pallas_call(
        body, name=name, out_shape=out_shape, in_specs=[_ANY] * nt, out_specs=[_ANY] * nt,
        scratch_shapes=[pltpu.SemaphoreType.DMA((7 * nt,)), pltpu.SemaphoreType.DMA((7 * nt,)),
                        pltpu.SemaphoreType.DMA((nt,))],
    )(*blocks)


def _peer(k):
    x, y, c = lax.axis_index("x"), lax.axis_index("y"), lax.axis_index("c")
    px = 1 - x if k & 4 else x
    py = 1 - y if k & 2 else y
    pc = 1 - c if k & 1 else c
    return (px, py, pc), 4 * px + 2 * py + pc


def _all_to_all(blocks, name):
    nt = len(blocks)

    def body(*refs):
        x_refs, o_refs = refs[:nt], refs[nt:2 * nt]
        send_sems, recv_sems, local_sems = refs[2 * nt:]
        _, me = _peer(0)
        mine = [pltpu.make_async_copy(x_refs[t].at[me], o_refs[t].at[me], local_sems.at[t]) for t in range(nt)]
        for cp in mine:
            cp.start()
        copies = []
        for k in range(1, N_DEV):
            peer, idx = _peer(k)
            for t in range(nt):
                cp = pltpu.make_async_remote_copy(
                    src_ref=x_refs[t].at[idx], dst_ref=o_refs[t].at[me],
                    send_sem=send_sems.at[7 * t + k - 1], recv_sem=recv_sems.at[7 * t + k - 1],
                    device_id=peer, device_id_type=MESH_IDS)
                cp.start()
                copies.append(cp)
        for cp in copies:
            cp.wait()
        for cp in mine:
            cp.wait()

    return pl.pallas_call(
        body, name=name, out_shape=[jax.ShapeDtypeStruct(b.shape, b.dtype) for b in blocks],
        in_specs=[_ANY] * nt, out_specs=[_ANY] * nt,
        scratch_shapes=[pltpu.SemaphoreType.DMA((7 * nt,)), pltpu.SemaphoreType.DMA((7 * nt,)),
                        pltpu.SemaphoreType.DMA((nt,))],
    )(*blocks)


_HBM = pl.BlockSpec(memory_space=pltpu.HBM)
_SEM = pl.BlockSpec(memory_space=pltpu.SEMAPHORE)
_EFFECT = pltpu.SideEffectType.DATAFLOW_SIDE_EFFECTING


def _split_copy(src_ref, land_ref, send_sems, recv_sems, t, k, gather):
    peer, idx = _peer(k)
    _, me = _peer(0)
    return pltpu.make_async_remote_copy(
        src_ref=src_ref if gather else src_ref.at[idx], dst_ref=land_ref.at[me],
        send_sem=send_sems.at[7 * t + k - 1], recv_sem=recv_sems.at[7 * t + k - 1],
        device_id=peer, device_id_type=MESH_IDS)


def _exchange_start(srcs, lands, gather, name):
    nt = len(srcs)

    def body(*refs):
        src_refs, land_refs = refs[:nt], refs[nt:2 * nt]
        send_sems, recv_sems = refs[2 * nt:2 * nt + 2]
        token = refs[-1]
        for k in range(1, N_DEV):
            for t in range(nt):
                _split_copy(src_refs[t], land_refs[t], send_sems, recv_sems, t, k, gather).start()
        token[...] = jnp.zeros_like(token)

    hbm = lambda a: pltpu.HBM(a.shape, a.dtype)
    outs = pl.pallas_call(
        body, name=name,
        out_shape=(pltpu.SemaphoreType.DMA((7 * nt,)), pltpu.SemaphoreType.DMA((7 * nt,)),
                   *[hbm(a) for a in srcs], *[hbm(a) for a in lands], jax.ShapeDtypeStruct((8, 128), F32)),
        in_specs=[_HBM] * (2 * nt),
        out_specs=(_SEM, _SEM, *[_HBM] * (2 * nt), pl.BlockSpec(memory_space=pltpu.VMEM)),
        input_output_aliases={j: 2 + j for j in range(2 * nt)},
        compiler_params=pltpu.CompilerParams(has_side_effects=_EFFECT),
    )(*[pltpu.with_memory_space_constraint(a, pltpu.HBM) for a in list(srcs) + list(lands)])
    return outs[0], outs[1], outs[2:2 + nt], outs[2 + nt:2 + 2 * nt], outs[-1]


def _exchange_wait(send_sems, recv_sems, srcs, lands, after, gather, name):
    nt = len(srcs)

    def body(*refs):
        src_refs, land_refs = refs[:nt], refs[nt:2 * nt]
        send_sems, recv_sems = refs[2 * nt:2 * nt + 2]
        for k in range(1, N_DEV):
            _, idx = _peer(k)
            for t in range(nt):
                _split_copy(src_refs[t], land_refs[t], send_sems, recv_sems, t, k, gather).wait_send()
                arrival = pltpu.make_async_remote_copy(
                    src_ref=land_refs[t].at[idx], dst_ref=land_refs[t].at[idx],
                    send_sem=send_sems.at[7 * t + k - 1], recv_sem=recv_sems.at[7 * t + k - 1],
                    device_id=_peer(k)[0], device_id_type=MESH_IDS)
                arrival.wait_recv()

    hbm = lambda a: pltpu.HBM(a.shape, a.dtype)
    outs = pl.pallas_call(
        body, name=name, out_shape=tuple(hbm(a) for a in list(srcs) + list(lands)),
        in_specs=[_HBM] * (2 * nt) + [_SEM, _SEM, _ANY], out_specs=tuple([_HBM] * (2 * nt)),
        input_output_aliases={j: j for j in range(2 * nt)},
        compiler_params=pltpu.CompilerParams(has_side_effects=_EFFECT),
    )(*srcs, *lands, send_sems, recv_sems, after)
    return outs[nt:]


def _adamw(parts, w, m, v, name, tr):
    n_layers, r, c_ = w.shape
    assert len(parts) == n_layers

    def body(*refs):
        p_refs = refs[:n_layers]
        w_ref, m_ref, v_ref, g_ref, d_ref, mo_ref, vo_ref = refs[n_layers:]

        def update(p_ref):
            g = p_ref[0].astype(F32)
            for j in range(1, N_DEV):
                g = g + p_ref[j].astype(F32)
            m2 = ADAM_B1 * m_ref[...] + (1.0 - ADAM_B1) * g
            v2 = ADAM_B2 * v_ref[...] + (1.0 - ADAM_B2) * (g * g)
            m_hat = m2 / (1.0 - ADAM_B1 ** ADAM_STEP)
            v_hat = v2 / (1.0 - ADAM_B2 ** ADAM_STEP)
            g_ref[...] = g
            d_ref[...] = -ADAM_LR * (m_hat / (jnp.sqrt(v_hat) + ADAM_EPS) + ADAM_WD * w_ref[...])
            mo_ref[...] = m2
            vo_ref[...] = v2

        for layer in range(n_layers):
            pl.when(pl.program_id(0) == layer)(lambda layer=layer: update(p_refs[layer]))

    spec = pl.BlockSpec((None, tr, c_), lambda l, i: (l, i, 0))
    p_spec = pl.BlockSpec((N_DEV, tr, c_), lambda l, i: (0, i, 0))
    return pl.pallas_call(
        body, name=name, grid=(n_layers, r // tr), in_specs=[p_spec] * n_layers + [spec] * 3,
        out_specs=[spec] * 4, out_shape=[_sds((n_layers, r, c_))] * 4,
        compiler_params=_params(("parallel", "parallel")),
    )(*parts, w, m, v)


def _pack_rows(n_elems, align):
    rows = -(-n_elems // PACK_COLS)
    return -(-rows // align) * align


def _pack(arrs, rows, dtype=F32):
    flat = jnp.concatenate([a.reshape(-1) for a in arrs]).astype(dtype)
    return jnp.pad(flat, (0, rows * PACK_COLS - flat.shape[0])).reshape(rows, PACK_COLS)


def _unpack(pack, shapes):
    flat = pack.reshape(-1)
    out, off = [], 0
    for shp in shapes:
        size = int(np.prod(shp))
        out.append(flat[off:off + size].reshape(shp))
        off += size
    return out


def _tile_rows(rows, target, align=16):
    best = align
    for t in range(align, target + 1, align):
        if rows % t == 0:
            best = t
    return best


COMM_NAMES = ("w_in", "ssm_glu_w", "w_out", "ffn_w_up", "ffn_w_down", "ple_w_gate", "ple_w_proj")
COMM_TRANSPOSED = ("w_in", "ffn_w_up", "ple_w_proj")
COMM_EARLY = ("ple_w_proj", "ple_w_gate", "ffn_w_down", "ffn_w_up")
COMM_LATE = ("w_in", "ssm_glu_w", "w_out")
SMALL_TILE_ROWS = 64
CONV_NAME = "ffn_conv_w"
UNPACKED = ("ssm_b_re", "ssm_b_im", "ssm_c_re", "ssm_c_im")


def _to_comm(name, a):
    return jnp.swapaxes(a, 1, 2) if name in COMM_TRANSPOSED else a


def kernel(x, p, rel_bias, norm_attn_g, w_in, sgu_ln_g, sgu_ln_b, sgu_w, sgu_b, ssm_a_re, ssm_a_im, ssm_log_dt, ssm_b_re, ssm_b_im, ssm_c_re, ssm_c_im, ssm_d, ssm_glu_w, ssm_glu_b, branch_norm_g, w_out, norm_ffn_g, ffn_w_up, ffn_conv_w, ffn_conv_b, ffn_w_down, norm_ple_g, ple_w_gate, ple_w_proj, final_norm_g, loss_target, m_rel_bias, m_norm_attn_g, m_w_in, m_sgu_ln_g, m_sgu_ln_b, m_sgu_w, m_sgu_b, m_ssm_a_re, m_ssm_a_im, m_ssm_log_dt, m_ssm_b_re, m_ssm_b_im, m_ssm_c_re, m_ssm_c_im, m_ssm_d, m_ssm_glu_w, m_ssm_glu_b, m_branch_norm_g, m_w_out, m_norm_ffn_g, m_ffn_w_up, m_ffn_conv_w, m_ffn_conv_b, m_ffn_w_down, m_norm_ple_g, m_ple_w_gate, m_ple_w_proj, m_final_norm_g, v_rel_bias, v_norm_attn_g, v_w_in, v_sgu_ln_g, v_sgu_ln_b, v_sgu_w, v_sgu_b, v_ssm_a_re, v_ssm_a_im, v_ssm_log_dt, v_ssm_b_re, v_ssm_b_im, v_ssm_c_re, v_ssm_c_im, v_ssm_d, v_ssm_glu_w, v_ssm_glu_b, v_branch_norm_g, v_w_out, v_norm_ffn_g, v_ffn_w_up, v_ffn_conv_w, v_ffn_conv_b, v_ffn_w_down, v_norm_ple_g, v_ple_w_gate, v_ple_w_proj, v_final_norm_g):
    given = dict(locals())
    w = {n: given[n] for n in WEIGHT_NAMES}
    m = {n: given["m_" + n] for n in WEIGHT_NAMES}
    v = {n: given["v_" + n] for n in WEIGHT_NAMES}
    depth = p.shape[0]
    dev = 4 * lax.axis_index("x") + 2 * lax.axis_index("y") + lax.axis_index("c")

    wc = {n: _to_comm(n, w[n]) for n in COMM_NAMES}
    wb = {n: wc[n].astype(BF16) for n in COMM_NAMES}
    conv_local = [w[CONV_NAME], m[CONV_NAME], v[CONV_NAME]]
    conv_rows = _pack_rows(sum(int(np.prod(t.shape)) for t in conv_local), 8)
    conv_g, = _all_gather([_pack(conv_local, conv_rows)], 0, "gather_conv_taps")
    conv_parts = zip(*[_unpack(conv_g[j], [t.shape for t in conv_local]) for j in range(N_DEV)])
    conv_w, conv_m, conv_v = [jnp.concatenate(parts, axis=2) for parts in conv_parts]
    small = {n: w[n] for n in SMALL_NAMES}

    def whole(names, blocks):
        return {n: t.reshape(-1, t.shape[-1]) for n, t in zip(names, blocks)}

    def own_slot(block):
        return lax.dynamic_update_slice_in_dim(jnp.zeros((N_DEV,) + block.shape, block.dtype), block[None], dev, 0)

    def start_gather(names, i, after):
        srcs, after = lax.optimization_barrier(([wb[n][i] for n in names], after))
        return _exchange_start(srcs, [own_slot(s) for s in srcs], True, "gather_weights_%d_start" % i), after

    def wait_gather(names, i, started, after):
        send_sems, recv_sems, srcs, lands, _ = started
        return whole(names, _exchange_wait(send_sems, recv_sems, srcs, lands, after, True,
                                           "gather_weights_%d_wait" % i))

    at_once = ("w_in", "ssm_glu_w")
    later = tuple(n for n in COMM_NAMES if n not in at_once)
    w_in_0 = _all_gather([wb[n][0] for n in at_once], 0, "gather_w_in_0")
    gathering = {}
    gathering[0], (w_in_0, _) = start_gather(later, 0, (w_in_0, conv_g))
    small["norm_attn_g"] = small["norm_attn_g"] + gathering[0][4][0, 0]

    def layer_weights(i, h):
        if i > 0:
            got = wait_gather(COMM_NAMES, i, gathering.pop(i), h)
            if i + 1 < depth:
                gathering[i + 1], ordered = start_gather(COMM_NAMES, i + 1, got["w_in"])
                got["w_in"] = ordered + gathering[i + 1][4][0, 0].astype(BF16)
            return dict(got, **{CONV_NAME: conv_w[i]})

        def rest(after):
            got = wait_gather(later, 0, gathering.pop(0), after)
            if depth > 1:
                gathering[1], ordered = start_gather(COMM_NAMES, 1, got["w_out"])
                got["w_out"] = ordered + gathering[1][4][0, 0].astype(BF16)
            return got

        return dict(whole(at_once, w_in_0), **{CONV_NAME: conv_w[0], "rest": rest})

    def as_slots(g, n):
        return g.reshape((N_DEV,) + wc[n].shape[1:])

    scattering = {}

    def layer_done(i, stage, g, small_now):
        if stage == "all" and i == 0:
            return small_now
        names = COMM_EARLY if stage == "ffn" else COMM_LATE
        srcs = [as_slots(g[n], n) for n in names]
        lands = [own_slot(lax.dynamic_index_in_dim(s, dev, 0, keepdims=False)) for s in srcs]
        started = _exchange_start(srcs, lands, False, "scatter_weight_grads_%d_%s_start" % (i, stage))
        scattering[i, stage] = (names, started)
        pin = "branch_norm_g" if stage == "ffn" else "norm_ple_g"
        return dict(small_now, **{pin: small_now[pin] + started[4][0, 0]})

    loss, dx, big_grads, grads = _local_step(x[0], p[:, 0], loss_target[0], layer_weights, small, layer_done)
    loss = lax.psum(loss, ("x", "y", "c"))

    recv = [{} for _ in range(depth)]
    last = _all_to_all([as_slots(big_grads[0][n], n) for n in COMM_LATE], "scatter_weight_grads_0_all")
    recv[0].update(zip(COMM_LATE, last))
    for (i, stage), (names, (send_sems, recv_sems, srcs, lands, _)) in scattering.items():
        got = _exchange_wait(send_sems, recv_sems, srcs, lands, dx, False,
                             "scatter_weight_grads_%d_%s_wait" % (i, stage))
        recv[i].update(zip(names, got))
    rep_names = tuple(n for n in SMALL_NAMES + (CONV_NAME,) if n not in UNPACKED)
    rep_w = dict({n: w[n] for n in SMALL_NAMES}, **{CONV_NAME: conv_w})
    rep_m = dict({n: m[n] for n in SMALL_NAMES}, **{CONV_NAME: conv_m})
    rep_v = dict({n: v[n] for n in SMALL_NAMES}, **{CONV_NAME: conv_v})
    rep_shapes = [rep_w[n].shape for n in rep_names]
    rep_rows = _pack_rows(sum(int(np.prod(s)) for s in rep_shapes), SMALL_TILE_ROWS)
    as_rows = lambda t: t.reshape(-1, t.shape[-1])
    rep_parts, *own_parts = _all_gather(
        [_pack([grads[n] for n in rep_names], rep_rows)] + [as_rows(grads[n]) for n in UNPACKED], 0,
        "gather_small_grads")

    out = {}
    for n in COMM_NAMES:
        res = _adamw([recv[i][n] for i in range(depth)], wc[n], _to_comm(n, m[n]), _to_comm(n, v[n]),
                     "adamw_" + n, _tile_rows(wc[n].shape[1], 256))
        out[n] = [_to_comm(n, r) for r in res]
    rep_out = _adamw([rep_parts], *[_pack([src[n] for n in rep_names], rep_rows)[None] for src in (rep_w, rep_m, rep_v)],
                     "adamw_replicated", SMALL_TILE_ROWS)
    for n, vals in zip(rep_names, zip(*[_unpack(r[0], rep_shapes) for r in rep_out])):
        out[n] = list(vals)
    for n, parts in zip(UNPACKED, own_parts):
        res = _adamw([parts], as_rows(w[n])[None], as_rows(m[n])[None], as_rows(v[n])[None], "adamw_" + n,
                     SMALL_TILE_ROWS)
        out[n] = [r.reshape(w[n].shape) for r in res]
    shard = ffn_conv_w.shape[2]
    out[CONV_NAME] = [lax.dynamic_slice_in_dim(t, dev * shard, shard, axis=2) for t in out[CONV_NAME]]
    results = [[out[n][kind] for n in WEIGHT_NAMES] for kind in range(4)]
    return (loss, dx[None], *results[0], *results[1], *results[2], *results[3])
```

```python
import math

import numpy as np
import jax
import jax.numpy as jnp
from jax import lax
from jax.experimental import pallas as pl
from jax.experimental.pallas import tpu as pltpu

F32 = jnp.float32
BF16 = jnp.bfloat16

D_MODEL = 1024
HEAD_DIM = 64
N_HEADS = 8
ATTN_W = 512
SGU_W = 256
SGU_GROUPS = 4
SGU_CHUNK = 128
SSM_W = 256
SSM_GROUPS = 16
SSM_CH = 16
SSM_STATE = 64
SSM_NS = SSM_GROUPS * SSM_STATE
IN_W = 2304
D_FF = 2816
PLE_DIM = 256
BRANCHES = ((128, 1), (512, 4), (2048, 16))
BLK = 128
N_BUCKETS = 32
REL_MAX = 2048
EPS = 1e-6
NEG_INF = -1e30
N_DEV = 8

ADAM_LR = 0.001
ADAM_B1 = 0.9
ADAM_B2 = 0.999
ADAM_EPS = 1e-08
ADAM_WD = 0.01
ADAM_STEP = 10

VMEM_LIMIT_BYTES = 56 * 1024 * 1024
GELU_C = math.sqrt(2.0 / math.pi)

SMALL_NAMES = ("rel_bias", "norm_attn_g", "sgu_ln_g", "sgu_ln_b", "sgu_w", "sgu_b", "ssm_a_re", "ssm_a_im",
               "ssm_log_dt", "ssm_b_re", "ssm_b_im", "ssm_c_re", "ssm_c_im", "ssm_d", "ssm_glu_b",
               "branch_norm_g", "norm_ffn_g", "ffn_conv_b", "norm_ple_g", "final_norm_g")
WEIGHT_NAMES = ("rel_bias", "norm_attn_g", "w_in", "sgu_ln_g", "sgu_ln_b", "sgu_w", "sgu_b", "ssm_a_re",
                "ssm_a_im", "ssm_log_dt", "ssm_b_re", "ssm_b_im", "ssm_c_re", "ssm_c_im", "ssm_d", "ssm_glu_w",
                "ssm_glu_b", "branch_norm_g", "w_out", "norm_ffn_g", "ffn_w_up", "ffn_conv_w", "ffn_conv_b",
                "ffn_w_down", "norm_ple_g", "ple_w_gate", "ple_w_proj", "final_norm_g")
PACK_COLS = 512


def _params(sem):
    return pltpu.CompilerParams(dimension_semantics=sem, vmem_limit_bytes=VMEM_LIMIT_BYTES)


def _pick(dim, target):
    if dim <= target:
        return dim
    best = None
    for t in range(128, target + 1, 128):
        if dim % t == 0:
            best = t
    return dim if best is None else best


def _gelu(x):
    return 0.5 * x * (1.0 + jnp.tanh(GELU_C * (x + 0.044715 * (x * x * x))))


def _gelu_grad(x):
    t = jnp.tanh(GELU_C * (x + 0.044715 * (x * x * x)))
    return 0.5 * (1.0 + t) + 0.5 * x * (1.0 - t * t) * (GELU_C * (1.0 + 3.0 * 0.044715 * (x * x)))


def _sigmoid(x):
    return 1.0 / (1.0 + jnp.exp(-x))


_DIMS = {"nn": (((1,), (0,)), ((), ())), "tn": (((0,), (0,)), ((), ())), "nt": (((1,), (1,)), ((), ()))}


def _mm(a, b, mode, name, add=None, out_dtype=F32, norm_gain=None, norm_bwd=None, tm=1408, tn=1408, tk=1408):
    if mode == "nn":
        m, k = a.shape
        k2, n = b.shape
    elif mode == "tn":
        k, m = a.shape
        k2, n = b.shape
    else:
        m, k = a.shape
        n, k2 = b.shape
    assert k == k2, (name, a.shape, b.shape, mode)
    tm, tn, tk = _pick(m, tm), _pick(n, tn), _pick(k, tk)
    nk = k // tk
    dims = _DIMS[mode]
    has_add = add is not None
    has_norm = norm_gain is not None
    has_nbwd = norm_bwd is not None
    assert not (has_norm or has_nbwd) or tn == n

    def body(*refs):
        a_ref, b_ref = refs[:2]
        rest = list(refs[2:])
        add_ref = rest.pop(0) if has_add else None
        g_ref = rest.pop(0) if has_norm else None
        h_ref, hg_ref = (rest.pop(0), rest.pop(0)) if has_nbwd else (None, None)
        o_ref = rest.pop(0)
        n_ref = rest.pop(0) if has_norm else None
        dg_ref = rest.pop(0) if has_nbwd else None
        part = lax.dot_general(a_ref[...].astype(BF16), b_ref[...].astype(BF16), dims,
                               preferred_element_type=F32)
        if has_nbwd:
            @pl.when((pl.program_id(0) == 0) & (pl.program_id(2) == 0))
            def _():
                dg_ref[...] = jnp.zeros_like(dg_ref)

        def finish(r):
            if has_nbwd:
                x = h_ref[...]
                scale = lax.rsqrt(jnp.mean(x * x, axis=-1, keepdims=True) + EPS)
                xh = x * scale
                dg_ref[...] += jnp.sum(r * xh, axis=0, keepdims=True)
                dxh = r * hg_ref[...]
                r = scale * (dxh - xh * jnp.mean(dxh * xh, axis=-1, keepdims=True))
            if has_add:
                r = r + add_ref[...]
            o_ref[...] = r.astype(out_dtype)
            if has_norm:
                scale = lax.rsqrt(jnp.mean(r * r, axis=-1, keepdims=True) + EPS)
                n_ref[...] = (r * scale * g_ref[...]).astype(BF16)

        if nk == 1:
            finish(part)
            return
        acc_ref = refs[-1]
        kk = pl.program_id(2)

        @pl.when(kk == 0)
        def _():
            acc_ref[...] = part

        @pl.when((kk > 0) & (kk < nk - 1))
        def _():
            acc_ref[...] += part

        @pl.when(kk == nk - 1)
        def _():
            finish(acc_ref[...] + part)

    if mode == "tn":
        a_spec = pl.BlockSpec((tk, tm), lambda i, j, kk: (kk, i))
    else:
        a_spec = pl.BlockSpec((tm, tk), lambda i, j, kk: (i, kk))
    if mode == "nt":
        b_spec = pl.BlockSpec((tn, tk), lambda i, j, kk: (j, kk))
    else:
        b_spec = pl.BlockSpec((tk, tn), lambda i, j, kk: (kk, j))
    o_spec = pl.BlockSpec((tm, tn), lambda i, j, kk: (i, j))
    in_specs = [a_spec, b_spec] + ([o_spec] if has_add else [])
    args = (a, b) + ((add,) if has_add else ())
    out_specs, out_shape = o_spec, jax.ShapeDtypeStruct((m, n), out_dtype)
    if has_norm:
        in_specs.append(pl.BlockSpec((1, n), lambda i, j, kk: (0, 0)))
        args += (norm_gain.reshape(1, n),)
        out_specs, out_shape = [o_spec, o_spec], [out_shape, jax.ShapeDtypeStruct((m, n), BF16)]
    if has_nbwd:
        row_spec = pl.BlockSpec((1, n), lambda i, j, kk: (0, 0))
        in_specs += [o_spec, row_spec]
        args += (norm_bwd[0], norm_bwd[1].reshape(1, n))
        out_specs, out_shape = [o_spec, row_spec], [out_shape, jax.ShapeDtypeStruct((1, n), F32)]
    sem = ("arbitrary",) * 3 if has_nbwd else ("parallel", "parallel", "arbitrary")
    return pl.pallas_call(
        body, name=name, grid=(m // tm, n // tn, nk),
        in_specs=in_specs, out_specs=out_specs, out_shape=out_shape,
        scratch_shapes=[pltpu.VMEM((tm, tn), F32)] if nk > 1 else [], compiler_params=_params(sem),
    )(*args)


def _rb(tm, w, cb=0):
    return pl.BlockSpec((tm, w), lambda i: (i, cb))


def _fb(shape):
    nd = len(shape)
    return pl.BlockSpec(shape, lambda i: (0,) * nd)


def _rowcall(body, name, n_rows, tm, in_specs, args, out_specs, out_shapes):
    return pl.pallas_call(
        body, name=name, grid=(n_rows // tm,), in_specs=in_specs, out_specs=out_specs, out_shape=out_shapes,
        compiler_params=_params(("arbitrary",)),
    )(*args)


def _sds(shape, dtype=F32):
    return jax.ShapeDtypeStruct(shape, dtype)


def _rms_fwd(h, g, name, tm=512):
    s, d = h.shape

    def body(h_ref, g_ref, o_ref):
        x = h_ref[...]
        r = lax.rsqrt(jnp.mean(x * x, axis=-1, keepdims=True) + EPS)
        o_ref[...] = (x * r * g_ref[...]).astype(BF16)

    return _rowcall(body, name, s, tm, [_rb(tm, d), _fb((1, d))], (h, g.reshape(1, d)), _rb(tm, d),
                    _sds((s, d), BF16))


def _loss_head(h, target, g, name, tm=512):
    s, d = h.shape

    def body(h_ref, t_ref, g_ref, dh_ref, loss_ref, dg_ref):
        @pl.when(pl.program_id(0) == 0)
        def _():
            dg_ref[...] = jnp.zeros_like(dg_ref)
            loss_ref[...] = jnp.zeros_like(loss_ref)

        x = h_ref[...]
        r = lax.rsqrt(jnp.mean(x * x, axis=-1, keepdims=True) + EPS)
        xh = x * r
        gg = g_ref[...]
        err = xh * gg - t_ref[...]
        loss_ref[...] += jnp.sum(err * err) * (0.5 / d)
        dy = err * (1.0 / d)
        dg_ref[...] += jnp.sum(dy * xh, axis=0, keepdims=True)
        dxh = dy * gg
        dh_ref[...] = r * (dxh - xh * jnp.mean(dxh * xh, axis=-1, keepdims=True))

    dh, loss, dg = _rowcall(body, name, s, tm, [_rb(tm, d), _rb(tm, d), _fb((1, d))], (h, target, g.reshape(1, d)),
                            [_rb(tm, d), _fb((1, 128)), _fb((1, d))], [_sds((s, d)), _sds((1, 128)), _sds((1, d))])
    return dh, loss[0, 0], dg.reshape(d)


_MIX_PARTS = ((0, 512), (512, 768), (768, 1024))


def _mix_fwd(ya, ysg, yss, g, name, tm=512):
    s = ya.shape[0]

    def body(a_ref, b_ref, c_ref, g_ref, o_ref):
        for ref, (lo, hi) in zip((a_ref, b_ref, c_ref), _MIX_PARTS):
            y = ref[...]
            r = lax.rsqrt(jnp.mean(y * y, axis=-1, keepdims=True) + EPS)
            o_ref[:, lo:hi] = (y * r * g_ref[:, lo:hi]).astype(BF16)

    return _rowcall(body, name, s, tm, [_rb(tm, 512), _rb(tm, 256), _rb(tm, 256), _fb((1, 1024))],
                    (ya, ysg, yss, g.reshape(1, 1024)), _rb(tm, 1024), _sds((s, 1024), BF16))


def _mix_bwd(dmix, ya, ysg, yss, g, name, tm=512):
    s = ya.shape[0]

    def body(dm_ref, a_ref, b_ref, c_ref, g_ref, da_ref, db_ref, dc_ref, dg_ref):
        @pl.when(pl.program_id(0) == 0)
        def _():
            dg_ref[...] = jnp.zeros_like(dg_ref)

        for ref, dref, (lo, hi) in zip((a_ref, b_ref, c_ref), (da_ref, db_ref, dc_ref), _MIX_PARTS):
            y = ref[...]
            r = lax.rsqrt(jnp.mean(y * y, axis=-1, keepdims=True) + EPS)
            xh = y * r
            dm = dm_ref[:, lo:hi]
            dg_ref[:, lo:hi] += jnp.sum(dm * xh, axis=0, keepdims=True)
            dxh = dm * g_ref[:, lo:hi]
            dref[...] = r * (dxh - xh * jnp.mean(dxh * xh, axis=-1, keepdims=True))

    da, db, dc, dg = _rowcall(
        body, name, s, tm, [_rb(tm, 1024), _rb(tm, 512), _rb(tm, 256), _rb(tm, 256), _fb((1, 1024))],
        (dmix, ya, ysg, yss, g.reshape(1, 1024)),
        [_rb(tm, 512), _rb(tm, 256), _rb(tm, 256), _fb((1, 1024))],
        [_sds((s, 512)), _sds((s, 256)), _sds((s, 256)), _sds((1, 1024))])
    return da, db, dc, dg.reshape(1024)


def _ssm_post_fwd(yc, z, d, gw, gb, name, tm=1024):
    s = yc.shape[0]

    def body(yc_ref, u_ref, d_ref, gw_ref, gb_ref, o_ref):
        y1 = yc_ref[...] + d_ref[...] * u_ref[...]
        y2 = _gelu(y1)
        gl = jnp.dot(y2.astype(BF16), gw_ref[...], preferred_element_type=F32) + gb_ref[...]
        o_ref[...] = y2 * _sigmoid(gl)

    return _rowcall(body, name, s, tm, [_rb(tm, 256), _rb(tm, 256, 8), _fb((1, 256)), _fb((256, 256)), _fb((1, 256))],
                    (yc, z, d.reshape(1, 256), gw, gb.reshape(1, 256)), _rb(tm, 256), _sds((s, 256)))


def _ssm_post_bwd(dy, yc, z, d, gw, gb, name, tm=1024):
    s = yc.shape[0]

    def body(dy_ref, yc_ref, u_ref, d_ref, gw_ref, gb_ref, dy1_ref, dgl_ref, y2_ref, dud_ref, dd_ref, dgb_ref):
        @pl.when(pl.program_id(0) == 0)
        def _():
            dd_ref[...] = jnp.zeros_like(dd_ref)
            dgb_ref[...] = jnp.zeros_like(dgb_ref)

        u = u_ref[...]
        dd = d_ref[...]
        y1 = yc_ref[...] + dd * u
        y2 = _gelu(y1)
        gw_v = gw_ref[...]
        gl = jnp.dot(y2.astype(BF16), gw_v, preferred_element_type=F32) + gb_ref[...]
        sg = _sigmoid(gl)
        dyv = dy_ref[...]
        dgl = dyv * y2 * sg * (1.0 - sg)
        dy2 = dyv * sg + lax.dot_general(dgl.astype(BF16), gw_v, _DIMS["nt"], preferred_element_type=F32)
        dy1 = dy2 * _gelu_grad(y1)
        dy1_ref[...] = dy1.astype(BF16)
        dgl_ref[...] = dgl.astype(BF16)
        y2_ref[...] = y2.astype(BF16)
        dud_ref[...] = dy1 * dd
        dd_ref[...] += jnp.sum(dy1 * u, axis=0, keepdims=True)
        dgb_ref[...] += jnp.sum(dgl, axis=0, keepdims=True)

    outs = _rowcall(
        body, name, s, tm,
        [_rb(tm, 256), _rb(tm, 256), _rb(tm, 256, 8), _fb((1, 256)), _fb((256, 256)), _fb((1, 256))],
        (dy, yc, z, d.reshape(1, 256), gw, gb.reshape(1, 256)),
        [_rb(tm, 256)] * 4 + [_fb((1, 256))] * 2,
        [_sds((s, 256), BF16)] * 3 + [_sds((s, 256))] + [_sds((1, 256))] * 2)
    dy1, dgl, y2, dud, dd, dgb = outs
    return dy1, dgl, y2, dud, dd.reshape(256), dgb.reshape(256)


SCAN_T = 512
N_SCAN_TABLES = 6


def _scan_tables(pr, pi, reverse):
    ns = pr.shape[0]
    sign = -1.0 if reverse else 1.0
    power = [(jnp.ones((ns,), F32), jnp.zeros((ns,), F32))] + [(pr[:, k], sign * pi[:, k]) for k in range(8)]
    zero = (jnp.zeros((ns,), F32), jnp.zeros((ns,), F32))

    def table(exponents):
        rows = [zero if e is None else power[e] for e in exponents]
        return jnp.stack([jnp.concatenate(row) for row in rows])

    tabs = []
    for k in (1, 2, 4):
        has_partner = [(s < 8 - k) if reverse else (s >= k) for s in range(8)]
        tabs.append(table([k if ok else None for ok in has_partner]))
    tabs.append(table([s if reverse else 7 - s for s in range(8)]))
    tabs.append(table([8 - s if reverse else s + 1 for s in range(8)]))
    tabs.append(table([8] * 8))
    return jnp.stack(tabs)


def _cmul(ar, ai, br, bi):
    return ar * br - ai * bi, ar * bi + ai * br


def _scan_group(ur, ui, cr, ci, tr_ref, ti_ref, reverse):
    xr, xi = ur, ui
    for n, k in enumerate((1, 2, 4)):
        shift = 8 - k if reverse else k
        pr, pi = _cmul(tr_ref[n], ti_ref[n], pltpu.roll(xr, shift, axis=0), pltpu.roll(xi, shift, axis=0))
        xr, xi = xr + pr, xi + pi
    sr, si = _cmul(tr_ref[3], ti_ref[3], ur, ui)
    for k in (1, 2, 4):
        sr, si = sr + pltpu.roll(sr, k, axis=0), si + pltpu.roll(si, k, axis=0)
    pr, pi = _cmul(tr_ref[4], ti_ref[4], cr, ci)
    nr, ni = _cmul(tr_ref[5], ti_ref[5], cr, ci)
    return xr + pr, xi + pi, nr + sr, ni + si


def _table_halves(t_ref):
    return t_ref.at[:, :, pl.ds(0, SSM_NS)], t_ref.at[:, :, pl.ds(SSM_NS, SSM_NS)]


_U_BLOCK = (IN_W - SSM_W) // SSM_W


def _ssm_fwd(z, bdt, cd, tabs, name):
    s = z.shape[0]
    ns = SSM_NS
    n_t = s // SCAN_T

    def body(u_ref, b_ref, c_ref, t_ref, xr_ref, xi_ref, y_ref, cr_ref, ci_ref, ur_ref, ui_ref):
        @pl.when(pl.program_id(0) == 0)
        def _():
            cr_ref[...] = jnp.zeros_like(cr_ref)
            ci_ref[...] = jnp.zeros_like(ci_ref)

        bu = lax.dot_general(u_ref[...].astype(BF16), b_ref[...], _DIMS["nt"], preferred_element_type=F32)
        ur_ref[...] = bu[:, :ns]
        ui_ref[...] = bu[:, ns:]
        tr_ref, ti_ref = _table_halves(t_ref)

        def group(g, carry):
            rows = pl.ds(pl.multiple_of(g * 8, 8), 8)
            xr, xi, cr, ci = _scan_group(ur_ref[rows, :], ui_ref[rows, :], *carry, tr_ref, ti_ref, False)
            xr_ref[rows, :] = xr
            xi_ref[rows, :] = xi
            return cr, ci

        cr, ci = lax.fori_loop(0, SCAN_T // 8, group, (cr_ref[...], ci_ref[...]), unroll=2)
        cr_ref[...] = cr
        ci_ref[...] = ci
        y_ref[...] = (jnp.dot(xr_ref[...].astype(BF16), c_ref[0:ns, :], preferred_element_type=F32)
                      + jnp.dot(xi_ref[...].astype(BF16), c_ref[ns:, :], preferred_element_type=F32))

    x_spec = pl.BlockSpec((SCAN_T, ns), lambda t: (t, 0))
    return pl.pallas_call(
        body, name=name, grid=(n_t,),
        in_specs=[pl.BlockSpec((SCAN_T, SSM_W), lambda t: (t, _U_BLOCK)), _fb((2 * ns, SSM_W)),
                  _fb((2 * ns, SSM_W)), _fb((N_SCAN_TABLES, 8, 2 * ns))],
        out_specs=[x_spec, x_spec, _rb(SCAN_T, SSM_W)],
        out_shape=[_sds((s, ns)), _sds((s, ns)), _sds((s, SSM_W))],
        scratch_shapes=[pltpu.VMEM((8, ns), F32)] * 2 + [pltpu.VMEM((SCAN_T, ns), F32)] * 2,
        compiler_params=_params(("arbitrary",)),
    )(z, bdt, cd, tabs)


def _ssm_bwd(dy1, dud, z, xr, xi, bdt, cd, tabs, name):
    s = z.shape[0]
    ns = SSM_NS
    n_t = s // SCAN_T
    n_groups = SCAN_T // 8

    def body(dy_ref, dud_ref, u_ref, xr_ref, xi_ref, pxr_ref, pxi_ref, b_ref, c_ref, t_ref,
             du_ref, dbd_ref, dcd_ref, dar_ref, dai_ref,
             cr_ref, ci_ref, ar_ref, ai_ref, sxr_ref, sxi_ref, gr_ref, gi_ref, lr_ref, li_ref, bacc_ref, cacc_ref):
        t = pl.program_id(0)

        @pl.when(t == 0)
        def _():
            for ref in (cr_ref, ci_ref, ar_ref, ai_ref, bacc_ref, cacc_ref):
                ref[...] = jnp.zeros_like(ref)

        dyb = dy_ref[...]
        g = lax.dot_general(dyb, c_ref[...], _DIMS["nt"], preferred_element_type=F32)
        gr_ref[...] = g[:, :ns]
        gi_ref[...] = g[:, ns:]
        has_before = (t < n_t - 1).astype(F32)
        sxr_ref[0:8, :] = pxr_ref[...] * has_before
        sxi_ref[0:8, :] = pxi_ref[...] * has_before
        sxr_ref[8:, :] = xr_ref[...]
        sxi_ref[8:, :] = xi_ref[...]
        first_row = lax.broadcasted_iota(jnp.int32, (8, ns), 0) == 0
        tr_ref, ti_ref = _table_halves(t_ref)

        def group(k, carry):
            cr, ci, ar, ai = carry
            g8 = pl.multiple_of((n_groups - 1 - k) * 8, 8)
            rows = pl.ds(g8, 8)
            lr, li, cr, ci = _scan_group(gr_ref[rows, :], gi_ref[rows, :], cr, ci, tr_ref, ti_ref, True)
            lr_ref[rows, :] = lr
            li_ref[rows, :] = li
            here, before = pl.ds(g8 + 8, 8), rows
            pr = jnp.where(first_row, pltpu.roll(sxr_ref[before, :], 1, axis=0), pltpu.roll(sxr_ref[here, :], 1, axis=0))
            pi = jnp.where(first_row, pltpu.roll(sxi_ref[before, :], 1, axis=0), pltpu.roll(sxi_ref[here, :], 1, axis=0))
            return cr, ci, ar + lr * pr + li * pi, ai + li * pr - lr * pi

        cr, ci, ar, ai = lax.fori_loop(0, n_groups, group,
                                       (cr_ref[...], ci_ref[...], ar_ref[...], ai_ref[...]), unroll=2)
        cr_ref[...] = cr
        ci_ref[...] = ci
        ar_ref[...] = ar
        ai_ref[...] = ai
        lrb = lr_ref[...].astype(BF16)
        lib = li_ref[...].astype(BF16)
        ub = u_ref[...].astype(BF16)
        du_ref[...] = (dud_ref[...] + jnp.dot(lrb, b_ref[0:ns, :], preferred_element_type=F32)
                       + jnp.dot(lib, b_ref[ns:, :], preferred_element_type=F32))
        bacc_ref[0:ns, :] += lax.dot_general(lrb, ub, _DIMS["tn"], preferred_element_type=F32)
        bacc_ref[ns:, :] += lax.dot_general(lib, ub, _DIMS["tn"], preferred_element_type=F32)
        cacc_ref[0:ns, :] += lax.dot_general(xr_ref[...].astype(BF16), dyb, _DIMS["tn"], preferred_element_type=F32)
        cacc_ref[ns:, :] += lax.dot_general(xi_ref[...].astype(BF16), dyb, _DIMS["tn"], preferred_element_type=F32)

        @pl.when(t == n_t - 1)
        def _():
            for k in (1, 2, 4):
                ar_ref[...] += pltpu.roll(ar_ref[...], k, axis=0)
                ai_ref[...] += pltpu.roll(ai_ref[...], k, axis=0)
            dar_ref[...] = ar_ref[...]
            dai_ref[...] = ai_ref[...]
            dbd_ref[...] = bacc_ref[...]
            dcd_ref[...] = cacc_ref[...]

    rev = lambda t: n_t - 1 - t
    row_spec = pl.BlockSpec((SCAN_T, SSM_W), lambda t: (rev(t), 0))
    x_spec = pl.BlockSpec((SCAN_T, ns), lambda t: (rev(t), 0))
    before_spec = pl.BlockSpec((8, ns), lambda t: (jnp.maximum(rev(t) * (SCAN_T // 8) - 1, 0), 0))
    du, dbd, dcd, dar, dai = pl.pallas_call(
        body, name=name, grid=(n_t,),
        in_specs=[row_spec, row_spec, pl.BlockSpec((SCAN_T, SSM_W), lambda t: (rev(t), _U_BLOCK)),
                  x_spec, x_spec, before_spec, before_spec,
                  _fb((2 * ns, SSM_W)), _fb((2 * ns, SSM_W)), _fb((N_SCAN_TABLES, 8, 2 * ns))],
        out_specs=[row_spec, _fb((2 * ns, SSM_W)), _fb((2 * ns, SSM_W)), _fb((8, ns)), _fb((8, ns))],
        out_shape=[_sds((s, SSM_W)), _sds((2 * ns, SSM_W)), _sds((2 * ns, SSM_W)), _sds((8, ns)), _sds((8, ns))],
        scratch_shapes=([pltpu.VMEM((8, ns), F32)] * 4 + [pltpu.VMEM((SCAN_T + 8, ns), F32)] * 2
                        + [pltpu.VMEM((SCAN_T, ns), F32)] * 4 + [pltpu.VMEM((2 * ns, SSM_W), F32)] * 2),
        compiler_params=_params(("arbitrary",)),
    )(dy1, dud, z, xr, xi, xr, xi, bdt, cd, tabs)
    return du, dbd, dcd, dar[0], dai[0]


def _group_ids():
    return lax.broadcasted_iota(jnp.int32, (1, SGU_W), 1) // 64


def _group_mean(val, gid):
    out = jnp.zeros_like(val)
    for g in range(SGU_GROUPS):
        mg = gid == g
        out = jnp.where(mg, jnp.sum(jnp.where(mg, val, 0.0), axis=1, keepdims=True) * (1.0 / 64), out)
    return out


def _causal_w(w_ref, g):
    t = lax.broadcasted_iota(jnp.int32, (SGU_CHUNK, SGU_CHUNK), 0)
    s = lax.broadcasted_iota(jnp.int32, (SGU_CHUNK, SGU_CHUNK), 1)
    return jnp.where(t >= s, w_ref[g], 0.0).astype(BF16)


def _sgu_core(x, lng, lnb, w_ref, bexp, gid):
    zz = _gelu(x)
    u = zz[:, :SGU_W]
    v = zz[:, SGU_W:]
    vc = v - _group_mean(v, gid)
    rstd = lax.rsqrt(_group_mean(vc * vc, gid) + EPS)
    vhat = vc * rstd
    vn = vhat * lng + lnb
    vnb = vn.astype(BF16)
    mixed = bexp
    for g in range(SGU_GROUPS):
        mm = jnp.dot(_causal_w(w_ref, g), vnb, preferred_element_type=F32)
        mixed = jnp.where(gid == g, mm + bexp, mixed)
    return u, rstd, vhat, vnb, mixed


def _sgu_fwd(z, lng, lnb, w, bexp, name, tm=512):
    s = z.shape[0]

    def body(z_ref, lng_ref, lnb_ref, w_ref, b_ref, o_ref):
        gid = _group_ids()
        for j in range(tm // SGU_CHUNK):
            rows = pl.ds(j * SGU_CHUNK, SGU_CHUNK)
            u, _, _, _, mixed = _sgu_core(z_ref[rows, :], lng_ref[...], lnb_ref[...], w_ref, b_ref[...], gid)
            o_ref[rows, :] = u * mixed

    return _rowcall(body, name, s, tm,
                    [_rb(tm, 512, 3), _fb((1, 256)), _fb((1, 256)), _fb((4, 128, 128)), _fb((128, 256))],
                    (z, lng.reshape(1, 256), lnb.reshape(1, 256), w, bexp), _rb(tm, 256), _sds((s, 256)))


def _sgu_bwd(z, dy, lng, lnb, w, bexp, name, tm=512):
    s = z.shape[0]

    def body(z_ref, dy_ref, lng_ref, lnb_ref, w_ref, b_ref, dz_ref, dw_ref, db_ref, dlng_ref, dlnb_ref):
        @pl.when(pl.program_id(0) == 0)
        def _():
            dw_ref[...] = jnp.zeros_like(dw_ref)
            db_ref[...] = jnp.zeros_like(db_ref)
            dlng_ref[...] = jnp.zeros_like(dlng_ref)
            dlnb_ref[...] = jnp.zeros_like(dlnb_ref)

        gid = _group_ids()
        t = lax.broadcasted_iota(jnp.int32, (SGU_CHUNK, SGU_CHUNK), 0)
        sidx = lax.broadcasted_iota(jnp.int32, (SGU_CHUNK, SGU_CHUNK), 1)
        lng_v = lng_ref[...]
        for j in range(tm // SGU_CHUNK):
            rows = pl.ds(j * SGU_CHUNK, SGU_CHUNK)
            x = z_ref[rows, :]
            u, rstd, vhat, vnb, mixed = _sgu_core(x, lng_v, lnb_ref[...], w_ref, b_ref[...], gid)
            dyv = dy_ref[rows, :]
            dmixed = dyv * u
            du = dyv * mixed
            db_ref[...] += dmixed
            dvn = jnp.zeros_like(dmixed)
            for g in range(SGU_GROUPS):
                dmg = jnp.where(gid == g, dmixed, 0.0).astype(BF16)
                dvn = dvn + lax.dot_general(_causal_w(w_ref, g), dmg, _DIMS["tn"], preferred_element_type=F32)
                dwg = lax.dot_general(dmg, vnb, _DIMS["nt"], preferred_element_type=F32)
                dw_ref[g] += jnp.where(t >= sidx, dwg, 0.0)
            dlnb_ref[...] += jnp.sum(dvn, axis=0, keepdims=True)
            dlng_ref[...] += jnp.sum(dvn * vhat, axis=0, keepdims=True)
            dvh = dvn * lng_v
            dv = rstd * (dvh - _group_mean(dvh, gid) - vhat * _group_mean(dvh * vhat, gid))
            gg = _gelu_grad(x)
            dz_ref[rows, 0:SGU_W] = du * gg[:, :SGU_W]
            dz_ref[rows, SGU_W:2 * SGU_W] = dv * gg[:, SGU_W:]

    dz, dw, db, dlng, dlnb = _rowcall(
        body, name, s, tm,
        [_rb(tm, 512, 3), _rb(tm, 256), _fb((1, 256)), _fb((1, 256)), _fb((4, 128, 128)), _fb((128, 256))],
        (z, dy, lng.reshape(1, 256), lnb.reshape(1, 256), w, bexp),
        [_rb(tm, 512), _fb((4, 128, 128)), _fb((128, 256)), _fb((1, 256)), _fb((1, 256))],
        [_sds((s, 512)), _sds((4, 128, 128)), _sds((128, 256)), _sds((1, 256)), _sds((1, 256))])
    return dz, dw, db, dlng.reshape(256), dlnb.reshape(256)


CONV_TC = 1408
N_CT = D_FF // CONV_TC


def _row_of(block8, j):
    r = lax.broadcasted_iota(jnp.int32, block8.shape, 0)
    return jnp.sum(jnp.where(r == j, block8, 0.0), axis=0, keepdims=True)


EDGE = 16


def _conv_fwd(hu, cw, cb, name, tm=256):
    s = hu.shape[0]

    def body(xv_ref, xg_ref, tv_ref, tg_ref, wv_ref, wg_ref, bv_ref, bg_ref, hv_ref, hg_ref, act_ref):
        has_prev = (pl.program_id(1) > 0).astype(F32)
        row = lax.broadcasted_iota(jnp.int32, (EDGE, CONV_TC), 0)

        def conv(x_ref, t_ref, w_ref, b_ref):
            x = x_ref[...].astype(F32)
            w0, w1, w2, bb = w_ref[0:1, :], w_ref[1:2, :], w_ref[2:3, :], b_ref[...]
            whole = w0 * pltpu.roll(x, 2, axis=0) + w1 * pltpu.roll(x, 1, axis=0) + w2 * x + bb
            tail = t_ref[...].astype(F32)
            r7 = _row_of(tail, EDGE - 1) * has_prev
            r6 = _row_of(tail, EDGE - 2) * has_prev
            xe = x_ref[0:EDGE, :].astype(F32)
            x1 = jnp.where(row == 0, r7, pltpu.roll(xe, 1, axis=0))
            x2 = jnp.where(row == 0, r6, jnp.where(row == 1, r7, pltpu.roll(xe, 2, axis=0)))
            return whole, w0 * x2 + w1 * x1 + w2 * xe + bb

        hv, hv_edge = conv(xv_ref, tv_ref, wv_ref, bv_ref)
        hg, hg_edge = conv(xg_ref, tg_ref, wg_ref, bg_ref)
        hv_ref[...] = hv.astype(BF16)
        hg_ref[...] = hg.astype(BF16)
        act_ref[...] = (_gelu(hg) * hv).astype(BF16)
        hv_ref[0:EDGE, :] = hv_edge.astype(BF16)
        hg_ref[0:EDGE, :] = hg_edge.astype(BF16)
        act_ref[0:EDGE, :] = (_gelu(hg_edge) * hv_edge).astype(BF16)

    def xs(off):
        return pl.BlockSpec((tm, CONV_TC), lambda j, i: (i, j + off))

    def ts(off):
        return pl.BlockSpec((EDGE, CONV_TC), lambda j, i: (jnp.maximum(i * (tm // EDGE) - 1, 0), j + off))

    def ws(rows, off):
        return pl.BlockSpec((rows, CONV_TC), lambda j, i: (0, j + off))

    o_spec = pl.BlockSpec((tm, CONV_TC), lambda j, i: (i, j))
    return pl.pallas_call(
        body, name=name, grid=(N_CT, s // tm),
        in_specs=[xs(0), xs(N_CT), ts(0), ts(N_CT), ws(3, 0), ws(3, N_CT), ws(1, 0), ws(1, N_CT)],
        out_specs=[o_spec] * 3, out_shape=[_sds((s, D_FF), BF16)] * 3,
        compiler_params=_params(("parallel", "arbitrary")),
    )(hu, hu, hu, hu, cw, cw, cb.reshape(1, 2 * D_FF), cb.reshape(1, 2 * D_FF))


HALO = EDGE


def _conv_bwd(dact, hv, hg, hu, cw, name, tm=256):
    s = dact.shape[0]

    def body(da_ref, dan_ref, hv_ref, hvn_ref, hg_ref, hgn_ref, x_ref, t_ref, w_ref, dx_ref, dw_ref, db_ref, d_scr):
        i = pl.program_id(1)
        is_value = pl.program_id(0) < N_CT

        @pl.when(i == 0)
        def _():
            dw_ref[...] = jnp.zeros_like(dw_ref)
            db_ref[...] = jnp.zeros_like(db_ref)

        for rows, (a_ref, v_ref, g_ref) in ((pl.ds(0, tm), (da_ref, hv_ref, hg_ref)),
                                            (pl.ds(tm, HALO), (dan_ref, hvn_ref, hgn_ref))):
            @pl.when(is_value)
            def _():
                d_scr[rows, :] = a_ref[...].astype(F32) * _gelu(g_ref[...].astype(F32))

            @pl.when(jnp.logical_not(is_value))
            def _():
                d_scr[rows, :] = (a_ref[...].astype(F32) * v_ref[...].astype(F32)
                                  * _gelu_grad(g_ref[...].astype(F32)))

        has_prev = (i > 0).astype(F32)
        has_next = (i < s // tm - 1).astype(F32)
        w0, w1, w2 = w_ref[0:1, :], w_ref[1:2, :], w_ref[2:3, :]
        d = d_scr[0:tm, :]
        dx_ref[...] = (w2 * d + w1 * pltpu.roll(d, tm - 1, axis=0) + w0 * pltpu.roll(d, tm - 2, axis=0)).astype(BF16)
        row = lax.broadcasted_iota(jnp.int32, (EDGE, CONV_TC), 0)
        nxt = d_scr[tm:tm + HALO, :]
        n0 = _row_of(nxt, 0) * has_next
        n1 = _row_of(nxt, 1) * has_next
        de = d_scr[tm - EDGE:tm, :]
        d1 = jnp.where(row == EDGE - 1, n0, pltpu.roll(de, EDGE - 1, axis=0))
        d2 = jnp.where(row == EDGE - 2, n0, jnp.where(row == EDGE - 1, n1, pltpu.roll(de, EDGE - 2, axis=0)))
        dx_ref[tm - EDGE:tm, :] = (w2 * de + w1 * d1 + w0 * d2).astype(BF16)
        x = x_ref[...].astype(F32)
        tail = t_ref[...].astype(F32)
        r7 = _row_of(tail, EDGE - 1) * has_prev
        r6 = _row_of(tail, EDGE - 2) * has_prev
        last = x_ref[tm - EDGE:tm, :].astype(F32)
        l7, l6 = _row_of(last, EDGE - 1), _row_of(last, EDGE - 2)
        head = d_scr[0:8, :]
        d0, d1h = _row_of(head, 0), _row_of(head, 1)
        dw_ref[0:1, :] += (jnp.sum(d * pltpu.roll(x, 2, axis=0), axis=0, keepdims=True)
                           + d0 * (r6 - l6) + d1h * (r7 - l7))
        dw_ref[1:2, :] += jnp.sum(d * pltpu.roll(x, 1, axis=0), axis=0, keepdims=True) + d0 * (r7 - l7)
        dw_ref[2:3, :] += jnp.sum(d * x, axis=0, keepdims=True)
        db_ref[...] += jnp.sum(d, axis=0, keepdims=True)

    a_spec = pl.BlockSpec((tm, CONV_TC), lambda j, i: (i, j % N_CT))
    an_spec = pl.BlockSpec((HALO, CONV_TC),
                           lambda j, i: (jnp.minimum((i + 1) * (tm // HALO), s // HALO - 1), j % N_CT))
    x_spec = pl.BlockSpec((tm, CONV_TC), lambda j, i: (i, j))
    t_spec = pl.BlockSpec((EDGE, CONV_TC), lambda j, i: (jnp.maximum(i * (tm // EDGE) - 1, 0), j))
    w_spec = pl.BlockSpec((3, CONV_TC), lambda j, i: (0, j))
    db_spec = pl.BlockSpec((1, CONV_TC), lambda j, i: (0, j))
    return pl.pallas_call(
        body, name=name, grid=(2 * N_CT, s // tm),
        in_specs=[a_spec, an_spec, a_spec, an_spec, a_spec, an_spec, x_spec, t_spec, w_spec],
        out_specs=[x_spec, w_spec, db_spec],
        out_shape=[_sds((s, 2 * D_FF), BF16), _sds((3, 2 * D_FF)), _sds((1, 2 * D_FF))],
        scratch_shapes=[pltpu.VMEM((tm + HALO, CONV_TC), F32)],
        compiler_params=_params(("parallel", "arbitrary")),
    )(dact, dact, hv, hv, hg, hg, hu, hu, cw)


def _ple_fwd(h, gp, pp, next_gain, name, tm=512):
    s, d = h.shape
    with_norm = next_gain is not None

    def body(*refs):
        h_ref, g_ref, p_ref = refs[:3]
        out = h_ref[...] + _sigmoid(g_ref[...].astype(F32)) * p_ref[...].astype(F32)
        if with_norm:
            n_ref, o_ref, a_ref = refs[3:]
            scale = lax.rsqrt(jnp.mean(out * out, axis=-1, keepdims=True) + EPS)
            a_ref[...] = (out * scale * n_ref[...]).astype(BF16)
        else:
            o_ref, = refs[3:]
        o_ref[...] = out

    if not with_norm:
        return _rowcall(body, name, s, tm, [_rb(tm, d)] * 3, (h, gp, pp), _rb(tm, d), _sds((s, d))), None
    return _rowcall(body, name, s, tm, [_rb(tm, d)] * 3 + [_fb((1, d))], (h, gp, pp, next_gain.reshape(1, d)),
                    [_rb(tm, d)] * 2, [_sds((s, d)), _sds((s, d), BF16)])


def _ple_bwd(dh, gp, pp, name, tm=512):
    s, d = dh.shape

    def body(d_ref, g_ref, p_ref, dp_ref, dg_ref):
        sg = _sigmoid(g_ref[...].astype(F32))
        dv = d_ref[...]
        dp_ref[...] = (dv * sg).astype(BF16)
        dg_ref[...] = (dv * p_ref[...].astype(F32) * sg * (1.0 - sg)).astype(BF16)

    return _rowcall(body, name, s, tm, [_rb(tm, d)] * 3, (dh, gp, pp), [_rb(tm, d)] * 2,
                    [_sds((s, d), BF16)] * 2)


SCALE = HEAD_DIM ** -0.5
ATT_ROWS = 2048


def _att_geom(s, dil):
    w = min(ATT_ROWS, s)
    p = BLK * dil
    assert w % p == 0 and s % w == 0
    return w, p, w // p


def _rows(start, dil):
    return pl.ds(start, BLK, stride=dil) if dil > 1 else pl.ds(start, BLK)


def _head_masks():
    lane = lax.broadcasted_iota(jnp.int32, (1, BLK), 1)
    return [lane < HEAD_DIM, lane >= HEAD_DIM]


def _band():
    rel = np.arange(BLK)[:, None] + BLK - np.arange(2 * BLK)[None, :]
    return (rel >= 0) & (rel <= BLK)


def _zcur(w):
    return lambda off: pl.BlockSpec((w, BLK), lambda hp, i: (i, off + hp))


def _zprev(p, nb):
    return lambda off: pl.BlockSpec((p, BLK), lambda hp, i: (jnp.maximum(i * nb - 1, 0), off + hp))


def _scur(w):
    return pl.BlockSpec((w, BLK), lambda hp, i: (i, hp))


def _pair_rows(t, masks):
    return jnp.concatenate([jnp.where(masks[0], t, 0.0), jnp.where(masks[1], t, 0.0)], axis=0).astype(BF16)


def _pair_bias_bwd(bias):
    return bias.reshape(4, 2, BLK, 2, BLK).transpose(0, 3, 2, 1, 4).reshape(4, 2, BLK, 2 * BLK)


def _unpair_bias_bwd(db):
    return db.reshape(4, 2, BLK, 2, BLK).transpose(0, 3, 2, 1, 4).reshape(N_HEADS, BLK, 2 * BLK)


def _attn_fwd(z, bias, state, dil, first, last, name):
    s = z.shape[0]
    w, p, nb = _att_geom(s, dil)

    def body(*refs):
        q_ref, kp_ref, kc_ref, vp_ref, vc_ref, b_ref = refs[:6]
        rest = refs[6:]
        if not first:
            m_ref, l_ref, a_ref = rest[:3]
            rest = rest[3:]
        i = pl.program_id(1)
        masks = _head_masks()
        own_block = lax.broadcasted_iota(jnp.int32, (1, 2 * BLK), 1) >= BLK
        for r in range(dil):
            for b in range(nb):
                rows = _rows(r + p * b, dil)
                prev_rows = _rows(r + p * (b - 1), dil) if b > 0 else _rows(r, dil)
                kprev, vprev = (kc_ref, vc_ref) if b > 0 else (kp_ref, vp_ref)
                q = q_ref[rows, :] * SCALE
                k = jnp.concatenate([kprev[prev_rows, :], kc_ref[rows, :]], axis=0).astype(BF16)
                v = jnp.concatenate([vprev[prev_rows, :], vc_ref[rows, :]], axis=0).astype(BF16)
                mb = lb = ob = None
                for hh, mh in enumerate(masks):
                    qh = jnp.where(mh, q, 0.0).astype(BF16)
                    sc = lax.dot_general(qh, k, _DIMS["nt"], preferred_element_type=F32) + b_ref[hh]
                    if b == 0:
                        sc = jnp.where(own_block | (i > 0), sc, NEG_INF)
                    mx = jnp.max(sc, axis=1, keepdims=True)
                    e = jnp.exp(sc - mx)
                    den = jnp.sum(e, axis=1, keepdims=True)
                    o = jnp.dot(e.astype(BF16), v, preferred_element_type=F32)
                    if hh == 0:
                        mb = jnp.broadcast_to(mx, (BLK, BLK))
                        lb = jnp.broadcast_to(den, (BLK, BLK))
                        ob = o
                    else:
                        mb = jnp.where(mh, mx, mb)
                        lb = jnp.where(mh, den, lb)
                        ob = jnp.where(mh, o, ob)
                if first:
                    m_new, l_new, a_new = mb, lb, ob
                else:
                    m_old = m_ref[rows, :]
                    m_new = jnp.maximum(m_old, mb)
                    al = jnp.exp(m_old - m_new)
                    be = jnp.exp(mb - m_new)
                    l_new = al * l_ref[rows, :] + be * lb
                    a_new = al * a_ref[rows, :] + be * ob
                if last:
                    y_ref, lse_ref = rest
                    y_ref[rows, :] = a_new / l_new
                    lse_ref[rows, :] = m_new + jnp.log(l_new)
                else:
                    mo_ref, lo_ref, ao_ref = rest
                    mo_ref[rows, :] = m_new
                    lo_ref[rows, :] = l_new
                    ao_ref[rows, :] = a_new

    cur, prv = _zcur(w), _zprev(p, nb)
    b_spec = pl.BlockSpec((2, BLK, 2 * BLK), lambda hp, i: (hp, 0, 0))
    in_specs = [cur(0), prv(4), cur(4), prv(8), cur(8), b_spec]
    args = [z, z, z, z, z, bias]
    if not first:
        in_specs += [_scur(w)] * 3
        args += list(state)
    n_out = 2 if last else 3
    return pl.pallas_call(
        body, name=name, grid=(4, s // w), in_specs=in_specs, out_specs=[_scur(w)] * n_out,
        out_shape=[_sds((s, ATTN_W))] * n_out,
        compiler_params=_params(("parallel", "parallel")),
    )(*args)


def _row_stats(mh, dy, y, lse):
    delta = jnp.sum(jnp.where(mh, dy * y, 0.0), axis=1, keepdims=True)
    lse_h = jnp.max(jnp.where(mh, lse, NEG_INF), axis=1, keepdims=True)
    return delta, lse_h


def _attn_bwd(z, bias, dy, y, lse, prev, dil, name):
    s = z.shape[0]
    w, p, nb = _att_geom(s, dil)
    n_steps = s // w
    first = prev is None

    def body(*refs):
        q_ref, kp_ref, kc_ref, vp_ref, vc_ref, b_ref, dy_ref, y_ref, lse_ref = refs[:9]
        rest = refs[9:]
        if not first:
            dqp_ref, dkp_ref, dvp_ref = rest[:3]
            rest = rest[3:]
        dq_ref, dk_ref, dv_ref, dkx_ref, dvx_ref, db_ref = rest
        i = pl.program_id(1)

        @pl.when(i == 0)
        def _():
            db_ref[...] = jnp.zeros_like(db_ref)

        masks = _head_masks()
        first_head = lax.broadcasted_iota(jnp.int32, (1, 2 * BLK), 1) < BLK

        def flush(rows, dk, dv):
            if not first:
                dk = dk + dkp_ref[rows, :]
                dv = dv + dvp_ref[rows, :]
            dk_ref[rows, :] = dk
            dv_ref[rows, :] = dv

        for r in range(dil):
            carry = None
            for b in range(nb):
                rows = _rows(r + p * b, dil)
                prev_rows = _rows(r + p * (b - 1), dil) if b > 0 else _rows(r, dil)
                kprev, vprev = (kc_ref, vc_ref) if b > 0 else (kp_ref, vp_ref)
                keys = [(_pair_rows(kprev[prev_rows, :], masks), _pair_rows(vprev[prev_rows, :], masks)),
                        (_pair_rows(kc_ref[rows, :], masks), _pair_rows(vc_ref[rows, :], masks))]
                q = (q_ref[rows, :] * SCALE).astype(BF16)
                dy_v = dy_ref[rows, :]
                dyb = dy_v.astype(BF16)
                stats = [_row_stats(mh, dy_v, y_ref[rows, :], lse_ref[rows, :]) for mh in masks]
                delta = jnp.where(first_head, stats[0][0], stats[1][0])
                lse_h = jnp.where(first_head, stats[0][1], stats[1][1])
                dq = jnp.zeros((BLK, BLK), F32)
                dk, dv = [], []
                for half in range(2):
                    kh, vh = keys[half]
                    sc = lax.dot_general(q, kh, _DIMS["nt"], preferred_element_type=F32) + b_ref[half]
                    pr = jnp.exp(sc - lse_h)
                    if b == 0 and half == 0:
                        pr = pr * (i > 0).astype(F32)
                    dp = lax.dot_general(dyb, vh, _DIMS["nt"], preferred_element_type=F32)
                    ds = pr * (dp - delta)
                    db_ref[half] += ds
                    dsb = ds.astype(BF16)
                    dq = dq + jnp.dot(dsb, kh, preferred_element_type=F32)
                    dk2 = lax.dot_general(dsb, q, _DIMS["tn"], preferred_element_type=F32)
                    dv2 = lax.dot_general(pr.astype(BF16), dyb, _DIMS["tn"], preferred_element_type=F32)
                    dk.append(jnp.where(masks[0], dk2[:BLK], dk2[BLK:]))
                    dv.append(jnp.where(masks[0], dv2[:BLK], dv2[BLK:]))
                dq = dq * SCALE
                if not first:
                    dq = dq + dqp_ref[rows, :]
                dq_ref[rows, :] = dq
                if b > 0:
                    flush(prev_rows, carry[0] + dk[0], carry[1] + dv[0])
                else:
                    dkx_ref[prev_rows, :] = dk[0]
                    dvx_ref[prev_rows, :] = dv[0]
                carry = (dk[1], dv[1])
            flush(_rows(r + p * (nb - 1), dil), *carry)

    cur, prv = _zcur(w), _zprev(p, nb)
    b_spec = pl.BlockSpec((None, 2, BLK, 2 * BLK), lambda hp, i: (hp, 0, 0, 0))
    in_specs = [cur(0), prv(4), cur(4), prv(8), cur(8), b_spec] + [_scur(w)] * 3
    args = [z, z, z, z, z, bias, dy, y, lse]
    if not first:
        in_specs += [_scur(w)] * 3
        args += list(prev)
    x_spec = pl.BlockSpec((p, BLK), lambda hp, i: (i, hp))
    *outs, db = pl.pallas_call(
        body, name=name, grid=(4, n_steps), in_specs=in_specs,
        out_specs=[_scur(w)] * 3 + [x_spec] * 2 + [b_spec],
        out_shape=[_sds((s, ATTN_W))] * 3 + [_sds((n_steps * p, ATTN_W))] * 2 + [_sds((4, 2, BLK, 2 * BLK))],
        compiler_params=_params(("parallel", "arbitrary")),
    )(*args)
    return (*outs, _unpair_bias_bwd(db))


ASM_ROWS = 512


def _assemble_dz(dq, dk, dv, extras, dzs, du, name):
    s = dq.shape[0]
    w = min(ATT_ROWS, s)
    n_steps = s // w
    per_step = w // ASM_ROWS
    assert w % ASM_ROWS == 0

    def body(*refs):
        dq_ref, dk_ref, dv_ref, dzs_ref, du_ref = refs[:5]
        x_refs = refs[5:5 + 2 * len(extras)]
        o_ref, acc_ref = refs[-2:]
        j = pl.program_id(0)
        step = j // per_step
        has_next = (step < n_steps - 1).astype(F32)
        last_of_step = ((j + 1) % per_step == 0).astype(F32)
        o_ref[:, 0:ATTN_W] = dq_ref[...].astype(BF16)
        o_ref[:, 3 * ATTN_W:3 * ATTN_W + 2 * SGU_W] = dzs_ref[...].astype(BF16)
        o_ref[:, 3 * ATTN_W + 2 * SGU_W:IN_W] = du_ref[...].astype(BF16)
        for part, (base_ref, col) in enumerate(((dk_ref, ATTN_W), (dv_ref, 2 * ATTN_W))):
            acc_ref[...] = base_ref[...]
            for n, (_, dil) in enumerate(BRANCHES):
                rows = min(BLK * dil, ASM_ROWS)
                scale = has_next if BLK * dil >= w else has_next * last_of_step
                acc_ref[ASM_ROWS - rows:, :] += x_refs[2 * n + part][...] * scale
            o_ref[:, col:col + ATTN_W] = acc_ref[...].astype(BF16)

    def x_spec(dil):
        p = BLK * dil
        rows = min(p, ASM_ROWS)
        blocks_per_step = p // rows
        total = n_steps * blocks_per_step

        def idx(j):
            step = j // per_step
            within = (j % per_step) - (per_step - blocks_per_step)
            return (jnp.clip((step + 1) * blocks_per_step + jnp.maximum(within, 0), 0, total - 1), 0)

        return pl.BlockSpec((rows, ATTN_W), idx)

    in_specs = [_rb(ASM_ROWS, ATTN_W)] * 3 + [_rb(ASM_ROWS, 2 * SGU_W), _rb(ASM_ROWS, SSM_W)]
    args = [dq, dk, dv, dzs, du]
    for (dkx, dvx), (_, dil) in zip(extras, BRANCHES):
        in_specs += [x_spec(dil)] * 2
        args += [dkx, dvx]
    return pl.pallas_call(
        body, name=name, grid=(s // ASM_ROWS,), in_specs=in_specs, out_specs=_rb(ASM_ROWS, IN_W),
        out_shape=_sds((s, IN_W), BF16), scratch_shapes=[pltpu.VMEM((ASM_ROWS, ATTN_W), F32)],
        compiler_params=_params(("parallel",)),
    )(*args)


def _t5_bucket(dist):
    max_exact = N_BUCKETS // 2
    d = np.maximum(dist, 0)
    large = max_exact + (np.log(np.maximum(d, 1) / max_exact) / np.log(REL_MAX / max_exact)
                         * (N_BUCKETS - max_exact)).astype(np.int32)
    large = np.minimum(large, N_BUCKETS - 1)
    return np.where(d < max_exact, d, large).astype(np.int32)


def _bias_tables(rel_bias):
    period = 3 * BLK
    tabs = []
    for _, dil in BRANCHES:
        onehot = np.zeros((period, N_BUCKETS), np.float32)
        d = np.arange(BLK + 1)
        onehot[d, _t5_bucket((BLK - d) * dil)] = 1.0
        f = jnp.dot(jnp.asarray(onehot), rel_bias, precision=lax.Precision.HIGHEST)
        flat = jnp.tile(f.T, (1, BLK))[:, :BLK * (period - 1)]
        tab = flat.reshape(N_HEADS, BLK, period - 1)[:, :, :2 * BLK]
        tabs.append(jnp.where(_band()[None], tab, NEG_INF))
    return tabs


def _bucket_onehot():
    maps = []
    q = np.arange(BLK)[:, None]
    k = np.arange(2 * BLK)[None, :]
    rel = q + BLK - k
    for _, dil in BRANCHES:
        maps.append(np.where((rel >= 0) & (rel <= BLK), _t5_bucket(rel * dil), -1).reshape(-1))
    bmap = jnp.asarray(np.concatenate(maps).astype(np.int32))
    return (bmap[:, None] == jnp.arange(128, dtype=jnp.int32)[None, :]).astype(BF16)


def _block_diag(t):
    g, n, c = t.shape
    eye = jnp.eye(g, dtype=t.dtype)
    return (t[:, :, None, :] * eye[:, None, :, None]).reshape(g * n, g * c)


def _ssm_prep(a_re, a_im, log_dt, b_re, b_im, c_re, c_im):
    lam = lax.complex(a_re, a_im)
    dt = jnp.exp(log_dt)[:, None]
    a_bar = jnp.exp(lam * dt)
    b_bar = ((a_bar - 1.0) / lam)[:, :, None] * lax.complex(b_re, b_im)
    bdt = jnp.concatenate([_block_diag(jnp.real(b_bar)), _block_diag(jnp.imag(b_bar))], axis=0)
    cd = jnp.concatenate([_block_diag(jnp.transpose(c_re, (0, 2, 1))),
                          _block_diag(-jnp.transpose(c_im, (0, 2, 1)))], axis=0)
    return jnp.real(a_bar).reshape(-1), jnp.imag(a_bar).reshape(-1), bdt, cd


def _powers(ar, ai):
    pr, pi = ar[:, None], ai[:, None]
    k = 1
    while k < 8:
        lr, li = pr[:, -1:], pi[:, -1:]
        pr, pi = (jnp.concatenate([pr, pr * lr - pi * li], axis=1),
                  jnp.concatenate([pi, pr * li + pi * lr], axis=1))
        k *= 2
    return pr, pi


def _sgu_bias_expand(b):
    return jnp.repeat(b.T, 64, axis=1)


def _layer_fwd(i, h, a1, p_i, big, small, bias_tabs, next_gain):
    nm = "l%d_" % i
    sv = {"h": h}
    if a1 is None:
        a1 = _rms_fwd(h, small["norm_attn_g"][i], nm + "rms_attn")
    z = _mm(a1, big["w_in"], "nt", nm + "in_proj")
    st = None
    for b, (_, dil) in enumerate(BRANCHES):
        st = _attn_fwd(z, bias_tabs[b][0], st, dil, b == 0, b == len(BRANCHES) - 1, nm + "attn_fwd%d" % b)
    y_attn, lse = st
    bexp = _sgu_bias_expand(small["sgu_b"][i])
    y_sgu = _sgu_fwd(z, small["sgu_ln_g"][i], small["sgu_ln_b"][i], small["sgu_w"][i], bexp, nm + "sgu_fwd")
    ar, ai, bdt, cd = _ssm_prep(*[small[k][i] for k in ("ssm_a_re", "ssm_a_im", "ssm_log_dt", "ssm_b_re",
                                                         "ssm_b_im", "ssm_c_re", "ssm_c_im")])
    xr, xi, yc = _ssm_fwd(z, bdt.astype(BF16), cd.astype(BF16), _scan_tables(*_powers(ar, ai), False),
                          nm + "ssm_core")
    y_ssm = _ssm_post_fwd(yc, z, small["ssm_d"][i], big["ssm_glu_w"], small["ssm_glu_b"][i], nm + "ssm_post")
    mix = _mix_fwd(y_attn, y_sgu, y_ssm, small["branch_norm_g"][i], nm + "mix")
    if "rest" in big:
        big = dict({k: t for k, t in big.items() if k != "rest"}, **big["rest"](mix))
    h2, a2 = _mm(mix, big["w_out"], "nn", nm + "out_proj", add=h, norm_gain=small["norm_ffn_g"][i])
    hu = _mm(a2, big["ffn_w_up"], "nt", nm + "ffn_up", out_dtype=BF16)
    hv, hg, act = _conv_fwd(hu, big["ffn_conv_w"], small["ffn_conv_b"][i], nm + "ffn_conv")
    h3, a3 = _mm(act, big["ffn_w_down"], "nn", nm + "ffn_down", add=h2, norm_gain=small["norm_ple_g"][i])
    gp = _mm(a3, big["ple_w_gate"], "nn", nm + "ple_gate", out_dtype=BF16)
    pp = _mm(p_i, big["ple_w_proj"], "nt", nm + "ple_proj", out_dtype=BF16)
    h4, a_next = _ple_fwd(h3, gp, pp, next_gain, nm + "ple_add")
    sv.update(big=big, a1=a1, z=z, y_attn=y_attn, lse=lse, y_sgu=y_sgu, y_ssm=y_ssm, yc=yc, xr=xr, xi=xi, mix=mix, h2=h2,
              a2=a2, hu=hu, hv=hv, hg=hg, act=act, h3=h3, a3=a3, gp=gp, pp=pp)
    return h4, a_next, sv


def _layer_bwd(i, dh4, sv, p_i, big, small, bias_tabs, ffn_done=None):
    nm = "l%d_" % i
    g = {}
    dpp, dgp = _ple_bwd(dh4, sv["gp"], sv["pp"], nm + "ple_bwd")
    g["ple_w_proj"] = _mm(dpp, p_i, "tn", nm + "d_ple_proj", out_dtype=BF16)
    g["ple_w_gate"] = _mm(sv["a3"], dgp, "tn", nm + "d_ple_gate", out_dtype=BF16)
    dh3, dgain = _mm(dgp, big["ple_w_gate"], "nt", nm + "ple_gate_t", add=dh4,
                     norm_bwd=(sv["h3"], small["norm_ple_g"][i]))
    g["norm_ple_g"] = dgain.reshape(D_MODEL)
    g["ffn_w_down"] = _mm(sv["act"], dh3, "tn", nm + "d_ffn_down", out_dtype=BF16)
    dact = _mm(dh3, big["ffn_w_down"], "nt", nm + "ffn_down_t", out_dtype=BF16)
    dhu, g["ffn_conv_w"], dcb = _conv_bwd(dact, sv["hv"], sv["hg"], sv["hu"], big["ffn_conv_w"],
                                          nm + "ffn_conv_bwd")
    g["ffn_conv_b"] = dcb.reshape(2 * D_FF)
    g["ffn_w_up"] = _mm(dhu, sv["a2"], "tn", nm + "d_ffn_up", out_dtype=BF16)
    dh2, dgain = _mm(dhu, big["ffn_w_up"], "nn", nm + "ffn_up_t", add=dh3,
                     norm_bwd=(sv["h2"], small["norm_ffn_g"][i]))
    g["norm_ffn_g"] = dgain.reshape(D_MODEL)
    if ffn_done is not None:
        small = ffn_done(g, small)
    g["w_out"] = _mm(sv["mix"], dh2, "tn", nm + "d_out_proj", out_dtype=BF16)
    dmix = _mm(dh2, big["w_out"], "nt", nm + "out_proj_t")
    dya, dysg, dyss, g["branch_norm_g"] = _mix_bwd(dmix, sv["y_attn"], sv["y_sgu"], sv["y_ssm"],
                                                   small["branch_norm_g"][i], nm + "mix_bwd")
    ssm_keys = ("ssm_a_re", "ssm_a_im", "ssm_log_dt", "ssm_b_re", "ssm_b_im", "ssm_c_re", "ssm_c_im")
    (ar, ai, bdt, cd), prep_vjp = jax.vjp(_ssm_prep, *[small[k][i] for k in ssm_keys])
    dy1, dgl, y2, dud, g["ssm_d"], g["ssm_glu_b"] = _ssm_post_bwd(
        dyss, sv["yc"], sv["z"], small["ssm_d"][i], big["ssm_glu_w"], small["ssm_glu_b"][i], nm + "ssm_post_bwd")
    g["ssm_glu_w"] = _mm(y2, dgl, "tn", nm + "d_ssm_glu", out_dtype=BF16)
    du, dbdt, dcd, dar, dai = _ssm_bwd(dy1, dud, sv["z"], sv["xr"], sv["xi"], bdt.astype(BF16), cd.astype(BF16),
                                       _scan_tables(*_powers(ar, ai), True), nm + "ssm_core_bwd")
    for k, val in zip(ssm_keys, prep_vjp((dar, dai, dbdt, dcd))):
        g[k] = val
    bexp, bexp_vjp = jax.vjp(_sgu_bias_expand, small["sgu_b"][i])
    dzs, g["sgu_w"], dbexp, g["sgu_ln_g"], g["sgu_ln_b"] = _sgu_bwd(
        sv["z"], dysg, small["sgu_ln_g"][i], small["sgu_ln_b"][i], small["sgu_w"][i], bexp, nm + "sgu_bwd")
    g["sgu_b"] = bexp_vjp(dbexp)[0]
    prev = None
    dbs, extras = [], []
    for b, (_, dil) in enumerate(BRANCHES):
        dq, dk, dv, dkx, dvx, db = _attn_bwd(sv["z"], bias_tabs[b][1], dya, sv["y_attn"], sv["lse"], prev, dil,
                                             nm + "attn_bwd%d" % b)
        prev = (dq, dk, dv)
        extras.append((dkx, dvx))
        dbs.append(db.reshape(N_HEADS, BLK * 2 * BLK))
    dz = _assemble_dz(dq, dk, dv, extras, dzs, du, nm + "assemble_dz")
    g["w_in"] = _mm(dz, sv["a1"], "tn", nm + "d_in_proj", out_dtype=BF16)
    dh, dgain = _mm(dz, big["w_in"], "nn", nm + "in_proj_t", add=dh2, norm_bwd=(sv["h"], small["norm_attn_g"][i]))
    g["norm_attn_g"] = dgain.reshape(D_MODEL)
    return dh, g, jnp.concatenate(dbs, axis=1)


def _local_step(x, p, target, layer_weights, small, layer_done=None):
    depth = p.shape[0]
    bias_tabs = [(t, _pair_bias_bwd(t)) for t in _bias_tables(small["rel_bias"])]
    h, a1 = x, None
    saved = []
    for i in range(depth):
        next_gain = small["norm_attn_g"][i + 1] if i + 1 < depth else None
        h, a1, sv = _layer_fwd(i, h, a1, p[i], layer_weights(i, h), small, bias_tabs, next_gain)
        saved.append(sv)
    dh, loss, g_final = _loss_head(h, target, small["final_norm_g"], "loss_head")
    layer_grads = [None] * depth
    dbias = [None] * depth
    for i in reversed(range(depth)):
        ffn_done = None if layer_done is None else (lambda g, sm, i=i: layer_done(i, "ffn", g, sm))
        dh, layer_grads[i], dbias[i] = _layer_bwd(i, dh, saved[i], p[i], saved[i]["big"], small, bias_tabs,
                                                  ffn_done)
        if layer_done is not None:
            small = layer_done(i, "all", layer_grads[i], small)
    big_grads = [{k: lg.pop(k) for k in COMM_NAMES} for lg in layer_grads]
    grads = {k: jnp.stack([layer_grads[i][k] for i in range(depth)]) for k in layer_grads[0]}
    grads["final_norm_g"] = g_final
    g_rb = _mm(sum(dbias[1:], dbias[0]), _bucket_onehot(), "nn", "d_rel_bias", tk=2048)
    grads["rel_bias"] = g_rb[:, :N_BUCKETS].T
    return loss, dh, big_grads, grads


_ANY = pl.BlockSpec(memory_space=pl.ANY)
MESH_IDS = pl.DeviceIdType.MESH


def _slot(ref, axis, j):
    return ref.at[(slice(None),) * axis + (j,)]


def _all_gather(blocks, axis, name):
    nt = len(blocks)

    def body(*refs):
        x_refs, o_refs = refs[:nt], refs[nt:2 * nt]
        send_sems, recv_sems, local_sems = refs[2 * nt:]
        x, y, c = lax.axis_index("x"), lax.axis_index("y"), lax.axis_index("c")
        me, sibling = (x, y, c), (x, y, 1 - c)
        chips = [(1 - x, y), (x, 1 - y), (1 - x, 1 - y)]

        def slot(t, px, py, pc):
            return _slot(o_refs[t], axis, 4 * px + 2 * py + pc)

        def copy(t, k, blk, to, src=None):
            return pltpu.make_async_remote_copy(
                src_ref=slot(t, *blk) if src is None else src, dst_ref=slot(t, *blk),
                send_sem=send_sems.at[7 * t + k], recv_sem=recv_sems.at[7 * t + k],
                device_id=to, device_id_type=MESH_IDS)

        mine = [pltpu.make_async_copy(x_refs[t], slot(t, *me), local_sems.at[t]) for t in range(nt)]
        for cp in mine:
            cp.start()
        first = []
        for t in range(nt):
            first.append(copy(t, 0, me, sibling, src=x_refs[t]))
            first += [copy(t, 1 + j, me, (*chip, c), src=x_refs[t]) for j, chip in enumerate(chips)]
        for cp in first:
            cp.start()
        passed = []
        for t in range(nt):
            for j, chip in enumerate(chips):
                copy(t, 1 + j, (*chip, c), me).wait_recv()
                passed.append(copy(t, 4 + j, (*chip, c), sibling))
                passed[-1].start()
        for t in range(nt):
            copy(t, 0, sibling, me).wait_recv()
            for j, chip in enumerate(chips):
                copy(t, 4 + j, (*chip, 1 - c), me).wait_recv()
        for cp in first + passed:
            cp.wait_send()
        for cp in mine:
            cp.wait()

    out_shape = [jax.ShapeDtypeStruct(b.shape[:axis] + (N_DEV,) + b.shape[axis:], b.dtype) for b in blocks]
    return pl.pallas_call(
        body, name=name, out_shape=out_shape, in_specs=[_ANY] * nt, out_specs=[_ANY] * nt,
        scratch_shapes=[pltpu.SemaphoreType.DMA((7 * nt,)), pltpu.SemaphoreType.DMA((7 * nt,)),
                        pltpu.SemaphoreType.DMA((nt,))],
    )(*blocks)


def _peer(k):
    x, y, c = lax.axis_index("x"), lax.axis_index("y"), lax.axis_index("c")
    px = 1 - x if k & 4 else x
    py = 1 - y if k & 2 else y
    pc = 1 - c if k & 1 else c
    return (px, py, pc), 4 * px + 2 * py + pc


def _all_to_all(blocks, name):
    nt = len(blocks)

    def body(*refs):
        x_refs, o_refs = refs[:nt], refs[nt:2 * nt]
        send_sems, recv_sems, local_sems = refs[2 * nt:]
        _, me = _peer(0)
        mine = [pltpu.make_async_copy(x_refs[t].at[me], o_refs[t].at[me], local_sems.at[t]) for t in range(nt)]
        for cp in mine:
            cp.start()
        copies = []
        for k in range(1, N_DEV):
            peer, idx = _peer(k)
            for t in range(nt):
                cp = pltpu.make_async_remote_copy(
                    src_ref=x_refs[t].at[idx], dst_ref=o_refs[t].at[me],
                    send_sem=send_sems.at[7 * t + k - 1], recv_sem=recv_sems.at[7 * t + k - 1],
                    device_id=peer, device_id_type=MESH_IDS)
                cp.start()
                copies.append(cp)
        for cp in copies:
            cp.wait()
        for cp in mine:
            cp.wait()

    return pl.pallas_call(
        body, name=name, out_shape=[jax.ShapeDtypeStruct(b.shape, b.dtype) for b in blocks],
        in_specs=[_ANY] * nt, out_specs=[_ANY] * nt,
        scratch_shapes=[pltpu.SemaphoreType.DMA((7 * nt,)), pltpu.SemaphoreType.DMA((7 * nt,)),
                        pltpu.SemaphoreType.DMA((nt,))],
    )(*blocks)


_HBM = pl.BlockSpec(memory_space=pltpu.HBM)
_SEM = pl.BlockSpec(memory_space=pltpu.SEMAPHORE)
_EFFECT = pltpu.SideEffectType.DATAFLOW_SIDE_EFFECTING


def _split_copy(src_ref, land_ref, send_sems, recv_sems, t, k, gather):
    peer, idx = _peer(k)
    _, me = _peer(0)
    return pltpu.make_async_remote_copy(
        src_ref=src_ref if gather else src_ref.at[idx], dst_ref=land_ref.at[me],
        send_sem=send_sems.at[7 * t + k - 1], recv_sem=recv_sems.at[7 * t + k - 1],
        device_id=peer, device_id_type=MESH_IDS)


def _exchange_start(srcs, lands, gather, name):
    nt = len(srcs)

    def body(*refs):
        src_refs, land_refs = refs[:nt], refs[nt:2 * nt]
        send_sems, recv_sems = refs[2 * nt:2 * nt + 2]
        token = refs[-1]
        for k in range(1, N_DEV):
            for t in range(nt):
                _split_copy(src_refs[t], land_refs[t], send_sems, recv_sems, t, k, gather).start()
        token[...] = jnp.zeros_like(token)

    hbm = lambda a: pltpu.HBM(a.shape, a.dtype)
    outs = pl.pallas_call(
        body, name=name,
        out_shape=(pltpu.SemaphoreType.DMA((7 * nt,)), pltpu.SemaphoreType.DMA((7 * nt,)),
                   *[hbm(a) for a in srcs], *[hbm(a) for a in lands], jax.ShapeDtypeStruct((8, 128), F32)),
        in_specs=[_HBM] * (2 * nt),
        out_specs=(_SEM, _SEM, *[_HBM] * (2 * nt), pl.BlockSpec(memory_space=pltpu.VMEM)),
        input_output_aliases={j: 2 + j for j in range(2 * nt)},
        compiler_params=pltpu.CompilerParams(has_side_effects=_EFFECT),
    )(*[pltpu.with_memory_space_constraint(a, pltpu.HBM) for a in list(srcs) + list(lands)])
    return outs[0], outs[1], outs[2:2 + nt], outs[2 + nt:2 + 2 * nt], outs[-1]


def _exchange_wait(send_sems, recv_sems, srcs, lands, after, gather, name):
    nt = len(srcs)

    def body(*refs):
        src_refs, land_refs = refs[:nt], refs[nt:2 * nt]
        send_sems, recv_sems = refs[2 * nt:2 * nt + 2]
        for k in range(1, N_DEV):
            _, idx = _peer(k)
            for t in range(nt):
                _split_copy(src_refs[t], land_refs[t], send_sems, recv_sems, t, k, gather).wait_send()
                arrival = pltpu.make_async_remote_copy(
                    src_ref=land_refs[t].at[idx], dst_ref=land_refs[t].at[idx],
                    send_sem=send_sems.at[7 * t + k - 1], recv_sem=recv_sems.at[7 * t + k - 1],
                    device_id=_peer(k)[0], device_id_type=MESH_IDS)
                arrival.wait_recv()

    hbm = lambda a: pltpu.HBM(a.shape, a.dtype)
    outs = pl.pallas_call(
        body, name=name, out_shape=tuple(hbm(a) for a in list(srcs) + list(lands)),
        in_specs=[_HBM] * (2 * nt) + [_SEM, _SEM, _ANY], out_specs=tuple([_HBM] * (2 * nt)),
        input_output_aliases={j: j for j in range(2 * nt)},
        compiler_params=pltpu.CompilerParams(has_side_effects=_EFFECT),
    )(*srcs, *lands, send_sems, recv_sems, after)
    return outs[nt:]


def _adamw(parts, w, m, v, name, tr):
    n_layers, r, c_ = w.shape
    assert len(parts) == n_layers

    def body(*refs):
        p_refs = refs[:n_layers]
        w_ref, m_ref, v_ref, g_ref, d_ref, mo_ref, vo_ref = refs[n_layers:]

        def update(p_ref):
            g = p_ref[0].astype(F32)
            for j in range(1, N_DEV):
                g = g + p_ref[j].astype(F32)
            m2 = ADAM_B1 * m_ref[...] + (1.0 - ADAM_B1) * g
            v2 = ADAM_B2 * v_ref[...] + (1.0 - ADAM_B2) * (g * g)
            m_hat = m2 / (1.0 - ADAM_B1 ** ADAM_STEP)
            v_hat = v2 / (1.0 - ADAM_B2 ** ADAM_STEP)
            g_ref[...] = g
            d_ref[...] = -ADAM_LR * (m_hat / (jnp.sqrt(v_hat) + ADAM_EPS) + ADAM_WD * w_ref[...])
            mo_ref[...] = m2
            vo_ref[...] = v2

        for layer in range(n_layers):
            pl.when(pl.program_id(0) == layer)(lambda layer=layer: update(p_refs[layer]))

    spec = pl.BlockSpec((None, tr, c_), lambda l, i: (l, i, 0))
    p_spec = pl.BlockSpec((N_DEV, tr, c_), lambda l, i: (0, i, 0))
    return pl.pallas_call(
        body, name=name, grid=(n_layers, r // tr), in_specs=[p_spec] * n_layers + [spec] * 3,
        out_specs=[spec] * 4, out_shape=[_sds((n_layers, r, c_))] * 4,
        compiler_params=_params(("parallel", "parallel")),
    )(*parts, w, m, v)


def _pack_rows(n_elems, align):
    rows = -(-n_elems // PACK_COLS)
    return -(-rows // align) * align


def _pack(arrs, rows, dtype=F32):
    flat = jnp.concatenate([a.reshape(-1) for a in arrs]).astype(dtype)
    return jnp.pad(flat, (0, rows * PACK_COLS - flat.shape[0])).reshape(rows, PACK_COLS)


def _unpack(pack, shapes):
    flat = pack.reshape(-1)
    out, off = [], 0
    for shp in shapes:
        size = int(np.prod(shp))
        out.append(flat[off:off + size].reshape(shp))
        off += size
    return out


def _tile_rows(rows, target, align=16):
    best = align
    for t in range(align, target + 1, align):
        if rows % t == 0:
            best = t
    return best


COMM_NAMES = ("w_in", "ssm_glu_w", "w_out", "ffn_w_up", "ffn_w_down", "ple_w_gate", "ple_w_proj")
COMM_TRANSPOSED = ("w_in", "ffn_w_up", "ple_w_proj")
COMM_EARLY = ("ple_w_proj", "ple_w_gate", "ffn_w_down", "ffn_w_up")
COMM_LATE = ("w_in", "ssm_glu_w", "w_out")
SMALL_TILE_ROWS = 64
CONV_NAME = "ffn_conv_w"


def _to_comm(name, a):
    return jnp.swapaxes(a, 1, 2) if name in COMM_TRANSPOSED else a


def kernel(x, p, rel_bias, norm_attn_g, w_in, sgu_ln_g, sgu_ln_b, sgu_w, sgu_b, ssm_a_re, ssm_a_im, ssm_log_dt, ssm_b_re, ssm_b_im, ssm_c_re, ssm_c_im, ssm_d, ssm_glu_w, ssm_glu_b, branch_norm_g, w_out, norm_ffn_g, ffn_w_up, ffn_conv_w, ffn_conv_b, ffn_w_down, norm_ple_g, ple_w_gate, ple_w_proj, final_norm_g, loss_target, m_rel_bias, m_norm_attn_g, m_w_in, m_sgu_ln_g, m_sgu_ln_b, m_sgu_w, m_sgu_b, m_ssm_a_re, m_ssm_a_im, m_ssm_log_dt, m_ssm_b_re, m_ssm_b_im, m_ssm_c_re, m_ssm_c_im, m_ssm_d, m_ssm_glu_w, m_ssm_glu_b, m_branch_norm_g, m_w_out, m_norm_ffn_g, m_ffn_w_up, m_ffn_conv_w, m_ffn_conv_b, m_ffn_w_down, m_norm_ple_g, m_ple_w_gate, m_ple_w_proj, m_final_norm_g, v_rel_bias, v_norm_attn_g, v_w_in, v_sgu_ln_g, v_sgu_ln_b, v_sgu_w, v_sgu_b, v_ssm_a_re, v_ssm_a_im, v_ssm_log_dt, v_ssm_b_re, v_ssm_b_im, v_ssm_c_re, v_ssm_c_im, v_ssm_d, v_ssm_glu_w, v_ssm_glu_b, v_branch_norm_g, v_w_out, v_norm_ffn_g, v_ffn_w_up, v_ffn_conv_w, v_ffn_conv_b, v_ffn_w_down, v_norm_ple_g, v_ple_w_gate, v_ple_w_proj, v_final_norm_g):
    given = dict(locals())
    w = {n: given[n] for n in WEIGHT_NAMES}
    m = {n: given["m_" + n] for n in WEIGHT_NAMES}
    v = {n: given["v_" + n] for n in WEIGHT_NAMES}
    depth = p.shape[0]
    dev = 4 * lax.axis_index("x") + 2 * lax.axis_index("y") + lax.axis_index("c")

    wc = {n: _to_comm(n, w[n]) for n in COMM_NAMES}
    wb = {n: wc[n].astype(BF16) for n in COMM_NAMES}
    conv_local = [w[CONV_NAME], m[CONV_NAME], v[CONV_NAME]]
    conv_rows = _pack_rows(sum(int(np.prod(t.shape)) for t in conv_local), 8)
    conv_g, = _all_gather([_pack(conv_local, conv_rows)], 0, "gather_conv_taps")
    conv_parts = zip(*[_unpack(conv_g[j], [t.shape for t in conv_local]) for j in range(N_DEV)])
    conv_w, conv_m, conv_v = [jnp.concatenate(parts, axis=2) for parts in conv_parts]
    small = {n: w[n] for n in SMALL_NAMES}

    def whole(names, blocks):
        return {n: t.reshape(-1, t.shape[-1]) for n, t in zip(names, blocks)}

    def own_slot(block):
        return lax.dynamic_update_slice_in_dim(jnp.zeros((N_DEV,) + block.shape, block.dtype), block[None], dev, 0)

    def start_gather(names, i, after):
        srcs, after = lax.optimization_barrier(([wb[n][i] for n in names], after))
        return _exchange_start(srcs, [own_slot(s) for s in srcs], True, "gather_weights_%d_start" % i), after

    def wait_gather(names, i, started, after):
        send_sems, recv_sems, srcs, lands, _ = started
        return whole(names, _exchange_wait(send_sems, recv_sems, srcs, lands, after, True,
                                           "gather_weights_%d_wait" % i))

    at_once = ("w_in", "ssm_glu_w")
    later = tuple(n for n in COMM_NAMES if n not in at_once)
    w_in_0 = _all_gather([wb[n][0] for n in at_once], 0, "gather_w_in_0")
    gathering = {}
    gathering[0], (w_in_0, _) = start_gather(later, 0, (w_in_0, conv_g))
    small["norm_attn_g"] = small["norm_attn_g"] + gathering[0][4][0, 0]

    def layer_weights(i, h):
        if i > 0:
            got = wait_gather(COMM_NAMES, i, gathering.pop(i), h)
            if i + 1 < depth:
                gathering[i + 1], ordered = start_gather(COMM_NAMES, i + 1, got["w_in"])
                got["w_in"] = ordered + gathering[i + 1][4][0, 0].astype(BF16)
            return dict(got, **{CONV_NAME: conv_w[i]})

        def rest(after):
            got = wait_gather(later, 0, gathering.pop(0), after)
            if depth > 1:
                gathering[1], ordered = start_gather(COMM_NAMES, 1, got["w_out"])
                got["w_out"] = ordered + gathering[1][4][0, 0].astype(BF16)
            return got

        return dict(whole(at_once, w_in_0), **{CONV_NAME: conv_w[0], "rest": rest})

    def as_slots(g, n):
        return g.reshape((N_DEV,) + wc[n].shape[1:])

    scattering = {}

    def layer_done(i, stage, g, small_now):
        if stage == "all" and i == 0:
            return small_now
        names = COMM_EARLY if stage == "ffn" else COMM_LATE
        srcs = [as_slots(g[n], n) for n in names]
        lands = [own_slot(lax.dynamic_index_in_dim(s, dev, 0, keepdims=False)) for s in srcs]
        started = _exchange_start(srcs, lands, False, "scatter_weight_grads_%d_%s_start" % (i, stage))
        scattering[i, stage] = (names, started)
        pin = "branch_norm_g" if stage == "ffn" else "norm_ple_g"
        return dict(small_now, **{pin: small_now[pin] + started[4][0, 0]})

    loss, dx, big_grads, grads = _local_step(x[0], p[:, 0], loss_target[0], layer_weights, small, layer_done)
    loss = lax.psum(loss, ("x", "y", "c"))

    recv = [{} for _ in range(depth)]
    for (i, stage), (names, (send_sems, recv_sems, srcs, lands, _)) in scattering.items():
        got = _exchange_wait(send_sems, recv_sems, srcs, lands, dx, False,
                             "scatter_weight_grads_%d_%s_wait" % (i, stage))
        recv[i].update(zip(names, got))
    srcs = [as_slots(big_grads[0][n], n) for n in COMM_LATE]
    lands = [own_slot(lax.dynamic_index_in_dim(s, dev, 0, keepdims=False)) for s in srcs]
    last = _exchange_start(srcs, lands, False, "scatter_weight_grads_0_all_start")
    out = {}

    def update(n, pin=None):
        weight = wc[n] if pin is None else wc[n] + pin
        res = _adamw([recv[i][n] for i in range(depth)], weight, _to_comm(n, m[n]), _to_comm(n, v[n]),
                     "adamw_" + n, _tile_rows(wc[n].shape[1], 256))
        out[n] = [_to_comm(n, r) for r in res]

    for j, n in enumerate(COMM_EARLY):
        update(n, last[4][0, 0] if j == 0 else None)
    got = _exchange_wait(last[0], last[1], last[2], last[3], out[COMM_EARLY[-1]][0], False,
                         "scatter_weight_grads_0_all_wait")
    recv[0].update(zip(COMM_LATE, got))

    rep_names = SMALL_NAMES + (CONV_NAME,)
    rep_w = dict({n: w[n] for n in SMALL_NAMES}, **{CONV_NAME: conv_w})
    rep_m = dict({n: m[n] for n in SMALL_NAMES}, **{CONV_NAME: conv_m})
    rep_v = dict({n: v[n] for n in SMALL_NAMES}, **{CONV_NAME: conv_v})
    rep_shapes = [rep_w[n].shape for n in rep_names]
    rep_rows = _pack_rows(sum(int(np.prod(s)) for s in rep_shapes), SMALL_TILE_ROWS)
    rep_parts, = _all_gather([_pack([grads[n] for n in rep_names], rep_rows)], 0, "gather_small_grads")
    for n in COMM_LATE:
        update(n)
    rep_out = _adamw([rep_parts], *[_pack([src[n] for n in rep_names], rep_rows)[None] for src in (rep_w, rep_m, rep_v)],
                     "adamw_replicated", SMALL_TILE_ROWS)
    for n, vals in zip(rep_names, zip(*[_unpack(r[0], rep_shapes) for r in rep_out])):
        out[n] = list(vals)
    shard = ffn_conv_w.shape[2]
    out[CONV_NAME] = [lax.dynamic_slice_in_dim(t, dev * shard, shard, axis=2) for t in out[CONV_NAME]]
    results = [[out[n][kind] for n in WEIGHT_NAMES] for kind in range(4)]
    return (loss, dx[None], *results[0], *results[1], *results[2], *results[3])
```

```python
import math

import numpy as np
import jax
import jax.numpy as jnp
from jax import lax
from jax.experimental import pallas as pl
from jax.experimental.pallas import tpu as pltpu

F32 = jnp.float32
BF16 = jnp.bfloat16

D_MODEL = 1024
HEAD_DIM = 64
N_HEADS = 8
ATTN_W = 512
SGU_W = 256
SGU_GROUPS = 4
SGU_CHUNK = 128
SSM_W = 256
SSM_GROUPS = 16
SSM_CH = 16
SSM_STATE = 64
SSM_NS = SSM_GROUPS * SSM_STATE
IN_W = 2304
D_FF = 2816
PLE_DIM = 256
BRANCHES = ((128, 1), (512, 4), (2048, 16))
BLK = 128
N_BUCKETS = 32
REL_MAX = 2048
EPS = 1e-6
NEG_INF = -1e30
N_DEV = 8

ADAM_LR = 0.001
ADAM_B1 = 0.9
ADAM_B2 = 0.999
ADAM_EPS = 1e-08
ADAM_WD = 0.01
ADAM_STEP = 10

VMEM_LIMIT_BYTES = 56 * 1024 * 1024
GELU_C = math.sqrt(2.0 / math.pi)

SMALL_NAMES = ("rel_bias", "norm_attn_g", "sgu_ln_g", "sgu_ln_b", "sgu_w", "sgu_b", "ssm_a_re", "ssm_a_im",
               "ssm_log_dt", "ssm_b_re", "ssm_b_im", "ssm_c_re", "ssm_c_im", "ssm_d", "ssm_glu_b",
               "branch_norm_g", "norm_ffn_g", "ffn_conv_b", "norm_ple_g", "final_norm_g")
WEIGHT_NAMES = ("rel_bias", "norm_attn_g", "w_in", "sgu_ln_g", "sgu_ln_b", "sgu_w", "sgu_b", "ssm_a_re",
                "ssm_a_im", "ssm_log_dt", "ssm_b_re", "ssm_b_im", "ssm_c_re", "ssm_c_im", "ssm_d", "ssm_glu_w",
                "ssm_glu_b", "branch_norm_g", "w_out", "norm_ffn_g", "ffn_w_up", "ffn_conv_w", "ffn_conv_b",
                "ffn_w_down", "norm_ple_g", "ple_w_gate", "ple_w_proj", "final_norm_g")
PACK_COLS = 512


def _params(sem):
    return pltpu.CompilerParams(dimension_semantics=sem, vmem_limit_bytes=VMEM_LIMIT_BYTES)


def _pick(dim, target):
    if dim <= target:
        return dim
    best = None
    for t in range(128, target + 1, 128):
        if dim % t == 0:
            best = t
    return dim if best is None else best


def _gelu(x):
    return 0.5 * x * (1.0 + jnp.tanh(GELU_C * (x + 0.044715 * (x * x * x))))


def _gelu_grad(x):
    t = jnp.tanh(GELU_C * (x + 0.044715 * (x * x * x)))
    return 0.5 * (1.0 + t) + 0.5 * x * (1.0 - t * t) * (GELU_C * (1.0 + 3.0 * 0.044715 * (x * x)))


def _sigmoid(x):
    return 1.0 / (1.0 + jnp.exp(-x))


_DIMS = {"nn": (((1,), (0,)), ((), ())), "tn": (((0,), (0,)), ((), ())), "nt": (((1,), (1,)), ((), ()))}


def _mm(a, b, mode, name, add=None, out_dtype=F32, norm_gain=None, norm_bwd=None, tm=1408, tn=1408, tk=1408):
    if mode == "nn":
        m, k = a.shape
        k2, n = b.shape
    elif mode == "tn":
        k, m = a.shape
        k2, n = b.shape
    else:
        m, k = a.shape
        n, k2 = b.shape
    assert k == k2, (name, a.shape, b.shape, mode)
    tm, tn, tk = _pick(m, tm), _pick(n, tn), _pick(k, tk)
    nk = k // tk
    dims = _DIMS[mode]
    has_add = add is not None
    has_norm = norm_gain is not None
    has_nbwd = norm_bwd is not None
    assert not (has_norm or has_nbwd) or tn == n

    def body(*refs):
        a_ref, b_ref = refs[:2]
        rest = list(refs[2:])
        add_ref = rest.pop(0) if has_add else None
        g_ref = rest.pop(0) if has_norm else None
        h_ref, hg_ref = (rest.pop(0), rest.pop(0)) if has_nbwd else (None, None)
        o_ref = rest.pop(0)
        n_ref = rest.pop(0) if has_norm else None
        dg_ref = rest.pop(0) if has_nbwd else None
        part = lax.dot_general(a_ref[...].astype(BF16), b_ref[...].astype(BF16), dims,
                               preferred_element_type=F32)
        if has_nbwd:
            @pl.when((pl.program_id(0) == 0) & (pl.program_id(2) == 0))
            def _():
                dg_ref[...] = jnp.zeros_like(dg_ref)

        def finish(r):
            if has_nbwd:
                x = h_ref[...]
                scale = lax.rsqrt(jnp.mean(x * x, axis=-1, keepdims=True) + EPS)
                xh = x * scale
                dg_ref[...] += jnp.sum(r * xh, axis=0, keepdims=True)
                dxh = r * hg_ref[...]
                r = scale * (dxh - xh * jnp.mean(dxh * xh, axis=-1, keepdims=True))
            if has_add:
                r = r + add_ref[...]
            o_ref[...] = r.astype(out_dtype)
            if has_norm:
                scale = lax.rsqrt(jnp.mean(r * r, axis=-1, keepdims=True) + EPS)
                n_ref[...] = (r * scale * g_ref[...]).astype(BF16)

        if nk == 1:
            finish(part)
            return
        acc_ref = refs[-1]
        kk = pl.program_id(2)

        @pl.when(kk == 0)
        def _():
            acc_ref[...] = part

        @pl.when((kk > 0) & (kk < nk - 1))
        def _():
            acc_ref[...] += part

        @pl.when(kk == nk - 1)
        def _():
            finish(acc_ref[...] + part)

    if mode == "tn":
        a_spec = pl.BlockSpec((tk, tm), lambda i, j, kk: (kk, i))
    else:
        a_spec = pl.BlockSpec((tm, tk), lambda i, j, kk: (i, kk))
    if mode == "nt":
        b_spec = pl.BlockSpec((tn, tk), lambda i, j, kk: (j, kk))
    else:
        b_spec = pl.BlockSpec((tk, tn), lambda i, j, kk: (kk, j))
    o_spec = pl.BlockSpec((tm, tn), lambda i, j, kk: (i, j))
    in_specs = [a_spec, b_spec] + ([o_spec] if has_add else [])
    args = (a, b) + ((add,) if has_add else ())
    out_specs, out_shape = o_spec, jax.ShapeDtypeStruct((m, n), out_dtype)
    if has_norm:
        in_specs.append(pl.BlockSpec((1, n), lambda i, j, kk: (0, 0)))
        args += (norm_gain.reshape(1, n),)
        out_specs, out_shape = [o_spec, o_spec], [out_shape, jax.ShapeDtypeStruct((m, n), BF16)]
    if has_nbwd:
        row_spec = pl.BlockSpec((1, n), lambda i, j, kk: (0, 0))
        in_specs += [o_spec, row_spec]
        args += (norm_bwd[0], norm_bwd[1].reshape(1, n))
        out_specs, out_shape = [o_spec, row_spec], [out_shape, jax.ShapeDtypeStruct((1, n), F32)]
    sem = ("arbitrary",) * 3 if has_nbwd else ("parallel", "parallel", "arbitrary")
    return pl.pallas_call(
        body, name=name, grid=(m // tm, n // tn, nk),
        in_specs=in_specs, out_specs=out_specs, out_shape=out_shape,
        scratch_shapes=[pltpu.VMEM((tm, tn), F32)] if nk > 1 else [], compiler_params=_params(sem),
    )(*args)


def _rb(tm, w, cb=0):
    return pl.BlockSpec((tm, w), lambda i: (i, cb))


def _fb(shape):
    nd = len(shape)
    return pl.BlockSpec(shape, lambda i: (0,) * nd)


def _rowcall(body, name, n_rows, tm, in_specs, args, out_specs, out_shapes):
    return pl.pallas_call(
        body, name=name, grid=(n_rows // tm,), in_specs=in_specs, out_specs=out_specs, out_shape=out_shapes,
        compiler_params=_params(("arbitrary",)),
    )(*args)


def _sds(shape, dtype=F32):
    return jax.ShapeDtypeStruct(shape, dtype)


def _rms_fwd(h, g, name, tm=512):
    s, d = h.shape

    def body(h_ref, g_ref, o_ref):
        x = h_ref[...]
        r = lax.rsqrt(jnp.mean(x * x, axis=-1, keepdims=True) + EPS)
        o_ref[...] = (x * r * g_ref[...]).astype(BF16)

    return _rowcall(body, name, s, tm, [_rb(tm, d), _fb((1, d))], (h, g.reshape(1, d)), _rb(tm, d),
                    _sds((s, d), BF16))


def _loss_head(h, target, g, name, tm=512):
    s, d = h.shape

    def body(h_ref, t_ref, g_ref, dh_ref, loss_ref, dg_ref):
        @pl.when(pl.program_id(0) == 0)
        def _():
            dg_ref[...] = jnp.zeros_like(dg_ref)
            loss_ref[...] = jnp.zeros_like(loss_ref)

        x = h_ref[...]
        r = lax.rsqrt(jnp.mean(x * x, axis=-1, keepdims=True) + EPS)
        xh = x * r
        gg = g_ref[...]
        err = xh * gg - t_ref[...]
        loss_ref[...] += jnp.sum(err * err) * (0.5 / d)
        dy = err * (1.0 / d)
        dg_ref[...] += jnp.sum(dy * xh, axis=0, keepdims=True)
        dxh = dy * gg
        dh_ref[...] = r * (dxh - xh * jnp.mean(dxh * xh, axis=-1, keepdims=True))

    dh, loss, dg = _rowcall(body, name, s, tm, [_rb(tm, d), _rb(tm, d), _fb((1, d))], (h, target, g.reshape(1, d)),
                            [_rb(tm, d), _fb((1, 128)), _fb((1, d))], [_sds((s, d)), _sds((1, 128)), _sds((1, d))])
    return dh, loss[0, 0], dg.reshape(d)


_MIX_PARTS = ((0, 512), (512, 768), (768, 1024))


def _mix_fwd(ya, ysg, yss, g, name, tm=512):
    s = ya.shape[0]

    def body(a_ref, b_ref, c_ref, g_ref, o_ref):
        for ref, (lo, hi) in zip((a_ref, b_ref, c_ref), _MIX_PARTS):
            y = ref[...]
            r = lax.rsqrt(jnp.mean(y * y, axis=-1, keepdims=True) + EPS)
            o_ref[:, lo:hi] = (y * r * g_ref[:, lo:hi]).astype(BF16)

    return _rowcall(body, name, s, tm, [_rb(tm, 512), _rb(tm, 256), _rb(tm, 256), _fb((1, 1024))],
                    (ya, ysg, yss, g.reshape(1, 1024)), _rb(tm, 1024), _sds((s, 1024), BF16))


def _mix_bwd(dmix, ya, ysg, yss, g, name, tm=512):
    s = ya.shape[0]

    def body(dm_ref, a_ref, b_ref, c_ref, g_ref, da_ref, db_ref, dc_ref, dg_ref):
        @pl.when(pl.program_id(0) == 0)
        def _():
            dg_ref[...] = jnp.zeros_like(dg_ref)

        for ref, dref, (lo, hi) in zip((a_ref, b_ref, c_ref), (da_ref, db_ref, dc_ref), _MIX_PARTS):
            y = ref[...]
            r = lax.rsqrt(jnp.mean(y * y, axis=-1, keepdims=True) + EPS)
            xh = y * r
            dm = dm_ref[:, lo:hi]
            dg_ref[:, lo:hi] += jnp.sum(dm * xh, axis=0, keepdims=True)
            dxh = dm * g_ref[:, lo:hi]
            dref[...] = r * (dxh - xh * jnp.mean(dxh * xh, axis=-1, keepdims=True))

    da, db, dc, dg = _rowcall(
        body, name, s, tm, [_rb(tm, 1024), _rb(tm, 512), _rb(tm, 256), _rb(tm, 256), _fb((1, 1024))],
        (dmix, ya, ysg, yss, g.reshape(1, 1024)),
        [_rb(tm, 512), _rb(tm, 256), _rb(tm, 256), _fb((1, 1024))],
        [_sds((s, 512)), _sds((s, 256)), _sds((s, 256)), _sds((1, 1024))])
    return da, db, dc, dg.reshape(1024)


def _ssm_post_fwd(yc, z, d, gw, gb, name, tm=1024):
    s = yc.shape[0]

    def body(yc_ref, u_ref, d_ref, gw_ref, gb_ref, o_ref):
        y1 = yc_ref[...] + d_ref[...] * u_ref[...]
        y2 = _gelu(y1)
        gl = jnp.dot(y2.astype(BF16), gw_ref[...], preferred_element_type=F32) + gb_ref[...]
        o_ref[...] = y2 * _sigmoid(gl)

    return _rowcall(body, name, s, tm, [_rb(tm, 256), _rb(tm, 256, 8), _fb((1, 256)), _fb((256, 256)), _fb((1, 256))],
                    (yc, z, d.reshape(1, 256), gw, gb.reshape(1, 256)), _rb(tm, 256), _sds((s, 256)))


def _ssm_post_bwd(dy, yc, z, d, gw, gb, name, tm=1024):
    s = yc.shape[0]

    def body(dy_ref, yc_ref, u_ref, d_ref, gw_ref, gb_ref, dy1_ref, dgl_ref, y2_ref, dud_ref, dd_ref, dgb_ref):
        @pl.when(pl.program_id(0) == 0)
        def _():
            dd_ref[...] = jnp.zeros_like(dd_ref)
            dgb_ref[...] = jnp.zeros_like(dgb_ref)

        u = u_ref[...]
        dd = d_ref[...]
        y1 = yc_ref[...] + dd * u
        y2 = _gelu(y1)
        gw_v = gw_ref[...]
        gl = jnp.dot(y2.astype(BF16), gw_v, preferred_element_type=F32) + gb_ref[...]
        sg = _sigmoid(gl)
        dyv = dy_ref[...]
        dgl = dyv * y2 * sg * (1.0 - sg)
        dy2 = dyv * sg + lax.dot_general(dgl.astype(BF16), gw_v, _DIMS["nt"], preferred_element_type=F32)
        dy1 = dy2 * _gelu_grad(y1)
        dy1_ref[...] = dy1.astype(BF16)
        dgl_ref[...] = dgl.astype(BF16)
        y2_ref[...] = y2.astype(BF16)
        dud_ref[...] = dy1 * dd
        dd_ref[...] += jnp.sum(dy1 * u, axis=0, keepdims=True)
        dgb_ref[...] += jnp.sum(dgl, axis=0, keepdims=True)

    outs = _rowcall(
        body, name, s, tm,
        [_rb(tm, 256), _rb(tm, 256), _rb(tm, 256, 8), _fb((1, 256)), _fb((256, 256)), _fb((1, 256))],
        (dy, yc, z, d.reshape(1, 256), gw, gb.reshape(1, 256)),
        [_rb(tm, 256)] * 4 + [_fb((1, 256))] * 2,
        [_sds((s, 256), BF16)] * 3 + [_sds((s, 256))] + [_sds((1, 256))] * 2)
    dy1, dgl, y2, dud, dd, dgb = outs
    return dy1, dgl, y2, dud, dd.reshape(256), dgb.reshape(256)


SCAN_T = 512
N_SCAN_TABLES = 6


def _scan_tables(pr, pi, reverse):
    ns = pr.shape[0]
    sign = -1.0 if reverse else 1.0
    power = [(jnp.ones((ns,), F32), jnp.zeros((ns,), F32))] + [(pr[:, k], sign * pi[:, k]) for k in range(8)]
    zero = (jnp.zeros((ns,), F32), jnp.zeros((ns,), F32))

    def table(exponents):
        rows = [zero if e is None else power[e] for e in exponents]
        return jnp.stack([jnp.concatenate(row) for row in rows])

    tabs = []
    for k in (1, 2, 4):
        has_partner = [(s < 8 - k) if reverse else (s >= k) for s in range(8)]
        tabs.append(table([k if ok else None for ok in has_partner]))
    tabs.append(table([s if reverse else 7 - s for s in range(8)]))
    tabs.append(table([8 - s if reverse else s + 1 for s in range(8)]))
    tabs.append(table([8] * 8))
    return jnp.stack(tabs)


def _cmul(ar, ai, br, bi):
    return ar * br - ai * bi, ar * bi + ai * br


def _scan_group(ur, ui, cr, ci, tr_ref, ti_ref, reverse):
    xr, xi = ur, ui
    for n, k in enumerate((1, 2, 4)):
        shift = 8 - k if reverse else k
        pr, pi = _cmul(tr_ref[n], ti_ref[n], pltpu.roll(xr, shift, axis=0), pltpu.roll(xi, shift, axis=0))
        xr, xi = xr + pr, xi + pi
    sr, si = _cmul(tr_ref[3], ti_ref[3], ur, ui)
    for k in (1, 2, 4):
        sr, si = sr + pltpu.roll(sr, k, axis=0), si + pltpu.roll(si, k, axis=0)
    pr, pi = _cmul(tr_ref[4], ti_ref[4], cr, ci)
    nr, ni = _cmul(tr_ref[5], ti_ref[5], cr, ci)
    return xr + pr, xi + pi, nr + sr, ni + si


def _table_halves(t_ref):
    return t_ref.at[:, :, pl.ds(0, SSM_NS)], t_ref.at[:, :, pl.ds(SSM_NS, SSM_NS)]


_U_BLOCK = (IN_W - SSM_W) // SSM_W


def _ssm_fwd(z, bdt, cd, tabs, name):
    s = z.shape[0]
    ns = SSM_NS
    n_t = s // SCAN_T

    def body(u_ref, b_ref, c_ref, t_ref, xr_ref, xi_ref, y_ref, cr_ref, ci_ref, ur_ref, ui_ref):
        @pl.when(pl.program_id(0) == 0)
        def _():
            cr_ref[...] = jnp.zeros_like(cr_ref)
            ci_ref[...] = jnp.zeros_like(ci_ref)

        bu = lax.dot_general(u_ref[...].astype(BF16), b_ref[...], _DIMS["nt"], preferred_element_type=F32)
        ur_ref[...] = bu[:, :ns]
        ui_ref[...] = bu[:, ns:]
        tr_ref, ti_ref = _table_halves(t_ref)

        def group(g, carry):
            rows = pl.ds(pl.multiple_of(g * 8, 8), 8)
            xr, xi, cr, ci = _scan_group(ur_ref[rows, :], ui_ref[rows, :], *carry, tr_ref, ti_ref, False)
            xr_ref[rows, :] = xr
            xi_ref[rows, :] = xi
            return cr, ci

        cr, ci = lax.fori_loop(0, SCAN_T // 8, group, (cr_ref[...], ci_ref[...]), unroll=2)
        cr_ref[...] = cr
        ci_ref[...] = ci
        y_ref[...] = (jnp.dot(xr_ref[...].astype(BF16), c_ref[0:ns, :], preferred_element_type=F32)
                      + jnp.dot(xi_ref[...].astype(BF16), c_ref[ns:, :], preferred_element_type=F32))

    x_spec = pl.BlockSpec((SCAN_T, ns), lambda t: (t, 0))
    return pl.pallas_call(
        body, name=name, grid=(n_t,),
        in_specs=[pl.BlockSpec((SCAN_T, SSM_W), lambda t: (t, _U_BLOCK)), _fb((2 * ns, SSM_W)),
                  _fb((2 * ns, SSM_W)), _fb((N_SCAN_TABLES, 8, 2 * ns))],
        out_specs=[x_spec, x_spec, _rb(SCAN_T, SSM_W)],
        out_shape=[_sds((s, ns)), _sds((s, ns)), _sds((s, SSM_W))],
        scratch_shapes=[pltpu.VMEM((8, ns), F32)] * 2 + [pltpu.VMEM((SCAN_T, ns), F32)] * 2,
        compiler_params=_params(("arbitrary",)),
    )(z, bdt, cd, tabs)


def _ssm_bwd(dy1, dud, z, xr, xi, bdt, cd, tabs, name):
    s = z.shape[0]
    ns = SSM_NS
    n_t = s // SCAN_T
    n_groups = SCAN_T // 8

    def body(dy_ref, dud_ref, u_ref, xr_ref, xi_ref, pxr_ref, pxi_ref, b_ref, c_ref, t_ref,
             du_ref, dbd_ref, dcd_ref, dar_ref, dai_ref,
             cr_ref, ci_ref, ar_ref, ai_ref, sxr_ref, sxi_ref, gr_ref, gi_ref, lr_ref, li_ref, bacc_ref, cacc_ref):
        t = pl.program_id(0)

        @pl.when(t == 0)
        def _():
            for ref in (cr_ref, ci_ref, ar_ref, ai_ref, bacc_ref, cacc_ref):
                ref[...] = jnp.zeros_like(ref)

        dyb = dy_ref[...]
        g = lax.dot_general(dyb, c_ref[...], _DIMS["nt"], preferred_element_type=F32)
        gr_ref[...] = g[:, :ns]
        gi_ref[...] = g[:, ns:]
        has_before = (t < n_t - 1).astype(F32)
        sxr_ref[0:8, :] = pxr_ref[...] * has_before
        sxi_ref[0:8, :] = pxi_ref[...] * has_before
        sxr_ref[8:, :] = xr_ref[...]
        sxi_ref[8:, :] = xi_ref[...]
        first_row = lax.broadcasted_iota(jnp.int32, (8, ns), 0) == 0
        tr_ref, ti_ref = _table_halves(t_ref)

        def group(k, carry):
            cr, ci, ar, ai = carry
            g8 = pl.multiple_of((n_groups - 1 - k) * 8, 8)
            rows = pl.ds(g8, 8)
            lr, li, cr, ci = _scan_group(gr_ref[rows, :], gi_ref[rows, :], cr, ci, tr_ref, ti_ref, True)
            lr_ref[rows, :] = lr
            li_ref[rows, :] = li
            here, before = pl.ds(g8 + 8, 8), rows
            pr = jnp.where(first_row, pltpu.roll(sxr_ref[before, :], 1, axis=0), pltpu.roll(sxr_ref[here, :], 1, axis=0))
            pi = jnp.where(first_row, pltpu.roll(sxi_ref[before, :], 1, axis=0), pltpu.roll(sxi_ref[here, :], 1, axis=0))
            return cr, ci, ar + lr * pr + li * pi, ai + li * pr - lr * pi

        cr, ci, ar, ai = lax.fori_loop(0, n_groups, group,
                                       (cr_ref[...], ci_ref[...], ar_ref[...], ai_ref[...]), unroll=2)
        cr_ref[...] = cr
        ci_ref[...] = ci
        ar_ref[...] = ar
        ai_ref[...] = ai
        lrb = lr_ref[...].astype(BF16)
        lib = li_ref[...].astype(BF16)
        ub = u_ref[...].astype(BF16)
        du_ref[...] = (dud_ref[...] + jnp.dot(lrb, b_ref[0:ns, :], preferred_element_type=F32)
                       + jnp.dot(lib, b_ref[ns:, :], preferred_element_type=F32))
        bacc_ref[0:ns, :] += lax.dot_general(lrb, ub, _DIMS["tn"], preferred_element_type=F32)
        bacc_ref[ns:, :] += lax.dot_general(lib, ub, _DIMS["tn"], preferred_element_type=F32)
        cacc_ref[0:ns, :] += lax.dot_general(xr_ref[...].astype(BF16), dyb, _DIMS["tn"], preferred_element_type=F32)
        cacc_ref[ns:, :] += lax.dot_general(xi_ref[...].astype(BF16), dyb, _DIMS["tn"], preferred_element_type=F32)

        @pl.when(t == n_t - 1)
        def _():
            for k in (1, 2, 4):
                ar_ref[...] += pltpu.roll(ar_ref[...], k, axis=0)
                ai_ref[...] += pltpu.roll(ai_ref[...], k, axis=0)
            dar_ref[...] = ar_ref[...]
            dai_ref[...] = ai_ref[...]
            dbd_ref[...] = bacc_ref[...]
            dcd_ref[...] = cacc_ref[...]

    rev = lambda t: n_t - 1 - t
    row_spec = pl.BlockSpec((SCAN_T, SSM_W), lambda t: (rev(t), 0))
    x_spec = pl.BlockSpec((SCAN_T, ns), lambda t: (rev(t), 0))
    before_spec = pl.BlockSpec((8, ns), lambda t: (jnp.maximum(rev(t) * (SCAN_T // 8) - 1, 0), 0))
    du, dbd, dcd, dar, dai = pl.pallas_call(
        body, name=name, grid=(n_t,),
        in_specs=[row_spec, row_spec, pl.BlockSpec((SCAN_T, SSM_W), lambda t: (rev(t), _U_BLOCK)),
                  x_spec, x_spec, before_spec, before_spec,
                  _fb((2 * ns, SSM_W)), _fb((2 * ns, SSM_W)), _fb((N_SCAN_TABLES, 8, 2 * ns))],
        out_specs=[row_spec, _fb((2 * ns, SSM_W)), _fb((2 * ns, SSM_W)), _fb((8, ns)), _fb((8, ns))],
        out_shape=[_sds((s, SSM_W)), _sds((2 * ns, SSM_W)), _sds((2 * ns, SSM_W)), _sds((8, ns)), _sds((8, ns))],
        scratch_shapes=([pltpu.VMEM((8, ns), F32)] * 4 + [pltpu.VMEM((SCAN_T + 8, ns), F32)] * 2
                        + [pltpu.VMEM((SCAN_T, ns), F32)] * 4 + [pltpu.VMEM((2 * ns, SSM_W), F32)] * 2),
        compiler_params=_params(("arbitrary",)),
    )(dy1, dud, z, xr, xi, xr, xi, bdt, cd, tabs)
    return du, dbd, dcd, dar[0], dai[0]


def _group_ids():
    return lax.broadcasted_iota(jnp.int32, (1, SGU_W), 1) // 64


def _group_mean(val, gid):
    out = jnp.zeros_like(val)
    for g in range(SGU_GROUPS):
        mg = gid == g
        out = jnp.where(mg, jnp.sum(jnp.where(mg, val, 0.0), axis=1, keepdims=True) * (1.0 / 64), out)
    return out


def _causal_w(w_ref, g):
    t = lax.broadcasted_iota(jnp.int32, (SGU_CHUNK, SGU_CHUNK), 0)
    s = lax.broadcasted_iota(jnp.int32, (SGU_CHUNK, SGU_CHUNK), 1)
    return jnp.where(t >= s, w_ref[g], 0.0).astype(BF16)


def _sgu_core(x, lng, lnb, w_ref, bexp, gid):
    zz = _gelu(x)
    u = zz[:, :SGU_W]
    v = zz[:, SGU_W:]
    vc = v - _group_mean(v, gid)
    rstd = lax.rsqrt(_group_mean(vc * vc, gid) + EPS)
    vhat = vc * rstd
    vn = vhat * lng + lnb
    vnb = vn.astype(BF16)
    mixed = bexp
    for g in range(SGU_GROUPS):
        mm = jnp.dot(_causal_w(w_ref, g), vnb, preferred_element_type=F32)
        mixed = jnp.where(gid == g, mm + bexp, mixed)
    return u, rstd, vhat, vnb, mixed


def _sgu_fwd(z, lng, lnb, w, bexp, name, tm=512):
    s = z.shape[0]

    def body(z_ref, lng_ref, lnb_ref, w_ref, b_ref, o_ref):
        gid = _group_ids()
        for j in range(tm // SGU_CHUNK):
            rows = pl.ds(j * SGU_CHUNK, SGU_CHUNK)
            u, _, _, _, mixed = _sgu_core(z_ref[rows, :], lng_ref[...], lnb_ref[...], w_ref, b_ref[...], gid)
            o_ref[rows, :] = u * mixed

    return _rowcall(body, name, s, tm,
                    [_rb(tm, 512, 3), _fb((1, 256)), _fb((1, 256)), _fb((4, 128, 128)), _fb((128, 256))],
                    (z, lng.reshape(1, 256), lnb.reshape(1, 256), w, bexp), _rb(tm, 256), _sds((s, 256)))


def _sgu_bwd(z, dy, lng, lnb, w, bexp, name, tm=512):
    s = z.shape[0]

    def body(z_ref, dy_ref, lng_ref, lnb_ref, w_ref, b_ref, dz_ref, dw_ref, db_ref, dlng_ref, dlnb_ref):
        @pl.when(pl.program_id(0) == 0)
        def _():
            dw_ref[...] = jnp.zeros_like(dw_ref)
            db_ref[...] = jnp.zeros_like(db_ref)
            dlng_ref[...] = jnp.zeros_like(dlng_ref)
            dlnb_ref[...] = jnp.zeros_like(dlnb_ref)

        gid = _group_ids()
        t = lax.broadcasted_iota(jnp.int32, (SGU_CHUNK, SGU_CHUNK), 0)
        sidx = lax.broadcasted_iota(jnp.int32, (SGU_CHUNK, SGU_CHUNK), 1)
        lng_v = lng_ref[...]
        for j in range(tm // SGU_CHUNK):
            rows = pl.ds(j * SGU_CHUNK, SGU_CHUNK)
            x = z_ref[rows, :]
            u, rstd, vhat, vnb, mixed = _sgu_core(x, lng_v, lnb_ref[...], w_ref, b_ref[...], gid)
            dyv = dy_ref[rows, :]
            dmixed = dyv * u
            du = dyv * mixed
            db_ref[...] += dmixed
            dvn = jnp.zeros_like(dmixed)
            for g in range(SGU_GROUPS):
                dmg = jnp.where(gid == g, dmixed, 0.0).astype(BF16)
                dvn = dvn + lax.dot_general(_causal_w(w_ref, g), dmg, _DIMS["tn"], preferred_element_type=F32)
                dwg = lax.dot_general(dmg, vnb, _DIMS["nt"], preferred_element_type=F32)
                dw_ref[g] += jnp.where(t >= sidx, dwg, 0.0)
            dlnb_ref[...] += jnp.sum(dvn, axis=0, keepdims=True)
            dlng_ref[...] += jnp.sum(dvn * vhat, axis=0, keepdims=True)
            dvh = dvn * lng_v
            dv = rstd * (dvh - _group_mean(dvh, gid) - vhat * _group_mean(dvh * vhat, gid))
            gg = _gelu_grad(x)
            dz_ref[rows, 0:SGU_W] = du * gg[:, :SGU_W]
            dz_ref[rows, SGU_W:2 * SGU_W] = dv * gg[:, SGU_W:]

    dz, dw, db, dlng, dlnb = _rowcall(
        body, name, s, tm,
        [_rb(tm, 512, 3), _rb(tm, 256), _fb((1, 256)), _fb((1, 256)), _fb((4, 128, 128)), _fb((128, 256))],
        (z, dy, lng.reshape(1, 256), lnb.reshape(1, 256), w, bexp),
        [_rb(tm, 512), _fb((4, 128, 128)), _fb((128, 256)), _fb((1, 256)), _fb((1, 256))],
        [_sds((s, 512)), _sds((4, 128, 128)), _sds((128, 256)), _sds((1, 256)), _sds((1, 256))])
    return dz, dw, db, dlng.reshape(256), dlnb.reshape(256)


CONV_TC = 1408
N_CT = D_FF // CONV_TC


def _row_of(block8, j):
    r = lax.broadcasted_iota(jnp.int32, block8.shape, 0)
    return jnp.sum(jnp.where(r == j, block8, 0.0), axis=0, keepdims=True)


EDGE = 16


def _conv_fwd(hu, cw, cb, name, tm=256):
    s = hu.shape[0]

    def body(xv_ref, xg_ref, tv_ref, tg_ref, wv_ref, wg_ref, bv_ref, bg_ref, hv_ref, hg_ref, act_ref):
        has_prev = (pl.program_id(1) > 0).astype(F32)
        row = lax.broadcasted_iota(jnp.int32, (EDGE, CONV_TC), 0)

        def conv(x_ref, t_ref, w_ref, b_ref):
            x = x_ref[...].astype(F32)
            w0, w1, w2, bb = w_ref[0:1, :], w_ref[1:2, :], w_ref[2:3, :], b_ref[...]
            whole = w0 * pltpu.roll(x, 2, axis=0) + w1 * pltpu.roll(x, 1, axis=0) + w2 * x + bb
            tail = t_ref[...].astype(F32)
            r7 = _row_of(tail, EDGE - 1) * has_prev
            r6 = _row_of(tail, EDGE - 2) * has_prev
            xe = x_ref[0:EDGE, :].astype(F32)
            x1 = jnp.where(row == 0, r7, pltpu.roll(xe, 1, axis=0))
            x2 = jnp.where(row == 0, r6, jnp.where(row == 1, r7, pltpu.roll(xe, 2, axis=0)))
            return whole, w0 * x2 + w1 * x1 + w2 * xe + bb

        hv, hv_edge = conv(xv_ref, tv_ref, wv_ref, bv_ref)
        hg, hg_edge = conv(xg_ref, tg_ref, wg_ref, bg_ref)
        hv_ref[...] = hv.astype(BF16)
        hg_ref[...] = hg.astype(BF16)
        act_ref[...] = (_gelu(hg) * hv).astype(BF16)
        hv_ref[0:EDGE, :] = hv_edge.astype(BF16)
        hg_ref[0:EDGE, :] = hg_edge.astype(BF16)
        act_ref[0:EDGE, :] = (_gelu(hg_edge) * hv_edge).astype(BF16)

    def xs(off):
        return pl.BlockSpec((tm, CONV_TC), lambda j, i: (i, j + off))

    def ts(off):
        return pl.BlockSpec((EDGE, CONV_TC), lambda j, i: (jnp.maximum(i * (tm // EDGE) - 1, 0), j + off))

    def ws(rows, off):
        return pl.BlockSpec((rows, CONV_TC), lambda j, i: (0, j + off))

    o_spec = pl.BlockSpec((tm, CONV_TC), lambda j, i: (i, j))
    return pl.pallas_call(
        body, name=name, grid=(N_CT, s // tm),
        in_specs=[xs(0), xs(N_CT), ts(0), ts(N_CT), ws(3, 0), ws(3, N_CT), ws(1, 0), ws(1, N_CT)],
        out_specs=[o_spec] * 3, out_shape=[_sds((s, D_FF), BF16)] * 3,
        compiler_params=_params(("parallel", "arbitrary")),
    )(hu, hu, hu, hu, cw, cw, cb.reshape(1, 2 * D_FF), cb.reshape(1, 2 * D_FF))


HALO = EDGE


def _conv_bwd(dact, hv, hg, hu, cw, name, tm=256):
    s = dact.shape[0]

    def body(da_ref, dan_ref, hv_ref, hvn_ref, hg_ref, hgn_ref, x_ref, t_ref, w_ref, dx_ref, dw_ref, db_ref, d_scr):
        i = pl.program_id(1)
        is_value = pl.program_id(0) < N_CT

        @pl.when(i == 0)
        def _():
            dw_ref[...] = jnp.zeros_like(dw_ref)
            db_ref[...] = jnp.zeros_like(db_ref)

        for rows, (a_ref, v_ref, g_ref) in ((pl.ds(0, tm), (da_ref, hv_ref, hg_ref)),
                                            (pl.ds(tm, HALO), (dan_ref, hvn_ref, hgn_ref))):
            @pl.when(is_value)
            def _():
                d_scr[rows, :] = a_ref[...].astype(F32) * _gelu(g_ref[...].astype(F32))

            @pl.when(jnp.logical_not(is_value))
            def _():
                d_scr[rows, :] = (a_ref[...].astype(F32) * v_ref[...].astype(F32)
                                  * _gelu_grad(g_ref[...].astype(F32)))

        has_prev = (i > 0).astype(F32)
        has_next = (i < s // tm - 1).astype(F32)
        w0, w1, w2 = w_ref[0:1, :], w_ref[1:2, :], w_ref[2:3, :]
        d = d_scr[0:tm, :]
        dx_ref[...] = (w2 * d + w1 * pltpu.roll(d, tm - 1, axis=0) + w0 * pltpu.roll(d, tm - 2, axis=0)).astype(BF16)
        row = lax.broadcasted_iota(jnp.int32, (EDGE, CONV_TC), 0)
        nxt = d_scr[tm:tm + HALO, :]
        n0 = _row_of(nxt, 0) * has_next
        n1 = _row_of(nxt, 1) * has_next
        de = d_scr[tm - EDGE:tm, :]
        d1 = jnp.where(row == EDGE - 1, n0, pltpu.roll(de, EDGE - 1, axis=0))
        d2 = jnp.where(row == EDGE - 2, n0, jnp.where(row == EDGE - 1, n1, pltpu.roll(de, EDGE - 2, axis=0)))
        dx_ref[tm - EDGE:tm, :] = (w2 * de + w1 * d1 + w0 * d2).astype(BF16)
        x = x_ref[...].astype(F32)
        tail = t_ref[...].astype(F32)
        r7 = _row_of(tail, EDGE - 1) * has_prev
        r6 = _row_of(tail, EDGE - 2) * has_prev
        last = x_ref[tm - EDGE:tm, :].astype(F32)
        l7, l6 = _row_of(last, EDGE - 1), _row_of(last, EDGE - 2)
        head = d_scr[0:8, :]
        d0, d1h = _row_of(head, 0), _row_of(head, 1)
        dw_ref[0:1, :] += (jnp.sum(d * pltpu.roll(x, 2, axis=0), axis=0, keepdims=True)
                           + d0 * (r6 - l6) + d1h * (r7 - l7))
        dw_ref[1:2, :] += jnp.sum(d * pltpu.roll(x, 1, axis=0), axis=0, keepdims=True) + d0 * (r7 - l7)
        dw_ref[2:3, :] += jnp.sum(d * x, axis=0, keepdims=True)
        db_ref[...] += jnp.sum(d, axis=0, keepdims=True)

    a_spec = pl.BlockSpec((tm, CONV_TC), lambda j, i: (i, j % N_CT))
    an_spec = pl.BlockSpec((HALO, CONV_TC),
                           lambda j, i: (jnp.minimum((i + 1) * (tm // HALO), s // HALO - 1), j % N_CT))
    x_spec = pl.BlockSpec((tm, CONV_TC), lambda j, i: (i, j))
    t_spec = pl.BlockSpec((EDGE, CONV_TC), lambda j, i: (jnp.maximum(i * (tm // EDGE) - 1, 0), j))
    w_spec = pl.BlockSpec((3, CONV_TC), lambda j, i: (0, j))
    db_spec = pl.BlockSpec((1, CONV_TC), lambda j, i: (0, j))
    return pl.pallas_call(
        body, name=name, grid=(2 * N_CT, s // tm),
        in_specs=[a_spec, an_spec, a_spec, an_spec, a_spec, an_spec, x_spec, t_spec, w_spec],
        out_specs=[x_spec, w_spec, db_spec],
        out_shape=[_sds((s, 2 * D_FF), BF16), _sds((3, 2 * D_FF)), _sds((1, 2 * D_FF))],
        scratch_shapes=[pltpu.VMEM((tm + HALO, CONV_TC), F32)],
        compiler_params=_params(("parallel", "arbitrary")),
    )(dact, dact, hv, hv, hg, hg, hu, hu, cw)


def _ple_fwd(h, gp, pp, next_gain, name, tm=512):
    s, d = h.shape
    with_norm = next_gain is not None

    def body(*refs):
        h_ref, g_ref, p_ref = refs[:3]
        out = h_ref[...] + _sigmoid(g_ref[...].astype(F32)) * p_ref[...].astype(F32)
        if with_norm:
            n_ref, o_ref, a_ref = refs[3:]
            scale = lax.rsqrt(jnp.mean(out * out, axis=-1, keepdims=True) + EPS)
            a_ref[...] = (out * scale * n_ref[...]).astype(BF16)
        else:
            o_ref, = refs[3:]
        o_ref[...] = out

    if not with_norm:
        return _rowcall(body, name, s, tm, [_rb(tm, d)] * 3, (h, gp, pp), _rb(tm, d), _sds((s, d))), None
    return _rowcall(body, name, s, tm, [_rb(tm, d)] * 3 + [_fb((1, d))], (h, gp, pp, next_gain.reshape(1, d)),
                    [_rb(tm, d)] * 2, [_sds((s, d)), _sds((s, d), BF16)])


def _ple_bwd(dh, gp, pp, name, tm=512):
    s, d = dh.shape

    def body(d_ref, g_ref, p_ref, dp_ref, dg_ref):
        sg = _sigmoid(g_ref[...].astype(F32))
        dv = d_ref[...]
        dp_ref[...] = (dv * sg).astype(BF16)
        dg_ref[...] = (dv * p_ref[...].astype(F32) * sg * (1.0 - sg)).astype(BF16)

    return _rowcall(body, name, s, tm, [_rb(tm, d)] * 3, (dh, gp, pp), [_rb(tm, d)] * 2,
                    [_sds((s, d), BF16)] * 2)


SCALE = HEAD_DIM ** -0.5
ATT_ROWS = 2048


def _att_geom(s, dil):
    w = min(ATT_ROWS, s)
    p = BLK * dil
    assert w % p == 0 and s % w == 0
    return w, p, w // p


def _rows(start, dil):
    return pl.ds(start, BLK, stride=dil) if dil > 1 else pl.ds(start, BLK)


def _head_masks():
    lane = lax.broadcasted_iota(jnp.int32, (1, BLK), 1)
    return [lane < HEAD_DIM, lane >= HEAD_DIM]


def _band():
    rel = np.arange(BLK)[:, None] + BLK - np.arange(2 * BLK)[None, :]
    return (rel >= 0) & (rel <= BLK)


def _zcur(w):
    return lambda off: pl.BlockSpec((w, BLK), lambda hp, i: (i, off + hp))


def _zprev(p, nb):
    return lambda off: pl.BlockSpec((p, BLK), lambda hp, i: (jnp.maximum(i * nb - 1, 0), off + hp))


def _scur(w):
    return pl.BlockSpec((w, BLK), lambda hp, i: (i, hp))


def _pair_rows(t, masks):
    return jnp.concatenate([jnp.where(masks[0], t, 0.0), jnp.where(masks[1], t, 0.0)], axis=0).astype(BF16)


def _pair_bias_bwd(bias):
    return bias.reshape(4, 2, BLK, 2, BLK).transpose(0, 3, 2, 1, 4).reshape(4, 2, BLK, 2 * BLK)


def _unpair_bias_bwd(db):
    return db.reshape(4, 2, BLK, 2, BLK).transpose(0, 3, 2, 1, 4).reshape(N_HEADS, BLK, 2 * BLK)


def _attn_fwd(z, biases, name):
    s = z.shape[0]
    w = min(ATT_ROWS, s)
    n_br = len(BRANCHES)

    def body(*refs):
        q_ref, kp_ref, kc_ref, vp_ref, vc_ref = refs[:5]
        b_refs = refs[5:5 + n_br]
        y_ref, lse_ref, m_ref, l_ref, a_ref = refs[5 + n_br:]
        i = pl.program_id(1)
        masks = _head_masks()
        own_block = lax.broadcasted_iota(jnp.int32, (1, 2 * BLK), 1) >= BLK
        for n, (_, dil) in enumerate(BRANCHES):
            _, p, nb = _att_geom(s, dil)
            for r in range(dil):
                for b in range(nb):
                    rows = _rows(r + p * b, dil)
                    prev_rows = _rows(r + p * (b - 1), dil) if b > 0 else _rows(w - p + r, dil)
                    kprev, vprev = (kc_ref, vc_ref) if b > 0 else (kp_ref, vp_ref)
                    q = q_ref[rows, :] * SCALE
                    k = jnp.concatenate([kprev[prev_rows, :], kc_ref[rows, :]], axis=0).astype(BF16)
                    v = jnp.concatenate([vprev[prev_rows, :], vc_ref[rows, :]], axis=0).astype(BF16)
                    mb = lb = ob = None
                    for hh, mh in enumerate(masks):
                        qh = jnp.where(mh, q, 0.0).astype(BF16)
                        sc = lax.dot_general(qh, k, _DIMS["nt"], preferred_element_type=F32) + b_refs[n][hh]
                        if b == 0:
                            sc = jnp.where(own_block | (i > 0), sc, NEG_INF)
                        mx = jnp.max(sc, axis=1, keepdims=True)
                        e = jnp.exp(sc - mx)
                        den = jnp.sum(e, axis=1, keepdims=True)
                        o = jnp.dot(e.astype(BF16), v, preferred_element_type=F32)
                        if hh == 0:
                            mb = jnp.broadcast_to(mx, (BLK, BLK))
                            lb = jnp.broadcast_to(den, (BLK, BLK))
                            ob = o
                        else:
                            mb = jnp.where(mh, mx, mb)
                            lb = jnp.where(mh, den, lb)
                            ob = jnp.where(mh, o, ob)
                    if n == 0:
                        m_new, l_new, a_new = mb, lb, ob
                    else:
                        m_old = m_ref[rows, :]
                        m_new = jnp.maximum(m_old, mb)
                        al = jnp.exp(m_old - m_new)
                        be = jnp.exp(mb - m_new)
                        l_new = al * l_ref[rows, :] + be * lb
                        a_new = al * a_ref[rows, :] + be * ob
                    if n == n_br - 1:
                        y_ref[rows, :] = a_new / l_new
                        lse_ref[rows, :] = m_new + jnp.log(l_new)
                    else:
                        m_ref[rows, :] = m_new
                        l_ref[rows, :] = l_new
                        a_ref[rows, :] = a_new

    cur, prv = _zcur(w), _zprev(w, 1)
    b_spec = pl.BlockSpec((2, BLK, 2 * BLK), lambda hp, i: (hp, 0, 0))
    return pl.pallas_call(
        body, name=name, grid=(4, s // w), in_specs=[cur(0), prv(4), cur(4), prv(8), cur(8)] + [b_spec] * n_br,
        out_specs=[_scur(w)] * 2, out_shape=[_sds((s, ATTN_W))] * 2,
        scratch_shapes=[pltpu.VMEM((w, BLK), F32)] * 3,
        compiler_params=_params(("parallel", "parallel")),
    )(z, z, z, z, z, *biases)


def _row_stats(mh, dy, y, lse):
    delta = jnp.sum(jnp.where(mh, dy * y, 0.0), axis=1, keepdims=True)
    lse_h = jnp.max(jnp.where(mh, lse, NEG_INF), axis=1, keepdims=True)
    return delta, lse_h


def _attn_bwd(z, bias, dy, y, lse, prev, dil, name):
    s = z.shape[0]
    w, p, nb = _att_geom(s, dil)
    n_steps = s // w
    first = prev is None

    def body(*refs):
        q_ref, kp_ref, kc_ref, vp_ref, vc_ref, b_ref, dy_ref, y_ref, lse_ref = refs[:9]
        rest = refs[9:]
        if not first:
            dqp_ref, dkp_ref, dvp_ref = rest[:3]
            rest = rest[3:]
        dq_ref, dk_ref, dv_ref, dkx_ref, dvx_ref, db_ref = rest
        i = pl.program_id(1)

        @pl.when(i == 0)
        def _():
            db_ref[...] = jnp.zeros_like(db_ref)

        masks = _head_masks()
        first_head = lax.broadcasted_iota(jnp.int32, (1, 2 * BLK), 1) < BLK

        def flush(rows, dk, dv):
            if not first:
                dk = dk + dkp_ref[rows, :]
                dv = dv + dvp_ref[rows, :]
            dk_ref[rows, :] = dk
            dv_ref[rows, :] = dv

        for r in range(dil):
            carry = None
            for b in range(nb):
                rows = _rows(r + p * b, dil)
                prev_rows = _rows(r + p * (b - 1), dil) if b > 0 else _rows(r, dil)
                kprev, vprev = (kc_ref, vc_ref) if b > 0 else (kp_ref, vp_ref)
                keys = [(_pair_rows(kprev[prev_rows, :], masks), _pair_rows(vprev[prev_rows, :], masks)),
                        (_pair_rows(kc_ref[rows, :], masks), _pair_rows(vc_ref[rows, :], masks))]
                q = (q_ref[rows, :] * SCALE).astype(BF16)
                dy_v = dy_ref[rows, :]
                dyb = dy_v.astype(BF16)
                stats = [_row_stats(mh, dy_v, y_ref[rows, :], lse_ref[rows, :]) for mh in masks]
                delta = jnp.where(first_head, stats[0][0], stats[1][0])
                lse_h = jnp.where(first_head, stats[0][1], stats[1][1])
                dq = jnp.zeros((BLK, BLK), F32)
                dk, dv = [], []
                for half in range(2):
                    kh, vh = keys[half]
                    sc = lax.dot_general(q, kh, _DIMS["nt"], preferred_element_type=F32) + b_ref[half]
                    pr = jnp.exp(sc - lse_h)
                    if b == 0 and half == 0:
                        pr = pr * (i > 0).astype(F32)
                    dp = lax.dot_general(dyb, vh, _DIMS["nt"], preferred_element_type=F32)
                    ds = pr * (dp - delta)
                    db_ref[half] += ds
                    dsb = ds.astype(BF16)
                    dq = dq + jnp.dot(dsb, kh, preferred_element_type=F32)
                    dk2 = lax.dot_general(dsb, q, _DIMS["tn"], preferred_element_type=F32)
                    dv2 = lax.dot_general(pr.astype(BF16), dyb, _DIMS["tn"], preferred_element_type=F32)
                    dk.append(jnp.where(masks[0], dk2[:BLK], dk2[BLK:]))
                    dv.append(jnp.where(masks[0], dv2[:BLK], dv2[BLK:]))
                dq = dq * SCALE
                if not first:
                    dq = dq + dqp_ref[rows, :]
                dq_ref[rows, :] = dq
                if b > 0:
                    flush(prev_rows, carry[0] + dk[0], carry[1] + dv[0])
                else:
                    dkx_ref[prev_rows, :] = dk[0]
                    dvx_ref[prev_rows, :] = dv[0]
                carry = (dk[1], dv[1])
            flush(_rows(r + p * (nb - 1), dil), *carry)

    cur, prv = _zcur(w), _zprev(p, nb)
    b_spec = pl.BlockSpec((None, 2, BLK, 2 * BLK), lambda hp, i: (hp, 0, 0, 0))
    in_specs = [cur(0), prv(4), cur(4), prv(8), cur(8), b_spec] + [_scur(w)] * 3
    args = [z, z, z, z, z, bias, dy, y, lse]
    if not first:
        in_specs += [_scur(w)] * 3
        args += list(prev)
    x_spec = pl.BlockSpec((p, BLK), lambda hp, i: (i, hp))
    *outs, db = pl.pallas_call(
        body, name=name, grid=(4, n_steps), in_specs=in_specs,
        out_specs=[_scur(w)] * 3 + [x_spec] * 2 + [b_spec],
        out_shape=[_sds((s, ATTN_W))] * 3 + [_sds((n_steps * p, ATTN_W))] * 2 + [_sds((4, 2, BLK, 2 * BLK))],
        compiler_params=_params(("parallel", "arbitrary")),
    )(*args)
    return (*outs, _unpair_bias_bwd(db))


ASM_ROWS = 512


def _assemble_dz(dq, dk, dv, extras, dzs, du, name):
    s = dq.shape[0]
    w = min(ATT_ROWS, s)
    n_steps = s // w
    per_step = w // ASM_ROWS
    assert w % ASM_ROWS == 0

    def body(*refs):
        dq_ref, dk_ref, dv_ref, dzs_ref, du_ref = refs[:5]
        x_refs = refs[5:5 + 2 * len(extras)]
        o_ref, acc_ref = refs[-2:]
        j = pl.program_id(0)
        step = j // per_step
        has_next = (step < n_steps - 1).astype(F32)
        last_of_step = ((j + 1) % per_step == 0).astype(F32)
        o_ref[:, 0:ATTN_W] = dq_ref[...].astype(BF16)
        o_ref[:, 3 * ATTN_W:3 * ATTN_W + 2 * SGU_W] = dzs_ref[...].astype(BF16)
        o_ref[:, 3 * ATTN_W + 2 * SGU_W:IN_W] = du_ref[...].astype(BF16)
        for part, (base_ref, col) in enumerate(((dk_ref, ATTN_W), (dv_ref, 2 * ATTN_W))):
            acc_ref[...] = base_ref[...]
            for n, (_, dil) in enumerate(BRANCHES):
                rows = min(BLK * dil, ASM_ROWS)
                scale = has_next if BLK * dil >= w else has_next * last_of_step
                acc_ref[ASM_ROWS - rows:, :] += x_refs[2 * n + part][...] * scale
            o_ref[:, col:col + ATTN_W] = acc_ref[...].astype(BF16)

    def x_spec(dil):
        p = BLK * dil
        rows = min(p, ASM_ROWS)
        blocks_per_step = p // rows
        total = n_steps * blocks_per_step

        def idx(j):
            step = j // per_step
            within = (j % per_step) - (per_step - blocks_per_step)
            return (jnp.clip((step + 1) * blocks_per_step + jnp.maximum(within, 0), 0, total - 1), 0)

        return pl.BlockSpec((rows, ATTN_W), idx)

    in_specs = [_rb(ASM_ROWS, ATTN_W)] * 3 + [_rb(ASM_ROWS, 2 * SGU_W), _rb(ASM_ROWS, SSM_W)]
    args = [dq, dk, dv, dzs, du]
    for (dkx, dvx), (_, dil) in zip(extras, BRANCHES):
        in_specs += [x_spec(dil)] * 2
        args += [dkx, dvx]
    return pl.pallas_call(
        body, name=name, grid=(s // ASM_ROWS,), in_specs=in_specs, out_specs=_rb(ASM_ROWS, IN_W),
        out_shape=_sds((s, IN_W), BF16), scratch_shapes=[pltpu.VMEM((ASM_ROWS, ATTN_W), F32)],
        compiler_params=_params(("parallel",)),
    )(*args)


def _t5_bucket(dist):
    max_exact = N_BUCKETS // 2
    d = np.maximum(dist, 0)
    large = max_exact + (np.log(np.maximum(d, 1) / max_exact) / np.log(REL_MAX / max_exact)
                         * (N_BUCKETS - max_exact)).astype(np.int32)
    large = np.minimum(large, N_BUCKETS - 1)
    return np.where(d < max_exact, d, large).astype(np.int32)


def _bias_tables(rel_bias):
    period = 3 * BLK
    tabs = []
    for _, dil in BRANCHES:
        onehot = np.zeros((period, N_BUCKETS), np.float32)
        d = np.arange(BLK + 1)
        onehot[d, _t5_bucket((BLK - d) * dil)] = 1.0
        f = jnp.dot(jnp.asarray(onehot), rel_bias, precision=lax.Precision.HIGHEST)
        flat = jnp.tile(f.T, (1, BLK))[:, :BLK * (period - 1)]
        tab = flat.reshape(N_HEADS, BLK, period - 1)[:, :, :2 * BLK]
        tabs.append(jnp.where(_band()[None], tab, NEG_INF))
    return tabs


def _bucket_onehot():
    maps = []
    q = np.arange(BLK)[:, None]
    k = np.arange(2 * BLK)[None, :]
    rel = q + BLK - k
    for _, dil in BRANCHES:
        maps.append(np.where((rel >= 0) & (rel <= BLK), _t5_bucket(rel * dil), -1).reshape(-1))
    bmap = jnp.asarray(np.concatenate(maps).astype(np.int32))
    return (bmap[:, None] == jnp.arange(128, dtype=jnp.int32)[None, :]).astype(BF16)


def _block_diag(t):
    g, n, c = t.shape
    eye = jnp.eye(g, dtype=t.dtype)
    return (t[:, :, None, :] * eye[:, None, :, None]).reshape(g * n, g * c)


def _ssm_prep(a_re, a_im, log_dt, b_re, b_im, c_re, c_im):
    lam = lax.complex(a_re, a_im)
    dt = jnp.exp(log_dt)[:, None]
    a_bar = jnp.exp(lam * dt)
    b_bar = ((a_bar - 1.0) / lam)[:, :, None] * lax.complex(b_re, b_im)
    bdt = jnp.concatenate([_block_diag(jnp.real(b_bar)), _block_diag(jnp.imag(b_bar))], axis=0)
    cd = jnp.concatenate([_block_diag(jnp.transpose(c_re, (0, 2, 1))),
                          _block_diag(-jnp.transpose(c_im, (0, 2, 1)))], axis=0)
    return jnp.real(a_bar).reshape(-1), jnp.imag(a_bar).reshape(-1), bdt, cd


def _powers(ar, ai):
    pr, pi = ar[:, None], ai[:, None]
    k = 1
    while k < 8:
        lr, li = pr[:, -1:], pi[:, -1:]
        pr, pi = (jnp.concatenate([pr, pr * lr - pi * li], axis=1),
                  jnp.concatenate([pi, pr * li + pi * lr], axis=1))
        k *= 2
    return pr, pi


def _sgu_bias_expand(b):
    return jnp.repeat(b.T, 64, axis=1)


def _layer_fwd(i, h, a1, p_i, big, small, bias_tabs, next_gain):
    nm = "l%d_" % i
    sv = {"h": h}
    if a1 is None:
        a1 = _rms_fwd(h, small["norm_attn_g"][i], nm + "rms_attn")
    z = _mm(a1, big["w_in"], "nt", nm + "in_proj")
    y_attn, lse = _attn_fwd(z, [t[0] for t in bias_tabs], nm + "attn_fwd")
    bexp = _sgu_bias_expand(small["sgu_b"][i])
    y_sgu = _sgu_fwd(z, small["sgu_ln_g"][i], small["sgu_ln_b"][i], small["sgu_w"][i], bexp, nm + "sgu_fwd")
    ar, ai, bdt, cd = _ssm_prep(*[small[k][i] for k in ("ssm_a_re", "ssm_a_im", "ssm_log_dt", "ssm_b_re",
                                                         "ssm_b_im", "ssm_c_re", "ssm_c_im")])
    xr, xi, yc = _ssm_fwd(z, bdt.astype(BF16), cd.astype(BF16), _scan_tables(*_powers(ar, ai), False),
                          nm + "ssm_core")
    y_ssm = _ssm_post_fwd(yc, z, small["ssm_d"][i], big["ssm_glu_w"], small["ssm_glu_b"][i], nm + "ssm_post")
    mix = _mix_fwd(y_attn, y_sgu, y_ssm, small["branch_norm_g"][i], nm + "mix")
    if "rest" in big:
        big = dict({k: t for k, t in big.items() if k != "rest"}, **big["rest"](mix))
    h2, a2 = _mm(mix, big["w_out"], "nn", nm + "out_proj", add=h, norm_gain=small["norm_ffn_g"][i])
    hu = _mm(a2, big["ffn_w_up"], "nt", nm + "ffn_up", out_dtype=BF16)
    hv, hg, act = _conv_fwd(hu, big["ffn_conv_w"], small["ffn_conv_b"][i], nm + "ffn_conv")
    h3, a3 = _mm(act, big["ffn_w_down"], "nn", nm + "ffn_down", add=h2, norm_gain=small["norm_ple_g"][i])
    gp = _mm(a3, big["ple_w_gate"], "nn", nm + "ple_gate", out_dtype=BF16)
    pp = _mm(p_i, big["ple_w_proj"], "nt", nm + "ple_proj", out_dtype=BF16)
    h4, a_next = _ple_fwd(h3, gp, pp, next_gain, nm + "ple_add")
    sv.update(big=big, a1=a1, z=z, y_attn=y_attn, lse=lse, y_sgu=y_sgu, y_ssm=y_ssm, yc=yc, xr=xr, xi=xi, mix=mix, h2=h2,
              a2=a2, hu=hu, hv=hv, hg=hg, act=act, h3=h3, a3=a3, gp=gp, pp=pp)
    return h4, a_next, sv


def _layer_bwd(i, dh4, sv, p_i, big, small, bias_tabs, ffn_done=None):
    nm = "l%d_" % i
    g = {}
    dpp, dgp = _ple_bwd(dh4, sv["gp"], sv["pp"], nm + "ple_bwd")
    g["ple_w_proj"] = _mm(dpp, p_i, "tn", nm + "d_ple_proj", out_dtype=BF16)
    g["ple_w_gate"] = _mm(sv["a3"], dgp, "tn", nm + "d_ple_gate", out_dtype=BF16)
    dh3, dgain = _mm(dgp, big["ple_w_gate"], "nt", nm + "ple_gate_t", add=dh4,
                     norm_bwd=(sv["h3"], small["norm_ple_g"][i]))
    g["norm_ple_g"] = dgain.reshape(D_MODEL)
    g["ffn_w_down"] = _mm(sv["act"], dh3, "tn", nm + "d_ffn_down", out_dtype=BF16)
    dact = _mm(dh3, big["ffn_w_down"], "nt", nm + "ffn_down_t", out_dtype=BF16)
    dhu, g["ffn_conv_w"], dcb = _conv_bwd(dact, sv["hv"], sv["hg"], sv["hu"], big["ffn_conv_w"],
                                          nm + "ffn_conv_bwd")
    g["ffn_conv_b"] = dcb.reshape(2 * D_FF)
    g["ffn_w_up"] = _mm(dhu, sv["a2"], "tn", nm + "d_ffn_up", out_dtype=BF16)
    dh2, dgain = _mm(dhu, big["ffn_w_up"], "nn", nm + "ffn_up_t", add=dh3,
                     norm_bwd=(sv["h2"], small["norm_ffn_g"][i]))
    g["norm_ffn_g"] = dgain.reshape(D_MODEL)
    if ffn_done is not None:
        small = ffn_done(g, small)
    g["w_out"] = _mm(sv["mix"], dh2, "tn", nm + "d_out_proj", out_dtype=BF16)
    dmix = _mm(dh2, big["w_out"], "nt", nm + "out_proj_t")
    dya, dysg, dyss, g["branch_norm_g"] = _mix_bwd(dmix, sv["y_attn"], sv["y_sgu"], sv["y_ssm"],
                                                   small["branch_norm_g"][i], nm + "mix_bwd")
    ssm_keys = ("ssm_a_re", "ssm_a_im", "ssm_log_dt", "ssm_b_re", "ssm_b_im", "ssm_c_re", "ssm_c_im")
    (ar, ai, bdt, cd), prep_vjp = jax.vjp(_ssm_prep, *[small[k][i] for k in ssm_keys])
    dy1, dgl, y2, dud, g["ssm_d"], g["ssm_glu_b"] = _ssm_post_bwd(
        dyss, sv["yc"], sv["z"], small["ssm_d"][i], big["ssm_glu_w"], small["ssm_glu_b"][i], nm + "ssm_post_bwd")
    g["ssm_glu_w"] = _mm(y2, dgl, "tn", nm + "d_ssm_glu", out_dtype=BF16)
    du, dbdt, dcd, dar, dai = _ssm_bwd(dy1, dud, sv["z"], sv["xr"], sv["xi"], bdt.astype(BF16), cd.astype(BF16),
                                       _scan_tables(*_powers(ar, ai), True), nm + "ssm_core_bwd")
    for k, val in zip(ssm_keys, prep_vjp((dar, dai, dbdt, dcd))):
        g[k] = val
    bexp, bexp_vjp = jax.vjp(_sgu_bias_expand, small["sgu_b"][i])
    dzs, g["sgu_w"], dbexp, g["sgu_ln_g"], g["sgu_ln_b"] = _sgu_bwd(
        sv["z"], dysg, small["sgu_ln_g"][i], small["sgu_ln_b"][i], small["sgu_w"][i], bexp, nm + "sgu_bwd")
    g["sgu_b"] = bexp_vjp(dbexp)[0]
    prev = None
    dbs, extras = [], []
    for b, (_, dil) in enumerate(BRANCHES):
        dq, dk, dv, dkx, dvx, db = _attn_bwd(sv["z"], bias_tabs[b][1], dya, sv["y_attn"], sv["lse"], prev, dil,
                                             nm + "attn_bwd%d" % b)
        prev = (dq, dk, dv)
        extras.append((dkx, dvx))
        dbs.append(db.reshape(N_HEADS, BLK * 2 * BLK))
    dz = _assemble_dz(dq, dk, dv, extras, dzs, du, nm + "assemble_dz")
    g["w_in"] = _mm(dz, sv["a1"], "tn", nm + "d_in_proj", out_dtype=BF16)
    dh, dgain = _mm(dz, big["w_in"], "nn", nm + "in_proj_t", add=dh2, norm_bwd=(sv["h"], small["norm_attn_g"][i]))
    g["norm_attn_g"] = dgain.reshape(D_MODEL)
    return dh, g, jnp.concatenate(dbs, axis=1)


def _local_step(x, p, target, layer_weights, small, layer_done=None):
    depth = p.shape[0]
    bias_tabs = [(t, _pair_bias_bwd(t)) for t in _bias_tables(small["rel_bias"])]
    h, a1 = x, None
    saved = []
    for i in range(depth):
        next_gain = small["norm_attn_g"][i + 1] if i + 1 < depth else None
        h, a1, sv = _layer_fwd(i, h, a1, p[i], layer_weights(i, h), small, bias_tabs, next_gain)
        saved.append(sv)
    dh, loss, g_final = _loss_head(h, target, small["final_norm_g"], "loss_head")
    layer_grads = [None] * depth
    dbias = [None] * depth
    for i in reversed(range(depth)):
        ffn_done = None if layer_done is None else (lambda g, sm, i=i: layer_done(i, "ffn", g, sm))
        dh, layer_grads[i], dbias[i] = _layer_bwd(i, dh, saved[i], p[i], saved[i]["big"], small, bias_tabs,
                                                  ffn_done)
        if layer_done is not None:
            small = layer_done(i, "all", layer_grads[i], small)
    big_grads = [{k: lg.pop(k) for k in COMM_NAMES} for lg in layer_grads]
    grads = {k: jnp.stack([layer_grads[i][k] for i in range(depth)]) for k in layer_grads[0]}
    grads["final_norm_g"] = g_final
    g_rb = _mm(sum(dbias[1:], dbias[0]), _bucket_onehot(), "nn", "d_rel_bias", tk=2048)
    grads["rel_bias"] = g_rb[:, :N_BUCKETS].T
    return loss, dh, big_grads, grads


_ANY = pl.BlockSpec(memory_space=pl.ANY)
MESH_IDS = pl.DeviceIdType.MESH


def _slot(ref, axis, j):
    return ref.at[(slice(None),) * axis + (j,)]


def _all_gather(blocks, axis, name):
    nt = len(blocks)

    def body(*refs):
        x_refs, o_refs = refs[:nt], refs[nt:2 * nt]
        send_sems, recv_sems, local_sems = refs[2 * nt:]
        x, y, c = lax.axis_index("x"), lax.axis_index("y"), lax.axis_index("c")
        me, sibling = (x, y, c), (x, y, 1 - c)
        chips = [(1 - x, y), (x, 1 - y), (1 - x, 1 - y)]

        def slot(t, px, py, pc):
            return _slot(o_refs[t], axis, 4 * px + 2 * py + pc)

        def copy(t, k, blk, to, src=None):
            return pltpu.make_async_remote_copy(
                src_ref=slot(t, *blk) if src is None else src, dst_ref=slot(t, *blk),
                send_sem=send_sems.at[7 * t + k], recv_sem=recv_sems.at[7 * t + k],
                device_id=to, device_id_type=MESH_IDS)

        mine = [pltpu.make_async_copy(x_refs[t], slot(t, *me), local_sems.at[t]) for t in range(nt)]
        for cp in mine:
            cp.start()
        first = []
        for t in range(nt):
            first.append(copy(t, 0, me, sibling, src=x_refs[t]))
            first += [copy(t, 1 + j, me, (*chip, c), src=x_refs[t]) for j, chip in enumerate(chips)]
        for cp in first:
            cp.start()
        passed = []
        for t in range(nt):
            for j, chip in enumerate(chips):
                copy(t, 1 + j, (*chip, c), me).wait_recv()
                passed.append(copy(t, 4 + j, (*chip, c), sibling))
                passed[-1].start()
        for t in range(nt):
            copy(t, 0, sibling, me).wait_recv()
            for j, chip in enumerate(chips):
                copy(t, 4 + j, (*chip, 1 - c), me).wait_recv()
        for cp in first + passed:
            cp.wait_send()
        for cp in mine:
            cp.wait()

    out_shape = [jax.ShapeDtypeStruct(b.shape[:axis] + (N_DEV,) + b.shape[axis:], b.dtype) for b in blocks]
    return pl.pallas_call(
        body, name=name, out_shape=out_shape, in_specs=[_ANY] * nt, out_specs=[_ANY] * nt,
        scratch_shapes=[pltpu.SemaphoreType.DMA((7 * nt,)), pltpu.SemaphoreType.DMA((7 * nt,)),
                        pltpu.SemaphoreType.DMA((nt,))],
    )(*blocks)


def _peer(k):
    x, y, c = lax.axis_index("x"), lax.axis_index("y"), lax.axis_index("c")
    px = 1 - x if k & 4 else x
    py = 1 - y if k & 2 else y
    pc = 1 - c if k & 1 else c
    return (px, py, pc), 4 * px + 2 * py + pc


def _all_to_all(blocks, name):
    nt = len(blocks)

    def body(*refs):
        x_refs, o_refs = refs[:nt], refs[nt:2 * nt]
        send_sems, recv_sems, local_sems = refs[2 * nt:]
        _, me = _peer(0)
        mine = [pltpu.make_async_copy(x_refs[t].at[me], o_refs[t].at[me], local_sems.at[t]) for t in range(nt)]
        for cp in mine:
            cp.start()
        copies = []
        for k in range(1, N_DEV):
            peer, idx = _peer(k)
            for t in range(nt):
                cp = pltpu.make_async_remote_copy(
                    src_ref=x_refs[t].at[idx], dst_ref=o_refs[t].at[me],
                    send_sem=send_sems.at[7 * t + k - 1], recv_sem=recv_sems.at[7 * t + k - 1],
                    device_id=peer, device_id_type=MESH_IDS)
                cp.start()
                copies.append(cp)
        for cp in copies:
            cp.wait()
        for cp in mine:
            cp.wait()

    return pl.pallas_call(
        body, name=name, out_shape=[jax.ShapeDtypeStruct(b.shape, b.dtype) for b in blocks],
        in_specs=[_ANY] * nt, out_specs=[_ANY] * nt,
        scratch_shapes=[pltpu.SemaphoreType.DMA((7 * nt,)), pltpu.SemaphoreType.DMA((7 * nt,)),
                        pltpu.SemaphoreType.DMA((nt,))],
    )(*blocks)


_HBM = pl.BlockSpec(memory_space=pltpu.HBM)
_SEM = pl.BlockSpec(memory_space=pltpu.SEMAPHORE)
_EFFECT = pltpu.SideEffectType.DATAFLOW_SIDE_EFFECTING


def _split_copy(src_ref, land_ref, send_sems, recv_sems, t, k, gather):
    peer, idx = _peer(k)
    _, me = _peer(0)
    return pltpu.make_async_remote_copy(
        src_ref=src_ref if gather else src_ref.at[idx], dst_ref=land_ref.at[me],
        send_sem=send_sems.at[7 * t + k - 1], recv_sem=recv_sems.at[7 * t + k - 1],
        device_id=peer, device_id_type=MESH_IDS)


def _exchange_start(srcs, lands, gather, name):
    nt = len(srcs)

    def body(*refs):
        src_refs, land_refs = refs[:nt], refs[nt:2 * nt]
        send_sems, recv_sems = refs[2 * nt:2 * nt + 2]
        token = refs[-1]
        for k in range(1, N_DEV):
            for t in range(nt):
                _split_copy(src_refs[t], land_refs[t], send_sems, recv_sems, t, k, gather).start()
        token[...] = jnp.zeros_like(token)

    hbm = lambda a: pltpu.HBM(a.shape, a.dtype)
    outs = pl.pallas_call(
        body, name=name,
        out_shape=(pltpu.SemaphoreType.DMA((7 * nt,)), pltpu.SemaphoreType.DMA((7 * nt,)),
                   *[hbm(a) for a in srcs], *[hbm(a) for a in lands], jax.ShapeDtypeStruct((8, 128), F32)),
        in_specs=[_HBM] * (2 * nt),
        out_specs=(_SEM, _SEM, *[_HBM] * (2 * nt), pl.BlockSpec(memory_space=pltpu.VMEM)),
        input_output_aliases={j: 2 + j for j in range(2 * nt)},
        compiler_params=pltpu.CompilerParams(has_side_effects=_EFFECT),
    )(*[pltpu.with_memory_space_constraint(a, pltpu.HBM) for a in list(srcs) + list(lands)])
    return outs[0], outs[1], outs[2:2 + nt], outs[2 + nt:2 + 2 * nt], outs[-1]


def _exchange_wait(send_sems, recv_sems, srcs, lands, after, gather, name):
    nt = len(srcs)

    def body(*refs):
        src_refs, land_refs = refs[:nt], refs[nt:2 * nt]
        send_sems, recv_sems = refs[2 * nt:2 * nt + 2]
        for k in range(1, N_DEV):
            _, idx = _peer(k)
            for t in range(nt):
                _split_copy(src_refs[t], land_refs[t], send_sems, recv_sems, t, k, gather).wait_send()
                arrival = pltpu.make_async_remote_copy(
                    src_ref=land_refs[t].at[idx], dst_ref=land_refs[t].at[idx],
                    send_sem=send_sems.at[7 * t + k - 1], recv_sem=recv_sems.at[7 * t + k - 1],
                    device_id=_peer(k)[0], device_id_type=MESH_IDS)
                arrival.wait_recv()

    hbm = lambda a: pltpu.HBM(a.shape, a.dtype)
    outs = pl.pallas_call(
        body, name=name, out_shape=tuple(hbm(a) for a in list(srcs) + list(lands)),
        in_specs=[_HBM] * (2 * nt) + [_SEM, _SEM, _ANY], out_specs=tuple([_HBM] * (2 * nt)),
        input_output_aliases={j: j for j in range(2 * nt)},
        compiler_params=pltpu.CompilerParams(has_side_effects=_EFFECT),
    )(*srcs, *lands, send_sems, recv_sems, after)
    return outs[nt:]


def _adamw(parts, w, m, v, name, tr):
    n_layers, r, c_ = w.shape
    assert len(parts) == n_layers

    def body(*refs):
        p_refs = refs[:n_layers]
        w_ref, m_ref, v_ref, g_ref, d_ref, mo_ref, vo_ref = refs[n_layers:]

        def update(p_ref):
            g = p_ref[0].astype(F32)
            for j in range(1, N_DEV):
                g = g + p_ref[j].astype(F32)
            m2 = ADAM_B1 * m_ref[...] + (1.0 - ADAM_B1) * g
            v2 = ADAM_B2 * v_ref[...] + (1.0 - ADAM_B2) * (g * g)
            m_hat = m2 / (1.0 - ADAM_B1 ** ADAM_STEP)
            v_hat = v2 / (1.0 - ADAM_B2 ** ADAM_STEP)
            g_ref[...] = g
            d_ref[...] = -ADAM_LR * (m_hat / (jnp.sqrt(v_hat) + ADAM_EPS) + ADAM_WD * w_ref[...])
            mo_ref[...] = m2
            vo_ref[...] = v2

        for layer in range(n_layers):
            pl.when(pl.program_id(0) == layer)(lambda layer=layer: update(p_refs[layer]))

    spec = pl.BlockSpec((None, tr, c_), lambda l, i: (l, i, 0))
    p_spec = pl.BlockSpec((N_DEV, tr, c_), lambda l, i: (0, i, 0))
    return pl.pallas_call(
        body, name=name, grid=(n_layers, r // tr), in_specs=[p_spec] * n_layers + [spec] * 3,
        out_specs=[spec] * 4, out_shape=[_sds((n_layers, r, c_))] * 4,
        compiler_params=_params(("parallel", "parallel")),
    )(*parts, w, m, v)


def _pack_rows(n_elems, align):
    rows = -(-n_elems // PACK_COLS)
    return -(-rows // align) * align


def _pack(arrs, rows, dtype=F32):
    flat = jnp.concatenate([a.reshape(-1) for a in arrs]).astype(dtype)
    return jnp.pad(flat, (0, rows * PACK_COLS - flat.shape[0])).reshape(rows, PACK_COLS)


def _unpack(pack, shapes):
    flat = pack.reshape(-1)
    out, off = [], 0
    for shp in shapes:
        size = int(np.prod(shp))
        out.append(flat[off:off + size].reshape(shp))
        off += size
    return out


def _tile_rows(rows, target, align=16):
    best = align
    for t in range(align, target + 1, align):
        if rows % t == 0:
            best = t
    return best


COMM_NAMES = ("w_in", "ssm_glu_w", "w_out", "ffn_w_up", "ffn_w_down", "ple_w_gate", "ple_w_proj")
COMM_TRANSPOSED = ("w_in", "ffn_w_up", "ple_w_proj")
COMM_EARLY = ("ple_w_proj", "ple_w_gate", "ffn_w_down", "ffn_w_up")
COMM_LATE = ("w_in", "ssm_glu_w", "w_out")
SMALL_TILE_ROWS = 64
CONV_NAME = "ffn_conv_w"


def _to_comm(name, a):
    return jnp.swapaxes(a, 1, 2) if name in COMM_TRANSPOSED else a


def kernel(x, p, rel_bias, norm_attn_g, w_in, sgu_ln_g, sgu_ln_b, sgu_w, sgu_b, ssm_a_re, ssm_a_im, ssm_log_dt, ssm_b_re, ssm_b_im, ssm_c_re, ssm_c_im, ssm_d, ssm_glu_w, ssm_glu_b, branch_norm_g, w_out, norm_ffn_g, ffn_w_up, ffn_conv_w, ffn_conv_b, ffn_w_down, norm_ple_g, ple_w_gate, ple_w_proj, final_norm_g, loss_target, m_rel_bias, m_norm_attn_g, m_w_in, m_sgu_ln_g, m_sgu_ln_b, m_sgu_w, m_sgu_b, m_ssm_a_re, m_ssm_a_im, m_ssm_log_dt, m_ssm_b_re, m_ssm_b_im, m_ssm_c_re, m_ssm_c_im, m_ssm_d, m_ssm_glu_w, m_ssm_glu_b, m_branch_norm_g, m_w_out, m_norm_ffn_g, m_ffn_w_up, m_ffn_conv_w, m_ffn_conv_b, m_ffn_w_down, m_norm_ple_g, m_ple_w_gate, m_ple_w_proj, m_final_norm_g, v_rel_bias, v_norm_attn_g, v_w_in, v_sgu_ln_g, v_sgu_ln_b, v_sgu_w, v_sgu_b, v_ssm_a_re, v_ssm_a_im, v_ssm_log_dt, v_ssm_b_re, v_ssm_b_im, v_ssm_c_re, v_ssm_c_im, v_ssm_d, v_ssm_glu_w, v_ssm_glu_b, v_branch_norm_g, v_w_out, v_norm_ffn_g, v_ffn_w_up, v_ffn_conv_w, v_ffn_conv_b, v_ffn_w_down, v_norm_ple_g, v_ple_w_gate, v_ple_w_proj, v_final_norm_g):
    given = dict(locals())
    w = {n: given[n] for n in WEIGHT_NAMES}
    m = {n: given["m_" + n] for n in WEIGHT_NAMES}
    v = {n: given["v_" + n] for n in WEIGHT_NAMES}
    depth = p.shape[0]
    dev = 4 * lax.axis_index("x") + 2 * lax.axis_index("y") + lax.axis_index("c")

    wc = {n: _to_comm(n, w[n]) for n in COMM_NAMES}
    wb = {n: wc[n].astype(BF16) for n in COMM_NAMES}
    conv_local = [w[CONV_NAME], m[CONV_NAME], v[CONV_NAME]]
    conv_rows = _pack_rows(sum(int(np.prod(t.shape)) for t in conv_local), 8)
    conv_g, = _all_gather([_pack(conv_local, conv_rows)], 0, "gather_conv_taps")
    conv_parts = zip(*[_unpack(conv_g[j], [t.shape for t in conv_local]) for j in range(N_DEV)])
    conv_w, conv_m, conv_v = [jnp.concatenate(parts, axis=2) for parts in conv_parts]
    small = {n: w[n] for n in SMALL_NAMES}

    def whole(names, blocks):
        return {n: t.reshape(-1, t.shape[-1]) for n, t in zip(names, blocks)}

    def own_slot(block):
        return lax.dynamic_update_slice_in_dim(jnp.zeros((N_DEV,) + block.shape, block.dtype), block[None], dev, 0)

    def start_gather(names, i, after):
        srcs, after = lax.optimization_barrier(([wb[n][i] for n in names], after))
        return _exchange_start(srcs, [own_slot(s) for s in srcs], True, "gather_weights_%d_start" % i), after

    def wait_gather(names, i, started, after):
        send_sems, recv_sems, srcs, lands, _ = started
        return whole(names, _exchange_wait(send_sems, recv_sems, srcs, lands, after, True,
                                           "gather_weights_%d_wait" % i))

    at_once = ("w_in", "ssm_glu_w")
    later = tuple(n for n in COMM_NAMES if n not in at_once)
    w_in_0 = _all_gather([wb[n][0] for n in at_once], 0, "gather_w_in_0")
    gathering = {}
    gathering[0], (w_in_0, _) = start_gather(later, 0, (w_in_0, conv_g))
    small["norm_attn_g"] = small["norm_attn_g"] + gathering[0][4][0, 0]

    def layer_weights(i, h):
        if i > 0:
            got = wait_gather(COMM_NAMES, i, gathering.pop(i), h)
            if i + 1 < depth:
                gathering[i + 1], ordered = start_gather(COMM_NAMES, i + 1, got["w_in"])
                got["w_in"] = ordered + gathering[i + 1][4][0, 0].astype(BF16)
            return dict(got, **{CONV_NAME: conv_w[i]})

        def rest(after):
            got = wait_gather(later, 0, gathering.pop(0), after)
            if depth > 1:
                gathering[1], ordered = start_gather(COMM_NAMES, 1, got["w_out"])
                got["w_out"] = ordered + gathering[1][4][0, 0].astype(BF16)
            return got

        return dict(whole(at_once, w_in_0), **{CONV_NAME: conv_w[0], "rest": rest})

    def as_slots(g, n):
        return g.reshape((N_DEV,) + wc[n].shape[1:])

    scattering = {}

    def layer_done(i, stage, g, small_now):
        if stage == "all" and i == 0:
            return small_now
        names = COMM_EARLY if stage == "ffn" else COMM_LATE
        srcs = [as_slots(g[n], n) for n in names]
        lands = [own_slot(lax.dynamic_index_in_dim(s, dev, 0, keepdims=False)) for s in srcs]
        started = _exchange_start(srcs, lands, False, "scatter_weight_grads_%d_%s_start" % (i, stage))
        scattering[i, stage] = (names, started)
        pin = "branch_norm_g" if stage == "ffn" else "norm_ple_g"
        return dict(small_now, **{pin: small_now[pin] + started[4][0, 0]})

    loss, dx, big_grads, grads = _local_step(x[0], p[:, 0], loss_target[0], layer_weights, small, layer_done)
    loss = lax.psum(loss, ("x", "y", "c"))

    recv = [{} for _ in range(depth)]
    for (i, stage), (names, (send_sems, recv_sems, srcs, lands, _)) in scattering.items():
        got = _exchange_wait(send_sems, recv_sems, srcs, lands, dx, False,
                             "scatter_weight_grads_%d_%s_wait" % (i, stage))
        recv[i].update(zip(names, got))
    srcs = [as_slots(big_grads[0][n], n) for n in COMM_LATE]
    lands = [own_slot(lax.dynamic_index_in_dim(s, dev, 0, keepdims=False)) for s in srcs]
    last = _exchange_start(srcs, lands, False, "scatter_weight_grads_0_all_start")
    out = {}

    def update(n, pin=None):
        weight = wc[n] if pin is None else wc[n] + pin
        res = _adamw([recv[i][n] for i in range(depth)], weight, _to_comm(n, m[n]), _to_comm(n, v[n]),
                     "adamw_" + n, _tile_rows(wc[n].shape[1], 256))
        out[n] = [_to_comm(n, r) for r in res]

    for j, n in enumerate(COMM_EARLY):
        update(n, last[4][0, 0] if j == 0 else None)
    got = _exchange_wait(last[0], last[1], last[2], last[3], out[COMM_EARLY[-1]][0], False,
                         "scatter_weight_grads_0_all_wait")
    recv[0].update(zip(COMM_LATE, got))

    rep_names = SMALL_NAMES + (CONV_NAME,)
    rep_w = dict({n: w[n] for n in SMALL_NAMES}, **{CONV_NAME: conv_w})
    rep_m = dict({n: m[n] for n in SMALL_NAMES}, **{CONV_NAME: conv_m})
    rep_v = dict({n: v[n] for n in SMALL_NAMES}, **{CONV_NAME: conv_v})
    rep_shapes = [rep_w[n].shape for n in rep_names]
    rep_rows = _pack_rows(sum(int(np.prod(s)) for s in rep_shapes), SMALL_TILE_ROWS)
    rep_parts, = _all_gather([_pack([grads[n] for n in rep_names], rep_rows)], 0, "gather_small_grads")
    for n in COMM_LATE:
        update(n)
    rep_out = _adamw([rep_parts], *[_pack([src[n] for n in rep_names], rep_rows)[None] for src in (rep_w, rep_m, rep_v)],
                     "adamw_replicated", SMALL_TILE_ROWS)
    for n, vals in zip(rep_names, zip(*[_unpack(r[0], rep_shapes) for r in rep_out])):
        out[n] = list(vals)
    shard = ffn_conv_w.shape[2]
    out[CONV_NAME] = [lax.dynamic_slice_in_dim(t, dev * shard, shard, axis=2) for t in out[CONV_NAME]]
    results = [[out[n][kind] for n in WEIGHT_NAMES] for kind in range(4)]
    return (loss, dx[None], *results[0], *results[1], *results[2], *results[3])
```

```python
import math

import numpy as np
import jax
import jax.numpy as jnp
from jax import lax
from jax.experimental import pallas as pl
from jax.experimental.pallas import tpu as pltpu

F32 = jnp.float32
BF16 = jnp.bfloat16

D_MODEL = 1024
HEAD_DIM = 64
N_HEADS = 8
ATTN_W = 512
SGU_W = 256
SGU_GROUPS = 4
SGU_CHUNK = 128
SSM_W = 256
SSM_GROUPS = 16
SSM_CH = 16
SSM_STATE = 64
SSM_NS = SSM_GROUPS * SSM_STATE
IN_W = 2304
D_FF = 2816
PLE_DIM = 256
BRANCHES = ((128, 1), (512, 4), (2048, 16))
BLK = 128
N_BUCKETS = 32
REL_MAX = 2048
EPS = 1e-6
NEG_INF = -1e30
N_DEV = 8

ADAM_LR = 0.001
ADAM_B1 = 0.9
ADAM_B2 = 0.999
ADAM_EPS = 1e-08
ADAM_WD = 0.01
ADAM_STEP = 10

VMEM_LIMIT_BYTES = 56 * 1024 * 1024
GELU_C = math.sqrt(2.0 / math.pi)

SMALL_NAMES = ("rel_bias", "norm_attn_g", "sgu_ln_g", "sgu_ln_b", "sgu_w", "sgu_b", "ssm_a_re", "ssm_a_im",
               "ssm_log_dt", "ssm_b_re", "ssm_b_im", "ssm_c_re", "ssm_c_im", "ssm_d", "ssm_glu_b",
               "branch_norm_g", "norm_ffn_g", "ffn_conv_b", "norm_ple_g", "final_norm_g")
WEIGHT_NAMES = ("rel_bias", "norm_attn_g", "w_in", "sgu_ln_g", "sgu_ln_b", "sgu_w", "sgu_b", "ssm_a_re",
                "ssm_a_im", "ssm_log_dt", "ssm_b_re", "ssm_b_im", "ssm_c_re", "ssm_c_im", "ssm_d", "ssm_glu_w",
                "ssm_glu_b", "branch_norm_g", "w_out", "norm_ffn_g", "ffn_w_up", "ffn_conv_w", "ffn_conv_b",
                "ffn_w_down", "norm_ple_g", "ple_w_gate", "ple_w_proj", "final_norm_g")
PACK_COLS = 512


def _params(sem):
    return pltpu.CompilerParams(dimension_semantics=sem, vmem_limit_bytes=VMEM_LIMIT_BYTES)


def _pick(dim, target):
    if dim <= target:
        return dim
    best = None
    for t in range(128, target + 1, 128):
        if dim % t == 0:
            best = t
    return dim if best is None else best


def _gelu(x):
    return 0.5 * x * (1.0 + jnp.tanh(GELU_C * (x + 0.044715 * (x * x * x))))


def _gelu_grad(x):
    t = jnp.tanh(GELU_C * (x + 0.044715 * (x * x * x)))
    return 0.5 * (1.0 + t) + 0.5 * x * (1.0 - t * t) * (GELU_C * (1.0 + 3.0 * 0.044715 * (x * x)))


def _sigmoid(x):
    return 1.0 / (1.0 + jnp.exp(-x))


_DIMS = {"nn": (((1,), (0,)), ((), ())), "tn": (((0,), (0,)), ((), ())), "nt": (((1,), (1,)), ((), ()))}


def _mm(a, b, mode, name, add=None, out_dtype=F32, norm_gain=None, norm_bwd=None, tm=1408, tn=1408, tk=1408):
    if mode == "nn":
        m, k = a.shape
        k2, n = b.shape
    elif mode == "tn":
        k, m = a.shape
        k2, n = b.shape
    else:
        m, k = a.shape
        n, k2 = b.shape
    assert k == k2, (name, a.shape, b.shape, mode)
    tm, tn, tk = _pick(m, tm), _pick(n, tn), _pick(k, tk)
    nk = k // tk
    dims = _DIMS[mode]
    has_add = add is not None
    has_norm = norm_gain is not None
    has_nbwd = norm_bwd is not None
    assert not (has_norm or has_nbwd) or tn == n

    def body(*refs):
        a_ref, b_ref = refs[:2]
        rest = list(refs[2:])
        add_ref = rest.pop(0) if has_add else None
        g_ref = rest.pop(0) if has_norm else None
        h_ref, hg_ref = (rest.pop(0), rest.pop(0)) if has_nbwd else (None, None)
        o_ref = rest.pop(0)
        n_ref = rest.pop(0) if has_norm else None
        dg_ref = rest.pop(0) if has_nbwd else None
        part = lax.dot_general(a_ref[...].astype(BF16), b_ref[...].astype(BF16), dims,
                               preferred_element_type=F32)
        if has_nbwd:
            @pl.when((pl.program_id(0) == 0) & (pl.program_id(2) == 0))
            def _():
                dg_ref[...] = jnp.zeros_like(dg_ref)

        def finish(r):
            if has_nbwd:
                x = h_ref[...]
                scale = lax.rsqrt(jnp.mean(x * x, axis=-1, keepdims=True) + EPS)
                xh = x * scale
                dg_ref[...] += jnp.sum(r * xh, axis=0, keepdims=True)
                dxh = r * hg_ref[...]
                r = scale * (dxh - xh * jnp.mean(dxh * xh, axis=-1, keepdims=True))
            if has_add:
                r = r + add_ref[...]
            o_ref[...] = r.astype(out_dtype)
            if has_norm:
                scale = lax.rsqrt(jnp.mean(r * r, axis=-1, keepdims=True) + EPS)
                n_ref[...] = (r * scale * g_ref[...]).astype(BF16)

        if nk == 1:
            finish(part)
            return
        acc_ref = refs[-1]
        kk = pl.program_id(2)

        @pl.when(kk == 0)
        def _():
            acc_ref[...] = part

        @pl.when((kk > 0) & (kk < nk - 1))
        def _():
            acc_ref[...] += part

        @pl.when(kk == nk - 1)
        def _():
            finish(acc_ref[...] + part)

    if mode == "tn":
        a_spec = pl.BlockSpec((tk, tm), lambda i, j, kk: (kk, i))
    else:
        a_spec = pl.BlockSpec((tm, tk), lambda i, j, kk: (i, kk))
    if mode == "nt":
        b_spec = pl.BlockSpec((tn, tk), lambda i, j, kk: (j, kk))
    else:
        b_spec = pl.BlockSpec((tk, tn), lambda i, j, kk: (kk, j))
    o_spec = pl.BlockSpec((tm, tn), lambda i, j, kk: (i, j))
    in_specs = [a_spec, b_spec] + ([o_spec] if has_add else [])
    args = (a, b) + ((add,) if has_add else ())
    out_specs, out_shape = o_spec, jax.ShapeDtypeStruct((m, n), out_dtype)
    if has_norm:
        in_specs.append(pl.BlockSpec((1, n), lambda i, j, kk: (0, 0)))
        args += (norm_gain.reshape(1, n),)
        out_specs, out_shape = [o_spec, o_spec], [out_shape, jax.ShapeDtypeStruct((m, n), BF16)]
    if has_nbwd:
        row_spec = pl.BlockSpec((1, n), lambda i, j, kk: (0, 0))
        in_specs += [o_spec, row_spec]
        args += (norm_bwd[0], norm_bwd[1].reshape(1, n))
        out_specs, out_shape = [o_spec, row_spec], [out_shape, jax.ShapeDtypeStruct((1, n), F32)]
    sem = ("arbitrary",) * 3 if has_nbwd else ("parallel", "parallel", "arbitrary")
    return pl.pallas_call(
        body, name=name, grid=(m // tm, n // tn, nk),
        in_specs=in_specs, out_specs=out_specs, out_shape=out_shape,
        scratch_shapes=[pltpu.VMEM((tm, tn), F32)] if nk > 1 else [], compiler_params=_params(sem),
    )(*args)


def _rb(tm, w, cb=0):
    return pl.BlockSpec((tm, w), lambda i: (i, cb))


def _fb(shape):
    nd = len(shape)
    return pl.BlockSpec(shape, lambda i: (0,) * nd)


def _rowcall(body, name, n_rows, tm, in_specs, args, out_specs, out_shapes):
    return pl.pallas_call(
        body, name=name, grid=(n_rows // tm,), in_specs=in_specs, out_specs=out_specs, out_shape=out_shapes,
        compiler_params=_params(("arbitrary",)),
    )(*args)


def _sds(shape, dtype=F32):
    return jax.ShapeDtypeStruct(shape, dtype)


def _rms_fwd(h, g, name, tm=512):
    s, d = h.shape

    def body(h_ref, g_ref, o_ref):
        x = h_ref[...]
        r = lax.rsqrt(jnp.mean(x * x, axis=-1, keepdims=True) + EPS)
        o_ref[...] = (x * r * g_ref[...]).astype(BF16)

    return _rowcall(body, name, s, tm, [_rb(tm, d), _fb((1, d))], (h, g.reshape(1, d)), _rb(tm, d),
                    _sds((s, d), BF16))


def _loss_head(h, target, g, name, tm=512):
    s, d = h.shape

    def body(h_ref, t_ref, g_ref, dh_ref, loss_ref, dg_ref):
        @pl.when(pl.program_id(0) == 0)
        def _():
            dg_ref[...] = jnp.zeros_like(dg_ref)
            loss_ref[...] = jnp.zeros_like(loss_ref)

        x = h_ref[...]
        r = lax.rsqrt(jnp.mean(x * x, axis=-1, keepdims=True) + EPS)
        xh = x * r
        gg = g_ref[...]
        err = xh * gg - t_ref[...]
        loss_ref[...] += jnp.sum(err * err) * (0.5 / d)
        dy = err * (1.0 / d)
        dg_ref[...] += jnp.sum(dy * xh, axis=0, keepdims=True)
        dxh = dy * gg
        dh_ref[...] = r * (dxh - xh * jnp.mean(dxh * xh, axis=-1, keepdims=True))

    dh, loss, dg = _rowcall(body, name, s, tm, [_rb(tm, d), _rb(tm, d), _fb((1, d))], (h, target, g.reshape(1, d)),
                            [_rb(tm, d), _fb((1, 128)), _fb((1, d))], [_sds((s, d)), _sds((1, 128)), _sds((1, d))])
    return dh, loss[0, 0], dg.reshape(d)


_MIX_PARTS = ((0, 512), (512, 768), (768, 1024))


def _mix_fwd(ya, ysg, yss, g, name, tm=512):
    s = ya.shape[0]

    def body(a_ref, b_ref, c_ref, g_ref, o_ref):
        for ref, (lo, hi) in zip((a_ref, b_ref, c_ref), _MIX_PARTS):
            y = ref[...]
            r = lax.rsqrt(jnp.mean(y * y, axis=-1, keepdims=True) + EPS)
            o_ref[:, lo:hi] = (y * r * g_ref[:, lo:hi]).astype(BF16)

    return _rowcall(body, name, s, tm, [_rb(tm, 512), _rb(tm, 256), _rb(tm, 256), _fb((1, 1024))],
                    (ya, ysg, yss, g.reshape(1, 1024)), _rb(tm, 1024), _sds((s, 1024), BF16))


def _mix_bwd(dmix, ya, ysg, yss, g, name, tm=512):
    s = ya.shape[0]

    def body(dm_ref, a_ref, b_ref, c_ref, g_ref, da_ref, db_ref, dc_ref, dg_ref):
        @pl.when(pl.program_id(0) == 0)
        def _():
            dg_ref[...] = jnp.zeros_like(dg_ref)

        for ref, dref, (lo, hi) in zip((a_ref, b_ref, c_ref), (da_ref, db_ref, dc_ref), _MIX_PARTS):
            y = ref[...]
            r = lax.rsqrt(jnp.mean(y * y, axis=-1, keepdims=True) + EPS)
            xh = y * r
            dm = dm_ref[:, lo:hi]
            dg_ref[:, lo:hi] += jnp.sum(dm * xh, axis=0, keepdims=True)
            dxh = dm * g_ref[:, lo:hi]
            dref[...] = r * (dxh - xh * jnp.mean(dxh * xh, axis=-1, keepdims=True))

    da, db, dc, dg = _rowcall(
        body, name, s, tm, [_rb(tm, 1024), _rb(tm, 512), _rb(tm, 256), _rb(tm, 256), _fb((1, 1024))],
        (dmix, ya, ysg, yss, g.reshape(1, 1024)),
        [_rb(tm, 512), _rb(tm, 256), _rb(tm, 256), _fb((1, 1024))],
        [_sds((s, 512)), _sds((s, 256)), _sds((s, 256)), _sds((1, 1024))])
    return da, db, dc, dg.reshape(1024)


def _ssm_post_fwd(yc, z, d, gw, gb, name, tm=1024):
    s = yc.shape[0]

    def body(yc_ref, u_ref, d_ref, gw_ref, gb_ref, o_ref):
        y1 = yc_ref[...] + d_ref[...] * u_ref[...]
        y2 = _gelu(y1)
        gl = jnp.dot(y2.astype(BF16), gw_ref[...], preferred_element_type=F32) + gb_ref[...]
        o_ref[...] = y2 * _sigmoid(gl)

    return _rowcall(body, name, s, tm, [_rb(tm, 256), _rb(tm, 256, 8), _fb((1, 256)), _fb((256, 256)), _fb((1, 256))],
                    (yc, z, d.reshape(1, 256), gw, gb.reshape(1, 256)), _rb(tm, 256), _sds((s, 256)))


def _ssm_post_bwd(dy, yc, z, d, gw, gb, name, tm=1024):
    s = yc.shape[0]

    def body(dy_ref, yc_ref, u_ref, d_ref, gw_ref, gb_ref, dy1_ref, dgl_ref, y2_ref, dud_ref, dd_ref, dgb_ref):
        @pl.when(pl.program_id(0) == 0)
        def _():
            dd_ref[...] = jnp.zeros_like(dd_ref)
            dgb_ref[...] = jnp.zeros_like(dgb_ref)

        u = u_ref[...]
        dd = d_ref[...]
        y1 = yc_ref[...] + dd * u
        y2 = _gelu(y1)
        gw_v = gw_ref[...]
        gl = jnp.dot(y2.astype(BF16), gw_v, preferred_element_type=F32) + gb_ref[...]
        sg = _sigmoid(gl)
        dyv = dy_ref[...]
        dgl = dyv * y2 * sg * (1.0 - sg)
        dy2 = dyv * sg + lax.dot_general(dgl.astype(BF16), gw_v, _DIMS["nt"], preferred_element_type=F32)
        dy1 = dy2 * _gelu_grad(y1)
        dy1_ref[...] = dy1.astype(BF16)
        dgl_ref[...] = dgl.astype(BF16)
        y2_ref[...] = y2.astype(BF16)
        dud_ref[...] = dy1 * dd
        dd_ref[...] += jnp.sum(dy1 * u, axis=0, keepdims=True)
        dgb_ref[...] += jnp.sum(dgl, axis=0, keepdims=True)

    outs = _rowcall(
        body, name, s, tm,
        [_rb(tm, 256), _rb(tm, 256), _rb(tm, 256, 8), _fb((1, 256)), _fb((256, 256)), _fb((1, 256))],
        (dy, yc, z, d.reshape(1, 256), gw, gb.reshape(1, 256)),
        [_rb(tm, 256)] * 4 + [_fb((1, 256))] * 2,
        [_sds((s, 256), BF16)] * 3 + [_sds((s, 256))] + [_sds((1, 256))] * 2)
    dy1, dgl, y2, dud, dd, dgb = outs
    return dy1, dgl, y2, dud, dd.reshape(256), dgb.reshape(256)


SCAN_T = 512
N_SCAN_TABLES = 6


def _scan_tables(pr, pi, reverse):
    ns = pr.shape[0]
    sign = -1.0 if reverse else 1.0
    power = [(jnp.ones((ns,), F32), jnp.zeros((ns,), F32))] + [(pr[:, k], sign * pi[:, k]) for k in range(8)]
    zero = (jnp.zeros((ns,), F32), jnp.zeros((ns,), F32))

    def table(exponents):
        rows = [zero if e is None else power[e] for e in exponents]
        return jnp.stack([jnp.concatenate(row) for row in rows])

    tabs = []
    for k in (1, 2, 4):
        has_partner = [(s < 8 - k) if reverse else (s >= k) for s in range(8)]
        tabs.append(table([k if ok else None for ok in has_partner]))
    tabs.append(table([s if reverse else 7 - s for s in range(8)]))
    tabs.append(table([8 - s if reverse else s + 1 for s in range(8)]))
    tabs.append(table([8] * 8))
    return jnp.stack(tabs)


def _cmul(ar, ai, br, bi):
    return ar * br - ai * bi, ar * bi + ai * br


def _scan_group(ur, ui, cr, ci, tr_ref, ti_ref, reverse):
    xr, xi = ur, ui
    for n, k in enumerate((1, 2, 4)):
        shift = 8 - k if reverse else k
        pr, pi = _cmul(tr_ref[n], ti_ref[n], pltpu.roll(xr, shift, axis=0), pltpu.roll(xi, shift, axis=0))
        xr, xi = xr + pr, xi + pi
    sr, si = _cmul(tr_ref[3], ti_ref[3], ur, ui)
    for k in (1, 2, 4):
        sr, si = sr + pltpu.roll(sr, k, axis=0), si + pltpu.roll(si, k, axis=0)
    pr, pi = _cmul(tr_ref[4], ti_ref[4], cr, ci)
    nr, ni = _cmul(tr_ref[5], ti_ref[5], cr, ci)
    return xr + pr, xi + pi, nr + sr, ni + si


def _table_halves(t_ref):
    return t_ref.at[:, :, pl.ds(0, SSM_NS)], t_ref.at[:, :, pl.ds(SSM_NS, SSM_NS)]


_U_BLOCK = (IN_W - SSM_W) // SSM_W


def _ssm_fwd(z, bdt, cd, tabs, name):
    s = z.shape[0]
    ns = SSM_NS
    n_t = s // SCAN_T

    def body(u_ref, b_ref, c_ref, t_ref, xr_ref, xi_ref, y_ref, cr_ref, ci_ref, ur_ref, ui_ref):
        @pl.when(pl.program_id(0) == 0)
        def _():
            cr_ref[...] = jnp.zeros_like(cr_ref)
            ci_ref[...] = jnp.zeros_like(ci_ref)

        bu = lax.dot_general(u_ref[...].astype(BF16), b_ref[...], _DIMS["nt"], preferred_element_type=F32)
        ur_ref[...] = bu[:, :ns]
        ui_ref[...] = bu[:, ns:]
        tr_ref, ti_ref = _table_halves(t_ref)

        def group(g, carry):
            rows = pl.ds(pl.multiple_of(g * 8, 8), 8)
            xr, xi, cr, ci = _scan_group(ur_ref[rows, :], ui_ref[rows, :], *carry, tr_ref, ti_ref, False)
            xr_ref[rows, :] = xr
            xi_ref[rows, :] = xi
            return cr, ci

        cr, ci = lax.fori_loop(0, SCAN_T // 8, group, (cr_ref[...], ci_ref[...]), unroll=2)
        cr_ref[...] = cr
        ci_ref[...] = ci
        y_ref[...] = (jnp.dot(xr_ref[...].astype(BF16), c_ref[0:ns, :], preferred_element_type=F32)
                      + jnp.dot(xi_ref[...].astype(BF16), c_ref[ns:, :], preferred_element_type=F32))

    x_spec = pl.BlockSpec((SCAN_T, ns), lambda t: (t, 0))
    return pl.pallas_call(
        body, name=name, grid=(n_t,),
        in_specs=[pl.BlockSpec((SCAN_T, SSM_W), lambda t: (t, _U_BLOCK)), _fb((2 * ns, SSM_W)),
                  _fb((2 * ns, SSM_W)), _fb((N_SCAN_TABLES, 8, 2 * ns))],
        out_specs=[x_spec, x_spec, _rb(SCAN_T, SSM_W)],
        out_shape=[_sds((s, ns)), _sds((s, ns)), _sds((s, SSM_W))],
        scratch_shapes=[pltpu.VMEM((8, ns), F32)] * 2 + [pltpu.VMEM((SCAN_T, ns), F32)] * 2,
        compiler_params=_params(("arbitrary",)),
    )(z, bdt, cd, tabs)


def _ssm_bwd(dy1, dud, z, xr, xi, bdt, cd, tabs, name):
    s = z.shape[0]
    ns = SSM_NS
    n_t = s // SCAN_T
    n_groups = SCAN_T // 8

    def body(dy_ref, dud_ref, u_ref, xr_ref, xi_ref, pxr_ref, pxi_ref, b_ref, c_ref, t_ref,
             du_ref, dbd_ref, dcd_ref, dar_ref, dai_ref,
             cr_ref, ci_ref, ar_ref, ai_ref, sxr_ref, sxi_ref, gr_ref, gi_ref, lr_ref, li_ref, bacc_ref, cacc_ref):
        t = pl.program_id(0)

        @pl.when(t == 0)
        def _():
            for ref in (cr_ref, ci_ref, ar_ref, ai_ref, bacc_ref, cacc_ref):
                ref[...] = jnp.zeros_like(ref)

        dyb = dy_ref[...]
        g = lax.dot_general(dyb, c_ref[...], _DIMS["nt"], preferred_element_type=F32)
        gr_ref[...] = g[:, :ns]
        gi_ref[...] = g[:, ns:]
        has_before = (t < n_t - 1).astype(F32)
        sxr_ref[0:8, :] = pxr_ref[...] * has_before
        sxi_ref[0:8, :] = pxi_ref[...] * has_before
        sxr_ref[8:, :] = xr_ref[...]
        sxi_ref[8:, :] = xi_ref[...]
        first_row = lax.broadcasted_iota(jnp.int32, (8, ns), 0) == 0
        tr_ref, ti_ref = _table_halves(t_ref)

        def group(k, carry):
            cr, ci, ar, ai = carry
            g8 = pl.multiple_of((n_groups - 1 - k) * 8, 8)
            rows = pl.ds(g8, 8)
            lr, li, cr, ci = _scan_group(gr_ref[rows, :], gi_ref[rows, :], cr, ci, tr_ref, ti_ref, True)
            lr_ref[rows, :] = lr
            li_ref[rows, :] = li
            here, before = pl.ds(g8 + 8, 8), rows
            pr = jnp.where(first_row, pltpu.roll(sxr_ref[before, :], 1, axis=0), pltpu.roll(sxr_ref[here, :], 1, axis=0))
            pi = jnp.where(first_row, pltpu.roll(sxi_ref[before, :], 1, axis=0), pltpu.roll(sxi_ref[here, :], 1, axis=0))
            return cr, ci, ar + lr * pr + li * pi, ai + li * pr - lr * pi

        cr, ci, ar, ai = lax.fori_loop(0, n_groups, group,
                                       (cr_ref[...], ci_ref[...], ar_ref[...], ai_ref[...]), unroll=2)
        cr_ref[...] = cr
        ci_ref[...] = ci
        ar_ref[...] = ar
        ai_ref[...] = ai
        lrb = lr_ref[...].astype(BF16)
        lib = li_ref[...].astype(BF16)
        ub = u_ref[...].astype(BF16)
        du_ref[...] = (dud_ref[...] + jnp.dot(lrb, b_ref[0:ns, :], preferred_element_type=F32)
                       + jnp.dot(lib, b_ref[ns:, :], preferred_element_type=F32))
        bacc_ref[0:ns, :] += lax.dot_general(lrb, ub, _DIMS["tn"], preferred_element_type=F32)
        bacc_ref[ns:, :] += lax.dot_general(lib, ub, _DIMS["tn"], preferred_element_type=F32)
        cacc_ref[0:ns, :] += lax.dot_general(xr_ref[...].astype(BF16), dyb, _DIMS["tn"], preferred_element_type=F32)
        cacc_ref[ns:, :] += lax.dot_general(xi_ref[...].astype(BF16), dyb, _DIMS["tn"], preferred_element_type=F32)

        @pl.when(t == n_t - 1)
        def _():
            for k in (1, 2, 4):
                ar_ref[...] += pltpu.roll(ar_ref[...], k, axis=0)
                ai_ref[...] += pltpu.roll(ai_ref[...], k, axis=0)
            dar_ref[...] = ar_ref[...]
            dai_ref[...] = ai_ref[...]
            dbd_ref[...] = bacc_ref[...]
            dcd_ref[...] = cacc_ref[...]

    rev = lambda t: n_t - 1 - t
    row_spec = pl.BlockSpec((SCAN_T, SSM_W), lambda t: (rev(t), 0))
    x_spec = pl.BlockSpec((SCAN_T, ns), lambda t: (rev(t), 0))
    before_spec = pl.BlockSpec((8, ns), lambda t: (jnp.maximum(rev(t) * (SCAN_T // 8) - 1, 0), 0))
    du, dbd, dcd, dar, dai = pl.pallas_call(
        body, name=name, grid=(n_t,),
        in_specs=[row_spec, row_spec, pl.BlockSpec((SCAN_T, SSM_W), lambda t: (rev(t), _U_BLOCK)),
                  x_spec, x_spec, before_spec, before_spec,
                  _fb((2 * ns, SSM_W)), _fb((2 * ns, SSM_W)), _fb((N_SCAN_TABLES, 8, 2 * ns))],
        out_specs=[row_spec, _fb((2 * ns, SSM_W)), _fb((2 * ns, SSM_W)), _fb((8, ns)), _fb((8, ns))],
        out_shape=[_sds((s, SSM_W)), _sds((2 * ns, SSM_W)), _sds((2 * ns, SSM_W)), _sds((8, ns)), _sds((8, ns))],
        scratch_shapes=([pltpu.VMEM((8, ns), F32)] * 4 + [pltpu.VMEM((SCAN_T + 8, ns), F32)] * 2
                        + [pltpu.VMEM((SCAN_T, ns), F32)] * 4 + [pltpu.VMEM((2 * ns, SSM_W), F32)] * 2),
        compiler_params=_params(("arbitrary",)),
    )(dy1, dud, z, xr, xi, xr, xi, bdt, cd, tabs)
    return du, dbd, dcd, dar[0], dai[0]


def _group_ids():
    return lax.broadcasted_iota(jnp.int32, (1, SGU_W), 1) // 64


def _group_mean(val, gid):
    out = jnp.zeros_like(val)
    for g in range(SGU_GROUPS):
        mg = gid == g
        out = jnp.where(mg, jnp.sum(jnp.where(mg, val, 0.0), axis=1, keepdims=True) * (1.0 / 64), out)
    return out


def _causal_w(w_ref, g):
    t = lax.broadcasted_iota(jnp.int32, (SGU_CHUNK, SGU_CHUNK), 0)
    s = lax.broadcasted_iota(jnp.int32, (SGU_CHUNK, SGU_CHUNK), 1)
    return jnp.where(t >= s, w_ref[g], 0.0).astype(BF16)


def _sgu_core(x, lng, lnb, w_ref, bexp, gid):
    zz = _gelu(x)
    u = zz[:, :SGU_W]
    v = zz[:, SGU_W:]
    vc = v - _group_mean(v, gid)
    rstd = lax.rsqrt(_group_mean(vc * vc, gid) + EPS)
    vhat = vc * rstd
    vn = vhat * lng + lnb
    vnb = vn.astype(BF16)
    mixed = bexp
    for g in range(SGU_GROUPS):
        mm = jnp.dot(_causal_w(w_ref, g), vnb, preferred_element_type=F32)
        mixed = jnp.where(gid == g, mm + bexp, mixed)
    return u, rstd, vhat, vnb, mixed


def _sgu_fwd(z, lng, lnb, w, bexp, name, tm=512):
    s = z.shape[0]

    def body(z_ref, lng_ref, lnb_ref, w_ref, b_ref, o_ref):
        gid = _group_ids()
        for j in range(tm // SGU_CHUNK):
            rows = pl.ds(j * SGU_CHUNK, SGU_CHUNK)
            u, _, _, _, mixed = _sgu_core(z_ref[rows, :], lng_ref[...], lnb_ref[...], w_ref, b_ref[...], gid)
            o_ref[rows, :] = u * mixed

    return _rowcall(body, name, s, tm,
                    [_rb(tm, 512, 3), _fb((1, 256)), _fb((1, 256)), _fb((4, 128, 128)), _fb((128, 256))],
                    (z, lng.reshape(1, 256), lnb.reshape(1, 256), w, bexp), _rb(tm, 256), _sds((s, 256)))


def _sgu_bwd(z, dy, lng, lnb, w, bexp, name, tm=512):
    s = z.shape[0]

    def body(z_ref, dy_ref, lng_ref, lnb_ref, w_ref, b_ref, dz_ref, dw_ref, db_ref, dlng_ref, dlnb_ref):
        @pl.when(pl.program_id(0) == 0)
        def _():
            dw_ref[...] = jnp.zeros_like(dw_ref)
            db_ref[...] = jnp.zeros_like(db_ref)
            dlng_ref[...] = jnp.zeros_like(dlng_ref)
            dlnb_ref[...] = jnp.zeros_like(dlnb_ref)

        gid = _group_ids()
        t = lax.broadcasted_iota(jnp.int32, (SGU_CHUNK, SGU_CHUNK), 0)
        sidx = lax.broadcasted_iota(jnp.int32, (SGU_CHUNK, SGU_CHUNK), 1)
        lng_v = lng_ref[...]
        for j in range(tm // SGU_CHUNK):
            rows = pl.ds(j * SGU_CHUNK, SGU_CHUNK)
            x = z_ref[rows, :]
            u, rstd, vhat, vnb, mixed = _sgu_core(x, lng_v, lnb_ref[...], w_ref, b_ref[...], gid)
            dyv = dy_ref[rows, :]
            dmixed = dyv * u
            du = dyv * mixed
            db_ref[...] += dmixed
            dvn = jnp.zeros_like(dmixed)
            for g in range(SGU_GROUPS):
                dmg = jnp.where(gid == g, dmixed, 0.0).astype(BF16)
                dvn = dvn + lax.dot_general(_causal_w(w_ref, g), dmg, _DIMS["tn"], preferred_element_type=F32)
                dwg = lax.dot_general(dmg, vnb, _DIMS["nt"], preferred_element_type=F32)
                dw_ref[g] += jnp.where(t >= sidx, dwg, 0.0)
            dlnb_ref[...] += jnp.sum(dvn, axis=0, keepdims=True)
            dlng_ref[...] += jnp.sum(dvn * vhat, axis=0, keepdims=True)
            dvh = dvn * lng_v
            dv = rstd * (dvh - _group_mean(dvh, gid) - vhat * _group_mean(dvh * vhat, gid))
            gg = _gelu_grad(x)
            dz_ref[rows, 0:SGU_W] = du * gg[:, :SGU_W]
            dz_ref[rows, SGU_W:2 * SGU_W] = dv * gg[:, SGU_W:]

    dz, dw, db, dlng, dlnb = _rowcall(
        body, name, s, tm,
        [_rb(tm, 512, 3), _rb(tm, 256), _fb((1, 256)), _fb((1, 256)), _fb((4, 128, 128)), _fb((128, 256))],
        (z, dy, lng.reshape(1, 256), lnb.reshape(1, 256), w, bexp),
        [_rb(tm, 512), _fb((4, 128, 128)), _fb((128, 256)), _fb((1, 256)), _fb((1, 256))],
        [_sds((s, 512)), _sds((4, 128, 128)), _sds((128, 256)), _sds((1, 256)), _sds((1, 256))])
    return dz, dw, db, dlng.reshape(256), dlnb.reshape(256)


CONV_TC = 1408
N_CT = D_FF // CONV_TC


def _row_of(block8, j):
    r = lax.broadcasted_iota(jnp.int32, block8.shape, 0)
    return jnp.sum(jnp.where(r == j, block8, 0.0), axis=0, keepdims=True)


EDGE = 16


def _conv_fwd(hu, cw, cb, name, tm=256):
    s = hu.shape[0]

    def body(xv_ref, xg_ref, tv_ref, tg_ref, wv_ref, wg_ref, bv_ref, bg_ref, hv_ref, hg_ref, act_ref):
        has_prev = (pl.program_id(1) > 0).astype(F32)
        row = lax.broadcasted_iota(jnp.int32, (EDGE, CONV_TC), 0)

        def conv(x_ref, t_ref, w_ref, b_ref):
            x = x_ref[...].astype(F32)
            w0, w1, w2, bb = w_ref[0:1, :], w_ref[1:2, :], w_ref[2:3, :], b_ref[...]
            whole = w0 * pltpu.roll(x, 2, axis=0) + w1 * pltpu.roll(x, 1, axis=0) + w2 * x + bb
            tail = t_ref[...].astype(F32)
            r7 = _row_of(tail, EDGE - 1) * has_prev
            r6 = _row_of(tail, EDGE - 2) * has_prev
            xe = x_ref[0:EDGE, :].astype(F32)
            x1 = jnp.where(row == 0, r7, pltpu.roll(xe, 1, axis=0))
            x2 = jnp.where(row == 0, r6, jnp.where(row == 1, r7, pltpu.roll(xe, 2, axis=0)))
            return whole, w0 * x2 + w1 * x1 + w2 * xe + bb

        hv, hv_edge = conv(xv_ref, tv_ref, wv_ref, bv_ref)
        hg, hg_edge = conv(xg_ref, tg_ref, wg_ref, bg_ref)
        hv_ref[...] = hv.astype(BF16)
        hg_ref[...] = hg.astype(BF16)
        act_ref[...] = (_gelu(hg) * hv).astype(BF16)
        hv_ref[0:EDGE, :] = hv_edge.astype(BF16)
        hg_ref[0:EDGE, :] = hg_edge.astype(BF16)
        act_ref[0:EDGE, :] = (_gelu(hg_edge) * hv_edge).astype(BF16)

    def xs(off):
        return pl.BlockSpec((tm, CONV_TC), lambda j, i: (i, j + off))

    def ts(off):
        return pl.BlockSpec((EDGE, CONV_TC), lambda j, i: (jnp.maximum(i * (tm // EDGE) - 1, 0), j + off))

    def ws(rows, off):
        return pl.BlockSpec((rows, CONV_TC), lambda j, i: (0, j + off))

    o_spec = pl.BlockSpec((tm, CONV_TC), lambda j, i: (i, j))
    return pl.pallas_call(
        body, name=name, grid=(N_CT, s // tm),
        in_specs=[xs(0), xs(N_CT), ts(0), ts(N_CT), ws(3, 0), ws(3, N_CT), ws(1, 0), ws(1, N_CT)],
        out_specs=[o_spec] * 3, out_shape=[_sds((s, D_FF), BF16)] * 3,
        compiler_params=_params(("parallel", "arbitrary")),
    )(hu, hu, hu, hu, cw, cw, cb.reshape(1, 2 * D_FF), cb.reshape(1, 2 * D_FF))


HALO = EDGE


def _conv_bwd(dact, hv, hg, hu, cw, name, tm=256):
    s = dact.shape[0]

    def body(da_ref, dan_ref, hv_ref, hvn_ref, hg_ref, hgn_ref, x_ref, t_ref, w_ref, dx_ref, dw_ref, db_ref, d_scr):
        i = pl.program_id(1)
        is_value = pl.program_id(0) < N_CT

        @pl.when(i == 0)
        def _():
            dw_ref[...] = jnp.zeros_like(dw_ref)
            db_ref[...] = jnp.zeros_like(db_ref)

        for rows, (a_ref, v_ref, g_ref) in ((pl.ds(0, tm), (da_ref, hv_ref, hg_ref)),
                                            (pl.ds(tm, HALO), (dan_ref, hvn_ref, hgn_ref))):
            @pl.when(is_value)
            def _():
                d_scr[rows, :] = a_ref[...].astype(F32) * _gelu(g_ref[...].astype(F32))

            @pl.when(jnp.logical_not(is_value))
            def _():
                d_scr[rows, :] = (a_ref[...].astype(F32) * v_ref[...].astype(F32)
                                  * _gelu_grad(g_ref[...].astype(F32)))

        has_prev = (i > 0).astype(F32)
        has_next = (i < s // tm - 1).astype(F32)
        w0, w1, w2 = w_ref[0:1, :], w_ref[1:2, :], w_ref[2:3, :]
        d = d_scr[0:tm, :]
        dx_ref[...] = (w2 * d + w1 * pltpu.roll(d, tm - 1, axis=0) + w0 * pltpu.roll(d, tm - 2, axis=0)).astype(BF16)
        row = lax.broadcasted_iota(jnp.int32, (EDGE, CONV_TC), 0)
        nxt = d_scr[tm:tm + HALO, :]
        n0 = _row_of(nxt, 0) * has_next
        n1 = _row_of(nxt, 1) * has_next
        de = d_scr[tm - EDGE:tm, :]
        d1 = jnp.where(row == EDGE - 1, n0, pltpu.roll(de, EDGE - 1, axis=0))
        d2 = jnp.where(row == EDGE - 2, n0, jnp.where(row == EDGE - 1, n1, pltpu.roll(de, EDGE - 2, axis=0)))
        dx_ref[tm - EDGE:tm, :] = (w2 * de + w1 * d1 + w0 * d2).astype(BF16)
        x = x_ref[...].astype(F32)
        tail = t_ref[...].astype(F32)
        r7 = _row_of(tail, EDGE - 1) * has_prev
        r6 = _row_of(tail, EDGE - 2) * has_prev
        last = x_ref[tm - EDGE:tm, :].astype(F32)
        l7, l6 = _row_of(last, EDGE - 1), _row_of(last, EDGE - 2)
        head = d_scr[0:8, :]
        d0, d1h = _row_of(head, 0), _row_of(head, 1)
        dw_ref[0:1, :] += (jnp.sum(d * pltpu.roll(x, 2, axis=0), axis=0, keepdims=True)
                           + d0 * (r6 - l6) + d1h * (r7 - l7))
        dw_ref[1:2, :] += jnp.sum(d * pltpu.roll(x, 1, axis=0), axis=0, keepdims=True) + d0 * (r7 - l7)
        dw_ref[2:3, :] += jnp.sum(d * x, axis=0, keepdims=True)
        db_ref[...] += jnp.sum(d, axis=0, keepdims=True)

    a_spec = pl.BlockSpec((tm, CONV_TC), lambda j, i: (i, j % N_CT))
    an_spec = pl.BlockSpec((HALO, CONV_TC),
                           lambda j, i: (jnp.minimum((i + 1) * (tm // HALO), s // HALO - 1), j % N_CT))
    x_spec = pl.BlockSpec((tm, CONV_TC), lambda j, i: (i, j))
    t_spec = pl.BlockSpec((EDGE, CONV_TC), lambda j, i: (jnp.maximum(i * (tm // EDGE) - 1, 0), j))
    w_spec = pl.BlockSpec((3, CONV_TC), lambda j, i: (0, j))
    db_spec = pl.BlockSpec((1, CONV_TC), lambda j, i: (0, j))
    return pl.pallas_call(
        body, name=name, grid=(2 * N_CT, s // tm),
        in_specs=[a_spec, an_spec, a_spec, an_spec, a_spec, an_spec, x_spec, t_spec, w_spec],
        out_specs=[x_spec, w_spec, db_spec],
        out_shape=[_sds((s, 2 * D_FF), BF16), _sds((3, 2 * D_FF)), _sds((1, 2 * D_FF))],
        scratch_shapes=[pltpu.VMEM((tm + HALO, CONV_TC), F32)],
        compiler_params=_params(("parallel", "arbitrary")),
    )(dact, dact, hv, hv, hg, hg, hu, hu, cw)


def _ple_fwd(h, gp, pp, next_gain, name, tm=512):
    s, d = h.shape
    with_norm = next_gain is not None

    def body(*refs):
        h_ref, g_ref, p_ref = refs[:3]
        out = h_ref[...] + _sigmoid(g_ref[...].astype(F32)) * p_ref[...].astype(F32)
        if with_norm:
            n_ref, o_ref, a_ref = refs[3:]
            scale = lax.rsqrt(jnp.mean(out * out, axis=-1, keepdims=True) + EPS)
            a_ref[...] = (out * scale * n_ref[...]).astype(BF16)
        else:
            o_ref, = refs[3:]
        o_ref[...] = out

    if not with_norm:
        return _rowcall(body, name, s, tm, [_rb(tm, d)] * 3, (h, gp, pp), _rb(tm, d), _sds((s, d))), None
    return _rowcall(body, name, s, tm, [_rb(tm, d)] * 3 + [_fb((1, d))], (h, gp, pp, next_gain.reshape(1, d)),
                    [_rb(tm, d)] * 2, [_sds((s, d)), _sds((s, d), BF16)])


def _ple_bwd(dh, gp, pp, name, tm=512):
    s, d = dh.shape

    def body(d_ref, g_ref, p_ref, dp_ref, dg_ref):
        sg = _sigmoid(g_ref[...].astype(F32))
        dv = d_ref[...]
        dp_ref[...] = (dv * sg).astype(BF16)
        dg_ref[...] = (dv * p_ref[...].astype(F32) * sg * (1.0 - sg)).astype(BF16)

    return _rowcall(body, name, s, tm, [_rb(tm, d)] * 3, (dh, gp, pp), [_rb(tm, d)] * 2,
                    [_sds((s, d), BF16)] * 2)


SCALE = HEAD_DIM ** -0.5
ATT_ROWS = 2048


def _att_geom(s, dil):
    w = min(ATT_ROWS, s)
    p = BLK * dil
    assert w % p == 0 and s % w == 0
    return w, p, w // p


def _rows(start, dil):
    return pl.ds(start, BLK, stride=dil) if dil > 1 else pl.ds(start, BLK)


def _head_masks():
    lane = lax.broadcasted_iota(jnp.int32, (1, BLK), 1)
    return [lane < HEAD_DIM, lane >= HEAD_DIM]


def _band():
    rel = np.arange(BLK)[:, None] + BLK - np.arange(2 * BLK)[None, :]
    return (rel >= 0) & (rel <= BLK)


def _zcur(w):
    return lambda off: pl.BlockSpec((w, BLK), lambda hp, i: (i, off + hp))


def _zprev(p, nb):
    return lambda off: pl.BlockSpec((p, BLK), lambda hp, i: (jnp.maximum(i * nb - 1, 0), off + hp))


def _scur(w):
    return pl.BlockSpec((w, BLK), lambda hp, i: (i, hp))


def _pair_rows(t, masks):
    return jnp.concatenate([jnp.where(masks[0], t, 0.0), jnp.where(masks[1], t, 0.0)], axis=0).astype(BF16)


def _pair_bias_bwd(bias):
    return bias.reshape(4, 2, BLK, 2, BLK).transpose(0, 3, 2, 1, 4).reshape(4, 2, BLK, 2 * BLK)


def _unpair_bias_bwd(db):
    return db.reshape(4, 2, BLK, 2, BLK).transpose(0, 3, 2, 1, 4).reshape(N_HEADS, BLK, 2 * BLK)


def _attn_fwd(z, biases, name):
    s = z.shape[0]
    w = min(ATT_ROWS, s)
    n_br = len(BRANCHES)

    def body(*refs):
        q_ref, kp_ref, kc_ref, vp_ref, vc_ref = refs[:5]
        b_refs = refs[5:5 + n_br]
        y_ref, lse_ref, m_ref, l_ref, a_ref = refs[5 + n_br:]
        i = pl.program_id(1)
        masks = _head_masks()
        own_block = lax.broadcasted_iota(jnp.int32, (1, 2 * BLK), 1) >= BLK
        for n, (_, dil) in enumerate(BRANCHES):
            _, p, nb = _att_geom(s, dil)
            for r in range(dil):
                for b in range(nb):
                    rows = _rows(r + p * b, dil)
                    prev_rows = _rows(r + p * (b - 1), dil) if b > 0 else _rows(w - p + r, dil)
                    kprev, vprev = (kc_ref, vc_ref) if b > 0 else (kp_ref, vp_ref)
                    q = q_ref[rows, :] * SCALE
                    k = jnp.concatenate([kprev[prev_rows, :], kc_ref[rows, :]], axis=0).astype(BF16)
                    v = jnp.concatenate([vprev[prev_rows, :], vc_ref[rows, :]], axis=0).astype(BF16)
                    mb = lb = ob = None
                    for hh, mh in enumerate(masks):
                        qh = jnp.where(mh, q, 0.0).astype(BF16)
                        sc = lax.dot_general(qh, k, _DIMS["nt"], preferred_element_type=F32) + b_refs[n][hh]
                        if b == 0:
                            sc = jnp.where(own_block | (i > 0), sc, NEG_INF)
                        mx = jnp.max(sc, axis=1, keepdims=True)
                        e = jnp.exp(sc - mx)
                        den = jnp.sum(e, axis=1, keepdims=True)
                        o = jnp.dot(e.astype(BF16), v, preferred_element_type=F32)
                        if hh == 0:
                            mb = jnp.broadcast_to(mx, (BLK, BLK))
                            lb = jnp.broadcast_to(den, (BLK, BLK))
                            ob = o
                        else:
                            mb = jnp.where(mh, mx, mb)
                            lb = jnp.where(mh, den, lb)
                            ob = jnp.where(mh, o, ob)
                    if n == 0:
                        m_new, l_new, a_new = mb, lb, ob
                    else:
                        m_old = m_ref[rows, :]
                        m_new = jnp.maximum(m_old, mb)
                        al = jnp.exp(m_old - m_new)
                        be = jnp.exp(mb - m_new)
                        l_new = al * l_ref[rows, :] + be * lb
                        a_new = al * a_ref[rows, :] + be * ob
                    if n == n_br - 1:
                        y_ref[rows, :] = a_new / l_new
                        lse_ref[rows, :] = m_new + jnp.log(l_new)
                    else:
                        m_ref[rows, :] = m_new
                        l_ref[rows, :] = l_new
                        a_ref[rows, :] = a_new

    cur, prv = _zcur(w), _zprev(w, 1)
    b_spec = pl.BlockSpec((2, BLK, 2 * BLK), lambda hp, i: (hp, 0, 0))
    return pl.pallas_call(
        body, name=name, grid=(4, s // w), in_specs=[cur(0), prv(4), cur(4), prv(8), cur(8)] + [b_spec] * n_br,
        out_specs=[_scur(w)] * 2, out_shape=[_sds((s, ATTN_W))] * 2,
        scratch_shapes=[pltpu.VMEM((w, BLK), F32)] * 3,
        compiler_params=_params(("parallel", "parallel")),
    )(z, z, z, z, z, *biases)


def _row_stats(mh, dy, y, lse):
    delta = jnp.sum(jnp.where(mh, dy * y, 0.0), axis=1, keepdims=True)
    lse_h = jnp.max(jnp.where(mh, lse, NEG_INF), axis=1, keepdims=True)
    return delta, lse_h


def _attn_bwd(z, biases, dy, y, lse, name):
    s = z.shape[0]
    w = min(ATT_ROWS, s)
    n_steps = s // w
    n_br = len(BRANCHES)

    def body(*refs):
        q_ref, kp_ref, kc_ref, vp_ref, vc_ref, dy_ref, y_ref, lse_ref = refs[:8]
        b_refs = refs[8:8 + n_br]
        outs = refs[8 + n_br:]
        dq_ref, dk_ref, dv_ref = outs[:3]
        x_refs = outs[3:3 + 2 * n_br]
        db_refs = outs[3 + 2 * n_br:3 + 3 * n_br]
        acc_refs = outs[3 + 3 * n_br:]
        i = pl.program_id(1)

        @pl.when(i == 0)
        def _():
            for ref in db_refs:
                ref[...] = jnp.zeros_like(ref)

        masks = _head_masks()
        first_head = lax.broadcasted_iota(jnp.int32, (1, 2 * BLK), 1) < BLK

        sums = {"q": (acc_refs[0], dq_ref), "k": (acc_refs[1], dk_ref), "v": (acc_refs[2], dv_ref)}

        def add_up(n, rows, **vals):
            for key, val in vals.items():
                acc_ref, out_ref = sums[key]
                if n > 0:
                    val = val + acc_ref[rows, :]
                if n == n_br - 1:
                    out_ref[rows, :] = val
                else:
                    acc_ref[rows, :] = val

        for n, (_, dil) in enumerate(BRANCHES):
            _, p, nb = _att_geom(s, dil)
            b_ref, db_ref = b_refs[n], db_refs[n]
            dkx_ref, dvx_ref = x_refs[2 * n], x_refs[2 * n + 1]
            for r in range(dil):
                carry = None
                for b in range(nb):
                    rows = _rows(r + p * b, dil)
                    prev_rows = _rows(r + p * (b - 1), dil) if b > 0 else _rows(w - p + r, dil)
                    kprev, vprev = (kc_ref, vc_ref) if b > 0 else (kp_ref, vp_ref)
                    keys = [(_pair_rows(kprev[prev_rows, :], masks), _pair_rows(vprev[prev_rows, :], masks)),
                            (_pair_rows(kc_ref[rows, :], masks), _pair_rows(vc_ref[rows, :], masks))]
                    q = (q_ref[rows, :] * SCALE).astype(BF16)
                    dy_v = dy_ref[rows, :]
                    dyb = dy_v.astype(BF16)
                    stats = [_row_stats(mh, dy_v, y_ref[rows, :], lse_ref[rows, :]) for mh in masks]
                    delta = jnp.where(first_head, stats[0][0], stats[1][0])
                    lse_h = jnp.where(first_head, stats[0][1], stats[1][1])
                    dq = jnp.zeros((BLK, BLK), F32)
                    dk, dv = [], []
                    for half in range(2):
                        kh, vh = keys[half]
                        sc = lax.dot_general(q, kh, _DIMS["nt"], preferred_element_type=F32) + b_ref[half]
                        pr = jnp.exp(sc - lse_h)
                        if b == 0 and half == 0:
                            pr = pr * (i > 0).astype(F32)
                        dp = lax.dot_general(dyb, vh, _DIMS["nt"], preferred_element_type=F32)
                        ds = pr * (dp - delta)
                        db_ref[half] += ds
                        dsb = ds.astype(BF16)
                        dq = dq + jnp.dot(dsb, kh, preferred_element_type=F32)
                        dk2 = lax.dot_general(dsb, q, _DIMS["tn"], preferred_element_type=F32)
                        dv2 = lax.dot_general(pr.astype(BF16), dyb, _DIMS["tn"], preferred_element_type=F32)
                        dk.append(jnp.where(masks[0], dk2[:BLK], dk2[BLK:]))
                        dv.append(jnp.where(masks[0], dv2[:BLK], dv2[BLK:]))
                    add_up(n, rows, q=dq * SCALE)
                    if b > 0:
                        add_up(n, _rows(r + p * (b - 1), dil), k=carry[0] + dk[0], v=carry[1] + dv[0])
                    else:
                        dkx_ref[_rows(r, dil), :] = dk[0]
                        dvx_ref[_rows(r, dil), :] = dv[0]
                    carry = (dk[1], dv[1])
                add_up(n, _rows(r + p * (nb - 1), dil), k=carry[0], v=carry[1])

    cur, prv = _zcur(w), _zprev(w, 1)
    b_spec = pl.BlockSpec((None, 2, BLK, 2 * BLK), lambda hp, i: (hp, 0, 0, 0))
    x_specs, x_shapes = [], []
    for _, dil in BRANCHES:
        p = BLK * dil
        x_specs += [pl.BlockSpec((p, BLK), lambda hp, i: (i, hp))] * 2
        x_shapes += [_sds((n_steps * p, ATTN_W))] * 2
    outs = pl.pallas_call(
        body, name=name, grid=(4, n_steps),
        in_specs=[cur(0), prv(4), cur(4), prv(8), cur(8)] + [_scur(w)] * 3 + [b_spec] * n_br,
        out_specs=[_scur(w)] * 3 + x_specs + [b_spec] * n_br,
        out_shape=[_sds((s, ATTN_W))] * 3 + x_shapes + [_sds((4, 2, BLK, 2 * BLK))] * n_br,
        scratch_shapes=[pltpu.VMEM((w, BLK), F32)] * 3,
        compiler_params=_params(("parallel", "arbitrary")),
    )(z, z, z, z, z, dy, y, lse, *biases)
    dq, dk, dv = outs[:3]
    extras = [(outs[3 + 2 * n], outs[4 + 2 * n]) for n in range(n_br)]
    return dq, dk, dv, extras, [_unpair_bias_bwd(db) for db in outs[3 + 2 * n_br:]]


ASM_ROWS = 512


def _assemble_dz(dq, dk, dv, extras, dzs, du, name):
    s = dq.shape[0]
    w = min(ATT_ROWS, s)
    n_steps = s // w
    per_step = w // ASM_ROWS
    assert w % ASM_ROWS == 0

    def body(*refs):
        dq_ref, dk_ref, dv_ref, dzs_ref, du_ref = refs[:5]
        x_refs = refs[5:5 + 2 * len(extras)]
        o_ref, acc_ref = refs[-2:]
        j = pl.program_id(0)
        step = j // per_step
        has_next = (step < n_steps - 1).astype(F32)
        last_of_step = ((j + 1) % per_step == 0).astype(F32)
        o_ref[:, 0:ATTN_W] = dq_ref[...].astype(BF16)
        o_ref[:, 3 * ATTN_W:3 * ATTN_W + 2 * SGU_W] = dzs_ref[...].astype(BF16)
        o_ref[:, 3 * ATTN_W + 2 * SGU_W:IN_W] = du_ref[...].astype(BF16)
        for part, (base_ref, col) in enumerate(((dk_ref, ATTN_W), (dv_ref, 2 * ATTN_W))):
            acc_ref[...] = base_ref[...]
            for n, (_, dil) in enumerate(BRANCHES):
                rows = min(BLK * dil, ASM_ROWS)
                scale = has_next if BLK * dil >= w else has_next * last_of_step
                acc_ref[ASM_ROWS - rows:, :] += x_refs[2 * n + part][...] * scale
            o_ref[:, col:col + ATTN_W] = acc_ref[...].astype(BF16)

    def x_spec(dil):
        p = BLK * dil
        rows = min(p, ASM_ROWS)
        blocks_per_step = p // rows
        total = n_steps * blocks_per_step

        def idx(j):
            step = j // per_step
            within = (j % per_step) - (per_step - blocks_per_step)
            return (jnp.clip((step + 1) * blocks_per_step + jnp.maximum(within, 0), 0, total - 1), 0)

        return pl.BlockSpec((rows, ATTN_W), idx)

    in_specs = [_rb(ASM_ROWS, ATTN_W)] * 3 + [_rb(ASM_ROWS, 2 * SGU_W), _rb(ASM_ROWS, SSM_W)]
    args = [dq, dk, dv, dzs, du]
    for (dkx, dvx), (_, dil) in zip(extras, BRANCHES):
        in_specs += [x_spec(dil)] * 2
        args += [dkx, dvx]
    return pl.pallas_call(
        body, name=name, grid=(s // ASM_ROWS,), in_specs=in_specs, out_specs=_rb(ASM_ROWS, IN_W),
        out_shape=_sds((s, IN_W), BF16), scratch_shapes=[pltpu.VMEM((ASM_ROWS, ATTN_W), F32)],
        compiler_params=_params(("parallel",)),
    )(*args)


def _t5_bucket(dist):
    max_exact = N_BUCKETS // 2
    d = np.maximum(dist, 0)
    large = max_exact + (np.log(np.maximum(d, 1) / max_exact) / np.log(REL_MAX / max_exact)
                         * (N_BUCKETS - max_exact)).astype(np.int32)
    large = np.minimum(large, N_BUCKETS - 1)
    return np.where(d < max_exact, d, large).astype(np.int32)


def _bias_tables(rel_bias):
    period = 3 * BLK
    tabs = []
    for _, dil in BRANCHES:
        onehot = np.zeros((period, N_BUCKETS), np.float32)
        d = np.arange(BLK + 1)
        onehot[d, _t5_bucket((BLK - d) * dil)] = 1.0
        f = jnp.dot(jnp.asarray(onehot), rel_bias, precision=lax.Precision.HIGHEST)
        flat = jnp.tile(f.T, (1, BLK))[:, :BLK * (period - 1)]
        tab = flat.reshape(N_HEADS, BLK, period - 1)[:, :, :2 * BLK]
        tabs.append(jnp.where(_band()[None], tab, NEG_INF))
    return tabs


def _bucket_onehot():
    maps = []
    q = np.arange(BLK)[:, None]
    k = np.arange(2 * BLK)[None, :]
    rel = q + BLK - k
    for _, dil in BRANCHES:
        maps.append(np.where((rel >= 0) & (rel <= BLK), _t5_bucket(rel * dil), -1).reshape(-1))
    bmap = jnp.asarray(np.concatenate(maps).astype(np.int32))
    return (bmap[:, None] == jnp.arange(128, dtype=jnp.int32)[None, :]).astype(BF16)


def _block_diag(t):
    g, n, c = t.shape
    eye = jnp.eye(g, dtype=t.dtype)
    return (t[:, :, None, :] * eye[:, None, :, None]).reshape(g * n, g * c)


def _ssm_prep(a_re, a_im, log_dt, b_re, b_im, c_re, c_im):
    lam = lax.complex(a_re, a_im)
    dt = jnp.exp(log_dt)[:, None]
    a_bar = jnp.exp(lam * dt)
    b_bar = ((a_bar - 1.0) / lam)[:, :, None] * lax.complex(b_re, b_im)
    bdt = jnp.concatenate([_block_diag(jnp.real(b_bar)), _block_diag(jnp.imag(b_bar))], axis=0)
    cd = jnp.concatenate([_block_diag(jnp.transpose(c_re, (0, 2, 1))),
                          _block_diag(-jnp.transpose(c_im, (0, 2, 1)))], axis=0)
    return jnp.real(a_bar).reshape(-1), jnp.imag(a_bar).reshape(-1), bdt, cd


def _powers(ar, ai):
    pr, pi = ar[:, None], ai[:, None]
    k = 1
    while k < 8:
        lr, li = pr[:, -1:], pi[:, -1:]
        pr, pi = (jnp.concatenate([pr, pr * lr - pi * li], axis=1),
                  jnp.concatenate([pi, pr * li + pi * lr], axis=1))
        k *= 2
    return pr, pi


def _sgu_bias_expand(b):
    return jnp.repeat(b.T, 64, axis=1)


def _layer_fwd(i, h, a1, p_i, big, small, bias_tabs, next_gain):
    nm = "l%d_" % i
    sv = {"h": h}
    if a1 is None:
        a1 = _rms_fwd(h, small["norm_attn_g"][i], nm + "rms_attn")
    z = _mm(a1, big["w_in"], "nt", nm + "in_proj")
    y_attn, lse = _attn_fwd(z, [t[0] for t in bias_tabs], nm + "attn_fwd")
    bexp = _sgu_bias_expand(small["sgu_b"][i])
    y_sgu = _sgu_fwd(z, small["sgu_ln_g"][i], small["sgu_ln_b"][i], small["sgu_w"][i], bexp, nm + "sgu_fwd")
    ar, ai, bdt, cd = _ssm_prep(*[small[k][i] for k in ("ssm_a_re", "ssm_a_im", "ssm_log_dt", "ssm_b_re",
                                                         "ssm_b_im", "ssm_c_re", "ssm_c_im")])
    xr, xi, yc = _ssm_fwd(z, bdt.astype(BF16), cd.astype(BF16), _scan_tables(*_powers(ar, ai), False),
                          nm + "ssm_core")
    y_ssm = _ssm_post_fwd(yc, z, small["ssm_d"][i], big["ssm_glu_w"], small["ssm_glu_b"][i], nm + "ssm_post")
    mix = _mix_fwd(y_attn, y_sgu, y_ssm, small["branch_norm_g"][i], nm + "mix")
    if "rest" in big:
        big = dict({k: t for k, t in big.items() if k != "rest"}, **big["rest"](mix))
    h2, a2 = _mm(mix, big["w_out"], "nn", nm + "out_proj", add=h, norm_gain=small["norm_ffn_g"][i])
    hu = _mm(a2, big["ffn_w_up"], "nt", nm + "ffn_up", out_dtype=BF16)
    hv, hg, act = _conv_fwd(hu, big["ffn_conv_w"], small["ffn_conv_b"][i], nm + "ffn_conv")
    h3, a3 = _mm(act, big["ffn_w_down"], "nn", nm + "ffn_down", add=h2, norm_gain=small["norm_ple_g"][i])
    gp = _mm(a3, big["ple_w_gate"], "nn", nm + "ple_gate", out_dtype=BF16)
    pp = _mm(p_i, big["ple_w_proj"], "nt", nm + "ple_proj", out_dtype=BF16)
    h4, a_next = _ple_fwd(h3, gp, pp, next_gain, nm + "ple_add")
    sv.update(big=big, a1=a1, z=z, y_attn=y_attn, lse=lse, y_sgu=y_sgu, y_ssm=y_ssm, yc=yc, xr=xr, xi=xi, mix=mix, h2=h2,
              a2=a2, hu=hu, hv=hv, hg=hg, act=act, h3=h3, a3=a3, gp=gp, pp=pp)
    return h4, a_next, sv


def _layer_bwd(i, dh4, sv, p_i, big, small, bias_tabs, ffn_done=None):
    nm = "l%d_" % i
    g = {}
    dpp, dgp = _ple_bwd(dh4, sv["gp"], sv["pp"], nm + "ple_bwd")
    g["ple_w_proj"] = _mm(dpp, p_i, "tn", nm + "d_ple_proj", out_dtype=BF16)
    g["ple_w_gate"] = _mm(sv["a3"], dgp, "tn", nm + "d_ple_gate", out_dtype=BF16)
    dh3, dgain = _mm(dgp, big["ple_w_gate"], "nt", nm + "ple_gate_t", add=dh4,
                     norm_bwd=(sv["h3"], small["norm_ple_g"][i]))
    g["norm_ple_g"] = dgain.reshape(D_MODEL)
    g["ffn_w_down"] = _mm(sv["act"], dh3, "tn", nm + "d_ffn_down", out_dtype=BF16)
    dact = _mm(dh3, big["ffn_w_down"], "nt", nm + "ffn_down_t", out_dtype=BF16)
    dhu, g["ffn_conv_w"], dcb = _conv_bwd(dact, sv["hv"], sv["hg"], sv["hu"], big["ffn_conv_w"],
                                          nm + "ffn_conv_bwd")
    g["ffn_conv_b"] = dcb.reshape(2 * D_FF)
    g["ffn_w_up"] = _mm(dhu, sv["a2"], "tn", nm + "d_ffn_up", out_dtype=BF16)
    dh2, dgain = _mm(dhu, big["ffn_w_up"], "nn", nm + "ffn_up_t", add=dh3,
                     norm_bwd=(sv["h2"], small["norm_ffn_g"][i]))
    g["norm_ffn_g"] = dgain.reshape(D_MODEL)
    if ffn_done is not None:
        small = ffn_done(g, small)
    g["w_out"] = _mm(sv["mix"], dh2, "tn", nm + "d_out_proj", out_dtype=BF16)
    dmix = _mm(dh2, big["w_out"], "nt", nm + "out_proj_t")
    dya, dysg, dyss, g["branch_norm_g"] = _mix_bwd(dmix, sv["y_attn"], sv["y_sgu"], sv["y_ssm"],
                                                   small["branch_norm_g"][i], nm + "mix_bwd")
    ssm_keys = ("ssm_a_re", "ssm_a_im", "ssm_log_dt", "ssm_b_re", "ssm_b_im", "ssm_c_re", "ssm_c_im")
    (ar, ai, bdt, cd), prep_vjp = jax.vjp(_ssm_prep, *[small[k][i] for k in ssm_keys])
    dy1, dgl, y2, dud, g["ssm_d"], g["ssm_glu_b"] = _ssm_post_bwd(
        dyss, sv["yc"], sv["z"], small["ssm_d"][i], big["ssm_glu_w"], small["ssm_glu_b"][i], nm + "ssm_post_bwd")
    g["ssm_glu_w"] = _mm(y2, dgl, "tn", nm + "d_ssm_glu", out_dtype=BF16)
    du, dbdt, dcd, dar, dai = _ssm_bwd(dy1, dud, sv["z"], sv["xr"], sv["xi"], bdt.astype(BF16), cd.astype(BF16),
                                       _scan_tables(*_powers(ar, ai), True), nm + "ssm_core_bwd")
    for k, val in zip(ssm_keys, prep_vjp((dar, dai, dbdt, dcd))):
        g[k] = val
    bexp, bexp_vjp = jax.vjp(_sgu_bias_expand, small["sgu_b"][i])
    dzs, g["sgu_w"], dbexp, g["sgu_ln_g"], g["sgu_ln_b"] = _sgu_bwd(
        sv["z"], dysg, small["sgu_ln_g"][i], small["sgu_ln_b"][i], small["sgu_w"][i], bexp, nm + "sgu_bwd")
    g["sgu_b"] = bexp_vjp(dbexp)[0]
    dq, dk, dv, extras, dbs = _attn_bwd(sv["z"], [t[1] for t in bias_tabs], dya, sv["y_attn"], sv["lse"],
                                        nm + "attn_bwd")
    dbs = [db.reshape(N_HEADS, BLK * 2 * BLK) for db in dbs]
    dz = _assemble_dz(dq, dk, dv, extras, dzs, du, nm + "assemble_dz")
    g["w_in"] = _mm(dz, sv["a1"], "tn", nm + "d_in_proj", out_dtype=BF16)
    dh, dgain = _mm(dz, big["w_in"], "nn", nm + "in_proj_t", add=dh2, norm_bwd=(sv["h"], small["norm_attn_g"][i]))
    g["norm_attn_g"] = dgain.reshape(D_MODEL)
    return dh, g, jnp.concatenate(dbs, axis=1)


def _local_step(x, p, target, layer_weights, small, layer_done=None):
    depth = p.shape[0]
    bias_tabs = [(t, _pair_bias_bwd(t)) for t in _bias_tables(small["rel_bias"])]
    h, a1 = x, None
    saved = []
    for i in range(depth):
        next_gain = small["norm_attn_g"][i + 1] if i + 1 < depth else None
        h, a1, sv = _layer_fwd(i, h, a1, p[i], layer_weights(i, h), small, bias_tabs, next_gain)
        saved.append(sv)
    dh, loss, g_final = _loss_head(h, target, small["final_norm_g"], "loss_head")
    layer_grads = [None] * depth
    dbias = [None] * depth
    for i in reversed(range(depth)):
        ffn_done = None if layer_done is None else (lambda g, sm, i=i: layer_done(i, "ffn", g, sm))
        dh, layer_grads[i], dbias[i] = _layer_bwd(i, dh, saved[i], p[i], saved[i]["big"], small, bias_tabs,
                                                  ffn_done)
        if layer_done is not None:
            small = layer_done(i, "all", layer_grads[i], small)
    big_grads = [{k: lg.pop(k) for k in COMM_NAMES} for lg in layer_grads]
    grads = {k: jnp.stack([layer_grads[i][k] for i in range(depth)]) for k in layer_grads[0]}
    grads["final_norm_g"] = g_final
    g_rb = _mm(sum(dbias[1:], dbias[0]), _bucket_onehot(), "nn", "d_rel_bias", tk=2048)
    grads["rel_bias"] = g_rb[:, :N_BUCKETS].T
    return loss, dh, big_grads, grads


_ANY = pl.BlockSpec(memory_space=pl.ANY)
MESH_IDS = pl.DeviceIdType.MESH


def _slot(ref, axis, j):
    return ref.at[(slice(None),) * axis + (j,)]


def _all_gather(blocks, axis, name):
    nt = len(blocks)

    def body(*refs):
        x_refs, o_refs = refs[:nt], refs[nt:2 * nt]
        send_sems, recv_sems, local_sems = refs[2 * nt:]
        x, y, c = lax.axis_index("x"), lax.axis_index("y"), lax.axis_index("c")
        me, sibling = (x, y, c), (x, y, 1 - c)
        chips = [(1 - x, y), (x, 1 - y), (1 - x, 1 - y)]

        def slot(t, px, py, pc):
            return _slot(o_refs[t], axis, 4 * px + 2 * py + pc)

        def copy(t, k, blk, to, src=None):
            return pltpu.make_async_remote_copy(
                src_ref=slot(t, *blk) if src is None else src, dst_ref=slot(t, *blk),
                send_sem=send_sems.at[7 * t + k], recv_sem=recv_sems.at[7 * t + k],
                device_id=to, device_id_type=MESH_IDS)

        mine = [pltpu.make_async_copy(x_refs[t], slot(t, *me), local_sems.at[t]) for t in range(nt)]
        for cp in mine:
            cp.start()
        first = []
        for t in range(nt):
            first.append(copy(t, 0, me, sibling, src=x_refs[t]))
            first += [copy(t, 1 + j, me, (*chip, c), src=x_refs[t]) for j, chip in enumerate(chips)]
        for cp in first:
            cp.start()
        passed = []
        for t in range(nt):
            for j, chip in enumerate(chips):
                copy(t, 1 + j, (*chip, c), me).wait_recv()
                passed.append(copy(t, 4 + j, (*chip, c), sibling))
                passed[-1].start()
        for t in range(nt):
            copy(t, 0, sibling, me).wait_recv()
            for j, chip in enumerate(chips):
                copy(t, 4 + j, (*chip, 1 - c), me).wait_recv()
        for cp in first + passed:
            cp.wait_send()
        for cp in mine:
            cp.wait()

    out_shape = [jax.ShapeDtypeStruct(b.shape[:axis] + (N_DEV,) + b.shape[axis:], b.dtype) for b in blocks]
    return pl.pallas_call(
        body, name=name, out_shape=out_shape, in_specs=[_ANY] * nt, out_specs=[_ANY] * nt,
        scratch_shapes=[pltpu.SemaphoreType.DMA((7 * nt,)), pltpu.SemaphoreType.DMA((7 * nt,)),
                        pltpu.SemaphoreType.DMA((nt,))],
    )(*blocks)


def _peer(k):
    x, y, c = lax.axis_index("x"), lax.axis_index("y"), lax.axis_index("c")
    px = 1 - x if k & 4 else x
    py = 1 - y if k & 2 else y
    pc = 1 - c if k & 1 else c
    return (px, py, pc), 4 * px + 2 * py + pc


def _all_to_all(blocks, name):
    nt = len(blocks)

    def body(*refs):
        x_refs, o_refs = refs[:nt], refs[nt:2 * nt]
        send_sems, recv_sems, local_sems = refs[2 * nt:]
        _, me = _peer(0)
        mine = [pltpu.make_async_copy(x_refs[t].at[me], o_refs[t].at[me], local_sems.at[t]) for t in range(nt)]
        for cp in mine:
            cp.start()
        copies = []
        for k in range(1, N_DEV):
            peer, idx = _peer(k)
            for t in range(nt):
                cp = pltpu.make_async_remote_copy(
                    src_ref=x_refs[t].at[idx], dst_ref=o_refs[t].at[me],
                    send_sem=send_sems.at[7 * t + k - 1], recv_sem=recv_sems.at[7 * t + k - 1],
                    device_id=peer, device_id_type=MESH_IDS)
                cp.start()
                copies.append(cp)
        for cp in copies:
            cp.wait()
        for cp in mine:
            cp.wait()

    return pl.pallas_call(
        body, name=name, out_shape=[jax.ShapeDtypeStruct(b.shape, b.dtype) for b in blocks],
        in_specs=[_ANY] * nt, out_specs=[_ANY] * nt,
        scratch_shapes=[pltpu.SemaphoreType.DMA((7 * nt,)), pltpu.SemaphoreType.DMA((7 * nt,)),
                        pltpu.SemaphoreType.DMA((nt,))],
    )(*blocks)


_HBM = pl.BlockSpec(memory_space=pltpu.HBM)
_SEM = pl.BlockSpec(memory_space=pltpu.SEMAPHORE)
_EFFECT = pltpu.SideEffectType.DATAFLOW_SIDE_EFFECTING


def _split_copy(src_ref, land_ref, send_sems, recv_sems, t, k, gather):
    peer, idx = _peer(k)
    _, me = _peer(0)
    return pltpu.make_async_remote_copy(
        src_ref=src_ref if gather else src_ref.at[idx], dst_ref=land_ref.at[me],
        send_sem=send_sems.at[7 * t + k - 1], recv_sem=recv_sems.at[7 * t + k - 1],
        device_id=peer, device_id_type=MESH_IDS)


def _exchange_start(srcs, lands, gather, name):
    nt = len(srcs)

    def body(*refs):
        src_refs, land_refs = refs[:nt], refs[nt:2 * nt]
        send_sems, recv_sems = refs[2 * nt:2 * nt + 2]
        token = refs[-1]
        for k in range(1, N_DEV):
            for t in range(nt):
                _split_copy(src_refs[t], land_refs[t], send_sems, recv_sems, t, k, gather).start()
        token[...] = jnp.zeros_like(token)

    hbm = lambda a: pltpu.HBM(a.shape, a.dtype)
    outs = pl.pallas_call(
        body, name=name,
        out_shape=(pltpu.SemaphoreType.DMA((7 * nt,)), pltpu.SemaphoreType.DMA((7 * nt,)),
                   *[hbm(a) for a in srcs], *[hbm(a) for a in lands], jax.ShapeDtypeStruct((8, 128), F32)),
        in_specs=[_HBM] * (2 * nt),
        out_specs=(_SEM, _SEM, *[_HBM] * (2 * nt), pl.BlockSpec(memory_space=pltpu.VMEM)),
        input_output_aliases={j: 2 + j for j in range(2 * nt)},
        compiler_params=pltpu.CompilerParams(has_side_effects=_EFFECT),
    )(*[pltpu.with_memory_space_constraint(a, pltpu.HBM) for a in list(srcs) + list(lands)])
    return outs[0], outs[1], outs[2:2 + nt], outs[2 + nt:2 + 2 * nt], outs[-1]


def _exchange_wait(send_sems, recv_sems, srcs, lands, after, gather, name):
    nt = len(srcs)

    def body(*refs):
        src_refs, land_refs = refs[:nt], refs[nt:2 * nt]
        send_sems, recv_sems = refs[2 * nt:2 * nt + 2]
        for k in range(1, N_DEV):
            _, idx = _peer(k)
            for t in range(nt):
                _split_copy(src_refs[t], land_refs[t], send_sems, recv_sems, t, k, gather).wait_send()
                arrival = pltpu.make_async_remote_copy(
                    src_ref=land_refs[t].at[idx], dst_ref=land_refs[t].at[idx],
                    send_sem=send_sems.at[7 * t + k - 1], recv_sem=recv_sems.at[7 * t + k - 1],
                    device_id=_peer(k)[0], device_id_type=MESH_IDS)
                arrival.wait_recv()

    hbm = lambda a: pltpu.HBM(a.shape, a.dtype)
    outs = pl.pallas_call(
        body, name=name, out_shape=tuple(hbm(a) for a in list(srcs) + list(lands)),
        in_specs=[_HBM] * (2 * nt) + [_SEM, _SEM, _ANY], out_specs=tuple([_HBM] * (2 * nt)),
        input_output_aliases={j: j for j in range(2 * nt)},
        compiler_params=pltpu.CompilerParams(has_side_effects=_EFFECT),
    )(*srcs, *lands, send_sems, recv_sems, after)
    return outs[nt:]


def _adamw(parts, w, m, v, name, tr):
    n_layers, r, c_ = w.shape
    assert len(parts) == n_layers

    def body(*refs):
        p_refs = refs[:n_layers]
        w_ref, m_ref, v_ref, g_ref, d_ref, mo_ref, vo_ref = refs[n_layers:]

        def update(p_ref):
            g = p_ref[0].astype(F32)
            for j in range(1, N_DEV):
                g = g + p_ref[j].astype(F32)
            m2 = ADAM_B1 * m_ref[...] + (1.0 - ADAM_B1) * g
            v2 = ADAM_B2 * v_ref[...] + (1.0 - ADAM_B2) * (g * g)
            m_hat = m2 / (1.0 - ADAM_B1 ** ADAM_STEP)
            v_hat = v2 / (1.0 - ADAM_B2 ** ADAM_STEP)
            g_ref[...] = g
            d_ref[...] = -ADAM_LR * (m_hat / (jnp.sqrt(v_hat) + ADAM_EPS) + ADAM_WD * w_ref[...])
            mo_ref[...] = m2
            vo_ref[...] = v2

        for layer in range(n_layers):
            pl.when(pl.program_id(0) == layer)(lambda layer=layer: update(p_refs[layer]))

    spec = pl.BlockSpec((None, tr, c_), lambda l, i: (l, i, 0))
    p_spec = pl.BlockSpec((N_DEV, tr, c_), lambda l, i: (0, i, 0))
    return pl.pallas_call(
        body, name=name, grid=(n_layers, r // tr), in_specs=[p_spec] * n_layers + [spec] * 3,
        out_specs=[spec] * 4, out_shape=[_sds((n_layers, r, c_))] * 4,
        compiler_params=_params(("parallel", "parallel")),
    )(*parts, w, m, v)


def _pack_rows(n_elems, align):
    rows = -(-n_elems // PACK_COLS)
    return -(-rows // align) * align


def _pack(arrs, rows, dtype=F32):
    flat = jnp.concatenate([a.reshape(-1) for a in arrs]).astype(dtype)
    return jnp.pad(flat, (0, rows * PACK_COLS - flat.shape[0])).reshape(rows, PACK_COLS)


def _unpack(pack, shapes):
    flat = pack.reshape(-1)
    out, off = [], 0
    for shp in shapes:
        size = int(np.prod(shp))
        out.append(flat[off:off + size].reshape(shp))
        off += size
    return out


def _tile_rows(rows, target, align=16):
    best = align
    for t in range(align, target + 1, align):
        if rows % t == 0:
            best = t
    return best


COMM_NAMES = ("w_in", "ssm_glu_w", "w_out", "ffn_w_up", "ffn_w_down", "ple_w_gate", "ple_w_proj")
COMM_TRANSPOSED = ("w_in", "ffn_w_up", "ple_w_proj")
COMM_EARLY = ("ple_w_proj", "ple_w_gate", "ffn_w_down", "ffn_w_up")
COMM_LATE = ("w_in", "ssm_glu_w", "w_out")
SMALL_TILE_ROWS = 64
CONV_NAME = "ffn_conv_w"


def _to_comm(name, a):
    return jnp.swapaxes(a, 1, 2) if name in COMM_TRANSPOSED else a


def kernel(x, p, rel_bias, norm_attn_g, w_in, sgu_ln_g, sgu_ln_b, sgu_w, sgu_b, ssm_a_re, ssm_a_im, ssm_log_dt, ssm_b_re, ssm_b_im, ssm_c_re, ssm_c_im, ssm_d, ssm_glu_w, ssm_glu_b, branch_norm_g, w_out, norm_ffn_g, ffn_w_up, ffn_conv_w, ffn_conv_b, ffn_w_down, norm_ple_g, ple_w_gate, ple_w_proj, final_norm_g, loss_target, m_rel_bias, m_norm_attn_g, m_w_in, m_sgu_ln_g, m_sgu_ln_b, m_sgu_w, m_sgu_b, m_ssm_a_re, m_ssm_a_im, m_ssm_log_dt, m_ssm_b_re, m_ssm_b_im, m_ssm_c_re, m_ssm_c_im, m_ssm_d, m_ssm_glu_w, m_ssm_glu_b, m_branch_norm_g, m_w_out, m_norm_ffn_g, m_ffn_w_up, m_ffn_conv_w, m_ffn_conv_b, m_ffn_w_down, m_norm_ple_g, m_ple_w_gate, m_ple_w_proj, m_final_norm_g, v_rel_bias, v_norm_attn_g, v_w_in, v_sgu_ln_g, v_sgu_ln_b, v_sgu_w, v_sgu_b, v_ssm_a_re, v_ssm_a_im, v_ssm_log_dt, v_ssm_b_re, v_ssm_b_im, v_ssm_c_re, v_ssm_c_im, v_ssm_d, v_ssm_glu_w, v_ssm_glu_b, v_branch_norm_g, v_w_out, v_norm_ffn_g, v_ffn_w_up, v_ffn_conv_w, v_ffn_conv_b, v_ffn_w_down, v_norm_ple_g, v_ple_w_gate, v_ple_w_proj, v_final_norm_g):
    given = dict(locals())
    w = {n: given[n] for n in WEIGHT_NAMES}
    m = {n: given["m_" + n] for n in WEIGHT_NAMES}
    v = {n: given["v_" + n] for n in WEIGHT_NAMES}
    depth = p.shape[0]
    dev = 4 * lax.axis_index("x") + 2 * lax.axis_index("y") + lax.axis_index("c")

    wc = {n: _to_comm(n, w[n]) for n in COMM_NAMES}
    wb = {n: wc[n].astype(BF16) for n in COMM_NAMES}
    conv_local = [w[CONV_NAME], m[CONV_NAME], v[CONV_NAME]]
    conv_rows = _pack_rows(sum(int(np.prod(t.shape)) for t in conv_local), 8)
    conv_g, = _all_gather([_pack(conv_local, conv_rows)], 0, "gather_conv_taps")
    conv_parts = zip(*[_unpack(conv_g[j], [t.shape for t in conv_local]) for j in range(N_DEV)])
    conv_w, conv_m, conv_v = [jnp.concatenate(parts, axis=2) for parts in conv_parts]
    small = {n: w[n] for n in SMALL_NAMES}

    def whole(names, blocks):
        return {n: t.reshape(-1, t.shape[-1]) for n, t in zip(names, blocks)}

    def own_slot(block):
        return lax.dynamic_update_slice_in_dim(jnp.zeros((N_DEV,) + block.shape, block.dtype), block[None], dev, 0)

    def start_gather(names, i, after):
        srcs, after = lax.optimization_barrier(([wb[n][i] for n in names], after))
        return _exchange_start(srcs, [own_slot(s) for s in srcs], True, "gather_weights_%d_start" % i), after

    def wait_gather(names, i, started, after):
        send_sems, recv_sems, srcs, lands, _ = started
        return whole(names, _exchange_wait(send_sems, recv_sems, srcs, lands, after, True,
                                           "gather_weights_%d_wait" % i))

    at_once = ("w_in", "ssm_glu_w")
    later = tuple(n for n in COMM_NAMES if n not in at_once)
    w_in_0 = _all_gather([wb[n][0] for n in at_once], 0, "gather_w_in_0")
    gathering = {}
    gathering[0], (w_in_0, _) = start_gather(later, 0, (w_in_0, conv_g))
    small["norm_attn_g"] = small["norm_attn_g"] + gathering[0][4][0, 0]

    def layer_weights(i, h):
        if i > 0:
            got = wait_gather(COMM_NAMES, i, gathering.pop(i), h)
            if i + 1 < depth:
                gathering[i + 1], ordered = start_gather(COMM_NAMES, i + 1, got["w_in"])
                got["w_in"] = ordered + gathering[i + 1][4][0, 0].astype(BF16)
            return dict(got, **{CONV_NAME: conv_w[i]})

        def rest(after):
            got = wait_gather(later, 0, gathering.pop(0), after)
            if depth > 1:
                gathering[1], ordered = start_gather(COMM_NAMES, 1, got["w_out"])
                got["w_out"] = ordered + gathering[1][4][0, 0].astype(BF16)
            return got

        return dict(whole(at_once, w_in_0), **{CONV_NAME: conv_w[0], "rest": rest})

    def as_slots(g, n):
        return g.reshape((N_DEV,) + wc[n].shape[1:])

    scattering = {}

    def layer_done(i, stage, g, small_now):
        if stage == "all" and i == 0:
            return small_now
        names = COMM_EARLY if stage == "ffn" else COMM_LATE
        srcs = [as_slots(g[n], n) for n in names]
        lands = [own_slot(lax.dynamic_index_in_dim(s, dev, 0, keepdims=False)) for s in srcs]
        started = _exchange_start(srcs, lands, False, "scatter_weight_grads_%d_%s_start" % (i, stage))
        scattering[i, stage] = (names, started)
        pin = "branch_norm_g" if stage == "ffn" else "norm_ple_g"
        return dict(small_now, **{pin: small_now[pin] + started[4][0, 0]})

    loss, dx, big_grads, grads = _local_step(x[0], p[:, 0], loss_target[0], layer_weights, small, layer_done)
    loss = lax.psum(loss, ("x", "y", "c"))

    recv = [{} for _ in range(depth)]
    for (i, stage), (names, (send_sems, recv_sems, srcs, lands, _)) in scattering.items():
        got = _exchange_wait(send_sems, recv_sems, srcs, lands, dx, False,
                             "scatter_weight_grads_%d_%s_wait" % (i, stage))
        recv[i].update(zip(names, got))
    srcs = [as_slots(big_grads[0][n], n) for n in COMM_LATE]
    lands = [own_slot(lax.dynamic_index_in_dim(s, dev, 0, keepdims=False)) for s in srcs]
    last = _exchange_start(srcs, lands, False, "scatter_weight_grads_0_all_start")
    out = {}

    def update(n, pin=None):
        weight = wc[n] if pin is None else wc[n] + pin
        res = _adamw([recv[i][n] for i in range(depth)], weight, _to_comm(n, m[n]), _to_comm(n, v[n]),
                     "adamw_" + n, _tile_rows(wc[n].shape[1], 256))
        out[n] = [_to_comm(n, r) for r in res]

    for j, n in enumerate(COMM_EARLY):
        update(n, last[4][0, 0] if j == 0 else None)
    got = _exchange_wait(last[0], last[1], last[2], last[3], out[COMM_EARLY[-1]][0], False,
                         "scatter_weight_grads_0_all_wait")
    recv[0].update(zip(COMM_LATE, got))

    rep_names = SMALL_NAMES + (CONV_NAME,)
    rep_w = dict({n: w[n] for n in SMALL_NAMES}, **{CONV_NAME: conv_w})
    rep_m = dict({n: m[n] for n in SMALL_NAMES}, **{CONV_NAME: conv_m})
    rep_v = dict({n: v[n] for n in SMALL_NAMES}, **{CONV_NAME: conv_v})
    rep_shapes = [rep_w[n].shape for n in rep_names]
    rep_rows = _pack_rows(sum(int(np.prod(s)) for s in rep_shapes), SMALL_TILE_ROWS)
    rep_parts, = _all_gather([_pack([grads[n] for n in rep_names], rep_rows)], 0, "gather_small_grads")
    for n in COMM_LATE:
        update(n)
    rep_out = _adamw([rep_parts], *[_pack([src[n] for n in rep_names], rep_rows)[None] for src in (rep_w, rep_m, rep_v)],
                     "adamw_replicated", SMALL_TILE_ROWS)
    for n, vals in zip(rep_names, zip(*[_unpack(r[0], rep_shapes) for r in rep_out])):
        out[n] = list(vals)
    shard = ffn_conv_w.shape[2]
    out[CONV_NAME] = [lax.dynamic_slice_in_dim(t, dev * shard, shard, axis=2) for t in out[CONV_NAME]]
    results = [[out[n][kind] for n in WEIGHT_NAMES] for kind in range(4)]
    return (loss, dx[None], *results[0], *results[1], *results[2], *results[3])
```

```python
import math

import numpy as np
import jax
import jax.numpy as jnp
from jax import lax
from jax.experimental import pallas as pl
from jax.experimental.pallas import tpu as pltpu

F32 = jnp.float32
BF16 = jnp.bfloat16

D_MODEL = 1024
HEAD_DIM = 64
N_HEADS = 8
ATTN_W = 512
SGU_W = 256
SGU_GROUPS = 4
SGU_CHUNK = 128
SSM_W = 256
SSM_GROUPS = 16
SSM_CH = 16
SSM_STATE = 64
SSM_NS = SSM_GROUPS * SSM_STATE
IN_W = 2304
D_FF = 2816
PLE_DIM = 256
BRANCHES = ((128, 1), (512, 4), (2048, 16))
BLK = 128
N_BUCKETS = 32
REL_MAX = 2048
EPS = 1e-6
NEG_INF = -1e30
N_DEV = 8

ADAM_LR = 0.001
ADAM_B1 = 0.9
ADAM_B2 = 0.999
ADAM_EPS = 1e-08
ADAM_WD = 0.01
ADAM_STEP = 10

VMEM_LIMIT_BYTES = 56 * 1024 * 1024
GELU_C = math.sqrt(2.0 / math.pi)

SMALL_NAMES = ("rel_bias", "norm_attn_g", "sgu_ln_g", "sgu_ln_b", "sgu_w", "sgu_b", "ssm_a_re", "ssm_a_im",
               "ssm_log_dt", "ssm_b_re", "ssm_b_im", "ssm_c_re", "ssm_c_im", "ssm_d", "ssm_glu_b",
               "branch_norm_g", "norm_ffn_g", "ffn_conv_b", "norm_ple_g", "final_norm_g")
WEIGHT_NAMES = ("rel_bias", "norm_attn_g", "w_in", "sgu_ln_g", "sgu_ln_b", "sgu_w", "sgu_b", "ssm_a_re",
                "ssm_a_im", "ssm_log_dt", "ssm_b_re", "ssm_b_im", "ssm_c_re", "ssm_c_im", "ssm_d", "ssm_glu_w",
                "ssm_glu_b", "branch_norm_g", "w_out", "norm_ffn_g", "ffn_w_up", "ffn_conv_w", "ffn_conv_b",
                "ffn_w_down", "norm_ple_g", "ple_w_gate", "ple_w_proj", "final_norm_g")
PACK_COLS = 512


def _params(sem):
    return pltpu.CompilerParams(dimension_semantics=sem, vmem_limit_bytes=VMEM_LIMIT_BYTES)


def _pick(dim, target):
    if dim <= target:
        return dim
    best = None
    for t in range(128, target + 1, 128):
        if dim % t == 0:
            best = t
    return dim if best is None else best


def _gelu(x):
    return 0.5 * x * (1.0 + jnp.tanh(GELU_C * (x + 0.044715 * (x * x * x))))


def _gelu_grad(x):
    t = jnp.tanh(GELU_C * (x + 0.044715 * (x * x * x)))
    return 0.5 * (1.0 + t) + 0.5 * x * (1.0 - t * t) * (GELU_C * (1.0 + 3.0 * 0.044715 * (x * x)))


def _sigmoid(x):
    return 1.0 / (1.0 + jnp.exp(-x))


_DIMS = {"nn": (((1,), (0,)), ((), ())), "tn": (((0,), (0,)), ((), ())), "nt": (((1,), (1,)), ((), ()))}


def _mm(a, b, mode, name, add=None, out_dtype=F32, norm_gain=None, norm_bwd=None, tm=1408, tn=1408, tk=1408):
    if mode == "nn":
        m, k = a.shape
        k2, n = b.shape
    elif mode == "tn":
        k, m = a.shape
        k2, n = b.shape
    else:
        m, k = a.shape
        n, k2 = b.shape
    assert k == k2, (name, a.shape, b.shape, mode)
    tm, tn, tk = _pick(m, tm), _pick(n, tn), _pick(k, tk)
    nk = k // tk
    dims = _DIMS[mode]
    has_add = add is not None
    has_norm = norm_gain is not None
    has_nbwd = norm_bwd is not None
    assert not (has_norm or has_nbwd) or tn == n

    def body(*refs):
        a_ref, b_ref = refs[:2]
        rest = list(refs[2:])
        add_ref = rest.pop(0) if has_add else None
        g_ref = rest.pop(0) if has_norm else None
        h_ref, hg_ref = (rest.pop(0), rest.pop(0)) if has_nbwd else (None, None)
        o_ref = rest.pop(0)
        n_ref = rest.pop(0) if has_norm else None
        dg_ref = rest.pop(0) if has_nbwd else None
        part = lax.dot_general(a_ref[...].astype(BF16), b_ref[...].astype(BF16), dims,
                               preferred_element_type=F32)
        if has_nbwd:
            @pl.when((pl.program_id(0) == 0) & (pl.program_id(2) == 0))
            def _():
                dg_ref[...] = jnp.zeros_like(dg_ref)

        def finish(r):
            if has_nbwd:
                x = h_ref[...]
                scale = lax.rsqrt(jnp.mean(x * x, axis=-1, keepdims=True) + EPS)
                xh = x * scale
                dg_ref[...] += jnp.sum(r * xh, axis=0, keepdims=True)
                dxh = r * hg_ref[...]
                r = scale * (dxh - xh * jnp.mean(dxh * xh, axis=-1, keepdims=True))
            if has_add:
                r = r + add_ref[...]
            o_ref[...] = r.astype(out_dtype)
            if has_norm:
                scale = lax.rsqrt(jnp.mean(r * r, axis=-1, keepdims=True) + EPS)
                n_ref[...] = (r * scale * g_ref[...]).astype(BF16)

        if nk == 1:
            finish(part)
            return
        acc_ref = refs[-1]
        kk = pl.program_id(2)

        @pl.when(kk == 0)
        def _():
            acc_ref[...] = part

        @pl.when((kk > 0) & (kk < nk - 1))
        def _():
            acc_ref[...] += part

        @pl.when(kk == nk - 1)
        def _():
            finish(acc_ref[...] + part)

    if mode == "tn":
        a_spec = pl.BlockSpec((tk, tm), lambda i, j, kk: (kk, i))
    else:
        a_spec = pl.BlockSpec((tm, tk), lambda i, j, kk: (i, kk))
    if mode == "nt":
        b_spec = pl.BlockSpec((tn, tk), lambda i, j, kk: (j, kk))
    else:
        b_spec = pl.BlockSpec((tk, tn), lambda i, j, kk: (kk, j))
    o_spec = pl.BlockSpec((tm, tn), lambda i, j, kk: (i, j))
    in_specs = [a_spec, b_spec] + ([o_spec] if has_add else [])
    args = (a, b) + ((add,) if has_add else ())
    out_specs, out_shape = o_spec, jax.ShapeDtypeStruct((m, n), out_dtype)
    if has_norm:
        in_specs.append(pl.BlockSpec((1, n), lambda i, j, kk: (0, 0)))
        args += (norm_gain.reshape(1, n),)
        out_specs, out_shape = [o_spec, o_spec], [out_shape, jax.ShapeDtypeStruct((m, n), BF16)]
    if has_nbwd:
        row_spec = pl.BlockSpec((1, n), lambda i, j, kk: (0, 0))
        in_specs += [o_spec, row_spec]
        args += (norm_bwd[0], norm_bwd[1].reshape(1, n))
        out_specs, out_shape = [o_spec, row_spec], [out_shape, jax.ShapeDtypeStruct((1, n), F32)]
    sem = ("arbitrary",) * 3 if has_nbwd else ("parallel", "parallel", "arbitrary")
    return pl.pallas_call(
        body, name=name, grid=(m // tm, n // tn, nk),
        in_specs=in_specs, out_specs=out_specs, out_shape=out_shape,
        scratch_shapes=[pltpu.VMEM((tm, tn), F32)] if nk > 1 else [], compiler_params=_params(sem),
    )(*args)


def _rb(tm, w, cb=0):
    return pl.BlockSpec((tm, w), lambda i: (i, cb))


def _fb(shape):
    nd = len(shape)
    return pl.BlockSpec(shape, lambda i: (0,) * nd)


def _rowcall(body, name, n_rows, tm, in_specs, args, out_specs, out_shapes):
    return pl.pallas_call(
        body, name=name, grid=(n_rows // tm,), in_specs=in_specs, out_specs=out_specs, out_shape=out_shapes,
        compiler_params=_params(("arbitrary",)),
    )(*args)


def _sds(shape, dtype=F32):
    return jax.ShapeDtypeStruct(shape, dtype)


def _rms_fwd(h, g, name, tm=512):
    s, d = h.shape

    def body(h_ref, g_ref, o_ref):
        x = h_ref[...]
        r = lax.rsqrt(jnp.mean(x * x, axis=-1, keepdims=True) + EPS)
        o_ref[...] = (x * r * g_ref[...]).astype(BF16)

    return _rowcall(body, name, s, tm, [_rb(tm, d), _fb((1, d))], (h, g.reshape(1, d)), _rb(tm, d),
                    _sds((s, d), BF16))


def _loss_head(h, target, g, name, tm=512):
    s, d = h.shape

    def body(h_ref, t_ref, g_ref, dh_ref, loss_ref, dg_ref):
        @pl.when(pl.program_id(0) == 0)
        def _():
            dg_ref[...] = jnp.zeros_like(dg_ref)
            loss_ref[...] = jnp.zeros_like(loss_ref)

        x = h_ref[...]
        r = lax.rsqrt(jnp.mean(x * x, axis=-1, keepdims=True) + EPS)
        xh = x * r
        gg = g_ref[...]
        err = xh * gg - t_ref[...]
        loss_ref[...] += jnp.sum(err * err) * (0.5 / d)
        dy = err * (1.0 / d)
        dg_ref[...] += jnp.sum(dy * xh, axis=0, keepdims=True)
        dxh = dy * gg
        dh_ref[...] = r * (dxh - xh * jnp.mean(dxh * xh, axis=-1, keepdims=True))

    dh, loss, dg = _rowcall(body, name, s, tm, [_rb(tm, d), _rb(tm, d), _fb((1, d))], (h, target, g.reshape(1, d)),
                            [_rb(tm, d), _fb((1, 128)), _fb((1, d))], [_sds((s, d)), _sds((1, 128)), _sds((1, d))])
    return dh, loss[0, 0], dg.reshape(d)


_MIX_PARTS = ((0, 512), (512, 768), (768, 1024))


def _mix_fwd(ya, ysg, yss, g, name, tm=512):
    s = ya.shape[0]

    def body(a_ref, b_ref, c_ref, g_ref, o_ref):
        for ref, (lo, hi) in zip((a_ref, b_ref, c_ref), _MIX_PARTS):
            y = ref[...]
            r = lax.rsqrt(jnp.mean(y * y, axis=-1, keepdims=True) + EPS)
            o_ref[:, lo:hi] = (y * r * g_ref[:, lo:hi]).astype(BF16)

    return _rowcall(body, name, s, tm, [_rb(tm, 512), _rb(tm, 256), _rb(tm, 256), _fb((1, 1024))],
                    (ya, ysg, yss, g.reshape(1, 1024)), _rb(tm, 1024), _sds((s, 1024), BF16))


def _mix_bwd(dmix, ya, ysg, yss, g, name, tm=512):
    s = ya.shape[0]

    def body(dm_ref, a_ref, b_ref, c_ref, g_ref, da_ref, db_ref, dc_ref, dg_ref):
        @pl.when(pl.program_id(0) == 0)
        def _():
            dg_ref[...] = jnp.zeros_like(dg_ref)

        for ref, dref, (lo, hi) in zip((a_ref, b_ref, c_ref), (da_ref, db_ref, dc_ref), _MIX_PARTS):
            y = ref[...]
            r = lax.rsqrt(jnp.mean(y * y, axis=-1, keepdims=True) + EPS)
            xh = y * r
            dm = dm_ref[:, lo:hi]
            dg_ref[:, lo:hi] += jnp.sum(dm * xh, axis=0, keepdims=True)
            dxh = dm * g_ref[:, lo:hi]
            dref[...] = r * (dxh - xh * jnp.mean(dxh * xh, axis=-1, keepdims=True))

    da, db, dc, dg = _rowcall(
        body, name, s, tm, [_rb(tm, 1024), _rb(tm, 512), _rb(tm, 256), _rb(tm, 256), _fb((1, 1024))],
        (dmix, ya, ysg, yss, g.reshape(1, 1024)),
        [_rb(tm, 512), _rb(tm, 256), _rb(tm, 256), _fb((1, 1024))],
        [_sds((s, 512)), _sds((s, 256)), _sds((s, 256)), _sds((1, 1024))])
    return da, db, dc, dg.reshape(1024)


def _ssm_post_fwd(yc, z, d, gw, gb, name, tm=1024):
    s = yc.shape[0]

    def body(yc_ref, u_ref, d_ref, gw_ref, gb_ref, o_ref):
        y1 = yc_ref[...] + d_ref[...] * u_ref[...]
        y2 = _gelu(y1)
        gl = jnp.dot(y2.astype(BF16), gw_ref[...], preferred_element_type=F32) + gb_ref[...]
        o_ref[...] = y2 * _sigmoid(gl)

    return _rowcall(body, name, s, tm, [_rb(tm, 256), _rb(tm, 256, 8), _fb((1, 256)), _fb((256, 256)), _fb((1, 256))],
                    (yc, z, d.reshape(1, 256), gw, gb.reshape(1, 256)), _rb(tm, 256), _sds((s, 256)))


def _ssm_post_bwd(dy, yc, z, d, gw, gb, name, tm=1024):
    s = yc.shape[0]

    def body(dy_ref, yc_ref, u_ref, d_ref, gw_ref, gb_ref, dy1_ref, dgl_ref, y2_ref, dud_ref, dd_ref, dgb_ref):
        @pl.when(pl.program_id(0) == 0)
        def _():
            dd_ref[...] = jnp.zeros_like(dd_ref)
            dgb_ref[...] = jnp.zeros_like(dgb_ref)

        u = u_ref[...]
        dd = d_ref[...]
        y1 = yc_ref[...] + dd * u
        y2 = _gelu(y1)
        gw_v = gw_ref[...]
        gl = jnp.dot(y2.astype(BF16), gw_v, preferred_element_type=F32) + gb_ref[...]
        sg = _sigmoid(gl)
        dyv = dy_ref[...]
        dgl = dyv * y2 * sg * (1.0 - sg)
        dy2 = dyv * sg + lax.dot_general(dgl.astype(BF16), gw_v, _DIMS["nt"], preferred_element_type=F32)
        dy1 = dy2 * _gelu_grad(y1)
        dy1_ref[...] = dy1.astype(BF16)
        dgl_ref[...] = dgl.astype(BF16)
        y2_ref[...] = y2.astype(BF16)
        dud_ref[...] = dy1 * dd
        dd_ref[...] += jnp.sum(dy1 * u, axis=0, keepdims=True)
        dgb_ref[...] += jnp.sum(dgl, axis=0, keepdims=True)

    outs = _rowcall(
        body, name, s, tm,
        [_rb(tm, 256), _rb(tm, 256), _rb(tm, 256, 8), _fb((1, 256)), _fb((256, 256)), _fb((1, 256))],
        (dy, yc, z, d.reshape(1, 256), gw, gb.reshape(1, 256)),
        [_rb(tm, 256)] * 4 + [_fb((1, 256))] * 2,
        [_sds((s, 256), BF16)] * 3 + [_sds((s, 256))] + [_sds((1, 256))] * 2)
    dy1, dgl, y2, dud, dd, dgb = outs
    return dy1, dgl, y2, dud, dd.reshape(256), dgb.reshape(256)


SCAN_T = 512
N_SCAN_TABLES = 6


def _scan_tables(pr, pi, reverse):
    ns = pr.shape[0]
    sign = -1.0 if reverse else 1.0
    power = [(jnp.ones((ns,), F32), jnp.zeros((ns,), F32))] + [(pr[:, k], sign * pi[:, k]) for k in range(8)]
    zero = (jnp.zeros((ns,), F32), jnp.zeros((ns,), F32))

    def table(exponents):
        rows = [zero if e is None else power[e] for e in exponents]
        return jnp.stack([jnp.concatenate(row) for row in rows])

    tabs = []
    for k in (1, 2, 4):
        has_partner = [(s < 8 - k) if reverse else (s >= k) for s in range(8)]
        tabs.append(table([k if ok else None for ok in has_partner]))
    tabs.append(table([s if reverse else 7 - s for s in range(8)]))
    tabs.append(table([8 - s if reverse else s + 1 for s in range(8)]))
    tabs.append(table([8] * 8))
    return jnp.stack(tabs)


def _cmul(ar, ai, br, bi):
    return ar * br - ai * bi, ar * bi + ai * br


def _scan_group(ur, ui, cr, ci, tr_ref, ti_ref, reverse):
    xr, xi = ur, ui
    for n, k in enumerate((1, 2, 4)):
        shift = 8 - k if reverse else k
        pr, pi = _cmul(tr_ref[n], ti_ref[n], pltpu.roll(xr, shift, axis=0), pltpu.roll(xi, shift, axis=0))
        xr, xi = xr + pr, xi + pi
    sr, si = _cmul(tr_ref[3], ti_ref[3], ur, ui)
    for k in (1, 2, 4):
        sr, si = sr + pltpu.roll(sr, k, axis=0), si + pltpu.roll(si, k, axis=0)
    pr, pi = _cmul(tr_ref[4], ti_ref[4], cr, ci)
    nr, ni = _cmul(tr_ref[5], ti_ref[5], cr, ci)
    return xr + pr, xi + pi, nr + sr, ni + si


def _table_halves(t_ref):
    return t_ref.at[:, :, pl.ds(0, SSM_NS)], t_ref.at[:, :, pl.ds(SSM_NS, SSM_NS)]


_U_BLOCK = (IN_W - SSM_W) // SSM_W


def _ssm_fwd(z, bdt, cd, tabs, name):
    s = z.shape[0]
    ns = SSM_NS
    n_t = s // SCAN_T

    def body(u_ref, b_ref, c_ref, t_ref, xr_ref, xi_ref, y_ref, cr_ref, ci_ref, ur_ref, ui_ref):
        @pl.when(pl.program_id(0) == 0)
        def _():
            cr_ref[...] = jnp.zeros_like(cr_ref)
            ci_ref[...] = jnp.zeros_like(ci_ref)

        bu = lax.dot_general(u_ref[...].astype(BF16), b_ref[...], _DIMS["nt"], preferred_element_type=F32)
        ur_ref[...] = bu[:, :ns]
        ui_ref[...] = bu[:, ns:]
        tr_ref, ti_ref = _table_halves(t_ref)

        def group(g, carry):
            rows = pl.ds(pl.multiple_of(g * 8, 8), 8)
            xr, xi, cr, ci = _scan_group(ur_ref[rows, :], ui_ref[rows, :], *carry, tr_ref, ti_ref, False)
            xr_ref[rows, :] = xr
            xi_ref[rows, :] = xi
            return cr, ci

        cr, ci = lax.fori_loop(0, SCAN_T // 8, group, (cr_ref[...], ci_ref[...]), unroll=2)
        cr_ref[...] = cr
        ci_ref[...] = ci
        y_ref[...] = (jnp.dot(xr_ref[...].astype(BF16), c_ref[0:ns, :], preferred_element_type=F32)
                      + jnp.dot(xi_ref[...].astype(BF16), c_ref[ns:, :], preferred_element_type=F32))

    x_spec = pl.BlockSpec((SCAN_T, ns), lambda t: (t, 0))
    return pl.pallas_call(
        body, name=name, grid=(n_t,),
        in_specs=[pl.BlockSpec((SCAN_T, SSM_W), lambda t: (t, _U_BLOCK)), _fb((2 * ns, SSM_W)),
                  _fb((2 * ns, SSM_W)), _fb((N_SCAN_TABLES, 8, 2 * ns))],
        out_specs=[x_spec, x_spec, _rb(SCAN_T, SSM_W)],
        out_shape=[_sds((s, ns)), _sds((s, ns)), _sds((s, SSM_W))],
        scratch_shapes=[pltpu.VMEM((8, ns), F32)] * 2 + [pltpu.VMEM((SCAN_T, ns), F32)] * 2,
        compiler_params=_params(("arbitrary",)),
    )(z, bdt, cd, tabs)


def _ssm_bwd(dy1, dud, z, xr, xi, bdt, cd, tabs, name):
    s = z.shape[0]
    ns = SSM_NS
    n_t = s // SCAN_T
    n_groups = SCAN_T // 8

    def body(dy_ref, dud_ref, u_ref, xr_ref, xi_ref, pxr_ref, pxi_ref, b_ref, c_ref, t_ref,
             du_ref, dbd_ref, dcd_ref, dar_ref, dai_ref,
             cr_ref, ci_ref, ar_ref, ai_ref, sxr_ref, sxi_ref, gr_ref, gi_ref, lr_ref, li_ref, bacc_ref, cacc_ref):
        t = pl.program_id(0)

        @pl.when(t == 0)
        def _():
            for ref in (cr_ref, ci_ref, ar_ref, ai_ref, bacc_ref, cacc_ref):
                ref[...] = jnp.zeros_like(ref)

        dyb = dy_ref[...]
        g = lax.dot_general(dyb, c_ref[...], _DIMS["nt"], preferred_element_type=F32)
        gr_ref[...] = g[:, :ns]
        gi_ref[...] = g[:, ns:]
        has_before = (t < n_t - 1).astype(F32)
        sxr_ref[0:8, :] = pxr_ref[...] * has_before
        sxi_ref[0:8, :] = pxi_ref[...] * has_before
        sxr_ref[8:, :] = xr_ref[...]
        sxi_ref[8:, :] = xi_ref[...]
        first_row = lax.broadcasted_iota(jnp.int32, (8, ns), 0) == 0
        tr_ref, ti_ref = _table_halves(t_ref)

        def group(k, carry):
            cr, ci, ar, ai = carry
            g8 = pl.multiple_of((n_groups - 1 - k) * 8, 8)
            rows = pl.ds(g8, 8)
            lr, li, cr, ci = _scan_group(gr_ref[rows, :], gi_ref[rows, :], cr, ci, tr_ref, ti_ref, True)
            lr_ref[rows, :] = lr
            li_ref[rows, :] = li
            here, before = pl.ds(g8 + 8, 8), rows
            pr = jnp.where(first_row, pltpu.roll(sxr_ref[before, :], 1, axis=0), pltpu.roll(sxr_ref[here, :], 1, axis=0))
            pi = jnp.where(first_row, pltpu.roll(sxi_ref[before, :], 1, axis=0), pltpu.roll(sxi_ref[here, :], 1, axis=0))
            return cr, ci, ar + lr * pr + li * pi, ai + li * pr - lr * pi

        cr, ci, ar, ai = lax.fori_loop(0, n_groups, group,
                                       (cr_ref[...], ci_ref[...], ar_ref[...], ai_ref[...]), unroll=2)
        cr_ref[...] = cr
        ci_ref[...] = ci
        ar_ref[...] = ar
        ai_ref[...] = ai
        lrb = lr_ref[...].astype(BF16)
        lib = li_ref[...].astype(BF16)
        ub = u_ref[...].astype(BF16)
        du_ref[...] = (dud_ref[...] + jnp.dot(lrb, b_ref[0:ns, :], preferred_element_type=F32)
                       + jnp.dot(lib, b_ref[ns:, :], preferred_element_type=F32))
        bacc_ref[0:ns, :] += lax.dot_general(lrb, ub, _DIMS["tn"], preferred_element_type=F32)
        bacc_ref[ns:, :] += lax.dot_general(lib, ub, _DIMS["tn"], preferred_element_type=F32)
        cacc_ref[0:ns, :] += lax.dot_general(xr_ref[...].astype(BF16), dyb, _DIMS["tn"], preferred_element_type=F32)
        cacc_ref[ns:, :] += lax.dot_general(xi_ref[...].astype(BF16), dyb, _DIMS["tn"], preferred_element_type=F32)

        @pl.when(t == n_t - 1)
        def _():
            for k in (1, 2, 4):
                ar_ref[...] += pltpu.roll(ar_ref[...], k, axis=0)
                ai_ref[...] += pltpu.roll(ai_ref[...], k, axis=0)
            dar_ref[...] = ar_ref[...]
            dai_ref[...] = ai_ref[...]
            dbd_ref[...] = bacc_ref[...]
            dcd_ref[...] = cacc_ref[...]

    rev = lambda t: n_t - 1 - t
    row_spec = pl.BlockSpec((SCAN_T, SSM_W), lambda t: (rev(t), 0))
    x_spec = pl.BlockSpec((SCAN_T, ns), lambda t: (rev(t), 0))
    before_spec = pl.BlockSpec((8, ns), lambda t: (jnp.maximum(rev(t) * (SCAN_T // 8) - 1, 0), 0))
    du, dbd, dcd, dar, dai = pl.pallas_call(
        body, name=name, grid=(n_t,),
        in_specs=[row_spec, row_spec, pl.BlockSpec((SCAN_T, SSM_W), lambda t: (rev(t), _U_BLOCK)),
                  x_spec, x_spec, before_spec, before_spec,
                  _fb((2 * ns, SSM_W)), _fb((2 * ns, SSM_W)), _fb((N_SCAN_TABLES, 8, 2 * ns))],
        out_specs=[row_spec, _fb((2 * ns, SSM_W)), _fb((2 * ns, SSM_W)), _fb((8, ns)), _fb((8, ns))],
        out_shape=[_sds((s, SSM_W)), _sds((2 * ns, SSM_W)), _sds((2 * ns, SSM_W)), _sds((8, ns)), _sds((8, ns))],
        scratch_shapes=([pltpu.VMEM((8, ns), F32)] * 4 + [pltpu.VMEM((SCAN_T + 8, ns), F32)] * 2
                        + [pltpu.VMEM((SCAN_T, ns), F32)] * 4 + [pltpu.VMEM((2 * ns, SSM_W), F32)] * 2),
        compiler_params=_params(("arbitrary",)),
    )(dy1, dud, z, xr, xi, xr, xi, bdt, cd, tabs)
    return du, dbd, dcd, dar[0], dai[0]


def _group_ids():
    return lax.broadcasted_iota(jnp.int32, (1, SGU_W), 1) // 64


def _group_mean(val, gid):
    out = jnp.zeros_like(val)
    for g in range(SGU_GROUPS):
        mg = gid == g
        out = jnp.where(mg, jnp.sum(jnp.where(mg, val, 0.0), axis=1, keepdims=True) * (1.0 / 64), out)
    return out


def _causal_w(w_ref, g):
    t = lax.broadcasted_iota(jnp.int32, (SGU_CHUNK, SGU_CHUNK), 0)
    s = lax.broadcasted_iota(jnp.int32, (SGU_CHUNK, SGU_CHUNK), 1)
    return jnp.where(t >= s, w_ref[g], 0.0).astype(BF16)


def _sgu_core(x, lng, lnb, w_ref, bexp, gid):
    zz = _gelu(x)
    u = zz[:, :SGU_W]
    v = zz[:, SGU_W:]
    vc = v - _group_mean(v, gid)
    rstd = lax.rsqrt(_group_mean(vc * vc, gid) + EPS)
    vhat = vc * rstd
    vn = vhat * lng + lnb
    vnb = vn.astype(BF16)
    mixed = bexp
    for g in range(SGU_GROUPS):
        mm = jnp.dot(_causal_w(w_ref, g), vnb, preferred_element_type=F32)
        mixed = jnp.where(gid == g, mm + bexp, mixed)
    return u, rstd, vhat, vnb, mixed


def _sgu_fwd(z, lng, lnb, w, bexp, name, tm=512):
    s = z.shape[0]

    def body(z_ref, lng_ref, lnb_ref, w_ref, b_ref, o_ref):
        gid = _group_ids()
        for j in range(tm // SGU_CHUNK):
            rows = pl.ds(j * SGU_CHUNK, SGU_CHUNK)
            u, _, _, _, mixed = _sgu_core(z_ref[rows, :], lng_ref[...], lnb_ref[...], w_ref, b_ref[...], gid)
            o_ref[rows, :] = u * mixed

    return _rowcall(body, name, s, tm,
                    [_rb(tm, 512, 3), _fb((1, 256)), _fb((1, 256)), _fb((4, 128, 128)), _fb((128, 256))],
                    (z, lng.reshape(1, 256), lnb.reshape(1, 256), w, bexp), _rb(tm, 256), _sds((s, 256)))


def _sgu_bwd(z, dy, lng, lnb, w, bexp, name, tm=512):
    s = z.shape[0]

    def body(z_ref, dy_ref, lng_ref, lnb_ref, w_ref, b_ref, dz_ref, dw_ref, db_ref, dlng_ref, dlnb_ref):
        @pl.when(pl.program_id(0) == 0)
        def _():
            dw_ref[...] = jnp.zeros_like(dw_ref)
            db_ref[...] = jnp.zeros_like(db_ref)
            dlng_ref[...] = jnp.zeros_like(dlng_ref)
            dlnb_ref[...] = jnp.zeros_like(dlnb_ref)

        gid = _group_ids()
        t = lax.broadcasted_iota(jnp.int32, (SGU_CHUNK, SGU_CHUNK), 0)
        sidx = lax.broadcasted_iota(jnp.int32, (SGU_CHUNK, SGU_CHUNK), 1)
        lng_v = lng_ref[...]
        for j in range(tm // SGU_CHUNK):
            rows = pl.ds(j * SGU_CHUNK, SGU_CHUNK)
            x = z_ref[rows, :]
            u, rstd, vhat, vnb, mixed = _sgu_core(x, lng_v, lnb_ref[...], w_ref, b_ref[...], gid)
            dyv = dy_ref[rows, :]
            dmixed = dyv * u
            du = dyv * mixed
            db_ref[...] += dmixed
            dvn = jnp.zeros_like(dmixed)
            for g in range(SGU_GROUPS):
                dmg = jnp.where(gid == g, dmixed, 0.0).astype(BF16)
                dvn = dvn + lax.dot_general(_causal_w(w_ref, g), dmg, _DIMS["tn"], preferred_element_type=F32)
                dwg = lax.dot_general(dmg, vnb, _DIMS["nt"], preferred_element_type=F32)
                dw_ref[g] += jnp.where(t >= sidx, dwg, 0.0)
            dlnb_ref[...] += jnp.sum(dvn, axis=0, keepdims=True)
            dlng_ref[...] += jnp.sum(dvn * vhat, axis=0, keepdims=True)
            dvh = dvn * lng_v
            dv = rstd * (dvh - _group_mean(dvh, gid) - vhat * _group_mean(dvh * vhat, gid))
            gg = _gelu_grad(x)
            dz_ref[rows, 0:SGU_W] = du * gg[:, :SGU_W]
            dz_ref[rows, SGU_W:2 * SGU_W] = dv * gg[:, SGU_W:]

    dz, dw, db, dlng, dlnb = _rowcall(
        body, name, s, tm,
        [_rb(tm, 512, 3), _rb(tm, 256), _fb((1, 256)), _fb((1, 256)), _fb((4, 128, 128)), _fb((128, 256))],
        (z, dy, lng.reshape(1, 256), lnb.reshape(1, 256), w, bexp),
        [_rb(tm, 512), _fb((4, 128, 128)), _fb((128, 256)), _fb((1, 256)), _fb((1, 256))],
        [_sds((s, 512)), _sds((4, 128, 128)), _sds((128, 256)), _sds((1, 256)), _sds((1, 256))])
    return dz, dw, db, dlng.reshape(256), dlnb.reshape(256)


CONV_TC = 1408
N_CT = D_FF // CONV_TC


def _row_of(block8, j):
    r = lax.broadcasted_iota(jnp.int32, block8.shape, 0)
    return jnp.sum(jnp.where(r == j, block8, 0.0), axis=0, keepdims=True)


EDGE = 16


def _conv_fwd(hu, cw, cb, name, tm=512):
    s = hu.shape[0]

    def body(xv_ref, xg_ref, tv_ref, tg_ref, wv_ref, wg_ref, bv_ref, bg_ref, hv_ref, hg_ref, act_ref):
        has_prev = (pl.program_id(1) > 0).astype(F32)
        row = lax.broadcasted_iota(jnp.int32, (EDGE, CONV_TC), 0)

        def conv(x_ref, t_ref, w_ref, b_ref):
            x = x_ref[...].astype(F32)
            w0, w1, w2, bb = w_ref[0:1, :], w_ref[1:2, :], w_ref[2:3, :], b_ref[...]
            whole = w0 * pltpu.roll(x, 2, axis=0) + w1 * pltpu.roll(x, 1, axis=0) + w2 * x + bb
            tail = t_ref[...].astype(F32)
            r7 = _row_of(tail, EDGE - 1) * has_prev
            r6 = _row_of(tail, EDGE - 2) * has_prev
            xe = x_ref[0:EDGE, :].astype(F32)
            x1 = jnp.where(row == 0, r7, pltpu.roll(xe, 1, axis=0))
            x2 = jnp.where(row == 0, r6, jnp.where(row == 1, r7, pltpu.roll(xe, 2, axis=0)))
            return whole, w0 * x2 + w1 * x1 + w2 * xe + bb

        hv, hv_edge = conv(xv_ref, tv_ref, wv_ref, bv_ref)
        hg, hg_edge = conv(xg_ref, tg_ref, wg_ref, bg_ref)
        hv_ref[...] = hv.astype(BF16)
        hg_ref[...] = hg.astype(BF16)
        act_ref[...] = (_gelu(hg) * hv).astype(BF16)
        hv_ref[0:EDGE, :] = hv_edge.astype(BF16)
        hg_ref[0:EDGE, :] = hg_edge.astype(BF16)
        act_ref[0:EDGE, :] = (_gelu(hg_edge) * hv_edge).astype(BF16)

    def xs(off):
        return pl.BlockSpec((tm, CONV_TC), lambda j, i: (i, j + off))

    def ts(off):
        return pl.BlockSpec((EDGE, CONV_TC), lambda j, i: (jnp.maximum(i * (tm // EDGE) - 1, 0), j + off))

    def ws(rows, off):
        return pl.BlockSpec((rows, CONV_TC), lambda j, i: (0, j + off))

    o_spec = pl.BlockSpec((tm, CONV_TC), lambda j, i: (i, j))
    return pl.pallas_call(
        body, name=name, grid=(N_CT, s // tm),
        in_specs=[xs(0), xs(N_CT), ts(0), ts(N_CT), ws(3, 0), ws(3, N_CT), ws(1, 0), ws(1, N_CT)],
        out_specs=[o_spec] * 3, out_shape=[_sds((s, D_FF), BF16)] * 3,
        compiler_params=_params(("parallel", "arbitrary")),
    )(hu, hu, hu, hu, cw, cw, cb.reshape(1, 2 * D_FF), cb.reshape(1, 2 * D_FF))


HALO = EDGE


def _conv_bwd(dact, hv, hg, hu, cw, name, tm=512):
    s = dact.shape[0]

    def body(da_ref, dan_ref, hv_ref, hvn_ref, hg_ref, hgn_ref, x_ref, t_ref, w_ref, dx_ref, dw_ref, db_ref, d_scr):
        i = pl.program_id(1)
        is_value = pl.program_id(0) < N_CT

        @pl.when(i == 0)
        def _():
            dw_ref[...] = jnp.zeros_like(dw_ref)
            db_ref[...] = jnp.zeros_like(db_ref)

        for rows, (a_ref, v_ref, g_ref) in ((pl.ds(0, tm), (da_ref, hv_ref, hg_ref)),
                                            (pl.ds(tm, HALO), (dan_ref, hvn_ref, hgn_ref))):
            @pl.when(is_value)
            def _():
                d_scr[rows, :] = a_ref[...].astype(F32) * _gelu(g_ref[...].astype(F32))

            @pl.when(jnp.logical_not(is_value))
            def _():
                d_scr[rows, :] = (a_ref[...].astype(F32) * v_ref[...].astype(F32)
                                  * _gelu_grad(g_ref[...].astype(F32)))

        has_prev = (i > 0).astype(F32)
        has_next = (i < s // tm - 1).astype(F32)
        w0, w1, w2 = w_ref[0:1, :], w_ref[1:2, :], w_ref[2:3, :]
        d = d_scr[0:tm, :]
        dx_ref[...] = (w2 * d + w1 * pltpu.roll(d, tm - 1, axis=0) + w0 * pltpu.roll(d, tm - 2, axis=0)).astype(BF16)
        row = lax.broadcasted_iota(jnp.int32, (EDGE, CONV_TC), 0)
        nxt = d_scr[tm:tm + HALO, :]
        n0 = _row_of(nxt, 0) * has_next
        n1 = _row_of(nxt, 1) * has_next
        de = d_scr[tm - EDGE:tm, :]
        d1 = jnp.where(row == EDGE - 1, n0, pltpu.roll(de, EDGE - 1, axis=0))
        d2 = jnp.where(row == EDGE - 2, n0, jnp.where(row == EDGE - 1, n1, pltpu.roll(de, EDGE - 2, axis=0)))
        dx_ref[tm - EDGE:tm, :] = (w2 * de + w1 * d1 + w0 * d2).astype(BF16)
        x = x_ref[...].astype(F32)
        tail = t_ref[...].astype(F32)
        r7 = _row_of(tail, EDGE - 1) * has_prev
        r6 = _row_of(tail, EDGE - 2) * has_prev
        last = x_ref[tm - EDGE:tm, :].astype(F32)
        l7, l6 = _row_of(last, EDGE - 1), _row_of(last, EDGE - 2)
        head = d_scr[0:8, :]
        d0, d1h = _row_of(head, 0), _row_of(head, 1)
        dw_ref[0:1, :] += (jnp.sum(d * pltpu.roll(x, 2, axis=0), axis=0, keepdims=True)
                           + d0 * (r6 - l6) + d1h * (r7 - l7))
        dw_ref[1:2, :] += jnp.sum(d * pltpu.roll(x, 1, axis=0), axis=0, keepdims=True) + d0 * (r7 - l7)
        dw_ref[2:3, :] += jnp.sum(d * x, axis=0, keepdims=True)
        db_ref[...] += jnp.sum(d, axis=0, keepdims=True)

    a_spec = pl.BlockSpec((tm, CONV_TC), lambda j, i: (i, j % N_CT))
    an_spec = pl.BlockSpec((HALO, CONV_TC),
                           lambda j, i: (jnp.minimum((i + 1) * (tm // HALO), s // HALO - 1), j % N_CT))
    x_spec = pl.BlockSpec((tm, CONV_TC), lambda j, i: (i, j))
    t_spec = pl.BlockSpec((EDGE, CONV_TC), lambda j, i: (jnp.maximum(i * (tm // EDGE) - 1, 0), j))
    w_spec = pl.BlockSpec((3, CONV_TC), lambda j, i: (0, j))
    db_spec = pl.BlockSpec((1, CONV_TC), lambda j, i: (0, j))
    return pl.pallas_call(
        body, name=name, grid=(2 * N_CT, s // tm),
        in_specs=[a_spec, an_spec, a_spec, an_spec, a_spec, an_spec, x_spec, t_spec, w_spec],
        out_specs=[x_spec, w_spec, db_spec],
        out_shape=[_sds((s, 2 * D_FF), BF16), _sds((3, 2 * D_FF)), _sds((1, 2 * D_FF))],
        scratch_shapes=[pltpu.VMEM((tm + HALO, CONV_TC), F32)],
        compiler_params=_params(("parallel", "arbitrary")),
    )(dact, dact, hv, hv, hg, hg, hu, hu, cw)


def _ple_fwd(h, gp, pp, next_gain, name, tm=512):
    s, d = h.shape
    with_norm = next_gain is not None

    def body(*refs):
        h_ref, g_ref, p_ref = refs[:3]
        out = h_ref[...] + _sigmoid(g_ref[...].astype(F32)) * p_ref[...].astype(F32)
        if with_norm:
            n_ref, o_ref, a_ref = refs[3:]
            scale = lax.rsqrt(jnp.mean(out * out, axis=-1, keepdims=True) + EPS)
            a_ref[...] = (out * scale * n_ref[...]).astype(BF16)
        else:
            o_ref, = refs[3:]
        o_ref[...] = out

    if not with_norm:
        return _rowcall(body, name, s, tm, [_rb(tm, d)] * 3, (h, gp, pp), _rb(tm, d), _sds((s, d))), None
    return _rowcall(body, name, s, tm, [_rb(tm, d)] * 3 + [_fb((1, d))], (h, gp, pp, next_gain.reshape(1, d)),
                    [_rb(tm, d)] * 2, [_sds((s, d)), _sds((s, d), BF16)])


def _ple_bwd(dh, gp, pp, name, tm=512):
    s, d = dh.shape

    def body(d_ref, g_ref, p_ref, dp_ref, dg_ref):
        sg = _sigmoid(g_ref[...].astype(F32))
        dv = d_ref[...]
        dp_ref[...] = (dv * sg).astype(BF16)
        dg_ref[...] = (dv * p_ref[...].astype(F32) * sg * (1.0 - sg)).astype(BF16)

    return _rowcall(body, name, s, tm, [_rb(tm, d)] * 3, (dh, gp, pp), [_rb(tm, d)] * 2,
                    [_sds((s, d), BF16)] * 2)


SCALE = HEAD_DIM ** -0.5
ATT_ROWS = 2048


def _att_geom(s, dil):
    w = min(ATT_ROWS, s)
    p = BLK * dil
    assert w % p == 0 and s % w == 0
    return w, p, w // p


def _rows(start, dil):
    return pl.ds(start, BLK, stride=dil) if dil > 1 else pl.ds(start, BLK)


def _head_masks():
    lane = lax.broadcasted_iota(jnp.int32, (1, BLK), 1)
    return [lane < HEAD_DIM, lane >= HEAD_DIM]


def _band():
    rel = np.arange(BLK)[:, None] + BLK - np.arange(2 * BLK)[None, :]
    return (rel >= 0) & (rel <= BLK)


def _zcur(w):
    return lambda off: pl.BlockSpec((w, BLK), lambda hp, i: (i, off + hp))


def _zprev(p, nb):
    return lambda off: pl.BlockSpec((p, BLK), lambda hp, i: (jnp.maximum(i * nb - 1, 0), off + hp))


def _scur(w):
    return pl.BlockSpec((w, BLK), lambda hp, i: (i, hp))


def _pair_rows(t, masks):
    return jnp.concatenate([jnp.where(masks[0], t, 0.0), jnp.where(masks[1], t, 0.0)], axis=0).astype(BF16)


def _pair_bias_bwd(bias):
    return bias.reshape(4, 2, BLK, 2, BLK).transpose(0, 3, 2, 1, 4).reshape(4, 2, BLK, 2 * BLK)


def _unpair_bias_bwd(db):
    return db.reshape(4, 2, BLK, 2, BLK).transpose(0, 3, 2, 1, 4).reshape(N_HEADS, BLK, 2 * BLK)


def _attn_fwd(z, biases, name):
    s = z.shape[0]
    w = min(ATT_ROWS, s)
    n_br = len(BRANCHES)

    def body(*refs):
        q_ref, kp_ref, kc_ref, vp_ref, vc_ref = refs[:5]
        b_refs = refs[5:5 + n_br]
        y_ref, lse_ref, m_ref, l_ref, a_ref = refs[5 + n_br:]
        i = pl.program_id(1)
        masks = _head_masks()
        own_block = lax.broadcasted_iota(jnp.int32, (1, 2 * BLK), 1) >= BLK
        for n, (_, dil) in enumerate(BRANCHES):
            _, p, nb = _att_geom(s, dil)
            for r in range(dil):
                for b in range(nb):
                    rows = _rows(r + p * b, dil)
                    prev_rows = _rows(r + p * (b - 1), dil) if b > 0 else _rows(w - p + r, dil)
                    kprev, vprev = (kc_ref, vc_ref) if b > 0 else (kp_ref, vp_ref)
                    q = q_ref[rows, :] * SCALE
                    k = jnp.concatenate([kprev[prev_rows, :], kc_ref[rows, :]], axis=0).astype(BF16)
                    v = jnp.concatenate([vprev[prev_rows, :], vc_ref[rows, :]], axis=0).astype(BF16)
                    mb = lb = ob = None
                    for hh, mh in enumerate(masks):
                        qh = jnp.where(mh, q, 0.0).astype(BF16)
                        sc = lax.dot_general(qh, k, _DIMS["nt"], preferred_element_type=F32) + b_refs[n][hh]
                        if b == 0:
                            sc = jnp.where(own_block | (i > 0), sc, NEG_INF)
                        mx = jnp.max(sc, axis=1, keepdims=True)
                        e = jnp.exp(sc - mx)
                        den = jnp.sum(e, axis=1, keepdims=True)
                        o = jnp.dot(e.astype(BF16), v, preferred_element_type=F32)
                        if hh == 0:
                            mb = jnp.broadcast_to(mx, (BLK, BLK))
                            lb = jnp.broadcast_to(den, (BLK, BLK))
                            ob = o
                        else:
                            mb = jnp.where(mh, mx, mb)
                            lb = jnp.where(mh, den, lb)
                            ob = jnp.where(mh, o, ob)
                    if n == 0:
                        m_new, l_new, a_new = mb, lb, ob
                    else:
                        m_old = m_ref[rows, :]
                        m_new = jnp.maximum(m_old, mb)
                        al = jnp.exp(m_old - m_new)
                        be = jnp.exp(mb - m_new)
                        l_new = al * l_ref[rows, :] + be * lb
                        a_new = al * a_ref[rows, :] + be * ob
                    if n == n_br - 1:
                        y_ref[rows, :] = a_new / l_new
                        lse_ref[rows, :] = m_new + jnp.log(l_new)
                    else:
                        m_ref[rows, :] = m_new
                        l_ref[rows, :] = l_new
                        a_ref[rows, :] = a_new

    cur, prv = _zcur(w), _zprev(w, 1)
    b_spec = pl.BlockSpec((2, BLK, 2 * BLK), lambda hp, i: (hp, 0, 0))
    return pl.pallas_call(
        body, name=name, grid=(4, s // w), in_specs=[cur(0), prv(4), cur(4), prv(8), cur(8)] + [b_spec] * n_br,
        out_specs=[_scur(w)] * 2, out_shape=[_sds((s, ATTN_W))] * 2,
        scratch_shapes=[pltpu.VMEM((w, BLK), F32)] * 3,
        compiler_params=_params(("parallel", "parallel")),
    )(z, z, z, z, z, *biases)


def _row_stats(mh, dy, y, lse):
    delta = jnp.sum(jnp.where(mh, dy * y, 0.0), axis=1, keepdims=True)
    lse_h = jnp.max(jnp.where(mh, lse, NEG_INF), axis=1, keepdims=True)
    return delta, lse_h


def _attn_bwd(z, biases, dy, y, lse, name):
    s = z.shape[0]
    w = min(ATT_ROWS, s)
    n_steps = s // w
    n_br = len(BRANCHES)

    def body(*refs):
        q_ref, kp_ref, kc_ref, vp_ref, vc_ref, dy_ref, y_ref, lse_ref = refs[:8]
        b_refs = refs[8:8 + n_br]
        outs = refs[8 + n_br:]
        dq_ref, dk_ref, dv_ref = outs[:3]
        x_refs = outs[3:3 + 2 * n_br]
        db_refs = outs[3 + 2 * n_br:3 + 3 * n_br]
        acc_refs = outs[3 + 3 * n_br:]
        i = pl.program_id(1)

        @pl.when(i == 0)
        def _():
            for ref in db_refs:
                ref[...] = jnp.zeros_like(ref)

        masks = _head_masks()
        first_head = lax.broadcasted_iota(jnp.int32, (1, 2 * BLK), 1) < BLK

        sums = {"q": (acc_refs[0], dq_ref), "k": (acc_refs[1], dk_ref), "v": (acc_refs[2], dv_ref)}

        def add_up(n, rows, **vals):
            for key, val in vals.items():
                acc_ref, out_ref = sums[key]
                if n > 0:
                    val = val + acc_ref[rows, :]
                if n == n_br - 1:
                    out_ref[rows, :] = val
                else:
                    acc_ref[rows, :] = val

        for n, (_, dil) in enumerate(BRANCHES):
            _, p, nb = _att_geom(s, dil)
            b_ref, db_ref = b_refs[n], db_refs[n]
            dkx_ref, dvx_ref = x_refs[2 * n], x_refs[2 * n + 1]
            for r in range(dil):
                carry = None
                for b in range(nb):
                    rows = _rows(r + p * b, dil)
                    prev_rows = _rows(r + p * (b - 1), dil) if b > 0 else _rows(w - p + r, dil)
                    kprev, vprev = (kc_ref, vc_ref) if b > 0 else (kp_ref, vp_ref)
                    keys = [(_pair_rows(kprev[prev_rows, :], masks), _pair_rows(vprev[prev_rows, :], masks)),
                            (_pair_rows(kc_ref[rows, :], masks), _pair_rows(vc_ref[rows, :], masks))]
                    q = (q_ref[rows, :] * SCALE).astype(BF16)
                    dy_v = dy_ref[rows, :]
                    dyb = dy_v.astype(BF16)
                    stats = [_row_stats(mh, dy_v, y_ref[rows, :], lse_ref[rows, :]) for mh in masks]
                    delta = jnp.where(first_head, stats[0][0], stats[1][0])
                    lse_h = jnp.where(first_head, stats[0][1], stats[1][1])
                    dq = jnp.zeros((BLK, BLK), F32)
                    dk, dv = [], []
                    for half in range(2):
                        kh, vh = keys[half]
                        sc = lax.dot_general(q, kh, _DIMS["nt"], preferred_element_type=F32) + b_ref[half]
                        pr = jnp.exp(sc - lse_h)
                        if b == 0 and half == 0:
                            pr = pr * (i > 0).astype(F32)
                        dp = lax.dot_general(dyb, vh, _DIMS["nt"], preferred_element_type=F32)
                        ds = pr * (dp - delta)
                        db_ref[half] += ds
                        dsb = ds.astype(BF16)
                        dq = dq + jnp.dot(dsb, kh, preferred_element_type=F32)
                        dk2 = lax.dot_general(dsb, q, _DIMS["tn"], preferred_element_type=F32)
                        dv2 = lax.dot_general(pr.astype(BF16), dyb, _DIMS["tn"], preferred_element_type=F32)
                        dk.append(jnp.where(masks[0], dk2[:BLK], dk2[BLK:]))
                        dv.append(jnp.where(masks[0], dv2[:BLK], dv2[BLK:]))
                    add_up(n, rows, q=dq * SCALE)
                    if b > 0:
                        add_up(n, _rows(r + p * (b - 1), dil), k=carry[0] + dk[0], v=carry[1] + dv[0])
                    else:
                        dkx_ref[_rows(r, dil), :] = dk[0]
                        dvx_ref[_rows(r, dil), :] = dv[0]
                    carry = (dk[1], dv[1])
                add_up(n, _rows(r + p * (nb - 1), dil), k=carry[0], v=carry[1])

    cur, prv = _zcur(w), _zprev(w, 1)
    b_spec = pl.BlockSpec((None, 2, BLK, 2 * BLK), lambda hp, i: (hp, 0, 0, 0))
    x_specs, x_shapes = [], []
    for _, dil in BRANCHES:
        p = BLK * dil
        x_specs += [pl.BlockSpec((p, BLK), lambda hp, i: (i, hp))] * 2
        x_shapes += [_sds((n_steps * p, ATTN_W))] * 2
    outs = pl.pallas_call(
        body, name=name, grid=(4, n_steps),
        in_specs=[cur(0), prv(4), cur(4), prv(8), cur(8)] + [_scur(w)] * 3 + [b_spec] * n_br,
        out_specs=[_scur(w)] * 3 + x_specs + [b_spec] * n_br,
        out_shape=[_sds((s, ATTN_W))] * 3 + x_shapes + [_sds((4, 2, BLK, 2 * BLK))] * n_br,
        scratch_shapes=[pltpu.VMEM((w, BLK), F32)] * 3,
        compiler_params=_params(("parallel", "arbitrary")),
    )(z, z, z, z, z, dy, y, lse, *biases)
    dq, dk, dv = outs[:3]
    extras = [(outs[3 + 2 * n], outs[4 + 2 * n]) for n in range(n_br)]
    return dq, dk, dv, extras, [_unpair_bias_bwd(db) for db in outs[3 + 2 * n_br:]]


ASM_ROWS = 512


def _assemble_dz(dq, dk, dv, extras, dzs, du, name):
    s = dq.shape[0]
    w = min(ATT_ROWS, s)
    n_steps = s // w
    per_step = w // ASM_ROWS
    assert w % ASM_ROWS == 0

    def body(*refs):
        dq_ref, dk_ref, dv_ref, dzs_ref, du_ref = refs[:5]
        x_refs = refs[5:5 + 2 * len(extras)]
        o_ref, acc_ref = refs[-2:]
        j = pl.program_id(0)
        step = j // per_step
        has_next = (step < n_steps - 1).astype(F32)
        last_of_step = ((j + 1) % per_step == 0).astype(F32)
        o_ref[:, 0:ATTN_W] = dq_ref[...].astype(BF16)
        o_ref[:, 3 * ATTN_W:3 * ATTN_W + 2 * SGU_W] = dzs_ref[...].astype(BF16)
        o_ref[:, 3 * ATTN_W + 2 * SGU_W:IN_W] = du_ref[...].astype(BF16)
        for part, (base_ref, col) in enumerate(((dk_ref, ATTN_W), (dv_ref, 2 * ATTN_W))):
            acc_ref[...] = base_ref[...]
            for n, (_, dil) in enumerate(BRANCHES):
                rows = min(BLK * dil, ASM_ROWS)
                scale = has_next if BLK * dil >= w else has_next * last_of_step
                acc_ref[ASM_ROWS - rows:, :] += x_refs[2 * n + part][...] * scale
            o_ref[:, col:col + ATTN_W] = acc_ref[...].astype(BF16)

    def x_spec(dil):
        p = BLK * dil
        rows = min(p, ASM_ROWS)
        blocks_per_step = p // rows
        total = n_steps * blocks_per_step

        def idx(j):
            step = j // per_step
            within = (j % per_step) - (per_step - blocks_per_step)
            return (jnp.clip((step + 1) * blocks_per_step + jnp.maximum(within, 0), 0, total - 1), 0)

        return pl.BlockSpec((rows, ATTN_W), idx)

    in_specs = [_rb(ASM_ROWS, ATTN_W)] * 3 + [_rb(ASM_ROWS, 2 * SGU_W), _rb(ASM_ROWS, SSM_W)]
    args = [dq, dk, dv, dzs, du]
    for (dkx, dvx), (_, dil) in zip(extras, BRANCHES):
        in_specs += [x_spec(dil)] * 2
        args += [dkx, dvx]
    return pl.pallas_call(
        body, name=name, grid=(s // ASM_ROWS,), in_specs=in_specs, out_specs=_rb(ASM_ROWS, IN_W),
        out_shape=_sds((s, IN_W), BF16), scratch_shapes=[pltpu.VMEM((ASM_ROWS, ATTN_W), F32)],
        compiler_params=_params(("parallel",)),
    )(*args)


def _t5_bucket(dist):
    max_exact = N_BUCKETS // 2
    d = np.maximum(dist, 0)
    large = max_exact + (np.log(np.maximum(d, 1) / max_exact) / np.log(REL_MAX / max_exact)
                         * (N_BUCKETS - max_exact)).astype(np.int32)
    large = np.minimum(large, N_BUCKETS - 1)
    return np.where(d < max_exact, d, large).astype(np.int32)


def _bias_tables(rel_bias):
    period = 3 * BLK
    tabs = []
    for _, dil in BRANCHES:
        onehot = np.zeros((period, N_BUCKETS), np.float32)
        d = np.arange(BLK + 1)
        onehot[d, _t5_bucket((BLK - d) * dil)] = 1.0
        f = jnp.dot(jnp.asarray(onehot), rel_bias, precision=lax.Precision.HIGHEST)
        flat = jnp.tile(f.T, (1, BLK))[:, :BLK * (period - 1)]
        tab = flat.reshape(N_HEADS, BLK, period - 1)[:, :, :2 * BLK]
        tabs.append(jnp.where(_band()[None], tab, NEG_INF))
    return tabs


def _bucket_onehot():
    maps = []
    q = np.arange(BLK)[:, None]
    k = np.arange(2 * BLK)[None, :]
    rel = q + BLK - k
    for _, dil in BRANCHES:
        maps.append(np.where((rel >= 0) & (rel <= BLK), _t5_bucket(rel * dil), -1).reshape(-1))
    bmap = jnp.asarray(np.concatenate(maps).astype(np.int32))
    return (bmap[:, None] == jnp.arange(128, dtype=jnp.int32)[None, :]).astype(BF16)


def _block_diag(t):
    g, n, c = t.shape
    eye = jnp.eye(g, dtype=t.dtype)
    return (t[:, :, None, :] * eye[:, None, :, None]).reshape(g * n, g * c)


def _ssm_prep(a_re, a_im, log_dt, b_re, b_im, c_re, c_im):
    lam = lax.complex(a_re, a_im)
    dt = jnp.exp(log_dt)[:, None]
    a_bar = jnp.exp(lam * dt)
    b_bar = ((a_bar - 1.0) / lam)[:, :, None] * lax.complex(b_re, b_im)
    bdt = jnp.concatenate([_block_diag(jnp.real(b_bar)), _block_diag(jnp.imag(b_bar))], axis=0)
    cd = jnp.concatenate([_block_diag(jnp.transpose(c_re, (0, 2, 1))),
                          _block_diag(-jnp.transpose(c_im, (0, 2, 1)))], axis=0)
    return jnp.real(a_bar).reshape(-1), jnp.imag(a_bar).reshape(-1), bdt, cd


def _powers(ar, ai):
    pr, pi = ar[:, None], ai[:, None]
    k = 1
    while k < 8:
        lr, li = pr[:, -1:], pi[:, -1:]
        pr, pi = (jnp.concatenate([pr, pr * lr - pi * li], axis=1),
                  jnp.concatenate([pi, pr * li + pi * lr], axis=1))
        k *= 2
    return pr, pi


def _sgu_bias_expand(b):
    return jnp.repeat(b.T, 64, axis=1)


def _layer_fwd(i, h, a1, p_i, big, small, bias_tabs, next_gain):
    nm = "l%d_" % i
    sv = {"h": h}
    if a1 is None:
        a1 = _rms_fwd(h, small["norm_attn_g"][i], nm + "rms_attn")
    z = _mm(a1, big["w_in"], "nt", nm + "in_proj")
    y_attn, lse = _attn_fwd(z, [t[0] for t in bias_tabs], nm + "attn_fwd")
    bexp = _sgu_bias_expand(small["sgu_b"][i])
    y_sgu = _sgu_fwd(z, small["sgu_ln_g"][i], small["sgu_ln_b"][i], small["sgu_w"][i], bexp, nm + "sgu_fwd")
    ar, ai, bdt, cd = _ssm_prep(*[small[k][i] for k in ("ssm_a_re", "ssm_a_im", "ssm_log_dt", "ssm_b_re",
                                                         "ssm_b_im", "ssm_c_re", "ssm_c_im")])
    xr, xi, yc = _ssm_fwd(z, bdt.astype(BF16), cd.astype(BF16), _scan_tables(*_powers(ar, ai), False),
                          nm + "ssm_core")
    y_ssm = _ssm_post_fwd(yc, z, small["ssm_d"][i], big["ssm_glu_w"], small["ssm_glu_b"][i], nm + "ssm_post")
    mix = _mix_fwd(y_attn, y_sgu, y_ssm, small["branch_norm_g"][i], nm + "mix")
    if "rest" in big:
        big = dict({k: t for k, t in big.items() if k != "rest"}, **big["rest"](mix))
    h2, a2 = _mm(mix, big["w_out"], "nn", nm + "out_proj", add=h, norm_gain=small["norm_ffn_g"][i])
    hu = _mm(a2, big["ffn_w_up"], "nt", nm + "ffn_up", out_dtype=BF16)
    hv, hg, act = _conv_fwd(hu, big["ffn_conv_w"], small["ffn_conv_b"][i], nm + "ffn_conv")
    h3, a3 = _mm(act, big["ffn_w_down"], "nn", nm + "ffn_down", add=h2, norm_gain=small["norm_ple_g"][i])
    gp = _mm(a3, big["ple_w_gate"], "nn", nm + "ple_gate", out_dtype=BF16)
    pp = _mm(p_i, big["ple_w_proj"], "nt", nm + "ple_proj", out_dtype=BF16)
    h4, a_next = _ple_fwd(h3, gp, pp, next_gain, nm + "ple_add")
    sv.update(big=big, a1=a1, z=z, y_attn=y_attn, lse=lse, y_sgu=y_sgu, y_ssm=y_ssm, yc=yc, xr=xr, xi=xi, mix=mix, h2=h2,
              a2=a2, hu=hu, hv=hv, hg=hg, act=act, h3=h3, a3=a3, gp=gp, pp=pp)
    return h4, a_next, sv


def _layer_bwd(i, dh4, sv, p_i, big, small, bias_tabs, ffn_done=None):
    nm = "l%d_" % i
    g = {}
    dpp, dgp = _ple_bwd(dh4, sv["gp"], sv["pp"], nm + "ple_bwd")
    g["ple_w_proj"] = _mm(dpp, p_i, "tn", nm + "d_ple_proj", out_dtype=BF16)
    g["ple_w_gate"] = _mm(sv["a3"], dgp, "tn", nm + "d_ple_gate", out_dtype=BF16)
    dh3, dgain = _mm(dgp, big["ple_w_gate"], "nt", nm + "ple_gate_t", add=dh4,
                     norm_bwd=(sv["h3"], small["norm_ple_g"][i]))
    g["norm_ple_g"] = dgain.reshape(D_MODEL)
    g["ffn_w_down"] = _mm(sv["act"], dh3, "tn", nm + "d_ffn_down", out_dtype=BF16)
    dact = _mm(dh3, big["ffn_w_down"], "nt", nm + "ffn_down_t", out_dtype=BF16)
    dhu, g["ffn_conv_w"], dcb = _conv_bwd(dact, sv["hv"], sv["hg"], sv["hu"], big["ffn_conv_w"],
                                          nm + "ffn_conv_bwd")
    g["ffn_conv_b"] = dcb.reshape(2 * D_FF)
    g["ffn_w_up"] = _mm(dhu, sv["a2"], "tn", nm + "d_ffn_up", out_dtype=BF16)
    dh2, dgain = _mm(dhu, big["ffn_w_up"], "nn", nm + "ffn_up_t", add=dh3,
                     norm_bwd=(sv["h2"], small["norm_ffn_g"][i]))
    g["norm_ffn_g"] = dgain.reshape(D_MODEL)
    if ffn_done is not None:
        small = ffn_done(g, small)
    g["w_out"] = _mm(sv["mix"], dh2, "tn", nm + "d_out_proj", out_dtype=BF16)
    dmix = _mm(dh2, big["w_out"], "nt", nm + "out_proj_t")
    dya, dysg, dyss, g["branch_norm_g"] = _mix_bwd(dmix, sv["y_attn"], sv["y_sgu"], sv["y_ssm"],
                                                   small["branch_norm_g"][i], nm + "mix_bwd")
    ssm_keys = ("ssm_a_re", "ssm_a_im", "ssm_log_dt", "ssm_b_re", "ssm_b_im", "ssm_c_re", "ssm_c_im")
    (ar, ai, bdt, cd), prep_vjp = jax.vjp(_ssm_prep, *[small[k][i] for k in ssm_keys])
    dy1, dgl, y2, dud, g["ssm_d"], g["ssm_glu_b"] = _ssm_post_bwd(
        dyss, sv["yc"], sv["z"], small["ssm_d"][i], big["ssm_glu_w"], small["ssm_glu_b"][i], nm + "ssm_post_bwd")
    g["ssm_glu_w"] = _mm(y2, dgl, "tn", nm + "d_ssm_glu", out_dtype=BF16)
    du, dbdt, dcd, dar, dai = _ssm_bwd(dy1, dud, sv["z"], sv["xr"], sv["xi"], bdt.astype(BF16), cd.astype(BF16),
                                       _scan_tables(*_powers(ar, ai), True), nm + "ssm_core_bwd")
    for k, val in zip(ssm_keys, prep_vjp((dar, dai, dbdt, dcd))):
        g[k] = val
    bexp, bexp_vjp = jax.vjp(_sgu_bias_expand, small["sgu_b"][i])
    dzs, g["sgu_w"], dbexp, g["sgu_ln_g"], g["sgu_ln_b"] = _sgu_bwd(
        sv["z"], dysg, small["sgu_ln_g"][i], small["sgu_ln_b"][i], small["sgu_w"][i], bexp, nm + "sgu_bwd")
    g["sgu_b"] = bexp_vjp(dbexp)[0]
    dq, dk, dv, extras, dbs = _attn_bwd(sv["z"], [t[1] for t in bias_tabs], dya, sv["y_attn"], sv["lse"],
                                        nm + "attn_bwd")
    dbs = [db.reshape(N_HEADS, BLK * 2 * BLK) for db in dbs]
    dz = _assemble_dz(dq, dk, dv, extras, dzs, du, nm + "assemble_dz")
    g["w_in"] = _mm(dz, sv["a1"], "tn", nm + "d_in_proj", out_dtype=BF16)
    dh, dgain = _mm(dz, big["w_in"], "nn", nm + "in_proj_t", add=dh2, norm_bwd=(sv["h"], small["norm_attn_g"][i]))
    g["norm_attn_g"] = dgain.reshape(D_MODEL)
    return dh, g, jnp.concatenate(dbs, axis=1)


def _local_step(x, p, target, layer_weights, small, layer_done=None):
    depth = p.shape[0]
    bias_tabs = [(t, _pair_bias_bwd(t)) for t in _bias_tables(small["rel_bias"])]
    h, a1 = x, None
    saved = []
    for i in range(depth):
        next_gain = small["norm_attn_g"][i + 1] if i + 1 < depth else None
        h, a1, sv = _layer_fwd(i, h, a1, p[i], layer_weights(i, h), small, bias_tabs, next_gain)
        saved.append(sv)
    dh, loss, g_final = _loss_head(h, target, small["final_norm_g"], "loss_head")
    layer_grads = [None] * depth
    dbias = [None] * depth
    for i in reversed(range(depth)):
        ffn_done = None if layer_done is None else (lambda g, sm, i=i: layer_done(i, "ffn", g, sm))
        dh, layer_grads[i], dbias[i] = _layer_bwd(i, dh, saved[i], p[i], saved[i]["big"], small, bias_tabs,
                                                  ffn_done)
        if layer_done is not None:
            small = layer_done(i, "all", layer_grads[i], small)
    big_grads = [{k: lg.pop(k) for k in COMM_NAMES} for lg in layer_grads]
    grads = {k: jnp.stack([layer_grads[i][k] for i in range(depth)]) for k in layer_grads[0]}
    grads["final_norm_g"] = g_final
    g_rb = _mm(sum(dbias[1:], dbias[0]), _bucket_onehot(), "nn", "d_rel_bias", tk=2048)
    grads["rel_bias"] = g_rb[:, :N_BUCKETS].T
    return loss, dh, big_grads, grads


_ANY = pl.BlockSpec(memory_space=pl.ANY)
MESH_IDS = pl.DeviceIdType.MESH


def _slot(ref, axis, j):
    return ref.at[(slice(None),) * axis + (j,)]


def _all_gather(blocks, axis, name):
    nt = len(blocks)

    def body(*refs):
        x_refs, o_refs = refs[:nt], refs[nt:2 * nt]
        send_sems, recv_sems, local_sems = refs[2 * nt:]
        x, y, c = lax.axis_index("x"), lax.axis_index("y"), lax.axis_index("c")
        me, sibling = (x, y, c), (x, y, 1 - c)
        chips = [(1 - x, y), (x, 1 - y), (1 - x, 1 - y)]

        def slot(t, px, py, pc):
            return _slot(o_refs[t], axis, 4 * px + 2 * py + pc)

        def copy(t, k, blk, to, src=None):
            return pltpu.make_async_remote_copy(
                src_ref=slot(t, *blk) if src is None else src, dst_ref=slot(t, *blk),
                send_sem=send_sems.at[7 * t + k], recv_sem=recv_sems.at[7 * t + k],
                device_id=to, device_id_type=MESH_IDS)

        mine = [pltpu.make_async_copy(x_refs[t], slot(t, *me), local_sems.at[t]) for t in range(nt)]
        for cp in mine:
            cp.start()
        first = []
        for t in range(nt):
            first.append(copy(t, 0, me, sibling, src=x_refs[t]))
            first += [copy(t, 1 + j, me, (*chip, c), src=x_refs[t]) for j, chip in enumerate(chips)]
        for cp in first:
            cp.start()
        passed = []
        for t in range(nt):
            for j, chip in enumerate(chips):
                copy(t, 1 + j, (*chip, c), me).wait_recv()
                passed.append(copy(t, 4 + j, (*chip, c), sibling))
                passed[-1].start()
        for t in range(nt):
            copy(t, 0, sibling, me).wait_recv()
            for j, chip in enumerate(chips):
                copy(t, 4 + j, (*chip, 1 - c), me).wait_recv()
        for cp in first + passed:
            cp.wait_send()
        for cp in mine:
            cp.wait()

    out_shape = [jax.ShapeDtypeStruct(b.shape[:axis] + (N_DEV,) + b.shape[axis:], b.dtype) for b in blocks]
    return pl.pallas_call(
        body, name=name, out_shape=out_shape, in_specs=[_ANY] * nt, out_specs=[_ANY] * nt,
        scratch_shapes=[pltpu.SemaphoreType.DMA((7 * nt,)), pltpu.SemaphoreType.DMA((7 * nt,)),
                        pltpu.SemaphoreType.DMA((nt,))],
    )(*blocks)


def _peer(k):
    x, y, c = lax.axis_index("x"), lax.axis_index("y"), lax.axis_index("c")
    px = 1 - x if k & 4 else x
    py = 1 - y if k & 2 else y
    pc = 1 - c if k & 1 else c
    return (px, py, pc), 4 * px + 2 * py + pc


def _all_to_all(blocks, name):
    nt = len(blocks)

    def body(*refs):
        x_refs, o_refs = refs[:nt], refs[nt:2 * nt]
        send_sems, recv_sems, local_sems = refs[2 * nt:]
        _, me = _peer(0)
        mine = [pltpu.make_async_copy(x_refs[t].at[me], o_refs[t].at[me], local_sems.at[t]) for t in range(nt)]
        for cp in mine:
            cp.start()
        copies = []
        for k in range(1, N_DEV):
            peer, idx = _peer(k)
            for t in range(nt):
                cp = pltpu.make_async_remote_copy(
                    src_ref=x_refs[t].at[idx], dst_ref=o_refs[t].at[me],
                    send_sem=send_sems.at[7 * t + k - 1], recv_sem=recv_sems.at[7 * t + k - 1],
                    device_id=peer, device_id_type=MESH_IDS)
                cp.start()
                copies.append(cp)
        for cp in copies:
            cp.wait()
        for cp in mine:
            cp.wait()

    return pl.pallas_call(
        body, name=name, out_shape=[jax.ShapeDtypeStruct(b.shape, b.dtype) for b in blocks],
        in_specs=[_ANY] * nt, out_specs=[_ANY] * nt,
        scratch_shapes=[pltpu.SemaphoreType.DMA((7 * nt,)), pltpu.SemaphoreType.DMA((7 * nt,)),
                        pltpu.SemaphoreType.DMA((nt,))],
    )(*blocks)


_HBM = pl.BlockSpec(memory_space=pltpu.HBM)
_SEM = pl.BlockSpec(memory_space=pltpu.SEMAPHORE)
_EFFECT = pltpu.SideEffectType.DATAFLOW_SIDE_EFFECTING


def _split_copy(src_ref, land_ref, send_sems, recv_sems, t, k, gather):
    peer, idx = _peer(k)
    _, me = _peer(0)
    return pltpu.make_async_remote_copy(
        src_ref=src_ref if gather else src_ref.at[idx], dst_ref=land_ref.at[me],
        send_sem=send_sems.at[7 * t + k - 1], recv_sem=recv_sems.at[7 * t + k - 1],
        device_id=peer, device_id_type=MESH_IDS)


def _exchange_start(srcs, lands, gather, name):
    nt = len(srcs)

    def body(*refs):
        src_refs, land_refs = refs[:nt], refs[nt:2 * nt]
        send_sems, recv_sems = refs[2 * nt:2 * nt + 2]
        token = refs[-1]
        for k in range(1, N_DEV):
            for t in range(nt):
                _split_copy(src_refs[t], land_refs[t], send_sems, recv_sems, t, k, gather).start()
        token[...] = jnp.zeros_like(token)

    hbm = lambda a: pltpu.HBM(a.shape, a.dtype)
    outs = pl.pallas_call(
        body, name=name,
        out_shape=(pltpu.SemaphoreType.DMA((7 * nt,)), pltpu.SemaphoreType.DMA((7 * nt,)),
                   *[hbm(a) for a in srcs], *[hbm(a) for a in lands], jax.ShapeDtypeStruct((8, 128), F32)),
        in_specs=[_HBM] * (2 * nt),
        out_specs=(_SEM, _SEM, *[_HBM] * (2 * nt), pl.BlockSpec(memory_space=pltpu.VMEM)),
        input_output_aliases={j: 2 + j for j in range(2 * nt)},
        compiler_params=pltpu.CompilerParams(has_side_effects=_EFFECT),
    )(*[pltpu.with_memory_space_constraint(a, pltpu.HBM) for a in list(srcs) + list(lands)])
    return outs[0], outs[1], outs[2:2 + nt], outs[2 + nt:2 + 2 * nt], outs[-1]


def _exchange_wait(send_sems, recv_sems, srcs, lands, after, gather, name):
    nt = len(srcs)

    def body(*refs):
        src_refs, land_refs = refs[:nt], refs[nt:2 * nt]
        send_sems, recv_sems = refs[2 * nt:2 * nt + 2]
        for k in range(1, N_DEV):
            _, idx = _peer(k)
            for t in range(nt):
                _split_copy(src_refs[t], land_refs[t], send_sems, recv_sems, t, k, gather).wait_send()
                arrival = pltpu.make_async_remote_copy(
                    src_ref=land_refs[t].at[idx], dst_ref=land_refs[t].at[idx],
                    send_sem=send_sems.at[7 * t + k - 1], recv_sem=recv_sems.at[7 * t + k - 1],
                    device_id=_peer(k)[0], device_id_type=MESH_IDS)
                arrival.wait_recv()

    hbm = lambda a: pltpu.HBM(a.shape, a.dtype)
    outs = pl.pallas_call(
        body, name=name, out_shape=tuple(hbm(a) for a in list(srcs) + list(lands)),
        in_specs=[_HBM] * (2 * nt) + [_SEM, _SEM, _ANY], out_specs=tuple([_HBM] * (2 * nt)),
        input_output_aliases={j: j for j in range(2 * nt)},
        compiler_params=pltpu.CompilerParams(has_side_effects=_EFFECT),
    )(*srcs, *lands, send_sems, recv_sems, after)
    return outs[nt:]


def _adamw(parts, w, m, v, name, tr):
    n_layers, r, c_ = w.shape
    assert len(parts) == n_layers

    def body(*refs):
        p_refs = refs[:n_layers]
        w_ref, m_ref, v_ref, g_ref, d_ref, mo_ref, vo_ref = refs[n_layers:]

        def update(p_ref):
            g = p_ref[0].astype(F32)
            for j in range(1, N_DEV):
                g = g + p_ref[j].astype(F32)
            m2 = ADAM_B1 * m_ref[...] + (1.0 - ADAM_B1) * g
            v2 = ADAM_B2 * v_ref[...] + (1.0 - ADAM_B2) * (g * g)
            m_hat = m2 / (1.0 - ADAM_B1 ** ADAM_STEP)
            v_hat = v2 / (1.0 - ADAM_B2 ** ADAM_STEP)
            g_ref[...] = g
            d_ref[...] = -ADAM_LR * (m_hat / (jnp.sqrt(v_hat) + ADAM_EPS) + ADAM_WD * w_ref[...])
            mo_ref[...] = m2
            vo_ref[...] = v2

        for layer in range(n_layers):
            pl.when(pl.program_id(0) == layer)(lambda layer=layer: update(p_refs[layer]))

    spec = pl.BlockSpec((None, tr, c_), lambda l, i: (l, i, 0))
    p_spec = pl.BlockSpec((N_DEV, tr, c_), lambda l, i: (0, i, 0))
    return pl.pallas_call(
        body, name=name, grid=(n_layers, r // tr), in_specs=[p_spec] * n_layers + [spec] * 3,
        out_specs=[spec] * 4, out_shape=[_sds((n_layers, r, c_))] * 4,
        compiler_params=_params(("parallel", "parallel")),
    )(*parts, w, m, v)


def _pack_rows(n_elems, align):
    rows = -(-n_elems // PACK_COLS)
    return -(-rows // align) * align


def _pack(arrs, rows, dtype=F32):
    flat = jnp.concatenate([a.reshape(-1) for a in arrs]).astype(dtype)
    return jnp.pad(flat, (0, rows * PACK_COLS - flat.shape[0])).reshape(rows, PACK_COLS)


def _unpack(pack, shapes):
    flat = pack.reshape(-1)
    out, off = [], 0
    for shp in shapes:
        size = int(np.prod(shp))
        out.append(flat[off:off + size].reshape(shp))
        off += size
    return out


def _tile_rows(rows, target, align=16):
    best = align
    for t in range(align, target + 1, align):
        if rows % t == 0:
            best = t
    return best


COMM_NAMES = ("w_in", "ssm_glu_w", "w_out", "ffn_w_up", "ffn_w_down", "ple_w_gate", "ple_w_proj")
COMM_TRANSPOSED = ("w_in", "ffn_w_up", "ple_w_proj")
COMM_EARLY = ("ple_w_proj", "ple_w_gate", "ffn_w_down", "ffn_w_up")
COMM_LATE = ("w_in", "ssm_glu_w", "w_out")
SMALL_TILE_ROWS = 64
CONV_NAME = "ffn_conv_w"


def _to_comm(name, a):
    return jnp.swapaxes(a, 1, 2) if name in COMM_TRANSPOSED else a


def kernel(x, p, rel_bias, norm_attn_g, w_in, sgu_ln_g, sgu_ln_b, sgu_w, sgu_b, ssm_a_re, ssm_a_im, ssm_log_dt, ssm_b_re, ssm_b_im, ssm_c_re, ssm_c_im, ssm_d, ssm_glu_w, ssm_glu_b, branch_norm_g, w_out, norm_ffn_g, ffn_w_up, ffn_conv_w, ffn_conv_b, ffn_w_down, norm_ple_g, ple_w_gate, ple_w_proj, final_norm_g, loss_target, m_rel_bias, m_norm_attn_g, m_w_in, m_sgu_ln_g, m_sgu_ln_b, m_sgu_w, m_sgu_b, m_ssm_a_re, m_ssm_a_im, m_ssm_log_dt, m_ssm_b_re, m_ssm_b_im, m_ssm_c_re, m_ssm_c_im, m_ssm_d, m_ssm_glu_w, m_ssm_glu_b, m_branch_norm_g, m_w_out, m_norm_ffn_g, m_ffn_w_up, m_ffn_conv_w, m_ffn_conv_b, m_ffn_w_down, m_norm_ple_g, m_ple_w_gate, m_ple_w_proj, m_final_norm_g, v_rel_bias, v_norm_attn_g, v_w_in, v_sgu_ln_g, v_sgu_ln_b, v_sgu_w, v_sgu_b, v_ssm_a_re, v_ssm_a_im, v_ssm_log_dt, v_ssm_b_re, v_ssm_b_im, v_ssm_c_re, v_ssm_c_im, v_ssm_d, v_ssm_glu_w, v_ssm_glu_b, v_branch_norm_g, v_w_out, v_norm_ffn_g, v_ffn_w_up, v_ffn_conv_w, v_ffn_conv_b, v_ffn_w_down, v_norm_ple_g, v_ple_w_gate, v_ple_w_proj, v_final_norm_g):
    given = dict(locals())
    w = {n: given[n] for n in WEIGHT_NAMES}
    m = {n: given["m_" + n] for n in WEIGHT_NAMES}
    v = {n: given["v_" + n] for n in WEIGHT_NAMES}
    depth = p.shape[0]
    dev = 4 * lax.axis_index("x") + 2 * lax.axis_index("y") + lax.axis_index("c")

    wc = {n: _to_comm(n, w[n]) for n in COMM_NAMES}
    wb = {n: wc[n].astype(BF16) for n in COMM_NAMES}
    conv_local = [w[CONV_NAME], m[CONV_NAME], v[CONV_NAME]]
    conv_rows = _pack_rows(sum(int(np.prod(t.shape)) for t in conv_local), 8)
    conv_g, = _all_gather([_pack(conv_local, conv_rows)], 0, "gather_conv_taps")
    conv_parts = zip(*[_unpack(conv_g[j], [t.shape for t in conv_local]) for j in range(N_DEV)])
    conv_w, conv_m, conv_v = [jnp.concatenate(parts, axis=2) for parts in conv_parts]
    small = {n: w[n] for n in SMALL_NAMES}

    def whole(names, blocks):
        return {n: t.reshape(-1, t.shape[-1]) for n, t in zip(names, blocks)}

    def own_slot(block):
        return lax.dynamic_update_slice_in_dim(jnp.zeros((N_DEV,) + block.shape, block.dtype), block[None], dev, 0)

    def start_gather(names, i, after):
        srcs, after = lax.optimization_barrier(([wb[n][i] for n in names], after))
        return _exchange_start(srcs, [own_slot(s) for s in srcs], True, "gather_weights_%d_start" % i), after

    def wait_gather(names, i, started, after):
        send_sems, recv_sems, srcs, lands, _ = started
        return whole(names, _exchange_wait(send_sems, recv_sems, srcs, lands, after, True,
                                           "gather_weights_%d_wait" % i))

    at_once = ("w_in", "ssm_glu_w")
    later = tuple(n for n in COMM_NAMES if n not in at_once)
    w_in_0 = _all_gather([wb[n][0] for n in at_once], 0, "gather_w_in_0")
    gathering = {}
    gathering[0], (w_in_0, _) = start_gather(later, 0, (w_in_0, conv_g))
    small["norm_attn_g"] = small["norm_attn_g"] + gathering[0][4][0, 0]

    def layer_weights(i, h):
        if i > 0:
            got = wait_gather(COMM_NAMES, i, gathering.pop(i), h)
            if i + 1 < depth:
                gathering[i + 1], ordered = start_gather(COMM_NAMES, i + 1, got["w_in"])
                got["w_in"] = ordered + gathering[i + 1][4][0, 0].astype(BF16)
            return dict(got, **{CONV_NAME: conv_w[i]})

        def rest(after):
            got = wait_gather(later, 0, gathering.pop(0), after)
            if depth > 1:
                gathering[1], ordered = start_gather(COMM_NAMES, 1, got["w_out"])
                got["w_out"] = ordered + gathering[1][4][0, 0].astype(BF16)
            return got

        return dict(whole(at_once, w_in_0), **{CONV_NAME: conv_w[0], "rest": rest})

    def as_slots(g, n):
        return g.reshape((N_DEV,) + wc[n].shape[1:])

    scattering = {}

    def layer_done(i, stage, g, small_now):
        if stage == "all" and i == 0:
            return small_now
        names = COMM_EARLY if stage == "ffn" else COMM_LATE
        srcs = [as_slots(g[n], n) for n in names]
        lands = [own_slot(lax.dynamic_index_in_dim(s, dev, 0, keepdims=False)) for s in srcs]
        started = _exchange_start(srcs, lands, False, "scatter_weight_grads_%d_%s_start" % (i, stage))
        scattering[i, stage] = (names, started)
        pin = "branch_norm_g" if stage == "ffn" else "norm_ple_g"
        return dict(small_now, **{pin: small_now[pin] + started[4][0, 0]})

    loss, dx, big_grads, grads = _local_step(x[0], p[:, 0], loss_target[0], layer_weights, small, layer_done)
    loss = lax.psum(loss, ("x", "y", "c"))

    recv = [{} for _ in range(depth)]
    for (i, stage), (names, (send_sems, recv_sems, srcs, lands, _)) in scattering.items():
        got = _exchange_wait(send_sems, recv_sems, srcs, lands, dx, False,
                             "scatter_weight_grads_%d_%s_wait" % (i, stage))
        recv[i].update(zip(names, got))
    srcs = [as_slots(big_grads[0][n], n) for n in COMM_LATE]
    lands = [own_slot(lax.dynamic_index_in_dim(s, dev, 0, keepdims=False)) for s in srcs]
    last = _exchange_start(srcs, lands, False, "scatter_weight_grads_0_all_start")
    out = {}

    def update(n, pin=None):
        weight = wc[n] if pin is None else wc[n] + pin
        res = _adamw([recv[i][n] for i in range(depth)], weight, _to_comm(n, m[n]), _to_comm(n, v[n]),
                     "adamw_" + n, _tile_rows(wc[n].shape[1], 256))
        out[n] = [_to_comm(n, r) for r in res]

    for j, n in enumerate(COMM_EARLY):
        update(n, last[4][0, 0] if j == 0 else None)
    got = _exchange_wait(last[0], last[1], last[2], last[3], out[COMM_EARLY[-1]][0], False,
                         "scatter_weight_grads_0_all_wait")
    recv[0].update(zip(COMM_LATE, got))

    rep_names = SMALL_NAMES + (CONV_NAME,)
    rep_w = dict({n: w[n] for n in SMALL_NAMES}, **{CONV_NAME: conv_w})
    rep_m = dict({n: m[n] for n in SMALL_NAMES}, **{CONV_NAME: conv_m})
    rep_v = dict({n: v[n] for n in SMALL_NAMES}, **{CONV_NAME: conv_v})
    rep_shapes = [rep_w[n].shape for n in rep_names]
    rep_rows = _pack_rows(sum(int(np.prod(s)) for s in rep_shapes), SMALL_TILE_ROWS)
    rep_parts, = _all_gather([_pack([grads[n] for n in rep_names], rep_rows)], 0, "gather_small_grads")
    for n in COMM_LATE:
        update(n)
    rep_out = _adamw([rep_parts], *[_pack([src[n] for n in rep_names], rep_rows)[None] for src in (rep_w, rep_m, rep_v)],
                     "adamw_replicated", SMALL_TILE_ROWS)
    for n, vals in zip(rep_names, zip(*[_unpack(r[0], rep_shapes) for r in rep_out])):
        out[n] = list(vals)
    shard = ffn_conv_w.shape[2]
    out[CONV_NAME] = [lax.dynamic_slice_in_dim(t, dev * shard, shard, axis=2) for t in out[CONV_NAME]]
    results = [[out[n][kind] for n in WEIGHT_NAMES] for kind in range(4)]
    return (loss, dx[None], *results[0], *results[1], *results[2], *results[3])
```

```python
import math

import numpy as np
import jax
import jax.numpy as jnp
from jax import lax
from jax.experimental import pallas as pl
from jax.experimental.pallas import tpu as pltpu

F32 = jnp.float32
BF16 = jnp.bfloat16

D_MODEL = 1024
HEAD_DIM = 64
N_HEADS = 8
ATTN_W = 512
SGU_W = 256
SGU_GROUPS = 4
SGU_CHUNK = 128
SSM_W = 256
SSM_GROUPS = 16
SSM_CH = 16
SSM_STATE = 64
SSM_NS = SSM_GROUPS * SSM_STATE
IN_W = 2304
D_FF = 2816
PLE_DIM = 256
BRANCHES = ((128, 1), (512, 4), (2048, 16))
BLK = 128
N_BUCKETS = 32
REL_MAX = 2048
EPS = 1e-6
NEG_INF = -1e30
N_DEV = 8

ADAM_LR = 0.001
ADAM_B1 = 0.9
ADAM_B2 = 0.999
ADAM_EPS = 1e-08
ADAM_WD = 0.01
ADAM_STEP = 10

VMEM_LIMIT_BYTES = 56 * 1024 * 1024
GELU_C = math.sqrt(2.0 / math.pi)

SMALL_NAMES = ("rel_bias", "norm_attn_g", "sgu_ln_g", "sgu_ln_b", "sgu_w", "sgu_b", "ssm_a_re", "ssm_a_im",
               "ssm_log_dt", "ssm_b_re", "ssm_b_im", "ssm_c_re", "ssm_c_im", "ssm_d", "ssm_glu_b",
               "branch_norm_g", "norm_ffn_g", "ffn_conv_b", "norm_ple_g", "final_norm_g")
WEIGHT_NAMES = ("rel_bias", "norm_attn_g", "w_in", "sgu_ln_g", "sgu_ln_b", "sgu_w", "sgu_b", "ssm_a_re",
                "ssm_a_im", "ssm_log_dt", "ssm_b_re", "ssm_b_im", "ssm_c_re", "ssm_c_im", "ssm_d", "ssm_glu_w",
                "ssm_glu_b", "branch_norm_g", "w_out", "norm_ffn_g", "ffn_w_up", "ffn_conv_w", "ffn_conv_b",
                "ffn_w_down", "norm_ple_g", "ple_w_gate", "ple_w_proj", "final_norm_g")
PACK_COLS = 512


def _params(sem):
    return pltpu.CompilerParams(dimension_semantics=sem, vmem_limit_bytes=VMEM_LIMIT_BYTES)


def _pick(dim, target):
    if dim <= target:
        return dim
    best = None
    for t in range(128, target + 1, 128):
        if dim % t == 0:
            best = t
    return dim if best is None else best


def _gelu(x):
    return 0.5 * x * (1.0 + jnp.tanh(GELU_C * (x + 0.044715 * (x * x * x))))


def _gelu_grad(x):
    t = jnp.tanh(GELU_C * (x + 0.044715 * (x * x * x)))
    return 0.5 * (1.0 + t) + 0.5 * x * (1.0 - t * t) * (GELU_C * (1.0 + 3.0 * 0.044715 * (x * x)))


def _sigmoid(x):
    return 1.0 / (1.0 + jnp.exp(-x))


_DIMS = {"nn": (((1,), (0,)), ((), ())), "tn": (((0,), (0,)), ((), ())), "nt": (((1,), (1,)), ((), ()))}


MM_TILE = D_FF // 2


def _mm(a, b, mode, name, add=None, out_dtype=F32, norm_gain=None, norm_bwd=None, tm=MM_TILE, tn=MM_TILE,
        tk=MM_TILE):
    if mode == "nn":
        m, k = a.shape
        k2, n = b.shape
    elif mode == "tn":
        k, m = a.shape
        k2, n = b.shape
    else:
        m, k = a.shape
        n, k2 = b.shape
    assert k == k2, (name, a.shape, b.shape, mode)
    tm, tn, tk = _pick(m, tm), _pick(n, tn), _pick(k, tk)
    nk = k // tk
    dims = _DIMS[mode]
    has_add = add is not None
    has_norm = norm_gain is not None
    has_nbwd = norm_bwd is not None
    assert not (has_norm or has_nbwd) or tn == n

    def body(*refs):
        a_ref, b_ref = refs[:2]
        rest = list(refs[2:])
        add_ref = rest.pop(0) if has_add else None
        g_ref = rest.pop(0) if has_norm else None
        h_ref, hg_ref = (rest.pop(0), rest.pop(0)) if has_nbwd else (None, None)
        o_ref = rest.pop(0)
        n_ref = rest.pop(0) if has_norm else None
        dg_ref = rest.pop(0) if has_nbwd else None
        part = lax.dot_general(a_ref[...].astype(BF16), b_ref[...].astype(BF16), dims,
                               preferred_element_type=F32)
        if has_nbwd:
            @pl.when((pl.program_id(0) == 0) & (pl.program_id(2) == 0))
            def _():
                dg_ref[...] = jnp.zeros_like(dg_ref)

        def finish(r):
            if has_nbwd:
                x = h_ref[...]
                scale = lax.rsqrt(jnp.mean(x * x, axis=-1, keepdims=True) + EPS)
                xh = x * scale
                dg_ref[...] += jnp.sum(r * xh, axis=0, keepdims=True)
                dxh = r * hg_ref[...]
                r = scale * (dxh - xh * jnp.mean(dxh * xh, axis=-1, keepdims=True))
            if has_add:
                r = r + add_ref[...]
            o_ref[...] = r.astype(out_dtype)
            if has_norm:
                scale = lax.rsqrt(jnp.mean(r * r, axis=-1, keepdims=True) + EPS)
                n_ref[...] = (r * scale * g_ref[...]).astype(BF16)

        if nk == 1:
            finish(part)
            return
        acc_ref = refs[-1]
        kk = pl.program_id(2)

        @pl.when(kk == 0)
        def _():
            acc_ref[...] = part

        @pl.when((kk > 0) & (kk < nk - 1))
        def _():
            acc_ref[...] += part

        @pl.when(kk == nk - 1)
        def _():
            finish(acc_ref[...] + part)

    if mode == "tn":
        a_spec = pl.BlockSpec((tk, tm), lambda i, j, kk: (kk, i))
    else:
        a_spec = pl.BlockSpec((tm, tk), lambda i, j, kk: (i, kk))
    if mode == "nt":
        b_spec = pl.BlockSpec((tn, tk), lambda i, j, kk: (j, kk))
    else:
        b_spec = pl.BlockSpec((tk, tn), lambda i, j, kk: (kk, j))
    o_spec = pl.BlockSpec((tm, tn), lambda i, j, kk: (i, j))
    in_specs = [a_spec, b_spec] + ([o_spec] if has_add else [])
    args = (a, b) + ((add,) if has_add else ())
    out_specs, out_shape = o_spec, jax.ShapeDtypeStruct((m, n), out_dtype)
    if has_norm:
        in_specs.append(pl.BlockSpec((1, n), lambda i, j, kk: (0, 0)))
        args += (norm_gain.reshape(1, n),)
        out_specs, out_shape = [o_spec, o_spec], [out_shape, jax.ShapeDtypeStruct((m, n), BF16)]
    if has_nbwd:
        row_spec = pl.BlockSpec((1, n), lambda i, j, kk: (0, 0))
        in_specs += [o_spec, row_spec]
        args += (norm_bwd[0], norm_bwd[1].reshape(1, n))
        out_specs, out_shape = [o_spec, row_spec], [out_shape, jax.ShapeDtypeStruct((1, n), F32)]
    sem = ("arbitrary",) * 3 if has_nbwd else ("parallel", "parallel", "arbitrary")
    return pl.pallas_call(
        body, name=name, grid=(m // tm, n // tn, nk),
        in_specs=in_specs, out_specs=out_specs, out_shape=out_shape,
        scratch_shapes=[pltpu.VMEM((tm, tn), F32)] if nk > 1 else [], compiler_params=_params(sem),
    )(*args)


def _rb(tm, w, cb=0):
    return pl.BlockSpec((tm, w), lambda i: (i, cb))


def _fb(shape):
    nd = len(shape)
    return pl.BlockSpec(shape, lambda i: (0,) * nd)


def _rowcall(body, name, n_rows, tm, in_specs, args, out_specs, out_shapes):
    return pl.pallas_call(
        body, name=name, grid=(n_rows // tm,), in_specs=in_specs, out_specs=out_specs, out_shape=out_shapes,
        compiler_params=_params(("arbitrary",)),
    )(*args)


def _sds(shape, dtype=F32):
    return jax.ShapeDtypeStruct(shape, dtype)


def _rms_fwd(h, g, name, tm=1024):
    s, d = h.shape

    def body(h_ref, g_ref, o_ref):
        x = h_ref[...]
        r = lax.rsqrt(jnp.mean(x * x, axis=-1, keepdims=True) + EPS)
        o_ref[...] = (x * r * g_ref[...]).astype(BF16)

    return _rowcall(body, name, s, tm, [_rb(tm, d), _fb((1, d))], (h, g.reshape(1, d)), _rb(tm, d),
                    _sds((s, d), BF16))


def _loss_head(h, target, g, name, tm=1024):
    s, d = h.shape

    def body(h_ref, t_ref, g_ref, dh_ref, loss_ref, dg_ref):
        @pl.when(pl.program_id(0) == 0)
        def _():
            dg_ref[...] = jnp.zeros_like(dg_ref)
            loss_ref[...] = jnp.zeros_like(loss_ref)

        x = h_ref[...]
        r = lax.rsqrt(jnp.mean(x * x, axis=-1, keepdims=True) + EPS)
        xh = x * r
        gg = g_ref[...]
        err = xh * gg - t_ref[...]
        loss_ref[...] += jnp.sum(err * err) * (0.5 / d)
        dy = err * (1.0 / d)
        dg_ref[...] += jnp.sum(dy * xh, axis=0, keepdims=True)
        dxh = dy * gg
        dh_ref[...] = r * (dxh - xh * jnp.mean(dxh * xh, axis=-1, keepdims=True))

    dh, loss, dg = _rowcall(body, name, s, tm, [_rb(tm, d), _rb(tm, d), _fb((1, d))], (h, target, g.reshape(1, d)),
                            [_rb(tm, d), _fb((1, 128)), _fb((1, d))], [_sds((s, d)), _sds((1, 128)), _sds((1, d))])
    return dh, loss[0, 0], dg.reshape(d)


_MIX_PARTS = ((0, 512), (512, 768), (768, 1024))


def _mix_fwd(ya, ysg, yss, g, name, tm=1024):
    s = ya.shape[0]

    def body(a_ref, b_ref, c_ref, g_ref, o_ref):
        for ref, (lo, hi) in zip((a_ref, b_ref, c_ref), _MIX_PARTS):
            y = ref[...]
            r = lax.rsqrt(jnp.mean(y * y, axis=-1, keepdims=True) + EPS)
            o_ref[:, lo:hi] = (y * r * g_ref[:, lo:hi]).astype(BF16)

    return _rowcall(body, name, s, tm, [_rb(tm, 512), _rb(tm, 256), _rb(tm, 256), _fb((1, 1024))],
                    (ya, ysg, yss, g.reshape(1, 1024)), _rb(tm, 1024), _sds((s, 1024), BF16))


def _mix_bwd(dmix, ya, ysg, yss, g, name, tm=1024):
    s = ya.shape[0]

    def body(dm_ref, a_ref, b_ref, c_ref, g_ref, da_ref, db_ref, dc_ref, dg_ref):
        @pl.when(pl.program_id(0) == 0)
        def _():
            dg_ref[...] = jnp.zeros_like(dg_ref)

        for ref, dref, (lo, hi) in zip((a_ref, b_ref, c_ref), (da_ref, db_ref, dc_ref), _MIX_PARTS):
            y = ref[...]
            r = lax.rsqrt(jnp.mean(y * y, axis=-1, keepdims=True) + EPS)
            xh = y * r
            dm = dm_ref[:, lo:hi]
            dg_ref[:, lo:hi] += jnp.sum(dm * xh, axis=0, keepdims=True)
            dxh = dm * g_ref[:, lo:hi]
            dref[...] = r * (dxh - xh * jnp.mean(dxh * xh, axis=-1, keepdims=True))

    da, db, dc, dg = _rowcall(
        body, name, s, tm, [_rb(tm, 1024), _rb(tm, 512), _rb(tm, 256), _rb(tm, 256), _fb((1, 1024))],
        (dmix, ya, ysg, yss, g.reshape(1, 1024)),
        [_rb(tm, 512), _rb(tm, 256), _rb(tm, 256), _fb((1, 1024))],
        [_sds((s, 512)), _sds((s, 256)), _sds((s, 256)), _sds((1, 1024))])
    return da, db, dc, dg.reshape(1024)


def _ssm_post_fwd(yc, z, d, gw, gb, name, tm=1024):
    s = yc.shape[0]

    def body(yc_ref, u_ref, d_ref, gw_ref, gb_ref, o_ref):
        y1 = yc_ref[...] + d_ref[...] * u_ref[...]
        y2 = _gelu(y1)
        gl = jnp.dot(y2.astype(BF16), gw_ref[...], preferred_element_type=F32) + gb_ref[...]
        o_ref[...] = y2 * _sigmoid(gl)

    return _rowcall(body, name, s, tm, [_rb(tm, 256), _rb(tm, 256, 8), _fb((1, 256)), _fb((256, 256)), _fb((1, 256))],
                    (yc, z, d.reshape(1, 256), gw, gb.reshape(1, 256)), _rb(tm, 256), _sds((s, 256)))


def _ssm_post_bwd(dy, yc, z, d, gw, gb, name, tm=1024):
    s = yc.shape[0]

    def body(dy_ref, yc_ref, u_ref, d_ref, gw_ref, gb_ref, dy1_ref, dgl_ref, y2_ref, dud_ref, dd_ref, dgb_ref):
        @pl.when(pl.program_id(0) == 0)
        def _():
            dd_ref[...] = jnp.zeros_like(dd_ref)
            dgb_ref[...] = jnp.zeros_like(dgb_ref)

        u = u_ref[...]
        dd = d_ref[...]
        y1 = yc_ref[...] + dd * u
        y2 = _gelu(y1)
        gw_v = gw_ref[...]
        gl = jnp.dot(y2.astype(BF16), gw_v, preferred_element_type=F32) + gb_ref[...]
        sg = _sigmoid(gl)
        dyv = dy_ref[...]
        dgl = dyv * y2 * sg * (1.0 - sg)
        dy2 = dyv * sg + lax.dot_general(dgl.astype(BF16), gw_v, _DIMS["nt"], preferred_element_type=F32)
        dy1 = dy2 * _gelu_grad(y1)
        dy1_ref[...] = dy1.astype(BF16)
        dgl_ref[...] = dgl.astype(BF16)
        y2_ref[...] = y2.astype(BF16)
        dud_ref[...] = dy1 * dd
        dd_ref[...] += jnp.sum(dy1 * u, axis=0, keepdims=True)
        dgb_ref[...] += jnp.sum(dgl, axis=0, keepdims=True)

    outs = _rowcall(
        body, name, s, tm,
        [_rb(tm, 256), _rb(tm, 256), _rb(tm, 256, 8), _fb((1, 256)), _fb((256, 256)), _fb((1, 256))],
        (dy, yc, z, d.reshape(1, 256), gw, gb.reshape(1, 256)),
        [_rb(tm, 256)] * 4 + [_fb((1, 256))] * 2,
        [_sds((s, 256), BF16)] * 3 + [_sds((s, 256))] + [_sds((1, 256))] * 2)
    dy1, dgl, y2, dud, dd, dgb = outs
    return dy1, dgl, y2, dud, dd.reshape(256), dgb.reshape(256)


SCAN_T = 512
N_SCAN_TABLES = 6


def _scan_tables(pr, pi, reverse):
    ns = pr.shape[0]
    sign = -1.0 if reverse else 1.0
    power = [(jnp.ones((ns,), F32), jnp.zeros((ns,), F32))] + [(pr[:, k], sign * pi[:, k]) for k in range(8)]
    zero = (jnp.zeros((ns,), F32), jnp.zeros((ns,), F32))

    def table(exponents):
        rows = [zero if e is None else power[e] for e in exponents]
        return jnp.stack([jnp.concatenate(row) for row in rows])

    tabs = []
    for k in (1, 2, 4):
        has_partner = [(s < 8 - k) if reverse else (s >= k) for s in range(8)]
        tabs.append(table([k if ok else None for ok in has_partner]))
    tabs.append(table([s if reverse else 7 - s for s in range(8)]))
    tabs.append(table([8 - s if reverse else s + 1 for s in range(8)]))
    tabs.append(table([8] * 8))
    return jnp.stack(tabs)


def _cmul(ar, ai, br, bi):
    return ar * br - ai * bi, ar * bi + ai * br


def _scan_group(ur, ui, cr, ci, tr_ref, ti_ref, reverse):
    xr, xi = ur, ui
    for n, k in enumerate((1, 2, 4)):
        shift = 8 - k if reverse else k
        pr, pi = _cmul(tr_ref[n], ti_ref[n], pltpu.roll(xr, shift, axis=0), pltpu.roll(xi, shift, axis=0))
        xr, xi = xr + pr, xi + pi
    sr, si = _cmul(tr_ref[3], ti_ref[3], ur, ui)
    for k in (1, 2, 4):
        sr, si = sr + pltpu.roll(sr, k, axis=0), si + pltpu.roll(si, k, axis=0)
    pr, pi = _cmul(tr_ref[4], ti_ref[4], cr, ci)
    nr, ni = _cmul(tr_ref[5], ti_ref[5], cr, ci)
    return xr + pr, xi + pi, nr + sr, ni + si


def _table_halves(t_ref):
    return t_ref.at[:, :, pl.ds(0, SSM_NS)], t_ref.at[:, :, pl.ds(SSM_NS, SSM_NS)]


_U_BLOCK = (IN_W - SSM_W) // SSM_W


def _ssm_fwd(z, bdt, cd, tabs, name):
    s = z.shape[0]
    ns = SSM_NS
    n_t = s // SCAN_T

    def body(u_ref, b_ref, c_ref, t_ref, xr_ref, xi_ref, y_ref, cr_ref, ci_ref, ur_ref, ui_ref):
        @pl.when(pl.program_id(0) == 0)
        def _():
            cr_ref[...] = jnp.zeros_like(cr_ref)
            ci_ref[...] = jnp.zeros_like(ci_ref)

        bu = lax.dot_general(u_ref[...].astype(BF16), b_ref[...], _DIMS["nt"], preferred_element_type=F32)
        ur_ref[...] = bu[:, :ns]
        ui_ref[...] = bu[:, ns:]
        tr_ref, ti_ref = _table_halves(t_ref)

        def group(g, carry):
            rows = pl.ds(pl.multiple_of(g * 8, 8), 8)
            xr, xi, cr, ci = _scan_group(ur_ref[rows, :], ui_ref[rows, :], *carry, tr_ref, ti_ref, False)
            xr_ref[rows, :] = xr
            xi_ref[rows, :] = xi
            return cr, ci

        cr, ci = lax.fori_loop(0, SCAN_T // 8, group, (cr_ref[...], ci_ref[...]), unroll=2)
        cr_ref[...] = cr
        ci_ref[...] = ci
        y_ref[...] = (jnp.dot(xr_ref[...].astype(BF16), c_ref[0:ns, :], preferred_element_type=F32)
                      + jnp.dot(xi_ref[...].astype(BF16), c_ref[ns:, :], preferred_element_type=F32))

    x_spec = pl.BlockSpec((SCAN_T, ns), lambda t: (t, 0))
    return pl.pallas_call(
        body, name=name, grid=(n_t,),
        in_specs=[pl.BlockSpec((SCAN_T, SSM_W), lambda t: (t, _U_BLOCK)), _fb((2 * ns, SSM_W)),
                  _fb((2 * ns, SSM_W)), _fb((N_SCAN_TABLES, 8, 2 * ns))],
        out_specs=[x_spec, x_spec, _rb(SCAN_T, SSM_W)],
        out_shape=[_sds((s, ns)), _sds((s, ns)), _sds((s, SSM_W))],
        scratch_shapes=[pltpu.VMEM((8, ns), F32)] * 2 + [pltpu.VMEM((SCAN_T, ns), F32)] * 2,
        compiler_params=_params(("arbitrary",)),
    )(z, bdt, cd, tabs)


def _ssm_bwd(dy1, dud, z, xr, xi, bdt, cd, tabs, name):
    s = z.shape[0]
    ns = SSM_NS
    n_t = s // SCAN_T
    n_groups = SCAN_T // 8

    def body(dy_ref, dud_ref, u_ref, xr_ref, xi_ref, pxr_ref, pxi_ref, b_ref, c_ref, t_ref,
             du_ref, dbd_ref, dcd_ref, dar_ref, dai_ref,
             cr_ref, ci_ref, ar_ref, ai_ref, sxr_ref, sxi_ref, gr_ref, gi_ref, lr_ref, li_ref, bacc_ref, cacc_ref):
        t = pl.program_id(0)

        @pl.when(t == 0)
        def _():
            for ref in (cr_ref, ci_ref, ar_ref, ai_ref, bacc_ref, cacc_ref):
                ref[...] = jnp.zeros_like(ref)

        dyb = dy_ref[...]
        g = lax.dot_general(dyb, c_ref[...], _DIMS["nt"], preferred_element_type=F32)
        gr_ref[...] = g[:, :ns]
        gi_ref[...] = g[:, ns:]
        has_before = (t < n_t - 1).astype(F32)
        sxr_ref[0:8, :] = pxr_ref[...] * has_before
        sxi_ref[0:8, :] = pxi_ref[...] * has_before
        sxr_ref[8:, :] = xr_ref[...]
        sxi_ref[8:, :] = xi_ref[...]
        first_row = lax.broadcasted_iota(jnp.int32, (8, ns), 0) == 0
        tr_ref, ti_ref = _table_halves(t_ref)

        def group(k, carry):
            cr, ci, ar, ai = carry
            g8 = pl.multiple_of((n_groups - 1 - k) * 8, 8)
            rows = pl.ds(g8, 8)
            lr, li, cr, ci = _scan_group(gr_ref[rows, :], gi_ref[rows, :], cr, ci, tr_ref, ti_ref, True)
            lr_ref[rows, :] = lr
            li_ref[rows, :] = li
            here, before = pl.ds(g8 + 8, 8), rows
            pr = jnp.where(first_row, pltpu.roll(sxr_ref[before, :], 1, axis=0), pltpu.roll(sxr_ref[here, :], 1, axis=0))
            pi = jnp.where(first_row, pltpu.roll(sxi_ref[before, :], 1, axis=0), pltpu.roll(sxi_ref[here, :], 1, axis=0))
            return cr, ci, ar + lr * pr + li * pi, ai + li * pr - lr * pi

        cr, ci, ar, ai = lax.fori_loop(0, n_groups, group,
                                       (cr_ref[...], ci_ref[...], ar_ref[...], ai_ref[...]), unroll=2)
        cr_ref[...] = cr
        ci_ref[...] = ci
        ar_ref[...] = ar
        ai_ref[...] = ai
        lrb = lr_ref[...].astype(BF16)
        lib = li_ref[...].astype(BF16)
        ub = u_ref[...].astype(BF16)
        du_ref[...] = (dud_ref[...] + jnp.dot(lrb, b_ref[0:ns, :], preferred_element_type=F32)
                       + jnp.dot(lib, b_ref[ns:, :], preferred_element_type=F32))
        bacc_ref[0:ns, :] += lax.dot_general(lrb, ub, _DIMS["tn"], preferred_element_type=F32)
        bacc_ref[ns:, :] += lax.dot_general(lib, ub, _DIMS["tn"], preferred_element_type=F32)
        cacc_ref[0:ns, :] += lax.dot_general(xr_ref[...].astype(BF16), dyb, _DIMS["tn"], preferred_element_type=F32)
        cacc_ref[ns:, :] += lax.dot_general(xi_ref[...].astype(BF16), dyb, _DIMS["tn"], preferred_element_type=F32)

        @pl.when(t == n_t - 1)
        def _():
            for k in (1, 2, 4):
                ar_ref[...] += pltpu.roll(ar_ref[...], k, axis=0)
                ai_ref[...] += pltpu.roll(ai_ref[...], k, axis=0)
            dar_ref[...] = ar_ref[...]
            dai_ref[...] = ai_ref[...]
            dbd_ref[...] = bacc_ref[...]
            dcd_ref[...] = cacc_ref[...]

    rev = lambda t: n_t - 1 - t
    row_spec = pl.BlockSpec((SCAN_T, SSM_W), lambda t: (rev(t), 0))
    x_spec = pl.BlockSpec((SCAN_T, ns), lambda t: (rev(t), 0))
    before_spec = pl.BlockSpec((8, ns), lambda t: (jnp.maximum(rev(t) * (SCAN_T // 8) - 1, 0), 0))
    du, dbd, dcd, dar, dai = pl.pallas_call(
        body, name=name, grid=(n_t,),
        in_specs=[row_spec, row_spec, pl.BlockSpec((SCAN_T, SSM_W), lambda t: (rev(t), _U_BLOCK)),
                  x_spec, x_spec, before_spec, before_spec,
                  _fb((2 * ns, SSM_W)), _fb((2 * ns, SSM_W)), _fb((N_SCAN_TABLES, 8, 2 * ns))],
        out_specs=[row_spec, _fb((2 * ns, SSM_W)), _fb((2 * ns, SSM_W)), _fb((8, ns)), _fb((8, ns))],
        out_shape=[_sds((s, SSM_W)), _sds((2 * ns, SSM_W)), _sds((2 * ns, SSM_W)), _sds((8, ns)), _sds((8, ns))],
        scratch_shapes=([pltpu.VMEM((8, ns), F32)] * 4 + [pltpu.VMEM((SCAN_T + 8, ns), F32)] * 2
                        + [pltpu.VMEM((SCAN_T, ns), F32)] * 4 + [pltpu.VMEM((2 * ns, SSM_W), F32)] * 2),
        compiler_params=_params(("arbitrary",)),
    )(dy1, dud, z, xr, xi, xr, xi, bdt, cd, tabs)
    return du, dbd, dcd, dar[0], dai[0]


def _group_ids():
    return lax.broadcasted_iota(jnp.int32, (1, SGU_W), 1) // 64


def _group_mean(val, gid):
    out = jnp.zeros_like(val)
    for g in range(SGU_GROUPS):
        mg = gid == g
        out = jnp.where(mg, jnp.sum(jnp.where(mg, val, 0.0), axis=1, keepdims=True) * (1.0 / 64), out)
    return out


def _causal_w(w_ref, g):
    t = lax.broadcasted_iota(jnp.int32, (SGU_CHUNK, SGU_CHUNK), 0)
    s = lax.broadcasted_iota(jnp.int32, (SGU_CHUNK, SGU_CHUNK), 1)
    return jnp.where(t >= s, w_ref[g], 0.0).astype(BF16)


def _sgu_core(x, lng, lnb, w_ref, bexp, gid):
    zz = _gelu(x)
    u = zz[:, :SGU_W]
    v = zz[:, SGU_W:]
    vc = v - _group_mean(v, gid)
    rstd = lax.rsqrt(_group_mean(vc * vc, gid) + EPS)
    vhat = vc * rstd
    vn = vhat * lng + lnb
    vnb = vn.astype(BF16)
    mixed = bexp
    for g in range(SGU_GROUPS):
        mm = jnp.dot(_causal_w(w_ref, g), vnb, preferred_element_type=F32)
        mixed = jnp.where(gid == g, mm + bexp, mixed)
    return u, rstd, vhat, vnb, mixed


def _sgu_fwd(z, lng, lnb, w, bexp, name, tm=512):
    s = z.shape[0]

    def body(z_ref, lng_ref, lnb_ref, w_ref, b_ref, o_ref):
        gid = _group_ids()
        for j in range(tm // SGU_CHUNK):
            rows = pl.ds(j * SGU_CHUNK, SGU_CHUNK)
            u, _, _, _, mixed = _sgu_core(z_ref[rows, :], lng_ref[...], lnb_ref[...], w_ref, b_ref[...], gid)
            o_ref[rows, :] = u * mixed

    return _rowcall(body, name, s, tm,
                    [_rb(tm, 512, 3), _fb((1, 256)), _fb((1, 256)), _fb((4, 128, 128)), _fb((128, 256))],
                    (z, lng.reshape(1, 256), lnb.reshape(1, 256), w, bexp), _rb(tm, 256), _sds((s, 256)))


def _sgu_bwd(z, dy, lng, lnb, w, bexp, name, tm=512):
    s = z.shape[0]

    def body(z_ref, dy_ref, lng_ref, lnb_ref, w_ref, b_ref, dz_ref, dw_ref, db_ref, dlng_ref, dlnb_ref):
        @pl.when(pl.program_id(0) == 0)
        def _():
            dw_ref[...] = jnp.zeros_like(dw_ref)
            db_ref[...] = jnp.zeros_like(db_ref)
            dlng_ref[...] = jnp.zeros_like(dlng_ref)
            dlnb_ref[...] = jnp.zeros_like(dlnb_ref)

        gid = _group_ids()
        t = lax.broadcasted_iota(jnp.int32, (SGU_CHUNK, SGU_CHUNK), 0)
        sidx = lax.broadcasted_iota(jnp.int32, (SGU_CHUNK, SGU_CHUNK), 1)
        lng_v = lng_ref[...]
        for j in range(tm // SGU_CHUNK):
            rows = pl.ds(j * SGU_CHUNK, SGU_CHUNK)
            x = z_ref[rows, :]
            u, rstd, vhat, vnb, mixed = _sgu_core(x, lng_v, lnb_ref[...], w_ref, b_ref[...], gid)
            dyv = dy_ref[rows, :]
            dmixed = dyv * u
            du = dyv * mixed
            db_ref[...] += dmixed
            dvn = jnp.zeros_like(dmixed)
            for g in range(SGU_GROUPS):
                dmg = jnp.where(gid == g, dmixed, 0.0).astype(BF16)
                dvn = dvn + lax.dot_general(_causal_w(w_ref, g), dmg, _DIMS["tn"], preferred_element_type=F32)
                dwg = lax.dot_general(dmg, vnb, _DIMS["nt"], preferred_element_type=F32)
                dw_ref[g] += jnp.where(t >= sidx, dwg, 0.0)
            dlnb_ref[...] += jnp.sum(dvn, axis=0, keepdims=True)
            dlng_ref[...] += jnp.sum(dvn * vhat, axis=0, keepdims=True)
            dvh = dvn * lng_v
            dv = rstd * (dvh - _group_mean(dvh, gid) - vhat * _group_mean(dvh * vhat, gid))
            gg = _gelu_grad(x)
            dz_ref[rows, 0:SGU_W] = du * gg[:, :SGU_W]
            dz_ref[rows, SGU_W:2 * SGU_W] = dv * gg[:, SGU_W:]

    dz, dw, db, dlng, dlnb = _rowcall(
        body, name, s, tm,
        [_rb(tm, 512, 3), _rb(tm, 256), _fb((1, 256)), _fb((1, 256)), _fb((4, 128, 128)), _fb((128, 256))],
        (z, dy, lng.reshape(1, 256), lnb.reshape(1, 256), w, bexp),
        [_rb(tm, 512), _fb((4, 128, 128)), _fb((128, 256)), _fb((1, 256)), _fb((1, 256))],
        [_sds((s, 512)), _sds((4, 128, 128)), _sds((128, 256)), _sds((1, 256)), _sds((1, 256))])
    return dz, dw, db, dlng.reshape(256), dlnb.reshape(256)


CONV_TC = 1408
N_CT = D_FF // CONV_TC


def _row_of(block8, j):
    r = lax.broadcasted_iota(jnp.int32, block8.shape, 0)
    return jnp.sum(jnp.where(r == j, block8, 0.0), axis=0, keepdims=True)


EDGE = 16


def _conv_fwd(hu, cw, cb, name, tm=512):
    s = hu.shape[0]

    def body(xv_ref, xg_ref, tv_ref, tg_ref, wv_ref, wg_ref, bv_ref, bg_ref, hv_ref, hg_ref, act_ref):
        has_prev = (pl.program_id(1) > 0).astype(F32)
        row = lax.broadcasted_iota(jnp.int32, (EDGE, CONV_TC), 0)

        def conv(x_ref, t_ref, w_ref, b_ref):
            x = x_ref[...].astype(F32)
            w0, w1, w2, bb = w_ref[0:1, :], w_ref[1:2, :], w_ref[2:3, :], b_ref[...]
            whole = w0 * pltpu.roll(x, 2, axis=0) + w1 * pltpu.roll(x, 1, axis=0) + w2 * x + bb
            tail = t_ref[...].astype(F32)
            r7 = _row_of(tail, EDGE - 1) * has_prev
            r6 = _row_of(tail, EDGE - 2) * has_prev
            xe = x_ref[0:EDGE, :].astype(F32)
            x1 = jnp.where(row == 0, r7, pltpu.roll(xe, 1, axis=0))
            x2 = jnp.where(row == 0, r6, jnp.where(row == 1, r7, pltpu.roll(xe, 2, axis=0)))
            return whole, w0 * x2 + w1 * x1 + w2 * xe + bb

        hv, hv_edge = conv(xv_ref, tv_ref, wv_ref, bv_ref)
        hg, hg_edge = conv(xg_ref, tg_ref, wg_ref, bg_ref)
        hv_ref[...] = hv.astype(BF16)
        hg_ref[...] = hg.astype(BF16)
        act_ref[...] = (_gelu(hg) * hv).astype(BF16)
        hv_ref[0:EDGE, :] = hv_edge.astype(BF16)
        hg_ref[0:EDGE, :] = hg_edge.astype(BF16)
        act_ref[0:EDGE, :] = (_gelu(hg_edge) * hv_edge).astype(BF16)

    def xs(off):
        return pl.BlockSpec((tm, CONV_TC), lambda j, i: (i, j + off))

    def ts(off):
        return pl.BlockSpec((EDGE, CONV_TC), lambda j, i: (jnp.maximum(i * (tm // EDGE) - 1, 0), j + off))

    def ws(rows, off):
        return pl.BlockSpec((rows, CONV_TC), lambda j, i: (0, j + off))

    o_spec = pl.BlockSpec((tm, CONV_TC), lambda j, i: (i, j))
    return pl.pallas_call(
        body, name=name, grid=(N_CT, s // tm),
        in_specs=[xs(0), xs(N_CT), ts(0), ts(N_CT), ws(3, 0), ws(3, N_CT), ws(1, 0), ws(1, N_CT)],
        out_specs=[o_spec] * 3, out_shape=[_sds((s, D_FF), BF16)] * 3,
        compiler_params=_params(("parallel", "arbitrary")),
    )(hu, hu, hu, hu, cw, cw, cb.reshape(1, 2 * D_FF), cb.reshape(1, 2 * D_FF))


HALO = EDGE


def _conv_bwd(dact, hv, hg, hu, cw, name, tm=512):
    s = dact.shape[0]

    def body(da_ref, dan_ref, hv_ref, hvn_ref, hg_ref, hgn_ref, x_ref, t_ref, w_ref, dx_ref, dw_ref, db_ref, d_scr):
        i = pl.program_id(1)
        is_value = pl.program_id(0) < N_CT

        @pl.when(i == 0)
        def _():
            dw_ref[...] = jnp.zeros_like(dw_ref)
            db_ref[...] = jnp.zeros_like(db_ref)

        for rows, (a_ref, v_ref, g_ref) in ((pl.ds(0, tm), (da_ref, hv_ref, hg_ref)),
                                            (pl.ds(tm, HALO), (dan_ref, hvn_ref, hgn_ref))):
            @pl.when(is_value)
            def _():
                d_scr[rows, :] = a_ref[...].astype(F32) * _gelu(g_ref[...].astype(F32))

            @pl.when(jnp.logical_not(is_value))
            def _():
                d_scr[rows, :] = (a_ref[...].astype(F32) * v_ref[...].astype(F32)
                                  * _gelu_grad(g_ref[...].astype(F32)))

        has_prev = (i > 0).astype(F32)
        has_next = (i < s // tm - 1).astype(F32)
        w0, w1, w2 = w_ref[0:1, :], w_ref[1:2, :], w_ref[2:3, :]
        d = d_scr[0:tm, :]
        dx_ref[...] = (w2 * d + w1 * pltpu.roll(d, tm - 1, axis=0) + w0 * pltpu.roll(d, tm - 2, axis=0)).astype(BF16)
        row = lax.broadcasted_iota(jnp.int32, (EDGE, CONV_TC), 0)
        nxt = d_scr[tm:tm + HALO, :]
        n0 = _row_of(nxt, 0) * has_next
        n1 = _row_of(nxt, 1) * has_next
        de = d_scr[tm - EDGE:tm, :]
        d1 = jnp.where(row == EDGE - 1, n0, pltpu.roll(de, EDGE - 1, axis=0))
        d2 = jnp.where(row == EDGE - 2, n0, jnp.where(row == EDGE - 1, n1, pltpu.roll(de, EDGE - 2, axis=0)))
        dx_ref[tm - EDGE:tm, :] = (w2 * de + w1 * d1 + w0 * d2).astype(BF16)
        x = x_ref[...].astype(F32)
        tail = t_ref[...].astype(F32)
        r7 = _row_of(tail, EDGE - 1) * has_prev
        r6 = _row_of(tail, EDGE - 2) * has_prev
        last = x_ref[tm - EDGE:tm, :].astype(F32)
        l7, l6 = _row_of(last, EDGE - 1), _row_of(last, EDGE - 2)
        head = d_scr[0:8, :]
        d0, d1h = _row_of(head, 0), _row_of(head, 1)
        dw_ref[0:1, :] += (jnp.sum(d * pltpu.roll(x, 2, axis=0), axis=0, keepdims=True)
                           + d0 * (r6 - l6) + d1h * (r7 - l7))
        dw_ref[1:2, :] += jnp.sum(d * pltpu.roll(x, 1, axis=0), axis=0, keepdims=True) + d0 * (r7 - l7)
        dw_ref[2:3, :] += jnp.sum(d * x, axis=0, keepdims=True)
        db_ref[...] += jnp.sum(d, axis=0, keepdims=True)

    a_spec = pl.BlockSpec((tm, CONV_TC), lambda j, i: (i, j % N_CT))
    an_spec = pl.BlockSpec((HALO, CONV_TC),
                           lambda j, i: (jnp.minimum((i + 1) * (tm // HALO), s // HALO - 1), j % N_CT))
    x_spec = pl.BlockSpec((tm, CONV_TC), lambda j, i: (i, j))
    t_spec = pl.BlockSpec((EDGE, CONV_TC), lambda j, i: (jnp.maximum(i * (tm // EDGE) - 1, 0), j))
    w_spec = pl.BlockSpec((3, CONV_TC), lambda j, i: (0, j))
    db_spec = pl.BlockSpec((1, CONV_TC), lambda j, i: (0, j))
    return pl.pallas_call(
        body, name=name, grid=(2 * N_CT, s // tm),
        in_specs=[a_spec, an_spec, a_spec, an_spec, a_spec, an_spec, x_spec, t_spec, w_spec],
        out_specs=[x_spec, w_spec, db_spec],
        out_shape=[_sds((s, 2 * D_FF), BF16), _sds((3, 2 * D_FF)), _sds((1, 2 * D_FF))],
        scratch_shapes=[pltpu.VMEM((tm + HALO, CONV_TC), F32)],
        compiler_params=_params(("parallel", "arbitrary")),
    )(dact, dact, hv, hv, hg, hg, hu, hu, cw)


def _ple_fwd(h, gp, pp, next_gain, name, tm=1024):
    s, d = h.shape
    with_norm = next_gain is not None

    def body(*refs):
        h_ref, g_ref, p_ref = refs[:3]
        out = h_ref[...] + _sigmoid(g_ref[...].astype(F32)) * p_ref[...].astype(F32)
        if with_norm:
            n_ref, o_ref, a_ref = refs[3:]
            scale = lax.rsqrt(jnp.mean(out * out, axis=-1, keepdims=True) + EPS)
            a_ref[...] = (out * scale * n_ref[...]).astype(BF16)
        else:
            o_ref, = refs[3:]
        o_ref[...] = out

    if not with_norm:
        return _rowcall(body, name, s, tm, [_rb(tm, d)] * 3, (h, gp, pp), _rb(tm, d), _sds((s, d))), None
    return _rowcall(body, name, s, tm, [_rb(tm, d)] * 3 + [_fb((1, d))], (h, gp, pp, next_gain.reshape(1, d)),
                    [_rb(tm, d)] * 2, [_sds((s, d)), _sds((s, d), BF16)])


def _ple_bwd(dh, gp, pp, name, tm=1024):
    s, d = dh.shape

    def body(d_ref, g_ref, p_ref, dp_ref, dg_ref):
        sg = _sigmoid(g_ref[...].astype(F32))
        dv = d_ref[...]
        dp_ref[...] = (dv * sg).astype(BF16)
        dg_ref[...] = (dv * p_ref[...].astype(F32) * sg * (1.0 - sg)).astype(BF16)

    return _rowcall(body, name, s, tm, [_rb(tm, d)] * 3, (dh, gp, pp), [_rb(tm, d)] * 2,
                    [_sds((s, d), BF16)] * 2)


SCALE = HEAD_DIM ** -0.5
ATT_ROWS = 2048


def _att_geom(s, dil):
    w = min(ATT_ROWS, s)
    p = BLK * dil
    assert w % p == 0 and s % w == 0
    return w, p, w // p


def _rows(start, dil):
    return pl.ds(start, BLK, stride=dil) if dil > 1 else pl.ds(start, BLK)


def _head_masks():
    lane = lax.broadcasted_iota(jnp.int32, (1, BLK), 1)
    return [lane < HEAD_DIM, lane >= HEAD_DIM]


def _band():
    rel = np.arange(BLK)[:, None] + BLK - np.arange(2 * BLK)[None, :]
    return (rel >= 0) & (rel <= BLK)


def _zcur(w):
    return lambda off: pl.BlockSpec((w, BLK), lambda hp, i: (i, off + hp))


def _zprev(p, nb):
    return lambda off: pl.BlockSpec((p, BLK), lambda hp, i: (jnp.maximum(i * nb - 1, 0), off + hp))


def _scur(w):
    return pl.BlockSpec((w, BLK), lambda hp, i: (i, hp))


def _pair_rows(t, masks):
    return jnp.concatenate([jnp.where(masks[0], t, 0.0), jnp.where(masks[1], t, 0.0)], axis=0).astype(BF16)


def _pair_bias_bwd(bias):
    return bias.reshape(4, 2, BLK, 2, BLK).transpose(0, 3, 2, 1, 4).reshape(4, 2, BLK, 2 * BLK)


def _unpair_bias_bwd(db):
    return db.reshape(4, 2, BLK, 2, BLK).transpose(0, 3, 2, 1, 4).reshape(N_HEADS, BLK, 2 * BLK)


def _attn_fwd(z, biases, name):
    s = z.shape[0]
    w = min(ATT_ROWS, s)
    n_br = len(BRANCHES)

    def body(*refs):
        q_ref, kp_ref, kc_ref, vp_ref, vc_ref = refs[:5]
        b_refs = refs[5:5 + n_br]
        y_ref, lse_ref, m_ref, l_ref, a_ref = refs[5 + n_br:]
        i = pl.program_id(1)
        masks = _head_masks()
        own_block = lax.broadcasted_iota(jnp.int32, (1, 2 * BLK), 1) >= BLK
        for n, (_, dil) in enumerate(BRANCHES):
            _, p, nb = _att_geom(s, dil)
            for r in range(dil):
                for b in range(nb):
                    rows = _rows(r + p * b, dil)
                    prev_rows = _rows(r + p * (b - 1), dil) if b > 0 else _rows(w - p + r, dil)
                    kprev, vprev = (kc_ref, vc_ref) if b > 0 else (kp_ref, vp_ref)
                    q = q_ref[rows, :] * SCALE
                    k = jnp.concatenate([kprev[prev_rows, :], kc_ref[rows, :]], axis=0).astype(BF16)
                    v = jnp.concatenate([vprev[prev_rows, :], vc_ref[rows, :]], axis=0).astype(BF16)
                    mb = lb = ob = None
                    for hh, mh in enumerate(masks):
                        qh = jnp.where(mh, q, 0.0).astype(BF16)
                        sc = lax.dot_general(qh, k, _DIMS["nt"], preferred_element_type=F32) + b_refs[n][hh]
                        if b == 0:
                            sc = jnp.where(own_block | (i > 0), sc, NEG_INF)
                        mx = jnp.max(sc, axis=1, keepdims=True)
                        e = jnp.exp(sc - mx)
                        den = jnp.sum(e, axis=1, keepdims=True)
                        o = jnp.dot(e.astype(BF16), v, preferred_element_type=F32)
                        if hh == 0:
                            mb = jnp.broadcast_to(mx, (BLK, BLK))
                            lb = jnp.broadcast_to(den, (BLK, BLK))
                            ob = o
                        else:
                            mb = jnp.where(mh, mx, mb)
                            lb = jnp.where(mh, den, lb)
                            ob = jnp.where(mh, o, ob)
                    if n == 0:
                        m_new, l_new, a_new = mb, lb, ob
                    else:
                        m_old = m_ref[rows, :]
                        m_new = jnp.maximum(m_old, mb)
                        al = jnp.exp(m_old - m_new)
                        be = jnp.exp(mb - m_new)
                        l_new = al * l_ref[rows, :] + be * lb
                        a_new = al * a_ref[rows, :] + be * ob
                    if n == n_br - 1:
                        y_ref[rows, :] = a_new / l_new
                        lse_ref[rows, :] = m_new + jnp.log(l_new)
                    else:
                        m_ref[rows, :] = m_new
                        l_ref[rows, :] = l_new
                        a_ref[rows, :] = a_new

    cur, prv = _zcur(w), _zprev(w, 1)
    b_spec = pl.BlockSpec((2, BLK, 2 * BLK), lambda hp, i: (hp, 0, 0))
    return pl.pallas_call(
        body, name=name, grid=(4, s // w), in_specs=[cur(0), prv(4), cur(4), prv(8), cur(8)] + [b_spec] * n_br,
        out_specs=[_scur(w)] * 2, out_shape=[_sds((s, ATTN_W))] * 2,
        scratch_shapes=[pltpu.VMEM((w, BLK), F32)] * 3,
        compiler_params=_params(("parallel", "parallel")),
    )(z, z, z, z, z, *biases)


def _row_stats(mh, dy, y, lse):
    delta = jnp.sum(jnp.where(mh, dy * y, 0.0), axis=1, keepdims=True)
    lse_h = jnp.max(jnp.where(mh, lse, NEG_INF), axis=1, keepdims=True)
    return delta, lse_h


def _attn_bwd(z, biases, dy, y, lse, name):
    s = z.shape[0]
    w = min(ATT_ROWS, s)
    n_steps = s // w
    n_br = len(BRANCHES)

    def body(*refs):
        q_ref, kp_ref, kc_ref, vp_ref, vc_ref, dy_ref, y_ref, lse_ref = refs[:8]
        b_refs = refs[8:8 + n_br]
        outs = refs[8 + n_br:]
        dq_ref, dk_ref, dv_ref = outs[:3]
        x_refs = outs[3:3 + 2 * n_br]
        db_refs = outs[3 + 2 * n_br:3 + 3 * n_br]
        acc_refs = outs[3 + 3 * n_br:]
        i = pl.program_id(1)

        @pl.when(i == 0)
        def _():
            for ref in db_refs:
                ref[...] = jnp.zeros_like(ref)

        masks = _head_masks()
        first_head = lax.broadcasted_iota(jnp.int32, (1, 2 * BLK), 1) < BLK

        sums = {"q": (acc_refs[0], dq_ref), "k": (acc_refs[1], dk_ref), "v": (acc_refs[2], dv_ref)}

        def add_up(n, rows, **vals):
            for key, val in vals.items():
                acc_ref, out_ref = sums[key]
                if n > 0:
                    val = val + acc_ref[rows, :]
                if n == n_br - 1:
                    out_ref[rows, :] = val
                else:
                    acc_ref[rows, :] = val

        for n, (_, dil) in enumerate(BRANCHES):
            _, p, nb = _att_geom(s, dil)
            b_ref, db_ref = b_refs[n], db_refs[n]
            dkx_ref, dvx_ref = x_refs[2 * n], x_refs[2 * n + 1]
            for r in range(dil):
                carry = None
                for b in range(nb):
                    rows = _rows(r + p * b, dil)
                    prev_rows = _rows(r + p * (b - 1), dil) if b > 0 else _rows(w - p + r, dil)
                    kprev, vprev = (kc_ref, vc_ref) if b > 0 else (kp_ref, vp_ref)
                    keys = [(_pair_rows(kprev[prev_rows, :], masks), _pair_rows(vprev[prev_rows, :], masks)),
                            (_pair_rows(kc_ref[rows, :], masks), _pair_rows(vc_ref[rows, :], masks))]
                    q = (q_ref[rows, :] * SCALE).astype(BF16)
                    dy_v = dy_ref[rows, :]
                    dyb = dy_v.astype(BF16)
                    stats = [_row_stats(mh, dy_v, y_ref[rows, :], lse_ref[rows, :]) for mh in masks]
                    delta = jnp.where(first_head, stats[0][0], stats[1][0])
                    lse_h = jnp.where(first_head, stats[0][1], stats[1][1])
                    dq = jnp.zeros((BLK, BLK), F32)
                    dk, dv = [], []
                    for half in range(2):
                        kh, vh = keys[half]
                        sc = lax.dot_general(q, kh, _DIMS["nt"], preferred_element_type=F32) + b_ref[half]
                        pr = jnp.exp(sc - lse_h)
                        if b == 0 and half == 0:
                            pr = pr * (i > 0).astype(F32)
                        dp = lax.dot_general(dyb, vh, _DIMS["nt"], preferred_element_type=F32)
                        ds = pr * (dp - delta)
                        db_ref[half] += ds
                        dsb = ds.astype(BF16)
                        dq = dq + jnp.dot(dsb, kh, preferred_element_type=F32)
                        dk2 = lax.dot_general(dsb, q, _DIMS["tn"], preferred_element_type=F32)
                        dv2 = lax.dot_general(pr.astype(BF16), dyb, _DIMS["tn"], preferred_element_type=F32)
                        dk.append(jnp.where(masks[0], dk2[:BLK], dk2[BLK:]))
                        dv.append(jnp.where(masks[0], dv2[:BLK], dv2[BLK:]))
                    add_up(n, rows, q=dq * SCALE)
                    if b > 0:
                        add_up(n, _rows(r + p * (b - 1), dil), k=carry[0] + dk[0], v=carry[1] + dv[0])
                    else:
                        dkx_ref[_rows(r, dil), :] = dk[0]
                        dvx_ref[_rows(r, dil), :] = dv[0]
                    carry = (dk[1], dv[1])
                add_up(n, _rows(r + p * (nb - 1), dil), k=carry[0], v=carry[1])

    cur, prv = _zcur(w), _zprev(w, 1)
    b_spec = pl.BlockSpec((None, 2, BLK, 2 * BLK), lambda hp, i: (hp, 0, 0, 0))
    x_specs, x_shapes = [], []
    for _, dil in BRANCHES:
        p = BLK * dil
        x_specs += [pl.BlockSpec((p, BLK), lambda hp, i: (i, hp))] * 2
        x_shapes += [_sds((n_steps * p, ATTN_W))] * 2
    outs = pl.pallas_call(
        body, name=name, grid=(4, n_steps),
        in_specs=[cur(0), prv(4), cur(4), prv(8), cur(8)] + [_scur(w)] * 3 + [b_spec] * n_br,
        out_specs=[_scur(w)] * 3 + x_specs + [b_spec] * n_br,
        out_shape=[_sds((s, ATTN_W))] * 3 + x_shapes + [_sds((4, 2, BLK, 2 * BLK))] * n_br,
        scratch_shapes=[pltpu.VMEM((w, BLK), F32)] * 3,
        compiler_params=_params(("parallel", "arbitrary")),
    )(z, z, z, z, z, dy, y, lse, *biases)
    dq, dk, dv = outs[:3]
    extras = [(outs[3 + 2 * n], outs[4 + 2 * n]) for n in range(n_br)]
    return dq, dk, dv, extras, [_unpair_bias_bwd(db) for db in outs[3 + 2 * n_br:]]


ASM_ROWS = 512


def _assemble_dz(dq, dk, dv, extras, dzs, du, name):
    s = dq.shape[0]
    w = min(ATT_ROWS, s)
    n_steps = s // w
    per_step = w // ASM_ROWS
    assert w % ASM_ROWS == 0

    def body(*refs):
        dq_ref, dk_ref, dv_ref, dzs_ref, du_ref = refs[:5]
        x_refs = refs[5:5 + 2 * len(extras)]
        o_ref, acc_ref = refs[-2:]
        j = pl.program_id(0)
        step = j // per_step
        has_next = (step < n_steps - 1).astype(F32)
        last_of_step = ((j + 1) % per_step == 0).astype(F32)
        o_ref[:, 0:ATTN_W] = dq_ref[...].astype(BF16)
        o_ref[:, 3 * ATTN_W:3 * ATTN_W + 2 * SGU_W] = dzs_ref[...].astype(BF16)
        o_ref[:, 3 * ATTN_W + 2 * SGU_W:IN_W] = du_ref[...].astype(BF16)
        for part, (base_ref, col) in enumerate(((dk_ref, ATTN_W), (dv_ref, 2 * ATTN_W))):
            acc_ref[...] = base_ref[...]
            for n, (_, dil) in enumerate(BRANCHES):
                rows = min(BLK * dil, ASM_ROWS)
                scale = has_next if BLK * dil >= w else has_next * last_of_step
                acc_ref[ASM_ROWS - rows:, :] += x_refs[2 * n + part][...] * scale
            o_ref[:, col:col + ATTN_W] = acc_ref[...].astype(BF16)

    def x_spec(dil):
        p = BLK * dil
        rows = min(p, ASM_ROWS)
        blocks_per_step = p // rows
        total = n_steps * blocks_per_step

        def idx(j):
            step = j // per_step
            within = (j % per_step) - (per_step - blocks_per_step)
            return (jnp.clip((step + 1) * blocks_per_step + jnp.maximum(within, 0), 0, total - 1), 0)

        return pl.BlockSpec((rows, ATTN_W), idx)

    in_specs = [_rb(ASM_ROWS, ATTN_W)] * 3 + [_rb(ASM_ROWS, 2 * SGU_W), _rb(ASM_ROWS, SSM_W)]
    args = [dq, dk, dv, dzs, du]
    for (dkx, dvx), (_, dil) in zip(extras, BRANCHES):
        in_specs += [x_spec(dil)] * 2
        args += [dkx, dvx]
    return pl.pallas_call(
        body, name=name, grid=(s // ASM_ROWS,), in_specs=in_specs, out_specs=_rb(ASM_ROWS, IN_W),
        out_shape=_sds((s, IN_W), BF16), scratch_shapes=[pltpu.VMEM((ASM_ROWS, ATTN_W), F32)],
        compiler_params=_params(("parallel",)),
    )(*args)


def _t5_bucket(dist):
    max_exact = N_BUCKETS // 2
    d = np.maximum(dist, 0)
    large = max_exact + (np.log(np.maximum(d, 1) / max_exact) / np.log(REL_MAX / max_exact)
                         * (N_BUCKETS - max_exact)).astype(np.int32)
    large = np.minimum(large, N_BUCKETS - 1)
    return np.where(d < max_exact, d, large).astype(np.int32)


def _bias_tables(rel_bias):
    period = 3 * BLK
    tabs = []
    for _, dil in BRANCHES:
        onehot = np.zeros((period, N_BUCKETS), np.float32)
        d = np.arange(BLK + 1)
        onehot[d, _t5_bucket((BLK - d) * dil)] = 1.0
        f = jnp.dot(jnp.asarray(onehot), rel_bias, precision=lax.Precision.HIGHEST)
        flat = jnp.tile(f.T, (1, BLK))[:, :BLK * (period - 1)]
        tab = flat.reshape(N_HEADS, BLK, period - 1)[:, :, :2 * BLK]
        tabs.append(jnp.where(_band()[None], tab, NEG_INF))
    return tabs


def _bucket_onehot():
    maps = []
    q = np.arange(BLK)[:, None]
    k = np.arange(2 * BLK)[None, :]
    rel = q + BLK - k
    for _, dil in BRANCHES:
        maps.append(np.where((rel >= 0) & (rel <= BLK), _t5_bucket(rel * dil), -1).reshape(-1))
    bmap = jnp.asarray(np.concatenate(maps).astype(np.int32))
    return (bmap[:, None] == jnp.arange(128, dtype=jnp.int32)[None, :]).astype(BF16)


def _block_diag(t):
    g, n, c = t.shape
    eye = jnp.eye(g, dtype=t.dtype)
    return (t[:, :, None, :] * eye[:, None, :, None]).reshape(g * n, g * c)


def _ssm_prep(a_re, a_im, log_dt, b_re, b_im, c_re, c_im):
    lam = lax.complex(a_re, a_im)
    dt = jnp.exp(log_dt)[:, None]
    a_bar = jnp.exp(lam * dt)
    b_bar = ((a_bar - 1.0) / lam)[:, :, None] * lax.complex(b_re, b_im)
    bdt = jnp.concatenate([_block_diag(jnp.real(b_bar)), _block_diag(jnp.imag(b_bar))], axis=0)
    cd = jnp.concatenate([_block_diag(jnp.transpose(c_re, (0, 2, 1))),
                          _block_diag(-jnp.transpose(c_im, (0, 2, 1)))], axis=0)
    return jnp.real(a_bar).reshape(-1), jnp.imag(a_bar).reshape(-1), bdt, cd


def _powers(ar, ai):
    pr, pi = ar[:, None], ai[:, None]
    k = 1
    while k < 8:
        lr, li = pr[:, -1:], pi[:, -1:]
        pr, pi = (jnp.concatenate([pr, pr * lr - pi * li], axis=1),
                  jnp.concatenate([pi, pr * li + pi * lr], axis=1))
        k *= 2
    return pr, pi


def _sgu_bias_expand(b):
    return jnp.repeat(b.T, 64, axis=1)


def _layer_fwd(i, h, a1, p_i, big, small, bias_tabs, next_gain):
    nm = "l%d_" % i
    sv = {"h": h}
    if a1 is None:
        a1 = _rms_fwd(h, small["norm_attn_g"][i], nm + "rms_attn")
    z = _mm(a1, big["w_in"], "nt", nm + "in_proj")
    y_attn, lse = _attn_fwd(z, [t[0] for t in bias_tabs], nm + "attn_fwd")
    bexp = _sgu_bias_expand(small["sgu_b"][i])
    y_sgu = _sgu_fwd(z, small["sgu_ln_g"][i], small["sgu_ln_b"][i], small["sgu_w"][i], bexp, nm + "sgu_fwd")
    ar, ai, bdt, cd = _ssm_prep(*[small[k][i] for k in ("ssm_a_re", "ssm_a_im", "ssm_log_dt", "ssm_b_re",
                                                         "ssm_b_im", "ssm_c_re", "ssm_c_im")])
    xr, xi, yc = _ssm_fwd(z, bdt.astype(BF16), cd.astype(BF16), _scan_tables(*_powers(ar, ai), False),
                          nm + "ssm_core")
    y_ssm = _ssm_post_fwd(yc, z, small["ssm_d"][i], big["ssm_glu_w"], small["ssm_glu_b"][i], nm + "ssm_post")
    mix = _mix_fwd(y_attn, y_sgu, y_ssm, small["branch_norm_g"][i], nm + "mix")
    if "rest" in big:
        big = dict({k: t for k, t in big.items() if k != "rest"}, **big["rest"](mix))
    h2, a2 = _mm(mix, big["w_out"], "nn", nm + "out_proj", add=h, norm_gain=small["norm_ffn_g"][i])
    hu = _mm(a2, big["ffn_w_up"], "nt", nm + "ffn_up", out_dtype=BF16)
    hv, hg, act = _conv_fwd(hu, big["ffn_conv_w"], small["ffn_conv_b"][i], nm + "ffn_conv")
    h3, a3 = _mm(act, big["ffn_w_down"], "nn", nm + "ffn_down", add=h2, norm_gain=small["norm_ple_g"][i])
    gp = _mm(a3, big["ple_w_gate"], "nn", nm + "ple_gate", out_dtype=BF16)
    pp = _mm(p_i, big["ple_w_proj"], "nt", nm + "ple_proj", out_dtype=BF16)
    h4, a_next = _ple_fwd(h3, gp, pp, next_gain, nm + "ple_add")
    sv.update(big=big, a1=a1, z=z, y_attn=y_attn, lse=lse, y_sgu=y_sgu, y_ssm=y_ssm, yc=yc, xr=xr, xi=xi, mix=mix, h2=h2,
              a2=a2, hu=hu, hv=hv, hg=hg, act=act, h3=h3, a3=a3, gp=gp, pp=pp)
    return h4, a_next, sv


def _layer_bwd(i, dh4, sv, p_i, big, small, bias_tabs, ffn_done=None):
    nm = "l%d_" % i
    g = {}
    dpp, dgp = _ple_bwd(dh4, sv["gp"], sv["pp"], nm + "ple_bwd")
    g["ple_w_proj"] = _mm(dpp, p_i, "tn", nm + "d_ple_proj", out_dtype=BF16)
    g["ple_w_gate"] = _mm(sv["a3"], dgp, "tn", nm + "d_ple_gate", out_dtype=BF16)
    dh3, dgain = _mm(dgp, big["ple_w_gate"], "nt", nm + "ple_gate_t", add=dh4,
                     norm_bwd=(sv["h3"], small["norm_ple_g"][i]))
    g["norm_ple_g"] = dgain.reshape(D_MODEL)
    g["ffn_w_down"] = _mm(sv["act"], dh3, "tn", nm + "d_ffn_down", out_dtype=BF16)
    dact = _mm(dh3, big["ffn_w_down"], "nt", nm + "ffn_down_t", out_dtype=BF16)
    dhu, g["ffn_conv_w"], dcb = _conv_bwd(dact, sv["hv"], sv["hg"], sv["hu"], big["ffn_conv_w"],
                                          nm + "ffn_conv_bwd")
    g["ffn_conv_b"] = dcb.reshape(2 * D_FF)
    g["ffn_w_up"] = _mm(dhu, sv["a2"], "tn", nm + "d_ffn_up", out_dtype=BF16)
    dh2, dgain = _mm(dhu, big["ffn_w_up"], "nn", nm + "ffn_up_t", add=dh3,
                     norm_bwd=(sv["h2"], small["norm_ffn_g"][i]))
    g["norm_ffn_g"] = dgain.reshape(D_MODEL)
    if ffn_done is not None:
        small = ffn_done(g, small)
    g["w_out"] = _mm(sv["mix"], dh2, "tn", nm + "d_out_proj", out_dtype=BF16)
    dmix = _mm(dh2, big["w_out"], "nt", nm + "out_proj_t")
    dya, dysg, dyss, g["branch_norm_g"] = _mix_bwd(dmix, sv["y_attn"], sv["y_sgu"], sv["y_ssm"],
                                                   small["branch_norm_g"][i], nm + "mix_bwd")
    ssm_keys = ("ssm_a_re", "ssm_a_im", "ssm_log_dt", "ssm_b_re", "ssm_b_im", "ssm_c_re", "ssm_c_im")
    (ar, ai, bdt, cd), prep_vjp = jax.vjp(_ssm_prep, *[small[k][i] for k in ssm_keys])
    dy1, dgl, y2, dud, g["ssm_d"], g["ssm_glu_b"] = _ssm_post_bwd(
        dyss, sv["yc"], sv["z"], small["ssm_d"][i], big["ssm_glu_w"], small["ssm_glu_b"][i], nm + "ssm_post_bwd")
    g["ssm_glu_w"] = _mm(y2, dgl, "tn", nm + "d_ssm_glu", out_dtype=BF16)
    du, dbdt, dcd, dar, dai = _ssm_bwd(dy1, dud, sv["z"], sv["xr"], sv["xi"], bdt.astype(BF16), cd.astype(BF16),
                                       _scan_tables(*_powers(ar, ai), True), nm + "ssm_core_bwd")
    for k, val in zip(ssm_keys, prep_vjp((dar, dai, dbdt, dcd))):
        g[k] = val
    bexp, bexp_vjp = jax.vjp(_sgu_bias_expand, small["sgu_b"][i])
    dzs, g["sgu_w"], dbexp, g["sgu_ln_g"], g["sgu_ln_b"] = _sgu_bwd(
        sv["z"], dysg, small["sgu_ln_g"][i], small["sgu_ln_b"][i], small["sgu_w"][i], bexp, nm + "sgu_bwd")
    g["sgu_b"] = bexp_vjp(dbexp)[0]
    dq, dk, dv, extras, dbs = _attn_bwd(sv["z"], [t[1] for t in bias_tabs], dya, sv["y_attn"], sv["lse"],
                                        nm + "attn_bwd")
    dbs = [db.reshape(N_HEADS, BLK * 2 * BLK) for db in dbs]
    dz = _assemble_dz(dq, dk, dv, extras, dzs, du, nm + "assemble_dz")
    g["w_in"] = _mm(dz, sv["a1"], "tn", nm + "d_in_proj", out_dtype=BF16)
    dh, dgain = _mm(dz, big["w_in"], "nn", nm + "in_proj_t", add=dh2, norm_bwd=(sv["h"], small["norm_attn_g"][i]))
    g["norm_attn_g"] = dgain.reshape(D_MODEL)
    return dh, g, jnp.concatenate(dbs, axis=1)


def _local_step(x, p, target, layer_weights, small, layer_done=None):
    depth = p.shape[0]
    bias_tabs = [(t, _pair_bias_bwd(t)) for t in _bias_tables(small["rel_bias"])]
    h, a1 = x, None
    saved = []
    for i in range(depth):
        next_gain = small["norm_attn_g"][i + 1] if i + 1 < depth else None
        h, a1, sv = _layer_fwd(i, h, a1, p[i], layer_weights(i, h), small, bias_tabs, next_gain)
        saved.append(sv)
    dh, loss, g_final = _loss_head(h, target, small["final_norm_g"], "loss_head")
    layer_grads = [None] * depth
    dbias = [None] * depth
    for i in reversed(range(depth)):
        ffn_done = None if layer_done is None else (lambda g, sm, i=i: layer_done(i, "ffn", g, sm))
        dh, layer_grads[i], dbias[i] = _layer_bwd(i, dh, saved[i], p[i], saved[i]["big"], small, bias_tabs,
                                                  ffn_done)
        if layer_done is not None:
            small = layer_done(i, "all", layer_grads[i], small)
    big_grads = [{k: lg.pop(k) for k in COMM_NAMES} for lg in layer_grads]
    grads = {k: jnp.stack([layer_grads[i][k] for i in range(depth)]) for k in layer_grads[0]}
    grads["final_norm_g"] = g_final
    g_rb = _mm(sum(dbias[1:], dbias[0]), _bucket_onehot(), "nn", "d_rel_bias", tk=2048)
    grads["rel_bias"] = g_rb[:, :N_BUCKETS].T
    return loss, dh, big_grads, grads


_ANY = pl.BlockSpec(memory_space=pl.ANY)
MESH_IDS = pl.DeviceIdType.MESH


def _slot(ref, axis, j):
    return ref.at[(slice(None),) * axis + (j,)]


def _all_gather(blocks, axis, name):
    nt = len(blocks)

    def body(*refs):
        x_refs, o_refs = refs[:nt], refs[nt:2 * nt]
        send_sems, recv_sems, local_sems = refs[2 * nt:]
        x, y, c = lax.axis_index("x"), lax.axis_index("y"), lax.axis_index("c")
        me, sibling = (x, y, c), (x, y, 1 - c)
        chips = [(1 - x, y), (x, 1 - y), (1 - x, 1 - y)]

        def slot(t, px, py, pc):
            return _slot(o_refs[t], axis, 4 * px + 2 * py + pc)

        def copy(t, k, blk, to, src=None):
            return pltpu.make_async_remote_copy(
                src_ref=slot(t, *blk) if src is None else src, dst_ref=slot(t, *blk),
                send_sem=send_sems.at[7 * t + k], recv_sem=recv_sems.at[7 * t + k],
                device_id=to, device_id_type=MESH_IDS)

        mine = [pltpu.make_async_copy(x_refs[t], slot(t, *me), local_sems.at[t]) for t in range(nt)]
        for cp in mine:
            cp.start()
        first = []
        for t in range(nt):
            first.append(copy(t, 0, me, sibling, src=x_refs[t]))
            first += [copy(t, 1 + j, me, (*chip, c), src=x_refs[t]) for j, chip in enumerate(chips)]
        for cp in first:
            cp.start()
        passed = []
        for t in range(nt):
            for j, chip in enumerate(chips):
                copy(t, 1 + j, (*chip, c), me).wait_recv()
                passed.append(copy(t, 4 + j, (*chip, c), sibling))
                passed[-1].start()
        for t in range(nt):
            copy(t, 0, sibling, me).wait_recv()
            for j, chip in enumerate(chips):
                copy(t, 4 + j, (*chip, 1 - c), me).wait_recv()
        for cp in first + passed:
            cp.wait_send()
        for cp in mine:
            cp.wait()

    out_shape = [jax.ShapeDtypeStruct(b.shape[:axis] + (N_DEV,) + b.shape[axis:], b.dtype) for b in blocks]
    return pl.pallas_call(
        body, name=name, out_shape=out_shape, in_specs=[_ANY] * nt, out_specs=[_ANY] * nt,
        scratch_shapes=[pltpu.SemaphoreType.DMA((7 * nt,)), pltpu.SemaphoreType.DMA((7 * nt,)),
                        pltpu.SemaphoreType.DMA((nt,))],
    )(*blocks)


def _peer(k):
    x, y, c = lax.axis_index("x"), lax.axis_index("y"), lax.axis_index("c")
    px = 1 - x if k & 4 else x
    py = 1 - y if k & 2 else y
    pc = 1 - c if k & 1 else c
    return (px, py, pc), 4 * px + 2 * py + pc


def _all_to_all(blocks, name):
    nt = len(blocks)

    def body(*refs):
        x_refs, o_refs = refs[:nt], refs[nt:2 * nt]
        send_sems, recv_sems, local_sems = refs[2 * nt:]
        _, me = _peer(0)
        mine = [pltpu.make_async_copy(x_refs[t].at[me], o_refs[t].at[me], local_sems.at[t]) for t in range(nt)]
        for cp in mine:
            cp.start()
        copies = []
        for k in range(1, N_DEV):
            peer, idx = _peer(k)
            for t in range(nt):
                cp = pltpu.make_async_remote_copy(
                    src_ref=x_refs[t].at[idx], dst_ref=o_refs[t].at[me],
                    send_sem=send_sems.at[7 * t + k - 1], recv_sem=recv_sems.at[7 * t + k - 1],
                    device_id=peer, device_id_type=MESH_IDS)
                cp.start()
                copies.append(cp)
        for cp in copies:
            cp.wait()
        for cp in mine:
            cp.wait()

    return pl.pallas_call(
        body, name=name, out_shape=[jax.ShapeDtypeStruct(b.shape, b.dtype) for b in blocks],
        in_specs=[_ANY] * nt, out_specs=[_ANY] * nt,
        scratch_shapes=[pltpu.SemaphoreType.DMA((7 * nt,)), pltpu.SemaphoreType.DMA((7 * nt,)),
                        pltpu.SemaphoreType.DMA((nt,))],
    )(*blocks)


_HBM = pl.BlockSpec(memory_space=pltpu.HBM)
_SEM = pl.BlockSpec(memory_space=pltpu.SEMAPHORE)
_EFFECT = pltpu.SideEffectType.DATAFLOW_SIDE_EFFECTING


def _split_copy(src_ref, land_ref, send_sems, recv_sems, t, k, gather):
    peer, idx = _peer(k)
    _, me = _peer(0)
    return pltpu.make_async_remote_copy(
        src_ref=src_ref if gather else src_ref.at[idx], dst_ref=land_ref.at[me],
        send_sem=send_sems.at[7 * t + k - 1], recv_sem=recv_sems.at[7 * t + k - 1],
        device_id=peer, device_id_type=MESH_IDS)


def _exchange_start(srcs, lands, gather, name):
    nt = len(srcs)

    def body(*refs):
        src_refs, land_refs = refs[:nt], refs[nt:2 * nt]
        send_sems, recv_sems = refs[2 * nt:2 * nt + 2]
        token = refs[-1]
        for k in range(1, N_DEV):
            for t in range(nt):
                _split_copy(src_refs[t], land_refs[t], send_sems, recv_sems, t, k, gather).start()
        token[...] = jnp.zeros_like(token)

    hbm = lambda a: pltpu.HBM(a.shape, a.dtype)
    outs = pl.pallas_call(
        body, name=name,
        out_shape=(pltpu.SemaphoreType.DMA((7 * nt,)), pltpu.SemaphoreType.DMA((7 * nt,)),
                   *[hbm(a) for a in srcs], *[hbm(a) for a in lands], jax.ShapeDtypeStruct((8, 128), F32)),
        in_specs=[_HBM] * (2 * nt),
        out_specs=(_SEM, _SEM, *[_HBM] * (2 * nt), pl.BlockSpec(memory_space=pltpu.VMEM)),
        input_output_aliases={j: 2 + j for j in range(2 * nt)},
        compiler_params=pltpu.CompilerParams(has_side_effects=_EFFECT),
    )(*[pltpu.with_memory_space_constraint(a, pltpu.HBM) for a in list(srcs) + list(lands)])
    return outs[0], outs[1], outs[2:2 + nt], outs[2 + nt:2 + 2 * nt], outs[-1]


def _exchange_wait(send_sems, recv_sems, srcs, lands, after, gather, name):
    nt = len(srcs)

    def body(*refs):
        src_refs, land_refs = refs[:nt], refs[nt:2 * nt]
        send_sems, recv_sems = refs[2 * nt:2 * nt + 2]
        for k in range(1, N_DEV):
            _, idx = _peer(k)
            for t in range(nt):
                _split_copy(src_refs[t], land_refs[t], send_sems, recv_sems, t, k, gather).wait_send()
                arrival = pltpu.make_async_remote_copy(
                    src_ref=land_refs[t].at[idx], dst_ref=land_refs[t].at[idx],
                    send_sem=send_sems.at[7 * t + k - 1], recv_sem=recv_sems.at[7 * t + k - 1],
                    device_id=_peer(k)[0], device_id_type=MESH_IDS)
                arrival.wait_recv()

    hbm = lambda a: pltpu.HBM(a.shape, a.dtype)
    outs = pl.pallas_call(
        body, name=name, out_shape=tuple(hbm(a) for a in list(srcs) + list(lands)),
        in_specs=[_HBM] * (2 * nt) + [_SEM, _SEM, _ANY], out_specs=tuple([_HBM] * (2 * nt)),
        input_output_aliases={j: j for j in range(2 * nt)},
        compiler_params=pltpu.CompilerParams(has_side_effects=_EFFECT),
    )(*srcs, *lands, send_sems, recv_sems, after)
    return outs[nt:]


def _adamw(parts, w, m, v, name, tr):
    n_layers, r, c_ = w.shape
    assert len(parts) == n_layers

    def body(*refs):
        p_refs = refs[:n_layers]
        w_ref, m_ref, v_ref, g_ref, d_ref, mo_ref, vo_ref = refs[n_layers:]

        def update(p_ref):
            g = p_ref[0].astype(F32)
            for j in range(1, N_DEV):
                g = g + p_ref[j].astype(F32)
            m2 = ADAM_B1 * m_ref[...] + (1.0 - ADAM_B1) * g
            v2 = ADAM_B2 * v_ref[...] + (1.0 - ADAM_B2) * (g * g)
            m_hat = m2 / (1.0 - ADAM_B1 ** ADAM_STEP)
            v_hat = v2 / (1.0 - ADAM_B2 ** ADAM_STEP)
            g_ref[...] = g
            d_ref[...] = -ADAM_LR * (m_hat / (jnp.sqrt(v_hat) + ADAM_EPS) + ADAM_WD * w_ref[...])
            mo_ref[...] = m2
            vo_ref[...] = v2

        for layer in range(n_layers):
            pl.when(pl.program_id(0) == layer)(lambda layer=layer: update(p_refs[layer]))

    spec = pl.BlockSpec((None, tr, c_), lambda l, i: (l, i, 0))
    p_spec = pl.BlockSpec((N_DEV, tr, c_), lambda l, i: (0, i, 0))
    return pl.pallas_call(
        body, name=name, grid=(n_layers, r // tr), in_specs=[p_spec] * n_layers + [spec] * 3,
        out_specs=[spec] * 4, out_shape=[_sds((n_layers, r, c_))] * 4,
        compiler_params=_params(("parallel", "parallel")),
    )(*parts, w, m, v)


def _pack_rows(n_elems, align):
    rows = -(-n_elems // PACK_COLS)
    return -(-rows // align) * align


def _pack(arrs, rows, dtype=F32):
    flat = jnp.concatenate([a.reshape(-1) for a in arrs]).astype(dtype)
    return jnp.pad(flat, (0, rows * PACK_COLS - flat.shape[0])).reshape(rows, PACK_COLS)


def _unpack(pack, shapes):
    flat = pack.reshape(-1)
    out, off = [], 0
    for shp in shapes:
        size = int(np.prod(shp))
        out.append(flat[off:off + size].reshape(shp))
        off += size
    return out


def _tile_rows(rows, target, align=16):
    best = align
    for t in range(align, target + 1, align):
        if rows % t == 0:
            best = t
    return best


COMM_NAMES = ("w_in", "ssm_glu_w", "w_out", "ffn_w_up", "ffn_w_down", "ple_w_gate", "ple_w_proj")
COMM_TRANSPOSED = ("w_in", "ffn_w_up", "ple_w_proj")
COMM_EARLY = ("ple_w_proj", "ple_w_gate", "ffn_w_down", "ffn_w_up")
COMM_LATE = ("w_in", "ssm_glu_w", "w_out")
SMALL_TILE_ROWS = 64
CONV_NAME = "ffn_conv_w"


def _to_comm(name, a):
    return jnp.swapaxes(a, 1, 2) if name in COMM_TRANSPOSED else a


def kernel(x, p, rel_bias, norm_attn_g, w_in, sgu_ln_g, sgu_ln_b, sgu_w, sgu_b, ssm_a_re, ssm_a_im, ssm_log_dt, ssm_b_re, ssm_b_im, ssm_c_re, ssm_c_im, ssm_d, ssm_glu_w, ssm_glu_b, branch_norm_g, w_out, norm_ffn_g, ffn_w_up, ffn_conv_w, ffn_conv_b, ffn_w_down, norm_ple_g, ple_w_gate, ple_w_proj, final_norm_g, loss_target, m_rel_bias, m_norm_attn_g, m_w_in, m_sgu_ln_g, m_sgu_ln_b, m_sgu_w, m_sgu_b, m_ssm_a_re, m_ssm_a_im, m_ssm_log_dt, m_ssm_b_re, m_ssm_b_im, m_ssm_c_re, m_ssm_c_im, m_ssm_d, m_ssm_glu_w, m_ssm_glu_b, m_branch_norm_g, m_w_out, m_norm_ffn_g, m_ffn_w_up, m_ffn_conv_w, m_ffn_conv_b, m_ffn_w_down, m_norm_ple_g, m_ple_w_gate, m_ple_w_proj, m_final_norm_g, v_rel_bias, v_norm_attn_g, v_w_in, v_sgu_ln_g, v_sgu_ln_b, v_sgu_w, v_sgu_b, v_ssm_a_re, v_ssm_a_im, v_ssm_log_dt, v_ssm_b_re, v_ssm_b_im, v_ssm_c_re, v_ssm_c_im, v_ssm_d, v_ssm_glu_w, v_ssm_glu_b, v_branch_norm_g, v_w_out, v_norm_ffn_g, v_ffn_w_up, v_ffn_conv_w, v_ffn_conv_b, v_ffn_w_down, v_norm_ple_g, v_ple_w_gate, v_ple_w_proj, v_final_norm_g):
    given = dict(locals())
    w = {n: given[n] for n in WEIGHT_NAMES}
    m = {n: given["m_" + n] for n in WEIGHT_NAMES}
    v = {n: given["v_" + n] for n in WEIGHT_NAMES}
    depth = p.shape[0]
    dev = 4 * lax.axis_index("x") + 2 * lax.axis_index("y") + lax.axis_index("c")

    wc = {n: _to_comm(n, w[n]) for n in COMM_NAMES}
    wb = {n: wc[n].astype(BF16) for n in COMM_NAMES}
    conv_local = [w[CONV_NAME], m[CONV_NAME], v[CONV_NAME]]
    conv_rows = _pack_rows(sum(int(np.prod(t.shape)) for t in conv_local), 8)
    conv_g, = _all_gather([_pack(conv_local, conv_rows)], 0, "gather_conv_taps")
    conv_parts = zip(*[_unpack(conv_g[j], [t.shape for t in conv_local]) for j in range(N_DEV)])
    conv_w, conv_m, conv_v = [jnp.concatenate(parts, axis=2) for parts in conv_parts]
    small = {n: w[n] for n in SMALL_NAMES}

    def whole(names, blocks):
        return {n: t.reshape(-1, t.shape[-1]) for n, t in zip(names, blocks)}

    def own_slot(block):
        return lax.dynamic_update_slice_in_dim(jnp.zeros((N_DEV,) + block.shape, block.dtype), block[None], dev, 0)

    def start_gather(names, i, after):
        srcs, after = lax.optimization_barrier(([wb[n][i] for n in names], after))
        return _exchange_start(srcs, [own_slot(s) for s in srcs], True, "gather_weights_%d_start" % i), after

    def wait_gather(names, i, started, after):
        send_sems, recv_sems, srcs, lands, _ = started
        return whole(names, _exchange_wait(send_sems, recv_sems, srcs, lands, after, True,
                                           "gather_weights_%d_wait" % i))

    at_once = ("w_in", "ssm_glu_w")
    later = tuple(n for n in COMM_NAMES if n not in at_once)
    w_in_0 = _all_gather([wb[n][0] for n in at_once], 0, "gather_w_in_0")
    gathering = {}
    gathering[0], (w_in_0, _) = start_gather(later, 0, (w_in_0, conv_g))
    small["norm_attn_g"] = small["norm_attn_g"] + gathering[0][4][0, 0]

    def layer_weights(i, h):
        if i > 0:
            got = wait_gather(COMM_NAMES, i, gathering.pop(i), h)
            if i + 1 < depth:
                gathering[i + 1], ordered = start_gather(COMM_NAMES, i + 1, got["w_in"])
                got["w_in"] = ordered + gathering[i + 1][4][0, 0].astype(BF16)
            return dict(got, **{CONV_NAME: conv_w[i]})

        def rest(after):
            got = wait_gather(later, 0, gathering.pop(0), after)
            if depth > 1:
                gathering[1], ordered = start_gather(COMM_NAMES, 1, got["w_out"])
                got["w_out"] = ordered + gathering[1][4][0, 0].astype(BF16)
            return got

        return dict(whole(at_once, w_in_0), **{CONV_NAME: conv_w[0], "rest": rest})

    def as_slots(g, n):
        return g.reshape((N_DEV,) + wc[n].shape[1:])

    scattering = {}

    def layer_done(i, stage, g, small_now):
        if stage == "all" and i == 0:
            return small_now
        names = COMM_EARLY if stage == "ffn" else COMM_LATE
        srcs = [as_slots(g[n], n) for n in names]
        lands = [own_slot(lax.dynamic_index_in_dim(s, dev, 0, keepdims=False)) for s in srcs]
        started = _exchange_start(srcs, lands, False, "scatter_weight_grads_%d_%s_start" % (i, stage))
        scattering[i, stage] = (names, started)
        pin = "branch_norm_g" if stage == "ffn" else "norm_ple_g"
        return dict(small_now, **{pin: small_now[pin] + started[4][0, 0]})

    loss, dx, big_grads, grads = _local_step(x[0], p[:, 0], loss_target[0], layer_weights, small, layer_done)
    loss = lax.psum(loss, ("x", "y", "c"))

    recv = [{} for _ in range(depth)]
    for (i, stage), (names, (send_sems, recv_sems, srcs, lands, _)) in scattering.items():
        got = _exchange_wait(send_sems, recv_sems, srcs, lands, dx, False,
                             "scatter_weight_grads_%d_%s_wait" % (i, stage))
        recv[i].update(zip(names, got))
    srcs = [as_slots(big_grads[0][n], n) for n in COMM_LATE]
    lands = [own_slot(lax.dynamic_index_in_dim(s, dev, 0, keepdims=False)) for s in srcs]
    last = _exchange_start(srcs, lands, False, "scatter_weight_grads_0_all_start")
    out = {}

    def update(n, pin=None):
        weight = wc[n] if pin is None else wc[n] + pin
        res = _adamw([recv[i][n] for i in range(depth)], weight, _to_comm(n, m[n]), _to_comm(n, v[n]),
                     "adamw_" + n, _tile_rows(wc[n].shape[1], 256))
        out[n] = [_to_comm(n, r) for r in res]

    for j, n in enumerate(COMM_EARLY):
        update(n, last[4][0, 0] if j == 0 else None)
    got = _exchange_wait(last[0], last[1], last[2], last[3], out[COMM_EARLY[-1]][0], False,
                         "scatter_weight_grads_0_all_wait")
    recv[0].update(zip(COMM_LATE, got))

    rep_names = SMALL_NAMES + (CONV_NAME,)
    rep_w = dict({n: w[n] for n in SMALL_NAMES}, **{CONV_NAME: conv_w})
    rep_m = dict({n: m[n] for n in SMALL_NAMES}, **{CONV_NAME: conv_m})
    rep_v = dict({n: v[n] for n in SMALL_NAMES}, **{CONV_NAME: conv_v})
    rep_shapes = [rep_w[n].shape for n in rep_names]
    rep_rows = _pack_rows(sum(int(np.prod(s)) for s in rep_shapes), SMALL_TILE_ROWS)
    rep_parts, = _all_gather([_pack([grads[n] for n in rep_names], rep_rows)], 0, "gather_small_grads")
    for n in COMM_LATE:
        update(n)
    rep_out = _adamw([rep_parts], *[_pack([src[n] for n in rep_names], rep_rows)[None] for src in (rep_w, rep_m, rep_v)],
                     "adamw_replicated", SMALL_TILE_ROWS)
    for n, vals in zip(rep_names, zip(*[_unpack(r[0], rep_shapes) for r in rep_out])):
        out[n] = list(vals)
    shard = ffn_conv_w.shape[2]
    out[CONV_NAME] = [lax.dynamic_slice_in_dim(t, dev * shard, shard, axis=2) for t in out[CONV_NAME]]
    results = [[out[n][kind] for n in WEIGHT_NAMES] for kind in range(4)]
    return (loss, dx[None], *results[0], *results[1], *results[2], *results[3])
```

```python
import math

import numpy as np
import jax
import jax.numpy as jnp
from jax import lax
from jax.experimental import pallas as pl
from jax.experimental.pallas import tpu as pltpu

F32 = jnp.float32
BF16 = jnp.bfloat16

D_MODEL = 1024
HEAD_DIM = 64
N_HEADS = 8
ATTN_W = 512
SGU_W = 256
SGU_GROUPS = 4
SGU_CHUNK = 128
SSM_W = 256
SSM_GROUPS = 16
SSM_STATE = 64
SSM_NS = SSM_GROUPS * SSM_STATE
IN_W = 2304
D_FF = 2816
BRANCHES = ((128, 1), (512, 4), (2048, 16))
BLK = 128
N_BUCKETS = 32
REL_MAX = 2048
EPS = 1e-6
NEG_INF = -1e30
N_DEV = 8

ADAM_LR = 0.001
ADAM_B1 = 0.9
ADAM_B2 = 0.999
ADAM_EPS = 1e-08
ADAM_WD = 0.01
ADAM_STEP = 10

VMEM_LIMIT_BYTES = 56 * 1024 * 1024
GELU_C = math.sqrt(2.0 / math.pi)

SMALL_NAMES = ("rel_bias", "norm_attn_g", "sgu_ln_g", "sgu_ln_b", "sgu_w", "sgu_b", "ssm_a_re", "ssm_a_im",
               "ssm_log_dt", "ssm_b_re", "ssm_b_im", "ssm_c_re", "ssm_c_im", "ssm_d", "ssm_glu_b",
               "branch_norm_g", "norm_ffn_g", "ffn_conv_b", "norm_ple_g", "final_norm_g")
WEIGHT_NAMES = ("rel_bias", "norm_attn_g", "w_in", "sgu_ln_g", "sgu_ln_b", "sgu_w", "sgu_b", "ssm_a_re",
                "ssm_a_im", "ssm_log_dt", "ssm_b_re", "ssm_b_im", "ssm_c_re", "ssm_c_im", "ssm_d", "ssm_glu_w",
                "ssm_glu_b", "branch_norm_g", "w_out", "norm_ffn_g", "ffn_w_up", "ffn_conv_w", "ffn_conv_b",
                "ffn_w_down", "norm_ple_g", "ple_w_gate", "ple_w_proj", "final_norm_g")
PACK_COLS = 512


def _params(sem):
    return pltpu.CompilerParams(dimension_semantics=sem, vmem_limit_bytes=VMEM_LIMIT_BYTES)


def _pick(dim, target):
    if dim <= target:
        return dim
    best = None
    for t in range(128, target + 1, 128):
        if dim % t == 0:
            best = t
    return dim if best is None else best


def _gelu(x):
    return 0.5 * x * (1.0 + jnp.tanh(GELU_C * (x + 0.044715 * (x * x * x))))


def _gelu_grad(x):
    t = jnp.tanh(GELU_C * (x + 0.044715 * (x * x * x)))
    return 0.5 * (1.0 + t) + 0.5 * x * (1.0 - t * t) * (GELU_C * (1.0 + 3.0 * 0.044715 * (x * x)))


def _sigmoid(x):
    return 1.0 / (1.0 + jnp.exp(-x))


_DIMS = {"nn": (((1,), (0,)), ((), ())), "tn": (((0,), (0,)), ((), ())), "nt": (((1,), (1,)), ((), ()))}


MM_TILE = D_FF // 2


def _mm(a, b, mode, name, add=None, out_dtype=F32, norm_gain=None, norm_bwd=None, tm=MM_TILE, tn=MM_TILE,
        tk=MM_TILE):
    if mode == "nn":
        m, k = a.shape
        k2, n = b.shape
    elif mode == "tn":
        k, m = a.shape
        k2, n = b.shape
    else:
        m, k = a.shape
        n, k2 = b.shape
    assert k == k2, (name, a.shape, b.shape, mode)
    tm, tn, tk = _pick(m, tm), _pick(n, tn), _pick(k, tk)
    nk = k // tk
    dims = _DIMS[mode]
    has_add = add is not None
    has_norm = norm_gain is not None
    has_nbwd = norm_bwd is not None
    assert not (has_norm or has_nbwd) or tn == n

    def body(*refs):
        a_ref, b_ref = refs[:2]
        rest = list(refs[2:])
        add_ref = rest.pop(0) if has_add else None
        g_ref = rest.pop(0) if has_norm else None
        h_ref, hg_ref = (rest.pop(0), rest.pop(0)) if has_nbwd else (None, None)
        o_ref = rest.pop(0)
        n_ref = rest.pop(0) if has_norm else None
        dg_ref = rest.pop(0) if has_nbwd else None
        part = lax.dot_general(a_ref[...].astype(BF16), b_ref[...].astype(BF16), dims,
                               preferred_element_type=F32)
        if has_nbwd:
            @pl.when((pl.program_id(0) == 0) & (pl.program_id(2) == 0))
            def _():
                dg_ref[...] = jnp.zeros_like(dg_ref)

        def finish(r):
            if has_nbwd:
                x = h_ref[...]
                scale = lax.rsqrt(jnp.mean(x * x, axis=-1, keepdims=True) + EPS)
                xh = x * scale
                dg_ref[...] += jnp.sum(r * xh, axis=0, keepdims=True)
                dxh = r * hg_ref[...]
                r = scale * (dxh - xh * jnp.mean(dxh * xh, axis=-1, keepdims=True))
            if has_add:
                r = r + add_ref[...]
            o_ref[...] = r.astype(out_dtype)
            if has_norm:
                scale = lax.rsqrt(jnp.mean(r * r, axis=-1, keepdims=True) + EPS)
                n_ref[...] = (r * scale * g_ref[...]).astype(BF16)

        if nk == 1:
            finish(part)
            return
        acc_ref = refs[-1]
        kk = pl.program_id(2)

        @pl.when(kk == 0)
        def _():
            acc_ref[...] = part

        @pl.when((kk > 0) & (kk < nk - 1))
        def _():
            acc_ref[...] += part

        @pl.when(kk == nk - 1)
        def _():
            finish(acc_ref[...] + part)

    if mode == "tn":
        a_spec = pl.BlockSpec((tk, tm), lambda i, j, kk: (kk, i))
    else:
        a_spec = pl.BlockSpec((tm, tk), lambda i, j, kk: (i, kk))
    if mode == "nt":
        b_spec = pl.BlockSpec((tn, tk), lambda i, j, kk: (j, kk))
    else:
        b_spec = pl.BlockSpec((tk, tn), lambda i, j, kk: (kk, j))
    o_spec = pl.BlockSpec((tm, tn), lambda i, j, kk: (i, j))
    in_specs = [a_spec, b_spec] + ([o_spec] if has_add else [])
    args = (a, b) + ((add,) if has_add else ())
    out_specs, out_shape = o_spec, jax.ShapeDtypeStruct((m, n), out_dtype)
    if has_norm:
        in_specs.append(pl.BlockSpec((1, n), lambda i, j, kk: (0, 0)))
        args += (norm_gain.reshape(1, n),)
        out_specs, out_shape = [o_spec, o_spec], [out_shape, jax.ShapeDtypeStruct((m, n), BF16)]
    if has_nbwd:
        row_spec = pl.BlockSpec((1, n), lambda i, j, kk: (0, 0))
        in_specs += [o_spec, row_spec]
        args += (norm_bwd[0], norm_bwd[1].reshape(1, n))
        out_specs, out_shape = [o_spec, row_spec], [out_shape, jax.ShapeDtypeStruct((1, n), F32)]
    sem = ("arbitrary",) * 3 if has_nbwd else ("parallel", "parallel", "arbitrary")
    return pl.pallas_call(
        body, name=name, grid=(m // tm, n // tn, nk),
        in_specs=in_specs, out_specs=out_specs, out_shape=out_shape,
        scratch_shapes=[pltpu.VMEM((tm, tn), F32)] if nk > 1 else [], compiler_params=_params(sem),
    )(*args)


def _rb(tm, w, cb=0):
    return pl.BlockSpec((tm, w), lambda i: (i, cb))


def _fb(shape):
    nd = len(shape)
    return pl.BlockSpec(shape, lambda i: (0,) * nd)


def _rowcall(body, name, n_rows, tm, in_specs, args, out_specs, out_shapes):
    return pl.pallas_call(
        body, name=name, grid=(n_rows // tm,), in_specs=in_specs, out_specs=out_specs, out_shape=out_shapes,
        compiler_params=_params(("arbitrary",)),
    )(*args)


def _sds(shape, dtype=F32):
    return jax.ShapeDtypeStruct(shape, dtype)


def _rms_fwd(h, g, name, tm=1024):
    s, d = h.shape

    def body(h_ref, g_ref, o_ref):
        x = h_ref[...]
        r = lax.rsqrt(jnp.mean(x * x, axis=-1, keepdims=True) + EPS)
        o_ref[...] = (x * r * g_ref[...]).astype(BF16)

    return _rowcall(body, name, s, tm, [_rb(tm, d), _fb((1, d))], (h, g.reshape(1, d)), _rb(tm, d),
                    _sds((s, d), BF16))


def _loss_head(h, target, g, name, tm=1024):
    s, d = h.shape

    def body(h_ref, t_ref, g_ref, dh_ref, loss_ref, dg_ref):
        @pl.when(pl.program_id(0) == 0)
        def _():
            dg_ref[...] = jnp.zeros_like(dg_ref)
            loss_ref[...] = jnp.zeros_like(loss_ref)

        x = h_ref[...]
        r = lax.rsqrt(jnp.mean(x * x, axis=-1, keepdims=True) + EPS)
        xh = x * r
        gg = g_ref[...]
        err = xh * gg - t_ref[...]
        loss_ref[...] += jnp.sum(err * err) * (0.5 / d)
        dy = err * (1.0 / d)
        dg_ref[...] += jnp.sum(dy * xh, axis=0, keepdims=True)
        dxh = dy * gg
        dh_ref[...] = r * (dxh - xh * jnp.mean(dxh * xh, axis=-1, keepdims=True))

    dh, loss, dg = _rowcall(body, name, s, tm, [_rb(tm, d), _rb(tm, d), _fb((1, d))], (h, target, g.reshape(1, d)),
                            [_rb(tm, d), _fb((1, 128)), _fb((1, d))], [_sds((s, d)), _sds((1, 128)), _sds((1, d))])
    return dh, loss[0, 0], dg.reshape(d)


_MIX_PARTS = ((0, 512), (512, 768), (768, 1024))


def _mix_fwd(ya, ysg, yss, g, name, tm=1024):
    s = ya.shape[0]

    def body(a_ref, b_ref, c_ref, g_ref, o_ref):
        for ref, (lo, hi) in zip((a_ref, b_ref, c_ref), _MIX_PARTS):
            y = ref[...]
            r = lax.rsqrt(jnp.mean(y * y, axis=-1, keepdims=True) + EPS)
            o_ref[:, lo:hi] = (y * r * g_ref[:, lo:hi]).astype(BF16)

    return _rowcall(body, name, s, tm, [_rb(tm, 512), _rb(tm, 256), _rb(tm, 256), _fb((1, 1024))],
                    (ya, ysg, yss, g.reshape(1, 1024)), _rb(tm, 1024), _sds((s, 1024), BF16))


def _mix_bwd(dmix, ya, ysg, yss, g, name, tm=1024):
    s = ya.shape[0]

    def body(dm_ref, a_ref, b_ref, c_ref, g_ref, da_ref, db_ref, dc_ref, dg_ref):
        @pl.when(pl.program_id(0) == 0)
        def _():
            dg_ref[...] = jnp.zeros_like(dg_ref)

        for ref, dref, (lo, hi) in zip((a_ref, b_ref, c_ref), (da_ref, db_ref, dc_ref), _MIX_PARTS):
            y = ref[...]
            r = lax.rsqrt(jnp.mean(y * y, axis=-1, keepdims=True) + EPS)
            xh = y * r
            dm = dm_ref[:, lo:hi]
            dg_ref[:, lo:hi] += jnp.sum(dm * xh, axis=0, keepdims=True)
            dxh = dm * g_ref[:, lo:hi]
            dref[...] = r * (dxh - xh * jnp.mean(dxh * xh, axis=-1, keepdims=True))

    da, db, dc, dg = _rowcall(
        body, name, s, tm, [_rb(tm, 1024), _rb(tm, 512), _rb(tm, 256), _rb(tm, 256), _fb((1, 1024))],
        (dmix, ya, ysg, yss, g.reshape(1, 1024)),
        [_rb(tm, 512), _rb(tm, 256), _rb(tm, 256), _fb((1, 1024))],
        [_sds((s, 512)), _sds((s, 256)), _sds((s, 256)), _sds((1, 1024))])
    return da, db, dc, dg.reshape(1024)


def _ssm_post_fwd(yc, z, d, gw, gb, name, tm=1024):
    s = yc.shape[0]

    def body(yc_ref, u_ref, d_ref, gw_ref, gb_ref, o_ref):
        y1 = yc_ref[...] + d_ref[...] * u_ref[...]
        y2 = _gelu(y1)
        gl = jnp.dot(y2.astype(BF16), gw_ref[...], preferred_element_type=F32) + gb_ref[...]
        o_ref[...] = y2 * _sigmoid(gl)

    return _rowcall(body, name, s, tm, [_rb(tm, 256), _rb(tm, 256, 8), _fb((1, 256)), _fb((256, 256)), _fb((1, 256))],
                    (yc, z, d.reshape(1, 256), gw, gb.reshape(1, 256)), _rb(tm, 256), _sds((s, 256)))


def _ssm_post_bwd(dy, yc, z, d, gw, gb, name, tm=1024):
    s = yc.shape[0]

    def body(dy_ref, yc_ref, u_ref, d_ref, gw_ref, gb_ref, dy1_ref, dgl_ref, y2_ref, dud_ref, dd_ref, dgb_ref):
        @pl.when(pl.program_id(0) == 0)
        def _():
            dd_ref[...] = jnp.zeros_like(dd_ref)
            dgb_ref[...] = jnp.zeros_like(dgb_ref)

        u = u_ref[...]
        dd = d_ref[...]
        y1 = yc_ref[...] + dd * u
        y2 = _gelu(y1)
        gw_v = gw_ref[...]
        gl = jnp.dot(y2.astype(BF16), gw_v, preferred_element_type=F32) + gb_ref[...]
        sg = _sigmoid(gl)
        dyv = dy_ref[...]
        dgl = dyv * y2 * sg * (1.0 - sg)
        dy2 = dyv * sg + lax.dot_general(dgl.astype(BF16), gw_v, _DIMS["nt"], preferred_element_type=F32)
        dy1 = dy2 * _gelu_grad(y1)
        dy1_ref[...] = dy1.astype(BF16)
        dgl_ref[...] = dgl.astype(BF16)
        y2_ref[...] = y2.astype(BF16)
        dud_ref[...] = dy1 * dd
        dd_ref[...] += jnp.sum(dy1 * u, axis=0, keepdims=True)
        dgb_ref[...] += jnp.sum(dgl, axis=0, keepdims=True)

    outs = _rowcall(
        body, name, s, tm,
        [_rb(tm, 256), _rb(tm, 256), _rb(tm, 256, 8), _fb((1, 256)), _fb((256, 256)), _fb((1, 256))],
        (dy, yc, z, d.reshape(1, 256), gw, gb.reshape(1, 256)),
        [_rb(tm, 256)] * 4 + [_fb((1, 256))] * 2,
        [_sds((s, 256), BF16)] * 3 + [_sds((s, 256))] + [_sds((1, 256))] * 2)
    dy1, dgl, y2, dud, dd, dgb = outs
    return dy1, dgl, y2, dud, dd.reshape(256), dgb.reshape(256)


SCAN_T = 512
N_SCAN_TABLES = 6


def _scan_tables(pr, pi, reverse):
    ns = pr.shape[0]
    sign = -1.0 if reverse else 1.0
    power = [(jnp.ones((ns,), F32), jnp.zeros((ns,), F32))] + [(pr[:, k], sign * pi[:, k]) for k in range(8)]
    zero = (jnp.zeros((ns,), F32), jnp.zeros((ns,), F32))

    def table(exponents):
        rows = [zero if e is None else power[e] for e in exponents]
        return jnp.stack([jnp.concatenate(row) for row in rows])

    tabs = []
    for k in (1, 2, 4):
        has_partner = [(s < 8 - k) if reverse else (s >= k) for s in range(8)]
        tabs.append(table([k if ok else None for ok in has_partner]))
    tabs.append(table([s if reverse else 7 - s for s in range(8)]))
    tabs.append(table([8 - s if reverse else s + 1 for s in range(8)]))
    tabs.append(table([8] * 8))
    return jnp.stack(tabs)


def _cmul(ar, ai, br, bi):
    return ar * br - ai * bi, ar * bi + ai * br


def _scan_group(ur, ui, cr, ci, tr_ref, ti_ref, reverse):
    xr, xi = ur, ui
    for n, k in enumerate((1, 2, 4)):
        shift = 8 - k if reverse else k
        pr, pi = _cmul(tr_ref[n], ti_ref[n], pltpu.roll(xr, shift, axis=0), pltpu.roll(xi, shift, axis=0))
        xr, xi = xr + pr, xi + pi
    sr, si = _cmul(tr_ref[3], ti_ref[3], ur, ui)
    for k in (1, 2, 4):
        sr, si = sr + pltpu.roll(sr, k, axis=0), si + pltpu.roll(si, k, axis=0)
    pr, pi = _cmul(tr_ref[4], ti_ref[4], cr, ci)
    nr, ni = _cmul(tr_ref[5], ti_ref[5], cr, ci)
    return xr + pr, xi + pi, nr + sr, ni + si


def _table_halves(t_ref):
    return t_ref.at[:, :, pl.ds(0, SSM_NS)], t_ref.at[:, :, pl.ds(SSM_NS, SSM_NS)]


_U_BLOCK = (IN_W - SSM_W) // SSM_W


def _ssm_fwd(z, bdt, cd, tabs, name):
    s = z.shape[0]
    ns = SSM_NS
    n_t = s // SCAN_T

    def body(u_ref, b_ref, c_ref, t_ref, xr_ref, xi_ref, y_ref, cr_ref, ci_ref, ur_ref, ui_ref):
        @pl.when(pl.program_id(0) == 0)
        def _():
            cr_ref[...] = jnp.zeros_like(cr_ref)
            ci_ref[...] = jnp.zeros_like(ci_ref)

        bu = lax.dot_general(u_ref[...].astype(BF16), b_ref[...], _DIMS["nt"], preferred_element_type=F32)
        ur_ref[...] = bu[:, :ns]
        ui_ref[...] = bu[:, ns:]
        tr_ref, ti_ref = _table_halves(t_ref)

        def group(g, carry):
            rows = pl.ds(pl.multiple_of(g * 8, 8), 8)
            xr, xi, cr, ci = _scan_group(ur_ref[rows, :], ui_ref[rows, :], *carry, tr_ref, ti_ref, False)
            xr_ref[rows, :] = xr
            xi_ref[rows, :] = xi
            return cr, ci

        cr, ci = lax.fori_loop(0, SCAN_T // 8, group, (cr_ref[...], ci_ref[...]), unroll=2)
        cr_ref[...] = cr
        ci_ref[...] = ci
        y_ref[...] = (jnp.dot(xr_ref[...].astype(BF16), c_ref[0:ns, :], preferred_element_type=F32)
                      + jnp.dot(xi_ref[...].astype(BF16), c_ref[ns:, :], preferred_element_type=F32))

    x_spec = pl.BlockSpec((SCAN_T, ns), lambda t: (t, 0))
    return pl.pallas_call(
        body, name=name, grid=(n_t,),
        in_specs=[pl.BlockSpec((SCAN_T, SSM_W), lambda t: (t, _U_BLOCK)), _fb((2 * ns, SSM_W)),
                  _fb((2 * ns, SSM_W)), _fb((N_SCAN_TABLES, 8, 2 * ns))],
        out_specs=[x_spec, x_spec, _rb(SCAN_T, SSM_W)],
        out_shape=[_sds((s, ns)), _sds((s, ns)), _sds((s, SSM_W))],
        scratch_shapes=[pltpu.VMEM((8, ns), F32)] * 2 + [pltpu.VMEM((SCAN_T, ns), F32)] * 2,
        compiler_params=_params(("arbitrary",)),
    )(z, bdt, cd, tabs)


def _ssm_bwd(dy1, dud, z, xr, xi, bdt, cd, tabs, name):
    s = z.shape[0]
    ns = SSM_NS
    n_t = s // SCAN_T
    n_groups = SCAN_T // 8

    def body(dy_ref, dud_ref, u_ref, xr_ref, xi_ref, pxr_ref, pxi_ref, b_ref, c_ref, t_ref,
             du_ref, dbd_ref, dcd_ref, dar_ref, dai_ref,
             cr_ref, ci_ref, ar_ref, ai_ref, sxr_ref, sxi_ref, gr_ref, gi_ref, lr_ref, li_ref, bacc_ref, cacc_ref):
        t = pl.program_id(0)

        @pl.when(t == 0)
        def _():
            for ref in (cr_ref, ci_ref, ar_ref, ai_ref, bacc_ref, cacc_ref):
                ref[...] = jnp.zeros_like(ref)

        dyb = dy_ref[...]
        g = lax.dot_general(dyb, c_ref[...], _DIMS["nt"], preferred_element_type=F32)
        gr_ref[...] = g[:, :ns]
        gi_ref[...] = g[:, ns:]
        has_before = (t < n_t - 1).astype(F32)
        sxr_ref[0:8, :] = pxr_ref[...] * has_before
        sxi_ref[0:8, :] = pxi_ref[...] * has_before
        sxr_ref[8:, :] = xr_ref[...]
        sxi_ref[8:, :] = xi_ref[...]
        first_row = lax.broadcasted_iota(jnp.int32, (8, ns), 0) == 0
        tr_ref, ti_ref = _table_halves(t_ref)

        def group(k, carry):
            cr, ci, ar, ai = carry
            g8 = pl.multiple_of((n_groups - 1 - k) * 8, 8)
            rows = pl.ds(g8, 8)
            lr, li, cr, ci = _scan_group(gr_ref[rows, :], gi_ref[rows, :], cr, ci, tr_ref, ti_ref, True)
            lr_ref[rows, :] = lr
            li_ref[rows, :] = li
            here, before = pl.ds(g8 + 8, 8), rows
            pr = jnp.where(first_row, pltpu.roll(sxr_ref[before, :], 1, axis=0), pltpu.roll(sxr_ref[here, :], 1, axis=0))
            pi = jnp.where(first_row, pltpu.roll(sxi_ref[before, :], 1, axis=0), pltpu.roll(sxi_ref[here, :], 1, axis=0))
            return cr, ci, ar + lr * pr + li * pi, ai + li * pr - lr * pi

        cr, ci, ar, ai = lax.fori_loop(0, n_groups, group,
                                       (cr_ref[...], ci_ref[...], ar_ref[...], ai_ref[...]), unroll=2)
        cr_ref[...] = cr
        ci_ref[...] = ci
        ar_ref[...] = ar
        ai_ref[...] = ai
        lrb = lr_ref[...].astype(BF16)
        lib = li_ref[...].astype(BF16)
        ub = u_ref[...].astype(BF16)
        du_ref[...] = (dud_ref[...] + jnp.dot(lrb, b_ref[0:ns, :], preferred_element_type=F32)
                       + jnp.dot(lib, b_ref[ns:, :], preferred_element_type=F32))
        bacc_ref[0:ns, :] += lax.dot_general(lrb, ub, _DIMS["tn"], preferred_element_type=F32)
        bacc_ref[ns:, :] += lax.dot_general(lib, ub, _DIMS["tn"], preferred_element_type=F32)
        cacc_ref[0:ns, :] += lax.dot_general(xr_ref[...].astype(BF16), dyb, _DIMS["tn"], preferred_element_type=F32)
        cacc_ref[ns:, :] += lax.dot_general(xi_ref[...].astype(BF16), dyb, _DIMS["tn"], preferred_element_type=F32)

        @pl.when(t == n_t - 1)
        def _():
            for k in (1, 2, 4):
                ar_ref[...] += pltpu.roll(ar_ref[...], k, axis=0)
                ai_ref[...] += pltpu.roll(ai_ref[...], k, axis=0)
            dar_ref[...] = ar_ref[...]
            dai_ref[...] = ai_ref[...]
            dbd_ref[...] = bacc_ref[...]
            dcd_ref[...] = cacc_ref[...]

    rev = lambda t: n_t - 1 - t
    row_spec = pl.BlockSpec((SCAN_T, SSM_W), lambda t: (rev(t), 0))
    x_spec = pl.BlockSpec((SCAN_T, ns), lambda t: (rev(t), 0))
    before_spec = pl.BlockSpec((8, ns), lambda t: (jnp.maximum(rev(t) * (SCAN_T // 8) - 1, 0), 0))
    du, dbd, dcd, dar, dai = pl.pallas_call(
        body, name=name, grid=(n_t,),
        in_specs=[row_spec, row_spec, pl.BlockSpec((SCAN_T, SSM_W), lambda t: (rev(t), _U_BLOCK)),
                  x_spec, x_spec, before_spec, before_spec,
                  _fb((2 * ns, SSM_W)), _fb((2 * ns, SSM_W)), _fb((N_SCAN_TABLES, 8, 2 * ns))],
        out_specs=[row_spec, _fb((2 * ns, SSM_W)), _fb((2 * ns, SSM_W)), _fb((8, ns)), _fb((8, ns))],
        out_shape=[_sds((s, SSM_W)), _sds((2 * ns, SSM_W)), _sds((2 * ns, SSM_W)), _sds((8, ns)), _sds((8, ns))],
        scratch_shapes=([pltpu.VMEM((8, ns), F32)] * 4 + [pltpu.VMEM((SCAN_T + 8, ns), F32)] * 2
                        + [pltpu.VMEM((SCAN_T, ns), F32)] * 4 + [pltpu.VMEM((2 * ns, SSM_W), F32)] * 2),
        compiler_params=_params(("arbitrary",)),
    )(dy1, dud, z, xr, xi, xr, xi, bdt, cd, tabs)
    return du, dbd, dcd, dar[0], dai[0]


def _group_ids():
    return lax.broadcasted_iota(jnp.int32, (1, SGU_W), 1) // 64


def _group_mean(val, gid):
    out = jnp.zeros_like(val)
    for g in range(SGU_GROUPS):
        mg = gid == g
        out = jnp.where(mg, jnp.sum(jnp.where(mg, val, 0.0), axis=1, keepdims=True) * (1.0 / 64), out)
    return out


def _causal_w(w_ref, g):
    t = lax.broadcasted_iota(jnp.int32, (SGU_CHUNK, SGU_CHUNK), 0)
    s = lax.broadcasted_iota(jnp.int32, (SGU_CHUNK, SGU_CHUNK), 1)
    return jnp.where(t >= s, w_ref[g], 0.0).astype(BF16)


def _sgu_core(x, lng, lnb, w_ref, bexp, gid):
    zz = _gelu(x)
    u = zz[:, :SGU_W]
    v = zz[:, SGU_W:]
    vc = v - _group_mean(v, gid)
    rstd = lax.rsqrt(_group_mean(vc * vc, gid) + EPS)
    vhat = vc * rstd
    vn = vhat * lng + lnb
    vnb = vn.astype(BF16)
    mixed = bexp
    for g in range(SGU_GROUPS):
        mm = jnp.dot(_causal_w(w_ref, g), vnb, preferred_element_type=F32)
        mixed = jnp.where(gid == g, mm + bexp, mixed)
    return u, rstd, vhat, vnb, mixed


def _sgu_fwd(z, lng, lnb, w, bexp, name, tm=1024):
    s = z.shape[0]

    def body(z_ref, lng_ref, lnb_ref, w_ref, b_ref, o_ref):
        gid = _group_ids()
        for j in range(tm // SGU_CHUNK):
            rows = pl.ds(j * SGU_CHUNK, SGU_CHUNK)
            u, _, _, _, mixed = _sgu_core(z_ref[rows, :], lng_ref[...], lnb_ref[...], w_ref, b_ref[...], gid)
            o_ref[rows, :] = u * mixed

    return _rowcall(body, name, s, tm,
                    [_rb(tm, 512, 3), _fb((1, 256)), _fb((1, 256)), _fb((4, 128, 128)), _fb((128, 256))],
                    (z, lng.reshape(1, 256), lnb.reshape(1, 256), w, bexp), _rb(tm, 256), _sds((s, 256)))


def _sgu_bwd(z, dy, lng, lnb, w, bexp, name, tm=1024):
    s = z.shape[0]

    def body(z_ref, dy_ref, lng_ref, lnb_ref, w_ref, b_ref, dz_ref, dw_ref, db_ref, dlng_ref, dlnb_ref):
        @pl.when(pl.program_id(0) == 0)
        def _():
            dw_ref[...] = jnp.zeros_like(dw_ref)
            db_ref[...] = jnp.zeros_like(db_ref)
            dlng_ref[...] = jnp.zeros_like(dlng_ref)
            dlnb_ref[...] = jnp.zeros_like(dlnb_ref)

        gid = _group_ids()
        t = lax.broadcasted_iota(jnp.int32, (SGU_CHUNK, SGU_CHUNK), 0)
        sidx = lax.broadcasted_iota(jnp.int32, (SGU_CHUNK, SGU_CHUNK), 1)
        lng_v = lng_ref[...]
        for j in range(tm // SGU_CHUNK):
            rows = pl.ds(j * SGU_CHUNK, SGU_CHUNK)
            x = z_ref[rows, :]
            u, rstd, vhat, vnb, mixed = _sgu_core(x, lng_v, lnb_ref[...], w_ref, b_ref[...], gid)
            dyv = dy_ref[rows, :]
            dmixed = dyv * u
            du = dyv * mixed
            db_ref[...] += dmixed
            dvn = jnp.zeros_like(dmixed)
            for g in range(SGU_GROUPS):
                dmg = jnp.where(gid == g, dmixed, 0.0).astype(BF16)
                dvn = dvn + lax.dot_general(_causal_w(w_ref, g), dmg, _DIMS["tn"], preferred_element_type=F32)
                dwg = lax.dot_general(dmg, vnb, _DIMS["nt"], preferred_element_type=F32)
                dw_ref[g] += jnp.where(t >= sidx, dwg, 0.0)
            dlnb_ref[...] += jnp.sum(dvn, axis=0, keepdims=True)
            dlng_ref[...] += jnp.sum(dvn * vhat, axis=0, keepdims=True)
            dvh = dvn * lng_v
            dv = rstd * (dvh - _group_mean(dvh, gid) - vhat * _group_mean(dvh * vhat, gid))
            gg = _gelu_grad(x)
            dz_ref[rows, 0:SGU_W] = du * gg[:, :SGU_W]
            dz_ref[rows, SGU_W:2 * SGU_W] = dv * gg[:, SGU_W:]

    dz, dw, db, dlng, dlnb = _rowcall(
        body, name, s, tm,
        [_rb(tm, 512, 3), _rb(tm, 256), _fb((1, 256)), _fb((1, 256)), _fb((4, 128, 128)), _fb((128, 256))],
        (z, dy, lng.reshape(1, 256), lnb.reshape(1, 256), w, bexp),
        [_rb(tm, 512), _fb((4, 128, 128)), _fb((128, 256)), _fb((1, 256)), _fb((1, 256))],
        [_sds((s, 512)), _sds((4, 128, 128)), _sds((128, 256)), _sds((1, 256)), _sds((1, 256))])
    return dz, dw, db, dlng.reshape(256), dlnb.reshape(256)


CONV_TC = 1408
N_CT = D_FF // CONV_TC


def _row_of(block8, j):
    r = lax.broadcasted_iota(jnp.int32, block8.shape, 0)
    return jnp.sum(jnp.where(r == j, block8, 0.0), axis=0, keepdims=True)


EDGE = 16


def _conv_fwd(hu, cw, cb, name, tm=512):
    s = hu.shape[0]

    def body(xv_ref, xg_ref, tv_ref, tg_ref, wv_ref, wg_ref, bv_ref, bg_ref, hv_ref, hg_ref, act_ref):
        has_prev = (pl.program_id(1) > 0).astype(F32)
        row = lax.broadcasted_iota(jnp.int32, (EDGE, CONV_TC), 0)

        def conv(x_ref, t_ref, w_ref, b_ref):
            x = x_ref[...].astype(F32)
            w0, w1, w2, bb = w_ref[0:1, :], w_ref[1:2, :], w_ref[2:3, :], b_ref[...]
            whole = w0 * pltpu.roll(x, 2, axis=0) + w1 * pltpu.roll(x, 1, axis=0) + w2 * x + bb
            tail = t_ref[...].astype(F32)
            r7 = _row_of(tail, EDGE - 1) * has_prev
            r6 = _row_of(tail, EDGE - 2) * has_prev
            xe = x_ref[0:EDGE, :].astype(F32)
            x1 = jnp.where(row == 0, r7, pltpu.roll(xe, 1, axis=0))
            x2 = jnp.where(row == 0, r6, jnp.where(row == 1, r7, pltpu.roll(xe, 2, axis=0)))
            return whole, w0 * x2 + w1 * x1 + w2 * xe + bb

        hv, hv_edge = conv(xv_ref, tv_ref, wv_ref, bv_ref)
        hg, hg_edge = conv(xg_ref, tg_ref, wg_ref, bg_ref)
        hv_ref[...] = hv.astype(BF16)
        hg_ref[...] = hg.astype(BF16)
        act_ref[...] = (_gelu(hg) * hv).astype(BF16)
        hv_ref[0:EDGE, :] = hv_edge.astype(BF16)
        hg_ref[0:EDGE, :] = hg_edge.astype(BF16)
        act_ref[0:EDGE, :] = (_gelu(hg_edge) * hv_edge).astype(BF16)

    def xs(off):
        return pl.BlockSpec((tm, CONV_TC), lambda j, i: (i, j + off))

    def ts(off):
        return pl.BlockSpec((EDGE, CONV_TC), lambda j, i: (jnp.maximum(i * (tm // EDGE) - 1, 0), j + off))

    def ws(rows, off):
        return pl.BlockSpec((rows, CONV_TC), lambda j, i: (0, j + off))

    o_spec = pl.BlockSpec((tm, CONV_TC), lambda j, i: (i, j))
    return pl.pallas_call(
        body, name=name, grid=(N_CT, s // tm),
        in_specs=[xs(0), xs(N_CT), ts(0), ts(N_CT), ws(3, 0), ws(3, N_CT), ws(1, 0), ws(1, N_CT)],
        out_specs=[o_spec] * 3, out_shape=[_sds((s, D_FF), BF16)] * 3,
        compiler_params=_params(("parallel", "arbitrary")),
    )(hu, hu, hu, hu, cw, cw, cb.reshape(1, 2 * D_FF), cb.reshape(1, 2 * D_FF))


HALO = EDGE


def _conv_bwd(dact, hv, hg, hu, cw, name, tm=512):
    s = dact.shape[0]

    def body(da_ref, dan_ref, hv_ref, hvn_ref, hg_ref, hgn_ref, x_ref, t_ref, w_ref, dx_ref, dw_ref, db_ref, d_scr):
        i = pl.program_id(1)
        is_value = pl.program_id(0) < N_CT

        @pl.when(i == 0)
        def _():
            dw_ref[...] = jnp.zeros_like(dw_ref)
            db_ref[...] = jnp.zeros_like(db_ref)

        for rows, (a_ref, v_ref, g_ref) in ((pl.ds(0, tm), (da_ref, hv_ref, hg_ref)),
                                            (pl.ds(tm, HALO), (dan_ref, hvn_ref, hgn_ref))):
            @pl.when(is_value)
            def _():
                d_scr[rows, :] = a_ref[...].astype(F32) * _gelu(g_ref[...].astype(F32))

            @pl.when(jnp.logical_not(is_value))
            def _():
                d_scr[rows, :] = (a_ref[...].astype(F32) * v_ref[...].astype(F32)
                                  * _gelu_grad(g_ref[...].astype(F32)))

        has_prev = (i > 0).astype(F32)
        has_next = (i < s // tm - 1).astype(F32)
        w0, w1, w2 = w_ref[0:1, :], w_ref[1:2, :], w_ref[2:3, :]
        d = d_scr[0:tm, :]
        dx_ref[...] = (w2 * d + w1 * pltpu.roll(d, tm - 1, axis=0) + w0 * pltpu.roll(d, tm - 2, axis=0)).astype(BF16)
        row = lax.broadcasted_iota(jnp.int32, (EDGE, CONV_TC), 0)
        nxt = d_scr[tm:tm + HALO, :]
        n0 = _row_of(nxt, 0) * has_next
        n1 = _row_of(nxt, 1) * has_next
        de = d_scr[tm - EDGE:tm, :]
        d1 = jnp.where(row == EDGE - 1, n0, pltpu.roll(de, EDGE - 1, axis=0))
        d2 = jnp.where(row == EDGE - 2, n0, jnp.where(row == EDGE - 1, n1, pltpu.roll(de, EDGE - 2, axis=0)))
        dx_ref[tm - EDGE:tm, :] = (w2 * de + w1 * d1 + w0 * d2).astype(BF16)
        x = x_ref[...].astype(F32)
        tail = t_ref[...].astype(F32)
        r7 = _row_of(tail, EDGE - 1) * has_prev
        r6 = _row_of(tail, EDGE - 2) * has_prev
        last = x_ref[tm - EDGE:tm, :].astype(F32)
        l7, l6 = _row_of(last, EDGE - 1), _row_of(last, EDGE - 2)
        head = d_scr[0:8, :]
        d0, d1h = _row_of(head, 0), _row_of(head, 1)
        dw_ref[0:1, :] += (jnp.sum(d * pltpu.roll(x, 2, axis=0), axis=0, keepdims=True)
                           + d0 * (r6 - l6) + d1h * (r7 - l7))
        dw_ref[1:2, :] += jnp.sum(d * pltpu.roll(x, 1, axis=0), axis=0, keepdims=True) + d0 * (r7 - l7)
        dw_ref[2:3, :] += jnp.sum(d * x, axis=0, keepdims=True)
        db_ref[...] += jnp.sum(d, axis=0, keepdims=True)

    a_spec = pl.BlockSpec((tm, CONV_TC), lambda j, i: (i, j % N_CT))
    an_spec = pl.BlockSpec((HALO, CONV_TC),
                           lambda j, i: (jnp.minimum((i + 1) * (tm // HALO), s // HALO - 1), j % N_CT))
    x_spec = pl.BlockSpec((tm, CONV_TC), lambda j, i: (i, j))
    t_spec = pl.BlockSpec((EDGE, CONV_TC), lambda j, i: (jnp.maximum(i * (tm // EDGE) - 1, 0), j))
    w_spec = pl.BlockSpec((3, CONV_TC), lambda j, i: (0, j))
    db_spec = pl.BlockSpec((1, CONV_TC), lambda j, i: (0, j))
    return pl.pallas_call(
        body, name=name, grid=(2 * N_CT, s // tm),
        in_specs=[a_spec, an_spec, a_spec, an_spec, a_spec, an_spec, x_spec, t_spec, w_spec],
        out_specs=[x_spec, w_spec, db_spec],
        out_shape=[_sds((s, 2 * D_FF), BF16), _sds((3, 2 * D_FF)), _sds((1, 2 * D_FF))],
        scratch_shapes=[pltpu.VMEM((tm + HALO, CONV_TC), F32)],
        compiler_params=_params(("parallel", "arbitrary")),
    )(dact, dact, hv, hv, hg, hg, hu, hu, cw)


def _ple_fwd(h, gp, pp, next_gain, name, tm=1024):
    s, d = h.shape
    with_norm = next_gain is not None

    def body(*refs):
        h_ref, g_ref, p_ref = refs[:3]
        out = h_ref[...] + _sigmoid(g_ref[...].astype(F32)) * p_ref[...].astype(F32)
        if with_norm:
            n_ref, o_ref, a_ref = refs[3:]
            scale = lax.rsqrt(jnp.mean(out * out, axis=-1, keepdims=True) + EPS)
            a_ref[...] = (out * scale * n_ref[...]).astype(BF16)
        else:
            o_ref, = refs[3:]
        o_ref[...] = out

    if not with_norm:
        return _rowcall(body, name, s, tm, [_rb(tm, d)] * 3, (h, gp, pp), _rb(tm, d), _sds((s, d))), None
    return _rowcall(body, name, s, tm, [_rb(tm, d)] * 3 + [_fb((1, d))], (h, gp, pp, next_gain.reshape(1, d)),
                    [_rb(tm, d)] * 2, [_sds((s, d)), _sds((s, d), BF16)])


def _ple_bwd(dh, gp, pp, name, tm=1024):
    s, d = dh.shape

    def body(d_ref, g_ref, p_ref, dp_ref, dg_ref):
        sg = _sigmoid(g_ref[...].astype(F32))
        dv = d_ref[...]
        dp_ref[...] = (dv * sg).astype(BF16)
        dg_ref[...] = (dv * p_ref[...].astype(F32) * sg * (1.0 - sg)).astype(BF16)

    return _rowcall(body, name, s, tm, [_rb(tm, d)] * 3, (dh, gp, pp), [_rb(tm, d)] * 2,
                    [_sds((s, d), BF16)] * 2)


SCALE = HEAD_DIM ** -0.5
ATT_ROWS = 2048


def _att_geom(s, dil):
    w = min(ATT_ROWS, s)
    p = BLK * dil
    assert w % p == 0 and s % w == 0
    return w, p, w // p


def _rows(start, dil):
    return pl.ds(start, BLK, stride=dil) if dil > 1 else pl.ds(start, BLK)


def _head_masks():
    lane = lax.broadcasted_iota(jnp.int32, (1, BLK), 1)
    return [lane < HEAD_DIM, lane >= HEAD_DIM]


def _band():
    rel = np.arange(BLK)[:, None] + BLK - np.arange(2 * BLK)[None, :]
    return (rel >= 0) & (rel <= BLK)


def _zcur(w):
    return lambda off: pl.BlockSpec((w, BLK), lambda hp, i: (i, off + hp))


def _zprev(p, nb):
    return lambda off: pl.BlockSpec((p, BLK), lambda hp, i: (jnp.maximum(i * nb - 1, 0), off + hp))


def _scur(w):
    return pl.BlockSpec((w, BLK), lambda hp, i: (i, hp))


def _pair_rows(t, masks):
    return jnp.concatenate([jnp.where(masks[0], t, 0.0), jnp.where(masks[1], t, 0.0)], axis=0).astype(BF16)


def _pair_bias_bwd(bias):
    return bias.reshape(4, 2, BLK, 2, BLK).transpose(0, 3, 2, 1, 4).reshape(4, 2, BLK, 2 * BLK)


def _unpair_bias_bwd(db):
    return db.reshape(4, 2, BLK, 2, BLK).transpose(0, 3, 2, 1, 4).reshape(N_HEADS, BLK, 2 * BLK)


def _attn_fwd(z, biases, name):
    s = z.shape[0]
    w = min(ATT_ROWS, s)
    n_br = len(BRANCHES)

    def body(*refs):
        q_ref, kp_ref, kc_ref, vp_ref, vc_ref = refs[:5]
        b_refs = refs[5:5 + n_br]
        y_ref, lse_ref, m_ref, l_ref, a_ref = refs[5 + n_br:]
        i = pl.program_id(1)
        masks = _head_masks()
        own_block = lax.broadcasted_iota(jnp.int32, (1, 2 * BLK), 1) >= BLK
        for n, (_, dil) in enumerate(BRANCHES):
            _, p, nb = _att_geom(s, dil)
            for r in range(dil):
                for b in range(nb):
                    rows = _rows(r + p * b, dil)
                    prev_rows = _rows(r + p * (b - 1), dil) if b > 0 else _rows(w - p + r, dil)
                    kprev, vprev = (kc_ref, vc_ref) if b > 0 else (kp_ref, vp_ref)
                    q = q_ref[rows, :] * SCALE
                    k = jnp.concatenate([kprev[prev_rows, :], kc_ref[rows, :]], axis=0).astype(BF16)
                    v = jnp.concatenate([vprev[prev_rows, :], vc_ref[rows, :]], axis=0).astype(BF16)
                    mb = lb = ob = None
                    for hh, mh in enumerate(masks):
                        qh = jnp.where(mh, q, 0.0).astype(BF16)
                        sc = lax.dot_general(qh, k, _DIMS["nt"], preferred_element_type=F32) + b_refs[n][hh]
                        if b == 0:
                            sc = jnp.where(own_block | (i > 0), sc, NEG_INF)
                        mx = jnp.max(sc, axis=1, keepdims=True)
                        e = jnp.exp(sc - mx)
                        den = jnp.sum(e, axis=1, keepdims=True)
                        o = jnp.dot(e.astype(BF16), v, preferred_element_type=F32)
                        if hh == 0:
                            mb = jnp.broadcast_to(mx, (BLK, BLK))
                            lb = jnp.broadcast_to(den, (BLK, BLK))
                            ob = o
                        else:
                            mb = jnp.where(mh, mx, mb)
                            lb = jnp.where(mh, den, lb)
                            ob = jnp.where(mh, o, ob)
                    if n == 0:
                        m_new, l_new, a_new = mb, lb, ob
                    else:
                        m_old = m_ref[rows, :]
                        m_new = jnp.maximum(m_old, mb)
                        al = jnp.exp(m_old - m_new)
                        be = jnp.exp(mb - m_new)
                        l_new = al * l_ref[rows, :] + be * lb
                        a_new = al * a_ref[rows, :] + be * ob
                    if n == n_br - 1:
                        y_ref[rows, :] = a_new / l_new
                        lse_ref[rows, :] = m_new + jnp.log(l_new)
                    else:
                        m_ref[rows, :] = m_new
                        l_ref[rows, :] = l_new
                        a_ref[rows, :] = a_new

    cur, prv = _zcur(w), _zprev(w, 1)
    b_spec = pl.BlockSpec((2, BLK, 2 * BLK), lambda hp, i: (hp, 0, 0))
    return pl.pallas_call(
        body, name=name, grid=(4, s // w), in_specs=[cur(0), prv(4), cur(4), prv(8), cur(8)] + [b_spec] * n_br,
        out_specs=[_scur(w)] * 2, out_shape=[_sds((s, ATTN_W))] * 2,
        scratch_shapes=[pltpu.VMEM((w, BLK), F32)] * 3,
        compiler_params=_params(("parallel", "parallel")),
    )(z, z, z, z, z, *biases)


def _row_stats(mh, dy, y, lse):
    delta = jnp.sum(jnp.where(mh, dy * y, 0.0), axis=1, keepdims=True)
    lse_h = jnp.max(jnp.where(mh, lse, NEG_INF), axis=1, keepdims=True)
    return delta, lse_h


def _attn_bwd(z, biases, dy, y, lse, name):
    s = z.shape[0]
    w = min(ATT_ROWS, s)
    n_steps = s // w
    n_br = len(BRANCHES)

    def body(*refs):
        q_ref, kp_ref, kc_ref, vp_ref, vc_ref, dy_ref, y_ref, lse_ref = refs[:8]
        b_refs = refs[8:8 + n_br]
        outs = refs[8 + n_br:]
        dq_ref, dk_ref, dv_ref = outs[:3]
        x_refs = outs[3:3 + 2 * n_br]
        db_refs = outs[3 + 2 * n_br:3 + 3 * n_br]
        acc_refs = outs[3 + 3 * n_br:]
        i = pl.program_id(1)

        @pl.when(i == 0)
        def _():
            for ref in db_refs:
                ref[...] = jnp.zeros_like(ref)

        masks = _head_masks()
        first_head = lax.broadcasted_iota(jnp.int32, (1, 2 * BLK), 1) < BLK

        sums = {"q": (acc_refs[0], dq_ref), "k": (acc_refs[1], dk_ref), "v": (acc_refs[2], dv_ref)}

        def add_up(n, rows, **vals):
            for key, val in vals.items():
                acc_ref, out_ref = sums[key]
                if n > 0:
                    val = val + acc_ref[rows, :]
                if n == n_br - 1:
                    out_ref[rows, :] = val
                else:
                    acc_ref[rows, :] = val

        for n, (_, dil) in enumerate(BRANCHES):
            _, p, nb = _att_geom(s, dil)
            b_ref, db_ref = b_refs[n], db_refs[n]
            dkx_ref, dvx_ref = x_refs[2 * n], x_refs[2 * n + 1]
            for r in range(dil):
                carry = None
                for b in range(nb):
                    rows = _rows(r + p * b, dil)
                    prev_rows = _rows(r + p * (b - 1), dil) if b > 0 else _rows(w - p + r, dil)
                    kprev, vprev = (kc_ref, vc_ref) if b > 0 else (kp_ref, vp_ref)
                    keys = [(_pair_rows(kprev[prev_rows, :], masks), _pair_rows(vprev[prev_rows, :], masks)),
                            (_pair_rows(kc_ref[rows, :], masks), _pair_rows(vc_ref[rows, :], masks))]
                    q = (q_ref[rows, :] * SCALE).astype(BF16)
                    dy_v = dy_ref[rows, :]
                    dyb = dy_v.astype(BF16)
                    stats = [_row_stats(mh, dy_v, y_ref[rows, :], lse_ref[rows, :]) for mh in masks]
                    delta = jnp.where(first_head, stats[0][0], stats[1][0])
                    lse_h = jnp.where(first_head, stats[0][1], stats[1][1])
                    dq = jnp.zeros((BLK, BLK), F32)
                    dk, dv = [], []
                    for half in range(2):
                        kh, vh = keys[half]
                        sc = lax.dot_general(q, kh, _DIMS["nt"], preferred_element_type=F32) + b_ref[half]
                        pr = jnp.exp(sc - lse_h)
                        if b == 0 and half == 0:
                            pr = pr * (i > 0).astype(F32)
                        dp = lax.dot_general(dyb, vh, _DIMS["nt"], preferred_element_type=F32)
                        ds = pr * (dp - delta)
                        db_ref[half] += ds
                        dsb = ds.astype(BF16)
                        dq = dq + jnp.dot(dsb, kh, preferred_element_type=F32)
                        dk2 = lax.dot_general(dsb, q, _DIMS["tn"], preferred_element_type=F32)
                        dv2 = lax.dot_general(pr.astype(BF16), dyb, _DIMS["tn"], preferred_element_type=F32)
                        dk.append(jnp.where(masks[0], dk2[:BLK], dk2[BLK:]))
                        dv.append(jnp.where(masks[0], dv2[:BLK], dv2[BLK:]))
                    add_up(n, rows, q=dq * SCALE)
                    if b > 0:
                        add_up(n, _rows(r + p * (b - 1), dil), k=carry[0] + dk[0], v=carry[1] + dv[0])
                    else:
                        dkx_ref[_rows(r, dil), :] = dk[0]
                        dvx_ref[_rows(r, dil), :] = dv[0]
                    carry = (dk[1], dv[1])
                add_up(n, _rows(r + p * (nb - 1), dil), k=carry[0], v=carry[1])

    cur, prv = _zcur(w), _zprev(w, 1)
    b_spec = pl.BlockSpec((None, 2, BLK, 2 * BLK), lambda hp, i: (hp, 0, 0, 0))
    x_specs, x_shapes = [], []
    for _, dil in BRANCHES:
        p = BLK * dil
        x_specs += [pl.BlockSpec((p, BLK), lambda hp, i: (i, hp))] * 2
        x_shapes += [_sds((n_steps * p, ATTN_W))] * 2
    outs = pl.pallas_call(
        body, name=name, grid=(4, n_steps),
        in_specs=[cur(0), prv(4), cur(4), prv(8), cur(8)] + [_scur(w)] * 3 + [b_spec] * n_br,
        out_specs=[_scur(w)] * 3 + x_specs + [b_spec] * n_br,
        out_shape=[_sds((s, ATTN_W))] * 3 + x_shapes + [_sds((4, 2, BLK, 2 * BLK))] * n_br,
        scratch_shapes=[pltpu.VMEM((w, BLK), F32)] * 3,
        compiler_params=_params(("parallel", "arbitrary")),
    )(z, z, z, z, z, dy, y, lse, *biases)
    dq, dk, dv = outs[:3]
    extras = [(outs[3 + 2 * n], outs[4 + 2 * n]) for n in range(n_br)]
    return dq, dk, dv, extras, [_unpair_bias_bwd(db) for db in outs[3 + 2 * n_br:]]


ASM_ROWS = 512


def _assemble_dz(dq, dk, dv, extras, dzs, du, name):
    s = dq.shape[0]
    w = min(ATT_ROWS, s)
    n_steps = s // w
    per_step = w // ASM_ROWS
    assert w % ASM_ROWS == 0

    def body(*refs):
        dq_ref, dk_ref, dv_ref, dzs_ref, du_ref = refs[:5]
        x_refs = refs[5:5 + 2 * len(extras)]
        o_ref, acc_ref = refs[-2:]
        j = pl.program_id(0)
        step = j // per_step
        has_next = (step < n_steps - 1).astype(F32)
        last_of_step = ((j + 1) % per_step == 0).astype(F32)
        o_ref[:, 0:ATTN_W] = dq_ref[...].astype(BF16)
        o_ref[:, 3 * ATTN_W:3 * ATTN_W + 2 * SGU_W] = dzs_ref[...].astype(BF16)
        o_ref[:, 3 * ATTN_W + 2 * SGU_W:IN_W] = du_ref[...].astype(BF16)
        for part, (base_ref, col) in enumerate(((dk_ref, ATTN_W), (dv_ref, 2 * ATTN_W))):
            acc_ref[...] = base_ref[...]
            for n, (_, dil) in enumerate(BRANCHES):
                rows = min(BLK * dil, ASM_ROWS)
                scale = has_next if BLK * dil >= w else has_next * last_of_step
                acc_ref[ASM_ROWS - rows:, :] += x_refs[2 * n + part][...] * scale
            o_ref[:, col:col + ATTN_W] = acc_ref[...].astype(BF16)

    def x_spec(dil):
        p = BLK * dil
        rows = min(p, ASM_ROWS)
        blocks_per_step = p // rows
        total = n_steps * blocks_per_step

        def idx(j):
            step = j // per_step
            within = (j % per_step) - (per_step - blocks_per_step)
            return (jnp.clip((step + 1) * blocks_per_step + jnp.maximum(within, 0), 0, total - 1), 0)

        return pl.BlockSpec((rows, ATTN_W), idx)

    in_specs = [_rb(ASM_ROWS, ATTN_W)] * 3 + [_rb(ASM_ROWS, 2 * SGU_W), _rb(ASM_ROWS, SSM_W)]
    args = [dq, dk, dv, dzs, du]
    for (dkx, dvx), (_, dil) in zip(extras, BRANCHES):
        in_specs += [x_spec(dil)] * 2
        args += [dkx, dvx]
    return pl.pallas_call(
        body, name=name, grid=(s // ASM_ROWS,), in_specs=in_specs, out_specs=_rb(ASM_ROWS, IN_W),
        out_shape=_sds((s, IN_W), BF16), scratch_shapes=[pltpu.VMEM((ASM_ROWS, ATTN_W), F32)],
        compiler_params=_params(("parallel",)),
    )(*args)


def _t5_bucket(dist):
    max_exact = N_BUCKETS // 2
    d = np.maximum(dist, 0)
    large = max_exact + (np.log(np.maximum(d, 1) / max_exact) / np.log(REL_MAX / max_exact)
                         * (N_BUCKETS - max_exact)).astype(np.int32)
    large = np.minimum(large, N_BUCKETS - 1)
    return np.where(d < max_exact, d, large).astype(np.int32)


def _bias_tables(rel_bias):
    period = 3 * BLK
    tabs = []
    for _, dil in BRANCHES:
        onehot = np.zeros((period, N_BUCKETS), np.float32)
        d = np.arange(BLK + 1)
        onehot[d, _t5_bucket((BLK - d) * dil)] = 1.0
        f = jnp.dot(jnp.asarray(onehot), rel_bias, precision=lax.Precision.HIGHEST)
        flat = jnp.tile(f.T, (1, BLK))[:, :BLK * (period - 1)]
        tab = flat.reshape(N_HEADS, BLK, period - 1)[:, :, :2 * BLK]
        tabs.append(jnp.where(_band()[None], tab, NEG_INF))
    return tabs


def _bucket_onehot():
    maps = []
    q = np.arange(BLK)[:, None]
    k = np.arange(2 * BLK)[None, :]
    rel = q + BLK - k
    for _, dil in BRANCHES:
        maps.append(np.where((rel >= 0) & (rel <= BLK), _t5_bucket(rel * dil), -1).reshape(-1))
    bmap = jnp.asarray(np.concatenate(maps).astype(np.int32))
    return (bmap[:, None] == jnp.arange(128, dtype=jnp.int32)[None, :]).astype(BF16)


def _block_diag(t):
    g, n, c = t.shape
    eye = jnp.eye(g, dtype=t.dtype)
    return (t[:, :, None, :] * eye[:, None, :, None]).reshape(g * n, g * c)


def _ssm_prep(a_re, a_im, log_dt, b_re, b_im, c_re, c_im):
    lam = lax.complex(a_re, a_im)
    dt = jnp.exp(log_dt)[:, None]
    a_bar = jnp.exp(lam * dt)
    b_bar = ((a_bar - 1.0) / lam)[:, :, None] * lax.complex(b_re, b_im)
    bdt = jnp.concatenate([_block_diag(jnp.real(b_bar)), _block_diag(jnp.imag(b_bar))], axis=0)
    cd = jnp.concatenate([_block_diag(jnp.transpose(c_re, (0, 2, 1))),
                          _block_diag(-jnp.transpose(c_im, (0, 2, 1)))], axis=0)
    return jnp.real(a_bar).reshape(-1), jnp.imag(a_bar).reshape(-1), bdt, cd


def _powers(ar, ai):
    pr, pi = ar[:, None], ai[:, None]
    k = 1
    while k < 8:
        lr, li = pr[:, -1:], pi[:, -1:]
        pr, pi = (jnp.concatenate([pr, pr * lr - pi * li], axis=1),
                  jnp.concatenate([pi, pr * li + pi * lr], axis=1))
        k *= 2
    return pr, pi


def _sgu_bias_expand(b):
    return jnp.repeat(b.T, 64, axis=1)


def _layer_fwd(i, h, a1, p_i, big, small, bias_tabs, next_gain):
    nm = "l%d_" % i
    sv = {"h": h}
    if a1 is None:
        a1 = _rms_fwd(h, small["norm_attn_g"][i], nm + "rms_attn")
    z = _mm(a1, big["w_in"], "nt", nm + "in_proj")
    y_attn, lse = _attn_fwd(z, [t[0] for t in bias_tabs], nm + "attn_fwd")
    bexp = _sgu_bias_expand(small["sgu_b"][i])
    y_sgu = _sgu_fwd(z, small["sgu_ln_g"][i], small["sgu_ln_b"][i], small["sgu_w"][i], bexp, nm + "sgu_fwd")
    ar, ai, bdt, cd = _ssm_prep(*[small[k][i] for k in ("ssm_a_re", "ssm_a_im", "ssm_log_dt", "ssm_b_re",
                                                         "ssm_b_im", "ssm_c_re", "ssm_c_im")])
    xr, xi, yc = _ssm_fwd(z, bdt.astype(BF16), cd.astype(BF16), _scan_tables(*_powers(ar, ai), False),
                          nm + "ssm_core")
    y_ssm = _ssm_post_fwd(yc, z, small["ssm_d"][i], big["ssm_glu_w"], small["ssm_glu_b"][i], nm + "ssm_post")
    mix = _mix_fwd(y_attn, y_sgu, y_ssm, small["branch_norm_g"][i], nm + "mix")
    if "rest" in big:
        big = dict({k: t for k, t in big.items() if k != "rest"}, **big["rest"](mix))
    h2, a2 = _mm(mix, big["w_out"], "nn", nm + "out_proj", add=h, norm_gain=small["norm_ffn_g"][i])
    hu = _mm(a2, big["ffn_w_up"], "nt", nm + "ffn_up", out_dtype=BF16)
    hv, hg, act = _conv_fwd(hu, big["ffn_conv_w"], small["ffn_conv_b"][i], nm + "ffn_conv")
    h3, a3 = _mm(act, big["ffn_w_down"], "nn", nm + "ffn_down", add=h2, norm_gain=small["norm_ple_g"][i])
    gp = _mm(a3, big["ple_w_gate"], "nn", nm + "ple_gate", out_dtype=BF16)
    pp = _mm(p_i, big["ple_w_proj"], "nt", nm + "ple_proj", out_dtype=BF16)
    h4, a_next = _ple_fwd(h3, gp, pp, next_gain, nm + "ple_add")
    sv.update(big=big, a1=a1, z=z, y_attn=y_attn, lse=lse, y_sgu=y_sgu, y_ssm=y_ssm, yc=yc, xr=xr, xi=xi, mix=mix, h2=h2,
              a2=a2, hu=hu, hv=hv, hg=hg, act=act, h3=h3, a3=a3, gp=gp, pp=pp)
    return h4, a_next, sv


def _layer_bwd(i, dh4, sv, p_i, big, small, bias_tabs, ffn_done=None):
    nm = "l%d_" % i
    g = {}
    dpp, dgp = _ple_bwd(dh4, sv["gp"], sv["pp"], nm + "ple_bwd")
    g["ple_w_proj"] = _mm(dpp, p_i, "tn", nm + "d_ple_proj", out_dtype=BF16)
    g["ple_w_gate"] = _mm(sv["a3"], dgp, "tn", nm + "d_ple_gate", out_dtype=BF16)
    dh3, dgain = _mm(dgp, big["ple_w_gate"], "nt", nm + "ple_gate_t", add=dh4,
                     norm_bwd=(sv["h3"], small["norm_ple_g"][i]))
    g["norm_ple_g"] = dgain.reshape(D_MODEL)
    g["ffn_w_down"] = _mm(sv["act"], dh3, "tn", nm + "d_ffn_down", out_dtype=BF16)
    dact = _mm(dh3, big["ffn_w_down"], "nt", nm + "ffn_down_t", out_dtype=BF16)
    dhu, g["ffn_conv_w"], dcb = _conv_bwd(dact, sv["hv"], sv["hg"], sv["hu"], big["ffn_conv_w"],
                                          nm + "ffn_conv_bwd")
    g["ffn_conv_b"] = dcb.reshape(2 * D_FF)
    g["ffn_w_up"] = _mm(dhu, sv["a2"], "tn", nm + "d_ffn_up", out_dtype=BF16)
    dh2, dgain = _mm(dhu, big["ffn_w_up"], "nn", nm + "ffn_up_t", add=dh3,
                     norm_bwd=(sv["h2"], small["norm_ffn_g"][i]))
    g["norm_ffn_g"] = dgain.reshape(D_MODEL)
    if ffn_done is not None:
        small = ffn_done(g, small)
    g["w_out"] = _mm(sv["mix"], dh2, "tn", nm + "d_out_proj", out_dtype=BF16)
    dmix = _mm(dh2, big["w_out"], "nt", nm + "out_proj_t")
    dya, dysg, dyss, g["branch_norm_g"] = _mix_bwd(dmix, sv["y_attn"], sv["y_sgu"], sv["y_ssm"],
                                                   small["branch_norm_g"][i], nm + "mix_bwd")
    ssm_keys = ("ssm_a_re", "ssm_a_im", "ssm_log_dt", "ssm_b_re", "ssm_b_im", "ssm_c_re", "ssm_c_im")
    (ar, ai, bdt, cd), prep_vjp = jax.vjp(_ssm_prep, *[small[k][i] for k in ssm_keys])
    dy1, dgl, y2, dud, g["ssm_d"], g["ssm_glu_b"] = _ssm_post_bwd(
        dyss, sv["yc"], sv["z"], small["ssm_d"][i], big["ssm_glu_w"], small["ssm_glu_b"][i], nm + "ssm_post_bwd")
    g["ssm_glu_w"] = _mm(y2, dgl, "tn", nm + "d_ssm_glu", out_dtype=BF16)
    du, dbdt, dcd, dar, dai = _ssm_bwd(dy1, dud, sv["z"], sv["xr"], sv["xi"], bdt.astype(BF16), cd.astype(BF16),
                                       _scan_tables(*_powers(ar, ai), True), nm + "ssm_core_bwd")
    for k, val in zip(ssm_keys, prep_vjp((dar, dai, dbdt, dcd))):
        g[k] = val
    bexp, bexp_vjp = jax.vjp(_sgu_bias_expand, small["sgu_b"][i])
    dzs, g["sgu_w"], dbexp, g["sgu_ln_g"], g["sgu_ln_b"] = _sgu_bwd(
        sv["z"], dysg, small["sgu_ln_g"][i], small["sgu_ln_b"][i], small["sgu_w"][i], bexp, nm + "sgu_bwd")
    g["sgu_b"] = bexp_vjp(dbexp)[0]
    dq, dk, dv, extras, dbs = _attn_bwd(sv["z"], [t[1] for t in bias_tabs], dya, sv["y_attn"], sv["lse"],
                                        nm + "attn_bwd")
    dbs = [db.reshape(N_HEADS, BLK * 2 * BLK) for db in dbs]
    dz = _assemble_dz(dq, dk, dv, extras, dzs, du, nm + "assemble_dz")
    g["w_in"] = _mm(dz, sv["a1"], "tn", nm + "d_in_proj", out_dtype=BF16)
    dh, dgain = _mm(dz, big["w_in"], "nn", nm + "in_proj_t", add=dh2, norm_bwd=(sv["h"], small["norm_attn_g"][i]))
    g["norm_attn_g"] = dgain.reshape(D_MODEL)
    return dh, g, jnp.concatenate(dbs, axis=1)


def _local_step(x, p, target, layer_weights, small, layer_done=None):
    depth = p.shape[0]
    bias_tabs = [(t, _pair_bias_bwd(t)) for t in _bias_tables(small["rel_bias"])]
    h, a1 = x, None
    saved = []
    for i in range(depth):
        next_gain = small["norm_attn_g"][i + 1] if i + 1 < depth else None
        h, a1, sv = _layer_fwd(i, h, a1, p[i], layer_weights(i, h), small, bias_tabs, next_gain)
        saved.append(sv)
    dh, loss, g_final = _loss_head(h, target, small["final_norm_g"], "loss_head")
    layer_grads = [None] * depth
    dbias = [None] * depth
    for i in reversed(range(depth)):
        ffn_done = None if layer_done is None else (lambda g, sm, i=i: layer_done(i, "ffn", g, sm))
        dh, layer_grads[i], dbias[i] = _layer_bwd(i, dh, saved[i], p[i], saved[i]["big"], small, bias_tabs,
                                                  ffn_done)
        if layer_done is not None:
            small = layer_done(i, "all", layer_grads[i], small)
    big_grads = [{k: lg.pop(k) for k in COMM_NAMES} for lg in layer_grads]
    grads = {k: jnp.stack([layer_grads[i][k] for i in range(depth)]) for k in layer_grads[0]}
    grads["final_norm_g"] = g_final
    g_rb = _mm(sum(dbias[1:], dbias[0]), _bucket_onehot(), "nn", "d_rel_bias", tk=2048)
    grads["rel_bias"] = g_rb[:, :N_BUCKETS].T
    return loss, dh, big_grads, grads


_ANY = pl.BlockSpec(memory_space=pl.ANY)
MESH_IDS = pl.DeviceIdType.MESH


def _slot(ref, axis, j):
    return ref.at[(slice(None),) * axis + (j,)]


def _all_gather(blocks, axis, name):
    nt = len(blocks)

    def body(*refs):
        x_refs, o_refs = refs[:nt], refs[nt:2 * nt]
        send_sems, recv_sems, local_sems = refs[2 * nt:]
        x, y, c = lax.axis_index("x"), lax.axis_index("y"), lax.axis_index("c")
        me, sibling = (x, y, c), (x, y, 1 - c)
        chips = [(1 - x, y), (x, 1 - y), (1 - x, 1 - y)]

        def slot(t, px, py, pc):
            return _slot(o_refs[t], axis, 4 * px + 2 * py + pc)

        def copy(t, k, blk, to, src=None):
            return pltpu.make_async_remote_copy(
                src_ref=slot(t, *blk) if src is None else src, dst_ref=slot(t, *blk),
                send_sem=send_sems.at[7 * t + k], recv_sem=recv_sems.at[7 * t + k],
                device_id=to, device_id_type=MESH_IDS)

        mine = [pltpu.make_async_copy(x_refs[t], slot(t, *me), local_sems.at[t]) for t in range(nt)]
        for cp in mine:
            cp.start()
        first = []
        for t in range(nt):
            first.append(copy(t, 0, me, sibling, src=x_refs[t]))
            first += [copy(t, 1 + j, me, (*chip, c), src=x_refs[t]) for j, chip in enumerate(chips)]
        for cp in first:
            cp.start()
        passed = []
        for t in range(nt):
            for j, chip in enumerate(chips):
                copy(t, 1 + j, (*chip, c), me).wait_recv()
                passed.append(copy(t, 4 + j, (*chip, c), sibling))
                passed[-1].start()
        for t in range(nt):
            copy(t, 0, sibling, me).wait_recv()
            for j, chip in enumerate(chips):
                copy(t, 4 + j, (*chip, 1 - c), me).wait_recv()
        for cp in first + passed:
            cp.wait_send()
        for cp in mine:
            cp.wait()

    out_shape = [jax.ShapeDtypeStruct(b.shape[:axis] + (N_DEV,) + b.shape[axis:], b.dtype) for b in blocks]
    return pl.pallas_call(
        body, name=name, out_shape=out_shape, in_specs=[_ANY] * nt, out_specs=[_ANY] * nt,
        scratch_shapes=[pltpu.SemaphoreType.DMA((7 * nt,)), pltpu.SemaphoreType.DMA((7 * nt,)),
                        pltpu.SemaphoreType.DMA((nt,))],
    )(*blocks)


def _peer(k):
    x, y, c = lax.axis_index("x"), lax.axis_index("y"), lax.axis_index("c")
    px = 1 - x if k & 4 else x
    py = 1 - y if k & 2 else y
    pc = 1 - c if k & 1 else c
    return (px, py, pc), 4 * px + 2 * py + pc


_HBM = pl.BlockSpec(memory_space=pltpu.HBM)
_SEM = pl.BlockSpec(memory_space=pltpu.SEMAPHORE)
_EFFECT = pltpu.SideEffectType.DATAFLOW_SIDE_EFFECTING


def _split_copy(src_ref, land_ref, send_sems, recv_sems, t, k, gather):
    peer, idx = _peer(k)
    _, me = _peer(0)
    return pltpu.make_async_remote_copy(
        src_ref=src_ref if gather else src_ref.at[idx], dst_ref=land_ref.at[me],
        send_sem=send_sems.at[7 * t + k - 1], recv_sem=recv_sems.at[7 * t + k - 1],
        device_id=peer, device_id_type=MESH_IDS)


def _exchange_start(srcs, lands, gather, name):
    nt = len(srcs)

    def body(*refs):
        src_refs, land_refs = refs[:nt], refs[nt:2 * nt]
        send_sems, recv_sems = refs[2 * nt:2 * nt + 2]
        token = refs[-1]
        for k in range(1, N_DEV):
            for t in range(nt):
                _split_copy(src_refs[t], land_refs[t], send_sems, recv_sems, t, k, gather).start()
        token[...] = jnp.zeros_like(token)

    hbm = lambda a: pltpu.HBM(a.shape, a.dtype)
    outs = pl.pallas_call(
        body, name=name,
        out_shape=(pltpu.SemaphoreType.DMA((7 * nt,)), pltpu.SemaphoreType.DMA((7 * nt,)),
                   *[hbm(a) for a in srcs], *[hbm(a) for a in lands], jax.ShapeDtypeStruct((8, 128), F32)),
        in_specs=[_HBM] * (2 * nt),
        out_specs=(_SEM, _SEM, *[_HBM] * (2 * nt), pl.BlockSpec(memory_space=pltpu.VMEM)),
        input_output_aliases={j: 2 + j for j in range(2 * nt)},
        compiler_params=pltpu.CompilerParams(has_side_effects=_EFFECT),
    )(*[pltpu.with_memory_space_constraint(a, pltpu.HBM) for a in list(srcs) + list(lands)])
    return outs[0], outs[1], outs[2:2 + nt], outs[2 + nt:2 + 2 * nt], outs[-1]


def _exchange_wait(send_sems, recv_sems, srcs, lands, after, gather, name):
    nt = len(srcs)

    def body(*refs):
        src_refs, land_refs = refs[:nt], refs[nt:2 * nt]
        send_sems, recv_sems = refs[2 * nt:2 * nt + 2]
        for k in range(1, N_DEV):
            _, idx = _peer(k)
            for t in range(nt):
                _split_copy(src_refs[t], land_refs[t], send_sems, recv_sems, t, k, gather).wait_send()
                arrival = pltpu.make_async_remote_copy(
                    src_ref=land_refs[t].at[idx], dst_ref=land_refs[t].at[idx],
                    send_sem=send_sems.at[7 * t + k - 1], recv_sem=recv_sems.at[7 * t + k - 1],
                    device_id=_peer(k)[0], device_id_type=MESH_IDS)
                arrival.wait_recv()

    hbm = lambda a: pltpu.HBM(a.shape, a.dtype)
    outs = pl.pallas_call(
        body, name=name, out_shape=tuple(hbm(a) for a in list(srcs) + list(lands)),
        in_specs=[_HBM] * (2 * nt) + [_SEM, _SEM, _ANY], out_specs=tuple([_HBM] * (2 * nt)),
        input_output_aliases={j: j for j in range(2 * nt)},
        compiler_params=pltpu.CompilerParams(has_side_effects=_EFFECT),
    )(*srcs, *lands, send_sems, recv_sems, after)
    return outs[nt:]


def _adamw(parts, w, m, v, name, tr):
    n_layers, r, c_ = w.shape
    assert len(parts) == n_layers

    def body(*refs):
        p_refs = refs[:n_layers]
        w_ref, m_ref, v_ref, g_ref, d_ref, mo_ref, vo_ref = refs[n_layers:]

        def update(p_ref):
            g = p_ref[0].astype(F32)
            for j in range(1, N_DEV):
                g = g + p_ref[j].astype(F32)
            m2 = ADAM_B1 * m_ref[...] + (1.0 - ADAM_B1) * g
            v2 = ADAM_B2 * v_ref[...] + (1.0 - ADAM_B2) * (g * g)
            m_hat = m2 / (1.0 - ADAM_B1 ** ADAM_STEP)
            v_hat = v2 / (1.0 - ADAM_B2 ** ADAM_STEP)
            g_ref[...] = g
            d_ref[...] = -ADAM_LR * (m_hat / (jnp.sqrt(v_hat) + ADAM_EPS) + ADAM_WD * w_ref[...])
            mo_ref[...] = m2
            vo_ref[...] = v2

        for layer in range(n_layers):
            pl.when(pl.program_id(0) == layer)(lambda layer=layer: update(p_refs[layer]))

    spec = pl.BlockSpec((None, tr, c_), lambda l, i: (l, i, 0))
    p_spec = pl.BlockSpec((N_DEV, tr, c_), lambda l, i: (0, i, 0))
    return pl.pallas_call(
        body, name=name, grid=(n_layers, r // tr), in_specs=[p_spec] * n_layers + [spec] * 3,
        out_specs=[spec] * 4, out_shape=[_sds((n_layers, r, c_))] * 4,
        compiler_params=_params(("parallel", "parallel")),
    )(*parts, w, m, v)


def _pack_rows(n_elems, align):
    rows = -(-n_elems // PACK_COLS)
    return -(-rows // align) * align


def _pack(arrs, rows, dtype=F32):
    flat = jnp.concatenate([a.reshape(-1) for a in arrs]).astype(dtype)
    return jnp.pad(flat, (0, rows * PACK_COLS - flat.shape[0])).reshape(rows, PACK_COLS)


def _unpack(pack, shapes):
    flat = pack.reshape(-1)
    out, off = [], 0
    for shp in shapes:
        size = int(np.prod(shp))
        out.append(flat[off:off + size].reshape(shp))
        off += size
    return out


def _tile_rows(rows, target, align=16):
    best = align
    for t in range(align, target + 1, align):
        if rows % t == 0:
            best = t
    return best


COMM_NAMES = ("w_in", "ssm_glu_w", "w_out", "ffn_w_up", "ffn_w_down", "ple_w_gate", "ple_w_proj")
COMM_TRANSPOSED = ("w_in", "ffn_w_up", "ple_w_proj")
COMM_EARLY = ("ple_w_proj", "ple_w_gate", "ffn_w_down", "ffn_w_up")
COMM_LATE = ("w_in", "ssm_glu_w", "w_out")
SMALL_TILE_ROWS = 64
CONV_NAME = "ffn_conv_w"


def _to_comm(name, a):
    return jnp.swapaxes(a, 1, 2) if name in COMM_TRANSPOSED else a


def kernel(x, p, rel_bias, norm_attn_g, w_in, sgu_ln_g, sgu_ln_b, sgu_w, sgu_b, ssm_a_re, ssm_a_im, ssm_log_dt, ssm_b_re, ssm_b_im, ssm_c_re, ssm_c_im, ssm_d, ssm_glu_w, ssm_glu_b, branch_norm_g, w_out, norm_ffn_g, ffn_w_up, ffn_conv_w, ffn_conv_b, ffn_w_down, norm_ple_g, ple_w_gate, ple_w_proj, final_norm_g, loss_target, m_rel_bias, m_norm_attn_g, m_w_in, m_sgu_ln_g, m_sgu_ln_b, m_sgu_w, m_sgu_b, m_ssm_a_re, m_ssm_a_im, m_ssm_log_dt, m_ssm_b_re, m_ssm_b_im, m_ssm_c_re, m_ssm_c_im, m_ssm_d, m_ssm_glu_w, m_ssm_glu_b, m_branch_norm_g, m_w_out, m_norm_ffn_g, m_ffn_w_up, m_ffn_conv_w, m_ffn_conv_b, m_ffn_w_down, m_norm_ple_g, m_ple_w_gate, m_ple_w_proj, m_final_norm_g, v_rel_bias, v_norm_attn_g, v_w_in, v_sgu_ln_g, v_sgu_ln_b, v_sgu_w, v_sgu_b, v_ssm_a_re, v_ssm_a_im, v_ssm_log_dt, v_ssm_b_re, v_ssm_b_im, v_ssm_c_re, v_ssm_c_im, v_ssm_d, v_ssm_glu_w, v_ssm_glu_b, v_branch_norm_g, v_w_out, v_norm_ffn_g, v_ffn_w_up, v_ffn_conv_w, v_ffn_conv_b, v_ffn_w_down, v_norm_ple_g, v_ple_w_gate, v_ple_w_proj, v_final_norm_g):
    given = dict(locals())
    w = {n: given[n] for n in WEIGHT_NAMES}
    m = {n: given["m_" + n] for n in WEIGHT_NAMES}
    v = {n: given["v_" + n] for n in WEIGHT_NAMES}
    depth = p.shape[0]
    dev = 4 * lax.axis_index("x") + 2 * lax.axis_index("y") + lax.axis_index("c")

    wc = {n: _to_comm(n, w[n]) for n in COMM_NAMES}
    wb = {n: wc[n].astype(BF16) for n in COMM_NAMES}
    conv_local = [w[CONV_NAME], m[CONV_NAME], v[CONV_NAME]]
    conv_rows = _pack_rows(sum(int(np.prod(t.shape)) for t in conv_local), 8)
    conv_g, = _all_gather([_pack(conv_local, conv_rows)], 0, "gather_conv_taps")
    conv_parts = zip(*[_unpack(conv_g[j], [t.shape for t in conv_local]) for j in range(N_DEV)])
    conv_w, conv_m, conv_v = [jnp.concatenate(parts, axis=2) for parts in conv_parts]
    small = {n: w[n] for n in SMALL_NAMES}

    def whole(names, blocks):
        return {n: t.reshape(-1, t.shape[-1]) for n, t in zip(names, blocks)}

    def own_slot(block):
        return lax.dynamic_update_slice_in_dim(jnp.zeros((N_DEV,) + block.shape, block.dtype), block[None], dev, 0)

    def start_gather(names, i, after):
        srcs, after = lax.optimization_barrier(([wb[n][i] for n in names], after))
        return _exchange_start(srcs, [own_slot(s) for s in srcs], True, "gather_weights_%d_start" % i), after

    def wait_gather(names, i, started, after):
        send_sems, recv_sems, srcs, lands, _ = started
        return whole(names, _exchange_wait(send_sems, recv_sems, srcs, lands, after, True,
                                           "gather_weights_%d_wait" % i))

    at_once = ("w_in", "ssm_glu_w")
    later = tuple(n for n in COMM_NAMES if n not in at_once)
    w_in_0 = _all_gather([wb[n][0] for n in at_once], 0, "gather_w_in_0")
    gathering = {}
    gathering[0], (w_in_0, _) = start_gather(later, 0, (w_in_0, conv_g))
    small["norm_attn_g"] = small["norm_attn_g"] + gathering[0][4][0, 0]

    def layer_weights(i, h):
        if i > 0:
            got = wait_gather(COMM_NAMES, i, gathering.pop(i), h)
            if i + 1 < depth:
                gathering[i + 1], ordered = start_gather(COMM_NAMES, i + 1, got["w_in"])
                got["w_in"] = ordered + gathering[i + 1][4][0, 0].astype(BF16)
            return dict(got, **{CONV_NAME: conv_w[i]})

        def rest(after):
            got = wait_gather(later, 0, gathering.pop(0), after)
            if depth > 1:
                gathering[1], ordered = start_gather(COMM_NAMES, 1, got["w_out"])
                got["w_out"] = ordered + gathering[1][4][0, 0].astype(BF16)
            return got

        return dict(whole(at_once, w_in_0), **{CONV_NAME: conv_w[0], "rest": rest})

    def as_slots(g, n):
        return g.reshape((N_DEV,) + wc[n].shape[1:])

    scattering = {}

    def layer_done(i, stage, g, small_now):
        if stage == "all" and i == 0:
            return small_now
        names = COMM_EARLY if stage == "ffn" else COMM_LATE
        srcs = [as_slots(g[n], n) for n in names]
        lands = [own_slot(lax.dynamic_index_in_dim(s, dev, 0, keepdims=False)) for s in srcs]
        started = _exchange_start(srcs, lands, False, "scatter_weight_grads_%d_%s_start" % (i, stage))
        scattering[i, stage] = (names, started)
        pin = "branch_norm_g" if stage == "ffn" else "norm_ple_g"
        return dict(small_now, **{pin: small_now[pin] + started[4][0, 0]})

    loss, dx, big_grads, grads = _local_step(x[0], p[:, 0], loss_target[0], layer_weights, small, layer_done)
    loss = lax.psum(loss, ("x", "y", "c"))

    recv = [{} for _ in range(depth)]
    for (i, stage), (names, (send_sems, recv_sems, srcs, lands, _)) in scattering.items():
        got = _exchange_wait(send_sems, recv_sems, srcs, lands, dx, False,
                             "scatter_weight_grads_%d_%s_wait" % (i, stage))
        recv[i].update(zip(names, got))
    srcs = [as_slots(big_grads[0][n], n) for n in COMM_LATE]
    lands = [own_slot(lax.dynamic_index_in_dim(s, dev, 0, keepdims=False)) for s in srcs]
    last = _exchange_start(srcs, lands, False, "scatter_weight_grads_0_all_start")
    out = {}

    def update(n, pin=None):
        weight = wc[n] if pin is None else wc[n] + pin
        res = _adamw([recv[i][n] for i in range(depth)], weight, _to_comm(n, m[n]), _to_comm(n, v[n]),
                     "adamw_" + n, _tile_rows(wc[n].shape[1], 256))
        out[n] = [_to_comm(n, r) for r in res]

    for j, n in enumerate(COMM_EARLY):
        update(n, last[4][0, 0] if j == 0 else None)
    got = _exchange_wait(last[0], last[1], last[2], last[3], out[COMM_EARLY[-1]][0], False,
                         "scatter_weight_grads_0_all_wait")
    recv[0].update(zip(COMM_LATE, got))

    rep_names = SMALL_NAMES + (CONV_NAME,)
    rep_w = dict({n: w[n] for n in SMALL_NAMES}, **{CONV_NAME: conv_w})
    rep_m = dict({n: m[n] for n in SMALL_NAMES}, **{CONV_NAME: conv_m})
    rep_v = dict({n: v[n] for n in SMALL_NAMES}, **{CONV_NAME: conv_v})
    rep_shapes = [rep_w[n].shape for n in rep_names]
    rep_rows = _pack_rows(sum(int(np.prod(s)) for s in rep_shapes), SMALL_TILE_ROWS)
    rep_parts, = _all_gather([_pack([grads[n] for n in rep_names], rep_rows)], 0, "gather_small_grads")
    for n in COMM_LATE:
        update(n)
    rep_out = _adamw([rep_parts], *[_pack([src[n] for n in rep_names], rep_rows)[None] for src in (rep_w, rep_m, rep_v)],
                     "adamw_replicated", SMALL_TILE_ROWS)
    for n, vals in zip(rep_names, zip(*[_unpack(r[0], rep_shapes) for r in rep_out])):
        out[n] = list(vals)
    shard = ffn_conv_w.shape[2]
    out[CONV_NAME] = [lax.dynamic_slice_in_dim(t, dev * shard, shard, axis=2) for t in out[CONV_NAME]]
    results = [[out[n][kind] for n in WEIGHT_NAMES] for kind in range(4)]
    return (loss, dx[None], *results[0], *results[1], *results[2], *results[3])
```

```python
import math

import numpy as np
import jax
import jax.numpy as jnp
from jax import lax
from jax.experimental import pallas as pl
from jax.experimental.pallas import tpu as pltpu

F32 = jnp.float32
BF16 = jnp.bfloat16

D_MODEL = 1024
HEAD_DIM = 64
N_HEADS = 8
ATTN_W = 512
SGU_W = 256
SGU_GROUPS = 4
SGU_CHUNK = 128
SSM_W = 256
SSM_GROUPS = 16
SSM_STATE = 64
SSM_NS = SSM_GROUPS * SSM_STATE
IN_W = 2304
D_FF = 2816
BRANCHES = ((128, 1), (512, 4), (2048, 16))
BLK = 128
N_BUCKETS = 32
REL_MAX = 2048
EPS = 1e-6
NEG_INF = -1e30
N_DEV = 8

ADAM_LR = 0.001
ADAM_B1 = 0.9
ADAM_B2 = 0.999
ADAM_EPS = 1e-08
ADAM_WD = 0.01
ADAM_STEP = 10

VMEM_LIMIT_BYTES = 56 * 1024 * 1024
GELU_C = math.sqrt(2.0 / math.pi)

SMALL_NAMES = ("rel_bias", "norm_attn_g", "sgu_ln_g", "sgu_ln_b", "sgu_w", "sgu_b", "ssm_a_re", "ssm_a_im",
               "ssm_log_dt", "ssm_b_re", "ssm_b_im", "ssm_c_re", "ssm_c_im", "ssm_d", "ssm_glu_b",
               "branch_norm_g", "norm_ffn_g", "ffn_conv_b", "norm_ple_g", "final_norm_g")
WEIGHT_NAMES = ("rel_bias", "norm_attn_g", "w_in", "sgu_ln_g", "sgu_ln_b", "sgu_w", "sgu_b", "ssm_a_re",
                "ssm_a_im", "ssm_log_dt", "ssm_b_re", "ssm_b_im", "ssm_c_re", "ssm_c_im", "ssm_d", "ssm_glu_w",
                "ssm_glu_b", "branch_norm_g", "w_out", "norm_ffn_g", "ffn_w_up", "ffn_conv_w", "ffn_conv_b",
                "ffn_w_down", "norm_ple_g", "ple_w_gate", "ple_w_proj", "final_norm_g")
PACK_COLS = 512


def _params(sem):
    return pltpu.CompilerParams(dimension_semantics=sem, vmem_limit_bytes=VMEM_LIMIT_BYTES)


def _pick(dim, target):
    if dim <= target:
        return dim
    best = None
    for t in range(128, target + 1, 128):
        if dim % t == 0:
            best = t
    return dim if best is None else best


def _gelu(x):
    return 0.5 * x * (1.0 + jnp.tanh(GELU_C * (x + 0.044715 * (x * x * x))))


def _gelu_grad(x):
    t = jnp.tanh(GELU_C * (x + 0.044715 * (x * x * x)))
    return 0.5 * (1.0 + t) + 0.5 * x * (1.0 - t * t) * (GELU_C * (1.0 + 3.0 * 0.044715 * (x * x)))


def _sigmoid(x):
    return 1.0 / (1.0 + jnp.exp(-x))


_DIMS = {"nn": (((1,), (0,)), ((), ())), "tn": (((0,), (0,)), ((), ())), "nt": (((1,), (1,)), ((), ()))}


MM_TILE = D_FF // 2


def _mm(a, b, mode, name, add=None, out_dtype=F32, norm_gain=None, norm_bwd=None, tm=MM_TILE, tn=MM_TILE,
        tk=MM_TILE):
    if mode == "nn":
        m, k = a.shape
        k2, n = b.shape
    elif mode == "tn":
        k, m = a.shape
        k2, n = b.shape
    else:
        m, k = a.shape
        n, k2 = b.shape
    assert k == k2, (name, a.shape, b.shape, mode)
    tm, tn, tk = _pick(m, tm), _pick(n, tn), _pick(k, tk)
    nk = k // tk
    dims = _DIMS[mode]
    has_add = add is not None
    has_norm = norm_gain is not None
    has_nbwd = norm_bwd is not None
    assert not (has_norm or has_nbwd) or tn == n

    def body(*refs):
        a_ref, b_ref = refs[:2]
        rest = list(refs[2:])
        add_ref = rest.pop(0) if has_add else None
        g_ref = rest.pop(0) if has_norm else None
        h_ref, hg_ref = (rest.pop(0), rest.pop(0)) if has_nbwd else (None, None)
        o_ref = rest.pop(0)
        n_ref = rest.pop(0) if has_norm else None
        dg_ref = rest.pop(0) if has_nbwd else None
        part = lax.dot_general(a_ref[...].astype(BF16), b_ref[...].astype(BF16), dims,
                               preferred_element_type=F32)
        if has_nbwd:
            @pl.when((pl.program_id(0) == 0) & (pl.program_id(2) == 0))
            def _():
                dg_ref[...] = jnp.zeros_like(dg_ref)

        def finish(r):
            if has_nbwd:
                x = h_ref[...]
                scale = lax.rsqrt(jnp.mean(x * x, axis=-1, keepdims=True) + EPS)
                xh = x * scale
                dg_ref[...] += jnp.sum(r * xh, axis=0, keepdims=True)
                dxh = r * hg_ref[...]
                r = scale * (dxh - xh * jnp.mean(dxh * xh, axis=-1, keepdims=True))
            if has_add:
                r = r + add_ref[...]
            o_ref[...] = r.astype(out_dtype)
            if has_norm:
                scale = lax.rsqrt(jnp.mean(r * r, axis=-1, keepdims=True) + EPS)
                n_ref[...] = (r * scale * g_ref[...]).astype(BF16)

        if nk == 1:
            finish(part)
            return
        acc_ref = refs[-1]
        kk = pl.program_id(2)

        @pl.when(kk == 0)
        def _():
            acc_ref[...] = part

        @pl.when((kk > 0) & (kk < nk - 1))
        def _():
            acc_ref[...] += part

        @pl.when(kk == nk - 1)
        def _():
            finish(acc_ref[...] + part)

    if mode == "tn":
        a_spec = pl.BlockSpec((tk, tm), lambda i, j, kk: (kk, i))
    else:
        a_spec = pl.BlockSpec((tm, tk), lambda i, j, kk: (i, kk))
    if mode == "nt":
        b_spec = pl.BlockSpec((tn, tk), lambda i, j, kk: (j, kk))
    else:
        b_spec = pl.BlockSpec((tk, tn), lambda i, j, kk: (kk, j))
    o_spec = pl.BlockSpec((tm, tn), lambda i, j, kk: (i, j))
    in_specs = [a_spec, b_spec] + ([o_spec] if has_add else [])
    args = (a, b) + ((add,) if has_add else ())
    out_specs, out_shape = o_spec, jax.ShapeDtypeStruct((m, n), out_dtype)
    if has_norm:
        in_specs.append(pl.BlockSpec((1, n), lambda i, j, kk: (0, 0)))
        args += (norm_gain.reshape(1, n),)
        out_specs, out_shape = [o_spec, o_spec], [out_shape, jax.ShapeDtypeStruct((m, n), BF16)]
    if has_nbwd:
        row_spec = pl.BlockSpec((1, n), lambda i, j, kk: (0, 0))
        in_specs += [o_spec, row_spec]
        args += (norm_bwd[0], norm_bwd[1].reshape(1, n))
        out_specs, out_shape = [o_spec, row_spec], [out_shape, jax.ShapeDtypeStruct((1, n), F32)]
    sem = ("arbitrary",) * 3 if has_nbwd else ("parallel", "parallel", "arbitrary")
    return pl.pallas_call(
        body, name=name, grid=(m // tm, n // tn, nk),
        in_specs=in_specs, out_specs=out_specs, out_shape=out_shape,
        scratch_shapes=[pltpu.VMEM((tm, tn), F32)] if nk > 1 else [], compiler_params=_params(sem),
    )(*args)


def _rb(tm, w, cb=0):
    return pl.BlockSpec((tm, w), lambda i: (i, cb))


def _fb(shape):
    nd = len(shape)
    return pl.BlockSpec(shape, lambda i: (0,) * nd)


def _rowcall(body, name, n_rows, tm, in_specs, args, out_specs, out_shapes):
    return pl.pallas_call(
        body, name=name, grid=(n_rows // tm,), in_specs=in_specs, out_specs=out_specs, out_shape=out_shapes,
        compiler_params=_params(("arbitrary",)),
    )(*args)


def _sds(shape, dtype=F32):
    return jax.ShapeDtypeStruct(shape, dtype)


def _rms_fwd(h, g, name, tm=1024):
    s, d = h.shape

    def body(h_ref, g_ref, o_ref):
        x = h_ref[...]
        r = lax.rsqrt(jnp.mean(x * x, axis=-1, keepdims=True) + EPS)
        o_ref[...] = (x * r * g_ref[...]).astype(BF16)

    return _rowcall(body, name, s, tm, [_rb(tm, d), _fb((1, d))], (h, g.reshape(1, d)), _rb(tm, d),
                    _sds((s, d), BF16))


def _loss_head(h, target, g, name, tm=1024):
    s, d = h.shape

    def body(h_ref, t_ref, g_ref, dh_ref, loss_ref, dg_ref):
        @pl.when(pl.program_id(0) == 0)
        def _():
            dg_ref[...] = jnp.zeros_like(dg_ref)
            loss_ref[...] = jnp.zeros_like(loss_ref)

        x = h_ref[...]
        r = lax.rsqrt(jnp.mean(x * x, axis=-1, keepdims=True) + EPS)
        xh = x * r
        gg = g_ref[...]
        err = xh * gg - t_ref[...]
        loss_ref[...] += jnp.sum(err * err) * (0.5 / d)
        dy = err * (1.0 / d)
        dg_ref[...] += jnp.sum(dy * xh, axis=0, keepdims=True)
        dxh = dy * gg
        dh_ref[...] = r * (dxh - xh * jnp.mean(dxh * xh, axis=-1, keepdims=True))

    dh, loss, dg = _rowcall(body, name, s, tm, [_rb(tm, d), _rb(tm, d), _fb((1, d))], (h, target, g.reshape(1, d)),
                            [_rb(tm, d), _fb((1, 128)), _fb((1, d))], [_sds((s, d)), _sds((1, 128)), _sds((1, d))])
    return dh, loss[0, 0], dg.reshape(d)


_MIX_PARTS = ((0, 512), (512, 768), (768, 1024))


def _mix_fwd(ya, ysg, yss, g, name, tm=1024):
    s = ya.shape[0]

    def body(a_ref, b_ref, c_ref, g_ref, o_ref):
        for ref, (lo, hi) in zip((a_ref, b_ref, c_ref), _MIX_PARTS):
            y = ref[...]
            r = lax.rsqrt(jnp.mean(y * y, axis=-1, keepdims=True) + EPS)
            o_ref[:, lo:hi] = (y * r * g_ref[:, lo:hi]).astype(BF16)

    return _rowcall(body, name, s, tm, [_rb(tm, 512), _rb(tm, 256), _rb(tm, 256), _fb((1, 1024))],
                    (ya, ysg, yss, g.reshape(1, 1024)), _rb(tm, 1024), _sds((s, 1024), BF16))


def _mix_bwd(dh, w_out, ya, ysg, yss, g, name, tm=1024):
    s = ya.shape[0]

    def body(dh_ref, w_ref, a_ref, b_ref, c_ref, g_ref, da_ref, db_ref, dc_ref, dg_ref):
        @pl.when(pl.program_id(0) == 0)
        def _():
            dg_ref[...] = jnp.zeros_like(dg_ref)

        dmix = lax.dot_general(dh_ref[...].astype(BF16), w_ref[...], _DIMS["nt"], preferred_element_type=F32)
        for ref, dref, (lo, hi) in zip((a_ref, b_ref, c_ref), (da_ref, db_ref, dc_ref), _MIX_PARTS):
            y = ref[...]
            r = lax.rsqrt(jnp.mean(y * y, axis=-1, keepdims=True) + EPS)
            xh = y * r
            dm = dmix[:, lo:hi]
            dg_ref[:, lo:hi] += jnp.sum(dm * xh, axis=0, keepdims=True)
            dxh = dm * g_ref[:, lo:hi]
            dref[...] = r * (dxh - xh * jnp.mean(dxh * xh, axis=-1, keepdims=True))

    da, db, dc, dg = _rowcall(
        body, name, s, tm,
        [_rb(tm, 1024), _fb((1024, 1024)), _rb(tm, 512), _rb(tm, 256), _rb(tm, 256), _fb((1, 1024))],
        (dh, w_out, ya, ysg, yss, g.reshape(1, 1024)),
        [_rb(tm, 512), _rb(tm, 256), _rb(tm, 256), _fb((1, 1024))],
        [_sds((s, 512)), _sds((s, 256)), _sds((s, 256)), _sds((1, 1024))])
    return da, db, dc, dg.reshape(1024)


def _ssm_post_fwd(yc, z, d, gw, gb, name, tm=1024):
    s = yc.shape[0]

    def body(yc_ref, u_ref, d_ref, gw_ref, gb_ref, o_ref):
        y1 = yc_ref[...] + d_ref[...] * u_ref[...]
        y2 = _gelu(y1)
        gl = jnp.dot(y2.astype(BF16), gw_ref[...], preferred_element_type=F32) + gb_ref[...]
        o_ref[...] = y2 * _sigmoid(gl)

    return _rowcall(body, name, s, tm, [_rb(tm, 256), _rb(tm, 256, 8), _fb((1, 256)), _fb((256, 256)), _fb((1, 256))],
                    (yc, z, d.reshape(1, 256), gw, gb.reshape(1, 256)), _rb(tm, 256), _sds((s, 256)))


def _ssm_post_bwd(dy, yc, z, d, gw, gb, name, tm=1024):
    s = yc.shape[0]

    def body(dy_ref, yc_ref, u_ref, d_ref, gw_ref, gb_ref, dy1_ref, dgl_ref, y2_ref, dud_ref, dd_ref, dgb_ref):
        @pl.when(pl.program_id(0) == 0)
        def _():
            dd_ref[...] = jnp.zeros_like(dd_ref)
            dgb_ref[...] = jnp.zeros_like(dgb_ref)

        u = u_ref[...]
        dd = d_ref[...]
        y1 = yc_ref[...] + dd * u
        y2 = _gelu(y1)
        gw_v = gw_ref[...]
        gl = jnp.dot(y2.astype(BF16), gw_v, preferred_element_type=F32) + gb_ref[...]
        sg = _sigmoid(gl)
        dyv = dy_ref[...]
        dgl = dyv * y2 * sg * (1.0 - sg)
        dy2 = dyv * sg + lax.dot_general(dgl.astype(BF16), gw_v, _DIMS["nt"], preferred_element_type=F32)
        dy1 = dy2 * _gelu_grad(y1)
        dy1_ref[...] = dy1.astype(BF16)
        dgl_ref[...] = dgl.astype(BF16)
        y2_ref[...] = y2.astype(BF16)
        dud_ref[...] = dy1 * dd
        dd_ref[...] += jnp.sum(dy1 * u, axis=0, keepdims=True)
        dgb_ref[...] += jnp.sum(dgl, axis=0, keepdims=True)

    outs = _rowcall(
        body, name, s, tm,
        [_rb(tm, 256), _rb(tm, 256), _rb(tm, 256, 8), _fb((1, 256)), _fb((256, 256)), _fb((1, 256))],
        (dy, yc, z, d.reshape(1, 256), gw, gb.reshape(1, 256)),
        [_rb(tm, 256)] * 4 + [_fb((1, 256))] * 2,
        [_sds((s, 256), BF16)] * 3 + [_sds((s, 256))] + [_sds((1, 256))] * 2)
    dy1, dgl, y2, dud, dd, dgb = outs
    return dy1, dgl, y2, dud, dd.reshape(256), dgb.reshape(256)


SCAN_T = 512
N_SCAN_TABLES = 6


def _scan_tables(pr, pi, reverse):
    ns = pr.shape[0]
    sign = -1.0 if reverse else 1.0
    power = [(jnp.ones((ns,), F32), jnp.zeros((ns,), F32))] + [(pr[:, k], sign * pi[:, k]) for k in range(8)]
    zero = (jnp.zeros((ns,), F32), jnp.zeros((ns,), F32))

    def table(exponents):
        rows = [zero if e is None else power[e] for e in exponents]
        return jnp.stack([jnp.concatenate(row) for row in rows])

    tabs = []
    for k in (1, 2, 4):
        has_partner = [(s < 8 - k) if reverse else (s >= k) for s in range(8)]
        tabs.append(table([k if ok else None for ok in has_partner]))
    tabs.append(table([s if reverse else 7 - s for s in range(8)]))
    tabs.append(table([8 - s if reverse else s + 1 for s in range(8)]))
    tabs.append(table([8] * 8))
    return jnp.stack(tabs)


def _cmul(ar, ai, br, bi):
    return ar * br - ai * bi, ar * bi + ai * br


def _scan_group(ur, ui, cr, ci, tr_ref, ti_ref, reverse):
    xr, xi = ur, ui
    for n, k in enumerate((1, 2, 4)):
        shift = 8 - k if reverse else k
        pr, pi = _cmul(tr_ref[n], ti_ref[n], pltpu.roll(xr, shift, axis=0), pltpu.roll(xi, shift, axis=0))
        xr, xi = xr + pr, xi + pi
    sr, si = _cmul(tr_ref[3], ti_ref[3], ur, ui)
    for k in (1, 2, 4):
        sr, si = sr + pltpu.roll(sr, k, axis=0), si + pltpu.roll(si, k, axis=0)
    pr, pi = _cmul(tr_ref[4], ti_ref[4], cr, ci)
    nr, ni = _cmul(tr_ref[5], ti_ref[5], cr, ci)
    return xr + pr, xi + pi, nr + sr, ni + si


def _table_halves(t_ref):
    return t_ref.at[:, :, pl.ds(0, SSM_NS)], t_ref.at[:, :, pl.ds(SSM_NS, SSM_NS)]


_U_BLOCK = (IN_W - SSM_W) // SSM_W


def _ssm_fwd(z, bdt, cd, tabs, name):
    s = z.shape[0]
    ns = SSM_NS
    n_t = s // SCAN_T

    def body(u_ref, b_ref, c_ref, t_ref, xr_ref, xi_ref, y_ref, cr_ref, ci_ref, ur_ref, ui_ref):
        @pl.when(pl.program_id(0) == 0)
        def _():
            cr_ref[...] = jnp.zeros_like(cr_ref)
            ci_ref[...] = jnp.zeros_like(ci_ref)

        bu = lax.dot_general(u_ref[...].astype(BF16), b_ref[...], _DIMS["nt"], preferred_element_type=F32)
        ur_ref[...] = bu[:, :ns]
        ui_ref[...] = bu[:, ns:]
        tr_ref, ti_ref = _table_halves(t_ref)

        def group(g, carry):
            rows = pl.ds(pl.multiple_of(g * 8, 8), 8)
            xr, xi, cr, ci = _scan_group(ur_ref[rows, :], ui_ref[rows, :], *carry, tr_ref, ti_ref, False)
            xr_ref[rows, :] = xr
            xi_ref[rows, :] = xi
            return cr, ci

        cr, ci = lax.fori_loop(0, SCAN_T // 8, group, (cr_ref[...], ci_ref[...]), unroll=2)
        cr_ref[...] = cr
        ci_ref[...] = ci
        y_ref[...] = (jnp.dot(xr_ref[...].astype(BF16), c_ref[0:ns, :], preferred_element_type=F32)
                      + jnp.dot(xi_ref[...].astype(BF16), c_ref[ns:, :], preferred_element_type=F32))

    x_spec = pl.BlockSpec((SCAN_T, ns), lambda t: (t, 0))
    return pl.pallas_call(
        body, name=name, grid=(n_t,),
        in_specs=[pl.BlockSpec((SCAN_T, SSM_W), lambda t: (t, _U_BLOCK)), _fb((2 * ns, SSM_W)),
                  _fb((2 * ns, SSM_W)), _fb((N_SCAN_TABLES, 8, 2 * ns))],
        out_specs=[x_spec, x_spec, _rb(SCAN_T, SSM_W)],
        out_shape=[_sds((s, ns)), _sds((s, ns)), _sds((s, SSM_W))],
        scratch_shapes=[pltpu.VMEM((8, ns), F32)] * 2 + [pltpu.VMEM((SCAN_T, ns), F32)] * 2,
        compiler_params=_params(("arbitrary",)),
    )(z, bdt, cd, tabs)


def _ssm_bwd(dy1, dud, z, xr, xi, bdt, cd, tabs, name):
    s = z.shape[0]
    ns = SSM_NS
    n_t = s // SCAN_T
    n_groups = SCAN_T // 8

    def body(dy_ref, dud_ref, u_ref, xr_ref, xi_ref, pxr_ref, pxi_ref, b_ref, c_ref, t_ref,
             du_ref, dbd_ref, dcd_ref, dar_ref, dai_ref,
             cr_ref, ci_ref, ar_ref, ai_ref, sxr_ref, sxi_ref, gr_ref, gi_ref, lr_ref, li_ref, bacc_ref, cacc_ref):
        t = pl.program_id(0)

        @pl.when(t == 0)
        def _():
            for ref in (cr_ref, ci_ref, ar_ref, ai_ref, bacc_ref, cacc_ref):
                ref[...] = jnp.zeros_like(ref)

        dyb = dy_ref[...]
        g = lax.dot_general(dyb, c_ref[...], _DIMS["nt"], preferred_element_type=F32)
        gr_ref[...] = g[:, :ns]
        gi_ref[...] = g[:, ns:]
        has_before = (t < n_t - 1).astype(F32)
        sxr_ref[0:8, :] = pxr_ref[...] * has_before
        sxi_ref[0:8, :] = pxi_ref[...] * has_before
        sxr_ref[8:, :] = xr_ref[...]
        sxi_ref[8:, :] = xi_ref[...]
        first_row = lax.broadcasted_iota(jnp.int32, (8, ns), 0) == 0
        tr_ref, ti_ref = _table_halves(t_ref)

        def group(k, carry):
            cr, ci, ar, ai = carry
            g8 = pl.multiple_of((n_groups - 1 - k) * 8, 8)
            rows = pl.ds(g8, 8)
            lr, li, cr, ci = _scan_group(gr_ref[rows, :], gi_ref[rows, :], cr, ci, tr_ref, ti_ref, True)
            lr_ref[rows, :] = lr
            li_ref[rows, :] = li
            here, before = pl.ds(g8 + 8, 8), rows
            pr = jnp.where(first_row, pltpu.roll(sxr_ref[before, :], 1, axis=0), pltpu.roll(sxr_ref[here, :], 1, axis=0))
            pi = jnp.where(first_row, pltpu.roll(sxi_ref[before, :], 1, axis=0), pltpu.roll(sxi_ref[here, :], 1, axis=0))
            return cr, ci, ar + lr * pr + li * pi, ai + li * pr - lr * pi

        cr, ci, ar, ai = lax.fori_loop(0, n_groups, group,
                                       (cr_ref[...], ci_ref[...], ar_ref[...], ai_ref[...]), unroll=2)
        cr_ref[...] = cr
        ci_ref[...] = ci
        ar_ref[...] = ar
        ai_ref[...] = ai
        lrb = lr_ref[...].astype(BF16)
        lib = li_ref[...].astype(BF16)
        ub = u_ref[...].astype(BF16)
        du_ref[...] = (dud_ref[...] + jnp.dot(lrb, b_ref[0:ns, :], preferred_element_type=F32)
                       + jnp.dot(lib, b_ref[ns:, :], preferred_element_type=F32))
        bacc_ref[0:ns, :] += lax.dot_general(lrb, ub, _DIMS["tn"], preferred_element_type=F32)
        bacc_ref[ns:, :] += lax.dot_general(lib, ub, _DIMS["tn"], preferred_element_type=F32)
        cacc_ref[0:ns, :] += lax.dot_general(xr_ref[...].astype(BF16), dyb, _DIMS["tn"], preferred_element_type=F32)
        cacc_ref[ns:, :] += lax.dot_general(xi_ref[...].astype(BF16), dyb, _DIMS["tn"], preferred_element_type=F32)

        @pl.when(t == n_t - 1)
        def _():
            for k in (1, 2, 4):
                ar_ref[...] += pltpu.roll(ar_ref[...], k, axis=0)
                ai_ref[...] += pltpu.roll(ai_ref[...], k, axis=0)
            dar_ref[...] = ar_ref[...]
            dai_ref[...] = ai_ref[...]
            dbd_ref[...] = bacc_ref[...]
            dcd_ref[...] = cacc_ref[...]

    rev = lambda t: n_t - 1 - t
    row_spec = pl.BlockSpec((SCAN_T, SSM_W), lambda t: (rev(t), 0))
    x_spec = pl.BlockSpec((SCAN_T, ns), lambda t: (rev(t), 0))
    before_spec = pl.BlockSpec((8, ns), lambda t: (jnp.maximum(rev(t) * (SCAN_T // 8) - 1, 0), 0))
    du, dbd, dcd, dar, dai = pl.pallas_call(
        body, name=name, grid=(n_t,),
        in_specs=[row_spec, row_spec, pl.BlockSpec((SCAN_T, SSM_W), lambda t: (rev(t), _U_BLOCK)),
                  x_spec, x_spec, before_spec, before_spec,
                  _fb((2 * ns, SSM_W)), _fb((2 * ns, SSM_W)), _fb((N_SCAN_TABLES, 8, 2 * ns))],
        out_specs=[row_spec, _fb((2 * ns, SSM_W)), _fb((2 * ns, SSM_W)), _fb((8, ns)), _fb((8, ns))],
        out_shape=[_sds((s, SSM_W)), _sds((2 * ns, SSM_W)), _sds((2 * ns, SSM_W)), _sds((8, ns)), _sds((8, ns))],
        scratch_shapes=([pltpu.VMEM((8, ns), F32)] * 4 + [pltpu.VMEM((SCAN_T + 8, ns), F32)] * 2
                        + [pltpu.VMEM((SCAN_T, ns), F32)] * 4 + [pltpu.VMEM((2 * ns, SSM_W), F32)] * 2),
        compiler_params=_params(("arbitrary",)),
    )(dy1, dud, z, xr, xi, xr, xi, bdt, cd, tabs)
    return du, dbd, dcd, dar[0], dai[0]


def _group_ids():
    return lax.broadcasted_iota(jnp.int32, (1, SGU_W), 1) // 64


def _group_mean(val, gid):
    out = jnp.zeros_like(val)
    for g in range(SGU_GROUPS):
        mg = gid == g
        out = jnp.where(mg, jnp.sum(jnp.where(mg, val, 0.0), axis=1, keepdims=True) * (1.0 / 64), out)
    return out


def _causal_w(w_ref, g):
    t = lax.broadcasted_iota(jnp.int32, (SGU_CHUNK, SGU_CHUNK), 0)
    s = lax.broadcasted_iota(jnp.int32, (SGU_CHUNK, SGU_CHUNK), 1)
    return jnp.where(t >= s, w_ref[g], 0.0).astype(BF16)


def _sgu_core(x, lng, lnb, w_ref, bexp, gid):
    zz = _gelu(x)
    u = zz[:, :SGU_W]
    v = zz[:, SGU_W:]
    vc = v - _group_mean(v, gid)
    rstd = lax.rsqrt(_group_mean(vc * vc, gid) + EPS)
    vhat = vc * rstd
    vn = vhat * lng + lnb
    vnb = vn.astype(BF16)
    mixed = bexp
    for g in range(SGU_GROUPS):
        mm = jnp.dot(_causal_w(w_ref, g), vnb, preferred_element_type=F32)
        mixed = jnp.where(gid == g, mm + bexp, mixed)
    return u, rstd, vhat, vnb, mixed


def _sgu_fwd(z, lng, lnb, w, bexp, name, tm=1024):
    s = z.shape[0]

    def body(z_ref, lng_ref, lnb_ref, w_ref, b_ref, o_ref):
        gid = _group_ids()
        for j in range(tm // SGU_CHUNK):
            rows = pl.ds(j * SGU_CHUNK, SGU_CHUNK)
            u, _, _, _, mixed = _sgu_core(z_ref[rows, :], lng_ref[...], lnb_ref[...], w_ref, b_ref[...], gid)
            o_ref[rows, :] = u * mixed

    return _rowcall(body, name, s, tm,
                    [_rb(tm, 512, 3), _fb((1, 256)), _fb((1, 256)), _fb((4, 128, 128)), _fb((128, 256))],
                    (z, lng.reshape(1, 256), lnb.reshape(1, 256), w, bexp), _rb(tm, 256), _sds((s, 256)))


def _sgu_bwd(z, dy, lng, lnb, w, bexp, name, tm=1024):
    s = z.shape[0]

    def body(z_ref, dy_ref, lng_ref, lnb_ref, w_ref, b_ref, dz_ref, dw_ref, db_ref, dlng_ref, dlnb_ref):
        @pl.when(pl.program_id(0) == 0)
        def _():
            dw_ref[...] = jnp.zeros_like(dw_ref)
            db_ref[...] = jnp.zeros_like(db_ref)
            dlng_ref[...] = jnp.zeros_like(dlng_ref)
            dlnb_ref[...] = jnp.zeros_like(dlnb_ref)

        gid = _group_ids()
        t = lax.broadcasted_iota(jnp.int32, (SGU_CHUNK, SGU_CHUNK), 0)
        sidx = lax.broadcasted_iota(jnp.int32, (SGU_CHUNK, SGU_CHUNK), 1)
        lng_v = lng_ref[...]
        for j in range(tm // SGU_CHUNK):
            rows = pl.ds(j * SGU_CHUNK, SGU_CHUNK)
            x = z_ref[rows, :]
            u, rstd, vhat, vnb, mixed = _sgu_core(x, lng_v, lnb_ref[...], w_ref, b_ref[...], gid)
            dyv = dy_ref[rows, :]
            dmixed = dyv * u
            du = dyv * mixed
            db_ref[...] += dmixed
            dvn = jnp.zeros_like(dmixed)
            for g in range(SGU_GROUPS):
                dmg = jnp.where(gid == g, dmixed, 0.0).astype(BF16)
                dvn = dvn + lax.dot_general(_causal_w(w_ref, g), dmg, _DIMS["tn"], preferred_element_type=F32)
                dwg = lax.dot_general(dmg, vnb, _DIMS["nt"], preferred_element_type=F32)
                dw_ref[g] += jnp.where(t >= sidx, dwg, 0.0)
            dlnb_ref[...] += jnp.sum(dvn, axis=0, keepdims=True)
            dlng_ref[...] += jnp.sum(dvn * vhat, axis=0, keepdims=True)
            dvh = dvn * lng_v
            dv = rstd * (dvh - _group_mean(dvh, gid) - vhat * _group_mean(dvh * vhat, gid))
            gg = _gelu_grad(x)
            dz_ref[rows, 0:SGU_W] = du * gg[:, :SGU_W]
            dz_ref[rows, SGU_W:2 * SGU_W] = dv * gg[:, SGU_W:]

    dz, dw, db, dlng, dlnb = _rowcall(
        body, name, s, tm,
        [_rb(tm, 512, 3), _rb(tm, 256), _fb((1, 256)), _fb((1, 256)), _fb((4, 128, 128)), _fb((128, 256))],
        (z, dy, lng.reshape(1, 256), lnb.reshape(1, 256), w, bexp),
        [_rb(tm, 512), _fb((4, 128, 128)), _fb((128, 256)), _fb((1, 256)), _fb((1, 256))],
        [_sds((s, 512)), _sds((4, 128, 128)), _sds((128, 256)), _sds((1, 256)), _sds((1, 256))])
    return dz, dw, db, dlng.reshape(256), dlnb.reshape(256)


CONV_TC = 1408
N_CT = D_FF // CONV_TC


def _row_of(block8, j):
    r = lax.broadcasted_iota(jnp.int32, block8.shape, 0)
    return jnp.sum(jnp.where(r == j, block8, 0.0), axis=0, keepdims=True)


EDGE = 16


def _conv_fwd(hu, cw, cb, name, tm=512):
    s = hu.shape[0]

    def body(xv_ref, xg_ref, tv_ref, tg_ref, wv_ref, wg_ref, bv_ref, bg_ref, hv_ref, hg_ref, act_ref):
        has_prev = (pl.program_id(1) > 0).astype(F32)
        row = lax.broadcasted_iota(jnp.int32, (EDGE, CONV_TC), 0)

        def conv(x_ref, t_ref, w_ref, b_ref):
            x = x_ref[...].astype(F32)
            w0, w1, w2, bb = w_ref[0:1, :], w_ref[1:2, :], w_ref[2:3, :], b_ref[...]
            whole = w0 * pltpu.roll(x, 2, axis=0) + w1 * pltpu.roll(x, 1, axis=0) + w2 * x + bb
            tail = t_ref[...].astype(F32)
            r7 = _row_of(tail, EDGE - 1) * has_prev
            r6 = _row_of(tail, EDGE - 2) * has_prev
            xe = x_ref[0:EDGE, :].astype(F32)
            x1 = jnp.where(row == 0, r7, pltpu.roll(xe, 1, axis=0))
            x2 = jnp.where(row == 0, r6, jnp.where(row == 1, r7, pltpu.roll(xe, 2, axis=0)))
            return whole, w0 * x2 + w1 * x1 + w2 * xe + bb

        hv, hv_edge = conv(xv_ref, tv_ref, wv_ref, bv_ref)
        hg, hg_edge = conv(xg_ref, tg_ref, wg_ref, bg_ref)
        hv_ref[...] = hv.astype(BF16)
        hg_ref[...] = hg.astype(BF16)
        act_ref[...] = (_gelu(hg) * hv).astype(BF16)
        hv_ref[0:EDGE, :] = hv_edge.astype(BF16)
        hg_ref[0:EDGE, :] = hg_edge.astype(BF16)
        act_ref[0:EDGE, :] = (_gelu(hg_edge) * hv_edge).astype(BF16)

    def xs(off):
        return pl.BlockSpec((tm, CONV_TC), lambda j, i: (i, j + off))

    def ts(off):
        return pl.BlockSpec((EDGE, CONV_TC), lambda j, i: (jnp.maximum(i * (tm // EDGE) - 1, 0), j + off))

    def ws(rows, off):
        return pl.BlockSpec((rows, CONV_TC), lambda j, i: (0, j + off))

    o_spec = pl.BlockSpec((tm, CONV_TC), lambda j, i: (i, j))
    return pl.pallas_call(
        body, name=name, grid=(N_CT, s // tm),
        in_specs=[xs(0), xs(N_CT), ts(0), ts(N_CT), ws(3, 0), ws(3, N_CT), ws(1, 0), ws(1, N_CT)],
        out_specs=[o_spec] * 3, out_shape=[_sds((s, D_FF), BF16)] * 3,
        compiler_params=_params(("parallel", "arbitrary")),
    )(hu, hu, hu, hu, cw, cw, cb.reshape(1, 2 * D_FF), cb.reshape(1, 2 * D_FF))


HALO = EDGE


def _conv_bwd(dact, hv, hg, hu, cw, name, tm=512):
    s = dact.shape[0]

    def body(da_ref, dan_ref, hv_ref, hvn_ref, hg_ref, hgn_ref, x_ref, t_ref, w_ref, dx_ref, dw_ref, db_ref, d_scr):
        i = pl.program_id(1)
        is_value = pl.program_id(0) < N_CT

        @pl.when(i == 0)
        def _():
            dw_ref[...] = jnp.zeros_like(dw_ref)
            db_ref[...] = jnp.zeros_like(db_ref)

        for rows, (a_ref, v_ref, g_ref) in ((pl.ds(0, tm), (da_ref, hv_ref, hg_ref)),
                                            (pl.ds(tm, HALO), (dan_ref, hvn_ref, hgn_ref))):
            @pl.when(is_value)
            def _():
                d_scr[rows, :] = a_ref[...].astype(F32) * _gelu(g_ref[...].astype(F32))

            @pl.when(jnp.logical_not(is_value))
            def _():
                d_scr[rows, :] = (a_ref[...].astype(F32) * v_ref[...].astype(F32)
                                  * _gelu_grad(g_ref[...].astype(F32)))

        has_prev = (i > 0).astype(F32)
        has_next = (i < s // tm - 1).astype(F32)
        w0, w1, w2 = w_ref[0:1, :], w_ref[1:2, :], w_ref[2:3, :]
        d = d_scr[0:tm, :]
        dx_ref[...] = (w2 * d + w1 * pltpu.roll(d, tm - 1, axis=0) + w0 * pltpu.roll(d, tm - 2, axis=0)).astype(BF16)
        row = lax.broadcasted_iota(jnp.int32, (EDGE, CONV_TC), 0)
        nxt = d_scr[tm:tm + HALO, :]
        n0 = _row_of(nxt, 0) * has_next
        n1 = _row_of(nxt, 1) * has_next
        de = d_scr[tm - EDGE:tm, :]
        d1 = jnp.where(row == EDGE - 1, n0, pltpu.roll(de, EDGE - 1, axis=0))
        d2 = jnp.where(row == EDGE - 2, n0, jnp.where(row == EDGE - 1, n1, pltpu.roll(de, EDGE - 2, axis=0)))
        dx_ref[tm - EDGE:tm, :] = (w2 * de + w1 * d1 + w0 * d2).astype(BF16)
        x = x_ref[...].astype(F32)
        tail = t_ref[...].astype(F32)
        r7 = _row_of(tail, EDGE - 1) * has_prev
        r6 = _row_of(tail, EDGE - 2) * has_prev
        last = x_ref[tm - EDGE:tm, :].astype(F32)
        l7, l6 = _row_of(last, EDGE - 1), _row_of(last, EDGE - 2)
        head = d_scr[0:8, :]
        d0, d1h = _row_of(head, 0), _row_of(head, 1)
        dw_ref[0:1, :] += (jnp.sum(d * pltpu.roll(x, 2, axis=0), axis=0, keepdims=True)
                           + d0 * (r6 - l6) + d1h * (r7 - l7))
        dw_ref[1:2, :] += jnp.sum(d * pltpu.roll(x, 1, axis=0), axis=0, keepdims=True) + d0 * (r7 - l7)
        dw_ref[2:3, :] += jnp.sum(d * x, axis=0, keepdims=True)
        db_ref[...] += jnp.sum(d, axis=0, keepdims=True)

    a_spec = pl.BlockSpec((tm, CONV_TC), lambda j, i: (i, j % N_CT))
    an_spec = pl.BlockSpec((HALO, CONV_TC),
                           lambda j, i: (jnp.minimum((i + 1) * (tm // HALO), s // HALO - 1), j % N_CT))
    x_spec = pl.BlockSpec((tm, CONV_TC), lambda j, i: (i, j))
    t_spec = pl.BlockSpec((EDGE, CONV_TC), lambda j, i: (jnp.maximum(i * (tm // EDGE) - 1, 0), j))
    w_spec = pl.BlockSpec((3, CONV_TC), lambda j, i: (0, j))
    db_spec = pl.BlockSpec((1, CONV_TC), lambda j, i: (0, j))
    return pl.pallas_call(
        body, name=name, grid=(2 * N_CT, s // tm),
        in_specs=[a_spec, an_spec, a_spec, an_spec, a_spec, an_spec, x_spec, t_spec, w_spec],
        out_specs=[x_spec, w_spec, db_spec],
        out_shape=[_sds((s, 2 * D_FF), BF16), _sds((3, 2 * D_FF)), _sds((1, 2 * D_FF))],
        scratch_shapes=[pltpu.VMEM((tm + HALO, CONV_TC), F32)],
        compiler_params=_params(("parallel", "arbitrary")),
    )(dact, dact, hv, hv, hg, hg, hu, hu, cw)


def _ple_fwd(h, gp, pp, next_gain, name, tm=1024):
    s, d = h.shape
    with_norm = next_gain is not None

    def body(*refs):
        h_ref, g_ref, p_ref = refs[:3]
        out = h_ref[...] + _sigmoid(g_ref[...].astype(F32)) * p_ref[...].astype(F32)
        if with_norm:
            n_ref, o_ref, a_ref = refs[3:]
            scale = lax.rsqrt(jnp.mean(out * out, axis=-1, keepdims=True) + EPS)
            a_ref[...] = (out * scale * n_ref[...]).astype(BF16)
        else:
            o_ref, = refs[3:]
        o_ref[...] = out

    if not with_norm:
        return _rowcall(body, name, s, tm, [_rb(tm, d)] * 3, (h, gp, pp), _rb(tm, d), _sds((s, d))), None
    return _rowcall(body, name, s, tm, [_rb(tm, d)] * 3 + [_fb((1, d))], (h, gp, pp, next_gain.reshape(1, d)),
                    [_rb(tm, d)] * 2, [_sds((s, d)), _sds((s, d), BF16)])


def _ple_bwd(dh, gp, pp, name, tm=1024):
    s, d = dh.shape

    def body(d_ref, g_ref, p_ref, dp_ref, dg_ref):
        sg = _sigmoid(g_ref[...].astype(F32))
        dv = d_ref[...]
        dp_ref[...] = (dv * sg).astype(BF16)
        dg_ref[...] = (dv * p_ref[...].astype(F32) * sg * (1.0 - sg)).astype(BF16)

    return _rowcall(body, name, s, tm, [_rb(tm, d)] * 3, (dh, gp, pp), [_rb(tm, d)] * 2,
                    [_sds((s, d), BF16)] * 2)


SCALE = HEAD_DIM ** -0.5
ATT_ROWS = 2048


def _att_geom(s, dil):
    w = min(ATT_ROWS, s)
    p = BLK * dil
    assert w % p == 0 and s % w == 0
    return w, p, w // p


def _rows(start, dil):
    return pl.ds(start, BLK, stride=dil) if dil > 1 else pl.ds(start, BLK)


def _head_masks():
    lane = lax.broadcasted_iota(jnp.int32, (1, BLK), 1)
    return [lane < HEAD_DIM, lane >= HEAD_DIM]


def _band():
    rel = np.arange(BLK)[:, None] + BLK - np.arange(2 * BLK)[None, :]
    return (rel >= 0) & (rel <= BLK)


def _zcur(w):
    return lambda off: pl.BlockSpec((w, BLK), lambda hp, i: (i, off + hp))


def _zprev(p, nb):
    return lambda off: pl.BlockSpec((p, BLK), lambda hp, i: (jnp.maximum(i * nb - 1, 0), off + hp))


def _scur(w):
    return pl.BlockSpec((w, BLK), lambda hp, i: (i, hp))


def _pair_rows(t, masks):
    return jnp.concatenate([jnp.where(masks[0], t, 0.0), jnp.where(masks[1], t, 0.0)], axis=0).astype(BF16)


def _pair_bias_bwd(bias):
    return bias.reshape(4, 2, BLK, 2, BLK).transpose(0, 3, 2, 1, 4).reshape(4, 2, BLK, 2 * BLK)


def _unpair_bias_bwd(db):
    return db.reshape(4, 2, BLK, 2, BLK).transpose(0, 3, 2, 1, 4).reshape(N_HEADS, BLK, 2 * BLK)


def _attn_fwd(z, biases, name):
    s = z.shape[0]
    w = min(ATT_ROWS, s)
    n_br = len(BRANCHES)

    def body(*refs):
        q_ref, kp_ref, kc_ref, vp_ref, vc_ref = refs[:5]
        b_refs = refs[5:5 + n_br]
        y_ref, lse_ref, m_ref, l_ref, a_ref = refs[5 + n_br:]
        i = pl.program_id(1)
        masks = _head_masks()
        own_block = lax.broadcasted_iota(jnp.int32, (1, 2 * BLK), 1) >= BLK
        for n, (_, dil) in enumerate(BRANCHES):
            _, p, nb = _att_geom(s, dil)
            for r in range(dil):
                for b in range(nb):
                    rows = _rows(r + p * b, dil)
                    prev_rows = _rows(r + p * (b - 1), dil) if b > 0 else _rows(w - p + r, dil)
                    kprev, vprev = (kc_ref, vc_ref) if b > 0 else (kp_ref, vp_ref)
                    q = q_ref[rows, :] * SCALE
                    k = jnp.concatenate([kprev[prev_rows, :], kc_ref[rows, :]], axis=0).astype(BF16)
                    v = jnp.concatenate([vprev[prev_rows, :], vc_ref[rows, :]], axis=0).astype(BF16)
                    mb = lb = ob = None
                    for hh, mh in enumerate(masks):
                        qh = jnp.where(mh, q, 0.0).astype(BF16)
                        sc = lax.dot_general(qh, k, _DIMS["nt"], preferred_element_type=F32) + b_refs[n][hh]
                        if b == 0:
                            sc = jnp.where(own_block | (i > 0), sc, NEG_INF)
                        mx = jnp.max(sc, axis=1, keepdims=True)
                        e = jnp.exp(sc - mx)
                        den = jnp.sum(e, axis=1, keepdims=True)
                        o = jnp.dot(e.astype(BF16), v, preferred_element_type=F32)
                        if hh == 0:
                            mb = jnp.broadcast_to(mx, (BLK, BLK))
                            lb = jnp.broadcast_to(den, (BLK, BLK))
                            ob = o
                        else:
                            mb = jnp.where(mh, mx, mb)
                            lb = jnp.where(mh, den, lb)
                            ob = jnp.where(mh, o, ob)
                    if n == 0:
                        m_new, l_new, a_new = mb, lb, ob
                    else:
                        m_old = m_ref[rows, :]
                        m_new = jnp.maximum(m_old, mb)
                        al = jnp.exp(m_old - m_new)
                        be = jnp.exp(mb - m_new)
                        l_new = al * l_ref[rows, :] + be * lb
                        a_new = al * a_ref[rows, :] + be * ob
                    if n == n_br - 1:
                        y_ref[rows, :] = a_new / l_new
                        lse_ref[rows, :] = m_new + jnp.log(l_new)
                    else:
                        m_ref[rows, :] = m_new
                        l_ref[rows, :] = l_new
                        a_ref[rows, :] = a_new

    cur, prv = _zcur(w), _zprev(w, 1)
    b_spec = pl.BlockSpec((2, BLK, 2 * BLK), lambda hp, i: (hp, 0, 0))
    return pl.pallas_call(
        body, name=name, grid=(4, s // w), in_specs=[cur(0), prv(4), cur(4), prv(8), cur(8)] + [b_spec] * n_br,
        out_specs=[_scur(w)] * 2, out_shape=[_sds((s, ATTN_W))] * 2,
        scratch_shapes=[pltpu.VMEM((w, BLK), F32)] * 3,
        compiler_params=_params(("parallel", "parallel")),
    )(z, z, z, z, z, *biases)


def _row_stats(mh, dy, y, lse):
    delta = jnp.sum(jnp.where(mh, dy * y, 0.0), axis=1, keepdims=True)
    lse_h = jnp.max(jnp.where(mh, lse, NEG_INF), axis=1, keepdims=True)
    return delta, lse_h


def _attn_bwd(z, biases, dy, y, lse, name):
    s = z.shape[0]
    w = min(ATT_ROWS, s)
    n_steps = s // w
    n_br = len(BRANCHES)

    def body(*refs):
        q_ref, kp_ref, kc_ref, vp_ref, vc_ref, dy_ref, y_ref, lse_ref = refs[:8]
        b_refs = refs[8:8 + n_br]
        outs = refs[8 + n_br:]
        dq_ref, dk_ref, dv_ref = outs[:3]
        x_refs = outs[3:3 + 2 * n_br]
        db_refs = outs[3 + 2 * n_br:3 + 3 * n_br]
        acc_refs = outs[3 + 3 * n_br:]
        i = pl.program_id(1)

        @pl.when(i == 0)
        def _():
            for ref in db_refs:
                ref[...] = jnp.zeros_like(ref)

        masks = _head_masks()
        first_head = lax.broadcasted_iota(jnp.int32, (1, 2 * BLK), 1) < BLK

        sums = {"q": (acc_refs[0], dq_ref), "k": (acc_refs[1], dk_ref), "v": (acc_refs[2], dv_ref)}

        def add_up(n, rows, **vals):
            for key, val in vals.items():
                acc_ref, out_ref = sums[key]
                if n > 0:
                    val = val + acc_ref[rows, :]
                if n == n_br - 1:
                    out_ref[rows, :] = val
                else:
                    acc_ref[rows, :] = val

        for n, (_, dil) in enumerate(BRANCHES):
            _, p, nb = _att_geom(s, dil)
            b_ref, db_ref = b_refs[n], db_refs[n]
            dkx_ref, dvx_ref = x_refs[2 * n], x_refs[2 * n + 1]
            for r in range(dil):
                carry = None
                for b in range(nb):
                    rows = _rows(r + p * b, dil)
                    prev_rows = _rows(r + p * (b - 1), dil) if b > 0 else _rows(w - p + r, dil)
                    kprev, vprev = (kc_ref, vc_ref) if b > 0 else (kp_ref, vp_ref)
                    keys = [(_pair_rows(kprev[prev_rows, :], masks), _pair_rows(vprev[prev_rows, :], masks)),
                            (_pair_rows(kc_ref[rows, :], masks), _pair_rows(vc_ref[rows, :], masks))]
                    q = (q_ref[rows, :] * SCALE).astype(BF16)
                    dy_v = dy_ref[rows, :]
                    dyb = dy_v.astype(BF16)
                    stats = [_row_stats(mh, dy_v, y_ref[rows, :], lse_ref[rows, :]) for mh in masks]
                    delta = jnp.where(first_head, stats[0][0], stats[1][0])
                    lse_h = jnp.where(first_head, stats[0][1], stats[1][1])
                    dq = jnp.zeros((BLK, BLK), F32)
                    dk, dv = [], []
                    for half in range(2):
                        kh, vh = keys[half]
                        sc = lax.dot_general(q, kh, _DIMS["nt"], preferred_element_type=F32) + b_ref[half]
                        pr = jnp.exp(sc - lse_h)
                        if b == 0 and half == 0:
                            pr = pr * (i > 0).astype(F32)
                        dp = lax.dot_general(dyb, vh, _DIMS["nt"], preferred_element_type=F32)
                        ds = pr * (dp - delta)
                        db_ref[half] += ds
                        dsb = ds.astype(BF16)
                        dq = dq + jnp.dot(dsb, kh, preferred_element_type=F32)
                        dk2 = lax.dot_general(dsb, q, _DIMS["tn"], preferred_element_type=F32)
                        dv2 = lax.dot_general(pr.astype(BF16), dyb, _DIMS["tn"], preferred_element_type=F32)
                        dk.append(jnp.where(masks[0], dk2[:BLK], dk2[BLK:]))
                        dv.append(jnp.where(masks[0], dv2[:BLK], dv2[BLK:]))
                    add_up(n, rows, q=dq * SCALE)
                    if b > 0:
                        add_up(n, _rows(r + p * (b - 1), dil), k=carry[0] + dk[0], v=carry[1] + dv[0])
                    else:
                        dkx_ref[_rows(r, dil), :] = dk[0]
                        dvx_ref[_rows(r, dil), :] = dv[0]
                    carry = (dk[1], dv[1])
                add_up(n, _rows(r + p * (nb - 1), dil), k=carry[0], v=carry[1])

    cur, prv = _zcur(w), _zprev(w, 1)
    b_spec = pl.BlockSpec((None, 2, BLK, 2 * BLK), lambda hp, i: (hp, 0, 0, 0))
    x_specs, x_shapes = [], []
    for _, dil in BRANCHES:
        p = BLK * dil
        x_specs += [pl.BlockSpec((p, BLK), lambda hp, i: (i, hp))] * 2
        x_shapes += [_sds((n_steps * p, ATTN_W))] * 2
    outs = pl.pallas_call(
        body, name=name, grid=(4, n_steps),
        in_specs=[cur(0), prv(4), cur(4), prv(8), cur(8)] + [_scur(w)] * 3 + [b_spec] * n_br,
        out_specs=[_scur(w)] * 3 + x_specs + [b_spec] * n_br,
        out_shape=[_sds((s, ATTN_W))] * 3 + x_shapes + [_sds((4, 2, BLK, 2 * BLK))] * n_br,
        scratch_shapes=[pltpu.VMEM((w, BLK), F32)] * 3,
        compiler_params=_params(("parallel", "arbitrary")),
    )(z, z, z, z, z, dy, y, lse, *biases)
    dq, dk, dv = outs[:3]
    extras = [(outs[3 + 2 * n], outs[4 + 2 * n]) for n in range(n_br)]
    return dq, dk, dv, extras, [_unpair_bias_bwd(db) for db in outs[3 + 2 * n_br:]]


ASM_ROWS = 512


def _assemble_dz(dq, dk, dv, extras, dzs, du, name):
    s = dq.shape[0]
    w = min(ATT_ROWS, s)
    n_steps = s // w
    per_step = w // ASM_ROWS
    assert w % ASM_ROWS == 0

    def body(*refs):
        dq_ref, dk_ref, dv_ref, dzs_ref, du_ref = refs[:5]
        x_refs = refs[5:5 + 2 * len(extras)]
        o_ref, acc_ref = refs[-2:]
        j = pl.program_id(0)
        step = j // per_step
        has_next = (step < n_steps - 1).astype(F32)
        last_of_step = ((j + 1) % per_step == 0).astype(F32)
        o_ref[:, 0:ATTN_W] = dq_ref[...].astype(BF16)
        o_ref[:, 3 * ATTN_W:3 * ATTN_W + 2 * SGU_W] = dzs_ref[...].astype(BF16)
        o_ref[:, 3 * ATTN_W + 2 * SGU_W:IN_W] = du_ref[...].astype(BF16)
        for part, (base_ref, col) in enumerate(((dk_ref, ATTN_W), (dv_ref, 2 * ATTN_W))):
            acc_ref[...] = base_ref[...]
            for n, (_, dil) in enumerate(BRANCHES):
                rows = min(BLK * dil, ASM_ROWS)
                scale = has_next if BLK * dil >= w else has_next * last_of_step
                acc_ref[ASM_ROWS - rows:, :] += x_refs[2 * n + part][...] * scale
            o_ref[:, col:col + ATTN_W] = acc_ref[...].astype(BF16)

    def x_spec(dil):
        p = BLK * dil
        rows = min(p, ASM_ROWS)
        blocks_per_step = p // rows
        total = n_steps * blocks_per_step

        def idx(j):
            step = j // per_step
            within = (j % per_step) - (per_step - blocks_per_step)
            return (jnp.clip((step + 1) * blocks_per_step + jnp.maximum(within, 0), 0, total - 1), 0)

        return pl.BlockSpec((rows, ATTN_W), idx)

    in_specs = [_rb(ASM_ROWS, ATTN_W)] * 3 + [_rb(ASM_ROWS, 2 * SGU_W), _rb(ASM_ROWS, SSM_W)]
    args = [dq, dk, dv, dzs, du]
    for (dkx, dvx), (_, dil) in zip(extras, BRANCHES):
        in_specs += [x_spec(dil)] * 2
        args += [dkx, dvx]
    return pl.pallas_call(
        body, name=name, grid=(s // ASM_ROWS,), in_specs=in_specs, out_specs=_rb(ASM_ROWS, IN_W),
        out_shape=_sds((s, IN_W), BF16), scratch_shapes=[pltpu.VMEM((ASM_ROWS, ATTN_W), F32)],
        compiler_params=_params(("parallel",)),
    )(*args)


def _t5_bucket(dist):
    max_exact = N_BUCKETS // 2
    d = np.maximum(dist, 0)
    large = max_exact + (np.log(np.maximum(d, 1) / max_exact) / np.log(REL_MAX / max_exact)
                         * (N_BUCKETS - max_exact)).astype(np.int32)
    large = np.minimum(large, N_BUCKETS - 1)
    return np.where(d < max_exact, d, large).astype(np.int32)


def _bias_tables(rel_bias):
    period = 3 * BLK
    tabs = []
    for _, dil in BRANCHES:
        onehot = np.zeros((period, N_BUCKETS), np.float32)
        d = np.arange(BLK + 1)
        onehot[d, _t5_bucket((BLK - d) * dil)] = 1.0
        f = jnp.dot(jnp.asarray(onehot), rel_bias, precision=lax.Precision.HIGHEST)
        flat = jnp.tile(f.T, (1, BLK))[:, :BLK * (period - 1)]
        tab = flat.reshape(N_HEADS, BLK, period - 1)[:, :, :2 * BLK]
        tabs.append(jnp.where(_band()[None], tab, NEG_INF))
    return tabs


def _bucket_onehot():
    maps = []
    q = np.arange(BLK)[:, None]
    k = np.arange(2 * BLK)[None, :]
    rel = q + BLK - k
    for _, dil in BRANCHES:
        maps.append(np.where((rel >= 0) & (rel <= BLK), _t5_bucket(rel * dil), -1).reshape(-1))
    bmap = jnp.asarray(np.concatenate(maps).astype(np.int32))
    return (bmap[:, None] == jnp.arange(128, dtype=jnp.int32)[None, :]).astype(BF16)


def _block_diag(t):
    g, n, c = t.shape
    eye = jnp.eye(g, dtype=t.dtype)
    return (t[:, :, None, :] * eye[:, None, :, None]).reshape(g * n, g * c)


def _ssm_prep(a_re, a_im, log_dt, b_re, b_im, c_re, c_im):
    lam = lax.complex(a_re, a_im)
    dt = jnp.exp(log_dt)[:, None]
    a_bar = jnp.exp(lam * dt)
    b_bar = ((a_bar - 1.0) / lam)[:, :, None] * lax.complex(b_re, b_im)
    bdt = jnp.concatenate([_block_diag(jnp.real(b_bar)), _block_diag(jnp.imag(b_bar))], axis=0)
    cd = jnp.concatenate([_block_diag(jnp.transpose(c_re, (0, 2, 1))),
                          _block_diag(-jnp.transpose(c_im, (0, 2, 1)))], axis=0)
    return jnp.real(a_bar).reshape(-1), jnp.imag(a_bar).reshape(-1), bdt, cd


def _powers(ar, ai):
    pr, pi = ar[:, None], ai[:, None]
    k = 1
    while k < 8:
        lr, li = pr[:, -1:], pi[:, -1:]
        pr, pi = (jnp.concatenate([pr, pr * lr - pi * li], axis=1),
                  jnp.concatenate([pi, pr * li + pi * lr], axis=1))
        k *= 2
    return pr, pi


def _sgu_bias_expand(b):
    return jnp.repeat(b.T, 64, axis=1)


def _layer_fwd(i, h, a1, p_i, big, small, bias_tabs, next_gain):
    nm = "l%d_" % i
    sv = {"h": h}
    if a1 is None:
        a1 = _rms_fwd(h, small["norm_attn_g"][i], nm + "rms_attn")
    z = _mm(a1, big["w_in"], "nt", nm + "in_proj")
    y_attn, lse = _attn_fwd(z, [t[0] for t in bias_tabs], nm + "attn_fwd")
    bexp = _sgu_bias_expand(small["sgu_b"][i])
    y_sgu = _sgu_fwd(z, small["sgu_ln_g"][i], small["sgu_ln_b"][i], small["sgu_w"][i], bexp, nm + "sgu_fwd")
    ar, ai, bdt, cd = _ssm_prep(*[small[k][i] for k in ("ssm_a_re", "ssm_a_im", "ssm_log_dt", "ssm_b_re",
                                                         "ssm_b_im", "ssm_c_re", "ssm_c_im")])
    xr, xi, yc = _ssm_fwd(z, bdt.astype(BF16), cd.astype(BF16), _scan_tables(*_powers(ar, ai), False),
                          nm + "ssm_core")
    y_ssm = _ssm_post_fwd(yc, z, small["ssm_d"][i], big["ssm_glu_w"], small["ssm_glu_b"][i], nm + "ssm_post")
    mix = _mix_fwd(y_attn, y_sgu, y_ssm, small["branch_norm_g"][i], nm + "mix")
    if "rest" in big:
        big = dict({k: t for k, t in big.items() if k != "rest"}, **big["rest"](mix))
    h2, a2 = _mm(mix, big["w_out"], "nn", nm + "out_proj", add=h, norm_gain=small["norm_ffn_g"][i])
    hu = _mm(a2, big["ffn_w_up"], "nt", nm + "ffn_up", out_dtype=BF16)
    hv, hg, act = _conv_fwd(hu, big["ffn_conv_w"], small["ffn_conv_b"][i], nm + "ffn_conv")
    h3, a3 = _mm(act, big["ffn_w_down"], "nn", nm + "ffn_down", add=h2, norm_gain=small["norm_ple_g"][i])
    gp = _mm(a3, big["ple_w_gate"], "nn", nm + "ple_gate", out_dtype=BF16)
    pp = _mm(p_i, big["ple_w_proj"], "nt", nm + "ple_proj", out_dtype=BF16)
    h4, a_next = _ple_fwd(h3, gp, pp, next_gain, nm + "ple_add")
    sv.update(big=big, a1=a1, z=z, y_attn=y_attn, lse=lse, y_sgu=y_sgu, y_ssm=y_ssm, yc=yc, xr=xr, xi=xi, mix=mix, h2=h2,
              a2=a2, hu=hu, hv=hv, hg=hg, act=act, h3=h3, a3=a3, gp=gp, pp=pp)
    return h4, a_next, sv


def _layer_bwd(i, dh4, sv, p_i, big, small, bias_tabs, ffn_done=None):
    nm = "l%d_" % i
    g = {}
    dpp, dgp = _ple_bwd(dh4, sv["gp"], sv["pp"], nm + "ple_bwd")
    g["ple_w_proj"] = _mm(dpp, p_i, "tn", nm + "d_ple_proj", out_dtype=BF16)
    g["ple_w_gate"] = _mm(sv["a3"], dgp, "tn", nm + "d_ple_gate", out_dtype=BF16)
    dh3, dgain = _mm(dgp, big["ple_w_gate"], "nt", nm + "ple_gate_t", add=dh4,
                     norm_bwd=(sv["h3"], small["norm_ple_g"][i]))
    g["norm_ple_g"] = dgain.reshape(D_MODEL)
    g["ffn_w_down"] = _mm(sv["act"], dh3, "tn", nm + "d_ffn_down", out_dtype=BF16)
    dact = _mm(dh3, big["ffn_w_down"], "nt", nm + "ffn_down_t", out_dtype=BF16)
    dhu, g["ffn_conv_w"], dcb = _conv_bwd(dact, sv["hv"], sv["hg"], sv["hu"], big["ffn_conv_w"],
                                          nm + "ffn_conv_bwd")
    g["ffn_conv_b"] = dcb.reshape(2 * D_FF)
    g["ffn_w_up"] = _mm(dhu, sv["a2"], "tn", nm + "d_ffn_up", out_dtype=BF16)
    dh2, dgain = _mm(dhu, big["ffn_w_up"], "nn", nm + "ffn_up_t", add=dh3,
                     norm_bwd=(sv["h2"], small["norm_ffn_g"][i]))
    g["norm_ffn_g"] = dgain.reshape(D_MODEL)
    if ffn_done is not None:
        small = ffn_done(g, small)
    g["w_out"] = _mm(sv["mix"], dh2, "tn", nm + "d_out_proj", out_dtype=BF16)
    dya, dysg, dyss, g["branch_norm_g"] = _mix_bwd(dh2, big["w_out"], sv["y_attn"], sv["y_sgu"], sv["y_ssm"],
                                                   small["branch_norm_g"][i], nm + "mix_bwd")
    ssm_keys = ("ssm_a_re", "ssm_a_im", "ssm_log_dt", "ssm_b_re", "ssm_b_im", "ssm_c_re", "ssm_c_im")
    (ar, ai, bdt, cd), prep_vjp = jax.vjp(_ssm_prep, *[small[k][i] for k in ssm_keys])
    dy1, dgl, y2, dud, g["ssm_d"], g["ssm_glu_b"] = _ssm_post_bwd(
        dyss, sv["yc"], sv["z"], small["ssm_d"][i], big["ssm_glu_w"], small["ssm_glu_b"][i], nm + "ssm_post_bwd")
    g["ssm_glu_w"] = _mm(y2, dgl, "tn", nm + "d_ssm_glu", out_dtype=BF16)
    du, dbdt, dcd, dar, dai = _ssm_bwd(dy1, dud, sv["z"], sv["xr"], sv["xi"], bdt.astype(BF16), cd.astype(BF16),
                                       _scan_tables(*_powers(ar, ai), True), nm + "ssm_core_bwd")
    for k, val in zip(ssm_keys, prep_vjp((dar, dai, dbdt, dcd))):
        g[k] = val
    bexp, bexp_vjp = jax.vjp(_sgu_bias_expand, small["sgu_b"][i])
    dzs, g["sgu_w"], dbexp, g["sgu_ln_g"], g["sgu_ln_b"] = _sgu_bwd(
        sv["z"], dysg, small["sgu_ln_g"][i], small["sgu_ln_b"][i], small["sgu_w"][i], bexp, nm + "sgu_bwd")
    g["sgu_b"] = bexp_vjp(dbexp)[0]
    dq, dk, dv, extras, dbs = _attn_bwd(sv["z"], [t[1] for t in bias_tabs], dya, sv["y_attn"], sv["lse"],
                                        nm + "attn_bwd")
    dbs = [db.reshape(N_HEADS, BLK * 2 * BLK) for db in dbs]
    dz = _assemble_dz(dq, dk, dv, extras, dzs, du, nm + "assemble_dz")
    g["w_in"] = _mm(dz, sv["a1"], "tn", nm + "d_in_proj", out_dtype=BF16)
    dh, dgain = _mm(dz, big["w_in"], "nn", nm + "in_proj_t", add=dh2, norm_bwd=(sv["h"], small["norm_attn_g"][i]))
    g["norm_attn_g"] = dgain.reshape(D_MODEL)
    return dh, g, jnp.concatenate(dbs, axis=1)


def _local_step(x, p, target, layer_weights, small, layer_done=None):
    depth = p.shape[0]
    bias_tabs = [(t, _pair_bias_bwd(t)) for t in _bias_tables(small["rel_bias"])]
    h, a1 = x, None
    saved = []
    for i in range(depth):
        next_gain = small["norm_attn_g"][i + 1] if i + 1 < depth else None
        h, a1, sv = _layer_fwd(i, h, a1, p[i], layer_weights(i, h), small, bias_tabs, next_gain)
        saved.append(sv)
    dh, loss, g_final = _loss_head(h, target, small["final_norm_g"], "loss_head")
    layer_grads = [None] * depth
    dbias = [None] * depth
    for i in reversed(range(depth)):
        ffn_done = None if layer_done is None else (lambda g, sm, i=i: layer_done(i, "ffn", g, sm))
        dh, layer_grads[i], dbias[i] = _layer_bwd(i, dh, saved[i], p[i], saved[i]["big"], small, bias_tabs,
                                                  ffn_done)
        if layer_done is not None:
            small = layer_done(i, "all", layer_grads[i], small)
    big_grads = [{k: lg.pop(k) for k in COMM_NAMES} for lg in layer_grads]
    grads = {k: jnp.stack([layer_grads[i][k] for i in range(depth)]) for k in layer_grads[0]}
    grads["final_norm_g"] = g_final
    g_rb = _mm(sum(dbias[1:], dbias[0]), _bucket_onehot(), "nn", "d_rel_bias", tk=2048)
    grads["rel_bias"] = g_rb[:, :N_BUCKETS].T
    return loss, dh, big_grads, grads


_ANY = pl.BlockSpec(memory_space=pl.ANY)
MESH_IDS = pl.DeviceIdType.MESH


def _slot(ref, axis, j):
    return ref.at[(slice(None),) * axis + (j,)]


def _all_gather(blocks, axis, name):
    nt = len(blocks)

    def body(*refs):
        x_refs, o_refs = refs[:nt], refs[nt:2 * nt]
        send_sems, recv_sems, local_sems = refs[2 * nt:]
        x, y, c = lax.axis_index("x"), lax.axis_index("y"), lax.axis_index("c")
        me, sibling = (x, y, c), (x, y, 1 - c)
        chips = [(1 - x, y), (x, 1 - y), (1 - x, 1 - y)]

        def slot(t, px, py, pc):
            return _slot(o_refs[t], axis, 4 * px + 2 * py + pc)

        def copy(t, k, blk, to, src=None):
            return pltpu.make_async_remote_copy(
                src_ref=slot(t, *blk) if src is None else src, dst_ref=slot(t, *blk),
                send_sem=send_sems.at[7 * t + k], recv_sem=recv_sems.at[7 * t + k],
                device_id=to, device_id_type=MESH_IDS)

        mine = [pltpu.make_async_copy(x_refs[t], slot(t, *me), local_sems.at[t]) for t in range(nt)]
        for cp in mine:
            cp.start()
        first = []
        for t in range(nt):
            first.append(copy(t, 0, me, sibling, src=x_refs[t]))
            first += [copy(t, 1 + j, me, (*chip, c), src=x_refs[t]) for j, chip in enumerate(chips)]
        for cp in first:
            cp.start()
        passed = []
        for t in range(nt):
            for j, chip in enumerate(chips):
                copy(t, 1 + j, (*chip, c), me).wait_recv()
                passed.append(copy(t, 4 + j, (*chip, c), sibling))
                passed[-1].start()
        for t in range(nt):
            copy(t, 0, sibling, me).wait_recv()
            for j, chip in enumerate(chips):
                copy(t, 4 + j, (*chip, 1 - c), me).wait_recv()
        for cp in first + passed:
            cp.wait_send()
        for cp in mine:
            cp.wait()

    out_shape = [jax.ShapeDtypeStruct(b.shape[:axis] + (N_DEV,) + b.shape[axis:], b.dtype) for b in blocks]
    return pl.pallas_call(
        body, name=name, out_shape=out_shape, in_specs=[_ANY] * nt, out_specs=[_ANY] * nt,
        scratch_shapes=[pltpu.SemaphoreType.DMA((7 * nt,)), pltpu.SemaphoreType.DMA((7 * nt,)),
                        pltpu.SemaphoreType.DMA((nt,))],
    )(*blocks)


def _peer(k):
    x, y, c = lax.axis_index("x"), lax.axis_index("y"), lax.axis_index("c")
    px = 1 - x if k & 4 else x
    py = 1 - y if k & 2 else y
    pc = 1 - c if k & 1 else c
    return (px, py, pc), 4 * px + 2 * py + pc


_HBM = pl.BlockSpec(memory_space=pltpu.HBM)
_SEM = pl.BlockSpec(memory_space=pltpu.SEMAPHORE)
_EFFECT = pltpu.SideEffectType.DATAFLOW_SIDE_EFFECTING


def _split_copy(src_ref, land_ref, send_sems, recv_sems, t, k, gather):
    peer, idx = _peer(k)
    _, me = _peer(0)
    return pltpu.make_async_remote_copy(
        src_ref=src_ref if gather else src_ref.at[idx], dst_ref=land_ref.at[me],
        send_sem=send_sems.at[7 * t + k - 1], recv_sem=recv_sems.at[7 * t + k - 1],
        device_id=peer, device_id_type=MESH_IDS)


def _exchange_start(srcs, lands, gather, name):
    nt = len(srcs)

    def body(*refs):
        src_refs, land_refs = refs[:nt], refs[nt:2 * nt]
        send_sems, recv_sems = refs[2 * nt:2 * nt + 2]
        token = refs[-1]
        for k in range(1, N_DEV):
            for t in range(nt):
                _split_copy(src_refs[t], land_refs[t], send_sems, recv_sems, t, k, gather).start()
        token[...] = jnp.zeros_like(token)

    hbm = lambda a: pltpu.HBM(a.shape, a.dtype)
    outs = pl.pallas_call(
        body, name=name,
        out_shape=(pltpu.SemaphoreType.DMA((7 * nt,)), pltpu.SemaphoreType.DMA((7 * nt,)),
                   *[hbm(a) for a in srcs], *[hbm(a) for a in lands], jax.ShapeDtypeStruct((8, 128), F32)),
        in_specs=[_HBM] * (2 * nt),
        out_specs=(_SEM, _SEM, *[_HBM] * (2 * nt), pl.BlockSpec(memory_space=pltpu.VMEM)),
        input_output_aliases={j: 2 + j for j in range(2 * nt)},
        compiler_params=pltpu.CompilerParams(has_side_effects=_EFFECT),
    )(*[pltpu.with_memory_space_constraint(a, pltpu.HBM) for a in list(srcs) + list(lands)])
    return outs[0], outs[1], outs[2:2 + nt], outs[2 + nt:2 + 2 * nt], outs[-1]


def _exchange_wait(send_sems, recv_sems, srcs, lands, after, gather, name):
    nt = len(srcs)

    def body(*refs):
        src_refs, land_refs = refs[:nt], refs[nt:2 * nt]
        send_sems, recv_sems = refs[2 * nt:2 * nt + 2]
        for k in range(1, N_DEV):
            _, idx = _peer(k)
            for t in range(nt):
                _split_copy(src_refs[t], land_refs[t], send_sems, recv_sems, t, k, gather).wait_send()
                arrival = pltpu.make_async_remote_copy(
                    src_ref=land_refs[t].at[idx], dst_ref=land_refs[t].at[idx],
                    send_sem=send_sems.at[7 * t + k - 1], recv_sem=recv_sems.at[7 * t + k - 1],
                    device_id=_peer(k)[0], device_id_type=MESH_IDS)
                arrival.wait_recv()

    hbm = lambda a: pltpu.HBM(a.shape, a.dtype)
    outs = pl.pallas_call(
        body, name=name, out_shape=tuple(hbm(a) for a in list(srcs) + list(lands)),
        in_specs=[_HBM] * (2 * nt) + [_SEM, _SEM, _ANY], out_specs=tuple([_HBM] * (2 * nt)),
        input_output_aliases={j: j for j in range(2 * nt)},
        compiler_params=pltpu.CompilerParams(has_side_effects=_EFFECT),
    )(*srcs, *lands, send_sems, recv_sems, after)
    return outs[nt:]


def _adamw(parts, w, m, v, name, tr):
    n_layers, r, c_ = w.shape
    assert len(parts) == n_layers

    def body(*refs):
        p_refs = refs[:n_layers]
        w_ref, m_ref, v_ref, g_ref, d_ref, mo_ref, vo_ref = refs[n_layers:]

        def update(p_ref):
            g = p_ref[0].astype(F32)
            for j in range(1, N_DEV):
                g = g + p_ref[j].astype(F32)
            m2 = ADAM_B1 * m_ref[...] + (1.0 - ADAM_B1) * g
            v2 = ADAM_B2 * v_ref[...] + (1.0 - ADAM_B2) * (g * g)
            m_hat = m2 / (1.0 - ADAM_B1 ** ADAM_STEP)
            v_hat = v2 / (1.0 - ADAM_B2 ** ADAM_STEP)
            g_ref[...] = g
            d_ref[...] = -ADAM_LR * (m_hat / (jnp.sqrt(v_hat) + ADAM_EPS) + ADAM_WD * w_ref[...])
            mo_ref[...] = m2
            vo_ref[...] = v2

        for layer in range(n_layers):
            pl.when(pl.program_id(0) == layer)(lambda layer=layer: update(p_refs[layer]))

    spec = pl.BlockSpec((None, tr, c_), lambda l, i: (l, i, 0))
    p_spec = pl.BlockSpec((N_DEV, tr, c_), lambda l, i: (0, i, 0))
    return pl.pallas_call(
        body, name=name, grid=(n_layers, r // tr), in_specs=[p_spec] * n_layers + [spec] * 3,
        out_specs=[spec] * 4, out_shape=[_sds((n_layers, r, c_))] * 4,
        compiler_params=_params(("parallel", "parallel")),
    )(*parts, w, m, v)


def _pack_rows(n_elems, align):
    rows = -(-n_elems // PACK_COLS)
    return -(-rows // align) * align


def _pack(arrs, rows, dtype=F32):
    flat = jnp.concatenate([a.reshape(-1) for a in arrs]).astype(dtype)
    return jnp.pad(flat, (0, rows * PACK_COLS - flat.shape[0])).reshape(rows, PACK_COLS)


def _unpack(pack, shapes):
    flat = pack.reshape(-1)
    out, off = [], 0
    for shp in shapes:
        size = int(np.prod(shp))
        out.append(flat[off:off + size].reshape(shp))
        off += size
    return out


def _tile_rows(rows, target, align=16):
    best = align
    for t in range(align, target + 1, align):
        if rows % t == 0:
            best = t
    return best


COMM_NAMES = ("w_in", "ssm_glu_w", "w_out", "ffn_w_up", "ffn_w_down", "ple_w_gate", "ple_w_proj")
COMM_TRANSPOSED = ("w_in", "ffn_w_up", "ple_w_proj")
COMM_EARLY = ("ple_w_proj", "ple_w_gate", "ffn_w_down", "ffn_w_up")
COMM_LATE = ("w_in", "ssm_glu_w", "w_out")
SMALL_TILE_ROWS = 64
CONV_NAME = "ffn_conv_w"


def _to_comm(name, a):
    return jnp.swapaxes(a, 1, 2) if name in COMM_TRANSPOSED else a


def kernel(x, p, rel_bias, norm_attn_g, w_in, sgu_ln_g, sgu_ln_b, sgu_w, sgu_b, ssm_a_re, ssm_a_im, ssm_log_dt, ssm_b_re, ssm_b_im, ssm_c_re, ssm_c_im, ssm_d, ssm_glu_w, ssm_glu_b, branch_norm_g, w_out, norm_ffn_g, ffn_w_up, ffn_conv_w, ffn_conv_b, ffn_w_down, norm_ple_g, ple_w_gate, ple_w_proj, final_norm_g, loss_target, m_rel_bias, m_norm_attn_g, m_w_in, m_sgu_ln_g, m_sgu_ln_b, m_sgu_w, m_sgu_b, m_ssm_a_re, m_ssm_a_im, m_ssm_log_dt, m_ssm_b_re, m_ssm_b_im, m_ssm_c_re, m_ssm_c_im, m_ssm_d, m_ssm_glu_w, m_ssm_glu_b, m_branch_norm_g, m_w_out, m_norm_ffn_g, m_ffn_w_up, m_ffn_conv_w, m_ffn_conv_b, m_ffn_w_down, m_norm_ple_g, m_ple_w_gate, m_ple_w_proj, m_final_norm_g, v_rel_bias, v_norm_attn_g, v_w_in, v_sgu_ln_g, v_sgu_ln_b, v_sgu_w, v_sgu_b, v_ssm_a_re, v_ssm_a_im, v_ssm_log_dt, v_ssm_b_re, v_ssm_b_im, v_ssm_c_re, v_ssm_c_im, v_ssm_d, v_ssm_glu_w, v_ssm_glu_b, v_branch_norm_g, v_w_out, v_norm_ffn_g, v_ffn_w_up, v_ffn_conv_w, v_ffn_conv_b, v_ffn_w_down, v_norm_ple_g, v_ple_w_gate, v_ple_w_proj, v_final_norm_g):
    given = dict(locals())
    w = {n: given[n] for n in WEIGHT_NAMES}
    m = {n: given["m_" + n] for n in WEIGHT_NAMES}
    v = {n: given["v_" + n] for n in WEIGHT_NAMES}
    depth = p.shape[0]
    dev = 4 * lax.axis_index("x") + 2 * lax.axis_index("y") + lax.axis_index("c")

    wc = {n: _to_comm(n, w[n]) for n in COMM_NAMES}
    wb = {n: wc[n].astype(BF16) for n in COMM_NAMES}
    conv_local = [w[CONV_NAME], m[CONV_NAME], v[CONV_NAME]]
    conv_rows = _pack_rows(sum(int(np.prod(t.shape)) for t in conv_local), 8)
    conv_g, = _all_gather([_pack(conv_local, conv_rows)], 0, "gather_conv_taps")
    conv_parts = zip(*[_unpack(conv_g[j], [t.shape for t in conv_local]) for j in range(N_DEV)])
    conv_w, conv_m, conv_v = [jnp.concatenate(parts, axis=2) for parts in conv_parts]
    small = {n: w[n] for n in SMALL_NAMES}

    def whole(names, blocks):
        return {n: t.reshape(-1, t.shape[-1]) for n, t in zip(names, blocks)}

    def own_slot(block):
        return lax.dynamic_update_slice_in_dim(jnp.zeros((N_DEV,) + block.shape, block.dtype), block[None], dev, 0)

    def start_gather(names, i, after):
        srcs, after = lax.optimization_barrier(([wb[n][i] for n in names], after))
        return _exchange_start(srcs, [own_slot(s) for s in srcs], True, "gather_weights_%d_start" % i), after

    def wait_gather(names, i, started, after):
        send_sems, recv_sems, srcs, lands, _ = started
        return whole(names, _exchange_wait(send_sems, recv_sems, srcs, lands, after, True,
                                           "gather_weights_%d_wait" % i))

    at_once = ("w_in", "ssm_glu_w")
    later = tuple(n for n in COMM_NAMES if n not in at_once)
    w_in_0 = _all_gather([wb[n][0] for n in at_once], 0, "gather_w_in_0")
    gathering = {}
    gathering[0], (w_in_0, _) = start_gather(later, 0, (w_in_0, conv_g))
    small["norm_attn_g"] = small["norm_attn_g"] + gathering[0][4][0, 0]

    def layer_weights(i, h):
        if i > 0:
            got = wait_gather(COMM_NAMES, i, gathering.pop(i), h)
            if i + 1 < depth:
                gathering[i + 1], ordered = start_gather(COMM_NAMES, i + 1, got["w_in"])
                got["w_in"] = ordered + gathering[i + 1][4][0, 0].astype(BF16)
            return dict(got, **{CONV_NAME: conv_w[i]})

        def rest(after):
            got = wait_gather(later, 0, gathering.pop(0), after)
            if depth > 1:
                gathering[1], ordered = start_gather(COMM_NAMES, 1, got["w_out"])
                got["w_out"] = ordered + gathering[1][4][0, 0].astype(BF16)
            return got

        return dict(whole(at_once, w_in_0), **{CONV_NAME: conv_w[0], "rest": rest})

    def as_slots(g, n):
        return g.reshape((N_DEV,) + wc[n].shape[1:])

    scattering = {}

    def layer_done(i, stage, g, small_now):
        if stage == "all" and i == 0:
            return small_now
        names = COMM_EARLY if stage == "ffn" else COMM_LATE
        srcs = [as_slots(g[n], n) for n in names]
        lands = [own_slot(lax.dynamic_index_in_dim(s, dev, 0, keepdims=False)) for s in srcs]
        started = _exchange_start(srcs, lands, False, "scatter_weight_grads_%d_%s_start" % (i, stage))
        scattering[i, stage] = (names, started)
        pin = "branch_norm_g" if stage == "ffn" else "norm_ple_g"
        return dict(small_now, **{pin: small_now[pin] + started[4][0, 0]})

    loss, dx, big_grads, grads = _local_step(x[0], p[:, 0], loss_target[0], layer_weights, small, layer_done)
    loss = lax.psum(loss, ("x", "y", "c"))

    recv = [{} for _ in range(depth)]
    for (i, stage), (names, (send_sems, recv_sems, srcs, lands, _)) in scattering.items():
        got = _exchange_wait(send_sems, recv_sems, srcs, lands, dx, False,
                             "scatter_weight_grads_%d_%s_wait" % (i, stage))
        recv[i].update(zip(names, got))
    srcs = [as_slots(big_grads[0][n], n) for n in COMM_LATE]
    lands = [own_slot(lax.dynamic_index_in_dim(s, dev, 0, keepdims=False)) for s in srcs]
    last = _exchange_start(srcs, lands, False, "scatter_weight_grads_0_all_start")
    out = {}

    def update(n, pin=None):
        weight = wc[n] if pin is None else wc[n] + pin
        res = _adamw([recv[i][n] for i in range(depth)], weight, _to_comm(n, m[n]), _to_comm(n, v[n]),
                     "adamw_" + n, _tile_rows(wc[n].shape[1], 256))
        out[n] = [_to_comm(n, r) for r in res]

    for j, n in enumerate(COMM_EARLY):
        update(n, last[4][0, 0] if j == 0 else None)
    got = _exchange_wait(last[0], last[1], last[2], last[3], out[COMM_EARLY[-1]][0], False,
                         "scatter_weight_grads_0_all_wait")
    recv[0].update(zip(COMM_LATE, got))

    rep_names = SMALL_NAMES + (CONV_NAME,)
    rep_w = dict({n: w[n] for n in SMALL_NAMES}, **{CONV_NAME: conv_w})
    rep_m = dict({n: m[n] for n in SMALL_NAMES}, **{CONV_NAME: conv_m})
    rep_v = dict({n: v[n] for n in SMALL_NAMES}, **{CONV_NAME: conv_v})
    rep_shapes = [rep_w[n].shape for n in rep_names]
    rep_rows = _pack_rows(sum(int(np.prod(s)) for s in rep_shapes), SMALL_TILE_ROWS)
    rep_parts, = _all_gather([_pack([grads[n] for n in rep_names], rep_rows)], 0, "gather_small_grads")
    for n in COMM_LATE:
        update(n)
    rep_out = _adamw([rep_parts], *[_pack([src[n] for n in rep_names], rep_rows)[None] for src in (rep_w, rep_m, rep_v)],
                     "adamw_replicated", SMALL_TILE_ROWS)
    for n, vals in zip(rep_names, zip(*[_unpack(r[0], rep_shapes) for r in rep_out])):
        out[n] = list(vals)
    shard = ffn_conv_w.shape[2]
    out[CONV_NAME] = [lax.dynamic_slice_in_dim(t, dev * shard, shard, axis=2) for t in out[CONV_NAME]]
    results = [[out[n][kind] for n in WEIGHT_NAMES] for kind in range(4)]
    return (loss, dx[None], *results[0], *results[1], *results[2], *results[3])
```

```python
import math

import numpy as np
import jax
import jax.numpy as jnp
from jax import lax
from jax.experimental import pallas as pl
from jax.experimental.pallas import tpu as pltpu

F32 = jnp.float32
BF16 = jnp.bfloat16

D_MODEL = 1024
HEAD_DIM = 64
N_HEADS = 8
ATTN_W = 512
SGU_W = 256
SGU_GROUPS = 4
SGU_CHUNK = 128
SSM_W = 256
SSM_GROUPS = 16
SSM_STATE = 64
SSM_NS = SSM_GROUPS * SSM_STATE
IN_W = 2304
D_FF = 2816
BRANCHES = ((128, 1), (512, 4), (2048, 16))
BLK = 128
N_BUCKETS = 32
REL_MAX = 2048
EPS = 1e-6
NEG_INF = -1e30
N_DEV = 8

ADAM_LR = 0.001
ADAM_B1 = 0.9
ADAM_B2 = 0.999
ADAM_EPS = 1e-08
ADAM_WD = 0.01
ADAM_STEP = 10

VMEM_LIMIT_BYTES = 56 * 1024 * 1024
GELU_C = math.sqrt(2.0 / math.pi)

SMALL_NAMES = ("rel_bias", "norm_attn_g", "sgu_ln_g", "sgu_ln_b", "sgu_w", "sgu_b", "ssm_a_re", "ssm_a_im",
               "ssm_log_dt", "ssm_b_re", "ssm_b_im", "ssm_c_re", "ssm_c_im", "ssm_d", "ssm_glu_b",
               "branch_norm_g", "norm_ffn_g", "ffn_conv_b", "norm_ple_g", "final_norm_g")
WEIGHT_NAMES = ("rel_bias", "norm_attn_g", "w_in", "sgu_ln_g", "sgu_ln_b", "sgu_w", "sgu_b", "ssm_a_re",
                "ssm_a_im", "ssm_log_dt", "ssm_b_re", "ssm_b_im", "ssm_c_re", "ssm_c_im", "ssm_d", "ssm_glu_w",
                "ssm_glu_b", "branch_norm_g", "w_out", "norm_ffn_g", "ffn_w_up", "ffn_conv_w", "ffn_conv_b",
                "ffn_w_down", "norm_ple_g", "ple_w_gate", "ple_w_proj", "final_norm_g")
PACK_COLS = 512


def _params(sem):
    return pltpu.CompilerParams(dimension_semantics=sem, vmem_limit_bytes=VMEM_LIMIT_BYTES)


def _pick(dim, target):
    if dim <= target:
        return dim
    best = None
    for t in range(128, target + 1, 128):
        if dim % t == 0:
            best = t
    return dim if best is None else best


def _gelu(x):
    return 0.5 * x * (1.0 + jnp.tanh(GELU_C * (x + 0.044715 * (x * x * x))))


def _gelu_grad(x):
    t = jnp.tanh(GELU_C * (x + 0.044715 * (x * x * x)))
    return 0.5 * (1.0 + t) + 0.5 * x * (1.0 - t * t) * (GELU_C * (1.0 + 3.0 * 0.044715 * (x * x)))


def _sigmoid(x):
    return 1.0 / (1.0 + jnp.exp(-x))


_DIMS = {"nn": (((1,), (0,)), ((), ())), "tn": (((0,), (0,)), ((), ())), "nt": (((1,), (1,)), ((), ()))}


MM_TILE = D_FF // 2


def _mm(a, b, mode, name, add=None, out_dtype=F32, norm_gain=None, norm_bwd=None, tm=MM_TILE, tn=MM_TILE,
        tk=MM_TILE):
    if mode == "nn":
        m, k = a.shape
        k2, n = b.shape
    elif mode == "tn":
        k, m = a.shape
        k2, n = b.shape
    else:
        m, k = a.shape
        n, k2 = b.shape
    assert k == k2, (name, a.shape, b.shape, mode)
    tm, tn, tk = _pick(m, tm), _pick(n, tn), _pick(k, tk)
    nk = k // tk
    dims = _DIMS[mode]
    has_add = add is not None
    has_norm = norm_gain is not None
    has_nbwd = norm_bwd is not None
    assert not (has_norm or has_nbwd) or tn == n

    def body(*refs):
        a_ref, b_ref = refs[:2]
        rest = list(refs[2:])
        add_ref = rest.pop(0) if has_add else None
        g_ref = rest.pop(0) if has_norm else None
        h_ref, hg_ref = (rest.pop(0), rest.pop(0)) if has_nbwd else (None, None)
        o_ref = rest.pop(0)
        n_ref = rest.pop(0) if has_norm else None
        dg_ref = rest.pop(0) if has_nbwd else None
        part = lax.dot_general(a_ref[...].astype(BF16), b_ref[...].astype(BF16), dims,
                               preferred_element_type=F32)
        if has_nbwd:
            @pl.when((pl.program_id(0) == 0) & (pl.program_id(2) == 0))
            def _():
                dg_ref[...] = jnp.zeros_like(dg_ref)

        def finish(r):
            if has_nbwd:
                x = h_ref[...]
                scale = lax.rsqrt(jnp.mean(x * x, axis=-1, keepdims=True) + EPS)
                xh = x * scale
                dg_ref[...] += jnp.sum(r * xh, axis=0, keepdims=True)
                dxh = r * hg_ref[...]
                r = scale * (dxh - xh * jnp.mean(dxh * xh, axis=-1, keepdims=True))
            if has_add:
                r = r + add_ref[...]
            o_ref[...] = r.astype(out_dtype)
            if has_norm:
                scale = lax.rsqrt(jnp.mean(r * r, axis=-1, keepdims=True) + EPS)
                n_ref[...] = (r * scale * g_ref[...]).astype(BF16)

        if nk == 1:
            finish(part)
            return
        acc_ref = refs[-1]
        kk = pl.program_id(2)

        @pl.when(kk == 0)
        def _():
            acc_ref[...] = part

        @pl.when((kk > 0) & (kk < nk - 1))
        def _():
            acc_ref[...] += part

        @pl.when(kk == nk - 1)
        def _():
            finish(acc_ref[...] + part)

    if mode == "tn":
        a_spec = pl.BlockSpec((tk, tm), lambda i, j, kk: (kk, i))
    else:
        a_spec = pl.BlockSpec((tm, tk), lambda i, j, kk: (i, kk))
    if mode == "nt":
        b_spec = pl.BlockSpec((tn, tk), lambda i, j, kk: (j, kk))
    else:
        b_spec = pl.BlockSpec((tk, tn), lambda i, j, kk: (kk, j))
    o_spec = pl.BlockSpec((tm, tn), lambda i, j, kk: (i, j))
    in_specs = [a_spec, b_spec] + ([o_spec] if has_add else [])
    args = (a, b) + ((add,) if has_add else ())
    out_specs, out_shape = o_spec, jax.ShapeDtypeStruct((m, n), out_dtype)
    if has_norm:
        in_specs.append(pl.BlockSpec((1, n), lambda i, j, kk: (0, 0)))
        args += (norm_gain.reshape(1, n),)
        out_specs, out_shape = [o_spec, o_spec], [out_shape, jax.ShapeDtypeStruct((m, n), BF16)]
    if has_nbwd:
        row_spec = pl.BlockSpec((1, n), lambda i, j, kk: (0, 0))
        in_specs += [o_spec, row_spec]
        args += (norm_bwd[0], norm_bwd[1].reshape(1, n))
        out_specs, out_shape = [o_spec, row_spec], [out_shape, jax.ShapeDtypeStruct((1, n), F32)]
    sem = ("arbitrary",) * 3 if has_nbwd else ("parallel", "parallel", "arbitrary")
    return pl.pallas_call(
        body, name=name, grid=(m // tm, n // tn, nk),
        in_specs=in_specs, out_specs=out_specs, out_shape=out_shape,
        scratch_shapes=[pltpu.VMEM((tm, tn), F32)] if nk > 1 else [], compiler_params=_params(sem),
    )(*args)


def _rb(tm, w, cb=0):
    return pl.BlockSpec((tm, w), lambda i: (i, cb))


def _fb(shape):
    nd = len(shape)
    return pl.BlockSpec(shape, lambda i: (0,) * nd)


def _rowcall(body, name, n_rows, tm, in_specs, args, out_specs, out_shapes):
    return pl.pallas_call(
        body, name=name, grid=(n_rows // tm,), in_specs=in_specs, out_specs=out_specs, out_shape=out_shapes,
        compiler_params=_params(("arbitrary",)),
    )(*args)


def _sds(shape, dtype=F32):
    return jax.ShapeDtypeStruct(shape, dtype)


def _rms_fwd(h, g, name, tm=1024):
    s, d = h.shape

    def body(h_ref, g_ref, o_ref):
        x = h_ref[...]
        r = lax.rsqrt(jnp.mean(x * x, axis=-1, keepdims=True) + EPS)
        o_ref[...] = (x * r * g_ref[...]).astype(BF16)

    return _rowcall(body, name, s, tm, [_rb(tm, d), _fb((1, d))], (h, g.reshape(1, d)), _rb(tm, d),
                    _sds((s, d), BF16))


def _loss_head(h, target, g, name, tm=1024):
    s, d = h.shape

    def body(h_ref, t_ref, g_ref, dh_ref, loss_ref, dg_ref):
        @pl.when(pl.program_id(0) == 0)
        def _():
            dg_ref[...] = jnp.zeros_like(dg_ref)
            loss_ref[...] = jnp.zeros_like(loss_ref)

        x = h_ref[...]
        r = lax.rsqrt(jnp.mean(x * x, axis=-1, keepdims=True) + EPS)
        xh = x * r
        gg = g_ref[...]
        err = xh * gg - t_ref[...]
        loss_ref[...] += jnp.sum(err * err) * (0.5 / d)
        dy = err * (1.0 / d)
        dg_ref[...] += jnp.sum(dy * xh, axis=0, keepdims=True)
        dxh = dy * gg
        dh_ref[...] = r * (dxh - xh * jnp.mean(dxh * xh, axis=-1, keepdims=True))

    dh, loss, dg = _rowcall(body, name, s, tm, [_rb(tm, d), _rb(tm, d), _fb((1, d))], (h, target, g.reshape(1, d)),
                            [_rb(tm, d), _fb((1, 128)), _fb((1, d))], [_sds((s, d)), _sds((1, 128)), _sds((1, d))])
    return dh, loss[0, 0], dg.reshape(d)


_MIX_PARTS = ((0, 512), (512, 768), (768, 1024))


def _mix_fwd(ya, ysg, yss, g, w_out, h, g_next, name, tm=1024):
    s = ya.shape[0]

    def body(a_ref, b_ref, c_ref, g_ref, w_ref, h_ref, n_ref, mix_ref, h2_ref, a2_ref):
        for ref, (lo, hi) in zip((a_ref, b_ref, c_ref), _MIX_PARTS):
            y = ref[...]
            r = lax.rsqrt(jnp.mean(y * y, axis=-1, keepdims=True) + EPS)
            mix_ref[:, lo:hi] = (y * r * g_ref[:, lo:hi]).astype(BF16)
        h2 = h_ref[...] + jnp.dot(mix_ref[...], w_ref[...], preferred_element_type=F32)
        h2_ref[...] = h2
        scale = lax.rsqrt(jnp.mean(h2 * h2, axis=-1, keepdims=True) + EPS)
        a2_ref[...] = (h2 * scale * n_ref[...]).astype(BF16)

    return _rowcall(
        body, name, s, tm,
        [_rb(tm, 512), _rb(tm, 256), _rb(tm, 256), _fb((1, 1024)), _fb((1024, 1024)), _rb(tm, 1024), _fb((1, 1024))],
        (ya, ysg, yss, g.reshape(1, 1024), w_out, h, g_next.reshape(1, 1024)),
        [_rb(tm, 1024)] * 3, [_sds((s, 1024), BF16), _sds((s, 1024)), _sds((s, 1024), BF16)])


def _mix_bwd(dh, w_out, ya, ysg, yss, g, name, tm=1024):
    s = ya.shape[0]

    def body(dh_ref, w_ref, a_ref, b_ref, c_ref, g_ref, da_ref, db_ref, dc_ref, dg_ref):
        @pl.when(pl.program_id(0) == 0)
        def _():
            dg_ref[...] = jnp.zeros_like(dg_ref)

        dmix = lax.dot_general(dh_ref[...].astype(BF16), w_ref[...], _DIMS["nt"], preferred_element_type=F32)
        for ref, dref, (lo, hi) in zip((a_ref, b_ref, c_ref), (da_ref, db_ref, dc_ref), _MIX_PARTS):
            y = ref[...]
            r = lax.rsqrt(jnp.mean(y * y, axis=-1, keepdims=True) + EPS)
            xh = y * r
            dm = dmix[:, lo:hi]
            dg_ref[:, lo:hi] += jnp.sum(dm * xh, axis=0, keepdims=True)
            dxh = dm * g_ref[:, lo:hi]
            dref[...] = r * (dxh - xh * jnp.mean(dxh * xh, axis=-1, keepdims=True))

    da, db, dc, dg = _rowcall(
        body, name, s, tm,
        [_rb(tm, 1024), _fb((1024, 1024)), _rb(tm, 512), _rb(tm, 256), _rb(tm, 256), _fb((1, 1024))],
        (dh, w_out, ya, ysg, yss, g.reshape(1, 1024)),
        [_rb(tm, 512), _rb(tm, 256), _rb(tm, 256), _fb((1, 1024))],
        [_sds((s, 512)), _sds((s, 256)), _sds((s, 256)), _sds((1, 1024))])
    return da, db, dc, dg.reshape(1024)


def _ssm_post_fwd(yc, z, d, gw, gb, name, tm=1024):
    s = yc.shape[0]

    def body(yc_ref, u_ref, d_ref, gw_ref, gb_ref, o_ref):
        y1 = yc_ref[...] + d_ref[...] * u_ref[...]
        y2 = _gelu(y1)
        gl = jnp.dot(y2.astype(BF16), gw_ref[...], preferred_element_type=F32) + gb_ref[...]
        o_ref[...] = y2 * _sigmoid(gl)

    return _rowcall(body, name, s, tm, [_rb(tm, 256), _rb(tm, 256, 8), _fb((1, 256)), _fb((256, 256)), _fb((1, 256))],
                    (yc, z, d.reshape(1, 256), gw, gb.reshape(1, 256)), _rb(tm, 256), _sds((s, 256)))


def _ssm_post_bwd(dy, yc, z, d, gw, gb, name, tm=1024):
    s = yc.shape[0]

    def body(dy_ref, yc_ref, u_ref, d_ref, gw_ref, gb_ref, dy1_ref, dgl_ref, y2_ref, dud_ref, dd_ref, dgb_ref):
        @pl.when(pl.program_id(0) == 0)
        def _():
            dd_ref[...] = jnp.zeros_like(dd_ref)
            dgb_ref[...] = jnp.zeros_like(dgb_ref)

        u = u_ref[...]
        dd = d_ref[...]
        y1 = yc_ref[...] + dd * u
        y2 = _gelu(y1)
        gw_v = gw_ref[...]
        gl = jnp.dot(y2.astype(BF16), gw_v, preferred_element_type=F32) + gb_ref[...]
        sg = _sigmoid(gl)
        dyv = dy_ref[...]
        dgl = dyv * y2 * sg * (1.0 - sg)
        dy2 = dyv * sg + lax.dot_general(dgl.astype(BF16), gw_v, _DIMS["nt"], preferred_element_type=F32)
        dy1 = dy2 * _gelu_grad(y1)
        dy1_ref[...] = dy1.astype(BF16)
        dgl_ref[...] = dgl.astype(BF16)
        y2_ref[...] = y2.astype(BF16)
        dud_ref[...] = dy1 * dd
        dd_ref[...] += jnp.sum(dy1 * u, axis=0, keepdims=True)
        dgb_ref[...] += jnp.sum(dgl, axis=0, keepdims=True)

    outs = _rowcall(
        body, name, s, tm,
        [_rb(tm, 256), _rb(tm, 256), _rb(tm, 256, 8), _fb((1, 256)), _fb((256, 256)), _fb((1, 256))],
        (dy, yc, z, d.reshape(1, 256), gw, gb.reshape(1, 256)),
        [_rb(tm, 256)] * 4 + [_fb((1, 256))] * 2,
        [_sds((s, 256), BF16)] * 3 + [_sds((s, 256))] + [_sds((1, 256))] * 2)
    dy1, dgl, y2, dud, dd, dgb = outs
    return dy1, dgl, y2, dud, dd.reshape(256), dgb.reshape(256)


SCAN_T = 512
N_SCAN_TABLES = 6


def _scan_tables(pr, pi, reverse):
    ns = pr.shape[0]
    sign = -1.0 if reverse else 1.0
    power = [(jnp.ones((ns,), F32), jnp.zeros((ns,), F32))] + [(pr[:, k], sign * pi[:, k]) for k in range(8)]
    zero = (jnp.zeros((ns,), F32), jnp.zeros((ns,), F32))

    def table(exponents):
        rows = [zero if e is None else power[e] for e in exponents]
        return jnp.stack([jnp.concatenate(row) for row in rows])

    tabs = []
    for k in (1, 2, 4):
        has_partner = [(s < 8 - k) if reverse else (s >= k) for s in range(8)]
        tabs.append(table([k if ok else None for ok in has_partner]))
    tabs.append(table([s if reverse else 7 - s for s in range(8)]))
    tabs.append(table([8 - s if reverse else s + 1 for s in range(8)]))
    tabs.append(table([8] * 8))
    return jnp.stack(tabs)


def _cmul(ar, ai, br, bi):
    return ar * br - ai * bi, ar * bi + ai * br


def _scan_group(ur, ui, cr, ci, tr_ref, ti_ref, reverse):
    xr, xi = ur, ui
    for n, k in enumerate((1, 2, 4)):
        shift = 8 - k if reverse else k
        pr, pi = _cmul(tr_ref[n], ti_ref[n], pltpu.roll(xr, shift, axis=0), pltpu.roll(xi, shift, axis=0))
        xr, xi = xr + pr, xi + pi
    sr, si = _cmul(tr_ref[3], ti_ref[3], ur, ui)
    for k in (1, 2, 4):
        sr, si = sr + pltpu.roll(sr, k, axis=0), si + pltpu.roll(si, k, axis=0)
    pr, pi = _cmul(tr_ref[4], ti_ref[4], cr, ci)
    nr, ni = _cmul(tr_ref[5], ti_ref[5], cr, ci)
    return xr + pr, xi + pi, nr + sr, ni + si


def _table_halves(t_ref):
    return t_ref.at[:, :, pl.ds(0, SSM_NS)], t_ref.at[:, :, pl.ds(SSM_NS, SSM_NS)]


_U_BLOCK = (IN_W - SSM_W) // SSM_W


def _ssm_fwd(z, bdt, cd, tabs, name):
    s = z.shape[0]
    ns = SSM_NS
    n_t = s // SCAN_T

    def body(u_ref, b_ref, c_ref, t_ref, xr_ref, xi_ref, y_ref, cr_ref, ci_ref, ur_ref, ui_ref):
        @pl.when(pl.program_id(0) == 0)
        def _():
            cr_ref[...] = jnp.zeros_like(cr_ref)
            ci_ref[...] = jnp.zeros_like(ci_ref)

        bu = lax.dot_general(u_ref[...].astype(BF16), b_ref[...], _DIMS["nt"], preferred_element_type=F32)
        ur_ref[...] = bu[:, :ns]
        ui_ref[...] = bu[:, ns:]
        tr_ref, ti_ref = _table_halves(t_ref)

        def group(g, carry):
            rows = pl.ds(pl.multiple_of(g * 8, 8), 8)
            xr, xi, cr, ci = _scan_group(ur_ref[rows, :], ui_ref[rows, :], *carry, tr_ref, ti_ref, False)
            xr_ref[rows, :] = xr
            xi_ref[rows, :] = xi
            return cr, ci

        cr, ci = lax.fori_loop(0, SCAN_T // 8, group, (cr_ref[...], ci_ref[...]), unroll=2)
        cr_ref[...] = cr
        ci_ref[...] = ci
        y_ref[...] = (jnp.dot(xr_ref[...].astype(BF16), c_ref[0:ns, :], preferred_element_type=F32)
                      + jnp.dot(xi_ref[...].astype(BF16), c_ref[ns:, :], preferred_element_type=F32))

    x_spec = pl.BlockSpec((SCAN_T, ns), lambda t: (t, 0))
    return pl.pallas_call(
        body, name=name, grid=(n_t,),
        in_specs=[pl.BlockSpec((SCAN_T, SSM_W), lambda t: (t, _U_BLOCK)), _fb((2 * ns, SSM_W)),
                  _fb((2 * ns, SSM_W)), _fb((N_SCAN_TABLES, 8, 2 * ns))],
        out_specs=[x_spec, x_spec, _rb(SCAN_T, SSM_W)],
        out_shape=[_sds((s, ns)), _sds((s, ns)), _sds((s, SSM_W))],
        scratch_shapes=[pltpu.VMEM((8, ns), F32)] * 2 + [pltpu.VMEM((SCAN_T, ns), F32)] * 2,
        compiler_params=_params(("arbitrary",)),
    )(z, bdt, cd, tabs)


def _ssm_bwd(dy1, dud, z, xr, xi, bdt, cd, tabs, name):
    s = z.shape[0]
    ns = SSM_NS
    n_t = s // SCAN_T
    n_groups = SCAN_T // 8

    def body(dy_ref, dud_ref, u_ref, xr_ref, xi_ref, pxr_ref, pxi_ref, b_ref, c_ref, t_ref,
             du_ref, dbd_ref, dcd_ref, dar_ref, dai_ref,
             cr_ref, ci_ref, ar_ref, ai_ref, sxr_ref, sxi_ref, gr_ref, gi_ref, lr_ref, li_ref, bacc_ref, cacc_ref):
        t = pl.program_id(0)

        @pl.when(t == 0)
        def _():
            for ref in (cr_ref, ci_ref, ar_ref, ai_ref, bacc_ref, cacc_ref):
                ref[...] = jnp.zeros_like(ref)

        dyb = dy_ref[...]
        g = lax.dot_general(dyb, c_ref[...], _DIMS["nt"], preferred_element_type=F32)
        gr_ref[...] = g[:, :ns]
        gi_ref[...] = g[:, ns:]
        has_before = (t < n_t - 1).astype(F32)
        sxr_ref[0:8, :] = pxr_ref[...] * has_before
        sxi_ref[0:8, :] = pxi_ref[...] * has_before
        sxr_ref[8:, :] = xr_ref[...]
        sxi_ref[8:, :] = xi_ref[...]
        first_row = lax.broadcasted_iota(jnp.int32, (8, ns), 0) == 0
        tr_ref, ti_ref = _table_halves(t_ref)

        def group(k, carry):
            cr, ci, ar, ai = carry
            g8 = pl.multiple_of((n_groups - 1 - k) * 8, 8)
            rows = pl.ds(g8, 8)
            lr, li, cr, ci = _scan_group(gr_ref[rows, :], gi_ref[rows, :], cr, ci, tr_ref, ti_ref, True)
            lr_ref[rows, :] = lr
            li_ref[rows, :] = li
            here, before = pl.ds(g8 + 8, 8), rows
            pr = jnp.where(first_row, pltpu.roll(sxr_ref[before, :], 1, axis=0), pltpu.roll(sxr_ref[here, :], 1, axis=0))
            pi = jnp.where(first_row, pltpu.roll(sxi_ref[before, :], 1, axis=0), pltpu.roll(sxi_ref[here, :], 1, axis=0))
            return cr, ci, ar + lr * pr + li * pi, ai + li * pr - lr * pi

        cr, ci, ar, ai = lax.fori_loop(0, n_groups, group,
                                       (cr_ref[...], ci_ref[...], ar_ref[...], ai_ref[...]), unroll=2)
        cr_ref[...] = cr
        ci_ref[...] = ci
        ar_ref[...] = ar
        ai_ref[...] = ai
        lrb = lr_ref[...].astype(BF16)
        lib = li_ref[...].astype(BF16)
        ub = u_ref[...].astype(BF16)
        du_ref[...] = (dud_ref[...] + jnp.dot(lrb, b_ref[0:ns, :], preferred_element_type=F32)
                       + jnp.dot(lib, b_ref[ns:, :], preferred_element_type=F32))
        bacc_ref[0:ns, :] += lax.dot_general(lrb, ub, _DIMS["tn"], preferred_element_type=F32)
        bacc_ref[ns:, :] += lax.dot_general(lib, ub, _DIMS["tn"], preferred_element_type=F32)
        cacc_ref[0:ns, :] += lax.dot_general(xr_ref[...].astype(BF16), dyb, _DIMS["tn"], preferred_element_type=F32)
        cacc_ref[ns:, :] += lax.dot_general(xi_ref[...].astype(BF16), dyb, _DIMS["tn"], preferred_element_type=F32)

        @pl.when(t == n_t - 1)
        def _():
            for k in (1, 2, 4):
                ar_ref[...] += pltpu.roll(ar_ref[...], k, axis=0)
                ai_ref[...] += pltpu.roll(ai_ref[...], k, axis=0)
            dar_ref[...] = ar_ref[...]
            dai_ref[...] = ai_ref[...]
            dbd_ref[...] = bacc_ref[...]
            dcd_ref[...] = cacc_ref[...]

    rev = lambda t: n_t - 1 - t
    row_spec = pl.BlockSpec((SCAN_T, SSM_W), lambda t: (rev(t), 0))
    x_spec = pl.BlockSpec((SCAN_T, ns), lambda t: (rev(t), 0))
    before_spec = pl.BlockSpec((8, ns), lambda t: (jnp.maximum(rev(t) * (SCAN_T // 8) - 1, 0), 0))
    du, dbd, dcd, dar, dai = pl.pallas_call(
        body, name=name, grid=(n_t,),
        in_specs=[row_spec, row_spec, pl.BlockSpec((SCAN_T, SSM_W), lambda t: (rev(t), _U_BLOCK)),
                  x_spec, x_spec, before_spec, before_spec,
                  _fb((2 * ns, SSM_W)), _fb((2 * ns, SSM_W)), _fb((N_SCAN_TABLES, 8, 2 * ns))],
        out_specs=[row_spec, _fb((2 * ns, SSM_W)), _fb((2 * ns, SSM_W)), _fb((8, ns)), _fb((8, ns))],
        out_shape=[_sds((s, SSM_W)), _sds((2 * ns, SSM_W)), _sds((2 * ns, SSM_W)), _sds((8, ns)), _sds((8, ns))],
        scratch_shapes=([pltpu.VMEM((8, ns), F32)] * 4 + [pltpu.VMEM((SCAN_T + 8, ns), F32)] * 2
                        + [pltpu.VMEM((SCAN_T, ns), F32)] * 4 + [pltpu.VMEM((2 * ns, SSM_W), F32)] * 2),
        compiler_params=_params(("arbitrary",)),
    )(dy1, dud, z, xr, xi, xr, xi, bdt, cd, tabs)
    return du, dbd, dcd, dar[0], dai[0]


def _group_ids():
    return lax.broadcasted_iota(jnp.int32, (1, SGU_W), 1) // 64


def _group_mean(val, gid):
    out = jnp.zeros_like(val)
    for g in range(SGU_GROUPS):
        mg = gid == g
        out = jnp.where(mg, jnp.sum(jnp.where(mg, val, 0.0), axis=1, keepdims=True) * (1.0 / 64), out)
    return out


def _causal_w(w_ref, g):
    t = lax.broadcasted_iota(jnp.int32, (SGU_CHUNK, SGU_CHUNK), 0)
    s = lax.broadcasted_iota(jnp.int32, (SGU_CHUNK, SGU_CHUNK), 1)
    return jnp.where(t >= s, w_ref[g], 0.0).astype(BF16)


def _sgu_core(x, lng, lnb, w_ref, bexp, gid):
    zz = _gelu(x)
    u = zz[:, :SGU_W]
    v = zz[:, SGU_W:]
    vc = v - _group_mean(v, gid)
    rstd = lax.rsqrt(_group_mean(vc * vc, gid) + EPS)
    vhat = vc * rstd
    vn = vhat * lng + lnb
    vnb = vn.astype(BF16)
    mixed = bexp
    for g in range(SGU_GROUPS):
        mm = jnp.dot(_causal_w(w_ref, g), vnb, preferred_element_type=F32)
        mixed = jnp.where(gid == g, mm + bexp, mixed)
    return u, rstd, vhat, vnb, mixed


def _sgu_fwd(z, lng, lnb, w, bexp, name, tm=1024):
    s = z.shape[0]

    def body(z_ref, lng_ref, lnb_ref, w_ref, b_ref, o_ref):
        gid = _group_ids()
        for j in range(tm // SGU_CHUNK):
            rows = pl.ds(j * SGU_CHUNK, SGU_CHUNK)
            u, _, _, _, mixed = _sgu_core(z_ref[rows, :], lng_ref[...], lnb_ref[...], w_ref, b_ref[...], gid)
            o_ref[rows, :] = u * mixed

    return _rowcall(body, name, s, tm,
                    [_rb(tm, 512, 3), _fb((1, 256)), _fb((1, 256)), _fb((4, 128, 128)), _fb((128, 256))],
                    (z, lng.reshape(1, 256), lnb.reshape(1, 256), w, bexp), _rb(tm, 256), _sds((s, 256)))


def _sgu_bwd(z, dy, lng, lnb, w, bexp, name, tm=1024):
    s = z.shape[0]

    def body(z_ref, dy_ref, lng_ref, lnb_ref, w_ref, b_ref, dz_ref, dw_ref, db_ref, dlng_ref, dlnb_ref):
        @pl.when(pl.program_id(0) == 0)
        def _():
            dw_ref[...] = jnp.zeros_like(dw_ref)
            db_ref[...] = jnp.zeros_like(db_ref)
            dlng_ref[...] = jnp.zeros_like(dlng_ref)
            dlnb_ref[...] = jnp.zeros_like(dlnb_ref)

        gid = _group_ids()
        t = lax.broadcasted_iota(jnp.int32, (SGU_CHUNK, SGU_CHUNK), 0)
        sidx = lax.broadcasted_iota(jnp.int32, (SGU_CHUNK, SGU_CHUNK), 1)
        lng_v = lng_ref[...]
        for j in range(tm // SGU_CHUNK):
            rows = pl.ds(j * SGU_CHUNK, SGU_CHUNK)
            x = z_ref[rows, :]
            u, rstd, vhat, vnb, mixed = _sgu_core(x, lng_v, lnb_ref[...], w_ref, b_ref[...], gid)
            dyv = dy_ref[rows, :]
            dmixed = dyv * u
            du = dyv * mixed
            db_ref[...] += dmixed
            dvn = jnp.zeros_like(dmixed)
            for g in range(SGU_GROUPS):
                dmg = jnp.where(gid == g, dmixed, 0.0).astype(BF16)
                dvn = dvn + lax.dot_general(_causal_w(w_ref, g), dmg, _DIMS["tn"], preferred_element_type=F32)
                dwg = lax.dot_general(dmg, vnb, _DIMS["nt"], preferred_element_type=F32)
                dw_ref[g] += jnp.where(t >= sidx, dwg, 0.0)
            dlnb_ref[...] += jnp.sum(dvn, axis=0, keepdims=True)
            dlng_ref[...] += jnp.sum(dvn * vhat, axis=0, keepdims=True)
            dvh = dvn * lng_v
            dv = rstd * (dvh - _group_mean(dvh, gid) - vhat * _group_mean(dvh * vhat, gid))
            gg = _gelu_grad(x)
            dz_ref[rows, 0:SGU_W] = du * gg[:, :SGU_W]
            dz_ref[rows, SGU_W:2 * SGU_W] = dv * gg[:, SGU_W:]

    dz, dw, db, dlng, dlnb = _rowcall(
        body, name, s, tm,
        [_rb(tm, 512, 3), _rb(tm, 256), _fb((1, 256)), _fb((1, 256)), _fb((4, 128, 128)), _fb((128, 256))],
        (z, dy, lng.reshape(1, 256), lnb.reshape(1, 256), w, bexp),
        [_rb(tm, 512), _fb((4, 128, 128)), _fb((128, 256)), _fb((1, 256)), _fb((1, 256))],
        [_sds((s, 512)), _sds((4, 128, 128)), _sds((128, 256)), _sds((1, 256)), _sds((1, 256))])
    return dz, dw, db, dlng.reshape(256), dlnb.reshape(256)


CONV_TC = 1408
N_CT = D_FF // CONV_TC


def _row_of(block8, j):
    r = lax.broadcasted_iota(jnp.int32, block8.shape, 0)
    return jnp.sum(jnp.where(r == j, block8, 0.0), axis=0, keepdims=True)


EDGE = 16


def _conv_fwd(hu, cw, cb, name, tm=512):
    s = hu.shape[0]

    def body(xv_ref, xg_ref, tv_ref, tg_ref, wv_ref, wg_ref, bv_ref, bg_ref, hv_ref, hg_ref, act_ref):
        has_prev = (pl.program_id(1) > 0).astype(F32)
        row = lax.broadcasted_iota(jnp.int32, (EDGE, CONV_TC), 0)

        def conv(x_ref, t_ref, w_ref, b_ref):
            x = x_ref[...].astype(F32)
            w0, w1, w2, bb = w_ref[0:1, :], w_ref[1:2, :], w_ref[2:3, :], b_ref[...]
            whole = w0 * pltpu.roll(x, 2, axis=0) + w1 * pltpu.roll(x, 1, axis=0) + w2 * x + bb
            tail = t_ref[...].astype(F32)
            r7 = _row_of(tail, EDGE - 1) * has_prev
            r6 = _row_of(tail, EDGE - 2) * has_prev
            xe = x_ref[0:EDGE, :].astype(F32)
            x1 = jnp.where(row == 0, r7, pltpu.roll(xe, 1, axis=0))
            x2 = jnp.where(row == 0, r6, jnp.where(row == 1, r7, pltpu.roll(xe, 2, axis=0)))
            return whole, w0 * x2 + w1 * x1 + w2 * xe + bb

        hv, hv_edge = conv(xv_ref, tv_ref, wv_ref, bv_ref)
        hg, hg_edge = conv(xg_ref, tg_ref, wg_ref, bg_ref)
        hv_ref[...] = hv.astype(BF16)
        hg_ref[...] = hg.astype(BF16)
        act_ref[...] = (_gelu(hg) * hv).astype(BF16)
        hv_ref[0:EDGE, :] = hv_edge.astype(BF16)
        hg_ref[0:EDGE, :] = hg_edge.astype(BF16)
        act_ref[0:EDGE, :] = (_gelu(hg_edge) * hv_edge).astype(BF16)

    def xs(off):
        return pl.BlockSpec((tm, CONV_TC), lambda j, i: (i, j + off))

    def ts(off):
        return pl.BlockSpec((EDGE, CONV_TC), lambda j, i: (jnp.maximum(i * (tm // EDGE) - 1, 0), j + off))

    def ws(rows, off):
        return pl.BlockSpec((rows, CONV_TC), lambda j, i: (0, j + off))

    o_spec = pl.BlockSpec((tm, CONV_TC), lambda j, i: (i, j))
    return pl.pallas_call(
        body, name=name, grid=(N_CT, s // tm),
        in_specs=[xs(0), xs(N_CT), ts(0), ts(N_CT), ws(3, 0), ws(3, N_CT), ws(1, 0), ws(1, N_CT)],
        out_specs=[o_spec] * 3, out_shape=[_sds((s, D_FF), BF16)] * 3,
        compiler_params=_params(("parallel", "arbitrary")),
    )(hu, hu, hu, hu, cw, cw, cb.reshape(1, 2 * D_FF), cb.reshape(1, 2 * D_FF))


HALO = EDGE


def _conv_bwd(dact, hv, hg, hu, cw, name, tm=512):
    s = dact.shape[0]

    def body(da_ref, dan_ref, hv_ref, hvn_ref, hg_ref, hgn_ref, x_ref, t_ref, w_ref, dx_ref, dw_ref, db_ref, d_scr):
        i = pl.program_id(1)
        is_value = pl.program_id(0) < N_CT

        @pl.when(i == 0)
        def _():
            dw_ref[...] = jnp.zeros_like(dw_ref)
            db_ref[...] = jnp.zeros_like(db_ref)

        for rows, (a_ref, v_ref, g_ref) in ((pl.ds(0, tm), (da_ref, hv_ref, hg_ref)),
                                            (pl.ds(tm, HALO), (dan_ref, hvn_ref, hgn_ref))):
            @pl.when(is_value)
            def _():
                d_scr[rows, :] = a_ref[...].astype(F32) * _gelu(g_ref[...].astype(F32))

            @pl.when(jnp.logical_not(is_value))
            def _():
                d_scr[rows, :] = (a_ref[...].astype(F32) * v_ref[...].astype(F32)
                                  * _gelu_grad(g_ref[...].astype(F32)))

        has_prev = (i > 0).astype(F32)
        has_next = (i < s // tm - 1).astype(F32)
        w0, w1, w2 = w_ref[0:1, :], w_ref[1:2, :], w_ref[2:3, :]
        d = d_scr[0:tm, :]
        dx_ref[...] = (w2 * d + w1 * pltpu.roll(d, tm - 1, axis=0) + w0 * pltpu.roll(d, tm - 2, axis=0)).astype(BF16)
        row = lax.broadcasted_iota(jnp.int32, (EDGE, CONV_TC), 0)
        nxt = d_scr[tm:tm + HALO, :]
        n0 = _row_of(nxt, 0) * has_next
        n1 = _row_of(nxt, 1) * has_next
        de = d_scr[tm - EDGE:tm, :]
        d1 = jnp.where(row == EDGE - 1, n0, pltpu.roll(de, EDGE - 1, axis=0))
        d2 = jnp.where(row == EDGE - 2, n0, jnp.where(row == EDGE - 1, n1, pltpu.roll(de, EDGE - 2, axis=0)))
        dx_ref[tm - EDGE:tm, :] = (w2 * de + w1 * d1 + w0 * d2).astype(BF16)
        x = x_ref[...].astype(F32)
        tail = t_ref[...].astype(F32)
        r7 = _row_of(tail, EDGE - 1) * has_prev
        r6 = _row_of(tail, EDGE - 2) * has_prev
        last = x_ref[tm - EDGE:tm, :].astype(F32)
        l7, l6 = _row_of(last, EDGE - 1), _row_of(last, EDGE - 2)
        head = d_scr[0:8, :]
        d0, d1h = _row_of(head, 0), _row_of(head, 1)
        dw_ref[0:1, :] += (jnp.sum(d * pltpu.roll(x, 2, axis=0), axis=0, keepdims=True)
                           + d0 * (r6 - l6) + d1h * (r7 - l7))
        dw_ref[1:2, :] += jnp.sum(d * pltpu.roll(x, 1, axis=0), axis=0, keepdims=True) + d0 * (r7 - l7)
        dw_ref[2:3, :] += jnp.sum(d * x, axis=0, keepdims=True)
        db_ref[...] += jnp.sum(d, axis=0, keepdims=True)

    a_spec = pl.BlockSpec((tm, CONV_TC), lambda j, i: (i, j % N_CT))
    an_spec = pl.BlockSpec((HALO, CONV_TC),
                           lambda j, i: (jnp.minimum((i + 1) * (tm // HALO), s // HALO - 1), j % N_CT))
    x_spec = pl.BlockSpec((tm, CONV_TC), lambda j, i: (i, j))
    t_spec = pl.BlockSpec((EDGE, CONV_TC), lambda j, i: (jnp.maximum(i * (tm // EDGE) - 1, 0), j))
    w_spec = pl.BlockSpec((3, CONV_TC), lambda j, i: (0, j))
    db_spec = pl.BlockSpec((1, CONV_TC), lambda j, i: (0, j))
    return pl.pallas_call(
        body, name=name, grid=(2 * N_CT, s // tm),
        in_specs=[a_spec, an_spec, a_spec, an_spec, a_spec, an_spec, x_spec, t_spec, w_spec],
        out_specs=[x_spec, w_spec, db_spec],
        out_shape=[_sds((s, 2 * D_FF), BF16), _sds((3, 2 * D_FF)), _sds((1, 2 * D_FF))],
        scratch_shapes=[pltpu.VMEM((tm + HALO, CONV_TC), F32)],
        compiler_params=_params(("parallel", "arbitrary")),
    )(dact, dact, hv, hv, hg, hg, hu, hu, cw)


def _ple_fwd(h, gp, pp, next_gain, name, tm=1024):
    s, d = h.shape
    with_norm = next_gain is not None

    def body(*refs):
        h_ref, g_ref, p_ref = refs[:3]
        out = h_ref[...] + _sigmoid(g_ref[...].astype(F32)) * p_ref[...].astype(F32)
        if with_norm:
            n_ref, o_ref, a_ref = refs[3:]
            scale = lax.rsqrt(jnp.mean(out * out, axis=-1, keepdims=True) + EPS)
            a_ref[...] = (out * scale * n_ref[...]).astype(BF16)
        else:
            o_ref, = refs[3:]
        o_ref[...] = out

    if not with_norm:
        return _rowcall(body, name, s, tm, [_rb(tm, d)] * 3, (h, gp, pp), _rb(tm, d), _sds((s, d))), None
    return _rowcall(body, name, s, tm, [_rb(tm, d)] * 3 + [_fb((1, d))], (h, gp, pp, next_gain.reshape(1, d)),
                    [_rb(tm, d)] * 2, [_sds((s, d)), _sds((s, d), BF16)])


def _ple_bwd(dh, gp, pp, name, tm=1024):
    s, d = dh.shape

    def body(d_ref, g_ref, p_ref, dp_ref, dg_ref):
        sg = _sigmoid(g_ref[...].astype(F32))
        dv = d_ref[...]
        dp_ref[...] = (dv * sg).astype(BF16)
        dg_ref[...] = (dv * p_ref[...].astype(F32) * sg * (1.0 - sg)).astype(BF16)

    return _rowcall(body, name, s, tm, [_rb(tm, d)] * 3, (dh, gp, pp), [_rb(tm, d)] * 2,
                    [_sds((s, d), BF16)] * 2)


SCALE = HEAD_DIM ** -0.5
ATT_ROWS = 2048


def _att_geom(s, dil):
    w = min(ATT_ROWS, s)
    p = BLK * dil
    assert w % p == 0 and s % w == 0
    return w, p, w // p


def _rows(start, dil):
    return pl.ds(start, BLK, stride=dil) if dil > 1 else pl.ds(start, BLK)


def _head_masks():
    lane = lax.broadcasted_iota(jnp.int32, (1, BLK), 1)
    return [lane < HEAD_DIM, lane >= HEAD_DIM]


def _band():
    rel = np.arange(BLK)[:, None] + BLK - np.arange(2 * BLK)[None, :]
    return (rel >= 0) & (rel <= BLK)


def _zcur(w):
    return lambda off: pl.BlockSpec((w, BLK), lambda hp, i: (i, off + hp))


def _zprev(p, nb):
    return lambda off: pl.BlockSpec((p, BLK), lambda hp, i: (jnp.maximum(i * nb - 1, 0), off + hp))


def _scur(w):
    return pl.BlockSpec((w, BLK), lambda hp, i: (i, hp))


def _pair_rows(t, masks):
    return jnp.concatenate([jnp.where(masks[0], t, 0.0), jnp.where(masks[1], t, 0.0)], axis=0).astype(BF16)


def _pair_bias_bwd(bias):
    return bias.reshape(4, 2, BLK, 2, BLK).transpose(0, 3, 2, 1, 4).reshape(4, 2, BLK, 2 * BLK)


def _unpair_bias_bwd(db):
    return db.reshape(4, 2, BLK, 2, BLK).transpose(0, 3, 2, 1, 4).reshape(N_HEADS, BLK, 2 * BLK)


def _attn_fwd(z, biases, name):
    s = z.shape[0]
    w = min(ATT_ROWS, s)
    n_br = len(BRANCHES)

    def body(*refs):
        q_ref, kp_ref, kc_ref, vp_ref, vc_ref = refs[:5]
        b_refs = refs[5:5 + n_br]
        y_ref, lse_ref, m_ref, l_ref, a_ref = refs[5 + n_br:]
        i = pl.program_id(1)
        masks = _head_masks()
        own_block = lax.broadcasted_iota(jnp.int32, (1, 2 * BLK), 1) >= BLK
        for n, (_, dil) in enumerate(BRANCHES):
            _, p, nb = _att_geom(s, dil)
            for r in range(dil):
                for b in range(nb):
                    rows = _rows(r + p * b, dil)
                    prev_rows = _rows(r + p * (b - 1), dil) if b > 0 else _rows(w - p + r, dil)
                    kprev, vprev = (kc_ref, vc_ref) if b > 0 else (kp_ref, vp_ref)
                    q = q_ref[rows, :] * SCALE
                    k = jnp.concatenate([kprev[prev_rows, :], kc_ref[rows, :]], axis=0).astype(BF16)
                    v = jnp.concatenate([vprev[prev_rows, :], vc_ref[rows, :]], axis=0).astype(BF16)
                    mb = lb = ob = None
                    for hh, mh in enumerate(masks):
                        qh = jnp.where(mh, q, 0.0).astype(BF16)
                        sc = lax.dot_general(qh, k, _DIMS["nt"], preferred_element_type=F32) + b_refs[n][hh]
                        if b == 0:
                            sc = jnp.where(own_block | (i > 0), sc, NEG_INF)
                        mx = jnp.max(sc, axis=1, keepdims=True)
                        e = jnp.exp(sc - mx)
                        den = jnp.sum(e, axis=1, keepdims=True)
                        o = jnp.dot(e.astype(BF16), v, preferred_element_type=F32)
                        if hh == 0:
                            mb = jnp.broadcast_to(mx, (BLK, BLK))
                            lb = jnp.broadcast_to(den, (BLK, BLK))
                            ob = o
                        else:
                            mb = jnp.where(mh, mx, mb)
                            lb = jnp.where(mh, den, lb)
                            ob = jnp.where(mh, o, ob)
                    if n == 0:
                        m_new, l_new, a_new = mb, lb, ob
                    else:
                        m_old = m_ref[rows, :]
                        m_new = jnp.maximum(m_old, mb)
                        al = jnp.exp(m_old - m_new)
                        be = jnp.exp(mb - m_new)
                        l_new = al * l_ref[rows, :] + be * lb
                        a_new = al * a_ref[rows, :] + be * ob
                    if n == n_br - 1:
                        y_ref[rows, :] = a_new / l_new
                        lse_ref[rows, :] = m_new + jnp.log(l_new)
                    else:
                        m_ref[rows, :] = m_new
                        l_ref[rows, :] = l_new
                        a_ref[rows, :] = a_new

    cur, prv = _zcur(w), _zprev(w, 1)
    b_spec = pl.BlockSpec((2, BLK, 2 * BLK), lambda hp, i: (hp, 0, 0))
    return pl.pallas_call(
        body, name=name, grid=(4, s // w), in_specs=[cur(0), prv(4), cur(4), prv(8), cur(8)] + [b_spec] * n_br,
        out_specs=[_scur(w)] * 2, out_shape=[_sds((s, ATTN_W))] * 2,
        scratch_shapes=[pltpu.VMEM((w, BLK), F32)] * 3,
        compiler_params=_params(("parallel", "parallel")),
    )(z, z, z, z, z, *biases)


def _row_stats(mh, dy, y, lse):
    delta = jnp.sum(jnp.where(mh, dy * y, 0.0), axis=1, keepdims=True)
    lse_h = jnp.max(jnp.where(mh, lse, NEG_INF), axis=1, keepdims=True)
    return delta, lse_h


def _attn_bwd(z, biases, dy, y, lse, name):
    s = z.shape[0]
    w = min(ATT_ROWS, s)
    n_steps = s // w
    n_br = len(BRANCHES)

    def body(*refs):
        q_ref, kp_ref, kc_ref, vp_ref, vc_ref, dy_ref, y_ref, lse_ref = refs[:8]
        b_refs = refs[8:8 + n_br]
        outs = refs[8 + n_br:]
        dq_ref, dk_ref, dv_ref = outs[:3]
        x_refs = outs[3:3 + 2 * n_br]
        db_refs = outs[3 + 2 * n_br:3 + 3 * n_br]
        acc_refs = outs[3 + 3 * n_br:]
        i = pl.program_id(1)

        @pl.when(i == 0)
        def _():
            for ref in db_refs:
                ref[...] = jnp.zeros_like(ref)

        masks = _head_masks()
        first_head = lax.broadcasted_iota(jnp.int32, (1, 2 * BLK), 1) < BLK

        sums = {"q": (acc_refs[0], dq_ref), "k": (acc_refs[1], dk_ref), "v": (acc_refs[2], dv_ref)}

        def add_up(n, rows, **vals):
            for key, val in vals.items():
                acc_ref, out_ref = sums[key]
                if n > 0:
                    val = val + acc_ref[rows, :]
                if n == n_br - 1:
                    out_ref[rows, :] = val
                else:
                    acc_ref[rows, :] = val

        for n, (_, dil) in enumerate(BRANCHES):
            _, p, nb = _att_geom(s, dil)
            b_ref, db_ref = b_refs[n], db_refs[n]
            dkx_ref, dvx_ref = x_refs[2 * n], x_refs[2 * n + 1]
            for r in range(dil):
                carry = None
                for b in range(nb):
                    rows = _rows(r + p * b, dil)
                    prev_rows = _rows(r + p * (b - 1), dil) if b > 0 else _rows(w - p + r, dil)
                    kprev, vprev = (kc_ref, vc_ref) if b > 0 else (kp_ref, vp_ref)
                    keys = [(_pair_rows(kprev[prev_rows, :], masks), _pair_rows(vprev[prev_rows, :], masks)),
                            (_pair_rows(kc_ref[rows, :], masks), _pair_rows(vc_ref[rows, :], masks))]
                    q = (q_ref[rows, :] * SCALE).astype(BF16)
                    dy_v = dy_ref[rows, :]
                    dyb = dy_v.astype(BF16)
                    stats = [_row_stats(mh, dy_v, y_ref[rows, :], lse_ref[rows, :]) for mh in masks]
                    delta = jnp.where(first_head, stats[0][0], stats[1][0])
                    lse_h = jnp.where(first_head, stats[0][1], stats[1][1])
                    dq = jnp.zeros((BLK, BLK), F32)
                    dk, dv = [], []
                    for half in range(2):
                        kh, vh = keys[half]
                        sc = lax.dot_general(q, kh, _DIMS["nt"], preferred_element_type=F32) + b_ref[half]
                        pr = jnp.exp(sc - lse_h)
                        if b == 0 and half == 0:
                            pr = pr * (i > 0).astype(F32)
                        dp = lax.dot_general(dyb, vh, _DIMS["nt"], preferred_element_type=F32)
                        ds = pr * (dp - delta)
                        db_ref[half] += ds
                        dsb = ds.astype(BF16)
                        dq = dq + jnp.dot(dsb, kh, preferred_element_type=F32)
                        dk2 = lax.dot_general(dsb, q, _DIMS["tn"], preferred_element_type=F32)
                        dv2 = lax.dot_general(pr.astype(BF16), dyb, _DIMS["tn"], preferred_element_type=F32)
                        dk.append(jnp.where(masks[0], dk2[:BLK], dk2[BLK:]))
                        dv.append(jnp.where(masks[0], dv2[:BLK], dv2[BLK:]))
                    add_up(n, rows, q=dq * SCALE)
                    if b > 0:
                        add_up(n, _rows(r + p * (b - 1), dil), k=carry[0] + dk[0], v=carry[1] + dv[0])
                    else:
                        dkx_ref[_rows(r, dil), :] = dk[0]
                        dvx_ref[_rows(r, dil), :] = dv[0]
                    carry = (dk[1], dv[1])
                add_up(n, _rows(r + p * (nb - 1), dil), k=carry[0], v=carry[1])

    cur, prv = _zcur(w), _zprev(w, 1)
    b_spec = pl.BlockSpec((None, 2, BLK, 2 * BLK), lambda hp, i: (hp, 0, 0, 0))
    x_specs, x_shapes = [], []
    for _, dil in BRANCHES:
        p = BLK * dil
        x_specs += [pl.BlockSpec((p, BLK), lambda hp, i: (i, hp))] * 2
        x_shapes += [_sds((n_steps * p, ATTN_W))] * 2
    outs = pl.pallas_call(
        body, name=name, grid=(4, n_steps),
        in_specs=[cur(0), prv(4), cur(4), prv(8), cur(8)] + [_scur(w)] * 3 + [b_spec] * n_br,
        out_specs=[_scur(w)] * 3 + x_specs + [b_spec] * n_br,
        out_shape=[_sds((s, ATTN_W))] * 3 + x_shapes + [_sds((4, 2, BLK, 2 * BLK))] * n_br,
        scratch_shapes=[pltpu.VMEM((w, BLK), F32)] * 3,
        compiler_params=_params(("parallel", "arbitrary")),
    )(z, z, z, z, z, dy, y, lse, *biases)
    dq, dk, dv = outs[:3]
    extras = [(outs[3 + 2 * n], outs[4 + 2 * n]) for n in range(n_br)]
    return dq, dk, dv, extras, [_unpair_bias_bwd(db) for db in outs[3 + 2 * n_br:]]


ASM_ROWS = 512


def _assemble_dz(dq, dk, dv, extras, dzs, du, name):
    s = dq.shape[0]
    w = min(ATT_ROWS, s)
    n_steps = s // w
    per_step = w // ASM_ROWS
    assert w % ASM_ROWS == 0

    def body(*refs):
        dq_ref, dk_ref, dv_ref, dzs_ref, du_ref = refs[:5]
        x_refs = refs[5:5 + 2 * len(extras)]
        o_ref, acc_ref = refs[-2:]
        j = pl.program_id(0)
        step = j // per_step
        has_next = (step < n_steps - 1).astype(F32)
        last_of_step = ((j + 1) % per_step == 0).astype(F32)
        o_ref[:, 0:ATTN_W] = dq_ref[...].astype(BF16)
        o_ref[:, 3 * ATTN_W:3 * ATTN_W + 2 * SGU_W] = dzs_ref[...].astype(BF16)
        o_ref[:, 3 * ATTN_W + 2 * SGU_W:IN_W] = du_ref[...].astype(BF16)
        for part, (base_ref, col) in enumerate(((dk_ref, ATTN_W), (dv_ref, 2 * ATTN_W))):
            acc_ref[...] = base_ref[...]
            for n, (_, dil) in enumerate(BRANCHES):
                rows = min(BLK * dil, ASM_ROWS)
                scale = has_next if BLK * dil >= w else has_next * last_of_step
                acc_ref[ASM_ROWS - rows:, :] += x_refs[2 * n + part][...] * scale
            o_ref[:, col:col + ATTN_W] = acc_ref[...].astype(BF16)

    def x_spec(dil):
        p = BLK * dil
        rows = min(p, ASM_ROWS)
        blocks_per_step = p // rows
        total = n_steps * blocks_per_step

        def idx(j):
            step = j // per_step
            within = (j % per_step) - (per_step - blocks_per_step)
            return (jnp.clip((step + 1) * blocks_per_step + jnp.maximum(within, 0), 0, total - 1), 0)

        return pl.BlockSpec((rows, ATTN_W), idx)

    in_specs = [_rb(ASM_ROWS, ATTN_W)] * 3 + [_rb(ASM_ROWS, 2 * SGU_W), _rb(ASM_ROWS, SSM_W)]
    args = [dq, dk, dv, dzs, du]
    for (dkx, dvx), (_, dil) in zip(extras, BRANCHES):
        in_specs += [x_spec(dil)] * 2
        args += [dkx, dvx]
    return pl.pallas_call(
        body, name=name, grid=(s // ASM_ROWS,), in_specs=in_specs, out_specs=_rb(ASM_ROWS, IN_W),
        out_shape=_sds((s, IN_W), BF16), scratch_shapes=[pltpu.VMEM((ASM_ROWS, ATTN_W), F32)],
        compiler_params=_params(("parallel",)),
    )(*args)


def _t5_bucket(dist):
    max_exact = N_BUCKETS // 2
    d = np.maximum(dist, 0)
    large = max_exact + (np.log(np.maximum(d, 1) / max_exact) / np.log(REL_MAX / max_exact)
                         * (N_BUCKETS - max_exact)).astype(np.int32)
    large = np.minimum(large, N_BUCKETS - 1)
    return np.where(d < max_exact, d, large).astype(np.int32)


def _bias_tables(rel_bias):
    period = 3 * BLK
    tabs = []
    for _, dil in BRANCHES:
        onehot = np.zeros((period, N_BUCKETS), np.float32)
        d = np.arange(BLK + 1)
        onehot[d, _t5_bucket((BLK - d) * dil)] = 1.0
        f = jnp.dot(jnp.asarray(onehot), rel_bias, precision=lax.Precision.HIGHEST)
        flat = jnp.tile(f.T, (1, BLK))[:, :BLK * (period - 1)]
        tab = flat.reshape(N_HEADS, BLK, period - 1)[:, :, :2 * BLK]
        tabs.append(jnp.where(_band()[None], tab, NEG_INF))
    return tabs


def _bucket_onehot():
    maps = []
    q = np.arange(BLK)[:, None]
    k = np.arange(2 * BLK)[None, :]
    rel = q + BLK - k
    for _, dil in BRANCHES:
        maps.append(np.where((rel >= 0) & (rel <= BLK), _t5_bucket(rel * dil), -1).reshape(-1))
    bmap = jnp.asarray(np.concatenate(maps).astype(np.int32))
    return (bmap[:, None] == jnp.arange(128, dtype=jnp.int32)[None, :]).astype(BF16)


def _block_diag(t):
    g, n, c = t.shape
    eye = jnp.eye(g, dtype=t.dtype)
    return (t[:, :, None, :] * eye[:, None, :, None]).reshape(g * n, g * c)


def _ssm_prep(a_re, a_im, log_dt, b_re, b_im, c_re, c_im):
    lam = lax.complex(a_re, a_im)
    dt = jnp.exp(log_dt)[:, None]
    a_bar = jnp.exp(lam * dt)
    b_bar = ((a_bar - 1.0) / lam)[:, :, None] * lax.complex(b_re, b_im)
    bdt = jnp.concatenate([_block_diag(jnp.real(b_bar)), _block_diag(jnp.imag(b_bar))], axis=0)
    cd = jnp.concatenate([_block_diag(jnp.transpose(c_re, (0, 2, 1))),
                          _block_diag(-jnp.transpose(c_im, (0, 2, 1)))], axis=0)
    return jnp.real(a_bar).reshape(-1), jnp.imag(a_bar).reshape(-1), bdt, cd


def _powers(ar, ai):
    pr, pi = ar[:, None], ai[:, None]
    k = 1
    while k < 8:
        lr, li = pr[:, -1:], pi[:, -1:]
        pr, pi = (jnp.concatenate([pr, pr * lr - pi * li], axis=1),
                  jnp.concatenate([pi, pr * li + pi * lr], axis=1))
        k *= 2
    return pr, pi


def _sgu_bias_expand(b):
    return jnp.repeat(b.T, 64, axis=1)


def _layer_fwd(i, h, a1, p_i, big, small, bias_tabs, next_gain):
    nm = "l%d_" % i
    sv = {"h": h}
    if a1 is None:
        a1 = _rms_fwd(h, small["norm_attn_g"][i], nm + "rms_attn")
    z = _mm(a1, big["w_in"], "nt", nm + "in_proj")
    y_attn, lse = _attn_fwd(z, [t[0] for t in bias_tabs], nm + "attn_fwd")
    bexp = _sgu_bias_expand(small["sgu_b"][i])
    y_sgu = _sgu_fwd(z, small["sgu_ln_g"][i], small["sgu_ln_b"][i], small["sgu_w"][i], bexp, nm + "sgu_fwd")
    ar, ai, bdt, cd = _ssm_prep(*[small[k][i] for k in ("ssm_a_re", "ssm_a_im", "ssm_log_dt", "ssm_b_re",
                                                         "ssm_b_im", "ssm_c_re", "ssm_c_im")])
    xr, xi, yc = _ssm_fwd(z, bdt.astype(BF16), cd.astype(BF16), _scan_tables(*_powers(ar, ai), False),
                          nm + "ssm_core")
    y_ssm = _ssm_post_fwd(yc, z, small["ssm_d"][i], big["ssm_glu_w"], small["ssm_glu_b"][i], nm + "ssm_post")
    if "rest" in big:
        big = dict({k: t for k, t in big.items() if k != "rest"}, **big["rest"](y_ssm))
    mix, h2, a2 = _mix_fwd(y_attn, y_sgu, y_ssm, small["branch_norm_g"][i], big["w_out"], h,
                           small["norm_ffn_g"][i], nm + "out_proj")
    hu = _mm(a2, big["ffn_w_up"], "nt", nm + "ffn_up", out_dtype=BF16)
    hv, hg, act = _conv_fwd(hu, big["ffn_conv_w"], small["ffn_conv_b"][i], nm + "ffn_conv")
    h3, a3 = _mm(act, big["ffn_w_down"], "nn", nm + "ffn_down", add=h2, norm_gain=small["norm_ple_g"][i])
    gp = _mm(a3, big["ple_w_gate"], "nn", nm + "ple_gate", out_dtype=BF16)
    pp = _mm(p_i, big["ple_w_proj"], "nt", nm + "ple_proj", out_dtype=BF16)
    h4, a_next = _ple_fwd(h3, gp, pp, next_gain, nm + "ple_add")
    sv.update(big=big, a1=a1, z=z, y_attn=y_attn, lse=lse, y_sgu=y_sgu, y_ssm=y_ssm, yc=yc, xr=xr, xi=xi, mix=mix, h2=h2,
              a2=a2, hu=hu, hv=hv, hg=hg, act=act, h3=h3, a3=a3, gp=gp, pp=pp)
    return h4, a_next, sv


def _layer_bwd(i, dh4, sv, p_i, big, small, bias_tabs, ffn_done=None):
    nm = "l%d_" % i
    g = {}
    dpp, dgp = _ple_bwd(dh4, sv["gp"], sv["pp"], nm + "ple_bwd")
    g["ple_w_proj"] = _mm(dpp, p_i, "tn", nm + "d_ple_proj", out_dtype=BF16)
    g["ple_w_gate"] = _mm(sv["a3"], dgp, "tn", nm + "d_ple_gate", out_dtype=BF16)
    dh3, dgain = _mm(dgp, big["ple_w_gate"], "nt", nm + "ple_gate_t", add=dh4,
                     norm_bwd=(sv["h3"], small["norm_ple_g"][i]))
    g["norm_ple_g"] = dgain.reshape(D_MODEL)
    g["ffn_w_down"] = _mm(sv["act"], dh3, "tn", nm + "d_ffn_down", out_dtype=BF16)
    dact = _mm(dh3, big["ffn_w_down"], "nt", nm + "ffn_down_t", out_dtype=BF16)
    dhu, g["ffn_conv_w"], dcb = _conv_bwd(dact, sv["hv"], sv["hg"], sv["hu"], big["ffn_conv_w"],
                                          nm + "ffn_conv_bwd")
    g["ffn_conv_b"] = dcb.reshape(2 * D_FF)
    g["ffn_w_up"] = _mm(dhu, sv["a2"], "tn", nm + "d_ffn_up", out_dtype=BF16)
    dh2, dgain = _mm(dhu, big["ffn_w_up"], "nn", nm + "ffn_up_t", add=dh3,
                     norm_bwd=(sv["h2"], small["norm_ffn_g"][i]))
    g["norm_ffn_g"] = dgain.reshape(D_MODEL)
    if ffn_done is not None:
        small = ffn_done(g, small)
    g["w_out"] = _mm(sv["mix"], dh2, "tn", nm + "d_out_proj", out_dtype=BF16)
    dya, dysg, dyss, g["branch_norm_g"] = _mix_bwd(dh2, big["w_out"], sv["y_attn"], sv["y_sgu"], sv["y_ssm"],
                                                   small["branch_norm_g"][i], nm + "mix_bwd")
    ssm_keys = ("ssm_a_re", "ssm_a_im", "ssm_log_dt", "ssm_b_re", "ssm_b_im", "ssm_c_re", "ssm_c_im")
    (ar, ai, bdt, cd), prep_vjp = jax.vjp(_ssm_prep, *[small[k][i] for k in ssm_keys])
    dy1, dgl, y2, dud, g["ssm_d"], g["ssm_glu_b"] = _ssm_post_bwd(
        dyss, sv["yc"], sv["z"], small["ssm_d"][i], big["ssm_glu_w"], small["ssm_glu_b"][i], nm + "ssm_post_bwd")
    g["ssm_glu_w"] = _mm(y2, dgl, "tn", nm + "d_ssm_glu", out_dtype=BF16)
    du, dbdt, dcd, dar, dai = _ssm_bwd(dy1, dud, sv["z"], sv["xr"], sv["xi"], bdt.astype(BF16), cd.astype(BF16),
                                       _scan_tables(*_powers(ar, ai), True), nm + "ssm_core_bwd")
    for k, val in zip(ssm_keys, prep_vjp((dar, dai, dbdt, dcd))):
        g[k] = val
    bexp, bexp_vjp = jax.vjp(_sgu_bias_expand, small["sgu_b"][i])
    dzs, g["sgu_w"], dbexp, g["sgu_ln_g"], g["sgu_ln_b"] = _sgu_bwd(
        sv["z"], dysg, small["sgu_ln_g"][i], small["sgu_ln_b"][i], small["sgu_w"][i], bexp, nm + "sgu_bwd")
    g["sgu_b"] = bexp_vjp(dbexp)[0]
    dq, dk, dv, extras, dbs = _attn_bwd(sv["z"], [t[1] for t in bias_tabs], dya, sv["y_attn"], sv["lse"],
                                        nm + "attn_bwd")
    dbs = [db.reshape(N_HEADS, BLK * 2 * BLK) for db in dbs]
    dz = _assemble_dz(dq, dk, dv, extras, dzs, du, nm + "assemble_dz")
    g["w_in"] = _mm(dz, sv["a1"], "tn", nm + "d_in_proj", out_dtype=BF16)
    dh, dgain = _mm(dz, big["w_in"], "nn", nm + "in_proj_t", add=dh2, norm_bwd=(sv["h"], small["norm_attn_g"][i]))
    g["norm_attn_g"] = dgain.reshape(D_MODEL)
    return dh, g, jnp.concatenate(dbs, axis=1)


def _local_step(x, p, target, layer_weights, small, layer_done=None):
    depth = p.shape[0]
    bias_tabs = [(t, _pair_bias_bwd(t)) for t in _bias_tables(small["rel_bias"])]
    h, a1 = x, None
    saved = []
    for i in range(depth):
        next_gain = small["norm_attn_g"][i + 1] if i + 1 < depth else None
        h, a1, sv = _layer_fwd(i, h, a1, p[i], layer_weights(i, h), small, bias_tabs, next_gain)
        saved.append(sv)
    dh, loss, g_final = _loss_head(h, target, small["final_norm_g"], "loss_head")
    layer_grads = [None] * depth
    dbias = [None] * depth
    for i in reversed(range(depth)):
        ffn_done = None if layer_done is None else (lambda g, sm, i=i: layer_done(i, "ffn", g, sm))
        dh, layer_grads[i], dbias[i] = _layer_bwd(i, dh, saved[i], p[i], saved[i]["big"], small, bias_tabs,
                                                  ffn_done)
        if layer_done is not None:
            small = layer_done(i, "all", layer_grads[i], small)
    big_grads = [{k: lg.pop(k) for k in COMM_NAMES} for lg in layer_grads]
    grads = {k: jnp.stack([layer_grads[i][k] for i in range(depth)]) for k in layer_grads[0]}
    grads["final_norm_g"] = g_final
    g_rb = _mm(sum(dbias[1:], dbias[0]), _bucket_onehot(), "nn", "d_rel_bias", tk=2048)
    grads["rel_bias"] = g_rb[:, :N_BUCKETS].T
    return loss, dh, big_grads, grads


_ANY = pl.BlockSpec(memory_space=pl.ANY)
MESH_IDS = pl.DeviceIdType.MESH


def _slot(ref, axis, j):
    return ref.at[(slice(None),) * axis + (j,)]


def _all_gather(blocks, axis, name):
    nt = len(blocks)

    def body(*refs):
        x_refs, o_refs = refs[:nt], refs[nt:2 * nt]
        send_sems, recv_sems, local_sems = refs[2 * nt:]
        x, y, c = lax.axis_index("x"), lax.axis_index("y"), lax.axis_index("c")
        me, sibling = (x, y, c), (x, y, 1 - c)
        chips = [(1 - x, y), (x, 1 - y), (1 - x, 1 - y)]

        def slot(t, px, py, pc):
            return _slot(o_refs[t], axis, 4 * px + 2 * py + pc)

        def copy(t, k, blk, to, src=None):
            return pltpu.make_async_remote_copy(
                src_ref=slot(t, *blk) if src is None else src, dst_ref=slot(t, *blk),
                send_sem=send_sems.at[7 * t + k], recv_sem=recv_sems.at[7 * t + k],
                device_id=to, device_id_type=MESH_IDS)

        mine = [pltpu.make_async_copy(x_refs[t], slot(t, *me), local_sems.at[t]) for t in range(nt)]
        for cp in mine:
            cp.start()
        first = []
        for t in range(nt):
            first.append(copy(t, 0, me, sibling, src=x_refs[t]))
            first += [copy(t, 1 + j, me, (*chip, c), src=x_refs[t]) for j, chip in enumerate(chips)]
        for cp in first:
            cp.start()
        passed = []
        for t in range(nt):
            for j, chip in enumerate(chips):
                copy(t, 1 + j, (*chip, c), me).wait_recv()
                passed.append(copy(t, 4 + j, (*chip, c), sibling))
                passed[-1].start()
        for t in range(nt):
            copy(t, 0, sibling, me).wait_recv()
            for j, chip in enumerate(chips):
                copy(t, 4 + j, (*chip, 1 - c), me).wait_recv()
        for cp in first + passed:
            cp.wait_send()
        for cp in mine:
            cp.wait()

    out_shape = [jax.ShapeDtypeStruct(b.shape[:axis] + (N_DEV,) + b.shape[axis:], b.dtype) for b in blocks]
    return pl.pallas_call(
        body, name=name, out_shape=out_shape, in_specs=[_ANY] * nt, out_specs=[_ANY] * nt,
        scratch_shapes=[pltpu.SemaphoreType.DMA((7 * nt,)), pltpu.SemaphoreType.DMA((7 * nt,)),
                        pltpu.SemaphoreType.DMA((nt,))],
    )(*blocks)


def _peer(k):
    x, y, c = lax.axis_index("x"), lax.axis_index("y"), lax.axis_index("c")
    px = 1 - x if k & 4 else x
    py = 1 - y if k & 2 else y
    pc = 1 - c if k & 1 else c
    return (px, py, pc), 4 * px + 2 * py + pc


_HBM = pl.BlockSpec(memory_space=pltpu.HBM)
_SEM = pl.BlockSpec(memory_space=pltpu.SEMAPHORE)
_EFFECT = pltpu.SideEffectType.DATAFLOW_SIDE_EFFECTING


def _split_copy(src_ref, land_ref, send_sems, recv_sems, t, k, gather):
    peer, idx = _peer(k)
    _, me = _peer(0)
    return pltpu.make_async_remote_copy(
        src_ref=src_ref if gather else src_ref.at[idx], dst_ref=land_ref.at[me],
        send_sem=send_sems.at[7 * t + k - 1], recv_sem=recv_sems.at[7 * t + k - 1],
        device_id=peer, device_id_type=MESH_IDS)


def _exchange_start(srcs, lands, gather, name):
    nt = len(srcs)

    def body(*refs):
        src_refs, land_refs = refs[:nt], refs[nt:2 * nt]
        send_sems, recv_sems = refs[2 * nt:2 * nt + 2]
        token = refs[-1]
        for k in range(1, N_DEV):
            for t in range(nt):
                _split_copy(src_refs[t], land_refs[t], send_sems, recv_sems, t, k, gather).start()
        token[...] = jnp.zeros_like(token)

    hbm = lambda a: pltpu.HBM(a.shape, a.dtype)
    outs = pl.pallas_call(
        body, name=name,
        out_shape=(pltpu.SemaphoreType.DMA((7 * nt,)), pltpu.SemaphoreType.DMA((7 * nt,)),
                   *[hbm(a) for a in srcs], *[hbm(a) for a in lands], jax.ShapeDtypeStruct((8, 128), F32)),
        in_specs=[_HBM] * (2 * nt),
        out_specs=(_SEM, _SEM, *[_HBM] * (2 * nt), pl.BlockSpec(memory_space=pltpu.VMEM)),
        input_output_aliases={j: 2 + j for j in range(2 * nt)},
        compiler_params=pltpu.CompilerParams(has_side_effects=_EFFECT),
    )(*[pltpu.with_memory_space_constraint(a, pltpu.HBM) for a in list(srcs) + list(lands)])
    return outs[0], outs[1], outs[2:2 + nt], outs[2 + nt:2 + 2 * nt], outs[-1]


def _exchange_wait(send_sems, recv_sems, srcs, lands, after, gather, name):
    nt = len(srcs)

    def body(*refs):
        src_refs, land_refs = refs[:nt], refs[nt:2 * nt]
        send_sems, recv_sems = refs[2 * nt:2 * nt + 2]
        for k in range(1, N_DEV):
            _, idx = _peer(k)
            for t in range(nt):
                _split_copy(src_refs[t], land_refs[t], send_sems, recv_sems, t, k, gather).wait_send()
                arrival = pltpu.make_async_remote_copy(
                    src_ref=land_refs[t].at[idx], dst_ref=land_refs[t].at[idx],
                    send_sem=send_sems.at[7 * t + k - 1], recv_sem=recv_sems.at[7 * t + k - 1],
                    device_id=_peer(k)[0], device_id_type=MESH_IDS)
                arrival.wait_recv()

    hbm = lambda a: pltpu.HBM(a.shape, a.dtype)
    outs = pl.pallas_call(
        body, name=name, out_shape=tuple(hbm(a) for a in list(srcs) + list(lands)),
        in_specs=[_HBM] * (2 * nt) + [_SEM, _SEM, _ANY], out_specs=tuple([_HBM] * (2 * nt)),
        input_output_aliases={j: j for j in range(2 * nt)},
        compiler_params=pltpu.CompilerParams(has_side_effects=_EFFECT),
    )(*srcs, *lands, send_sems, recv_sems, after)
    return outs[nt:]


def _adamw(parts, w, m, v, name, tr):
    n_layers, r, c_ = w.shape
    assert len(parts) == n_layers

    def body(*refs):
        p_refs = refs[:n_layers]
        w_ref, m_ref, v_ref, g_ref, d_ref, mo_ref, vo_ref = refs[n_layers:]

        def update(p_ref):
            g = p_ref[0].astype(F32)
            for j in range(1, N_DEV):
                g = g + p_ref[j].astype(F32)
            m2 = ADAM_B1 * m_ref[...] + (1.0 - ADAM_B1) * g
            v2 = ADAM_B2 * v_ref[...] + (1.0 - ADAM_B2) * (g * g)
            m_hat = m2 / (1.0 - ADAM_B1 ** ADAM_STEP)
            v_hat = v2 / (1.0 - ADAM_B2 ** ADAM_STEP)
            g_ref[...] = g
            d_ref[...] = -ADAM_LR * (m_hat / (jnp.sqrt(v_hat) + ADAM_EPS) + ADAM_WD * w_ref[...])
            mo_ref[...] = m2
            vo_ref[...] = v2

        for layer in range(n_layers):
            pl.when(pl.program_id(0) == layer)(lambda layer=layer: update(p_refs[layer]))

    spec = pl.BlockSpec((None, tr, c_), lambda l, i: (l, i, 0))
    p_spec = pl.BlockSpec((N_DEV, tr, c_), lambda l, i: (0, i, 0))
    return pl.pallas_call(
        body, name=name, grid=(n_layers, r // tr), in_specs=[p_spec] * n_layers + [spec] * 3,
        out_specs=[spec] * 4, out_shape=[_sds((n_layers, r, c_))] * 4,
        compiler_params=_params(("parallel", "parallel")),
    )(*parts, w, m, v)


def _pack_rows(n_elems, align):
    rows = -(-n_elems // PACK_COLS)
    return -(-rows // align) * align


def _pack(arrs, rows, dtype=F32):
    flat = jnp.concatenate([a.reshape(-1) for a in arrs]).astype(dtype)
    return jnp.pad(flat, (0, rows * PACK_COLS - flat.shape[0])).reshape(rows, PACK_COLS)


def _unpack(pack, shapes):
    flat = pack.reshape(-1)
    out, off = [], 0
    for shp in shapes:
        size = int(np.prod(shp))
        out.append(flat[off:off + size].reshape(shp))
        off += size
    return out


def _tile_rows(rows, target, align=16):
    best = align
    for t in range(align, target + 1, align):
        if rows % t == 0:
            best = t
    return best


COMM_NAMES = ("w_in", "ssm_glu_w", "w_out", "ffn_w_up", "ffn_w_down", "ple_w_gate", "ple_w_proj")
COMM_TRANSPOSED = ("w_in", "ffn_w_up", "ple_w_proj")
COMM_EARLY = ("ple_w_proj", "ple_w_gate", "ffn_w_down", "ffn_w_up")
COMM_LATE = ("w_in", "ssm_glu_w", "w_out")
SMALL_TILE_ROWS = 64
CONV_NAME = "ffn_conv_w"


def _to_comm(name, a):
    return jnp.swapaxes(a, 1, 2) if name in COMM_TRANSPOSED else a


def kernel(x, p, rel_bias, norm_attn_g, w_in, sgu_ln_g, sgu_ln_b, sgu_w, sgu_b, ssm_a_re, ssm_a_im, ssm_log_dt, ssm_b_re, ssm_b_im, ssm_c_re, ssm_c_im, ssm_d, ssm_glu_w, ssm_glu_b, branch_norm_g, w_out, norm_ffn_g, ffn_w_up, ffn_conv_w, ffn_conv_b, ffn_w_down, norm_ple_g, ple_w_gate, ple_w_proj, final_norm_g, loss_target, m_rel_bias, m_norm_attn_g, m_w_in, m_sgu_ln_g, m_sgu_ln_b, m_sgu_w, m_sgu_b, m_ssm_a_re, m_ssm_a_im, m_ssm_log_dt, m_ssm_b_re, m_ssm_b_im, m_ssm_c_re, m_ssm_c_im, m_ssm_d, m_ssm_glu_w, m_ssm_glu_b, m_branch_norm_g, m_w_out, m_norm_ffn_g, m_ffn_w_up, m_ffn_conv_w, m_ffn_conv_b, m_ffn_w_down, m_norm_ple_g, m_ple_w_gate, m_ple_w_proj, m_final_norm_g, v_rel_bias, v_norm_attn_g, v_w_in, v_sgu_ln_g, v_sgu_ln_b, v_sgu_w, v_sgu_b, v_ssm_a_re, v_ssm_a_im, v_ssm_log_dt, v_ssm_b_re, v_ssm_b_im, v_ssm_c_re, v_ssm_c_im, v_ssm_d, v_ssm_glu_w, v_ssm_glu_b, v_branch_norm_g, v_w_out, v_norm_ffn_g, v_ffn_w_up, v_ffn_conv_w, v_ffn_conv_b, v_ffn_w_down, v_norm_ple_g, v_ple_w_gate, v_ple_w_proj, v_final_norm_g):
    given = dict(locals())
    w = {n: given[n] for n in WEIGHT_NAMES}
    m = {n: given["m_" + n] for n in WEIGHT_NAMES}
    v = {n: given["v_" + n] for n in WEIGHT_NAMES}
    depth = p.shape[0]
    dev = 4 * lax.axis_index("x") + 2 * lax.axis_index("y") + lax.axis_index("c")

    wc = {n: _to_comm(n, w[n]) for n in COMM_NAMES}
    wb = {n: wc[n].astype(BF16) for n in COMM_NAMES}
    conv_local = [w[CONV_NAME], m[CONV_NAME], v[CONV_NAME]]
    conv_rows = _pack_rows(sum(int(np.prod(t.shape)) for t in conv_local), 8)
    conv_g, = _all_gather([_pack(conv_local, conv_rows)], 0, "gather_conv_taps")
    conv_parts = zip(*[_unpack(conv_g[j], [t.shape for t in conv_local]) for j in range(N_DEV)])
    conv_w, conv_m, conv_v = [jnp.concatenate(parts, axis=2) for parts in conv_parts]
    small = {n: w[n] for n in SMALL_NAMES}

    def whole(names, blocks):
        return {n: t.reshape(-1, t.shape[-1]) for n, t in zip(names, blocks)}

    def own_slot(block):
        return lax.dynamic_update_slice_in_dim(jnp.zeros((N_DEV,) + block.shape, block.dtype), block[None], dev, 0)

    def start_gather(names, i, after):
        srcs, after = lax.optimization_barrier(([wb[n][i] for n in names], after))
        return _exchange_start(srcs, [own_slot(s) for s in srcs], True, "gather_weights_%d_start" % i), after

    def wait_gather(names, i, started, after):
        send_sems, recv_sems, srcs, lands, _ = started
        return whole(names, _exchange_wait(send_sems, recv_sems, srcs, lands, after, True,
                                           "gather_weights_%d_wait" % i))

    at_once = ("w_in", "ssm_glu_w")
    later = tuple(n for n in COMM_NAMES if n not in at_once)
    w_in_0 = _all_gather([wb[n][0] for n in at_once], 0, "gather_w_in_0")
    gathering = {}
    gathering[0], (w_in_0, _) = start_gather(later, 0, (w_in_0, conv_g))
    small["norm_attn_g"] = small["norm_attn_g"] + gathering[0][4][0, 0]

    def layer_weights(i, h):
        if i > 0:
            got = wait_gather(COMM_NAMES, i, gathering.pop(i), h)
            if i + 1 < depth:
                gathering[i + 1], ordered = start_gather(COMM_NAMES, i + 1, got["w_in"])
                got["w_in"] = ordered + gathering[i + 1][4][0, 0].astype(BF16)
            return dict(got, **{CONV_NAME: conv_w[i]})

        def rest(after):
            got = wait_gather(later, 0, gathering.pop(0), after)
            if depth > 1:
                gathering[1], ordered = start_gather(COMM_NAMES, 1, got["w_out"])
                got["w_out"] = ordered + gathering[1][4][0, 0].astype(BF16)
            return got

        return dict(whole(at_once, w_in_0), **{CONV_NAME: conv_w[0], "rest": rest})

    def as_slots(g, n):
        return g.reshape((N_DEV,) + wc[n].shape[1:])

    scattering = {}

    def layer_done(i, stage, g, small_now):
        if stage == "all" and i == 0:
            return small_now
        names = COMM_EARLY if stage == "ffn" else COMM_LATE
        srcs = [as_slots(g[n], n) for n in names]
        lands = [own_slot(lax.dynamic_index_in_dim(s, dev, 0, keepdims=False)) for s in srcs]
        started = _exchange_start(srcs, lands, False, "scatter_weight_grads_%d_%s_start" % (i, stage))
        scattering[i, stage] = (names, started)
        pin = "branch_norm_g" if stage == "ffn" else "norm_ple_g"
        return dict(small_now, **{pin: small_now[pin] + started[4][0, 0]})

    loss, dx, big_grads, grads = _local_step(x[0], p[:, 0], loss_target[0], layer_weights, small, layer_done)
    loss = lax.psum(loss, ("x", "y", "c"))

    recv = [{} for _ in range(depth)]
    for (i, stage), (names, (send_sems, recv_sems, srcs, lands, _)) in scattering.items():
        got = _exchange_wait(send_sems, recv_sems, srcs, lands, dx, False,
                             "scatter_weight_grads_%d_%s_wait" % (i, stage))
        recv[i].update(zip(names, got))
    srcs = [as_slots(big_grads[0][n], n) for n in COMM_LATE]
    lands = [own_slot(lax.dynamic_index_in_dim(s, dev, 0, keepdims=False)) for s in srcs]
    last = _exchange_start(srcs, lands, False, "scatter_weight_grads_0_all_start")
    out = {}

    def update(n, pin=None):
        weight = wc[n] if pin is None else wc[n] + pin
        res = _adamw([recv[i][n] for i in range(depth)], weight, _to_comm(n, m[n]), _to_comm(n, v[n]),
                     "adamw_" + n, _tile_rows(wc[n].shape[1], 256))
        out[n] = [_to_comm(n, r) for r in res]

    for j, n in enumerate(COMM_EARLY):
        update(n, last[4][0, 0] if j == 0 else None)
    got = _exchange_wait(last[0], last[1], last[2], last[3], out[COMM_EARLY[-1]][0], False,
                         "scatter_weight_grads_0_all_wait")
    recv[0].update(zip(COMM_LATE, got))

    rep_names = SMALL_NAMES + (CONV_NAME,)
    rep_w = dict({n: w[n] for n in SMALL_NAMES}, **{CONV_NAME: conv_w})
    rep_m = dict({n: m[n] for n in SMALL_NAMES}, **{CONV_NAME: conv_m})
    rep_v = dict({n: v[n] for n in SMALL_NAMES}, **{CONV_NAME: conv_v})
    rep_shapes = [rep_w[n].shape for n in rep_names]
    rep_rows = _pack_rows(sum(int(np.prod(s)) for s in rep_shapes), SMALL_TILE_ROWS)
    rep_parts, = _all_gather([_pack([grads[n] for n in rep_names], rep_rows)], 0, "gather_small_grads")
    for n in COMM_LATE:
        update(n)
    rep_out = _adamw([rep_parts], *[_pack([src[n] for n in rep_names], rep_rows)[None] for src in (rep_w, rep_m, rep_v)],
                     "adamw_replicated", SMALL_TILE_ROWS)
    for n, vals in zip(rep_names, zip(*[_unpack(r[0], rep_shapes) for r in rep_out])):
        out[n] = list(vals)
    shard = ffn_conv_w.shape[2]
    out[CONV_NAME] = [lax.dynamic_slice_in_dim(t, dev * shard, shard, axis=2) for t in out[CONV_NAME]]
    results = [[out[n][kind] for n in WEIGHT_NAMES] for kind in range(4)]
    return (loss, dx[None], *results[0], *results[1], *results[2], *results[3])
```

```python
import math

import numpy as np
import jax
import jax.numpy as jnp
from jax import lax
from jax.experimental import pallas as pl
from jax.experimental.pallas import tpu as pltpu

F32 = jnp.float32
BF16 = jnp.bfloat16

D_MODEL = 1024
HEAD_DIM = 64
N_HEADS = 8
ATTN_W = 512
SGU_W = 256
SGU_GROUPS = 4
SGU_CHUNK = 128
SSM_W = 256
SSM_GROUPS = 16
SSM_STATE = 64
SSM_NS = SSM_GROUPS * SSM_STATE
IN_W = 2304
D_FF = 2816
BRANCHES = ((128, 1), (512, 4), (2048, 16))
BLK = 128
N_BUCKETS = 32
REL_MAX = 2048
EPS = 1e-6
NEG_INF = -1e30
N_DEV = 8

ADAM_LR = 0.001
ADAM_B1 = 0.9
ADAM_B2 = 0.999
ADAM_EPS = 1e-08
ADAM_WD = 0.01
ADAM_STEP = 10

VMEM_LIMIT_BYTES = 56 * 1024 * 1024
GELU_C = math.sqrt(2.0 / math.pi)

SMALL_NAMES = ("rel_bias", "norm_attn_g", "sgu_ln_g", "sgu_ln_b", "sgu_w", "sgu_b", "ssm_a_re", "ssm_a_im",
               "ssm_log_dt", "ssm_b_re", "ssm_b_im", "ssm_c_re", "ssm_c_im", "ssm_d", "ssm_glu_b",
               "branch_norm_g", "norm_ffn_g", "ffn_conv_b", "norm_ple_g", "final_norm_g")
WEIGHT_NAMES = ("rel_bias", "norm_attn_g", "w_in", "sgu_ln_g", "sgu_ln_b", "sgu_w", "sgu_b", "ssm_a_re",
                "ssm_a_im", "ssm_log_dt", "ssm_b_re", "ssm_b_im", "ssm_c_re", "ssm_c_im", "ssm_d", "ssm_glu_w",
                "ssm_glu_b", "branch_norm_g", "w_out", "norm_ffn_g", "ffn_w_up", "ffn_conv_w", "ffn_conv_b",
                "ffn_w_down", "norm_ple_g", "ple_w_gate", "ple_w_proj", "final_norm_g")
PACK_COLS = 512


def _params(sem):
    return pltpu.CompilerParams(dimension_semantics=sem, vmem_limit_bytes=VMEM_LIMIT_BYTES)


def _pick(dim, target):
    if dim <= target:
        return dim
    best = None
    for t in range(128, target + 1, 128):
        if dim % t == 0:
            best = t
    return dim if best is None else best


def _gelu(x):
    return 0.5 * x * (1.0 + jnp.tanh(GELU_C * (x + 0.044715 * (x * x * x))))


def _gelu_grad(x):
    t = jnp.tanh(GELU_C * (x + 0.044715 * (x * x * x)))
    return 0.5 * (1.0 + t) + 0.5 * x * (1.0 - t * t) * (GELU_C * (1.0 + 3.0 * 0.044715 * (x * x)))


def _sigmoid(x):
    return 1.0 / (1.0 + jnp.exp(-x))


_DIMS = {"nn": (((1,), (0,)), ((), ())), "tn": (((0,), (0,)), ((), ())), "nt": (((1,), (1,)), ((), ()))}


MM_TILE = D_FF // 2


def _mm(a, b, mode, name, add=None, out_dtype=F32, norm_gain=None, norm_bwd=None, tm=MM_TILE, tn=MM_TILE,
        tk=MM_TILE):
    if mode == "nn":
        m, k = a.shape
        k2, n = b.shape
    elif mode == "tn":
        k, m = a.shape
        k2, n = b.shape
    else:
        m, k = a.shape
        n, k2 = b.shape
    assert k == k2, (name, a.shape, b.shape, mode)
    tm, tn, tk = _pick(m, tm), _pick(n, tn), _pick(k, tk)
    nk = k // tk
    dims = _DIMS[mode]
    has_add = add is not None
    has_norm = norm_gain is not None
    has_nbwd = norm_bwd is not None
    assert not (has_norm or has_nbwd) or tn == n

    def body(*refs):
        a_ref, b_ref = refs[:2]
        rest = list(refs[2:])
        add_ref = rest.pop(0) if has_add else None
        g_ref = rest.pop(0) if has_norm else None
        h_ref, hg_ref = (rest.pop(0), rest.pop(0)) if has_nbwd else (None, None)
        o_ref = rest.pop(0)
        n_ref = rest.pop(0) if has_norm else None
        dg_ref = rest.pop(0) if has_nbwd else None
        part = lax.dot_general(a_ref[...].astype(BF16), b_ref[...].astype(BF16), dims,
                               preferred_element_type=F32)
        if has_nbwd:
            @pl.when((pl.program_id(0) == 0) & (pl.program_id(2) == 0))
            def _():
                dg_ref[...] = jnp.zeros_like(dg_ref)

        def finish(r):
            if has_nbwd:
                x = h_ref[...]
                scale = lax.rsqrt(jnp.mean(x * x, axis=-1, keepdims=True) + EPS)
                xh = x * scale
                dg_ref[...] += jnp.sum(r * xh, axis=0, keepdims=True)
                dxh = r * hg_ref[...]
                r = scale * (dxh - xh * jnp.mean(dxh * xh, axis=-1, keepdims=True))
            if has_add:
                r = r + add_ref[...]
            o_ref[...] = r.astype(out_dtype)
            if has_norm:
                scale = lax.rsqrt(jnp.mean(r * r, axis=-1, keepdims=True) + EPS)
                n_ref[...] = (r * scale * g_ref[...]).astype(BF16)

        if nk == 1:
            finish(part)
            return
        acc_ref = refs[-1]
        kk = pl.program_id(2)

        @pl.when(kk == 0)
        def _():
            acc_ref[...] = part

        @pl.when((kk > 0) & (kk < nk - 1))
        def _():
            acc_ref[...] += part

        @pl.when(kk == nk - 1)
        def _():
            finish(acc_ref[...] + part)

    if mode == "tn":
        a_spec = pl.BlockSpec((tk, tm), lambda i, j, kk: (kk, i))
    else:
        a_spec = pl.BlockSpec((tm, tk), lambda i, j, kk: (i, kk))
    if mode == "nt":
        b_spec = pl.BlockSpec((tn, tk), lambda i, j, kk: (j, kk))
    else:
        b_spec = pl.BlockSpec((tk, tn), lambda i, j, kk: (kk, j))
    o_spec = pl.BlockSpec((tm, tn), lambda i, j, kk: (i, j))
    in_specs = [a_spec, b_spec] + ([o_spec] if has_add else [])
    args = (a, b) + ((add,) if has_add else ())
    out_specs, out_shape = o_spec, jax.ShapeDtypeStruct((m, n), out_dtype)
    if has_norm:
        in_specs.append(pl.BlockSpec((1, n), lambda i, j, kk: (0, 0)))
        args += (norm_gain.reshape(1, n),)
        out_specs, out_shape = [o_spec, o_spec], [out_shape, jax.ShapeDtypeStruct((m, n), BF16)]
    if has_nbwd:
        row_spec = pl.BlockSpec((1, n), lambda i, j, kk: (0, 0))
        in_specs += [o_spec, row_spec]
        args += (norm_bwd[0], norm_bwd[1].reshape(1, n))
        out_specs, out_shape = [o_spec, row_spec], [out_shape, jax.ShapeDtypeStruct((1, n), F32)]
    sem = ("arbitrary",) * 3 if has_nbwd else ("parallel", "parallel", "arbitrary")
    return pl.pallas_call(
        body, name=name, grid=(m // tm, n // tn, nk),
        in_specs=in_specs, out_specs=out_specs, out_shape=out_shape,
        scratch_shapes=[pltpu.VMEM((tm, tn), F32)] if nk > 1 else [], compiler_params=_params(sem),
    )(*args)


def _rb(tm, w, cb=0):
    return pl.BlockSpec((tm, w), lambda i: (i, cb))


def _fb(shape):
    nd = len(shape)
    return pl.BlockSpec(shape, lambda i: (0,) * nd)


def _rowcall(body, name, n_rows, tm, in_specs, args, out_specs, out_shapes):
    return pl.pallas_call(
        body, name=name, grid=(n_rows // tm,), in_specs=in_specs, out_specs=out_specs, out_shape=out_shapes,
        compiler_params=_params(("arbitrary",)),
    )(*args)


def _sds(shape, dtype=F32):
    return jax.ShapeDtypeStruct(shape, dtype)


def _rms_fwd(h, g, name, tm=1024):
    s, d = h.shape

    def body(h_ref, g_ref, o_ref):
        x = h_ref[...]
        r = lax.rsqrt(jnp.mean(x * x, axis=-1, keepdims=True) + EPS)
        o_ref[...] = (x * r * g_ref[...]).astype(BF16)

    return _rowcall(body, name, s, tm, [_rb(tm, d), _fb((1, d))], (h, g.reshape(1, d)), _rb(tm, d),
                    _sds((s, d), BF16))


def _loss_head(h, target, g, name, tm=1024):
    s, d = h.shape

    def body(h_ref, t_ref, g_ref, dh_ref, loss_ref, dg_ref):
        @pl.when(pl.program_id(0) == 0)
        def _():
            dg_ref[...] = jnp.zeros_like(dg_ref)
            loss_ref[...] = jnp.zeros_like(loss_ref)

        x = h_ref[...]
        r = lax.rsqrt(jnp.mean(x * x, axis=-1, keepdims=True) + EPS)
        xh = x * r
        gg = g_ref[...]
        err = xh * gg - t_ref[...]
        loss_ref[...] += jnp.sum(err * err) * (0.5 / d)
        dy = err * (1.0 / d)
        dg_ref[...] += jnp.sum(dy * xh, axis=0, keepdims=True)
        dxh = dy * gg
        dh_ref[...] = r * (dxh - xh * jnp.mean(dxh * xh, axis=-1, keepdims=True))

    dh, loss, dg = _rowcall(body, name, s, tm, [_rb(tm, d), _rb(tm, d), _fb((1, d))], (h, target, g.reshape(1, d)),
                            [_rb(tm, d), _fb((1, 128)), _fb((1, d))], [_sds((s, d)), _sds((1, 128)), _sds((1, d))])
    return dh, loss[0, 0], dg.reshape(d)


_MIX_PARTS = ((0, 512), (512, 768), (768, 1024))


def _mix_fwd(ya, ysg, yss, g, w_out, h, g_next, name, tm=1024):
    s = ya.shape[0]

    def body(a_ref, b_ref, c_ref, g_ref, w_ref, h_ref, n_ref, mix_ref, h2_ref, a2_ref):
        for ref, (lo, hi) in zip((a_ref, b_ref, c_ref), _MIX_PARTS):
            y = ref[...]
            r = lax.rsqrt(jnp.mean(y * y, axis=-1, keepdims=True) + EPS)
            mix_ref[:, lo:hi] = (y * r * g_ref[:, lo:hi]).astype(BF16)
        h2 = h_ref[...] + jnp.dot(mix_ref[...], w_ref[...], preferred_element_type=F32)
        h2_ref[...] = h2
        scale = lax.rsqrt(jnp.mean(h2 * h2, axis=-1, keepdims=True) + EPS)
        a2_ref[...] = (h2 * scale * n_ref[...]).astype(BF16)

    return _rowcall(
        body, name, s, tm,
        [_rb(tm, 512), _rb(tm, 256), _rb(tm, 256), _fb((1, 1024)), _fb((1024, 1024)), _rb(tm, 1024), _fb((1, 1024))],
        (ya, ysg, yss, g.reshape(1, 1024), w_out, h, g_next.reshape(1, 1024)),
        [_rb(tm, 1024)] * 3, [_sds((s, 1024), BF16), _sds((s, 1024)), _sds((s, 1024), BF16)])


def _mix_bwd(dh, w_out, ya, ysg, yss, g, name, tm=1024):
    s = ya.shape[0]

    def body(dh_ref, w_ref, a_ref, b_ref, c_ref, g_ref, da_ref, db_ref, dc_ref, dg_ref):
        @pl.when(pl.program_id(0) == 0)
        def _():
            dg_ref[...] = jnp.zeros_like(dg_ref)

        dmix = lax.dot_general(dh_ref[...].astype(BF16), w_ref[...], _DIMS["nt"], preferred_element_type=F32)
        for ref, dref, (lo, hi) in zip((a_ref, b_ref, c_ref), (da_ref, db_ref, dc_ref), _MIX_PARTS):
            y = ref[...]
            r = lax.rsqrt(jnp.mean(y * y, axis=-1, keepdims=True) + EPS)
            xh = y * r
            dm = dmix[:, lo:hi]
            dg_ref[:, lo:hi] += jnp.sum(dm * xh, axis=0, keepdims=True)
            dxh = dm * g_ref[:, lo:hi]
            dref[...] = r * (dxh - xh * jnp.mean(dxh * xh, axis=-1, keepdims=True))

    da, db, dc, dg = _rowcall(
        body, name, s, tm,
        [_rb(tm, 1024), _fb((1024, 1024)), _rb(tm, 512), _rb(tm, 256), _rb(tm, 256), _fb((1, 1024))],
        (dh, w_out, ya, ysg, yss, g.reshape(1, 1024)),
        [_rb(tm, 512), _rb(tm, 256), _rb(tm, 256), _fb((1, 1024))],
        [_sds((s, 512)), _sds((s, 256)), _sds((s, 256)), _sds((1, 1024))])
    return da, db, dc, dg.reshape(1024)


def _ssm_post_fwd(yc, z, d, gw, gb, name, tm=1024):
    s = yc.shape[0]

    def body(yc_ref, u_ref, d_ref, gw_ref, gb_ref, o_ref):
        y1 = yc_ref[...] + d_ref[...] * u_ref[...]
        y2 = _gelu(y1)
        gl = jnp.dot(y2.astype(BF16), gw_ref[...], preferred_element_type=F32) + gb_ref[...]
        o_ref[...] = y2 * _sigmoid(gl)

    return _rowcall(body, name, s, tm, [_rb(tm, 256), _rb(tm, 256, 8), _fb((1, 256)), _fb((256, 256)), _fb((1, 256))],
                    (yc, z, d.reshape(1, 256), gw, gb.reshape(1, 256)), _rb(tm, 256), _sds((s, 256)))


def _ssm_post_bwd(dy, yc, z, d, gw, gb, name, tm=1024):
    s = yc.shape[0]

    def body(dy_ref, yc_ref, u_ref, d_ref, gw_ref, gb_ref, dy1_ref, dgl_ref, y2_ref, dud_ref, dd_ref, dgb_ref):
        @pl.when(pl.program_id(0) == 0)
        def _():
            dd_ref[...] = jnp.zeros_like(dd_ref)
            dgb_ref[...] = jnp.zeros_like(dgb_ref)

        u = u_ref[...]
        dd = d_ref[...]
        y1 = yc_ref[...] + dd * u
        y2 = _gelu(y1)
        gw_v = gw_ref[...]
        gl = jnp.dot(y2.astype(BF16), gw_v, preferred_element_type=F32) + gb_ref[...]
        sg = _sigmoid(gl)
        dyv = dy_ref[...]
        dgl = dyv * y2 * sg * (1.0 - sg)
        dy2 = dyv * sg + lax.dot_general(dgl.astype(BF16), gw_v, _DIMS["nt"], preferred_element_type=F32)
        dy1 = dy2 * _gelu_grad(y1)
        dy1_ref[...] = dy1.astype(BF16)
        dgl_ref[...] = dgl.astype(BF16)
        y2_ref[...] = y2.astype(BF16)
        dud_ref[...] = dy1 * dd
        dd_ref[...] += jnp.sum(dy1 * u, axis=0, keepdims=True)
        dgb_ref[...] += jnp.sum(dgl, axis=0, keepdims=True)

    outs = _rowcall(
        body, name, s, tm,
        [_rb(tm, 256), _rb(tm, 256), _rb(tm, 256, 8), _fb((1, 256)), _fb((256, 256)), _fb((1, 256))],
        (dy, yc, z, d.reshape(1, 256), gw, gb.reshape(1, 256)),
        [_rb(tm, 256)] * 4 + [_fb((1, 256))] * 2,
        [_sds((s, 256), BF16)] * 3 + [_sds((s, 256))] + [_sds((1, 256))] * 2)
    dy1, dgl, y2, dud, dd, dgb = outs
    return dy1, dgl, y2, dud, dd.reshape(256), dgb.reshape(256)


SCAN_T = 512
N_SCAN_TABLES = 6


def _scan_tables(pr, pi, reverse):
    ns = pr.shape[0]
    sign = -1.0 if reverse else 1.0
    power = [(jnp.ones((ns,), F32), jnp.zeros((ns,), F32))] + [(pr[:, k], sign * pi[:, k]) for k in range(8)]
    zero = (jnp.zeros((ns,), F32), jnp.zeros((ns,), F32))

    def table(exponents):
        rows = [zero if e is None else power[e] for e in exponents]
        return jnp.stack([jnp.concatenate(row) for row in rows])

    tabs = []
    for k in (1, 2, 4):
        has_partner = [(s < 8 - k) if reverse else (s >= k) for s in range(8)]
        tabs.append(table([k if ok else None for ok in has_partner]))
    tabs.append(table([s if reverse else 7 - s for s in range(8)]))
    tabs.append(table([8 - s if reverse else s + 1 for s in range(8)]))
    tabs.append(table([8] * 8))
    return jnp.stack(tabs)


def _cmul(ar, ai, br, bi):
    return ar * br - ai * bi, ar * bi + ai * br


def _scan_group(ur, ui, cr, ci, tr_ref, ti_ref, reverse):
    xr, xi = ur, ui
    for n, k in enumerate((1, 2, 4)):
        shift = 8 - k if reverse else k
        pr, pi = _cmul(tr_ref[n], ti_ref[n], pltpu.roll(xr, shift, axis=0), pltpu.roll(xi, shift, axis=0))
        xr, xi = xr + pr, xi + pi
    sr, si = _cmul(tr_ref[3], ti_ref[3], ur, ui)
    for k in (1, 2, 4):
        sr, si = sr + pltpu.roll(sr, k, axis=0), si + pltpu.roll(si, k, axis=0)
    pr, pi = _cmul(tr_ref[4], ti_ref[4], cr, ci)
    nr, ni = _cmul(tr_ref[5], ti_ref[5], cr, ci)
    return xr + pr, xi + pi, nr + sr, ni + si


def _table_halves(t_ref):
    return t_ref.at[:, :, pl.ds(0, SSM_NS)], t_ref.at[:, :, pl.ds(SSM_NS, SSM_NS)]


_U_BLOCK = (IN_W - SSM_W) // SSM_W


def _ssm_fwd(z, bdt, cd, tabs, name):
    s = z.shape[0]
    ns = SSM_NS
    n_t = s // SCAN_T

    def body(u_ref, b_ref, c_ref, t_ref, xr_ref, xi_ref, y_ref, cr_ref, ci_ref, ur_ref, ui_ref):
        @pl.when(pl.program_id(0) == 0)
        def _():
            cr_ref[...] = jnp.zeros_like(cr_ref)
            ci_ref[...] = jnp.zeros_like(ci_ref)

        bu = lax.dot_general(u_ref[...].astype(BF16), b_ref[...], _DIMS["nt"], preferred_element_type=F32)
        ur_ref[...] = bu[:, :ns]
        ui_ref[...] = bu[:, ns:]
        tr_ref, ti_ref = _table_halves(t_ref)

        def group(g, carry):
            rows = pl.ds(pl.multiple_of(g * 8, 8), 8)
            xr, xi, cr, ci = _scan_group(ur_ref[rows, :], ui_ref[rows, :], *carry, tr_ref, ti_ref, False)
            xr_ref[rows, :] = xr
            xi_ref[rows, :] = xi
            return cr, ci

        cr, ci = lax.fori_loop(0, SCAN_T // 8, group, (cr_ref[...], ci_ref[...]), unroll=2)
        cr_ref[...] = cr
        ci_ref[...] = ci
        y_ref[...] = (jnp.dot(xr_ref[...].astype(BF16), c_ref[0:ns, :], preferred_element_type=F32)
                      + jnp.dot(xi_ref[...].astype(BF16), c_ref[ns:, :], preferred_element_type=F32))

    x_spec = pl.BlockSpec((SCAN_T, ns), lambda t: (t, 0))
    return pl.pallas_call(
        body, name=name, grid=(n_t,),
        in_specs=[pl.BlockSpec((SCAN_T, SSM_W), lambda t: (t, _U_BLOCK)), _fb((2 * ns, SSM_W)),
                  _fb((2 * ns, SSM_W)), _fb((N_SCAN_TABLES, 8, 2 * ns))],
        out_specs=[x_spec, x_spec, _rb(SCAN_T, SSM_W)],
        out_shape=[_sds((s, ns)), _sds((s, ns)), _sds((s, SSM_W))],
        scratch_shapes=[pltpu.VMEM((8, ns), F32)] * 2 + [pltpu.VMEM((SCAN_T, ns), F32)] * 2,
        compiler_params=_params(("arbitrary",)),
    )(z, bdt, cd, tabs)


def _ssm_bwd(dy1, dud, z, xr, xi, bdt, cd, tabs, name):
    s = z.shape[0]
    ns = SSM_NS
    n_t = s // SCAN_T
    n_groups = SCAN_T // 8

    def body(dy_ref, dud_ref, u_ref, xr_ref, xi_ref, pxr_ref, pxi_ref, b_ref, c_ref, t_ref,
             du_ref, dbd_ref, dcd_ref, dar_ref, dai_ref,
             cr_ref, ci_ref, ar_ref, ai_ref, sxr_ref, sxi_ref, gr_ref, gi_ref, lr_ref, li_ref, bacc_ref, cacc_ref):
        t = pl.program_id(0)

        @pl.when(t == 0)
        def _():
            for ref in (cr_ref, ci_ref, ar_ref, ai_ref, bacc_ref, cacc_ref):
                ref[...] = jnp.zeros_like(ref)

        dyb = dy_ref[...]
        g = lax.dot_general(dyb, c_ref[...], _DIMS["nt"], preferred_element_type=F32)
        gr_ref[...] = g[:, :ns]
        gi_ref[...] = g[:, ns:]
        has_before = (t < n_t - 1).astype(F32)
        sxr_ref[0:8, :] = pxr_ref[...] * has_before
        sxi_ref[0:8, :] = pxi_ref[...] * has_before
        sxr_ref[8:, :] = xr_ref[...]
        sxi_ref[8:, :] = xi_ref[...]
        first_row = lax.broadcasted_iota(jnp.int32, (8, ns), 0) == 0
        tr_ref, ti_ref = _table_halves(t_ref)

        def group(k, carry):
            cr, ci, ar, ai = carry
            g8 = pl.multiple_of((n_groups - 1 - k) * 8, 8)
            rows = pl.ds(g8, 8)
            lr, li, cr, ci = _scan_group(gr_ref[rows, :], gi_ref[rows, :], cr, ci, tr_ref, ti_ref, True)
            lr_ref[rows, :] = lr
            li_ref[rows, :] = li
            here, before = pl.ds(g8 + 8, 8), rows
            pr = jnp.where(first_row, pltpu.roll(sxr_ref[before, :], 1, axis=0), pltpu.roll(sxr_ref[here, :], 1, axis=0))
            pi = jnp.where(first_row, pltpu.roll(sxi_ref[before, :], 1, axis=0), pltpu.roll(sxi_ref[here, :], 1, axis=0))
            return cr, ci, ar + lr * pr + li * pi, ai + li * pr - lr * pi

        cr, ci, ar, ai = lax.fori_loop(0, n_groups, group,
                                       (cr_ref[...], ci_ref[...], ar_ref[...], ai_ref[...]), unroll=2)
        cr_ref[...] = cr
        ci_ref[...] = ci
        ar_ref[...] = ar
        ai_ref[...] = ai
        lrb = lr_ref[...].astype(BF16)
        lib = li_ref[...].astype(BF16)
        ub = u_ref[...].astype(BF16)
        du_ref[...] = (dud_ref[...] + jnp.dot(lrb, b_ref[0:ns, :], preferred_element_type=F32)
                       + jnp.dot(lib, b_ref[ns:, :], preferred_element_type=F32))
        bacc_ref[0:ns, :] += lax.dot_general(lrb, ub, _DIMS["tn"], preferred_element_type=F32)
        bacc_ref[ns:, :] += lax.dot_general(lib, ub, _DIMS["tn"], preferred_element_type=F32)
        cacc_ref[0:ns, :] += lax.dot_general(xr_ref[...].astype(BF16), dyb, _DIMS["tn"], preferred_element_type=F32)
        cacc_ref[ns:, :] += lax.dot_general(xi_ref[...].astype(BF16), dyb, _DIMS["tn"], preferred_element_type=F32)

        @pl.when(t == n_t - 1)
        def _():
            for k in (1, 2, 4):
                ar_ref[...] += pltpu.roll(ar_ref[...], k, axis=0)
                ai_ref[...] += pltpu.roll(ai_ref[...], k, axis=0)
            dar_ref[...] = ar_ref[...]
            dai_ref[...] = ai_ref[...]
            dbd_ref[...] = bacc_ref[...]
            dcd_ref[...] = cacc_ref[...]

    rev = lambda t: n_t - 1 - t
    row_spec = pl.BlockSpec((SCAN_T, SSM_W), lambda t: (rev(t), 0))
    x_spec = pl.BlockSpec((SCAN_T, ns), lambda t: (rev(t), 0))
    before_spec = pl.BlockSpec((8, ns), lambda t: (jnp.maximum(rev(t) * (SCAN_T // 8) - 1, 0), 0))
    du, dbd, dcd, dar, dai = pl.pallas_call(
        body, name=name, grid=(n_t,),
        in_specs=[row_spec, row_spec, pl.BlockSpec((SCAN_T, SSM_W), lambda t: (rev(t), _U_BLOCK)),
                  x_spec, x_spec, before_spec, before_spec,
                  _fb((2 * ns, SSM_W)), _fb((2 * ns, SSM_W)), _fb((N_SCAN_TABLES, 8, 2 * ns))],
        out_specs=[row_spec, _fb((2 * ns, SSM_W)), _fb((2 * ns, SSM_W)), _fb((8, ns)), _fb((8, ns))],
        out_shape=[_sds((s, SSM_W)), _sds((2 * ns, SSM_W)), _sds((2 * ns, SSM_W)), _sds((8, ns)), _sds((8, ns))],
        scratch_shapes=([pltpu.VMEM((8, ns), F32)] * 4 + [pltpu.VMEM((SCAN_T + 8, ns), F32)] * 2
                        + [pltpu.VMEM((SCAN_T, ns), F32)] * 4 + [pltpu.VMEM((2 * ns, SSM_W), F32)] * 2),
        compiler_params=_params(("arbitrary",)),
    )(dy1, dud, z, xr, xi, xr, xi, bdt, cd, tabs)
    return du, dbd, dcd, dar[0], dai[0]


def _group_ids():
    return lax.broadcasted_iota(jnp.int32, (1, SGU_W), 1) // 64


def _group_mean(val, gid):
    out = jnp.zeros_like(val)
    for g in range(SGU_GROUPS):
        mg = gid == g
        out = jnp.where(mg, jnp.sum(jnp.where(mg, val, 0.0), axis=1, keepdims=True) * (1.0 / 64), out)
    return out


def _causal_w(w_ref, g):
    t = lax.broadcasted_iota(jnp.int32, (SGU_CHUNK, SGU_CHUNK), 0)
    s = lax.broadcasted_iota(jnp.int32, (SGU_CHUNK, SGU_CHUNK), 1)
    return jnp.where(t >= s, w_ref[g], 0.0).astype(BF16)


def _sgu_core(x, lng, lnb, w_ref, bexp, gid):
    zz = _gelu(x)
    u = zz[:, :SGU_W]
    v = zz[:, SGU_W:]
    vc = v - _group_mean(v, gid)
    rstd = lax.rsqrt(_group_mean(vc * vc, gid) + EPS)
    vhat = vc * rstd
    vn = vhat * lng + lnb
    vnb = vn.astype(BF16)
    mixed = bexp
    for g in range(SGU_GROUPS):
        mm = jnp.dot(_causal_w(w_ref, g), vnb, preferred_element_type=F32)
        mixed = jnp.where(gid == g, mm + bexp, mixed)
    return u, rstd, vhat, vnb, mixed


def _sgu_fwd(z, lng, lnb, w, bexp, name, tm=1024):
    s = z.shape[0]

    def body(z_ref, lng_ref, lnb_ref, w_ref, b_ref, o_ref):
        gid = _group_ids()
        for j in range(tm // SGU_CHUNK):
            rows = pl.ds(j * SGU_CHUNK, SGU_CHUNK)
            u, _, _, _, mixed = _sgu_core(z_ref[rows, :], lng_ref[...], lnb_ref[...], w_ref, b_ref[...], gid)
            o_ref[rows, :] = u * mixed

    return _rowcall(body, name, s, tm,
                    [_rb(tm, 512, 3), _fb((1, 256)), _fb((1, 256)), _fb((4, 128, 128)), _fb((128, 256))],
                    (z, lng.reshape(1, 256), lnb.reshape(1, 256), w, bexp), _rb(tm, 256), _sds((s, 256)))


def _sgu_bwd(z, dy, lng, lnb, w, bexp, name, tm=1024):
    s = z.shape[0]

    def body(z_ref, dy_ref, lng_ref, lnb_ref, w_ref, b_ref, dz_ref, dw_ref, db_ref, dlng_ref, dlnb_ref):
        @pl.when(pl.program_id(0) == 0)
        def _():
            dw_ref[...] = jnp.zeros_like(dw_ref)
            db_ref[...] = jnp.zeros_like(db_ref)
            dlng_ref[...] = jnp.zeros_like(dlng_ref)
            dlnb_ref[...] = jnp.zeros_like(dlnb_ref)

        gid = _group_ids()
        t = lax.broadcasted_iota(jnp.int32, (SGU_CHUNK, SGU_CHUNK), 0)
        sidx = lax.broadcasted_iota(jnp.int32, (SGU_CHUNK, SGU_CHUNK), 1)
        lng_v = lng_ref[...]
        for j in range(tm // SGU_CHUNK):
            rows = pl.ds(j * SGU_CHUNK, SGU_CHUNK)
            x = z_ref[rows, :]
            u, rstd, vhat, vnb, mixed = _sgu_core(x, lng_v, lnb_ref[...], w_ref, b_ref[...], gid)
            dyv = dy_ref[rows, :]
            dmixed = dyv * u
            du = dyv * mixed
            db_ref[...] += dmixed
            dvn = jnp.zeros_like(dmixed)
            for g in range(SGU_GROUPS):
                dmg = jnp.where(gid == g, dmixed, 0.0).astype(BF16)
                dvn = dvn + lax.dot_general(_causal_w(w_ref, g), dmg, _DIMS["tn"], preferred_element_type=F32)
                dwg = lax.dot_general(dmg, vnb, _DIMS["nt"], preferred_element_type=F32)
                dw_ref[g] += jnp.where(t >= sidx, dwg, 0.0)
            dlnb_ref[...] += jnp.sum(dvn, axis=0, keepdims=True)
            dlng_ref[...] += jnp.sum(dvn * vhat, axis=0, keepdims=True)
            dvh = dvn * lng_v
            dv = rstd * (dvh - _group_mean(dvh, gid) - vhat * _group_mean(dvh * vhat, gid))
            gg = _gelu_grad(x)
            dz_ref[rows, 0:SGU_W] = du * gg[:, :SGU_W]
            dz_ref[rows, SGU_W:2 * SGU_W] = dv * gg[:, SGU_W:]

    dz, dw, db, dlng, dlnb = _rowcall(
        body, name, s, tm,
        [_rb(tm, 512, 3), _rb(tm, 256), _fb((1, 256)), _fb((1, 256)), _fb((4, 128, 128)), _fb((128, 256))],
        (z, dy, lng.reshape(1, 256), lnb.reshape(1, 256), w, bexp),
        [_rb(tm, 512), _fb((4, 128, 128)), _fb((128, 256)), _fb((1, 256)), _fb((1, 256))],
        [_sds((s, 512)), _sds((4, 128, 128)), _sds((128, 256)), _sds((1, 256)), _sds((1, 256))])
    return dz, dw, db, dlng.reshape(256), dlnb.reshape(256)


CONV_TC = 1408
N_CT = D_FF // CONV_TC


def _row_of(block8, j):
    r = lax.broadcasted_iota(jnp.int32, block8.shape, 0)
    return jnp.sum(jnp.where(r == j, block8, 0.0), axis=0, keepdims=True)


EDGE = 16


def _conv_fwd(hu, cw, cb, name, tm=512):
    s = hu.shape[0]

    def body(xv_ref, xg_ref, tv_ref, tg_ref, wv_ref, wg_ref, bv_ref, bg_ref, hv_ref, hg_ref, act_ref):
        has_prev = (pl.program_id(1) > 0).astype(F32)
        row = lax.broadcasted_iota(jnp.int32, (EDGE, CONV_TC), 0)

        def conv(x_ref, t_ref, w_ref, b_ref):
            x = x_ref[...].astype(F32)
            w0, w1, w2, bb = w_ref[0:1, :], w_ref[1:2, :], w_ref[2:3, :], b_ref[...]
            whole = w0 * pltpu.roll(x, 2, axis=0) + w1 * pltpu.roll(x, 1, axis=0) + w2 * x + bb
            tail = t_ref[...].astype(F32)
            r7 = _row_of(tail, EDGE - 1) * has_prev
            r6 = _row_of(tail, EDGE - 2) * has_prev
            xe = x_ref[0:EDGE, :].astype(F32)
            x1 = jnp.where(row == 0, r7, pltpu.roll(xe, 1, axis=0))
            x2 = jnp.where(row == 0, r6, jnp.where(row == 1, r7, pltpu.roll(xe, 2, axis=0)))
            return whole, w0 * x2 + w1 * x1 + w2 * xe + bb

        hv, hv_edge = conv(xv_ref, tv_ref, wv_ref, bv_ref)
        hg, hg_edge = conv(xg_ref, tg_ref, wg_ref, bg_ref)
        hv_ref[...] = hv.astype(BF16)
        hg_ref[...] = hg.astype(BF16)
        act_ref[...] = (_gelu(hg) * hv).astype(BF16)
        hv_ref[0:EDGE, :] = hv_edge.astype(BF16)
        hg_ref[0:EDGE, :] = hg_edge.astype(BF16)
        act_ref[0:EDGE, :] = (_gelu(hg_edge) * hv_edge).astype(BF16)

    def xs(off):
        return pl.BlockSpec((tm, CONV_TC), lambda j, i: (i, j + off))

    def ts(off):
        return pl.BlockSpec((EDGE, CONV_TC), lambda j, i: (jnp.maximum(i * (tm // EDGE) - 1, 0), j + off))

    def ws(rows, off):
        return pl.BlockSpec((rows, CONV_TC), lambda j, i: (0, j + off))

    o_spec = pl.BlockSpec((tm, CONV_TC), lambda j, i: (i, j))
    return pl.pallas_call(
        body, name=name, grid=(N_CT, s // tm),
        in_specs=[xs(0), xs(N_CT), ts(0), ts(N_CT), ws(3, 0), ws(3, N_CT), ws(1, 0), ws(1, N_CT)],
        out_specs=[o_spec] * 3, out_shape=[_sds((s, D_FF), BF16)] * 3,
        compiler_params=_params(("parallel", "arbitrary")),
    )(hu, hu, hu, hu, cw, cw, cb.reshape(1, 2 * D_FF), cb.reshape(1, 2 * D_FF))


HALO = EDGE


def _conv_bwd(dact, hv, hg, hu, cw, name, tm=512):
    s = dact.shape[0]

    def body(da_ref, dan_ref, hv_ref, hvn_ref, hg_ref, hgn_ref, x_ref, t_ref, w_ref, dx_ref, dw_ref, db_ref, d_scr):
        i = pl.program_id(1)
        is_value = pl.program_id(0) < N_CT

        @pl.when(i == 0)
        def _():
            dw_ref[...] = jnp.zeros_like(dw_ref)
            db_ref[...] = jnp.zeros_like(db_ref)

        for rows, (a_ref, v_ref, g_ref) in ((pl.ds(0, tm), (da_ref, hv_ref, hg_ref)),
                                            (pl.ds(tm, HALO), (dan_ref, hvn_ref, hgn_ref))):
            @pl.when(is_value)
            def _():
                d_scr[rows, :] = a_ref[...].astype(F32) * _gelu(g_ref[...].astype(F32))

            @pl.when(jnp.logical_not(is_value))
            def _():
                d_scr[rows, :] = (a_ref[...].astype(F32) * v_ref[...].astype(F32)
                                  * _gelu_grad(g_ref[...].astype(F32)))

        has_prev = (i > 0).astype(F32)
        has_next = (i < s // tm - 1).astype(F32)
        w0, w1, w2 = w_ref[0:1, :], w_ref[1:2, :], w_ref[2:3, :]
        d = d_scr[0:tm, :]
        dx_ref[...] = (w2 * d + w1 * pltpu.roll(d, tm - 1, axis=0) + w0 * pltpu.roll(d, tm - 2, axis=0)).astype(BF16)
        row = lax.broadcasted_iota(jnp.int32, (EDGE, CONV_TC), 0)
        nxt = d_scr[tm:tm + HALO, :]
        n0 = _row_of(nxt, 0) * has_next
        n1 = _row_of(nxt, 1) * has_next
        de = d_scr[tm - EDGE:tm, :]
        d1 = jnp.where(row == EDGE - 1, n0, pltpu.roll(de, EDGE - 1, axis=0))
        d2 = jnp.where(row == EDGE - 2, n0, jnp.where(row == EDGE - 1, n1, pltpu.roll(de, EDGE - 2, axis=0)))
        dx_ref[tm - EDGE:tm, :] = (w2 * de + w1 * d1 + w0 * d2).astype(BF16)
        x = x_ref[...].astype(F32)
        tail = t_ref[...].astype(F32)
        r7 = _row_of(tail, EDGE - 1) * has_prev
        r6 = _row_of(tail, EDGE - 2) * has_prev
        last = x_ref[tm - EDGE:tm, :].astype(F32)
        l7, l6 = _row_of(last, EDGE - 1), _row_of(last, EDGE - 2)
        head = d_scr[0:8, :]
        d0, d1h = _row_of(head, 0), _row_of(head, 1)
        dw_ref[0:1, :] += (jnp.sum(d * pltpu.roll(x, 2, axis=0), axis=0, keepdims=True)
                           + d0 * (r6 - l6) + d1h * (r7 - l7))
        dw_ref[1:2, :] += jnp.sum(d * pltpu.roll(x, 1, axis=0), axis=0, keepdims=True) + d0 * (r7 - l7)
        dw_ref[2:3, :] += jnp.sum(d * x, axis=0, keepdims=True)
        db_ref[...] += jnp.sum(d, axis=0, keepdims=True)

    a_spec = pl.BlockSpec((tm, CONV_TC), lambda j, i: (i, j % N_CT))
    an_spec = pl.BlockSpec((HALO, CONV_TC),
                           lambda j, i: (jnp.minimum((i + 1) * (tm // HALO), s // HALO - 1), j % N_CT))
    x_spec = pl.BlockSpec((tm, CONV_TC), lambda j, i: (i, j))
    t_spec = pl.BlockSpec((EDGE, CONV_TC), lambda j, i: (jnp.maximum(i * (tm // EDGE) - 1, 0), j))
    w_spec = pl.BlockSpec((3, CONV_TC), lambda j, i: (0, j))
    db_spec = pl.BlockSpec((1, CONV_TC), lambda j, i: (0, j))
    return pl.pallas_call(
        body, name=name, grid=(2 * N_CT, s // tm),
        in_specs=[a_spec, an_spec, a_spec, an_spec, a_spec, an_spec, x_spec, t_spec, w_spec],
        out_specs=[x_spec, w_spec, db_spec],
        out_shape=[_sds((s, 2 * D_FF), BF16), _sds((3, 2 * D_FF)), _sds((1, 2 * D_FF))],
        scratch_shapes=[pltpu.VMEM((tm + HALO, CONV_TC), F32)],
        compiler_params=_params(("parallel", "arbitrary")),
    )(dact, dact, hv, hv, hg, hg, hu, hu, cw)


def _ple_fwd(h, gp, pp, next_gain, name, tm=1024):
    s, d = h.shape
    with_norm = next_gain is not None

    def body(*refs):
        h_ref, g_ref, p_ref = refs[:3]
        out = h_ref[...] + _sigmoid(g_ref[...].astype(F32)) * p_ref[...].astype(F32)
        if with_norm:
            n_ref, o_ref, a_ref = refs[3:]
            scale = lax.rsqrt(jnp.mean(out * out, axis=-1, keepdims=True) + EPS)
            a_ref[...] = (out * scale * n_ref[...]).astype(BF16)
        else:
            o_ref, = refs[3:]
        o_ref[...] = out

    if not with_norm:
        return _rowcall(body, name, s, tm, [_rb(tm, d)] * 3, (h, gp, pp), _rb(tm, d), _sds((s, d))), None
    return _rowcall(body, name, s, tm, [_rb(tm, d)] * 3 + [_fb((1, d))], (h, gp, pp, next_gain.reshape(1, d)),
                    [_rb(tm, d)] * 2, [_sds((s, d)), _sds((s, d), BF16)])


def _ple_bwd(dh, gp, pp, name, tm=1024):
    s, d = dh.shape

    def body(d_ref, g_ref, p_ref, dp_ref, dg_ref):
        sg = _sigmoid(g_ref[...].astype(F32))
        dv = d_ref[...]
        dp_ref[...] = (dv * sg).astype(BF16)
        dg_ref[...] = (dv * p_ref[...].astype(F32) * sg * (1.0 - sg)).astype(BF16)

    return _rowcall(body, name, s, tm, [_rb(tm, d)] * 3, (dh, gp, pp), [_rb(tm, d)] * 2,
                    [_sds((s, d), BF16)] * 2)


SCALE = HEAD_DIM ** -0.5
ATT_ROWS = 2048


def _att_geom(s, dil):
    w = min(ATT_ROWS, s)
    p = BLK * dil
    assert w % p == 0 and s % w == 0
    return w, p, w // p


ATT_Q = 64


def _rows(start, dil, n=BLK):
    return pl.ds(start, n, stride=dil) if dil > 1 else pl.ds(start, n)


def _head_masks():
    lane = lax.broadcasted_iota(jnp.int32, (1, BLK), 1)
    return [lane < HEAD_DIM, lane >= HEAD_DIM]


def _band():
    rel = np.arange(BLK)[:, None] + BLK - np.arange(2 * BLK)[None, :]
    return (rel >= 0) & (rel <= BLK)


def _zcur(w):
    return lambda off: pl.BlockSpec((w, BLK), lambda hp, i: (i, off + hp))


def _zprev(p, nb):
    return lambda off: pl.BlockSpec((p, BLK), lambda hp, i: (jnp.maximum(i * nb - 1, 0), off + hp))


def _scur(w):
    return pl.BlockSpec((w, BLK), lambda hp, i: (i, hp))


def _pair_rows(t, masks):
    return jnp.concatenate([jnp.where(masks[0], t, 0.0), jnp.where(masks[1], t, 0.0)], axis=0).astype(BF16)


def _pair_bias_bwd(bias):
    return bias.reshape(4, 2, BLK, 2, BLK).transpose(0, 3, 2, 1, 4).reshape(4, 2, BLK, 2 * BLK)


def _unpair_bias_bwd(db):
    return db.reshape(4, 2, BLK, 2, BLK).transpose(0, 3, 2, 1, 4).reshape(N_HEADS, BLK, 2 * BLK)


def _attn_fwd(z, biases, name):
    s = z.shape[0]
    w = min(ATT_ROWS, s)
    n_br = len(BRANCHES)

    def body(*refs):
        q_ref, kp_ref, kc_ref, vp_ref, vc_ref = refs[:5]
        b_refs = refs[5:5 + n_br]
        y_ref, lse_ref, m_ref, l_ref, a_ref = refs[5 + n_br:]
        i = pl.program_id(1)
        masks = _head_masks()
        own_block = lax.broadcasted_iota(jnp.int32, (1, 2 * BLK), 1) >= BLK
        for n, (_, dil) in enumerate(BRANCHES):
            _, p, nb = _att_geom(s, dil)
            for r in range(dil):
                for b, half in [(b, h) for b in range(nb) for h in range(BLK // ATT_Q)]:
                    rows = _rows(r + p * b, dil)
                    qrows = _rows(r + p * b + half * ATT_Q * dil, dil, ATT_Q)
                    prev_rows = _rows(r + p * (b - 1), dil) if b > 0 else _rows(w - p + r, dil)
                    kprev, vprev = (kc_ref, vc_ref) if b > 0 else (kp_ref, vp_ref)
                    q = q_ref[qrows, :] * SCALE
                    k = jnp.concatenate([kprev[prev_rows, :], kc_ref[rows, :]], axis=0).astype(BF16)
                    v = jnp.concatenate([vprev[prev_rows, :], vc_ref[rows, :]], axis=0).astype(BF16)
                    mb = lb = ob = None
                    for hh, mh in enumerate(masks):
                        qh = jnp.where(mh, q, 0.0).astype(BF16)
                        sc = (lax.dot_general(qh, k, _DIMS["nt"], preferred_element_type=F32)
                              + b_refs[n][hh, half * ATT_Q:(half + 1) * ATT_Q, :])
                        if b == 0:
                            sc = jnp.where(own_block | (i > 0), sc, NEG_INF)
                        mx = jnp.max(sc, axis=1, keepdims=True)
                        e = jnp.exp(sc - mx)
                        den = jnp.sum(e, axis=1, keepdims=True)
                        o = jnp.dot(e.astype(BF16), v, preferred_element_type=F32)
                        if hh == 0:
                            mb = jnp.broadcast_to(mx, (ATT_Q, BLK))
                            lb = jnp.broadcast_to(den, (ATT_Q, BLK))
                            ob = o
                        else:
                            mb = jnp.where(mh, mx, mb)
                            lb = jnp.where(mh, den, lb)
                            ob = jnp.where(mh, o, ob)
                    if n == 0:
                        m_new, l_new, a_new = mb, lb, ob
                    else:
                        m_old = m_ref[qrows, :]
                        m_new = jnp.maximum(m_old, mb)
                        al = jnp.exp(m_old - m_new)
                        be = jnp.exp(mb - m_new)
                        l_new = al * l_ref[qrows, :] + be * lb
                        a_new = al * a_ref[qrows, :] + be * ob
                    if n == n_br - 1:
                        y_ref[qrows, :] = a_new / l_new
                        lse_ref[qrows, :] = m_new + jnp.log(l_new)
                    else:
                        m_ref[qrows, :] = m_new
                        l_ref[qrows, :] = l_new
                        a_ref[qrows, :] = a_new

    cur, prv = _zcur(w), _zprev(w, 1)
    b_spec = pl.BlockSpec((2, BLK, 2 * BLK), lambda hp, i: (hp, 0, 0))
    return pl.pallas_call(
        body, name=name, grid=(4, s // w), in_specs=[cur(0), prv(4), cur(4), prv(8), cur(8)] + [b_spec] * n_br,
        out_specs=[_scur(w)] * 2, out_shape=[_sds((s, ATTN_W))] * 2,
        scratch_shapes=[pltpu.VMEM((w, BLK), F32)] * 3,
        compiler_params=_params(("parallel", "parallel")),
    )(z, z, z, z, z, *biases)


def _row_stats(mh, dy, y, lse):
    delta = jnp.sum(jnp.where(mh, dy * y, 0.0), axis=1, keepdims=True)
    lse_h = jnp.max(jnp.where(mh, lse, NEG_INF), axis=1, keepdims=True)
    return delta, lse_h


def _attn_bwd(z, biases, dy, y, lse, name):
    s = z.shape[0]
    w = min(ATT_ROWS, s)
    n_steps = s // w
    n_br = len(BRANCHES)

    def body(*refs):
        q_ref, kp_ref, kc_ref, vp_ref, vc_ref, dy_ref, y_ref, lse_ref = refs[:8]
        b_refs = refs[8:8 + n_br]
        outs = refs[8 + n_br:]
        dq_ref, dk_ref, dv_ref = outs[:3]
        x_refs = outs[3:3 + 2 * n_br]
        db_refs = outs[3 + 2 * n_br:3 + 3 * n_br]
        acc_refs = outs[3 + 3 * n_br:]
        i = pl.program_id(1)

        @pl.when(i == 0)
        def _():
            for ref in db_refs:
                ref[...] = jnp.zeros_like(ref)

        masks = _head_masks()
        first_head = lax.broadcasted_iota(jnp.int32, (1, 2 * BLK), 1) < BLK

        sums = {"q": (acc_refs[0], dq_ref), "k": (acc_refs[1], dk_ref), "v": (acc_refs[2], dv_ref)}

        def add_up(n, rows, **vals):
            for key, val in vals.items():
                acc_ref, out_ref = sums[key]
                if n > 0:
                    val = val + acc_ref[rows, :]
                if n == n_br - 1:
                    out_ref[rows, :] = val
                else:
                    acc_ref[rows, :] = val

        for n, (_, dil) in enumerate(BRANCHES):
            _, p, nb = _att_geom(s, dil)
            b_ref, db_ref = b_refs[n], db_refs[n]
            dkx_ref, dvx_ref = x_refs[2 * n], x_refs[2 * n + 1]
            for r in range(dil):
                carry = None
                for b in range(nb):
                    rows = _rows(r + p * b, dil)
                    prev_rows = _rows(r + p * (b - 1), dil) if b > 0 else _rows(w - p + r, dil)
                    kprev, vprev = (kc_ref, vc_ref) if b > 0 else (kp_ref, vp_ref)
                    keys = [(_pair_rows(kprev[prev_rows, :], masks), _pair_rows(vprev[prev_rows, :], masks)),
                            (_pair_rows(kc_ref[rows, :], masks), _pair_rows(vc_ref[rows, :], masks))]
                    q = (q_ref[rows, :] * SCALE).astype(BF16)
                    dy_v = dy_ref[rows, :]
                    dyb = dy_v.astype(BF16)
                    stats = [_row_stats(mh, dy_v, y_ref[rows, :], lse_ref[rows, :]) for mh in masks]
                    delta = jnp.where(first_head, stats[0][0], stats[1][0])
                    lse_h = jnp.where(first_head, stats[0][1], stats[1][1])
                    dq = jnp.zeros((BLK, BLK), F32)
                    dk, dv = [], []
                    for half in range(2):
                        kh, vh = keys[half]
                        sc = lax.dot_general(q, kh, _DIMS["nt"], preferred_element_type=F32) + b_ref[half]
                        pr = jnp.exp(sc - lse_h)
                        if b == 0 and half == 0:
                            pr = pr * (i > 0).astype(F32)
                        dp = lax.dot_general(dyb, vh, _DIMS["nt"], preferred_element_type=F32)
                        ds = pr * (dp - delta)
                        db_ref[half] += ds
                        dsb = ds.astype(BF16)
                        dq = dq + jnp.dot(dsb, kh, preferred_element_type=F32)
                        dk2 = lax.dot_general(dsb, q, _DIMS["tn"], preferred_element_type=F32)
                        dv2 = lax.dot_general(pr.astype(BF16), dyb, _DIMS["tn"], preferred_element_type=F32)
                        dk.append(jnp.where(masks[0], dk2[:BLK], dk2[BLK:]))
                        dv.append(jnp.where(masks[0], dv2[:BLK], dv2[BLK:]))
                    add_up(n, rows, q=dq * SCALE)
                    if b > 0:
                        add_up(n, _rows(r + p * (b - 1), dil), k=carry[0] + dk[0], v=carry[1] + dv[0])
                    else:
                        dkx_ref[_rows(r, dil), :] = dk[0]
                        dvx_ref[_rows(r, dil), :] = dv[0]
                    carry = (dk[1], dv[1])
                add_up(n, _rows(r + p * (nb - 1), dil), k=carry[0], v=carry[1])

    cur, prv = _zcur(w), _zprev(w, 1)
    b_spec = pl.BlockSpec((None, 2, BLK, 2 * BLK), lambda hp, i: (hp, 0, 0, 0))
    x_specs, x_shapes = [], []
    for _, dil in BRANCHES:
        p = BLK * dil
        x_specs += [pl.BlockSpec((p, BLK), lambda hp, i: (i, hp))] * 2
        x_shapes += [_sds((n_steps * p, ATTN_W))] * 2
    outs = pl.pallas_call(
        body, name=name, grid=(4, n_steps),
        in_specs=[cur(0), prv(4), cur(4), prv(8), cur(8)] + [_scur(w)] * 3 + [b_spec] * n_br,
        out_specs=[_scur(w)] * 3 + x_specs + [b_spec] * n_br,
        out_shape=[_sds((s, ATTN_W))] * 3 + x_shapes + [_sds((4, 2, BLK, 2 * BLK))] * n_br,
        scratch_shapes=[pltpu.VMEM((w, BLK), F32)] * 3,
        compiler_params=_params(("parallel", "arbitrary")),
    )(z, z, z, z, z, dy, y, lse, *biases)
    dq, dk, dv = outs[:3]
    extras = [(outs[3 + 2 * n], outs[4 + 2 * n]) for n in range(n_br)]
    return dq, dk, dv, extras, [_unpair_bias_bwd(db) for db in outs[3 + 2 * n_br:]]


ASM_ROWS = 512


def _assemble_dz(dq, dk, dv, extras, dzs, du, name):
    s = dq.shape[0]
    w = min(ATT_ROWS, s)
    n_steps = s // w
    per_step = w // ASM_ROWS
    assert w % ASM_ROWS == 0

    def body(*refs):
        dq_ref, dk_ref, dv_ref, dzs_ref, du_ref = refs[:5]
        x_refs = refs[5:5 + 2 * len(extras)]
        o_ref, acc_ref = refs[-2:]
        j = pl.program_id(0)
        step = j // per_step
        has_next = (step < n_steps - 1).astype(F32)
        last_of_step = ((j + 1) % per_step == 0).astype(F32)
        o_ref[:, 0:ATTN_W] = dq_ref[...].astype(BF16)
        o_ref[:, 3 * ATTN_W:3 * ATTN_W + 2 * SGU_W] = dzs_ref[...].astype(BF16)
        o_ref[:, 3 * ATTN_W + 2 * SGU_W:IN_W] = du_ref[...].astype(BF16)
        for part, (base_ref, col) in enumerate(((dk_ref, ATTN_W), (dv_ref, 2 * ATTN_W))):
            acc_ref[...] = base_ref[...]
            for n, (_, dil) in enumerate(BRANCHES):
                rows = min(BLK * dil, ASM_ROWS)
                scale = has_next if BLK * dil >= w else has_next * last_of_step
                acc_ref[ASM_ROWS - rows:, :] += x_refs[2 * n + part][...] * scale
            o_ref[:, col:col + ATTN_W] = acc_ref[...].astype(BF16)

    def x_spec(dil):
        p = BLK * dil
        rows = min(p, ASM_ROWS)
        blocks_per_step = p // rows
        total = n_steps * blocks_per_step

        def idx(j):
            step = j // per_step
            within = (j % per_step) - (per_step - blocks_per_step)
            return (jnp.clip((step + 1) * blocks_per_step + jnp.maximum(within, 0), 0, total - 1), 0)

        return pl.BlockSpec((rows, ATTN_W), idx)

    in_specs = [_rb(ASM_ROWS, ATTN_W)] * 3 + [_rb(ASM_ROWS, 2 * SGU_W), _rb(ASM_ROWS, SSM_W)]
    args = [dq, dk, dv, dzs, du]
    for (dkx, dvx), (_, dil) in zip(extras, BRANCHES):
        in_specs += [x_spec(dil)] * 2
        args += [dkx, dvx]
    return pl.pallas_call(
        body, name=name, grid=(s // ASM_ROWS,), in_specs=in_specs, out_specs=_rb(ASM_ROWS, IN_W),
        out_shape=_sds((s, IN_W), BF16), scratch_shapes=[pltpu.VMEM((ASM_ROWS, ATTN_W), F32)],
        compiler_params=_params(("parallel",)),
    )(*args)


def _t5_bucket(dist):
    max_exact = N_BUCKETS // 2
    d = np.maximum(dist, 0)
    large = max_exact + (np.log(np.maximum(d, 1) / max_exact) / np.log(REL_MAX / max_exact)
                         * (N_BUCKETS - max_exact)).astype(np.int32)
    large = np.minimum(large, N_BUCKETS - 1)
    return np.where(d < max_exact, d, large).astype(np.int32)


def _bias_tables(rel_bias):
    period = 3 * BLK
    tabs = []
    for _, dil in BRANCHES:
        onehot = np.zeros((period, N_BUCKETS), np.float32)
        d = np.arange(BLK + 1)
        onehot[d, _t5_bucket((BLK - d) * dil)] = 1.0
        f = jnp.dot(jnp.asarray(onehot), rel_bias, precision=lax.Precision.HIGHEST)
        flat = jnp.tile(f.T, (1, BLK))[:, :BLK * (period - 1)]
        tab = flat.reshape(N_HEADS, BLK, period - 1)[:, :, :2 * BLK]
        tabs.append(jnp.where(_band()[None], tab, NEG_INF))
    return tabs


def _bucket_onehot():
    maps = []
    q = np.arange(BLK)[:, None]
    k = np.arange(2 * BLK)[None, :]
    rel = q + BLK - k
    for _, dil in BRANCHES:
        maps.append(np.where((rel >= 0) & (rel <= BLK), _t5_bucket(rel * dil), -1).reshape(-1))
    bmap = jnp.asarray(np.concatenate(maps).astype(np.int32))
    return (bmap[:, None] == jnp.arange(128, dtype=jnp.int32)[None, :]).astype(BF16)


def _block_diag(t):
    g, n, c = t.shape
    eye = jnp.eye(g, dtype=t.dtype)
    return (t[:, :, None, :] * eye[:, None, :, None]).reshape(g * n, g * c)


def _ssm_prep(a_re, a_im, log_dt, b_re, b_im, c_re, c_im):
    lam = lax.complex(a_re, a_im)
    dt = jnp.exp(log_dt)[:, None]
    a_bar = jnp.exp(lam * dt)
    b_bar = ((a_bar - 1.0) / lam)[:, :, None] * lax.complex(b_re, b_im)
    bdt = jnp.concatenate([_block_diag(jnp.real(b_bar)), _block_diag(jnp.imag(b_bar))], axis=0)
    cd = jnp.concatenate([_block_diag(jnp.transpose(c_re, (0, 2, 1))),
                          _block_diag(-jnp.transpose(c_im, (0, 2, 1)))], axis=0)
    return jnp.real(a_bar).reshape(-1), jnp.imag(a_bar).reshape(-1), bdt, cd


def _powers(ar, ai):
    pr, pi = ar[:, None], ai[:, None]
    k = 1
    while k < 8:
        lr, li = pr[:, -1:], pi[:, -1:]
        pr, pi = (jnp.concatenate([pr, pr * lr - pi * li], axis=1),
                  jnp.concatenate([pi, pr * li + pi * lr], axis=1))
        k *= 2
    return pr, pi


def _sgu_bias_expand(b):
    return jnp.repeat(b.T, 64, axis=1)


def _layer_fwd(i, h, a1, p_i, big, small, bias_tabs, next_gain):
    nm = "l%d_" % i
    sv = {"h": h}
    if a1 is None:
        a1 = _rms_fwd(h, small["norm_attn_g"][i], nm + "rms_attn")
    z = _mm(a1, big["w_in"], "nt", nm + "in_proj")
    y_attn, lse = _attn_fwd(z, [t[0] for t in bias_tabs], nm + "attn_fwd")
    bexp = _sgu_bias_expand(small["sgu_b"][i])
    y_sgu = _sgu_fwd(z, small["sgu_ln_g"][i], small["sgu_ln_b"][i], small["sgu_w"][i], bexp, nm + "sgu_fwd")
    ar, ai, bdt, cd = _ssm_prep(*[small[k][i] for k in ("ssm_a_re", "ssm_a_im", "ssm_log_dt", "ssm_b_re",
                                                         "ssm_b_im", "ssm_c_re", "ssm_c_im")])
    xr, xi, yc = _ssm_fwd(z, bdt.astype(BF16), cd.astype(BF16), _scan_tables(*_powers(ar, ai), False),
                          nm + "ssm_core")
    y_ssm = _ssm_post_fwd(yc, z, small["ssm_d"][i], big["ssm_glu_w"], small["ssm_glu_b"][i], nm + "ssm_post")
    if "rest" in big:
        big = dict({k: t for k, t in big.items() if k != "rest"}, **big["rest"](y_ssm))
    mix, h2, a2 = _mix_fwd(y_attn, y_sgu, y_ssm, small["branch_norm_g"][i], big["w_out"], h,
                           small["norm_ffn_g"][i], nm + "out_proj")
    hu = _mm(a2, big["ffn_w_up"], "nt", nm + "ffn_up", out_dtype=BF16)
    hv, hg, act = _conv_fwd(hu, big["ffn_conv_w"], small["ffn_conv_b"][i], nm + "ffn_conv")
    h3, a3 = _mm(act, big["ffn_w_down"], "nn", nm + "ffn_down", add=h2, norm_gain=small["norm_ple_g"][i])
    gp = _mm(a3, big["ple_w_gate"], "nn", nm + "ple_gate", out_dtype=BF16)
    pp = _mm(p_i, big["ple_w_proj"], "nt", nm + "ple_proj", out_dtype=BF16)
    h4, a_next = _ple_fwd(h3, gp, pp, next_gain, nm + "ple_add")
    sv.update(big=big, a1=a1, z=z, y_attn=y_attn, lse=lse, y_sgu=y_sgu, y_ssm=y_ssm, yc=yc, xr=xr, xi=xi, mix=mix, h2=h2,
              a2=a2, hu=hu, hv=hv, hg=hg, act=act, h3=h3, a3=a3, gp=gp, pp=pp)
    return h4, a_next, sv


def _layer_bwd(i, dh4, sv, p_i, big, small, bias_tabs, ffn_done=None):
    nm = "l%d_" % i
    g = {}
    dpp, dgp = _ple_bwd(dh4, sv["gp"], sv["pp"], nm + "ple_bwd")
    g["ple_w_proj"] = _mm(dpp, p_i, "tn", nm + "d_ple_proj", out_dtype=BF16)
    g["ple_w_gate"] = _mm(sv["a3"], dgp, "tn", nm + "d_ple_gate", out_dtype=BF16)
    dh3, dgain = _mm(dgp, big["ple_w_gate"], "nt", nm + "ple_gate_t", add=dh4,
                     norm_bwd=(sv["h3"], small["norm_ple_g"][i]))
    g["norm_ple_g"] = dgain.reshape(D_MODEL)
    g["ffn_w_down"] = _mm(sv["act"], dh3, "tn", nm + "d_ffn_down", out_dtype=BF16)
    dact = _mm(dh3, big["ffn_w_down"], "nt", nm + "ffn_down_t", out_dtype=BF16)
    dhu, g["ffn_conv_w"], dcb = _conv_bwd(dact, sv["hv"], sv["hg"], sv["hu"], big["ffn_conv_w"],
                                          nm + "ffn_conv_bwd")
    g["ffn_conv_b"] = dcb.reshape(2 * D_FF)
    g["ffn_w_up"] = _mm(dhu, sv["a2"], "tn", nm + "d_ffn_up", out_dtype=BF16)
    dh2, dgain = _mm(dhu, big["ffn_w_up"], "nn", nm + "ffn_up_t", add=dh3,
                     norm_bwd=(sv["h2"], small["norm_ffn_g"][i]))
    g["norm_ffn_g"] = dgain.reshape(D_MODEL)
    if ffn_done is not None:
        small = ffn_done(g, small)
    g["w_out"] = _mm(sv["mix"], dh2, "tn", nm + "d_out_proj", out_dtype=BF16)
    dya, dysg, dyss, g["branch_norm_g"] = _mix_bwd(dh2, big["w_out"], sv["y_attn"], sv["y_sgu"], sv["y_ssm"],
                                                   small["branch_norm_g"][i], nm + "mix_bwd")
    ssm_keys = ("ssm_a_re", "ssm_a_im", "ssm_log_dt", "ssm_b_re", "ssm_b_im", "ssm_c_re", "ssm_c_im")
    (ar, ai, bdt, cd), prep_vjp = jax.vjp(_ssm_prep, *[small[k][i] for k in ssm_keys])
    dy1, dgl, y2, dud, g["ssm_d"], g["ssm_glu_b"] = _ssm_post_bwd(
        dyss, sv["yc"], sv["z"], small["ssm_d"][i], big["ssm_glu_w"], small["ssm_glu_b"][i], nm + "ssm_post_bwd")
    g["ssm_glu_w"] = _mm(y2, dgl, "tn", nm + "d_ssm_glu", out_dtype=BF16)
    du, dbdt, dcd, dar, dai = _ssm_bwd(dy1, dud, sv["z"], sv["xr"], sv["xi"], bdt.astype(BF16), cd.astype(BF16),
                                       _scan_tables(*_powers(ar, ai), True), nm + "ssm_core_bwd")
    for k, val in zip(ssm_keys, prep_vjp((dar, dai, dbdt, dcd))):
        g[k] = val
    bexp, bexp_vjp = jax.vjp(_sgu_bias_expand, small["sgu_b"][i])
    dzs, g["sgu_w"], dbexp, g["sgu_ln_g"], g["sgu_ln_b"] = _sgu_bwd(
        sv["z"], dysg, small["sgu_ln_g"][i], small["sgu_ln_b"][i], small["sgu_w"][i], bexp, nm + "sgu_bwd")
    g["sgu_b"] = bexp_vjp(dbexp)[0]
    dq, dk, dv, extras, dbs = _attn_bwd(sv["z"], [t[1] for t in bias_tabs], dya, sv["y_attn"], sv["lse"],
                                        nm + "attn_bwd")
    dbs = [db.reshape(N_HEADS, BLK * 2 * BLK) for db in dbs]
    dz = _assemble_dz(dq, dk, dv, extras, dzs, du, nm + "assemble_dz")
    g["w_in"] = _mm(dz, sv["a1"], "tn", nm + "d_in_proj", out_dtype=BF16)
    dh, dgain = _mm(dz, big["w_in"], "nn", nm + "in_proj_t", add=dh2, norm_bwd=(sv["h"], small["norm_attn_g"][i]))
    g["norm_attn_g"] = dgain.reshape(D_MODEL)
    return dh, g, jnp.concatenate(dbs, axis=1)


def _local_step(x, p, target, layer_weights, small, layer_done=None):
    depth = p.shape[0]
    bias_tabs = [(t, _pair_bias_bwd(t)) for t in _bias_tables(small["rel_bias"])]
    h, a1 = x, None
    saved = []
    for i in range(depth):
        next_gain = small["norm_attn_g"][i + 1] if i + 1 < depth else None
        h, a1, sv = _layer_fwd(i, h, a1, p[i], layer_weights(i, h), small, bias_tabs, next_gain)
        saved.append(sv)
    dh, loss, g_final = _loss_head(h, target, small["final_norm_g"], "loss_head")
    layer_grads = [None] * depth
    dbias = [None] * depth
    for i in reversed(range(depth)):
        ffn_done = None if layer_done is None else (lambda g, sm, i=i: layer_done(i, "ffn", g, sm))
        dh, layer_grads[i], dbias[i] = _layer_bwd(i, dh, saved[i], p[i], saved[i]["big"], small, bias_tabs,
                                                  ffn_done)
        if layer_done is not None:
            small = layer_done(i, "all", layer_grads[i], small)
    big_grads = [{k: lg.pop(k) for k in COMM_NAMES} for lg in layer_grads]
    grads = {k: jnp.stack([layer_grads[i][k] for i in range(depth)]) for k in layer_grads[0]}
    grads["final_norm_g"] = g_final
    g_rb = _mm(sum(dbias[1:], dbias[0]), _bucket_onehot(), "nn", "d_rel_bias", tk=2048)
    grads["rel_bias"] = g_rb[:, :N_BUCKETS].T
    return loss, dh, big_grads, grads


_ANY = pl.BlockSpec(memory_space=pl.ANY)
MESH_IDS = pl.DeviceIdType.MESH


def _slot(ref, axis, j):
    return ref.at[(slice(None),) * axis + (j,)]


def _all_gather(blocks, axis, name):
    nt = len(blocks)

    def body(*refs):
        x_refs, o_refs = refs[:nt], refs[nt:2 * nt]
        send_sems, recv_sems, local_sems = refs[2 * nt:]
        x, y, c = lax.axis_index("x"), lax.axis_index("y"), lax.axis_index("c")
        me, sibling = (x, y, c), (x, y, 1 - c)
        chips = [(1 - x, y), (x, 1 - y), (1 - x, 1 - y)]

        def slot(t, px, py, pc):
            return _slot(o_refs[t], axis, 4 * px + 2 * py + pc)

        def copy(t, k, blk, to, src=None):
            return pltpu.make_async_remote_copy(
                src_ref=slot(t, *blk) if src is None else src, dst_ref=slot(t, *blk),
                send_sem=send_sems.at[7 * t + k], recv_sem=recv_sems.at[7 * t + k],
                device_id=to, device_id_type=MESH_IDS)

        mine = [pltpu.make_async_copy(x_refs[t], slot(t, *me), local_sems.at[t]) for t in range(nt)]
        for cp in mine:
            cp.start()
        first = []
        for t in range(nt):
            first.append(copy(t, 0, me, sibling, src=x_refs[t]))
            first += [copy(t, 1 + j, me, (*chip, c), src=x_refs[t]) for j, chip in enumerate(chips)]
        for cp in first:
            cp.start()
        passed = []
        for t in range(nt):
            for j, chip in enumerate(chips):
                copy(t, 1 + j, (*chip, c), me).wait_recv()
                passed.append(copy(t, 4 + j, (*chip, c), sibling))
                passed[-1].start()
        for t in range(nt):
            copy(t, 0, sibling, me).wait_recv()
            for j, chip in enumerate(chips):
                copy(t, 4 + j, (*chip, 1 - c), me).wait_recv()
        for cp in first + passed:
            cp.wait_send()
        for cp in mine:
            cp.wait()

    out_shape = [jax.ShapeDtypeStruct(b.shape[:axis] + (N_DEV,) + b.shape[axis:], b.dtype) for b in blocks]
    return pl.pallas_call(
        body, name=name, out_shape=out_shape, in_specs=[_ANY] * nt, out_specs=[_ANY] * nt,
        scratch_shapes=[pltpu.SemaphoreType.DMA((7 * nt,)), pltpu.SemaphoreType.DMA((7 * nt,)),
                        pltpu.SemaphoreType.DMA((nt,))],
    )(*blocks)


def _peer(k):
    x, y, c = lax.axis_index("x"), lax.axis_index("y"), lax.axis_index("c")
    px = 1 - x if k & 4 else x
    py = 1 - y if k & 2 else y
    pc = 1 - c if k & 1 else c
    return (px, py, pc), 4 * px + 2 * py + pc


_HBM = pl.BlockSpec(memory_space=pltpu.HBM)
_SEM = pl.BlockSpec(memory_space=pltpu.SEMAPHORE)
_EFFECT = pltpu.SideEffectType.DATAFLOW_SIDE_EFFECTING


def _split_copy(src_ref, land_ref, send_sems, recv_sems, t, k, gather):
    peer, idx = _peer(k)
    _, me = _peer(0)
    return pltpu.make_async_remote_copy(
        src_ref=src_ref if gather else src_ref.at[idx], dst_ref=land_ref.at[me],
        send_sem=send_sems.at[7 * t + k - 1], recv_sem=recv_sems.at[7 * t + k - 1],
        device_id=peer, device_id_type=MESH_IDS)


def _exchange_start(srcs, lands, gather, name):
    nt = len(srcs)

    def body(*refs):
        src_refs, land_refs = refs[:nt], refs[nt:2 * nt]
        send_sems, recv_sems = refs[2 * nt:2 * nt + 2]
        token = refs[-1]
        for k in range(1, N_DEV):
            for t in range(nt):
                _split_copy(src_refs[t], land_refs[t], send_sems, recv_sems, t, k, gather).start()
        token[...] = jnp.zeros_like(token)

    hbm = lambda a: pltpu.HBM(a.shape, a.dtype)
    outs = pl.pallas_call(
        body, name=name,
        out_shape=(pltpu.SemaphoreType.DMA((7 * nt,)), pltpu.SemaphoreType.DMA((7 * nt,)),
                   *[hbm(a) for a in srcs], *[hbm(a) for a in lands], jax.ShapeDtypeStruct((8, 128), F32)),
        in_specs=[_HBM] * (2 * nt),
        out_specs=(_SEM, _SEM, *[_HBM] * (2 * nt), pl.BlockSpec(memory_space=pltpu.VMEM)),
        input_output_aliases={j: 2 + j for j in range(2 * nt)},
        compiler_params=pltpu.CompilerParams(has_side_effects=_EFFECT),
    )(*[pltpu.with_memory_space_constraint(a, pltpu.HBM) for a in list(srcs) + list(lands)])
    return outs[0], outs[1], outs[2:2 + nt], outs[2 + nt:2 + 2 * nt], outs[-1]


def _exchange_wait(send_sems, recv_sems, srcs, lands, after, gather, name):
    nt = len(srcs)

    def body(*refs):
        src_refs, land_refs = refs[:nt], refs[nt:2 * nt]
        send_sems, recv_sems = refs[2 * nt:2 * nt + 2]
        for k in range(1, N_DEV):
            _, idx = _peer(k)
            for t in range(nt):
                _split_copy(src_refs[t], land_refs[t], send_sems, recv_sems, t, k, gather).wait_send()
                arrival = pltpu.make_async_remote_copy(
                    src_ref=land_refs[t].at[idx], dst_ref=land_refs[t].at[idx],
                    send_sem=send_sems.at[7 * t + k - 1], recv_sem=recv_sems.at[7 * t + k - 1],
                    device_id=_peer(k)[0], device_id_type=MESH_IDS)
                arrival.wait_recv()

    hbm = lambda a: pltpu.HBM(a.shape, a.dtype)
    outs = pl.pallas_call(
        body, name=name, out_shape=tuple(hbm(a) for a in list(srcs) + list(lands)),
        in_specs=[_HBM] * (2 * nt) + [_SEM, _SEM, _ANY], out_specs=tuple([_HBM] * (2 * nt)),
        input_output_aliases={j: j for j in range(2 * nt)},
        compiler_params=pltpu.CompilerParams(has_side_effects=_EFFECT),
    )(*srcs, *lands, send_sems, recv_sems, after)
    return outs[nt:]


def _adamw(parts, w, m, v, name, tr):
    n_layers, r, c_ = w.shape
    assert len(parts) == n_layers

    def body(*refs):
        p_refs = refs[:n_layers]
        w_ref, m_ref, v_ref, g_ref, d_ref, mo_ref, vo_ref = refs[n_layers:]

        def update(p_ref):
            g = p_ref[0].astype(F32)
            for j in range(1, N_DEV):
                g = g + p_ref[j].astype(F32)
            m2 = ADAM_B1 * m_ref[...] + (1.0 - ADAM_B1) * g
            v2 = ADAM_B2 * v_ref[...] + (1.0 - ADAM_B2) * (g * g)
            m_hat = m2 / (1.0 - ADAM_B1 ** ADAM_STEP)
            v_hat = v2 / (1.0 - ADAM_B2 ** ADAM_STEP)
            g_ref[...] = g
            d_ref[...] = -ADAM_LR * (m_hat / (jnp.sqrt(v_hat) + ADAM_EPS) + ADAM_WD * w_ref[...])
            mo_ref[...] = m2
            vo_ref[...] = v2

        for layer in range(n_layers):
            pl.when(pl.program_id(0) == layer)(lambda layer=layer: update(p_refs[layer]))

    spec = pl.BlockSpec((None, tr, c_), lambda l, i: (l, i, 0))
    p_spec = pl.BlockSpec((N_DEV, tr, c_), lambda l, i: (0, i, 0))
    return pl.pallas_call(
        body, name=name, grid=(n_layers, r // tr), in_specs=[p_spec] * n_layers + [spec] * 3,
        out_specs=[spec] * 4, out_shape=[_sds((n_layers, r, c_))] * 4,
        compiler_params=_params(("parallel", "parallel")),
    )(*parts, w, m, v)


def _pack_rows(n_elems, align):
    rows = -(-n_elems // PACK_COLS)
    return -(-rows // align) * align


def _pack(arrs, rows, dtype=F32):
    flat = jnp.concatenate([a.reshape(-1) for a in arrs]).astype(dtype)
    return jnp.pad(flat, (0, rows * PACK_COLS - flat.shape[0])).reshape(rows, PACK_COLS)


def _unpack(pack, shapes):
    flat = pack.reshape(-1)
    out, off = [], 0
    for shp in shapes:
        size = int(np.prod(shp))
        out.append(flat[off:off + size].reshape(shp))
        off += size
    return out


def _tile_rows(rows, target, align=16):
    best = align
    for t in range(align, target + 1, align):
        if rows % t == 0:
            best = t
    return best


COMM_NAMES = ("w_in", "ssm_glu_w", "w_out", "ffn_w_up", "ffn_w_down", "ple_w_gate", "ple_w_proj")
COMM_TRANSPOSED = ("w_in", "ffn_w_up", "ple_w_proj")
COMM_EARLY = ("ple_w_proj", "ple_w_gate", "ffn_w_down", "ffn_w_up")
COMM_LATE = ("w_in", "ssm_glu_w", "w_out")
SMALL_TILE_ROWS = 64
CONV_NAME = "ffn_conv_w"


def _to_comm(name, a):
    return jnp.swapaxes(a, 1, 2) if name in COMM_TRANSPOSED else a


def kernel(x, p, rel_bias, norm_attn_g, w_in, sgu_ln_g, sgu_ln_b, sgu_w, sgu_b, ssm_a_re, ssm_a_im, ssm_log_dt, ssm_b_re, ssm_b_im, ssm_c_re, ssm_c_im, ssm_d, ssm_glu_w, ssm_glu_b, branch_norm_g, w_out, norm_ffn_g, ffn_w_up, ffn_conv_w, ffn_conv_b, ffn_w_down, norm_ple_g, ple_w_gate, ple_w_proj, final_norm_g, loss_target, m_rel_bias, m_norm_attn_g, m_w_in, m_sgu_ln_g, m_sgu_ln_b, m_sgu_w, m_sgu_b, m_ssm_a_re, m_ssm_a_im, m_ssm_log_dt, m_ssm_b_re, m_ssm_b_im, m_ssm_c_re, m_ssm_c_im, m_ssm_d, m_ssm_glu_w, m_ssm_glu_b, m_branch_norm_g, m_w_out, m_norm_ffn_g, m_ffn_w_up, m_ffn_conv_w, m_ffn_conv_b, m_ffn_w_down, m_norm_ple_g, m_ple_w_gate, m_ple_w_proj, m_final_norm_g, v_rel_bias, v_norm_attn_g, v_w_in, v_sgu_ln_g, v_sgu_ln_b, v_sgu_w, v_sgu_b, v_ssm_a_re, v_ssm_a_im, v_ssm_log_dt, v_ssm_b_re, v_ssm_b_im, v_ssm_c_re, v_ssm_c_im, v_ssm_d, v_ssm_glu_w, v_ssm_glu_b, v_branch_norm_g, v_w_out, v_norm_ffn_g, v_ffn_w_up, v_ffn_conv_w, v_ffn_conv_b, v_ffn_w_down, v_norm_ple_g, v_ple_w_gate, v_ple_w_proj, v_final_norm_g):
    given = dict(locals())
    w = {n: given[n] for n in WEIGHT_NAMES}
    m = {n: given["m_" + n] for n in WEIGHT_NAMES}
    v = {n: given["v_" + n] for n in WEIGHT_NAMES}
    depth = p.shape[0]
    dev = 4 * lax.axis_index("x") + 2 * lax.axis_index("y") + lax.axis_index("c")

    wc = {n: _to_comm(n, w[n]) for n in COMM_NAMES}
    wb = {n: wc[n].astype(BF16) for n in COMM_NAMES}
    conv_local = [w[CONV_NAME], m[CONV_NAME], v[CONV_NAME]]
    conv_rows = _pack_rows(sum(int(np.prod(t.shape)) for t in conv_local), 8)
    conv_g, = _all_gather([_pack(conv_local, conv_rows)], 0, "gather_conv_taps")
    conv_parts = zip(*[_unpack(conv_g[j], [t.shape for t in conv_local]) for j in range(N_DEV)])
    conv_w, conv_m, conv_v = [jnp.concatenate(parts, axis=2) for parts in conv_parts]
    small = {n: w[n] for n in SMALL_NAMES}

    def whole(names, blocks):
        return {n: t.reshape(-1, t.shape[-1]) for n, t in zip(names, blocks)}

    def own_slot(block):
        return lax.dynamic_update_slice_in_dim(jnp.zeros((N_DEV,) + block.shape, block.dtype), block[None], dev, 0)

    def start_gather(names, i, after):
        srcs, after = lax.optimization_barrier(([wb[n][i] for n in names], after))
        return _exchange_start(srcs, [own_slot(s) for s in srcs], True, "gather_weights_%d_start" % i), after

    def wait_gather(names, i, started, after):
        send_sems, recv_sems, srcs, lands, _ = started
        return whole(names, _exchange_wait(send_sems, recv_sems, srcs, lands, after, True,
                                           "gather_weights_%d_wait" % i))

    at_once = ("w_in", "ssm_glu_w")
    later = tuple(n for n in COMM_NAMES if n not in at_once)
    w_in_0 = _all_gather([wb[n][0] for n in at_once], 0, "gather_w_in_0")
    gathering = {}
    gathering[0], (w_in_0, _) = start_gather(later, 0, (w_in_0, conv_g))
    small["norm_attn_g"] = small["norm_attn_g"] + gathering[0][4][0, 0]

    def layer_weights(i, h):
        if i > 0:
            got = wait_gather(COMM_NAMES, i, gathering.pop(i), h)
            if i + 1 < depth:
                gathering[i + 1], ordered = start_gather(COMM_NAMES, i + 1, got["w_in"])
                got["w_in"] = ordered + gathering[i + 1][4][0, 0].astype(BF16)
            return dict(got, **{CONV_NAME: conv_w[i]})

        def rest(after):
            got = wait_gather(later, 0, gathering.pop(0), after)
            if depth > 1:
                gathering[1], ordered = start_gather(COMM_NAMES, 1, got["w_out"])
                got["w_out"] = ordered + gathering[1][4][0, 0].astype(BF16)
            return got

        return dict(whole(at_once, w_in_0), **{CONV_NAME: conv_w[0], "rest": rest})

    def as_slots(g, n):
        return g.reshape((N_DEV,) + wc[n].shape[1:])

    scattering = {}

    def layer_done(i, stage, g, small_now):
        if stage == "all" and i == 0:
            return small_now
        names = COMM_EARLY if stage == "ffn" else COMM_LATE
        srcs = [as_slots(g[n], n) for n in names]
        lands = [own_slot(lax.dynamic_index_in_dim(s, dev, 0, keepdims=False)) for s in srcs]
        started = _exchange_start(srcs, lands, False, "scatter_weight_grads_%d_%s_start" % (i, stage))
        scattering[i, stage] = (names, started)
        pin = "branch_norm_g" if stage == "ffn" else "norm_ple_g"
        return dict(small_now, **{pin: small_now[pin] + started[4][0, 0]})

    loss, dx, big_grads, grads = _local_step(x[0], p[:, 0], loss_target[0], layer_weights, small, layer_done)
    loss = lax.psum(loss, ("x", "y", "c"))

    recv = [{} for _ in range(depth)]
    for (i, stage), (names, (send_sems, recv_sems, srcs, lands, _)) in scattering.items():
        got = _exchange_wait(send_sems, recv_sems, srcs, lands, dx, False,
                             "scatter_weight_grads_%d_%s_wait" % (i, stage))
        recv[i].update(zip(names, got))
    srcs = [as_slots(big_grads[0][n], n) for n in COMM_LATE]
    lands = [own_slot(lax.dynamic_index_in_dim(s, dev, 0, keepdims=False)) for s in srcs]
    last = _exchange_start(srcs, lands, False, "scatter_weight_grads_0_all_start")
    out = {}

    def update(n, pin=None):
        weight = wc[n] if pin is None else wc[n] + pin
        res = _adamw([recv[i][n] for i in range(depth)], weight, _to_comm(n, m[n]), _to_comm(n, v[n]),
                     "adamw_" + n, _tile_rows(wc[n].shape[1], 256))
        out[n] = [_to_comm(n, r) for r in res]

    for j, n in enumerate(COMM_EARLY):
        update(n, last[4][0, 0] if j == 0 else None)
    got = _exchange_wait(last[0], last[1], last[2], last[3], out[COMM_EARLY[-1]][0], False,
                         "scatter_weight_grads_0_all_wait")
    recv[0].update(zip(COMM_LATE, got))

    rep_names = SMALL_NAMES + (CONV_NAME,)
    rep_w = dict({n: w[n] for n in SMALL_NAMES}, **{CONV_NAME: conv_w})
    rep_m = dict({n: m[n] for n in SMALL_NAMES}, **{CONV_NAME: conv_m})
    rep_v = dict({n: v[n] for n in SMALL_NAMES}, **{CONV_NAME: conv_v})
    rep_shapes = [rep_w[n].shape for n in rep_names]
    rep_rows = _pack_rows(sum(int(np.prod(s)) for s in rep_shapes), SMALL_TILE_ROWS)
    rep_parts, = _all_gather([_pack([grads[n] for n in rep_names], rep_rows)], 0, "gather_small_grads")
    for n in COMM_LATE:
        update(n)
    rep_out = _adamw([rep_parts], *[_pack([src[n] for n in rep_names], rep_rows)[None] for src in (rep_w, rep_m, rep_v)],
                     "adamw_replicated", SMALL_TILE_ROWS)
    for n, vals in zip(rep_names, zip(*[_unpack(r[0], rep_shapes) for r in rep_out])):
        out[n] = list(vals)
    shard = ffn_conv_w.shape[2]
    out[CONV_NAME] = [lax.dynamic_slice_in_dim(t, dev * shard, shard, axis=2) for t in out[CONV_NAME]]
    results = [[out[n][kind] for n in WEIGHT_NAMES] for kind in range(4)]
    return (loss, dx[None], *results[0], *results[1], *results[2], *results[3])
```
